```python
import jax, jax.numpy as jnp
from jax import lax
import numpy as np

D_MODEL = 1024
BATCH = 8
SEQ = 4096
DEPTH = 1

CONV_DIM = D_MODEL // 2
CONV_WIDTH = 3
RWKV_DIM = D_MODEL - CONV_DIM
HEAD_DIM = 64
RWKV_HEADS = RWKV_DIM // HEAD_DIM
DECAY_RANK = 64
ICLR_RANK = 64
GATE_RANK = 160
D_FF = 4 * D_MODEL
PLE_DIM = 256
CONV_COLS = 3 * CONV_DIM
RWKV_COLS = 3 * RWKV_DIM + DECAY_RANK + ICLR_RANK + GATE_RANK
IN_COLS = CONV_COLS + RWKV_COLS
RWKV_SPLITS = (RWKV_DIM, 2 * RWKV_DIM, 3 * RWKV_DIM, 3 * RWKV_DIM + DECAY_RANK, 3 * RWKV_DIM + DECAY_RANK + ICLR_RANK)
RMS_EPS = 1e-6
GN_EPS = 64e-5
L2_EPS = 1e-12

kernel_name = 'hybrid_shortconv_rwkv7_block'


def _rms_norm(h, g):
    hf = h.astype(jnp.float32)
    y = hf * lax.rsqrt(jnp.mean(hf * hf, axis=-1, keepdims=True) + RMS_EPS)
    return (y * g.astype(jnp.float32)).astype(h.dtype)


def _shift_one(u):
    return jnp.pad(u, ((0, 0), (1, 0), (0, 0)))[:, :-1]


def _short_gated_conv(cols, conv_w):
    gate_b, gate_c, hx = jnp.split(cols, 3, axis=-1)
    u = gate_c * hx
    t_len = u.shape[1]
    up = jnp.pad(u, ((0, 0), (CONV_WIDTH - 1, 0), (0, 0)))
    conv = up[:, 0:t_len] * conv_w[0]
    for j in range(1, CONV_WIDTH):
        conv = conv + up[:, j:j + t_len] * conv_w[j]
    return gate_b * conv


def _rwkv7_recurrence(r, w, k, v, a, b):
    bsz, _, n_heads, n = r.shape

    def step(s, inp):
        r_t, w_t, k_t, v_t, a_t, b_t = inp
        sa = jnp.einsum('bhvk,bhk->bhv', s, a_t)
        s = s * w_t[:, :, None, :] + sa[..., None] * b_t[:, :, None, :] + v_t[..., None] * k_t[:, :, None, :]
        return s, jnp.einsum('bhvk,bhk->bhv', s, r_t)

    xs = tuple(jnp.moveaxis(t, 1, 0) for t in (r, w, k, v, a, b))
    s0 = jnp.zeros((bsz, n_heads, n, n), jnp.float32)
    _, ys = lax.scan(step, s0, xs)
    return jnp.moveaxis(ys, 0, 1)


def _rwkv7_time_mix(cols, shift_mu, w_lora_up, w0, a_lora_up, a0, g_lora_up, k_k, k_a, r_k, ln_x_g, ln_x_b):
    f32 = jnp.float32
    bsz, t_len, _ = cols.shape
    u = cols + shift_mu * (_shift_one(cols) - cols)
    r, k, v, xw, xa, xg = jnp.split(u, RWKV_SPLITS, axis=-1)
    w_log = -jax.nn.softplus(-(w0 + jnp.tanh(xw) @ w_lora_up).astype(f32)) - 0.5
    decay = jnp.exp(-jnp.exp(w_log))
    iclr = jax.nn.sigmoid((a0 + xa @ a_lora_up).astype(f32))
    g = jax.nn.sigmoid(xg) @ g_lora_up

    def heads(t):
        return t.astype(f32).reshape(bsz, t_len, RWKV_HEADS, HEAD_DIM)

    kk = heads(k * k_k)
    kk = kk / jnp.maximum(jnp.sqrt(jnp.sum(kk * kk, axis=-1, keepdims=True)), L2_EPS)
    k_h = heads(k.astype(f32) * (1.0 + (iclr - 1.0) * k_a.astype(f32)))
    r_h, v_h, a_h = heads(r), heads(v), heads(iclr)
    y = _rwkv7_recurrence(r_h, heads(decay), k_h, v_h, -kk, kk * a_h)
    mu = jnp.mean(y, axis=-1, keepdims=True)
    var = jnp.mean(jnp.square(y - mu), axis=-1, keepdims=True)
    y = ((y - mu) * lax.rsqrt(var + GN_EPS)).reshape(bsz, t_len, RWKV_DIM)
    y = y * ln_x_g.astype(f32) + ln_x_b.astype(f32)
    bonus = jnp.sum(r_h * k_h * r_k.astype(f32), axis=-1, keepdims=True) * v_h
    y = (y + bonus.reshape(bsz, t_len, RWKV_DIM)) * g.astype(f32)
    return y.astype(cols.dtype)


def _fwd_setup_inputs(seed: int = 0) -> dict:
    key = jax.random.key(seed)
    ks = jax.random.split(key, 26)
    f32 = jnp.float32
    L, D = DEPTH, D_MODEL

    def nrm(k, shape, scale):
        return jax.random.normal(k, shape, f32) * scale

    return {
        'x': nrm(ks[0], (BATCH, SEQ, D), 1.0),
        'p': nrm(ks[1], (DEPTH, BATCH, SEQ, PLE_DIM), 1.0),
        'norm_mix_g': 1.0 + nrm(ks[2], (L, D), 0.02),
        'w_in': nrm(ks[3], (L, D, IN_COLS), D ** -0.5),
        'conv_w': nrm(ks[4], (L, CONV_WIDTH, CONV_DIM), CONV_WIDTH ** -0.5),
        'shift_mu': jax.random.uniform(ks[5], (L, RWKV_COLS), f32),
        'w_lora_up': nrm(ks[6], (L, DECAY_RANK, RWKV_DIM), DECAY_RANK ** -0.5),
        'w0': jax.random.uniform(ks[7], (L, RWKV_DIM), f32, -4.0, 1.0),
        'a_lora_up': nrm(ks[8], (L, ICLR_RANK, RWKV_DIM), ICLR_RANK ** -0.5),
        'a0': nrm(ks[9], (L, RWKV_DIM), 0.1),
        'g_lora_up': nrm(ks[10], (L, GATE_RANK, RWKV_DIM), GATE_RANK ** -0.5),
        'k_k': 0.85 + nrm(ks[11], (L, RWKV_DIM), 0.02),
        'k_a': 1.0 + nrm(ks[12], (L, RWKV_DIM), 0.02),
        'r_k': nrm(ks[13], (L, RWKV_HEADS, HEAD_DIM), 0.1),
        'ln_x_g': 1.0 + nrm(ks[14], (L, RWKV_DIM), 0.02),
        'ln_x_b': nrm(ks[15], (L, RWKV_DIM), 0.02),
        'w_out': nrm(ks[16], (L, D, D), D ** -0.5),
        'norm_mlp_g': 1.0 + nrm(ks[17], (L, D), 0.02),
        'w_up': nrm(ks[18], (L, D, D_FF), D ** -0.5),
        'w_down': nrm(ks[19], (L, D_FF, D), D_FF ** -0.5),
        'norm_ple_g': 1.0 + nrm(ks[20], (L, D), 0.02),
        'w_ple_gate': nrm(ks[21], (L, D, D), D ** -0.5),
        'w_ple_proj': nrm(ks[22], (L, PLE_DIM, D), PLE_DIM ** -0.5),
        'norm_final_g': 1.0 + nrm(ks[23], (D,), 0.02),
    }


def _fwd_reference(x, p, norm_mix_g, w_in, conv_w, shift_mu, w_lora_up, w0, a_lora_up, a0, g_lora_up, k_k, k_a, r_k, ln_x_g, ln_x_b, w_out, norm_mlp_g, w_up, w_down, norm_ple_g, w_ple_gate, w_ple_proj, norm_final_g):
    h = x
    for i in range(DEPTH):
        proj = _rms_norm(h, norm_mix_g[i]) @ w_in[i]
        y_conv = _short_gated_conv(proj[..., :CONV_COLS], conv_w[i])
        y_rwkv = _rwkv7_time_mix(proj[..., CONV_COLS:], shift_mu[i], w_lora_up[i], w0[i], a_lora_up[i], a0[i], g_lora_up[i], k_k[i], k_a[i], r_k[i], ln_x_g[i], ln_x_b[i])
        h = h + jnp.concatenate([y_conv, y_rwkv], axis=-1) @ w_out[i]
        hidden = jax.nn.relu(_rms_norm(h, norm_mlp_g[i]) @ w_up[i])
        h = h + jnp.square(hidden) @ w_down[i]
        gate = jax.nn.sigmoid(_rms_norm(h, norm_ple_g[i]) @ w_ple_gate[i])
        h = h + gate * (p[i] @ w_ple_proj[i])
    return _rms_norm(h, norm_final_g)


import jax as _jax
import jax.numpy as _jnp

TWIN_FORMAT = 'train_step'
FWD_PARAMS = ['x', 'p', 'norm_mix_g', 'w_in', 'conv_w', 'shift_mu', 'w_lora_up', 'w0', 'a_lora_up', 'a0', 'g_lora_up', 'k_k', 'k_a', 'r_k', 'ln_x_g', 'ln_x_b', 'w_out', 'norm_mlp_g', 'w_up', 'w_down', 'norm_ple_g', 'w_ple_gate', 'w_ple_proj', 'norm_final_g']
TWIN_WEIGHTS = ['norm_mix_g', 'w_in', 'conv_w', 'shift_mu', 'w_lora_up', 'w0', 'a_lora_up', 'a0', 'g_lora_up', 'k_k', 'k_a', 'r_k', 'ln_x_g', 'ln_x_b', 'w_out', 'norm_mlp_g', 'w_up', 'w_down', 'norm_ple_g', 'w_ple_gate', 'w_ple_proj', 'norm_final_g']
TWIN_DIFF_INPUT = 'x'
TWIN_INPUTS = ['x', 'p', 'norm_mix_g', 'w_in', 'conv_w', 'shift_mu', 'w_lora_up', 'w0', 'a_lora_up', 'a0', 'g_lora_up', 'k_k', 'k_a', 'r_k', 'ln_x_g', 'ln_x_b', 'w_out', 'norm_mlp_g', 'w_up', 'w_down', 'norm_ple_g', 'w_ple_gate', 'w_ple_proj', 'norm_final_g', 'loss_target', 'm_norm_mix_g', 'm_w_in', 'm_conv_w', 'm_shift_mu', 'm_w_lora_up', 'm_w0', 'm_a_lora_up', 'm_a0', 'm_g_lora_up', 'm_k_k', 'm_k_a', 'm_r_k', 'm_ln_x_g', 'm_ln_x_b', 'm_w_out', 'm_norm_mlp_g', 'm_w_up', 'm_w_down', 'm_norm_ple_g', 'm_w_ple_gate', 'm_w_ple_proj', 'm_norm_final_g', 'v_norm_mix_g', 'v_w_in', 'v_conv_w', 'v_shift_mu', 'v_w_lora_up', 'v_w0', 'v_a_lora_up', 'v_a0', 'v_g_lora_up', 'v_k_k', 'v_k_a', 'v_r_k', 'v_ln_x_g', 'v_ln_x_b', 'v_w_out', 'v_norm_mlp_g', 'v_w_up', 'v_w_down', 'v_norm_ple_g', 'v_w_ple_gate', 'v_w_ple_proj', 'v_norm_final_g']
TWIN_OUTPUTS = ['loss', 'grad_x', 'grad_norm_mix_g', 'grad_w_in', 'grad_conv_w', 'grad_shift_mu', 'grad_w_lora_up', 'grad_w0', 'grad_a_lora_up', 'grad_a0', 'grad_g_lora_up', 'grad_k_k', 'grad_k_a', 'grad_r_k', 'grad_ln_x_g', 'grad_ln_x_b', 'grad_w_out', 'grad_norm_mlp_g', 'grad_w_up', 'grad_w_down', 'grad_norm_ple_g', 'grad_w_ple_gate', 'grad_w_ple_proj', 'grad_norm_final_g', 'delta_norm_mix_g', 'delta_w_in', 'delta_conv_w', 'delta_shift_mu', 'delta_w_lora_up', 'delta_w0', 'delta_a_lora_up', 'delta_a0', 'delta_g_lora_up', 'delta_k_k', 'delta_k_a', 'delta_r_k', 'delta_ln_x_g', 'delta_ln_x_b', 'delta_w_out', 'delta_norm_mlp_g', 'delta_w_up', 'delta_w_down', 'delta_norm_ple_g', 'delta_w_ple_gate', 'delta_w_ple_proj', 'delta_norm_final_g', 'new_m_norm_mix_g', 'new_m_w_in', 'new_m_conv_w', 'new_m_shift_mu', 'new_m_w_lora_up', 'new_m_w0', 'new_m_a_lora_up', 'new_m_a0', 'new_m_g_lora_up', 'new_m_k_k', 'new_m_k_a', 'new_m_r_k', 'new_m_ln_x_g', 'new_m_ln_x_b', 'new_m_w_out', 'new_m_norm_mlp_g', 'new_m_w_up', 'new_m_w_down', 'new_m_norm_ple_g', 'new_m_w_ple_gate', 'new_m_w_ple_proj', 'new_m_norm_final_g', 'new_v_norm_mix_g', 'new_v_w_in', 'new_v_conv_w', 'new_v_shift_mu', 'new_v_w_lora_up', 'new_v_w0', 'new_v_a_lora_up', 'new_v_a0', 'new_v_g_lora_up', 'new_v_k_k', 'new_v_k_a', 'new_v_r_k', 'new_v_ln_x_g', 'new_v_ln_x_b', 'new_v_w_out', 'new_v_norm_mlp_g', 'new_v_w_up', 'new_v_w_down', 'new_v_norm_ple_g', 'new_v_w_ple_gate', 'new_v_w_ple_proj', 'new_v_norm_final_g']
TWIN_LEAF_KINDS = {'loss': 'loss', 'grad_x': 'grad_x', 'grad_norm_mix_g': 'grad_w', 'grad_w_in': 'grad_w', 'grad_conv_w': 'grad_w', 'grad_shift_mu': 'grad_w', 'grad_w_lora_up': 'grad_w', 'grad_w0': 'grad_w', 'grad_a_lora_up': 'grad_w', 'grad_a0': 'grad_w', 'grad_g_lora_up': 'grad_w', 'grad_k_k': 'grad_w', 'grad_k_a': 'grad_w', 'grad_r_k': 'grad_w', 'grad_ln_x_g': 'grad_w', 'grad_ln_x_b': 'grad_w', 'grad_w_out': 'grad_w', 'grad_norm_mlp_g': 'grad_w', 'grad_w_up': 'grad_w', 'grad_w_down': 'grad_w', 'grad_norm_ple_g': 'grad_w', 'grad_w_ple_gate': 'grad_w', 'grad_w_ple_proj': 'grad_w', 'grad_norm_final_g': 'grad_w', 'delta_norm_mix_g': 'delta_w', 'delta_w_in': 'delta_w', 'delta_conv_w': 'delta_w', 'delta_shift_mu': 'delta_w', 'delta_w_lora_up': 'delta_w', 'delta_w0': 'delta_w', 'delta_a_lora_up': 'delta_w', 'delta_a0': 'delta_w', 'delta_g_lora_up': 'delta_w', 'delta_k_k': 'delta_w', 'delta_k_a': 'delta_w', 'delta_r_k': 'delta_w', 'delta_ln_x_g': 'delta_w', 'delta_ln_x_b': 'delta_w', 'delta_w_out': 'delta_w', 'delta_norm_mlp_g': 'delta_w', 'delta_w_up': 'delta_w', 'delta_w_down': 'delta_w', 'delta_norm_ple_g': 'delta_w', 'delta_w_ple_gate': 'delta_w', 'delta_w_ple_proj': 'delta_w', 'delta_norm_final_g': 'delta_w', 'new_m_norm_mix_g': 'new_m', 'new_m_w_in': 'new_m', 'new_m_conv_w': 'new_m', 'new_m_shift_mu': 'new_m', 'new_m_w_lora_up': 'new_m', 'new_m_w0': 'new_m', 'new_m_a_lora_up': 'new_m', 'new_m_a0': 'new_m', 'new_m_g_lora_up': 'new_m', 'new_m_k_k': 'new_m', 'new_m_k_a': 'new_m', 'new_m_r_k': 'new_m', 'new_m_ln_x_g': 'new_m', 'new_m_ln_x_b': 'new_m', 'new_m_w_out': 'new_m', 'new_m_norm_mlp_g': 'new_m', 'new_m_w_up': 'new_m', 'new_m_w_down': 'new_m', 'new_m_norm_ple_g': 'new_m', 'new_m_w_ple_gate': 'new_m', 'new_m_w_ple_proj': 'new_m', 'new_m_norm_final_g': 'new_m', 'new_v_norm_mix_g': 'new_v', 'new_v_w_in': 'new_v', 'new_v_conv_w': 'new_v', 'new_v_shift_mu': 'new_v', 'new_v_w_lora_up': 'new_v', 'new_v_w0': 'new_v', 'new_v_a_lora_up': 'new_v', 'new_v_a0': 'new_v', 'new_v_g_lora_up': 'new_v', 'new_v_k_k': 'new_v', 'new_v_k_a': 'new_v', 'new_v_r_k': 'new_v', 'new_v_ln_x_g': 'new_v', 'new_v_ln_x_b': 'new_v', 'new_v_w_out': 'new_v', 'new_v_norm_mlp_g': 'new_v', 'new_v_w_up': 'new_v', 'new_v_w_down': 'new_v', 'new_v_norm_ple_g': 'new_v', 'new_v_w_ple_gate': 'new_v', 'new_v_w_ple_proj': 'new_v', 'new_v_norm_final_g': 'new_v'}


def _forward(args):
    return _fwd_reference(*[args[k] for k in FWD_PARAMS])


def _output_shape():
    def fwd():
        inp = _fwd_setup_inputs(0)
        return _fwd_reference(*[inp[k] for k in FWD_PARAMS])
    out = _jax.eval_shape(fwd)
    return out.shape, out.dtype

N_MICROBATCH = 1
ADAM_LR = 0.001
ADAM_B1 = 0.9
ADAM_B2 = 0.999
ADAM_EPS = 1e-08
ADAM_WD = 0.01
ADAM_STEP = 10
PER_EXAMPLE_BATCH_AXIS = {'x': 0, 'p': 1, 'loss_target': 0}
SHARED_INPUTS = []
_WEIGHT_DTYPES = {'norm_mix_g': _jnp.float32, 'w_in': _jnp.float32, 'conv_w': _jnp.float32, 'shift_mu': _jnp.float32, 'w_lora_up': _jnp.float32, 'w0': _jnp.float32, 'a_lora_up': _jnp.float32, 'a0': _jnp.float32, 'g_lora_up': _jnp.float32, 'k_k': _jnp.float32, 'k_a': _jnp.float32, 'r_k': _jnp.float32, 'ln_x_g': _jnp.float32, 'ln_x_b': _jnp.float32, 'w_out': _jnp.float32, 'norm_mlp_g': _jnp.float32, 'w_up': _jnp.float32, 'w_down': _jnp.float32, 'norm_ple_g': _jnp.float32, 'w_ple_gate': _jnp.float32, 'w_ple_proj': _jnp.float32, 'norm_final_g': _jnp.float32}
MOMENT_SCALE = {'norm_mix_g': 2.172256e-01, 'w_in': 1.139277e-01, 'conv_w': 1.509811e-01, 'shift_mu': 1.316245e-01, 'w_lora_up': 6.517033e-03, 'w0': 3.505068e-02, 'a_lora_up': 2.929734e-02, 'a0': 3.452092e-02, 'g_lora_up': 8.097560e-02, 'k_k': 3.991346e-02, 'k_a': 9.169932e-02, 'r_k': 1.789238e-01, 'ln_x_g': 8.325228e-02, 'ln_x_b': 1.125423e-01, 'w_out': 1.140970e-01, 'norm_mlp_g': 1.356061e-01, 'w_up': 6.844894e-02, 'w_down': 1.363924e-01, 'norm_ple_g': 2.264151e-02, 'w_ple_gate': 2.160817e-02, 'w_ple_proj': 5.168622e-02, 'norm_final_g': 3.216520e+01}


def _to_microbatches(a, axis):
    t = _jnp.moveaxis(a, axis, 0)
    t = t.reshape((N_MICROBATCH, t.shape[0] // N_MICROBATCH) + t.shape[1:])
    return _jnp.moveaxis(t, 1, axis + 1)


def setup_inputs(seed: int = 0) -> dict:
    inp = _fwd_setup_inputs(seed)
    key = _jax.random.fold_in(_jax.random.key(seed), 7919)
    shape, _ = _output_shape()
    out = dict(inp)
    out["loss_target"] = _jax.random.normal(_jax.random.fold_in(key, 0), shape, _jnp.float32)
    for i, name in enumerate(TWIN_WEIGHTS):
        w = inp[name].astype(_jnp.float32)
        if MOMENT_SCALE is None:
            s = _jnp.sqrt(_jnp.mean(_jnp.square(w)) + 1e-30)
        else:
            s = MOMENT_SCALE[name]
        km, kv = _jax.random.split(_jax.random.fold_in(key, i + 1))
        out[name] = w
        out["m_" + name] = s * _jax.random.normal(km, w.shape, _jnp.float32)
        out["v_" + name] = (s * s) * _jax.random.uniform(kv, w.shape, _jnp.float32, 0.5, 1.5)
    if N_MICROBATCH > 1:
        for name, axis in PER_EXAMPLE_BATCH_AXIS.items():
            out[name] = _to_microbatches(out[name], axis)
    return {'x': out['x'], 'p': out['p'], 'norm_mix_g': out['norm_mix_g'], 'w_in': out['w_in'], 'conv_w': out['conv_w'], 'shift_mu': out['shift_mu'], 'w_lora_up': out['w_lora_up'], 'w0': out['w0'], 'a_lora_up': out['a_lora_up'], 'a0': out['a0'], 'g_lora_up': out['g_lora_up'], 'k_k': out['k_k'], 'k_a': out['k_a'], 'r_k': out['r_k'], 'ln_x_g': out['ln_x_g'], 'ln_x_b': out['ln_x_b'], 'w_out': out['w_out'], 'norm_mlp_g': out['norm_mlp_g'], 'w_up': out['w_up'], 'w_down': out['w_down'], 'norm_ple_g': out['norm_ple_g'], 'w_ple_gate': out['w_ple_gate'], 'w_ple_proj': out['w_ple_proj'], 'norm_final_g': out['norm_final_g'], 'loss_target': out['loss_target'], 'm_norm_mix_g': out['m_norm_mix_g'], 'm_w_in': out['m_w_in'], 'm_conv_w': out['m_conv_w'], 'm_shift_mu': out['m_shift_mu'], 'm_w_lora_up': out['m_w_lora_up'], 'm_w0': out['m_w0'], 'm_a_lora_up': out['m_a_lora_up'], 'm_a0': out['m_a0'], 'm_g_lora_up': out['m_g_lora_up'], 'm_k_k': out['m_k_k'], 'm_k_a': out['m_k_a'], 'm_r_k': out['m_r_k'], 'm_ln_x_g': out['m_ln_x_g'], 'm_ln_x_b': out['m_ln_x_b'], 'm_w_out': out['m_w_out'], 'm_norm_mlp_g': out['m_norm_mlp_g'], 'm_w_up': out['m_w_up'], 'm_w_down': out['m_w_down'], 'm_norm_ple_g': out['m_norm_ple_g'], 'm_w_ple_gate': out['m_w_ple_gate'], 'm_w_ple_proj': out['m_w_ple_proj'], 'm_norm_final_g': out['m_norm_final_g'], 'v_norm_mix_g': out['v_norm_mix_g'], 'v_w_in': out['v_w_in'], 'v_conv_w': out['v_conv_w'], 'v_shift_mu': out['v_shift_mu'], 'v_w_lora_up': out['v_w_lora_up'], 'v_w0': out['v_w0'], 'v_a_lora_up': out['v_a_lora_up'], 'v_a0': out['v_a0'], 'v_g_lora_up': out['v_g_lora_up'], 'v_k_k': out['v_k_k'], 'v_k_a': out['v_k_a'], 'v_r_k': out['v_r_k'], 'v_ln_x_g': out['v_ln_x_g'], 'v_ln_x_b': out['v_ln_x_b'], 'v_w_out': out['v_w_out'], 'v_norm_mlp_g': out['v_norm_mlp_g'], 'v_w_up': out['v_w_up'], 'v_w_down': out['v_w_down'], 'v_norm_ple_g': out['v_norm_ple_g'], 'v_w_ple_gate': out['v_w_ple_gate'], 'v_w_ple_proj': out['v_w_ple_proj'], 'v_norm_final_g': out['v_norm_final_g']}


def _loss(weights, diff, rest, loss_target):
    with _jax.named_scope("forward"):
        args = {**rest, TWIN_DIFF_INPUT: diff, **{k: w.astype(_WEIGHT_DTYPES[k]) for k, w in weights.items()}}
        y = _forward(args)
    with _jax.named_scope("loss_head"):
        err = _jnp.square(y.astype(_jnp.float32) - loss_target)
        return 0.5 * _jnp.sum(_jnp.mean(err, axis=-1)) if err.ndim else 0.5 * err


def _adamw(w, g, m, v):
    m = ADAM_B1 * m + (1.0 - ADAM_B1) * g
    v = ADAM_B2 * v + (1.0 - ADAM_B2) * _jnp.square(g)
    m_hat = m / (1.0 - ADAM_B1 ** ADAM_STEP)
    v_hat = v / (1.0 - ADAM_B2 ** ADAM_STEP)
    delta = -ADAM_LR * (m_hat / (_jnp.sqrt(v_hat) + ADAM_EPS) + ADAM_WD * w)
    return delta, m, v


def reference(x, p, norm_mix_g, w_in, conv_w, shift_mu, w_lora_up, w0, a_lora_up, a0, g_lora_up, k_k, k_a, r_k, ln_x_g, ln_x_b, w_out, norm_mlp_g, w_up, w_down, norm_ple_g, w_ple_gate, w_ple_proj, norm_final_g, loss_target, m_norm_mix_g, m_w_in, m_conv_w, m_shift_mu, m_w_lora_up, m_w0, m_a_lora_up, m_a0, m_g_lora_up, m_k_k, m_k_a, m_r_k, m_ln_x_g, m_ln_x_b, m_w_out, m_norm_mlp_g, m_w_up, m_w_down, m_norm_ple_g, m_w_ple_gate, m_w_ple_proj, m_norm_final_g, v_norm_mix_g, v_w_in, v_conv_w, v_shift_mu, v_w_lora_up, v_w0, v_a_lora_up, v_a0, v_g_lora_up, v_k_k, v_k_a, v_r_k, v_ln_x_g, v_ln_x_b, v_w_out, v_norm_mlp_g, v_w_up, v_w_down, v_norm_ple_g, v_w_ple_gate, v_w_ple_proj, v_norm_final_g):
    given = dict(x=x, p=p, norm_mix_g=norm_mix_g, w_in=w_in, conv_w=conv_w, shift_mu=shift_mu, w_lora_up=w_lora_up, w0=w0, a_lora_up=a_lora_up, a0=a0, g_lora_up=g_lora_up, k_k=k_k, k_a=k_a, r_k=r_k, ln_x_g=ln_x_g, ln_x_b=ln_x_b, w_out=w_out, norm_mlp_g=norm_mlp_g, w_up=w_up, w_down=w_down, norm_ple_g=norm_ple_g, w_ple_gate=w_ple_gate, w_ple_proj=w_ple_proj, norm_final_g=norm_final_g, loss_target=loss_target, m_norm_mix_g=m_norm_mix_g, m_w_in=m_w_in, m_conv_w=m_conv_w, m_shift_mu=m_shift_mu, m_w_lora_up=m_w_lora_up, m_w0=m_w0, m_a_lora_up=m_a_lora_up, m_a0=m_a0, m_g_lora_up=m_g_lora_up, m_k_k=m_k_k, m_k_a=m_k_a, m_r_k=m_r_k, m_ln_x_g=m_ln_x_g, m_ln_x_b=m_ln_x_b, m_w_out=m_w_out, m_norm_mlp_g=m_norm_mlp_g, m_w_up=m_w_up, m_w_down=m_w_down, m_norm_ple_g=m_norm_ple_g, m_w_ple_gate=m_w_ple_gate, m_w_ple_proj=m_w_ple_proj, m_norm_final_g=m_norm_final_g, v_norm_mix_g=v_norm_mix_g, v_w_in=v_w_in, v_conv_w=v_conv_w, v_shift_mu=v_shift_mu, v_w_lora_up=v_w_lora_up, v_w0=v_w0, v_a_lora_up=v_a_lora_up, v_a0=v_a0, v_g_lora_up=v_g_lora_up, v_k_k=v_k_k, v_k_a=v_k_a, v_r_k=v_r_k, v_ln_x_g=v_ln_x_g, v_ln_x_b=v_ln_x_b, v_w_out=v_w_out, v_norm_mlp_g=v_norm_mlp_g, v_w_up=v_w_up, v_w_down=v_w_down, v_norm_ple_g=v_norm_ple_g, v_w_ple_gate=v_w_ple_gate, v_w_ple_proj=v_w_ple_proj, v_norm_final_g=v_norm_final_g)
    weights = {n: given[n] for n in TWIN_WEIGHTS}
    shared = {n: given[n] for n in SHARED_INPUTS}
    per_example = {n: given[n] for n in ['x', 'p']}
    grad_fn = _jax.value_and_grad(_loss, argnums=(0, 1))

    def one_microbatch(ex, loss_target):
        ex = dict(ex)
        diff = ex.pop(TWIN_DIFF_INPUT)
        return grad_fn(weights, diff, {**shared, **ex}, loss_target)

    if N_MICROBATCH == 1:
        loss, (grad_w, grad_x) = one_microbatch(per_example, given["loss_target"])
    else:
        def body(carry, xs):
            loss_sum, grad_sum = carry
            l_k, (gw_k, gx_k) = one_microbatch(xs[0], xs[1])
            with _jax.named_scope("update"):
                return (loss_sum + l_k, _jax.tree.map(_jnp.add, grad_sum, gw_k)), gx_k

        init = (_jnp.zeros((), _jnp.float32), _jax.tree.map(_jnp.zeros_like, weights))
        (loss, grad_w), grad_x = _jax.lax.scan(body, init, (per_example, given["loss_target"]))
    with _jax.named_scope("update"):
        delta_w, new_m, new_v = {}, {}, {}
        for n in TWIN_WEIGHTS:
            delta_w[n], new_m[n], new_v[n] = _adamw(weights[n], grad_w[n], given["m_" + n], given["v_" + n])
    return (loss, grad_x, *[grad_w[n] for n in TWIN_WEIGHTS], *[delta_w[n] for n in TWIN_WEIGHTS],
            *[new_m[n] for n in TWIN_WEIGHTS], *[new_v[n] for n in TWIN_WEIGHTS])
```

```python
import functools

import jax
import jax.numpy as jnp
from jax import lax
from jax.experimental import pallas as pl
from jax.experimental.pallas import tpu as pltpu

F32 = jnp.float32
BF16 = jnp.bfloat16

N_DEV = 8
D_MODEL = 1024
CONV_DIM = 512
RWKV_DIM = 512
HEAD_DIM = 64
N_HEADS = 8
D_FF = 4096
PLE_DIM = 256
RMS_EPS = 1e-6
GN_EPS = 64e-5
L2_EPS = 1e-12
ADAM_LR, ADAM_B1, ADAM_B2, ADAM_EPS, ADAM_WD, ADAM_STEP = 0.001, 0.9, 0.999, 1e-08, 0.01, 10

CONV_COLS = 3 * CONV_DIM
RW_PAD = 2048
IN_PAD = CONV_COLS + RW_PAD
IN_COLS = 3360
XW_OFF, XA_OFF, XG_OFF = 1536, 1664, 1792
REC_CHUNK = 64
ROW_BLOCK = 256
LANE = 128
VMEM_LIMIT = 56 * 1024 * 1024


def _dims(dn, ndim):
    if ndim == 3:
        return {"nn": (((2,), (1,)), ((0,), (0,))), "nt": (((2,), (2,)), ((0,), (0,))),
                "tn": (((1,), (1,)), ((0,), (0,)))}[dn]
    return {"nn": (((1,), (0,)), ((), ())), "nt": (((1,), (1,)), ((), ())), "tn": (((0,), (0,)), ((), ()))}[dn]


def _split2(x):
    hi = x.astype(BF16)
    return hi, (x - hi.astype(F32)).astype(BF16)


def _mm_raw(x, y, dn, passes):
    f = lambda p, q: lax.dot_general(p, q, _dims(dn, x.ndim), preferred_element_type=F32)
    if passes == 1:
        return f(x.astype(BF16), y.astype(BF16))
    xh, xl = _split2(x)
    yh, yl = _split2(y)
    return f(xh, yh) + f(xh, yl) + f(xl, yh)


@functools.partial(jax.custom_vjp, nondiff_argnums=(2, 3))
def _mm(x, y, dn, passes):
    return _mm_raw(x, y, dn, passes)


def _mm_fwd(x, y, dn, passes):
    return _mm_raw(x, y, dn, passes), (x, y)


def _mm_bwd(dn, passes, res, d):
    x, y = res
    if dn == "nn":
        return _mm(d, y, "nt", passes), _mm(x, d, "tn", passes)
    if dn == "nt":
        return _mm(d, y, "nn", passes), _mm(d, x, "tn", passes)
    return _mm(y, d, "nt", passes), _mm(x, d, "nn", passes)


_mm.defvjp(_mm_fwd, _mm_bwd)


def _head_ones():
    i = lax.broadcasted_iota(jnp.int32, (RWKV_DIM, RWKV_DIM), 0) // HEAD_DIM
    j = lax.broadcasted_iota(jnp.int32, (RWKV_DIM, RWKV_DIM), 1) // HEAD_DIM
    return (i == j).astype(BF16)


def _hsum_raw(x):
    ones = _head_ones()
    f = lambda p: lax.dot_general(p, ones, _dims("nn", 2), preferred_element_type=F32)
    x1 = x.astype(BF16)
    r1 = x - x1.astype(F32)
    x2 = r1.astype(BF16)
    x3 = (r1 - x2.astype(F32)).astype(BF16)
    return f(x1) + f(x2) + f(x3)


@jax.custom_vjp
def _hsum(x):
    return _hsum_raw(x)


_hsum.defvjp(lambda x: (_hsum_raw(x), None), lambda _, d: (_hsum(d),))


def _sigmoid(x):
    return 1.0 / (1.0 + jnp.exp(-x))


def _softplus(x):
    return jnp.maximum(x, 0.0) + jnp.log(1.0 + jnp.exp(-jnp.abs(x)))


def _params(sem):
    return pltpu.CompilerParams(dimension_semantics=sem, vmem_limit_bytes=VMEM_LIMIT)


def _rowwise(name, fn, rows, consts, row_outs, acc_outs=(), tr=ROW_BLOCK):
    rows = [r if isinstance(r, tuple) else (r, r.shape[1], 0) for r in rows]
    t_len = rows[0][0].shape[0]
    tr = min(tr, t_len)
    n_r, n_c, n_o, n_a = len(rows), len(consts), len(row_outs), len(acc_outs)

    def body(*refs):
        ins = [r[...] for r in refs[:n_r + n_c]]
        outs = fn(*ins)
        o_refs = refs[n_r + n_c:n_r + n_c + n_o]
        a_refs = refs[n_r + n_c + n_o:]
        for o_ref, val in zip(o_refs, outs[:n_o]):
            o_ref[...] = val.astype(o_ref.dtype)
        if n_a:
            first = pl.program_id(0) == 0

            @pl.when(first)
            def _():
                for a_ref, val in zip(a_refs, outs[n_o:]):
                    a_ref[...] = val

            @pl.when(jnp.logical_not(first))
            def _():
                for a_ref, val in zip(a_refs, outs[n_o:]):
                    a_ref[...] += val

    in_specs = [pl.BlockSpec((tr, w), functools.partial(lambda i, c: (i, c), c=cb)) for _, w, cb in rows]
    in_specs += [pl.BlockSpec(c.shape, functools.partial(lambda i, n: (0,) * n, n=c.ndim)) for c in consts]
    out_specs = [pl.BlockSpec((tr, w), lambda i: (i, 0)) for w, _ in row_outs]
    out_specs += [pl.BlockSpec(s, functools.partial(lambda i, n: (0,) * n, n=len(s))) for s in acc_outs]
    out_shape = [jax.ShapeDtypeStruct((t_len, w), dt) for w, dt in row_outs]
    out_shape += [jax.ShapeDtypeStruct(s, F32) for s in acc_outs]
    return pl.pallas_call(
        body, name=name, grid=(t_len // tr,), in_specs=in_specs, out_specs=out_specs, out_shape=out_shape,
        compiler_params=_params(("arbitrary",)),
    )(*[r[0] for r in rows], *consts)


def _colwise(name, fn, cols, prms, col_outs, prm_outs=()):
    t_len = cols[0][0].shape[0]
    n_blocks = col_outs[0][0] // LANE
    n_i, n_o = len(cols) + len(prms), len(col_outs)

    def body(*refs):
        outs = fn(*[r[...] for r in refs[:n_i]])
        for o_ref, val in zip(refs[n_i:], outs):
            o_ref[...] = val.astype(o_ref.dtype)

    def spec(r, off):
        return pl.BlockSpec((r, LANE), functools.partial(lambda j, o: (0, o + j), o=off))

    in_specs = [spec(t_len, off) for _, off in cols] + [spec(a.shape[0], off) for a, off in prms]
    out_specs = [spec(t_len, 0) for _ in col_outs] + [spec(r, 0) for r, _ in prm_outs]
    out_shape = [jax.ShapeDtypeStruct((t_len, w), dt) for w, dt in col_outs]
    out_shape += [jax.ShapeDtypeStruct((r, w), F32) for r, w in prm_outs]
    return pl.pallas_call(
        body, name=name, grid=(n_blocks,), in_specs=in_specs, out_specs=out_specs, out_shape=out_shape,
        compiler_params=_params(("arbitrary",)),
    )(*[c[0] for c in cols], *[p[0] for p in prms])


def _matmul(name, a, b, dn, outs, *, tm=512, tn=512, tk=512, extras=(), epilogue=None):
    if dn == "nn":
        (m, k), n = a.shape, b.shape[1]
    elif dn == "nt":
        (m, k), n = a.shape, b.shape[0]
    else:
        (k, m), n = a.shape, b.shape[1]
    tm, tn, tk = min(tm, m), min(tn, n), min(tk, k)
    nk = k // tk
    a_spec = pl.BlockSpec((tk, tm), lambda i, j, q: (q, i)) if dn == "tn" else pl.BlockSpec((tm, tk), lambda i, j, q: (i, q))
    b_spec = pl.BlockSpec((tn, tk), lambda i, j, q: (j, q)) if dn == "nt" else pl.BlockSpec((tk, tn), lambda i, j, q: (q, j))
    o_spec = pl.BlockSpec((tm, tn), lambda i, j, q: (i, j))
    n_e, n_o = len(extras), len(outs)

    def body(*refs):
        a_ref, b_ref = refs[:2]
        e_refs = refs[2:2 + n_e]
        o_refs = refs[2 + n_e:2 + n_e + n_o]
        acc_ref = refs[-1]
        q = pl.program_id(2)
        part = lax.dot_general(a_ref[...].astype(BF16), b_ref[...].astype(BF16), _dims(dn, 2), preferred_element_type=F32)

        @pl.when(q == 0)
        def _():
            acc_ref[...] = part

        @pl.when(q > 0)
        def _():
            acc_ref[...] += part

        @pl.when(q == nk - 1)
        def _():
            acc = acc_ref[...]
            vals = (acc,) if epilogue is None else epilogue(acc, *[e[...] for e in e_refs])
            for o_ref, val in zip(o_refs, vals):
                o_ref[...] = val.astype(o_ref.dtype)

    res = pl.pallas_call(
        body, name=name, grid=(m // tm, n // tn, nk),
        in_specs=[a_spec, b_spec] + [o_spec] * n_e, out_specs=[o_spec] * n_o,
        out_shape=[jax.ShapeDtypeStruct((m, n), dt) for dt in outs],
        scratch_shapes=[pltpu.VMEM((tm, tn), F32)],
        compiler_params=_params(("parallel", "parallel", "arbitrary")),
    )(a, b, *extras)
    return res[0] if n_o == 1 else res


def _rms(h, g):
    return h * lax.rsqrt(jnp.mean(h * h, axis=-1, keepdims=True) + RMS_EPS) * g


def _rms_bwd(h, g, dy):
    rs = lax.rsqrt(jnp.mean(h * h, axis=-1, keepdims=True) + RMS_EPS)
    n = h * rs
    dn = dy * g
    dh = rs * (dn - n * jnp.mean(dn * n, axis=-1, keepdims=True))
    return dh, jnp.sum(dy * n, axis=0, keepdims=True)


def _rwkv_pre(k, xw, xa, xg, w0, a0, k_k, k_a, wl, al, gl):
    zw = w0 + _mm(jnp.tanh(xw), wl, "nn", 1)
    lw = -jnp.exp(-_softplus(-zw) - 0.5)
    iclr = _sigmoid(a0 + _mm(xa, al, "nn", 1))
    g = _mm(_sigmoid(xg), gl, "nn", 1)
    kk0 = k * k_k
    kk = kk0 / jnp.maximum(jnp.sqrt(_hsum(kk0 * kk0)), L2_EPS)
    k_h = k * (1.0 + (iclr - 1.0) * k_a)
    return lw, k_h, -kk, kk * iclr, g


def _rwkv_post(y, r, k_h, v, g, ln_g, ln_b, r_k):
    mu = _hsum(y) * (1.0 / HEAD_DIM)
    yc = y - mu
    var = _hsum(yc * yc) * (1.0 / HEAD_DIM)
    yo = yc * lax.rsqrt(var + GN_EPS) * ln_g + ln_b
    bonus = _hsum(r * k_h * r_k) * v
    return (yo + bonus) * g


def _shift_down(x, n):
    rows = lax.broadcasted_iota(jnp.int32, x.shape, 0)
    return jnp.where(rows < n, 0.0, pltpu.roll(x, n, 0))


def _shift_up(x, n):
    t_len = x.shape[0]
    rows = lax.broadcasted_iota(jnp.int32, x.shape, 0)
    return jnp.where(rows >= t_len - n, 0.0, pltpu.roll(x, t_len - n, 0))


def _chunk_fwd(z0, r, lw, k, v, a, b):
    n_h, c, n_k = r.shape
    mm = functools.partial(_mm, passes=3)
    ti = lax.broadcasted_iota(jnp.int32, (c, c), 0)
    si = lax.broadcasted_iota(jnp.int32, (c, c), 1)
    strict, incl = si < ti, si <= ti
    cum = mm(jnp.broadcast_to(incl.astype(F32), (n_h, c, c)), lw, "nn")
    cum_end = cum[:, c - 1:c, :]
    at = a * jnp.exp(cum - lw)
    bt = b * jnp.exp(-cum)
    kt = k * jnp.exp(-cum)
    rt = r * jnp.exp(cum)
    be = b * jnp.exp(cum_end - cum)
    ke = k * jnp.exp(cum_end - cum)
    lab = jnp.where(strict, mm(at, bt, "nt"), 0.0)
    lak = jnp.where(strict, mm(at, kt, "nt"), 0.0)
    u = mm(at, z0, "nn") + mm(lak, v, "nn")
    p = lab
    n = 1
    while n < c:
        u = u + mm(p, u, "nn")
        n *= 2
        if n < c:
            p = mm(p, p, "nn")
    rb = jnp.where(incl, mm(rt, bt, "nt"), 0.0)
    rk = jnp.where(incl, mm(rt, kt, "nt"), 0.0)
    y = mm(rt, z0, "nn") + mm(rb, u, "nn") + mm(rk, v, "nn")
    ki = lax.broadcasted_iota(jnp.int32, (n_k, n_k), 0)
    kj = lax.broadcasted_iota(jnp.int32, (n_k, n_k), 1)
    dmat = jnp.where(ki == kj, jnp.broadcast_to(jnp.exp(cum_end), (n_h, n_k, n_k)), 0.0)
    z_end = mm(dmat, z0, "nn") + mm(be, u, "tn") + mm(ke, v, "tn")
    return y, z_end


def _heads(x):
    return jnp.stack([x[:, h * HEAD_DIM:(h + 1) * HEAD_DIM] for h in range(N_HEADS)])


def _unheads(x):
    return jnp.concatenate([x[h] for h in range(N_HEADS)], axis=-1)


def _rec_fwd(u, lw, k, a, b):
    t_len = lw.shape[0]
    c = min(REC_CHUNK, t_len)
    nc = t_len // c

    def body(r_ref, v_ref, lw_ref, k_ref, a_ref, b_ref, y_ref, zs_ref, z_scr):
        @pl.when(pl.program_id(0) == 0)
        def _():
            z_scr[...] = jnp.zeros_like(z_scr)

        z0 = z_scr[...]
        zs_ref[0] = z0
        y, z_end = _chunk_fwd(z0, _heads(r_ref[...]), _heads(lw_ref[...]), _heads(k_ref[...]), _heads(v_ref[...]),
                              _heads(a_ref[...]), _heads(b_ref[...]))
        y_ref[...] = _unheads(y)
        z_scr[...] = z_end

    blk = lambda cb: pl.BlockSpec((c, RWKV_DIM), functools.partial(lambda i, q: (i, q), q=cb))
    return pl.pallas_call(
        body, name="rwkv_rec_fwd", grid=(nc,),
        in_specs=[blk(0), blk(2)] + [blk(0)] * 4,
        out_specs=[blk(0), pl.BlockSpec((1, N_HEADS, HEAD_DIM, HEAD_DIM), lambda i: (i, 0, 0, 0))],
        out_shape=[jax.ShapeDtypeStruct((t_len, RWKV_DIM), F32),
                   jax.ShapeDtypeStruct((nc, N_HEADS, HEAD_DIM, HEAD_DIM), F32)],
        scratch_shapes=[pltpu.VMEM((N_HEADS, HEAD_DIM, HEAD_DIM), F32)],
        compiler_params=_params(("arbitrary",)),
    )(u, u, lw, k, a, b)


def _rec_bwd(u, lw, k, a, b, zs, dy):
    t_len = lw.shape[0]
    c = min(REC_CHUNK, t_len)
    nc = t_len // c

    def body(r_ref, v_ref, lw_ref, k_ref, a_ref, b_ref, zs_ref, dy_ref, dr_ref, dv_ref, dlw_ref, dk_ref, da_ref, db_ref, dz_scr):
        @pl.when(pl.program_id(0) == 0)
        def _():
            dz_scr[...] = jnp.zeros_like(dz_scr)

        _, vjp = jax.vjp(_chunk_fwd, zs_ref[0], _heads(r_ref[...]), _heads(lw_ref[...]), _heads(k_ref[...]),
                         _heads(v_ref[...]), _heads(a_ref[...]), _heads(b_ref[...]))
        dz0, dr, dlw, dk, dv, da, db = vjp((_heads(dy_ref[...]), dz_scr[...]))
        for ref, val in ((dr_ref, dr), (dv_ref, dv), (dlw_ref, dlw), (dk_ref, dk), (da_ref, da), (db_ref, db)):
            ref[...] = _unheads(val)
        dz_scr[...] = dz0

    blk = lambda cb: pl.BlockSpec((c, RWKV_DIM), functools.partial(lambda i, q: (nc - 1 - i, q), q=cb))
    return pl.pallas_call(
        body, name="rwkv_rec_bwd", grid=(nc,),
        in_specs=[blk(0), blk(2)] + [blk(0)] * 4
                 + [pl.BlockSpec((1, N_HEADS, HEAD_DIM, HEAD_DIM), lambda i: (nc - 1 - i, 0, 0, 0)), blk(0)],
        out_specs=[blk(0)] * 6,
        out_shape=[jax.ShapeDtypeStruct((t_len, RWKV_DIM), F32)] * 6,
        scratch_shapes=[pltpu.VMEM((N_HEADS, HEAD_DIM, HEAD_DIM), F32)],
        compiler_params=_params(("arbitrary",)),
    )(u, u, lw, k, a, b, zs, dy)


def _local_step(x, p, tgt, w):
    t_len = x.shape[0]
    row = lambda v: v.reshape(1, -1)

    (xn1,) = _rowwise("rms_mix", lambda h, g: (_rms(h, g),), [x], [w["norm_mix_g"]], [(D_MODEL, BF16)])
    projc = _matmul("in_proj_conv", xn1, w["w_in"][:, :CONV_COLS], "nn", [F32], tk=D_MODEL)
    projr = _matmul("in_proj_rwkv", xn1, w["w_in"][:, CONV_COLS:], "nn", [F32], tk=D_MODEL)

    def conv_fwd(gb, gc, hx, cw):
        uu = gc * hx
        return (gb * (uu * cw[2:3] + _shift_down(uu, 1) * cw[1:2] + _shift_down(uu, 2) * cw[0:1]),)

    (y_conv,) = _colwise("conv_fwd", conv_fwd, [(projc, 0), (projc, 4), (projc, 8)], [(w["conv_w"], 0)], [(CONV_DIM, BF16)])

    def shift_fwd(rr, mu):
        return (rr + mu * (_shift_down(rr, 1) - rr),)

    (u,) = _colwise("shift_fwd", shift_fwd, [(projr, 0)], [(w["shift_mu"], 0)], [(RW_PAD, F32)])

    small = [w["w0"], w["a0"], w["k_k"], w["k_a"], w["w_lora_up"], w["a_lora_up"], w["g_lora_up"]]
    u_k, u_xw, u_xa, u_xg = (u, 512, 1), (u, LANE, XW_OFF // LANE), (u, LANE, XA_OFF // LANE), (u, 2 * LANE, XG_OFF // (2 * LANE))
    lw, k_h, ra, rb, g = _rowwise("rwkv_pre", _rwkv_pre, [u_k, u_xw, u_xa, u_xg], small, [(RWKV_DIM, F32)] * 5)
    y_rec, zs = _rec_fwd(u, lw, k_h, ra, rb)
    post_c = [w["ln_x_g"], w["ln_x_b"], w["r_k"]]
    u_r, u_v = (u, 512, 0), (u, 512, 2)
    (y_rwkv,) = _rowwise("rwkv_post", lambda *xs: (_rwkv_post(*xs),), [y_rec, u_r, k_h, u_v, g], post_c, [(RWKV_DIM, BF16)])
    ycat = jnp.concatenate([y_conv, y_rwkv], axis=1)
    res = lambda acc, r_: (acc + r_,)
    h1 = _matmul("out_proj", ycat, w["w_out"], "nn", [F32], tk=D_MODEL, extras=[x], epilogue=res)
    (xn2,) = _rowwise("rms_mlp", lambda h, g_: (_rms(h, g_),), [h1], [w["norm_mlp_g"]], [(D_MODEL, BF16)])

    def relu2(acc):
        hid = jnp.maximum(acc, 0.0)
        return acc, hid * hid

    pre, hsq = _matmul("mlp_up", xn2, w["w_up"], "nn", [F32, BF16], tk=D_MODEL, epilogue=relu2)
    h2 = _matmul("mlp_down", hsq, w["w_down"], "nn", [F32], tk=1024, extras=[h1], epilogue=res)
    (xn3,) = _rowwise("rms_ple", lambda h, g_: (_rms(h, g_),), [h2], [w["norm_ple_g"]], [(D_MODEL, BF16)])
    zg = _matmul("ple_gate", xn3, w["w_ple_gate"], "nn", [F32], tk=D_MODEL)
    pp = _matmul("ple_proj", p, w["w_ple_proj"], "nn", [F32], tk=PLE_DIM)

    def head(h2_, zg_, pp_, tg, gf):
        h3 = h2_ + _sigmoid(zg_) * pp_
        out = _rms(h3, gf)
        err = out - tg
        dh3, dgf = _rms_bwd(h3, gf, err * (1.0 / D_MODEL))
        loss = jnp.sum(jnp.sum(err * err, axis=1, keepdims=True), axis=0, keepdims=True) * (0.5 / D_MODEL)
        return dh3, dgf, loss

    dh3, d_norm_final, loss = _rowwise("head", head, [h2, zg, pp, tgt], [row(w["norm_final_g"])], [(D_MODEL, F32)],
                                       [(1, D_MODEL), (1, 1)])

    def ple_bwd(dh3_, zg_, pp_):
        gate = _sigmoid(zg_)
        return dh3_ * pp_ * gate * (1.0 - gate), dh3_ * gate

    dzg, dpp = _rowwise("ple_bwd", ple_bwd, [dh3, zg, pp], [], [(D_MODEL, BF16)] * 2)
    d_w_ple_proj = _matmul("d_ple_proj", p, dpp, "tn", [F32], tm=PLE_DIM)
    d_w_ple_gate = _matmul("d_ple_gate", xn3, dzg, "tn", [F32])
    dxn3 = _matmul("dx_ple_gate", dzg, w["w_ple_gate"], "nt", [F32], tk=D_MODEL)

    def norm_bwd(h, dy, dres, g_):
        dh, dg = _rms_bwd(h, g_, dy)
        return dh + dres, dg

    dh2, d_norm_ple = _rowwise("rms_ple_bwd", norm_bwd, [h2, dxn3, dh3], [w["norm_ple_g"]], [(D_MODEL, F32)], [(1, D_MODEL)])
    d_w_down = _matmul("d_mlp_down", hsq, dh2, "tn", [F32])
    dpre = _matmul("dx_mlp_down", dh2, w["w_down"], "nt", [BF16], tk=D_MODEL, extras=[pre],
                   epilogue=lambda acc, pre_: (acc * (2.0 * jnp.maximum(pre_, 0.0)),))
    d_w_up = _matmul("d_mlp_up", xn2, dpre, "tn", [F32])
    dxn2 = _matmul("dx_mlp_up", dpre, w["w_up"], "nt", [F32], tk=1024)
    dh1, d_norm_mlp = _rowwise("rms_mlp_bwd", norm_bwd, [h1, dxn2, dh2], [w["norm_mlp_g"]], [(D_MODEL, F32)], [(1, D_MODEL)])
    d_w_out = _matmul("d_out_proj", ycat, dh1, "tn", [F32])
    dycat = _matmul("dx_out_proj", dh1, w["w_out"], "nt", [F32], tk=D_MODEL)

    def conv_bwd(dy, gb, gc, hx, cw):
        uu = gc * hx
        u1, u2 = _shift_down(uu, 1), _shift_down(uu, 2)
        dconv = dy * gb
        du = dconv * cw[2:3] + _shift_up(dconv, 1) * cw[1:2] + _shift_up(dconv, 2) * cw[0:1]
        s = lambda z: jnp.sum(z, axis=0, keepdims=True)
        return (dy * (uu * cw[2:3] + u1 * cw[1:2] + u2 * cw[0:1]), du * hx, du * gc,
                s(dconv * u2), s(dconv * u1), s(dconv * uu))

    dgb, dgc, dhx, dcw0, dcw1, dcw2 = _colwise(
        "conv_bwd", conv_bwd, [(dycat, 0), (projc, 0), (projc, 4), (projc, 8)], [(w["conv_w"], 0)],
        [(CONV_DIM, BF16)] * 3, [(1, CONV_DIM)] * 3)

    def post_bwd(dy, y, r, k_h_, v, g_, ln_g, ln_b, r_k):
        _, vjp = jax.vjp(_rwkv_post, y, r, k_h_, v, g_, ln_g, ln_b, r_k)
        return vjp(dy)

    dy_rec, dr_p, dk_p, dv_p, dg, d_ln_g, d_ln_b, d_r_k = _rowwise(
        "rwkv_post_bwd", post_bwd, [(dycat, 512, 1), y_rec, u_r, k_h, u_v, g], post_c,
        [(RWKV_DIM, F32)] * 5, [(1, RWKV_DIM)] * 3)
    dr_r, dv_r, dlw, dk_r, da, db = _rec_bwd(u, lw, k_h, ra, rb, zs, dy_rec)

    def pre_bwd(k, xw, xa, xg, dr1, dr2, dv1, dv2, dlw_, dk1, dk2, da_, db_, dg_, w0, a0, k_k, k_a, wl, al, gl):
        _, vjp = jax.vjp(_rwkv_pre, k, xw, xa, xg, w0, a0, k_k, k_a, wl, al, gl)
        dk, dxw, dxa, dxg, dw0, da0, dk_k, dk_a, dwl, dal, dgl = vjp((dlw_, dk1 + dk2, da_, db_, dg_))
        du = jnp.concatenate([dr1 + dr2, dk, dv1 + dv2, dxw, dxa, dxg], axis=1)
        return du, dw0, da0, dk_k, dk_a, dwl, dal, dgl

    du, d_w0, d_a0, d_k_k, d_k_a, d_wl, d_al, d_gl = _rowwise(
        "rwkv_pre_bwd", pre_bwd, [u_k, u_xw, u_xa, u_xg, dr_p, dr_r, dv_p, dv_r, dlw, dk_p, dk_r, da, db, dg], small,
        [(RW_PAD, F32)], [(1, RWKV_DIM)] * 4 + [(LANE, RWKV_DIM), (LANE, RWKV_DIM), (2 * LANE, RWKV_DIM)])

    def shift_bwd(du_, rr, mu):
        return du_ - mu * du_ + mu * _shift_up(du_, 1), jnp.sum(du_ * (_shift_down(rr, 1) - rr), axis=0, keepdims=True)

    dprojr, d_mu = _colwise("shift_bwd", shift_bwd, [(du, 0), (projr, 0)], [(w["shift_mu"], 0)], [(RW_PAD, BF16)], [(1, RW_PAD)])
    dproj = jnp.concatenate([dgb, dgc, dhx, dprojr], axis=1)
    d_w_in = _matmul("d_in_proj", xn1, dproj, "tn", [F32])
    dxn1 = _matmul("dx_in_proj", dproj, w["w_in"], "nt", [F32], tk=512)
    dx, d_norm_mix = _rowwise("rms_mix_bwd", norm_bwd, [x, dxn1, dh1], [w["norm_mix_g"]], [(D_MODEL, F32)], [(1, D_MODEL)])

    grads = dict(
        norm_mix_g=d_norm_mix, w_in=d_w_in, conv_w=jnp.concatenate([dcw0, dcw1, dcw2], axis=0), shift_mu=d_mu,
        w_lora_up=d_wl, w0=d_w0, a_lora_up=d_al, a0=d_a0, g_lora_up=d_gl, k_k=d_k_k, k_a=d_k_a, r_k=d_r_k,
        ln_x_g=d_ln_g, ln_x_b=d_ln_b, w_out=d_w_out, norm_mlp_g=d_norm_mlp, w_up=d_w_up, w_down=d_w_down,
        norm_ple_g=d_norm_ple, w_ple_gate=d_w_ple_gate, w_ple_proj=d_w_ple_proj, norm_final_g=d_norm_final)
    return loss, dx, grads


def _exchange(name, arrays, scatter):
    n = len(arrays)

    def body(*refs):
        ins, outs = refs[:n], refs[n:2 * n]
        send_sems, recv_sems, local_sems = refs[2 * n:]
        x, y, c = lax.axis_index("x"), lax.axis_index("y"), lax.axis_index("c")
        me = 4 * x + 2 * y + c

        def copies(i):
            src_local = ins[i].at[me] if scatter else ins[i]
            cps = [pltpu.make_async_copy(src_local, outs[i].at[me], local_sems.at[i])]
            for rel in range(1, N_DEV):
                peer = (x ^ (rel >> 2), y ^ ((rel >> 1) & 1), c ^ (rel & 1))
                src = ins[i].at[me ^ rel] if scatter else ins[i]
                cps.append(pltpu.make_async_remote_copy(
                    src_ref=src, dst_ref=outs[i].at[me], send_sem=send_sems.at[i, rel - 1], recv_sem=recv_sems.at[i, rel - 1],
                    device_id=peer, device_id_type=pl.DeviceIdType.MESH))
            return cps

        started = [copies(i) for i in range(n)]
        for cps in started:
            for cp in cps:
                cp.start()
        for i, cps in enumerate(started):
            cps[0].wait()
            for rel in range(1, N_DEV):
                pltpu.make_async_remote_copy(
                    src_ref=outs[i].at[me ^ rel], dst_ref=outs[i].at[me ^ rel], send_sem=send_sems.at[i, rel - 1],
                    recv_sem=recv_sems.at[i, rel - 1], device_id=(x, y, c), device_id_type=pl.DeviceIdType.MESH).wait_recv()
            for cp in cps[1:]:
                cp.wait_send()

    any_spec = pl.BlockSpec(memory_space=pl.ANY)
    out_shape = [jax.ShapeDtypeStruct(a.shape if scatter else (N_DEV,) + a.shape, a.dtype) for a in arrays]
    return pl.pallas_call(
        body, name=name, in_specs=[any_spec] * n, out_specs=[any_spec] * n, out_shape=out_shape,
        scratch_shapes=[pltpu.SemaphoreType.DMA((n, N_DEV - 1)), pltpu.SemaphoreType.DMA((n, N_DEV - 1)),
                        pltpu.SemaphoreType.DMA((n,))],
        compiler_params=pltpu.CompilerParams(has_side_effects=True),
    )(*arrays)


def _adamw(name, parts, w, m, v):
    rows, cols = w.shape
    tr = rows if rows * cols * 4 * 8 <= (4 << 20) else max(8, (4 << 20) // (cols * 4 * 8) // 8 * 8)
    while rows % tr:
        tr -= 8

    def body(p_ref, w_ref, m_ref, v_ref, g_ref, d_ref, nm_ref, nv_ref):
        g = p_ref[0]
        for s in range(1, N_DEV):
            g = g + p_ref[s]
        nm = ADAM_B1 * m_ref[...] + (1.0 - ADAM_B1) * g
        nv = ADAM_B2 * v_ref[...] + (1.0 - ADAM_B2) * (g * g)
        m_hat = nm / (1.0 - ADAM_B1 ** ADAM_STEP)
        v_hat = nv / (1.0 - ADAM_B2 ** ADAM_STEP)
        g_ref[...] = g
        d_ref[...] = -ADAM_LR * (m_hat / (jnp.sqrt(v_hat) + ADAM_EPS) + ADAM_WD * w_ref[...])
        nm_ref[...] = nm
        nv_ref[...] = nv

    blk = pl.BlockSpec((tr, cols), lambda i: (i, 0))
    return pl.pallas_call(
        body, name=name, grid=(rows // tr,),
        in_specs=[pl.BlockSpec((N_DEV, tr, cols), lambda i: (0, i, 0)), blk, blk, blk], out_specs=[blk] * 4,
        out_shape=[jax.ShapeDtypeStruct((rows, cols), F32)] * 4,
        compiler_params=_params(("arbitrary",)),
    )(parts, w, m, v)


_SHARDED = ["w_in", "conv_w", "w_lora_up", "a_lora_up", "g_lora_up", "w_out", "w_up", "w_down", "w_ple_gate", "w_ple_proj"]
_COL_SHARDED = {"w_in", "conv_w", "w_lora_up", "a_lora_up", "g_lora_up", "w_up", "w_ple_proj"}
_BF16_GATHER = {"w_in", "w_out", "w_up", "w_down", "w_ple_gate", "w_ple_proj"}
_REPLICATED = ["norm_mix_g", "shift_mu", "w0", "a0", "k_k", "k_a", "r_k", "ln_x_g", "ln_x_b", "norm_mlp_g", "norm_ple_g",
               "norm_final_g"]
_WEIGHTS = ["norm_mix_g", "w_in", "conv_w", "shift_mu", "w_lora_up", "w0", "a_lora_up", "a0", "g_lora_up", "k_k", "k_a", "r_k",
            "ln_x_g", "ln_x_b", "w_out", "norm_mlp_g", "w_up", "w_down", "norm_ple_g", "w_ple_gate", "w_ple_proj", "norm_final_g"]
_PACK_ROWS = 80


def _unshard(name, g):
    if name in _COL_SHARDED:
        return jnp.moveaxis(g, 0, 1).reshape(g.shape[1], N_DEV * g.shape[2])
    return g.reshape(N_DEV * g.shape[1], g.shape[2])


def _reshard(name, full):
    if name in _COL_SHARDED:
        return jnp.moveaxis(full.reshape(full.shape[0], N_DEV, full.shape[1] // N_DEV), 1, 0)
    return full.reshape(N_DEV, full.shape[0] // N_DEV, full.shape[1])


def _pad_in_cols(a):
    z = lambda n: jnp.zeros(a.shape[:-1] + (n,), a.dtype)
    return jnp.concatenate([a[..., :3136], z(64), a[..., 3136:3200], z(64), a[..., 3200:3360], z(96)], axis=-1)


def _unpad_in_cols(a):
    return jnp.concatenate([a[..., :3136], a[..., 3200:3264], a[..., 3328:3488]], axis=-1)


def _pad_rows(a, rows):
    return jnp.concatenate([a, jnp.zeros((rows - a.shape[0],) + a.shape[1:], a.dtype)], axis=0)


def _pack(vals):
    flat = jnp.concatenate([v.reshape(-1) for v in vals])
    return jnp.concatenate([flat, jnp.zeros((_PACK_ROWS * LANE - flat.shape[0],), F32)]).reshape(_PACK_ROWS, LANE)


def kernel(x, p, norm_mix_g, w_in, conv_w, shift_mu, w_lora_up, w0, a_lora_up, a0, g_lora_up, k_k, k_a, r_k, ln_x_g, ln_x_b, w_out, norm_mlp_g, w_up, w_down, norm_ple_g, w_ple_gate, w_ple_proj, norm_final_g, loss_target, m_norm_mix_g, m_w_in, m_conv_w, m_shift_mu, m_w_lora_up, m_w0, m_a_lora_up, m_a0, m_g_lora_up, m_k_k, m_k_a, m_r_k, m_ln_x_g, m_ln_x_b, m_w_out, m_norm_mlp_g, m_w_up, m_w_down, m_norm_ple_g, m_w_ple_gate, m_w_ple_proj, m_norm_final_g, v_norm_mix_g, v_w_in, v_conv_w, v_shift_mu, v_w_lora_up, v_w0, v_a_lora_up, v_a0, v_g_lora_up, v_k_k, v_k_a, v_r_k, v_ln_x_g, v_ln_x_b, v_w_out, v_norm_mlp_g, v_w_up, v_w_down, v_norm_ple_g, v_w_ple_gate, v_w_ple_proj, v_norm_final_g):
    args = dict(locals())
    wts = {n: args[n] for n in _WEIGHTS}
    mom = {n: args["m_" + n] for n in _WEIGHTS}
    var = {n: args["v_" + n] for n in _WEIGHTS}
    shard2d = lambda a: a.reshape(a.shape[-2:])

    shards = [shard2d(wts[n]).astype(BF16 if n in _BF16_GATHER else F32) for n in _SHARDED]
    gathered = _exchange("gather_weights", shards, scatter=False)
    full = {n: _unshard(n, g) for n, g in zip(_SHARDED, gathered)}
    w = dict(full)
    w["w_in"] = _pad_in_cols(full["w_in"])
    w["w_lora_up"] = _pad_rows(full["w_lora_up"], LANE)
    w["a_lora_up"] = _pad_rows(full["a_lora_up"], LANE)
    w["g_lora_up"] = _pad_rows(full["g_lora_up"], 2 * LANE)
    for n in _REPLICATED:
        w[n] = wts[n].reshape(1, -1)
    w["norm_final_g"] = wts["norm_final_g"]
    w["shift_mu"] = _pad_in_cols(jnp.concatenate([jnp.zeros((1, CONV_COLS), F32), wts["shift_mu"]], axis=1))[:, CONV_COLS:]

    loss, dx, grads = _local_step(x[0], p[0, 0], loss_target[0], w)

    grads["w_in"] = _unpad_in_cols(grads["w_in"])
    grads["shift_mu"] = _unpad_in_cols(jnp.concatenate([jnp.zeros((1, CONV_COLS), F32), grads["shift_mu"]], axis=1))[:, CONV_COLS:]
    grads["w_lora_up"] = grads["w_lora_up"][:64]
    grads["a_lora_up"] = grads["a_lora_up"][:64]
    grads["g_lora_up"] = grads["g_lora_up"][:160]
    parts = _exchange("scatter_grads", [_reshard(n, grads[n]) for n in _SHARDED], scatter=True)
    (small_parts,) = _exchange("gather_small", [_pack([grads[n] for n in _REPLICATED] + [loss])], scatter=False)

    out = {}
    for n, prt in zip(_SHARDED, parts):
        res = _adamw("adamw_" + n, prt, shard2d(wts[n]), shard2d(mom[n]), shard2d(var[n]))
        out[n] = [r.reshape(wts[n].shape) for r in res]
    sm = _adamw("adamw_small", small_parts, _pack([wts[n] for n in _REPLICATED]), _pack([mom[n] for n in _REPLICATED]),
                _pack([var[n] for n in _REPLICATED]))
    off = 0
    for n in _REPLICATED:
        size = wts[n].size
        out[n] = [r.reshape(-1)[off:off + size].reshape(wts[n].shape) for r in sm]
        off += size
    loss_total = sm[0].reshape(-1)[off]
    return (loss_total, dx[None], *[out[n][0] for n in _WEIGHTS], *[out[n][1] for n in _WEIGHTS],
            *[out[n][2] for n in _WEIGHTS], *[out[n][3] for n in _WEIGHTS])
```

```python
import functools

import jax
import jax.numpy as jnp
from jax import lax
from jax.experimental import pallas as pl
from jax.experimental.pallas import tpu as pltpu

F32 = jnp.float32
BF16 = jnp.bfloat16

N_DEV = 8
D_MODEL = 1024
CONV_DIM = 512
RWKV_DIM = 512
HEAD_DIM = 64
N_HEADS = 8
D_FF = 4096
PLE_DIM = 256
RMS_EPS = 1e-6
GN_EPS = 64e-5
L2_EPS = 1e-12
ADAM_LR, ADAM_B1, ADAM_B2, ADAM_EPS, ADAM_WD, ADAM_STEP = 0.001, 0.9, 0.999, 1e-08, 0.01, 10

CONV_COLS = 3 * CONV_DIM
RW_PAD = 2048
IN_PAD = CONV_COLS + RW_PAD
XW_OFF, XA_OFF, XG_OFF = 1536, 1664, 1792
REC_CHUNK = 64
ROW_BLOCK = 256
LANE = 128
VMEM_LIMIT = 56 * 1024 * 1024


def _dims(dn, ndim):
    if ndim == 3:
        return {"nn": (((2,), (1,)), ((0,), (0,))), "nt": (((2,), (2,)), ((0,), (0,))),
                "tn": (((1,), (1,)), ((0,), (0,)))}[dn]
    return {"nn": (((1,), (0,)), ((), ())), "nt": (((1,), (1,)), ((), ())), "tn": (((0,), (0,)), ((), ()))}[dn]


def _split2(x):
    hi = x.astype(BF16)
    return hi, (x - hi.astype(F32)).astype(BF16)


def _mm_raw(x, y, dn, passes):
    f = lambda p, q: lax.dot_general(p, q, _dims(dn, x.ndim), preferred_element_type=F32)
    if passes == 1:
        return f(x.astype(BF16), y.astype(BF16))
    xh, xl = _split2(x)
    yh, yl = _split2(y)
    return f(xh, yh) + f(xh, yl) + f(xl, yh)


@functools.partial(jax.custom_vjp, nondiff_argnums=(2, 3))
def _mm(x, y, dn, passes):
    return _mm_raw(x, y, dn, passes)


def _mm_fwd(x, y, dn, passes):
    return _mm_raw(x, y, dn, passes), (x, y)


def _mm_bwd(dn, passes, res, d):
    x, y = res
    if dn == "nn":
        return _mm(d, y, "nt", passes), _mm(x, d, "tn", passes)
    if dn == "nt":
        return _mm(d, y, "nn", passes), _mm(d, x, "tn", passes)
    return _mm(y, d, "nt", passes), _mm(x, d, "nn", passes)


_mm.defvjp(_mm_fwd, _mm_bwd)


def _head_ones():
    i = lax.broadcasted_iota(jnp.int32, (RWKV_DIM, RWKV_DIM), 0) // HEAD_DIM
    j = lax.broadcasted_iota(jnp.int32, (RWKV_DIM, RWKV_DIM), 1) // HEAD_DIM
    return (i == j).astype(BF16)


def _hsum_raw(x):
    ones = _head_ones()
    f = lambda p: lax.dot_general(p, ones, _dims("nn", 2), preferred_element_type=F32)
    x1 = x.astype(BF16)
    r1 = x - x1.astype(F32)
    x2 = r1.astype(BF16)
    x3 = (r1 - x2.astype(F32)).astype(BF16)
    return f(x1) + f(x2) + f(x3)


@jax.custom_vjp
def _hsum(x):
    return _hsum_raw(x)


_hsum.defvjp(lambda x: (_hsum_raw(x), None), lambda _, d: (_hsum(d),))


def _sigmoid(x):
    return 1.0 / (1.0 + jnp.exp(-x))


def _softplus(x):
    return jnp.maximum(x, 0.0) + jnp.log(1.0 + jnp.exp(-jnp.abs(x)))


def _params(sem):
    return pltpu.CompilerParams(dimension_semantics=sem, vmem_limit_bytes=VMEM_LIMIT)


def _rowwise(name, fn, rows, consts, row_outs, acc_outs=(), tr=ROW_BLOCK):
    rows = [r if isinstance(r, tuple) else (r, r.shape[1], 0) for r in rows]
    t_len = rows[0][0].shape[0]
    tr = min(tr, t_len)
    n_r, n_c, n_o, n_a = len(rows), len(consts), len(row_outs), len(acc_outs)

    def body(*refs):
        ins = [r[...] for r in refs[:n_r + n_c]]
        outs = fn(*ins)
        o_refs = refs[n_r + n_c:n_r + n_c + n_o]
        a_refs = refs[n_r + n_c + n_o:]
        for o_ref, val in zip(o_refs, outs[:n_o]):
            o_ref[...] = val.astype(o_ref.dtype)
        if n_a:
            first = pl.program_id(0) == 0

            @pl.when(first)
            def _():
                for a_ref, val in zip(a_refs, outs[n_o:]):
                    a_ref[...] = val

            @pl.when(jnp.logical_not(first))
            def _():
                for a_ref, val in zip(a_refs, outs[n_o:]):
                    a_ref[...] += val

    in_specs = [pl.BlockSpec((tr, w), functools.partial(lambda i, c: (i, c), c=cb)) for _, w, cb in rows]
    in_specs += [pl.BlockSpec(c.shape, functools.partial(lambda i, n: (0,) * n, n=c.ndim)) for c in consts]
    out_specs = [pl.BlockSpec((tr, w), lambda i: (i, 0)) for w, _ in row_outs]
    out_specs += [pl.BlockSpec(s, functools.partial(lambda i, n: (0,) * n, n=len(s))) for s in acc_outs]
    out_shape = [jax.ShapeDtypeStruct((t_len, w), dt) for w, dt in row_outs]
    out_shape += [jax.ShapeDtypeStruct(s, F32) for s in acc_outs]
    return pl.pallas_call(
        body, name=name, grid=(t_len // tr,), in_specs=in_specs, out_specs=out_specs, out_shape=out_shape,
        compiler_params=_params(("arbitrary",)),
    )(*[r[0] for r in rows], *consts)


def _colwise(name, fn, cols, prms, col_outs, prm_outs=()):
    t_len = cols[0][0].shape[0]
    n_blocks = col_outs[0][0] // LANE
    n_i = len(cols) + len(prms)

    def body(*refs):
        outs = fn(*[r[...] for r in refs[:n_i]])
        for o_ref, val in zip(refs[n_i:], outs):
            o_ref[...] = val.astype(o_ref.dtype)

    def spec(r, off):
        return pl.BlockSpec((r, LANE), functools.partial(lambda j, o: (0, o + j), o=off))

    in_specs = [spec(t_len, off) for _, off in cols] + [spec(a.shape[0], off) for a, off in prms]
    out_specs = [spec(t_len, 0) for _ in col_outs] + [spec(r, 0) for r, _ in prm_outs]
    out_shape = [jax.ShapeDtypeStruct((t_len, w), dt) for w, dt in col_outs]
    out_shape += [jax.ShapeDtypeStruct((r, w), F32) for r, w in prm_outs]
    return pl.pallas_call(
        body, name=name, grid=(n_blocks,), in_specs=in_specs, out_specs=out_specs, out_shape=out_shape,
        compiler_params=_params(("arbitrary",)),
    )(*[c[0] for c in cols], *[p[0] for p in prms])


def _matmul(name, a, b, dn, outs, *, tm, tn, tk, extras=(), epilogue=None):
    if dn == "nn":
        (m, k), n = a.shape, b.shape[1]
    elif dn == "nt":
        (m, k), n = a.shape, b.shape[0]
    else:
        (k, m), n = a.shape, b.shape[1]
    tm, tn, tk = min(tm, m), min(tn, n), min(tk, k)
    nk = k // tk
    a_spec = pl.BlockSpec((tk, tm), lambda i, j, q: (q, i)) if dn == "tn" else pl.BlockSpec((tm, tk), lambda i, j, q: (i, q))
    b_spec = pl.BlockSpec((tn, tk), lambda i, j, q: (j, q)) if dn == "nt" else pl.BlockSpec((tk, tn), lambda i, j, q: (q, j))
    o_spec = pl.BlockSpec((tm, tn), lambda i, j, q: (i, j))
    n_e, n_o = len(extras), len(outs)

    def body(*refs):
        a_ref, b_ref = refs[:2]
        e_refs = refs[2:2 + n_e]
        o_refs = refs[2 + n_e:2 + n_e + n_o]
        part = lax.dot_general(a_ref[...].astype(BF16), b_ref[...].astype(BF16), _dims(dn, 2), preferred_element_type=F32)

        def finish(acc):
            vals = (acc,) if epilogue is None else epilogue(acc, *[e[...] for e in e_refs])
            for o_ref, val in zip(o_refs, vals):
                o_ref[...] = val.astype(o_ref.dtype)

        if nk == 1:
            finish(part)
            return
        acc_ref = refs[-1]
        q = pl.program_id(2)

        @pl.when(q == 0)
        def _():
            acc_ref[...] = part

        @pl.when(q > 0)
        def _():
            acc_ref[...] += part

        @pl.when(q == nk - 1)
        def _():
            finish(acc_ref[...])

    res = pl.pallas_call(
        body, name=name, grid=(m // tm, n // tn, nk),
        in_specs=[a_spec, b_spec] + [o_spec] * n_e, out_specs=[o_spec] * n_o,
        out_shape=[jax.ShapeDtypeStruct((m, n), dt) for dt in outs],
        scratch_shapes=[pltpu.VMEM((tm, tn), F32)] if nk > 1 else [],
        compiler_params=_params(("parallel", "parallel", "arbitrary")),
    )(a, b, *extras)
    return res[0] if n_o == 1 else res


def _rms(h, g):
    return h * lax.rsqrt(jnp.mean(h * h, axis=-1, keepdims=True) + RMS_EPS) * g


def _rms_bwd(h, g, dy):
    rs = lax.rsqrt(jnp.mean(h * h, axis=-1, keepdims=True) + RMS_EPS)
    n = h * rs
    dn = dy * g
    dh = rs * (dn - n * jnp.mean(dn * n, axis=-1, keepdims=True))
    return dh, jnp.sum(dy * n, axis=0, keepdims=True)


def _rwkv_pre(k, xw, xa, xg, w0, a0, k_k, k_a, wl, al, gl):
    zw = w0 + _mm(jnp.tanh(xw), wl, "nn", 1)
    lw = -jnp.exp(-_softplus(-zw) - 0.5)
    iclr = _sigmoid(a0 + _mm(xa, al, "nn", 1))
    g = _mm(_sigmoid(xg), gl, "nn", 1)
    kk0 = k * k_k
    kk = kk0 / jnp.maximum(jnp.sqrt(_hsum(kk0 * kk0)), L2_EPS)
    k_h = k * (1.0 + (iclr - 1.0) * k_a)
    return lw, k_h, -kk, kk * iclr, g


def _rwkv_post(y, r, k_h, v, g, ln_g, ln_b, r_k):
    mu = _hsum(y) * (1.0 / HEAD_DIM)
    yc = y - mu
    var = _hsum(yc * yc) * (1.0 / HEAD_DIM)
    yo = yc * lax.rsqrt(var + GN_EPS) * ln_g + ln_b
    bonus = _hsum(r * k_h * r_k) * v
    return (yo + bonus) * g


def _shift_down(x, n):
    rows = lax.broadcasted_iota(jnp.int32, x.shape, 0)
    return jnp.where(rows < n, 0.0, pltpu.roll(x, n, 0))


def _shift_up(x, n):
    t_len = x.shape[0]
    rows = lax.broadcasted_iota(jnp.int32, x.shape, 0)
    return jnp.where(rows >= t_len - n, 0.0, pltpu.roll(x, t_len - n, 0))


def _exchange_plan(ins, outs, scatter, send_sems, recv_sems, local_sems):
    x, y, c = lax.axis_index("x"), lax.axis_index("y"), lax.axis_index("c")
    me = 4 * x + 2 * y + c

    def local(i):
        return pltpu.make_async_copy(ins[i].at[me] if scatter[i] else ins[i], outs[i].at[me], local_sems.at[i])

    def send(i, rel):
        return pltpu.make_async_remote_copy(
            src_ref=ins[i].at[me ^ rel] if scatter[i] else ins[i], dst_ref=outs[i].at[me],
            send_sem=send_sems.at[i, rel - 1], recv_sem=recv_sems.at[i, rel - 1],
            device_id=(x ^ (rel >> 2), y ^ ((rel >> 1) & 1), c ^ (rel & 1)), device_id_type=pl.DeviceIdType.MESH)

    def landed(i, rel):
        slot = outs[i].at[me ^ rel]
        return pltpu.make_async_remote_copy(
            src_ref=slot, dst_ref=slot, send_sem=send_sems.at[i, rel - 1], recv_sem=recv_sems.at[i, rel - 1],
            device_id=(x, y, c), device_id_type=pl.DeviceIdType.MESH)

    def start():
        for i in range(len(ins)):
            local(i).start()
            for rel in range(1, N_DEV):
                send(i, rel).start()

    def wait():
        for i in range(len(ins)):
            local(i).wait()
            for rel in range(1, N_DEV):
                landed(i, rel).wait_recv()
            for rel in range(1, N_DEV):
                send(i, rel).wait_send()

    return start, wait


def _exchange_io(arrays, scatter):
    n = len(arrays)
    any_spec = pl.BlockSpec(memory_space=pl.ANY)
    out_shape = [jax.ShapeDtypeStruct(a.shape if sc else (N_DEV,) + a.shape, a.dtype) for a, sc in zip(arrays, scatter)]
    sems = [pltpu.SemaphoreType.DMA((n, N_DEV - 1)), pltpu.SemaphoreType.DMA((n, N_DEV - 1)), pltpu.SemaphoreType.DMA((n,))]
    return [any_spec] * n, out_shape, sems


def _exchange(name, arrays, scatter):
    n = len(arrays)
    specs, out_shape, sems = _exchange_io(arrays, scatter)

    def body(*refs):
        start, wait = _exchange_plan(refs[:n], refs[n:2 * n], scatter, *refs[2 * n:])
        start()
        wait()

    return pl.pallas_call(
        body, name=name, in_specs=specs, out_specs=specs, out_shape=out_shape, scratch_shapes=sems,
        compiler_params=pltpu.CompilerParams(has_side_effects=True),
    )(*arrays)


def _chunk_fwd(z0, r, lw, k, v, a, b):
    n_h, c, n_k = r.shape
    mm = functools.partial(_mm, passes=3)
    ti = lax.broadcasted_iota(jnp.int32, (c, c), 0)
    si = lax.broadcasted_iota(jnp.int32, (c, c), 1)
    strict, incl = si < ti, si <= ti
    cum = mm(jnp.broadcast_to(incl.astype(F32), (n_h, c, c)), lw, "nn")
    cum_end = cum[:, c - 1:c, :]
    at = a * jnp.exp(cum - lw)
    bt = b * jnp.exp(-cum)
    kt = k * jnp.exp(-cum)
    rt = r * jnp.exp(cum)
    be = b * jnp.exp(cum_end - cum)
    ke = k * jnp.exp(cum_end - cum)
    lab = jnp.where(strict, mm(at, bt, "nt"), 0.0)
    lak = jnp.where(strict, mm(at, kt, "nt"), 0.0)
    u = mm(at, z0, "nn") + mm(lak, v, "nn")
    p = lab
    n = 1
    while n < c:
        u = u + mm(p, u, "nn")
        n *= 2
        if n < c:
            p = mm(p, p, "nn")
    rb = jnp.where(incl, mm(rt, bt, "nt"), 0.0)
    rk = jnp.where(incl, mm(rt, kt, "nt"), 0.0)
    y = mm(rt, z0, "nn") + mm(rb, u, "nn") + mm(rk, v, "nn")
    ki = lax.broadcasted_iota(jnp.int32, (n_k, n_k), 0)
    kj = lax.broadcasted_iota(jnp.int32, (n_k, n_k), 1)
    dmat = jnp.where(ki == kj, jnp.broadcast_to(jnp.exp(cum_end), (n_h, n_k, n_k)), 0.0)
    z_end = mm(dmat, z0, "nn") + mm(be, u, "tn") + mm(ke, v, "tn")
    return y, z_end


def _heads(x):
    return jnp.stack([x[:, h * HEAD_DIM:(h + 1) * HEAD_DIM] for h in range(N_HEADS)])


def _unheads(x):
    return jnp.concatenate([x[h] for h in range(N_HEADS)], axis=-1)


def _rec_params():
    return pltpu.CompilerParams(dimension_semantics=("arbitrary",), vmem_limit_bytes=VMEM_LIMIT, has_side_effects=True)


def _rec_fwd(u, lw, k, a, b, xch, xch_scatter):
    t_len = lw.shape[0]
    c = min(REC_CHUNK, t_len)
    nc = t_len // c
    n_x = len(xch)
    x_specs, x_shapes, x_sems = _exchange_io(xch, xch_scatter)

    def body(*refs):
        r_ref, v_ref, lw_ref, k_ref, a_ref, b_ref = refs[:6]
        x_in = refs[6:6 + n_x]
        y_ref, zs_ref = refs[6 + n_x:8 + n_x]
        x_out = refs[8 + n_x:8 + 2 * n_x]
        z_scr = refs[8 + 2 * n_x]
        start, wait = _exchange_plan(x_in, x_out, xch_scatter, *refs[9 + 2 * n_x:])
        i = pl.program_id(0)

        @pl.when(i == 0)
        def _():
            start()
            z_scr[...] = jnp.zeros_like(z_scr)

        z0 = z_scr[...]
        zs_ref[0] = z0
        y, z_end = _chunk_fwd(z0, _heads(r_ref[...]), _heads(lw_ref[...]), _heads(k_ref[...]), _heads(v_ref[...]),
                              _heads(a_ref[...]), _heads(b_ref[...]))
        y_ref[...] = _unheads(y)
        z_scr[...] = z_end

        @pl.when(i == nc - 1)
        def _():
            wait()

    blk = lambda cb: pl.BlockSpec((c, RWKV_DIM), functools.partial(lambda i, q: (i, q), q=cb))
    res = pl.pallas_call(
        body, name="rwkv_rec_fwd", grid=(nc,),
        in_specs=[blk(0), blk(2)] + [blk(0)] * 4 + x_specs,
        out_specs=[blk(0), pl.BlockSpec((1, N_HEADS, HEAD_DIM, HEAD_DIM), lambda i: (i, 0, 0, 0))] + x_specs,
        out_shape=[jax.ShapeDtypeStruct((t_len, RWKV_DIM), F32),
                   jax.ShapeDtypeStruct((nc, N_HEADS, HEAD_DIM, HEAD_DIM), F32)] + x_shapes,
        scratch_shapes=[pltpu.VMEM((N_HEADS, HEAD_DIM, HEAD_DIM), F32)] + x_sems,
        compiler_params=_rec_params(),
    )(u, u, lw, k, a, b, *xch)
    return res[0], res[1], res[2:]


def _rec_bwd(u, lw, k, a, b, zs, dy, xch, xch_scatter):
    t_len = lw.shape[0]
    c = min(REC_CHUNK, t_len)
    nc = t_len // c
    n_x = len(xch)
    x_specs, x_shapes, x_sems = _exchange_io(xch, xch_scatter)

    def body(*refs):
        r_ref, v_ref, lw_ref, k_ref, a_ref, b_ref, zs_ref, dy_ref = refs[:8]
        x_in = refs[8:8 + n_x]
        g_refs = refs[8 + n_x:14 + n_x]
        x_out = refs[14 + n_x:14 + 2 * n_x]
        dz_scr = refs[14 + 2 * n_x]
        start, wait = _exchange_plan(x_in, x_out, xch_scatter, *refs[15 + 2 * n_x:])
        i = pl.program_id(0)

        @pl.when(i == 0)
        def _():
            start()
            dz_scr[...] = jnp.zeros_like(dz_scr)

        _, vjp = jax.vjp(_chunk_fwd, zs_ref[0], _heads(r_ref[...]), _heads(lw_ref[...]), _heads(k_ref[...]),
                         _heads(v_ref[...]), _heads(a_ref[...]), _heads(b_ref[...]))
        dz0, dr, dlw, dk, dv, da, db = vjp((_heads(dy_ref[...]), dz_scr[...]))
        for ref, val in zip(g_refs, (dr, dv, dlw, dk, da, db)):
            ref[...] = _unheads(val)
        dz_scr[...] = dz0

        @pl.when(i == nc - 1)
        def _():
            wait()

    blk = lambda cb: pl.BlockSpec((c, RWKV_DIM), functools.partial(lambda i, q: (nc - 1 - i, q), q=cb))
    res = pl.pallas_call(
        body, name="rwkv_rec_bwd", grid=(nc,),
        in_specs=[blk(0), blk(2)] + [blk(0)] * 4
                 + [pl.BlockSpec((1, N_HEADS, HEAD_DIM, HEAD_DIM), lambda i: (nc - 1 - i, 0, 0, 0)), blk(0)] + x_specs,
        out_specs=[blk(0)] * 6 + x_specs,
        out_shape=[jax.ShapeDtypeStruct((t_len, RWKV_DIM), F32)] * 6 + x_shapes,
        scratch_shapes=[pltpu.VMEM((N_HEADS, HEAD_DIM, HEAD_DIM), F32)] + x_sems,
        compiler_params=_rec_params(),
    )(u, u, lw, k, a, b, zs, dy, *xch)
    return res[:6], res[6:]


_EARLY = ["w_in", "conv_w", "w_lora_up", "a_lora_up", "g_lora_up"]
_LATE = ["w_out", "w_up", "w_down", "w_ple_gate", "w_ple_proj"]
_SHARDED = _EARLY + _LATE
_COL_SHARDED = {"w_in", "conv_w", "w_lora_up", "a_lora_up", "g_lora_up", "w_up", "w_ple_proj"}
_BF16_GATHER = {"w_in", "w_out", "w_up", "w_down", "w_ple_gate", "w_ple_proj"}
_REPLICATED = ["norm_mix_g", "shift_mu", "w0", "a0", "k_k", "k_a", "r_k", "ln_x_g", "ln_x_b", "norm_mlp_g", "norm_ple_g",
               "norm_final_g"]
_WEIGHTS = ["norm_mix_g", "w_in", "conv_w", "shift_mu", "w_lora_up", "w0", "a_lora_up", "a0", "g_lora_up", "k_k", "k_a", "r_k",
            "ln_x_g", "ln_x_b", "w_out", "norm_mlp_g", "w_up", "w_down", "norm_ple_g", "w_ple_gate", "w_ple_proj", "norm_final_g"]
_PACK_ROWS = 80


def _unshard(name, g):
    if name in _COL_SHARDED:
        return jnp.moveaxis(g, 0, 1).reshape(g.shape[1], N_DEV * g.shape[2])
    return g.reshape(N_DEV * g.shape[1], g.shape[2])


def _reshard(name, full):
    if name in _COL_SHARDED:
        return jnp.moveaxis(full.reshape(full.shape[0], N_DEV, full.shape[1] // N_DEV), 1, 0)
    return full.reshape(N_DEV, full.shape[0] // N_DEV, full.shape[1])


def _pad_in_cols(a):
    z = lambda n: jnp.zeros(a.shape[:-1] + (n,), a.dtype)
    return jnp.concatenate([a[..., :3136], z(64), a[..., 3136:3200], z(64), a[..., 3200:3360], z(96)], axis=-1)


def _unpad_in_cols(a):
    return jnp.concatenate([a[..., :3136], a[..., 3200:3264], a[..., 3328:3488]], axis=-1)


def _pad_rows(a, rows):
    return jnp.concatenate([a, jnp.zeros((rows - a.shape[0],) + a.shape[1:], a.dtype)], axis=0)


def _pack(vals):
    flat = jnp.concatenate([v.reshape(-1) for v in vals])
    return jnp.concatenate([flat, jnp.zeros((_PACK_ROWS * LANE - flat.shape[0],), F32)]).reshape(_PACK_ROWS, LANE)


def _local_step(x, p, tgt, w, late_shards):
    row = lambda v: v.reshape(1, -1)
    w = dict(w)

    (xn1,) = _rowwise("rms_mix", lambda h, g: (_rms(h, g),), [x], [w["norm_mix_g"]], [(D_MODEL, BF16)])
    proj = _matmul("in_proj", xn1, w["w_in"], "nn", [F32], tm=1024, tn=512, tk=D_MODEL)
    proj_gb, proj_gc, proj_h, proj_rw = (proj, 0), (proj, 4), (proj, 8), (proj, CONV_COLS // LANE)

    def conv_fwd(gb, gc, hx, cw):
        uu = gc * hx
        return (gb * (uu * cw[2:3] + _shift_down(uu, 1) * cw[1:2] + _shift_down(uu, 2) * cw[0:1]),)

    (y_conv,) = _colwise("conv_fwd", conv_fwd, [proj_gb, proj_gc, proj_h], [(w["conv_w"], 0)], [(CONV_DIM, BF16)])

    def shift_fwd(rr, mu):
        return (rr + mu * (_shift_down(rr, 1) - rr),)

    (u,) = _colwise("shift_fwd", shift_fwd, [proj_rw], [(w["shift_mu"], 0)], [(RW_PAD, F32)])

    small = [w["w0"], w["a0"], w["k_k"], w["k_a"], w["w_lora_up"], w["a_lora_up"], w["g_lora_up"]]
    u_k, u_xw, u_xa, u_xg = (u, 512, 1), (u, LANE, XW_OFF // LANE), (u, LANE, XA_OFF // LANE), (u, 2 * LANE, XG_OFF // (2 * LANE))
    lw, k_h, ra, rb, g = _rowwise("rwkv_pre", _rwkv_pre, [u_k, u_xw, u_xa, u_xg], small, [(RWKV_DIM, F32)] * 5)
    y_rec, zs, late = _rec_fwd(u, lw, k_h, ra, rb, late_shards, [False] * len(_LATE))
    for n, gathered in zip(_LATE, late):
        w[n] = _unshard(n, gathered)
    post_c = [w["ln_x_g"], w["ln_x_b"], w["r_k"]]
    u_r, u_v = (u, 512, 0), (u, 512, 2)
    (y_rwkv,) = _rowwise("rwkv_post", lambda *xs: (_rwkv_post(*xs),), [y_rec, u_r, k_h, u_v, g], post_c, [(RWKV_DIM, BF16)])
    ycat = jnp.concatenate([y_conv, y_rwkv], axis=1)
    res = lambda acc, r_: (acc + r_,)
    h1 = _matmul("out_proj", ycat, w["w_out"], "nn", [F32], tm=1024, tn=1024, tk=D_MODEL, extras=[x], epilogue=res)
    (xn2,) = _rowwise("rms_mlp", lambda h, g_: (_rms(h, g_),), [h1], [w["norm_mlp_g"]], [(D_MODEL, BF16)])

    def relu2(acc):
        hid = jnp.maximum(acc, 0.0)
        return acc, hid * hid

    pre, hsq = _matmul("mlp_up", xn2, w["w_up"], "nn", [F32, BF16], tm=1024, tn=1024, tk=D_MODEL, epilogue=relu2)
    h2 = _matmul("mlp_down", hsq, w["w_down"], "nn", [F32], tm=512, tn=1024, tk=D_FF, extras=[h1], epilogue=res)
    (xn3,) = _rowwise("rms_ple", lambda h, g_: (_rms(h, g_),), [h2], [w["norm_ple_g"]], [(D_MODEL, BF16)])
    zg = _matmul("ple_gate", xn3, w["w_ple_gate"], "nn", [F32], tm=1024, tn=1024, tk=D_MODEL)
    pp = _matmul("ple_proj", p, w["w_ple_proj"], "nn", [F32], tm=1024, tn=1024, tk=PLE_DIM)

    def head(h2_, zg_, pp_, tg, gf):
        h3 = h2_ + _sigmoid(zg_) * pp_
        out = _rms(h3, gf)
        err = out - tg
        dh3, dgf = _rms_bwd(h3, gf, err * (1.0 / D_MODEL))
        loss = jnp.sum(jnp.sum(err * err, axis=1, keepdims=True), axis=0, keepdims=True) * (0.5 / D_MODEL)
        return dh3, dgf, loss

    dh3, d_norm_final, loss = _rowwise("head", head, [h2, zg, pp, tgt], [row(w["norm_final_g"])], [(D_MODEL, F32)],
                                       [(1, D_MODEL), (1, 1)])

    def ple_bwd(dh3_, zg_, pp_):
        gate = _sigmoid(zg_)
        return dh3_ * pp_ * gate * (1.0 - gate), dh3_ * gate

    dzg, dpp = _rowwise("ple_bwd", ple_bwd, [dh3, zg, pp], [], [(D_MODEL, BF16)] * 2)
    d_w_ple_proj = _matmul("d_ple_proj", p, dpp, "tn", [F32], tm=PLE_DIM, tn=1024, tk=4096)
    d_w_ple_gate = _matmul("d_ple_gate", xn3, dzg, "tn", [F32], tm=512, tn=1024, tk=4096)
    dxn3 = _matmul("dx_ple_gate", dzg, w["w_ple_gate"], "nt", [F32], tm=1024, tn=1024, tk=D_MODEL)

    def norm_bwd(h, dy, dres, g_):
        dh, dg = _rms_bwd(h, g_, dy)
        return dh + dres, dg

    dh2, d_norm_ple = _rowwise("rms_ple_bwd", norm_bwd, [h2, dxn3, dh3], [w["norm_ple_g"]], [(D_MODEL, F32)], [(1, D_MODEL)])
    d_w_down = _matmul("d_mlp_down", hsq, dh2, "tn", [F32], tm=512, tn=512, tk=4096)
    dpre = _matmul("dx_mlp_down", dh2, w["w_down"], "nt", [BF16], tm=1024, tn=1024, tk=D_MODEL, extras=[pre],
                   epilogue=lambda acc, pre_: (acc * (2.0 * jnp.maximum(pre_, 0.0)),))
    d_w_up = _matmul("d_mlp_up", xn2, dpre, "tn", [F32], tm=512, tn=1024, tk=4096)
    dxn2 = _matmul("dx_mlp_up", dpre, w["w_up"], "nt", [F32], tm=512, tn=1024, tk=D_FF)
    dh1, d_norm_mlp = _rowwise("rms_mlp_bwd", norm_bwd, [h1, dxn2, dh2], [w["norm_mlp_g"]], [(D_MODEL, F32)], [(1, D_MODEL)])
    d_w_out = _matmul("d_out_proj", ycat, dh1, "tn", [F32], tm=512, tn=512, tk=4096)
    dycat = _matmul("dx_out_proj", dh1, w["w_out"], "nt", [F32], tm=1024, tn=1024, tk=D_MODEL)
    late_grads = dict(w_out=d_w_out, w_up=d_w_up, w_down=d_w_down, w_ple_gate=d_w_ple_gate, w_ple_proj=d_w_ple_proj)

    def conv_bwd(dy, gb, gc, hx, cw):
        uu = gc * hx
        u1, u2 = _shift_down(uu, 1), _shift_down(uu, 2)
        dconv = dy * gb
        du = dconv * cw[2:3] + _shift_up(dconv, 1) * cw[1:2] + _shift_up(dconv, 2) * cw[0:1]
        s = lambda z: jnp.sum(z, axis=0, keepdims=True)
        return (dy * (uu * cw[2:3] + u1 * cw[1:2] + u2 * cw[0:1]), du * hx, du * gc,
                s(dconv * u2), s(dconv * u1), s(dconv * uu))

    dgb, dgc, dhx, dcw0, dcw1, dcw2 = _colwise(
        "conv_bwd", conv_bwd, [(dycat, 0), proj_gb, proj_gc, proj_h], [(w["conv_w"], 0)],
        [(CONV_DIM, BF16)] * 3, [(1, CONV_DIM)] * 3)

    def post_bwd(dy, y, r, k_h_, v, g_, ln_g, ln_b, r_k):
        _, vjp = jax.vjp(_rwkv_post, y, r, k_h_, v, g_, ln_g, ln_b, r_k)
        return vjp(dy)

    dy_rec, dr_p, dk_p, dv_p, dg, d_ln_g, d_ln_b, d_r_k = _rowwise(
        "rwkv_post_bwd", post_bwd, [(dycat, 512, 1), y_rec, u_r, k_h, u_v, g], post_c,
        [(RWKV_DIM, F32)] * 5, [(1, RWKV_DIM)] * 3)
    (dr_r, dv_r, dlw, dk_r, da, db), late_parts = _rec_bwd(
        u, lw, k_h, ra, rb, zs, dy_rec, [_reshard(n, late_grads[n]) for n in _LATE], [True] * len(_LATE))

    def pre_bwd(k, xw, xa, xg, dr1, dr2, dv1, dv2, dlw_, dk1, dk2, da_, db_, dg_, w0, a0, k_k, k_a, wl, al, gl):
        _, vjp = jax.vjp(_rwkv_pre, k, xw, xa, xg, w0, a0, k_k, k_a, wl, al, gl)
        dk, dxw, dxa, dxg, dw0, da0, dk_k, dk_a, dwl, dal, dgl = vjp((dlw_, dk1 + dk2, da_, db_, dg_))
        du = jnp.concatenate([dr1 + dr2, dk, dv1 + dv2, dxw, dxa, dxg], axis=1)
        return du, dw0, da0, dk_k, dk_a, dwl, dal, dgl

    du, d_w0, d_a0, d_k_k, d_k_a, d_wl, d_al, d_gl = _rowwise(
        "rwkv_pre_bwd", pre_bwd, [u_k, u_xw, u_xa, u_xg, dr_p, dr_r, dv_p, dv_r, dlw, dk_p, dk_r, da, db, dg], small,
        [(RW_PAD, F32)], [(1, RWKV_DIM)] * 4 + [(LANE, RWKV_DIM), (LANE, RWKV_DIM), (2 * LANE, RWKV_DIM)])

    def shift_bwd(du_, rr, mu):
        return du_ - mu * du_ + mu * _shift_up(du_, 1), jnp.sum(du_ * (_shift_down(rr, 1) - rr), axis=0, keepdims=True)

    dprojr, d_mu = _colwise("shift_bwd", shift_bwd, [(du, 0), proj_rw], [(w["shift_mu"], 0)], [(RW_PAD, BF16)], [(1, RW_PAD)])
    dproj = jnp.concatenate([dgb, dgc, dhx, dprojr], axis=1)
    d_w_in = _matmul("d_in_proj", xn1, dproj, "tn", [F32], tm=512, tn=896, tk=4096)
    dxn1 = _matmul("dx_in_proj", dproj, w["w_in"], "nt", [F32], tm=512, tn=1024, tk=IN_PAD)
    dx, d_norm_mix = _rowwise("rms_mix_bwd", norm_bwd, [x, dxn1, dh1], [w["norm_mix_g"]], [(D_MODEL, F32)], [(1, D_MODEL)])

    grads = dict(
        norm_mix_g=d_norm_mix, w_in=d_w_in, conv_w=jnp.concatenate([dcw0, dcw1, dcw2], axis=0), shift_mu=d_mu,
        w_lora_up=d_wl, w0=d_w0, a_lora_up=d_al, a0=d_a0, g_lora_up=d_gl, k_k=d_k_k, k_a=d_k_a, r_k=d_r_k,
        ln_x_g=d_ln_g, ln_x_b=d_ln_b, norm_mlp_g=d_norm_mlp, norm_ple_g=d_norm_ple, norm_final_g=d_norm_final)
    return loss, dx, grads, late_parts


def _adamw(name, parts, w, m, v):
    rows, cols = w.shape
    tr = rows if rows * cols * 4 * 8 <= (4 << 20) else max(8, (4 << 20) // (cols * 4 * 8) // 8 * 8)
    while rows % tr:
        tr -= 8

    def body(p_ref, w_ref, m_ref, v_ref, g_ref, d_ref, nm_ref, nv_ref):
        g = p_ref[0]
        for s in range(1, N_DEV):
            g = g + p_ref[s]
        nm = ADAM_B1 * m_ref[...] + (1.0 - ADAM_B1) * g
        nv = ADAM_B2 * v_ref[...] + (1.0 - ADAM_B2) * (g * g)
        m_hat = nm / (1.0 - ADAM_B1 ** ADAM_STEP)
        v_hat = nv / (1.0 - ADAM_B2 ** ADAM_STEP)
        g_ref[...] = g
        d_ref[...] = -ADAM_LR * (m_hat / (jnp.sqrt(v_hat) + ADAM_EPS) + ADAM_WD * w_ref[...])
        nm_ref[...] = nm
        nv_ref[...] = nv

    blk = pl.BlockSpec((tr, cols), lambda i: (i, 0))
    return pl.pallas_call(
        body, name=name, grid=(rows // tr,),
        in_specs=[pl.BlockSpec((N_DEV, tr, cols), lambda i: (0, i, 0)), blk, blk, blk], out_specs=[blk] * 4,
        out_shape=[jax.ShapeDtypeStruct((rows, cols), F32)] * 4,
        compiler_params=_params(("arbitrary",)),
    )(parts, w, m, v)


def kernel(x, p, norm_mix_g, w_in, conv_w, shift_mu, w_lora_up, w0, a_lora_up, a0, g_lora_up, k_k, k_a, r_k, ln_x_g, ln_x_b, w_out, norm_mlp_g, w_up, w_down, norm_ple_g, w_ple_gate, w_ple_proj, norm_final_g, loss_target, m_norm_mix_g, m_w_in, m_conv_w, m_shift_mu, m_w_lora_up, m_w0, m_a_lora_up, m_a0, m_g_lora_up, m_k_k, m_k_a, m_r_k, m_ln_x_g, m_ln_x_b, m_w_out, m_norm_mlp_g, m_w_up, m_w_down, m_norm_ple_g, m_w_ple_gate, m_w_ple_proj, m_norm_final_g, v_norm_mix_g, v_w_in, v_conv_w, v_shift_mu, v_w_lora_up, v_w0, v_a_lora_up, v_a0, v_g_lora_up, v_k_k, v_k_a, v_r_k, v_ln_x_g, v_ln_x_b, v_w_out, v_norm_mlp_g, v_w_up, v_w_down, v_norm_ple_g, v_w_ple_gate, v_w_ple_proj, v_norm_final_g):
    args = dict(locals())
    wts = {n: args[n] for n in _WEIGHTS}
    mom = {n: args["m_" + n] for n in _WEIGHTS}
    var = {n: args["v_" + n] for n in _WEIGHTS}
    shard2d = lambda a: a.reshape(a.shape[-2:])
    pad_mu = lambda a: _pad_in_cols(jnp.concatenate([jnp.zeros((1, CONV_COLS), F32), a], axis=1))[:, CONV_COLS:]
    unpad_mu = lambda a: _unpad_in_cols(jnp.concatenate([jnp.zeros((1, CONV_COLS), F32), a], axis=1))[:, CONV_COLS:]

    shards = {n: shard2d(wts[n]).astype(BF16 if n in _BF16_GATHER else F32) for n in _SHARDED}
    gathered = _exchange("gather_early", [shards[n] for n in _EARLY], [False] * len(_EARLY))
    w = {n: _unshard(n, g) for n, g in zip(_EARLY, gathered)}
    w["w_in"] = _pad_in_cols(w["w_in"])
    w["w_lora_up"] = _pad_rows(w["w_lora_up"], LANE)
    w["a_lora_up"] = _pad_rows(w["a_lora_up"], LANE)
    w["g_lora_up"] = _pad_rows(w["g_lora_up"], 2 * LANE)
    for n in _REPLICATED:
        w[n] = wts[n].reshape(1, -1)
    w["shift_mu"] = pad_mu(wts["shift_mu"])

    loss, dx, grads, late_parts = _local_step(x[0], p[0, 0], loss_target[0], w, [shards[n] for n in _LATE])

    grads["w_in"] = _unpad_in_cols(grads["w_in"])
    grads["shift_mu"] = unpad_mu(grads["shift_mu"])
    grads["w_lora_up"] = grads["w_lora_up"][:64]
    grads["a_lora_up"] = grads["a_lora_up"][:64]
    grads["g_lora_up"] = grads["g_lora_up"][:160]
    fin = _exchange("exchange_final", [_reshard(n, grads[n]) for n in _EARLY] + [_pack([grads[n] for n in _REPLICATED] + [loss])],
                    [True] * len(_EARLY) + [False])
    parts = dict(zip(_EARLY, fin[:-1]))
    parts.update(zip(_LATE, late_parts))

    out = {}
    for n in _SHARDED:
        res = _adamw("adamw_" + n, parts[n], shard2d(wts[n]), shard2d(mom[n]), shard2d(var[n]))
        out[n] = [r.reshape(wts[n].shape) for r in res]
    sm = _adamw("adamw_small", fin[-1], _pack([wts[n] for n in _REPLICATED]), _pack([mom[n] for n in _REPLICATED]),
                _pack([var[n] for n in _REPLICATED]))
    off = 0
    for n in _REPLICATED:
        size = wts[n].size
        out[n] = [r.reshape(-1)[off:off + size].reshape(wts[n].shape) for r in sm]
        off += size
    loss_total = sm[0].reshape(-1)[off]
    return (loss_total, dx[None], *[out[n][0] for n in _WEIGHTS], *[out[n][1] for n in _WEIGHTS],
            *[out[n][2] for n in _WEIGHTS], *[out[n][3] for n in _WEIGHTS])
```

```python
import functools

import jax
import jax.numpy as jnp
from jax import lax
from jax.experimental import pallas as pl
from jax.experimental.pallas import tpu as pltpu

F32 = jnp.float32
BF16 = jnp.bfloat16

N_DEV = 8
D_MODEL = 1024
CONV_DIM = 512
RWKV_DIM = 512
HEAD_DIM = 64
N_HEADS = 8
D_FF = 4096
PLE_DIM = 256
RMS_EPS = 1e-6
GN_EPS = 64e-5
L2_EPS = 1e-12
ADAM_LR, ADAM_B1, ADAM_B2, ADAM_EPS, ADAM_WD, ADAM_STEP = 0.001, 0.9, 0.999, 1e-08, 0.01, 10

CONV_COLS = 3 * CONV_DIM
RW_PAD = 2048
IN_PAD = CONV_COLS + RW_PAD
XW_OFF, XA_OFF, XG_OFF = 1536, 1664, 1792
REC_CHUNK = 128
REC_PASSES = 1
ROW_BLOCK = 256
LANE = 128
VMEM_LIMIT = 56 * 1024 * 1024


def _dims(dn, ndim):
    if ndim == 3:
        return {"nn": (((2,), (1,)), ((0,), (0,))), "nt": (((2,), (2,)), ((0,), (0,))),
                "tn": (((1,), (1,)), ((0,), (0,)))}[dn]
    return {"nn": (((1,), (0,)), ((), ())), "nt": (((1,), (1,)), ((), ())), "tn": (((0,), (0,)), ((), ()))}[dn]


def _split2(x):
    hi = x.astype(BF16)
    return hi, (x - hi.astype(F32)).astype(BF16)


def _mm_raw(x, y, dn, passes):
    f = lambda p, q: lax.dot_general(p, q, _dims(dn, x.ndim), preferred_element_type=F32)
    if passes == 1:
        return f(x.astype(BF16), y.astype(BF16))
    xh, xl = _split2(x)
    yh, yl = _split2(y)
    return f(xh, yh) + f(xh, yl) + f(xl, yh)


@functools.partial(jax.custom_vjp, nondiff_argnums=(2, 3))
def _mm(x, y, dn, passes):
    return _mm_raw(x, y, dn, passes)


def _mm_fwd(x, y, dn, passes):
    return _mm_raw(x, y, dn, passes), (x, y)


def _mm_bwd(dn, passes, res, d):
    x, y = res
    if dn == "nn":
        return _mm(d, y, "nt", passes), _mm(x, d, "tn", passes)
    if dn == "nt":
        return _mm(d, y, "nn", passes), _mm(d, x, "tn", passes)
    return _mm(y, d, "nt", passes), _mm(x, d, "nn", passes)


_mm.defvjp(_mm_fwd, _mm_bwd)


def _head_ones():
    i = lax.broadcasted_iota(jnp.int32, (RWKV_DIM, RWKV_DIM), 0) // HEAD_DIM
    j = lax.broadcasted_iota(jnp.int32, (RWKV_DIM, RWKV_DIM), 1) // HEAD_DIM
    return (i == j).astype(BF16)


def _hsum_raw(x):
    ones = _head_ones()
    f = lambda p: lax.dot_general(p, ones, _dims("nn", 2), preferred_element_type=F32)
    x1 = x.astype(BF16)
    r1 = x - x1.astype(F32)
    x2 = r1.astype(BF16)
    x3 = (r1 - x2.astype(F32)).astype(BF16)
    return f(x1) + f(x2) + f(x3)


@jax.custom_vjp
def _hsum(x):
    return _hsum_raw(x)


_hsum.defvjp(lambda x: (_hsum_raw(x), None), lambda _, d: (_hsum(d),))


def _sigmoid(x):
    return 1.0 / (1.0 + jnp.exp(-x))


def _softplus(x):
    return jnp.maximum(x, 0.0) + jnp.log(1.0 + jnp.exp(-jnp.abs(x)))


def _params(sem):
    return pltpu.CompilerParams(dimension_semantics=sem, vmem_limit_bytes=VMEM_LIMIT)


def _rowwise(name, fn, rows, consts, row_outs, acc_outs=(), tr=ROW_BLOCK):
    rows = [r if isinstance(r, tuple) else (r, r.shape[1], 0) for r in rows]
    t_len = rows[0][0].shape[0]
    tr = min(tr, t_len)
    n_r, n_c, n_o, n_a = len(rows), len(consts), len(row_outs), len(acc_outs)

    def body(*refs):
        ins = [r[...] for r in refs[:n_r + n_c]]
        outs = fn(*ins)
        o_refs = refs[n_r + n_c:n_r + n_c + n_o]
        a_refs = refs[n_r + n_c + n_o:]
        for o_ref, val in zip(o_refs, outs[:n_o]):
            o_ref[...] = val.astype(o_ref.dtype)
        if n_a:
            first = pl.program_id(0) == 0

            @pl.when(first)
            def _():
                for a_ref, val in zip(a_refs, outs[n_o:]):
                    a_ref[...] = val

            @pl.when(jnp.logical_not(first))
            def _():
                for a_ref, val in zip(a_refs, outs[n_o:]):
                    a_ref[...] += val

    in_specs = [pl.BlockSpec((tr, w), functools.partial(lambda i, c: (i, c), c=cb)) for _, w, cb in rows]
    in_specs += [pl.BlockSpec(c.shape, functools.partial(lambda i, n: (0,) * n, n=c.ndim)) for c in consts]
    out_specs = [pl.BlockSpec((tr, w), lambda i: (i, 0)) for w, _ in row_outs]
    out_specs += [pl.BlockSpec(s, functools.partial(lambda i, n: (0,) * n, n=len(s))) for s in acc_outs]
    out_shape = [jax.ShapeDtypeStruct((t_len, w), dt) for w, dt in row_outs]
    out_shape += [jax.ShapeDtypeStruct(s, F32) for s in acc_outs]
    return pl.pallas_call(
        body, name=name, grid=(t_len // tr,), in_specs=in_specs, out_specs=out_specs, out_shape=out_shape,
        compiler_params=_params(("arbitrary",)),
    )(*[r[0] for r in rows], *consts)


def _colwise(name, fn, cols, prms, col_outs, prm_outs=()):
    t_len = cols[0][0].shape[0]
    n_blocks = col_outs[0][0] // LANE
    n_i = len(cols) + len(prms)

    def body(*refs):
        outs = fn(*[r[...] for r in refs[:n_i]])
        for o_ref, val in zip(refs[n_i:], outs):
            o_ref[...] = val.astype(o_ref.dtype)

    def spec(r, off):
        return pl.BlockSpec((r, LANE), functools.partial(lambda j, o: (0, o + j), o=off))

    in_specs = [spec(t_len, off) for _, off in cols] + [spec(a.shape[0], off) for a, off in prms]
    out_specs = [spec(t_len, 0) for _ in col_outs] + [spec(r, 0) for r, _ in prm_outs]
    out_shape = [jax.ShapeDtypeStruct((t_len, w), dt) for w, dt in col_outs]
    out_shape += [jax.ShapeDtypeStruct((r, w), F32) for r, w in prm_outs]
    return pl.pallas_call(
        body, name=name, grid=(n_blocks,), in_specs=in_specs, out_specs=out_specs, out_shape=out_shape,
        compiler_params=_params(("arbitrary",)),
    )(*[c[0] for c in cols], *[p[0] for p in prms])


def _matmul(name, a, b, dn, outs, *, tm, tn, tk, extras=(), epilogue=None):
    if dn == "nn":
        (m, k), n = a.shape, b.shape[1]
    elif dn == "nt":
        (m, k), n = a.shape, b.shape[0]
    else:
        (k, m), n = a.shape, b.shape[1]
    tm, tn, tk = min(tm, m), min(tn, n), min(tk, k)
    nk = k // tk
    a_spec = pl.BlockSpec((tk, tm), lambda i, j, q: (q, i)) if dn == "tn" else pl.BlockSpec((tm, tk), lambda i, j, q: (i, q))
    b_spec = pl.BlockSpec((tn, tk), lambda i, j, q: (j, q)) if dn == "nt" else pl.BlockSpec((tk, tn), lambda i, j, q: (q, j))
    o_spec = pl.BlockSpec((tm, tn), lambda i, j, q: (i, j))
    n_e, n_o = len(extras), len(outs)

    def body(*refs):
        a_ref, b_ref = refs[:2]
        e_refs = refs[2:2 + n_e]
        o_refs = refs[2 + n_e:2 + n_e + n_o]
        part = lax.dot_general(a_ref[...].astype(BF16), b_ref[...].astype(BF16), _dims(dn, 2), preferred_element_type=F32)

        def finish(acc):
            vals = (acc,) if epilogue is None else epilogue(acc, *[e[...] for e in e_refs])
            for o_ref, val in zip(o_refs, vals):
                o_ref[...] = val.astype(o_ref.dtype)

        if nk == 1:
            finish(part)
            return
        acc_ref = refs[-1]
        q = pl.program_id(2)

        @pl.when(q == 0)
        def _():
            acc_ref[...] = part

        @pl.when(q > 0)
        def _():
            acc_ref[...] += part

        @pl.when(q == nk - 1)
        def _():
            finish(acc_ref[...])

    res = pl.pallas_call(
        body, name=name, grid=(m // tm, n // tn, nk),
        in_specs=[a_spec, b_spec] + [o_spec] * n_e, out_specs=[o_spec] * n_o,
        out_shape=[jax.ShapeDtypeStruct((m, n), dt) for dt in outs],
        scratch_shapes=[pltpu.VMEM((tm, tn), F32)] if nk > 1 else [],
        compiler_params=_params(("parallel", "parallel", "arbitrary")),
    )(a, b, *extras)
    return res[0] if n_o == 1 else res


def _rms(h, g):
    return h * lax.rsqrt(jnp.mean(h * h, axis=-1, keepdims=True) + RMS_EPS) * g


def _rms_bwd(h, g, dy):
    rs = lax.rsqrt(jnp.mean(h * h, axis=-1, keepdims=True) + RMS_EPS)
    n = h * rs
    dn = dy * g
    dh = rs * (dn - n * jnp.mean(dn * n, axis=-1, keepdims=True))
    return dh, jnp.sum(dy * n, axis=0, keepdims=True)


def _rwkv_pre(k, xw, xa, xg, w0, a0, k_k, k_a, wl, al, gl):
    zw = w0 + _mm(jnp.tanh(xw), wl, "nn", 1)
    lw = -jnp.exp(-_softplus(-zw) - 0.5)
    iclr = _sigmoid(a0 + _mm(xa, al, "nn", 1))
    g = _mm(_sigmoid(xg), gl, "nn", 1)
    kk0 = k * k_k
    kk = kk0 / jnp.maximum(jnp.sqrt(_hsum(kk0 * kk0)), L2_EPS)
    k_h = k * (1.0 + (iclr - 1.0) * k_a)
    return lw, k_h, -kk, kk * iclr, g


def _rwkv_post(y, r, k_h, v, g, ln_g, ln_b, r_k):
    mu = _hsum(y) * (1.0 / HEAD_DIM)
    yc = y - mu
    var = _hsum(yc * yc) * (1.0 / HEAD_DIM)
    yo = yc * lax.rsqrt(var + GN_EPS) * ln_g + ln_b
    bonus = _hsum(r * k_h * r_k) * v
    return (yo + bonus) * g


def _shift_down(x, n):
    rows = lax.broadcasted_iota(jnp.int32, x.shape, 0)
    return jnp.where(rows < n, 0.0, pltpu.roll(x, n, 0))


def _shift_up(x, n):
    t_len = x.shape[0]
    rows = lax.broadcasted_iota(jnp.int32, x.shape, 0)
    return jnp.where(rows >= t_len - n, 0.0, pltpu.roll(x, t_len - n, 0))


def _exchange_plan(ins, outs, scatter, send_sems, recv_sems, local_sems):
    x, y, c = lax.axis_index("x"), lax.axis_index("y"), lax.axis_index("c")
    me = 4 * x + 2 * y + c

    def local(i):
        return pltpu.make_async_copy(ins[i].at[me] if scatter[i] else ins[i], outs[i].at[me], local_sems.at[i])

    def send(i, rel):
        return pltpu.make_async_remote_copy(
            src_ref=ins[i].at[me ^ rel] if scatter[i] else ins[i], dst_ref=outs[i].at[me],
            send_sem=send_sems.at[i, rel - 1], recv_sem=recv_sems.at[i, rel - 1],
            device_id=(x ^ (rel >> 2), y ^ ((rel >> 1) & 1), c ^ (rel & 1)), device_id_type=pl.DeviceIdType.MESH)

    def landed(i, rel):
        slot = outs[i].at[me ^ rel]
        return pltpu.make_async_remote_copy(
            src_ref=slot, dst_ref=slot, send_sem=send_sems.at[i, rel - 1], recv_sem=recv_sems.at[i, rel - 1],
            device_id=(x, y, c), device_id_type=pl.DeviceIdType.MESH)

    def start():
        for i in range(len(ins)):
            local(i).start()
            for rel in range(1, N_DEV):
                send(i, rel).start()

    def wait():
        for i in range(len(ins)):
            local(i).wait()
            for rel in range(1, N_DEV):
                landed(i, rel).wait_recv()
            for rel in range(1, N_DEV):
                send(i, rel).wait_send()

    return start, wait


def _exchange_io(arrays, scatter):
    n = len(arrays)
    any_spec = pl.BlockSpec(memory_space=pl.ANY)
    out_shape = [jax.ShapeDtypeStruct(a.shape if sc else (N_DEV,) + a.shape, a.dtype) for a, sc in zip(arrays, scatter)]
    sems = [pltpu.SemaphoreType.DMA((n, N_DEV - 1)), pltpu.SemaphoreType.DMA((n, N_DEV - 1)), pltpu.SemaphoreType.DMA((n,))]
    return [any_spec] * n, out_shape, sems


def _exchange(name, arrays, scatter):
    n = len(arrays)
    specs, out_shape, sems = _exchange_io(arrays, scatter)

    def body(*refs):
        start, wait = _exchange_plan(refs[:n], refs[n:2 * n], scatter, *refs[2 * n:])
        start()
        wait()

    return pl.pallas_call(
        body, name=name, in_specs=specs, out_specs=specs, out_shape=out_shape, scratch_shapes=sems,
        compiler_params=pltpu.CompilerParams(has_side_effects=True),
    )(*arrays)


def _chunk_fwd(z0, r, lw, k, v, a, b):
    n_h, c, n_k = r.shape
    mm = functools.partial(_mm, passes=REC_PASSES)
    ti = lax.broadcasted_iota(jnp.int32, (c, c), 0)
    si = lax.broadcasted_iota(jnp.int32, (c, c), 1)
    strict, incl = si < ti, si <= ti
    cum = _mm(jnp.broadcast_to(incl.astype(F32), (n_h, c, c)), lw, "nn", 3)
    cum_end = cum[:, c - 1:c, :]
    at = a * jnp.exp(cum - lw)
    bt = b * jnp.exp(-cum)
    kt = k * jnp.exp(-cum)
    rt = r * jnp.exp(cum)
    be = b * jnp.exp(cum_end - cum)
    ke = k * jnp.exp(cum_end - cum)
    lab = jnp.where(strict, mm(at, bt, "nt"), 0.0)
    lak = jnp.where(strict, mm(at, kt, "nt"), 0.0)
    u = mm(at, z0, "nn") + mm(lak, v, "nn")
    p = lab
    n = 1
    while n < c:
        u = u + mm(p, u, "nn")
        n *= 2
        if n < c:
            p = mm(p, p, "nn")
    rb = jnp.where(incl, mm(rt, bt, "nt"), 0.0)
    rk = jnp.where(incl, mm(rt, kt, "nt"), 0.0)
    y = mm(rt, z0, "nn") + mm(rb, u, "nn") + mm(rk, v, "nn")
    ki = lax.broadcasted_iota(jnp.int32, (n_k, n_k), 0)
    kj = lax.broadcasted_iota(jnp.int32, (n_k, n_k), 1)
    dmat = jnp.where(ki == kj, jnp.broadcast_to(jnp.exp(cum_end), (n_h, n_k, n_k)), 0.0)
    z_end = mm(dmat, z0, "nn") + mm(be, u, "tn") + mm(ke, v, "tn")
    return y, z_end


def _heads(x):
    return jnp.stack([x[:, h * HEAD_DIM:(h + 1) * HEAD_DIM] for h in range(N_HEADS)])


def _unheads(x):
    return jnp.concatenate([x[h] for h in range(N_HEADS)], axis=-1)


def _rec_params():
    return pltpu.CompilerParams(dimension_semantics=("arbitrary",), vmem_limit_bytes=VMEM_LIMIT, has_side_effects=True)


def _rec_fwd(u, lw, k, a, b, xch, xch_scatter):
    t_len = lw.shape[0]
    c = min(REC_CHUNK, t_len)
    nc = t_len // c
    n_x = len(xch)
    x_specs, x_shapes, x_sems = _exchange_io(xch, xch_scatter)

    def body(*refs):
        r_ref, v_ref, lw_ref, k_ref, a_ref, b_ref = refs[:6]
        x_in = refs[6:6 + n_x]
        y_ref, zs_ref = refs[6 + n_x:8 + n_x]
        x_out = refs[8 + n_x:8 + 2 * n_x]
        z_scr = refs[8 + 2 * n_x]
        start, wait = _exchange_plan(x_in, x_out, xch_scatter, *refs[9 + 2 * n_x:])
        i = pl.program_id(0)

        @pl.when(i == 0)
        def _():
            start()
            z_scr[...] = jnp.zeros_like(z_scr)

        z0 = z_scr[...]
        zs_ref[0] = z0
        y, z_end = _chunk_fwd(z0, _heads(r_ref[...]), _heads(lw_ref[...]), _heads(k_ref[...]), _heads(v_ref[...]),
                              _heads(a_ref[...]), _heads(b_ref[...]))
        y_ref[...] = _unheads(y)
        z_scr[...] = z_end

        @pl.when(i == nc - 1)
        def _():
            wait()

    blk = lambda cb: pl.BlockSpec((c, RWKV_DIM), functools.partial(lambda i, q: (i, q), q=cb))
    res = pl.pallas_call(
        body, name="rwkv_rec_fwd", grid=(nc,),
        in_specs=[blk(0), blk(2)] + [blk(0)] * 4 + x_specs,
        out_specs=[blk(0), pl.BlockSpec((1, N_HEADS, HEAD_DIM, HEAD_DIM), lambda i: (i, 0, 0, 0))] + x_specs,
        out_shape=[jax.ShapeDtypeStruct((t_len, RWKV_DIM), F32),
                   jax.ShapeDtypeStruct((nc, N_HEADS, HEAD_DIM, HEAD_DIM), F32)] + x_shapes,
        scratch_shapes=[pltpu.VMEM((N_HEADS, HEAD_DIM, HEAD_DIM), F32)] + x_sems,
        compiler_params=_rec_params(),
    )(u, u, lw, k, a, b, *xch)
    return res[0], res[1], res[2:]


def _rec_bwd(u, lw, k, a, b, zs, dy, xch, xch_scatter):
    t_len = lw.shape[0]
    c = min(REC_CHUNK, t_len)
    nc = t_len // c
    n_x = len(xch)
    x_specs, x_shapes, x_sems = _exchange_io(xch, xch_scatter)

    def body(*refs):
        r_ref, v_ref, lw_ref, k_ref, a_ref, b_ref, zs_ref, dy_ref = refs[:8]
        x_in = refs[8:8 + n_x]
        g_refs = refs[8 + n_x:14 + n_x]
        x_out = refs[14 + n_x:14 + 2 * n_x]
        dz_scr = refs[14 + 2 * n_x]
        start, wait = _exchange_plan(x_in, x_out, xch_scatter, *refs[15 + 2 * n_x:])
        i = pl.program_id(0)

        @pl.when(i == 0)
        def _():
            start()
            dz_scr[...] = jnp.zeros_like(dz_scr)

        _, vjp = jax.vjp(_chunk_fwd, zs_ref[0], _heads(r_ref[...]), _heads(lw_ref[...]), _heads(k_ref[...]),
                         _heads(v_ref[...]), _heads(a_ref[...]), _heads(b_ref[...]))
        dz0, dr, dlw, dk, dv, da, db = vjp((_heads(dy_ref[...]), dz_scr[...]))
        for ref, val in zip(g_refs, (dr, dv, dlw, dk, da, db)):
            ref[...] = _unheads(val)
        dz_scr[...] = dz0

        @pl.when(i == nc - 1)
        def _():
            wait()

    blk = lambda cb: pl.BlockSpec((c, RWKV_DIM), functools.partial(lambda i, q: (nc - 1 - i, q), q=cb))
    res = pl.pallas_call(
        body, name="rwkv_rec_bwd", grid=(nc,),
        in_specs=[blk(0), blk(2)] + [blk(0)] * 4
                 + [pl.BlockSpec((1, N_HEADS, HEAD_DIM, HEAD_DIM), lambda i: (nc - 1 - i, 0, 0, 0)), blk(0)] + x_specs,
        out_specs=[blk(0)] * 6 + x_specs,
        out_shape=[jax.ShapeDtypeStruct((t_len, RWKV_DIM), F32)] * 6 + x_shapes,
        scratch_shapes=[pltpu.VMEM((N_HEADS, HEAD_DIM, HEAD_DIM), F32)] + x_sems,
        compiler_params=_rec_params(),
    )(u, u, lw, k, a, b, zs, dy, *xch)
    return res[:6], res[6:]


_EARLY = ["w_in", "conv_w", "w_lora_up", "a_lora_up", "g_lora_up"]
_LATE = ["w_out", "w_up", "w_down", "w_ple_gate", "w_ple_proj"]
_SHARDED = _EARLY + _LATE
_COL_SHARDED = {"w_in", "conv_w", "w_lora_up", "a_lora_up", "g_lora_up", "w_up", "w_ple_proj"}
_BF16_GATHER = {"w_in", "w_out", "w_up", "w_down", "w_ple_gate", "w_ple_proj"}
_REPLICATED = ["norm_mix_g", "shift_mu", "w0", "a0", "k_k", "k_a", "r_k", "ln_x_g", "ln_x_b", "norm_mlp_g", "norm_ple_g",
               "norm_final_g"]
_WEIGHTS = ["norm_mix_g", "w_in", "conv_w", "shift_mu", "w_lora_up", "w0", "a_lora_up", "a0", "g_lora_up", "k_k", "k_a", "r_k",
            "ln_x_g", "ln_x_b", "w_out", "norm_mlp_g", "w_up", "w_down", "norm_ple_g", "w_ple_gate", "w_ple_proj", "norm_final_g"]
_PACK_ROWS = 80


def _unshard(name, g):
    if name in _COL_SHARDED:
        return jnp.moveaxis(g, 0, 1).reshape(g.shape[1], N_DEV * g.shape[2])
    return g.reshape(N_DEV * g.shape[1], g.shape[2])


def _reshard(name, full):
    if name in _COL_SHARDED:
        return jnp.moveaxis(full.reshape(full.shape[0], N_DEV, full.shape[1] // N_DEV), 1, 0)
    return full.reshape(N_DEV, full.shape[0] // N_DEV, full.shape[1])


def _pad_in_cols(a):
    z = lambda n: jnp.zeros(a.shape[:-1] + (n,), a.dtype)
    return jnp.concatenate([a[..., :3136], z(64), a[..., 3136:3200], z(64), a[..., 3200:3360], z(96)], axis=-1)


def _unpad_in_cols(a):
    return jnp.concatenate([a[..., :3136], a[..., 3200:3264], a[..., 3328:3488]], axis=-1)


def _pad_rows(a, rows):
    return jnp.concatenate([a, jnp.zeros((rows - a.shape[0],) + a.shape[1:], a.dtype)], axis=0)


def _pack(vals):
    flat = jnp.concatenate([v.reshape(-1) for v in vals])
    return jnp.concatenate([flat, jnp.zeros((_PACK_ROWS * LANE - flat.shape[0],), F32)]).reshape(_PACK_ROWS, LANE)


def _local_step(x, p, tgt, w, late_shards):
    row = lambda v: v.reshape(1, -1)
    w = dict(w)

    (xn1,) = _rowwise("rms_mix", lambda h, g: (_rms(h, g),), [x], [w["norm_mix_g"]], [(D_MODEL, BF16)])
    proj = _matmul("in_proj", xn1, w["w_in"], "nn", [F32], tm=1024, tn=512, tk=D_MODEL)
    proj_gb, proj_gc, proj_h, proj_rw = (proj, 0), (proj, 4), (proj, 8), (proj, CONV_COLS // LANE)

    def conv_fwd(gb, gc, hx, cw):
        uu = gc * hx
        return (gb * (uu * cw[2:3] + _shift_down(uu, 1) * cw[1:2] + _shift_down(uu, 2) * cw[0:1]),)

    (y_conv,) = _colwise("conv_fwd", conv_fwd, [proj_gb, proj_gc, proj_h], [(w["conv_w"], 0)], [(CONV_DIM, BF16)])

    def shift_fwd(rr, mu):
        return (rr + mu * (_shift_down(rr, 1) - rr),)

    (u,) = _colwise("shift_fwd", shift_fwd, [proj_rw], [(w["shift_mu"], 0)], [(RW_PAD, F32)])

    small = [w["w0"], w["a0"], w["k_k"], w["k_a"], w["w_lora_up"], w["a_lora_up"], w["g_lora_up"]]
    u_k, u_xw, u_xa, u_xg = (u, 512, 1), (u, LANE, XW_OFF // LANE), (u, LANE, XA_OFF // LANE), (u, 2 * LANE, XG_OFF // (2 * LANE))
    lw, k_h, ra, rb, g = _rowwise("rwkv_pre", _rwkv_pre, [u_k, u_xw, u_xa, u_xg], small, [(RWKV_DIM, F32)] * 5)
    y_rec, zs, late = _rec_fwd(u, lw, k_h, ra, rb, late_shards, [False] * len(_LATE))
    for n, gathered in zip(_LATE, late):
        w[n] = _unshard(n, gathered)
    post_c = [w["ln_x_g"], w["ln_x_b"], w["r_k"]]
    u_r, u_v = (u, 512, 0), (u, 512, 2)
    (y_rwkv,) = _rowwise("rwkv_post", lambda *xs: (_rwkv_post(*xs),), [y_rec, u_r, k_h, u_v, g], post_c, [(RWKV_DIM, BF16)])
    ycat = jnp.concatenate([y_conv, y_rwkv], axis=1)
    res = lambda acc, r_: (acc + r_,)
    h1 = _matmul("out_proj", ycat, w["w_out"], "nn", [F32], tm=1024, tn=1024, tk=D_MODEL, extras=[x], epilogue=res)
    (xn2,) = _rowwise("rms_mlp", lambda h, g_: (_rms(h, g_),), [h1], [w["norm_mlp_g"]], [(D_MODEL, BF16)])

    def relu2(acc):
        hid = jnp.maximum(acc, 0.0)
        return acc, hid * hid

    pre, hsq = _matmul("mlp_up", xn2, w["w_up"], "nn", [F32, BF16], tm=1024, tn=1024, tk=D_MODEL, epilogue=relu2)
    h2 = _matmul("mlp_down", hsq, w["w_down"], "nn", [F32], tm=512, tn=1024, tk=D_FF, extras=[h1], epilogue=res)
    (xn3,) = _rowwise("rms_ple", lambda h, g_: (_rms(h, g_),), [h2], [w["norm_ple_g"]], [(D_MODEL, BF16)])
    zg = _matmul("ple_gate", xn3, w["w_ple_gate"], "nn", [F32], tm=1024, tn=1024, tk=D_MODEL)
    pp = _matmul("ple_proj", p, w["w_ple_proj"], "nn", [F32], tm=1024, tn=1024, tk=PLE_DIM)

    def head(h2_, zg_, pp_, tg, gf):
        h3 = h2_ + _sigmoid(zg_) * pp_
        out = _rms(h3, gf)
        err = out - tg
        dh3, dgf = _rms_bwd(h3, gf, err * (1.0 / D_MODEL))
        loss = jnp.sum(jnp.sum(err * err, axis=1, keepdims=True), axis=0, keepdims=True) * (0.5 / D_MODEL)
        return dh3, dgf, loss

    dh3, d_norm_final, loss = _rowwise("head", head, [h2, zg, pp, tgt], [row(w["norm_final_g"])], [(D_MODEL, F32)],
                                       [(1, D_MODEL), (1, 1)])

    def ple_bwd(dh3_, zg_, pp_):
        gate = _sigmoid(zg_)
        return dh3_ * pp_ * gate * (1.0 - gate), dh3_ * gate

    dzg, dpp = _rowwise("ple_bwd", ple_bwd, [dh3, zg, pp], [], [(D_MODEL, BF16)] * 2)
    d_w_ple_proj = _matmul("d_ple_proj", p, dpp, "tn", [F32], tm=PLE_DIM, tn=1024, tk=4096)
    d_w_ple_gate = _matmul("d_ple_gate", xn3, dzg, "tn", [F32], tm=512, tn=1024, tk=4096)
    dxn3 = _matmul("dx_ple_gate", dzg, w["w_ple_gate"], "nt", [F32], tm=1024, tn=1024, tk=D_MODEL)

    def norm_bwd(h, dy, dres, g_):
        dh, dg = _rms_bwd(h, g_, dy)
        return dh + dres, dg

    dh2, d_norm_ple = _rowwise("rms_ple_bwd", norm_bwd, [h2, dxn3, dh3], [w["norm_ple_g"]], [(D_MODEL, F32)], [(1, D_MODEL)])
    d_w_down = _matmul("d_mlp_down", hsq, dh2, "tn", [F32], tm=512, tn=512, tk=4096)
    dpre = _matmul("dx_mlp_down", dh2, w["w_down"], "nt", [BF16], tm=1024, tn=1024, tk=D_MODEL, extras=[pre],
                   epilogue=lambda acc, pre_: (acc * (2.0 * jnp.maximum(pre_, 0.0)),))
    d_w_up = _matmul("d_mlp_up", xn2, dpre, "tn", [F32], tm=512, tn=1024, tk=4096)
    dxn2 = _matmul("dx_mlp_up", dpre, w["w_up"], "nt", [F32], tm=512, tn=1024, tk=D_FF)
    dh1, d_norm_mlp = _rowwise("rms_mlp_bwd", norm_bwd, [h1, dxn2, dh2], [w["norm_mlp_g"]], [(D_MODEL, F32)], [(1, D_MODEL)])
    d_w_out = _matmul("d_out_proj", ycat, dh1, "tn", [F32], tm=512, tn=512, tk=4096)
    dycat = _matmul("dx_out_proj", dh1, w["w_out"], "nt", [F32], tm=1024, tn=1024, tk=D_MODEL)
    late_grads = dict(w_out=d_w_out, w_up=d_w_up, w_down=d_w_down, w_ple_gate=d_w_ple_gate, w_ple_proj=d_w_ple_proj)

    def conv_bwd(dy, gb, gc, hx, cw):
        uu = gc * hx
        u1, u2 = _shift_down(uu, 1), _shift_down(uu, 2)
        dconv = dy * gb
        du = dconv * cw[2:3] + _shift_up(dconv, 1) * cw[1:2] + _shift_up(dconv, 2) * cw[0:1]
        s = lambda z: jnp.sum(z, axis=0, keepdims=True)
        return (dy * (uu * cw[2:3] + u1 * cw[1:2] + u2 * cw[0:1]), du * hx, du * gc,
                s(dconv * u2), s(dconv * u1), s(dconv * uu))

    dgb, dgc, dhx, dcw0, dcw1, dcw2 = _colwise(
        "conv_bwd", conv_bwd, [(dycat, 0), proj_gb, proj_gc, proj_h], [(w["conv_w"], 0)],
        [(CONV_DIM, BF16)] * 3, [(1, CONV_DIM)] * 3)

    def post_bwd(dy, y, r, k_h_, v, g_, ln_g, ln_b, r_k):
        _, vjp = jax.vjp(_rwkv_post, y, r, k_h_, v, g_, ln_g, ln_b, r_k)
        return vjp(dy)

    dy_rec, dr_p, dk_p, dv_p, dg, d_ln_g, d_ln_b, d_r_k = _rowwise(
        "rwkv_post_bwd", post_bwd, [(dycat, 512, 1), y_rec, u_r, k_h, u_v, g], post_c,
        [(RWKV_DIM, F32)] * 5, [(1, RWKV_DIM)] * 3)
    (dr_r, dv_r, dlw, dk_r, da, db), late_parts = _rec_bwd(
        u, lw, k_h, ra, rb, zs, dy_rec, [_reshard(n, late_grads[n]) for n in _LATE], [True] * len(_LATE))

    def pre_bwd(k, xw, xa, xg, dr1, dr2, dv1, dv2, dlw_, dk1, dk2, da_, db_, dg_, w0, a0, k_k, k_a, wl, al, gl):
        _, vjp = jax.vjp(_rwkv_pre, k, xw, xa, xg, w0, a0, k_k, k_a, wl, al, gl)
        dk, dxw, dxa, dxg, dw0, da0, dk_k, dk_a, dwl, dal, dgl = vjp((dlw_, dk1 + dk2, da_, db_, dg_))
        du = jnp.concatenate([dr1 + dr2, dk, dv1 + dv2, dxw, dxa, dxg], axis=1)
        return du, dw0, da0, dk_k, dk_a, dwl, dal, dgl

    du, d_w0, d_a0, d_k_k, d_k_a, d_wl, d_al, d_gl = _rowwise(
        "rwkv_pre_bwd", pre_bwd, [u_k, u_xw, u_xa, u_xg, dr_p, dr_r, dv_p, dv_r, dlw, dk_p, dk_r, da, db, dg], small,
        [(RW_PAD, F32)], [(1, RWKV_DIM)] * 4 + [(LANE, RWKV_DIM), (LANE, RWKV_DIM), (2 * LANE, RWKV_DIM)])

    def shift_bwd(du_, rr, mu):
        return du_ - mu * du_ + mu * _shift_up(du_, 1), jnp.sum(du_ * (_shift_down(rr, 1) - rr), axis=0, keepdims=True)

    dprojr, d_mu = _colwise("shift_bwd", shift_bwd, [(du, 0), proj_rw], [(w["shift_mu"], 0)], [(RW_PAD, BF16)], [(1, RW_PAD)])
    dproj = jnp.concatenate([dgb, dgc, dhx, dprojr], axis=1)
    d_w_in = _matmul("d_in_proj", xn1, dproj, "tn", [F32], tm=512, tn=896, tk=4096)
    dxn1 = _matmul("dx_in_proj", dproj, w["w_in"], "nt", [F32], tm=512, tn=1024, tk=IN_PAD)
    dx, d_norm_mix = _rowwise("rms_mix_bwd", norm_bwd, [x, dxn1, dh1], [w["norm_mix_g"]], [(D_MODEL, F32)], [(1, D_MODEL)])

    grads = dict(
        norm_mix_g=d_norm_mix, w_in=d_w_in, conv_w=jnp.concatenate([dcw0, dcw1, dcw2], axis=0), shift_mu=d_mu,
        w_lora_up=d_wl, w0=d_w0, a_lora_up=d_al, a0=d_a0, g_lora_up=d_gl, k_k=d_k_k, k_a=d_k_a, r_k=d_r_k,
        ln_x_g=d_ln_g, ln_x_b=d_ln_b, norm_mlp_g=d_norm_mlp, norm_ple_g=d_norm_ple, norm_final_g=d_norm_final)
    return loss, dx, grads, late_parts


def _adamw(name, parts, w, m, v):
    rows, cols = w.shape
    tr = rows if rows * cols * 4 * 8 <= (4 << 20) else max(8, (4 << 20) // (cols * 4 * 8) // 8 * 8)
    while rows % tr:
        tr -= 8

    def body(p_ref, w_ref, m_ref, v_ref, g_ref, d_ref, nm_ref, nv_ref):
        g = p_ref[0].astype(F32)
        for s in range(1, N_DEV):
            g = g + p_ref[s].astype(F32)
        nm = ADAM_B1 * m_ref[...] + (1.0 - ADAM_B1) * g
        nv = ADAM_B2 * v_ref[...] + (1.0 - ADAM_B2) * (g * g)
        m_hat = nm / (1.0 - ADAM_B1 ** ADAM_STEP)
        v_hat = nv / (1.0 - ADAM_B2 ** ADAM_STEP)
        g_ref[...] = g
        d_ref[...] = -ADAM_LR * (m_hat / (jnp.sqrt(v_hat) + ADAM_EPS) + ADAM_WD * w_ref[...])
        nm_ref[...] = nm
        nv_ref[...] = nv

    blk = pl.BlockSpec((tr, cols), lambda i: (i, 0))
    return pl.pallas_call(
        body, name=name, grid=(rows // tr,),
        in_specs=[pl.BlockSpec((N_DEV, tr, cols), lambda i: (0, i, 0)), blk, blk, blk], out_specs=[blk] * 4,
        out_shape=[jax.ShapeDtypeStruct((rows, cols), F32)] * 4,
        compiler_params=_params(("arbitrary",)),
    )(parts, w, m, v)


def kernel(x, p, norm_mix_g, w_in, conv_w, shift_mu, w_lora_up, w0, a_lora_up, a0, g_lora_up, k_k, k_a, r_k, ln_x_g, ln_x_b, w_out, norm_mlp_g, w_up, w_down, norm_ple_g, w_ple_gate, w_ple_proj, norm_final_g, loss_target, m_norm_mix_g, m_w_in, m_conv_w, m_shift_mu, m_w_lora_up, m_w0, m_a_lora_up, m_a0, m_g_lora_up, m_k_k, m_k_a, m_r_k, m_ln_x_g, m_ln_x_b, m_w_out, m_norm_mlp_g, m_w_up, m_w_down, m_norm_ple_g, m_w_ple_gate, m_w_ple_proj, m_norm_final_g, v_norm_mix_g, v_w_in, v_conv_w, v_shift_mu, v_w_lora_up, v_w0, v_a_lora_up, v_a0, v_g_lora_up, v_k_k, v_k_a, v_r_k, v_ln_x_g, v_ln_x_b, v_w_out, v_norm_mlp_g, v_w_up, v_w_down, v_norm_ple_g, v_w_ple_gate, v_w_ple_proj, v_norm_final_g):
    args = dict(locals())
    wts = {n: args[n] for n in _WEIGHTS}
    mom = {n: args["m_" + n] for n in _WEIGHTS}
    var = {n: args["v_" + n] for n in _WEIGHTS}
    shard2d = lambda a: a.reshape(a.shape[-2:])
    pad_mu = lambda a: _pad_in_cols(jnp.concatenate([jnp.zeros((1, CONV_COLS), F32), a], axis=1))[:, CONV_COLS:]
    unpad_mu = lambda a: _unpad_in_cols(jnp.concatenate([jnp.zeros((1, CONV_COLS), F32), a], axis=1))[:, CONV_COLS:]

    shards = {n: shard2d(wts[n]).astype(BF16 if n in _BF16_GATHER else F32) for n in _SHARDED}
    gathered = _exchange("gather_early", [shards[n] for n in _EARLY], [False] * len(_EARLY))
    w = {n: _unshard(n, g) for n, g in zip(_EARLY, gathered)}
    w["w_in"] = _pad_in_cols(w["w_in"])
    w["w_lora_up"] = _pad_rows(w["w_lora_up"], LANE)
    w["a_lora_up"] = _pad_rows(w["a_lora_up"], LANE)
    w["g_lora_up"] = _pad_rows(w["g_lora_up"], 2 * LANE)
    for n in _REPLICATED:
        w[n] = wts[n].reshape(1, -1)
    w["shift_mu"] = pad_mu(wts["shift_mu"])

    loss, dx, grads, late_parts = _local_step(x[0], p[0, 0], loss_target[0], w, [shards[n] for n in _LATE])

    grads["w_in"] = _unpad_in_cols(grads["w_in"])
    grads["shift_mu"] = unpad_mu(grads["shift_mu"])
    grads["w_lora_up"] = grads["w_lora_up"][:64]
    grads["a_lora_up"] = grads["a_lora_up"][:64]
    grads["g_lora_up"] = grads["g_lora_up"][:160]
    early_parts = [_reshard(n, grads[n]).astype(BF16 if n == "w_in" else F32) for n in _EARLY]
    fin = _exchange("exchange_final", early_parts + [_pack([grads[n] for n in _REPLICATED] + [loss])],
                    [True] * len(_EARLY) + [False])
    parts = dict(zip(_EARLY, fin[:-1]))
    parts.update(zip(_LATE, late_parts))

    out = {}
    for n in _SHARDED:
        res = _adamw("adamw_" + n, parts[n], shard2d(wts[n]), shard2d(mom[n]), shard2d(var[n]))
        out[n] = [r.reshape(wts[n].shape) for r in res]
    sm = _adamw("adamw_small", fin[-1], _pack([wts[n] for n in _REPLICATED]), _pack([mom[n] for n in _REPLICATED]),
                _pack([var[n] for n in _REPLICATED]))
    off = 0
    for n in _REPLICATED:
        size = wts[n].size
        out[n] = [r.reshape(-1)[off:off + size].reshape(wts[n].shape) for r in sm]
        off += size
    loss_total = sm[0].reshape(-1)[off]
    return (loss_total, dx[None], *[out[n][0] for n in _WEIGHTS], *[out[n][1] for n in _WEIGHTS],
            *[out[n][2] for n in _WEIGHTS], *[out[n][3] for n in _WEIGHTS])
```

```python
import functools

import jax
import jax.numpy as jnp
from jax import lax
from jax.experimental import pallas as pl
from jax.experimental.pallas import tpu as pltpu

F32 = jnp.float32
BF16 = jnp.bfloat16

N_DEV = 8
D_MODEL = 1024
CONV_DIM = 512
RWKV_DIM = 512
HEAD_DIM = 64
N_HEADS = 8
D_FF = 4096
PLE_DIM = 256
RMS_EPS = 1e-6
GN_EPS = 64e-5
L2_EPS = 1e-12
ADAM_LR, ADAM_B1, ADAM_B2, ADAM_EPS, ADAM_WD, ADAM_STEP = 0.001, 0.9, 0.999, 1e-08, 0.01, 10

CONV_COLS = 3 * CONV_DIM
RW_PAD = 2048
IN_PAD = CONV_COLS + RW_PAD
XW_OFF, XA_OFF, XG_OFF = 1536, 1664, 1792
REC_CHUNK = 128
REC_PASSES = 1
ROW_BLOCK = 256
LANE = 128
VMEM_LIMIT = 56 * 1024 * 1024


def _dims(dn, ndim):
    if ndim == 3:
        return {"nn": (((2,), (1,)), ((0,), (0,))), "nt": (((2,), (2,)), ((0,), (0,))),
                "tn": (((1,), (1,)), ((0,), (0,)))}[dn]
    return {"nn": (((1,), (0,)), ((), ())), "nt": (((1,), (1,)), ((), ())), "tn": (((0,), (0,)), ((), ()))}[dn]


def _split2(x):
    hi = x.astype(BF16)
    return hi, (x - hi.astype(F32)).astype(BF16)


def _mm_raw(x, y, dn, passes):
    f = lambda p, q: lax.dot_general(p, q, _dims(dn, x.ndim), preferred_element_type=F32)
    if passes == 1:
        return f(x.astype(BF16), y.astype(BF16))
    xh, xl = _split2(x)
    yh, yl = _split2(y)
    return f(xh, yh) + f(xh, yl) + f(xl, yh)


@functools.partial(jax.custom_vjp, nondiff_argnums=(2, 3))
def _mm(x, y, dn, passes):
    return _mm_raw(x, y, dn, passes)


def _mm_fwd(x, y, dn, passes):
    return _mm_raw(x, y, dn, passes), (x, y)


def _mm_bwd(dn, passes, res, d):
    x, y = res
    if dn == "nn":
        return _mm(d, y, "nt", passes), _mm(x, d, "tn", passes)
    if dn == "nt":
        return _mm(d, y, "nn", passes), _mm(d, x, "tn", passes)
    return _mm(y, d, "nt", passes), _mm(x, d, "nn", passes)


_mm.defvjp(_mm_fwd, _mm_bwd)


def _head_ones():
    i = lax.broadcasted_iota(jnp.int32, (RWKV_DIM, RWKV_DIM), 0) // HEAD_DIM
    j = lax.broadcasted_iota(jnp.int32, (RWKV_DIM, RWKV_DIM), 1) // HEAD_DIM
    return (i == j).astype(BF16)


def _hsum_raw(x):
    ones = _head_ones()
    f = lambda p: lax.dot_general(p, ones, _dims("nn", 2), preferred_element_type=F32)
    x1 = x.astype(BF16)
    r1 = x - x1.astype(F32)
    x2 = r1.astype(BF16)
    x3 = (r1 - x2.astype(F32)).astype(BF16)
    return f(x1) + f(x2) + f(x3)


@jax.custom_vjp
def _hsum(x):
    return _hsum_raw(x)


_hsum.defvjp(lambda x: (_hsum_raw(x), None), lambda _, d: (_hsum(d),))


def _sigmoid(x):
    return 1.0 / (1.0 + jnp.exp(-x))


def _softplus(x):
    return jnp.maximum(x, 0.0) + jnp.log(1.0 + jnp.exp(-jnp.abs(x)))


def _params(sem):
    return pltpu.CompilerParams(dimension_semantics=sem, vmem_limit_bytes=VMEM_LIMIT)


def _rowwise(name, fn, rows, consts, row_outs, acc_outs=(), tr=ROW_BLOCK):
    rows = [r if isinstance(r, tuple) else (r, r.shape[1], 0) for r in rows]
    t_len = rows[0][0].shape[0]
    tr = min(tr, t_len)
    n_r, n_c, n_o, n_a = len(rows), len(consts), len(row_outs), len(acc_outs)

    def body(*refs):
        ins = [r[...] for r in refs[:n_r + n_c]]
        outs = fn(*ins)
        o_refs = refs[n_r + n_c:n_r + n_c + n_o]
        a_refs = refs[n_r + n_c + n_o:]
        for o_ref, val in zip(o_refs, outs[:n_o]):
            o_ref[...] = val.astype(o_ref.dtype)
        if n_a:
            first = pl.program_id(0) == 0

            @pl.when(first)
            def _():
                for a_ref, val in zip(a_refs, outs[n_o:]):
                    a_ref[...] = val

            @pl.when(jnp.logical_not(first))
            def _():
                for a_ref, val in zip(a_refs, outs[n_o:]):
                    a_ref[...] += val

    in_specs = [pl.BlockSpec((tr, w), functools.partial(lambda i, c: (i, c), c=cb)) for _, w, cb in rows]
    in_specs += [pl.BlockSpec(c.shape, functools.partial(lambda i, n: (0,) * n, n=c.ndim)) for c in consts]
    out_specs = [pl.BlockSpec((tr, w), lambda i: (i, 0)) for w, _ in row_outs]
    out_specs += [pl.BlockSpec(s, functools.partial(lambda i, n: (0,) * n, n=len(s))) for s in acc_outs]
    out_shape = [jax.ShapeDtypeStruct((t_len, w), dt) for w, dt in row_outs]
    out_shape += [jax.ShapeDtypeStruct(s, F32) for s in acc_outs]
    return pl.pallas_call(
        body, name=name, grid=(t_len // tr,), in_specs=in_specs, out_specs=out_specs, out_shape=out_shape,
        compiler_params=_params(("arbitrary",)),
    )(*[r[0] for r in rows], *consts)


def _colwise(name, fn, cols, prms, col_outs, prm_outs=()):
    t_len = cols[0][0].shape[0]
    n_blocks = col_outs[0][0] // LANE
    n_i = len(cols) + len(prms)

    def body(*refs):
        outs = fn(*[r[...] for r in refs[:n_i]])
        for o_ref, val in zip(refs[n_i:], outs):
            o_ref[...] = val.astype(o_ref.dtype)

    def spec(r, off):
        return pl.BlockSpec((r, LANE), functools.partial(lambda j, o: (0, o + j), o=off))

    in_specs = [spec(t_len, off) for _, off in cols] + [spec(a.shape[0], off) for a, off in prms]
    out_specs = [spec(t_len, 0) for _ in col_outs] + [spec(r, 0) for r, _ in prm_outs]
    out_shape = [jax.ShapeDtypeStruct((t_len, w), dt) for w, dt in col_outs]
    out_shape += [jax.ShapeDtypeStruct((r, w), F32) for r, w in prm_outs]
    return pl.pallas_call(
        body, name=name, grid=(n_blocks,), in_specs=in_specs, out_specs=out_specs, out_shape=out_shape,
        compiler_params=_params(("arbitrary",)),
    )(*[c[0] for c in cols], *[p[0] for p in prms])


def _matmul(name, a, b, dn, outs, *, tm, tn, tk, extras=(), epilogue=None):
    if dn == "nn":
        (m, k), n = a.shape, b.shape[1]
    elif dn == "nt":
        (m, k), n = a.shape, b.shape[0]
    else:
        (k, m), n = a.shape, b.shape[1]
    tm, tn, tk = min(tm, m), min(tn, n), min(tk, k)
    nk = k // tk
    a_spec = pl.BlockSpec((tk, tm), lambda i, j, q: (q, i)) if dn == "tn" else pl.BlockSpec((tm, tk), lambda i, j, q: (i, q))
    b_spec = pl.BlockSpec((tn, tk), lambda i, j, q: (j, q)) if dn == "nt" else pl.BlockSpec((tk, tn), lambda i, j, q: (q, j))
    o_spec = pl.BlockSpec((tm, tn), lambda i, j, q: (i, j))
    n_e, n_o = len(extras), len(outs)

    def body(*refs):
        a_ref, b_ref = refs[:2]
        e_refs = refs[2:2 + n_e]
        o_refs = refs[2 + n_e:2 + n_e + n_o]
        part = lax.dot_general(a_ref[...].astype(BF16), b_ref[...].astype(BF16), _dims(dn, 2), preferred_element_type=F32)

        def finish(acc):
            vals = (acc,) if epilogue is None else epilogue(acc, *[e[...] for e in e_refs])
            for o_ref, val in zip(o_refs, vals):
                o_ref[...] = val.astype(o_ref.dtype)

        if nk == 1:
            finish(part)
            return
        acc_ref = refs[-1]
        q = pl.program_id(2)

        @pl.when(q == 0)
        def _():
            acc_ref[...] = part

        @pl.when(q > 0)
        def _():
            acc_ref[...] += part

        @pl.when(q == nk - 1)
        def _():
            finish(acc_ref[...])

    res = pl.pallas_call(
        body, name=name, grid=(m // tm, n // tn, nk),
        in_specs=[a_spec, b_spec] + [o_spec] * n_e, out_specs=[o_spec] * n_o,
        out_shape=[jax.ShapeDtypeStruct((m, n), dt) for dt in outs],
        scratch_shapes=[pltpu.VMEM((tm, tn), F32)] if nk > 1 else [],
        compiler_params=_params(("parallel", "parallel", "arbitrary")),
    )(a, b, *extras)
    return res[0] if n_o == 1 else res


def _rms(h, g):
    return h * lax.rsqrt(jnp.mean(h * h, axis=-1, keepdims=True) + RMS_EPS) * g


def _rms_bwd(h, g, dy):
    rs = lax.rsqrt(jnp.mean(h * h, axis=-1, keepdims=True) + RMS_EPS)
    n = h * rs
    dn = dy * g
    dh = rs * (dn - n * jnp.mean(dn * n, axis=-1, keepdims=True))
    return dh, jnp.sum(dy * n, axis=0, keepdims=True)


def _rwkv_pre(k, xw, xa, xg, w0, a0, k_k, k_a, wl, al, gl):
    zw = w0 + _mm(jnp.tanh(xw), wl, "nn", 1)
    lw = -jnp.exp(-_softplus(-zw) - 0.5)
    iclr = _sigmoid(a0 + _mm(xa, al, "nn", 1))
    g = _mm(_sigmoid(xg), gl, "nn", 1)
    kk0 = k * k_k
    kk = kk0 / jnp.maximum(jnp.sqrt(_hsum(kk0 * kk0)), L2_EPS)
    k_h = k * (1.0 + (iclr - 1.0) * k_a)
    return lw, k_h, -kk, kk * iclr, g


def _rwkv_post(y, r, k_h, v, g, ln_g, ln_b, r_k):
    mu = _hsum(y) * (1.0 / HEAD_DIM)
    yc = y - mu
    var = _hsum(yc * yc) * (1.0 / HEAD_DIM)
    yo = yc * lax.rsqrt(var + GN_EPS) * ln_g + ln_b
    bonus = _hsum(r * k_h * r_k) * v
    return (yo + bonus) * g


def _shift_down(x, n):
    rows = lax.broadcasted_iota(jnp.int32, x.shape, 0)
    return jnp.where(rows < n, 0.0, pltpu.roll(x, n, 0))


def _shift_up(x, n):
    t_len = x.shape[0]
    rows = lax.broadcasted_iota(jnp.int32, x.shape, 0)
    return jnp.where(rows >= t_len - n, 0.0, pltpu.roll(x, t_len - n, 0))


def _exchange_plan(ins, outs, scatter, send_sems, recv_sems, local_sems):
    x, y, c = lax.axis_index("x"), lax.axis_index("y"), lax.axis_index("c")
    me = 4 * x + 2 * y + c

    def local(i):
        return pltpu.make_async_copy(ins[i].at[me] if scatter[i] else ins[i], outs[i].at[me], local_sems.at[i])

    def send(i, rel):
        return pltpu.make_async_remote_copy(
            src_ref=ins[i].at[me ^ rel] if scatter[i] else ins[i], dst_ref=outs[i].at[me],
            send_sem=send_sems.at[i, rel - 1], recv_sem=recv_sems.at[i, rel - 1],
            device_id=(x ^ (rel >> 2), y ^ ((rel >> 1) & 1), c ^ (rel & 1)), device_id_type=pl.DeviceIdType.MESH)

    def landed(i, rel):
        slot = outs[i].at[me ^ rel]
        return pltpu.make_async_remote_copy(
            src_ref=slot, dst_ref=slot, send_sem=send_sems.at[i, rel - 1], recv_sem=recv_sems.at[i, rel - 1],
            device_id=(x, y, c), device_id_type=pl.DeviceIdType.MESH)

    def start():
        for i in range(len(ins)):
            local(i).start()
            for rel in range(1, N_DEV):
                send(i, rel).start()

    def wait():
        for i in range(len(ins)):
            local(i).wait()
            for rel in range(1, N_DEV):
                landed(i, rel).wait_recv()
            for rel in range(1, N_DEV):
                send(i, rel).wait_send()

    return start, wait


def _gather_plan(ins, outs, send_sems, recv_sems, local_sems):
    x, y, c = lax.axis_index("x"), lax.axis_index("y"), lax.axis_index("c")
    me = 4 * x + 2 * y + c
    direct, chips = (1, 2, 4, 6), (2, 4, 6)

    def local(i):
        return pltpu.make_async_copy(ins[i], outs[i].at[me], local_sems.at[i])

    def send(i, rel):
        return pltpu.make_async_remote_copy(
            src_ref=ins[i], dst_ref=outs[i].at[me], send_sem=send_sems.at[i, rel - 1], recv_sem=recv_sems.at[i, rel - 1],
            device_id=(x ^ (rel >> 2), y ^ ((rel >> 1) & 1), c ^ (rel & 1)), device_id_type=pl.DeviceIdType.MESH)

    def passed(i, rel):
        slot = outs[i].at[me ^ rel]
        return pltpu.make_async_remote_copy(
            src_ref=slot, dst_ref=slot, send_sem=send_sems.at[i, rel], recv_sem=recv_sems.at[i, rel],
            device_id=(x, y, 1 - c), device_id_type=pl.DeviceIdType.MESH)

    def landed(i, rel):
        slot = outs[i].at[me ^ rel]
        return pltpu.make_async_remote_copy(
            src_ref=slot, dst_ref=slot, send_sem=send_sems.at[i, rel - 1], recv_sem=recv_sems.at[i, rel - 1],
            device_id=(x, y, c), device_id_type=pl.DeviceIdType.MESH)

    def start():
        for i in range(len(ins)):
            local(i).start()
            for rel in direct:
                send(i, rel).start()

    def forward(i):
        for rel in chips:
            landed(i, rel).wait_recv()
            passed(i, rel).start()

    def wait():
        for i in range(len(ins)):
            local(i).wait()
            for rel in (1, 3, 5, 7):
                landed(i, rel).wait_recv()
            for rel in direct:
                send(i, rel).wait_send()
            for rel in chips:
                passed(i, rel).wait_send()

    return start, forward, wait


def _exchange_io(arrays, scatter):
    n = len(arrays)
    any_spec = pl.BlockSpec(memory_space=pl.ANY)
    out_shape = [jax.ShapeDtypeStruct(a.shape if sc else (N_DEV,) + a.shape, a.dtype) for a, sc in zip(arrays, scatter)]
    sems = [pltpu.SemaphoreType.DMA((n, N_DEV - 1)), pltpu.SemaphoreType.DMA((n, N_DEV - 1)), pltpu.SemaphoreType.DMA((n,))]
    return [any_spec] * n, out_shape, sems


def _exchange(name, arrays, scatter):
    n = len(arrays)
    specs, out_shape, sems = _exchange_io(arrays, scatter)

    def body(*refs):
        start, wait = _exchange_plan(refs[:n], refs[n:2 * n], scatter, *refs[2 * n:])
        start()
        wait()

    return pl.pallas_call(
        body, name=name, in_specs=specs, out_specs=specs, out_shape=out_shape, scratch_shapes=sems,
        compiler_params=pltpu.CompilerParams(has_side_effects=True),
    )(*arrays)


def _chunk_fwd(z0, r, lw, k, v, a, b):
    n_h, c, n_k = r.shape
    mm = functools.partial(_mm, passes=REC_PASSES)
    gram = functools.partial(_mm, passes=3)
    ti = lax.broadcasted_iota(jnp.int32, (c, c), 0)
    si = lax.broadcasted_iota(jnp.int32, (c, c), 1)
    strict, incl = si < ti, si <= ti
    cum = _mm(jnp.broadcast_to(incl.astype(F32), (n_h, c, c)), lw, "nn", 3)
    cum_end = cum[:, c - 1:c, :]
    at = a * jnp.exp(cum - lw)
    bt = b * jnp.exp(-cum)
    kt = k * jnp.exp(-cum)
    rt = r * jnp.exp(cum)
    be = b * jnp.exp(cum_end - cum)
    ke = k * jnp.exp(cum_end - cum)
    lab = jnp.where(strict, gram(at, bt, "nt"), 0.0)
    lak = jnp.where(strict, gram(at, kt, "nt"), 0.0)
    u = mm(at, z0, "nn") + mm(lak, v, "nn")
    p = lab
    n = 1
    while n < c:
        u = u + mm(p, u, "nn")
        n *= 2
        if n < c:
            p = mm(p, p, "nn")
    rb = jnp.where(incl, gram(rt, bt, "nt"), 0.0)
    rk = jnp.where(incl, gram(rt, kt, "nt"), 0.0)
    y = mm(rt, z0, "nn") + mm(rb, u, "nn") + mm(rk, v, "nn")
    ki = lax.broadcasted_iota(jnp.int32, (n_k, n_k), 0)
    kj = lax.broadcasted_iota(jnp.int32, (n_k, n_k), 1)
    dmat = jnp.where(ki == kj, jnp.broadcast_to(jnp.exp(cum_end), (n_h, n_k, n_k)), 0.0)
    z_end = mm(dmat, z0, "nn") + mm(be, u, "tn") + mm(ke, v, "tn")
    return y, z_end


def _heads(x):
    return jnp.stack([x[:, h * HEAD_DIM:(h + 1) * HEAD_DIM] for h in range(N_HEADS)])


def _unheads(x):
    return jnp.concatenate([x[h] for h in range(N_HEADS)], axis=-1)


def _rec_params():
    return pltpu.CompilerParams(dimension_semantics=("arbitrary",), vmem_limit_bytes=VMEM_LIMIT, has_side_effects=True)


def _rec_fwd(u, lw, k, a, b, xch):
    t_len = lw.shape[0]
    c = min(REC_CHUNK, t_len)
    nc = t_len // c
    n_x = len(xch)
    x_specs, x_shapes, x_sems = _exchange_io(xch, [False] * n_x)
    sizes = [a_.size * a_.dtype.itemsize for a_ in xch]
    pass_step = [min(nc - 1, int(0.9 * nc * sum(sizes[:j + 1]) / sum(sizes)) + 1) for j in range(n_x)]

    def body(*refs):
        r_ref, v_ref, lw_ref, k_ref, a_ref, b_ref = refs[:6]
        x_in = refs[6:6 + n_x]
        y_ref, zs_ref = refs[6 + n_x:8 + n_x]
        x_out = refs[8 + n_x:8 + 2 * n_x]
        z_scr = refs[8 + 2 * n_x]
        start, forward, wait = _gather_plan(x_in, x_out, *refs[9 + 2 * n_x:])
        i = pl.program_id(0)

        @pl.when(i == 0)
        def _():
            start()
            z_scr[...] = jnp.zeros_like(z_scr)

        z0 = z_scr[...]
        zs_ref[0] = z0
        y, z_end = _chunk_fwd(z0, _heads(r_ref[...]), _heads(lw_ref[...]), _heads(k_ref[...]), _heads(v_ref[...]),
                              _heads(a_ref[...]), _heads(b_ref[...]))
        y_ref[...] = _unheads(y)
        z_scr[...] = z_end

        for j in range(n_x):
            pl.when(i == pass_step[j])(functools.partial(forward, j))

        @pl.when(i == nc - 1)
        def _():
            wait()

    blk = lambda cb: pl.BlockSpec((c, RWKV_DIM), functools.partial(lambda i, q: (i, q), q=cb))
    res = pl.pallas_call(
        body, name="rwkv_rec_fwd", grid=(nc,),
        in_specs=[blk(0), blk(2)] + [blk(0)] * 4 + x_specs,
        out_specs=[blk(0), pl.BlockSpec((1, N_HEADS, HEAD_DIM, HEAD_DIM), lambda i: (i, 0, 0, 0))] + x_specs,
        out_shape=[jax.ShapeDtypeStruct((t_len, RWKV_DIM), F32),
                   jax.ShapeDtypeStruct((nc, N_HEADS, HEAD_DIM, HEAD_DIM), F32)] + x_shapes,
        scratch_shapes=[pltpu.VMEM((N_HEADS, HEAD_DIM, HEAD_DIM), F32)] + x_sems,
        compiler_params=_rec_params(),
    )(u, u, lw, k, a, b, *xch)
    return res[0], res[1], res[2:]


def _rec_bwd(u, lw, k, a, b, zs, dy, xch, xch_scatter):
    t_len = lw.shape[0]
    c = min(REC_CHUNK, t_len)
    nc = t_len // c
    n_x = len(xch)
    x_specs, x_shapes, x_sems = _exchange_io(xch, xch_scatter)

    def body(*refs):
        r_ref, v_ref, lw_ref, k_ref, a_ref, b_ref, zs_ref, dy_ref = refs[:8]
        x_in = refs[8:8 + n_x]
        g_refs = refs[8 + n_x:14 + n_x]
        x_out = refs[14 + n_x:14 + 2 * n_x]
        dz_scr = refs[14 + 2 * n_x]
        start, wait = _exchange_plan(x_in, x_out, xch_scatter, *refs[15 + 2 * n_x:])
        i = pl.program_id(0)

        @pl.when(i == 0)
        def _():
            start()
            dz_scr[...] = jnp.zeros_like(dz_scr)

        _, vjp = jax.vjp(_chunk_fwd, zs_ref[0], _heads(r_ref[...]), _heads(lw_ref[...]), _heads(k_ref[...]),
                         _heads(v_ref[...]), _heads(a_ref[...]), _heads(b_ref[...]))
        dz0, dr, dlw, dk, dv, da, db = vjp((_heads(dy_ref[...]), dz_scr[...]))
        for ref, val in zip(g_refs, (dr, dv, dlw, dk, da, db)):
            ref[...] = _unheads(val)
        dz_scr[...] = dz0

        @pl.when(i == nc - 1)
        def _():
            wait()

    blk = lambda cb: pl.BlockSpec((c, RWKV_DIM), functools.partial(lambda i, q: (nc - 1 - i, q), q=cb))
    res = pl.pallas_call(
        body, name="rwkv_rec_bwd", grid=(nc,),
        in_specs=[blk(0), blk(2)] + [blk(0)] * 4
                 + [pl.BlockSpec((1, N_HEADS, HEAD_DIM, HEAD_DIM), lambda i: (nc - 1 - i, 0, 0, 0)), blk(0)] + x_specs,
        out_specs=[blk(0)] * 6 + x_specs,
        out_shape=[jax.ShapeDtypeStruct((t_len, RWKV_DIM), F32)] * 6 + x_shapes,
        scratch_shapes=[pltpu.VMEM((N_HEADS, HEAD_DIM, HEAD_DIM), F32)] + x_sems,
        compiler_params=_rec_params(),
    )(u, u, lw, k, a, b, zs, dy, *xch)
    return res[:6], res[6:]


_EARLY = ["w_in", "conv_w", "w_lora_up", "a_lora_up", "g_lora_up"]
_LATE = ["w_out", "w_up", "w_down", "w_ple_gate", "w_ple_proj"]
_SHARDED = _EARLY + _LATE
_COL_SHARDED = {"w_in", "conv_w", "w_lora_up", "a_lora_up", "g_lora_up", "w_up", "w_ple_proj"}
_BF16_GATHER = {"w_in", "w_out", "w_up", "w_down", "w_ple_gate", "w_ple_proj"}
_REPLICATED = ["norm_mix_g", "shift_mu", "w0", "a0", "k_k", "k_a", "r_k", "ln_x_g", "ln_x_b", "norm_mlp_g", "norm_ple_g",
               "norm_final_g"]
_WEIGHTS = ["norm_mix_g", "w_in", "conv_w", "shift_mu", "w_lora_up", "w0", "a_lora_up", "a0", "g_lora_up", "k_k", "k_a", "r_k",
            "ln_x_g", "ln_x_b", "w_out", "norm_mlp_g", "w_up", "w_down", "norm_ple_g", "w_ple_gate", "w_ple_proj", "norm_final_g"]
_PACK_ROWS = 80


def _unshard(name, g):
    if name in _COL_SHARDED:
        return jnp.moveaxis(g, 0, 1).reshape(g.shape[1], N_DEV * g.shape[2])
    return g.reshape(N_DEV * g.shape[1], g.shape[2])


def _reshard(name, full):
    if name in _COL_SHARDED:
        return jnp.moveaxis(full.reshape(full.shape[0], N_DEV, full.shape[1] // N_DEV), 1, 0)
    return full.reshape(N_DEV, full.shape[0] // N_DEV, full.shape[1])


def _pad_in_cols(a):
    z = lambda n: jnp.zeros(a.shape[:-1] + (n,), a.dtype)
    return jnp.concatenate([a[..., :3136], z(64), a[..., 3136:3200], z(64), a[..., 3200:3360], z(96)], axis=-1)


def _unpad_in_cols(a):
    return jnp.concatenate([a[..., :3136], a[..., 3200:3264], a[..., 3328:3488]], axis=-1)


def _pad_rows(a, rows):
    return jnp.concatenate([a, jnp.zeros((rows - a.shape[0],) + a.shape[1:], a.dtype)], axis=0)


def _pack(vals):
    flat = jnp.concatenate([v.reshape(-1) for v in vals])
    return jnp.concatenate([flat, jnp.zeros((_PACK_ROWS * LANE - flat.shape[0],), F32)]).reshape(_PACK_ROWS, LANE)


def _local_step(x, p, tgt, w, late_shards):
    row = lambda v: v.reshape(1, -1)
    w = dict(w)

    (xn1,) = _rowwise("rms_mix", lambda h, g: (_rms(h, g),), [x], [w["norm_mix_g"]], [(D_MODEL, BF16)])
    proj = _matmul("in_proj", xn1, w["w_in"], "nn", [F32], tm=1024, tn=512, tk=D_MODEL)
    proj_gb, proj_gc, proj_h, proj_rw = (proj, 0), (proj, 4), (proj, 8), (proj, CONV_COLS // LANE)

    def conv_fwd(gb, gc, hx, cw):
        uu = gc * hx
        return (gb * (uu * cw[2:3] + _shift_down(uu, 1) * cw[1:2] + _shift_down(uu, 2) * cw[0:1]),)

    (y_conv,) = _colwise("conv_fwd", conv_fwd, [proj_gb, proj_gc, proj_h], [(w["conv_w"], 0)], [(CONV_DIM, BF16)])

    def shift_fwd(rr, mu):
        return (rr + mu * (_shift_down(rr, 1) - rr),)

    (u,) = _colwise("shift_fwd", shift_fwd, [proj_rw], [(w["shift_mu"], 0)], [(RW_PAD, F32)])

    small = [w["w0"], w["a0"], w["k_k"], w["k_a"], w["w_lora_up"], w["a_lora_up"], w["g_lora_up"]]
    u_k, u_xw, u_xa, u_xg = (u, 512, 1), (u, LANE, XW_OFF // LANE), (u, LANE, XA_OFF // LANE), (u, 2 * LANE, XG_OFF // (2 * LANE))
    lw, k_h, ra, rb, g = _rowwise("rwkv_pre", _rwkv_pre, [u_k, u_xw, u_xa, u_xg], small, [(RWKV_DIM, F32)] * 5)
    y_rec, zs, late = _rec_fwd(u, lw, k_h, ra, rb, late_shards)
    for n, gathered in zip(_LATE, late):
        w[n] = _unshard(n, gathered)
    post_c = [w["ln_x_g"], w["ln_x_b"], w["r_k"]]
    u_r, u_v = (u, 512, 0), (u, 512, 2)
    (y_rwkv,) = _rowwise("rwkv_post", lambda *xs: (_rwkv_post(*xs),), [y_rec, u_r, k_h, u_v, g], post_c, [(RWKV_DIM, BF16)])
    ycat = jnp.concatenate([y_conv, y_rwkv], axis=1)
    res = lambda acc, r_: (acc + r_,)
    h1 = _matmul("out_proj", ycat, w["w_out"], "nn", [F32], tm=1024, tn=1024, tk=D_MODEL, extras=[x], epilogue=res)
    (xn2,) = _rowwise("rms_mlp", lambda h, g_: (_rms(h, g_),), [h1], [w["norm_mlp_g"]], [(D_MODEL, BF16)])

    def relu2(acc):
        hid = jnp.maximum(acc, 0.0)
        return acc, hid * hid

    pre, hsq = _matmul("mlp_up", xn2, w["w_up"], "nn", [F32, BF16], tm=1024, tn=1024, tk=D_MODEL, epilogue=relu2)
    h2 = _matmul("mlp_down", hsq, w["w_down"], "nn", [F32], tm=512, tn=1024, tk=D_FF, extras=[h1], epilogue=res)
    (xn3,) = _rowwise("rms_ple", lambda h, g_: (_rms(h, g_),), [h2], [w["norm_ple_g"]], [(D_MODEL, BF16)])
    zg = _matmul("ple_gate", xn3, w["w_ple_gate"], "nn", [F32], tm=1024, tn=1024, tk=D_MODEL)
    pp = _matmul("ple_proj", p, w["w_ple_proj"], "nn", [F32], tm=1024, tn=1024, tk=PLE_DIM)

    def head(h2_, zg_, pp_, tg, gf):
        h3 = h2_ + _sigmoid(zg_) * pp_
        out = _rms(h3, gf)
        err = out - tg
        dh3, dgf = _rms_bwd(h3, gf, err * (1.0 / D_MODEL))
        loss = jnp.sum(jnp.sum(err * err, axis=1, keepdims=True), axis=0, keepdims=True) * (0.5 / D_MODEL)
        return dh3, dgf, loss

    dh3, d_norm_final, loss = _rowwise("head", head, [h2, zg, pp, tgt], [row(w["norm_final_g"])], [(D_MODEL, F32)],
                                       [(1, D_MODEL), (1, 1)])

    def ple_bwd(dh3_, zg_, pp_):
        gate = _sigmoid(zg_)
        return dh3_ * pp_ * gate * (1.0 - gate), dh3_ * gate

    dzg, dpp = _rowwise("ple_bwd", ple_bwd, [dh3, zg, pp], [], [(D_MODEL, BF16)] * 2)
    d_w_ple_proj = _matmul("d_ple_proj", p, dpp, "tn", [BF16], tm=PLE_DIM, tn=1024, tk=4096)
    d_w_ple_gate = _matmul("d_ple_gate", xn3, dzg, "tn", [BF16], tm=512, tn=1024, tk=4096)
    dxn3 = _matmul("dx_ple_gate", dzg, w["w_ple_gate"], "nt", [F32], tm=1024, tn=1024, tk=D_MODEL)

    def norm_bwd(h, dy, dres, g_):
        dh, dg = _rms_bwd(h, g_, dy)
        return dh + dres, dg

    dh2, d_norm_ple = _rowwise("rms_ple_bwd", norm_bwd, [h2, dxn3, dh3], [w["norm_ple_g"]], [(D_MODEL, F32)], [(1, D_MODEL)])
    d_w_down = _matmul("d_mlp_down", hsq, dh2, "tn", [BF16], tm=512, tn=512, tk=4096)
    dpre = _matmul("dx_mlp_down", dh2, w["w_down"], "nt", [BF16], tm=1024, tn=1024, tk=D_MODEL, extras=[pre],
                   epilogue=lambda acc, pre_: (acc * (2.0 * jnp.maximum(pre_, 0.0)),))
    d_w_up = _matmul("d_mlp_up", xn2, dpre, "tn", [BF16], tm=512, tn=1024, tk=4096)
    dxn2 = _matmul("dx_mlp_up", dpre, w["w_up"], "nt", [F32], tm=512, tn=1024, tk=D_FF)
    dh1, d_norm_mlp = _rowwise("rms_mlp_bwd", norm_bwd, [h1, dxn2, dh2], [w["norm_mlp_g"]], [(D_MODEL, F32)], [(1, D_MODEL)])
    d_w_out = _matmul("d_out_proj", ycat, dh1, "tn", [BF16], tm=512, tn=512, tk=4096)
    dycat = _matmul("dx_out_proj", dh1, w["w_out"], "nt", [F32], tm=1024, tn=1024, tk=D_MODEL)
    late_grads = dict(w_out=d_w_out, w_up=d_w_up, w_down=d_w_down, w_ple_gate=d_w_ple_gate, w_ple_proj=d_w_ple_proj)

    def conv_bwd(dy, gb, gc, hx, cw):
        uu = gc * hx
        u1, u2 = _shift_down(uu, 1), _shift_down(uu, 2)
        dconv = dy * gb
        du = dconv * cw[2:3] + _shift_up(dconv, 1) * cw[1:2] + _shift_up(dconv, 2) * cw[0:1]
        s = lambda z: jnp.sum(z, axis=0, keepdims=True)
        return (dy * (uu * cw[2:3] + u1 * cw[1:2] + u2 * cw[0:1]), du * hx, du * gc,
                s(dconv * u2), s(dconv * u1), s(dconv * uu))

    dgb, dgc, dhx, dcw0, dcw1, dcw2 = _colwise(
        "conv_bwd", conv_bwd, [(dycat, 0), proj_gb, proj_gc, proj_h], [(w["conv_w"], 0)],
        [(CONV_DIM, BF16)] * 3, [(1, CONV_DIM)] * 3)

    def post_bwd(dy, y, r, k_h_, v, g_, ln_g, ln_b, r_k):
        _, vjp = jax.vjp(_rwkv_post, y, r, k_h_, v, g_, ln_g, ln_b, r_k)
        return vjp(dy)

    dy_rec, dr_p, dk_p, dv_p, dg, d_ln_g, d_ln_b, d_r_k = _rowwise(
        "rwkv_post_bwd", post_bwd, [(dycat, 512, 1), y_rec, u_r, k_h, u_v, g], post_c,
        [(RWKV_DIM, F32)] * 5, [(1, RWKV_DIM)] * 3)
    (dr_r, dv_r, dlw, dk_r, da, db), late_parts = _rec_bwd(
        u, lw, k_h, ra, rb, zs, dy_rec, [_reshard(n, late_grads[n]) for n in _LATE], [True] * len(_LATE))

    def pre_bwd(k, xw, xa, xg, dr1, dr2, dv1, dv2, dlw_, dk1, dk2, da_, db_, dg_, w0, a0, k_k, k_a, wl, al, gl):
        _, vjp = jax.vjp(_rwkv_pre, k, xw, xa, xg, w0, a0, k_k, k_a, wl, al, gl)
        dk, dxw, dxa, dxg, dw0, da0, dk_k, dk_a, dwl, dal, dgl = vjp((dlw_, dk1 + dk2, da_, db_, dg_))
        du = jnp.concatenate([dr1 + dr2, dk, dv1 + dv2, dxw, dxa, dxg], axis=1)
        return du, dw0, da0, dk_k, dk_a, dwl, dal, dgl

    du, d_w0, d_a0, d_k_k, d_k_a, d_wl, d_al, d_gl = _rowwise(
        "rwkv_pre_bwd", pre_bwd, [u_k, u_xw, u_xa, u_xg, dr_p, dr_r, dv_p, dv_r, dlw, dk_p, dk_r, da, db, dg], small,
        [(RW_PAD, F32)], [(1, RWKV_DIM)] * 4 + [(LANE, RWKV_DIM), (LANE, RWKV_DIM), (2 * LANE, RWKV_DIM)])

    def shift_bwd(du_, rr, mu):
        return du_ - mu * du_ + mu * _shift_up(du_, 1), jnp.sum(du_ * (_shift_down(rr, 1) - rr), axis=0, keepdims=True)

    dprojr, d_mu = _colwise("shift_bwd", shift_bwd, [(du, 0), proj_rw], [(w["shift_mu"], 0)], [(RW_PAD, BF16)], [(1, RW_PAD)])
    dproj = jnp.concatenate([dgb, dgc, dhx, dprojr], axis=1)
    d_w_in = _matmul("d_in_proj", xn1, dproj, "tn", [BF16], tm=512, tn=896, tk=4096)
    dxn1 = _matmul("dx_in_proj", dproj, w["w_in"], "nt", [F32], tm=512, tn=1024, tk=IN_PAD)
    dx, d_norm_mix = _rowwise("rms_mix_bwd", norm_bwd, [x, dxn1, dh1], [w["norm_mix_g"]], [(D_MODEL, F32)], [(1, D_MODEL)])

    grads = dict(
        norm_mix_g=d_norm_mix, w_in=d_w_in, conv_w=jnp.concatenate([dcw0, dcw1, dcw2], axis=0), shift_mu=d_mu,
        w_lora_up=d_wl, w0=d_w0, a_lora_up=d_al, a0=d_a0, g_lora_up=d_gl, k_k=d_k_k, k_a=d_k_a, r_k=d_r_k,
        ln_x_g=d_ln_g, ln_x_b=d_ln_b, norm_mlp_g=d_norm_mlp, norm_ple_g=d_norm_ple, norm_final_g=d_norm_final)
    return loss, dx, grads, late_parts


def _adamw(name, parts, w, m, v):
    rows, cols = w.shape
    tr = rows if rows * cols * 4 * 8 <= (4 << 20) else max(8, (4 << 20) // (cols * 4 * 8) // 8 * 8)
    while rows % tr:
        tr -= 8

    def body(p_ref, w_ref, m_ref, v_ref, g_ref, d_ref, nm_ref, nv_ref):
        g = p_ref[0].astype(F32)
        for s in range(1, N_DEV):
            g = g + p_ref[s].astype(F32)
        nm = ADAM_B1 * m_ref[...] + (1.0 - ADAM_B1) * g
        nv = ADAM_B2 * v_ref[...] + (1.0 - ADAM_B2) * (g * g)
        m_hat = nm / (1.0 - ADAM_B1 ** ADAM_STEP)
        v_hat = nv / (1.0 - ADAM_B2 ** ADAM_STEP)
        g_ref[...] = g
        d_ref[...] = -ADAM_LR * (m_hat / (jnp.sqrt(v_hat) + ADAM_EPS) + ADAM_WD * w_ref[...])
        nm_ref[...] = nm
        nv_ref[...] = nv

    blk = pl.BlockSpec((tr, cols), lambda i: (i, 0))
    return pl.pallas_call(
        body, name=name, grid=(rows // tr,),
        in_specs=[pl.BlockSpec((N_DEV, tr, cols), lambda i: (0, i, 0)), blk, blk, blk], out_specs=[blk] * 4,
        out_shape=[jax.ShapeDtypeStruct((rows, cols), F32)] * 4,
        compiler_params=_params(("arbitrary",)),
    )(parts, w, m, v)


def kernel(x, p, norm_mix_g, w_in, conv_w, shift_mu, w_lora_up, w0, a_lora_up, a0, g_lora_up, k_k, k_a, r_k, ln_x_g, ln_x_b, w_out, norm_mlp_g, w_up, w_down, norm_ple_g, w_ple_gate, w_ple_proj, norm_final_g, loss_target, m_norm_mix_g, m_w_in, m_conv_w, m_shift_mu, m_w_lora_up, m_w0, m_a_lora_up, m_a0, m_g_lora_up, m_k_k, m_k_a, m_r_k, m_ln_x_g, m_ln_x_b, m_w_out, m_norm_mlp_g, m_w_up, m_w_down, m_norm_ple_g, m_w_ple_gate, m_w_ple_proj, m_norm_final_g, v_norm_mix_g, v_w_in, v_conv_w, v_shift_mu, v_w_lora_up, v_w0, v_a_lora_up, v_a0, v_g_lora_up, v_k_k, v_k_a, v_r_k, v_ln_x_g, v_ln_x_b, v_w_out, v_norm_mlp_g, v_w_up, v_w_down, v_norm_ple_g, v_w_ple_gate, v_w_ple_proj, v_norm_final_g):
    args = dict(locals())
    wts = {n: args[n] for n in _WEIGHTS}
    mom = {n: args["m_" + n] for n in _WEIGHTS}
    var = {n: args["v_" + n] for n in _WEIGHTS}
    shard2d = lambda a: a.reshape(a.shape[-2:])
    pad_mu = lambda a: _pad_in_cols(jnp.concatenate([jnp.zeros((1, CONV_COLS), F32), a], axis=1))[:, CONV_COLS:]
    unpad_mu = lambda a: _unpad_in_cols(jnp.concatenate([jnp.zeros((1, CONV_COLS), F32), a], axis=1))[:, CONV_COLS:]

    shards = {n: shard2d(wts[n]).astype(BF16 if n in _BF16_GATHER else F32) for n in _SHARDED}
    gathered = _exchange("gather_early", [shards[n] for n in _EARLY], [False] * len(_EARLY))
    w = {n: _unshard(n, g) for n, g in zip(_EARLY, gathered)}
    w["w_in"] = _pad_in_cols(w["w_in"])
    w["w_lora_up"] = _pad_rows(w["w_lora_up"], LANE)
    w["a_lora_up"] = _pad_rows(w["a_lora_up"], LANE)
    w["g_lora_up"] = _pad_rows(w["g_lora_up"], 2 * LANE)
    for n in _REPLICATED:
        w[n] = wts[n].reshape(1, -1)
    w["shift_mu"] = pad_mu(wts["shift_mu"])

    loss, dx, grads, late_parts = _local_step(x[0], p[0, 0], loss_target[0], w, [shards[n] for n in _LATE])

    grads["w_in"] = _unpad_in_cols(grads["w_in"])
    grads["shift_mu"] = unpad_mu(grads["shift_mu"])
    grads["w_lora_up"] = grads["w_lora_up"][:64]
    grads["a_lora_up"] = grads["a_lora_up"][:64]
    grads["g_lora_up"] = grads["g_lora_up"][:160]
    early_parts = [_reshard(n, grads[n]) for n in _EARLY]
    fin = _exchange("exchange_final", early_parts + [_pack([grads[n] for n in _REPLICATED] + [loss])],
                    [True] * len(_EARLY) + [False])
    parts = dict(zip(_EARLY, fin[:-1]))
    parts.update(zip(_LATE, late_parts))

    out = {}
    for n in _SHARDED:
        res = _adamw("adamw_" + n, parts[n], shard2d(wts[n]), shard2d(mom[n]), shard2d(var[n]))
        out[n] = [r.reshape(wts[n].shape) for r in res]
    sm = _adamw("adamw_small", fin[-1], _pack([wts[n] for n in _REPLICATED]), _pack([mom[n] for n in _REPLICATED]),
                _pack([var[n] for n in _REPLICATED]))
    off = 0
    for n in _REPLICATED:
        size = wts[n].size
        out[n] = [r.reshape(-1)[off:off + size].reshape(wts[n].shape) for r in sm]
        off += size
    loss_total = sm[0].reshape(-1)[off]
    return (loss_total, dx[None], *[out[n][0] for n in _WEIGHTS], *[out[n][1] for n in _WEIGHTS],
            *[out[n][2] for n in _WEIGHTS], *[out[n][3] for n in _WEIGHTS])
```

```python
import functools

import jax
import jax.numpy as jnp
from jax import lax
from jax.experimental import pallas as pl
from jax.experimental.pallas import tpu as pltpu

F32 = jnp.float32
BF16 = jnp.bfloat16

N_DEV = 8
D_MODEL = 1024
CONV_DIM = 512
RWKV_DIM = 512
HEAD_DIM = 64
N_HEADS = 8
D_FF = 4096
PLE_DIM = 256
RMS_EPS = 1e-6
GN_EPS = 64e-5
L2_EPS = 1e-12
ADAM_LR, ADAM_B1, ADAM_B2, ADAM_EPS, ADAM_WD, ADAM_STEP = 0.001, 0.9, 0.999, 1e-08, 0.01, 10

CONV_COLS = 3 * CONV_DIM
RW_PAD = 2048
IN_PAD = CONV_COLS + RW_PAD
XW_OFF, XA_OFF, XG_OFF = 1536, 1664, 1792
REC_CHUNK = 128
REC_PASSES = 1
ROW_BLOCK = 256
LANE = 128
VMEM_LIMIT = 56 * 1024 * 1024


def _dims(dn, ndim):
    if ndim == 3:
        return {"nn": (((2,), (1,)), ((0,), (0,))), "nt": (((2,), (2,)), ((0,), (0,))),
                "tn": (((1,), (1,)), ((0,), (0,)))}[dn]
    return {"nn": (((1,), (0,)), ((), ())), "nt": (((1,), (1,)), ((), ())), "tn": (((0,), (0,)), ((), ()))}[dn]


def _split2(x):
    hi = x.astype(BF16)
    return hi, (x - hi.astype(F32)).astype(BF16)


def _mm_raw(x, y, dn, passes):
    f = lambda p, q: lax.dot_general(p, q, _dims(dn, x.ndim), preferred_element_type=F32)
    if passes == 1:
        return f(x.astype(BF16), y.astype(BF16))
    xh, xl = _split2(x)
    yh, yl = _split2(y)
    return f(xh, yh) + f(xh, yl) + f(xl, yh)


@functools.partial(jax.custom_vjp, nondiff_argnums=(2, 3))
def _mm(x, y, dn, passes):
    return _mm_raw(x, y, dn, passes)


def _mm_fwd(x, y, dn, passes):
    return _mm_raw(x, y, dn, passes), (x, y)


def _mm_bwd(dn, passes, res, d):
    x, y = res
    if dn == "nn":
        return _mm(d, y, "nt", passes), _mm(x, d, "tn", passes)
    if dn == "nt":
        return _mm(d, y, "nn", passes), _mm(d, x, "tn", passes)
    return _mm(y, d, "nt", passes), _mm(x, d, "nn", passes)


_mm.defvjp(_mm_fwd, _mm_bwd)


def _head_ones():
    i = lax.broadcasted_iota(jnp.int32, (RWKV_DIM, RWKV_DIM), 0) // HEAD_DIM
    j = lax.broadcasted_iota(jnp.int32, (RWKV_DIM, RWKV_DIM), 1) // HEAD_DIM
    return (i == j).astype(BF16)


def _hsum_raw(x):
    ones = _head_ones()
    f = lambda p: lax.dot_general(p, ones, _dims("nn", 2), preferred_element_type=F32)
    x1 = x.astype(BF16)
    r1 = x - x1.astype(F32)
    x2 = r1.astype(BF16)
    x3 = (r1 - x2.astype(F32)).astype(BF16)
    return f(x1) + f(x2) + f(x3)


@jax.custom_vjp
def _hsum(x):
    return _hsum_raw(x)


_hsum.defvjp(lambda x: (_hsum_raw(x), None), lambda _, d: (_hsum(d),))


def _sigmoid(x):
    return 1.0 / (1.0 + jnp.exp(-x))


def _softplus(x):
    return jnp.maximum(x, 0.0) + jnp.log(1.0 + jnp.exp(-jnp.abs(x)))


def _params(sem):
    return pltpu.CompilerParams(dimension_semantics=sem, vmem_limit_bytes=VMEM_LIMIT)


def _rowwise(name, fn, rows, consts, row_outs, acc_outs=(), tr=ROW_BLOCK):
    rows = [r if isinstance(r, tuple) else (r, r.shape[1], 0) for r in rows]
    t_len = rows[0][0].shape[0]
    tr = min(tr, t_len)
    n_r, n_c, n_o, n_a = len(rows), len(consts), len(row_outs), len(acc_outs)

    def body(*refs):
        ins = [r[...] for r in refs[:n_r + n_c]]
        outs = fn(*ins)
        o_refs = refs[n_r + n_c:n_r + n_c + n_o]
        a_refs = refs[n_r + n_c + n_o:]
        for o_ref, val in zip(o_refs, outs[:n_o]):
            o_ref[...] = val.astype(o_ref.dtype)
        if n_a:
            first = pl.program_id(0) == 0

            @pl.when(first)
            def _():
                for a_ref, val in zip(a_refs, outs[n_o:]):
                    a_ref[...] = val

            @pl.when(jnp.logical_not(first))
            def _():
                for a_ref, val in zip(a_refs, outs[n_o:]):
                    a_ref[...] += val

    in_specs = [pl.BlockSpec((tr, w), functools.partial(lambda i, c: (i, c), c=cb)) for _, w, cb in rows]
    in_specs += [pl.BlockSpec(c.shape, functools.partial(lambda i, n: (0,) * n, n=c.ndim)) for c in consts]
    out_specs = [pl.BlockSpec((tr, w), lambda i: (i, 0)) for w, _ in row_outs]
    out_specs += [pl.BlockSpec(s, functools.partial(lambda i, n: (0,) * n, n=len(s))) for s in acc_outs]
    out_shape = [jax.ShapeDtypeStruct((t_len, w), dt) for w, dt in row_outs]
    out_shape += [jax.ShapeDtypeStruct(s, F32) for s in acc_outs]
    return pl.pallas_call(
        body, name=name, grid=(t_len // tr,), in_specs=in_specs, out_specs=out_specs, out_shape=out_shape,
        compiler_params=_params(("arbitrary",)),
    )(*[r[0] for r in rows], *consts)


def _colwise(name, fn, cols, prms, col_outs, prm_outs=()):
    t_len = cols[0][0].shape[0]
    n_blocks = col_outs[0][0] // LANE
    n_i = len(cols) + len(prms)

    def body(*refs):
        outs = fn(*[r[...] for r in refs[:n_i]])
        for o_ref, val in zip(refs[n_i:], outs):
            o_ref[...] = val.astype(o_ref.dtype)

    def spec(r, off):
        return pl.BlockSpec((r, LANE), functools.partial(lambda j, o: (0, o + j), o=off))

    in_specs = [spec(t_len, off) for _, off in cols] + [spec(a.shape[0], off) for a, off in prms]
    out_specs = [spec(t_len, 0) for _ in col_outs] + [spec(r, 0) for r, _ in prm_outs]
    out_shape = [jax.ShapeDtypeStruct((t_len, w), dt) for w, dt in col_outs]
    out_shape += [jax.ShapeDtypeStruct((r, w), F32) for r, w in prm_outs]
    return pl.pallas_call(
        body, name=name, grid=(n_blocks,), in_specs=in_specs, out_specs=out_specs, out_shape=out_shape,
        compiler_params=_params(("arbitrary",)),
    )(*[c[0] for c in cols], *[p[0] for p in prms])


def _matmul(name, a, b, dn, outs, *, tm, tn, tk, extras=(), consts=(), epilogue=None, sums=(), xch=(), xch_scatter=()):
    if dn == "nn":
        (m, k), n = a.shape, b.shape[1]
    elif dn == "nt":
        (m, k), n = a.shape, b.shape[0]
    else:
        (k, m), n = a.shape, b.shape[1]
    tm, tn, tk = min(tm, m), min(tn, n), min(tk, k)
    nk = k // tk
    grid = (m // tm, n // tn, nk)
    assert not sums or (grid[1] == 1 and nk == 1)
    a_spec = pl.BlockSpec((tk, tm), lambda i, j, q: (q, i)) if dn == "tn" else pl.BlockSpec((tm, tk), lambda i, j, q: (i, q))
    b_spec = pl.BlockSpec((tn, tk), lambda i, j, q: (j, q)) if dn == "nt" else pl.BlockSpec((tk, tn), lambda i, j, q: (q, j))
    o_spec = pl.BlockSpec((tm, tn), lambda i, j, q: (i, j))
    c_spec = pl.BlockSpec((1, tn), lambda i, j, q: (0, j))
    n_e, n_c, n_o, n_s, n_x = len(extras), len(consts), len(outs), len(sums), len(xch)
    x_specs, x_shapes, x_sems = _exchange_io(xch, xch_scatter) if n_x else ([], [], [])

    def body(*refs):
        a_ref, b_ref = refs[:2]
        e_refs = refs[2:2 + n_e + n_c]
        x_in = refs[2 + n_e + n_c:2 + n_e + n_c + n_x]
        rest = refs[2 + n_e + n_c + n_x:]
        o_refs, s_refs, x_out, scratch = rest[:n_o], rest[n_o:n_o + n_s], rest[n_o + n_s:n_o + n_s + n_x], rest[n_o + n_s + n_x:]
        step = (pl.program_id(0) * grid[1] + pl.program_id(1)) * nk + pl.program_id(2)
        if n_x:
            start, wait = _exchange_plan(x_in, x_out, xch_scatter, *scratch[len(scratch) - 3:])
            pl.when(step == 0)(start)
        part = lax.dot_general(a_ref[...].astype(BF16), b_ref[...].astype(BF16), _dims(dn, 2), preferred_element_type=F32)

        def finish(acc):
            vals = (acc,) if epilogue is None else epilogue(acc, *[e[...] for e in e_refs])
            for o_ref, val in zip(o_refs, vals[:n_o]):
                o_ref[...] = val.astype(o_ref.dtype)
            if n_s:
                @pl.when(step == 0)
                def _():
                    for s_ref, val in zip(s_refs, vals[n_o:]):
                        s_ref[...] = val

                @pl.when(step > 0)
                def _():
                    for s_ref, val in zip(s_refs, vals[n_o:]):
                        s_ref[...] += val

        if nk == 1:
            finish(part)
        else:
            acc_ref = scratch[0]
            q = pl.program_id(2)

            @pl.when(q == 0)
            def _():
                acc_ref[...] = part

            @pl.when(q > 0)
            def _():
                acc_ref[...] += part

            @pl.when(q == nk - 1)
            def _():
                finish(acc_ref[...])

        if n_x:
            pl.when(step == grid[0] * grid[1] * nk - 1)(wait)

    plain = not (n_s or n_x)
    res = pl.pallas_call(
        body, name=name, grid=grid,
        in_specs=[a_spec, b_spec] + [o_spec] * n_e + [c_spec] * n_c + x_specs,
        out_specs=[o_spec] * n_o + [c_spec] * n_s + x_specs,
        out_shape=[jax.ShapeDtypeStruct((m, n), dt) for dt in outs] + [jax.ShapeDtypeStruct(s, F32) for s in sums] + x_shapes,
        scratch_shapes=([pltpu.VMEM((tm, tn), F32)] if nk > 1 else []) + x_sems,
        compiler_params=pltpu.CompilerParams(
            dimension_semantics=("parallel", "parallel", "arbitrary") if plain else ("arbitrary",) * 3,
            vmem_limit_bytes=VMEM_LIMIT, has_side_effects=bool(n_x)),
    )(a, b, *extras, *consts, *xch)
    return res[0] if len(res) == 1 else res


def _rms(h, g):
    return h * lax.rsqrt(jnp.mean(h * h, axis=-1, keepdims=True) + RMS_EPS) * g


def _rms_bwd(h, g, dy):
    rs = lax.rsqrt(jnp.mean(h * h, axis=-1, keepdims=True) + RMS_EPS)
    n = h * rs
    dn = dy * g
    dh = rs * (dn - n * jnp.mean(dn * n, axis=-1, keepdims=True))
    return dh, jnp.sum(dy * n, axis=0, keepdims=True)


def _rwkv_pre(k, xw, xa, xg, w0, a0, k_k, k_a, wl, al, gl):
    zw = w0 + _mm(jnp.tanh(xw), wl, "nn", 1)
    lw = -jnp.exp(-_softplus(-zw) - 0.5)
    iclr = _sigmoid(a0 + _mm(xa, al, "nn", 1))
    g = _mm(_sigmoid(xg), gl, "nn", 1)
    kk0 = k * k_k
    kk = kk0 / jnp.maximum(jnp.sqrt(_hsum(kk0 * kk0)), L2_EPS)
    k_h = k * (1.0 + (iclr - 1.0) * k_a)
    return lw, k_h, -kk, kk * iclr, g


def _rwkv_post(y, r, k_h, v, g, ln_g, ln_b, r_k):
    mu = _hsum(y) * (1.0 / HEAD_DIM)
    yc = y - mu
    var = _hsum(yc * yc) * (1.0 / HEAD_DIM)
    yo = yc * lax.rsqrt(var + GN_EPS) * ln_g + ln_b
    bonus = _hsum(r * k_h * r_k) * v
    return (yo + bonus) * g


def _shift_down(x, n):
    rows = lax.broadcasted_iota(jnp.int32, x.shape, 0)
    return jnp.where(rows < n, 0.0, pltpu.roll(x, n, 0))


def _shift_up(x, n):
    t_len = x.shape[0]
    rows = lax.broadcasted_iota(jnp.int32, x.shape, 0)
    return jnp.where(rows >= t_len - n, 0.0, pltpu.roll(x, t_len - n, 0))


def _exchange_plan(ins, outs, scatter, send_sems, recv_sems, local_sems):
    x, y, c = lax.axis_index("x"), lax.axis_index("y"), lax.axis_index("c")
    me = 4 * x + 2 * y + c

    def local(i):
        return pltpu.make_async_copy(ins[i].at[me] if scatter[i] else ins[i], outs[i].at[me], local_sems.at[i])

    def send(i, rel):
        return pltpu.make_async_remote_copy(
            src_ref=ins[i].at[me ^ rel] if scatter[i] else ins[i], dst_ref=outs[i].at[me],
            send_sem=send_sems.at[i, rel - 1], recv_sem=recv_sems.at[i, rel - 1],
            device_id=(x ^ (rel >> 2), y ^ ((rel >> 1) & 1), c ^ (rel & 1)), device_id_type=pl.DeviceIdType.MESH)

    def landed(i, rel):
        slot = outs[i].at[me ^ rel]
        return pltpu.make_async_remote_copy(
            src_ref=slot, dst_ref=slot, send_sem=send_sems.at[i, rel - 1], recv_sem=recv_sems.at[i, rel - 1],
            device_id=(x, y, c), device_id_type=pl.DeviceIdType.MESH)

    def start():
        for i in range(len(ins)):
            local(i).start()
            for rel in range(1, N_DEV):
                send(i, rel).start()

    def wait():
        for i in range(len(ins)):
            local(i).wait()
            for rel in range(1, N_DEV):
                landed(i, rel).wait_recv()
            for rel in range(1, N_DEV):
                send(i, rel).wait_send()

    return start, wait


def _gather_plan(ins, outs, send_sems, recv_sems, local_sems):
    x, y, c = lax.axis_index("x"), lax.axis_index("y"), lax.axis_index("c")
    me = 4 * x + 2 * y + c
    direct, chips = (1, 2, 4, 6), (2, 4, 6)

    def local(i):
        return pltpu.make_async_copy(ins[i], outs[i].at[me], local_sems.at[i])

    def send(i, rel):
        return pltpu.make_async_remote_copy(
            src_ref=ins[i], dst_ref=outs[i].at[me], send_sem=send_sems.at[i, rel - 1], recv_sem=recv_sems.at[i, rel - 1],
            device_id=(x ^ (rel >> 2), y ^ ((rel >> 1) & 1), c ^ (rel & 1)), device_id_type=pl.DeviceIdType.MESH)

    def passed(i, rel):
        slot = outs[i].at[me ^ rel]
        return pltpu.make_async_remote_copy(
            src_ref=slot, dst_ref=slot, send_sem=send_sems.at[i, rel], recv_sem=recv_sems.at[i, rel],
            device_id=(x, y, 1 - c), device_id_type=pl.DeviceIdType.MESH)

    def landed(i, rel):
        slot = outs[i].at[me ^ rel]
        return pltpu.make_async_remote_copy(
            src_ref=slot, dst_ref=slot, send_sem=send_sems.at[i, rel - 1], recv_sem=recv_sems.at[i, rel - 1],
            device_id=(x, y, c), device_id_type=pl.DeviceIdType.MESH)

    def start():
        for i in range(len(ins)):
            local(i).start()
            for rel in direct:
                send(i, rel).start()

    def forward(i):
        for rel in chips:
            landed(i, rel).wait_recv()
            passed(i, rel).start()

    def wait():
        for i in range(len(ins)):
            local(i).wait()
            for rel in (1, 3, 5, 7):
                landed(i, rel).wait_recv()
            for rel in direct:
                send(i, rel).wait_send()
            for rel in chips:
                passed(i, rel).wait_send()

    return start, forward, wait


def _exchange_io(arrays, scatter):
    n = len(arrays)
    any_spec = pl.BlockSpec(memory_space=pl.ANY)
    out_shape = [jax.ShapeDtypeStruct(a.shape if sc else (N_DEV,) + a.shape, a.dtype) for a, sc in zip(arrays, scatter)]
    sems = [pltpu.SemaphoreType.DMA((n, N_DEV - 1)), pltpu.SemaphoreType.DMA((n, N_DEV - 1)), pltpu.SemaphoreType.DMA((n,))]
    return [any_spec] * n, out_shape, sems


def _exchange(name, arrays, scatter):
    n = len(arrays)
    specs, out_shape, sems = _exchange_io(arrays, scatter)

    def body(*refs):
        if any(scatter):
            start, wait = _exchange_plan(refs[:n], refs[n:2 * n], scatter, *refs[2 * n:])
            start()
        else:
            start, forward, wait = _gather_plan(refs[:n], refs[n:2 * n], *refs[2 * n:])
            start()
            for i in range(n):
                forward(i)
        wait()

    return pl.pallas_call(
        body, name=name, in_specs=specs, out_specs=specs, out_shape=out_shape, scratch_shapes=sems,
        compiler_params=pltpu.CompilerParams(has_side_effects=True),
    )(*arrays)


def _chunk_fwd(z0, r, lw, k, v, a, b):
    n_h, c, n_k = r.shape
    mm = functools.partial(_mm, passes=REC_PASSES)
    gram = functools.partial(_mm, passes=3)
    ti = lax.broadcasted_iota(jnp.int32, (c, c), 0)
    si = lax.broadcasted_iota(jnp.int32, (c, c), 1)
    strict, incl = si < ti, si <= ti
    cum = _mm(jnp.broadcast_to(incl.astype(F32), (n_h, c, c)), lw, "nn", 3)
    cum_end = cum[:, c - 1:c, :]
    at = a * jnp.exp(cum - lw)
    bt = b * jnp.exp(-cum)
    kt = k * jnp.exp(-cum)
    rt = r * jnp.exp(cum)
    be = b * jnp.exp(cum_end - cum)
    ke = k * jnp.exp(cum_end - cum)
    lab = jnp.where(strict, gram(at, bt, "nt"), 0.0)
    lak = jnp.where(strict, gram(at, kt, "nt"), 0.0)
    u = mm(at, z0, "nn") + mm(lak, v, "nn")
    p = lab
    n = 1
    while n < c:
        u = u + mm(p, u, "nn")
        n *= 2
        if n < c:
            p = mm(p, p, "nn")
    rb = jnp.where(incl, gram(rt, bt, "nt"), 0.0)
    rk = jnp.where(incl, gram(rt, kt, "nt"), 0.0)
    y = mm(rt, z0, "nn") + mm(rb, u, "nn") + mm(rk, v, "nn")
    ki = lax.broadcasted_iota(jnp.int32, (n_k, n_k), 0)
    kj = lax.broadcasted_iota(jnp.int32, (n_k, n_k), 1)
    dmat = jnp.where(ki == kj, jnp.broadcast_to(jnp.exp(cum_end), (n_h, n_k, n_k)), 0.0)
    z_end = mm(dmat, z0, "nn") + mm(be, u, "tn") + mm(ke, v, "tn")
    return y, z_end


def _heads(x):
    return jnp.stack([x[:, h * HEAD_DIM:(h + 1) * HEAD_DIM] for h in range(N_HEADS)])


def _unheads(x):
    return jnp.concatenate([x[h] for h in range(N_HEADS)], axis=-1)


def _rec_params():
    return pltpu.CompilerParams(dimension_semantics=("arbitrary",), vmem_limit_bytes=VMEM_LIMIT, has_side_effects=True)


def _rec_fwd(u, lw, k, a, b, xch):
    t_len = lw.shape[0]
    c = min(REC_CHUNK, t_len)
    nc = t_len // c
    n_x = len(xch)
    x_specs, x_shapes, x_sems = _exchange_io(xch, [False] * n_x)
    sizes = [a_.size * a_.dtype.itemsize for a_ in xch]
    pass_step = [min(nc - 1, int(0.9 * nc * sum(sizes[:j + 1]) / sum(sizes)) + 1) for j in range(n_x)]

    def body(*refs):
        r_ref, v_ref, lw_ref, k_ref, a_ref, b_ref = refs[:6]
        x_in = refs[6:6 + n_x]
        y_ref, zs_ref = refs[6 + n_x:8 + n_x]
        x_out = refs[8 + n_x:8 + 2 * n_x]
        z_scr = refs[8 + 2 * n_x]
        start, forward, wait = _gather_plan(x_in, x_out, *refs[9 + 2 * n_x:])
        i = pl.program_id(0)

        @pl.when(i == 0)
        def _():
            start()
            z_scr[...] = jnp.zeros_like(z_scr)

        z0 = z_scr[...]
        zs_ref[0] = z0
        y, z_end = _chunk_fwd(z0, _heads(r_ref[...]), _heads(lw_ref[...]), _heads(k_ref[...]), _heads(v_ref[...]),
                              _heads(a_ref[...]), _heads(b_ref[...]))
        y_ref[...] = _unheads(y)
        z_scr[...] = z_end

        for j in range(n_x):
            pl.when(i == pass_step[j])(functools.partial(forward, j))

        @pl.when(i == nc - 1)
        def _():
            wait()

    blk = lambda cb: pl.BlockSpec((c, RWKV_DIM), functools.partial(lambda i, q: (i, q), q=cb))
    res = pl.pallas_call(
        body, name="rwkv_rec_fwd", grid=(nc,),
        in_specs=[blk(0), blk(2)] + [blk(0)] * 4 + x_specs,
        out_specs=[blk(0), pl.BlockSpec((1, N_HEADS, HEAD_DIM, HEAD_DIM), lambda i: (i, 0, 0, 0))] + x_specs,
        out_shape=[jax.ShapeDtypeStruct((t_len, RWKV_DIM), F32),
                   jax.ShapeDtypeStruct((nc, N_HEADS, HEAD_DIM, HEAD_DIM), F32)] + x_shapes,
        scratch_shapes=[pltpu.VMEM((N_HEADS, HEAD_DIM, HEAD_DIM), F32)] + x_sems,
        compiler_params=_rec_params(),
    )(u, u, lw, k, a, b, *xch)
    return res[0], res[1], res[2:]


def _rec_bwd(u, lw, k, a, b, zs, dy, xch, xch_scatter):
    t_len = lw.shape[0]
    c = min(REC_CHUNK, t_len)
    nc = t_len // c
    n_x = len(xch)
    x_specs, x_shapes, x_sems = _exchange_io(xch, xch_scatter)

    def body(*refs):
        r_ref, v_ref, lw_ref, k_ref, a_ref, b_ref, zs_ref, dy_ref = refs[:8]
        x_in = refs[8:8 + n_x]
        g_refs = refs[8 + n_x:14 + n_x]
        x_out = refs[14 + n_x:14 + 2 * n_x]
        dz_scr = refs[14 + 2 * n_x]
        start, wait = _exchange_plan(x_in, x_out, xch_scatter, *refs[15 + 2 * n_x:])
        i = pl.program_id(0)

        @pl.when(i == 0)
        def _():
            start()
            dz_scr[...] = jnp.zeros_like(dz_scr)

        _, vjp = jax.vjp(_chunk_fwd, zs_ref[0], _heads(r_ref[...]), _heads(lw_ref[...]), _heads(k_ref[...]),
                         _heads(v_ref[...]), _heads(a_ref[...]), _heads(b_ref[...]))
        dz0, dr, dlw, dk, dv, da, db = vjp((_heads(dy_ref[...]), dz_scr[...]))
        for ref, val in zip(g_refs, (dr, dv, dlw, dk, da, db)):
            ref[...] = _unheads(val)
        dz_scr[...] = dz0

        @pl.when(i == nc - 1)
        def _():
            wait()

    blk = lambda cb: pl.BlockSpec((c, RWKV_DIM), functools.partial(lambda i, q: (nc - 1 - i, q), q=cb))
    res = pl.pallas_call(
        body, name="rwkv_rec_bwd", grid=(nc,),
        in_specs=[blk(0), blk(2)] + [blk(0)] * 4
                 + [pl.BlockSpec((1, N_HEADS, HEAD_DIM, HEAD_DIM), lambda i: (nc - 1 - i, 0, 0, 0)), blk(0)] + x_specs,
        out_specs=[blk(0)] * 6 + x_specs,
        out_shape=[jax.ShapeDtypeStruct((t_len, RWKV_DIM), F32)] * 6 + x_shapes,
        scratch_shapes=[pltpu.VMEM((N_HEADS, HEAD_DIM, HEAD_DIM), F32)] + x_sems,
        compiler_params=_rec_params(),
    )(u, u, lw, k, a, b, zs, dy, *xch)
    return res[:6], res[6:]


_EARLY = ["w_in", "conv_w", "w_lora_up", "a_lora_up", "g_lora_up"]
_LATE = ["w_out", "w_up", "w_down", "w_ple_gate", "w_ple_proj"]
_SHARDED = _EARLY + _LATE
_COL_SHARDED = {"w_in", "conv_w", "w_lora_up", "a_lora_up", "g_lora_up", "w_up", "w_ple_proj"}
_BF16_GATHER = {"w_in", "w_out", "w_up", "w_down", "w_ple_gate", "w_ple_proj"}
_REPLICATED = ["norm_mix_g", "shift_mu", "w0", "a0", "k_k", "k_a", "r_k", "ln_x_g", "ln_x_b", "norm_mlp_g", "norm_ple_g",
               "norm_final_g"]
_WEIGHTS = ["norm_mix_g", "w_in", "conv_w", "shift_mu", "w_lora_up", "w0", "a_lora_up", "a0", "g_lora_up", "k_k", "k_a", "r_k",
            "ln_x_g", "ln_x_b", "w_out", "norm_mlp_g", "w_up", "w_down", "norm_ple_g", "w_ple_gate", "w_ple_proj", "norm_final_g"]
_PACK_ROWS = 80


def _unshard(name, g):
    if name in _COL_SHARDED:
        return jnp.moveaxis(g, 0, 1).reshape(g.shape[1], N_DEV * g.shape[2])
    return g.reshape(N_DEV * g.shape[1], g.shape[2])


def _reshard(name, full):
    if name in _COL_SHARDED:
        return jnp.moveaxis(full.reshape(full.shape[0], N_DEV, full.shape[1] // N_DEV), 1, 0)
    return full.reshape(N_DEV, full.shape[0] // N_DEV, full.shape[1])


def _pad_in_cols(a):
    z = lambda n: jnp.zeros(a.shape[:-1] + (n,), a.dtype)
    return jnp.concatenate([a[..., :3136], z(64), a[..., 3136:3200], z(64), a[..., 3200:3360], z(96)], axis=-1)


def _unpad_in_cols(a):
    return jnp.concatenate([a[..., :3136], a[..., 3200:3264], a[..., 3328:3488]], axis=-1)


def _pad_rows(a, rows):
    return jnp.concatenate([a, jnp.zeros((rows - a.shape[0],) + a.shape[1:], a.dtype)], axis=0)


def _pack(vals):
    flat = jnp.concatenate([v.reshape(-1) for v in vals])
    return jnp.concatenate([flat, jnp.zeros((_PACK_ROWS * LANE - flat.shape[0],), F32)]).reshape(_PACK_ROWS, LANE)


def _local_step(x, p, tgt, w, late_shards):
    row = lambda v: v.reshape(1, -1)
    w = dict(w)

    (xn1,) = _rowwise("rms_mix", lambda h, g: (_rms(h, g),), [x], [w["norm_mix_g"]], [(D_MODEL, BF16)])
    proj = _matmul("in_proj", xn1, w["w_in"], "nn", [F32], tm=1024, tn=512, tk=D_MODEL)
    proj_gb, proj_gc, proj_h, proj_rw = (proj, 0), (proj, 4), (proj, 8), (proj, CONV_COLS // LANE)

    def conv_fwd(gb, gc, hx, cw):
        uu = gc * hx
        return (gb * (uu * cw[2:3] + _shift_down(uu, 1) * cw[1:2] + _shift_down(uu, 2) * cw[0:1]),)

    (y_conv,) = _colwise("conv_fwd", conv_fwd, [proj_gb, proj_gc, proj_h], [(w["conv_w"], 0)], [(CONV_DIM, BF16)])

    def shift_fwd(rr, mu):
        return (rr + mu * (_shift_down(rr, 1) - rr),)

    (u,) = _colwise("shift_fwd", shift_fwd, [proj_rw], [(w["shift_mu"], 0)], [(RW_PAD, F32)])

    small = [w["w0"], w["a0"], w["k_k"], w["k_a"], w["w_lora_up"], w["a_lora_up"], w["g_lora_up"]]
    u_k, u_xw, u_xa, u_xg = (u, 512, 1), (u, LANE, XW_OFF // LANE), (u, LANE, XA_OFF // LANE), (u, 2 * LANE, XG_OFF // (2 * LANE))
    lw, k_h, ra, rb, g = _rowwise("rwkv_pre", _rwkv_pre, [u_k, u_xw, u_xa, u_xg], small, [(RWKV_DIM, F32)] * 5)
    y_rec, zs, late = _rec_fwd(u, lw, k_h, ra, rb, late_shards)
    for n, gathered in zip(_LATE, late):
        w[n] = _unshard(n, gathered)
    post_c = [w["ln_x_g"], w["ln_x_b"], w["r_k"]]
    u_r, u_v = (u, 512, 0), (u, 512, 2)
    (y_rwkv,) = _rowwise("rwkv_post", lambda *xs: (_rwkv_post(*xs),), [y_rec, u_r, k_h, u_v, g], post_c, [(RWKV_DIM, BF16)])
    ycat = jnp.concatenate([y_conv, y_rwkv], axis=1)
    res = lambda acc, r_: (acc + r_,)
    h1 = _matmul("out_proj", ycat, w["w_out"], "nn", [F32], tm=1024, tn=1024, tk=D_MODEL, extras=[x], epilogue=res)
    (xn2,) = _rowwise("rms_mlp", lambda h, g_: (_rms(h, g_),), [h1], [w["norm_mlp_g"]], [(D_MODEL, BF16)])

    def relu2(acc):
        hid = jnp.maximum(acc, 0.0)
        return acc, hid * hid

    pre, hsq = _matmul("mlp_up", xn2, w["w_up"], "nn", [F32, BF16], tm=1024, tn=1024, tk=D_MODEL, epilogue=relu2)
    h2 = _matmul("mlp_down", hsq, w["w_down"], "nn", [F32], tm=512, tn=1024, tk=D_FF, extras=[h1], epilogue=res)
    (xn3,) = _rowwise("rms_ple", lambda h, g_: (_rms(h, g_),), [h2], [w["norm_ple_g"]], [(D_MODEL, BF16)])
    zg = _matmul("ple_gate", xn3, w["w_ple_gate"], "nn", [F32], tm=1024, tn=1024, tk=D_MODEL)
    pp = _matmul("ple_proj", p, w["w_ple_proj"], "nn", [F32], tm=1024, tn=1024, tk=PLE_DIM)

    def head(h2_, zg_, pp_, tg, gf):
        gate = _sigmoid(zg_)
        h3 = h2_ + gate * pp_
        out = _rms(h3, gf)
        err = out - tg
        dh3, dgf = _rms_bwd(h3, gf, err * (1.0 / D_MODEL))
        loss = jnp.sum(jnp.sum(err * err, axis=1, keepdims=True), axis=0, keepdims=True) * (0.5 / D_MODEL)
        return dh3, dh3 * pp_ * gate * (1.0 - gate), dh3 * gate, dgf, loss

    dh3, dzg, dpp, d_norm_final, loss = _rowwise(
        "head", head, [h2, zg, pp, tgt], [row(w["norm_final_g"])], [(D_MODEL, F32), (D_MODEL, BF16), (D_MODEL, BF16)],
        [(1, D_MODEL), (1, 1)])

    d_w_ple_proj = _matmul("d_ple_proj", p, dpp, "tn", [BF16], tm=PLE_DIM, tn=1024, tk=4096)
    d_w_ple_gate = _matmul("d_ple_gate", xn3, dzg, "tn", [BF16], tm=512, tn=1024, tk=4096)

    def norm_bwd(dxn, h, dres, g_):
        dh, dg = _rms_bwd(h, g_, dxn)
        dh = dh + dres
        return dh, dh, dg

    nb = dict(tm=512, tn=D_MODEL, epilogue=norm_bwd, sums=[(1, D_MODEL)])
    dh2, dh2_b, d_norm_ple = _matmul("dx_ple_gate", dzg, w["w_ple_gate"], "nt", [F32, BF16], tk=D_MODEL,
                                     extras=[h2, dh3], consts=[w["norm_ple_g"]], **nb)
    d_w_down = _matmul("d_mlp_down", hsq, dh2_b, "tn", [BF16], tm=512, tn=1024, tk=4096)
    dpre = _matmul("dx_mlp_down", dh2_b, w["w_down"], "nt", [BF16], tm=1024, tn=1024, tk=D_MODEL, extras=[pre],
                   epilogue=lambda acc, pre_: (acc * (2.0 * jnp.maximum(pre_, 0.0)),))
    d_w_up = _matmul("d_mlp_up", xn2, dpre, "tn", [BF16], tm=512, tn=1024, tk=4096)
    dh1, dh1_b, d_norm_mlp = _matmul("dx_mlp_up", dpre, w["w_up"], "nt", [F32, BF16], tk=D_FF,
                                     extras=[h1, dh2], consts=[w["norm_mlp_g"]], **nb)
    d_w_out = _matmul("d_out_proj", ycat, dh1_b, "tn", [BF16], tm=512, tn=1024, tk=4096)
    dycat = _matmul("dx_out_proj", dh1_b, w["w_out"], "nt", [F32], tm=1024, tn=1024, tk=D_MODEL)
    late_grads = dict(w_out=d_w_out, w_up=d_w_up, w_down=d_w_down, w_ple_gate=d_w_ple_gate, w_ple_proj=d_w_ple_proj)

    def conv_bwd(dy, gb, gc, hx, cw):
        uu = gc * hx
        u1, u2 = _shift_down(uu, 1), _shift_down(uu, 2)
        dconv = dy * gb
        du = dconv * cw[2:3] + _shift_up(dconv, 1) * cw[1:2] + _shift_up(dconv, 2) * cw[0:1]
        s = lambda z: jnp.sum(z, axis=0, keepdims=True)
        return (dy * (uu * cw[2:3] + u1 * cw[1:2] + u2 * cw[0:1]), du * hx, du * gc,
                s(dconv * u2), s(dconv * u1), s(dconv * uu))

    dgb, dgc, dhx, dcw0, dcw1, dcw2 = _colwise(
        "conv_bwd", conv_bwd, [(dycat, 0), proj_gb, proj_gc, proj_h], [(w["conv_w"], 0)],
        [(CONV_DIM, BF16)] * 3, [(1, CONV_DIM)] * 3)

    def post_bwd(dy, y, r, k_h_, v, g_, ln_g, ln_b, r_k):
        _, vjp = jax.vjp(_rwkv_post, y, r, k_h_, v, g_, ln_g, ln_b, r_k)
        return vjp(dy)

    dy_rec, dr_p, dk_p, dv_p, dg, d_ln_g, d_ln_b, d_r_k = _rowwise(
        "rwkv_post_bwd", post_bwd, [(dycat, 512, 1), y_rec, u_r, k_h, u_v, g], post_c,
        [(RWKV_DIM, F32)] * 5, [(1, RWKV_DIM)] * 3)
    (dr_r, dv_r, dlw, dk_r, da, db), late_parts = _rec_bwd(
        u, lw, k_h, ra, rb, zs, dy_rec, [_reshard(n, late_grads[n]) for n in _LATE], [True] * len(_LATE))

    def pre_bwd(k, xw, xa, xg, dr1, dr2, dv1, dv2, dlw_, dk1, dk2, da_, db_, dg_, w0, a0, k_k, k_a, wl, al, gl):
        _, vjp = jax.vjp(_rwkv_pre, k, xw, xa, xg, w0, a0, k_k, k_a, wl, al, gl)
        dk, dxw, dxa, dxg, dw0, da0, dk_k, dk_a, dwl, dal, dgl = vjp((dlw_, dk1 + dk2, da_, db_, dg_))
        du = jnp.concatenate([dr1 + dr2, dk, dv1 + dv2, dxw, dxa, dxg], axis=1)
        return du, dw0, da0, dk_k, dk_a, dwl, dal, dgl

    du, d_w0, d_a0, d_k_k, d_k_a, d_wl, d_al, d_gl = _rowwise(
        "rwkv_pre_bwd", pre_bwd, [u_k, u_xw, u_xa, u_xg, dr_p, dr_r, dv_p, dv_r, dlw, dk_p, dk_r, da, db, dg], small,
        [(RW_PAD, F32)], [(1, RWKV_DIM)] * 4 + [(LANE, RWKV_DIM), (LANE, RWKV_DIM), (2 * LANE, RWKV_DIM)])

    def shift_bwd(du_, rr, mu):
        return du_ - mu * du_ + mu * _shift_up(du_, 1), jnp.sum(du_ * (_shift_down(rr, 1) - rr), axis=0, keepdims=True)

    dprojr, d_mu = _colwise("shift_bwd", shift_bwd, [(du, 0), proj_rw], [(w["shift_mu"], 0)], [(RW_PAD, BF16)], [(1, RW_PAD)])
    dproj = jnp.concatenate([dgb, dgc, dhx, dprojr], axis=1)
    d_w_in = _matmul("d_in_proj", xn1, dproj, "tn", [BF16], tm=512, tn=896, tk=4096)
    early_grads = dict(w_in=_unpad_in_cols(d_w_in), conv_w=jnp.concatenate([dcw0, dcw1, dcw2], axis=0),
                       w_lora_up=d_wl[:64], a_lora_up=d_al[:64], g_lora_up=d_gl[:160])
    dx, d_norm_mix, *early_parts = _matmul(
        "dx_in_proj", dproj, w["w_in"], "nt", [F32], tk=IN_PAD, extras=[x, dh1], consts=[w["norm_mix_g"]],
        xch=[_reshard(n, early_grads[n]) for n in _EARLY], xch_scatter=[True] * len(_EARLY),
        **dict(nb, epilogue=lambda *a: norm_bwd(*a)[1:]))

    grads = dict(
        norm_mix_g=d_norm_mix, shift_mu=d_mu, w0=d_w0, a0=d_a0, k_k=d_k_k, k_a=d_k_a, r_k=d_r_k,
        ln_x_g=d_ln_g, ln_x_b=d_ln_b, norm_mlp_g=d_norm_mlp, norm_ple_g=d_norm_ple, norm_final_g=d_norm_final)
    parts = dict(zip(_EARLY, early_parts))
    parts.update(zip(_LATE, late_parts))
    return loss, dx, grads, parts


def _adamw(name, parts, w, m, v):
    rows, cols = w.shape
    tr = rows if rows * cols * 4 * 8 <= (4 << 20) else max(8, (4 << 20) // (cols * 4 * 8) // 8 * 8)
    while rows % tr:
        tr -= 8

    def body(p_ref, w_ref, m_ref, v_ref, g_ref, d_ref, nm_ref, nv_ref):
        g = p_ref[0].astype(F32)
        for s in range(1, N_DEV):
            g = g + p_ref[s].astype(F32)
        nm = ADAM_B1 * m_ref[...] + (1.0 - ADAM_B1) * g
        nv = ADAM_B2 * v_ref[...] + (1.0 - ADAM_B2) * (g * g)
        m_hat = nm / (1.0 - ADAM_B1 ** ADAM_STEP)
        v_hat = nv / (1.0 - ADAM_B2 ** ADAM_STEP)
        g_ref[...] = g
        d_ref[...] = -ADAM_LR * (m_hat / (jnp.sqrt(v_hat) + ADAM_EPS) + ADAM_WD * w_ref[...])
        nm_ref[...] = nm
        nv_ref[...] = nv

    blk = pl.BlockSpec((tr, cols), lambda i: (i, 0))
    return pl.pallas_call(
        body, name=name, grid=(rows // tr,),
        in_specs=[pl.BlockSpec((N_DEV, tr, cols), lambda i: (0, i, 0)), blk, blk, blk], out_specs=[blk] * 4,
        out_shape=[jax.ShapeDtypeStruct((rows, cols), F32)] * 4,
        compiler_params=_params(("arbitrary",)),
    )(parts, w, m, v)


def kernel(x, p, norm_mix_g, w_in, conv_w, shift_mu, w_lora_up, w0, a_lora_up, a0, g_lora_up, k_k, k_a, r_k, ln_x_g, ln_x_b, w_out, norm_mlp_g, w_up, w_down, norm_ple_g, w_ple_gate, w_ple_proj, norm_final_g, loss_target, m_norm_mix_g, m_w_in, m_conv_w, m_shift_mu, m_w_lora_up, m_w0, m_a_lora_up, m_a0, m_g_lora_up, m_k_k, m_k_a, m_r_k, m_ln_x_g, m_ln_x_b, m_w_out, m_norm_mlp_g, m_w_up, m_w_down, m_norm_ple_g, m_w_ple_gate, m_w_ple_proj, m_norm_final_g, v_norm_mix_g, v_w_in, v_conv_w, v_shift_mu, v_w_lora_up, v_w0, v_a_lora_up, v_a0, v_g_lora_up, v_k_k, v_k_a, v_r_k, v_ln_x_g, v_ln_x_b, v_w_out, v_norm_mlp_g, v_w_up, v_w_down, v_norm_ple_g, v_w_ple_gate, v_w_ple_proj, v_norm_final_g):
    args = dict(locals())
    wts = {n: args[n] for n in _WEIGHTS}
    mom = {n: args["m_" + n] for n in _WEIGHTS}
    var = {n: args["v_" + n] for n in _WEIGHTS}
    shard2d = lambda a: a.reshape(a.shape[-2:])
    pad_mu = lambda a: _pad_in_cols(jnp.concatenate([jnp.zeros((1, CONV_COLS), F32), a], axis=1))[:, CONV_COLS:]
    unpad_mu = lambda a: _unpad_in_cols(jnp.concatenate([jnp.zeros((1, CONV_COLS), F32), a], axis=1))[:, CONV_COLS:]

    shards = {n: shard2d(wts[n]).astype(BF16 if n in _BF16_GATHER else F32) for n in _SHARDED}
    gathered = _exchange("gather_early", [shards[n] for n in _EARLY], [False] * len(_EARLY))
    w = {n: _unshard(n, g) for n, g in zip(_EARLY, gathered)}
    w["w_in"] = _pad_in_cols(w["w_in"])
    w["w_lora_up"] = _pad_rows(w["w_lora_up"], LANE)
    w["a_lora_up"] = _pad_rows(w["a_lora_up"], LANE)
    w["g_lora_up"] = _pad_rows(w["g_lora_up"], 2 * LANE)
    for n in _REPLICATED:
        w[n] = wts[n].reshape(1, -1)
    w["shift_mu"] = pad_mu(wts["shift_mu"])

    loss, dx, grads, parts = _local_step(x[0], p[0, 0], loss_target[0], w, [shards[n] for n in _LATE])

    grads["shift_mu"] = unpad_mu(grads["shift_mu"])
    (small_parts,) = _exchange("gather_small", [_pack([grads[n] for n in _REPLICATED] + [loss])], [False])

    out = {}
    for n in _SHARDED:
        res = _adamw("adamw_" + n, parts[n], shard2d(wts[n]), shard2d(mom[n]), shard2d(var[n]))
        out[n] = [r.reshape(wts[n].shape) for r in res]
    sm = _adamw("adamw_small", small_parts, _pack([wts[n] for n in _REPLICATED]), _pack([mom[n] for n in _REPLICATED]),
                _pack([var[n] for n in _REPLICATED]))
    off = 0
    for n in _REPLICATED:
        size = wts[n].size
        out[n] = [r.reshape(-1)[off:off + size].reshape(wts[n].shape) for r in sm]
        off += size
    loss_total = sm[0].reshape(-1)[off]
    return (loss_total, dx[None], *[out[n][0] for n in _WEIGHTS], *[out[n][1] for n in _WEIGHTS],
            *[out[n][2] for n in _WEIGHTS], *[out[n][3] for n in _WEIGHTS])
```

```python
import functools

import jax
import jax.numpy as jnp
from jax import lax
from jax.experimental import pallas as pl
from jax.experimental.pallas import tpu as pltpu

F32 = jnp.float32
BF16 = jnp.bfloat16

N_DEV = 8
D_MODEL = 1024
CONV_DIM = 512
RWKV_DIM = 512
HEAD_DIM = 64
N_HEADS = 8
D_FF = 4096
PLE_DIM = 256
RMS_EPS = 1e-6
GN_EPS = 64e-5
L2_EPS = 1e-12
ADAM_LR, ADAM_B1, ADAM_B2, ADAM_EPS, ADAM_WD, ADAM_STEP = 0.001, 0.9, 0.999, 1e-08, 0.01, 10

CONV_COLS = 3 * CONV_DIM
RW_PAD = 2048
IN_PAD = CONV_COLS + RW_PAD
IN_COLS = 3360
XW_OFF, XA_OFF, XG_OFF = 1536, 1664, 1792
REC_CHUNK = 128
REC_PASSES = 1
ROW_BLOCK = 256
LANE = 128
VMEM_LIMIT = 56 * 1024 * 1024


def _dims(dn, ndim):
    if ndim == 3:
        return {"nn": (((2,), (1,)), ((0,), (0,))), "nt": (((2,), (2,)), ((0,), (0,))),
                "tn": (((1,), (1,)), ((0,), (0,)))}[dn]
    return {"nn": (((1,), (0,)), ((), ())), "nt": (((1,), (1,)), ((), ())), "tn": (((0,), (0,)), ((), ()))}[dn]


def _split2(x):
    hi = x.astype(BF16)
    return hi, (x - hi.astype(F32)).astype(BF16)


def _mm_raw(x, y, dn, passes):
    f = lambda p, q: lax.dot_general(p, q, _dims(dn, x.ndim), preferred_element_type=F32)
    if passes == 1:
        return f(x.astype(BF16), y.astype(BF16))
    xh, xl = _split2(x)
    yh, yl = _split2(y)
    return f(xh, yh) + f(xh, yl) + f(xl, yh)


@functools.partial(jax.custom_vjp, nondiff_argnums=(2, 3))
def _mm(x, y, dn, passes):
    return _mm_raw(x, y, dn, passes)


def _mm_fwd(x, y, dn, passes):
    return _mm_raw(x, y, dn, passes), (x, y)


def _mm_bwd(dn, passes, res, d):
    x, y = res
    if dn == "nn":
        return _mm(d, y, "nt", passes), _mm(x, d, "tn", passes)
    if dn == "nt":
        return _mm(d, y, "nn", passes), _mm(d, x, "tn", passes)
    return _mm(y, d, "nt", passes), _mm(x, d, "nn", passes)


_mm.defvjp(_mm_fwd, _mm_bwd)


def _head_ones():
    i = lax.broadcasted_iota(jnp.int32, (RWKV_DIM, RWKV_DIM), 0) // HEAD_DIM
    j = lax.broadcasted_iota(jnp.int32, (RWKV_DIM, RWKV_DIM), 1) // HEAD_DIM
    return (i == j).astype(BF16)


def _hsum_raw(x):
    ones = _head_ones()
    f = lambda p: lax.dot_general(p, ones, _dims("nn", 2), preferred_element_type=F32)
    x1 = x.astype(BF16)
    r1 = x - x1.astype(F32)
    x2 = r1.astype(BF16)
    x3 = (r1 - x2.astype(F32)).astype(BF16)
    return f(x1) + f(x2) + f(x3)


@jax.custom_vjp
def _hsum(x):
    return _hsum_raw(x)


_hsum.defvjp(lambda x: (_hsum_raw(x), None), lambda _, d: (_hsum(d),))


def _sigmoid(x):
    return 1.0 / (1.0 + jnp.exp(-x))


def _softplus(x):
    return jnp.maximum(x, 0.0) + jnp.log(1.0 + jnp.exp(-jnp.abs(x)))


def _params(sem):
    return pltpu.CompilerParams(dimension_semantics=sem, vmem_limit_bytes=VMEM_LIMIT)


def _rowwise(name, fn, rows, consts, row_outs, acc_outs=(), tr=ROW_BLOCK, halo=False):
    rows = [r if isinstance(r, tuple) else (r, r.shape[1], 0) for r in rows]
    t_len = rows[0][0].shape[0]
    tr = min(tr, t_len)
    n_r, n_c, n_o, n_a = len(rows), len(consts), len(row_outs), len(acc_outs)
    n_h = n_r if halo else 0
    sub = 8

    def body(*refs):
        ins = [r[...] for r in refs[:n_r]]
        ins += [jnp.where(pl.program_id(0) == 0, 0.0, r[sub - 1:sub, :]) for r in refs[n_r:n_r + n_h]]
        ins += [r[...] for r in refs[n_r + n_h:n_r + n_h + n_c]]
        refs = refs[:n_r] + refs[n_r + n_h:]
        outs = fn(*ins)
        o_refs = refs[n_r + n_c:n_r + n_c + n_o]
        a_refs = refs[n_r + n_c + n_o:]
        for o_ref, val in zip(o_refs, outs[:n_o]):
            o_ref[...] = val.astype(o_ref.dtype)
        if n_a:
            first = pl.program_id(0) == 0

            @pl.when(first)
            def _():
                for a_ref, val in zip(a_refs, outs[n_o:]):
                    a_ref[...] = val

            @pl.when(jnp.logical_not(first))
            def _():
                for a_ref, val in zip(a_refs, outs[n_o:]):
                    a_ref[...] += val

    in_specs = [pl.BlockSpec((tr, w), functools.partial(lambda i, c: (i, c), c=cb)) for _, w, cb in rows]
    if halo:
        in_specs += [pl.BlockSpec((sub, w), functools.partial(lambda i, c: (jnp.maximum(i * (tr // sub) - 1, 0), c), c=cb))
                     for _, w, cb in rows]
    in_specs += [pl.BlockSpec(c.shape, functools.partial(lambda i, n: (0,) * n, n=c.ndim)) for c in consts]
    out_specs = [pl.BlockSpec((tr, w), lambda i: (i, 0)) for w, _ in row_outs]
    out_specs += [pl.BlockSpec(s, functools.partial(lambda i, n: (0,) * n, n=len(s))) for s in acc_outs]
    out_shape = [jax.ShapeDtypeStruct((t_len, w), dt) for w, dt in row_outs]
    out_shape += [jax.ShapeDtypeStruct(s, F32) for s in acc_outs]
    return pl.pallas_call(
        body, name=name, grid=(t_len // tr,), in_specs=in_specs, out_specs=out_specs, out_shape=out_shape,
        compiler_params=_params(("arbitrary",)),
    )(*[r[0] for r in rows], *([r[0] for r in rows] if halo else []), *consts)


def _colwise(name, fn, cols, prms, col_outs, prm_outs=()):
    t_len = cols[0][0].shape[0]
    n_blocks = col_outs[0][0] // LANE
    n_i = len(cols) + len(prms)

    def body(*refs):
        outs = fn(*[r[...] for r in refs[:n_i]])
        for o_ref, val in zip(refs[n_i:], outs):
            o_ref[...] = val.astype(o_ref.dtype)

    def spec(r, off):
        return pl.BlockSpec((r, LANE), functools.partial(lambda j, o: (0, o + j), o=off))

    in_specs = [spec(t_len, off) for _, off in cols] + [spec(a.shape[0], off) for a, off in prms]
    out_specs = [spec(t_len, 0) for _ in col_outs] + [spec(r, 0) for r, _ in prm_outs]
    out_shape = [jax.ShapeDtypeStruct((t_len, w), dt) for w, dt in col_outs]
    out_shape += [jax.ShapeDtypeStruct((r, w), F32) for r, w in prm_outs]
    return pl.pallas_call(
        body, name=name, grid=(n_blocks,), in_specs=in_specs, out_specs=out_specs, out_shape=out_shape,
        compiler_params=_params(("arbitrary",)),
    )(*[c[0] for c in cols], *[p[0] for p in prms])


def _matmul(name, a, b, dn, outs, *, tm, tn, tk, extras=(), consts=(), epilogue=None, sums=(), xch=(), xch_scatter=()):
    if dn == "nn":
        (m, k), n = a.shape, b.shape[1]
    elif dn == "nt":
        (m, k), n = a.shape, b.shape[0]
    else:
        (k, m), n = a.shape, b.shape[1]
    tm, tn, tk = min(tm, m), min(tn, n), min(tk, k)
    nk = k // tk
    grid = (m // tm, n // tn, nk)
    assert not sums or (grid[1] == 1 and nk == 1)
    a_spec = pl.BlockSpec((tk, tm), lambda i, j, q: (q, i)) if dn == "tn" else pl.BlockSpec((tm, tk), lambda i, j, q: (i, q))
    b_spec = pl.BlockSpec((tn, tk), lambda i, j, q: (j, q)) if dn == "nt" else pl.BlockSpec((tk, tn), lambda i, j, q: (q, j))
    o_spec = pl.BlockSpec((tm, tn), lambda i, j, q: (i, j))
    c_spec = pl.BlockSpec((1, tn), lambda i, j, q: (0, j))
    n_e, n_c, n_o, n_s, n_x = len(extras), len(consts), len(outs), len(sums), len(xch)
    x_specs, x_shapes, x_sems = _exchange_io(xch, xch_scatter) if n_x else ([], [], [])

    def body(*refs):
        a_ref, b_ref = refs[:2]
        e_refs = refs[2:2 + n_e + n_c]
        x_in = refs[2 + n_e + n_c:2 + n_e + n_c + n_x]
        rest = refs[2 + n_e + n_c + n_x:]
        o_refs, s_refs, x_out, scratch = rest[:n_o], rest[n_o:n_o + n_s], rest[n_o + n_s:n_o + n_s + n_x], rest[n_o + n_s + n_x:]
        step = (pl.program_id(0) * grid[1] + pl.program_id(1)) * nk + pl.program_id(2)
        if n_x:
            start, wait = _exchange_plan(x_in, x_out, xch_scatter, *scratch[len(scratch) - 3:])
            pl.when(step == 0)(start)
        part = lax.dot_general(a_ref[...].astype(BF16), b_ref[...].astype(BF16), _dims(dn, 2), preferred_element_type=F32)

        def finish(acc):
            vals = (acc,) if epilogue is None else epilogue(acc, *[e[...] for e in e_refs])
            for o_ref, val in zip(o_refs, vals[:n_o]):
                o_ref[...] = val.astype(o_ref.dtype)
            if n_s:
                @pl.when(step == 0)
                def _():
                    for s_ref, val in zip(s_refs, vals[n_o:]):
                        s_ref[...] = val

                @pl.when(step > 0)
                def _():
                    for s_ref, val in zip(s_refs, vals[n_o:]):
                        s_ref[...] += val

        if nk == 1:
            finish(part)
        else:
            acc_ref = scratch[0]
            q = pl.program_id(2)

            @pl.when(q == 0)
            def _():
                acc_ref[...] = part

            @pl.when(q > 0)
            def _():
                acc_ref[...] += part

            @pl.when(q == nk - 1)
            def _():
                finish(acc_ref[...])

        if n_x:
            pl.when(step == grid[0] * grid[1] * nk - 1)(wait)

    plain = not (n_s or n_x)
    res = pl.pallas_call(
        body, name=name, grid=grid,
        in_specs=[a_spec, b_spec] + [o_spec] * n_e + [c_spec] * n_c + x_specs,
        out_specs=[o_spec] * n_o + [c_spec] * n_s + x_specs,
        out_shape=[jax.ShapeDtypeStruct((m, n), dt) for dt in outs] + [jax.ShapeDtypeStruct(s, F32) for s in sums] + x_shapes,
        scratch_shapes=([pltpu.VMEM((tm, tn), F32)] if nk > 1 else []) + x_sems,
        compiler_params=pltpu.CompilerParams(
            dimension_semantics=("parallel", "parallel", "arbitrary") if plain else ("arbitrary",) * 3,
            vmem_limit_bytes=VMEM_LIMIT, has_side_effects=bool(n_x)),
    )(a, b, *extras, *consts, *xch)
    return res[0] if len(res) == 1 else res


def _rms(h, g):
    return h * lax.rsqrt(jnp.mean(h * h, axis=-1, keepdims=True) + RMS_EPS) * g


def _rms_bwd(h, g, dy):
    rs = lax.rsqrt(jnp.mean(h * h, axis=-1, keepdims=True) + RMS_EPS)
    n = h * rs
    dn = dy * g
    dh = rs * (dn - n * jnp.mean(dn * n, axis=-1, keepdims=True))
    return dh, jnp.sum(dy * n, axis=0, keepdims=True)


def _rwkv_pre(k, xw, xa, xg, w0, a0, k_k, k_a, wl, al, gl):
    zw = w0 + _mm(jnp.tanh(xw), wl, "nn", 1)
    lw = -jnp.exp(-_softplus(-zw) - 0.5)
    iclr = _sigmoid(a0 + _mm(xa, al, "nn", 1))
    g = _mm(_sigmoid(xg), gl, "nn", 1)
    kk0 = k * k_k
    kk = kk0 / jnp.maximum(jnp.sqrt(_hsum(kk0 * kk0)), L2_EPS)
    k_h = k * (1.0 + (iclr - 1.0) * k_a)
    return lw, k_h, -kk, kk * iclr, g


def _rwkv_post(y, r, k_h, v, g, ln_g, ln_b, r_k):
    mu = _hsum(y) * (1.0 / HEAD_DIM)
    yc = y - mu
    var = _hsum(yc * yc) * (1.0 / HEAD_DIM)
    yo = yc * lax.rsqrt(var + GN_EPS) * ln_g + ln_b
    bonus = _hsum(r * k_h * r_k) * v
    return (yo + bonus) * g


def _shift_down(x, n):
    rows = lax.broadcasted_iota(jnp.int32, x.shape, 0)
    return jnp.where(rows < n, 0.0, pltpu.roll(x, n, 0))


def _shift_up(x, n):
    t_len = x.shape[0]
    rows = lax.broadcasted_iota(jnp.int32, x.shape, 0)
    return jnp.where(rows >= t_len - n, 0.0, pltpu.roll(x, t_len - n, 0))


def _exchange_plan(ins, outs, scatter, send_sems, recv_sems, local_sems):
    x, y, c = lax.axis_index("x"), lax.axis_index("y"), lax.axis_index("c")
    me = 4 * x + 2 * y + c

    def local(i):
        return pltpu.make_async_copy(ins[i].at[me] if scatter[i] else ins[i], outs[i].at[me], local_sems.at[i])

    def send(i, rel):
        return pltpu.make_async_remote_copy(
            src_ref=ins[i].at[me ^ rel] if scatter[i] else ins[i], dst_ref=outs[i].at[me],
            send_sem=send_sems.at[i, rel - 1], recv_sem=recv_sems.at[i, rel - 1],
            device_id=(x ^ (rel >> 2), y ^ ((rel >> 1) & 1), c ^ (rel & 1)), device_id_type=pl.DeviceIdType.MESH)

    def landed(i, rel):
        slot = outs[i].at[me ^ rel]
        return pltpu.make_async_remote_copy(
            src_ref=slot, dst_ref=slot, send_sem=send_sems.at[i, rel - 1], recv_sem=recv_sems.at[i, rel - 1],
            device_id=(x, y, c), device_id_type=pl.DeviceIdType.MESH)

    def start():
        for i in range(len(ins)):
            local(i).start()
            for rel in range(1, N_DEV):
                send(i, rel).start()

    def wait():
        for i in range(len(ins)):
            local(i).wait()
            for rel in range(1, N_DEV):
                landed(i, rel).wait_recv()
            for rel in range(1, N_DEV):
                send(i, rel).wait_send()

    return start, wait


def _gather_plan(ins, outs, send_sems, recv_sems, local_sems):
    x, y, c = lax.axis_index("x"), lax.axis_index("y"), lax.axis_index("c")
    me = 4 * x + 2 * y + c
    direct, chips = (1, 2, 4, 6), (2, 4, 6)

    def local(i):
        return pltpu.make_async_copy(ins[i], outs[i].at[me], local_sems.at[i])

    def send(i, rel):
        return pltpu.make_async_remote_copy(
            src_ref=ins[i], dst_ref=outs[i].at[me], send_sem=send_sems.at[i, rel - 1], recv_sem=recv_sems.at[i, rel - 1],
            device_id=(x ^ (rel >> 2), y ^ ((rel >> 1) & 1), c ^ (rel & 1)), device_id_type=pl.DeviceIdType.MESH)

    def passed(i, rel):
        slot = outs[i].at[me ^ rel]
        return pltpu.make_async_remote_copy(
            src_ref=slot, dst_ref=slot, send_sem=send_sems.at[i, rel], recv_sem=recv_sems.at[i, rel],
            device_id=(x, y, 1 - c), device_id_type=pl.DeviceIdType.MESH)

    def landed(i, rel):
        slot = outs[i].at[me ^ rel]
        return pltpu.make_async_remote_copy(
            src_ref=slot, dst_ref=slot, send_sem=send_sems.at[i, rel - 1], recv_sem=recv_sems.at[i, rel - 1],
            device_id=(x, y, c), device_id_type=pl.DeviceIdType.MESH)

    def start():
        for i in range(len(ins)):
            local(i).start()
            for rel in direct:
                send(i, rel).start()

    def forward(i):
        for rel in chips:
            landed(i, rel).wait_recv()
            passed(i, rel).start()

    def wait():
        for i in range(len(ins)):
            local(i).wait()
            for rel in (1, 3, 5, 7):
                landed(i, rel).wait_recv()
            for rel in direct:
                send(i, rel).wait_send()
            for rel in chips:
                passed(i, rel).wait_send()

    return start, forward, wait


def _exchange_io(arrays, scatter):
    n = len(arrays)
    any_spec = pl.BlockSpec(memory_space=pl.ANY)
    out_shape = [jax.ShapeDtypeStruct(a.shape if sc else (N_DEV,) + a.shape, a.dtype) for a, sc in zip(arrays, scatter)]
    sems = [pltpu.SemaphoreType.DMA((n, N_DEV - 1)), pltpu.SemaphoreType.DMA((n, N_DEV - 1)), pltpu.SemaphoreType.DMA((n,))]
    return [any_spec] * n, out_shape, sems


def _exchange(name, arrays, scatter):
    n = len(arrays)
    specs, out_shape, sems = _exchange_io(arrays, scatter)

    def body(*refs):
        if any(scatter):
            start, wait = _exchange_plan(refs[:n], refs[n:2 * n], scatter, *refs[2 * n:])
            start()
        else:
            start, forward, wait = _gather_plan(refs[:n], refs[n:2 * n], *refs[2 * n:])
            start()
            for i in range(n):
                forward(i)
        wait()

    return pl.pallas_call(
        body, name=name, in_specs=specs, out_specs=specs, out_shape=out_shape, scratch_shapes=sems,
        compiler_params=pltpu.CompilerParams(has_side_effects=True),
    )(*arrays)


def _chunk_fwd(z0, r, lw, k, v, a, b):
    n_h, c, n_k = r.shape
    mm = functools.partial(_mm, passes=REC_PASSES)
    gram = functools.partial(_mm, passes=3)
    ti = lax.broadcasted_iota(jnp.int32, (c, c), 0)
    si = lax.broadcasted_iota(jnp.int32, (c, c), 1)
    strict, incl = si < ti, si <= ti
    cum = _mm(jnp.broadcast_to(incl.astype(F32), (n_h, c, c)), lw, "nn", 3)
    cum_end = cum[:, c - 1:c, :]
    at = a * jnp.exp(cum - lw)
    bt = b * jnp.exp(-cum)
    kt = k * jnp.exp(-cum)
    rt = r * jnp.exp(cum)
    be = b * jnp.exp(cum_end - cum)
    ke = k * jnp.exp(cum_end - cum)
    lab = jnp.where(strict, gram(at, bt, "nt"), 0.0)
    lak = jnp.where(strict, gram(at, kt, "nt"), 0.0)
    u = mm(at, z0, "nn") + mm(lak, v, "nn")
    p = lab
    n = 1
    while n < c:
        u = u + mm(p, u, "nn")
        n *= 2
        if n < c:
            p = mm(p, p, "nn")
    rb = jnp.where(incl, gram(rt, bt, "nt"), 0.0)
    rk = jnp.where(incl, gram(rt, kt, "nt"), 0.0)
    y = mm(rt, z0, "nn") + mm(rb, u, "nn") + mm(rk, v, "nn")
    ki = lax.broadcasted_iota(jnp.int32, (n_k, n_k), 0)
    kj = lax.broadcasted_iota(jnp.int32, (n_k, n_k), 1)
    dmat = jnp.where(ki == kj, jnp.broadcast_to(jnp.exp(cum_end), (n_h, n_k, n_k)), 0.0)
    z_end = mm(dmat, z0, "nn") + mm(be, u, "tn") + mm(ke, v, "tn")
    return y, z_end


def _heads(x):
    return jnp.stack([x[:, h * HEAD_DIM:(h + 1) * HEAD_DIM] for h in range(N_HEADS)])


def _unheads(x):
    return jnp.concatenate([x[h] for h in range(N_HEADS)], axis=-1)


def _rec_params():
    return pltpu.CompilerParams(dimension_semantics=("arbitrary",), vmem_limit_bytes=VMEM_LIMIT, has_side_effects=True)


def _rec_fwd(u, lw, k, a, b, xch):
    t_len = lw.shape[0]
    c = min(REC_CHUNK, t_len)
    nc = t_len // c
    n_x = len(xch)
    x_specs, x_shapes, x_sems = _exchange_io(xch, [False] * n_x)
    sizes = [a_.size * a_.dtype.itemsize for a_ in xch]
    pass_step = [min(nc - 1, int(0.9 * nc * sum(sizes[:j + 1]) / sum(sizes)) + 1) for j in range(n_x)]

    def body(*refs):
        r_ref, v_ref, lw_ref, k_ref, a_ref, b_ref = refs[:6]
        x_in = refs[6:6 + n_x]
        y_ref, zs_ref = refs[6 + n_x:8 + n_x]
        x_out = refs[8 + n_x:8 + 2 * n_x]
        z_scr = refs[8 + 2 * n_x]
        start, forward, wait = _gather_plan(x_in, x_out, *refs[9 + 2 * n_x:])
        i = pl.program_id(0)

        @pl.when(i == 0)
        def _():
            start()
            z_scr[...] = jnp.zeros_like(z_scr)

        z0 = z_scr[...]
        zs_ref[0] = z0
        y, z_end = _chunk_fwd(z0, _heads(r_ref[...]), _heads(lw_ref[...]), _heads(k_ref[...]), _heads(v_ref[...]),
                              _heads(a_ref[...]), _heads(b_ref[...]))
        y_ref[...] = _unheads(y)
        z_scr[...] = z_end

        for j in range(n_x):
            pl.when(i == pass_step[j])(functools.partial(forward, j))

        @pl.when(i == nc - 1)
        def _():
            wait()

    blk = lambda cb: pl.BlockSpec((c, RWKV_DIM), functools.partial(lambda i, q: (i, q), q=cb))
    res = pl.pallas_call(
        body, name="rwkv_rec_fwd", grid=(nc,),
        in_specs=[blk(0), blk(2)] + [blk(0)] * 4 + x_specs,
        out_specs=[blk(0), pl.BlockSpec((1, N_HEADS, HEAD_DIM, HEAD_DIM), lambda i: (i, 0, 0, 0))] + x_specs,
        out_shape=[jax.ShapeDtypeStruct((t_len, RWKV_DIM), F32),
                   jax.ShapeDtypeStruct((nc, N_HEADS, HEAD_DIM, HEAD_DIM), F32)] + x_shapes,
        scratch_shapes=[pltpu.VMEM((N_HEADS, HEAD_DIM, HEAD_DIM), F32)] + x_sems,
        compiler_params=_rec_params(),
    )(u, u, lw, k, a, b, *xch)
    return res[0], res[1], res[2:]


def _rec_bwd(u, lw, k, a, b, zs, dy, xch, xch_scatter):
    t_len = lw.shape[0]
    c = min(REC_CHUNK, t_len)
    nc = t_len // c
    n_x = len(xch)
    x_specs, x_shapes, x_sems = _exchange_io(xch, xch_scatter)

    def body(*refs):
        r_ref, v_ref, lw_ref, k_ref, a_ref, b_ref, zs_ref, dy_ref = refs[:8]
        x_in = refs[8:8 + n_x]
        g_refs = refs[8 + n_x:14 + n_x]
        x_out = refs[14 + n_x:14 + 2 * n_x]
        dz_scr = refs[14 + 2 * n_x]
        start, wait = _exchange_plan(x_in, x_out, xch_scatter, *refs[15 + 2 * n_x:])
        i = pl.program_id(0)

        @pl.when(i == 0)
        def _():
            start()
            dz_scr[...] = jnp.zeros_like(dz_scr)

        _, vjp = jax.vjp(_chunk_fwd, zs_ref[0], _heads(r_ref[...]), _heads(lw_ref[...]), _heads(k_ref[...]),
                         _heads(v_ref[...]), _heads(a_ref[...]), _heads(b_ref[...]))
        dz0, dr, dlw, dk, dv, da, db = vjp((_heads(dy_ref[...]), dz_scr[...]))
        for ref, val in zip(g_refs, (dr, dv, dlw, dk, da, db)):
            ref[...] = _unheads(val)
        dz_scr[...] = dz0

        @pl.when(i == nc - 1)
        def _():
            wait()

    blk = lambda cb: pl.BlockSpec((c, RWKV_DIM), functools.partial(lambda i, q: (nc - 1 - i, q), q=cb))
    res = pl.pallas_call(
        body, name="rwkv_rec_bwd", grid=(nc,),
        in_specs=[blk(0), blk(2)] + [blk(0)] * 4
                 + [pl.BlockSpec((1, N_HEADS, HEAD_DIM, HEAD_DIM), lambda i: (nc - 1 - i, 0, 0, 0)), blk(0)] + x_specs,
        out_specs=[blk(0)] * 6 + x_specs,
        out_shape=[jax.ShapeDtypeStruct((t_len, RWKV_DIM), F32)] * 6 + x_shapes,
        scratch_shapes=[pltpu.VMEM((N_HEADS, HEAD_DIM, HEAD_DIM), F32)] + x_sems,
        compiler_params=_rec_params(),
    )(u, u, lw, k, a, b, zs, dy, *xch)
    return res[:6], res[6:]


_EARLY = ["w_in", "conv_w", "w_lora_up", "a_lora_up", "g_lora_up"]
_LATE = ["w_out", "w_up", "w_down", "w_ple_gate", "w_ple_proj"]
_SHARDED = _EARLY + _LATE
_COL_SHARDED = {"w_in", "conv_w", "w_lora_up", "a_lora_up", "g_lora_up", "w_up", "w_ple_proj"}
_BF16_GATHER = {"w_in", "w_out", "w_up", "w_down", "w_ple_gate", "w_ple_proj"}
_REPLICATED = ["norm_mix_g", "shift_mu", "w0", "a0", "k_k", "k_a", "r_k", "ln_x_g", "ln_x_b", "norm_mlp_g", "norm_ple_g",
               "norm_final_g"]
_WEIGHTS = ["norm_mix_g", "w_in", "conv_w", "shift_mu", "w_lora_up", "w0", "a_lora_up", "a0", "g_lora_up", "k_k", "k_a", "r_k",
            "ln_x_g", "ln_x_b", "w_out", "norm_mlp_g", "w_up", "w_down", "norm_ple_g", "w_ple_gate", "w_ple_proj", "norm_final_g"]
_PACK_ROWS = 80


def _unshard(name, g):
    if name in _COL_SHARDED:
        return jnp.moveaxis(g, 0, 1).reshape(g.shape[1], N_DEV * g.shape[2])
    return g.reshape(N_DEV * g.shape[1], g.shape[2])


def _reshard(name, full):
    if name in _COL_SHARDED:
        return jnp.moveaxis(full.reshape(full.shape[0], N_DEV, full.shape[1] // N_DEV), 1, 0)
    return full.reshape(N_DEV, full.shape[0] // N_DEV, full.shape[1])


def _pad_in_cols(a):
    z = lambda n: jnp.zeros(a.shape[:-1] + (n,), a.dtype)
    return jnp.concatenate([a[..., :3136], z(64), a[..., 3136:3200], z(64), a[..., 3200:3360], z(96)], axis=-1)


def _unpad_in_cols(a):
    return jnp.concatenate([a[..., :3136], a[..., 3200:3264], a[..., 3328:3488]], axis=-1)


def _assemble_w_in(g):
    n_dev, rows, cols = g.shape

    def body(g_ref, o_ref):
        o_ref[...] = _pad_in_cols(jnp.concatenate([g_ref[d] for d in range(n_dev)], axis=1))

    return pl.pallas_call(
        body, name="w_in_assemble", grid=(rows // ROW_BLOCK,),
        in_specs=[pl.BlockSpec((n_dev, ROW_BLOCK, cols), lambda i: (0, i, 0))],
        out_specs=pl.BlockSpec((ROW_BLOCK, IN_PAD), lambda i: (i, 0)),
        out_shape=jax.ShapeDtypeStruct((rows, IN_PAD), g.dtype), compiler_params=_params(("arbitrary",)),
    )(g)


def _split_w_in_grad(dw):
    rows = dw.shape[0]
    cols = IN_COLS // N_DEV

    def body(d_ref, o_ref):
        full = _unpad_in_cols(d_ref[...])
        for d in range(N_DEV):
            o_ref[d] = full[:, cols * d:cols * (d + 1)]

    return pl.pallas_call(
        body, name="w_in_grad_split", grid=(rows // ROW_BLOCK,),
        in_specs=[pl.BlockSpec((ROW_BLOCK, IN_PAD), lambda i: (i, 0))],
        out_specs=pl.BlockSpec((N_DEV, ROW_BLOCK, cols), lambda i: (0, i, 0)),
        out_shape=jax.ShapeDtypeStruct((N_DEV, rows, cols), dw.dtype), compiler_params=_params(("arbitrary",)),
    )(dw)


def _pad_rows(a, rows):
    return jnp.concatenate([a, jnp.zeros((rows - a.shape[0],) + a.shape[1:], a.dtype)], axis=0)


def _pack(vals):
    flat = jnp.concatenate([v.reshape(-1) for v in vals])
    return jnp.concatenate([flat, jnp.zeros((_PACK_ROWS * LANE - flat.shape[0],), F32)]).reshape(_PACK_ROWS, LANE)


def _local_step(x, p, tgt, w, late_shards):
    row = lambda v: v.reshape(1, -1)
    w = dict(w)

    (xn1,) = _rowwise("rms_mix", lambda h, g: (_rms(h, g),), [x], [w["norm_mix_g"]], [(D_MODEL, BF16)])
    proj = _matmul("in_proj", xn1, w["w_in"], "nn", [F32], tm=1024, tn=512, tk=D_MODEL)
    proj_gb, proj_gc, proj_h, proj_rw = (proj, 0), (proj, 4), (proj, 8), (proj, CONV_COLS // LANE)

    def conv_fwd(gb, gc, hx, cw):
        uu = gc * hx
        return (gb * (uu * cw[2:3] + _shift_down(uu, 1) * cw[1:2] + _shift_down(uu, 2) * cw[0:1]),)

    (y_conv,) = _colwise("conv_fwd", conv_fwd, [proj_gb, proj_gc, proj_h], [(w["conv_w"], 0)], [(CONV_DIM, BF16)])

    small = [w["w0"], w["a0"], w["k_k"], w["k_a"], w["w_lora_up"], w["a_lora_up"], w["g_lora_up"]]
    seg_w = [RWKV_DIM, RWKV_DIM, RWKV_DIM, LANE, LANE, 2 * LANE]
    seg_off = [0, 512, 1024, XW_OFF, XA_OFF, XG_OFF]

    def pre_fwd(*xs):
        cur, prev_rows, mu, prm = xs[:6], xs[6:12], xs[12], xs[13:]
        segs = []
        for c_, p_, off, wd in zip(cur, prev_rows, seg_off, seg_w):
            rows = lax.broadcasted_iota(jnp.int32, c_.shape, 0)
            prev = jnp.where(rows == 0, p_, pltpu.roll(c_, 1, 0))
            segs.append(c_ + mu[:, off:off + wd] * (prev - c_))
        return (jnp.concatenate(segs, axis=1),) + tuple(_rwkv_pre(segs[1], segs[3], segs[4], segs[5], *prm))

    proj_segs = [(proj, wd, (CONV_COLS + off) // wd) for off, wd in zip(seg_off, seg_w)]
    u, lw, k_h, ra, rb, g = _rowwise("rwkv_pre", pre_fwd, proj_segs, [w["shift_mu"]] + small,
                                     [(RW_PAD, F32)] + [(RWKV_DIM, F32)] * 5, halo=True)
    u_k, u_xw, u_xa, u_xg = (u, 512, 1), (u, LANE, XW_OFF // LANE), (u, LANE, XA_OFF // LANE), (u, 2 * LANE, XG_OFF // (2 * LANE))
    y_rec, zs, late = _rec_fwd(u, lw, k_h, ra, rb, late_shards)
    for n, gathered in zip(_LATE, late):
        w[n] = _unshard(n, gathered)
    post_c = [w["ln_x_g"], w["ln_x_b"], w["r_k"]]
    u_r, u_v = (u, 512, 0), (u, 512, 2)
    (y_rwkv,) = _rowwise("rwkv_post", lambda *xs: (_rwkv_post(*xs),), [y_rec, u_r, k_h, u_v, g], post_c, [(RWKV_DIM, BF16)])
    ycat = jnp.concatenate([y_conv, y_rwkv], axis=1)
    def res_norm(acc, r_, g_):
        h = acc + r_
        return h, _rms(h, g_)

    h1, xn2 = _matmul("out_proj", ycat, w["w_out"], "nn", [F32, BF16], tm=1024, tn=D_MODEL, tk=D_MODEL, extras=[x],
                      consts=[w["norm_mlp_g"]], epilogue=res_norm)

    def relu2(acc):
        hid = jnp.maximum(acc, 0.0)
        return hid, hid * hid

    hid, hsq = _matmul("mlp_up", xn2, w["w_up"], "nn", [BF16, BF16], tm=1024, tn=1024, tk=D_MODEL, epilogue=relu2)
    h2, xn3 = _matmul("mlp_down", hsq, w["w_down"], "nn", [F32, BF16], tm=512, tn=D_MODEL, tk=D_FF, extras=[h1],
                      consts=[w["norm_ple_g"]], epilogue=res_norm)
    zg = _matmul("ple_gate", xn3, w["w_ple_gate"], "nn", [F32], tm=1024, tn=1024, tk=D_MODEL)
    pp = _matmul("ple_proj", p, w["w_ple_proj"], "nn", [F32], tm=1024, tn=1024, tk=PLE_DIM)

    def head(h2_, zg_, pp_, tg, gf):
        gate = _sigmoid(zg_)
        h3 = h2_ + gate * pp_
        out = _rms(h3, gf)
        err = out - tg
        dh3, dgf = _rms_bwd(h3, gf, err * (1.0 / D_MODEL))
        loss = jnp.sum(jnp.sum(err * err, axis=1, keepdims=True), axis=0, keepdims=True) * (0.5 / D_MODEL)
        return dh3, dh3 * pp_ * gate * (1.0 - gate), dh3 * gate, dgf, loss

    dh3, dzg, dpp, d_norm_final, loss = _rowwise(
        "head", head, [h2, zg, pp, tgt], [row(w["norm_final_g"])], [(D_MODEL, F32), (D_MODEL, BF16), (D_MODEL, BF16)],
        [(1, D_MODEL), (1, 1)])

    d_w_ple_proj = _matmul("d_ple_proj", p, dpp, "tn", [BF16], tm=PLE_DIM, tn=1024, tk=4096)
    d_w_ple_gate = _matmul("d_ple_gate", xn3, dzg, "tn", [BF16], tm=512, tn=1024, tk=4096)

    def norm_bwd(dxn, h, dres, g_):
        dh, dg = _rms_bwd(h, g_, dxn)
        dh = dh + dres
        return dh, dh, dg

    nb = dict(tm=512, tn=D_MODEL, epilogue=norm_bwd, sums=[(1, D_MODEL)])
    dh2, dh2_b, d_norm_ple = _matmul("dx_ple_gate", dzg, w["w_ple_gate"], "nt", [F32, BF16], tk=D_MODEL,
                                     extras=[h2, dh3], consts=[w["norm_ple_g"]], **nb)
    d_w_down = _matmul("d_mlp_down", hsq, dh2_b, "tn", [BF16], tm=512, tn=1024, tk=4096)
    dpre = _matmul("dx_mlp_down", dh2_b, w["w_down"], "nt", [BF16], tm=1024, tn=1024, tk=D_MODEL, extras=[hid],
                   epilogue=lambda acc, hid_: (acc * (2.0 * hid_.astype(F32)),))
    d_w_up = _matmul("d_mlp_up", xn2, dpre, "tn", [BF16], tm=512, tn=1024, tk=4096)
    dh1, dh1_b, d_norm_mlp = _matmul("dx_mlp_up", dpre, w["w_up"], "nt", [F32, BF16], tk=D_FF,
                                     extras=[h1, dh2], consts=[w["norm_mlp_g"]], **nb)
    d_w_out = _matmul("d_out_proj", ycat, dh1_b, "tn", [BF16], tm=512, tn=1024, tk=4096)
    dycat = _matmul("dx_out_proj", dh1_b, w["w_out"], "nt", [F32], tm=1024, tn=1024, tk=D_MODEL)
    late_grads = dict(w_out=d_w_out, w_up=d_w_up, w_down=d_w_down, w_ple_gate=d_w_ple_gate, w_ple_proj=d_w_ple_proj)

    def conv_bwd(dy, gb, gc, hx, cw):
        uu = gc * hx
        u1, u2 = _shift_down(uu, 1), _shift_down(uu, 2)
        dconv = dy * gb
        du = dconv * cw[2:3] + _shift_up(dconv, 1) * cw[1:2] + _shift_up(dconv, 2) * cw[0:1]
        s = lambda z: jnp.sum(z, axis=0, keepdims=True)
        return (dy * (uu * cw[2:3] + u1 * cw[1:2] + u2 * cw[0:1]), du * hx, du * gc,
                s(dconv * u2), s(dconv * u1), s(dconv * uu))

    dgb, dgc, dhx, dcw0, dcw1, dcw2 = _colwise(
        "conv_bwd", conv_bwd, [(dycat, 0), proj_gb, proj_gc, proj_h], [(w["conv_w"], 0)],
        [(CONV_DIM, BF16)] * 3, [(1, CONV_DIM)] * 3)

    def post_bwd(dy, y, r, k_h_, v, g_, ln_g, ln_b, r_k):
        _, vjp = jax.vjp(_rwkv_post, y, r, k_h_, v, g_, ln_g, ln_b, r_k)
        return vjp(dy)

    dy_rec, dr_p, dk_p, dv_p, dg, d_ln_g, d_ln_b, d_r_k = _rowwise(
        "rwkv_post_bwd", post_bwd, [(dycat, 512, 1), y_rec, u_r, k_h, u_v, g], post_c,
        [(RWKV_DIM, F32)] * 5, [(1, RWKV_DIM)] * 3)
    (dr_r, dv_r, dlw, dk_r, da, db), late_parts = _rec_bwd(
        u, lw, k_h, ra, rb, zs, dy_rec, [_reshard(n, late_grads[n]) for n in _LATE], [True] * len(_LATE))

    def pre_bwd(k, xw, xa, xg, dr1, dr2, dv1, dv2, dlw_, dk1, dk2, da_, db_, dg_, w0, a0, k_k, k_a, wl, al, gl):
        _, vjp = jax.vjp(_rwkv_pre, k, xw, xa, xg, w0, a0, k_k, k_a, wl, al, gl)
        dk, dxw, dxa, dxg, dw0, da0, dk_k, dk_a, dwl, dal, dgl = vjp((dlw_, dk1 + dk2, da_, db_, dg_))
        du = jnp.concatenate([dr1 + dr2, dk, dv1 + dv2, dxw, dxa, dxg], axis=1)
        return du, dw0, da0, dk_k, dk_a, dwl, dal, dgl

    du, d_w0, d_a0, d_k_k, d_k_a, d_wl, d_al, d_gl = _rowwise(
        "rwkv_pre_bwd", pre_bwd, [u_k, u_xw, u_xa, u_xg, dr_p, dr_r, dv_p, dv_r, dlw, dk_p, dk_r, da, db, dg], small,
        [(RW_PAD, F32)], [(1, RWKV_DIM)] * 4 + [(LANE, RWKV_DIM), (LANE, RWKV_DIM), (2 * LANE, RWKV_DIM)])

    def shift_bwd(du_, rr, mu):
        return du_ - mu * du_ + mu * _shift_up(du_, 1), jnp.sum(du_ * (_shift_down(rr, 1) - rr), axis=0, keepdims=True)

    dprojr, d_mu = _colwise("shift_bwd", shift_bwd, [(du, 0), proj_rw], [(w["shift_mu"], 0)], [(RW_PAD, BF16)], [(1, RW_PAD)])
    dproj = jnp.concatenate([dgb, dgc, dhx, dprojr], axis=1)
    d_w_in = _matmul("d_in_proj", xn1, dproj, "tn", [BF16], tm=512, tn=896, tk=4096)
    early_grads = dict(conv_w=jnp.concatenate([dcw0, dcw1, dcw2], axis=0),
                       w_lora_up=d_wl[:64], a_lora_up=d_al[:64], g_lora_up=d_gl[:160])
    early_send = [_split_w_in_grad(d_w_in)] + [_reshard(n, early_grads[n]) for n in _EARLY[1:]]
    dx, d_norm_mix, *early_parts = _matmul(
        "dx_in_proj", dproj, w["w_in"], "nt", [F32], tk=IN_PAD, extras=[x, dh1], consts=[w["norm_mix_g"]],
        xch=early_send, xch_scatter=[True] * len(_EARLY), **dict(nb, epilogue=lambda *a: norm_bwd(*a)[1:]))

    grads = dict(
        norm_mix_g=d_norm_mix, shift_mu=d_mu, w0=d_w0, a0=d_a0, k_k=d_k_k, k_a=d_k_a, r_k=d_r_k,
        ln_x_g=d_ln_g, ln_x_b=d_ln_b, norm_mlp_g=d_norm_mlp, norm_ple_g=d_norm_ple, norm_final_g=d_norm_final)
    parts = dict(zip(_EARLY, early_parts))
    parts.update(zip(_LATE, late_parts))
    return loss, dx, grads, parts


def _adamw(name, parts, w, m, v):
    rows, cols = w.shape[-2:]
    lead = w.ndim - 2
    tr = rows if rows * cols * 4 * 8 <= (4 << 20) else max(8, (4 << 20) // (cols * 4 * 8) // 8 * 8)
    while rows % tr:
        tr -= 8

    def body(p_ref, w_ref, m_ref, v_ref, g_ref, d_ref, nm_ref, nv_ref):
        g = p_ref[0].astype(F32)
        for s in range(1, N_DEV):
            g = g + p_ref[s].astype(F32)
        nm = ADAM_B1 * m_ref[...] + (1.0 - ADAM_B1) * g
        nv = ADAM_B2 * v_ref[...] + (1.0 - ADAM_B2) * (g * g)
        m_hat = nm / (1.0 - ADAM_B1 ** ADAM_STEP)
        v_hat = nv / (1.0 - ADAM_B2 ** ADAM_STEP)
        g_ref[...] = g
        d_ref[...] = -ADAM_LR * (m_hat / (jnp.sqrt(v_hat) + ADAM_EPS) + ADAM_WD * w_ref[...])
        nm_ref[...] = nm
        nv_ref[...] = nv

    blk = pl.BlockSpec((None,) * lead + (tr, cols), lambda i: (0,) * lead + (i, 0))
    return pl.pallas_call(
        body, name=name, grid=(rows // tr,),
        in_specs=[pl.BlockSpec((N_DEV, tr, cols), lambda i: (0, i, 0)), blk, blk, blk], out_specs=[blk] * 4,
        out_shape=[jax.ShapeDtypeStruct(w.shape, F32)] * 4,
        compiler_params=_params(("arbitrary",)),
    )(parts, w, m, v)


def kernel(x, p, norm_mix_g, w_in, conv_w, shift_mu, w_lora_up, w0, a_lora_up, a0, g_lora_up, k_k, k_a, r_k, ln_x_g, ln_x_b, w_out, norm_mlp_g, w_up, w_down, norm_ple_g, w_ple_gate, w_ple_proj, norm_final_g, loss_target, m_norm_mix_g, m_w_in, m_conv_w, m_shift_mu, m_w_lora_up, m_w0, m_a_lora_up, m_a0, m_g_lora_up, m_k_k, m_k_a, m_r_k, m_ln_x_g, m_ln_x_b, m_w_out, m_norm_mlp_g, m_w_up, m_w_down, m_norm_ple_g, m_w_ple_gate, m_w_ple_proj, m_norm_final_g, v_norm_mix_g, v_w_in, v_conv_w, v_shift_mu, v_w_lora_up, v_w0, v_a_lora_up, v_a0, v_g_lora_up, v_k_k, v_k_a, v_r_k, v_ln_x_g, v_ln_x_b, v_w_out, v_norm_mlp_g, v_w_up, v_w_down, v_norm_ple_g, v_w_ple_gate, v_w_ple_proj, v_norm_final_g):
    args = dict(locals())
    wts = {n: args[n] for n in _WEIGHTS}
    mom = {n: args["m_" + n] for n in _WEIGHTS}
    var = {n: args["v_" + n] for n in _WEIGHTS}
    shard2d = lambda a: a.reshape(a.shape[-2:])
    pad_mu = lambda a: _pad_in_cols(jnp.concatenate([jnp.zeros((1, CONV_COLS), F32), a], axis=1))[:, CONV_COLS:]
    unpad_mu = lambda a: _unpad_in_cols(jnp.concatenate([jnp.zeros((1, CONV_COLS), F32), a], axis=1))[:, CONV_COLS:]

    shards = {n: shard2d(wts[n]).astype(BF16 if n in _BF16_GATHER else F32) for n in _SHARDED}
    gathered = _exchange("gather_early", [shards[n] for n in _EARLY], [False] * len(_EARLY))
    w = {n: _unshard(n, g) for n, g in zip(_EARLY[1:], gathered[1:])}
    w["w_in"] = _assemble_w_in(gathered[0])
    w["w_lora_up"] = _pad_rows(w["w_lora_up"], LANE)
    w["a_lora_up"] = _pad_rows(w["a_lora_up"], LANE)
    w["g_lora_up"] = _pad_rows(w["g_lora_up"], 2 * LANE)
    for n in _REPLICATED:
        w[n] = wts[n].reshape(1, -1)
    w["shift_mu"] = pad_mu(wts["shift_mu"])

    loss, dx, grads, parts = _local_step(x[0], p[0, 0], loss_target[0], w, [shards[n] for n in _LATE])

    grads["shift_mu"] = unpad_mu(grads["shift_mu"])
    (small_parts,) = _exchange("gather_small", [_pack([grads[n] for n in _REPLICATED] + [loss])], [False])

    out = {}
    for n in _SHARDED:
        out[n] = _adamw("adamw_" + n, parts[n], wts[n], mom[n], var[n])
    sm = _adamw("adamw_small", small_parts, _pack([wts[n] for n in _REPLICATED]), _pack([mom[n] for n in _REPLICATED]),
                _pack([var[n] for n in _REPLICATED]))
    off = 0
    for n in _REPLICATED:
        size = wts[n].size
        out[n] = [r.reshape(-1)[off:off + size].reshape(wts[n].shape) for r in sm]
        off += size
    loss_total = sm[0].reshape(-1)[off]
    return (loss_total, dx[None], *[out[n][0] for n in _WEIGHTS], *[out[n][1] for n in _WEIGHTS],
            *[out[n][2] for n in _WEIGHTS], *[out[n][3] for n in _WEIGHTS])
```

```python
import functools

import jax
import jax.numpy as jnp
from jax import lax
from jax.experimental import pallas as pl
from jax.experimental.pallas import tpu as pltpu

F32 = jnp.float32
BF16 = jnp.bfloat16

N_DEV = 8
D_MODEL = 1024
CONV_DIM = 512
RWKV_DIM = 512
HEAD_DIM = 64
N_HEADS = 8
D_FF = 4096
PLE_DIM = 256
RMS_EPS = 1e-6
GN_EPS = 64e-5
L2_EPS = 1e-12
ADAM_LR, ADAM_B1, ADAM_B2, ADAM_EPS, ADAM_WD, ADAM_STEP = 0.001, 0.9, 0.999, 1e-08, 0.01, 10

CONV_COLS = 3 * CONV_DIM
RW_PAD = 2048
IN_PAD = CONV_COLS + RW_PAD
IN_COLS = 3360
XW_OFF, XA_OFF, XG_OFF = 1536, 1664, 1792
REC_CHUNK = 128
REC_PASSES = 1
ROW_BLOCK = 256
LANE = 128
VMEM_LIMIT = 56 * 1024 * 1024


def _dims(dn, ndim):
    if ndim == 3:
        return {"nn": (((2,), (1,)), ((0,), (0,))), "nt": (((2,), (2,)), ((0,), (0,))),
                "tn": (((1,), (1,)), ((0,), (0,)))}[dn]
    return {"nn": (((1,), (0,)), ((), ())), "nt": (((1,), (1,)), ((), ())), "tn": (((0,), (0,)), ((), ()))}[dn]


def _split2(x):
    hi = x.astype(BF16)
    return hi, (x - hi.astype(F32)).astype(BF16)


def _mm_raw(x, y, dn, passes):
    f = lambda p, q: lax.dot_general(p, q, _dims(dn, x.ndim), preferred_element_type=F32)
    if passes == 1:
        return f(x.astype(BF16), y.astype(BF16))
    xh, xl = _split2(x)
    yh, yl = _split2(y)
    return f(xh, yh) + f(xh, yl) + f(xl, yh)


@functools.partial(jax.custom_vjp, nondiff_argnums=(2, 3))
def _mm(x, y, dn, passes):
    return _mm_raw(x, y, dn, passes)


def _mm_fwd(x, y, dn, passes):
    return _mm_raw(x, y, dn, passes), (x, y)


def _mm_bwd(dn, passes, res, d):
    x, y = res
    if dn == "nn":
        return _mm(d, y, "nt", passes), _mm(x, d, "tn", passes)
    if dn == "nt":
        return _mm(d, y, "nn", passes), _mm(d, x, "tn", passes)
    return _mm(y, d, "nt", passes), _mm(x, d, "nn", passes)


_mm.defvjp(_mm_fwd, _mm_bwd)


def _head_ones():
    i = lax.broadcasted_iota(jnp.int32, (RWKV_DIM, RWKV_DIM), 0) // HEAD_DIM
    j = lax.broadcasted_iota(jnp.int32, (RWKV_DIM, RWKV_DIM), 1) // HEAD_DIM
    return (i == j).astype(BF16)


def _hsum_raw(x):
    ones = _head_ones()
    f = lambda p: lax.dot_general(p, ones, _dims("nn", 2), preferred_element_type=F32)
    x1 = x.astype(BF16)
    r1 = x - x1.astype(F32)
    x2 = r1.astype(BF16)
    x3 = (r1 - x2.astype(F32)).astype(BF16)
    return f(x1) + f(x2) + f(x3)


@jax.custom_vjp
def _hsum(x):
    return _hsum_raw(x)


_hsum.defvjp(lambda x: (_hsum_raw(x), None), lambda _, d: (_hsum(d),))


def _sigmoid(x):
    return 1.0 / (1.0 + jnp.exp(-x))


def _softplus(x):
    return jnp.maximum(x, 0.0) + jnp.log(1.0 + jnp.exp(-jnp.abs(x)))


def _params(sem):
    return pltpu.CompilerParams(dimension_semantics=sem, vmem_limit_bytes=VMEM_LIMIT)


def _rowwise(name, fn, rows, consts, row_outs, acc_outs=(), tr=ROW_BLOCK, halo=False):
    rows = [r if isinstance(r, tuple) else (r, r.shape[1], 0) for r in rows]
    t_len = rows[0][0].shape[0]
    tr = min(tr, t_len)
    n_r, n_c, n_o, n_a = len(rows), len(consts), len(row_outs), len(acc_outs)
    n_h = n_r if halo else 0
    sub = 8

    def body(*refs):
        ins = [r[...] for r in refs[:n_r]]
        ins += [jnp.where(pl.program_id(0) == 0, 0.0, r[sub - 1:sub, :]) for r in refs[n_r:n_r + n_h]]
        ins += [r[...] for r in refs[n_r + n_h:n_r + n_h + n_c]]
        refs = refs[:n_r] + refs[n_r + n_h:]
        outs = fn(*ins)
        o_refs = refs[n_r + n_c:n_r + n_c + n_o]
        a_refs = refs[n_r + n_c + n_o:]
        for o_ref, val in zip(o_refs, outs[:n_o]):
            o_ref[...] = val.astype(o_ref.dtype)
        if n_a:
            first = pl.program_id(0) == 0

            @pl.when(first)
            def _():
                for a_ref, val in zip(a_refs, outs[n_o:]):
                    a_ref[...] = val

            @pl.when(jnp.logical_not(first))
            def _():
                for a_ref, val in zip(a_refs, outs[n_o:]):
                    a_ref[...] += val

    in_specs = [pl.BlockSpec((tr, w), functools.partial(lambda i, c: (i, c), c=cb)) for _, w, cb in rows]
    if halo:
        in_specs += [pl.BlockSpec((sub, w), functools.partial(lambda i, c: (jnp.maximum(i * (tr // sub) - 1, 0), c), c=cb))
                     for _, w, cb in rows]
    in_specs += [pl.BlockSpec(c.shape, functools.partial(lambda i, n: (0,) * n, n=c.ndim)) for c in consts]
    out_specs = [pl.BlockSpec((tr, w), lambda i: (i, 0)) for w, _ in row_outs]
    out_specs += [pl.BlockSpec(s, functools.partial(lambda i, n: (0,) * n, n=len(s))) for s in acc_outs]
    out_shape = [jax.ShapeDtypeStruct((t_len, w), dt) for w, dt in row_outs]
    out_shape += [jax.ShapeDtypeStruct(s, F32) for s in acc_outs]
    return pl.pallas_call(
        body, name=name, grid=(t_len // tr,), in_specs=in_specs, out_specs=out_specs, out_shape=out_shape,
        compiler_params=_params(("arbitrary",)),
    )(*[r[0] for r in rows], *([r[0] for r in rows] if halo else []), *consts)


def _colwise(name, fn, n_blocks, cols, prms, col_outs, prm_outs=()):
    t_len = cols[0][0].shape[0]
    n_i = len(cols) + len(prms)

    def body(*refs):
        outs = fn(*[r[...] for r in refs[:n_i]])
        for o_ref, val in zip(refs[n_i:], outs):
            o_ref[...] = val.astype(o_ref.dtype)

    spec = lambda r, w: pl.BlockSpec((r, w), lambda j: (0, j))
    in_specs = [spec(t_len, w) for _, w in cols] + [spec(a.shape[0], LANE) for a in prms]
    out_specs = [spec(t_len, bw) for _, _, bw in col_outs] + [spec(r, LANE) for r, _ in prm_outs]
    out_shape = [jax.ShapeDtypeStruct((t_len, w), dt) for w, dt, _ in col_outs]
    out_shape += [jax.ShapeDtypeStruct((r, w), F32) for r, w in prm_outs]
    return pl.pallas_call(
        body, name=name, grid=(n_blocks,), in_specs=in_specs, out_specs=out_specs, out_shape=out_shape,
        compiler_params=_params(("arbitrary",)),
    )(*[c[0] for c in cols], *prms)


def _matmul(name, a, b, dn, outs, *, tm, tn, tk, extras=(), consts=(), epilogue=None, sums=(), xch=(), xch_scatter=()):
    if dn == "nn":
        (m, k), n = a.shape, b.shape[1]
    elif dn == "nt":
        (m, k), n = a.shape, b.shape[0]
    else:
        (k, m), n = a.shape, b.shape[1]
    tm, tn, tk = min(tm, m), min(tn, n), min(tk, k)
    nk = k // tk
    grid = (m // tm, n // tn, nk)
    assert not sums or (grid[1] == 1 and nk == 1)
    a_spec = pl.BlockSpec((tk, tm), lambda i, j, q: (q, i)) if dn == "tn" else pl.BlockSpec((tm, tk), lambda i, j, q: (i, q))
    b_spec = pl.BlockSpec((tn, tk), lambda i, j, q: (j, q)) if dn == "nt" else pl.BlockSpec((tk, tn), lambda i, j, q: (q, j))
    o_spec = pl.BlockSpec((tm, tn), lambda i, j, q: (i, j))
    c_spec = pl.BlockSpec((1, tn), lambda i, j, q: (0, j))
    n_e, n_c, n_o, n_s, n_x = len(extras), len(consts), len(outs), len(sums), len(xch)
    x_specs, x_shapes, x_sems = _exchange_io(xch, xch_scatter) if n_x else ([], [], [])

    def body(*refs):
        a_ref, b_ref = refs[:2]
        e_refs = refs[2:2 + n_e + n_c]
        x_in = refs[2 + n_e + n_c:2 + n_e + n_c + n_x]
        rest = refs[2 + n_e + n_c + n_x:]
        o_refs, s_refs, x_out, scratch = rest[:n_o], rest[n_o:n_o + n_s], rest[n_o + n_s:n_o + n_s + n_x], rest[n_o + n_s + n_x:]
        step = (pl.program_id(0) * grid[1] + pl.program_id(1)) * nk + pl.program_id(2)
        if n_x:
            start, wait = _exchange_plan(x_in, x_out, xch_scatter, *scratch[len(scratch) - 3:])
            pl.when(step == 0)(start)
        part = lax.dot_general(a_ref[...].astype(BF16), b_ref[...].astype(BF16), _dims(dn, 2), preferred_element_type=F32)

        def finish(acc):
            vals = (acc,) if epilogue is None else epilogue(acc, *[e[...] for e in e_refs])
            for o_ref, val in zip(o_refs, vals[:n_o]):
                o_ref[...] = val.astype(o_ref.dtype)
            if n_s:
                @pl.when(step == 0)
                def _():
                    for s_ref, val in zip(s_refs, vals[n_o:]):
                        s_ref[...] = val

                @pl.when(step > 0)
                def _():
                    for s_ref, val in zip(s_refs, vals[n_o:]):
                        s_ref[...] += val

        if nk == 1:
            finish(part)
        else:
            acc_ref = scratch[0]
            q = pl.program_id(2)

            @pl.when(q == 0)
            def _():
                acc_ref[...] = part

            @pl.when(q > 0)
            def _():
                acc_ref[...] += part

            @pl.when(q == nk - 1)
            def _():
                finish(acc_ref[...])

        if n_x:
            pl.when(step == grid[0] * grid[1] * nk - 1)(wait)

    plain = not (n_s or n_x)
    res = pl.pallas_call(
        body, name=name, grid=grid,
        in_specs=[a_spec, b_spec] + [o_spec] * n_e + [c_spec] * n_c + x_specs,
        out_specs=[o_spec] * n_o + [c_spec] * n_s + x_specs,
        out_shape=[jax.ShapeDtypeStruct((m, n), dt) for dt in outs] + [jax.ShapeDtypeStruct(s, F32) for s in sums] + x_shapes,
        scratch_shapes=([pltpu.VMEM((tm, tn), F32)] if nk > 1 else []) + x_sems,
        compiler_params=pltpu.CompilerParams(
            dimension_semantics=("parallel", "parallel", "arbitrary") if plain else ("arbitrary",) * 3,
            vmem_limit_bytes=VMEM_LIMIT, has_side_effects=bool(n_x)),
    )(a, b, *extras, *consts, *xch)
    return res[0] if len(res) == 1 else res


def _rms(h, g):
    return h * lax.rsqrt(jnp.mean(h * h, axis=-1, keepdims=True) + RMS_EPS) * g


def _rms_bwd(h, g, dy):
    rs = lax.rsqrt(jnp.mean(h * h, axis=-1, keepdims=True) + RMS_EPS)
    n = h * rs
    dn = dy * g
    dh = rs * (dn - n * jnp.mean(dn * n, axis=-1, keepdims=True))
    return dh, jnp.sum(dy * n, axis=0, keepdims=True)


def _rwkv_pre(k, xw, xa, xg, w0, a0, k_k, k_a, wl, al, gl):
    zw = w0 + _mm(jnp.tanh(xw), wl, "nn", 1)
    lw = -jnp.exp(-_softplus(-zw) - 0.5)
    iclr = _sigmoid(a0 + _mm(xa, al, "nn", 1))
    g = _mm(_sigmoid(xg), gl, "nn", 1)
    kk0 = k * k_k
    kk = kk0 / jnp.maximum(jnp.sqrt(_hsum(kk0 * kk0)), L2_EPS)
    k_h = k * (1.0 + (iclr - 1.0) * k_a)
    return lw, k_h, -kk, kk * iclr, g


def _rwkv_post(y, r, k_h, v, g, ln_g, ln_b, r_k):
    mu = _hsum(y) * (1.0 / HEAD_DIM)
    yc = y - mu
    var = _hsum(yc * yc) * (1.0 / HEAD_DIM)
    yo = yc * lax.rsqrt(var + GN_EPS) * ln_g + ln_b
    bonus = _hsum(r * k_h * r_k) * v
    return (yo + bonus) * g


def _shift_down(x, n):
    rows = lax.broadcasted_iota(jnp.int32, x.shape, 0)
    return jnp.where(rows < n, 0.0, pltpu.roll(x, n, 0))


def _shift_up(x, n):
    t_len = x.shape[0]
    rows = lax.broadcasted_iota(jnp.int32, x.shape, 0)
    return jnp.where(rows >= t_len - n, 0.0, pltpu.roll(x, t_len - n, 0))


def _exchange_plan(ins, outs, scatter, send_sems, recv_sems, local_sems):
    x, y, c = lax.axis_index("x"), lax.axis_index("y"), lax.axis_index("c")
    me = 4 * x + 2 * y + c

    def local(i):
        return pltpu.make_async_copy(ins[i].at[me] if scatter[i] else ins[i], outs[i].at[me], local_sems.at[i])

    def send(i, rel):
        return pltpu.make_async_remote_copy(
            src_ref=ins[i].at[me ^ rel] if scatter[i] else ins[i], dst_ref=outs[i].at[me],
            send_sem=send_sems.at[i, rel - 1], recv_sem=recv_sems.at[i, rel - 1],
            device_id=(x ^ (rel >> 2), y ^ ((rel >> 1) & 1), c ^ (rel & 1)), device_id_type=pl.DeviceIdType.MESH)

    def landed(i, rel):
        slot = outs[i].at[me ^ rel]
        return pltpu.make_async_remote_copy(
            src_ref=slot, dst_ref=slot, send_sem=send_sems.at[i, rel - 1], recv_sem=recv_sems.at[i, rel - 1],
            device_id=(x, y, c), device_id_type=pl.DeviceIdType.MESH)

    def start():
        for i in range(len(ins)):
            local(i).start()
            for rel in range(1, N_DEV):
                send(i, rel).start()

    def wait():
        for i in range(len(ins)):
            local(i).wait()
            for rel in range(1, N_DEV):
                landed(i, rel).wait_recv()
            for rel in range(1, N_DEV):
                send(i, rel).wait_send()

    return start, wait


def _gather_plan(ins, outs, send_sems, recv_sems, local_sems):
    x, y, c = lax.axis_index("x"), lax.axis_index("y"), lax.axis_index("c")
    me = 4 * x + 2 * y + c
    direct, chips = (1, 2, 4, 6), (2, 4, 6)

    def local(i):
        return pltpu.make_async_copy(ins[i], outs[i].at[me], local_sems.at[i])

    def send(i, rel):
        return pltpu.make_async_remote_copy(
            src_ref=ins[i], dst_ref=outs[i].at[me], send_sem=send_sems.at[i, rel - 1], recv_sem=recv_sems.at[i, rel - 1],
            device_id=(x ^ (rel >> 2), y ^ ((rel >> 1) & 1), c ^ (rel & 1)), device_id_type=pl.DeviceIdType.MESH)

    def passed(i, rel):
        slot = outs[i].at[me ^ rel]
        return pltpu.make_async_remote_copy(
            src_ref=slot, dst_ref=slot, send_sem=send_sems.at[i, rel], recv_sem=recv_sems.at[i, rel],
            device_id=(x, y, 1 - c), device_id_type=pl.DeviceIdType.MESH)

    def landed(i, rel):
        slot = outs[i].at[me ^ rel]
        return pltpu.make_async_remote_copy(
            src_ref=slot, dst_ref=slot, send_sem=send_sems.at[i, rel - 1], recv_sem=recv_sems.at[i, rel - 1],
            device_id=(x, y, c), device_id_type=pl.DeviceIdType.MESH)

    def start():
        for i in range(len(ins)):
            local(i).start()
            for rel in direct:
                send(i, rel).start()

    def forward(i):
        for rel in chips:
            landed(i, rel).wait_recv()
            passed(i, rel).start()

    def wait():
        for i in range(len(ins)):
            local(i).wait()
            for rel in (1, 3, 5, 7):
                landed(i, rel).wait_recv()
            for rel in direct:
                send(i, rel).wait_send()
            for rel in chips:
                passed(i, rel).wait_send()

    return start, forward, wait


def _exchange_io(arrays, scatter):
    n = len(arrays)
    any_spec = pl.BlockSpec(memory_space=pl.ANY)
    out_shape = [jax.ShapeDtypeStruct(a.shape if sc else (N_DEV,) + a.shape, a.dtype) for a, sc in zip(arrays, scatter)]
    sems = [pltpu.SemaphoreType.DMA((n, N_DEV - 1)), pltpu.SemaphoreType.DMA((n, N_DEV - 1)), pltpu.SemaphoreType.DMA((n,))]
    return [any_spec] * n, out_shape, sems


def _exchange(name, arrays, scatter):
    n = len(arrays)
    specs, out_shape, sems = _exchange_io(arrays, scatter)

    def body(*refs):
        if any(scatter):
            start, wait = _exchange_plan(refs[:n], refs[n:2 * n], scatter, *refs[2 * n:])
            start()
        else:
            start, forward, wait = _gather_plan(refs[:n], refs[n:2 * n], *refs[2 * n:])
            start()
            for i in range(n):
                forward(i)
        wait()

    return pl.pallas_call(
        body, name=name, in_specs=specs, out_specs=specs, out_shape=out_shape, scratch_shapes=sems,
        compiler_params=pltpu.CompilerParams(has_side_effects=True),
    )(*arrays)


def _chunk_fwd(z0, r, lw, k, v, a, b):
    n_h, c, n_k = r.shape
    mm = functools.partial(_mm, passes=REC_PASSES)
    gram = functools.partial(_mm, passes=3)
    ti = lax.broadcasted_iota(jnp.int32, (c, c), 0)
    si = lax.broadcasted_iota(jnp.int32, (c, c), 1)
    strict, incl = si < ti, si <= ti
    cum = _mm(jnp.broadcast_to(incl.astype(F32), (n_h, c, c)), lw, "nn", 3)
    cum_end = cum[:, c - 1:c, :]
    at = a * jnp.exp(cum - lw)
    bt = b * jnp.exp(-cum)
    kt = k * jnp.exp(-cum)
    rt = r * jnp.exp(cum)
    be = b * jnp.exp(cum_end - cum)
    ke = k * jnp.exp(cum_end - cum)
    lab = jnp.where(strict, gram(at, bt, "nt"), 0.0)
    lak = jnp.where(strict, gram(at, kt, "nt"), 0.0)
    u = mm(at, z0, "nn") + mm(lak, v, "nn")
    p = lab
    n = 1
    while n < c:
        u = u + mm(p, u, "nn")
        n *= 2
        if n < c:
            p = mm(p, p, "nn")
    rb = jnp.where(incl, gram(rt, bt, "nt"), 0.0)
    rk = jnp.where(incl, gram(rt, kt, "nt"), 0.0)
    y = mm(rt, z0, "nn") + mm(rb, u, "nn") + mm(rk, v, "nn")
    ki = lax.broadcasted_iota(jnp.int32, (n_k, n_k), 0)
    kj = lax.broadcasted_iota(jnp.int32, (n_k, n_k), 1)
    dmat = jnp.where(ki == kj, jnp.broadcast_to(jnp.exp(cum_end), (n_h, n_k, n_k)), 0.0)
    z_end = mm(dmat, z0, "nn") + mm(be, u, "tn") + mm(ke, v, "tn")
    return y, z_end


def _heads(x):
    return jnp.stack([x[:, h * HEAD_DIM:(h + 1) * HEAD_DIM] for h in range(N_HEADS)])


def _unheads(x):
    return jnp.concatenate([x[h] for h in range(N_HEADS)], axis=-1)


def _rec_params():
    return pltpu.CompilerParams(dimension_semantics=("arbitrary",), vmem_limit_bytes=VMEM_LIMIT, has_side_effects=True)


def _rec_fwd(u, lw, k, a, b, xch):
    t_len = lw.shape[0]
    c = min(REC_CHUNK, t_len)
    nc = t_len // c
    n_x = len(xch)
    x_specs, x_shapes, x_sems = _exchange_io(xch, [False] * n_x)
    sizes = [a_.size * a_.dtype.itemsize for a_ in xch]
    pass_step = [min(nc - 1, int(0.9 * nc * sum(sizes[:j + 1]) / sum(sizes)) + 1) for j in range(n_x)]

    def body(*refs):
        r_ref, v_ref, lw_ref, k_ref, a_ref, b_ref = refs[:6]
        x_in = refs[6:6 + n_x]
        y_ref, zs_ref = refs[6 + n_x:8 + n_x]
        x_out = refs[8 + n_x:8 + 2 * n_x]
        z_scr = refs[8 + 2 * n_x]
        start, forward, wait = _gather_plan(x_in, x_out, *refs[9 + 2 * n_x:])
        i = pl.program_id(0)

        @pl.when(i == 0)
        def _():
            start()
            z_scr[...] = jnp.zeros_like(z_scr)

        z0 = z_scr[...]
        zs_ref[0] = z0
        y, z_end = _chunk_fwd(z0, _heads(r_ref[...]), _heads(lw_ref[...]), _heads(k_ref[...]), _heads(v_ref[...]),
                              _heads(a_ref[...]), _heads(b_ref[...]))
        y_ref[...] = _unheads(y)
        z_scr[...] = z_end

        for j in range(n_x):
            pl.when(i == pass_step[j])(functools.partial(forward, j))

        @pl.when(i == nc - 1)
        def _():
            wait()

    blk = lambda cb: pl.BlockSpec((c, RWKV_DIM), functools.partial(lambda i, q: (i, q), q=cb))
    res = pl.pallas_call(
        body, name="rwkv_rec_fwd", grid=(nc,),
        in_specs=[blk(0), blk(2)] + [blk(0)] * 4 + x_specs,
        out_specs=[blk(0), pl.BlockSpec((1, N_HEADS, HEAD_DIM, HEAD_DIM), lambda i: (i, 0, 0, 0))] + x_specs,
        out_shape=[jax.ShapeDtypeStruct((t_len, RWKV_DIM), F32),
                   jax.ShapeDtypeStruct((nc, N_HEADS, HEAD_DIM, HEAD_DIM), F32)] + x_shapes,
        scratch_shapes=[pltpu.VMEM((N_HEADS, HEAD_DIM, HEAD_DIM), F32)] + x_sems,
        compiler_params=_rec_params(),
    )(u, u, lw, k, a, b, *xch)
    return res[0], res[1], res[2:]


def _rec_bwd(u, lw, k, a, b, zs, dy, xch, xch_scatter):
    t_len = lw.shape[0]
    c = min(REC_CHUNK, t_len)
    nc = t_len // c
    n_x = len(xch)
    x_specs, x_shapes, x_sems = _exchange_io(xch, xch_scatter)

    def body(*refs):
        r_ref, v_ref, lw_ref, k_ref, a_ref, b_ref, zs_ref, dy_ref = refs[:8]
        x_in = refs[8:8 + n_x]
        g_refs = refs[8 + n_x:14 + n_x]
        x_out = refs[14 + n_x:14 + 2 * n_x]
        dz_scr = refs[14 + 2 * n_x]
        start, wait = _exchange_plan(x_in, x_out, xch_scatter, *refs[15 + 2 * n_x:])
        i = pl.program_id(0)

        @pl.when(i == 0)
        def _():
            start()
            dz_scr[...] = jnp.zeros_like(dz_scr)

        _, vjp = jax.vjp(_chunk_fwd, zs_ref[0], _heads(r_ref[...]), _heads(lw_ref[...]), _heads(k_ref[...]),
                         _heads(v_ref[...]), _heads(a_ref[...]), _heads(b_ref[...]))
        dz0, dr, dlw, dk, dv, da, db = vjp((_heads(dy_ref[...]), dz_scr[...]))
        for ref, val in zip(g_refs, (dr, dv, dlw, dk, da, db)):
            ref[...] = _unheads(val)
        dz_scr[...] = dz0

        @pl.when(i == nc - 1)
        def _():
            wait()

    blk = lambda cb: pl.BlockSpec((c, RWKV_DIM), functools.partial(lambda i, q: (nc - 1 - i, q), q=cb))
    res = pl.pallas_call(
        body, name="rwkv_rec_bwd", grid=(nc,),
        in_specs=[blk(0), blk(2)] + [blk(0)] * 4
                 + [pl.BlockSpec((1, N_HEADS, HEAD_DIM, HEAD_DIM), lambda i: (nc - 1 - i, 0, 0, 0)), blk(0)] + x_specs,
        out_specs=[blk(0)] * 6 + x_specs,
        out_shape=[jax.ShapeDtypeStruct((t_len, RWKV_DIM), F32)] * 6 + x_shapes,
        scratch_shapes=[pltpu.VMEM((N_HEADS, HEAD_DIM, HEAD_DIM), F32)] + x_sems,
        compiler_params=_rec_params(),
    )(u, u, lw, k, a, b, zs, dy, *xch)
    return res[:6], res[6:]


_EARLY = ["w_in", "conv_w", "w_lora_up", "a_lora_up", "g_lora_up"]
_LATE = ["w_out", "w_up", "w_down", "w_ple_gate", "w_ple_proj"]
_SHARDED = _EARLY + _LATE
_COL_SHARDED = {"w_in", "conv_w", "w_lora_up", "a_lora_up", "g_lora_up", "w_up", "w_ple_proj"}
_BF16_GATHER = {"w_in", "w_out", "w_up", "w_down", "w_ple_gate", "w_ple_proj"}
_REPLICATED = ["norm_mix_g", "shift_mu", "w0", "a0", "k_k", "k_a", "r_k", "ln_x_g", "ln_x_b", "norm_mlp_g", "norm_ple_g",
               "norm_final_g"]
_WEIGHTS = ["norm_mix_g", "w_in", "conv_w", "shift_mu", "w_lora_up", "w0", "a_lora_up", "a0", "g_lora_up", "k_k", "k_a", "r_k",
            "ln_x_g", "ln_x_b", "w_out", "norm_mlp_g", "w_up", "w_down", "norm_ple_g", "w_ple_gate", "w_ple_proj", "norm_final_g"]
_PACK_ROWS = 80


def _unshard(name, g):
    if name in _COL_SHARDED:
        return jnp.moveaxis(g, 0, 1).reshape(g.shape[1], N_DEV * g.shape[2])
    return g.reshape(N_DEV * g.shape[1], g.shape[2])


def _reshard(name, full):
    if name in _COL_SHARDED:
        return jnp.moveaxis(full.reshape(full.shape[0], N_DEV, full.shape[1] // N_DEV), 1, 0)
    return full.reshape(N_DEV, full.shape[0] // N_DEV, full.shape[1])


def _pad_in_cols(a):
    z = lambda n: jnp.zeros(a.shape[:-1] + (n,), a.dtype)
    conv = [a[..., part * CONV_DIM + j * LANE:part * CONV_DIM + (j + 1) * LANE] for j in range(CONV_DIM // LANE) for part in range(3)]
    return jnp.concatenate(conv + [a[..., CONV_COLS:3136], z(64), a[..., 3136:3200], z(64), a[..., 3200:3360], z(96)], axis=-1)


def _unpad_in_cols(a):
    conv = [a[..., (3 * j + part) * LANE:(3 * j + part + 1) * LANE] for part in range(3) for j in range(CONV_DIM // LANE)]
    return jnp.concatenate(conv + [a[..., CONV_COLS:3136], a[..., 3200:3264], a[..., 3328:3488]], axis=-1)


def _assemble_w_in(g):
    n_dev, rows, cols = g.shape

    def body(g_ref, o_ref):
        o_ref[...] = _pad_in_cols(jnp.concatenate([g_ref[d] for d in range(n_dev)], axis=1))

    return pl.pallas_call(
        body, name="w_in_assemble", grid=(rows // ROW_BLOCK,),
        in_specs=[pl.BlockSpec((n_dev, ROW_BLOCK, cols), lambda i: (0, i, 0))],
        out_specs=pl.BlockSpec((ROW_BLOCK, IN_PAD), lambda i: (i, 0)),
        out_shape=jax.ShapeDtypeStruct((rows, IN_PAD), g.dtype), compiler_params=_params(("arbitrary",)),
    )(g)


def _split_w_in_grad(dw):
    rows = dw.shape[0]
    cols = IN_COLS // N_DEV

    def body(d_ref, o_ref):
        full = _unpad_in_cols(d_ref[...])
        for d in range(N_DEV):
            o_ref[d] = full[:, cols * d:cols * (d + 1)]

    return pl.pallas_call(
        body, name="w_in_grad_split", grid=(rows // ROW_BLOCK,),
        in_specs=[pl.BlockSpec((ROW_BLOCK, IN_PAD), lambda i: (i, 0))],
        out_specs=pl.BlockSpec((N_DEV, ROW_BLOCK, cols), lambda i: (0, i, 0)),
        out_shape=jax.ShapeDtypeStruct((N_DEV, rows, cols), dw.dtype), compiler_params=_params(("arbitrary",)),
    )(dw)


def _pad_rows(a, rows):
    return jnp.concatenate([a, jnp.zeros((rows - a.shape[0],) + a.shape[1:], a.dtype)], axis=0)


def _pack(vals):
    flat = jnp.concatenate([v.reshape(-1) for v in vals])
    return jnp.concatenate([flat, jnp.zeros((_PACK_ROWS * LANE - flat.shape[0],), F32)]).reshape(_PACK_ROWS, LANE)


SEG_W = [RWKV_DIM, RWKV_DIM, RWKV_DIM, LANE, LANE, 2 * LANE]
SEG_OFF = [0, 512, 1024, XW_OFF, XA_OFF, XG_OFF]


def _rwkv_pre_bwd(proj, u, grads, mu, small, dproj):
    t_len = u.shape[0]
    tr = min(ROW_BLOCK, t_len)
    nb = t_len // tr
    sub = 8
    n_g = len(grads)
    acc_shapes = [(1, RW_PAD)] + [(1, RWKV_DIM)] * 4 + [(LANE, RWKV_DIM), (LANE, RWKV_DIM), (2 * LANE, RWKV_DIM)]

    def body(*refs):
        seg_refs, halo_refs = refs[:6], refs[6:12]
        k_ref, xw_ref, xa_ref, xg_ref = refs[12:16]
        g_refs = refs[16:16 + n_g]
        mu_ref = refs[16 + n_g]
        prm_refs = refs[17 + n_g:24 + n_g]
        out_hbm = refs[25 + n_g]
        acc_refs = refs[26 + n_g:26 + n_g + len(acc_shapes)]
        vbuf, sems, carry = refs[26 + n_g + len(acc_shapes):]
        i = pl.program_id(0)
        blk = nb - 1 - i
        dr1, dr2, dv1, dv2, dlw, dk1, dk2, da, db, dg = [g[...] for g in g_refs]
        _, vjp = jax.vjp(_rwkv_pre, k_ref[...], xw_ref[...], xa_ref[...], xg_ref[...], *[p_[...] for p_ in prm_refs])
        dk, dxw, dxa, dxg, *dprm = vjp((dlw, dk1 + dk2, da, db, dg))
        du = jnp.concatenate([dr1 + dr2, dk, dv1 + dv2, dxw, dxa, dxg], axis=1)
        mu_v = mu_ref[...]

        @pl.when(i == 0)
        def _():
            carry[...] = jnp.zeros_like(carry)

        rows = lax.broadcasted_iota(jnp.int32, du.shape, 0)
        nxt = jnp.where(rows == tr - 1, carry[...], pltpu.roll(du, tr - 1, 0))
        d_rw = du - mu_v * du + mu_v * nxt
        d_mu = []
        for s_ref, h_ref, off, wd in zip(seg_refs, halo_refs, SEG_OFF, SEG_W):
            cur = s_ref[...]
            r0 = lax.broadcasted_iota(jnp.int32, cur.shape, 0)
            prev = jnp.where(r0 == 0, jnp.where(blk == 0, 0.0, h_ref[sub - 1:sub, :]), pltpu.roll(cur, 1, 0))
            d_mu.append(jnp.sum(du[:, off:off + wd] * (prev - cur), axis=0, keepdims=True))
        sums = [jnp.concatenate(d_mu, axis=1)] + list(dprm)

        @pl.when(i == 0)
        def _():
            for a_ref, val in zip(acc_refs, sums):
                a_ref[...] = val

        @pl.when(i > 0)
        def _():
            for a_ref, val in zip(acc_refs, sums):
                a_ref[...] += val

        carry[...] = du[0:1, :]
        slot = i % 2

        def writeback(s, b):
            return pltpu.make_async_copy(vbuf.at[s], out_hbm.at[pl.ds(b * tr, tr), pl.ds(CONV_COLS, RW_PAD)], sems.at[s])

        @pl.when(i >= 2)
        def _():
            writeback(slot, blk + 2).wait()

        vbuf[slot] = d_rw.astype(vbuf.dtype)
        writeback(slot, blk).start()

        @pl.when(i == nb - 1)
        def _():
            writeback(slot, blk).wait()
            if nb > 1:
                writeback(1 - slot, blk + 1).wait()

    rev = lambda w_, cb: pl.BlockSpec((tr, w_), functools.partial(lambda i, c: (nb - 1 - i, c), c=cb))
    halo = lambda w_, cb: pl.BlockSpec((sub, w_), functools.partial(
        lambda i, c: (jnp.maximum((nb - 1 - i) * (tr // sub) - 1, 0), c), c=cb))
    whole = lambda a: pl.BlockSpec(a.shape, functools.partial(lambda i, n: (0,) * n, n=a.ndim))
    segs = [(wd, (CONV_COLS + off) // wd) for off, wd in zip(SEG_OFF, SEG_W)]
    u_cols = [(512, 1), (LANE, XW_OFF // LANE), (LANE, XA_OFF // LANE), (2 * LANE, XG_OFF // (2 * LANE))]
    any_spec = pl.BlockSpec(memory_space=pl.ANY)
    res = pl.pallas_call(
        body, name="rwkv_pre_bwd", grid=(nb,),
        in_specs=[rev(*s) for s in segs] + [halo(*s) for s in segs] + [rev(*c) for c in u_cols]
                 + [rev(RWKV_DIM, 0)] * n_g + [whole(mu)] + [whole(p_) for p_ in small] + [any_spec],
        out_specs=[any_spec] + [pl.BlockSpec(s, functools.partial(lambda i, n: (0,) * n, n=len(s))) for s in acc_shapes],
        out_shape=[jax.ShapeDtypeStruct(dproj.shape, dproj.dtype)] + [jax.ShapeDtypeStruct(s, F32) for s in acc_shapes],
        scratch_shapes=[pltpu.VMEM((2, tr, RW_PAD), dproj.dtype), pltpu.SemaphoreType.DMA((2,)), pltpu.VMEM((1, RW_PAD), F32)],
        input_output_aliases={24 + n_g: 0},
        compiler_params=_params(("arbitrary",)),
    )(*[proj] * 12, *[u] * 4, *grads, mu, *small, dproj)
    return res


def _local_step(x, p, tgt, w, late_shards):
    row = lambda v: v.reshape(1, -1)
    w = dict(w)

    (xn1,) = _rowwise("rms_mix", lambda h, g: (_rms(h, g),), [x], [w["norm_mix_g"]], [(D_MODEL, BF16)])
    proj = _matmul("in_proj", xn1, w["w_in"], "nn", [F32], tm=1024, tn=512, tk=D_MODEL)
    n_cb = CONV_DIM // LANE

    def conv_fwd(blk, cw):
        gb, gc, hx = blk[:, :LANE], blk[:, LANE:2 * LANE], blk[:, 2 * LANE:]
        uu = gc * hx
        return (gb * (uu * cw[2:3] + _shift_down(uu, 1) * cw[1:2] + _shift_down(uu, 2) * cw[0:1]),)

    (y_conv,) = _colwise("conv_fwd", conv_fwd, n_cb, [(proj, 3 * LANE)], [w["conv_w"]], [(CONV_DIM, BF16, LANE)])

    small = [w["w0"], w["a0"], w["k_k"], w["k_a"], w["w_lora_up"], w["a_lora_up"], w["g_lora_up"]]
    def pre_fwd(*xs):
        cur, prev_rows, mu, prm = xs[:6], xs[6:12], xs[12], xs[13:]
        segs = []
        for c_, p_, off, wd in zip(cur, prev_rows, SEG_OFF, SEG_W):
            rows = lax.broadcasted_iota(jnp.int32, c_.shape, 0)
            prev = jnp.where(rows == 0, p_, pltpu.roll(c_, 1, 0))
            segs.append(c_ + mu[:, off:off + wd] * (prev - c_))
        return (jnp.concatenate(segs, axis=1),) + tuple(_rwkv_pre(segs[1], segs[3], segs[4], segs[5], *prm))

    proj_segs = [(proj, wd, (CONV_COLS + off) // wd) for off, wd in zip(SEG_OFF, SEG_W)]
    u, lw, k_h, ra, rb, g = _rowwise("rwkv_pre", pre_fwd, proj_segs, [w["shift_mu"]] + small,
                                     [(RW_PAD, F32)] + [(RWKV_DIM, F32)] * 5, halo=True)
    u_k, u_xw, u_xa, u_xg = (u, 512, 1), (u, LANE, XW_OFF // LANE), (u, LANE, XA_OFF // LANE), (u, 2 * LANE, XG_OFF // (2 * LANE))
    y_rec, zs, late = _rec_fwd(u, lw, k_h, ra, rb, late_shards)
    for n, gathered in zip(_LATE, late):
        w[n] = _unshard(n, gathered)
    post_c = [w["ln_x_g"], w["ln_x_b"], w["r_k"]]
    u_r, u_v = (u, 512, 0), (u, 512, 2)
    (y_rwkv,) = _rowwise("rwkv_post", lambda *xs: (_rwkv_post(*xs),), [y_rec, u_r, k_h, u_v, g], post_c, [(RWKV_DIM, BF16)])
    ycat = jnp.concatenate([y_conv, y_rwkv], axis=1)
    def res_norm(acc, r_, g_):
        h = acc + r_
        return h, _rms(h, g_)

    h1, xn2 = _matmul("out_proj", ycat, w["w_out"], "nn", [F32, BF16], tm=1024, tn=D_MODEL, tk=D_MODEL, extras=[x],
                      consts=[w["norm_mlp_g"]], epilogue=res_norm)

    def relu2(acc):
        hid = jnp.maximum(acc, 0.0)
        return hid, hid * hid

    hid, hsq = _matmul("mlp_up", xn2, w["w_up"], "nn", [BF16, BF16], tm=1024, tn=1024, tk=D_MODEL, epilogue=relu2)
    h2, xn3 = _matmul("mlp_down", hsq, w["w_down"], "nn", [F32, BF16], tm=512, tn=D_MODEL, tk=D_FF, extras=[h1],
                      consts=[w["norm_ple_g"]], epilogue=res_norm)
    zg = _matmul("ple_gate", xn3, w["w_ple_gate"], "nn", [F32], tm=1024, tn=1024, tk=D_MODEL)
    pp = _matmul("ple_proj", p, w["w_ple_proj"], "nn", [F32], tm=1024, tn=1024, tk=PLE_DIM)

    def head(h2_, zg_, pp_, tg, gf):
        gate = _sigmoid(zg_)
        h3 = h2_ + gate * pp_
        out = _rms(h3, gf)
        err = out - tg
        dh3, dgf = _rms_bwd(h3, gf, err * (1.0 / D_MODEL))
        loss = jnp.sum(jnp.sum(err * err, axis=1, keepdims=True), axis=0, keepdims=True) * (0.5 / D_MODEL)
        return dh3, dh3 * pp_ * gate * (1.0 - gate), dh3 * gate, dgf, loss

    dh3, dzg, dpp, d_norm_final, loss = _rowwise(
        "head", head, [h2, zg, pp, tgt], [row(w["norm_final_g"])], [(D_MODEL, F32), (D_MODEL, BF16), (D_MODEL, BF16)],
        [(1, D_MODEL), (1, 1)])

    d_w_ple_proj = _matmul("d_ple_proj", p, dpp, "tn", [BF16], tm=PLE_DIM, tn=1024, tk=4096)
    d_w_ple_gate = _matmul("d_ple_gate", xn3, dzg, "tn", [BF16], tm=512, tn=1024, tk=4096)

    def norm_bwd(dxn, h, dres, g_):
        dh, dg = _rms_bwd(h, g_, dxn)
        dh = dh + dres
        return dh, dh, dg

    nb = dict(tm=512, tn=D_MODEL, epilogue=norm_bwd, sums=[(1, D_MODEL)])
    dh2, dh2_b, d_norm_ple = _matmul("dx_ple_gate", dzg, w["w_ple_gate"], "nt", [F32, BF16], tk=D_MODEL,
                                     extras=[h2, dh3], consts=[w["norm_ple_g"]], **nb)
    d_w_down = _matmul("d_mlp_down", hsq, dh2_b, "tn", [BF16], tm=512, tn=1024, tk=4096)
    dpre = _matmul("dx_mlp_down", dh2_b, w["w_down"], "nt", [BF16], tm=1024, tn=1024, tk=D_MODEL, extras=[hid],
                   epilogue=lambda acc, hid_: (acc * (2.0 * hid_.astype(F32)),))
    d_w_up = _matmul("d_mlp_up", xn2, dpre, "tn", [BF16], tm=512, tn=1024, tk=4096)
    dh1, dh1_b, d_norm_mlp = _matmul("dx_mlp_up", dpre, w["w_up"], "nt", [F32, BF16], tk=D_FF,
                                     extras=[h1, dh2], consts=[w["norm_mlp_g"]], **nb)
    d_w_out = _matmul("d_out_proj", ycat, dh1_b, "tn", [BF16], tm=512, tn=1024, tk=4096)
    dycat = _matmul("dx_out_proj", dh1_b, w["w_out"], "nt", [F32], tm=1024, tn=1024, tk=D_MODEL)
    late_grads = dict(w_out=d_w_out, w_up=d_w_up, w_down=d_w_down, w_ple_gate=d_w_ple_gate, w_ple_proj=d_w_ple_proj)

    def conv_bwd(dy, blk, cw):
        gb, gc, hx = blk[:, :LANE], blk[:, LANE:2 * LANE], blk[:, 2 * LANE:]
        uu = gc * hx
        u1, u2 = _shift_down(uu, 1), _shift_down(uu, 2)
        dconv = dy * gb
        du = dconv * cw[2:3] + _shift_up(dconv, 1) * cw[1:2] + _shift_up(dconv, 2) * cw[0:1]
        s = lambda z: jnp.sum(z, axis=0, keepdims=True)
        d_blk = jnp.concatenate([dy * (uu * cw[2:3] + u1 * cw[1:2] + u2 * cw[0:1]), du * hx, du * gc], axis=1)
        return d_blk, s(dconv * u2), s(dconv * u1), s(dconv * uu)

    dproj, dcw0, dcw1, dcw2 = _colwise(
        "conv_bwd", conv_bwd, n_cb, [(dycat, LANE), (proj, 3 * LANE)], [w["conv_w"]],
        [(IN_PAD, BF16, 3 * LANE)], [(1, CONV_DIM)] * 3)

    def post_bwd(dy, y, r, k_h_, v, g_, ln_g, ln_b, r_k):
        _, vjp = jax.vjp(_rwkv_post, y, r, k_h_, v, g_, ln_g, ln_b, r_k)
        return vjp(dy)

    dy_rec, dr_p, dk_p, dv_p, dg, d_ln_g, d_ln_b, d_r_k = _rowwise(
        "rwkv_post_bwd", post_bwd, [(dycat, 512, 1), y_rec, u_r, k_h, u_v, g], post_c,
        [(RWKV_DIM, F32)] * 5, [(1, RWKV_DIM)] * 3)
    (dr_r, dv_r, dlw, dk_r, da, db), late_parts = _rec_bwd(
        u, lw, k_h, ra, rb, zs, dy_rec, [_reshard(n, late_grads[n]) for n in _LATE], [True] * len(_LATE))

    dproj, d_mu, d_w0, d_a0, d_k_k, d_k_a, d_wl, d_al, d_gl = _rwkv_pre_bwd(
        proj, u, [dr_p, dr_r, dv_p, dv_r, dlw, dk_p, dk_r, da, db, dg], w["shift_mu"], small, dproj)
    d_w_in = _matmul("d_in_proj", xn1, dproj, "tn", [BF16], tm=512, tn=896, tk=4096)
    early_grads = dict(conv_w=jnp.concatenate([dcw0, dcw1, dcw2], axis=0),
                       w_lora_up=d_wl[:64], a_lora_up=d_al[:64], g_lora_up=d_gl[:160])
    early_send = [_split_w_in_grad(d_w_in)] + [_reshard(n, early_grads[n]) for n in _EARLY[1:]]
    dx, d_norm_mix, *early_parts = _matmul(
        "dx_in_proj", dproj, w["w_in"], "nt", [F32], tk=IN_PAD, extras=[x, dh1], consts=[w["norm_mix_g"]],
        xch=early_send, xch_scatter=[True] * len(_EARLY), **dict(nb, epilogue=lambda *a: norm_bwd(*a)[1:]))

    grads = dict(
        norm_mix_g=d_norm_mix, shift_mu=d_mu, w0=d_w0, a0=d_a0, k_k=d_k_k, k_a=d_k_a, r_k=d_r_k,
        ln_x_g=d_ln_g, ln_x_b=d_ln_b, norm_mlp_g=d_norm_mlp, norm_ple_g=d_norm_ple, norm_final_g=d_norm_final)
    parts = dict(zip(_EARLY, early_parts))
    parts.update(zip(_LATE, late_parts))
    return loss, dx, grads, parts


def _adamw(name, parts, w, m, v):
    rows, cols = w.shape[-2:]
    lead = w.ndim - 2
    tr = rows if rows * cols * 4 * 8 <= (4 << 20) else max(8, (4 << 20) // (cols * 4 * 8) // 8 * 8)
    while rows % tr:
        tr -= 8

    def body(p_ref, w_ref, m_ref, v_ref, g_ref, d_ref, nm_ref, nv_ref):
        g = p_ref[0].astype(F32)
        for s in range(1, N_DEV):
            g = g + p_ref[s].astype(F32)
        nm = ADAM_B1 * m_ref[...] + (1.0 - ADAM_B1) * g
        nv = ADAM_B2 * v_ref[...] + (1.0 - ADAM_B2) * (g * g)
        m_hat = nm / (1.0 - ADAM_B1 ** ADAM_STEP)
        v_hat = nv / (1.0 - ADAM_B2 ** ADAM_STEP)
        g_ref[...] = g
        d_ref[...] = -ADAM_LR * (m_hat / (jnp.sqrt(v_hat) + ADAM_EPS) + ADAM_WD * w_ref[...])
        nm_ref[...] = nm
        nv_ref[...] = nv

    blk = pl.BlockSpec((None,) * lead + (tr, cols), lambda i: (0,) * lead + (i, 0))
    return pl.pallas_call(
        body, name=name, grid=(rows // tr,),
        in_specs=[pl.BlockSpec((N_DEV, tr, cols), lambda i: (0, i, 0)), blk, blk, blk], out_specs=[blk] * 4,
        out_shape=[jax.ShapeDtypeStruct(w.shape, F32)] * 4,
        compiler_params=_params(("arbitrary",)),
    )(parts, w, m, v)


def kernel(x, p, norm_mix_g, w_in, conv_w, shift_mu, w_lora_up, w0, a_lora_up, a0, g_lora_up, k_k, k_a, r_k, ln_x_g, ln_x_b, w_out, norm_mlp_g, w_up, w_down, norm_ple_g, w_ple_gate, w_ple_proj, norm_final_g, loss_target, m_norm_mix_g, m_w_in, m_conv_w, m_shift_mu, m_w_lora_up, m_w0, m_a_lora_up, m_a0, m_g_lora_up, m_k_k, m_k_a, m_r_k, m_ln_x_g, m_ln_x_b, m_w_out, m_norm_mlp_g, m_w_up, m_w_down, m_norm_ple_g, m_w_ple_gate, m_w_ple_proj, m_norm_final_g, v_norm_mix_g, v_w_in, v_conv_w, v_shift_mu, v_w_lora_up, v_w0, v_a_lora_up, v_a0, v_g_lora_up, v_k_k, v_k_a, v_r_k, v_ln_x_g, v_ln_x_b, v_w_out, v_norm_mlp_g, v_w_up, v_w_down, v_norm_ple_g, v_w_ple_gate, v_w_ple_proj, v_norm_final_g):
    args = dict(locals())
    wts = {n: args[n] for n in _WEIGHTS}
    mom = {n: args["m_" + n] for n in _WEIGHTS}
    var = {n: args["v_" + n] for n in _WEIGHTS}
    shard2d = lambda a: a.reshape(a.shape[-2:])
    pad_mu = lambda a: _pad_in_cols(jnp.concatenate([jnp.zeros((1, CONV_COLS), F32), a], axis=1))[:, CONV_COLS:]
    unpad_mu = lambda a: _unpad_in_cols(jnp.concatenate([jnp.zeros((1, CONV_COLS), F32), a], axis=1))[:, CONV_COLS:]

    shards = {n: shard2d(wts[n]).astype(BF16 if n in _BF16_GATHER else F32) for n in _SHARDED}
    gathered = _exchange("gather_early", [shards[n] for n in _EARLY], [False] * len(_EARLY))
    w = {n: _unshard(n, g) for n, g in zip(_EARLY[1:], gathered[1:])}
    w["w_in"] = _assemble_w_in(gathered[0])
    w["w_lora_up"] = _pad_rows(w["w_lora_up"], LANE)
    w["a_lora_up"] = _pad_rows(w["a_lora_up"], LANE)
    w["g_lora_up"] = _pad_rows(w["g_lora_up"], 2 * LANE)
    for n in _REPLICATED:
        w[n] = wts[n].reshape(1, -1)
    w["shift_mu"] = pad_mu(wts["shift_mu"])

    loss, dx, grads, parts = _local_step(x[0], p[0, 0], loss_target[0], w, [shards[n] for n in _LATE])

    grads["shift_mu"] = unpad_mu(grads["shift_mu"])
    (small_parts,) = _exchange("gather_small", [_pack([grads[n] for n in _REPLICATED] + [loss])], [False])

    out = {}
    for n in _SHARDED:
        out[n] = _adamw("adamw_" + n, parts[n], wts[n], mom[n], var[n])
    sm = _adamw("adamw_small", small_parts, _pack([wts[n] for n in _REPLICATED]), _pack([mom[n] for n in _REPLICATED]),
                _pack([var[n] for n in _REPLICATED]))
    off = 0
    for n in _REPLICATED:
        size = wts[n].size
        out[n] = [r.reshape(-1)[off:off + size].reshape(wts[n].shape) for r in sm]
        off += size
    loss_total = sm[0].reshape(-1)[off]
    return (loss_total, dx[None], *[out[n][0] for n in _WEIGHTS], *[out[n][1] for n in _WEIGHTS],
            *[out[n][2] for n in _WEIGHTS], *[out[n][3] for n in _WEIGHTS])
```

```python
import functools

import jax
import jax.numpy as jnp
from jax import lax
from jax.experimental import pallas as pl
from jax.experimental.pallas import tpu as pltpu

F32 = jnp.float32
BF16 = jnp.bfloat16

N_DEV = 8
D_MODEL = 1024
CONV_DIM = 512
RWKV_DIM = 512
HEAD_DIM = 64
N_HEADS = 8
D_FF = 4096
PLE_DIM = 256
RMS_EPS = 1e-6
GN_EPS = 64e-5
L2_EPS = 1e-12
ADAM_LR, ADAM_B1, ADAM_B2, ADAM_EPS, ADAM_WD, ADAM_STEP = 0.001, 0.9, 0.999, 1e-08, 0.01, 10

CONV_COLS = 3 * CONV_DIM
RW_PAD = 2048
IN_PAD = CONV_COLS + RW_PAD
IN_COLS = 3360
XW_OFF, XA_OFF, XG_OFF = 1536, 1664, 1792
REC_CHUNK = 128
REC_PASSES = 1
ROW_BLOCK = 256
LANE = 128
VMEM_LIMIT = 56 * 1024 * 1024


def _dims(dn, ndim):
    if ndim == 3:
        return {"nn": (((2,), (1,)), ((0,), (0,))), "nt": (((2,), (2,)), ((0,), (0,))),
                "tn": (((1,), (1,)), ((0,), (0,)))}[dn]
    return {"nn": (((1,), (0,)), ((), ())), "nt": (((1,), (1,)), ((), ())), "tn": (((0,), (0,)), ((), ()))}[dn]


def _split2(x):
    hi = x.astype(BF16)
    return hi, (x - hi.astype(F32)).astype(BF16)


def _mm_raw(x, y, dn, passes):
    f = lambda p, q: lax.dot_general(p, q, _dims(dn, x.ndim), preferred_element_type=F32)
    if passes == 1:
        return f(x.astype(BF16), y.astype(BF16))
    xh, xl = _split2(x)
    yh, yl = _split2(y)
    return f(xh, yh) + f(xh, yl) + f(xl, yh)


@functools.partial(jax.custom_vjp, nondiff_argnums=(2, 3))
def _mm(x, y, dn, passes):
    return _mm_raw(x, y, dn, passes)


def _mm_fwd(x, y, dn, passes):
    return _mm_raw(x, y, dn, passes), (x, y)


def _mm_bwd(dn, passes, res, d):
    x, y = res
    if dn == "nn":
        return _mm(d, y, "nt", passes), _mm(x, d, "tn", passes)
    if dn == "nt":
        return _mm(d, y, "nn", passes), _mm(d, x, "tn", passes)
    return _mm(y, d, "nt", passes), _mm(x, d, "nn", passes)


_mm.defvjp(_mm_fwd, _mm_bwd)


def _head_ones():
    i = lax.broadcasted_iota(jnp.int32, (RWKV_DIM, RWKV_DIM), 0) // HEAD_DIM
    j = lax.broadcasted_iota(jnp.int32, (RWKV_DIM, RWKV_DIM), 1) // HEAD_DIM
    return (i == j).astype(BF16)


def _hsum_raw(x):
    ones = _head_ones()
    f = lambda p: lax.dot_general(p, ones, _dims("nn", 2), preferred_element_type=F32)
    x1, x2 = _split2(x)
    return f(x1) + f(x2)


@jax.custom_vjp
def _hsum(x):
    return _hsum_raw(x)


_hsum.defvjp(lambda x: (_hsum_raw(x), None), lambda _, d: (_hsum(d),))


def _sigmoid(x):
    return 1.0 / (1.0 + jnp.exp(-x))


def _softplus(x):
    return jnp.maximum(x, 0.0) + jnp.log(1.0 + jnp.exp(-jnp.abs(x)))


def _params(sem):
    return pltpu.CompilerParams(dimension_semantics=sem, vmem_limit_bytes=VMEM_LIMIT)


def _rowwise(name, fn, rows, consts, row_outs, acc_outs=(), tr=ROW_BLOCK, halo=False):
    rows = [r if isinstance(r, tuple) else (r, r.shape[1], 0) for r in rows]
    t_len = rows[0][0].shape[0]
    tr = min(tr, t_len)
    n_r, n_c, n_o, n_a = len(rows), len(consts), len(row_outs), len(acc_outs)
    n_h = n_r if halo else 0
    sub = 8

    def body(*refs):
        ins = [r[...] for r in refs[:n_r]]
        ins += [jnp.where(pl.program_id(0) == 0, 0.0, r[sub - 1:sub, :]) for r in refs[n_r:n_r + n_h]]
        ins += [r[...] for r in refs[n_r + n_h:n_r + n_h + n_c]]
        refs = refs[:n_r] + refs[n_r + n_h:]
        outs = fn(*ins)
        o_refs = refs[n_r + n_c:n_r + n_c + n_o]
        a_refs = refs[n_r + n_c + n_o:]
        for o_ref, val in zip(o_refs, outs[:n_o]):
            o_ref[...] = val.astype(o_ref.dtype)
        if n_a:
            first = pl.program_id(0) == 0

            @pl.when(first)
            def _():
                for a_ref, val in zip(a_refs, outs[n_o:]):
                    a_ref[...] = val

            @pl.when(jnp.logical_not(first))
            def _():
                for a_ref, val in zip(a_refs, outs[n_o:]):
                    a_ref[...] += val

    in_specs = [pl.BlockSpec((tr, w), functools.partial(lambda i, c: (i, c), c=cb)) for _, w, cb in rows]
    if halo:
        in_specs += [pl.BlockSpec((sub, w), functools.partial(lambda i, c: (jnp.maximum(i * (tr // sub) - 1, 0), c), c=cb))
                     for _, w, cb in rows]
    in_specs += [pl.BlockSpec(c.shape, functools.partial(lambda i, n: (0,) * n, n=c.ndim)) for c in consts]
    out_specs = [pl.BlockSpec((tr, w), lambda i: (i, 0)) for w, _ in row_outs]
    out_specs += [pl.BlockSpec(s, functools.partial(lambda i, n: (0,) * n, n=len(s))) for s in acc_outs]
    out_shape = [jax.ShapeDtypeStruct((t_len, w), dt) for w, dt in row_outs]
    out_shape += [jax.ShapeDtypeStruct(s, F32) for s in acc_outs]
    return pl.pallas_call(
        body, name=name, grid=(t_len // tr,), in_specs=in_specs, out_specs=out_specs, out_shape=out_shape,
        compiler_params=_params(("arbitrary",)),
    )(*[r[0] for r in rows], *([r[0] for r in rows] if halo else []), *consts)


def _colwise(name, fn, n_blocks, cols, prms, col_outs, prm_outs=()):
    t_len = cols[0][0].shape[0]
    n_i = len(cols) + len(prms)

    def body(*refs):
        outs = fn(*[r[...] for r in refs[:n_i]])
        for o_ref, val in zip(refs[n_i:], outs):
            o_ref[...] = val.astype(o_ref.dtype)

    spec = lambda r, w: pl.BlockSpec((r, w), lambda j: (0, j))
    in_specs = [spec(t_len, w) for _, w in cols] + [spec(a.shape[0], LANE) for a in prms]
    out_specs = [spec(t_len, bw) for _, _, bw in col_outs] + [spec(r, LANE) for r, _ in prm_outs]
    out_shape = [jax.ShapeDtypeStruct((t_len, w), dt) for w, dt, _ in col_outs]
    out_shape += [jax.ShapeDtypeStruct((r, w), F32) for r, w in prm_outs]
    return pl.pallas_call(
        body, name=name, grid=(n_blocks,), in_specs=in_specs, out_specs=out_specs, out_shape=out_shape,
        compiler_params=_params(("arbitrary",)),
    )(*[c[0] for c in cols], *prms)


def _matmul(name, a, b, dn, outs, *, tm, tn, tk, extras=(), consts=(), epilogue=None, sums=(), xch=(), xch_scatter=()):
    if dn == "nn":
        (m, k), n = a.shape, b.shape[1]
    elif dn == "nt":
        (m, k), n = a.shape, b.shape[0]
    else:
        (k, m), n = a.shape, b.shape[1]
    tm, tn, tk = min(tm, m), min(tn, n), min(tk, k)
    nk = k // tk
    grid = (m // tm, n // tn, nk)
    assert not sums or (grid[1] == 1 and nk == 1)
    a_spec = pl.BlockSpec((tk, tm), lambda i, j, q: (q, i)) if dn == "tn" else pl.BlockSpec((tm, tk), lambda i, j, q: (i, q))
    b_spec = pl.BlockSpec((tn, tk), lambda i, j, q: (j, q)) if dn == "nt" else pl.BlockSpec((tk, tn), lambda i, j, q: (q, j))
    o_spec = pl.BlockSpec((tm, tn), lambda i, j, q: (i, j))
    c_spec = pl.BlockSpec((1, tn), lambda i, j, q: (0, j))
    n_e, n_c, n_o, n_s, n_x = len(extras), len(consts), len(outs), len(sums), len(xch)
    x_specs, x_shapes, x_sems = _exchange_io(xch, xch_scatter) if n_x else ([], [], [])

    def body(*refs):
        a_ref, b_ref = refs[:2]
        e_refs = refs[2:2 + n_e + n_c]
        x_in = refs[2 + n_e + n_c:2 + n_e + n_c + n_x]
        rest = refs[2 + n_e + n_c + n_x:]
        o_refs, s_refs, x_out, scratch = rest[:n_o], rest[n_o:n_o + n_s], rest[n_o + n_s:n_o + n_s + n_x], rest[n_o + n_s + n_x:]
        step = (pl.program_id(0) * grid[1] + pl.program_id(1)) * nk + pl.program_id(2)
        if n_x:
            start, wait = _exchange_plan(x_in, x_out, xch_scatter, *scratch[len(scratch) - 3:])
            pl.when(step == 0)(start)
        part = lax.dot_general(a_ref[...].astype(BF16), b_ref[...].astype(BF16), _dims(dn, 2), preferred_element_type=F32)

        def finish(acc):
            vals = (acc,) if epilogue is None else epilogue(acc, *[e[...] for e in e_refs])
            for o_ref, val in zip(o_refs, vals[:n_o]):
                o_ref[...] = val.astype(o_ref.dtype)
            if n_s:
                @pl.when(step == 0)
                def _():
                    for s_ref, val in zip(s_refs, vals[n_o:]):
                        s_ref[...] = val

                @pl.when(step > 0)
                def _():
                    for s_ref, val in zip(s_refs, vals[n_o:]):
                        s_ref[...] += val

        if nk == 1:
            finish(part)
        else:
            acc_ref = scratch[0]
            q = pl.program_id(2)

            @pl.when(q == 0)
            def _():
                acc_ref[...] = part

            @pl.when(q > 0)
            def _():
                acc_ref[...] += part

            @pl.when(q == nk - 1)
            def _():
                finish(acc_ref[...])

        if n_x:
            pl.when(step == grid[0] * grid[1] * nk - 1)(wait)

    plain = not (n_s or n_x)
    res = pl.pallas_call(
        body, name=name, grid=grid,
        in_specs=[a_spec, b_spec] + [o_spec] * n_e + [c_spec] * n_c + x_specs,
        out_specs=[o_spec] * n_o + [c_spec] * n_s + x_specs,
        out_shape=[jax.ShapeDtypeStruct((m, n), dt) for dt in outs] + [jax.ShapeDtypeStruct(s, F32) for s in sums] + x_shapes,
        scratch_shapes=([pltpu.VMEM((tm, tn), F32)] if nk > 1 else []) + x_sems,
        compiler_params=pltpu.CompilerParams(
            dimension_semantics=("parallel", "parallel", "arbitrary") if plain else ("arbitrary",) * 3,
            vmem_limit_bytes=VMEM_LIMIT, has_side_effects=bool(n_x)),
    )(a, b, *extras, *consts, *xch)
    return res[0] if len(res) == 1 else res


def _rms(h, g):
    return h * lax.rsqrt(jnp.mean(h * h, axis=-1, keepdims=True) + RMS_EPS) * g


def _rms_bwd(h, g, dy):
    rs = lax.rsqrt(jnp.mean(h * h, axis=-1, keepdims=True) + RMS_EPS)
    n = h * rs
    dn = dy * g
    dh = rs * (dn - n * jnp.mean(dn * n, axis=-1, keepdims=True))
    return dh, jnp.sum(dy * n, axis=0, keepdims=True)


def _rwkv_pre(k, xw, xa, xg, w0, a0, k_k, k_a, wl, al, gl):
    zw = w0 + _mm(jnp.tanh(xw), wl, "nn", 1)
    lw = -jnp.exp(-_softplus(-zw) - 0.5)
    iclr = _sigmoid(a0 + _mm(xa, al, "nn", 1))
    g = _mm(_sigmoid(xg), gl, "nn", 1)
    kk0 = k * k_k
    kk = kk0 / jnp.maximum(jnp.sqrt(_hsum(kk0 * kk0)), L2_EPS)
    k_h = k * (1.0 + (iclr - 1.0) * k_a)
    return lw, k_h, -kk, kk * iclr, g


def _rwkv_post(y, r, k_h, v, g, ln_g, ln_b, r_k):
    mu = _hsum(y) * (1.0 / HEAD_DIM)
    yc = y - mu
    var = _hsum(yc * yc) * (1.0 / HEAD_DIM)
    yo = yc * lax.rsqrt(var + GN_EPS) * ln_g + ln_b
    bonus = _hsum(r * k_h * r_k) * v
    return (yo + bonus) * g


def _shift_down(x, n):
    rows = lax.broadcasted_iota(jnp.int32, x.shape, 0)
    return jnp.where(rows < n, 0.0, pltpu.roll(x, n, 0))


def _shift_up(x, n):
    t_len = x.shape[0]
    rows = lax.broadcasted_iota(jnp.int32, x.shape, 0)
    return jnp.where(rows >= t_len - n, 0.0, pltpu.roll(x, t_len - n, 0))


def _exchange_plan(ins, outs, scatter, send_sems, recv_sems, local_sems):
    x, y, c = lax.axis_index("x"), lax.axis_index("y"), lax.axis_index("c")
    me = 4 * x + 2 * y + c

    def local(i):
        return pltpu.make_async_copy(ins[i].at[me] if scatter[i] else ins[i], outs[i].at[me], local_sems.at[i])

    def send(i, rel):
        return pltpu.make_async_remote_copy(
            src_ref=ins[i].at[me ^ rel] if scatter[i] else ins[i], dst_ref=outs[i].at[me],
            send_sem=send_sems.at[i, rel - 1], recv_sem=recv_sems.at[i, rel - 1],
            device_id=(x ^ (rel >> 2), y ^ ((rel >> 1) & 1), c ^ (rel & 1)), device_id_type=pl.DeviceIdType.MESH)

    def landed(i, rel):
        slot = outs[i].at[me ^ rel]
        return pltpu.make_async_remote_copy(
            src_ref=slot, dst_ref=slot, send_sem=send_sems.at[i, rel - 1], recv_sem=recv_sems.at[i, rel - 1],
            device_id=(x, y, c), device_id_type=pl.DeviceIdType.MESH)

    def start():
        for i in range(len(ins)):
            local(i).start()
            for rel in range(1, N_DEV):
                send(i, rel).start()

    def wait():
        for i in range(len(ins)):
            local(i).wait()
            for rel in range(1, N_DEV):
                landed(i, rel).wait_recv()
            for rel in range(1, N_DEV):
                send(i, rel).wait_send()

    return start, wait


def _gather_plan(ins, outs, send_sems, recv_sems, local_sems):
    x, y, c = lax.axis_index("x"), lax.axis_index("y"), lax.axis_index("c")
    me = 4 * x + 2 * y + c
    direct, chips = (1, 2, 4, 6), (2, 4, 6)

    def local(i):
        return pltpu.make_async_copy(ins[i], outs[i].at[me], local_sems.at[i])

    def send(i, rel):
        return pltpu.make_async_remote_copy(
            src_ref=ins[i], dst_ref=outs[i].at[me], send_sem=send_sems.at[i, rel - 1], recv_sem=recv_sems.at[i, rel - 1],
            device_id=(x ^ (rel >> 2), y ^ ((rel >> 1) & 1), c ^ (rel & 1)), device_id_type=pl.DeviceIdType.MESH)

    def passed(i, rel):
        slot = outs[i].at[me ^ rel]
        return pltpu.make_async_remote_copy(
            src_ref=slot, dst_ref=slot, send_sem=send_sems.at[i, rel], recv_sem=recv_sems.at[i, rel],
            device_id=(x, y, 1 - c), device_id_type=pl.DeviceIdType.MESH)

    def landed(i, rel):
        slot = outs[i].at[me ^ rel]
        return pltpu.make_async_remote_copy(
            src_ref=slot, dst_ref=slot, send_sem=send_sems.at[i, rel - 1], recv_sem=recv_sems.at[i, rel - 1],
            device_id=(x, y, c), device_id_type=pl.DeviceIdType.MESH)

    def start():
        for i in range(len(ins)):
            local(i).start()
            for rel in direct:
                send(i, rel).start()

    def forward(i):
        for rel in chips:
            landed(i, rel).wait_recv()
            passed(i, rel).start()

    def wait():
        for i in range(len(ins)):
            local(i).wait()
            for rel in (1, 3, 5, 7):
                landed(i, rel).wait_recv()
            for rel in direct:
                send(i, rel).wait_send()
            for rel in chips:
                passed(i, rel).wait_send()

    return start, forward, wait


def _exchange_io(arrays, scatter):
    n = len(arrays)
    any_spec = pl.BlockSpec(memory_space=pl.ANY)
    out_shape = [jax.ShapeDtypeStruct(a.shape if sc else (N_DEV,) + a.shape, a.dtype) for a, sc in zip(arrays, scatter)]
    sems = [pltpu.SemaphoreType.DMA((n, N_DEV - 1)), pltpu.SemaphoreType.DMA((n, N_DEV - 1)), pltpu.SemaphoreType.DMA((n,))]
    return [any_spec] * n, out_shape, sems


def _exchange(name, arrays, scatter):
    n = len(arrays)
    specs, out_shape, sems = _exchange_io(arrays, scatter)

    def body(*refs):
        if any(scatter):
            start, wait = _exchange_plan(refs[:n], refs[n:2 * n], scatter, *refs[2 * n:])
            start()
        else:
            start, forward, wait = _gather_plan(refs[:n], refs[n:2 * n], *refs[2 * n:])
            start()
            for i in range(n):
                forward(i)
        wait()

    return pl.pallas_call(
        body, name=name, in_specs=specs, out_specs=specs, out_shape=out_shape, scratch_shapes=sems,
        compiler_params=pltpu.CompilerParams(has_side_effects=True),
    )(*arrays)


def _tri_powers(low):
    powers, n = [low], 1
    while 2 * n < low.shape[-1]:
        powers.append(_mm(powers[-1], powers[-1], "nn", REC_PASSES))
        n *= 2
    return powers


@jax.custom_vjp
def _tri_solve(low, rhs):
    for p in _tri_powers(low):
        rhs = rhs + _mm(p, rhs, "nn", REC_PASSES)
    return rhs


def _tri_solve_fwd(low, rhs):
    powers = _tri_powers(low)
    for p in powers:
        rhs = rhs + _mm(p, rhs, "nn", REC_PASSES)
    return rhs, (powers, rhs)


def _tri_solve_bwd(res, d):
    powers, u = res
    for p in powers:
        d = d + _mm(p, d, "tn", REC_PASSES)
    return _mm(d, u, "nt", REC_PASSES), d


_tri_solve.defvjp(_tri_solve_fwd, _tri_solve_bwd)


def _chunk_fwd(z0, r, lw, k, v, a, b):
    n_h, c, n_k = r.shape
    mm = functools.partial(_mm, passes=REC_PASSES)
    gram = functools.partial(_mm, passes=3)
    ti = lax.broadcasted_iota(jnp.int32, (c, c), 0)
    si = lax.broadcasted_iota(jnp.int32, (c, c), 1)
    strict, incl = si < ti, si <= ti
    cum = _mm(jnp.broadcast_to(incl.astype(F32), (n_h, c, c)), lw, "nn", 3)
    cum_end = cum[:, c - 1:c, :]
    at = a * jnp.exp(cum - lw)
    bt = b * jnp.exp(-cum)
    kt = k * jnp.exp(-cum)
    rt = r * jnp.exp(cum)
    be = b * jnp.exp(cum_end - cum)
    ke = k * jnp.exp(cum_end - cum)
    lab = jnp.where(strict, gram(at, bt, "nt"), 0.0)
    lak = jnp.where(strict, gram(at, kt, "nt"), 0.0)
    u = _tri_solve(lab, mm(at, z0, "nn") + mm(lak, v, "nn"))
    rb =jnp.where(incl, gram(rt, bt, "nt"), 0.0)
    rk = jnp.where(incl, gram(rt, kt, "nt"), 0.0)
    y = mm(rt, z0, "nn") + mm(rb, u, "nn") + mm(rk, v, "nn")
    ki = lax.broadcasted_iota(jnp.int32, (n_k, n_k), 0)
    kj = lax.broadcasted_iota(jnp.int32, (n_k, n_k), 1)
    dmat = jnp.where(ki == kj, jnp.broadcast_to(jnp.exp(cum_end), (n_h, n_k, n_k)), 0.0)
    z_end = mm(dmat, z0, "nn") + mm(be, u, "tn") + mm(ke, v, "tn")
    return y, z_end


def _heads(x):
    return jnp.stack([x[:, h * HEAD_DIM:(h + 1) * HEAD_DIM] for h in range(N_HEADS)])


def _unheads(x):
    return jnp.concatenate([x[h] for h in range(N_HEADS)], axis=-1)


def _rec_params():
    return pltpu.CompilerParams(dimension_semantics=("arbitrary",), vmem_limit_bytes=VMEM_LIMIT, has_side_effects=True)


def _rec_fwd(u, lw, k, a, b, xch):
    t_len = lw.shape[0]
    c = min(REC_CHUNK, t_len)
    nc = t_len // c
    n_x = len(xch)
    x_specs, x_shapes, x_sems = _exchange_io(xch, [False] * n_x)
    sizes = [a_.size * a_.dtype.itemsize for a_ in xch]
    pass_step = [min(nc - 1, int(0.9 * nc * sum(sizes[:j + 1]) / sum(sizes)) + 1) for j in range(n_x)]

    def body(*refs):
        r_ref, v_ref, lw_ref, k_ref, a_ref, b_ref = refs[:6]
        x_in = refs[6:6 + n_x]
        y_ref, zs_ref = refs[6 + n_x:8 + n_x]
        x_out = refs[8 + n_x:8 + 2 * n_x]
        z_scr = refs[8 + 2 * n_x]
        start, forward, wait = _gather_plan(x_in, x_out, *refs[9 + 2 * n_x:])
        i = pl.program_id(0)

        @pl.when(i == 0)
        def _():
            start()
            z_scr[...] = jnp.zeros_like(z_scr)

        z0 = z_scr[...]
        zs_ref[0] = z0
        y, z_end = _chunk_fwd(z0, _heads(r_ref[...]), _heads(lw_ref[...]), _heads(k_ref[...]), _heads(v_ref[...]),
                              _heads(a_ref[...]), _heads(b_ref[...]))
        y_ref[...] = _unheads(y)
        z_scr[...] = z_end

        for j in range(n_x):
            pl.when(i == pass_step[j])(functools.partial(forward, j))

        @pl.when(i == nc - 1)
        def _():
            wait()

    blk = lambda cb: pl.BlockSpec((c, RWKV_DIM), functools.partial(lambda i, q: (i, q), q=cb))
    res = pl.pallas_call(
        body, name="rwkv_rec_fwd", grid=(nc,),
        in_specs=[blk(0), blk(2)] + [blk(0)] * 4 + x_specs,
        out_specs=[blk(0), pl.BlockSpec((1, N_HEADS, HEAD_DIM, HEAD_DIM), lambda i: (i, 0, 0, 0))] + x_specs,
        out_shape=[jax.ShapeDtypeStruct((t_len, RWKV_DIM), F32),
                   jax.ShapeDtypeStruct((nc, N_HEADS, HEAD_DIM, HEAD_DIM), F32)] + x_shapes,
        scratch_shapes=[pltpu.VMEM((N_HEADS, HEAD_DIM, HEAD_DIM), F32)] + x_sems,
        compiler_params=_rec_params(),
    )(u, u, lw, k, a, b, *xch)
    return res[0], res[1], res[2:]


def _rec_bwd(u, lw, k, a, b, zs, dy, xch, xch_scatter):
    t_len = lw.shape[0]
    c = min(REC_CHUNK, t_len)
    nc = t_len // c
    n_x = len(xch)
    x_specs, x_shapes, x_sems = _exchange_io(xch, xch_scatter)

    def body(*refs):
        r_ref, v_ref, lw_ref, k_ref, a_ref, b_ref, zs_ref, dy_ref = refs[:8]
        x_in = refs[8:8 + n_x]
        g_refs = refs[8 + n_x:14 + n_x]
        x_out = refs[14 + n_x:14 + 2 * n_x]
        dz_scr = refs[14 + 2 * n_x]
        start, wait = _exchange_plan(x_in, x_out, xch_scatter, *refs[15 + 2 * n_x:])
        i = pl.program_id(0)

        @pl.when(i == 0)
        def _():
            start()
            dz_scr[...] = jnp.zeros_like(dz_scr)

        _, vjp = jax.vjp(_chunk_fwd, zs_ref[0], _heads(r_ref[...]), _heads(lw_ref[...]), _heads(k_ref[...]),
                         _heads(v_ref[...]), _heads(a_ref[...]), _heads(b_ref[...]))
        dz0, dr, dlw, dk, dv, da, db = vjp((_heads(dy_ref[...]), dz_scr[...]))
        for ref, val in zip(g_refs, (dr, dv, dlw, dk, da, db)):
            ref[...] = _unheads(val)
        dz_scr[...] = dz0

        @pl.when(i == nc - 1)
        def _():
            wait()

    blk = lambda cb: pl.BlockSpec((c, RWKV_DIM), functools.partial(lambda i, q: (nc - 1 - i, q), q=cb))
    res = pl.pallas_call(
        body, name="rwkv_rec_bwd", grid=(nc,),
        in_specs=[blk(0), blk(2)] + [blk(0)] * 4
                 + [pl.BlockSpec((1, N_HEADS, HEAD_DIM, HEAD_DIM), lambda i: (nc - 1 - i, 0, 0, 0)), blk(0)] + x_specs,
        out_specs=[blk(0)] * 6 + x_specs,
        out_shape=[jax.ShapeDtypeStruct((t_len, RWKV_DIM), F32)] * 6 + x_shapes,
        scratch_shapes=[pltpu.VMEM((N_HEADS, HEAD_DIM, HEAD_DIM), F32)] + x_sems,
        compiler_params=_rec_params(),
    )(u, u, lw, k, a, b, zs, dy, *xch)
    return res[:6], res[6:]


_EARLY = ["w_in", "conv_w", "w_lora_up", "a_lora_up", "g_lora_up"]
_LATE = ["w_out", "w_up", "w_down", "w_ple_gate", "w_ple_proj"]
_SHARDED = _EARLY + _LATE
_COL_SHARDED = {"w_in", "conv_w", "w_lora_up", "a_lora_up", "g_lora_up", "w_up", "w_ple_proj"}
_BF16_GATHER = {"w_in", "w_out", "w_up", "w_down", "w_ple_gate", "w_ple_proj"}
_REPLICATED = ["norm_mix_g", "shift_mu", "w0", "a0", "k_k", "k_a", "r_k", "ln_x_g", "ln_x_b", "norm_mlp_g", "norm_ple_g",
               "norm_final_g"]
_WEIGHTS = ["norm_mix_g", "w_in", "conv_w", "shift_mu", "w_lora_up", "w0", "a_lora_up", "a0", "g_lora_up", "k_k", "k_a", "r_k",
            "ln_x_g", "ln_x_b", "w_out", "norm_mlp_g", "w_up", "w_down", "norm_ple_g", "w_ple_gate", "w_ple_proj", "norm_final_g"]
_PACK_ROWS = 80


def _unshard(name, g):
    if name in _COL_SHARDED:
        return jnp.moveaxis(g, 0, 1).reshape(g.shape[1], N_DEV * g.shape[2])
    return g.reshape(N_DEV * g.shape[1], g.shape[2])


def _reshard(name, full):
    if name in _COL_SHARDED:
        return jnp.moveaxis(full.reshape(full.shape[0], N_DEV, full.shape[1] // N_DEV), 1, 0)
    return full.reshape(N_DEV, full.shape[0] // N_DEV, full.shape[1])


def _pad_in_cols(a):
    z = lambda n: jnp.zeros(a.shape[:-1] + (n,), a.dtype)
    conv = [a[..., part * CONV_DIM + j * LANE:part * CONV_DIM + (j + 1) * LANE] for j in range(CONV_DIM // LANE) for part in range(3)]
    return jnp.concatenate(conv + [a[..., CONV_COLS:3136], z(64), a[..., 3136:3200], z(64), a[..., 3200:3360], z(96)], axis=-1)


def _unpad_in_cols(a):
    conv = [a[..., (3 * j + part) * LANE:(3 * j + part + 1) * LANE] for part in range(3) for j in range(CONV_DIM // LANE)]
    return jnp.concatenate(conv + [a[..., CONV_COLS:3136], a[..., 3200:3264], a[..., 3328:3488]], axis=-1)


def _assemble_w_in(g):
    n_dev, rows, cols = g.shape

    def body(g_ref, o_ref):
        o_ref[...] = _pad_in_cols(jnp.concatenate([g_ref[d] for d in range(n_dev)], axis=1))

    return pl.pallas_call(
        body, name="w_in_assemble", grid=(rows // ROW_BLOCK,),
        in_specs=[pl.BlockSpec((n_dev, ROW_BLOCK, cols), lambda i: (0, i, 0))],
        out_specs=pl.BlockSpec((ROW_BLOCK, IN_PAD), lambda i: (i, 0)),
        out_shape=jax.ShapeDtypeStruct((rows, IN_PAD), g.dtype), compiler_params=_params(("arbitrary",)),
    )(g)


def _split_w_in_grad(dw):
    rows = dw.shape[0]
    cols = IN_COLS // N_DEV

    def body(d_ref, o_ref):
        full = _unpad_in_cols(d_ref[...])
        for d in range(N_DEV):
            o_ref[d] = full[:, cols * d:cols * (d + 1)]

    return pl.pallas_call(
        body, name="w_in_grad_split", grid=(rows // ROW_BLOCK,),
        in_specs=[pl.BlockSpec((ROW_BLOCK, IN_PAD), lambda i: (i, 0))],
        out_specs=pl.BlockSpec((N_DEV, ROW_BLOCK, cols), lambda i: (0, i, 0)),
        out_shape=jax.ShapeDtypeStruct((N_DEV, rows, cols), dw.dtype), compiler_params=_params(("arbitrary",)),
    )(dw)


def _pad_rows(a, rows):
    return jnp.concatenate([a, jnp.zeros((rows - a.shape[0],) + a.shape[1:], a.dtype)], axis=0)


def _pack(vals):
    flat = jnp.concatenate([v.reshape(-1) for v in vals])
    return jnp.concatenate([flat, jnp.zeros((_PACK_ROWS * LANE - flat.shape[0],), F32)]).reshape(_PACK_ROWS, LANE)


SEG_W = [RWKV_DIM, RWKV_DIM, RWKV_DIM, LANE, LANE, 2 * LANE]
SEG_OFF = [0, 512, 1024, XW_OFF, XA_OFF, XG_OFF]


def _rwkv_pre_bwd(proj, u, grads, mu, small, dproj):
    t_len = u.shape[0]
    tr = min(ROW_BLOCK, t_len)
    nb = t_len // tr
    sub = 8
    n_g = len(grads)
    acc_shapes = [(1, RW_PAD)] + [(1, RWKV_DIM)] * 4 + [(LANE, RWKV_DIM), (LANE, RWKV_DIM), (2 * LANE, RWKV_DIM)]

    def body(*refs):
        seg_refs, halo_refs = refs[:6], refs[6:12]
        k_ref, xw_ref, xa_ref, xg_ref = refs[12:16]
        g_refs = refs[16:16 + n_g]
        mu_ref = refs[16 + n_g]
        prm_refs = refs[17 + n_g:24 + n_g]
        out_hbm = refs[25 + n_g]
        acc_refs = refs[26 + n_g:26 + n_g + len(acc_shapes)]
        vbuf, sems, carry = refs[26 + n_g + len(acc_shapes):]
        i = pl.program_id(0)
        blk = nb - 1 - i
        dr1, dr2, dv1, dv2, dlw, dk1, dk2, da, db, dg = [g[...] for g in g_refs]
        _, vjp = jax.vjp(_rwkv_pre, k_ref[...], xw_ref[...], xa_ref[...], xg_ref[...], *[p_[...] for p_ in prm_refs])
        dk, dxw, dxa, dxg, *dprm = vjp((dlw, dk1 + dk2, da, db, dg))
        du = jnp.concatenate([dr1 + dr2, dk, dv1 + dv2, dxw, dxa, dxg], axis=1)
        mu_v = mu_ref[...]

        @pl.when(i == 0)
        def _():
            carry[...] = jnp.zeros_like(carry)

        rows = lax.broadcasted_iota(jnp.int32, du.shape, 0)
        nxt = jnp.where(rows == tr - 1, carry[...], pltpu.roll(du, tr - 1, 0))
        d_rw = du - mu_v * du + mu_v * nxt
        d_mu = []
        for s_ref, h_ref, off, wd in zip(seg_refs, halo_refs, SEG_OFF, SEG_W):
            cur = s_ref[...]
            r0 = lax.broadcasted_iota(jnp.int32, cur.shape, 0)
            prev = jnp.where(r0 == 0, jnp.where(blk == 0, 0.0, h_ref[sub - 1:sub, :]), pltpu.roll(cur, 1, 0))
            d_mu.append(jnp.sum(du[:, off:off + wd] * (prev - cur), axis=0, keepdims=True))
        sums = [jnp.concatenate(d_mu, axis=1)] + list(dprm)

        @pl.when(i == 0)
        def _():
            for a_ref, val in zip(acc_refs, sums):
                a_ref[...] = val

        @pl.when(i > 0)
        def _():
            for a_ref, val in zip(acc_refs, sums):
                a_ref[...] += val

        carry[...] = du[0:1, :]
        slot = i % 2

        def writeback(s, b):
            return pltpu.make_async_copy(vbuf.at[s], out_hbm.at[pl.ds(b * tr, tr), pl.ds(CONV_COLS, RW_PAD)], sems.at[s])

        @pl.when(i >= 2)
        def _():
            writeback(slot, blk + 2).wait()

        vbuf[slot] = d_rw.astype(vbuf.dtype)
        writeback(slot, blk).start()

        @pl.when(i == nb - 1)
        def _():
            writeback(slot, blk).wait()
            if nb > 1:
                writeback(1 - slot, blk + 1).wait()

    rev = lambda w_, cb: pl.BlockSpec((tr, w_), functools.partial(lambda i, c: (nb - 1 - i, c), c=cb))
    halo = lambda w_, cb: pl.BlockSpec((sub, w_), functools.partial(
        lambda i, c: (jnp.maximum((nb - 1 - i) * (tr // sub) - 1, 0), c), c=cb))
    whole = lambda a: pl.BlockSpec(a.shape, functools.partial(lambda i, n: (0,) * n, n=a.ndim))
    segs = [(wd, (CONV_COLS + off) // wd) for off, wd in zip(SEG_OFF, SEG_W)]
    u_cols = [(512, 1), (LANE, XW_OFF // LANE), (LANE, XA_OFF // LANE), (2 * LANE, XG_OFF // (2 * LANE))]
    any_spec = pl.BlockSpec(memory_space=pl.ANY)
    res = pl.pallas_call(
        body, name="rwkv_pre_bwd", grid=(nb,),
        in_specs=[rev(*s) for s in segs] + [halo(*s) for s in segs] + [rev(*c) for c in u_cols]
                 + [rev(RWKV_DIM, 0)] * n_g + [whole(mu)] + [whole(p_) for p_ in small] + [any_spec],
        out_specs=[any_spec] + [pl.BlockSpec(s, functools.partial(lambda i, n: (0,) * n, n=len(s))) for s in acc_shapes],
        out_shape=[jax.ShapeDtypeStruct(dproj.shape, dproj.dtype)] + [jax.ShapeDtypeStruct(s, F32) for s in acc_shapes],
        scratch_shapes=[pltpu.VMEM((2, tr, RW_PAD), dproj.dtype), pltpu.SemaphoreType.DMA((2,)), pltpu.VMEM((1, RW_PAD), F32)],
        input_output_aliases={24 + n_g: 0},
        compiler_params=_params(("arbitrary",)),
    )(*[proj] * 12, *[u] * 4, *grads, mu, *small, dproj)
    return res


def _local_step(x, p, tgt, w, late_shards):
    row = lambda v: v.reshape(1, -1)
    w = dict(w)

    (xn1,) = _rowwise("rms_mix", lambda h, g: (_rms(h, g),), [x], [w["norm_mix_g"]], [(D_MODEL, BF16)])
    proj = _matmul("in_proj", xn1, w["w_in"], "nn", [F32], tm=1024, tn=512, tk=D_MODEL)
    n_cb = CONV_DIM // LANE

    def conv_fwd(blk, cw):
        gb, gc, hx = blk[:, :LANE], blk[:, LANE:2 * LANE], blk[:, 2 * LANE:]
        uu = gc * hx
        return (gb * (uu * cw[2:3] + _shift_down(uu, 1) * cw[1:2] + _shift_down(uu, 2) * cw[0:1]),)

    (y_conv,) = _colwise("conv_fwd", conv_fwd, n_cb, [(proj, 3 * LANE)], [w["conv_w"]], [(CONV_DIM, BF16, LANE)])

    small = [w["w0"], w["a0"], w["k_k"], w["k_a"], w["w_lora_up"], w["a_lora_up"], w["g_lora_up"]]
    def pre_fwd(*xs):
        cur, prev_rows, mu, prm = xs[:6], xs[6:12], xs[12], xs[13:]
        segs = []
        for c_, p_, off, wd in zip(cur, prev_rows, SEG_OFF, SEG_W):
            rows = lax.broadcasted_iota(jnp.int32, c_.shape, 0)
            prev = jnp.where(rows == 0, p_, pltpu.roll(c_, 1, 0))
            segs.append(c_ + mu[:, off:off + wd] * (prev - c_))
        return (jnp.concatenate(segs, axis=1),) + tuple(_rwkv_pre(segs[1], segs[3], segs[4], segs[5], *prm))

    proj_segs = [(proj, wd, (CONV_COLS + off) // wd) for off, wd in zip(SEG_OFF, SEG_W)]
    u, lw, k_h, ra, rb, g = _rowwise("rwkv_pre", pre_fwd, proj_segs, [w["shift_mu"]] + small,
                                     [(RW_PAD, F32)] + [(RWKV_DIM, F32)] * 5, halo=True)
    u_k, u_xw, u_xa, u_xg = (u, 512, 1), (u, LANE, XW_OFF // LANE), (u, LANE, XA_OFF // LANE), (u, 2 * LANE, XG_OFF // (2 * LANE))
    y_rec, zs, late = _rec_fwd(u, lw, k_h, ra, rb, late_shards)
    for n, gathered in zip(_LATE, late):
        w[n] = _unshard(n, gathered)
    post_c = [w["ln_x_g"], w["ln_x_b"], w["r_k"]]
    u_r, u_v = (u, 512, 0), (u, 512, 2)
    (y_rwkv,) = _rowwise("rwkv_post", lambda *xs: (_rwkv_post(*xs),), [y_rec, u_r, k_h, u_v, g], post_c, [(RWKV_DIM, BF16)])
    ycat = jnp.concatenate([y_conv, y_rwkv], axis=1)
    def res_norm(acc, r_, g_):
        h = acc + r_
        return h, _rms(h, g_)

    h1, xn2 = _matmul("out_proj", ycat, w["w_out"], "nn", [F32, BF16], tm=1024, tn=D_MODEL, tk=D_MODEL, extras=[x],
                      consts=[w["norm_mlp_g"]], epilogue=res_norm)

    def relu2(acc):
        hid = jnp.maximum(acc, 0.0)
        return hid, hid * hid

    hid, hsq = _matmul("mlp_up", xn2, w["w_up"], "nn", [BF16, BF16], tm=1024, tn=1024, tk=D_MODEL, epilogue=relu2)
    h2, xn3 = _matmul("mlp_down", hsq, w["w_down"], "nn", [F32, BF16], tm=512, tn=D_MODEL, tk=D_FF, extras=[h1],
                      consts=[w["norm_ple_g"]], epilogue=res_norm)
    zg = _matmul("ple_gate", xn3, w["w_ple_gate"], "nn", [F32], tm=1024, tn=1024, tk=D_MODEL)
    pp = _matmul("ple_proj", p, w["w_ple_proj"], "nn", [F32], tm=1024, tn=1024, tk=PLE_DIM)

    def head(h2_, zg_, pp_, tg, gf):
        gate = _sigmoid(zg_)
        h3 = h2_ + gate * pp_
        out = _rms(h3, gf)
        err = out - tg
        dh3, dgf = _rms_bwd(h3, gf, err * (1.0 / D_MODEL))
        loss = jnp.sum(jnp.sum(err * err, axis=1, keepdims=True), axis=0, keepdims=True) * (0.5 / D_MODEL)
        return dh3, dh3 * pp_ * gate * (1.0 - gate), dh3 * gate, dgf, loss

    dh3, dzg, dpp, d_norm_final, loss = _rowwise(
        "head", head, [h2, zg, pp, tgt], [row(w["norm_final_g"])], [(D_MODEL, F32), (D_MODEL, BF16), (D_MODEL, BF16)],
        [(1, D_MODEL), (1, 1)])

    d_w_ple_proj = _matmul("d_ple_proj", p, dpp, "tn", [BF16], tm=PLE_DIM, tn=1024, tk=4096)
    d_w_ple_gate = _matmul("d_ple_gate", xn3, dzg, "tn", [BF16], tm=512, tn=1024, tk=4096)

    def norm_bwd(dxn, h, dres, g_):
        dh, dg = _rms_bwd(h, g_, dxn)
        dh = dh + dres
        return dh, dh, dg

    nb = dict(tm=512, tn=D_MODEL, epilogue=norm_bwd, sums=[(1, D_MODEL)])
    dh2, dh2_b, d_norm_ple = _matmul("dx_ple_gate", dzg, w["w_ple_gate"], "nt", [F32, BF16], tk=D_MODEL,
                                     extras=[h2, dh3], consts=[w["norm_ple_g"]], **nb)
    d_w_down = _matmul("d_mlp_down", hsq, dh2_b, "tn", [BF16], tm=512, tn=1024, tk=4096)
    dpre = _matmul("dx_mlp_down", dh2_b, w["w_down"], "nt", [BF16], tm=1024, tn=1024, tk=D_MODEL, extras=[hid],
                   epilogue=lambda acc, hid_: (acc * (2.0 * hid_.astype(F32)),))
    d_w_up = _matmul("d_mlp_up", xn2, dpre, "tn", [BF16], tm=512, tn=1024, tk=4096)
    dh1, dh1_b, d_norm_mlp = _matmul("dx_mlp_up", dpre, w["w_up"], "nt", [F32, BF16], tk=D_FF,
                                     extras=[h1, dh2], consts=[w["norm_mlp_g"]], **nb)
    d_w_out = _matmul("d_out_proj", ycat, dh1_b, "tn", [BF16], tm=512, tn=1024, tk=4096)
    dycat = _matmul("dx_out_proj", dh1_b, w["w_out"], "nt", [F32], tm=1024, tn=1024, tk=D_MODEL)
    late_grads = dict(w_out=d_w_out, w_up=d_w_up, w_down=d_w_down, w_ple_gate=d_w_ple_gate, w_ple_proj=d_w_ple_proj)

    def conv_bwd(dy, blk, cw):
        gb, gc, hx = blk[:, :LANE], blk[:, LANE:2 * LANE], blk[:, 2 * LANE:]
        uu = gc * hx
        u1, u2 = _shift_down(uu, 1), _shift_down(uu, 2)
        dconv = dy * gb
        du = dconv * cw[2:3] + _shift_up(dconv, 1) * cw[1:2] + _shift_up(dconv, 2) * cw[0:1]
        s = lambda z: jnp.sum(z, axis=0, keepdims=True)
        d_blk = jnp.concatenate([dy * (uu * cw[2:3] + u1 * cw[1:2] + u2 * cw[0:1]), du * hx, du * gc], axis=1)
        return d_blk, s(dconv * u2), s(dconv * u1), s(dconv * uu)

    dproj, dcw0, dcw1, dcw2 = _colwise(
        "conv_bwd", conv_bwd, n_cb, [(dycat, LANE), (proj, 3 * LANE)], [w["conv_w"]],
        [(IN_PAD, BF16, 3 * LANE)], [(1, CONV_DIM)] * 3)

    def post_bwd(dy, y, r, k_h_, v, g_, ln_g, ln_b, r_k):
        _, vjp = jax.vjp(_rwkv_post, y, r, k_h_, v, g_, ln_g, ln_b, r_k)
        return vjp(dy)

    dy_rec, dr_p, dk_p, dv_p, dg, d_ln_g, d_ln_b, d_r_k = _rowwise(
        "rwkv_post_bwd", post_bwd, [(dycat, 512, 1), y_rec, u_r, k_h, u_v, g], post_c,
        [(RWKV_DIM, F32)] * 5, [(1, RWKV_DIM)] * 3)
    (dr_r, dv_r, dlw, dk_r, da, db), late_parts = _rec_bwd(
        u, lw, k_h, ra, rb, zs, dy_rec, [_reshard(n, late_grads[n]) for n in _LATE], [True] * len(_LATE))

    dproj, d_mu, d_w0, d_a0, d_k_k, d_k_a, d_wl, d_al, d_gl = _rwkv_pre_bwd(
        proj, u, [dr_p, dr_r, dv_p, dv_r, dlw, dk_p, dk_r, da, db, dg], w["shift_mu"], small, dproj)
    d_w_in = _matmul("d_in_proj", xn1, dproj, "tn", [BF16], tm=512, tn=896, tk=4096)
    early_grads = dict(conv_w=jnp.concatenate([dcw0, dcw1, dcw2], axis=0),
                       w_lora_up=d_wl[:64], a_lora_up=d_al[:64], g_lora_up=d_gl[:160])
    early_send = [_split_w_in_grad(d_w_in)] + [_reshard(n, early_grads[n]) for n in _EARLY[1:]]
    dx, d_norm_mix, *early_parts = _matmul(
        "dx_in_proj", dproj, w["w_in"], "nt", [F32], tk=IN_PAD, extras=[x, dh1], consts=[w["norm_mix_g"]],
        xch=early_send, xch_scatter=[True] * len(_EARLY), **dict(nb, epilogue=lambda *a: norm_bwd(*a)[1:]))

    grads = dict(
        norm_mix_g=d_norm_mix, shift_mu=d_mu, w0=d_w0, a0=d_a0, k_k=d_k_k, k_a=d_k_a, r_k=d_r_k,
        ln_x_g=d_ln_g, ln_x_b=d_ln_b, norm_mlp_g=d_norm_mlp, norm_ple_g=d_norm_ple, norm_final_g=d_norm_final)
    parts = dict(zip(_EARLY, early_parts))
    parts.update(zip(_LATE, late_parts))
    return loss, dx, grads, parts


def _adamw(name, parts, w, m, v):
    rows, cols = w.shape[-2:]
    lead = w.ndim - 2
    tr = rows if rows * cols * 4 * 8 <= (4 << 20) else max(8, (4 << 20) // (cols * 4 * 8) // 8 * 8)
    while rows % tr:
        tr -= 8

    def body(p_ref, w_ref, m_ref, v_ref, g_ref, d_ref, nm_ref, nv_ref):
        g = p_ref[0].astype(F32)
        for s in range(1, N_DEV):
            g = g + p_ref[s].astype(F32)
        nm = ADAM_B1 * m_ref[...] + (1.0 - ADAM_B1) * g
        nv = ADAM_B2 * v_ref[...] + (1.0 - ADAM_B2) * (g * g)
        m_hat = nm / (1.0 - ADAM_B1 ** ADAM_STEP)
        v_hat = nv / (1.0 - ADAM_B2 ** ADAM_STEP)
        g_ref[...] = g
        d_ref[...] = -ADAM_LR * (m_hat / (jnp.sqrt(v_hat) + ADAM_EPS) + ADAM_WD * w_ref[...])
        nm_ref[...] = nm
        nv_ref[...] = nv

    blk = pl.BlockSpec((None,) * lead + (tr, cols), lambda i: (0,) * lead + (i, 0))
    return pl.pallas_call(
        body, name=name, grid=(rows // tr,),
        in_specs=[pl.BlockSpec((N_DEV, tr, cols), lambda i: (0, i, 0)), blk, blk, blk], out_specs=[blk] * 4,
        out_shape=[jax.ShapeDtypeStruct(w.shape, F32)] * 4,
        compiler_params=_params(("arbitrary",)),
    )(parts, w, m, v)


def kernel(x, p, norm_mix_g, w_in, conv_w, shift_mu, w_lora_up, w0, a_lora_up, a0, g_lora_up, k_k, k_a, r_k, ln_x_g, ln_x_b, w_out, norm_mlp_g, w_up, w_down, norm_ple_g, w_ple_gate, w_ple_proj, norm_final_g, loss_target, m_norm_mix_g, m_w_in, m_conv_w, m_shift_mu, m_w_lora_up, m_w0, m_a_lora_up, m_a0, m_g_lora_up, m_k_k, m_k_a, m_r_k, m_ln_x_g, m_ln_x_b, m_w_out, m_norm_mlp_g, m_w_up, m_w_down, m_norm_ple_g, m_w_ple_gate, m_w_ple_proj, m_norm_final_g, v_norm_mix_g, v_w_in, v_conv_w, v_shift_mu, v_w_lora_up, v_w0, v_a_lora_up, v_a0, v_g_lora_up, v_k_k, v_k_a, v_r_k, v_ln_x_g, v_ln_x_b, v_w_out, v_norm_mlp_g, v_w_up, v_w_down, v_norm_ple_g, v_w_ple_gate, v_w_ple_proj, v_norm_final_g):
    args = dict(locals())
    wts = {n: args[n] for n in _WEIGHTS}
    mom = {n: args["m_" + n] for n in _WEIGHTS}
    var = {n: args["v_" + n] for n in _WEIGHTS}
    shard2d = lambda a: a.reshape(a.shape[-2:])
    pad_mu = lambda a: _pad_in_cols(jnp.concatenate([jnp.zeros((1, CONV_COLS), F32), a], axis=1))[:, CONV_COLS:]
    unpad_mu = lambda a: _unpad_in_cols(jnp.concatenate([jnp.zeros((1, CONV_COLS), F32), a], axis=1))[:, CONV_COLS:]

    shards = {n: shard2d(wts[n]).astype(BF16 if n in _BF16_GATHER else F32) for n in _SHARDED}
    gathered = _exchange("gather_early", [shards[n] for n in _EARLY], [False] * len(_EARLY))
    w = {n: _unshard(n, g) for n, g in zip(_EARLY[1:], gathered[1:])}
    w["w_in"] = _assemble_w_in(gathered[0])
    w["w_lora_up"] = _pad_rows(w["w_lora_up"], LANE)
    w["a_lora_up"] = _pad_rows(w["a_lora_up"], LANE)
    w["g_lora_up"] = _pad_rows(w["g_lora_up"], 2 * LANE)
    for n in _REPLICATED:
        w[n] = wts[n].reshape(1, -1)
    w["shift_mu"] = pad_mu(wts["shift_mu"])

    loss, dx, grads, parts = _local_step(x[0], p[0, 0], loss_target[0], w, [shards[n] for n in _LATE])

    grads["shift_mu"] = unpad_mu(grads["shift_mu"])
    (small_parts,) = _exchange("gather_small", [_pack([grads[n] for n in _REPLICATED] + [loss])], [False])

    out = {}
    for n in _SHARDED:
        out[n] = _adamw("adamw_" + n, parts[n], wts[n], mom[n], var[n])
    sm = _adamw("adamw_small", small_parts, _pack([wts[n] for n in _REPLICATED]), _pack([mom[n] for n in _REPLICATED]),
                _pack([var[n] for n in _REPLICATED]))
    off = 0
    for n in _REPLICATED:
        size = wts[n].size
        out[n] = [r.reshape(-1)[off:off + size].reshape(wts[n].shape) for r in sm]
        off += size
    loss_total = sm[0].reshape(-1)[off]
    return (loss_total, dx[None], *[out[n][0] for n in _WEIGHTS], *[out[n][1] for n in _WEIGHTS],
            *[out[n][2] for n in _WEIGHTS], *[out[n][3] for n in _WEIGHTS])
```

```python
import functools

import jax
import jax.numpy as jnp
from jax import lax
from jax.experimental import pallas as pl
from jax.experimental.pallas import tpu as pltpu

F32 = jnp.float32
BF16 = jnp.bfloat16

N_DEV = 8
D_MODEL = 1024
CONV_DIM = 512
RWKV_DIM = 512
HEAD_DIM = 64
N_HEADS = 8
D_FF = 4096
PLE_DIM = 256
RMS_EPS = 1e-6
GN_EPS = 64e-5
L2_EPS = 1e-12
ADAM_LR, ADAM_B1, ADAM_B2, ADAM_EPS, ADAM_WD, ADAM_STEP = 0.001, 0.9, 0.999, 1e-08, 0.01, 10

CONV_COLS = 3 * CONV_DIM
RW_PAD = 2048
IN_PAD = CONV_COLS + RW_PAD
IN_COLS = 3360
XW_OFF, XA_OFF, XG_OFF = 1536, 1664, 1792
REC_CHUNK = 128
REC_PASSES = 1
ROW_BLOCK = 256
LANE = 128
VMEM_LIMIT = 56 * 1024 * 1024


def _dims(dn, ndim):
    if ndim == 3:
        return {"nn": (((2,), (1,)), ((0,), (0,))), "nt": (((2,), (2,)), ((0,), (0,))),
                "tn": (((1,), (1,)), ((0,), (0,)))}[dn]
    return {"nn": (((1,), (0,)), ((), ())), "nt": (((1,), (1,)), ((), ())), "tn": (((0,), (0,)), ((), ()))}[dn]


def _split2(x):
    hi = x.astype(BF16)
    return hi, (x - hi.astype(F32)).astype(BF16)


def _mm_raw(x, y, dn, passes):
    f = lambda p, q: lax.dot_general(p, q, _dims(dn, x.ndim), preferred_element_type=F32)
    if passes == 1:
        return f(x.astype(BF16), y.astype(BF16))
    xh, xl = _split2(x)
    yh, yl = _split2(y)
    return f(xh, yh) + f(xh, yl) + f(xl, yh)


@functools.partial(jax.custom_vjp, nondiff_argnums=(2, 3))
def _mm(x, y, dn, passes):
    return _mm_raw(x, y, dn, passes)


def _mm_fwd(x, y, dn, passes):
    return _mm_raw(x, y, dn, passes), (x, y)


def _mm_bwd(dn, passes, res, d):
    x, y = res
    if dn == "nn":
        return _mm(d, y, "nt", passes), _mm(x, d, "tn", passes)
    if dn == "nt":
        return _mm(d, y, "nn", passes), _mm(d, x, "tn", passes)
    return _mm(y, d, "nt", passes), _mm(x, d, "nn", passes)


_mm.defvjp(_mm_fwd, _mm_bwd)


def _head_ones():
    i = lax.broadcasted_iota(jnp.int32, (RWKV_DIM, RWKV_DIM), 0) // HEAD_DIM
    j = lax.broadcasted_iota(jnp.int32, (RWKV_DIM, RWKV_DIM), 1) // HEAD_DIM
    return (i == j).astype(BF16)


def _hsum_raw(x):
    ones = _head_ones()
    f = lambda p: lax.dot_general(p, ones, _dims("nn", 2), preferred_element_type=F32)
    x1, x2 = _split2(x)
    return f(x1) + f(x2)


@jax.custom_vjp
def _hsum(x):
    return _hsum_raw(x)


_hsum.defvjp(lambda x: (_hsum_raw(x), None), lambda _, d: (_hsum(d),))


def _sigmoid(x):
    return 1.0 / (1.0 + jnp.exp(-x))


def _softplus(x):
    return jnp.maximum(x, 0.0) + jnp.log(1.0 + jnp.exp(-jnp.abs(x)))


def _params(sem):
    return pltpu.CompilerParams(dimension_semantics=sem, vmem_limit_bytes=VMEM_LIMIT)


def _rowwise(name, fn, rows, consts, row_outs, acc_outs=(), tr=ROW_BLOCK, halo=False, gather=()):
    rows = [r if isinstance(r, tuple) else (r, r.shape[1], 0) for r in rows]
    t_len = rows[0][0].shape[0]
    tr = min(tr, t_len)
    n_r, n_c, n_o, n_a, n_x = len(rows), len(consts), len(row_outs), len(acc_outs), len(gather)
    n_h = n_r if halo else 0
    sub = 8
    x_specs, x_shapes, x_sems = _exchange_io(gather, [False] * n_x) if n_x else ([], [], [])
    nb = t_len // tr

    def body(*refs):
        if n_x:
            n_in = n_r + n_h + n_c
            start, forward, wait = _gather_plan(refs[n_in:n_in + n_x], refs[len(refs) - 3 - n_x:len(refs) - 3], *refs[len(refs) - 3:])
            pl.when(pl.program_id(0) == 0)(start)
            refs = refs[:n_in] + refs[n_in + n_x:len(refs) - 3 - n_x]
        ins = [r[...] for r in refs[:n_r]]
        ins += [jnp.where(pl.program_id(0) == 0, 0.0, r[sub - 1:sub, :]) for r in refs[n_r:n_r + n_h]]
        ins += [r[...] for r in refs[n_r + n_h:n_r + n_h + n_c]]
        refs = refs[:n_r] + refs[n_r + n_h:]
        outs = fn(*ins)
        o_refs = refs[n_r + n_c:n_r + n_c + n_o]
        a_refs = refs[n_r + n_c + n_o:]
        for o_ref, val in zip(o_refs, outs[:n_o]):
            o_ref[...] = val.astype(o_ref.dtype)
        if n_a:
            first = pl.program_id(0) == 0

            @pl.when(first)
            def _():
                for a_ref, val in zip(a_refs, outs[n_o:]):
                    a_ref[...] = val

            @pl.when(jnp.logical_not(first))
            def _():
                for a_ref, val in zip(a_refs, outs[n_o:]):
                    a_ref[...] += val

        if n_x:
            @pl.when(pl.program_id(0) == nb - 1)
            def _():
                for j in range(n_x):
                    forward(j)
                wait()

    in_specs = [pl.BlockSpec((tr, w), functools.partial(lambda i, c: (i, c), c=cb)) for _, w, cb in rows]
    if halo:
        in_specs += [pl.BlockSpec((sub, w), functools.partial(lambda i, c: (jnp.maximum(i * (tr // sub) - 1, 0), c), c=cb))
                     for _, w, cb in rows]
    in_specs += [pl.BlockSpec(c.shape, functools.partial(lambda i, n: (0,) * n, n=c.ndim)) for c in consts]
    out_specs = [pl.BlockSpec((tr, w), lambda i: (i, 0)) for w, _ in row_outs]
    out_specs += [pl.BlockSpec(s, functools.partial(lambda i, n: (0,) * n, n=len(s))) for s in acc_outs]
    out_shape = [jax.ShapeDtypeStruct((t_len, w), dt) for w, dt in row_outs]
    out_shape += [jax.ShapeDtypeStruct(s, F32) for s in acc_outs]
    return pl.pallas_call(
        body, name=name, grid=(nb,), in_specs=in_specs + x_specs, out_specs=out_specs + x_specs,
        out_shape=out_shape + x_shapes, scratch_shapes=x_sems,
        compiler_params=pltpu.CompilerParams(dimension_semantics=("arbitrary",), vmem_limit_bytes=VMEM_LIMIT,
                                             has_side_effects=bool(n_x)),
    )(*[r[0] for r in rows], *([r[0] for r in rows] if halo else []), *consts, *gather)


def _colwise(name, fn, n_blocks, cols, prms, col_outs, prm_outs=()):
    t_len = cols[0][0].shape[0]
    n_i = len(cols) + len(prms)

    def body(*refs):
        outs = fn(*[r[...] for r in refs[:n_i]])
        for o_ref, val in zip(refs[n_i:], outs):
            o_ref[...] = val.astype(o_ref.dtype)

    spec = lambda r, w: pl.BlockSpec((r, w), lambda j: (0, j))
    in_specs = [spec(t_len, w) for _, w in cols] + [spec(a.shape[0], LANE) for a in prms]
    out_specs = [spec(t_len, bw) for _, _, bw in col_outs] + [spec(r, LANE) for r, _ in prm_outs]
    out_shape = [jax.ShapeDtypeStruct((t_len, w), dt) for w, dt, _ in col_outs]
    out_shape += [jax.ShapeDtypeStruct((r, w), F32) for r, w in prm_outs]
    return pl.pallas_call(
        body, name=name, grid=(n_blocks,), in_specs=in_specs, out_specs=out_specs, out_shape=out_shape,
        compiler_params=_params(("arbitrary",)),
    )(*[c[0] for c in cols], *prms)


def _matmul(name, a, b, dn, outs, *, tm, tn, tk, extras=(), consts=(), epilogue=None, sums=(), xch=(), xch_scatter=()):
    if dn == "nn":
        (m, k), n = a.shape, b.shape[1]
    elif dn == "nt":
        (m, k), n = a.shape, b.shape[0]
    else:
        (k, m), n = a.shape, b.shape[1]
    tm, tn, tk = min(tm, m), min(tn, n), min(tk, k)
    nk = k // tk
    grid = (m // tm, n // tn, nk)
    assert not sums or (grid[1] == 1 and nk == 1)
    a_spec = pl.BlockSpec((tk, tm), lambda i, j, q: (q, i)) if dn == "tn" else pl.BlockSpec((tm, tk), lambda i, j, q: (i, q))
    b_spec = pl.BlockSpec((tn, tk), lambda i, j, q: (j, q)) if dn == "nt" else pl.BlockSpec((tk, tn), lambda i, j, q: (q, j))
    o_spec = pl.BlockSpec((tm, tn), lambda i, j, q: (i, j))
    c_spec = pl.BlockSpec((1, tn), lambda i, j, q: (0, j))
    n_e, n_c, n_o, n_s, n_x = len(extras), len(consts), len(outs), len(sums), len(xch)
    x_specs, x_shapes, x_sems = _exchange_io(xch, xch_scatter) if n_x else ([], [], [])

    def body(*refs):
        a_ref, b_ref = refs[:2]
        e_refs = refs[2:2 + n_e + n_c]
        x_in = refs[2 + n_e + n_c:2 + n_e + n_c + n_x]
        rest = refs[2 + n_e + n_c + n_x:]
        o_refs, s_refs, x_out, scratch = rest[:n_o], rest[n_o:n_o + n_s], rest[n_o + n_s:n_o + n_s + n_x], rest[n_o + n_s + n_x:]
        step = (pl.program_id(0) * grid[1] + pl.program_id(1)) * nk + pl.program_id(2)
        if n_x:
            start, wait = _exchange_plan(x_in, x_out, xch_scatter, *scratch[len(scratch) - 3:])
            pl.when(step == 0)(start)
        part = lax.dot_general(a_ref[...].astype(BF16), b_ref[...].astype(BF16), _dims(dn, 2), preferred_element_type=F32)

        def finish(acc):
            vals = (acc,) if epilogue is None else epilogue(acc, *[e[...] for e in e_refs])
            for o_ref, val in zip(o_refs, vals[:n_o]):
                o_ref[...] = val.astype(o_ref.dtype)
            if n_s:
                @pl.when(step == 0)
                def _():
                    for s_ref, val in zip(s_refs, vals[n_o:]):
                        s_ref[...] = val

                @pl.when(step > 0)
                def _():
                    for s_ref, val in zip(s_refs, vals[n_o:]):
                        s_ref[...] += val

        if nk == 1:
            finish(part)
        else:
            acc_ref = scratch[0]
            q = pl.program_id(2)

            @pl.when(q == 0)
            def _():
                acc_ref[...] = part

            @pl.when(q > 0)
            def _():
                acc_ref[...] += part

            @pl.when(q == nk - 1)
            def _():
                finish(acc_ref[...])

        if n_x:
            pl.when(step == grid[0] * grid[1] * nk - 1)(wait)

    plain = not (n_s or n_x)
    res = pl.pallas_call(
        body, name=name, grid=grid,
        in_specs=[a_spec, b_spec] + [o_spec] * n_e + [c_spec] * n_c + x_specs,
        out_specs=[o_spec] * n_o + [c_spec] * n_s + x_specs,
        out_shape=[jax.ShapeDtypeStruct((m, n), dt) for dt in outs] + [jax.ShapeDtypeStruct(s, F32) for s in sums] + x_shapes,
        scratch_shapes=([pltpu.VMEM((tm, tn), F32)] if nk > 1 else []) + x_sems,
        compiler_params=pltpu.CompilerParams(
            dimension_semantics=("parallel", "parallel", "arbitrary") if plain else ("arbitrary",) * 3,
            vmem_limit_bytes=VMEM_LIMIT, has_side_effects=bool(n_x)),
    )(a, b, *extras, *consts, *xch)
    return res[0] if len(res) == 1 else res


def _rms(h, g):
    return h * lax.rsqrt(jnp.mean(h * h, axis=-1, keepdims=True) + RMS_EPS) * g


def _rms_bwd(h, g, dy):
    rs = lax.rsqrt(jnp.mean(h * h, axis=-1, keepdims=True) + RMS_EPS)
    n = h * rs
    dn = dy * g
    dh = rs * (dn - n * jnp.mean(dn * n, axis=-1, keepdims=True))
    return dh, jnp.sum(dy * n, axis=0, keepdims=True)


def _rwkv_pre(k, xw, xa, xg, w0, a0, k_k, k_a, wl, al, gl):
    zw = w0 + _mm(jnp.tanh(xw), wl, "nn", 1)
    lw = -jnp.exp(-_softplus(-zw) - 0.5)
    iclr = _sigmoid(a0 + _mm(xa, al, "nn", 1))
    g = _mm(_sigmoid(xg), gl, "nn", 1)
    kk0 = k * k_k
    kk = kk0 / jnp.maximum(jnp.sqrt(_hsum(kk0 * kk0)), L2_EPS)
    k_h = k * (1.0 + (iclr - 1.0) * k_a)
    return lw, k_h, -kk, kk * iclr, g


def _rwkv_post(y, r, k_h, v, g, ln_g, ln_b, r_k):
    mu = _hsum(y) * (1.0 / HEAD_DIM)
    yc = y - mu
    var = _hsum(yc * yc) * (1.0 / HEAD_DIM)
    yo = yc * lax.rsqrt(var + GN_EPS) * ln_g + ln_b
    bonus = _hsum(r * k_h * r_k) * v
    return (yo + bonus) * g


def _shift_down(x, n):
    rows = lax.broadcasted_iota(jnp.int32, x.shape, 0)
    return jnp.where(rows < n, 0.0, pltpu.roll(x, n, 0))


def _shift_up(x, n):
    t_len = x.shape[0]
    rows = lax.broadcasted_iota(jnp.int32, x.shape, 0)
    return jnp.where(rows >= t_len - n, 0.0, pltpu.roll(x, t_len - n, 0))


def _exchange_plan(ins, outs, scatter, send_sems, recv_sems, local_sems):
    x, y, c = lax.axis_index("x"), lax.axis_index("y"), lax.axis_index("c")
    me = 4 * x + 2 * y + c

    def local(i):
        return pltpu.make_async_copy(ins[i].at[me] if scatter[i] else ins[i], outs[i].at[me], local_sems.at[i])

    def send(i, rel):
        return pltpu.make_async_remote_copy(
            src_ref=ins[i].at[me ^ rel] if scatter[i] else ins[i], dst_ref=outs[i].at[me],
            send_sem=send_sems.at[i, rel - 1], recv_sem=recv_sems.at[i, rel - 1],
            device_id=(x ^ (rel >> 2), y ^ ((rel >> 1) & 1), c ^ (rel & 1)), device_id_type=pl.DeviceIdType.MESH)

    def landed(i, rel):
        slot = outs[i].at[me ^ rel]
        return pltpu.make_async_remote_copy(
            src_ref=slot, dst_ref=slot, send_sem=send_sems.at[i, rel - 1], recv_sem=recv_sems.at[i, rel - 1],
            device_id=(x, y, c), device_id_type=pl.DeviceIdType.MESH)

    def start():
        for i in range(len(ins)):
            local(i).start()
            for rel in range(1, N_DEV):
                send(i, rel).start()

    def wait():
        for i in range(len(ins)):
            local(i).wait()
            for rel in range(1, N_DEV):
                landed(i, rel).wait_recv()
            for rel in range(1, N_DEV):
                send(i, rel).wait_send()

    return start, wait


def _gather_plan(ins, outs, send_sems, recv_sems, local_sems):
    x, y, c = lax.axis_index("x"), lax.axis_index("y"), lax.axis_index("c")
    me = 4 * x + 2 * y + c
    direct, chips = (1, 2, 4, 6), (2, 4, 6)

    def local(i):
        return pltpu.make_async_copy(ins[i], outs[i].at[me], local_sems.at[i])

    def send(i, rel):
        return pltpu.make_async_remote_copy(
            src_ref=ins[i], dst_ref=outs[i].at[me], send_sem=send_sems.at[i, rel - 1], recv_sem=recv_sems.at[i, rel - 1],
            device_id=(x ^ (rel >> 2), y ^ ((rel >> 1) & 1), c ^ (rel & 1)), device_id_type=pl.DeviceIdType.MESH)

    def passed(i, rel):
        slot = outs[i].at[me ^ rel]
        return pltpu.make_async_remote_copy(
            src_ref=slot, dst_ref=slot, send_sem=send_sems.at[i, rel], recv_sem=recv_sems.at[i, rel],
            device_id=(x, y, 1 - c), device_id_type=pl.DeviceIdType.MESH)

    def landed(i, rel):
        slot = outs[i].at[me ^ rel]
        return pltpu.make_async_remote_copy(
            src_ref=slot, dst_ref=slot, send_sem=send_sems.at[i, rel - 1], recv_sem=recv_sems.at[i, rel - 1],
            device_id=(x, y, c), device_id_type=pl.DeviceIdType.MESH)

    def start():
        for i in range(len(ins)):
            local(i).start()
            for rel in direct:
                send(i, rel).start()

    def forward(i):
        for rel in chips:
            landed(i, rel).wait_recv()
            passed(i, rel).start()

    def wait():
        for i in range(len(ins)):
            local(i).wait()
            for rel in (1, 3, 5, 7):
                landed(i, rel).wait_recv()
            for rel in direct:
                send(i, rel).wait_send()
            for rel in chips:
                passed(i, rel).wait_send()

    return start, forward, wait


def _exchange_io(arrays, scatter):
    n = len(arrays)
    any_spec = pl.BlockSpec(memory_space=pl.ANY)
    out_shape = [jax.ShapeDtypeStruct(a.shape if sc else (N_DEV,) + a.shape, a.dtype) for a, sc in zip(arrays, scatter)]
    sems = [pltpu.SemaphoreType.DMA((n, N_DEV - 1)), pltpu.SemaphoreType.DMA((n, N_DEV - 1)), pltpu.SemaphoreType.DMA((n,))]
    return [any_spec] * n, out_shape, sems


def _exchange(name, arrays, scatter):
    n = len(arrays)
    specs, out_shape, sems = _exchange_io(arrays, scatter)

    def body(*refs):
        if any(scatter):
            start, wait = _exchange_plan(refs[:n], refs[n:2 * n], scatter, *refs[2 * n:])
            start()
        else:
            start, forward, wait = _gather_plan(refs[:n], refs[n:2 * n], *refs[2 * n:])
            start()
            for i in range(n):
                forward(i)
        wait()

    return pl.pallas_call(
        body, name=name, in_specs=specs, out_specs=specs, out_shape=out_shape, scratch_shapes=sems,
        compiler_params=pltpu.CompilerParams(has_side_effects=True),
    )(*arrays)


def _tri_powers(low):
    powers, n = [low], 1
    while 2 * n < low.shape[-1]:
        powers.append(_mm(powers[-1], powers[-1], "nn", REC_PASSES))
        n *= 2
    return powers


@jax.custom_vjp
def _tri_solve(low, rhs):
    for p in _tri_powers(low):
        rhs = rhs + _mm(p, rhs, "nn", REC_PASSES)
    return rhs


def _tri_solve_fwd(low, rhs):
    powers = _tri_powers(low)
    for p in powers:
        rhs = rhs + _mm(p, rhs, "nn", REC_PASSES)
    return rhs, (powers, rhs)


def _tri_solve_bwd(res, d):
    powers, u = res
    for p in powers:
        d = d + _mm(p, d, "tn", REC_PASSES)
    return _mm(d, u, "nt", REC_PASSES), d


_tri_solve.defvjp(_tri_solve_fwd, _tri_solve_bwd)


def _chunk_fwd(z0, r, lw, k, v, a, b):
    n_h, c, n_k = r.shape
    mm = functools.partial(_mm, passes=REC_PASSES)
    gram = functools.partial(_mm, passes=3)
    ti = lax.broadcasted_iota(jnp.int32, (c, c), 0)
    si = lax.broadcasted_iota(jnp.int32, (c, c), 1)
    strict, incl = si < ti, si <= ti
    cum = _mm(jnp.broadcast_to(incl.astype(F32), (n_h, c, c)), lw, "nn", 3)
    cum_end = cum[:, c - 1:c, :]
    e_neg, e_end = jnp.exp(-cum), jnp.exp(cum_end - cum)
    x2 = jnp.concatenate([a * jnp.exp(cum - lw), r * jnp.exp(cum)], axis=1)
    y2 = jnp.concatenate([b * e_neg, k * e_neg], axis=1)
    mask = jnp.concatenate([jnp.concatenate([strict, strict], axis=1), jnp.concatenate([incl, incl], axis=1)], axis=0)
    g2 = jnp.where(mask, gram(x2, y2, "nt"), 0.0)
    t2 = mm(x2, z0, "nn") + mm(g2[:, :, c:], v, "nn")
    u = _tri_solve(g2[:, :c, :c], t2[:, :c])
    y = t2[:, c:] + mm(g2[:, c:, :c], u, "nn")
    ki = lax.broadcasted_iota(jnp.int32, (n_k, n_k), 0)
    kj = lax.broadcasted_iota(jnp.int32, (n_k, n_k), 1)
    dmat = jnp.where(ki == kj, jnp.broadcast_to(jnp.exp(cum_end), (n_h, n_k, n_k)), 0.0)
    z_end = mm(dmat, z0, "nn") + mm(jnp.concatenate([b * e_end, k * e_end], axis=1), jnp.concatenate([u, v], axis=1), "tn")
    return y, z_end


def _heads(x):
    return jnp.stack([x[:, h * HEAD_DIM:(h + 1) * HEAD_DIM] for h in range(N_HEADS)])


def _unheads(x):
    return jnp.concatenate([x[h] for h in range(N_HEADS)], axis=-1)


def _rec_params():
    return pltpu.CompilerParams(dimension_semantics=("arbitrary",), vmem_limit_bytes=VMEM_LIMIT, has_side_effects=True)


def _rec_fwd(u, lw, k, a, b, xch):
    t_len = lw.shape[0]
    c = min(REC_CHUNK, t_len)
    nc = t_len // c
    n_x = len(xch)
    x_specs, x_shapes, x_sems = _exchange_io(xch, [False] * n_x)
    sizes = [a_.size * a_.dtype.itemsize for a_ in xch]
    pass_step = [min(nc - 1, int(0.9 * nc * sum(sizes[:j + 1]) / sum(sizes)) + 1) for j in range(n_x)]

    def body(*refs):
        r_ref, v_ref, lw_ref, k_ref, a_ref, b_ref = refs[:6]
        x_in = refs[6:6 + n_x]
        y_ref, zs_ref = refs[6 + n_x:8 + n_x]
        x_out = refs[8 + n_x:8 + 2 * n_x]
        z_scr = refs[8 + 2 * n_x]
        start, forward, wait = _gather_plan(x_in, x_out, *refs[9 + 2 * n_x:])
        i = pl.program_id(0)

        @pl.when(i == 0)
        def _():
            start()
            z_scr[...] = jnp.zeros_like(z_scr)

        z0 = z_scr[...]
        zs_ref[0] = z0
        y, z_end = _chunk_fwd(z0, _heads(r_ref[...]), _heads(lw_ref[...]), _heads(k_ref[...]), _heads(v_ref[...]),
                              _heads(a_ref[...]), _heads(b_ref[...]))
        y_ref[...] = _unheads(y)
        z_scr[...] = z_end

        for j in range(n_x):
            pl.when(i == pass_step[j])(functools.partial(forward, j))

        @pl.when(i == nc - 1)
        def _():
            wait()

    blk = lambda cb: pl.BlockSpec((c, RWKV_DIM), functools.partial(lambda i, q: (i, q), q=cb))
    res = pl.pallas_call(
        body, name="rwkv_rec_fwd", grid=(nc,),
        in_specs=[blk(0), blk(2)] + [blk(0)] * 4 + x_specs,
        out_specs=[blk(0), pl.BlockSpec((1, N_HEADS, HEAD_DIM, HEAD_DIM), lambda i: (i, 0, 0, 0))] + x_specs,
        out_shape=[jax.ShapeDtypeStruct((t_len, RWKV_DIM), F32),
                   jax.ShapeDtypeStruct((nc, N_HEADS, HEAD_DIM, HEAD_DIM), F32)] + x_shapes,
        scratch_shapes=[pltpu.VMEM((N_HEADS, HEAD_DIM, HEAD_DIM), F32)] + x_sems,
        compiler_params=_rec_params(),
    )(u, u, lw, k, a, b, *xch)
    return res[0], res[1], res[2:]


def _rec_bwd(u, lw, k, a, b, zs, dy, xch, xch_scatter):
    t_len = lw.shape[0]
    c = min(REC_CHUNK, t_len)
    nc = t_len // c
    n_x = len(xch)
    x_specs, x_shapes, x_sems = _exchange_io(xch, xch_scatter)

    def body(*refs):
        r_ref, v_ref, lw_ref, k_ref, a_ref, b_ref, zs_ref, dy_ref = refs[:8]
        x_in = refs[8:8 + n_x]
        g_refs = refs[8 + n_x:14 + n_x]
        x_out = refs[14 + n_x:14 + 2 * n_x]
        dz_scr = refs[14 + 2 * n_x]
        start, wait = _exchange_plan(x_in, x_out, xch_scatter, *refs[15 + 2 * n_x:])
        i = pl.program_id(0)

        @pl.when(i == 0)
        def _():
            start()
            dz_scr[...] = jnp.zeros_like(dz_scr)

        _, vjp = jax.vjp(_chunk_fwd, zs_ref[0], _heads(r_ref[...]), _heads(lw_ref[...]), _heads(k_ref[...]),
                         _heads(v_ref[...]), _heads(a_ref[...]), _heads(b_ref[...]))
        dz0, dr, dlw, dk, dv, da, db = vjp((_heads(dy_ref[...]), dz_scr[...]))
        for ref, val in zip(g_refs, (dr, dv, dlw, dk, da, db)):
            ref[...] = _unheads(val)
        dz_scr[...] = dz0

        @pl.when(i == nc - 1)
        def _():
            wait()

    blk = lambda cb: pl.BlockSpec((c, RWKV_DIM), functools.partial(lambda i, q: (nc - 1 - i, q), q=cb))
    res = pl.pallas_call(
        body, name="rwkv_rec_bwd", grid=(nc,),
        in_specs=[blk(0), blk(2)] + [blk(0)] * 4
                 + [pl.BlockSpec((1, N_HEADS, HEAD_DIM, HEAD_DIM), lambda i: (nc - 1 - i, 0, 0, 0)), blk(0)] + x_specs,
        out_specs=[blk(0)] * 6 + x_specs,
        out_shape=[jax.ShapeDtypeStruct((t_len, RWKV_DIM), F32)] * 6 + x_shapes,
        scratch_shapes=[pltpu.VMEM((N_HEADS, HEAD_DIM, HEAD_DIM), F32)] + x_sems,
        compiler_params=_rec_params(),
    )(u, u, lw, k, a, b, zs, dy, *xch)
    return res[:6], res[6:]


_EARLY = ["w_in", "conv_w", "w_lora_up", "a_lora_up", "g_lora_up"]
_LATE = ["w_out", "w_up", "w_down", "w_ple_gate", "w_ple_proj"]
_SHARDED = _EARLY + _LATE
_COL_SHARDED = {"w_in", "conv_w", "w_lora_up", "a_lora_up", "g_lora_up", "w_up", "w_ple_proj"}
_BF16_GATHER = {"w_in", "w_out", "w_up", "w_down", "w_ple_gate", "w_ple_proj"}
_REPLICATED = ["norm_mix_g", "shift_mu", "w0", "a0", "k_k", "k_a", "r_k", "ln_x_g", "ln_x_b", "norm_mlp_g", "norm_ple_g",
               "norm_final_g"]
_WEIGHTS = ["norm_mix_g", "w_in", "conv_w", "shift_mu", "w_lora_up", "w0", "a_lora_up", "a0", "g_lora_up", "k_k", "k_a", "r_k",
            "ln_x_g", "ln_x_b", "w_out", "norm_mlp_g", "w_up", "w_down", "norm_ple_g", "w_ple_gate", "w_ple_proj", "norm_final_g"]
_PACK_ROWS = 80


def _unshard(name, g):
    if name in _COL_SHARDED:
        return jnp.moveaxis(g, 0, 1).reshape(g.shape[1], N_DEV * g.shape[2])
    return g.reshape(N_DEV * g.shape[1], g.shape[2])


def _reshard(name, full):
    if name in _COL_SHARDED:
        return jnp.moveaxis(full.reshape(full.shape[0], N_DEV, full.shape[1] // N_DEV), 1, 0)
    return full.reshape(N_DEV, full.shape[0] // N_DEV, full.shape[1])


def _pad_in_cols(a):
    z = lambda n: jnp.zeros(a.shape[:-1] + (n,), a.dtype)
    conv = [a[..., part * CONV_DIM + j * LANE:part * CONV_DIM + (j + 1) * LANE] for j in range(CONV_DIM // LANE) for part in range(3)]
    return jnp.concatenate(conv + [a[..., CONV_COLS:3136], z(64), a[..., 3136:3200], z(64), a[..., 3200:3360], z(96)], axis=-1)


def _unpad_in_cols(a):
    conv = [a[..., (3 * j + part) * LANE:(3 * j + part + 1) * LANE] for part in range(3) for j in range(CONV_DIM // LANE)]
    return jnp.concatenate(conv + [a[..., CONV_COLS:3136], a[..., 3200:3264], a[..., 3328:3488]], axis=-1)


def _assemble_w_in(g):
    n_dev, rows, cols = g.shape

    def body(g_ref, o_ref):
        o_ref[...] = _pad_in_cols(jnp.concatenate([g_ref[d] for d in range(n_dev)], axis=1))

    return pl.pallas_call(
        body, name="w_in_assemble", grid=(rows // ROW_BLOCK,),
        in_specs=[pl.BlockSpec((n_dev, ROW_BLOCK, cols), lambda i: (0, i, 0))],
        out_specs=pl.BlockSpec((ROW_BLOCK, IN_PAD), lambda i: (i, 0)),
        out_shape=jax.ShapeDtypeStruct((rows, IN_PAD), g.dtype), compiler_params=_params(("arbitrary",)),
    )(g)


def _split_w_in_grad(dw):
    rows = dw.shape[0]
    cols = IN_COLS // N_DEV

    def body(d_ref, o_ref):
        full = _unpad_in_cols(d_ref[...])
        for d in range(N_DEV):
            o_ref[d] = full[:, cols * d:cols * (d + 1)]

    return pl.pallas_call(
        body, name="w_in_grad_split", grid=(rows // ROW_BLOCK,),
        in_specs=[pl.BlockSpec((ROW_BLOCK, IN_PAD), lambda i: (i, 0))],
        out_specs=pl.BlockSpec((N_DEV, ROW_BLOCK, cols), lambda i: (0, i, 0)),
        out_shape=jax.ShapeDtypeStruct((N_DEV, rows, cols), dw.dtype), compiler_params=_params(("arbitrary",)),
    )(dw)


def _pad_rows(a, rows):
    return jnp.concatenate([a, jnp.zeros((rows - a.shape[0],) + a.shape[1:], a.dtype)], axis=0)


def _pack(vals):
    flat = jnp.concatenate([v.reshape(-1) for v in vals])
    return jnp.concatenate([flat, jnp.zeros((_PACK_ROWS * LANE - flat.shape[0],), F32)]).reshape(_PACK_ROWS, LANE)


SEG_W = [RWKV_DIM, RWKV_DIM, RWKV_DIM, LANE, LANE, 2 * LANE]
SEG_OFF = [0, 512, 1024, XW_OFF, XA_OFF, XG_OFF]


def _rwkv_pre_bwd(proj, u, grads, mu, small, dproj):
    t_len = u.shape[0]
    tr = min(ROW_BLOCK, t_len)
    nb = t_len // tr
    sub = 8
    n_g = len(grads)
    acc_shapes = [(1, RW_PAD)] + [(1, RWKV_DIM)] * 4 + [(LANE, RWKV_DIM), (LANE, RWKV_DIM), (2 * LANE, RWKV_DIM)]

    def body(*refs):
        seg_refs, halo_refs = refs[:6], refs[6:12]
        k_ref, xw_ref, xa_ref, xg_ref = refs[12:16]
        g_refs = refs[16:16 + n_g]
        mu_ref = refs[16 + n_g]
        prm_refs = refs[17 + n_g:24 + n_g]
        out_hbm = refs[25 + n_g]
        acc_refs = refs[26 + n_g:26 + n_g + len(acc_shapes)]
        vbuf, sems, carry = refs[26 + n_g + len(acc_shapes):]
        i = pl.program_id(0)
        blk = nb - 1 - i
        dr1, dr2, dv1, dv2, dlw, dk1, dk2, da, db, dg = [g[...] for g in g_refs]
        _, vjp = jax.vjp(_rwkv_pre, k_ref[...], xw_ref[...], xa_ref[...], xg_ref[...], *[p_[...] for p_ in prm_refs])
        dk, dxw, dxa, dxg, *dprm = vjp((dlw, dk1 + dk2, da, db, dg))
        du = jnp.concatenate([dr1 + dr2, dk, dv1 + dv2, dxw, dxa, dxg], axis=1)
        mu_v = mu_ref[...]

        @pl.when(i == 0)
        def _():
            carry[...] = jnp.zeros_like(carry)

        rows = lax.broadcasted_iota(jnp.int32, du.shape, 0)
        nxt = jnp.where(rows == tr - 1, carry[...], pltpu.roll(du, tr - 1, 0))
        d_rw = du - mu_v * du + mu_v * nxt
        d_mu = []
        for s_ref, h_ref, off, wd in zip(seg_refs, halo_refs, SEG_OFF, SEG_W):
            cur = s_ref[...]
            r0 = lax.broadcasted_iota(jnp.int32, cur.shape, 0)
            prev = jnp.where(r0 == 0, jnp.where(blk == 0, 0.0, h_ref[sub - 1:sub, :]), pltpu.roll(cur, 1, 0))
            d_mu.append(jnp.sum(du[:, off:off + wd] * (prev - cur), axis=0, keepdims=True))
        sums = [jnp.concatenate(d_mu, axis=1)] + list(dprm)

        @pl.when(i == 0)
        def _():
            for a_ref, val in zip(acc_refs, sums):
                a_ref[...] = val

        @pl.when(i > 0)
        def _():
            for a_ref, val in zip(acc_refs, sums):
                a_ref[...] += val

        carry[...] = du[0:1, :]
        slot = i % 2

        def writeback(s, b):
            return pltpu.make_async_copy(vbuf.at[s], out_hbm.at[pl.ds(b * tr, tr), pl.ds(CONV_COLS, RW_PAD)], sems.at[s])

        @pl.when(i >= 2)
        def _():
            writeback(slot, blk + 2).wait()

        vbuf[slot] = d_rw.astype(vbuf.dtype)
        writeback(slot, blk).start()

        @pl.when(i == nb - 1)
        def _():
            writeback(slot, blk).wait()
            if nb > 1:
                writeback(1 - slot, blk + 1).wait()

    rev = lambda w_, cb: pl.BlockSpec((tr, w_), functools.partial(lambda i, c: (nb - 1 - i, c), c=cb))
    halo = lambda w_, cb: pl.BlockSpec((sub, w_), functools.partial(
        lambda i, c: (jnp.maximum((nb - 1 - i) * (tr // sub) - 1, 0), c), c=cb))
    whole = lambda a: pl.BlockSpec(a.shape, functools.partial(lambda i, n: (0,) * n, n=a.ndim))
    segs = [(wd, (CONV_COLS + off) // wd) for off, wd in zip(SEG_OFF, SEG_W)]
    u_cols = [(512, 1), (LANE, XW_OFF // LANE), (LANE, XA_OFF // LANE), (2 * LANE, XG_OFF // (2 * LANE))]
    any_spec = pl.BlockSpec(memory_space=pl.ANY)
    res = pl.pallas_call(
        body, name="rwkv_pre_bwd", grid=(nb,),
        in_specs=[rev(*s) for s in segs] + [halo(*s) for s in segs] + [rev(*c) for c in u_cols]
                 + [rev(RWKV_DIM, 0)] * n_g + [whole(mu)] + [whole(p_) for p_ in small] + [any_spec],
        out_specs=[any_spec] + [pl.BlockSpec(s, functools.partial(lambda i, n: (0,) * n, n=len(s))) for s in acc_shapes],
        out_shape=[jax.ShapeDtypeStruct(dproj.shape, dproj.dtype)] + [jax.ShapeDtypeStruct(s, F32) for s in acc_shapes],
        scratch_shapes=[pltpu.VMEM((2, tr, RW_PAD), dproj.dtype), pltpu.SemaphoreType.DMA((2,)), pltpu.VMEM((1, RW_PAD), F32)],
        input_output_aliases={24 + n_g: 0},
        compiler_params=_params(("arbitrary",)),
    )(*[proj] * 12, *[u] * 4, *grads, mu, *small, dproj)
    return res


def _local_step(x, p, tgt, w, early_shards, late_shards):
    row = lambda v: v.reshape(1, -1)
    w = dict(w)

    xn1, *gathered = _rowwise("rms_mix", lambda h, g: (_rms(h, g),), [x], [w["norm_mix_g"]], [(D_MODEL, BF16)],
                              gather=early_shards)
    w.update({n: _unshard(n, g_) for n, g_ in zip(_EARLY[1:], gathered[1:])})
    w["w_in"] = _assemble_w_in(gathered[0])
    w["w_lora_up"] = _pad_rows(w["w_lora_up"], LANE)
    w["a_lora_up"] = _pad_rows(w["a_lora_up"], LANE)
    w["g_lora_up"] = _pad_rows(w["g_lora_up"], 2 * LANE)
    proj =_matmul("in_proj", xn1, w["w_in"], "nn", [F32], tm=1024, tn=512, tk=D_MODEL)
    n_cb = CONV_DIM // LANE

    def conv_fwd(blk, cw):
        gb, gc, hx = blk[:, :LANE], blk[:, LANE:2 * LANE], blk[:, 2 * LANE:]
        uu = gc * hx
        return (gb * (uu * cw[2:3] + _shift_down(uu, 1) * cw[1:2] + _shift_down(uu, 2) * cw[0:1]),)

    (y_conv,) = _colwise("conv_fwd", conv_fwd, n_cb, [(proj, 3 * LANE)], [w["conv_w"]], [(CONV_DIM, BF16, LANE)])

    small = [w["w0"], w["a0"], w["k_k"], w["k_a"], w["w_lora_up"], w["a_lora_up"], w["g_lora_up"]]
    def pre_fwd(*xs):
        cur, prev_rows, mu, prm = xs[:6], xs[6:12], xs[12], xs[13:]
        segs = []
        for c_, p_, off, wd in zip(cur, prev_rows, SEG_OFF, SEG_W):
            rows = lax.broadcasted_iota(jnp.int32, c_.shape, 0)
            prev = jnp.where(rows == 0, p_, pltpu.roll(c_, 1, 0))
            segs.append(c_ + mu[:, off:off + wd] * (prev - c_))
        return (jnp.concatenate(segs, axis=1),) + tuple(_rwkv_pre(segs[1], segs[3], segs[4], segs[5], *prm))

    proj_segs = [(proj, wd, (CONV_COLS + off) // wd) for off, wd in zip(SEG_OFF, SEG_W)]
    u, lw, k_h, ra, rb, g = _rowwise("rwkv_pre", pre_fwd, proj_segs, [w["shift_mu"]] + small,
                                     [(RW_PAD, F32)] + [(RWKV_DIM, F32)] * 5, halo=True)
    u_k, u_xw, u_xa, u_xg = (u, 512, 1), (u, LANE, XW_OFF // LANE), (u, LANE, XA_OFF // LANE), (u, 2 * LANE, XG_OFF // (2 * LANE))
    y_rec, zs, late = _rec_fwd(u, lw, k_h, ra, rb, late_shards)
    for n, gathered in zip(_LATE, late):
        w[n] = _unshard(n, gathered)
    post_c = [w["ln_x_g"], w["ln_x_b"], w["r_k"]]
    u_r, u_v = (u, 512, 0), (u, 512, 2)
    (y_rwkv,) = _rowwise("rwkv_post", lambda *xs: (_rwkv_post(*xs),), [y_rec, u_r, k_h, u_v, g], post_c, [(RWKV_DIM, BF16)])
    ycat = jnp.concatenate([y_conv, y_rwkv], axis=1)
    def res_norm(acc, r_, g_):
        h = acc + r_
        return h, _rms(h, g_)

    h1, xn2 = _matmul("out_proj", ycat, w["w_out"], "nn", [F32, BF16], tm=1024, tn=D_MODEL, tk=D_MODEL, extras=[x],
                      consts=[w["norm_mlp_g"]], epilogue=res_norm)

    def relu2(acc):
        hid = jnp.maximum(acc, 0.0)
        return hid, hid * hid

    hid, hsq = _matmul("mlp_up", xn2, w["w_up"], "nn", [BF16, BF16], tm=1024, tn=1024, tk=D_MODEL, epilogue=relu2)
    h2, xn3 = _matmul("mlp_down", hsq, w["w_down"], "nn", [F32, BF16], tm=512, tn=D_MODEL, tk=D_FF, extras=[h1],
                      consts=[w["norm_ple_g"]], epilogue=res_norm)
    zg = _matmul("ple_gate", xn3, w["w_ple_gate"], "nn", [F32], tm=1024, tn=1024, tk=D_MODEL)
    pp = _matmul("ple_proj", p, w["w_ple_proj"], "nn", [F32], tm=1024, tn=1024, tk=PLE_DIM)

    def head(h2_, zg_, pp_, tg, gf):
        gate = _sigmoid(zg_)
        h3 = h2_ + gate * pp_
        out = _rms(h3, gf)
        err = out - tg
        dh3, dgf = _rms_bwd(h3, gf, err * (1.0 / D_MODEL))
        loss = jnp.sum(jnp.sum(err * err, axis=1, keepdims=True), axis=0, keepdims=True) * (0.5 / D_MODEL)
        return dh3, dh3 * pp_ * gate * (1.0 - gate), dh3 * gate, dgf, loss

    dh3, dzg, dpp, d_norm_final, loss = _rowwise(
        "head", head, [h2, zg, pp, tgt], [row(w["norm_final_g"])], [(D_MODEL, F32), (D_MODEL, BF16), (D_MODEL, BF16)],
        [(1, D_MODEL), (1, 1)])

    d_w_ple_proj = _matmul("d_ple_proj", p, dpp, "tn", [BF16], tm=PLE_DIM, tn=1024, tk=4096)
    d_w_ple_gate = _matmul("d_ple_gate", xn3, dzg, "tn", [BF16], tm=512, tn=1024, tk=4096)

    def norm_bwd(dxn, h, dres, g_):
        dh, dg = _rms_bwd(h, g_, dxn)
        dh = dh + dres
        return dh, dh, dg

    nb = dict(tm=512, tn=D_MODEL, epilogue=norm_bwd, sums=[(1, D_MODEL)])
    dh2, dh2_b, d_norm_ple = _matmul("dx_ple_gate", dzg, w["w_ple_gate"], "nt", [F32, BF16], tk=D_MODEL,
                                     extras=[h2, dh3], consts=[w["norm_ple_g"]], **nb)
    d_w_down = _matmul("d_mlp_down", hsq, dh2_b, "tn", [BF16], tm=512, tn=1024, tk=4096)
    dpre = _matmul("dx_mlp_down", dh2_b, w["w_down"], "nt", [BF16], tm=1024, tn=1024, tk=D_MODEL, extras=[hid],
                   epilogue=lambda acc, hid_: (acc * (2.0 * hid_.astype(F32)),))
    d_w_up = _matmul("d_mlp_up", xn2, dpre, "tn", [BF16], tm=512, tn=1024, tk=4096)
    dh1, dh1_b, d_norm_mlp = _matmul("dx_mlp_up", dpre, w["w_up"], "nt", [F32, BF16], tk=D_FF,
                                     extras=[h1, dh2], consts=[w["norm_mlp_g"]], **nb)
    d_w_out = _matmul("d_out_proj", ycat, dh1_b, "tn", [BF16], tm=512, tn=1024, tk=4096)
    dycat = _matmul("dx_out_proj", dh1_b, w["w_out"], "nt", [F32], tm=1024, tn=1024, tk=D_MODEL)
    late_grads = dict(w_out=d_w_out, w_up=d_w_up, w_down=d_w_down, w_ple_gate=d_w_ple_gate, w_ple_proj=d_w_ple_proj)

    def conv_bwd(dy, blk, cw):
        gb, gc, hx = blk[:, :LANE], blk[:, LANE:2 * LANE], blk[:, 2 * LANE:]
        uu = gc * hx
        u1, u2 = _shift_down(uu, 1), _shift_down(uu, 2)
        dconv = dy * gb
        du = dconv * cw[2:3] + _shift_up(dconv, 1) * cw[1:2] + _shift_up(dconv, 2) * cw[0:1]
        s = lambda z: jnp.sum(z, axis=0, keepdims=True)
        d_blk = jnp.concatenate([dy * (uu * cw[2:3] + u1 * cw[1:2] + u2 * cw[0:1]), du * hx, du * gc], axis=1)
        return d_blk, s(dconv * u2), s(dconv * u1), s(dconv * uu)

    dproj, dcw0, dcw1, dcw2 = _colwise(
        "conv_bwd", conv_bwd, n_cb, [(dycat, LANE), (proj, 3 * LANE)], [w["conv_w"]],
        [(IN_PAD, BF16, 3 * LANE)], [(1, CONV_DIM)] * 3)

    def post_bwd(dy, y, r, k_h_, v, g_, ln_g, ln_b, r_k):
        _, vjp = jax.vjp(_rwkv_post, y, r, k_h_, v, g_, ln_g, ln_b, r_k)
        return vjp(dy)

    dy_rec, dr_p, dk_p, dv_p, dg, d_ln_g, d_ln_b, d_r_k = _rowwise(
        "rwkv_post_bwd", post_bwd, [(dycat, 512, 1), y_rec, u_r, k_h, u_v, g], post_c,
        [(RWKV_DIM, F32)] * 5, [(1, RWKV_DIM)] * 3)
    (dr_r, dv_r, dlw, dk_r, da, db), late_parts = _rec_bwd(
        u, lw, k_h, ra, rb, zs, dy_rec, [_reshard(n, late_grads[n]) for n in _LATE], [True] * len(_LATE))

    dproj, d_mu, d_w0, d_a0, d_k_k, d_k_a, d_wl, d_al, d_gl = _rwkv_pre_bwd(
        proj, u, [dr_p, dr_r, dv_p, dv_r, dlw, dk_p, dk_r, da, db, dg], w["shift_mu"], small, dproj)
    d_w_in = _matmul("d_in_proj", xn1, dproj, "tn", [BF16], tm=512, tn=896, tk=4096)
    early_grads = dict(conv_w=jnp.concatenate([dcw0, dcw1, dcw2], axis=0),
                       w_lora_up=d_wl[:64], a_lora_up=d_al[:64], g_lora_up=d_gl[:160])
    early_send = [_split_w_in_grad(d_w_in)] + [_reshard(n, early_grads[n]) for n in _EARLY[1:]]
    dx, d_norm_mix, *early_parts = _matmul(
        "dx_in_proj", dproj, w["w_in"], "nt", [F32], tk=IN_PAD, extras=[x, dh1], consts=[w["norm_mix_g"]],
        xch=early_send, xch_scatter=[True] * len(_EARLY), **dict(nb, epilogue=lambda *a: norm_bwd(*a)[1:]))

    grads = dict(
        norm_mix_g=d_norm_mix, shift_mu=d_mu, w0=d_w0, a0=d_a0, k_k=d_k_k, k_a=d_k_a, r_k=d_r_k,
        ln_x_g=d_ln_g, ln_x_b=d_ln_b, norm_mlp_g=d_norm_mlp, norm_ple_g=d_norm_ple, norm_final_g=d_norm_final)
    parts = dict(zip(_EARLY, early_parts))
    parts.update(zip(_LATE, late_parts))
    return loss, dx, grads, parts


def _adamw(name, parts, w, m, v):
    rows, cols = w.shape[-2:]
    lead = w.ndim - 2
    tr = rows if rows * cols * 4 * 8 <= (4 << 20) else max(8, (4 << 20) // (cols * 4 * 8) // 8 * 8)
    while rows % tr:
        tr -= 8

    def body(p_ref, w_ref, m_ref, v_ref, g_ref, d_ref, nm_ref, nv_ref):
        g = p_ref[0].astype(F32)
        for s in range(1, N_DEV):
            g = g + p_ref[s].astype(F32)
        nm = ADAM_B1 * m_ref[...] + (1.0 - ADAM_B1) * g
        nv = ADAM_B2 * v_ref[...] + (1.0 - ADAM_B2) * (g * g)
        m_hat = nm / (1.0 - ADAM_B1 ** ADAM_STEP)
        v_hat = nv / (1.0 - ADAM_B2 ** ADAM_STEP)
        g_ref[...] = g
        d_ref[...] = -ADAM_LR * (m_hat / (jnp.sqrt(v_hat) + ADAM_EPS) + ADAM_WD * w_ref[...])
        nm_ref[...] = nm
        nv_ref[...] = nv

    blk = pl.BlockSpec((None,) * lead + (tr, cols), lambda i: (0,) * lead + (i, 0))
    return pl.pallas_call(
        body, name=name, grid=(rows // tr,),
        in_specs=[pl.BlockSpec((N_DEV, tr, cols), lambda i: (0, i, 0)), blk, blk, blk], out_specs=[blk] * 4,
        out_shape=[jax.ShapeDtypeStruct(w.shape, F32)] * 4,
        compiler_params=_params(("arbitrary",)),
    )(parts, w, m, v)


def kernel(x, p, norm_mix_g, w_in, conv_w, shift_mu, w_lora_up, w0, a_lora_up, a0, g_lora_up, k_k, k_a, r_k, ln_x_g, ln_x_b, w_out, norm_mlp_g, w_up, w_down, norm_ple_g, w_ple_gate, w_ple_proj, norm_final_g, loss_target, m_norm_mix_g, m_w_in, m_conv_w, m_shift_mu, m_w_lora_up, m_w0, m_a_lora_up, m_a0, m_g_lora_up, m_k_k, m_k_a, m_r_k, m_ln_x_g, m_ln_x_b, m_w_out, m_norm_mlp_g, m_w_up, m_w_down, m_norm_ple_g, m_w_ple_gate, m_w_ple_proj, m_norm_final_g, v_norm_mix_g, v_w_in, v_conv_w, v_shift_mu, v_w_lora_up, v_w0, v_a_lora_up, v_a0, v_g_lora_up, v_k_k, v_k_a, v_r_k, v_ln_x_g, v_ln_x_b, v_w_out, v_norm_mlp_g, v_w_up, v_w_down, v_norm_ple_g, v_w_ple_gate, v_w_ple_proj, v_norm_final_g):
    args = dict(locals())
    wts = {n: args[n] for n in _WEIGHTS}
    mom = {n: args["m_" + n] for n in _WEIGHTS}
    var = {n: args["v_" + n] for n in _WEIGHTS}
    shard2d = lambda a: a.reshape(a.shape[-2:])
    pad_mu = lambda a: _pad_in_cols(jnp.concatenate([jnp.zeros((1, CONV_COLS), F32), a], axis=1))[:, CONV_COLS:]
    unpad_mu = lambda a: _unpad_in_cols(jnp.concatenate([jnp.zeros((1, CONV_COLS), F32), a], axis=1))[:, CONV_COLS:]

    shards = {n: shard2d(wts[n]).astype(BF16 if n in _BF16_GATHER else F32) for n in _SHARDED}
    w = {n: wts[n].reshape(1, -1) for n in _REPLICATED}
    w["shift_mu"] = pad_mu(wts["shift_mu"])

    loss, dx, grads, parts = _local_step(x[0], p[0, 0], loss_target[0], w, [shards[n] for n in _EARLY],
                                         [shards[n] for n in _LATE])

    grads["shift_mu"] = unpad_mu(grads["shift_mu"])
    (small_parts,) = _exchange("gather_small", [_pack([grads[n] for n in _REPLICATED] + [loss])], [False])

    out = {}
    for n in _SHARDED:
        out[n] = _adamw("adamw_" + n, parts[n], wts[n], mom[n], var[n])
    sm = _adamw("adamw_small", small_parts, _pack([wts[n] for n in _REPLICATED]), _pack([mom[n] for n in _REPLICATED]),
                _pack([var[n] for n in _REPLICATED]))
    off = 0
    for n in _REPLICATED:
        size = wts[n].size
        out[n] = [r.reshape(-1)[off:off + size].reshape(wts[n].shape) for r in sm]
        off += size
    loss_total = sm[0].reshape(-1)[off]
    return (loss_total, dx[None], *[out[n][0] for n in _WEIGHTS], *[out[n][1] for n in _WEIGHTS],
            *[out[n][2] for n in _WEIGHTS], *[out[n][3] for n in _WEIGHTS])
```

```python
import functools

import jax
import jax.numpy as jnp
from jax import lax
from jax.experimental import pallas as pl
from jax.experimental.pallas import tpu as pltpu

F32 = jnp.float32
BF16 = jnp.bfloat16

N_DEV = 8
D_MODEL = 1024
CONV_DIM = 512
RWKV_DIM = 512
HEAD_DIM = 64
N_HEADS = 8
D_FF = 4096
PLE_DIM = 256
RMS_EPS = 1e-6
GN_EPS = 64e-5
L2_EPS = 1e-12
ADAM_LR, ADAM_B1, ADAM_B2, ADAM_EPS, ADAM_WD, ADAM_STEP = 0.001, 0.9, 0.999, 1e-08, 0.01, 10

CONV_COLS = 3 * CONV_DIM
RW_PAD = 2048
IN_PAD = CONV_COLS + RW_PAD
IN_COLS = 3360
XW_OFF, XA_OFF, XG_OFF = 1536, 1664, 1792
REC_CHUNK = 128
REC_PASSES = 1
ROW_BLOCK = 256
LANE = 128
VMEM_LIMIT = 56 * 1024 * 1024


def _dims(dn, ndim):
    if ndim == 3:
        return {"nn": (((2,), (1,)), ((0,), (0,))), "nt": (((2,), (2,)), ((0,), (0,))),
                "tn": (((1,), (1,)), ((0,), (0,)))}[dn]
    return {"nn": (((1,), (0,)), ((), ())), "nt": (((1,), (1,)), ((), ())), "tn": (((0,), (0,)), ((), ()))}[dn]


def _split2(x):
    hi = x.astype(BF16)
    return hi, (x - hi.astype(F32)).astype(BF16)


def _mm_raw(x, y, dn, passes):
    f = lambda p, q: lax.dot_general(p, q, _dims(dn, x.ndim), preferred_element_type=F32)
    if passes == 1:
        return f(x.astype(BF16), y.astype(BF16))
    xh, xl = _split2(x)
    yh, yl = _split2(y)
    return f(xh, yh) + f(xh, yl) + f(xl, yh)


@functools.partial(jax.custom_vjp, nondiff_argnums=(2, 3))
def _mm(x, y, dn, passes):
    return _mm_raw(x, y, dn, passes)


def _mm_fwd(x, y, dn, passes):
    return _mm_raw(x, y, dn, passes), (x, y)


def _mm_bwd(dn, passes, res, d):
    x, y = res
    if dn == "nn":
        return _mm(d, y, "nt", passes), _mm(x, d, "tn", passes)
    if dn == "nt":
        return _mm(d, y, "nn", passes), _mm(d, x, "tn", passes)
    return _mm(y, d, "nt", passes), _mm(x, d, "nn", passes)


_mm.defvjp(_mm_fwd, _mm_bwd)


def _head_ones():
    i = lax.broadcasted_iota(jnp.int32, (RWKV_DIM, RWKV_DIM), 0) // HEAD_DIM
    j = lax.broadcasted_iota(jnp.int32, (RWKV_DIM, RWKV_DIM), 1) // HEAD_DIM
    return (i == j).astype(BF16)


def _hsum_raw(x):
    ones = _head_ones()
    f = lambda p: lax.dot_general(p, ones, _dims("nn", 2), preferred_element_type=F32)
    x1, x2 = _split2(x)
    return f(x1) + f(x2)


@jax.custom_vjp
def _hsum(x):
    return _hsum_raw(x)


_hsum.defvjp(lambda x: (_hsum_raw(x), None), lambda _, d: (_hsum(d),))


def _sigmoid(x):
    return 1.0 / (1.0 + jnp.exp(-x))


def _softplus(x):
    return jnp.maximum(x, 0.0) + jnp.log(1.0 + jnp.exp(-jnp.abs(x)))


def _params(sem):
    return pltpu.CompilerParams(dimension_semantics=sem, vmem_limit_bytes=VMEM_LIMIT)


def _rowwise(name, fn, rows, consts, row_outs, acc_outs=(), tr=ROW_BLOCK, halo=False, gather=()):
    rows = [r if isinstance(r, tuple) else (r, r.shape[1], 0) for r in rows]
    t_len = rows[0][0].shape[0]
    tr = min(tr, t_len)
    n_r, n_c, n_o, n_a, n_x = len(rows), len(consts), len(row_outs), len(acc_outs), len(gather)
    n_h = n_r if halo else 0
    sub = 8
    x_specs, x_shapes, x_sems = _exchange_io(gather, [False] * n_x) if n_x else ([], [], [])
    nb = t_len // tr

    def body(*refs):
        if n_x:
            n_in = n_r + n_h + n_c
            start, forward, wait = _gather_plan(refs[n_in:n_in + n_x], refs[len(refs) - 3 - n_x:len(refs) - 3], *refs[len(refs) - 3:])
            pl.when(pl.program_id(0) == 0)(start)
            refs = refs[:n_in] + refs[n_in + n_x:len(refs) - 3 - n_x]
        ins = [r[...] for r in refs[:n_r]]
        ins += [jnp.where(pl.program_id(0) == 0, 0.0, r[sub - 1:sub, :]) for r in refs[n_r:n_r + n_h]]
        ins += [r[...] for r in refs[n_r + n_h:n_r + n_h + n_c]]
        refs = refs[:n_r] + refs[n_r + n_h:]
        outs = fn(*ins)
        o_refs = refs[n_r + n_c:n_r + n_c + n_o]
        a_refs = refs[n_r + n_c + n_o:]
        for o_ref, val in zip(o_refs, outs[:n_o]):
            o_ref[...] = val.astype(o_ref.dtype)
        if n_a:
            first = pl.program_id(0) == 0

            @pl.when(first)
            def _():
                for a_ref, val in zip(a_refs, outs[n_o:]):
                    a_ref[...] = val

            @pl.when(jnp.logical_not(first))
            def _():
                for a_ref, val in zip(a_refs, outs[n_o:]):
                    a_ref[...] += val

        if n_x:
            @pl.when(pl.program_id(0) == nb - 1)
            def _():
                for j in range(n_x):
                    forward(j)
                wait()

    in_specs = [pl.BlockSpec((tr, w), functools.partial(lambda i, c: (i, c), c=cb)) for _, w, cb in rows]
    if halo:
        in_specs += [pl.BlockSpec((sub, w), functools.partial(lambda i, c: (jnp.maximum(i * (tr // sub) - 1, 0), c), c=cb))
                     for _, w, cb in rows]
    in_specs += [pl.BlockSpec(c.shape, functools.partial(lambda i, n: (0,) * n, n=c.ndim)) for c in consts]
    out_specs = [pl.BlockSpec((tr, w), lambda i: (i, 0)) for w, _ in row_outs]
    out_specs += [pl.BlockSpec(s, functools.partial(lambda i, n: (0,) * n, n=len(s))) for s in acc_outs]
    out_shape = [jax.ShapeDtypeStruct((t_len, w), dt) for w, dt in row_outs]
    out_shape += [jax.ShapeDtypeStruct(s, F32) for s in acc_outs]
    return pl.pallas_call(
        body, name=name, grid=(nb,), in_specs=in_specs + x_specs, out_specs=out_specs + x_specs,
        out_shape=out_shape + x_shapes, scratch_shapes=x_sems,
        compiler_params=pltpu.CompilerParams(dimension_semantics=("arbitrary",), vmem_limit_bytes=VMEM_LIMIT,
                                             has_side_effects=bool(n_x)),
    )(*[r[0] for r in rows], *([r[0] for r in rows] if halo else []), *consts, *gather)


def _colwise(name, fn, n_blocks, cols, prms, col_outs, prm_outs=()):
    t_len = cols[0][0].shape[0]
    n_i = len(cols) + len(prms)

    def body(*refs):
        outs = fn(*[r[...] for r in refs[:n_i]])
        for o_ref, val in zip(refs[n_i:], outs):
            o_ref[...] = val.astype(o_ref.dtype)

    spec = lambda r, w: pl.BlockSpec((r, w), lambda j: (0, j))
    in_specs = [spec(t_len, w) for _, w in cols] + [spec(a.shape[0], LANE) for a in prms]
    out_specs = [spec(t_len, bw) for _, _, bw in col_outs] + [spec(r, LANE) for r, _ in prm_outs]
    out_shape = [jax.ShapeDtypeStruct((t_len, w), dt) for w, dt, _ in col_outs]
    out_shape += [jax.ShapeDtypeStruct((r, w), F32) for r, w in prm_outs]
    return pl.pallas_call(
        body, name=name, grid=(n_blocks,), in_specs=in_specs, out_specs=out_specs, out_shape=out_shape,
        compiler_params=_params(("arbitrary",)),
    )(*[c[0] for c in cols], *prms)


def _matmul(name, a, b, dn, outs, *, tm, tn, tk, extras=(), consts=(), epilogue=None, sums=(), xch=(), xch_scatter=(),
            a_map=None):
    if dn == "nn":
        (m, k), n = a.shape, b.shape[1]
    elif dn == "nt":
        (m, k), n = a.shape, b.shape[0]
    else:
        (k, m), n = a.shape, b.shape[1]
    tm, tn, tk = min(tm, m), min(tn, n), min(tk, k)
    nk = k // tk
    grid = (m // tm, n // tn, nk)
    assert nk == 1 and (not sums or grid[1] == 1)
    a_spec = pl.BlockSpec((tk, tm), lambda i, j, q: (q, i)) if dn == "tn" else pl.BlockSpec((tm, tk), lambda i, j, q: (i, q))
    b_spec = pl.BlockSpec((tn, tk), lambda i, j, q: (j, q)) if dn == "nt" else pl.BlockSpec((tk, tn), lambda i, j, q: (q, j))
    o_spec = pl.BlockSpec((tm, tn), lambda i, j, q: (i, j))
    c_spec = pl.BlockSpec((1, tn), lambda i, j, q: (0, j))
    n_e, n_c, n_o, n_s, n_x = len(extras), len(consts), len(outs), len(sums), len(xch)
    x_specs, x_shapes, x_sems = _exchange_io(xch, xch_scatter) if n_x else ([], [], [])

    def body(*refs):
        a_ref, b_ref = refs[:2]
        e_refs = refs[2:2 + n_e + n_c]
        x_in = refs[2 + n_e + n_c:2 + n_e + n_c + n_x]
        rest = refs[2 + n_e + n_c + n_x:]
        o_refs, s_refs, x_out, scratch = rest[:n_o], rest[n_o:n_o + n_s], rest[n_o + n_s:n_o + n_s + n_x], rest[n_o + n_s + n_x:]
        step = (pl.program_id(0) * grid[1] + pl.program_id(1)) * nk + pl.program_id(2)
        if n_x:
            start, wait = _exchange_plan(x_in, x_out, xch_scatter, *scratch[len(scratch) - 3:])
            pl.when(step == 0)(start)
        a_blk = a_ref[...] if a_map is None else a_map(a_ref[...])
        acc = lax.dot_general(a_blk.astype(BF16), b_ref[...].astype(BF16), _dims(dn, 2), preferred_element_type=F32)
        vals = (acc,) if epilogue is None else epilogue(acc, *[e[...] for e in e_refs])
        for o_ref, val in zip(o_refs, vals[:n_o]):
            o_ref[...] = val.astype(o_ref.dtype)
        if n_s:
            @pl.when(step == 0)
            def _():
                for s_ref, val in zip(s_refs, vals[n_o:]):
                    s_ref[...] = val

            @pl.when(step > 0)
            def _():
                for s_ref, val in zip(s_refs, vals[n_o:]):
                    s_ref[...] += val

        if n_x:
            pl.when(step == grid[0] * grid[1] * nk - 1)(wait)

    plain = not (n_s or n_x)
    res = pl.pallas_call(
        body, name=name, grid=grid,
        in_specs=[a_spec, b_spec] + [o_spec] * n_e + [c_spec] * n_c + x_specs,
        out_specs=[o_spec] * n_o + [c_spec] * n_s + x_specs,
        out_shape=[jax.ShapeDtypeStruct((m, n), dt) for dt in outs] + [jax.ShapeDtypeStruct(s, F32) for s in sums] + x_shapes,
        scratch_shapes=x_sems,
        compiler_params=pltpu.CompilerParams(
            dimension_semantics=("parallel", "parallel", "arbitrary") if plain else ("arbitrary",) * 3,
            vmem_limit_bytes=VMEM_LIMIT, has_side_effects=bool(n_x)),
    )(a, b, *extras, *consts, *xch)
    return res[0] if len(res) == 1 else res


def _rms(h, g):
    return h * lax.rsqrt(jnp.mean(h * h, axis=-1, keepdims=True) + RMS_EPS) * g


def _rms_bwd(h, g, dy):
    rs = lax.rsqrt(jnp.mean(h * h, axis=-1, keepdims=True) + RMS_EPS)
    n = h * rs
    dn = dy * g
    dh = rs * (dn - n * jnp.mean(dn * n, axis=-1, keepdims=True))
    return dh, jnp.sum(dy * n, axis=0, keepdims=True)


def _rwkv_pre(k, xw, xa, xg, w0, a0, k_k, k_a, wl, al, gl):
    zw = w0 + _mm(jnp.tanh(xw), wl, "nn", 1)
    lw = -jnp.exp(-_softplus(-zw) - 0.5)
    iclr = _sigmoid(a0 + _mm(xa, al, "nn", 1))
    g = _mm(_sigmoid(xg), gl, "nn", 1)
    kk0 = k * k_k
    kk = kk0 / jnp.maximum(jnp.sqrt(_hsum(kk0 * kk0)), L2_EPS)
    k_h = k * (1.0 + (iclr - 1.0) * k_a)
    return lw, k_h, -kk, kk * iclr, g


def _rwkv_post(y, r, k_h, v, g, ln_g, ln_b, r_k):
    mu = _hsum(y) * (1.0 / HEAD_DIM)
    yc = y - mu
    var = _hsum(yc * yc) * (1.0 / HEAD_DIM)
    yo = yc * lax.rsqrt(var + GN_EPS) * ln_g + ln_b
    bonus = _hsum(r * k_h * r_k) * v
    return (yo + bonus) * g


def _shift_down(x, n):
    rows = lax.broadcasted_iota(jnp.int32, x.shape, 0)
    return jnp.where(rows < n, 0.0, pltpu.roll(x, n, 0))


def _shift_up(x, n):
    t_len = x.shape[0]
    rows = lax.broadcasted_iota(jnp.int32, x.shape, 0)
    return jnp.where(rows >= t_len - n, 0.0, pltpu.roll(x, t_len - n, 0))


def _exchange_plan(ins, outs, scatter, send_sems, recv_sems, local_sems):
    x, y, c = lax.axis_index("x"), lax.axis_index("y"), lax.axis_index("c")
    me = 4 * x + 2 * y + c

    def local(i):
        return pltpu.make_async_copy(ins[i].at[me] if scatter[i] else ins[i], outs[i].at[me], local_sems.at[i])

    def send(i, rel):
        return pltpu.make_async_remote_copy(
            src_ref=ins[i].at[me ^ rel] if scatter[i] else ins[i], dst_ref=outs[i].at[me],
            send_sem=send_sems.at[i, rel - 1], recv_sem=recv_sems.at[i, rel - 1],
            device_id=(x ^ (rel >> 2), y ^ ((rel >> 1) & 1), c ^ (rel & 1)), device_id_type=pl.DeviceIdType.MESH)

    def landed(i, rel):
        slot = outs[i].at[me ^ rel]
        return pltpu.make_async_remote_copy(
            src_ref=slot, dst_ref=slot, send_sem=send_sems.at[i, rel - 1], recv_sem=recv_sems.at[i, rel - 1],
            device_id=(x, y, c), device_id_type=pl.DeviceIdType.MESH)

    def start():
        for i in range(len(ins)):
            local(i).start()
            for rel in range(1, N_DEV):
                send(i, rel).start()

    def wait():
        for i in range(len(ins)):
            local(i).wait()
            for rel in range(1, N_DEV):
                landed(i, rel).wait_recv()
            for rel in range(1, N_DEV):
                send(i, rel).wait_send()

    return start, wait


def _gather_plan(ins, outs, send_sems, recv_sems, local_sems):
    x, y, c = lax.axis_index("x"), lax.axis_index("y"), lax.axis_index("c")
    me = 4 * x + 2 * y + c
    direct, chips = (1, 2, 4, 6), (2, 4, 6)

    def local(i):
        return pltpu.make_async_copy(ins[i], outs[i].at[me], local_sems.at[i])

    def send(i, rel):
        return pltpu.make_async_remote_copy(
            src_ref=ins[i], dst_ref=outs[i].at[me], send_sem=send_sems.at[i, rel - 1], recv_sem=recv_sems.at[i, rel - 1],
            device_id=(x ^ (rel >> 2), y ^ ((rel >> 1) & 1), c ^ (rel & 1)), device_id_type=pl.DeviceIdType.MESH)

    def passed(i, rel):
        slot = outs[i].at[me ^ rel]
        return pltpu.make_async_remote_copy(
            src_ref=slot, dst_ref=slot, send_sem=send_sems.at[i, rel], recv_sem=recv_sems.at[i, rel],
            device_id=(x, y, 1 - c), device_id_type=pl.DeviceIdType.MESH)

    def landed(i, rel):
        slot = outs[i].at[me ^ rel]
        return pltpu.make_async_remote_copy(
            src_ref=slot, dst_ref=slot, send_sem=send_sems.at[i, rel - 1], recv_sem=recv_sems.at[i, rel - 1],
            device_id=(x, y, c), device_id_type=pl.DeviceIdType.MESH)

    def start():
        for i in range(len(ins)):
            local(i).start()
            for rel in direct:
                send(i, rel).start()

    def forward(i):
        for rel in chips:
            landed(i, rel).wait_recv()
            passed(i, rel).start()

    def wait():
        for i in range(len(ins)):
            local(i).wait()
            for rel in (1, 3, 5, 7):
                landed(i, rel).wait_recv()
            for rel in direct:
                send(i, rel).wait_send()
            for rel in chips:
                passed(i, rel).wait_send()

    return start, forward, wait


def _exchange_io(arrays, scatter):
    n = len(arrays)
    any_spec = pl.BlockSpec(memory_space=pl.ANY)
    out_shape = [jax.ShapeDtypeStruct(a.shape if sc else (N_DEV,) + a.shape, a.dtype) for a, sc in zip(arrays, scatter)]
    sems = [pltpu.SemaphoreType.DMA((n, N_DEV - 1)), pltpu.SemaphoreType.DMA((n, N_DEV - 1)), pltpu.SemaphoreType.DMA((n,))]
    return [any_spec] * n, out_shape, sems


def _exchange(name, arrays, scatter):
    n = len(arrays)
    specs, out_shape, sems = _exchange_io(arrays, scatter)

    def body(*refs):
        if any(scatter):
            start, wait = _exchange_plan(refs[:n], refs[n:2 * n], scatter, *refs[2 * n:])
            start()
        else:
            start, forward, wait = _gather_plan(refs[:n], refs[n:2 * n], *refs[2 * n:])
            start()
            for i in range(n):
                forward(i)
        wait()

    return pl.pallas_call(
        body, name=name, in_specs=specs, out_specs=specs, out_shape=out_shape, scratch_shapes=sems,
        compiler_params=pltpu.CompilerParams(has_side_effects=True),
    )(*arrays)


def _tri_powers(low):
    powers, n = [low], 1
    while 2 * n < low.shape[-1]:
        powers.append(_mm(powers[-1], powers[-1], "nn", REC_PASSES))
        n *= 2
    return powers


@jax.custom_vjp
def _tri_solve(low, rhs):
    for p in _tri_powers(low):
        rhs = rhs + _mm(p, rhs, "nn", REC_PASSES)
    return rhs


def _tri_solve_fwd(low, rhs):
    powers = _tri_powers(low)
    for p in powers:
        rhs = rhs + _mm(p, rhs, "nn", REC_PASSES)
    return rhs, (powers, rhs)


def _tri_solve_bwd(res, d):
    powers, u = res
    for p in powers:
        d = d + _mm(p, d, "tn", REC_PASSES)
    return _mm(d, u, "nt", REC_PASSES), d


_tri_solve.defvjp(_tri_solve_fwd, _tri_solve_bwd)


def _chunk_fwd(z0, r, lw, k, v, a, b):
    n_h, c, n_k = r.shape
    mm = functools.partial(_mm, passes=REC_PASSES)
    gram = functools.partial(_mm, passes=3)
    ti = lax.broadcasted_iota(jnp.int32, (c, c), 0)
    si = lax.broadcasted_iota(jnp.int32, (c, c), 1)
    strict, incl = si < ti, si <= ti
    cum = _mm(jnp.broadcast_to(incl.astype(F32), (n_h, c, c)), lw, "nn", 3)
    cum_end = cum[:, c - 1:c, :]
    e_neg, e_end = jnp.exp(-cum), jnp.exp(cum_end - cum)
    x2 = jnp.concatenate([a * jnp.exp(cum - lw), r * jnp.exp(cum)], axis=1)
    y2 = jnp.concatenate([b * e_neg, k * e_neg], axis=1)
    mask = jnp.concatenate([jnp.concatenate([strict, strict], axis=1), jnp.concatenate([incl, incl], axis=1)], axis=0)
    g2 = jnp.where(mask, gram(x2, y2, "nt"), 0.0)
    t2 = mm(x2, z0, "nn") + mm(g2[:, :, c:], v, "nn")
    u = _tri_solve(g2[:, :c, :c], t2[:, :c])
    y = t2[:, c:] + mm(g2[:, c:, :c], u, "nn")
    ki = lax.broadcasted_iota(jnp.int32, (n_k, n_k), 0)
    kj = lax.broadcasted_iota(jnp.int32, (n_k, n_k), 1)
    dmat = jnp.where(ki == kj, jnp.broadcast_to(jnp.exp(cum_end), (n_h, n_k, n_k)), 0.0)
    z_end = mm(dmat, z0, "nn") + mm(jnp.concatenate([b * e_end, k * e_end], axis=1), jnp.concatenate([u, v], axis=1), "tn")
    return y, z_end


def _heads(x):
    return jnp.stack([x[:, h * HEAD_DIM:(h + 1) * HEAD_DIM] for h in range(N_HEADS)])


def _unheads(x):
    return jnp.concatenate([x[h] for h in range(N_HEADS)], axis=-1)


def _rec_params():
    return pltpu.CompilerParams(dimension_semantics=("arbitrary",), vmem_limit_bytes=VMEM_LIMIT, has_side_effects=True)


def _rec_fwd(u, lw, k, a, b, xch):
    t_len = lw.shape[0]
    c = min(REC_CHUNK, t_len)
    nc = t_len // c
    n_x = len(xch)
    x_specs, x_shapes, x_sems = _exchange_io(xch, [False] * n_x)
    sizes = [a_.size * a_.dtype.itemsize for a_ in xch]
    pass_step = [min(nc - 1, int(0.9 * nc * sum(sizes[:j + 1]) / sum(sizes)) + 1) for j in range(n_x)]

    def body(*refs):
        r_ref, v_ref, lw_ref, k_ref, a_ref, b_ref = refs[:6]
        x_in = refs[6:6 + n_x]
        y_ref, zs_ref = refs[6 + n_x:8 + n_x]
        x_out = refs[8 + n_x:8 + 2 * n_x]
        z_scr = refs[8 + 2 * n_x]
        start, forward, wait = _gather_plan(x_in, x_out, *refs[9 + 2 * n_x:])
        i = pl.program_id(0)

        @pl.when(i == 0)
        def _():
            start()
            z_scr[...] = jnp.zeros_like(z_scr)

        z0 = z_scr[...]
        zs_ref[0] = z0
        y, z_end = _chunk_fwd(z0, _heads(r_ref[...]), _heads(lw_ref[...]), _heads(k_ref[...]), _heads(v_ref[...]),
                              _heads(a_ref[...]), _heads(b_ref[...]))
        y_ref[...] = _unheads(y)
        z_scr[...] = z_end

        for j in range(n_x):
            pl.when(i == pass_step[j])(functools.partial(forward, j))

        @pl.when(i == nc - 1)
        def _():
            wait()

    blk = lambda cb: pl.BlockSpec((c, RWKV_DIM), functools.partial(lambda i, q: (i, q), q=cb))
    res = pl.pallas_call(
        body, name="rwkv_rec_fwd", grid=(nc,),
        in_specs=[blk(0), blk(2)] + [blk(0)] * 4 + x_specs,
        out_specs=[blk(0), pl.BlockSpec((1, N_HEADS, HEAD_DIM, HEAD_DIM), lambda i: (i, 0, 0, 0))] + x_specs,
        out_shape=[jax.ShapeDtypeStruct((t_len, RWKV_DIM), F32),
                   jax.ShapeDtypeStruct((nc, N_HEADS, HEAD_DIM, HEAD_DIM), F32)] + x_shapes,
        scratch_shapes=[pltpu.VMEM((N_HEADS, HEAD_DIM, HEAD_DIM), F32)] + x_sems,
        compiler_params=_rec_params(),
    )(u, u, lw, k, a, b, *xch)
    return res[0], res[1], res[2:]


def _rec_bwd(u, lw, k, a, b, zs, dy, xch, xch_scatter):
    t_len = lw.shape[0]
    c = min(REC_CHUNK, t_len)
    nc = t_len // c
    n_x = len(xch)
    x_specs, x_shapes, x_sems = _exchange_io(xch, xch_scatter)

    def body(*refs):
        r_ref, v_ref, lw_ref, k_ref, a_ref, b_ref, zs_ref, dy_ref = refs[:8]
        x_in = refs[8:8 + n_x]
        g_refs = refs[8 + n_x:14 + n_x]
        x_out = refs[14 + n_x:14 + 2 * n_x]
        dz_scr = refs[14 + 2 * n_x]
        start, wait = _exchange_plan(x_in, x_out, xch_scatter, *refs[15 + 2 * n_x:])
        i = pl.program_id(0)

        @pl.when(i == 0)
        def _():
            start()
            dz_scr[...] = jnp.zeros_like(dz_scr)

        _, vjp = jax.vjp(_chunk_fwd, zs_ref[0], _heads(r_ref[...]), _heads(lw_ref[...]), _heads(k_ref[...]),
                         _heads(v_ref[...]), _heads(a_ref[...]), _heads(b_ref[...]))
        dz0, dr, dlw, dk, dv, da, db = vjp((_heads(dy_ref[...]), dz_scr[...]))
        for ref, val in zip(g_refs, (dr, dv, dlw, dk, da, db)):
            ref[...] = _unheads(val)
        dz_scr[...] = dz0

        @pl.when(i == nc - 1)
        def _():
            wait()

    blk = lambda cb: pl.BlockSpec((c, RWKV_DIM), functools.partial(lambda i, q: (nc - 1 - i, q), q=cb))
    res = pl.pallas_call(
        body, name="rwkv_rec_bwd", grid=(nc,),
        in_specs=[blk(0), blk(2)] + [blk(0)] * 4
                 + [pl.BlockSpec((1, N_HEADS, HEAD_DIM, HEAD_DIM), lambda i: (nc - 1 - i, 0, 0, 0)), blk(0)] + x_specs,
        out_specs=[blk(0)] * 6 + x_specs,
        out_shape=[jax.ShapeDtypeStruct((t_len, RWKV_DIM), F32)] * 6 + x_shapes,
        scratch_shapes=[pltpu.VMEM((N_HEADS, HEAD_DIM, HEAD_DIM), F32)] + x_sems,
        compiler_params=_rec_params(),
    )(u, u, lw, k, a, b, zs, dy, *xch)
    return res[:6], res[6:]


_EARLY = ["w_in", "conv_w", "w_lora_up", "a_lora_up", "g_lora_up"]
_LATE = ["w_out", "w_up", "w_down", "w_ple_gate", "w_ple_proj"]
_SHARDED = _EARLY + _LATE
_COL_SHARDED = {"w_in", "conv_w", "w_lora_up", "a_lora_up", "g_lora_up", "w_up", "w_ple_proj"}
_BF16_GATHER = {"w_in", "w_out", "w_up", "w_down", "w_ple_gate", "w_ple_proj"}
_REPLICATED = ["norm_mix_g", "shift_mu", "w0", "a0", "k_k", "k_a", "r_k", "ln_x_g", "ln_x_b", "norm_mlp_g", "norm_ple_g",
               "norm_final_g"]
_WEIGHTS = ["norm_mix_g", "w_in", "conv_w", "shift_mu", "w_lora_up", "w0", "a_lora_up", "a0", "g_lora_up", "k_k", "k_a", "r_k",
            "ln_x_g", "ln_x_b", "w_out", "norm_mlp_g", "w_up", "w_down", "norm_ple_g", "w_ple_gate", "w_ple_proj", "norm_final_g"]
_PACK_ROWS = 80


def _unshard(name, g):
    if name in _COL_SHARDED:
        return jnp.moveaxis(g, 0, 1).reshape(g.shape[1], N_DEV * g.shape[2])
    return g.reshape(N_DEV * g.shape[1], g.shape[2])


def _reshard(name, full):
    if name in _COL_SHARDED:
        return jnp.moveaxis(full.reshape(full.shape[0], N_DEV, full.shape[1] // N_DEV), 1, 0)
    return full.reshape(N_DEV, full.shape[0] // N_DEV, full.shape[1])


def _pad_in_cols(a):
    z = lambda n: jnp.zeros(a.shape[:-1] + (n,), a.dtype)
    conv = [a[..., part * CONV_DIM + j * LANE:part * CONV_DIM + (j + 1) * LANE] for j in range(CONV_DIM // LANE) for part in range(3)]
    return jnp.concatenate(conv + [a[..., CONV_COLS:3136], z(64), a[..., 3136:3200], z(64), a[..., 3200:3360], z(96)], axis=-1)


def _unpad_in_cols(a):
    conv = [a[..., (3 * j + part) * LANE:(3 * j + part + 1) * LANE] for part in range(3) for j in range(CONV_DIM // LANE)]
    return jnp.concatenate(conv + [a[..., CONV_COLS:3136], a[..., 3200:3264], a[..., 3328:3488]], axis=-1)


def _assemble_w_in(g):
    n_dev, rows, cols = g.shape

    def body(g_ref, o_ref):
        o_ref[...] = _pad_in_cols(jnp.concatenate([g_ref[d] for d in range(n_dev)], axis=1))

    return pl.pallas_call(
        body, name="w_in_assemble", grid=(rows // ROW_BLOCK,),
        in_specs=[pl.BlockSpec((n_dev, ROW_BLOCK, cols), lambda i: (0, i, 0))],
        out_specs=pl.BlockSpec((ROW_BLOCK, IN_PAD), lambda i: (i, 0)),
        out_shape=jax.ShapeDtypeStruct((rows, IN_PAD), g.dtype), compiler_params=_params(("arbitrary",)),
    )(g)


def _split_w_in_grad(dw):
    rows = dw.shape[0]
    cols = IN_COLS // N_DEV

    def body(d_ref, o_ref):
        full = _unpad_in_cols(d_ref[...])
        for d in range(N_DEV):
            o_ref[d] = full[:, cols * d:cols * (d + 1)]

    return pl.pallas_call(
        body, name="w_in_grad_split", grid=(rows // ROW_BLOCK,),
        in_specs=[pl.BlockSpec((ROW_BLOCK, IN_PAD), lambda i: (i, 0))],
        out_specs=pl.BlockSpec((N_DEV, ROW_BLOCK, cols), lambda i: (0, i, 0)),
        out_shape=jax.ShapeDtypeStruct((N_DEV, rows, cols), dw.dtype), compiler_params=_params(("arbitrary",)),
    )(dw)


def _pad_rows(a, rows):
    return jnp.concatenate([a, jnp.zeros((rows - a.shape[0],) + a.shape[1:], a.dtype)], axis=0)


def _pack(vals):
    flat = jnp.concatenate([v.reshape(-1) for v in vals])
    return jnp.concatenate([flat, jnp.zeros((_PACK_ROWS * LANE - flat.shape[0],), F32)]).reshape(_PACK_ROWS, LANE)


SEG_W = [RWKV_DIM, RWKV_DIM, RWKV_DIM, LANE, LANE, 2 * LANE]
SEG_OFF = [0, 512, 1024, XW_OFF, XA_OFF, XG_OFF]


def _rwkv_pre_bwd(proj, u, grads, mu, small, dproj):
    t_len = u.shape[0]
    tr = min(ROW_BLOCK, t_len)
    nb = t_len // tr
    sub = 8
    n_g = len(grads)
    acc_shapes = [(1, RW_PAD)] + [(1, RWKV_DIM)] * 4 + [(LANE, RWKV_DIM), (LANE, RWKV_DIM), (2 * LANE, RWKV_DIM)]

    def body(*refs):
        seg_refs, halo_refs = refs[:6], refs[6:12]
        k_ref, xw_ref, xa_ref, xg_ref = refs[12:16]
        g_refs = refs[16:16 + n_g]
        mu_ref = refs[16 + n_g]
        prm_refs = refs[17 + n_g:24 + n_g]
        out_hbm = refs[25 + n_g]
        acc_refs = refs[26 + n_g:26 + n_g + len(acc_shapes)]
        vbuf, sems, carry = refs[26 + n_g + len(acc_shapes):]
        i = pl.program_id(0)
        blk = nb - 1 - i
        dr1, dr2, dv1, dv2, dlw, dk1, dk2, da, db, dg = [g[...] for g in g_refs]
        _, vjp = jax.vjp(_rwkv_pre, k_ref[...], xw_ref[...], xa_ref[...], xg_ref[...], *[p_[...] for p_ in prm_refs])
        dk, dxw, dxa, dxg, *dprm = vjp((dlw, dk1 + dk2, da, db, dg))
        du = jnp.concatenate([dr1 + dr2, dk, dv1 + dv2, dxw, dxa, dxg], axis=1)
        mu_v = mu_ref[...]

        @pl.when(i == 0)
        def _():
            carry[...] = jnp.zeros_like(carry)

        rows = lax.broadcasted_iota(jnp.int32, du.shape, 0)
        nxt = jnp.where(rows == tr - 1, carry[...], pltpu.roll(du, tr - 1, 0))
        d_rw = du - mu_v * du + mu_v * nxt
        d_mu = []
        for s_ref, h_ref, off, wd in zip(seg_refs, halo_refs, SEG_OFF, SEG_W):
            cur = s_ref[...]
            r0 = lax.broadcasted_iota(jnp.int32, cur.shape, 0)
            prev = jnp.where(r0 == 0, jnp.where(blk == 0, 0.0, h_ref[sub - 1:sub, :]), pltpu.roll(cur, 1, 0))
            d_mu.append(jnp.sum(du[:, off:off + wd] * (prev - cur), axis=0, keepdims=True))
        sums = [jnp.concatenate(d_mu, axis=1)] + list(dprm)

        @pl.when(i == 0)
        def _():
            for a_ref, val in zip(acc_refs, sums):
                a_ref[...] = val

        @pl.when(i > 0)
        def _():
            for a_ref, val in zip(acc_refs, sums):
                a_ref[...] += val

        carry[...] = du[0:1, :]
        slot = i % 2

        def writeback(s, b):
            return pltpu.make_async_copy(vbuf.at[s], out_hbm.at[pl.ds(b * tr, tr), pl.ds(CONV_COLS, RW_PAD)], sems.at[s])

        @pl.when(i >= 2)
        def _():
            writeback(slot, blk + 2).wait()

        vbuf[slot] = d_rw.astype(vbuf.dtype)
        writeback(slot, blk).start()

        @pl.when(i == nb - 1)
        def _():
            writeback(slot, blk).wait()
            if nb > 1:
                writeback(1 - slot, blk + 1).wait()

    rev = lambda w_, cb: pl.BlockSpec((tr, w_), functools.partial(lambda i, c: (nb - 1 - i, c), c=cb))
    halo = lambda w_, cb: pl.BlockSpec((sub, w_), functools.partial(
        lambda i, c: (jnp.maximum((nb - 1 - i) * (tr // sub) - 1, 0), c), c=cb))
    whole = lambda a: pl.BlockSpec(a.shape, functools.partial(lambda i, n: (0,) * n, n=a.ndim))
    segs = [(wd, (CONV_COLS + off) // wd) for off, wd in zip(SEG_OFF, SEG_W)]
    u_cols = [(512, 1), (LANE, XW_OFF // LANE), (LANE, XA_OFF // LANE), (2 * LANE, XG_OFF // (2 * LANE))]
    any_spec = pl.BlockSpec(memory_space=pl.ANY)
    res = pl.pallas_call(
        body, name="rwkv_pre_bwd", grid=(nb,),
        in_specs=[rev(*s) for s in segs] + [halo(*s) for s in segs] + [rev(*c) for c in u_cols]
                 + [rev(RWKV_DIM, 0)] * n_g + [whole(mu)] + [whole(p_) for p_ in small] + [any_spec],
        out_specs=[any_spec] + [pl.BlockSpec(s, functools.partial(lambda i, n: (0,) * n, n=len(s))) for s in acc_shapes],
        out_shape=[jax.ShapeDtypeStruct(dproj.shape, dproj.dtype)] + [jax.ShapeDtypeStruct(s, F32) for s in acc_shapes],
        scratch_shapes=[pltpu.VMEM((2, tr, RW_PAD), dproj.dtype), pltpu.SemaphoreType.DMA((2,)), pltpu.VMEM((1, RW_PAD), F32)],
        input_output_aliases={24 + n_g: 0},
        compiler_params=_params(("arbitrary",)),
    )(*[proj] * 12, *[u] * 4, *grads, mu, *small, dproj)
    return res


def _local_step(x, p, tgt, w, early_shards, late_shards):
    row = lambda v: v.reshape(1, -1)
    w = dict(w)

    xn1, *gathered = _rowwise("rms_mix", lambda h, g: (_rms(h, g),), [x], [w["norm_mix_g"]], [(D_MODEL, BF16)],
                              gather=early_shards)
    w.update({n: _unshard(n, g_) for n, g_ in zip(_EARLY[1:], gathered[1:])})
    w["w_in"] = _assemble_w_in(gathered[0])
    w["w_lora_up"] = _pad_rows(w["w_lora_up"], LANE)
    w["a_lora_up"] = _pad_rows(w["a_lora_up"], LANE)
    w["g_lora_up"] = _pad_rows(w["g_lora_up"], 2 * LANE)
    proj = _matmul("in_proj", xn1, w["w_in"], "nn", [F32], tm=2048, tn=512, tk=D_MODEL)
    n_cb = CONV_DIM // LANE

    def conv_fwd(blk, cw):
        gb, gc, hx = blk[:, :LANE], blk[:, LANE:2 * LANE], blk[:, 2 * LANE:]
        uu = gc * hx
        return (gb * (uu * cw[2:3] + _shift_down(uu, 1) * cw[1:2] + _shift_down(uu, 2) * cw[0:1]),)

    (y_conv,) = _colwise("conv_fwd", conv_fwd, n_cb, [(proj, 3 * LANE)], [w["conv_w"]], [(CONV_DIM, BF16, LANE)])

    small = [w["w0"], w["a0"], w["k_k"], w["k_a"], w["w_lora_up"], w["a_lora_up"], w["g_lora_up"]]
    def pre_fwd(*xs):
        cur, prev_rows, mu, prm = xs[:6], xs[6:12], xs[12], xs[13:]
        segs = []
        for c_, p_, off, wd in zip(cur, prev_rows, SEG_OFF, SEG_W):
            rows = lax.broadcasted_iota(jnp.int32, c_.shape, 0)
            prev = jnp.where(rows == 0, p_, pltpu.roll(c_, 1, 0))
            segs.append(c_ + mu[:, off:off + wd] * (prev - c_))
        return (jnp.concatenate(segs, axis=1),) + tuple(_rwkv_pre(segs[1], segs[3], segs[4], segs[5], *prm))

    proj_segs = [(proj, wd, (CONV_COLS + off) // wd) for off, wd in zip(SEG_OFF, SEG_W)]
    u, lw, k_h, ra, rb, g = _rowwise("rwkv_pre", pre_fwd, proj_segs, [w["shift_mu"]] + small,
                                     [(RW_PAD, F32)] + [(RWKV_DIM, F32)] * 5, halo=True)
    u_k, u_xw, u_xa, u_xg = (u, 512, 1), (u, LANE, XW_OFF // LANE), (u, LANE, XA_OFF // LANE), (u, 2 * LANE, XG_OFF // (2 * LANE))
    y_rec, zs, late = _rec_fwd(u, lw, k_h, ra, rb, late_shards)
    for n, gathered in zip(_LATE, late):
        w[n] = _unshard(n, gathered)
    post_c = [w["ln_x_g"], w["ln_x_b"], w["r_k"]]
    u_r, u_v = (u, 512, 0), (u, 512, 2)
    (y_rwkv,) = _rowwise("rwkv_post", lambda *xs: (_rwkv_post(*xs),), [y_rec, u_r, k_h, u_v, g], post_c, [(RWKV_DIM, BF16)])
    ycat = jnp.concatenate([y_conv, y_rwkv], axis=1)
    def res_norm(acc, r_, g_):
        h = acc + r_
        return h, _rms(h, g_)

    h1, xn2 = _matmul("out_proj", ycat, w["w_out"], "nn", [F32, BF16], tm=1024, tn=D_MODEL, tk=D_MODEL, extras=[x],
                      consts=[w["norm_mlp_g"]], epilogue=res_norm)

    square = lambda h: h.astype(F32) * h.astype(F32)
    hid = _matmul("mlp_up", xn2, w["w_up"], "nn", [BF16], tm=2048, tn=1024, tk=D_MODEL,
                  epilogue=lambda acc: (jnp.maximum(acc, 0.0),))
    h2, xn3 = _matmul("mlp_down", hid, w["w_down"], "nn", [F32, BF16], tm=512, tn=D_MODEL, tk=D_FF, extras=[h1],
                      consts=[w["norm_ple_g"]], epilogue=res_norm, a_map=square)
    zg = _matmul("ple_gate", xn3, w["w_ple_gate"], "nn", [F32], tm=1024, tn=1024, tk=D_MODEL)
    pp = _matmul("ple_proj", p, w["w_ple_proj"], "nn", [F32], tm=1024, tn=1024, tk=PLE_DIM)

    def head(h2_, zg_, pp_, tg, gf):
        gate = _sigmoid(zg_)
        h3 = h2_ + gate * pp_
        out = _rms(h3, gf)
        err = out - tg
        dh3, dgf = _rms_bwd(h3, gf, err * (1.0 / D_MODEL))
        loss = jnp.sum(jnp.sum(err * err, axis=1, keepdims=True), axis=0, keepdims=True) * (0.5 / D_MODEL)
        return dh3, dh3 * pp_ * gate * (1.0 - gate), dh3 * gate, dgf, loss

    dh3, dzg, dpp, d_norm_final, loss = _rowwise(
        "head", head, [h2, zg, pp, tgt], [row(w["norm_final_g"])], [(D_MODEL, F32), (D_MODEL, BF16), (D_MODEL, BF16)],
        [(1, D_MODEL), (1, 1)])

    d_w_ple_proj = _matmul("d_ple_proj", p, dpp, "tn", [BF16], tm=PLE_DIM, tn=1024, tk=4096)
    d_w_ple_gate = _matmul("d_ple_gate", xn3, dzg, "tn", [BF16], tm=512, tn=1024, tk=4096)

    def norm_bwd(dxn, h, dres, g_):
        dh, dg = _rms_bwd(h, g_, dxn)
        dh = dh + dres
        return dh, dh, dg

    nb = dict(tm=512, tn=D_MODEL, epilogue=norm_bwd, sums=[(1, D_MODEL)])
    dh2, dh2_b, d_norm_ple = _matmul("dx_ple_gate", dzg, w["w_ple_gate"], "nt", [F32, BF16], tk=D_MODEL,
                                     extras=[h2, dh3], consts=[w["norm_ple_g"]], **nb)
    d_w_down = _matmul("d_mlp_down", hid, dh2_b, "tn", [BF16], tm=512, tn=1024, tk=4096, a_map=square)
    dpre = _matmul("dx_mlp_down", dh2_b, w["w_down"], "nt", [BF16], tm=2048, tn=1024, tk=D_MODEL, extras=[hid],
                   epilogue=lambda acc, hid_: (acc * (2.0 * hid_.astype(F32)),))
    d_w_up = _matmul("d_mlp_up", xn2, dpre, "tn", [BF16], tm=1024, tn=1024, tk=4096)
    dh1, dh1_b, d_norm_mlp = _matmul("dx_mlp_up", dpre, w["w_up"], "nt", [F32, BF16], tk=D_FF,
                                     extras=[h1, dh2], consts=[w["norm_mlp_g"]], **nb)
    d_w_out = _matmul("d_out_proj", ycat, dh1_b, "tn", [BF16], tm=512, tn=1024, tk=4096)
    dycat = _matmul("dx_out_proj", dh1_b, w["w_out"], "nt", [F32], tm=1024, tn=1024, tk=D_MODEL)
    late_grads = dict(w_out=d_w_out, w_up=d_w_up, w_down=d_w_down, w_ple_gate=d_w_ple_gate, w_ple_proj=d_w_ple_proj)

    def conv_bwd(dy, blk, cw):
        gb, gc, hx = blk[:, :LANE], blk[:, LANE:2 * LANE], blk[:, 2 * LANE:]
        uu = gc * hx
        u1, u2 = _shift_down(uu, 1), _shift_down(uu, 2)
        dconv = dy * gb
        du = dconv * cw[2:3] + _shift_up(dconv, 1) * cw[1:2] + _shift_up(dconv, 2) * cw[0:1]
        s = lambda z: jnp.sum(z, axis=0, keepdims=True)
        d_blk = jnp.concatenate([dy * (uu * cw[2:3] + u1 * cw[1:2] + u2 * cw[0:1]), du * hx, du * gc], axis=1)
        return d_blk, s(dconv * u2), s(dconv * u1), s(dconv * uu)

    dproj, dcw0, dcw1, dcw2 = _colwise(
        "conv_bwd", conv_bwd, n_cb, [(dycat, LANE), (proj, 3 * LANE)], [w["conv_w"]],
        [(IN_PAD, BF16, 3 * LANE)], [(1, CONV_DIM)] * 3)

    def post_bwd(dy, y, r, k_h_, v, g_, ln_g, ln_b, r_k):
        _, vjp = jax.vjp(_rwkv_post, y, r, k_h_, v, g_, ln_g, ln_b, r_k)
        return vjp(dy)

    dy_rec, dr_p, dk_p, dv_p, dg, d_ln_g, d_ln_b, d_r_k = _rowwise(
        "rwkv_post_bwd", post_bwd, [(dycat, 512, 1), y_rec, u_r, k_h, u_v, g], post_c,
        [(RWKV_DIM, F32)] * 5, [(1, RWKV_DIM)] * 3)
    (dr_r, dv_r, dlw, dk_r, da, db), late_parts = _rec_bwd(
        u, lw, k_h, ra, rb, zs, dy_rec, [_reshard(n, late_grads[n]) for n in _LATE], [True] * len(_LATE))

    dproj, d_mu, d_w0, d_a0, d_k_k, d_k_a, d_wl, d_al, d_gl = _rwkv_pre_bwd(
        proj, u, [dr_p, dr_r, dv_p, dv_r, dlw, dk_p, dk_r, da, db, dg], w["shift_mu"], small, dproj)
    d_w_in = _matmul("d_in_proj", xn1, dproj, "tn", [BF16], tm=1024, tn=896, tk=4096)
    early_grads = dict(conv_w=jnp.concatenate([dcw0, dcw1, dcw2], axis=0),
                       w_lora_up=d_wl[:64], a_lora_up=d_al[:64], g_lora_up=d_gl[:160])
    early_send = [_split_w_in_grad(d_w_in)] + [_reshard(n, early_grads[n]) for n in _EARLY[1:]]
    dx, d_norm_mix, *early_parts = _matmul(
        "dx_in_proj", dproj, w["w_in"], "nt", [F32], tk=IN_PAD, extras=[x, dh1], consts=[w["norm_mix_g"]],
        xch=early_send, xch_scatter=[True] * len(_EARLY), **dict(nb, epilogue=lambda *a: norm_bwd(*a)[1:]))

    grads = dict(
        norm_mix_g=d_norm_mix, shift_mu=d_mu, w0=d_w0, a0=d_a0, k_k=d_k_k, k_a=d_k_a, r_k=d_r_k,
        ln_x_g=d_ln_g, ln_x_b=d_ln_b, norm_mlp_g=d_norm_mlp, norm_ple_g=d_norm_ple, norm_final_g=d_norm_final)
    parts = dict(zip(_EARLY, early_parts))
    parts.update(zip(_LATE, late_parts))
    return loss, dx, grads, parts


def _adamw(name, parts, w, m, v):
    rows, cols = w.shape[-2:]
    lead = w.ndim - 2
    tr = rows if rows * cols * 4 * 8 <= (4 << 20) else max(8, (4 << 20) // (cols * 4 * 8) // 8 * 8)
    while rows % tr:
        tr -= 8

    def body(p_ref, w_ref, m_ref, v_ref, g_ref, d_ref, nm_ref, nv_ref):
        g = p_ref[0].astype(F32)
        for s in range(1, N_DEV):
            g = g + p_ref[s].astype(F32)
        nm = ADAM_B1 * m_ref[...] + (1.0 - ADAM_B1) * g
        nv = ADAM_B2 * v_ref[...] + (1.0 - ADAM_B2) * (g * g)
        m_hat = nm / (1.0 - ADAM_B1 ** ADAM_STEP)
        v_hat = nv / (1.0 - ADAM_B2 ** ADAM_STEP)
        g_ref[...] = g
        d_ref[...] = -ADAM_LR * (m_hat / (jnp.sqrt(v_hat) + ADAM_EPS) + ADAM_WD * w_ref[...])
        nm_ref[...] = nm
        nv_ref[...] = nv

    blk = pl.BlockSpec((None,) * lead + (tr, cols), lambda i: (0,) * lead + (i, 0))
    return pl.pallas_call(
        body, name=name, grid=(rows // tr,),
        in_specs=[pl.BlockSpec((N_DEV, tr, cols), lambda i: (0, i, 0)), blk, blk, blk], out_specs=[blk] * 4,
        out_shape=[jax.ShapeDtypeStruct(w.shape, F32)] * 4,
        compiler_params=_params(("arbitrary",)),
    )(parts, w, m, v)


def kernel(x, p, norm_mix_g, w_in, conv_w, shift_mu, w_lora_up, w0, a_lora_up, a0, g_lora_up, k_k, k_a, r_k, ln_x_g, ln_x_b, w_out, norm_mlp_g, w_up, w_down, norm_ple_g, w_ple_gate, w_ple_proj, norm_final_g, loss_target, m_norm_mix_g, m_w_in, m_conv_w, m_shift_mu, m_w_lora_up, m_w0, m_a_lora_up, m_a0, m_g_lora_up, m_k_k, m_k_a, m_r_k, m_ln_x_g, m_ln_x_b, m_w_out, m_norm_mlp_g, m_w_up, m_w_down, m_norm_ple_g, m_w_ple_gate, m_w_ple_proj, m_norm_final_g, v_norm_mix_g, v_w_in, v_conv_w, v_shift_mu, v_w_lora_up, v_w0, v_a_lora_up, v_a0, v_g_lora_up, v_k_k, v_k_a, v_r_k, v_ln_x_g, v_ln_x_b, v_w_out, v_norm_mlp_g, v_w_up, v_w_down, v_norm_ple_g, v_w_ple_gate, v_w_ple_proj, v_norm_final_g):
    args = dict(locals())
    wts = {n: args[n] for n in _WEIGHTS}
    mom = {n: args["m_" + n] for n in _WEIGHTS}
    var = {n: args["v_" + n] for n in _WEIGHTS}
    shard2d = lambda a: a.reshape(a.shape[-2:])
    pad_mu = lambda a: _pad_in_cols(jnp.concatenate([jnp.zeros((1, CONV_COLS), F32), a], axis=1))[:, CONV_COLS:]
    unpad_mu = lambda a: _unpad_in_cols(jnp.concatenate([jnp.zeros((1, CONV_COLS), F32), a], axis=1))[:, CONV_COLS:]

    shards = {n: shard2d(wts[n]).astype(BF16 if n in _BF16_GATHER else F32) for n in _SHARDED}
    w = {n: wts[n].reshape(1, -1) for n in _REPLICATED}
    w["shift_mu"] = pad_mu(wts["shift_mu"])

    loss, dx, grads, parts = _local_step(x[0], p[0, 0], loss_target[0], w, [shards[n] for n in _EARLY],
                                         [shards[n] for n in _LATE])

    grads["shift_mu"] = unpad_mu(grads["shift_mu"])
    (small_parts,) = _exchange("gather_small", [_pack([grads[n] for n in _REPLICATED] + [loss])], [False])

    out = {}
    for n in _SHARDED:
        out[n] = _adamw("adamw_" + n, parts[n], wts[n], mom[n], var[n])
    sm = _adamw("adamw_small", small_parts, _pack([wts[n] for n in _REPLICATED]), _pack([mom[n] for n in _REPLICATED]),
                _pack([var[n] for n in _REPLICATED]))
    off = 0
    for n in _REPLICATED:
        size = wts[n].size
        out[n] = [r.reshape(-1)[off:off + size].reshape(wts[n].shape) for r in sm]
        off += size
    loss_total = sm[0].reshape(-1)[off]
    return (loss_total, dx[None], *[out[n][0] for n in _WEIGHTS], *[out[n][1] for n in _WEIGHTS],
            *[out[n][2] for n in _WEIGHTS], *[out[n][3] for n in _WEIGHTS])
```

```python
import functools

import jax
import jax.numpy as jnp
from jax import lax
from jax.experimental import pallas as pl
from jax.experimental.pallas import tpu as pltpu

F32 = jnp.float32
BF16 = jnp.bfloat16

N_DEV = 8
D_MODEL = 1024
CONV_DIM = 512
RWKV_DIM = 512
HEAD_DIM = 64
N_HEADS = 8
D_FF = 4096
PLE_DIM = 256
RMS_EPS = 1e-6
GN_EPS = 64e-5
L2_EPS = 1e-12
ADAM_LR, ADAM_B1, ADAM_B2, ADAM_EPS, ADAM_WD, ADAM_STEP = 0.001, 0.9, 0.999, 1e-08, 0.01, 10

CONV_COLS = 3 * CONV_DIM
RW_PAD = 2048
IN_PAD = CONV_COLS + RW_PAD
IN_COLS = 3360
XW_OFF, XA_OFF, XG_OFF = 1536, 1664, 1792
REC_CHUNK = 128
REC_PASSES = 1
ROW_BLOCK = 256
LANE = 128
VMEM_LIMIT = 56 * 1024 * 1024


def _dims(dn, ndim):
    if ndim == 3:
        return {"nn": (((2,), (1,)), ((0,), (0,))), "nt": (((2,), (2,)), ((0,), (0,))),
                "tn": (((1,), (1,)), ((0,), (0,)))}[dn]
    return {"nn": (((1,), (0,)), ((), ())), "nt": (((1,), (1,)), ((), ())), "tn": (((0,), (0,)), ((), ()))}[dn]


def _split2(x):
    hi = x.astype(BF16)
    return hi, (x - hi.astype(F32)).astype(BF16)


def _mm_raw(x, y, dn, passes):
    f = lambda p, q: lax.dot_general(p, q, _dims(dn, x.ndim), preferred_element_type=F32)
    if passes == 1:
        return f(x.astype(BF16), y.astype(BF16))
    xh, xl = _split2(x)
    yh, yl = _split2(y)
    return f(xh, yh) + f(xh, yl) + f(xl, yh)


@functools.partial(jax.custom_vjp, nondiff_argnums=(2, 3))
def _mm(x, y, dn, passes):
    return _mm_raw(x, y, dn, passes)


def _mm_fwd(x, y, dn, passes):
    return _mm_raw(x, y, dn, passes), (x, y)


def _mm_bwd(dn, passes, res, d):
    x, y = res
    if dn == "nn":
        return _mm(d, y, "nt", passes), _mm(x, d, "tn", passes)
    if dn == "nt":
        return _mm(d, y, "nn", passes), _mm(d, x, "tn", passes)
    return _mm(y, d, "nt", passes), _mm(x, d, "nn", passes)


_mm.defvjp(_mm_fwd, _mm_bwd)


def _head_ones():
    i = lax.broadcasted_iota(jnp.int32, (RWKV_DIM, RWKV_DIM), 0) // HEAD_DIM
    j = lax.broadcasted_iota(jnp.int32, (RWKV_DIM, RWKV_DIM), 1) // HEAD_DIM
    return (i == j).astype(BF16)


def _hsum_raw(x):
    ones = _head_ones()
    f = lambda p: lax.dot_general(p, ones, _dims("nn", 2), preferred_element_type=F32)
    x1, x2 = _split2(x)
    return f(x1) + f(x2)


@jax.custom_vjp
def _hsum(x):
    return _hsum_raw(x)


_hsum.defvjp(lambda x: (_hsum_raw(x), None), lambda _, d: (_hsum(d),))


def _sigmoid(x):
    return 1.0 / (1.0 + jnp.exp(-x))


def _softplus(x):
    return jnp.maximum(x, 0.0) + jnp.log(1.0 + jnp.exp(-jnp.abs(x)))


def _params(sem):
    return pltpu.CompilerParams(dimension_semantics=sem, vmem_limit_bytes=VMEM_LIMIT)


def _rowwise(name, fn, rows, consts, row_outs, acc_outs=(), tr=ROW_BLOCK, halo=False, gather=()):
    rows = [r if isinstance(r, tuple) else (r, r.shape[1], 0) for r in rows]
    t_len = rows[0][0].shape[0]
    tr = min(tr, t_len)
    n_r, n_c, n_o, n_a, n_x = len(rows), len(consts), len(row_outs), len(acc_outs), len(gather)
    n_h = n_r if halo else 0
    sub = 8
    x_specs, x_shapes, x_sems = _exchange_io(gather, [False] * n_x) if n_x else ([], [], [])
    nb = t_len // tr

    def body(*refs):
        if n_x:
            n_in = n_r + n_h + n_c
            start, forward, wait = _gather_plan(refs[n_in:n_in + n_x], refs[len(refs) - 3 - n_x:len(refs) - 3], *refs[len(refs) - 3:])
            pl.when(pl.program_id(0) == 0)(start)
            refs = refs[:n_in] + refs[n_in + n_x:len(refs) - 3 - n_x]
        ins = [r[...] for r in refs[:n_r]]
        ins += [jnp.where(pl.program_id(0) == 0, 0.0, r[sub - 1:sub, :]) for r in refs[n_r:n_r + n_h]]
        ins += [r[...] for r in refs[n_r + n_h:n_r + n_h + n_c]]
        refs = refs[:n_r] + refs[n_r + n_h:]
        outs = fn(*ins)
        o_refs = refs[n_r + n_c:n_r + n_c + n_o]
        a_refs = refs[n_r + n_c + n_o:]
        for o_ref, val in zip(o_refs, outs[:n_o]):
            o_ref[...] = val.astype(o_ref.dtype)
        if n_a:
            first = pl.program_id(0) == 0

            @pl.when(first)
            def _():
                for a_ref, val in zip(a_refs, outs[n_o:]):
                    a_ref[...] = val

            @pl.when(jnp.logical_not(first))
            def _():
                for a_ref, val in zip(a_refs, outs[n_o:]):
                    a_ref[...] += val

        if n_x:
            @pl.when(pl.program_id(0) == nb - 1)
            def _():
                for j in range(n_x):
                    forward(j)
                wait()

    in_specs = [pl.BlockSpec((tr, w), functools.partial(lambda i, c: (i, c), c=cb)) for _, w, cb in rows]
    if halo:
        in_specs += [pl.BlockSpec((sub, w), functools.partial(lambda i, c: (jnp.maximum(i * (tr // sub) - 1, 0), c), c=cb))
                     for _, w, cb in rows]
    in_specs += [pl.BlockSpec(c.shape, functools.partial(lambda i, n: (0,) * n, n=c.ndim)) for c in consts]
    out_specs = [pl.BlockSpec((tr, w), lambda i: (i, 0)) for w, _ in row_outs]
    out_specs += [pl.BlockSpec(s, functools.partial(lambda i, n: (0,) * n, n=len(s))) for s in acc_outs]
    out_shape = [jax.ShapeDtypeStruct((t_len, w), dt) for w, dt in row_outs]
    out_shape += [jax.ShapeDtypeStruct(s, F32) for s in acc_outs]
    return pl.pallas_call(
        body, name=name, grid=(nb,), in_specs=in_specs + x_specs, out_specs=out_specs + x_specs,
        out_shape=out_shape + x_shapes, scratch_shapes=x_sems,
        compiler_params=pltpu.CompilerParams(dimension_semantics=("arbitrary",), vmem_limit_bytes=VMEM_LIMIT,
                                             has_side_effects=bool(n_x)),
    )(*[r[0] for r in rows], *([r[0] for r in rows] if halo else []), *consts, *gather)


def _colwise(name, fn, n_blocks, cols, prms, col_outs, prm_outs=()):
    t_len = cols[0][0].shape[0]
    n_i = len(cols) + len(prms)

    def body(*refs):
        outs = fn(*[r[...] for r in refs[:n_i]])
        for o_ref, val in zip(refs[n_i:], outs):
            o_ref[...] = val.astype(o_ref.dtype)

    spec = lambda r, w: pl.BlockSpec((r, w), lambda j: (0, j))
    in_specs = [spec(t_len, w) for _, w in cols] + [spec(a.shape[0], LANE) for a in prms]
    out_specs = [spec(t_len, bw) for _, _, bw in col_outs] + [spec(r, LANE) for r, _ in prm_outs]
    out_shape = [jax.ShapeDtypeStruct((t_len, w), dt) for w, dt, _ in col_outs]
    out_shape += [jax.ShapeDtypeStruct((r, w), F32) for r, w in prm_outs]
    return pl.pallas_call(
        body, name=name, grid=(n_blocks,), in_specs=in_specs, out_specs=out_specs, out_shape=out_shape,
        compiler_params=_params(("arbitrary",)),
    )(*[c[0] for c in cols], *prms)


def _matmul(name, a, b, dn, outs, *, tm, tn, tk, extras=(), consts=(), epilogue=None, sums=(), xch=(), xch_scatter=(),
            a_map=None):
    if dn == "nn":
        (m, k), n = a.shape, b.shape[1]
    elif dn == "nt":
        (m, k), n = a.shape, b.shape[0]
    else:
        (k, m), n = a.shape, b.shape[1]
    tm, tn, tk = min(tm, m), min(tn, n), min(tk, k)
    nk = k // tk
    grid = (m // tm, n // tn, nk)
    assert nk == 1 and (not sums or grid[1] == 1)
    a_spec = pl.BlockSpec((tk, tm), lambda i, j, q: (q, i)) if dn == "tn" else pl.BlockSpec((tm, tk), lambda i, j, q: (i, q))
    b_spec = pl.BlockSpec((tn, tk), lambda i, j, q: (j, q)) if dn == "nt" else pl.BlockSpec((tk, tn), lambda i, j, q: (q, j))
    o_spec = pl.BlockSpec((tm, tn), lambda i, j, q: (i, j))
    c_spec = pl.BlockSpec((1, tn), lambda i, j, q: (0, j))
    n_e, n_c, n_o, n_s, n_x = len(extras), len(consts), len(outs), len(sums), len(xch)
    x_specs, x_shapes, x_sems = _exchange_io(xch, xch_scatter) if n_x else ([], [], [])

    def body(*refs):
        a_ref, b_ref = refs[:2]
        e_refs = refs[2:2 + n_e + n_c]
        x_in = refs[2 + n_e + n_c:2 + n_e + n_c + n_x]
        rest = refs[2 + n_e + n_c + n_x:]
        o_refs, s_refs, x_out, scratch = rest[:n_o], rest[n_o:n_o + n_s], rest[n_o + n_s:n_o + n_s + n_x], rest[n_o + n_s + n_x:]
        step = (pl.program_id(0) * grid[1] + pl.program_id(1)) * nk + pl.program_id(2)
        if n_x:
            start, wait = _exchange_plan(x_in, x_out, xch_scatter, *scratch[len(scratch) - 3:])
            pl.when(step == 0)(start)
        a_blk = a_ref[...] if a_map is None else a_map(a_ref[...])
        acc = lax.dot_general(a_blk.astype(BF16), b_ref[...].astype(BF16), _dims(dn, 2), preferred_element_type=F32)
        vals = (acc,) if epilogue is None else epilogue(acc, *[e[...] for e in e_refs])
        for o_ref, val in zip(o_refs, vals[:n_o]):
            o_ref[...] = val.astype(o_ref.dtype)
        if n_s:
            @pl.when(step == 0)
            def _():
                for s_ref, val in zip(s_refs, vals[n_o:]):
                    s_ref[...] = val

            @pl.when(step > 0)
            def _():
                for s_ref, val in zip(s_refs, vals[n_o:]):
                    s_ref[...] += val

        if n_x:
            pl.when(step == grid[0] * grid[1] * nk - 1)(wait)

    plain = not (n_s or n_x)
    res = pl.pallas_call(
        body, name=name, grid=grid,
        in_specs=[a_spec, b_spec] + [o_spec] * n_e + [c_spec] * n_c + x_specs,
        out_specs=[o_spec] * n_o + [c_spec] * n_s + x_specs,
        out_shape=[jax.ShapeDtypeStruct((m, n), dt) for dt in outs] + [jax.ShapeDtypeStruct(s, F32) for s in sums] + x_shapes,
        scratch_shapes=x_sems,
        compiler_params=pltpu.CompilerParams(
            dimension_semantics=("parallel", "parallel", "arbitrary") if plain else ("arbitrary",) * 3,
            vmem_limit_bytes=VMEM_LIMIT, has_side_effects=bool(n_x)),
    )(a, b, *extras, *consts, *xch)
    return res[0] if len(res) == 1 else res


def _rms(h, g):
    return h * lax.rsqrt(jnp.mean(h * h, axis=-1, keepdims=True) + RMS_EPS) * g


def _rms_bwd(h, g, dy):
    rs = lax.rsqrt(jnp.mean(h * h, axis=-1, keepdims=True) + RMS_EPS)
    n = h * rs
    dn = dy * g
    dh = rs * (dn - n * jnp.mean(dn * n, axis=-1, keepdims=True))
    return dh, jnp.sum(dy * n, axis=0, keepdims=True)


def _rwkv_pre(k, xw, xa, xg, w0, a0, k_k, k_a, wl, al, gl):
    zw = w0 + _mm(jnp.tanh(xw), wl, "nn", 1)
    lw = -jnp.exp(-_softplus(-zw) - 0.5)
    iclr = _sigmoid(a0 + _mm(xa, al, "nn", 1))
    g = _mm(_sigmoid(xg), gl, "nn", 1)
    kk0 = k * k_k
    kk = kk0 / jnp.maximum(jnp.sqrt(_hsum(kk0 * kk0)), L2_EPS)
    k_h = k * (1.0 + (iclr - 1.0) * k_a)
    return lw, k_h, -kk, kk * iclr, g


def _rwkv_post(y, r, k_h, v, g, ln_g, ln_b, r_k):
    mu = _hsum(y) * (1.0 / HEAD_DIM)
    yc = y - mu
    var = _hsum(yc * yc) * (1.0 / HEAD_DIM)
    yo = yc * lax.rsqrt(var + GN_EPS) * ln_g + ln_b
    bonus = _hsum(r * k_h * r_k) * v
    return (yo + bonus) * g


def _shift_down(x, n):
    rows = lax.broadcasted_iota(jnp.int32, x.shape, 0)
    return jnp.where(rows < n, 0.0, pltpu.roll(x, n, 0))


def _shift_up(x, n):
    t_len = x.shape[0]
    rows = lax.broadcasted_iota(jnp.int32, x.shape, 0)
    return jnp.where(rows >= t_len - n, 0.0, pltpu.roll(x, t_len - n, 0))


def _exchange_plan(ins, outs, scatter, send_sems, recv_sems, local_sems):
    x, y, c = lax.axis_index("x"), lax.axis_index("y"), lax.axis_index("c")
    me = 4 * x + 2 * y + c

    def local(i):
        return pltpu.make_async_copy(ins[i].at[me] if scatter[i] else ins[i], outs[i].at[me], local_sems.at[i])

    def send(i, rel):
        return pltpu.make_async_remote_copy(
            src_ref=ins[i].at[me ^ rel] if scatter[i] else ins[i], dst_ref=outs[i].at[me],
            send_sem=send_sems.at[i, rel - 1], recv_sem=recv_sems.at[i, rel - 1],
            device_id=(x ^ (rel >> 2), y ^ ((rel >> 1) & 1), c ^ (rel & 1)), device_id_type=pl.DeviceIdType.MESH)

    def landed(i, rel):
        slot = outs[i].at[me ^ rel]
        return pltpu.make_async_remote_copy(
            src_ref=slot, dst_ref=slot, send_sem=send_sems.at[i, rel - 1], recv_sem=recv_sems.at[i, rel - 1],
            device_id=(x, y, c), device_id_type=pl.DeviceIdType.MESH)

    def start():
        for i in range(len(ins)):
            local(i).start()
            for rel in range(1, N_DEV):
                send(i, rel).start()

    def wait():
        for i in range(len(ins)):
            local(i).wait()
            for rel in range(1, N_DEV):
                landed(i, rel).wait_recv()
            for rel in range(1, N_DEV):
                send(i, rel).wait_send()

    return start, wait


def _gather_plan(ins, outs, send_sems, recv_sems, local_sems):
    x, y, c = lax.axis_index("x"), lax.axis_index("y"), lax.axis_index("c")
    me = 4 * x + 2 * y + c
    direct, chips = (1, 2, 4, 6), (2, 4, 6)

    def local(i):
        return pltpu.make_async_copy(ins[i], outs[i].at[me], local_sems.at[i])

    def send(i, rel):
        return pltpu.make_async_remote_copy(
            src_ref=ins[i], dst_ref=outs[i].at[me], send_sem=send_sems.at[i, rel - 1], recv_sem=recv_sems.at[i, rel - 1],
            device_id=(x ^ (rel >> 2), y ^ ((rel >> 1) & 1), c ^ (rel & 1)), device_id_type=pl.DeviceIdType.MESH)

    def passed(i, rel):
        slot = outs[i].at[me ^ rel]
        return pltpu.make_async_remote_copy(
            src_ref=slot, dst_ref=slot, send_sem=send_sems.at[i, rel], recv_sem=recv_sems.at[i, rel],
            device_id=(x, y, 1 - c), device_id_type=pl.DeviceIdType.MESH)

    def landed(i, rel):
        slot = outs[i].at[me ^ rel]
        return pltpu.make_async_remote_copy(
            src_ref=slot, dst_ref=slot, send_sem=send_sems.at[i, rel - 1], recv_sem=recv_sems.at[i, rel - 1],
            device_id=(x, y, c), device_id_type=pl.DeviceIdType.MESH)

    def start():
        for i in range(len(ins)):
            local(i).start()
            for rel in direct:
                send(i, rel).start()

    def forward(i):
        for rel in chips:
            landed(i, rel).wait_recv()
            passed(i, rel).start()

    def wait():
        for i in range(len(ins)):
            local(i).wait()
            for rel in (1, 3, 5, 7):
                landed(i, rel).wait_recv()
            for rel in direct:
                send(i, rel).wait_send()
            for rel in chips:
                passed(i, rel).wait_send()

    return start, forward, wait


def _exchange_io(arrays, scatter):
    n = len(arrays)
    any_spec = pl.BlockSpec(memory_space=pl.ANY)
    out_shape = [jax.ShapeDtypeStruct(a.shape if sc else (N_DEV,) + a.shape, a.dtype) for a, sc in zip(arrays, scatter)]
    sems = [pltpu.SemaphoreType.DMA((n, N_DEV - 1)), pltpu.SemaphoreType.DMA((n, N_DEV - 1)), pltpu.SemaphoreType.DMA((n,))]
    return [any_spec] * n, out_shape, sems


def _exchange(name, arrays, scatter):
    n = len(arrays)
    specs, out_shape, sems = _exchange_io(arrays, scatter)

    def body(*refs):
        if any(scatter):
            start, wait = _exchange_plan(refs[:n], refs[n:2 * n], scatter, *refs[2 * n:])
            start()
        else:
            start, forward, wait = _gather_plan(refs[:n], refs[n:2 * n], *refs[2 * n:])
            start()
            for i in range(n):
                forward(i)
        wait()

    return pl.pallas_call(
        body, name=name, in_specs=specs, out_specs=specs, out_shape=out_shape, scratch_shapes=sems,
        compiler_params=pltpu.CompilerParams(has_side_effects=True),
    )(*arrays)


def _tri_powers(low):
    powers, n = [low], 1
    while 2 * n < low.shape[-1]:
        powers.append(_mm(powers[-1], powers[-1], "nn", REC_PASSES))
        n *= 2
    return powers


@jax.custom_vjp
def _tri_solve(low, rhs):
    for p in _tri_powers(low):
        rhs = rhs + _mm(p, rhs, "nn", REC_PASSES)
    return rhs


def _tri_solve_fwd(low, rhs):
    powers = _tri_powers(low)
    for p in powers:
        rhs = rhs + _mm(p, rhs, "nn", REC_PASSES)
    return rhs, (powers, rhs)


def _tri_solve_bwd(res, d):
    powers, u = res
    for p in powers:
        d = d + _mm(p, d, "tn", REC_PASSES)
    return _mm(d, u, "nt", REC_PASSES), d


_tri_solve.defvjp(_tri_solve_fwd, _tri_solve_bwd)


def _chunk_fwd(z0, r, lw, k, v, a, b):
    n_h, c, n_k = r.shape
    mm = functools.partial(_mm, passes=REC_PASSES)
    gram = functools.partial(_mm, passes=3)
    ti = lax.broadcasted_iota(jnp.int32, (c, c), 0)
    si = lax.broadcasted_iota(jnp.int32, (c, c), 1)
    strict, incl = si < ti, si <= ti
    cum = _mm(jnp.broadcast_to(incl.astype(F32), (n_h, c, c)), lw, "nn", 3)
    cum_end = cum[:, c - 1:c, :]
    e_neg, e_end = jnp.exp(-cum), jnp.exp(cum_end - cum)
    x2 = jnp.concatenate([a * jnp.exp(cum - lw), r * jnp.exp(cum)], axis=1)
    y2 = jnp.concatenate([b * e_neg, k * e_neg], axis=1)
    mask = jnp.concatenate([jnp.concatenate([strict, strict], axis=1), jnp.concatenate([incl, incl], axis=1)], axis=0)
    g2 = jnp.where(mask, gram(x2, y2, "nt"), 0.0)
    t2 = mm(x2, z0, "nn") + mm(g2[:, :, c:], v, "nn")
    u = _tri_solve(g2[:, :c, :c], t2[:, :c])
    y = t2[:, c:] + mm(g2[:, c:, :c], u, "nn")
    ki = lax.broadcasted_iota(jnp.int32, (n_k, n_k), 0)
    kj = lax.broadcasted_iota(jnp.int32, (n_k, n_k), 1)
    dmat = jnp.where(ki == kj, jnp.broadcast_to(jnp.exp(cum_end), (n_h, n_k, n_k)), 0.0)
    z_end = mm(dmat, z0, "nn") + mm(jnp.concatenate([b * e_end, k * e_end], axis=1), jnp.concatenate([u, v], axis=1), "tn")
    return y, z_end


def _heads(x):
    return jnp.stack([x[:, h * HEAD_DIM:(h + 1) * HEAD_DIM] for h in range(N_HEADS)])


def _unheads(x):
    return jnp.concatenate([x[h] for h in range(N_HEADS)], axis=-1)


def _rec_params():
    return pltpu.CompilerParams(dimension_semantics=("arbitrary",), vmem_limit_bytes=VMEM_LIMIT, has_side_effects=True)


def _rec_fwd(u, lw, k, a, b, xch):
    t_len = lw.shape[0]
    c = min(REC_CHUNK, t_len)
    nc = t_len // c
    n_x = len(xch)
    x_specs, x_shapes, x_sems = _exchange_io(xch, [False] * n_x)
    sizes = [a_.size * a_.dtype.itemsize for a_ in xch]
    pass_step = [min(nc - 1, int(0.9 * nc * sum(sizes[:j + 1]) / sum(sizes)) + 1) for j in range(n_x)]

    def body(*refs):
        r_ref, v_ref, lw_ref, k_ref, a_ref, b_ref = refs[:6]
        x_in = refs[6:6 + n_x]
        y_ref, zs_ref = refs[6 + n_x:8 + n_x]
        x_out = refs[8 + n_x:8 + 2 * n_x]
        z_scr = refs[8 + 2 * n_x]
        start, forward, wait = _gather_plan(x_in, x_out, *refs[9 + 2 * n_x:])
        i = pl.program_id(0)

        @pl.when(i == 0)
        def _():
            start()
            z_scr[...] = jnp.zeros_like(z_scr)

        z0 = z_scr[...]
        zs_ref[0] = z0
        y, z_end = _chunk_fwd(z0, _heads(r_ref[...]), _heads(lw_ref[...]), _heads(k_ref[...]), _heads(v_ref[...]),
                              _heads(a_ref[...]), _heads(b_ref[...]))
        y_ref[...] = _unheads(y)
        z_scr[...] = z_end

        for j in range(n_x):
            pl.when(i == pass_step[j])(functools.partial(forward, j))

        @pl.when(i == nc - 1)
        def _():
            wait()

    blk = lambda cb: pl.BlockSpec((c, RWKV_DIM), functools.partial(lambda i, q: (i, q), q=cb))
    res = pl.pallas_call(
        body, name="rwkv_rec_fwd", grid=(nc,),
        in_specs=[blk(0), blk(2)] + [blk(0)] * 4 + x_specs,
        out_specs=[blk(0), pl.BlockSpec((1, N_HEADS, HEAD_DIM, HEAD_DIM), lambda i: (i, 0, 0, 0))] + x_specs,
        out_shape=[jax.ShapeDtypeStruct((t_len, RWKV_DIM), F32),
                   jax.ShapeDtypeStruct((nc, N_HEADS, HEAD_DIM, HEAD_DIM), F32)] + x_shapes,
        scratch_shapes=[pltpu.VMEM((N_HEADS, HEAD_DIM, HEAD_DIM), F32)] + x_sems,
        compiler_params=_rec_params(),
    )(u, u, lw, k, a, b, *xch)
    return res[0], res[1], res[2:]


def _rec_bwd(u, lw, k, a, b, zs, dy, xch, xch_scatter):
    t_len = lw.shape[0]
    c = min(REC_CHUNK, t_len)
    nc = t_len // c
    n_x = len(xch)
    x_specs, x_shapes, x_sems = _exchange_io(xch, xch_scatter)

    def body(*refs):
        r_ref, v_ref, lw_ref, k_ref, a_ref, b_ref, zs_ref, dy_ref = refs[:8]
        x_in = refs[8:8 + n_x]
        g_refs = refs[8 + n_x:14 + n_x]
        x_out = refs[14 + n_x:14 + 2 * n_x]
        dz_scr = refs[14 + 2 * n_x]
        start, wait = _exchange_plan(x_in, x_out, xch_scatter, *refs[15 + 2 * n_x:])
        i = pl.program_id(0)

        @pl.when(i == 0)
        def _():
            start()
            dz_scr[...] = jnp.zeros_like(dz_scr)

        _, vjp = jax.vjp(_chunk_fwd, zs_ref[0], _heads(r_ref[...]), _heads(lw_ref[...]), _heads(k_ref[...]),
                         _heads(v_ref[...]), _heads(a_ref[...]), _heads(b_ref[...]))
        dz0, dr, dlw, dk, dv, da, db = vjp((_heads(dy_ref[...]), dz_scr[...]))
        for ref, val in zip(g_refs, (dr, dv, dlw, dk, da, db)):
            ref[...] = _unheads(val)
        dz_scr[...] = dz0

        @pl.when(i == nc - 1)
        def _():
            wait()

    blk = lambda cb: pl.BlockSpec((c, RWKV_DIM), functools.partial(lambda i, q: (nc - 1 - i, q), q=cb))
    res = pl.pallas_call(
        body, name="rwkv_rec_bwd", grid=(nc,),
        in_specs=[blk(0), blk(2)] + [blk(0)] * 4
                 + [pl.BlockSpec((1, N_HEADS, HEAD_DIM, HEAD_DIM), lambda i: (nc - 1 - i, 0, 0, 0)), blk(0)] + x_specs,
        out_specs=[blk(0)] * 6 + x_specs,
        out_shape=[jax.ShapeDtypeStruct((t_len, RWKV_DIM), F32)] * 6 + x_shapes,
        scratch_shapes=[pltpu.VMEM((N_HEADS, HEAD_DIM, HEAD_DIM), F32)] + x_sems,
        compiler_params=_rec_params(),
    )(u, u, lw, k, a, b, zs, dy, *xch)
    return res[:6], res[6:]


_EARLY = ["w_in", "conv_w", "w_lora_up", "a_lora_up", "g_lora_up"]
_LATE = ["w_out", "w_up", "w_down", "w_ple_gate", "w_ple_proj"]
_SHARDED = _EARLY + _LATE
_COL_SHARDED = {"w_in", "conv_w", "w_lora_up", "a_lora_up", "g_lora_up", "w_up", "w_ple_proj"}
_BF16_GATHER = {"w_in", "w_out", "w_up", "w_down", "w_ple_gate", "w_ple_proj"}
_REPLICATED = ["norm_mix_g", "shift_mu", "w0", "a0", "k_k", "k_a", "r_k", "ln_x_g", "ln_x_b", "norm_mlp_g", "norm_ple_g",
               "norm_final_g"]
_WEIGHTS = ["norm_mix_g", "w_in", "conv_w", "shift_mu", "w_lora_up", "w0", "a_lora_up", "a0", "g_lora_up", "k_k", "k_a", "r_k",
            "ln_x_g", "ln_x_b", "w_out", "norm_mlp_g", "w_up", "w_down", "norm_ple_g", "w_ple_gate", "w_ple_proj", "norm_final_g"]


def _unshard(name, g):
    if name in _COL_SHARDED:
        return jnp.moveaxis(g, 0, 1).reshape(g.shape[1], N_DEV * g.shape[2])
    return g.reshape(N_DEV * g.shape[1], g.shape[2])


def _reshard(name, full):
    if name in _COL_SHARDED:
        return jnp.moveaxis(full.reshape(full.shape[0], N_DEV, full.shape[1] // N_DEV), 1, 0)
    return full.reshape(N_DEV, full.shape[0] // N_DEV, full.shape[1])


def _pad_in_cols(a):
    z = lambda n: jnp.zeros(a.shape[:-1] + (n,), a.dtype)
    conv = [a[..., part * CONV_DIM + j * LANE:part * CONV_DIM + (j + 1) * LANE] for j in range(CONV_DIM // LANE) for part in range(3)]
    return jnp.concatenate(conv + [a[..., CONV_COLS:3136], z(64), a[..., 3136:3200], z(64), a[..., 3200:3360], z(96)], axis=-1)


def _unpad_in_cols(a):
    conv = [a[..., (3 * j + part) * LANE:(3 * j + part + 1) * LANE] for part in range(3) for j in range(CONV_DIM // LANE)]
    return jnp.concatenate(conv + [a[..., CONV_COLS:3136], a[..., 3200:3264], a[..., 3328:3488]], axis=-1)


def _assemble_w_in(g):
    n_dev, rows, cols = g.shape

    def body(g_ref, o_ref):
        o_ref[...] = _pad_in_cols(jnp.concatenate([g_ref[d] for d in range(n_dev)], axis=1))

    return pl.pallas_call(
        body, name="w_in_assemble", grid=(rows // ROW_BLOCK,),
        in_specs=[pl.BlockSpec((n_dev, ROW_BLOCK, cols), lambda i: (0, i, 0))],
        out_specs=pl.BlockSpec((ROW_BLOCK, IN_PAD), lambda i: (i, 0)),
        out_shape=jax.ShapeDtypeStruct((rows, IN_PAD), g.dtype), compiler_params=_params(("arbitrary",)),
    )(g)


def _split_w_in_grad(dw):
    rows = dw.shape[0]
    cols = IN_COLS // N_DEV

    def body(d_ref, o_ref):
        full = _unpad_in_cols(d_ref[...])
        for d in range(N_DEV):
            o_ref[d] = full[:, cols * d:cols * (d + 1)]

    return pl.pallas_call(
        body, name="w_in_grad_split", grid=(rows // ROW_BLOCK,),
        in_specs=[pl.BlockSpec((ROW_BLOCK, IN_PAD), lambda i: (i, 0))],
        out_specs=pl.BlockSpec((N_DEV, ROW_BLOCK, cols), lambda i: (0, i, 0)),
        out_shape=jax.ShapeDtypeStruct((N_DEV, rows, cols), dw.dtype), compiler_params=_params(("arbitrary",)),
    )(dw)


def _pad_rows(a, rows):
    return jnp.concatenate([a, jnp.zeros((rows - a.shape[0],) + a.shape[1:], a.dtype)], axis=0)


SEG_W = [RWKV_DIM, RWKV_DIM, RWKV_DIM, LANE, LANE, 2 * LANE]
SEG_OFF = [0, 512, 1024, XW_OFF, XA_OFF, XG_OFF]


def _rwkv_pre_bwd(proj, u, grads, mu, small, dproj):
    t_len = u.shape[0]
    tr = min(ROW_BLOCK, t_len)
    nb = t_len // tr
    sub = 8
    n_g = len(grads)
    acc_shapes = [(1, RW_PAD)] + [(1, RWKV_DIM)] * 4 + [(LANE, RWKV_DIM), (LANE, RWKV_DIM), (2 * LANE, RWKV_DIM)]

    def body(*refs):
        seg_refs, halo_refs = refs[:6], refs[6:12]
        k_ref, xw_ref, xa_ref, xg_ref = refs[12:16]
        g_refs = refs[16:16 + n_g]
        mu_ref = refs[16 + n_g]
        prm_refs = refs[17 + n_g:24 + n_g]
        out_hbm = refs[25 + n_g]
        acc_refs = refs[26 + n_g:26 + n_g + len(acc_shapes)]
        vbuf, sems, carry = refs[26 + n_g + len(acc_shapes):]
        i = pl.program_id(0)
        blk = nb - 1 - i
        dr1, dr2, dv1, dv2, dlw, dk1, dk2, da, db, dg = [g[...] for g in g_refs]
        _, vjp = jax.vjp(_rwkv_pre, k_ref[...], xw_ref[...], xa_ref[...], xg_ref[...], *[p_[...] for p_ in prm_refs])
        dk, dxw, dxa, dxg, *dprm = vjp((dlw, dk1 + dk2, da, db, dg))
        du = jnp.concatenate([dr1 + dr2, dk, dv1 + dv2, dxw, dxa, dxg], axis=1)
        mu_v = mu_ref[...]

        @pl.when(i == 0)
        def _():
            carry[...] = jnp.zeros_like(carry)

        rows = lax.broadcasted_iota(jnp.int32, du.shape, 0)
        nxt = jnp.where(rows == tr - 1, carry[...], pltpu.roll(du, tr - 1, 0))
        d_rw = du - mu_v * du + mu_v * nxt
        d_mu = []
        for s_ref, h_ref, off, wd in zip(seg_refs, halo_refs, SEG_OFF, SEG_W):
            cur = s_ref[...]
            r0 = lax.broadcasted_iota(jnp.int32, cur.shape, 0)
            prev = jnp.where(r0 == 0, jnp.where(blk == 0, 0.0, h_ref[sub - 1:sub, :]), pltpu.roll(cur, 1, 0))
            d_mu.append(jnp.sum(du[:, off:off + wd] * (prev - cur), axis=0, keepdims=True))
        sums = [jnp.concatenate(d_mu, axis=1)] + list(dprm)

        @pl.when(i == 0)
        def _():
            for a_ref, val in zip(acc_refs, sums):
                a_ref[...] = val

        @pl.when(i > 0)
        def _():
            for a_ref, val in zip(acc_refs, sums):
                a_ref[...] += val

        carry[...] = du[0:1, :]
        slot = i % 2

        def writeback(s, b):
            return pltpu.make_async_copy(vbuf.at[s], out_hbm.at[pl.ds(b * tr, tr), pl.ds(CONV_COLS, RW_PAD)], sems.at[s])

        @pl.when(i >= 2)
        def _():
            writeback(slot, blk + 2).wait()

        vbuf[slot] = d_rw.astype(vbuf.dtype)
        writeback(slot, blk).start()

        @pl.when(i == nb - 1)
        def _():
            writeback(slot, blk).wait()
            if nb > 1:
                writeback(1 - slot, blk + 1).wait()

    rev = lambda w_, cb: pl.BlockSpec((tr, w_), functools.partial(lambda i, c: (nb - 1 - i, c), c=cb))
    halo = lambda w_, cb: pl.BlockSpec((sub, w_), functools.partial(
        lambda i, c: (jnp.maximum((nb - 1 - i) * (tr // sub) - 1, 0), c), c=cb))
    whole = lambda a: pl.BlockSpec(a.shape, functools.partial(lambda i, n: (0,) * n, n=a.ndim))
    segs = [(wd, (CONV_COLS + off) // wd) for off, wd in zip(SEG_OFF, SEG_W)]
    u_cols = [(512, 1), (LANE, XW_OFF // LANE), (LANE, XA_OFF // LANE), (2 * LANE, XG_OFF // (2 * LANE))]
    any_spec = pl.BlockSpec(memory_space=pl.ANY)
    res = pl.pallas_call(
        body, name="rwkv_pre_bwd", grid=(nb,),
        in_specs=[rev(*s) for s in segs] + [halo(*s) for s in segs] + [rev(*c) for c in u_cols]
                 + [rev(RWKV_DIM, 0)] * n_g + [whole(mu)] + [whole(p_) for p_ in small] + [any_spec],
        out_specs=[any_spec] + [pl.BlockSpec(s, functools.partial(lambda i, n: (0,) * n, n=len(s))) for s in acc_shapes],
        out_shape=[jax.ShapeDtypeStruct(dproj.shape, dproj.dtype)] + [jax.ShapeDtypeStruct(s, F32) for s in acc_shapes],
        scratch_shapes=[pltpu.VMEM((2, tr, RW_PAD), dproj.dtype), pltpu.SemaphoreType.DMA((2,)), pltpu.VMEM((1, RW_PAD), F32)],
        input_output_aliases={24 + n_g: 0},
        compiler_params=_params(("arbitrary",)),
    )(*[proj] * 12, *[u] * 4, *grads, mu, *small, dproj)
    return res


def _local_step(x, p, tgt, w, early_shards, late_shards):
    row = lambda v: v.reshape(1, -1)
    w = dict(w)

    xn1, *gathered = _rowwise("rms_mix", lambda h, g: (_rms(h, g),), [x], [w["norm_mix_g"]], [(D_MODEL, BF16)],
                              gather=early_shards)
    w.update({n: _unshard(n, g_) for n, g_ in zip(_EARLY[1:], gathered[1:])})
    w["w_in"] = _assemble_w_in(gathered[0])
    w["w_lora_up"] = _pad_rows(w["w_lora_up"], LANE)
    w["a_lora_up"] = _pad_rows(w["a_lora_up"], LANE)
    w["g_lora_up"] = _pad_rows(w["g_lora_up"], 2 * LANE)
    proj = _matmul("in_proj", xn1, w["w_in"], "nn", [F32], tm=2048, tn=512, tk=D_MODEL)
    n_cb = CONV_DIM // LANE

    def conv_fwd(blk, cw):
        gb, gc, hx = blk[:, :LANE], blk[:, LANE:2 * LANE], blk[:, 2 * LANE:]
        uu = gc * hx
        return (gb * (uu * cw[2:3] + _shift_down(uu, 1) * cw[1:2] + _shift_down(uu, 2) * cw[0:1]),)

    (y_conv,) = _colwise("conv_fwd", conv_fwd, n_cb, [(proj, 3 * LANE)], [w["conv_w"]], [(CONV_DIM, BF16, LANE)])

    small = [w["w0"], w["a0"], w["k_k"], w["k_a"], w["w_lora_up"], w["a_lora_up"], w["g_lora_up"]]
    def pre_fwd(*xs):
        cur, prev_rows, mu, prm = xs[:6], xs[6:12], xs[12], xs[13:]
        segs = []
        for c_, p_, off, wd in zip(cur, prev_rows, SEG_OFF, SEG_W):
            rows = lax.broadcasted_iota(jnp.int32, c_.shape, 0)
            prev = jnp.where(rows == 0, p_, pltpu.roll(c_, 1, 0))
            segs.append(c_ + mu[:, off:off + wd] * (prev - c_))
        return (jnp.concatenate(segs, axis=1),) + tuple(_rwkv_pre(segs[1], segs[3], segs[4], segs[5], *prm))

    proj_segs = [(proj, wd, (CONV_COLS + off) // wd) for off, wd in zip(SEG_OFF, SEG_W)]
    u, lw, k_h, ra, rb, g = _rowwise("rwkv_pre", pre_fwd, proj_segs, [w["shift_mu"]] + small,
                                     [(RW_PAD, F32)] + [(RWKV_DIM, F32)] * 5, halo=True)
    u_k, u_xw, u_xa, u_xg = (u, 512, 1), (u, LANE, XW_OFF // LANE), (u, LANE, XA_OFF // LANE), (u, 2 * LANE, XG_OFF // (2 * LANE))
    y_rec, zs, late = _rec_fwd(u, lw, k_h, ra, rb, late_shards)
    for n, gathered in zip(_LATE, late):
        w[n] = _unshard(n, gathered)
    post_c = [w["ln_x_g"], w["ln_x_b"], w["r_k"]]
    u_r, u_v = (u, 512, 0), (u, 512, 2)
    (y_rwkv,) = _rowwise("rwkv_post", lambda *xs: (_rwkv_post(*xs),), [y_rec, u_r, k_h, u_v, g], post_c, [(RWKV_DIM, BF16)])
    ycat = jnp.concatenate([y_conv, y_rwkv], axis=1)
    def res_norm(acc, r_, g_):
        h = acc + r_
        return h, _rms(h, g_)

    h1, xn2 = _matmul("out_proj", ycat, w["w_out"], "nn", [F32, BF16], tm=1024, tn=D_MODEL, tk=D_MODEL, extras=[x],
                      consts=[w["norm_mlp_g"]], epilogue=res_norm)

    square = lambda h: h.astype(F32) * h.astype(F32)
    hid = _matmul("mlp_up", xn2, w["w_up"], "nn", [BF16], tm=2048, tn=1024, tk=D_MODEL,
                  epilogue=lambda acc: (jnp.maximum(acc, 0.0),))
    h2, xn3 = _matmul("mlp_down", hid, w["w_down"], "nn", [F32, BF16], tm=512, tn=D_MODEL, tk=D_FF, extras=[h1],
                      consts=[w["norm_ple_g"]], epilogue=res_norm, a_map=square)
    zg = _matmul("ple_gate", xn3, w["w_ple_gate"], "nn", [F32], tm=1024, tn=1024, tk=D_MODEL)
    pp = _matmul("ple_proj", p, w["w_ple_proj"], "nn", [F32], tm=1024, tn=1024, tk=PLE_DIM)

    def head(h2_, zg_, pp_, tg, gf):
        gate = _sigmoid(zg_)
        h3 = h2_ + gate * pp_
        out = _rms(h3, gf)
        err = out - tg
        dh3, dgf = _rms_bwd(h3, gf, err * (1.0 / D_MODEL))
        loss = jnp.sum(jnp.sum(err * err, axis=1, keepdims=True), axis=0, keepdims=True) * (0.5 / D_MODEL)
        return dh3, dh3 * pp_ * gate * (1.0 - gate), dh3 * gate, dgf, loss

    dh3, dzg, dpp, d_norm_final, loss = _rowwise(
        "head", head, [h2, zg, pp, tgt], [row(w["norm_final_g"])], [(D_MODEL, F32), (D_MODEL, BF16), (D_MODEL, BF16)],
        [(1, D_MODEL), (1, 1)])

    d_w_ple_proj = _matmul("d_ple_proj", p, dpp, "tn", [BF16], tm=PLE_DIM, tn=1024, tk=4096)
    d_w_ple_gate = _matmul("d_ple_gate", xn3, dzg, "tn", [BF16], tm=512, tn=1024, tk=4096)

    def norm_bwd(dxn, h, dres, g_):
        dh, dg = _rms_bwd(h, g_, dxn)
        dh = dh + dres
        return dh, dh, dg

    nb = dict(tm=512, tn=D_MODEL, epilogue=norm_bwd, sums=[(1, D_MODEL)])
    dh2, dh2_b, d_norm_ple = _matmul("dx_ple_gate", dzg, w["w_ple_gate"], "nt", [F32, BF16], tk=D_MODEL,
                                     extras=[h2, dh3], consts=[w["norm_ple_g"]], **nb)
    d_w_down = _matmul("d_mlp_down", hid, dh2_b, "tn", [BF16], tm=512, tn=1024, tk=4096, a_map=square)
    dpre = _matmul("dx_mlp_down", dh2_b, w["w_down"], "nt", [BF16], tm=2048, tn=1024, tk=D_MODEL, extras=[hid],
                   epilogue=lambda acc, hid_: (acc * (2.0 * hid_.astype(F32)),))
    d_w_up = _matmul("d_mlp_up", xn2, dpre, "tn", [BF16], tm=1024, tn=1024, tk=4096)
    dh1, dh1_b, d_norm_mlp = _matmul("dx_mlp_up", dpre, w["w_up"], "nt", [F32, BF16], tk=D_FF,
                                     extras=[h1, dh2], consts=[w["norm_mlp_g"]], **nb)
    d_w_out = _matmul("d_out_proj", ycat, dh1_b, "tn", [BF16], tm=512, tn=1024, tk=4096)
    dycat = _matmul("dx_out_proj", dh1_b, w["w_out"], "nt", [F32], tm=1024, tn=1024, tk=D_MODEL)
    late_grads = dict(w_out=d_w_out, w_up=d_w_up, w_down=d_w_down, w_ple_gate=d_w_ple_gate, w_ple_proj=d_w_ple_proj)

    def conv_bwd(dy, blk, cw):
        gb, gc, hx = blk[:, :LANE], blk[:, LANE:2 * LANE], blk[:, 2 * LANE:]
        uu = gc * hx
        u1, u2 = _shift_down(uu, 1), _shift_down(uu, 2)
        dconv = dy * gb
        du = dconv * cw[2:3] + _shift_up(dconv, 1) * cw[1:2] + _shift_up(dconv, 2) * cw[0:1]
        s = lambda z: jnp.sum(z, axis=0, keepdims=True)
        d_blk = jnp.concatenate([dy * (uu * cw[2:3] + u1 * cw[1:2] + u2 * cw[0:1]), du * hx, du * gc], axis=1)
        return d_blk, s(dconv * u2), s(dconv * u1), s(dconv * uu)

    dproj, dcw0, dcw1, dcw2 = _colwise(
        "conv_bwd", conv_bwd, n_cb, [(dycat, LANE), (proj, 3 * LANE)], [w["conv_w"]],
        [(IN_PAD, BF16, 3 * LANE)], [(1, CONV_DIM)] * 3)

    def post_bwd(dy, y, r, k_h_, v, g_, ln_g, ln_b, r_k):
        _, vjp = jax.vjp(_rwkv_post, y, r, k_h_, v, g_, ln_g, ln_b, r_k)
        return vjp(dy)

    dy_rec, dr_p, dk_p, dv_p, dg, d_ln_g, d_ln_b, d_r_k = _rowwise(
        "rwkv_post_bwd", post_bwd, [(dycat, 512, 1), y_rec, u_r, k_h, u_v, g], post_c,
        [(RWKV_DIM, F32)] * 5, [(1, RWKV_DIM)] * 3)
    (dr_r, dv_r, dlw, dk_r, da, db), late_parts = _rec_bwd(
        u, lw, k_h, ra, rb, zs, dy_rec, [_reshard(n, late_grads[n]) for n in _LATE], [True] * len(_LATE))

    dproj, d_mu, d_w0, d_a0, d_k_k, d_k_a, d_wl, d_al, d_gl = _rwkv_pre_bwd(
        proj, u, [dr_p, dr_r, dv_p, dv_r, dlw, dk_p, dk_r, da, db, dg], w["shift_mu"], small, dproj)
    d_w_in = _matmul("d_in_proj", xn1, dproj, "tn", [BF16], tm=1024, tn=896, tk=4096)
    early_grads = dict(conv_w=jnp.concatenate([dcw0, dcw1, dcw2], axis=0),
                       w_lora_up=d_wl[:64], a_lora_up=d_al[:64], g_lora_up=d_gl[:160])
    early_send = [_split_w_in_grad(d_w_in)] + [_reshard(n, early_grads[n]) for n in _EARLY[1:]]
    dx, d_norm_mix, *early_parts = _matmul(
        "dx_in_proj", dproj, w["w_in"], "nt", [F32], tk=IN_PAD, extras=[x, dh1], consts=[w["norm_mix_g"]],
        xch=early_send, xch_scatter=[True] * len(_EARLY), **dict(nb, epilogue=lambda *a: norm_bwd(*a)[1:]))

    grads = dict(
        norm_mix_g=d_norm_mix, shift_mu=d_mu, w0=d_w0, a0=d_a0, k_k=d_k_k, k_a=d_k_a, r_k=d_r_k,
        ln_x_g=d_ln_g, ln_x_b=d_ln_b, norm_mlp_g=d_norm_mlp, norm_ple_g=d_norm_ple, norm_final_g=d_norm_final)
    parts = dict(zip(_EARLY, early_parts))
    parts.update(zip(_LATE, late_parts))
    return loss, dx, grads, parts


def _adam_update(partials, w_ref, m_ref, v_ref, g_ref, d_ref, nm_ref, nv_ref):
    g = partials[0].astype(F32)
    for part in partials[1:]:
        g = g + part.astype(F32)
    nm =ADAM_B1 * m_ref[...] + (1.0 - ADAM_B1) * g
    nv = ADAM_B2 * v_ref[...] + (1.0 - ADAM_B2) * (g * g)
    m_hat = nm / (1.0 - ADAM_B1 ** ADAM_STEP)
    v_hat = nv / (1.0 - ADAM_B2 ** ADAM_STEP)
    g_ref[...] = g
    d_ref[...] = -ADAM_LR * (m_hat / (jnp.sqrt(v_hat) + ADAM_EPS) + ADAM_WD * w_ref[...])
    nm_ref[...] = nm
    nv_ref[...] = nv


def _small_layout(widths):
    offs, off = [], 0
    for wd in list(widths) + [1]:
        offs.append(off)
        off += -(-wd // LANE) * LANE
    return offs, off


def _pack_small(vecs, loss):
    offs, total = _small_layout([v_.shape[1] for v_ in vecs])
    n = len(vecs)

    def body(*refs):
        pieces = []
        for ref in refs[:n + 1]:
            val = ref[...]
            pad = -val.shape[1] % LANE
            pieces += [val] + ([jnp.zeros((1, pad), F32)] if pad else [])
        row = jnp.concatenate(pieces, axis=1)
        rows = lax.broadcasted_iota(jnp.int32, (8, total), 0)
        refs[n + 1][...] = jnp.where(rows == 0, jnp.broadcast_to(row, (8, total)), 0.0)

    return pl.pallas_call(body, name="pack_small", out_shape=jax.ShapeDtypeStruct((8, total), F32))(*vecs, loss)


def _adamw_small(packed, ws, ms, vs):
    n = len(ws)
    offs, _ = _small_layout([w_.shape[1] for w_ in ws])

    def body(p_ref, *refs):
        w_refs, m_refs, v_refs, outs = refs[:n], refs[n:2 * n], refs[2 * n:3 * n], refs[3 * n:]
        for j in range(n):
            cols = pl.ds(offs[j], ws[j].shape[1])
            _adam_update([p_ref[s, 0:1, cols] for s in range(N_DEV)], w_refs[j], m_refs[j], v_refs[j], *outs[4 * j:4 * j + 4])
        total = p_ref[0, 0:1, offs[n]:offs[n] + 1]
        for s in range(1, N_DEV):
            total = total + p_ref[s, 0:1, offs[n]:offs[n] + 1]
        outs[4 * n][...] = total

    res = pl.pallas_call(
        body, name="adamw_small",
        out_shape=[jax.ShapeDtypeStruct(w_.shape, F32) for w_ in ws for _ in range(4)] + [jax.ShapeDtypeStruct((1, 1), F32)],
    )(packed, *ws, *ms, *vs)
    return [res[4 * j:4 * j + 4] for j in range(n)], res[4 * n]


def _adamw(name, parts, w, m, v):
    rows, cols = w.shape[-2:]
    lead = w.ndim - 2
    tr = rows if rows * cols * 4 * 8 <= (4 << 20) else max(8, (4 << 20) // (cols * 4 * 8) // 8 * 8)
    while rows % tr:
        tr -= 8

    def body(p_ref, *refs):
        _adam_update([p_ref[s] for s in range(N_DEV)], *refs)

    blk = pl.BlockSpec((None,) * lead + (tr, cols), lambda i: (0,) * lead + (i, 0))
    return pl.pallas_call(
        body, name=name, grid=(rows // tr,),
        in_specs=[pl.BlockSpec((N_DEV, tr, cols), lambda i: (0, i, 0)), blk, blk, blk], out_specs=[blk] * 4,
        out_shape=[jax.ShapeDtypeStruct(w.shape, F32)] * 4,
        compiler_params=_params(("arbitrary",)),
    )(parts, w, m, v)


def kernel(x, p, norm_mix_g, w_in, conv_w, shift_mu, w_lora_up, w0, a_lora_up, a0, g_lora_up, k_k, k_a, r_k, ln_x_g, ln_x_b, w_out, norm_mlp_g, w_up, w_down, norm_ple_g, w_ple_gate, w_ple_proj, norm_final_g, loss_target, m_norm_mix_g, m_w_in, m_conv_w, m_shift_mu, m_w_lora_up, m_w0, m_a_lora_up, m_a0, m_g_lora_up, m_k_k, m_k_a, m_r_k, m_ln_x_g, m_ln_x_b, m_w_out, m_norm_mlp_g, m_w_up, m_w_down, m_norm_ple_g, m_w_ple_gate, m_w_ple_proj, m_norm_final_g, v_norm_mix_g, v_w_in, v_conv_w, v_shift_mu, v_w_lora_up, v_w0, v_a_lora_up, v_a0, v_g_lora_up, v_k_k, v_k_a, v_r_k, v_ln_x_g, v_ln_x_b, v_w_out, v_norm_mlp_g, v_w_up, v_w_down, v_norm_ple_g, v_w_ple_gate, v_w_ple_proj, v_norm_final_g):
    args = dict(locals())
    wts = {n: args[n] for n in _WEIGHTS}
    mom = {n: args["m_" + n] for n in _WEIGHTS}
    var = {n: args["v_" + n] for n in _WEIGHTS}
    shard2d = lambda a: a.reshape(a.shape[-2:])
    pad_mu = lambda a: _pad_in_cols(jnp.concatenate([jnp.zeros((1, CONV_COLS), F32), a], axis=1))[:, CONV_COLS:]
    unpad_mu = lambda a: _unpad_in_cols(jnp.concatenate([jnp.zeros((1, CONV_COLS), F32), a], axis=1))[:, CONV_COLS:]

    shards = {n: shard2d(wts[n]).astype(BF16 if n in _BF16_GATHER else F32) for n in _SHARDED}
    w = {n: wts[n].reshape(1, -1) for n in _REPLICATED}
    w["shift_mu"] = pad_mu(wts["shift_mu"])

    loss, dx, grads, parts = _local_step(x[0], p[0, 0], loss_target[0], w, [shards[n] for n in _EARLY],
                                         [shards[n] for n in _LATE])

    grads["shift_mu"] = unpad_mu(grads["shift_mu"])
    flat = lambda a: a.reshape(1, -1)
    (small_parts,) = _exchange("gather_small", [_pack_small([flat(grads[n]) for n in _REPLICATED], loss)], [False])

    out = {}
    for n in _SHARDED:
        out[n] = _adamw("adamw_" + n, parts[n], wts[n], mom[n], var[n])
    small, loss_total = _adamw_small(small_parts, *[[flat(d[n]) for n in _REPLICATED] for d in (wts, mom, var)])
    for n, res in zip(_REPLICATED, small):
        out[n] = [r.reshape(wts[n].shape) for r in res]
    return (loss_total[0, 0], dx[None], *[out[n][0] for n in _WEIGHTS], *[out[n][1] for n in _WEIGHTS],
            *[out[n][2] for n in _WEIGHTS], *[out[n][3] for n in _WEIGHTS])
```

```python
import functools

import jax
import jax.numpy as jnp
from jax import lax
from jax.experimental import pallas as pl
from jax.experimental.pallas import tpu as pltpu

F32 = jnp.float32
BF16 = jnp.bfloat16

N_DEV = 8
D_MODEL = 1024
CONV_DIM = 512
RWKV_DIM = 512
HEAD_DIM = 64
N_HEADS = 8
D_FF = 4096
PLE_DIM = 256
RMS_EPS = 1e-6
GN_EPS = 64e-5
L2_EPS = 1e-12
ADAM_LR, ADAM_B1, ADAM_B2, ADAM_EPS, ADAM_WD, ADAM_STEP = 0.001, 0.9, 0.999, 1e-08, 0.01, 10

CONV_COLS = 3 * CONV_DIM
RW_PAD = 2048
IN_PAD = CONV_COLS + RW_PAD
IN_COLS = 3360
XW_OFF, XA_OFF, XG_OFF = 1536, 1664, 1792
REC_CHUNK = 128
REC_PASSES = 1
ROW_BLOCK = 256
LANE = 128
VMEM_LIMIT = 56 * 1024 * 1024


def _dims(dn, ndim):
    if ndim == 3:
        return {"nn": (((2,), (1,)), ((0,), (0,))), "nt": (((2,), (2,)), ((0,), (0,))),
                "tn": (((1,), (1,)), ((0,), (0,)))}[dn]
    return {"nn": (((1,), (0,)), ((), ())), "nt": (((1,), (1,)), ((), ())), "tn": (((0,), (0,)), ((), ()))}[dn]


def _split2(x):
    hi = x.astype(BF16)
    return hi, (x - hi.astype(F32)).astype(BF16)


def _mm_raw(x, y, dn, passes):
    f = lambda p, q: lax.dot_general(p, q, _dims(dn, x.ndim), preferred_element_type=F32)
    if passes == 1:
        return f(x.astype(BF16), y.astype(BF16))
    xh, xl = _split2(x)
    yh, yl = _split2(y)
    if passes == 2:
        return f(xh, yh) + f(xh, yl)
    return f(xh, yh) + f(xh, yl) + f(xl, yh)


@functools.partial(jax.custom_vjp, nondiff_argnums=(2, 3))
def _mm(x, y, dn, passes):
    return _mm_raw(x, y, dn, passes)


def _mm_fwd(x, y, dn, passes):
    return _mm_raw(x, y, dn, passes), (x, y)


def _mm_bwd(dn, passes, res, d):
    x, y = res
    if dn == "nn":
        return _mm(d, y, "nt", passes), _mm(x, d, "tn", passes)
    if dn == "nt":
        return _mm(d, y, "nn", passes), _mm(d, x, "tn", passes)
    return _mm(y, d, "nt", passes), _mm(x, d, "nn", passes)


_mm.defvjp(_mm_fwd, _mm_bwd)


def _head_ones():
    i = lax.broadcasted_iota(jnp.int32, (RWKV_DIM, RWKV_DIM), 0) // HEAD_DIM
    j = lax.broadcasted_iota(jnp.int32, (RWKV_DIM, RWKV_DIM), 1) // HEAD_DIM
    return (i == j).astype(BF16)


def _hsum_raw(x):
    ones = _head_ones()
    f = lambda p: lax.dot_general(p, ones, _dims("nn", 2), preferred_element_type=F32)
    x1, x2 = _split2(x)
    return f(x1) + f(x2)


@jax.custom_vjp
def _hsum(x):
    return _hsum_raw(x)


_hsum.defvjp(lambda x: (_hsum_raw(x), None), lambda _, d: (_hsum(d),))


def _sigmoid(x):
    return 1.0 / (1.0 + jnp.exp(-x))


def _softplus(x):
    return jnp.maximum(x, 0.0) + jnp.log(1.0 + jnp.exp(-jnp.abs(x)))


def _params(sem):
    return pltpu.CompilerParams(dimension_semantics=sem, vmem_limit_bytes=VMEM_LIMIT)


def _rowwise(name, fn, rows, consts, row_outs, acc_outs=(), tr=ROW_BLOCK, halo=False, gather=()):
    rows = [r if isinstance(r, tuple) else (r, r.shape[1], 0) for r in rows]
    t_len = rows[0][0].shape[0]
    tr = min(tr, t_len)
    n_r, n_c, n_o, n_a, n_x = len(rows), len(consts), len(row_outs), len(acc_outs), len(gather)
    n_h = n_r if halo else 0
    sub = 8
    x_specs, x_shapes, x_sems = _exchange_io(gather, [False] * n_x) if n_x else ([], [], [])
    nb = t_len // tr

    def body(*refs):
        if n_x:
            n_in = n_r + n_h + n_c
            start, forward, wait = _gather_plan(refs[n_in:n_in + n_x], refs[len(refs) - 3 - n_x:len(refs) - 3], *refs[len(refs) - 3:])
            pl.when(pl.program_id(0) == 0)(start)
            refs = refs[:n_in] + refs[n_in + n_x:len(refs) - 3 - n_x]
        ins = [r[...] for r in refs[:n_r]]
        ins += [jnp.where(pl.program_id(0) == 0, 0.0, r[sub - 1:sub, :]) for r in refs[n_r:n_r + n_h]]
        ins += [r[...] for r in refs[n_r + n_h:n_r + n_h + n_c]]
        refs = refs[:n_r] + refs[n_r + n_h:]
        outs = fn(*ins)
        o_refs = refs[n_r + n_c:n_r + n_c + n_o]
        a_refs = refs[n_r + n_c + n_o:]
        for o_ref, val in zip(o_refs, outs[:n_o]):
            o_ref[...] = val.astype(o_ref.dtype)
        if n_a:
            first = pl.program_id(0) == 0

            @pl.when(first)
            def _():
                for a_ref, val in zip(a_refs, outs[n_o:]):
                    a_ref[...] = val

            @pl.when(jnp.logical_not(first))
            def _():
                for a_ref, val in zip(a_refs, outs[n_o:]):
                    a_ref[...] += val

        if n_x:
            @pl.when(pl.program_id(0) == nb - 1)
            def _():
                for j in range(n_x):
                    forward(j)
                wait()

    in_specs = [pl.BlockSpec((tr, w), functools.partial(lambda i, c: (i, c), c=cb)) for _, w, cb in rows]
    if halo:
        in_specs += [pl.BlockSpec((sub, w), functools.partial(lambda i, c: (jnp.maximum(i * (tr // sub) - 1, 0), c), c=cb))
                     for _, w, cb in rows]
    in_specs += [pl.BlockSpec(c.shape, functools.partial(lambda i, n: (0,) * n, n=c.ndim)) for c in consts]
    out_specs = [pl.BlockSpec((tr, w), lambda i: (i, 0)) for w, _ in row_outs]
    out_specs += [pl.BlockSpec(s, functools.partial(lambda i, n: (0,) * n, n=len(s))) for s in acc_outs]
    out_shape = [jax.ShapeDtypeStruct((t_len, w), dt) for w, dt in row_outs]
    out_shape += [jax.ShapeDtypeStruct(s, F32) for s in acc_outs]
    return pl.pallas_call(
        body, name=name, grid=(nb,), in_specs=in_specs + x_specs, out_specs=out_specs + x_specs,
        out_shape=out_shape + x_shapes, scratch_shapes=x_sems,
        compiler_params=pltpu.CompilerParams(dimension_semantics=("arbitrary",), vmem_limit_bytes=VMEM_LIMIT,
                                             has_side_effects=bool(n_x)),
    )(*[r[0] for r in rows], *([r[0] for r in rows] if halo else []), *consts, *gather)


def _colwise(name, fn, n_blocks, cols, prms, col_outs, prm_outs=()):
    t_len = cols[0][0].shape[0]
    n_i = len(cols) + len(prms)

    def body(*refs):
        outs = fn(*[r[...] for r in refs[:n_i]])
        for o_ref, val in zip(refs[n_i:], outs):
            o_ref[...] = val.astype(o_ref.dtype)

    spec = lambda r, w: pl.BlockSpec((r, w), lambda j: (0, j))
    in_specs = [spec(t_len, w) for _, w in cols] + [spec(a.shape[0], LANE) for a in prms]
    out_specs = [spec(t_len, bw) for _, _, bw in col_outs] + [spec(r, LANE) for r, _ in prm_outs]
    out_shape = [jax.ShapeDtypeStruct((t_len, w), dt) for w, dt, _ in col_outs]
    out_shape += [jax.ShapeDtypeStruct((r, w), F32) for r, w in prm_outs]
    return pl.pallas_call(
        body, name=name, grid=(n_blocks,), in_specs=in_specs, out_specs=out_specs, out_shape=out_shape,
        compiler_params=_params(("arbitrary",)),
    )(*[c[0] for c in cols], *prms)


def _matmul(name, a, b, dn, outs, *, tm, tn, tk, extras=(), consts=(), epilogue=None, sums=(), xch=(), xch_scatter=(),
            a_map=None, col_blocks_out=False):
    if dn == "nn":
        (m, k), n = a.shape, b.shape[1]
    elif dn == "nt":
        (m, k), n = a.shape, b.shape[0]
    else:
        (k, m), n = a.shape, b.shape[1]
    tm, tn, tk = min(tm, m), min(tn, n), min(tk, k)
    nk = k // tk
    grid = (m // tm, n // tn, nk)
    assert nk == 1 and (not sums or grid[1] == 1)
    a_spec = pl.BlockSpec((tk, tm), lambda i, j, q: (q, i)) if dn == "tn" else pl.BlockSpec((tm, tk), lambda i, j, q: (i, q))
    b_spec = pl.BlockSpec((tn, tk), lambda i, j, q: (j, q)) if dn == "nt" else pl.BlockSpec((tk, tn), lambda i, j, q: (q, j))
    o_spec = pl.BlockSpec((tm, tn), lambda i, j, q: (i, j))
    c_spec = pl.BlockSpec((1, tn), lambda i, j, q: (0, j))
    n_e, n_c, n_o, n_s, n_x = len(extras), len(consts), len(outs), len(sums), len(xch)
    x_specs, x_shapes, x_sems = _exchange_io(xch, xch_scatter) if n_x else ([], [], [])

    def body(*refs):
        a_ref, b_ref = refs[:2]
        e_refs = refs[2:2 + n_e + n_c]
        x_in = refs[2 + n_e + n_c:2 + n_e + n_c + n_x]
        rest = refs[2 + n_e + n_c + n_x:]
        o_refs, s_refs, x_out, scratch = rest[:n_o], rest[n_o:n_o + n_s], rest[n_o + n_s:n_o + n_s + n_x], rest[n_o + n_s + n_x:]
        step = (pl.program_id(0) * grid[1] + pl.program_id(1)) * nk + pl.program_id(2)
        if n_x:
            start, wait = _exchange_plan(x_in, x_out, xch_scatter, *scratch[len(scratch) - 3:])
            pl.when(step == 0)(start)
        a_blk = a_ref[...] if a_map is None else a_map(a_ref[...])
        acc = lax.dot_general(a_blk.astype(BF16), b_ref[...].astype(BF16), _dims(dn, 2), preferred_element_type=F32)
        vals = (acc,) if epilogue is None else epilogue(acc, *[e[...] for e in e_refs])
        for o_ref, val in zip(o_refs, vals[:n_o]):
            o_ref[...] = val.astype(o_ref.dtype)
        if n_s:
            @pl.when(step == 0)
            def _():
                for s_ref, val in zip(s_refs, vals[n_o:]):
                    s_ref[...] = val

            @pl.when(step > 0)
            def _():
                for s_ref, val in zip(s_refs, vals[n_o:]):
                    s_ref[...] += val

        if n_x:
            pl.when(step == grid[0] * grid[1] * nk - 1)(wait)

    plain = not (n_s or n_x)
    res = pl.pallas_call(
        body, name=name, grid=grid,
        in_specs=[a_spec, b_spec] + [o_spec] * n_e + [c_spec] * n_c + x_specs,
        out_specs=[pl.BlockSpec((None, tm, tn), lambda i, j, q: (j, i, 0)) if col_blocks_out else o_spec] * n_o
                  + [c_spec] * n_s + x_specs,
        out_shape=[jax.ShapeDtypeStruct((n // tn, m, tn) if col_blocks_out else (m, n), dt) for dt in outs] + [jax.ShapeDtypeStruct(s, F32) for s in sums] + x_shapes,
        scratch_shapes=x_sems,
        compiler_params=pltpu.CompilerParams(
            dimension_semantics=("parallel", "parallel", "arbitrary") if plain else ("arbitrary",) * 3,
            vmem_limit_bytes=VMEM_LIMIT, has_side_effects=bool(n_x)),
    )(a, b, *extras, *consts, *xch)
    return res[0] if len(res) == 1 else res


def _rms(h, g):
    return h * lax.rsqrt(jnp.mean(h * h, axis=-1, keepdims=True) + RMS_EPS) * g


def _rms_bwd(h, g, dy):
    rs = lax.rsqrt(jnp.mean(h * h, axis=-1, keepdims=True) + RMS_EPS)
    n = h * rs
    dn = dy * g
    dh = rs * (dn - n * jnp.mean(dn * n, axis=-1, keepdims=True))
    return dh, jnp.sum(dy * n, axis=0, keepdims=True)


def _rwkv_pre(k, xw, xa, xg, w0, a0, k_k, k_a, wl, al, gl):
    zw = w0 + _mm(jnp.tanh(xw), wl, "nn", 1)
    lw = -jnp.exp(-_softplus(-zw) - 0.5)
    iclr = _sigmoid(a0 + _mm(xa, al, "nn", 1))
    g = _mm(_sigmoid(xg), gl, "nn", 1)
    kk0 = k * k_k
    kk = kk0 / jnp.maximum(jnp.sqrt(_hsum(kk0 * kk0)), L2_EPS)
    k_h = k * (1.0 + (iclr - 1.0) * k_a)
    return lw, k_h, -kk, kk * iclr, g


def _rwkv_post(y, r, k_h, v, g, ln_g, ln_b, r_k):
    mu = _hsum(y) * (1.0 / HEAD_DIM)
    yc = y - mu
    var = _hsum(yc * yc) * (1.0 / HEAD_DIM)
    yo = yc * lax.rsqrt(var + GN_EPS) * ln_g + ln_b
    bonus = _hsum(r * k_h * r_k) * v
    return (yo + bonus) * g


def _shift_down(x, n):
    rows = lax.broadcasted_iota(jnp.int32, x.shape, 0)
    return jnp.where(rows < n, 0.0, pltpu.roll(x, n, 0))


def _shift_up(x, n):
    t_len = x.shape[0]
    rows = lax.broadcasted_iota(jnp.int32, x.shape, 0)
    return jnp.where(rows >= t_len - n, 0.0, pltpu.roll(x, t_len - n, 0))


def _exchange_plan(ins, outs, scatter, send_sems, recv_sems, local_sems):
    x, y, c = lax.axis_index("x"), lax.axis_index("y"), lax.axis_index("c")
    me = 4 * x + 2 * y + c

    def local(i):
        return pltpu.make_async_copy(ins[i].at[me] if scatter[i] else ins[i], outs[i].at[me], local_sems.at[i])

    def send(i, rel):
        return pltpu.make_async_remote_copy(
            src_ref=ins[i].at[me ^ rel] if scatter[i] else ins[i], dst_ref=outs[i].at[me],
            send_sem=send_sems.at[i, rel - 1], recv_sem=recv_sems.at[i, rel - 1],
            device_id=(x ^ (rel >> 2), y ^ ((rel >> 1) & 1), c ^ (rel & 1)), device_id_type=pl.DeviceIdType.MESH)

    def landed(i, rel):
        slot = outs[i].at[me ^ rel]
        return pltpu.make_async_remote_copy(
            src_ref=slot, dst_ref=slot, send_sem=send_sems.at[i, rel - 1], recv_sem=recv_sems.at[i, rel - 1],
            device_id=(x, y, c), device_id_type=pl.DeviceIdType.MESH)

    def start():
        for i in range(len(ins)):
            local(i).start()
            for rel in range(1, N_DEV):
                send(i, rel).start()

    def wait():
        for i in range(len(ins)):
            local(i).wait()
            for rel in range(1, N_DEV):
                landed(i, rel).wait_recv()
            for rel in range(1, N_DEV):
                send(i, rel).wait_send()

    return start, wait


def _gather_plan(ins, outs, send_sems, recv_sems, local_sems):
    x, y, c = lax.axis_index("x"), lax.axis_index("y"), lax.axis_index("c")
    me = 4 * x + 2 * y + c
    direct, chips = (1, 2, 4, 6), (2, 4, 6)

    def local(i):
        return pltpu.make_async_copy(ins[i], outs[i].at[me], local_sems.at[i])

    def send(i, rel):
        return pltpu.make_async_remote_copy(
            src_ref=ins[i], dst_ref=outs[i].at[me], send_sem=send_sems.at[i, rel - 1], recv_sem=recv_sems.at[i, rel - 1],
            device_id=(x ^ (rel >> 2), y ^ ((rel >> 1) & 1), c ^ (rel & 1)), device_id_type=pl.DeviceIdType.MESH)

    def passed(i, rel):
        slot = outs[i].at[me ^ rel]
        return pltpu.make_async_remote_copy(
            src_ref=slot, dst_ref=slot, send_sem=send_sems.at[i, rel], recv_sem=recv_sems.at[i, rel],
            device_id=(x, y, 1 - c), device_id_type=pl.DeviceIdType.MESH)

    def landed(i, rel):
        slot = outs[i].at[me ^ rel]
        return pltpu.make_async_remote_copy(
            src_ref=slot, dst_ref=slot, send_sem=send_sems.at[i, rel - 1], recv_sem=recv_sems.at[i, rel - 1],
            device_id=(x, y, c), device_id_type=pl.DeviceIdType.MESH)

    def start():
        for i in range(len(ins)):
            local(i).start()
            for rel in direct:
                send(i, rel).start()

    def forward(i):
        for rel in chips:
            landed(i, rel).wait_recv()
            passed(i, rel).start()

    def wait():
        for i in range(len(ins)):
            local(i).wait()
            for rel in (1, 3, 5, 7):
                landed(i, rel).wait_recv()
            for rel in direct:
                send(i, rel).wait_send()
            for rel in chips:
                passed(i, rel).wait_send()

    return start, forward, wait


def _exchange_io(arrays, scatter):
    n = len(arrays)
    any_spec = pl.BlockSpec(memory_space=pl.ANY)
    out_shape = [jax.ShapeDtypeStruct(a.shape if sc else (N_DEV,) + a.shape, a.dtype) for a, sc in zip(arrays, scatter)]
    sems = [pltpu.SemaphoreType.DMA((n, N_DEV - 1)), pltpu.SemaphoreType.DMA((n, N_DEV - 1)), pltpu.SemaphoreType.DMA((n,))]
    return [any_spec] * n, out_shape, sems


def _exchange(name, arrays, scatter):
    n = len(arrays)
    specs, out_shape, sems = _exchange_io(arrays, scatter)

    def body(*refs):
        if any(scatter):
            start, wait = _exchange_plan(refs[:n], refs[n:2 * n], scatter, *refs[2 * n:])
            start()
        else:
            start, forward, wait = _gather_plan(refs[:n], refs[n:2 * n], *refs[2 * n:])
            start()
            for i in range(n):
                forward(i)
        wait()

    return pl.pallas_call(
        body, name=name, in_specs=specs, out_specs=specs, out_shape=out_shape, scratch_shapes=sems,
        compiler_params=pltpu.CompilerParams(has_side_effects=True),
    )(*arrays)


def _tri_powers(low):
    powers, n = [low], 1
    while 2 * n < low.shape[-1]:
        powers.append(_mm(powers[-1], powers[-1], "nn", REC_PASSES))
        n *= 2
    return powers


@jax.custom_vjp
def _tri_solve(low, rhs):
    for p in _tri_powers(low):
        rhs = rhs + _mm(p, rhs, "nn", REC_PASSES)
    return rhs


def _tri_solve_fwd(low, rhs):
    powers = _tri_powers(low)
    for p in powers:
        rhs = rhs + _mm(p, rhs, "nn", REC_PASSES)
    return rhs, (powers, rhs)


def _tri_solve_bwd(res, d):
    powers, u = res
    for p in powers:
        d = d + _mm(p, d, "tn", REC_PASSES)
    return _mm(d, u, "nt", REC_PASSES), d


_tri_solve.defvjp(_tri_solve_fwd, _tri_solve_bwd)


def _chunk_fwd(z0, r, lw, k, v, a, b):
    n_h, c, n_k = r.shape
    mm = functools.partial(_mm, passes=REC_PASSES)
    gram = functools.partial(_mm, passes=2)
    ti = lax.broadcasted_iota(jnp.int32, (c, c), 0)
    si = lax.broadcasted_iota(jnp.int32, (c, c), 1)
    strict, incl = si < ti, si <= ti
    cum = _mm(jnp.broadcast_to(incl.astype(F32), (n_h, c, c)), lw, "nn", 3)
    cum_end = cum[:, c - 1:c, :]
    e_neg, e_end = jnp.exp(-cum), jnp.exp(cum_end - cum)
    x2 = jnp.concatenate([a * jnp.exp(cum - lw), r * jnp.exp(cum)], axis=1)
    y2 = jnp.concatenate([b * e_neg, k * e_neg], axis=1)
    mask = jnp.concatenate([jnp.concatenate([strict, strict], axis=1), jnp.concatenate([incl, incl], axis=1)], axis=0)
    g2 = jnp.where(mask, gram(x2, y2, "nt"), 0.0)
    t2 = mm(x2, z0, "nn") + mm(g2[:, :, c:], v, "nn")
    u = _tri_solve(g2[:, :c, :c], t2[:, :c])
    y = t2[:, c:] + mm(g2[:, c:, :c], u, "nn")
    ki = lax.broadcasted_iota(jnp.int32, (n_k, n_k), 0)
    kj = lax.broadcasted_iota(jnp.int32, (n_k, n_k), 1)
    dmat = jnp.where(ki == kj, jnp.broadcast_to(jnp.exp(cum_end), (n_h, n_k, n_k)), 0.0)
    z_end = mm(dmat, z0, "nn") + mm(jnp.concatenate([b * e_end, k * e_end], axis=1), jnp.concatenate([u, v], axis=1), "tn")
    return y, z_end


def _heads(x):
    return jnp.stack([x[:, h * HEAD_DIM:(h + 1) * HEAD_DIM] for h in range(N_HEADS)])


def _unheads(x):
    return jnp.concatenate([x[h] for h in range(N_HEADS)], axis=-1)


def _rec_params():
    return pltpu.CompilerParams(dimension_semantics=("arbitrary",), vmem_limit_bytes=VMEM_LIMIT, has_side_effects=True)


def _rec_fwd(u, lw, k, a, b, xch):
    t_len = lw.shape[0]
    c = min(REC_CHUNK, t_len)
    nc = t_len // c
    n_x = len(xch)
    x_specs, x_shapes, x_sems = _exchange_io(xch, [False] * n_x)
    sizes = [a_.size * a_.dtype.itemsize for a_ in xch]
    pass_step = [min(nc - 1, int(0.9 * nc * sum(sizes[:j + 1]) / sum(sizes)) + 1) for j in range(n_x)]

    def body(*refs):
        r_ref, v_ref, lw_ref, k_ref, a_ref, b_ref = refs[:6]
        x_in = refs[6:6 + n_x]
        y_ref, zs_ref = refs[6 + n_x:8 + n_x]
        x_out = refs[8 + n_x:8 + 2 * n_x]
        z_scr = refs[8 + 2 * n_x]
        start, forward, wait = _gather_plan(x_in, x_out, *refs[9 + 2 * n_x:])
        i = pl.program_id(0)

        @pl.when(i == 0)
        def _():
            start()
            z_scr[...] = jnp.zeros_like(z_scr)

        z0 = z_scr[...]
        zs_ref[0] = z0
        y, z_end = _chunk_fwd(z0, _heads(r_ref[...]), _heads(lw_ref[...]), _heads(k_ref[...]), _heads(v_ref[...]),
                              _heads(a_ref[...]), _heads(b_ref[...]))
        y_ref[...] = _unheads(y)
        z_scr[...] = z_end

        for j in range(n_x):
            pl.when(i == pass_step[j])(functools.partial(forward, j))

        @pl.when(i == nc - 1)
        def _():
            wait()

    blk = lambda cb: pl.BlockSpec((c, RWKV_DIM), functools.partial(lambda i, q: (i, q), q=cb))
    res = pl.pallas_call(
        body, name="rwkv_rec_fwd", grid=(nc,),
        in_specs=[blk(0), blk(2)] + [blk(0)] * 4 + x_specs,
        out_specs=[blk(0), pl.BlockSpec((1, N_HEADS, HEAD_DIM, HEAD_DIM), lambda i: (i, 0, 0, 0))] + x_specs,
        out_shape=[jax.ShapeDtypeStruct((t_len, RWKV_DIM), F32),
                   jax.ShapeDtypeStruct((nc, N_HEADS, HEAD_DIM, HEAD_DIM), F32)] + x_shapes,
        scratch_shapes=[pltpu.VMEM((N_HEADS, HEAD_DIM, HEAD_DIM), F32)] + x_sems,
        compiler_params=_rec_params(),
    )(u, u, lw, k, a, b, *xch)
    return res[0], res[1], res[2:]


def _rec_bwd(u, lw, k, a, b, zs, dy, xch, xch_scatter):
    t_len = lw.shape[0]
    c = min(REC_CHUNK, t_len)
    nc = t_len // c
    n_x = len(xch)
    x_specs, x_shapes, x_sems = _exchange_io(xch, xch_scatter)

    def body(*refs):
        r_ref, v_ref, lw_ref, k_ref, a_ref, b_ref, zs_ref, dy_ref = refs[:8]
        x_in = refs[8:8 + n_x]
        g_refs = refs[8 + n_x:14 + n_x]
        x_out = refs[14 + n_x:14 + 2 * n_x]
        dz_scr = refs[14 + 2 * n_x]
        start, wait = _exchange_plan(x_in, x_out, xch_scatter, *refs[15 + 2 * n_x:])
        i = pl.program_id(0)

        @pl.when(i == 0)
        def _():
            start()
            dz_scr[...] = jnp.zeros_like(dz_scr)

        _, vjp = jax.vjp(_chunk_fwd, zs_ref[0], _heads(r_ref[...]), _heads(lw_ref[...]), _heads(k_ref[...]),
                         _heads(v_ref[...]), _heads(a_ref[...]), _heads(b_ref[...]))
        dz0, dr, dlw, dk, dv, da, db = vjp((_heads(dy_ref[...]), dz_scr[...]))
        for ref, val in zip(g_refs, (dr, dv, dlw, dk, da, db)):
            ref[...] = _unheads(val)
        dz_scr[...] = dz0

        @pl.when(i == nc - 1)
        def _():
            wait()

    blk = lambda cb: pl.BlockSpec((c, RWKV_DIM), functools.partial(lambda i, q: (nc - 1 - i, q), q=cb))
    res = pl.pallas_call(
        body, name="rwkv_rec_bwd", grid=(nc,),
        in_specs=[blk(0), blk(2)] + [blk(0)] * 4
                 + [pl.BlockSpec((1, N_HEADS, HEAD_DIM, HEAD_DIM), lambda i: (nc - 1 - i, 0, 0, 0)), blk(0)] + x_specs,
        out_specs=[blk(0)] * 6 + x_specs,
        out_shape=[jax.ShapeDtypeStruct((t_len, RWKV_DIM), F32)] * 6 + x_shapes,
        scratch_shapes=[pltpu.VMEM((N_HEADS, HEAD_DIM, HEAD_DIM), F32)] + x_sems,
        compiler_params=_rec_params(),
    )(u, u, lw, k, a, b, zs, dy, *xch)
    return res[:6], res[6:]


_EARLY = ["w_in", "conv_w", "w_lora_up", "a_lora_up", "g_lora_up"]
_LATE = ["w_out", "w_up", "w_down", "w_ple_gate", "w_ple_proj"]
_SHARDED = _EARLY + _LATE
_COL_SHARDED = {"w_in", "conv_w", "w_lora_up", "a_lora_up", "g_lora_up", "w_up", "w_ple_proj"}
_BF16_GATHER = {"w_in", "w_out", "w_up", "w_down", "w_ple_gate", "w_ple_proj"}
_REPLICATED = ["norm_mix_g", "shift_mu", "w0", "a0", "k_k", "k_a", "r_k", "ln_x_g", "ln_x_b", "norm_mlp_g", "norm_ple_g",
               "norm_final_g"]
_WEIGHTS = ["norm_mix_g", "w_in", "conv_w", "shift_mu", "w_lora_up", "w0", "a_lora_up", "a0", "g_lora_up", "k_k", "k_a", "r_k",
            "ln_x_g", "ln_x_b", "w_out", "norm_mlp_g", "w_up", "w_down", "norm_ple_g", "w_ple_gate", "w_ple_proj", "norm_final_g"]


def _unshard(name, g):
    if name in _COL_SHARDED:
        return jnp.moveaxis(g, 0, 1).reshape(g.shape[1], N_DEV * g.shape[2])
    return g.reshape(N_DEV * g.shape[1], g.shape[2])


def _reshard(name, full):
    if name in _COL_SHARDED:
        return jnp.moveaxis(full.reshape(full.shape[0], N_DEV, full.shape[1] // N_DEV), 1, 0)
    return full.reshape(N_DEV, full.shape[0] // N_DEV, full.shape[1])


def _pad_in_cols(a):
    z = lambda n: jnp.zeros(a.shape[:-1] + (n,), a.dtype)
    conv = [a[..., part * CONV_DIM + j * LANE:part * CONV_DIM + (j + 1) * LANE] for j in range(CONV_DIM // LANE) for part in range(3)]
    return jnp.concatenate(conv + [a[..., CONV_COLS:3136], z(64), a[..., 3136:3200], z(64), a[..., 3200:3360], z(96)], axis=-1)


def _unpad_in_cols(a):
    conv = [a[..., (3 * j + part) * LANE:(3 * j + part + 1) * LANE] for part in range(3) for j in range(CONV_DIM // LANE)]
    return jnp.concatenate(conv + [a[..., CONV_COLS:3136], a[..., 3200:3264], a[..., 3328:3488]], axis=-1)


def _assemble_w_in(g):
    n_dev, rows, cols = g.shape

    def body(g_ref, o_ref):
        o_ref[...] = _pad_in_cols(jnp.concatenate([g_ref[d] for d in range(n_dev)], axis=1))

    return pl.pallas_call(
        body, name="w_in_assemble", grid=(rows // ROW_BLOCK,),
        in_specs=[pl.BlockSpec((n_dev, ROW_BLOCK, cols), lambda i: (0, i, 0))],
        out_specs=pl.BlockSpec((ROW_BLOCK, IN_PAD), lambda i: (i, 0)),
        out_shape=jax.ShapeDtypeStruct((rows, IN_PAD), g.dtype), compiler_params=_params(("arbitrary",)),
    )(g)


def _split_w_in_grad(dw):
    rows = dw.shape[0]
    cols = IN_COLS // N_DEV

    def body(d_ref, o_ref):
        full = _unpad_in_cols(d_ref[...])
        for d in range(N_DEV):
            o_ref[d] = full[:, cols * d:cols * (d + 1)]

    return pl.pallas_call(
        body, name="w_in_grad_split", grid=(rows // ROW_BLOCK,),
        in_specs=[pl.BlockSpec((ROW_BLOCK, IN_PAD), lambda i: (i, 0))],
        out_specs=pl.BlockSpec((N_DEV, ROW_BLOCK, cols), lambda i: (0, i, 0)),
        out_shape=jax.ShapeDtypeStruct((N_DEV, rows, cols), dw.dtype), compiler_params=_params(("arbitrary",)),
    )(dw)


def _pad_rows(a, rows):
    return jnp.concatenate([a, jnp.zeros((rows - a.shape[0],) + a.shape[1:], a.dtype)], axis=0)


SEG_W = [RWKV_DIM, RWKV_DIM, RWKV_DIM, LANE, LANE, 2 * LANE]
SEG_OFF = [0, 512, 1024, XW_OFF, XA_OFF, XG_OFF]


def _rwkv_pre_bwd(proj, u, grads, mu, small, dproj):
    t_len = u.shape[0]
    tr = min(ROW_BLOCK, t_len)
    nb = t_len // tr
    sub = 8
    n_g = len(grads)
    acc_shapes = [(1, RW_PAD)] + [(1, RWKV_DIM)] * 4 + [(LANE, RWKV_DIM), (LANE, RWKV_DIM), (2 * LANE, RWKV_DIM)]

    def body(*refs):
        seg_refs, halo_refs = refs[:6], refs[6:12]
        k_ref, xw_ref, xa_ref, xg_ref = refs[12:16]
        g_refs = refs[16:16 + n_g]
        mu_ref = refs[16 + n_g]
        prm_refs = refs[17 + n_g:24 + n_g]
        out_hbm = refs[25 + n_g]
        acc_refs = refs[26 + n_g:26 + n_g + len(acc_shapes)]
        vbuf, sems, carry = refs[26 + n_g + len(acc_shapes):]
        i = pl.program_id(0)
        blk = nb - 1 - i
        dr1, dr2, dv1, dv2, dlw, dk1, dk2, da, db, dg = [g[...] for g in g_refs]
        _, vjp = jax.vjp(_rwkv_pre, k_ref[...], xw_ref[...], xa_ref[...], xg_ref[...], *[p_[...] for p_ in prm_refs])
        dk, dxw, dxa, dxg, *dprm = vjp((dlw, dk1 + dk2, da, db, dg))
        du = jnp.concatenate([dr1 + dr2, dk, dv1 + dv2, dxw, dxa, dxg], axis=1)
        mu_v = mu_ref[...]

        @pl.when(i == 0)
        def _():
            carry[...] = jnp.zeros_like(carry)

        rows = lax.broadcasted_iota(jnp.int32, du.shape, 0)
        nxt = jnp.where(rows == tr - 1, carry[...], pltpu.roll(du, tr - 1, 0))
        d_rw = du - mu_v * du + mu_v * nxt
        d_mu = []
        for s_ref, h_ref, off, wd in zip(seg_refs, halo_refs, SEG_OFF, SEG_W):
            cur = s_ref[...]
            r0 = lax.broadcasted_iota(jnp.int32, cur.shape, 0)
            prev = jnp.where(r0 == 0, jnp.where(blk == 0, 0.0, h_ref[sub - 1:sub, :]), pltpu.roll(cur, 1, 0))
            d_mu.append(jnp.sum(du[:, off:off + wd] * (prev - cur), axis=0, keepdims=True))
        sums = [jnp.concatenate(d_mu, axis=1)] + list(dprm)

        @pl.when(i == 0)
        def _():
            for a_ref, val in zip(acc_refs, sums):
                a_ref[...] = val

        @pl.when(i > 0)
        def _():
            for a_ref, val in zip(acc_refs, sums):
                a_ref[...] += val

        carry[...] = du[0:1, :]
        slot = i % 2

        def writeback(s, b):
            return pltpu.make_async_copy(vbuf.at[s], out_hbm.at[pl.ds(b * tr, tr), pl.ds(CONV_COLS, RW_PAD)], sems.at[s])

        @pl.when(i >= 2)
        def _():
            writeback(slot, blk + 2).wait()

        vbuf[slot] = d_rw.astype(vbuf.dtype)
        writeback(slot, blk).start()

        @pl.when(i == nb - 1)
        def _():
            writeback(slot, blk).wait()
            if nb > 1:
                writeback(1 - slot, blk + 1).wait()

    rev = lambda w_, cb: pl.BlockSpec((tr, w_), functools.partial(lambda i, c: (nb - 1 - i, c), c=cb))
    halo = lambda w_, cb: pl.BlockSpec((sub, w_), functools.partial(
        lambda i, c: (jnp.maximum((nb - 1 - i) * (tr // sub) - 1, 0), c), c=cb))
    whole = lambda a: pl.BlockSpec(a.shape, functools.partial(lambda i, n: (0,) * n, n=a.ndim))
    segs = [(wd, (CONV_COLS + off) // wd) for off, wd in zip(SEG_OFF, SEG_W)]
    u_cols = [(512, 1), (LANE, XW_OFF // LANE), (LANE, XA_OFF // LANE), (2 * LANE, XG_OFF // (2 * LANE))]
    any_spec = pl.BlockSpec(memory_space=pl.ANY)
    res = pl.pallas_call(
        body, name="rwkv_pre_bwd", grid=(nb,),
        in_specs=[rev(*s) for s in segs] + [halo(*s) for s in segs] + [rev(*c) for c in u_cols]
                 + [rev(RWKV_DIM, 0)] * n_g + [whole(mu)] + [whole(p_) for p_ in small] + [any_spec],
        out_specs=[any_spec] + [pl.BlockSpec(s, functools.partial(lambda i, n: (0,) * n, n=len(s))) for s in acc_shapes],
        out_shape=[jax.ShapeDtypeStruct(dproj.shape, dproj.dtype)] + [jax.ShapeDtypeStruct(s, F32) for s in acc_shapes],
        scratch_shapes=[pltpu.VMEM((2, tr, RW_PAD), dproj.dtype), pltpu.SemaphoreType.DMA((2,)), pltpu.VMEM((1, RW_PAD), F32)],
        input_output_aliases={24 + n_g: 0},
        compiler_params=_params(("arbitrary",)),
    )(*[proj] * 12, *[u] * 4, *grads, mu, *small, dproj)
    return res


def _local_step(x, p, tgt, w, early_shards, late_shards):
    row = lambda v: v.reshape(1, -1)
    w = dict(w)

    xn1, *gathered = _rowwise("rms_mix", lambda h, g: (_rms(h, g),), [x], [w["norm_mix_g"]], [(D_MODEL, BF16)],
                              gather=early_shards)
    w.update({n: _unshard(n, g_) for n, g_ in zip(_EARLY[1:], gathered[1:])})
    w["w_in"] = _assemble_w_in(gathered[0])
    w["w_lora_up"] = _pad_rows(w["w_lora_up"], LANE)
    w["a_lora_up"] = _pad_rows(w["a_lora_up"], LANE)
    w["g_lora_up"] = _pad_rows(w["g_lora_up"], 2 * LANE)
    proj = _matmul("in_proj", xn1, w["w_in"], "nn", [F32], tm=2048, tn=512, tk=D_MODEL)
    n_cb = CONV_DIM // LANE

    def conv_fwd(blk, cw):
        gb, gc, hx = blk[:, :LANE], blk[:, LANE:2 * LANE], blk[:, 2 * LANE:]
        uu = gc * hx
        return (gb * (uu * cw[2:3] + _shift_down(uu, 1) * cw[1:2] + _shift_down(uu, 2) * cw[0:1]),)

    (y_conv,) = _colwise("conv_fwd", conv_fwd, n_cb, [(proj, 3 * LANE)], [w["conv_w"]], [(CONV_DIM, BF16, LANE)])

    small = [w["w0"], w["a0"], w["k_k"], w["k_a"], w["w_lora_up"], w["a_lora_up"], w["g_lora_up"]]
    def pre_fwd(*xs):
        cur, prev_rows, mu, prm = xs[:6], xs[6:12], xs[12], xs[13:]
        segs = []
        for c_, p_, off, wd in zip(cur, prev_rows, SEG_OFF, SEG_W):
            rows = lax.broadcasted_iota(jnp.int32, c_.shape, 0)
            prev = jnp.where(rows == 0, p_, pltpu.roll(c_, 1, 0))
            segs.append(c_ + mu[:, off:off + wd] * (prev - c_))
        return (jnp.concatenate(segs, axis=1),) + tuple(_rwkv_pre(segs[1], segs[3], segs[4], segs[5], *prm))

    proj_segs = [(proj, wd, (CONV_COLS + off) // wd) for off, wd in zip(SEG_OFF, SEG_W)]
    u, lw, k_h, ra, rb, g = _rowwise("rwkv_pre", pre_fwd, proj_segs, [w["shift_mu"]] + small,
                                     [(RW_PAD, F32)] + [(RWKV_DIM, F32)] * 5, halo=True)
    u_k, u_xw, u_xa, u_xg = (u, 512, 1), (u, LANE, XW_OFF // LANE), (u, LANE, XA_OFF // LANE), (u, 2 * LANE, XG_OFF // (2 * LANE))
    y_rec, zs, late = _rec_fwd(u, lw, k_h, ra, rb, late_shards)
    for n, gathered in zip(_LATE, late):
        w[n] = _unshard(n, gathered)
    post_c = [w["ln_x_g"], w["ln_x_b"], w["r_k"]]
    u_r, u_v = (u, 512, 0), (u, 512, 2)
    (y_rwkv,) = _rowwise("rwkv_post", lambda *xs: (_rwkv_post(*xs),), [y_rec, u_r, k_h, u_v, g], post_c, [(RWKV_DIM, BF16)])
    ycat = jnp.concatenate([y_conv, y_rwkv], axis=1)
    def res_norm(acc, r_, g_):
        h = acc + r_
        return h, _rms(h, g_)

    h1, xn2 = _matmul("out_proj", ycat, w["w_out"], "nn", [F32, BF16], tm=1024, tn=D_MODEL, tk=D_MODEL, extras=[x],
                      consts=[w["norm_mlp_g"]], epilogue=res_norm)

    square = lambda h: h.astype(F32) * h.astype(F32)
    hid = _matmul("mlp_up", xn2, w["w_up"], "nn", [BF16], tm=2048, tn=1024, tk=D_MODEL,
                  epilogue=lambda acc: (jnp.maximum(acc, 0.0),))
    h2, xn3 = _matmul("mlp_down", hid, w["w_down"], "nn", [F32, BF16], tm=512, tn=D_MODEL, tk=D_FF, extras=[h1],
                      consts=[w["norm_ple_g"]], epilogue=res_norm, a_map=square)
    zg = _matmul("ple_gate", xn3, w["w_ple_gate"], "nn", [F32], tm=1024, tn=1024, tk=D_MODEL)
    pp = _matmul("ple_proj", p, w["w_ple_proj"], "nn", [F32], tm=1024, tn=1024, tk=PLE_DIM)

    def head(h2_, zg_, pp_, tg, gf):
        gate = _sigmoid(zg_)
        h3 = h2_ + gate * pp_
        out = _rms(h3, gf)
        err = out - tg
        dh3, dgf = _rms_bwd(h3, gf, err * (1.0 / D_MODEL))
        loss = jnp.sum(jnp.sum(err * err, axis=1, keepdims=True), axis=0, keepdims=True) * (0.5 / D_MODEL)
        return dh3, dh3 * pp_ * gate * (1.0 - gate), dh3 * gate, dgf, loss

    dh3, dzg, dpp, d_norm_final, loss = _rowwise(
        "head", head, [h2, zg, pp, tgt], [row(w["norm_final_g"])], [(D_MODEL, F32), (D_MODEL, BF16), (D_MODEL, BF16)],
        [(1, D_MODEL), (1, 1)])

    d_w_ple_proj = _matmul("d_ple_proj", p, dpp, "tn", [BF16], tm=PLE_DIM, tn=D_MODEL // N_DEV, tk=4096, col_blocks_out=True)
    d_w_ple_gate = _matmul("d_ple_gate", xn3, dzg, "tn", [BF16], tm=512, tn=1024, tk=4096)

    def norm_bwd(dxn, h, dres, g_):
        dh, dg = _rms_bwd(h, g_, dxn)
        dh = dh + dres
        return dh, dh, dg

    nb = dict(tm=512, tn=D_MODEL, epilogue=norm_bwd, sums=[(1, D_MODEL)])
    dh2, dh2_b, d_norm_ple = _matmul("dx_ple_gate", dzg, w["w_ple_gate"], "nt", [F32, BF16], tk=D_MODEL,
                                     extras=[h2, dh3], consts=[w["norm_ple_g"]], **nb)
    d_w_down = _matmul("d_mlp_down", hid, dh2_b, "tn", [BF16], tm=512, tn=1024, tk=4096, a_map=square)
    dpre = _matmul("dx_mlp_down", dh2_b, w["w_down"], "nt", [BF16], tm=2048, tn=1024, tk=D_MODEL, extras=[hid],
                   epilogue=lambda acc, hid_: (acc * (2.0 * hid_.astype(F32)),))
    d_w_up = _matmul("d_mlp_up", xn2, dpre, "tn", [BF16], tm=1024, tn=D_FF // N_DEV, tk=4096, col_blocks_out=True)
    dh1, dh1_b, d_norm_mlp = _matmul("dx_mlp_up", dpre, w["w_up"], "nt", [F32, BF16], tk=D_FF,
                                     extras=[h1, dh2], consts=[w["norm_mlp_g"]], **nb)
    d_w_out = _matmul("d_out_proj", ycat, dh1_b, "tn", [BF16], tm=512, tn=1024, tk=4096)
    dycat = _matmul("dx_out_proj", dh1_b, w["w_out"], "nt", [F32], tm=1024, tn=1024, tk=D_MODEL)
    late_grads = dict(w_out=d_w_out, w_up=d_w_up, w_down=d_w_down, w_ple_gate=d_w_ple_gate, w_ple_proj=d_w_ple_proj)

    def conv_bwd(dy, blk, cw):
        gb, gc, hx = blk[:, :LANE], blk[:, LANE:2 * LANE], blk[:, 2 * LANE:]
        uu = gc * hx
        u1, u2 = _shift_down(uu, 1), _shift_down(uu, 2)
        dconv = dy * gb
        du = dconv * cw[2:3] + _shift_up(dconv, 1) * cw[1:2] + _shift_up(dconv, 2) * cw[0:1]
        s = lambda z: jnp.sum(z, axis=0, keepdims=True)
        d_blk = jnp.concatenate([dy * (uu * cw[2:3] + u1 * cw[1:2] + u2 * cw[0:1]), du * hx, du * gc], axis=1)
        return d_blk, s(dconv * u2), s(dconv * u1), s(dconv * uu)

    dproj, dcw0, dcw1, dcw2 = _colwise(
        "conv_bwd", conv_bwd, n_cb, [(dycat, LANE), (proj, 3 * LANE)], [w["conv_w"]],
        [(IN_PAD, BF16, 3 * LANE)], [(1, CONV_DIM)] * 3)

    def post_bwd(dy, y, r, k_h_, v, g_, ln_g, ln_b, r_k):
        _, vjp = jax.vjp(_rwkv_post, y, r, k_h_, v, g_, ln_g, ln_b, r_k)
        return vjp(dy)

    dy_rec, dr_p, dk_p, dv_p, dg, d_ln_g, d_ln_b, d_r_k = _rowwise(
        "rwkv_post_bwd", post_bwd, [(dycat, 512, 1), y_rec, u_r, k_h, u_v, g], post_c,
        [(RWKV_DIM, F32)] * 5, [(1, RWKV_DIM)] * 3)
    (dr_r, dv_r, dlw, dk_r, da, db), late_parts = _rec_bwd(
        u, lw, k_h, ra, rb, zs, dy_rec,
        [late_grads[n] if n in ("w_up", "w_ple_proj") else _reshard(n, late_grads[n]) for n in _LATE], [True] * len(_LATE))

    dproj, d_mu, d_w0, d_a0, d_k_k, d_k_a, d_wl, d_al, d_gl = _rwkv_pre_bwd(
        proj, u, [dr_p, dr_r, dv_p, dv_r, dlw, dk_p, dk_r, da, db, dg], w["shift_mu"], small, dproj)
    d_w_in = _matmul("d_in_proj", xn1, dproj, "tn", [BF16], tm=1024, tn=896, tk=4096)
    early_grads = dict(conv_w=jnp.concatenate([dcw0, dcw1, dcw2], axis=0),
                       w_lora_up=d_wl[:64], a_lora_up=d_al[:64], g_lora_up=d_gl[:160])
    early_send = [_split_w_in_grad(d_w_in)] + [_reshard(n, early_grads[n]) for n in _EARLY[1:]]
    dx, d_norm_mix, *early_parts = _matmul(
        "dx_in_proj", dproj, w["w_in"], "nt", [F32], tk=IN_PAD, extras=[x, dh1], consts=[w["norm_mix_g"]],
        xch=early_send, xch_scatter=[True] * len(_EARLY), **dict(nb, epilogue=lambda *a: norm_bwd(*a)[1:]))

    grads = dict(
        norm_mix_g=d_norm_mix, shift_mu=d_mu, w0=d_w0, a0=d_a0, k_k=d_k_k, k_a=d_k_a, r_k=d_r_k,
        ln_x_g=d_ln_g, ln_x_b=d_ln_b, norm_mlp_g=d_norm_mlp, norm_ple_g=d_norm_ple, norm_final_g=d_norm_final)
    parts = dict(zip(_EARLY, early_parts))
    parts.update(zip(_LATE, late_parts))
    return loss, dx, grads, parts


def _adam_update(partials, w_ref, m_ref, v_ref, g_ref, d_ref, nm_ref, nv_ref):
    g = partials[0].astype(F32)
    for part in partials[1:]:
        g = g + part.astype(F32)
    nm =ADAM_B1 * m_ref[...] + (1.0 - ADAM_B1) * g
    nv = ADAM_B2 * v_ref[...] + (1.0 - ADAM_B2) * (g * g)
    m_hat = nm / (1.0 - ADAM_B1 ** ADAM_STEP)
    v_hat = nv / (1.0 - ADAM_B2 ** ADAM_STEP)
    g_ref[...] = g
    d_ref[...] = -ADAM_LR * (m_hat / (jnp.sqrt(v_hat) + ADAM_EPS) + ADAM_WD * w_ref[...])
    nm_ref[...] = nm
    nv_ref[...] = nv


def _small_layout(widths):
    offs, off = [], 0
    for wd in list(widths) + [1]:
        offs.append(off)
        off += -(-wd // LANE) * LANE
    return offs, off


def _pack_small(vecs, loss):
    offs, total = _small_layout([v_.shape[1] for v_ in vecs])
    n = len(vecs)

    def body(*refs):
        pieces = []
        for ref in refs[:n + 1]:
            val = ref[...]
            pad = -val.shape[1] % LANE
            pieces += [val] + ([jnp.zeros((1, pad), F32)] if pad else [])
        row = jnp.concatenate(pieces, axis=1)
        rows = lax.broadcasted_iota(jnp.int32, (8, total), 0)
        refs[n + 1][...] = jnp.where(rows == 0, jnp.broadcast_to(row, (8, total)), 0.0)

    return pl.pallas_call(body, name="pack_small", out_shape=jax.ShapeDtypeStruct((8, total), F32))(*vecs, loss)


def _adamw_small(packed, ws, ms, vs):
    n = len(ws)
    offs, _ = _small_layout([w_.shape[1] for w_ in ws])

    def body(p_ref, *refs):
        w_refs, m_refs, v_refs, outs = refs[:n], refs[n:2 * n], refs[2 * n:3 * n], refs[3 * n:]
        for j in range(n):
            cols = pl.ds(offs[j], ws[j].shape[1])
            _adam_update([p_ref[s, 0:1, cols] for s in range(N_DEV)], w_refs[j], m_refs[j], v_refs[j], *outs[4 * j:4 * j + 4])
        total = p_ref[0, 0:1, offs[n]:offs[n] + 1]
        for s in range(1, N_DEV):
            total = total + p_ref[s, 0:1, offs[n]:offs[n] + 1]
        outs[4 * n][...] = total

    res = pl.pallas_call(
        body, name="adamw_small",
        out_shape=[jax.ShapeDtypeStruct(w_.shape, F32) for w_ in ws for _ in range(4)] + [jax.ShapeDtypeStruct((1, 1), F32)],
    )(packed, *ws, *ms, *vs)
    return [res[4 * j:4 * j + 4] for j in range(n)], res[4 * n]


def _adamw(name, parts, w, m, v):
    rows, cols = w.shape[-2:]
    lead = w.ndim - 2
    tr = rows if rows * cols * 4 * 8 <= (4 << 20) else max(8, (4 << 20) // (cols * 4 * 8) // 8 * 8)
    while rows % tr:
        tr -= 8

    def body(p_ref, *refs):
        _adam_update([p_ref[s] for s in range(N_DEV)], *refs)

    blk = pl.BlockSpec((None,) * lead + (tr, cols), lambda i: (0,) * lead + (i, 0))
    return pl.pallas_call(
        body, name=name, grid=(rows // tr,),
        in_specs=[pl.BlockSpec((N_DEV, tr, cols), lambda i: (0, i, 0)), blk, blk, blk], out_specs=[blk] * 4,
        out_shape=[jax.ShapeDtypeStruct(w.shape, F32)] * 4,
        compiler_params=_params(("arbitrary",)),
    )(parts, w, m, v)


def kernel(x, p, norm_mix_g, w_in, conv_w, shift_mu, w_lora_up, w0, a_lora_up, a0, g_lora_up, k_k, k_a, r_k, ln_x_g, ln_x_b, w_out, norm_mlp_g, w_up, w_down, norm_ple_g, w_ple_gate, w_ple_proj, norm_final_g, loss_target, m_norm_mix_g, m_w_in, m_conv_w, m_shift_mu, m_w_lora_up, m_w0, m_a_lora_up, m_a0, m_g_lora_up, m_k_k, m_k_a, m_r_k, m_ln_x_g, m_ln_x_b, m_w_out, m_norm_mlp_g, m_w_up, m_w_down, m_norm_ple_g, m_w_ple_gate, m_w_ple_proj, m_norm_final_g, v_norm_mix_g, v_w_in, v_conv_w, v_shift_mu, v_w_lora_up, v_w0, v_a_lora_up, v_a0, v_g_lora_up, v_k_k, v_k_a, v_r_k, v_ln_x_g, v_ln_x_b, v_w_out, v_norm_mlp_g, v_w_up, v_w_down, v_norm_ple_g, v_w_ple_gate, v_w_ple_proj, v_norm_final_g):
    args = dict(locals())
    wts = {n: args[n] for n in _WEIGHTS}
    mom = {n: args["m_" + n] for n in _WEIGHTS}
    var = {n: args["v_" + n] for n in _WEIGHTS}
    shard2d = lambda a: a.reshape(a.shape[-2:])
    pad_mu = lambda a: _pad_in_cols(jnp.concatenate([jnp.zeros((1, CONV_COLS), F32), a], axis=1))[:, CONV_COLS:]
    unpad_mu = lambda a: _unpad_in_cols(jnp.concatenate([jnp.zeros((1, CONV_COLS), F32), a], axis=1))[:, CONV_COLS:]

    shards = {n: shard2d(wts[n]).astype(BF16 if n in _BF16_GATHER else F32) for n in _SHARDED}
    w = {n: wts[n].reshape(1, -1) for n in _REPLICATED}
    w["shift_mu"] = pad_mu(wts["shift_mu"])

    loss, dx, grads, parts = _local_step(x[0], p[0, 0], loss_target[0], w, [shards[n] for n in _EARLY],
                                         [shards[n] for n in _LATE])

    grads["shift_mu"] = unpad_mu(grads["shift_mu"])
    flat = lambda a: a.reshape(1, -1)
    (small_parts,) = _exchange("gather_small", [_pack_small([flat(grads[n]) for n in _REPLICATED], loss)], [False])

    out = {}
    for n in _SHARDED:
        out[n] = _adamw("adamw_" + n, parts[n], wts[n], mom[n], var[n])
    small, loss_total = _adamw_small(small_parts, *[[flat(d[n]) for n in _REPLICATED] for d in (wts, mom, var)])
    for n, res in zip(_REPLICATED, small):
        out[n] = [r.reshape(wts[n].shape) for r in res]
    return (loss_total[0, 0], dx[None], *[out[n][0] for n in _WEIGHTS], *[out[n][1] for n in _WEIGHTS],
            *[out[n][2] for n in _WEIGHTS], *[out[n][3] for n in _WEIGHTS])
```

```python
import functools

import jax
import jax.numpy as jnp
from jax import lax
from jax.experimental import pallas as pl
from jax.experimental.pallas import tpu as pltpu

F32 = jnp.float32
BF16 = jnp.bfloat16

N_DEV = 8
D_MODEL = 1024
CONV_DIM = 512
RWKV_DIM = 512
HEAD_DIM = 64
N_HEADS = 8
D_FF = 4096
PLE_DIM = 256
RMS_EPS = 1e-6
GN_EPS = 64e-5
L2_EPS = 1e-12
ADAM_LR, ADAM_B1, ADAM_B2, ADAM_EPS, ADAM_WD, ADAM_STEP = 0.001, 0.9, 0.999, 1e-08, 0.01, 10

CONV_COLS = 3 * CONV_DIM
RW_PAD = 2048
IN_PAD = CONV_COLS + RW_PAD
IN_COLS = 3360
XW_OFF, XA_OFF, XG_OFF = 1536, 1664, 1792
REC_CHUNK = 128
REC_PASSES = 1
ROW_BLOCK = 256
LANE = 128
VMEM_LIMIT = 56 * 1024 * 1024


def _dims(dn, ndim):
    if ndim == 3:
        return {"nn": (((2,), (1,)), ((0,), (0,))), "nt": (((2,), (2,)), ((0,), (0,))),
                "tn": (((1,), (1,)), ((0,), (0,)))}[dn]
    return {"nn": (((1,), (0,)), ((), ())), "nt": (((1,), (1,)), ((), ())), "tn": (((0,), (0,)), ((), ()))}[dn]


def _split2(x):
    hi = x.astype(BF16)
    return hi, (x - hi.astype(F32)).astype(BF16)


def _mm_raw(x, y, dn, passes):
    f = lambda p, q: lax.dot_general(p, q, _dims(dn, x.ndim), preferred_element_type=F32)
    if passes == 1:
        return f(x.astype(BF16), y.astype(BF16))
    xh, xl = _split2(x)
    yh, yl = _split2(y)
    if passes == 2:
        return f(xh, yh) + f(xh, yl)
    return f(xh, yh) + f(xh, yl) + f(xl, yh)


@functools.partial(jax.custom_vjp, nondiff_argnums=(2, 3))
def _mm(x, y, dn, passes):
    return _mm_raw(x, y, dn, passes)


def _mm_fwd(x, y, dn, passes):
    return _mm_raw(x, y, dn, passes), (x, y)


def _mm_bwd(dn, passes, res, d):
    x, y = res
    if dn == "nn":
        return _mm(d, y, "nt", passes), _mm(x, d, "tn", passes)
    if dn == "nt":
        return _mm(d, y, "nn", passes), _mm(d, x, "tn", passes)
    return _mm(y, d, "nt", passes), _mm(x, d, "nn", passes)


_mm.defvjp(_mm_fwd, _mm_bwd)


def _head_ones():
    i = lax.broadcasted_iota(jnp.int32, (RWKV_DIM, RWKV_DIM), 0) // HEAD_DIM
    j = lax.broadcasted_iota(jnp.int32, (RWKV_DIM, RWKV_DIM), 1) // HEAD_DIM
    return (i == j).astype(BF16)


def _hsum_raw(x):
    ones = _head_ones()
    f = lambda p: lax.dot_general(p, ones, _dims("nn", 2), preferred_element_type=F32)
    x1, x2 = _split2(x)
    return f(x1) + f(x2)


@jax.custom_vjp
def _hsum(x):
    return _hsum_raw(x)


_hsum.defvjp(lambda x: (_hsum_raw(x), None), lambda _, d: (_hsum(d),))


def _sigmoid(x):
    return 1.0 / (1.0 + jnp.exp(-x))


def _softplus(x):
    return jnp.maximum(x, 0.0) + jnp.log(1.0 + jnp.exp(-jnp.abs(x)))


def _params(sem):
    return pltpu.CompilerParams(dimension_semantics=sem, vmem_limit_bytes=VMEM_LIMIT)


def _rowwise(name, fn, rows, consts, row_outs, acc_outs=(), tr=ROW_BLOCK, halo=False, gather=()):
    rows = [r if isinstance(r, tuple) else (r, r.shape[1], 0) for r in rows]
    t_len = rows[0][0].shape[0]
    tr = min(tr, t_len)
    n_r, n_c, n_o, n_a, n_x = len(rows), len(consts), len(row_outs), len(acc_outs), len(gather)
    n_h = n_r if halo else 0
    sub = 8
    x_specs, x_shapes, x_sems = _exchange_io(gather, [False] * n_x) if n_x else ([], [], [])
    nb = t_len // tr

    def body(*refs):
        if n_x:
            n_in = n_r + n_h + n_c
            start, forward, wait = _gather_plan(refs[n_in:n_in + n_x], refs[len(refs) - 3 - n_x:len(refs) - 3], *refs[len(refs) - 3:])
            pl.when(pl.program_id(0) == 0)(start)
            refs = refs[:n_in] + refs[n_in + n_x:len(refs) - 3 - n_x]
        ins = [r[...] for r in refs[:n_r]]
        ins += [jnp.where(pl.program_id(0) == 0, 0.0, r[sub - 1:sub, :]) for r in refs[n_r:n_r + n_h]]
        ins += [r[...] for r in refs[n_r + n_h:n_r + n_h + n_c]]
        refs = refs[:n_r] + refs[n_r + n_h:]
        outs = fn(*ins)
        o_refs = refs[n_r + n_c:n_r + n_c + n_o]
        a_refs = refs[n_r + n_c + n_o:]
        for o_ref, val in zip(o_refs, outs[:n_o]):
            o_ref[...] = val.astype(o_ref.dtype)
        if n_a:
            first = pl.program_id(0) == 0

            @pl.when(first)
            def _():
                for a_ref, val in zip(a_refs, outs[n_o:]):
                    a_ref[...] = val

            @pl.when(jnp.logical_not(first))
            def _():
                for a_ref, val in zip(a_refs, outs[n_o:]):
                    a_ref[...] += val

        if n_x:
            @pl.when(pl.program_id(0) == nb - 1)
            def _():
                for j in range(n_x):
                    forward(j)
                wait()

    in_specs = [pl.BlockSpec((tr, w), functools.partial(lambda i, c: (i, c), c=cb)) for _, w, cb in rows]
    if halo:
        in_specs += [pl.BlockSpec((sub, w), functools.partial(lambda i, c: (jnp.maximum(i * (tr // sub) - 1, 0), c), c=cb))
                     for _, w, cb in rows]
    in_specs += [pl.BlockSpec(c.shape, functools.partial(lambda i, n: (0,) * n, n=c.ndim)) for c in consts]
    out_specs = [pl.BlockSpec((tr, w), lambda i: (i, 0)) for w, _ in row_outs]
    out_specs += [pl.BlockSpec(s, functools.partial(lambda i, n: (0,) * n, n=len(s))) for s in acc_outs]
    out_shape = [jax.ShapeDtypeStruct((t_len, w), dt) for w, dt in row_outs]
    out_shape += [jax.ShapeDtypeStruct(s, F32) for s in acc_outs]
    return pl.pallas_call(
        body, name=name, grid=(nb,), in_specs=in_specs + x_specs, out_specs=out_specs + x_specs,
        out_shape=out_shape + x_shapes, scratch_shapes=x_sems,
        compiler_params=pltpu.CompilerParams(dimension_semantics=("arbitrary",), vmem_limit_bytes=VMEM_LIMIT,
                                             has_side_effects=bool(n_x)),
    )(*[r[0] for r in rows], *([r[0] for r in rows] if halo else []), *consts, *gather)


def _colwise(name, fn, n_blocks, cols, prms, col_outs, prm_outs=()):
    t_len = cols[0][0].shape[0]
    n_i = len(cols) + len(prms)

    def body(*refs):
        outs = fn(*[r[...] for r in refs[:n_i]])
        for o_ref, val in zip(refs[n_i:], outs):
            o_ref[...] = val.astype(o_ref.dtype)

    spec = lambda r, w: pl.BlockSpec((r, w), lambda j: (0, j))
    in_specs = [spec(t_len, w) for _, w in cols] + [spec(a.shape[0], LANE) for a in prms]
    out_specs = [spec(t_len, bw) for _, _, bw in col_outs] + [spec(r, LANE) for r, _ in prm_outs]
    out_shape = [jax.ShapeDtypeStruct((t_len, w), dt) for w, dt, _ in col_outs]
    out_shape += [jax.ShapeDtypeStruct((r, w), F32) for r, w in prm_outs]
    return pl.pallas_call(
        body, name=name, grid=(n_blocks,), in_specs=in_specs, out_specs=out_specs, out_shape=out_shape,
        compiler_params=_params(("arbitrary",)),
    )(*[c[0] for c in cols], *prms)


def _matmul(name, a, b, dn, outs, *, tm, tn, tk, extras=(), consts=(), epilogue=None, sums=(), xch=(), xch_scatter=(),
            a_map=None, col_blocks_out=False):
    if dn == "nn":
        (m, k), n = a.shape, b.shape[1]
    elif dn == "nt":
        (m, k), n = a.shape, b.shape[0]
    else:
        (k, m), n = a.shape, b.shape[1]
    tm, tn, tk = min(tm, m), min(tn, n), min(tk, k)
    nk = k // tk
    grid = (m // tm, n // tn, nk)
    assert nk == 1 and (not sums or grid[1] == 1)
    a_spec = pl.BlockSpec((tk, tm), lambda i, j, q: (q, i)) if dn == "tn" else pl.BlockSpec((tm, tk), lambda i, j, q: (i, q))
    b_spec = pl.BlockSpec((tn, tk), lambda i, j, q: (j, q)) if dn == "nt" else pl.BlockSpec((tk, tn), lambda i, j, q: (q, j))
    o_spec = pl.BlockSpec((tm, tn), lambda i, j, q: (i, j))
    c_spec = pl.BlockSpec((1, tn), lambda i, j, q: (0, j))
    n_e, n_c, n_o, n_s, n_x = len(extras), len(consts), len(outs), len(sums), len(xch)
    x_specs, x_shapes, x_sems = _exchange_io(xch, xch_scatter) if n_x else ([], [], [])

    def body(*refs):
        a_ref, b_ref = refs[:2]
        e_refs = refs[2:2 + n_e + n_c]
        x_in = refs[2 + n_e + n_c:2 + n_e + n_c + n_x]
        rest = refs[2 + n_e + n_c + n_x:]
        o_refs, s_refs, x_out, scratch = rest[:n_o], rest[n_o:n_o + n_s], rest[n_o + n_s:n_o + n_s + n_x], rest[n_o + n_s + n_x:]
        step = (pl.program_id(0) * grid[1] + pl.program_id(1)) * nk + pl.program_id(2)
        if n_x:
            start, wait = _exchange_plan(x_in, x_out, xch_scatter, *scratch[len(scratch) - 3:])
            pl.when(step == 0)(start)
        a_blk = a_ref[...] if a_map is None else a_map(a_ref[...])
        acc = lax.dot_general(a_blk.astype(BF16), b_ref[...].astype(BF16), _dims(dn, 2), preferred_element_type=F32)
        vals = (acc,) if epilogue is None else epilogue(acc, *[e[...] for e in e_refs])
        for o_ref, val in zip(o_refs, vals[:n_o]):
            o_ref[...] = val.astype(o_ref.dtype)
        if n_s:
            @pl.when(step == 0)
            def _():
                for s_ref, val in zip(s_refs, vals[n_o:]):
                    s_ref[...] = val

            @pl.when(step > 0)
            def _():
                for s_ref, val in zip(s_refs, vals[n_o:]):
                    s_ref[...] += val

        if n_x:
            pl.when(step == grid[0] * grid[1] * nk - 1)(wait)

    plain = not (n_s or n_x)
    res = pl.pallas_call(
        body, name=name, grid=grid,
        in_specs=[a_spec, b_spec] + [o_spec] * n_e + [c_spec] * n_c + x_specs,
        out_specs=[pl.BlockSpec((None, tm, tn), lambda i, j, q: (j, i, 0)) if col_blocks_out else o_spec] * n_o
                  + [c_spec] * n_s + x_specs,
        out_shape=[jax.ShapeDtypeStruct((n // tn, m, tn) if col_blocks_out else (m, n), dt) for dt in outs] + [jax.ShapeDtypeStruct(s, F32) for s in sums] + x_shapes,
        scratch_shapes=x_sems,
        compiler_params=pltpu.CompilerParams(
            dimension_semantics=("parallel", "parallel", "arbitrary") if plain else ("arbitrary",) * 3,
            vmem_limit_bytes=VMEM_LIMIT, has_side_effects=bool(n_x)),
    )(a, b, *extras, *consts, *xch)
    return res[0] if len(res) == 1 else res


def _rms(h, g):
    return h * lax.rsqrt(jnp.mean(h * h, axis=-1, keepdims=True) + RMS_EPS) * g


def _rms_bwd(h, g, dy):
    rs = lax.rsqrt(jnp.mean(h * h, axis=-1, keepdims=True) + RMS_EPS)
    n = h * rs
    dn = dy * g
    dh = rs * (dn - n * jnp.mean(dn * n, axis=-1, keepdims=True))
    return dh, jnp.sum(dy * n, axis=0, keepdims=True)


def _rwkv_pre(k, xw, xa, xg, w0, a0, k_k, k_a, wl, al, gl):
    zw = w0 + _mm(jnp.tanh(xw), wl, "nn", 1)
    lw = -jnp.exp(-_softplus(-zw) - 0.5)
    iclr = _sigmoid(a0 + _mm(xa, al, "nn", 1))
    g = _mm(_sigmoid(xg), gl, "nn", 1)
    kk0 = k * k_k
    kk = kk0 / jnp.maximum(jnp.sqrt(_hsum(kk0 * kk0)), L2_EPS)
    k_h = k * (1.0 + (iclr - 1.0) * k_a)
    return lw, k_h, -kk, kk * iclr, g


def _rwkv_post(y, r, k_h, v, g, ln_g, ln_b, r_k):
    mu = _hsum(y) * (1.0 / HEAD_DIM)
    yc = y - mu
    var = _hsum(yc * yc) * (1.0 / HEAD_DIM)
    yo = yc * lax.rsqrt(var + GN_EPS) * ln_g + ln_b
    bonus = _hsum(r * k_h * r_k) * v
    return (yo + bonus) * g


def _shift_down(x, n):
    rows = lax.broadcasted_iota(jnp.int32, x.shape, 0)
    return jnp.where(rows < n, 0.0, pltpu.roll(x, n, 0))


def _shift_up(x, n):
    t_len = x.shape[0]
    rows = lax.broadcasted_iota(jnp.int32, x.shape, 0)
    return jnp.where(rows >= t_len - n, 0.0, pltpu.roll(x, t_len - n, 0))


def _exchange_plan(ins, outs, scatter, send_sems, recv_sems, local_sems):
    x, y, c = lax.axis_index("x"), lax.axis_index("y"), lax.axis_index("c")
    me = 4 * x + 2 * y + c

    def local(i):
        return pltpu.make_async_copy(ins[i].at[me] if scatter[i] else ins[i], outs[i].at[me], local_sems.at[i])

    def send(i, rel):
        return pltpu.make_async_remote_copy(
            src_ref=ins[i].at[me ^ rel] if scatter[i] else ins[i], dst_ref=outs[i].at[me],
            send_sem=send_sems.at[i, rel - 1], recv_sem=recv_sems.at[i, rel - 1],
            device_id=(x ^ (rel >> 2), y ^ ((rel >> 1) & 1), c ^ (rel & 1)), device_id_type=pl.DeviceIdType.MESH)

    def landed(i, rel):
        slot = outs[i].at[me ^ rel]
        return pltpu.make_async_remote_copy(
            src_ref=slot, dst_ref=slot, send_sem=send_sems.at[i, rel - 1], recv_sem=recv_sems.at[i, rel - 1],
            device_id=(x, y, c), device_id_type=pl.DeviceIdType.MESH)

    def start():
        for i in range(len(ins)):
            local(i).start()
            for rel in range(1, N_DEV):
                send(i, rel).start()

    def wait():
        for i in range(len(ins)):
            local(i).wait()
            for rel in range(1, N_DEV):
                landed(i, rel).wait_recv()
            for rel in range(1, N_DEV):
                send(i, rel).wait_send()

    return start, wait


def _gather_plan(ins, outs, send_sems, recv_sems, local_sems):
    x, y, c = lax.axis_index("x"), lax.axis_index("y"), lax.axis_index("c")
    me = 4 * x + 2 * y + c
    direct, chips = (1, 2, 4, 6), (2, 4, 6)

    def local(i):
        return pltpu.make_async_copy(ins[i], outs[i].at[me], local_sems.at[i])

    def send(i, rel):
        return pltpu.make_async_remote_copy(
            src_ref=ins[i], dst_ref=outs[i].at[me], send_sem=send_sems.at[i, rel - 1], recv_sem=recv_sems.at[i, rel - 1],
            device_id=(x ^ (rel >> 2), y ^ ((rel >> 1) & 1), c ^ (rel & 1)), device_id_type=pl.DeviceIdType.MESH)

    def passed(i, rel):
        slot = outs[i].at[me ^ rel]
        return pltpu.make_async_remote_copy(
            src_ref=slot, dst_ref=slot, send_sem=send_sems.at[i, rel], recv_sem=recv_sems.at[i, rel],
            device_id=(x, y, 1 - c), device_id_type=pl.DeviceIdType.MESH)

    def landed(i, rel):
        slot = outs[i].at[me ^ rel]
        return pltpu.make_async_remote_copy(
            src_ref=slot, dst_ref=slot, send_sem=send_sems.at[i, rel - 1], recv_sem=recv_sems.at[i, rel - 1],
            device_id=(x, y, c), device_id_type=pl.DeviceIdType.MESH)

    def start():
        for i in range(len(ins)):
            local(i).start()
            for rel in direct:
                send(i, rel).start()

    def forward(i):
        for rel in chips:
            landed(i, rel).wait_recv()
            passed(i, rel).start()

    def wait():
        for i in range(len(ins)):
            local(i).wait()
            for rel in (1, 3, 5, 7):
                landed(i, rel).wait_recv()
            for rel in direct:
                send(i, rel).wait_send()
            for rel in chips:
                passed(i, rel).wait_send()

    return start, forward, wait


def _exchange_io(arrays, scatter):
    n = len(arrays)
    any_spec = pl.BlockSpec(memory_space=pl.ANY)
    out_shape = [jax.ShapeDtypeStruct(a.shape if sc else (N_DEV,) + a.shape, a.dtype) for a, sc in zip(arrays, scatter)]
    sems = [pltpu.SemaphoreType.DMA((n, N_DEV - 1)), pltpu.SemaphoreType.DMA((n, N_DEV - 1)), pltpu.SemaphoreType.DMA((n,))]
    return [any_spec] * n, out_shape, sems


def _exchange(name, arrays, scatter):
    n = len(arrays)
    specs, out_shape, sems = _exchange_io(arrays, scatter)

    def body(*refs):
        if any(scatter):
            start, wait = _exchange_plan(refs[:n], refs[n:2 * n], scatter, *refs[2 * n:])
            start()
        else:
            start, forward, wait = _gather_plan(refs[:n], refs[n:2 * n], *refs[2 * n:])
            start()
            for i in range(n):
                forward(i)
        wait()

    return pl.pallas_call(
        body, name=name, in_specs=specs, out_specs=specs, out_shape=out_shape, scratch_shapes=sems,
        compiler_params=pltpu.CompilerParams(has_side_effects=True),
    )(*arrays)


def _own_slots(name, arrays):
    n = len(arrays)

    def body(*refs):
        me = 4 * lax.axis_index("x") + 2 * lax.axis_index("y") + lax.axis_index("c")
        cps = [pltpu.make_async_copy(refs[i].at[me], refs[n + i].at[me], refs[2 * n].at[i]) for i in range(n)]
        for cp in cps:
            cp.start()
        for cp in cps:
            cp.wait()

    any_spec = pl.BlockSpec(memory_space=pl.ANY)
    return pl.pallas_call(
        body, name=name, in_specs=[any_spec] * n, out_specs=[any_spec] * n,
        out_shape=[jax.ShapeDtypeStruct(a.shape, a.dtype) for a in arrays],
        scratch_shapes=[pltpu.SemaphoreType.DMA((n,))], compiler_params=pltpu.CompilerParams(has_side_effects=True),
    )(*arrays)


def _scatter_start(name, arrays, lands):
    n = len(arrays)
    hbm = pl.BlockSpec(memory_space=pltpu.HBM)

    def body(*refs):
        ins, land, send_sems, recv_sems = refs[:n], refs[n:2 * n], refs[2 * n], refs[2 * n + 1]
        token = refs[4 * n + 2]
        x, y, c = lax.axis_index("x"), lax.axis_index("y"), lax.axis_index("c")
        me = 4 * x + 2 * y + c
        for i in range(n):
            for rel in range(1, N_DEV):
                k = i * (N_DEV - 1) + rel - 1
                pltpu.make_async_remote_copy(
                    src_ref=ins[i].at[me ^ rel], dst_ref=land[i].at[me], send_sem=send_sems.at[k],
                    recv_sem=recv_sems.at[k], device_id=(x ^ (rel >> 2), y ^ ((rel >> 1) & 1), c ^ (rel & 1)),
                    device_id_type=pl.DeviceIdType.MESH).start()
        token[...] = jnp.zeros_like(token)

    sem = pltpu.SemaphoreType.DMA((n * (N_DEV - 1),))
    bufs = [pltpu.HBM(a.shape, a.dtype) for a in list(arrays) + list(lands)]
    res = pl.pallas_call(
        body, name=name, out_shape=(sem, sem, *bufs, jax.ShapeDtypeStruct((8, LANE), F32)),
        in_specs=[hbm] * (2 * n),
        out_specs=(pl.BlockSpec(memory_space=pltpu.SEMAPHORE),) * 2 + (hbm,) * (2 * n) + (pl.BlockSpec(memory_space=pltpu.VMEM),),
        input_output_aliases={i: 2 + i for i in range(2 * n)},
        compiler_params=pltpu.CompilerParams(has_side_effects=pltpu.SideEffectType.DATAFLOW_SIDE_EFFECTING),
    )(*[pltpu.with_memory_space_constraint(a, pltpu.HBM) for a in list(arrays) + list(lands)])
    return res[0], res[1], res[2:2 + n], res[2 + n:2 + 2 * n], res[2 + 2 * n]


def _scatter_wait(name, send_sems, recv_sems, arrays, lands, after):
    n, n_after = len(arrays), len(after)
    hbm = pl.BlockSpec(memory_space=pltpu.HBM)

    def body(*refs):
        ins, land, s_sems, r_sems = refs[:n], refs[n:2 * n], refs[2 * n], refs[2 * n + 1]
        x, y, c = lax.axis_index("x"), lax.axis_index("y"), lax.axis_index("c")
        me = 4 * x + 2 * y + c
        for i in range(n):
            for rel in range(1, N_DEV):
                k = i * (N_DEV - 1) + rel - 1
                cp = pltpu.make_async_remote_copy(
                    src_ref=ins[i].at[me ^ rel], dst_ref=land[i].at[me ^ rel], send_sem=s_sems.at[k],
                    recv_sem=r_sems.at[k], device_id=(x, y, c), device_id_type=pl.DeviceIdType.MESH)
                cp.wait_send()
                cp.wait_recv()

    res = pl.pallas_call(
        body, name=name, out_shape=[pltpu.HBM(a.shape, a.dtype) for a in list(arrays) + list(lands)],
        in_specs=[hbm] * (2 * n) + [pl.BlockSpec(memory_space=pltpu.SEMAPHORE)] * 2 + [pl.BlockSpec(memory_space=pl.ANY)] * n_after,
        out_specs=[hbm] * (2 * n), input_output_aliases={i: i for i in range(2 * n)},
        compiler_params=pltpu.CompilerParams(has_side_effects=pltpu.SideEffectType.DATAFLOW_SIDE_EFFECTING),
    )(*arrays, *lands, send_sems, recv_sems, *after)
    return res[n:]


def _tri_powers(low):
    powers, n = [low], 1
    while 2 * n < low.shape[-1]:
        powers.append(_mm(powers[-1], powers[-1], "nn", REC_PASSES))
        n *= 2
    return powers


@jax.custom_vjp
def _tri_solve(low, rhs):
    for p in _tri_powers(low):
        rhs = rhs + _mm(p, rhs, "nn", REC_PASSES)
    return rhs


def _tri_solve_fwd(low, rhs):
    powers = _tri_powers(low)
    for p in powers:
        rhs = rhs + _mm(p, rhs, "nn", REC_PASSES)
    return rhs, (powers, rhs)


def _tri_solve_bwd(res, d):
    powers, u = res
    for p in powers:
        d = d + _mm(p, d, "tn", REC_PASSES)
    return _mm(d, u, "nt", REC_PASSES), d


_tri_solve.defvjp(_tri_solve_fwd, _tri_solve_bwd)


def _chunk_fwd(z0, r, lw, k, v, a, b):
    n_h, c, n_k = r.shape
    mm = functools.partial(_mm, passes=REC_PASSES)
    gram = functools.partial(_mm, passes=2)
    ti = lax.broadcasted_iota(jnp.int32, (c, c), 0)
    si = lax.broadcasted_iota(jnp.int32, (c, c), 1)
    strict, incl = si < ti, si <= ti
    cum = _mm(jnp.broadcast_to(incl.astype(F32), (n_h, c, c)), lw, "nn", 3)
    cum_end = cum[:, c - 1:c, :]
    e_neg, e_end = jnp.exp(-cum), jnp.exp(cum_end - cum)
    x2 = jnp.concatenate([a * jnp.exp(cum - lw), r * jnp.exp(cum)], axis=1)
    y2 = jnp.concatenate([b * e_neg, k * e_neg], axis=1)
    mask = jnp.concatenate([jnp.concatenate([strict, strict], axis=1), jnp.concatenate([incl, incl], axis=1)], axis=0)
    g2 = jnp.where(mask, gram(x2, y2, "nt"), 0.0)
    t2 = mm(x2, z0, "nn") + mm(g2[:, :, c:], v, "nn")
    u = _tri_solve(g2[:, :c, :c], t2[:, :c])
    y = t2[:, c:] + mm(g2[:, c:, :c], u, "nn")
    ki = lax.broadcasted_iota(jnp.int32, (n_k, n_k), 0)
    kj = lax.broadcasted_iota(jnp.int32, (n_k, n_k), 1)
    dmat = jnp.where(ki == kj, jnp.broadcast_to(jnp.exp(cum_end), (n_h, n_k, n_k)), 0.0)
    z_end = mm(dmat, z0, "nn") + mm(jnp.concatenate([b * e_end, k * e_end], axis=1), jnp.concatenate([u, v], axis=1), "tn")
    return y, z_end


def _heads(x):
    return jnp.stack([x[:, h * HEAD_DIM:(h + 1) * HEAD_DIM] for h in range(N_HEADS)])


def _unheads(x):
    return jnp.concatenate([x[h] for h in range(N_HEADS)], axis=-1)


def _rec_params():
    return pltpu.CompilerParams(dimension_semantics=("arbitrary",), vmem_limit_bytes=VMEM_LIMIT, has_side_effects=True)


def _rec_fwd(u, lw, k, a, b, xch):
    t_len = lw.shape[0]
    c = min(REC_CHUNK, t_len)
    nc = t_len // c
    n_x = len(xch)
    x_specs, x_shapes, x_sems = _exchange_io(xch, [False] * n_x)
    sizes = [a_.size * a_.dtype.itemsize for a_ in xch]
    pass_step = [min(nc - 1, int(0.9 * nc * sum(sizes[:j + 1]) / sum(sizes)) + 1) for j in range(n_x)]

    def body(*refs):
        r_ref, v_ref, lw_ref, k_ref, a_ref, b_ref = refs[:6]
        x_in = refs[6:6 + n_x]
        y_ref, zs_ref = refs[6 + n_x:8 + n_x]
        x_out = refs[8 + n_x:8 + 2 * n_x]
        z_scr = refs[8 + 2 * n_x]
        start, forward, wait = _gather_plan(x_in, x_out, *refs[9 + 2 * n_x:])
        i = pl.program_id(0)

        @pl.when(i == 0)
        def _():
            start()
            z_scr[...] = jnp.zeros_like(z_scr)

        z0 = z_scr[...]
        zs_ref[0] = z0
        y, z_end = _chunk_fwd(z0, _heads(r_ref[...]), _heads(lw_ref[...]), _heads(k_ref[...]), _heads(v_ref[...]),
                              _heads(a_ref[...]), _heads(b_ref[...]))
        y_ref[...] = _unheads(y)
        z_scr[...] = z_end

        for j in range(n_x):
            pl.when(i == pass_step[j])(functools.partial(forward, j))

        @pl.when(i == nc - 1)
        def _():
            wait()

    blk = lambda cb: pl.BlockSpec((c, RWKV_DIM), functools.partial(lambda i, q: (i, q), q=cb))
    res = pl.pallas_call(
        body, name="rwkv_rec_fwd", grid=(nc,),
        in_specs=[blk(0), blk(2)] + [blk(0)] * 4 + x_specs,
        out_specs=[blk(0), pl.BlockSpec((1, N_HEADS, HEAD_DIM, HEAD_DIM), lambda i: (i, 0, 0, 0))] + x_specs,
        out_shape=[jax.ShapeDtypeStruct((t_len, RWKV_DIM), F32),
                   jax.ShapeDtypeStruct((nc, N_HEADS, HEAD_DIM, HEAD_DIM), F32)] + x_shapes,
        scratch_shapes=[pltpu.VMEM((N_HEADS, HEAD_DIM, HEAD_DIM), F32)] + x_sems,
        compiler_params=_rec_params(),
    )(u, u, lw, k, a, b, *xch)
    return res[0], res[1], res[2:]


def _rec_bwd(u, lw, k, a, b, zs, dy, xch, xch_scatter):
    t_len = lw.shape[0]
    c = min(REC_CHUNK, t_len)
    nc = t_len // c
    n_x = len(xch)
    x_specs, x_shapes, x_sems = _exchange_io(xch, xch_scatter)

    def body(*refs):
        r_ref, v_ref, lw_ref, k_ref, a_ref, b_ref, zs_ref, dy_ref = refs[:8]
        x_in = refs[8:8 + n_x]
        g_refs = refs[8 + n_x:14 + n_x]
        x_out = refs[14 + n_x:14 + 2 * n_x]
        dz_scr = refs[14 + 2 * n_x]
        start, wait = _exchange_plan(x_in, x_out, xch_scatter, *refs[15 + 2 * n_x:])
        i = pl.program_id(0)

        @pl.when(i == 0)
        def _():
            start()
            dz_scr[...] = jnp.zeros_like(dz_scr)

        _, vjp = jax.vjp(_chunk_fwd, zs_ref[0], _heads(r_ref[...]), _heads(lw_ref[...]), _heads(k_ref[...]),
                         _heads(v_ref[...]), _heads(a_ref[...]), _heads(b_ref[...]))
        dz0, dr, dlw, dk, dv, da, db = vjp((_heads(dy_ref[...]), dz_scr[...]))
        for ref, val in zip(g_refs, (dr, dv, dlw, dk, da, db)):
            ref[...] = _unheads(val)
        dz_scr[...] = dz0

        @pl.when(i == nc - 1)
        def _():
            wait()

    blk = lambda cb: pl.BlockSpec((c, RWKV_DIM), functools.partial(lambda i, q: (nc - 1 - i, q), q=cb))
    res = pl.pallas_call(
        body, name="rwkv_rec_bwd", grid=(nc,),
        in_specs=[blk(0), blk(2)] + [blk(0)] * 4
                 + [pl.BlockSpec((1, N_HEADS, HEAD_DIM, HEAD_DIM), lambda i: (nc - 1 - i, 0, 0, 0)), blk(0)] + x_specs,
        out_specs=[blk(0)] * 6 + x_specs,
        out_shape=[jax.ShapeDtypeStruct((t_len, RWKV_DIM), F32)] * 6 + x_shapes,
        scratch_shapes=[pltpu.VMEM((N_HEADS, HEAD_DIM, HEAD_DIM), F32)] + x_sems,
        compiler_params=_rec_params(),
    )(u, u, lw, k, a, b, zs, dy, *xch)
    return res[:6], res[6:]


_EARLY = ["w_in", "conv_w", "w_lora_up", "a_lora_up", "g_lora_up"]
_LATE = ["w_out", "w_up", "w_down", "w_ple_gate", "w_ple_proj"]
_SHARDED = _EARLY + _LATE
_COL_SHARDED = {"w_in", "conv_w", "w_lora_up", "a_lora_up", "g_lora_up", "w_up", "w_ple_proj"}
_BF16_GATHER = {"w_in", "w_out", "w_up", "w_down", "w_ple_gate", "w_ple_proj"}
_REPLICATED = ["norm_mix_g", "shift_mu", "w0", "a0", "k_k", "k_a", "r_k", "ln_x_g", "ln_x_b", "norm_mlp_g", "norm_ple_g",
               "norm_final_g"]
_WEIGHTS = ["norm_mix_g", "w_in", "conv_w", "shift_mu", "w_lora_up", "w0", "a_lora_up", "a0", "g_lora_up", "k_k", "k_a", "r_k",
            "ln_x_g", "ln_x_b", "w_out", "norm_mlp_g", "w_up", "w_down", "norm_ple_g", "w_ple_gate", "w_ple_proj", "norm_final_g"]


def _unshard(name, g):
    if name in _COL_SHARDED:
        return jnp.moveaxis(g, 0, 1).reshape(g.shape[1], N_DEV * g.shape[2])
    return g.reshape(N_DEV * g.shape[1], g.shape[2])


def _reshard(name, full):
    if name in _COL_SHARDED:
        return jnp.moveaxis(full.reshape(full.shape[0], N_DEV, full.shape[1] // N_DEV), 1, 0)
    return full.reshape(N_DEV, full.shape[0] // N_DEV, full.shape[1])


def _pad_in_cols(a):
    z = lambda n: jnp.zeros(a.shape[:-1] + (n,), a.dtype)
    conv = [a[..., part * CONV_DIM + j * LANE:part * CONV_DIM + (j + 1) * LANE] for j in range(CONV_DIM // LANE) for part in range(3)]
    return jnp.concatenate(conv + [a[..., CONV_COLS:3136], z(64), a[..., 3136:3200], z(64), a[..., 3200:3360], z(96)], axis=-1)


def _unpad_in_cols(a):
    conv = [a[..., (3 * j + part) * LANE:(3 * j + part + 1) * LANE] for part in range(3) for j in range(CONV_DIM // LANE)]
    return jnp.concatenate(conv + [a[..., CONV_COLS:3136], a[..., 3200:3264], a[..., 3328:3488]], axis=-1)


def _assemble_w_in(g):
    n_dev, rows, cols = g.shape

    def body(g_ref, o_ref):
        o_ref[...] = _pad_in_cols(jnp.concatenate([g_ref[d] for d in range(n_dev)], axis=1))

    return pl.pallas_call(
        body, name="w_in_assemble", grid=(rows // ROW_BLOCK,),
        in_specs=[pl.BlockSpec((n_dev, ROW_BLOCK, cols), lambda i: (0, i, 0))],
        out_specs=pl.BlockSpec((ROW_BLOCK, IN_PAD), lambda i: (i, 0)),
        out_shape=jax.ShapeDtypeStruct((rows, IN_PAD), g.dtype), compiler_params=_params(("arbitrary",)),
    )(g)


def _split_w_in_grad(dw):
    rows = dw.shape[0]
    cols = IN_COLS // N_DEV

    def body(d_ref, o_ref):
        full = _unpad_in_cols(d_ref[...])
        for d in range(N_DEV):
            o_ref[d] = full[:, cols * d:cols * (d + 1)]

    return pl.pallas_call(
        body, name="w_in_grad_split", grid=(rows // ROW_BLOCK,),
        in_specs=[pl.BlockSpec((ROW_BLOCK, IN_PAD), lambda i: (i, 0))],
        out_specs=pl.BlockSpec((N_DEV, ROW_BLOCK, cols), lambda i: (0, i, 0)),
        out_shape=jax.ShapeDtypeStruct((N_DEV, rows, cols), dw.dtype), compiler_params=_params(("arbitrary",)),
    )(dw)


def _pad_rows(a, rows):
    return jnp.concatenate([a, jnp.zeros((rows - a.shape[0],) + a.shape[1:], a.dtype)], axis=0)


SEG_W = [RWKV_DIM, RWKV_DIM, RWKV_DIM, LANE, LANE, 2 * LANE]
SEG_OFF = [0, 512, 1024, XW_OFF, XA_OFF, XG_OFF]


def _rwkv_pre_bwd(proj, u, grads, mu, small, dproj):
    t_len = u.shape[0]
    tr = min(ROW_BLOCK, t_len)
    nb = t_len // tr
    sub = 8
    n_g = len(grads)
    acc_shapes = [(1, RW_PAD)] + [(1, RWKV_DIM)] * 4 + [(LANE, RWKV_DIM), (LANE, RWKV_DIM), (2 * LANE, RWKV_DIM)]

    def body(*refs):
        seg_refs, halo_refs = refs[:6], refs[6:12]
        k_ref, xw_ref, xa_ref, xg_ref = refs[12:16]
        g_refs = refs[16:16 + n_g]
        mu_ref = refs[16 + n_g]
        prm_refs = refs[17 + n_g:24 + n_g]
        out_hbm = refs[25 + n_g]
        acc_refs = refs[26 + n_g:26 + n_g + len(acc_shapes)]
        vbuf, sems, carry = refs[26 + n_g + len(acc_shapes):]
        i = pl.program_id(0)
        blk = nb - 1 - i
        dr1, dr2, dv1, dv2, dlw, dk1, dk2, da, db, dg = [g[...] for g in g_refs]
        _, vjp = jax.vjp(_rwkv_pre, k_ref[...], xw_ref[...], xa_ref[...], xg_ref[...], *[p_[...] for p_ in prm_refs])
        dk, dxw, dxa, dxg, *dprm = vjp((dlw, dk1 + dk2, da, db, dg))
        du = jnp.concatenate([dr1 + dr2, dk, dv1 + dv2, dxw, dxa, dxg], axis=1)
        mu_v = mu_ref[...]

        @pl.when(i == 0)
        def _():
            carry[...] = jnp.zeros_like(carry)

        rows = lax.broadcasted_iota(jnp.int32, du.shape, 0)
        nxt = jnp.where(rows == tr - 1, carry[...], pltpu.roll(du, tr - 1, 0))
        d_rw = du - mu_v * du + mu_v * nxt
        d_mu = []
        for s_ref, h_ref, off, wd in zip(seg_refs, halo_refs, SEG_OFF, SEG_W):
            cur = s_ref[...]
            r0 = lax.broadcasted_iota(jnp.int32, cur.shape, 0)
            prev = jnp.where(r0 == 0, jnp.where(blk == 0, 0.0, h_ref[sub - 1:sub, :]), pltpu.roll(cur, 1, 0))
            d_mu.append(jnp.sum(du[:, off:off + wd] * (prev - cur), axis=0, keepdims=True))
        sums = [jnp.concatenate(d_mu, axis=1)] + list(dprm)

        @pl.when(i == 0)
        def _():
            for a_ref, val in zip(acc_refs, sums):
                a_ref[...] = val

        @pl.when(i > 0)
        def _():
            for a_ref, val in zip(acc_refs, sums):
                a_ref[...] += val

        carry[...] = du[0:1, :]
        slot = i % 2

        def writeback(s, b):
            return pltpu.make_async_copy(vbuf.at[s], out_hbm.at[pl.ds(b * tr, tr), pl.ds(CONV_COLS, RW_PAD)], sems.at[s])

        @pl.when(i >= 2)
        def _():
            writeback(slot, blk + 2).wait()

        vbuf[slot] = d_rw.astype(vbuf.dtype)
        writeback(slot, blk).start()

        @pl.when(i == nb - 1)
        def _():
            writeback(slot, blk).wait()
            if nb > 1:
                writeback(1 - slot, blk + 1).wait()

    rev = lambda w_, cb: pl.BlockSpec((tr, w_), functools.partial(lambda i, c: (nb - 1 - i, c), c=cb))
    halo = lambda w_, cb: pl.BlockSpec((sub, w_), functools.partial(
        lambda i, c: (jnp.maximum((nb - 1 - i) * (tr // sub) - 1, 0), c), c=cb))
    whole = lambda a: pl.BlockSpec(a.shape, functools.partial(lambda i, n: (0,) * n, n=a.ndim))
    segs = [(wd, (CONV_COLS + off) // wd) for off, wd in zip(SEG_OFF, SEG_W)]
    u_cols = [(512, 1), (LANE, XW_OFF // LANE), (LANE, XA_OFF // LANE), (2 * LANE, XG_OFF // (2 * LANE))]
    any_spec = pl.BlockSpec(memory_space=pl.ANY)
    res = pl.pallas_call(
        body, name="rwkv_pre_bwd", grid=(nb,),
        in_specs=[rev(*s) for s in segs] + [halo(*s) for s in segs] + [rev(*c) for c in u_cols]
                 + [rev(RWKV_DIM, 0)] * n_g + [whole(mu)] + [whole(p_) for p_ in small] + [any_spec],
        out_specs=[any_spec] + [pl.BlockSpec(s, functools.partial(lambda i, n: (0,) * n, n=len(s))) for s in acc_shapes],
        out_shape=[jax.ShapeDtypeStruct(dproj.shape, dproj.dtype)] + [jax.ShapeDtypeStruct(s, F32) for s in acc_shapes],
        scratch_shapes=[pltpu.VMEM((2, tr, RW_PAD), dproj.dtype), pltpu.SemaphoreType.DMA((2,)), pltpu.VMEM((1, RW_PAD), F32)],
        input_output_aliases={24 + n_g: 0},
        compiler_params=_params(("arbitrary",)),
    )(*[proj] * 12, *[u] * 4, *grads, mu, *small, dproj)
    return res


def _local_step(x, p, tgt, w, early_shards, late_shards):
    row = lambda v: v.reshape(1, -1)
    w = dict(w)

    xn1, *gathered = _rowwise("rms_mix", lambda h, g: (_rms(h, g),), [x], [w["norm_mix_g"]], [(D_MODEL, BF16)],
                              gather=early_shards)
    w.update({n: _unshard(n, g_) for n, g_ in zip(_EARLY[1:], gathered[1:])})
    w["w_in"] = _assemble_w_in(gathered[0])
    w["w_lora_up"] = _pad_rows(w["w_lora_up"], LANE)
    w["a_lora_up"] = _pad_rows(w["a_lora_up"], LANE)
    w["g_lora_up"] = _pad_rows(w["g_lora_up"], 2 * LANE)
    proj = _matmul("in_proj", xn1, w["w_in"], "nn", [F32], tm=2048, tn=512, tk=D_MODEL)
    n_cb = CONV_DIM // LANE

    def conv_fwd(blk, cw):
        gb, gc, hx = blk[:, :LANE], blk[:, LANE:2 * LANE], blk[:, 2 * LANE:]
        uu = gc * hx
        return (gb * (uu * cw[2:3] + _shift_down(uu, 1) * cw[1:2] + _shift_down(uu, 2) * cw[0:1]),)

    (y_conv,) = _colwise("conv_fwd", conv_fwd, n_cb, [(proj, 3 * LANE)], [w["conv_w"]], [(CONV_DIM, BF16, LANE)])

    small = [w["w0"], w["a0"], w["k_k"], w["k_a"], w["w_lora_up"], w["a_lora_up"], w["g_lora_up"]]
    def pre_fwd(*xs):
        cur, prev_rows, mu, prm = xs[:6], xs[6:12], xs[12], xs[13:]
        segs = []
        for c_, p_, off, wd in zip(cur, prev_rows, SEG_OFF, SEG_W):
            rows = lax.broadcasted_iota(jnp.int32, c_.shape, 0)
            prev = jnp.where(rows == 0, p_, pltpu.roll(c_, 1, 0))
            segs.append(c_ + mu[:, off:off + wd] * (prev - c_))
        return (jnp.concatenate(segs, axis=1),) + tuple(_rwkv_pre(segs[1], segs[3], segs[4], segs[5], *prm))

    proj_segs = [(proj, wd, (CONV_COLS + off) // wd) for off, wd in zip(SEG_OFF, SEG_W)]
    u, lw, k_h, ra, rb, g = _rowwise("rwkv_pre", pre_fwd, proj_segs, [w["shift_mu"]] + small,
                                     [(RW_PAD, F32)] + [(RWKV_DIM, F32)] * 5, halo=True)
    u_k, u_xw, u_xa, u_xg = (u, 512, 1), (u, LANE, XW_OFF // LANE), (u, LANE, XA_OFF // LANE), (u, 2 * LANE, XG_OFF // (2 * LANE))
    y_rec, zs, late = _rec_fwd(u, lw, k_h, ra, rb, late_shards)
    for n, gathered in zip(_LATE, late):
        w[n] = _unshard(n, gathered)
    post_c = [w["ln_x_g"], w["ln_x_b"], w["r_k"]]
    u_r, u_v = (u, 512, 0), (u, 512, 2)
    (y_rwkv,) = _rowwise("rwkv_post", lambda *xs: (_rwkv_post(*xs),), [y_rec, u_r, k_h, u_v, g], post_c, [(RWKV_DIM, BF16)])
    ycat = jnp.concatenate([y_conv, y_rwkv], axis=1)
    def res_norm(acc, r_, g_):
        h = acc + r_
        return h, _rms(h, g_)

    h1, xn2 = _matmul("out_proj", ycat, w["w_out"], "nn", [F32, BF16], tm=1024, tn=D_MODEL, tk=D_MODEL, extras=[x],
                      consts=[w["norm_mlp_g"]], epilogue=res_norm)

    square = lambda h: h.astype(F32) * h.astype(F32)
    hid = _matmul("mlp_up", xn2, w["w_up"], "nn", [BF16], tm=2048, tn=1024, tk=D_MODEL,
                  epilogue=lambda acc: (jnp.maximum(acc, 0.0),))
    h2, xn3 = _matmul("mlp_down", hid, w["w_down"], "nn", [F32, BF16], tm=512, tn=D_MODEL, tk=D_FF, extras=[h1],
                      consts=[w["norm_ple_g"]], epilogue=res_norm, a_map=square)
    zg = _matmul("ple_gate", xn3, w["w_ple_gate"], "nn", [F32], tm=1024, tn=1024, tk=D_MODEL)
    pp = _matmul("ple_proj", p, w["w_ple_proj"], "nn", [F32], tm=1024, tn=1024, tk=PLE_DIM)

    def head(h2_, zg_, pp_, tg, gf):
        gate = _sigmoid(zg_)
        h3 = h2_ + gate * pp_
        out = _rms(h3, gf)
        err = out - tg
        dh3, dgf = _rms_bwd(h3, gf, err * (1.0 / D_MODEL))
        loss = jnp.sum(jnp.sum(err * err, axis=1, keepdims=True), axis=0, keepdims=True) * (0.5 / D_MODEL)
        return dh3, dh3 * pp_ * gate * (1.0 - gate), dh3 * gate, dgf, loss

    dh3, dzg, dpp, d_norm_final, loss = _rowwise(
        "head", head, [h2, zg, pp, tgt], [row(w["norm_final_g"])], [(D_MODEL, F32), (D_MODEL, BF16), (D_MODEL, BF16)],
        [(1, D_MODEL), (1, 1)])

    d_w_ple_proj = _matmul("d_ple_proj", p, dpp, "tn", [BF16], tm=PLE_DIM, tn=D_MODEL // N_DEV, tk=4096, col_blocks_out=True)
    d_w_ple_gate = _matmul("d_ple_gate", xn3, dzg, "tn", [BF16], tm=512, tn=1024, tk=4096)

    def norm_bwd(dxn, h, dres, g_):
        dh, dg = _rms_bwd(h, g_, dxn)
        dh = dh + dres
        return dh, dh, dg

    nb = dict(tm=512, tn=D_MODEL, epilogue=norm_bwd, sums=[(1, D_MODEL)])
    dh2, dh2_b, d_norm_ple = _matmul("dx_ple_gate", dzg, w["w_ple_gate"], "nt", [F32, BF16], tk=D_MODEL,
                                     extras=[h2, dh3], consts=[w["norm_ple_g"]], **nb)
    d_w_down = _matmul("d_mlp_down", hid, dh2_b, "tn", [BF16], tm=512, tn=1024, tk=4096, a_map=square)
    dpre = _matmul("dx_mlp_down", dh2_b, w["w_down"], "nt", [BF16], tm=2048, tn=1024, tk=D_MODEL, extras=[hid],
                   epilogue=lambda acc, hid_: (acc * (2.0 * hid_.astype(F32)),))
    d_w_up = _matmul("d_mlp_up", xn2, dpre, "tn", [BF16], tm=1024, tn=D_FF // N_DEV, tk=4096, col_blocks_out=True)
    dh1, dh1_b, d_norm_mlp = _matmul("dx_mlp_up", dpre, w["w_up"], "nt", [F32, BF16], tk=D_FF,
                                     extras=[h1, dh2], consts=[w["norm_mlp_g"]], **nb)
    d_w_out = _matmul("d_out_proj", ycat, dh1_b, "tn", [BF16], tm=512, tn=1024, tk=4096)
    dycat = _matmul("dx_out_proj", dh1_b, w["w_out"], "nt", [F32], tm=1024, tn=1024, tk=D_MODEL)
    late_grads = dict(w_out=d_w_out, w_up=d_w_up, w_down=d_w_down, w_ple_gate=d_w_ple_gate, w_ple_proj=d_w_ple_proj)

    def conv_bwd(dy, blk, cw):
        gb, gc, hx = blk[:, :LANE], blk[:, LANE:2 * LANE], blk[:, 2 * LANE:]
        uu = gc * hx
        u1, u2 = _shift_down(uu, 1), _shift_down(uu, 2)
        dconv = dy * gb
        du = dconv * cw[2:3] + _shift_up(dconv, 1) * cw[1:2] + _shift_up(dconv, 2) * cw[0:1]
        s = lambda z: jnp.sum(z, axis=0, keepdims=True)
        d_blk = jnp.concatenate([dy * (uu * cw[2:3] + u1 * cw[1:2] + u2 * cw[0:1]), du * hx, du * gc], axis=1)
        return d_blk, s(dconv * u2), s(dconv * u1), s(dconv * uu)

    dproj, dcw0, dcw1, dcw2 = _colwise(
        "conv_bwd", conv_bwd, n_cb, [(dycat, LANE), (proj, 3 * LANE)], [w["conv_w"]],
        [(IN_PAD, BF16, 3 * LANE)], [(1, CONV_DIM)] * 3)

    def post_bwd(dy, y, r, k_h_, v, g_, ln_g, ln_b, r_k):
        _, vjp = jax.vjp(_rwkv_post, y, r, k_h_, v, g_, ln_g, ln_b, r_k)
        return vjp(dy)

    dy_rec, dr_p, dk_p, dv_p, dg, d_ln_g, d_ln_b, d_r_k = _rowwise(
        "rwkv_post_bwd", post_bwd, [(dycat, 512, 1), y_rec, u_r, k_h, u_v, g], post_c,
        [(RWKV_DIM, F32)] * 5, [(1, RWKV_DIM)] * 3)
    (dr_r, dv_r, dlw, dk_r, da, db), late_parts = _rec_bwd(
        u, lw, k_h, ra, rb, zs, dy_rec,
        [late_grads[n] if n in ("w_up", "w_ple_proj") else _reshard(n, late_grads[n]) for n in _LATE], [True] * len(_LATE))

    dproj, d_mu, d_w0, d_a0, d_k_k, d_k_a, d_wl, d_al, d_gl = _rwkv_pre_bwd(
        proj, u, [dr_p, dr_r, dv_p, dv_r, dlw, dk_p, dk_r, da, db, dg], w["shift_mu"], small, dproj)
    d_w_in = _matmul("d_in_proj", xn1, dproj, "tn", [BF16], tm=1024, tn=896, tk=4096)
    early_grads = dict(conv_w=jnp.concatenate([dcw0, dcw1, dcw2], axis=0),
                       w_lora_up=d_wl[:64], a_lora_up=d_al[:64], g_lora_up=d_gl[:160])
    early_send = [_split_w_in_grad(d_w_in)] + [_reshard(n, early_grads[n]) for n in _EARLY[1:]]
    s_sems, r_sems, sent, landing, token = _scatter_start("early_scatter_start", early_send, _own_slots("early_own_slots", early_send))
    dx, d_norm_mix = _matmul(
        "dx_in_proj", dproj, w["w_in"], "nt", [F32], tk=IN_PAD, extras=[x, dh1], consts=[w["norm_mix_g"] + token[0:1, 0:1]],
        **dict(nb, epilogue=lambda *a: norm_bwd(*a)[1:]))

    grads = dict(
        norm_mix_g=d_norm_mix, shift_mu=d_mu, w0=d_w0, a0=d_a0, k_k=d_k_k, k_a=d_k_a, r_k=d_r_k,
        ln_x_g=d_ln_g, ln_x_b=d_ln_b, norm_mlp_g=d_norm_mlp, norm_ple_g=d_norm_ple, norm_final_g=d_norm_final)
    return loss, dx, grads, dict(zip(_LATE, late_parts)), (s_sems, r_sems, sent, landing)


def _adam_update(partials, w_ref, m_ref, v_ref, g_ref, d_ref, nm_ref, nv_ref):
    g = partials[0].astype(F32)
    for part in partials[1:]:
        g = g + part.astype(F32)
    nm =ADAM_B1 * m_ref[...] + (1.0 - ADAM_B1) * g
    nv = ADAM_B2 * v_ref[...] + (1.0 - ADAM_B2) * (g * g)
    m_hat = nm / (1.0 - ADAM_B1 ** ADAM_STEP)
    v_hat = nv / (1.0 - ADAM_B2 ** ADAM_STEP)
    g_ref[...] = g
    d_ref[...] = -ADAM_LR * (m_hat / (jnp.sqrt(v_hat) + ADAM_EPS) + ADAM_WD * w_ref[...])
    nm_ref[...] = nm
    nv_ref[...] = nv


def _small_layout(widths):
    offs, off = [], 0
    for wd in list(widths) + [1]:
        offs.append(off)
        off += -(-wd // LANE) * LANE
    return offs, off


def _pack_small(vecs, loss):
    offs, total = _small_layout([v_.shape[1] for v_ in vecs])
    n = len(vecs)

    def body(*refs):
        pieces = []
        for ref in refs[:n + 1]:
            val = ref[...]
            pad = -val.shape[1] % LANE
            pieces += [val] + ([jnp.zeros((1, pad), F32)] if pad else [])
        row = jnp.concatenate(pieces, axis=1)
        rows = lax.broadcasted_iota(jnp.int32, (8, total), 0)
        refs[n + 1][...] = jnp.where(rows == 0, jnp.broadcast_to(row, (8, total)), 0.0)

    return pl.pallas_call(body, name="pack_small", out_shape=jax.ShapeDtypeStruct((8, total), F32))(*vecs, loss)


def _adamw_small(packed, ws, ms, vs):
    n = len(ws)
    offs, _ = _small_layout([w_.shape[1] for w_ in ws])

    def body(p_ref, *refs):
        w_refs, m_refs, v_refs, outs = refs[:n], refs[n:2 * n], refs[2 * n:3 * n], refs[3 * n:]
        for j in range(n):
            cols = pl.ds(offs[j], ws[j].shape[1])
            _adam_update([p_ref[s, 0:1, cols] for s in range(N_DEV)], w_refs[j], m_refs[j], v_refs[j], *outs[4 * j:4 * j + 4])
        total = p_ref[0, 0:1, offs[n]:offs[n] + 1]
        for s in range(1, N_DEV):
            total = total + p_ref[s, 0:1, offs[n]:offs[n] + 1]
        outs[4 * n][...] = total

    res = pl.pallas_call(
        body, name="adamw_small",
        out_shape=[jax.ShapeDtypeStruct(w_.shape, F32) for w_ in ws for _ in range(4)] + [jax.ShapeDtypeStruct((1, 1), F32)],
    )(packed, *ws, *ms, *vs)
    return [res[4 * j:4 * j + 4] for j in range(n)], res[4 * n]


def _adamw(name, parts, w, m, v):
    rows, cols = w.shape[-2:]
    lead = w.ndim - 2
    tr = rows if rows * cols * 4 * 8 <= (4 << 20) else max(8, (4 << 20) // (cols * 4 * 8) // 8 * 8)
    while rows % tr:
        tr -= 8

    def body(p_ref, *refs):
        _adam_update([p_ref[s] for s in range(N_DEV)], *refs)

    blk = pl.BlockSpec((None,) * lead + (tr, cols), lambda i: (0,) * lead + (i, 0))
    return pl.pallas_call(
        body, name=name, grid=(rows // tr,),
        in_specs=[pl.BlockSpec((N_DEV, tr, cols), lambda i: (0, i, 0)), blk, blk, blk], out_specs=[blk] * 4,
        out_shape=[jax.ShapeDtypeStruct(w.shape, F32)] * 4,
        compiler_params=_params(("arbitrary",)),
    )(parts, w, m, v)


def kernel(x, p, norm_mix_g, w_in, conv_w, shift_mu, w_lora_up, w0, a_lora_up, a0, g_lora_up, k_k, k_a, r_k, ln_x_g, ln_x_b, w_out, norm_mlp_g, w_up, w_down, norm_ple_g, w_ple_gate, w_ple_proj, norm_final_g, loss_target, m_norm_mix_g, m_w_in, m_conv_w, m_shift_mu, m_w_lora_up, m_w0, m_a_lora_up, m_a0, m_g_lora_up, m_k_k, m_k_a, m_r_k, m_ln_x_g, m_ln_x_b, m_w_out, m_norm_mlp_g, m_w_up, m_w_down, m_norm_ple_g, m_w_ple_gate, m_w_ple_proj, m_norm_final_g, v_norm_mix_g, v_w_in, v_conv_w, v_shift_mu, v_w_lora_up, v_w0, v_a_lora_up, v_a0, v_g_lora_up, v_k_k, v_k_a, v_r_k, v_ln_x_g, v_ln_x_b, v_w_out, v_norm_mlp_g, v_w_up, v_w_down, v_norm_ple_g, v_w_ple_gate, v_w_ple_proj, v_norm_final_g):
    args = dict(locals())
    wts = {n: args[n] for n in _WEIGHTS}
    mom = {n: args["m_" + n] for n in _WEIGHTS}
    var = {n: args["v_" + n] for n in _WEIGHTS}
    shard2d = lambda a: a.reshape(a.shape[-2:])
    pad_mu = lambda a: _pad_in_cols(jnp.concatenate([jnp.zeros((1, CONV_COLS), F32), a], axis=1))[:, CONV_COLS:]
    unpad_mu = lambda a: _unpad_in_cols(jnp.concatenate([jnp.zeros((1, CONV_COLS), F32), a], axis=1))[:, CONV_COLS:]

    shards = {n: shard2d(wts[n]).astype(BF16 if n in _BF16_GATHER else F32) for n in _SHARDED}
    w = {n: wts[n].reshape(1, -1) for n in _REPLICATED}
    w["shift_mu"] = pad_mu(wts["shift_mu"])

    loss, dx, grads, parts, in_flight = _local_step(x[0], p[0, 0], loss_target[0], w, [shards[n] for n in _EARLY],
                                                    [shards[n] for n in _LATE])

    out = {n: _adamw("adamw_" + n, parts[n], wts[n], mom[n], var[n]) for n in _LATE}
    early_parts = _scatter_wait("early_scatter_wait", *in_flight, after=[dx] + [out[n][1] for n in _LATE])
    for n, prt in zip(_EARLY, early_parts):
        out[n] = _adamw("adamw_" + n, prt, wts[n], mom[n], var[n])

    grads["shift_mu"] = unpad_mu(grads["shift_mu"])
    flat = lambda a: a.reshape(1, -1)
    (small_parts,) = _exchange("gather_small", [_pack_small([flat(grads[n]) for n in _REPLICATED], loss)], [False])
    small, loss_total = _adamw_small(small_parts, *[[flat(d[n]) for n in _REPLICATED] for d in (wts, mom, var)])
    for n, res in zip(_REPLICATED, small):
        out[n] = [r.reshape(wts[n].shape) for r in res]
    return (loss_total[0, 0], dx[None], *[out[n][0] for n in _WEIGHTS], *[out[n][1] for n in _WEIGHTS],
            *[out[n][2] for n in _WEIGHTS], *[out[n][3] for n in _WEIGHTS])
```

```python
import functools

import jax
import jax.numpy as jnp
from jax import lax
from jax.experimental import pallas as pl
from jax.experimental.pallas import tpu as pltpu

F32 = jnp.float32
BF16 = jnp.bfloat16

N_DEV = 8
D_MODEL = 1024
CONV_DIM = 512
RWKV_DIM = 512
HEAD_DIM = 64
N_HEADS = 8
D_FF = 4096
PLE_DIM = 256
RMS_EPS = 1e-6
GN_EPS = 64e-5
L2_EPS = 1e-12
ADAM_LR, ADAM_B1, ADAM_B2, ADAM_EPS, ADAM_WD, ADAM_STEP = 0.001, 0.9, 0.999, 1e-08, 0.01, 10

CONV_COLS = 3 * CONV_DIM
RW_PAD = 2048
IN_PAD = CONV_COLS + RW_PAD
IN_COLS = 3360
XW_OFF, XA_OFF, XG_OFF = 1536, 1664, 1792
REC_CHUNK = 128
REC_PASSES = 1
ROW_BLOCK = 256
LANE = 128
VMEM_LIMIT = 56 * 1024 * 1024


def _dims(dn, ndim):
    if ndim == 3:
        return {"nn": (((2,), (1,)), ((0,), (0,))), "nt": (((2,), (2,)), ((0,), (0,))),
                "tn": (((1,), (1,)), ((0,), (0,)))}[dn]
    return {"nn": (((1,), (0,)), ((), ())), "nt": (((1,), (1,)), ((), ())), "tn": (((0,), (0,)), ((), ()))}[dn]


def _split2(x):
    hi = x.astype(BF16)
    return hi, (x - hi.astype(F32)).astype(BF16)


def _mm_raw(x, y, dn, passes):
    f = lambda p, q: lax.dot_general(p, q, _dims(dn, x.ndim), preferred_element_type=F32)
    if passes == 1:
        return f(x.astype(BF16), y.astype(BF16))
    xh, xl = _split2(x)
    yh, yl = _split2(y)
    if passes == 2:
        return f(xh, yh) + f(xh, yl)
    return f(xh, yh) + f(xh, yl) + f(xl, yh)


@functools.partial(jax.custom_vjp, nondiff_argnums=(2, 3))
def _mm(x, y, dn, passes):
    return _mm_raw(x, y, dn, passes)


def _mm_fwd(x, y, dn, passes):
    return _mm_raw(x, y, dn, passes), (x, y)


def _mm_bwd(dn, passes, res, d):
    x, y = res
    if dn == "nn":
        return _mm(d, y, "nt", passes), _mm(x, d, "tn", passes)
    if dn == "nt":
        return _mm(d, y, "nn", passes), _mm(d, x, "tn", passes)
    return _mm(y, d, "nt", passes), _mm(x, d, "nn", passes)


_mm.defvjp(_mm_fwd, _mm_bwd)


def _head_ones():
    i = lax.broadcasted_iota(jnp.int32, (RWKV_DIM, RWKV_DIM), 0) // HEAD_DIM
    j = lax.broadcasted_iota(jnp.int32, (RWKV_DIM, RWKV_DIM), 1) // HEAD_DIM
    return (i == j).astype(BF16)


def _hsum_raw(x):
    ones = _head_ones()
    f = lambda p: lax.dot_general(p, ones, _dims("nn", 2), preferred_element_type=F32)
    x1, x2 = _split2(x)
    return f(x1) + f(x2)


@jax.custom_vjp
def _hsum(x):
    return _hsum_raw(x)


_hsum.defvjp(lambda x: (_hsum_raw(x), None), lambda _, d: (_hsum(d),))


def _sigmoid(x):
    return 1.0 / (1.0 + jnp.exp(-x))


def _softplus(x):
    return jnp.maximum(x, 0.0) + jnp.log(1.0 + jnp.exp(-jnp.abs(x)))


def _params(sem):
    return pltpu.CompilerParams(dimension_semantics=sem, vmem_limit_bytes=VMEM_LIMIT)


def _rowwise(name, fn, rows, consts, row_outs, acc_outs=(), tr=ROW_BLOCK, halo=False, gather=()):
    rows = [r if isinstance(r, tuple) else (r, r.shape[1], 0) for r in rows]
    t_len = rows[0][0].shape[0]
    tr = min(tr, t_len)
    n_r, n_c, n_o, n_a, n_x = len(rows), len(consts), len(row_outs), len(acc_outs), len(gather)
    n_h = n_r if halo else 0
    sub = 8
    x_specs, x_shapes, x_sems = _exchange_io(gather, [False] * n_x) if n_x else ([], [], [])
    nb = t_len // tr

    def body(*refs):
        if n_x:
            n_in = n_r + n_h + n_c
            start, forward, wait = _gather_plan(refs[n_in:n_in + n_x], refs[len(refs) - 3 - n_x:len(refs) - 3], *refs[len(refs) - 3:])
            pl.when(pl.program_id(0) == 0)(start)
            refs = refs[:n_in] + refs[n_in + n_x:len(refs) - 3 - n_x]
        ins = [r[...] for r in refs[:n_r]]
        ins += [jnp.where(pl.program_id(0) == 0, 0.0, r[sub - 1:sub, :]) for r in refs[n_r:n_r + n_h]]
        ins += [r[...] for r in refs[n_r + n_h:n_r + n_h + n_c]]
        refs = refs[:n_r] + refs[n_r + n_h:]
        outs = fn(*ins)
        o_refs = refs[n_r + n_c:n_r + n_c + n_o]
        a_refs = refs[n_r + n_c + n_o:]
        for o_ref, val in zip(o_refs, outs[:n_o]):
            o_ref[...] = val.astype(o_ref.dtype)
        if n_a:
            first = pl.program_id(0) == 0

            @pl.when(first)
            def _():
                for a_ref, val in zip(a_refs, outs[n_o:]):
                    a_ref[...] = val

            @pl.when(jnp.logical_not(first))
            def _():
                for a_ref, val in zip(a_refs, outs[n_o:]):
                    a_ref[...] += val

        if n_x:
            @pl.when(pl.program_id(0) == nb - 1)
            def _():
                for j in range(n_x):
                    forward(j)
                wait()

    in_specs = [pl.BlockSpec((tr, w), functools.partial(lambda i, c: (i, c), c=cb)) for _, w, cb in rows]
    if halo:
        in_specs += [pl.BlockSpec((sub, w), functools.partial(lambda i, c: (jnp.maximum(i * (tr // sub) - 1, 0), c), c=cb))
                     for _, w, cb in rows]
    in_specs += [pl.BlockSpec(c.shape, functools.partial(lambda i, n: (0,) * n, n=c.ndim)) for c in consts]
    out_specs = [pl.BlockSpec((tr, w), lambda i: (i, 0)) for w, _ in row_outs]
    out_specs += [pl.BlockSpec(s, functools.partial(lambda i, n: (0,) * n, n=len(s))) for s in acc_outs]
    out_shape = [jax.ShapeDtypeStruct((t_len, w), dt) for w, dt in row_outs]
    out_shape += [jax.ShapeDtypeStruct(s, F32) for s in acc_outs]
    return pl.pallas_call(
        body, name=name, grid=(nb,), in_specs=in_specs + x_specs, out_specs=out_specs + x_specs,
        out_shape=out_shape + x_shapes, scratch_shapes=x_sems,
        compiler_params=pltpu.CompilerParams(dimension_semantics=("arbitrary",), vmem_limit_bytes=VMEM_LIMIT,
                                             has_side_effects=bool(n_x)),
    )(*[r[0] for r in rows], *([r[0] for r in rows] if halo else []), *consts, *gather)


def _colwise(name, fn, n_blocks, cols, prms, col_outs, prm_outs=()):
    t_len = cols[0][0].shape[0]
    n_i = len(cols) + len(prms)

    def body(*refs):
        outs = fn(*[r[...] for r in refs[:n_i]])
        for o_ref, val in zip(refs[n_i:], outs):
            o_ref[...] = val.astype(o_ref.dtype)

    spec = lambda r, w: pl.BlockSpec((r, w), lambda j: (0, j))
    in_specs = [spec(t_len, w) for _, w in cols] + [spec(a.shape[0], LANE) for a in prms]
    out_specs = [spec(t_len, bw) for _, _, bw in col_outs] + [spec(r, LANE) for r, _ in prm_outs]
    out_shape = [jax.ShapeDtypeStruct((t_len, w), dt) for w, dt, _ in col_outs]
    out_shape += [jax.ShapeDtypeStruct((r, w), F32) for r, w in prm_outs]
    return pl.pallas_call(
        body, name=name, grid=(n_blocks,), in_specs=in_specs, out_specs=out_specs, out_shape=out_shape,
        compiler_params=_params(("arbitrary",)),
    )(*[c[0] for c in cols], *prms)


def _matmul(name, a, b, dn, outs, *, tm, tn, tk, extras=(), consts=(), epilogue=None, sums=(), xch=(), xch_scatter=(),
            a_map=None, col_blocks_out=False):
    if dn == "nn":
        (m, k), n = a.shape, b.shape[1]
    elif dn == "nt":
        (m, k), n = a.shape, b.shape[0]
    else:
        (k, m), n = a.shape, b.shape[1]
    tm, tn, tk = min(tm, m), min(tn, n), min(tk, k)
    nk = k // tk
    grid = (m // tm, n // tn, nk)
    assert nk == 1 and (not sums or grid[1] == 1)
    a_spec = pl.BlockSpec((tk, tm), lambda i, j, q: (q, i)) if dn == "tn" else pl.BlockSpec((tm, tk), lambda i, j, q: (i, q))
    b_spec = pl.BlockSpec((tn, tk), lambda i, j, q: (j, q)) if dn == "nt" else pl.BlockSpec((tk, tn), lambda i, j, q: (q, j))
    o_spec = pl.BlockSpec((tm, tn), lambda i, j, q: (i, j))
    c_spec = pl.BlockSpec((1, tn), lambda i, j, q: (0, j))
    n_e, n_c, n_o, n_s, n_x = len(extras), len(consts), len(outs), len(sums), len(xch)
    x_specs, x_shapes, x_sems = _exchange_io(xch, xch_scatter) if n_x else ([], [], [])

    def body(*refs):
        a_ref, b_ref = refs[:2]
        e_refs = refs[2:2 + n_e + n_c]
        x_in = refs[2 + n_e + n_c:2 + n_e + n_c + n_x]
        rest = refs[2 + n_e + n_c + n_x:]
        o_refs, s_refs, x_out, scratch = rest[:n_o], rest[n_o:n_o + n_s], rest[n_o + n_s:n_o + n_s + n_x], rest[n_o + n_s + n_x:]
        step = (pl.program_id(0) * grid[1] + pl.program_id(1)) * nk + pl.program_id(2)
        if n_x:
            start, wait = _exchange_plan(x_in, x_out, xch_scatter, *scratch[len(scratch) - 3:])
            pl.when(step == 0)(start)
        a_blk = a_ref[...] if a_map is None else a_map(a_ref[...])
        acc = lax.dot_general(a_blk.astype(BF16), b_ref[...].astype(BF16), _dims(dn, 2), preferred_element_type=F32)
        vals = (acc,) if epilogue is None else epilogue(acc, *[e[...] for e in e_refs])
        for o_ref, val in zip(o_refs, vals[:n_o]):
            o_ref[...] = val.astype(o_ref.dtype)
        if n_s:
            @pl.when(step == 0)
            def _():
                for s_ref, val in zip(s_refs, vals[n_o:]):
                    s_ref[...] = val

            @pl.when(step > 0)
            def _():
                for s_ref, val in zip(s_refs, vals[n_o:]):
                    s_ref[...] += val

        if n_x:
            pl.when(step == grid[0] * grid[1] * nk - 1)(wait)

    plain = not (n_s or n_x)
    res = pl.pallas_call(
        body, name=name, grid=grid,
        in_specs=[a_spec, b_spec] + [o_spec] * n_e + [c_spec] * n_c + x_specs,
        out_specs=[pl.BlockSpec((None, tm, tn), lambda i, j, q: (j, i, 0)) if col_blocks_out else o_spec] * n_o
                  + [c_spec] * n_s + x_specs,
        out_shape=[jax.ShapeDtypeStruct((n // tn, m, tn) if col_blocks_out else (m, n), dt) for dt in outs] + [jax.ShapeDtypeStruct(s, F32) for s in sums] + x_shapes,
        scratch_shapes=x_sems,
        compiler_params=pltpu.CompilerParams(
            dimension_semantics=("parallel", "parallel", "arbitrary") if plain else ("arbitrary",) * 3,
            vmem_limit_bytes=VMEM_LIMIT, has_side_effects=bool(n_x)),
    )(a, b, *extras, *consts, *xch)
    return res[0] if len(res) == 1 else res


def _rms(h, g):
    return h * lax.rsqrt(jnp.mean(h * h, axis=-1, keepdims=True) + RMS_EPS) * g


def _rms_bwd(h, g, dy):
    rs = lax.rsqrt(jnp.mean(h * h, axis=-1, keepdims=True) + RMS_EPS)
    n = h * rs
    dn = dy * g
    dh = rs * (dn - n * jnp.mean(dn * n, axis=-1, keepdims=True))
    return dh, jnp.sum(dy * n, axis=0, keepdims=True)


def _rwkv_pre(k, xw, xa, xg, w0, a0, k_k, k_a, wl, al, gl):
    zw = w0 + _mm(jnp.tanh(xw), wl, "nn", 1)
    lw = -jnp.exp(-_softplus(-zw) - 0.5)
    iclr = _sigmoid(a0 + _mm(xa, al, "nn", 1))
    g = _mm(_sigmoid(xg), gl, "nn", 1)
    kk0 = k * k_k
    kk = kk0 / jnp.maximum(jnp.sqrt(_hsum(kk0 * kk0)), L2_EPS)
    k_h = k * (1.0 + (iclr - 1.0) * k_a)
    return lw, k_h, -kk, kk * iclr, g


def _rwkv_post(y, r, k_h, v, g, ln_g, ln_b, r_k):
    mu = _hsum(y) * (1.0 / HEAD_DIM)
    yc = y - mu
    var = _hsum(yc * yc) * (1.0 / HEAD_DIM)
    yo = yc * lax.rsqrt(var + GN_EPS) * ln_g + ln_b
    bonus = _hsum(r * k_h * r_k) * v
    return (yo + bonus) * g


def _shift_down(x, n):
    rows = lax.broadcasted_iota(jnp.int32, x.shape, 0)
    return jnp.where(rows < n, 0.0, pltpu.roll(x, n, 0))


def _shift_up(x, n):
    t_len = x.shape[0]
    rows = lax.broadcasted_iota(jnp.int32, x.shape, 0)
    return jnp.where(rows >= t_len - n, 0.0, pltpu.roll(x, t_len - n, 0))


def _exchange_plan(ins, outs, scatter, send_sems, recv_sems, local_sems):
    x, y, c = lax.axis_index("x"), lax.axis_index("y"), lax.axis_index("c")
    me = 4 * x + 2 * y + c

    def local(i):
        return pltpu.make_async_copy(ins[i].at[me] if scatter[i] else ins[i], outs[i].at[me], local_sems.at[i])

    def send(i, rel):
        return pltpu.make_async_remote_copy(
            src_ref=ins[i].at[me ^ rel] if scatter[i] else ins[i], dst_ref=outs[i].at[me],
            send_sem=send_sems.at[i, rel - 1], recv_sem=recv_sems.at[i, rel - 1],
            device_id=(x ^ (rel >> 2), y ^ ((rel >> 1) & 1), c ^ (rel & 1)), device_id_type=pl.DeviceIdType.MESH)

    def landed(i, rel):
        slot = outs[i].at[me ^ rel]
        return pltpu.make_async_remote_copy(
            src_ref=slot, dst_ref=slot, send_sem=send_sems.at[i, rel - 1], recv_sem=recv_sems.at[i, rel - 1],
            device_id=(x, y, c), device_id_type=pl.DeviceIdType.MESH)

    def start():
        for i in range(len(ins)):
            local(i).start()
            for rel in range(1, N_DEV):
                send(i, rel).start()

    def wait():
        for i in range(len(ins)):
            local(i).wait()
            for rel in range(1, N_DEV):
                landed(i, rel).wait_recv()
            for rel in range(1, N_DEV):
                send(i, rel).wait_send()

    return start, wait


def _gather_plan(ins, outs, send_sems, recv_sems, local_sems):
    x, y, c = lax.axis_index("x"), lax.axis_index("y"), lax.axis_index("c")
    me = 4 * x + 2 * y + c
    direct, chips = (1, 2, 4, 6), (2, 4, 6)

    def local(i):
        return pltpu.make_async_copy(ins[i], outs[i].at[me], local_sems.at[i])

    def send(i, rel):
        return pltpu.make_async_remote_copy(
            src_ref=ins[i], dst_ref=outs[i].at[me], send_sem=send_sems.at[i, rel - 1], recv_sem=recv_sems.at[i, rel - 1],
            device_id=(x ^ (rel >> 2), y ^ ((rel >> 1) & 1), c ^ (rel & 1)), device_id_type=pl.DeviceIdType.MESH)

    def passed(i, rel):
        slot = outs[i].at[me ^ rel]
        return pltpu.make_async_remote_copy(
            src_ref=slot, dst_ref=slot, send_sem=send_sems.at[i, rel], recv_sem=recv_sems.at[i, rel],
            device_id=(x, y, 1 - c), device_id_type=pl.DeviceIdType.MESH)

    def landed(i, rel):
        slot = outs[i].at[me ^ rel]
        return pltpu.make_async_remote_copy(
            src_ref=slot, dst_ref=slot, send_sem=send_sems.at[i, rel - 1], recv_sem=recv_sems.at[i, rel - 1],
            device_id=(x, y, c), device_id_type=pl.DeviceIdType.MESH)

    def start():
        for i in range(len(ins)):
            local(i).start()
            for rel in direct:
                send(i, rel).start()

    def forward(i):
        for rel in chips:
            landed(i, rel).wait_recv()
            passed(i, rel).start()

    def wait():
        for i in range(len(ins)):
            local(i).wait()
            for rel in (1, 3, 5, 7):
                landed(i, rel).wait_recv()
            for rel in direct:
                send(i, rel).wait_send()
            for rel in chips:
                passed(i, rel).wait_send()

    return start, forward, wait


def _exchange_io(arrays, scatter):
    n = len(arrays)
    any_spec = pl.BlockSpec(memory_space=pl.ANY)
    out_shape = [jax.ShapeDtypeStruct(a.shape if sc else (N_DEV,) + a.shape, a.dtype) for a, sc in zip(arrays, scatter)]
    sems = [pltpu.SemaphoreType.DMA((n, N_DEV - 1)), pltpu.SemaphoreType.DMA((n, N_DEV - 1)), pltpu.SemaphoreType.DMA((n,))]
    return [any_spec] * n, out_shape, sems


def _exchange(name, arrays, scatter):
    n = len(arrays)
    specs, out_shape, sems = _exchange_io(arrays, scatter)

    def body(*refs):
        if any(scatter):
            start, wait = _exchange_plan(refs[:n], refs[n:2 * n], scatter, *refs[2 * n:])
            start()
        else:
            start, forward, wait = _gather_plan(refs[:n], refs[n:2 * n], *refs[2 * n:])
            start()
            for i in range(n):
                forward(i)
        wait()

    return pl.pallas_call(
        body, name=name, in_specs=specs, out_specs=specs, out_shape=out_shape, scratch_shapes=sems,
        compiler_params=pltpu.CompilerParams(has_side_effects=True),
    )(*arrays)


def _scatter_start(name, arrays, lands):
    n = len(arrays)
    hbm = pl.BlockSpec(memory_space=pltpu.HBM)

    def body(*refs):
        ins, land, send_sems, recv_sems = refs[:n], refs[n:2 * n], refs[2 * n], refs[2 * n + 1]
        token = refs[4 * n + 2]
        x, y, c = lax.axis_index("x"), lax.axis_index("y"), lax.axis_index("c")
        me = 4 * x + 2 * y + c
        for i in range(n):
            for rel in range(1, N_DEV):
                k = i * (N_DEV - 1) + rel - 1
                pltpu.make_async_remote_copy(
                    src_ref=ins[i].at[me ^ rel], dst_ref=land[i].at[me], send_sem=send_sems.at[k],
                    recv_sem=recv_sems.at[k], device_id=(x ^ (rel >> 2), y ^ ((rel >> 1) & 1), c ^ (rel & 1)),
                    device_id_type=pl.DeviceIdType.MESH).start()
        token[...] = jnp.zeros_like(token)

    sem = pltpu.SemaphoreType.DMA((n * (N_DEV - 1),))
    bufs = [pltpu.HBM(a.shape, a.dtype) for a in list(arrays) + list(lands)]
    res = pl.pallas_call(
        body, name=name, out_shape=(sem, sem, *bufs, jax.ShapeDtypeStruct((8, LANE), F32)),
        in_specs=[hbm] * (2 * n),
        out_specs=(pl.BlockSpec(memory_space=pltpu.SEMAPHORE),) * 2 + (hbm,) * (2 * n) + (pl.BlockSpec(memory_space=pltpu.VMEM),),
        input_output_aliases={i: 2 + i for i in range(2 * n)},
        compiler_params=pltpu.CompilerParams(has_side_effects=pltpu.SideEffectType.DATAFLOW_SIDE_EFFECTING),
    )(*[pltpu.with_memory_space_constraint(a, pltpu.HBM) for a in list(arrays) + list(lands)])
    return res[0], res[1], res[2:2 + n], res[2 + n:2 + 2 * n], res[2 + 2 * n]


def _scatter_wait(name, send_sems, recv_sems, arrays, lands, after):
    n, n_after = len(arrays), len(after)
    hbm = pl.BlockSpec(memory_space=pltpu.HBM)

    def body(*refs):
        ins, land, s_sems, r_sems = refs[:n], refs[n:2 * n], refs[2 * n], refs[2 * n + 1]
        x, y, c = lax.axis_index("x"), lax.axis_index("y"), lax.axis_index("c")
        me = 4 * x + 2 * y + c
        for i in range(n):
            for rel in range(1, N_DEV):
                k = i * (N_DEV - 1) + rel - 1
                cp = pltpu.make_async_remote_copy(
                    src_ref=ins[i].at[me ^ rel], dst_ref=land[i].at[me ^ rel], send_sem=s_sems.at[k],
                    recv_sem=r_sems.at[k], device_id=(x, y, c), device_id_type=pl.DeviceIdType.MESH)
                cp.wait_send()
                cp.wait_recv()

    res = pl.pallas_call(
        body, name=name, out_shape=[pltpu.HBM(a.shape, a.dtype) for a in list(arrays) + list(lands)],
        in_specs=[hbm] * (2 * n) + [pl.BlockSpec(memory_space=pltpu.SEMAPHORE)] * 2 + [pl.BlockSpec(memory_space=pl.ANY)] * n_after,
        out_specs=[hbm] * (2 * n), input_output_aliases={i: i for i in range(2 * n)},
        compiler_params=pltpu.CompilerParams(has_side_effects=pltpu.SideEffectType.DATAFLOW_SIDE_EFFECTING),
    )(*arrays, *lands, send_sems, recv_sems, *after)
    return res[:n], res[n:]


def _tri_powers(low):
    powers, n = [low], 1
    while 2 * n < low.shape[-1]:
        powers.append(_mm(powers[-1], powers[-1], "nn", REC_PASSES))
        n *= 2
    return powers


@jax.custom_vjp
def _tri_solve(low, rhs):
    for p in _tri_powers(low):
        rhs = rhs + _mm(p, rhs, "nn", REC_PASSES)
    return rhs


def _tri_solve_fwd(low, rhs):
    powers = _tri_powers(low)
    for p in powers:
        rhs = rhs + _mm(p, rhs, "nn", REC_PASSES)
    return rhs, (powers, rhs)


def _tri_solve_bwd(res, d):
    powers, u = res
    for p in powers:
        d = d + _mm(p, d, "tn", REC_PASSES)
    return _mm(d, u, "nt", REC_PASSES), d


_tri_solve.defvjp(_tri_solve_fwd, _tri_solve_bwd)


def _chunk_fwd(z0, r, lw, k, v, a, b):
    n_h, c, n_k = r.shape
    mm = functools.partial(_mm, passes=REC_PASSES)
    gram = functools.partial(_mm, passes=2)
    ti = lax.broadcasted_iota(jnp.int32, (c, c), 0)
    si = lax.broadcasted_iota(jnp.int32, (c, c), 1)
    strict, incl = si < ti, si <= ti
    cum = _mm(jnp.broadcast_to(incl.astype(F32), (n_h, c, c)), lw, "nn", 3)
    cum_end = cum[:, c - 1:c, :]
    e_neg, e_end = jnp.exp(-cum), jnp.exp(cum_end - cum)
    x2 = jnp.concatenate([a * jnp.exp(cum - lw), r * jnp.exp(cum)], axis=1)
    y2 = jnp.concatenate([b * e_neg, k * e_neg], axis=1)
    mask = jnp.concatenate([jnp.concatenate([strict, strict], axis=1), jnp.concatenate([incl, incl], axis=1)], axis=0)
    g2 = jnp.where(mask, gram(x2, y2, "nt"), 0.0)
    t2 = mm(x2, z0, "nn") + mm(g2[:, :, c:], v, "nn")
    u = _tri_solve(g2[:, :c, :c], t2[:, :c])
    y = t2[:, c:] + mm(g2[:, c:, :c], u, "nn")
    ki = lax.broadcasted_iota(jnp.int32, (n_k, n_k), 0)
    kj = lax.broadcasted_iota(jnp.int32, (n_k, n_k), 1)
    dmat = jnp.where(ki == kj, jnp.broadcast_to(jnp.exp(cum_end), (n_h, n_k, n_k)), 0.0)
    z_end = mm(dmat, z0, "nn") + mm(jnp.concatenate([b * e_end, k * e_end], axis=1), jnp.concatenate([u, v], axis=1), "tn")
    return y, z_end


def _heads(x):
    return jnp.stack([x[:, h * HEAD_DIM:(h + 1) * HEAD_DIM] for h in range(N_HEADS)])


def _unheads(x):
    return jnp.concatenate([x[h] for h in range(N_HEADS)], axis=-1)


def _rec_params():
    return pltpu.CompilerParams(dimension_semantics=("arbitrary",), vmem_limit_bytes=VMEM_LIMIT, has_side_effects=True)


def _rec_fwd(u, lw, k, a, b, xch):
    t_len = lw.shape[0]
    c = min(REC_CHUNK, t_len)
    nc = t_len // c
    n_x = len(xch)
    x_specs, x_shapes, x_sems = _exchange_io(xch, [False] * n_x)
    sizes = [a_.size * a_.dtype.itemsize for a_ in xch]
    pass_step = [min(nc - 1, int(0.9 * nc * sum(sizes[:j + 1]) / sum(sizes)) + 1) for j in range(n_x)]

    def body(*refs):
        r_ref, v_ref, lw_ref, k_ref, a_ref, b_ref = refs[:6]
        x_in = refs[6:6 + n_x]
        y_ref, zs_ref = refs[6 + n_x:8 + n_x]
        x_out = refs[8 + n_x:8 + 2 * n_x]
        z_scr = refs[8 + 2 * n_x]
        start, forward, wait = _gather_plan(x_in, x_out, *refs[9 + 2 * n_x:])
        i = pl.program_id(0)

        @pl.when(i == 0)
        def _():
            start()
            z_scr[...] = jnp.zeros_like(z_scr)

        z0 = z_scr[...]
        zs_ref[0] = z0
        y, z_end = _chunk_fwd(z0, _heads(r_ref[...]), _heads(lw_ref[...]), _heads(k_ref[...]), _heads(v_ref[...]),
                              _heads(a_ref[...]), _heads(b_ref[...]))
        y_ref[...] = _unheads(y)
        z_scr[...] = z_end

        for j in range(n_x):
            pl.when(i == pass_step[j])(functools.partial(forward, j))

        @pl.when(i == nc - 1)
        def _():
            wait()

    blk = lambda cb: pl.BlockSpec((c, RWKV_DIM), functools.partial(lambda i, q: (i, q), q=cb))
    res = pl.pallas_call(
        body, name="rwkv_rec_fwd", grid=(nc,),
        in_specs=[blk(0), blk(2)] + [blk(0)] * 4 + x_specs,
        out_specs=[blk(0), pl.BlockSpec((1, N_HEADS, HEAD_DIM, HEAD_DIM), lambda i: (i, 0, 0, 0))] + x_specs,
        out_shape=[jax.ShapeDtypeStruct((t_len, RWKV_DIM), F32),
                   jax.ShapeDtypeStruct((nc, N_HEADS, HEAD_DIM, HEAD_DIM), F32)] + x_shapes,
        scratch_shapes=[pltpu.VMEM((N_HEADS, HEAD_DIM, HEAD_DIM), F32)] + x_sems,
        compiler_params=_rec_params(),
    )(u, u, lw, k, a, b, *xch)
    return res[0], res[1], res[2:]


def _rec_bwd(u, lw, k, a, b, zs, dy, xch, xch_scatter):
    t_len = lw.shape[0]
    c = min(REC_CHUNK, t_len)
    nc = t_len // c
    n_x = len(xch)
    x_specs, x_shapes, x_sems = _exchange_io(xch, xch_scatter)

    def body(*refs):
        r_ref, v_ref, lw_ref, k_ref, a_ref, b_ref, zs_ref, dy_ref = refs[:8]
        x_in = refs[8:8 + n_x]
        g_refs = refs[8 + n_x:14 + n_x]
        x_out = refs[14 + n_x:14 + 2 * n_x]
        dz_scr = refs[14 + 2 * n_x]
        start, wait = _exchange_plan(x_in, x_out, xch_scatter, *refs[15 + 2 * n_x:])
        i = pl.program_id(0)

        @pl.when(i == 0)
        def _():
            start()
            dz_scr[...] = jnp.zeros_like(dz_scr)

        _, vjp = jax.vjp(_chunk_fwd, zs_ref[0], _heads(r_ref[...]), _heads(lw_ref[...]), _heads(k_ref[...]),
                         _heads(v_ref[...]), _heads(a_ref[...]), _heads(b_ref[...]))
        dz0, dr, dlw, dk, dv, da, db = vjp((_heads(dy_ref[...]), dz_scr[...]))
        for ref, val in zip(g_refs, (dr, dv, dlw, dk, da, db)):
            ref[...] = _unheads(val)
        dz_scr[...] = dz0

        @pl.when(i == nc - 1)
        def _():
            wait()

    blk = lambda cb: pl.BlockSpec((c, RWKV_DIM), functools.partial(lambda i, q: (nc - 1 - i, q), q=cb))
    res = pl.pallas_call(
        body, name="rwkv_rec_bwd", grid=(nc,),
        in_specs=[blk(0), blk(2)] + [blk(0)] * 4
                 + [pl.BlockSpec((1, N_HEADS, HEAD_DIM, HEAD_DIM), lambda i: (nc - 1 - i, 0, 0, 0)), blk(0)] + x_specs,
        out_specs=[blk(0)] * 6 + x_specs,
        out_shape=[jax.ShapeDtypeStruct((t_len, RWKV_DIM), F32)] * 6 + x_shapes,
        scratch_shapes=[pltpu.VMEM((N_HEADS, HEAD_DIM, HEAD_DIM), F32)] + x_sems,
        compiler_params=_rec_params(),
    )(u, u, lw, k, a, b, zs, dy, *xch)
    return res[:6], res[6:]


_EARLY = ["w_in", "conv_w", "w_lora_up", "a_lora_up", "g_lora_up"]
_LATE = ["w_out", "w_up", "w_down", "w_ple_gate", "w_ple_proj"]
_SHARDED = _EARLY + _LATE
_COL_SHARDED = {"w_in", "conv_w", "w_lora_up", "a_lora_up", "g_lora_up", "w_up", "w_ple_proj"}
_BF16_GATHER = {"w_in", "w_out", "w_up", "w_down", "w_ple_gate", "w_ple_proj"}
_REPLICATED = ["norm_mix_g", "shift_mu", "w0", "a0", "k_k", "k_a", "r_k", "ln_x_g", "ln_x_b", "norm_mlp_g", "norm_ple_g",
               "norm_final_g"]
_WEIGHTS = ["norm_mix_g", "w_in", "conv_w", "shift_mu", "w_lora_up", "w0", "a_lora_up", "a0", "g_lora_up", "k_k", "k_a", "r_k",
            "ln_x_g", "ln_x_b", "w_out", "norm_mlp_g", "w_up", "w_down", "norm_ple_g", "w_ple_gate", "w_ple_proj", "norm_final_g"]


def _unshard(name, g):
    if name in _COL_SHARDED:
        return jnp.moveaxis(g, 0, 1).reshape(g.shape[1], N_DEV * g.shape[2])
    return g.reshape(N_DEV * g.shape[1], g.shape[2])


def _reshard(name, full):
    if name in _COL_SHARDED:
        return jnp.moveaxis(full.reshape(full.shape[0], N_DEV, full.shape[1] // N_DEV), 1, 0)
    return full.reshape(N_DEV, full.shape[0] // N_DEV, full.shape[1])


def _pad_in_cols(a):
    z = lambda n: jnp.zeros(a.shape[:-1] + (n,), a.dtype)
    conv = [a[..., part * CONV_DIM + j * LANE:part * CONV_DIM + (j + 1) * LANE] for j in range(CONV_DIM // LANE) for part in range(3)]
    return jnp.concatenate(conv + [a[..., CONV_COLS:3136], z(64), a[..., 3136:3200], z(64), a[..., 3200:3360], z(96)], axis=-1)


def _unpad_in_cols(a):
    conv = [a[..., (3 * j + part) * LANE:(3 * j + part + 1) * LANE] for part in range(3) for j in range(CONV_DIM // LANE)]
    return jnp.concatenate(conv + [a[..., CONV_COLS:3136], a[..., 3200:3264], a[..., 3328:3488]], axis=-1)


def _assemble_w_in(g):
    n_dev, rows, cols = g.shape

    def body(g_ref, o_ref):
        o_ref[...] = _pad_in_cols(jnp.concatenate([g_ref[d] for d in range(n_dev)], axis=1))

    return pl.pallas_call(
        body, name="w_in_assemble", grid=(rows // ROW_BLOCK,),
        in_specs=[pl.BlockSpec((n_dev, ROW_BLOCK, cols), lambda i: (0, i, 0))],
        out_specs=pl.BlockSpec((ROW_BLOCK, IN_PAD), lambda i: (i, 0)),
        out_shape=jax.ShapeDtypeStruct((rows, IN_PAD), g.dtype), compiler_params=_params(("arbitrary",)),
    )(g)


def _split_w_in_grad(dw):
    rows = dw.shape[0]
    cols = IN_COLS // N_DEV

    def body(d_ref, o_ref):
        full = _unpad_in_cols(d_ref[...])
        for d in range(N_DEV):
            o_ref[d] = full[:, cols * d:cols * (d + 1)]

    return pl.pallas_call(
        body, name="w_in_grad_split", grid=(rows // ROW_BLOCK,),
        in_specs=[pl.BlockSpec((ROW_BLOCK, IN_PAD), lambda i: (i, 0))],
        out_specs=pl.BlockSpec((N_DEV, ROW_BLOCK, cols), lambda i: (0, i, 0)),
        out_shape=jax.ShapeDtypeStruct((N_DEV, rows, cols), dw.dtype), compiler_params=_params(("arbitrary",)),
    )(dw)


def _pad_rows(a, rows):
    return jnp.concatenate([a, jnp.zeros((rows - a.shape[0],) + a.shape[1:], a.dtype)], axis=0)


SEG_W = [RWKV_DIM, RWKV_DIM, RWKV_DIM, LANE, LANE, 2 * LANE]
SEG_OFF = [0, 512, 1024, XW_OFF, XA_OFF, XG_OFF]


def _rwkv_pre_bwd(proj, u, grads, mu, small, dproj):
    t_len = u.shape[0]
    tr = min(ROW_BLOCK, t_len)
    nb = t_len // tr
    sub = 8
    n_g = len(grads)
    acc_shapes = [(1, RW_PAD)] + [(1, RWKV_DIM)] * 4 + [(LANE, RWKV_DIM), (LANE, RWKV_DIM), (2 * LANE, RWKV_DIM)]

    def body(*refs):
        seg_refs, halo_refs = refs[:6], refs[6:12]
        k_ref, xw_ref, xa_ref, xg_ref = refs[12:16]
        g_refs = refs[16:16 + n_g]
        mu_ref = refs[16 + n_g]
        prm_refs = refs[17 + n_g:24 + n_g]
        out_hbm = refs[25 + n_g]
        acc_refs = refs[26 + n_g:26 + n_g + len(acc_shapes)]
        vbuf, sems, carry = refs[26 + n_g + len(acc_shapes):]
        i = pl.program_id(0)
        blk = nb - 1 - i
        dr1, dr2, dv1, dv2, dlw, dk1, dk2, da, db, dg = [g[...] for g in g_refs]
        _, vjp = jax.vjp(_rwkv_pre, k_ref[...], xw_ref[...], xa_ref[...], xg_ref[...], *[p_[...] for p_ in prm_refs])
        dk, dxw, dxa, dxg, *dprm = vjp((dlw, dk1 + dk2, da, db, dg))
        du = jnp.concatenate([dr1 + dr2, dk, dv1 + dv2, dxw, dxa, dxg], axis=1)
        mu_v = mu_ref[...]

        @pl.when(i == 0)
        def _():
            carry[...] = jnp.zeros_like(carry)

        rows = lax.broadcasted_iota(jnp.int32, du.shape, 0)
        nxt = jnp.where(rows == tr - 1, carry[...], pltpu.roll(du, tr - 1, 0))
        d_rw = du - mu_v * du + mu_v * nxt
        d_mu = []
        for s_ref, h_ref, off, wd in zip(seg_refs, halo_refs, SEG_OFF, SEG_W):
            cur = s_ref[...]
            r0 = lax.broadcasted_iota(jnp.int32, cur.shape, 0)
            prev = jnp.where(r0 == 0, jnp.where(blk == 0, 0.0, h_ref[sub - 1:sub, :]), pltpu.roll(cur, 1, 0))
            d_mu.append(jnp.sum(du[:, off:off + wd] * (prev - cur), axis=0, keepdims=True))
        sums = [jnp.concatenate(d_mu, axis=1)] + list(dprm)

        @pl.when(i == 0)
        def _():
            for a_ref, val in zip(acc_refs, sums):
                a_ref[...] = val

        @pl.when(i > 0)
        def _():
            for a_ref, val in zip(acc_refs, sums):
                a_ref[...] += val

        carry[...] = du[0:1, :]
        slot = i % 2

        def writeback(s, b):
            return pltpu.make_async_copy(vbuf.at[s], out_hbm.at[pl.ds(b * tr, tr), pl.ds(CONV_COLS, RW_PAD)], sems.at[s])

        @pl.when(i >= 2)
        def _():
            writeback(slot, blk + 2).wait()

        vbuf[slot] = d_rw.astype(vbuf.dtype)
        writeback(slot, blk).start()

        @pl.when(i == nb - 1)
        def _():
            writeback(slot, blk).wait()
            if nb > 1:
                writeback(1 - slot, blk + 1).wait()

    rev = lambda w_, cb: pl.BlockSpec((tr, w_), functools.partial(lambda i, c: (nb - 1 - i, c), c=cb))
    halo = lambda w_, cb: pl.BlockSpec((sub, w_), functools.partial(
        lambda i, c: (jnp.maximum((nb - 1 - i) * (tr // sub) - 1, 0), c), c=cb))
    whole = lambda a: pl.BlockSpec(a.shape, functools.partial(lambda i, n: (0,) * n, n=a.ndim))
    segs = [(wd, (CONV_COLS + off) // wd) for off, wd in zip(SEG_OFF, SEG_W)]
    u_cols = [(512, 1), (LANE, XW_OFF // LANE), (LANE, XA_OFF // LANE), (2 * LANE, XG_OFF // (2 * LANE))]
    any_spec = pl.BlockSpec(memory_space=pl.ANY)
    res = pl.pallas_call(
        body, name="rwkv_pre_bwd", grid=(nb,),
        in_specs=[rev(*s) for s in segs] + [halo(*s) for s in segs] + [rev(*c) for c in u_cols]
                 + [rev(RWKV_DIM, 0)] * n_g + [whole(mu)] + [whole(p_) for p_ in small] + [any_spec],
        out_specs=[any_spec] + [pl.BlockSpec(s, functools.partial(lambda i, n: (0,) * n, n=len(s))) for s in acc_shapes],
        out_shape=[jax.ShapeDtypeStruct(dproj.shape, dproj.dtype)] + [jax.ShapeDtypeStruct(s, F32) for s in acc_shapes],
        scratch_shapes=[pltpu.VMEM((2, tr, RW_PAD), dproj.dtype), pltpu.SemaphoreType.DMA((2,)), pltpu.VMEM((1, RW_PAD), F32)],
        input_output_aliases={24 + n_g: 0},
        compiler_params=_params(("arbitrary",)),
    )(*[proj] * 12, *[u] * 4, *grads, mu, *small, dproj)
    return res


def _local_step(x, p, tgt, w, early_shards, late_shards):
    row = lambda v: v.reshape(1, -1)
    w = dict(w)

    xn1, *gathered = _rowwise("rms_mix", lambda h, g: (_rms(h, g),), [x], [w["norm_mix_g"]], [(D_MODEL, BF16)],
                              gather=early_shards)
    w.update({n: _unshard(n, g_) for n, g_ in zip(_EARLY[1:], gathered[1:])})
    w["w_in"] = _assemble_w_in(gathered[0])
    w["w_lora_up"] = _pad_rows(w["w_lora_up"], LANE)
    w["a_lora_up"] = _pad_rows(w["a_lora_up"], LANE)
    w["g_lora_up"] = _pad_rows(w["g_lora_up"], 2 * LANE)
    proj = _matmul("in_proj", xn1, w["w_in"], "nn", [F32], tm=2048, tn=512, tk=D_MODEL)
    n_cb = CONV_DIM // LANE

    def conv_fwd(blk, cw):
        gb, gc, hx = blk[:, :LANE], blk[:, LANE:2 * LANE], blk[:, 2 * LANE:]
        uu = gc * hx
        return (gb * (uu * cw[2:3] + _shift_down(uu, 1) * cw[1:2] + _shift_down(uu, 2) * cw[0:1]),)

    (y_conv,) = _colwise("conv_fwd", conv_fwd, n_cb, [(proj, 3 * LANE)], [w["conv_w"]], [(CONV_DIM, BF16, LANE)])

    small = [w["w0"], w["a0"], w["k_k"], w["k_a"], w["w_lora_up"], w["a_lora_up"], w["g_lora_up"]]
    def pre_fwd(*xs):
        cur, prev_rows, mu, prm = xs[:6], xs[6:12], xs[12], xs[13:]
        segs = []
        for c_, p_, off, wd in zip(cur, prev_rows, SEG_OFF, SEG_W):
            rows = lax.broadcasted_iota(jnp.int32, c_.shape, 0)
            prev = jnp.where(rows == 0, p_, pltpu.roll(c_, 1, 0))
            segs.append(c_ + mu[:, off:off + wd] * (prev - c_))
        return (jnp.concatenate(segs, axis=1),) + tuple(_rwkv_pre(segs[1], segs[3], segs[4], segs[5], *prm))

    proj_segs = [(proj, wd, (CONV_COLS + off) // wd) for off, wd in zip(SEG_OFF, SEG_W)]
    u, lw, k_h, ra, rb, g = _rowwise("rwkv_pre", pre_fwd, proj_segs, [w["shift_mu"]] + small,
                                     [(RW_PAD, F32)] + [(RWKV_DIM, F32)] * 5, halo=True)
    u_k, u_xw, u_xa, u_xg = (u, 512, 1), (u, LANE, XW_OFF // LANE), (u, LANE, XA_OFF // LANE), (u, 2 * LANE, XG_OFF // (2 * LANE))
    y_rec, zs, late = _rec_fwd(u, lw, k_h, ra, rb, late_shards)
    for n, gathered in zip(_LATE, late):
        w[n] = _unshard(n, gathered)
    post_c = [w["ln_x_g"], w["ln_x_b"], w["r_k"]]
    u_r, u_v = (u, 512, 0), (u, 512, 2)
    (y_rwkv,) = _rowwise("rwkv_post", lambda *xs: (_rwkv_post(*xs),), [y_rec, u_r, k_h, u_v, g], post_c, [(RWKV_DIM, BF16)])
    ycat = jnp.concatenate([y_conv, y_rwkv], axis=1)
    def res_norm(acc, r_, g_):
        h = acc + r_
        return h, _rms(h, g_)

    h1, xn2 = _matmul("out_proj", ycat, w["w_out"], "nn", [F32, BF16], tm=1024, tn=D_MODEL, tk=D_MODEL, extras=[x],
                      consts=[w["norm_mlp_g"]], epilogue=res_norm)

    square = lambda h: h.astype(F32) * h.astype(F32)
    hid = _matmul("mlp_up", xn2, w["w_up"], "nn", [BF16], tm=2048, tn=1024, tk=D_MODEL,
                  epilogue=lambda acc: (jnp.maximum(acc, 0.0),))
    h2, xn3 = _matmul("mlp_down", hid, w["w_down"], "nn", [F32, BF16], tm=512, tn=D_MODEL, tk=D_FF, extras=[h1],
                      consts=[w["norm_ple_g"]], epilogue=res_norm, a_map=square)
    zg = _matmul("ple_gate", xn3, w["w_ple_gate"], "nn", [F32], tm=1024, tn=1024, tk=D_MODEL)
    pp = _matmul("ple_proj", p, w["w_ple_proj"], "nn", [F32], tm=1024, tn=1024, tk=PLE_DIM)

    def head(h2_, zg_, pp_, tg, gf):
        gate = _sigmoid(zg_)
        h3 = h2_ + gate * pp_
        out = _rms(h3, gf)
        err = out - tg
        dh3, dgf = _rms_bwd(h3, gf, err * (1.0 / D_MODEL))
        loss = jnp.sum(jnp.sum(err * err, axis=1, keepdims=True), axis=0, keepdims=True) * (0.5 / D_MODEL)
        return dh3, dh3 * pp_ * gate * (1.0 - gate), dh3 * gate, dgf, loss

    dh3, dzg, dpp, d_norm_final, loss = _rowwise(
        "head", head, [h2, zg, pp, tgt], [row(w["norm_final_g"])], [(D_MODEL, F32), (D_MODEL, BF16), (D_MODEL, BF16)],
        [(1, D_MODEL), (1, 1)])

    d_w_ple_proj = _matmul("d_ple_proj", p, dpp, "tn", [BF16], tm=PLE_DIM, tn=D_MODEL // N_DEV, tk=4096, col_blocks_out=True)
    d_w_ple_gate = _matmul("d_ple_gate", xn3, dzg, "tn", [BF16], tm=512, tn=1024, tk=4096)

    def norm_bwd(dxn, h, dres, g_):
        dh, dg = _rms_bwd(h, g_, dxn)
        dh = dh + dres
        return dh, dh, dg

    nb = dict(tm=512, tn=D_MODEL, epilogue=norm_bwd, sums=[(1, D_MODEL)])
    dh2, dh2_b, d_norm_ple = _matmul("dx_ple_gate", dzg, w["w_ple_gate"], "nt", [F32, BF16], tk=D_MODEL,
                                     extras=[h2, dh3], consts=[w["norm_ple_g"]], **nb)
    d_w_down = _matmul("d_mlp_down", hid, dh2_b, "tn", [BF16], tm=512, tn=1024, tk=4096, a_map=square)
    dpre = _matmul("dx_mlp_down", dh2_b, w["w_down"], "nt", [BF16], tm=2048, tn=1024, tk=D_MODEL, extras=[hid],
                   epilogue=lambda acc, hid_: (acc * (2.0 * hid_.astype(F32)),))
    d_w_up = _matmul("d_mlp_up", xn2, dpre, "tn", [BF16], tm=1024, tn=D_FF // N_DEV, tk=4096, col_blocks_out=True)
    dh1, dh1_b, d_norm_mlp = _matmul("dx_mlp_up", dpre, w["w_up"], "nt", [F32, BF16], tk=D_FF,
                                     extras=[h1, dh2], consts=[w["norm_mlp_g"]], **nb)
    d_w_out = _matmul("d_out_proj", ycat, dh1_b, "tn", [BF16], tm=512, tn=1024, tk=4096)
    dycat = _matmul("dx_out_proj", dh1_b, w["w_out"], "nt", [F32], tm=1024, tn=1024, tk=D_MODEL)
    late_grads = dict(w_out=d_w_out, w_up=d_w_up, w_down=d_w_down, w_ple_gate=d_w_ple_gate, w_ple_proj=d_w_ple_proj)

    def conv_bwd(dy, blk, cw):
        gb, gc, hx = blk[:, :LANE], blk[:, LANE:2 * LANE], blk[:, 2 * LANE:]
        uu = gc * hx
        u1, u2 = _shift_down(uu, 1), _shift_down(uu, 2)
        dconv = dy * gb
        du = dconv * cw[2:3] + _shift_up(dconv, 1) * cw[1:2] + _shift_up(dconv, 2) * cw[0:1]
        s = lambda z: jnp.sum(z, axis=0, keepdims=True)
        d_blk = jnp.concatenate([dy * (uu * cw[2:3] + u1 * cw[1:2] + u2 * cw[0:1]), du * hx, du * gc], axis=1)
        return d_blk, s(dconv * u2), s(dconv * u1), s(dconv * uu)

    dproj, dcw0, dcw1, dcw2 = _colwise(
        "conv_bwd", conv_bwd, n_cb, [(dycat, LANE), (proj, 3 * LANE)], [w["conv_w"]],
        [(IN_PAD, BF16, 3 * LANE)], [(1, CONV_DIM)] * 3)

    def post_bwd(dy, y, r, k_h_, v, g_, ln_g, ln_b, r_k):
        _, vjp = jax.vjp(_rwkv_post, y, r, k_h_, v, g_, ln_g, ln_b, r_k)
        return vjp(dy)

    dy_rec, dr_p, dk_p, dv_p, dg, d_ln_g, d_ln_b, d_r_k = _rowwise(
        "rwkv_post_bwd", post_bwd, [(dycat, 512, 1), y_rec, u_r, k_h, u_v, g], post_c,
        [(RWKV_DIM, F32)] * 5, [(1, RWKV_DIM)] * 3)
    (dr_r, dv_r, dlw, dk_r, da, db), late_parts = _rec_bwd(
        u, lw, k_h, ra, rb, zs, dy_rec,
        [late_grads[n] if n in ("w_up", "w_ple_proj") else _reshard(n, late_grads[n]) for n in _LATE], [True] * len(_LATE))

    dproj, d_mu, d_w0, d_a0, d_k_k, d_k_a, d_wl, d_al, d_gl = _rwkv_pre_bwd(
        proj, u, [dr_p, dr_r, dv_p, dv_r, dlw, dk_p, dk_r, da, db, dg], w["shift_mu"], small, dproj)
    d_w_in = _matmul("d_in_proj", xn1, dproj, "tn", [BF16], tm=1024, tn=896, tk=4096)
    early_grads = dict(conv_w=jnp.concatenate([dcw0, dcw1, dcw2], axis=0),
                       w_lora_up=d_wl[:64], a_lora_up=d_al[:64], g_lora_up=d_gl[:160])
    early_send = [_split_w_in_grad(d_w_in)] + [_reshard(n, early_grads[n]) for n in _EARLY[1:]]
    s_sems, r_sems, sent, landing, token = _scatter_start("early_scatter_start", early_send,
                                                          [jnp.zeros(a.shape, a.dtype) for a in early_send])
    dx, d_norm_mix = _matmul(
        "dx_in_proj", dproj, w["w_in"], "nt", [F32], tk=IN_PAD, extras=[x, dh1], consts=[w["norm_mix_g"] + token[0:1, 0:1]],
        **dict(nb, epilogue=lambda *a: norm_bwd(*a)[1:]))

    grads = dict(
        norm_mix_g=d_norm_mix, shift_mu=d_mu, w0=d_w0, a0=d_a0, k_k=d_k_k, k_a=d_k_a, r_k=d_r_k,
        ln_x_g=d_ln_g, ln_x_b=d_ln_b, norm_mlp_g=d_norm_mlp, norm_ple_g=d_norm_ple, norm_final_g=d_norm_final)
    return loss, dx, grads, dict(zip(_LATE, late_parts)), (s_sems, r_sems, sent, landing)


def _adam_update(partials, w_ref, m_ref, v_ref, g_ref, d_ref, nm_ref, nv_ref):
    g = partials[0].astype(F32)
    for part in partials[1:]:
        g = g + part.astype(F32)
    nm =ADAM_B1 * m_ref[...] + (1.0 - ADAM_B1) * g
    nv = ADAM_B2 * v_ref[...] + (1.0 - ADAM_B2) * (g * g)
    m_hat = nm / (1.0 - ADAM_B1 ** ADAM_STEP)
    v_hat = nv / (1.0 - ADAM_B2 ** ADAM_STEP)
    g_ref[...] = g
    d_ref[...] = -ADAM_LR * (m_hat / (jnp.sqrt(v_hat) + ADAM_EPS) + ADAM_WD * w_ref[...])
    nm_ref[...] = nm
    nv_ref[...] = nv


SMALL_ROWS = 8


def _small_layout(widths):
    widths = list(widths) + [1]
    fill, place = [0] * SMALL_ROWS, [None] * len(widths)
    for j in sorted(range(len(widths)), key=lambda q: -widths[q]):
        row = fill.index(min(fill))
        place[j] = (row, fill[row])
        fill[row] += -(-widths[j] // LANE) * LANE
    return place, max(fill)


def _pack_small(vecs, loss):
    place, total = _small_layout([v_.shape[1] for v_ in vecs])
    n = len(vecs)

    def body(*refs):
        out = jnp.zeros((SMALL_ROWS, total), F32)
        row_id = lax.broadcasted_iota(jnp.int32, (SMALL_ROWS, total), 0)
        for row in range(SMALL_ROWS):
            mine = sorted((off, j) for j, (r_, off) in enumerate(place) if r_ == row)
            pieces, at = [], 0
            for off, j in mine:
                val = refs[j][...]
                pieces.append(val)
                at = off + val.shape[1]
                pad = -val.shape[1] % LANE
                if pad:
                    pieces.append(jnp.zeros((1, pad), F32))
                    at += pad
            if total > at:
                pieces.append(jnp.zeros((1, total - at), F32))
            out = jnp.where(row_id == row, jnp.broadcast_to(jnp.concatenate(pieces, axis=1), (SMALL_ROWS, total)), out)
        refs[n + 1][...] = out

    return pl.pallas_call(body, name="pack_small", out_shape=jax.ShapeDtypeStruct((SMALL_ROWS, total), F32))(*vecs, loss)


def _adamw_small(packed, ws, ms, vs):
    n = len(ws)
    place, _ = _small_layout([w_.shape[1] for w_ in ws])

    def body(p_ref, *refs):
        w_refs, m_refs, v_refs, outs = refs[:n], refs[n:2 * n], refs[2 * n:3 * n], refs[3 * n:]
        for j in range(n):
            row, off = place[j]
            cols = pl.ds(off, ws[j].shape[1])
            _adam_update([p_ref[s, row:row + 1, cols] for s in range(N_DEV)], w_refs[j], m_refs[j], v_refs[j],
                         *outs[4 * j:4 * j + 4])
        row, off = place[n]
        total = p_ref[0, row:row + 1, off:off + 1]
        for s in range(1, N_DEV):
            total = total + p_ref[s, row:row + 1, off:off + 1]
        outs[4 * n][...] = total

    res = pl.pallas_call(
        body, name="adamw_small",
        out_shape=[jax.ShapeDtypeStruct(w_.shape, F32) for w_ in ws for _ in range(4)] + [jax.ShapeDtypeStruct((1, 1), F32)],
    )(packed, *ws, *ms, *vs)
    return [res[4 * j:4 * j + 4] for j in range(n)], res[4 * n]


def _adamw(name, parts, w, m, v, own=None, me=None):
    rows, cols = w.shape[-2:]
    lead = w.ndim - 2
    tr = rows if rows * cols * 4 * 8 <= (4 << 20) else max(8, (4 << 20) // (cols * 4 * 8) // 8 * 8)
    while rows % tr:
        tr -= 8
    shape4 = [jax.ShapeDtypeStruct(w.shape, F32)] * 4
    if own is None:
        def body(p_ref, *refs):
            _adam_update([p_ref[s] for s in range(N_DEV)], *refs)

        blk = pl.BlockSpec((None,) * lead + (tr, cols), lambda i: (0,) * lead + (i, 0))
        return pl.pallas_call(
            body, name=name, grid=(rows // tr,),
            in_specs=[pl.BlockSpec((N_DEV, tr, cols), lambda i: (0, i, 0)), blk, blk, blk], out_specs=[blk] * 4,
            out_shape=shape4, compiler_params=_params(("arbitrary",)),
        )(parts, w, m, v)

    def body_own(me_ref, p_ref, own_ref, *refs):
        mine = own_ref[...]
        _adam_update([jnp.where(me_ref[0] == s, mine, p_ref[s]) for s in range(N_DEV)], *refs)

    blk = pl.BlockSpec((None,) * lead + (tr, cols), lambda i, me_ref: (0,) * lead + (i, 0))
    return pl.pallas_call(
        body_own, name=name, out_shape=shape4,
        grid_spec=pltpu.PrefetchScalarGridSpec(
            num_scalar_prefetch=1, grid=(rows // tr,),
            in_specs=[pl.BlockSpec((N_DEV, tr, cols), lambda i, me_ref: (0, i, 0)),
                      pl.BlockSpec((None, tr, cols), lambda i, me_ref: (me_ref[0], i, 0)), blk, blk, blk],
            out_specs=[blk] * 4),
        compiler_params=_params(("arbitrary",)),
    )(me, parts, own, w, m, v)


def kernel(x, p, norm_mix_g, w_in, conv_w, shift_mu, w_lora_up, w0, a_lora_up, a0, g_lora_up, k_k, k_a, r_k, ln_x_g, ln_x_b, w_out, norm_mlp_g, w_up, w_down, norm_ple_g, w_ple_gate, w_ple_proj, norm_final_g, loss_target, m_norm_mix_g, m_w_in, m_conv_w, m_shift_mu, m_w_lora_up, m_w0, m_a_lora_up, m_a0, m_g_lora_up, m_k_k, m_k_a, m_r_k, m_ln_x_g, m_ln_x_b, m_w_out, m_norm_mlp_g, m_w_up, m_w_down, m_norm_ple_g, m_w_ple_gate, m_w_ple_proj, m_norm_final_g, v_norm_mix_g, v_w_in, v_conv_w, v_shift_mu, v_w_lora_up, v_w0, v_a_lora_up, v_a0, v_g_lora_up, v_k_k, v_k_a, v_r_k, v_ln_x_g, v_ln_x_b, v_w_out, v_norm_mlp_g, v_w_up, v_w_down, v_norm_ple_g, v_w_ple_gate, v_w_ple_proj, v_norm_final_g):
    args = dict(locals())
    wts = {n: args[n] for n in _WEIGHTS}
    mom = {n: args["m_" + n] for n in _WEIGHTS}
    var = {n: args["v_" + n] for n in _WEIGHTS}
    shard2d = lambda a: a.reshape(a.shape[-2:])
    pad_mu = lambda a: _pad_in_cols(jnp.concatenate([jnp.zeros((1, CONV_COLS), F32), a], axis=1))[:, CONV_COLS:]
    unpad_mu = lambda a: _unpad_in_cols(jnp.concatenate([jnp.zeros((1, CONV_COLS), F32), a], axis=1))[:, CONV_COLS:]

    shards = {n: shard2d(wts[n]).astype(BF16 if n in _BF16_GATHER else F32) for n in _SHARDED}
    w = {n: wts[n].reshape(1, -1) for n in _REPLICATED}
    w["shift_mu"] = pad_mu(wts["shift_mu"])

    loss, dx, grads, parts, in_flight = _local_step(x[0], p[0, 0], loss_target[0], w, [shards[n] for n in _EARLY],
                                                    [shards[n] for n in _LATE])

    out = {n: _adamw("adamw_" + n, parts[n], wts[n], mom[n], var[n]) for n in _LATE}
    sent, early_parts = _scatter_wait("early_scatter_wait", *in_flight, after=[dx] + [out[n][1] for n in _LATE])
    me = (4 * lax.axis_index("x") + 2 * lax.axis_index("y") + lax.axis_index("c")).astype(jnp.int32).reshape(1)
    for n, prt, own in zip(_EARLY, early_parts, sent):
        out[n] = _adamw("adamw_" + n, prt, wts[n], mom[n], var[n], own=own, me=me)

    grads["shift_mu"] = unpad_mu(grads["shift_mu"])
    flat = lambda a: a.reshape(1, -1)
    (small_parts,) = _exchange("gather_small", [_pack_small([flat(grads[n]) for n in _REPLICATED], loss)], [False])
    small, loss_total = _adamw_small(small_parts, *[[flat(d[n]) for n in _REPLICATED] for d in (wts, mom, var)])
    for n, res in zip(_REPLICATED, small):
        out[n] = [r.reshape(wts[n].shape) for r in res]
    return (loss_total[0, 0], dx[None], *[out[n][0] for n in _WEIGHTS], *[out[n][1] for n in _WEIGHTS],
            *[out[n][2] for n in _WEIGHTS], *[out[n][3] for n in _WEIGHTS])
```

```python
import functools

import jax
import jax.numpy as jnp
from jax import lax
from jax.experimental import pallas as pl
from jax.experimental.pallas import tpu as pltpu

F32 = jnp.float32
BF16 = jnp.bfloat16

N_DEV = 8
D_MODEL = 1024
CONV_DIM = 512
RWKV_DIM = 512
HEAD_DIM = 64
N_HEADS = 8
D_FF = 4096
PLE_DIM = 256
RMS_EPS = 1e-6
GN_EPS = 64e-5
L2_EPS = 1e-12
ADAM_LR, ADAM_B1, ADAM_B2, ADAM_EPS, ADAM_WD, ADAM_STEP = 0.001, 0.9, 0.999, 1e-08, 0.01, 10

CONV_COLS = 3 * CONV_DIM
RW_PAD = 2048
IN_PAD = CONV_COLS + RW_PAD
IN_COLS = 3360
XW_OFF, XA_OFF, XG_OFF = 1536, 1664, 1792
REC_CHUNK = 128
REC_PASSES = 1
ROW_BLOCK = 256
LANE = 128
VMEM_LIMIT = 56 * 1024 * 1024


def _dims(dn, ndim):
    if ndim == 3:
        return {"nn": (((2,), (1,)), ((0,), (0,))), "nt": (((2,), (2,)), ((0,), (0,))),
                "tn": (((1,), (1,)), ((0,), (0,)))}[dn]
    return {"nn": (((1,), (0,)), ((), ())), "nt": (((1,), (1,)), ((), ())), "tn": (((0,), (0,)), ((), ()))}[dn]


def _split2(x):
    hi = x.astype(BF16)
    return hi, (x - hi.astype(F32)).astype(BF16)


def _mm_raw(x, y, dn, passes):
    f = lambda p, q: lax.dot_general(p, q, _dims(dn, x.ndim), preferred_element_type=F32)
    if passes == 1:
        return f(x.astype(BF16), y.astype(BF16))
    xh, xl = _split2(x)
    yh, yl = _split2(y)
    if passes == 2:
        return f(xh, yh) + f(xh, yl)
    return f(xh, yh) + f(xh, yl) + f(xl, yh)


@functools.partial(jax.custom_vjp, nondiff_argnums=(2, 3))
def _mm(x, y, dn, passes):
    return _mm_raw(x, y, dn, passes)


def _mm_fwd(x, y, dn, passes):
    return _mm_raw(x, y, dn, passes), (x, y)


def _mm_bwd(dn, passes, res, d):
    x, y = res
    if dn == "nn":
        return _mm(d, y, "nt", passes), _mm(x, d, "tn", passes)
    if dn == "nt":
        return _mm(d, y, "nn", passes), _mm(d, x, "tn", passes)
    return _mm(y, d, "nt", passes), _mm(x, d, "nn", passes)


_mm.defvjp(_mm_fwd, _mm_bwd)


def _head_ones():
    i = lax.broadcasted_iota(jnp.int32, (RWKV_DIM, RWKV_DIM), 0) // HEAD_DIM
    j = lax.broadcasted_iota(jnp.int32, (RWKV_DIM, RWKV_DIM), 1) // HEAD_DIM
    return (i == j).astype(BF16)


def _hsum_raw(x):
    ones = _head_ones()
    f = lambda p: lax.dot_general(p, ones, _dims("nn", 2), preferred_element_type=F32)
    x1, x2 = _split2(x)
    return f(x1) + f(x2)


@jax.custom_vjp
def _hsum(x):
    return _hsum_raw(x)


_hsum.defvjp(lambda x: (_hsum_raw(x), None), lambda _, d: (_hsum(d),))


def _sigmoid(x):
    return 0.5 + 0.5 * jnp.tanh(0.5 * x)


def _softplus(x):
    return jnp.maximum(x, 0.0) + jnp.log(1.0 + jnp.exp(-jnp.abs(x)))


def _params(sem):
    return pltpu.CompilerParams(dimension_semantics=sem, vmem_limit_bytes=VMEM_LIMIT)


def _rowwise(name, fn, rows, consts, row_outs, acc_outs=(), tr=ROW_BLOCK, halo=False, gather=()):
    rows = [r if isinstance(r, tuple) else (r, r.shape[1], 0) for r in rows]
    t_len = rows[0][0].shape[0]
    tr = min(tr, t_len)
    n_r, n_c, n_o, n_a, n_x = len(rows), len(consts), len(row_outs), len(acc_outs), len(gather)
    n_h = n_r if halo else 0
    sub = 8
    x_specs, x_shapes, x_sems = _exchange_io(gather, [False] * n_x) if n_x else ([], [], [])
    nb = t_len // tr

    def body(*refs):
        if n_x:
            n_in = n_r + n_h + n_c
            start, forward, wait = _gather_plan(refs[n_in:n_in + n_x], refs[len(refs) - 3 - n_x:len(refs) - 3], *refs[len(refs) - 3:])
            pl.when(pl.program_id(0) == 0)(start)
            refs = refs[:n_in] + refs[n_in + n_x:len(refs) - 3 - n_x]
        ins = [r[...] for r in refs[:n_r]]
        ins += [jnp.where(pl.program_id(0) == 0, 0.0, r[sub - 1:sub, :]) for r in refs[n_r:n_r + n_h]]
        ins += [r[...] for r in refs[n_r + n_h:n_r + n_h + n_c]]
        refs = refs[:n_r] + refs[n_r + n_h:]
        outs = fn(*ins)
        o_refs = refs[n_r + n_c:n_r + n_c + n_o]
        a_refs = refs[n_r + n_c + n_o:]
        for o_ref, val in zip(o_refs, outs[:n_o]):
            o_ref[...] = val.astype(o_ref.dtype)
        if n_a:
            first = pl.program_id(0) == 0

            @pl.when(first)
            def _():
                for a_ref, val in zip(a_refs, outs[n_o:]):
                    a_ref[...] = val

            @pl.when(jnp.logical_not(first))
            def _():
                for a_ref, val in zip(a_refs, outs[n_o:]):
                    a_ref[...] += val

        if n_x:
            @pl.when(pl.program_id(0) == nb - 1)
            def _():
                for j in range(n_x):
                    forward(j)
                wait()

    in_specs = [pl.BlockSpec((tr, w), functools.partial(lambda i, c: (i, c), c=cb)) for _, w, cb in rows]
    if halo:
        in_specs += [pl.BlockSpec((sub, w), functools.partial(lambda i, c: (jnp.maximum(i * (tr // sub) - 1, 0), c), c=cb))
                     for _, w, cb in rows]
    in_specs += [pl.BlockSpec(c.shape, functools.partial(lambda i, n: (0,) * n, n=c.ndim)) for c in consts]
    out_specs = [pl.BlockSpec((tr, w), lambda i: (i, 0)) for w, _ in row_outs]
    out_specs += [pl.BlockSpec(s, functools.partial(lambda i, n: (0,) * n, n=len(s))) for s in acc_outs]
    out_shape = [jax.ShapeDtypeStruct((t_len, w), dt) for w, dt in row_outs]
    out_shape += [jax.ShapeDtypeStruct(s, F32) for s in acc_outs]
    return pl.pallas_call(
        body, name=name, grid=(nb,), in_specs=in_specs + x_specs, out_specs=out_specs + x_specs,
        out_shape=out_shape + x_shapes, scratch_shapes=x_sems,
        compiler_params=pltpu.CompilerParams(dimension_semantics=("arbitrary",), vmem_limit_bytes=VMEM_LIMIT,
                                             has_side_effects=bool(n_x)),
    )(*[r[0] for r in rows], *([r[0] for r in rows] if halo else []), *consts, *gather)


def _colwise(name, fn, n_blocks, cols, prms, col_outs, prm_outs=()):
    t_len = cols[0][0].shape[0]
    n_i = len(cols) + len(prms)

    def body(*refs):
        outs = fn(*[r[...] for r in refs[:n_i]])
        for o_ref, val in zip(refs[n_i:], outs):
            o_ref[...] = val.astype(o_ref.dtype)

    spec = lambda r, w: pl.BlockSpec((r, w), lambda j: (0, j))
    in_specs = [spec(t_len, w) for _, w in cols] + [spec(a.shape[0], LANE) for a in prms]
    out_specs = [spec(t_len, bw) for _, _, bw in col_outs] + [spec(r, LANE) for r, _ in prm_outs]
    out_shape = [jax.ShapeDtypeStruct((t_len, w), dt) for w, dt, _ in col_outs]
    out_shape += [jax.ShapeDtypeStruct((r, w), F32) for r, w in prm_outs]
    return pl.pallas_call(
        body, name=name, grid=(n_blocks,), in_specs=in_specs, out_specs=out_specs, out_shape=out_shape,
        compiler_params=_params(("arbitrary",)),
    )(*[c[0] for c in cols], *prms)


def _matmul(name, a, b, dn, outs, *, tm, tn, tk, extras=(), consts=(), epilogue=None, sums=(), xch=(), xch_scatter=(),
            a_map=None, col_blocks_out=False):
    if dn == "nn":
        (m, k), n = a.shape, b.shape[1]
    elif dn == "nt":
        (m, k), n = a.shape, b.shape[0]
    else:
        (k, m), n = a.shape, b.shape[1]
    tm, tn, tk = min(tm, m), min(tn, n), min(tk, k)
    nk = k // tk
    grid = (m // tm, n // tn, nk)
    assert nk == 1 and (not sums or grid[1] == 1)
    a_spec = pl.BlockSpec((tk, tm), lambda i, j, q: (q, i)) if dn == "tn" else pl.BlockSpec((tm, tk), lambda i, j, q: (i, q))
    b_spec = pl.BlockSpec((tn, tk), lambda i, j, q: (j, q)) if dn == "nt" else pl.BlockSpec((tk, tn), lambda i, j, q: (q, j))
    o_spec = pl.BlockSpec((tm, tn), lambda i, j, q: (i, j))
    c_spec = pl.BlockSpec((1, tn), lambda i, j, q: (0, j))
    n_e, n_c, n_o, n_s, n_x = len(extras), len(consts), len(outs), len(sums), len(xch)
    x_specs, x_shapes, x_sems = _exchange_io(xch, xch_scatter) if n_x else ([], [], [])

    def body(*refs):
        a_ref, b_ref = refs[:2]
        e_refs = refs[2:2 + n_e + n_c]
        x_in = refs[2 + n_e + n_c:2 + n_e + n_c + n_x]
        rest = refs[2 + n_e + n_c + n_x:]
        o_refs, s_refs, x_out, scratch = rest[:n_o], rest[n_o:n_o + n_s], rest[n_o + n_s:n_o + n_s + n_x], rest[n_o + n_s + n_x:]
        step = (pl.program_id(0) * grid[1] + pl.program_id(1)) * nk + pl.program_id(2)
        if n_x:
            start, wait = _exchange_plan(x_in, x_out, xch_scatter, *scratch[len(scratch) - 3:])
            pl.when(step == 0)(start)
        a_blk = a_ref[...] if a_map is None else a_map(a_ref[...])
        acc = lax.dot_general(a_blk.astype(BF16), b_ref[...].astype(BF16), _dims(dn, 2), preferred_element_type=F32)
        vals = (acc,) if epilogue is None else epilogue(acc, *[e[...] for e in e_refs])
        for o_ref, val in zip(o_refs, vals[:n_o]):
            o_ref[...] = val.astype(o_ref.dtype)
        if n_s:
            @pl.when(step == 0)
            def _():
                for s_ref, val in zip(s_refs, vals[n_o:]):
                    s_ref[...] = val

            @pl.when(step > 0)
            def _():
                for s_ref, val in zip(s_refs, vals[n_o:]):
                    s_ref[...] += val

        if n_x:
            pl.when(step == grid[0] * grid[1] * nk - 1)(wait)

    plain = not (n_s or n_x)
    res = pl.pallas_call(
        body, name=name, grid=grid,
        in_specs=[a_spec, b_spec] + [o_spec] * n_e + [c_spec] * n_c + x_specs,
        out_specs=[pl.BlockSpec((None, tm, tn), lambda i, j, q: (j, i, 0)) if col_blocks_out else o_spec] * n_o
                  + [c_spec] * n_s + x_specs,
        out_shape=[jax.ShapeDtypeStruct((n // tn, m, tn) if col_blocks_out else (m, n), dt) for dt in outs] + [jax.ShapeDtypeStruct(s, F32) for s in sums] + x_shapes,
        scratch_shapes=x_sems,
        compiler_params=pltpu.CompilerParams(
            dimension_semantics=("parallel", "parallel", "arbitrary") if plain else ("arbitrary",) * 3,
            vmem_limit_bytes=VMEM_LIMIT, has_side_effects=bool(n_x)),
    )(a, b, *extras, *consts, *xch)
    return res[0] if len(res) == 1 else res


def _rms(h, g):
    return h * lax.rsqrt(jnp.mean(h * h, axis=-1, keepdims=True) + RMS_EPS) * g


def _rms_bwd(h, g, dy):
    rs = lax.rsqrt(jnp.mean(h * h, axis=-1, keepdims=True) + RMS_EPS)
    n = h * rs
    dn = dy * g
    dh = rs * (dn - n * jnp.mean(dn * n, axis=-1, keepdims=True))
    return dh, jnp.sum(dy * n, axis=0, keepdims=True)


def _rwkv_pre(k, xw, xa, xg, w0, a0, k_k, k_a, wl, al, gl):
    zw = w0 + _mm(jnp.tanh(xw), wl, "nn", 1)
    lw = -jnp.exp(-_softplus(-zw) - 0.5)
    iclr = _sigmoid(a0 + _mm(xa, al, "nn", 1))
    g = _mm(_sigmoid(xg), gl, "nn", 1)
    kk0 = k * k_k
    kk = kk0 * lax.rsqrt(jnp.maximum(_hsum(kk0 * kk0), L2_EPS * L2_EPS))
    k_h = k * (1.0 + (iclr - 1.0) * k_a)
    return lw, k_h, -kk, kk * iclr, g


def _rwkv_post(y, r, k_h, v, g, ln_g, ln_b, r_k):
    mu = _hsum(y) * (1.0 / HEAD_DIM)
    yc = y - mu
    var = _hsum(yc * yc) * (1.0 / HEAD_DIM)
    yo = yc * lax.rsqrt(var + GN_EPS) * ln_g + ln_b
    bonus = _hsum(r * k_h * r_k) * v
    return (yo + bonus) * g


def _shift_down(x, n):
    rows = lax.broadcasted_iota(jnp.int32, x.shape, 0)
    return jnp.where(rows < n, 0.0, pltpu.roll(x, n, 0))


def _shift_up(x, n):
    t_len = x.shape[0]
    rows = lax.broadcasted_iota(jnp.int32, x.shape, 0)
    return jnp.where(rows >= t_len - n, 0.0, pltpu.roll(x, t_len - n, 0))


def _exchange_plan(ins, outs, scatter, send_sems, recv_sems, local_sems):
    x, y, c = lax.axis_index("x"), lax.axis_index("y"), lax.axis_index("c")
    me = 4 * x + 2 * y + c

    def local(i):
        return pltpu.make_async_copy(ins[i].at[me] if scatter[i] else ins[i], outs[i].at[me], local_sems.at[i])

    def send(i, rel):
        return pltpu.make_async_remote_copy(
            src_ref=ins[i].at[me ^ rel] if scatter[i] else ins[i], dst_ref=outs[i].at[me],
            send_sem=send_sems.at[i, rel - 1], recv_sem=recv_sems.at[i, rel - 1],
            device_id=(x ^ (rel >> 2), y ^ ((rel >> 1) & 1), c ^ (rel & 1)), device_id_type=pl.DeviceIdType.MESH)

    def landed(i, rel):
        slot = outs[i].at[me ^ rel]
        return pltpu.make_async_remote_copy(
            src_ref=slot, dst_ref=slot, send_sem=send_sems.at[i, rel - 1], recv_sem=recv_sems.at[i, rel - 1],
            device_id=(x, y, c), device_id_type=pl.DeviceIdType.MESH)

    def start():
        for i in range(len(ins)):
            local(i).start()
            for rel in range(1, N_DEV):
                send(i, rel).start()

    def wait():
        for i in range(len(ins)):
            local(i).wait()
            for rel in range(1, N_DEV):
                landed(i, rel).wait_recv()
            for rel in range(1, N_DEV):
                send(i, rel).wait_send()

    return start, wait


def _gather_plan(ins, outs, send_sems, recv_sems, local_sems):
    x, y, c = lax.axis_index("x"), lax.axis_index("y"), lax.axis_index("c")
    me = 4 * x + 2 * y + c
    direct, chips = (1, 2, 4, 6), (2, 4, 6)

    def local(i):
        return pltpu.make_async_copy(ins[i], outs[i].at[me], local_sems.at[i])

    def send(i, rel):
        return pltpu.make_async_remote_copy(
            src_ref=ins[i], dst_ref=outs[i].at[me], send_sem=send_sems.at[i, rel - 1], recv_sem=recv_sems.at[i, rel - 1],
            device_id=(x ^ (rel >> 2), y ^ ((rel >> 1) & 1), c ^ (rel & 1)), device_id_type=pl.DeviceIdType.MESH)

    def passed(i, rel):
        slot = outs[i].at[me ^ rel]
        return pltpu.make_async_remote_copy(
            src_ref=slot, dst_ref=slot, send_sem=send_sems.at[i, rel], recv_sem=recv_sems.at[i, rel],
            device_id=(x, y, 1 - c), device_id_type=pl.DeviceIdType.MESH)

    def landed(i, rel):
        slot = outs[i].at[me ^ rel]
        return pltpu.make_async_remote_copy(
            src_ref=slot, dst_ref=slot, send_sem=send_sems.at[i, rel - 1], recv_sem=recv_sems.at[i, rel - 1],
            device_id=(x, y, c), device_id_type=pl.DeviceIdType.MESH)

    def start():
        for i in range(len(ins)):
            local(i).start()
            for rel in direct:
                send(i, rel).start()

    def forward(i):
        for rel in chips:
            landed(i, rel).wait_recv()
            passed(i, rel).start()

    def wait():
        for i in range(len(ins)):
            local(i).wait()
            for rel in (1, 3, 5, 7):
                landed(i, rel).wait_recv()
            for rel in direct:
                send(i, rel).wait_send()
            for rel in chips:
                passed(i, rel).wait_send()

    return start, forward, wait


def _exchange_io(arrays, scatter):
    n = len(arrays)
    any_spec = pl.BlockSpec(memory_space=pl.ANY)
    out_shape = [jax.ShapeDtypeStruct(a.shape if sc else (N_DEV,) + a.shape, a.dtype) for a, sc in zip(arrays, scatter)]
    sems = [pltpu.SemaphoreType.DMA((n, N_DEV - 1)), pltpu.SemaphoreType.DMA((n, N_DEV - 1)), pltpu.SemaphoreType.DMA((n,))]
    return [any_spec] * n, out_shape, sems


def _exchange(name, arrays, scatter):
    n = len(arrays)
    specs, out_shape, sems = _exchange_io(arrays, scatter)

    def body(*refs):
        if any(scatter):
            start, wait = _exchange_plan(refs[:n], refs[n:2 * n], scatter, *refs[2 * n:])
            start()
        else:
            start, forward, wait = _gather_plan(refs[:n], refs[n:2 * n], *refs[2 * n:])
            start()
            for i in range(n):
                forward(i)
        wait()

    return pl.pallas_call(
        body, name=name, in_specs=specs, out_specs=specs, out_shape=out_shape, scratch_shapes=sems,
        compiler_params=pltpu.CompilerParams(has_side_effects=True),
    )(*arrays)


def _scatter_start(name, arrays, lands):
    n = len(arrays)
    hbm = pl.BlockSpec(memory_space=pltpu.HBM)

    def body(*refs):
        ins, land, send_sems, recv_sems = refs[:n], refs[n:2 * n], refs[2 * n], refs[2 * n + 1]
        token = refs[4 * n + 2]
        x, y, c = lax.axis_index("x"), lax.axis_index("y"), lax.axis_index("c")
        me = 4 * x + 2 * y + c
        for i in range(n):
            for rel in range(1, N_DEV):
                k = i * (N_DEV - 1) + rel - 1
                pltpu.make_async_remote_copy(
                    src_ref=ins[i].at[me ^ rel], dst_ref=land[i].at[me], send_sem=send_sems.at[k],
                    recv_sem=recv_sems.at[k], device_id=(x ^ (rel >> 2), y ^ ((rel >> 1) & 1), c ^ (rel & 1)),
                    device_id_type=pl.DeviceIdType.MESH).start()
        token[...] = jnp.zeros_like(token)

    sem = pltpu.SemaphoreType.DMA((n * (N_DEV - 1),))
    bufs = [pltpu.HBM(a.shape, a.dtype) for a in list(arrays) + list(lands)]
    res = pl.pallas_call(
        body, name=name, out_shape=(sem, sem, *bufs, jax.ShapeDtypeStruct((8, LANE), F32)),
        in_specs=[hbm] * (2 * n),
        out_specs=(pl.BlockSpec(memory_space=pltpu.SEMAPHORE),) * 2 + (hbm,) * (2 * n) + (pl.BlockSpec(memory_space=pltpu.VMEM),),
        input_output_aliases={i: 2 + i for i in range(2 * n)},
        compiler_params=pltpu.CompilerParams(has_side_effects=pltpu.SideEffectType.DATAFLOW_SIDE_EFFECTING),
    )(*[pltpu.with_memory_space_constraint(a, pltpu.HBM) for a in list(arrays) + list(lands)])
    return res[0], res[1], res[2:2 + n], res[2 + n:2 + 2 * n], res[2 + 2 * n]


def _scatter_wait(name, send_sems, recv_sems, arrays, lands, after):
    n, n_after = len(arrays), len(after)
    hbm = pl.BlockSpec(memory_space=pltpu.HBM)

    def body(*refs):
        ins, land, s_sems, r_sems = refs[:n], refs[n:2 * n], refs[2 * n], refs[2 * n + 1]
        x, y, c = lax.axis_index("x"), lax.axis_index("y"), lax.axis_index("c")
        me = 4 * x + 2 * y + c
        for i in range(n):
            for rel in range(1, N_DEV):
                k = i * (N_DEV - 1) + rel - 1
                cp = pltpu.make_async_remote_copy(
                    src_ref=ins[i].at[me ^ rel], dst_ref=land[i].at[me ^ rel], send_sem=s_sems.at[k],
                    recv_sem=r_sems.at[k], device_id=(x, y, c), device_id_type=pl.DeviceIdType.MESH)
                cp.wait_send()
                cp.wait_recv()

    res = pl.pallas_call(
        body, name=name, out_shape=[pltpu.HBM(a.shape, a.dtype) for a in list(arrays) + list(lands)],
        in_specs=[hbm] * (2 * n) + [pl.BlockSpec(memory_space=pltpu.SEMAPHORE)] * 2 + [pl.BlockSpec(memory_space=pl.ANY)] * n_after,
        out_specs=[hbm] * (2 * n), input_output_aliases={i: i for i in range(2 * n)},
        compiler_params=pltpu.CompilerParams(has_side_effects=pltpu.SideEffectType.DATAFLOW_SIDE_EFFECTING),
    )(*arrays, *lands, send_sems, recv_sems, *after)
    return res[:n], res[n:]


def _tri_powers(low):
    powers, n = [low], 1
    while 2 * n < low.shape[-1]:
        powers.append(_mm(powers[-1], powers[-1], "nn", REC_PASSES))
        n *= 2
    return powers


@jax.custom_vjp
def _tri_solve(low, rhs):
    for p in _tri_powers(low):
        rhs = rhs + _mm(p, rhs, "nn", REC_PASSES)
    return rhs


def _tri_solve_fwd(low, rhs):
    powers = _tri_powers(low)
    for p in powers:
        rhs = rhs + _mm(p, rhs, "nn", REC_PASSES)
    return rhs, (powers, rhs)


def _tri_solve_bwd(res, d):
    powers, u = res
    for p in powers:
        d = d + _mm(p, d, "tn", REC_PASSES)
    return _mm(d, u, "nt", REC_PASSES), d


_tri_solve.defvjp(_tri_solve_fwd, _tri_solve_bwd)


def _chunk_fwd(z0, r, lw, k, v, a, b):
    n_h, c, n_k = r.shape
    mm = functools.partial(_mm, passes=REC_PASSES)
    gram = functools.partial(_mm, passes=2)
    ti = lax.broadcasted_iota(jnp.int32, (c, c), 0)
    si = lax.broadcasted_iota(jnp.int32, (c, c), 1)
    strict, incl = si < ti, si <= ti
    cum = _mm(jnp.broadcast_to(incl.astype(F32), (n_h, c, c)), lw, "nn", 3)
    cum_end = cum[:, c - 1:c, :]
    e_neg, e_end = jnp.exp(-cum), jnp.exp(cum_end - cum)
    x2 = jnp.concatenate([a * jnp.exp(cum - lw), r * jnp.exp(cum)], axis=1)
    y2 = jnp.concatenate([b * e_neg, k * e_neg], axis=1)
    mask = jnp.concatenate([jnp.concatenate([strict, strict], axis=1), jnp.concatenate([incl, incl], axis=1)], axis=0)
    g2 = jnp.where(mask, gram(x2, y2, "nt"), 0.0)
    t2 = mm(x2, z0, "nn") + mm(g2[:, :, c:], v, "nn")
    u = _tri_solve(g2[:, :c, :c], t2[:, :c])
    y = t2[:, c:] + mm(g2[:, c:, :c], u, "nn")
    ki = lax.broadcasted_iota(jnp.int32, (n_k, n_k), 0)
    kj = lax.broadcasted_iota(jnp.int32, (n_k, n_k), 1)
    dmat = jnp.where(ki == kj, jnp.broadcast_to(jnp.exp(cum_end), (n_h, n_k, n_k)), 0.0)
    z_end = mm(dmat, z0, "nn") + mm(jnp.concatenate([b * e_end, k * e_end], axis=1), jnp.concatenate([u, v], axis=1), "tn")
    return y, z_end


def _heads(x):
    return jnp.stack([x[:, h * HEAD_DIM:(h + 1) * HEAD_DIM] for h in range(N_HEADS)])


def _unheads(x):
    return jnp.concatenate([x[h] for h in range(N_HEADS)], axis=-1)


def _rec_params():
    return pltpu.CompilerParams(dimension_semantics=("arbitrary",), vmem_limit_bytes=VMEM_LIMIT, has_side_effects=True)


def _rec_fwd(u, lw, k, a, b, xch):
    t_len = lw.shape[0]
    c = min(REC_CHUNK, t_len)
    nc = t_len // c
    n_x = len(xch)
    x_specs, x_shapes, x_sems = _exchange_io(xch, [False] * n_x)
    sizes = [a_.size * a_.dtype.itemsize for a_ in xch]
    pass_step = [min(nc - 1, int(0.9 * nc * sum(sizes[:j + 1]) / sum(sizes)) + 1) for j in range(n_x)]

    def body(*refs):
        r_ref, v_ref, lw_ref, k_ref, a_ref, b_ref = refs[:6]
        x_in = refs[6:6 + n_x]
        y_ref, zs_ref = refs[6 + n_x:8 + n_x]
        x_out = refs[8 + n_x:8 + 2 * n_x]
        z_scr = refs[8 + 2 * n_x]
        start, forward, wait = _gather_plan(x_in, x_out, *refs[9 + 2 * n_x:])
        i = pl.program_id(0)

        @pl.when(i == 0)
        def _():
            start()
            z_scr[...] = jnp.zeros_like(z_scr)

        z0 = z_scr[...]
        zs_ref[0] = z0
        y, z_end = _chunk_fwd(z0, _heads(r_ref[...]), _heads(lw_ref[...]), _heads(k_ref[...]), _heads(v_ref[...]),
                              _heads(a_ref[...]), _heads(b_ref[...]))
        y_ref[...] = _unheads(y)
        z_scr[...] = z_end

        for j in range(n_x):
            pl.when(i == pass_step[j])(functools.partial(forward, j))

        @pl.when(i == nc - 1)
        def _():
            wait()

    blk = lambda cb: pl.BlockSpec((c, RWKV_DIM), functools.partial(lambda i, q: (i, q), q=cb))
    res = pl.pallas_call(
        body, name="rwkv_rec_fwd", grid=(nc,),
        in_specs=[blk(0), blk(2)] + [blk(0)] * 4 + x_specs,
        out_specs=[blk(0), pl.BlockSpec((1, N_HEADS, HEAD_DIM, HEAD_DIM), lambda i: (i, 0, 0, 0))] + x_specs,
        out_shape=[jax.ShapeDtypeStruct((t_len, RWKV_DIM), F32),
                   jax.ShapeDtypeStruct((nc, N_HEADS, HEAD_DIM, HEAD_DIM), F32)] + x_shapes,
        scratch_shapes=[pltpu.VMEM((N_HEADS, HEAD_DIM, HEAD_DIM), F32)] + x_sems,
        compiler_params=_rec_params(),
    )(u, u, lw, k, a, b, *xch)
    return res[0], res[1], res[2:]


def _rec_bwd(u, lw, k, a, b, zs, dy, xch, xch_scatter):
    t_len = lw.shape[0]
    c = min(REC_CHUNK, t_len)
    nc = t_len // c
    n_x = len(xch)
    x_specs, x_shapes, x_sems = _exchange_io(xch, xch_scatter)

    def body(*refs):
        r_ref, v_ref, lw_ref, k_ref, a_ref, b_ref, zs_ref, dy_ref = refs[:8]
        x_in = refs[8:8 + n_x]
        g_refs = refs[8 + n_x:14 + n_x]
        x_out = refs[14 + n_x:14 + 2 * n_x]
        dz_scr = refs[14 + 2 * n_x]
        start, wait = _exchange_plan(x_in, x_out, xch_scatter, *refs[15 + 2 * n_x:])
        i = pl.program_id(0)

        @pl.when(i == 0)
        def _():
            start()
            dz_scr[...] = jnp.zeros_like(dz_scr)

        _, vjp = jax.vjp(_chunk_fwd, zs_ref[0], _heads(r_ref[...]), _heads(lw_ref[...]), _heads(k_ref[...]),
                         _heads(v_ref[...]), _heads(a_ref[...]), _heads(b_ref[...]))
        dz0, dr, dlw, dk, dv, da, db = vjp((_heads(dy_ref[...]), dz_scr[...]))
        for ref, val in zip(g_refs, (dr, dv, dlw, dk, da, db)):
            ref[...] = _unheads(val)
        dz_scr[...] = dz0

        @pl.when(i == nc - 1)
        def _():
            wait()

    blk = lambda cb: pl.BlockSpec((c, RWKV_DIM), functools.partial(lambda i, q: (nc - 1 - i, q), q=cb))
    res = pl.pallas_call(
        body, name="rwkv_rec_bwd", grid=(nc,),
        in_specs=[blk(0), blk(2)] + [blk(0)] * 4
                 + [pl.BlockSpec((1, N_HEADS, HEAD_DIM, HEAD_DIM), lambda i: (nc - 1 - i, 0, 0, 0)), blk(0)] + x_specs,
        out_specs=[blk(0)] * 6 + x_specs,
        out_shape=[jax.ShapeDtypeStruct((t_len, RWKV_DIM), F32)] * 6 + x_shapes,
        scratch_shapes=[pltpu.VMEM((N_HEADS, HEAD_DIM, HEAD_DIM), F32)] + x_sems,
        compiler_params=_rec_params(),
    )(u, u, lw, k, a, b, zs, dy, *xch)
    return res[:6], res[6:]


_EARLY = ["w_in", "conv_w", "w_lora_up", "a_lora_up", "g_lora_up"]
_LATE = ["w_out", "w_up", "w_down", "w_ple_gate", "w_ple_proj"]
_SHARDED = _EARLY + _LATE
_COL_SHARDED = {"w_in", "conv_w", "w_lora_up", "a_lora_up", "g_lora_up", "w_up", "w_ple_proj"}
_BF16_GATHER = {"w_in", "w_out", "w_up", "w_down", "w_ple_gate", "w_ple_proj"}
_REPLICATED = ["norm_mix_g", "shift_mu", "w0", "a0", "k_k", "k_a", "r_k", "ln_x_g", "ln_x_b", "norm_mlp_g", "norm_ple_g",
               "norm_final_g"]
_WEIGHTS = ["norm_mix_g", "w_in", "conv_w", "shift_mu", "w_lora_up", "w0", "a_lora_up", "a0", "g_lora_up", "k_k", "k_a", "r_k",
            "ln_x_g", "ln_x_b", "w_out", "norm_mlp_g", "w_up", "w_down", "norm_ple_g", "w_ple_gate", "w_ple_proj", "norm_final_g"]


def _unshard(name, g):
    if name in _COL_SHARDED:
        return jnp.moveaxis(g, 0, 1).reshape(g.shape[1], N_DEV * g.shape[2])
    return g.reshape(N_DEV * g.shape[1], g.shape[2])


def _reshard(name, full):
    if name in _COL_SHARDED:
        return jnp.moveaxis(full.reshape(full.shape[0], N_DEV, full.shape[1] // N_DEV), 1, 0)
    return full.reshape(N_DEV, full.shape[0] // N_DEV, full.shape[1])


def _pad_in_cols(a):
    z = lambda n: jnp.zeros(a.shape[:-1] + (n,), a.dtype)
    conv = [a[..., part * CONV_DIM + j * LANE:part * CONV_DIM + (j + 1) * LANE] for j in range(CONV_DIM // LANE) for part in range(3)]
    return jnp.concatenate(conv + [a[..., CONV_COLS:3136], z(64), a[..., 3136:3200], z(64), a[..., 3200:3360], z(96)], axis=-1)


def _unpad_in_cols(a):
    conv = [a[..., (3 * j + part) * LANE:(3 * j + part + 1) * LANE] for part in range(3) for j in range(CONV_DIM // LANE)]
    return jnp.concatenate(conv + [a[..., CONV_COLS:3136], a[..., 3200:3264], a[..., 3328:3488]], axis=-1)


def _assemble_w_in(g):
    n_dev, rows, cols = g.shape

    def body(g_ref, o_ref):
        o_ref[...] = _pad_in_cols(jnp.concatenate([g_ref[d] for d in range(n_dev)], axis=1))

    return pl.pallas_call(
        body, name="w_in_assemble", grid=(rows // ROW_BLOCK,),
        in_specs=[pl.BlockSpec((n_dev, ROW_BLOCK, cols), lambda i: (0, i, 0))],
        out_specs=pl.BlockSpec((ROW_BLOCK, IN_PAD), lambda i: (i, 0)),
        out_shape=jax.ShapeDtypeStruct((rows, IN_PAD), g.dtype), compiler_params=_params(("arbitrary",)),
    )(g)


def _split_w_in_grad(dw):
    rows = dw.shape[0]
    cols = IN_COLS // N_DEV

    def body(d_ref, o_ref):
        full = _unpad_in_cols(d_ref[...])
        for d in range(N_DEV):
            o_ref[d] = full[:, cols * d:cols * (d + 1)]

    return pl.pallas_call(
        body, name="w_in_grad_split", grid=(rows // ROW_BLOCK,),
        in_specs=[pl.BlockSpec((ROW_BLOCK, IN_PAD), lambda i: (i, 0))],
        out_specs=pl.BlockSpec((N_DEV, ROW_BLOCK, cols), lambda i: (0, i, 0)),
        out_shape=jax.ShapeDtypeStruct((N_DEV, rows, cols), dw.dtype), compiler_params=_params(("arbitrary",)),
    )(dw)


def _pad_rows(a, rows):
    return jnp.concatenate([a, jnp.zeros((rows - a.shape[0],) + a.shape[1:], a.dtype)], axis=0)


SEG_W = [RWKV_DIM, RWKV_DIM, RWKV_DIM, LANE, LANE, 2 * LANE]
SEG_OFF = [0, 512, 1024, XW_OFF, XA_OFF, XG_OFF]


def _rwkv_pre_bwd(proj, u, grads, mu, small, dproj):
    t_len = u.shape[0]
    tr = min(ROW_BLOCK, t_len)
    nb = t_len // tr
    sub = 8
    n_g = len(grads)
    acc_shapes = [(1, RW_PAD)] + [(1, RWKV_DIM)] * 4 + [(LANE, RWKV_DIM), (LANE, RWKV_DIM), (2 * LANE, RWKV_DIM)]

    def body(*refs):
        seg_refs, halo_refs = refs[:6], refs[6:12]
        k_ref, xw_ref, xa_ref, xg_ref = refs[12:16]
        g_refs = refs[16:16 + n_g]
        mu_ref = refs[16 + n_g]
        prm_refs = refs[17 + n_g:24 + n_g]
        out_hbm = refs[25 + n_g]
        acc_refs = refs[26 + n_g:26 + n_g + len(acc_shapes)]
        vbuf, sems, carry = refs[26 + n_g + len(acc_shapes):]
        i = pl.program_id(0)
        blk = nb - 1 - i
        dr1, dr2, dv1, dv2, dlw, dk1, dk2, da, db, dg = [g[...] for g in g_refs]
        _, vjp = jax.vjp(_rwkv_pre, k_ref[...], xw_ref[...], xa_ref[...], xg_ref[...], *[p_[...] for p_ in prm_refs])
        dk, dxw, dxa, dxg, *dprm = vjp((dlw, dk1 + dk2, da, db, dg))
        du = jnp.concatenate([dr1 + dr2, dk, dv1 + dv2, dxw, dxa, dxg], axis=1)
        mu_v = mu_ref[...]

        @pl.when(i == 0)
        def _():
            carry[...] = jnp.zeros_like(carry)

        rows = lax.broadcasted_iota(jnp.int32, du.shape, 0)
        nxt = jnp.where(rows == tr - 1, carry[...], pltpu.roll(du, tr - 1, 0))
        d_rw = du - mu_v * du + mu_v * nxt
        d_mu = []
        for s_ref, h_ref, off, wd in zip(seg_refs, halo_refs, SEG_OFF, SEG_W):
            cur = s_ref[...]
            r0 = lax.broadcasted_iota(jnp.int32, cur.shape, 0)
            prev = jnp.where(r0 == 0, jnp.where(blk == 0, 0.0, h_ref[sub - 1:sub, :]), pltpu.roll(cur, 1, 0))
            d_mu.append(jnp.sum(du[:, off:off + wd] * (prev - cur), axis=0, keepdims=True))
        sums = [jnp.concatenate(d_mu, axis=1)] + list(dprm)

        @pl.when(i == 0)
        def _():
            for a_ref, val in zip(acc_refs, sums):
                a_ref[...] = val

        @pl.when(i > 0)
        def _():
            for a_ref, val in zip(acc_refs, sums):
                a_ref[...] += val

        carry[...] = du[0:1, :]
        slot = i % 2

        def writeback(s, b):
            return pltpu.make_async_copy(vbuf.at[s], out_hbm.at[pl.ds(b * tr, tr), pl.ds(CONV_COLS, RW_PAD)], sems.at[s])

        @pl.when(i >= 2)
        def _():
            writeback(slot, blk + 2).wait()

        vbuf[slot] = d_rw.astype(vbuf.dtype)
        writeback(slot, blk).start()

        @pl.when(i == nb - 1)
        def _():
            writeback(slot, blk).wait()
            if nb > 1:
                writeback(1 - slot, blk + 1).wait()

    rev = lambda w_, cb: pl.BlockSpec((tr, w_), functools.partial(lambda i, c: (nb - 1 - i, c), c=cb))
    halo = lambda w_, cb: pl.BlockSpec((sub, w_), functools.partial(
        lambda i, c: (jnp.maximum((nb - 1 - i) * (tr // sub) - 1, 0), c), c=cb))
    whole = lambda a: pl.BlockSpec(a.shape, functools.partial(lambda i, n: (0,) * n, n=a.ndim))
    segs = [(wd, (CONV_COLS + off) // wd) for off, wd in zip(SEG_OFF, SEG_W)]
    u_cols = [(512, 1), (LANE, XW_OFF // LANE), (LANE, XA_OFF // LANE), (2 * LANE, XG_OFF // (2 * LANE))]
    any_spec = pl.BlockSpec(memory_space=pl.ANY)
    res = pl.pallas_call(
        body, name="rwkv_pre_bwd", grid=(nb,),
        in_specs=[rev(*s) for s in segs] + [halo(*s) for s in segs] + [rev(*c) for c in u_cols]
                 + [rev(RWKV_DIM, 0)] * n_g + [whole(mu)] + [whole(p_) for p_ in small] + [any_spec],
        out_specs=[any_spec] + [pl.BlockSpec(s, functools.partial(lambda i, n: (0,) * n, n=len(s))) for s in acc_shapes],
        out_shape=[jax.ShapeDtypeStruct(dproj.shape, dproj.dtype)] + [jax.ShapeDtypeStruct(s, F32) for s in acc_shapes],
        scratch_shapes=[pltpu.VMEM((2, tr, RW_PAD), dproj.dtype), pltpu.SemaphoreType.DMA((2,)), pltpu.VMEM((1, RW_PAD), F32)],
        input_output_aliases={24 + n_g: 0},
        compiler_params=_params(("arbitrary",)),
    )(*[proj] * 12, *[u] * 4, *grads, mu, *small, dproj)
    return res


def _local_step(x, p, tgt, w, early_shards, late_shards):
    row = lambda v: v.reshape(1, -1)
    w = dict(w)

    xn1, *gathered = _rowwise("rms_mix", lambda h, g: (_rms(h, g),), [x], [w["norm_mix_g"]], [(D_MODEL, BF16)],
                              gather=early_shards)
    w.update({n: _unshard(n, g_) for n, g_ in zip(_EARLY[1:], gathered[1:])})
    w["w_in"] = _assemble_w_in(gathered[0])
    w["w_lora_up"] = _pad_rows(w["w_lora_up"], LANE)
    w["a_lora_up"] = _pad_rows(w["a_lora_up"], LANE)
    w["g_lora_up"] = _pad_rows(w["g_lora_up"], 2 * LANE)
    proj = _matmul("in_proj", xn1, w["w_in"], "nn", [F32], tm=2048, tn=512, tk=D_MODEL)
    n_cb = CONV_DIM // LANE

    def conv_fwd(blk, cw):
        gb, gc, hx = blk[:, :LANE], blk[:, LANE:2 * LANE], blk[:, 2 * LANE:]
        uu = gc * hx
        return (gb * (uu * cw[2:3] + _shift_down(uu, 1) * cw[1:2] + _shift_down(uu, 2) * cw[0:1]),)

    (y_conv,) = _colwise("conv_fwd", conv_fwd, n_cb, [(proj, 3 * LANE)], [w["conv_w"]], [(CONV_DIM, BF16, LANE)])

    small = [w["w0"], w["a0"], w["k_k"], w["k_a"], w["w_lora_up"], w["a_lora_up"], w["g_lora_up"]]
    def pre_fwd(*xs):
        cur, prev_rows, mu, prm = xs[:6], xs[6:12], xs[12], xs[13:]
        segs = []
        for c_, p_, off, wd in zip(cur, prev_rows, SEG_OFF, SEG_W):
            rows = lax.broadcasted_iota(jnp.int32, c_.shape, 0)
            prev = jnp.where(rows == 0, p_, pltpu.roll(c_, 1, 0))
            segs.append(c_ + mu[:, off:off + wd] * (prev - c_))
        return (jnp.concatenate(segs, axis=1),) + tuple(_rwkv_pre(segs[1], segs[3], segs[4], segs[5], *prm))

    proj_segs = [(proj, wd, (CONV_COLS + off) // wd) for off, wd in zip(SEG_OFF, SEG_W)]
    u, lw, k_h, ra, rb, g = _rowwise("rwkv_pre", pre_fwd, proj_segs, [w["shift_mu"]] + small,
                                     [(RW_PAD, F32)] + [(RWKV_DIM, F32)] * 5, halo=True)
    u_k, u_xw, u_xa, u_xg = (u, 512, 1), (u, LANE, XW_OFF // LANE), (u, LANE, XA_OFF // LANE), (u, 2 * LANE, XG_OFF // (2 * LANE))
    y_rec, zs, late = _rec_fwd(u, lw, k_h, ra, rb, late_shards)
    for n, gathered in zip(_LATE, late):
        w[n] = _unshard(n, gathered)
    post_c = [w["ln_x_g"], w["ln_x_b"], w["r_k"]]
    u_r, u_v = (u, 512, 0), (u, 512, 2)
    (y_rwkv,) = _rowwise("rwkv_post", lambda *xs: (_rwkv_post(*xs),), [y_rec, u_r, k_h, u_v, g], post_c, [(RWKV_DIM, BF16)])
    ycat = jnp.concatenate([y_conv, y_rwkv], axis=1)
    def res_norm(acc, r_, g_):
        h = acc + r_
        return h, _rms(h, g_)

    h1, xn2 = _matmul("out_proj", ycat, w["w_out"], "nn", [F32, BF16], tm=1024, tn=D_MODEL, tk=D_MODEL, extras=[x],
                      consts=[w["norm_mlp_g"]], epilogue=res_norm)

    square = lambda h: h.astype(F32) * h.astype(F32)
    hid = _matmul("mlp_up", xn2, w["w_up"], "nn", [BF16], tm=2048, tn=1024, tk=D_MODEL,
                  epilogue=lambda acc: (jnp.maximum(acc, 0.0),))
    h2, xn3 = _matmul("mlp_down", hid, w["w_down"], "nn", [F32, BF16], tm=512, tn=D_MODEL, tk=D_FF, extras=[h1],
                      consts=[w["norm_ple_g"]], epilogue=res_norm, a_map=square)
    zg = _matmul("ple_gate", xn3, w["w_ple_gate"], "nn", [F32], tm=1024, tn=1024, tk=D_MODEL)
    pp = _matmul("ple_proj", p, w["w_ple_proj"], "nn", [F32], tm=1024, tn=1024, tk=PLE_DIM)

    def head(h2_, zg_, pp_, tg, gf):
        gate = _sigmoid(zg_)
        h3 = h2_ + gate * pp_
        out = _rms(h3, gf)
        err = out - tg
        dh3, dgf = _rms_bwd(h3, gf, err * (1.0 / D_MODEL))
        loss = jnp.sum(jnp.sum(err * err, axis=1, keepdims=True), axis=0, keepdims=True) * (0.5 / D_MODEL)
        return dh3, dh3 * pp_ * gate * (1.0 - gate), dh3 * gate, dgf, loss

    dh3, dzg, dpp, d_norm_final, loss = _rowwise(
        "head", head, [h2, zg, pp, tgt], [row(w["norm_final_g"])], [(D_MODEL, F32), (D_MODEL, BF16), (D_MODEL, BF16)],
        [(1, D_MODEL), (1, 1)])

    d_w_ple_proj = _matmul("d_ple_proj", p, dpp, "tn", [BF16], tm=PLE_DIM, tn=D_MODEL // N_DEV, tk=4096, col_blocks_out=True)
    d_w_ple_gate = _matmul("d_ple_gate", xn3, dzg, "tn", [BF16], tm=512, tn=1024, tk=4096)

    def norm_bwd(dxn, h, dres, g_):
        dh, dg = _rms_bwd(h, g_, dxn)
        dh = dh + dres
        return dh, dh, dg

    nb = dict(tm=512, tn=D_MODEL, epilogue=norm_bwd, sums=[(1, D_MODEL)])
    dh2, dh2_b, d_norm_ple = _matmul("dx_ple_gate", dzg, w["w_ple_gate"], "nt", [F32, BF16], tk=D_MODEL,
                                     extras=[h2, dh3], consts=[w["norm_ple_g"]], **nb)
    d_w_down = _matmul("d_mlp_down", hid, dh2_b, "tn", [BF16], tm=512, tn=1024, tk=4096, a_map=square)
    dpre = _matmul("dx_mlp_down", dh2_b, w["w_down"], "nt", [BF16], tm=2048, tn=1024, tk=D_MODEL, extras=[hid],
                   epilogue=lambda acc, hid_: (acc * (2.0 * hid_.astype(F32)),))
    d_w_up = _matmul("d_mlp_up", xn2, dpre, "tn", [BF16], tm=1024, tn=D_FF // N_DEV, tk=4096, col_blocks_out=True)
    dh1, dh1_b, d_norm_mlp = _matmul("dx_mlp_up", dpre, w["w_up"], "nt", [F32, BF16], tk=D_FF,
                                     extras=[h1, dh2], consts=[w["norm_mlp_g"]], **nb)
    d_w_out = _matmul("d_out_proj", ycat, dh1_b, "tn", [BF16], tm=512, tn=1024, tk=4096)
    dycat = _matmul("dx_out_proj", dh1_b, w["w_out"], "nt", [F32], tm=1024, tn=1024, tk=D_MODEL)
    late_grads = dict(w_out=d_w_out, w_up=d_w_up, w_down=d_w_down, w_ple_gate=d_w_ple_gate, w_ple_proj=d_w_ple_proj)

    def conv_bwd(dy, blk, cw):
        gb, gc, hx = blk[:, :LANE], blk[:, LANE:2 * LANE], blk[:, 2 * LANE:]
        uu = gc * hx
        u1, u2 = _shift_down(uu, 1), _shift_down(uu, 2)
        dconv = dy * gb
        du = dconv * cw[2:3] + _shift_up(dconv, 1) * cw[1:2] + _shift_up(dconv, 2) * cw[0:1]
        s = lambda z: jnp.sum(z, axis=0, keepdims=True)
        d_blk = jnp.concatenate([dy * (uu * cw[2:3] + u1 * cw[1:2] + u2 * cw[0:1]), du * hx, du * gc], axis=1)
        return d_blk, s(dconv * u2), s(dconv * u1), s(dconv * uu)

    dproj, dcw0, dcw1, dcw2 = _colwise(
        "conv_bwd", conv_bwd, n_cb, [(dycat, LANE), (proj, 3 * LANE)], [w["conv_w"]],
        [(IN_PAD, BF16, 3 * LANE)], [(1, CONV_DIM)] * 3)

    def post_bwd(dy, y, r, k_h_, v, g_, ln_g, ln_b, r_k):
        _, vjp = jax.vjp(_rwkv_post, y, r, k_h_, v, g_, ln_g, ln_b, r_k)
        return vjp(dy)

    dy_rec, dr_p, dk_p, dv_p, dg, d_ln_g, d_ln_b, d_r_k = _rowwise(
        "rwkv_post_bwd", post_bwd, [(dycat, 512, 1), y_rec, u_r, k_h, u_v, g], post_c,
        [(RWKV_DIM, F32)] * 5, [(1, RWKV_DIM)] * 3)
    (dr_r, dv_r, dlw, dk_r, da, db), late_parts = _rec_bwd(
        u, lw, k_h, ra, rb, zs, dy_rec,
        [late_grads[n] if n in ("w_up", "w_ple_proj") else _reshard(n, late_grads[n]) for n in _LATE], [True] * len(_LATE))

    dproj, d_mu, d_w0, d_a0, d_k_k, d_k_a, d_wl, d_al, d_gl = _rwkv_pre_bwd(
        proj, u, [dr_p, dr_r, dv_p, dv_r, dlw, dk_p, dk_r, da, db, dg], w["shift_mu"], small, dproj)
    d_w_in = _matmul("d_in_proj", xn1, dproj, "tn", [BF16], tm=1024, tn=896, tk=4096)
    early_grads = dict(conv_w=jnp.concatenate([dcw0, dcw1, dcw2], axis=0),
                       w_lora_up=d_wl[:64], a_lora_up=d_al[:64], g_lora_up=d_gl[:160])
    early_send = [_split_w_in_grad(d_w_in)] + [_reshard(n, early_grads[n]) for n in _EARLY[1:]]
    s_sems, r_sems, sent, landing, token = _scatter_start("early_scatter_start", early_send,
                                                          [jnp.zeros(a.shape, a.dtype) for a in early_send])
    dx, d_norm_mix = _matmul(
        "dx_in_proj", dproj, w["w_in"], "nt", [F32], tk=IN_PAD, extras=[x, dh1], consts=[w["norm_mix_g"] + token[0:1, 0:1]],
        **dict(nb, epilogue=lambda *a: norm_bwd(*a)[1:]))

    grads = dict(
        norm_mix_g=d_norm_mix, shift_mu=d_mu, w0=d_w0, a0=d_a0, k_k=d_k_k, k_a=d_k_a, r_k=d_r_k,
        ln_x_g=d_ln_g, ln_x_b=d_ln_b, norm_mlp_g=d_norm_mlp, norm_ple_g=d_norm_ple, norm_final_g=d_norm_final)
    return loss, dx, grads, dict(zip(_LATE, late_parts)), (s_sems, r_sems, sent, landing)


def _adam_update(partials, w_ref, m_ref, v_ref, g_ref, d_ref, nm_ref, nv_ref):
    g = partials[0].astype(F32)
    for part in partials[1:]:
        g = g + part.astype(F32)
    nm =ADAM_B1 * m_ref[...] + (1.0 - ADAM_B1) * g
    nv = ADAM_B2 * v_ref[...] + (1.0 - ADAM_B2) * (g * g)
    m_hat = nm / (1.0 - ADAM_B1 ** ADAM_STEP)
    v_hat = nv / (1.0 - ADAM_B2 ** ADAM_STEP)
    g_ref[...] = g
    d_ref[...] = -ADAM_LR * (m_hat / (jnp.sqrt(v_hat) + ADAM_EPS) + ADAM_WD * w_ref[...])
    nm_ref[...] = nm
    nv_ref[...] = nv


SMALL_ROWS = 8


def _small_layout(widths):
    widths = list(widths) + [1]
    fill, place = [0] * SMALL_ROWS, [None] * len(widths)
    for j in sorted(range(len(widths)), key=lambda q: -widths[q]):
        row = fill.index(min(fill))
        place[j] = (row, fill[row])
        fill[row] += -(-widths[j] // LANE) * LANE
    return place, max(fill)


def _pack_small(vecs, loss):
    place, total = _small_layout([v_.shape[1] for v_ in vecs])
    n = len(vecs)

    def body(*refs):
        out = jnp.zeros((SMALL_ROWS, total), F32)
        row_id = lax.broadcasted_iota(jnp.int32, (SMALL_ROWS, total), 0)
        for row in range(SMALL_ROWS):
            mine = sorted((off, j) for j, (r_, off) in enumerate(place) if r_ == row)
            pieces, at = [], 0
            for off, j in mine:
                val = refs[j][...]
                pieces.append(val)
                at = off + val.shape[1]
                pad = -val.shape[1] % LANE
                if pad:
                    pieces.append(jnp.zeros((1, pad), F32))
                    at += pad
            if total > at:
                pieces.append(jnp.zeros((1, total - at), F32))
            out = jnp.where(row_id == row, jnp.broadcast_to(jnp.concatenate(pieces, axis=1), (SMALL_ROWS, total)), out)
        refs[n + 1][...] = out

    return pl.pallas_call(body, name="pack_small", out_shape=jax.ShapeDtypeStruct((SMALL_ROWS, total), F32))(*vecs, loss)


def _adamw_small(packed, ws, ms, vs):
    n = len(ws)
    place, _ = _small_layout([w_.shape[1] for w_ in ws])

    def body(p_ref, *refs):
        w_refs, m_refs, v_refs, outs = refs[:n], refs[n:2 * n], refs[2 * n:3 * n], refs[3 * n:]
        for j in range(n):
            row, off = place[j]
            cols = pl.ds(off, ws[j].shape[1])
            _adam_update([p_ref[s, row:row + 1, cols] for s in range(N_DEV)], w_refs[j], m_refs[j], v_refs[j],
                         *outs[4 * j:4 * j + 4])
        row, off = place[n]
        total = p_ref[0, row:row + 1, off:off + 1]
        for s in range(1, N_DEV):
            total = total + p_ref[s, row:row + 1, off:off + 1]
        outs[4 * n][...] = total

    res = pl.pallas_call(
        body, name="adamw_small",
        out_shape=[jax.ShapeDtypeStruct(w_.shape, F32) for w_ in ws for _ in range(4)] + [jax.ShapeDtypeStruct((1, 1), F32)],
    )(packed, *ws, *ms, *vs)
    return [res[4 * j:4 * j + 4] for j in range(n)], res[4 * n]


def _adamw(name, parts, w, m, v, own=None, me=None):
    rows, cols = w.shape[-2:]
    lead = w.ndim - 2
    tr = rows if rows * cols * 4 * 8 <= (4 << 20) else max(8, (4 << 20) // (cols * 4 * 8) // 8 * 8)
    while rows % tr:
        tr -= 8
    shape4 = [jax.ShapeDtypeStruct(w.shape, F32)] * 4
    if own is None:
        def body(p_ref, *refs):
            _adam_update([p_ref[s] for s in range(N_DEV)], *refs)

        blk = pl.BlockSpec((None,) * lead + (tr, cols), lambda i: (0,) * lead + (i, 0))
        return pl.pallas_call(
            body, name=name, grid=(rows // tr,),
            in_specs=[pl.BlockSpec((N_DEV, tr, cols), lambda i: (0, i, 0)), blk, blk, blk], out_specs=[blk] * 4,
            out_shape=shape4, compiler_params=_params(("arbitrary",)),
        )(parts, w, m, v)

    def body_own(me_ref, p_ref, own_ref, *refs):
        mine = own_ref[...]
        _adam_update([jnp.where(me_ref[0] == s, mine, p_ref[s]) for s in range(N_DEV)], *refs)

    blk = pl.BlockSpec((None,) * lead + (tr, cols), lambda i, me_ref: (0,) * lead + (i, 0))
    return pl.pallas_call(
        body_own, name=name, out_shape=shape4,
        grid_spec=pltpu.PrefetchScalarGridSpec(
            num_scalar_prefetch=1, grid=(rows // tr,),
            in_specs=[pl.BlockSpec((N_DEV, tr, cols), lambda i, me_ref: (0, i, 0)),
                      pl.BlockSpec((None, tr, cols), lambda i, me_ref: (me_ref[0], i, 0)), blk, blk, blk],
            out_specs=[blk] * 4),
        compiler_params=_params(("arbitrary",)),
    )(me, parts, own, w, m, v)


def kernel(x, p, norm_mix_g, w_in, conv_w, shift_mu, w_lora_up, w0, a_lora_up, a0, g_lora_up, k_k, k_a, r_k, ln_x_g, ln_x_b, w_out, norm_mlp_g, w_up, w_down, norm_ple_g, w_ple_gate, w_ple_proj, norm_final_g, loss_target, m_norm_mix_g, m_w_in, m_conv_w, m_shift_mu, m_w_lora_up, m_w0, m_a_lora_up, m_a0, m_g_lora_up, m_k_k, m_k_a, m_r_k, m_ln_x_g, m_ln_x_b, m_w_out, m_norm_mlp_g, m_w_up, m_w_down, m_norm_ple_g, m_w_ple_gate, m_w_ple_proj, m_norm_final_g, v_norm_mix_g, v_w_in, v_conv_w, v_shift_mu, v_w_lora_up, v_w0, v_a_lora_up, v_a0, v_g_lora_up, v_k_k, v_k_a, v_r_k, v_ln_x_g, v_ln_x_b, v_w_out, v_norm_mlp_g, v_w_up, v_w_down, v_norm_ple_g, v_w_ple_gate, v_w_ple_proj, v_norm_final_g):
    args = dict(locals())
    wts = {n: args[n] for n in _WEIGHTS}
    mom = {n: args["m_" + n] for n in _WEIGHTS}
    var = {n: args["v_" + n] for n in _WEIGHTS}
    shard2d = lambda a: a.reshape(a.shape[-2:])
    pad_mu = lambda a: _pad_in_cols(jnp.concatenate([jnp.zeros((1, CONV_COLS), F32), a], axis=1))[:, CONV_COLS:]
    unpad_mu = lambda a: _unpad_in_cols(jnp.concatenate([jnp.zeros((1, CONV_COLS), F32), a], axis=1))[:, CONV_COLS:]

    shards = {n: shard2d(wts[n]).astype(BF16 if n in _BF16_GATHER else F32) for n in _SHARDED}
    w = {n: wts[n].reshape(1, -1) for n in _REPLICATED}
    w["shift_mu"] = pad_mu(wts["shift_mu"])

    loss, dx, grads, parts, in_flight = _local_step(x[0], p[0, 0], loss_target[0], w, [shards[n] for n in _EARLY],
                                                    [shards[n] for n in _LATE])

    out = {n: _adamw("adamw_" + n, parts[n], wts[n], mom[n], var[n]) for n in _LATE}
    grads["shift_mu"] = unpad_mu(grads["shift_mu"])
    flat = lambda a: a.reshape(1, -1)
    (small_parts,) = _exchange("gather_small", [_pack_small([flat(grads[n]) for n in _REPLICATED], loss)], [False])
    small, loss_total = _adamw_small(small_parts, *[[flat(d[n]) for n in _REPLICATED] for d in (wts, mom, var)])
    for n, res in zip(_REPLICATED, small):
        out[n] = [r.reshape(wts[n].shape) for r in res]

    sent, early_parts = _scatter_wait("early_scatter_wait", *in_flight, after=[dx] + [out[n][1] for n in _LATE])
    me = (4 * lax.axis_index("x") + 2 * lax.axis_index("y") + lax.axis_index("c")).astype(jnp.int32).reshape(1)
    for n, prt, own in zip(_EARLY, early_parts, sent):
        out[n] = _adamw("adamw_" + n, prt, wts[n], mom[n], var[n], own=own, me=me)
    return (loss_total[0, 0], dx[None], *[out[n][0] for n in _WEIGHTS], *[out[n][1] for n in _WEIGHTS],
            *[out[n][2] for n in _WEIGHTS], *[out[n][3] for n in _WEIGHTS])
```

```python
import functools

import jax
import jax.numpy as jnp
from jax import lax
from jax.experimental import pallas as pl
from jax.experimental.pallas import tpu as pltpu

F32 = jnp.float32
BF16 = jnp.bfloat16

N_DEV = 8
D_MODEL = 1024
CONV_DIM = 512
RWKV_DIM = 512
HEAD_DIM = 64
N_HEADS = 8
D_FF = 4096
PLE_DIM = 256
RMS_EPS = 1e-6
GN_EPS = 64e-5
L2_EPS = 1e-12
ADAM_LR, ADAM_B1, ADAM_B2, ADAM_EPS, ADAM_WD, ADAM_STEP = 0.001, 0.9, 0.999, 1e-08, 0.01, 10

CONV_COLS = 3 * CONV_DIM
RW_PAD = 2048
IN_PAD = CONV_COLS + RW_PAD
IN_COLS = 3360
XW_OFF, XA_OFF, XG_OFF = 1536, 1664, 1792
REC_CHUNK = 128
REC_PASSES = 1
ROW_BLOCK = 256
LANE = 128
VMEM_LIMIT = 56 * 1024 * 1024


def _dims(dn, ndim):
    if ndim == 3:
        return {"nn": (((2,), (1,)), ((0,), (0,))), "nt": (((2,), (2,)), ((0,), (0,))),
                "tn": (((1,), (1,)), ((0,), (0,)))}[dn]
    return {"nn": (((1,), (0,)), ((), ())), "nt": (((1,), (1,)), ((), ())), "tn": (((0,), (0,)), ((), ()))}[dn]


def _split2(x):
    hi = x.astype(BF16)
    return hi, (x - hi.astype(F32)).astype(BF16)


def _mm_raw(x, y, dn, passes):
    f = lambda p, q: lax.dot_general(p, q, _dims(dn, x.ndim), preferred_element_type=F32)
    if passes == 1:
        return f(x.astype(BF16), y.astype(BF16))
    xh, xl = _split2(x)
    yh, yl = _split2(y)
    if passes == 2:
        return f(xh, yh) + f(xh, yl)
    return f(xh, yh) + f(xh, yl) + f(xl, yh)


@functools.partial(jax.custom_vjp, nondiff_argnums=(2, 3))
def _mm(x, y, dn, passes):
    return _mm_raw(x, y, dn, passes)


def _mm_fwd(x, y, dn, passes):
    return _mm_raw(x, y, dn, passes), (x, y)


def _mm_bwd(dn, passes, res, d):
    x, y = res
    if dn == "nn":
        return _mm(d, y, "nt", passes), _mm(x, d, "tn", passes)
    if dn == "nt":
        return _mm(d, y, "nn", passes), _mm(d, x, "tn", passes)
    return _mm(y, d, "nt", passes), _mm(x, d, "nn", passes)


_mm.defvjp(_mm_fwd, _mm_bwd)


def _head_ones():
    i = lax.broadcasted_iota(jnp.int32, (RWKV_DIM, RWKV_DIM), 0) // HEAD_DIM
    j = lax.broadcasted_iota(jnp.int32, (RWKV_DIM, RWKV_DIM), 1) // HEAD_DIM
    return (i == j).astype(BF16)


def _hsum_raw(x):
    ones = _head_ones()
    f = lambda p: lax.dot_general(p, ones, _dims("nn", 2), preferred_element_type=F32)
    x1, x2 = _split2(x)
    return f(x1) + f(x2)


@jax.custom_vjp
def _hsum(x):
    return _hsum_raw(x)


_hsum.defvjp(lambda x: (_hsum_raw(x), None), lambda _, d: (_hsum(d),))


def _sigmoid(x):
    return 0.5 + 0.5 * jnp.tanh(0.5 * x)


def _softplus(x):
    return jnp.maximum(x, 0.0) + jnp.log(1.0 + jnp.exp(-jnp.abs(x)))


def _params(sem):
    return pltpu.CompilerParams(dimension_semantics=sem, vmem_limit_bytes=VMEM_LIMIT)


def _rowwise(name, fn, rows, consts, row_outs, acc_outs=(), tr=ROW_BLOCK, halo=False, gather=()):
    rows = [r if isinstance(r, tuple) else (r, r.shape[1], 0) for r in rows]
    t_len = rows[0][0].shape[0]
    tr = min(tr, t_len)
    n_r, n_c, n_o, n_a, n_x = len(rows), len(consts), len(row_outs), len(acc_outs), len(gather)
    n_h = n_r if halo else 0
    sub = 8
    x_specs, x_shapes, x_sems = _exchange_io(gather, [False] * n_x) if n_x else ([], [], [])
    nb = t_len // tr

    def body(*refs):
        if n_x:
            n_in = n_r + n_h + n_c
            start, forward, wait = _gather_plan(refs[n_in:n_in + n_x], refs[len(refs) - 3 - n_x:len(refs) - 3], *refs[len(refs) - 3:])
            pl.when(pl.program_id(0) == 0)(start)
            refs = refs[:n_in] + refs[n_in + n_x:len(refs) - 3 - n_x]
        ins = [r[...] for r in refs[:n_r]]
        ins += [jnp.where(pl.program_id(0) == 0, 0.0, r[sub - 1:sub, :]) for r in refs[n_r:n_r + n_h]]
        ins += [r[...] for r in refs[n_r + n_h:n_r + n_h + n_c]]
        refs = refs[:n_r] + refs[n_r + n_h:]
        outs = fn(*ins)
        o_refs = refs[n_r + n_c:n_r + n_c + n_o]
        a_refs = refs[n_r + n_c + n_o:]
        for o_ref, val in zip(o_refs, outs[:n_o]):
            o_ref[...] = val.astype(o_ref.dtype)
        if n_a:
            first = pl.program_id(0) == 0

            @pl.when(first)
            def _():
                for a_ref, val in zip(a_refs, outs[n_o:]):
                    a_ref[...] = val

            @pl.when(jnp.logical_not(first))
            def _():
                for a_ref, val in zip(a_refs, outs[n_o:]):
                    a_ref[...] += val

        if n_x:
            @pl.when(pl.program_id(0) == nb - 1)
            def _():
                for j in range(n_x):
                    forward(j)
                wait()

    in_specs = [pl.BlockSpec((tr, w), functools.partial(lambda i, c: (i, c), c=cb)) for _, w, cb in rows]
    if halo:
        in_specs += [pl.BlockSpec((sub, w), functools.partial(lambda i, c: (jnp.maximum(i * (tr // sub) - 1, 0), c), c=cb))
                     for _, w, cb in rows]
    in_specs += [pl.BlockSpec(c.shape, functools.partial(lambda i, n: (0,) * n, n=c.ndim)) for c in consts]
    out_specs = [pl.BlockSpec((tr, w), lambda i: (i, 0)) for w, _ in row_outs]
    out_specs += [pl.BlockSpec(s, functools.partial(lambda i, n: (0,) * n, n=len(s))) for s in acc_outs]
    out_shape = [jax.ShapeDtypeStruct((t_len, w), dt) for w, dt in row_outs]
    out_shape += [jax.ShapeDtypeStruct(s, F32) for s in acc_outs]
    return pl.pallas_call(
        body, name=name, grid=(nb,), in_specs=in_specs + x_specs, out_specs=out_specs + x_specs,
        out_shape=out_shape + x_shapes, scratch_shapes=x_sems,
        compiler_params=pltpu.CompilerParams(dimension_semantics=("arbitrary",), vmem_limit_bytes=VMEM_LIMIT,
                                             has_side_effects=bool(n_x)),
    )(*[r[0] for r in rows], *([r[0] for r in rows] if halo else []), *consts, *gather)


def _colwise(name, fn, n_blocks, cols, prms, col_outs, prm_outs=()):
    t_len = cols[0][0].shape[0]
    n_i = len(cols) + len(prms)

    def body(*refs):
        outs = fn(*[r[...] for r in refs[:n_i]])
        for o_ref, val in zip(refs[n_i:], outs):
            o_ref[...] = val.astype(o_ref.dtype)

    spec = lambda r, w: pl.BlockSpec((r, w), lambda j: (0, j))
    in_specs = [spec(t_len, w) for _, w in cols] + [spec(a.shape[0], LANE) for a in prms]
    out_specs = [spec(t_len, bw) for _, _, bw in col_outs] + [spec(r, LANE) for r, _ in prm_outs]
    out_shape = [jax.ShapeDtypeStruct((t_len, w), dt) for w, dt, _ in col_outs]
    out_shape += [jax.ShapeDtypeStruct((r, w), F32) for r, w in prm_outs]
    return pl.pallas_call(
        body, name=name, grid=(n_blocks,), in_specs=in_specs, out_specs=out_specs, out_shape=out_shape,
        compiler_params=_params(("arbitrary",)),
    )(*[c[0] for c in cols], *prms)


def _matmul(name, a, b, dn, outs, *, tm, tn, tk, extras=(), consts=(), epilogue=None, sums=(), xch=(), xch_scatter=(),
            a_map=None, col_blocks_out=False):
    if dn == "nn":
        (m, k), n = a.shape, b.shape[1]
    elif dn == "nt":
        (m, k), n = a.shape, b.shape[0]
    else:
        (k, m), n = a.shape, b.shape[1]
    tm, tn, tk = min(tm, m), min(tn, n), min(tk, k)
    nk = k // tk
    grid = (m // tm, n // tn, nk)
    assert nk == 1 and (not sums or grid[1] == 1)
    a_spec = pl.BlockSpec((tk, tm), lambda i, j, q: (q, i)) if dn == "tn" else pl.BlockSpec((tm, tk), lambda i, j, q: (i, q))
    b_spec = pl.BlockSpec((tn, tk), lambda i, j, q: (j, q)) if dn == "nt" else pl.BlockSpec((tk, tn), lambda i, j, q: (q, j))
    o_spec = pl.BlockSpec((tm, tn), lambda i, j, q: (i, j))
    c_spec = pl.BlockSpec((1, tn), lambda i, j, q: (0, j))
    n_e, n_c, n_o, n_s, n_x = len(extras), len(consts), len(outs), len(sums), len(xch)
    x_specs, x_shapes, x_sems = _exchange_io(xch, xch_scatter) if n_x else ([], [], [])

    def body(*refs):
        a_ref, b_ref = refs[:2]
        e_refs = refs[2:2 + n_e + n_c]
        x_in = refs[2 + n_e + n_c:2 + n_e + n_c + n_x]
        rest = refs[2 + n_e + n_c + n_x:]
        o_refs, s_refs, x_out, scratch = rest[:n_o], rest[n_o:n_o + n_s], rest[n_o + n_s:n_o + n_s + n_x], rest[n_o + n_s + n_x:]
        step = (pl.program_id(0) * grid[1] + pl.program_id(1)) * nk + pl.program_id(2)
        if n_x:
            start, wait = _exchange_plan(x_in, x_out, xch_scatter, *scratch[len(scratch) - 3:])
            pl.when(step == 0)(start)
        a_blk = a_ref[...] if a_map is None else a_map(a_ref[...])
        acc = lax.dot_general(a_blk.astype(BF16), b_ref[...].astype(BF16), _dims(dn, 2), preferred_element_type=F32)
        vals = (acc,) if epilogue is None else epilogue(acc, *[e[...] for e in e_refs])
        for o_ref, val in zip(o_refs, vals[:n_o]):
            o_ref[...] = val.astype(o_ref.dtype)
        if n_s:
            @pl.when(step == 0)
            def _():
                for s_ref, val in zip(s_refs, vals[n_o:]):
                    s_ref[...] = val

            @pl.when(step > 0)
            def _():
                for s_ref, val in zip(s_refs, vals[n_o:]):
                    s_ref[...] += val

        if n_x:
            pl.when(step == grid[0] * grid[1] * nk - 1)(wait)

    plain = not (n_s or n_x)
    res = pl.pallas_call(
        body, name=name, grid=grid,
        in_specs=[a_spec, b_spec] + [o_spec] * n_e + [c_spec] * n_c + x_specs,
        out_specs=[pl.BlockSpec((None, tm, tn), lambda i, j, q: (j, i, 0)) if col_blocks_out else o_spec] * n_o
                  + [c_spec] * n_s + x_specs,
        out_shape=[jax.ShapeDtypeStruct((n // tn, m, tn) if col_blocks_out else (m, n), dt) for dt in outs] + [jax.ShapeDtypeStruct(s, F32) for s in sums] + x_shapes,
        scratch_shapes=x_sems,
        compiler_params=pltpu.CompilerParams(
            dimension_semantics=("parallel", "parallel", "arbitrary") if plain else ("arbitrary",) * 3,
            vmem_limit_bytes=VMEM_LIMIT, has_side_effects=bool(n_x)),
    )(a, b, *extras, *consts, *xch)
    return res[0] if len(res) == 1 else res


def _rms(h, g):
    return h * lax.rsqrt(jnp.mean(h * h, axis=-1, keepdims=True) + RMS_EPS) * g


def _rms_bwd(h, g, dy):
    rs = lax.rsqrt(jnp.mean(h * h, axis=-1, keepdims=True) + RMS_EPS)
    n = h * rs
    dn = dy * g
    dh = rs * (dn - n * jnp.mean(dn * n, axis=-1, keepdims=True))
    return dh, jnp.sum(dy * n, axis=0, keepdims=True)


def _rwkv_pre(k, xw, xa, xg, w0, a0, k_k, k_a, wl, al, gl):
    zw = w0 + _mm(jnp.tanh(xw), wl, "nn", 1)
    lw = -jnp.exp(-_softplus(-zw) - 0.5)
    iclr = _sigmoid(a0 + _mm(xa, al, "nn", 1))
    g = _mm(_sigmoid(xg), gl, "nn", 1)
    kk0 = k * k_k
    kk = kk0 * lax.rsqrt(jnp.maximum(_hsum(kk0 * kk0), L2_EPS * L2_EPS))
    k_h = k * (1.0 + (iclr - 1.0) * k_a)
    return lw, k_h, -kk, kk * iclr, g


def _rwkv_post(y, r, k_h, v, g, ln_g, ln_b, r_k):
    mu = _hsum(y) * (1.0 / HEAD_DIM)
    yc = y - mu
    var = _hsum(yc * yc) * (1.0 / HEAD_DIM)
    yo = yc * lax.rsqrt(var + GN_EPS) * ln_g + ln_b
    bonus = _hsum(r * k_h * r_k) * v
    return (yo + bonus) * g


def _shift_down(x, n):
    rows = lax.broadcasted_iota(jnp.int32, x.shape, 0)
    return jnp.where(rows < n, 0.0, pltpu.roll(x, n, 0))


def _shift_up(x, n):
    t_len = x.shape[0]
    rows = lax.broadcasted_iota(jnp.int32, x.shape, 0)
    return jnp.where(rows >= t_len - n, 0.0, pltpu.roll(x, t_len - n, 0))


def _exchange_plan(ins, outs, scatter, send_sems, recv_sems, local_sems):
    x, y, c = lax.axis_index("x"), lax.axis_index("y"), lax.axis_index("c")
    me = 4 * x + 2 * y + c

    def local(i):
        return pltpu.make_async_copy(ins[i].at[me] if scatter[i] else ins[i], outs[i].at[me], local_sems.at[i])

    def send(i, rel):
        return pltpu.make_async_remote_copy(
            src_ref=ins[i].at[me ^ rel] if scatter[i] else ins[i], dst_ref=outs[i].at[me],
            send_sem=send_sems.at[i, rel - 1], recv_sem=recv_sems.at[i, rel - 1],
            device_id=(x ^ (rel >> 2), y ^ ((rel >> 1) & 1), c ^ (rel & 1)), device_id_type=pl.DeviceIdType.MESH)

    def landed(i, rel):
        slot = outs[i].at[me ^ rel]
        return pltpu.make_async_remote_copy(
            src_ref=slot, dst_ref=slot, send_sem=send_sems.at[i, rel - 1], recv_sem=recv_sems.at[i, rel - 1],
            device_id=(x, y, c), device_id_type=pl.DeviceIdType.MESH)

    def start():
        for i in range(len(ins)):
            local(i).start()
            for rel in range(1, N_DEV):
                send(i, rel).start()

    def wait():
        for i in range(len(ins)):
            local(i).wait()
            for rel in range(1, N_DEV):
                landed(i, rel).wait_recv()
            for rel in range(1, N_DEV):
                send(i, rel).wait_send()

    return start, wait


def _gather_plan(ins, outs, send_sems, recv_sems, local_sems):
    x, y, c = lax.axis_index("x"), lax.axis_index("y"), lax.axis_index("c")
    me = 4 * x + 2 * y + c
    direct, chips = (1, 2, 4, 6), (2, 4, 6)

    def local(i):
        return pltpu.make_async_copy(ins[i], outs[i].at[me], local_sems.at[i])

    def send(i, rel):
        return pltpu.make_async_remote_copy(
            src_ref=ins[i], dst_ref=outs[i].at[me], send_sem=send_sems.at[i, rel - 1], recv_sem=recv_sems.at[i, rel - 1],
            device_id=(x ^ (rel >> 2), y ^ ((rel >> 1) & 1), c ^ (rel & 1)), device_id_type=pl.DeviceIdType.MESH)

    def passed(i, rel):
        slot = outs[i].at[me ^ rel]
        return pltpu.make_async_remote_copy(
            src_ref=slot, dst_ref=slot, send_sem=send_sems.at[i, rel], recv_sem=recv_sems.at[i, rel],
            device_id=(x, y, 1 - c), device_id_type=pl.DeviceIdType.MESH)

    def landed(i, rel):
        slot = outs[i].at[me ^ rel]
        return pltpu.make_async_remote_copy(
            src_ref=slot, dst_ref=slot, send_sem=send_sems.at[i, rel - 1], recv_sem=recv_sems.at[i, rel - 1],
            device_id=(x, y, c), device_id_type=pl.DeviceIdType.MESH)

    def start():
        for i in range(len(ins)):
            local(i).start()
            for rel in direct:
                send(i, rel).start()

    def forward(i):
        for rel in chips:
            landed(i, rel).wait_recv()
            passed(i, rel).start()

    def wait():
        for i in range(len(ins)):
            local(i).wait()
            for rel in (1, 3, 5, 7):
                landed(i, rel).wait_recv()
            for rel in direct:
                send(i, rel).wait_send()
            for rel in chips:
                passed(i, rel).wait_send()

    return start, forward, wait


def _exchange_io(arrays, scatter):
    n = len(arrays)
    any_spec = pl.BlockSpec(memory_space=pl.ANY)
    out_shape = [jax.ShapeDtypeStruct(a.shape if sc else (N_DEV,) + a.shape, a.dtype) for a, sc in zip(arrays, scatter)]
    sems = [pltpu.SemaphoreType.DMA((n, N_DEV - 1)), pltpu.SemaphoreType.DMA((n, N_DEV - 1)), pltpu.SemaphoreType.DMA((n,))]
    return [any_spec] * n, out_shape, sems


def _exchange(name, arrays, scatter):
    n = len(arrays)
    specs, out_shape, sems = _exchange_io(arrays, scatter)

    def body(*refs):
        if any(scatter):
            start, wait = _exchange_plan(refs[:n], refs[n:2 * n], scatter, *refs[2 * n:])
            start()
        else:
            start, forward, wait = _gather_plan(refs[:n], refs[n:2 * n], *refs[2 * n:])
            start()
            for i in range(n):
                forward(i)
        wait()

    return pl.pallas_call(
        body, name=name, in_specs=specs, out_specs=specs, out_shape=out_shape, scratch_shapes=sems,
        compiler_params=pltpu.CompilerParams(has_side_effects=True),
    )(*arrays)


def _scatter_start(name, arrays, lands):
    n = len(arrays)
    hbm = pl.BlockSpec(memory_space=pltpu.HBM)

    def body(*refs):
        ins, land, send_sems, recv_sems = refs[:n], refs[n:2 * n], refs[2 * n], refs[2 * n + 1]
        token = refs[4 * n + 2]
        x, y, c = lax.axis_index("x"), lax.axis_index("y"), lax.axis_index("c")
        me = 4 * x + 2 * y + c
        for i in range(n):
            for rel in range(1, N_DEV):
                k = i * (N_DEV - 1) + rel - 1
                pltpu.make_async_remote_copy(
                    src_ref=ins[i].at[me ^ rel], dst_ref=land[i].at[me], send_sem=send_sems.at[k],
                    recv_sem=recv_sems.at[k], device_id=(x ^ (rel >> 2), y ^ ((rel >> 1) & 1), c ^ (rel & 1)),
                    device_id_type=pl.DeviceIdType.MESH).start()
        token[...] = jnp.zeros_like(token)

    sem = pltpu.SemaphoreType.DMA((n * (N_DEV - 1),))
    bufs = [pltpu.HBM(a.shape, a.dtype) for a in list(arrays) + list(lands)]
    res = pl.pallas_call(
        body, name=name, out_shape=(sem, sem, *bufs, jax.ShapeDtypeStruct((8, LANE), F32)),
        in_specs=[hbm] * (2 * n),
        out_specs=(pl.BlockSpec(memory_space=pltpu.SEMAPHORE),) * 2 + (hbm,) * (2 * n) + (pl.BlockSpec(memory_space=pltpu.VMEM),),
        input_output_aliases={i: 2 + i for i in range(2 * n)},
        compiler_params=pltpu.CompilerParams(has_side_effects=pltpu.SideEffectType.DATAFLOW_SIDE_EFFECTING),
    )(*[pltpu.with_memory_space_constraint(a, pltpu.HBM) for a in list(arrays) + list(lands)])
    return res[0], res[1], res[2:2 + n], res[2 + n:2 + 2 * n], res[2 + 2 * n]


def _scatter_wait(name, send_sems, recv_sems, arrays, lands, after):
    n, n_after = len(arrays), len(after)
    hbm = pl.BlockSpec(memory_space=pltpu.HBM)

    def body(*refs):
        ins, land, s_sems, r_sems = refs[:n], refs[n:2 * n], refs[2 * n], refs[2 * n + 1]
        x, y, c = lax.axis_index("x"), lax.axis_index("y"), lax.axis_index("c")
        me = 4 * x + 2 * y + c
        for i in range(n):
            for rel in range(1, N_DEV):
                k = i * (N_DEV - 1) + rel - 1
                cp = pltpu.make_async_remote_copy(
                    src_ref=ins[i].at[me ^ rel], dst_ref=land[i].at[me ^ rel], send_sem=s_sems.at[k],
                    recv_sem=r_sems.at[k], device_id=(x, y, c), device_id_type=pl.DeviceIdType.MESH)
                cp.wait_send()
                cp.wait_recv()

    res = pl.pallas_call(
        body, name=name, out_shape=[pltpu.HBM(a.shape, a.dtype) for a in list(arrays) + list(lands)],
        in_specs=[hbm] * (2 * n) + [pl.BlockSpec(memory_space=pltpu.SEMAPHORE)] * 2 + [pl.BlockSpec(memory_space=pl.ANY)] * n_after,
        out_specs=[hbm] * (2 * n), input_output_aliases={i: i for i in range(2 * n)},
        compiler_params=pltpu.CompilerParams(has_side_effects=pltpu.SideEffectType.DATAFLOW_SIDE_EFFECTING),
    )(*arrays, *lands, send_sems, recv_sems, *after)
    return res[:n], res[n:]


def _tri_powers(low):
    powers, n = [low], 1
    while 2 * n < low.shape[-1]:
        powers.append(_mm(powers[-1], powers[-1], "nn", REC_PASSES))
        n *= 2
    return powers


@jax.custom_vjp
def _tri_solve(low, rhs):
    for p in _tri_powers(low):
        rhs = rhs + _mm(p, rhs, "nn", REC_PASSES)
    return rhs


def _tri_solve_fwd(low, rhs):
    powers = _tri_powers(low)
    for p in powers:
        rhs = rhs + _mm(p, rhs, "nn", REC_PASSES)
    return rhs, (powers, rhs)


def _tri_solve_bwd(res, d):
    powers, u = res
    for p in powers:
        d = d + _mm(p, d, "tn", REC_PASSES)
    return _mm(d, u, "nt", REC_PASSES), d


_tri_solve.defvjp(_tri_solve_fwd, _tri_solve_bwd)


def _chunk_fwd(z0, r, lw, k, v, a, b):
    n_h, c, n_k = r.shape
    mm = functools.partial(_mm, passes=REC_PASSES)
    gram = functools.partial(_mm, passes=2)
    ti = lax.broadcasted_iota(jnp.int32, (c, c), 0)
    si = lax.broadcasted_iota(jnp.int32, (c, c), 1)
    strict, incl = si < ti, si <= ti
    cum = _mm(jnp.broadcast_to(incl.astype(F32), (n_h, c, c)), lw, "nn", 3)
    cum_end = cum[:, c - 1:c, :]
    e_neg, e_end = jnp.exp(-cum), jnp.exp(cum_end - cum)
    x2 = jnp.concatenate([a * jnp.exp(cum - lw), r * jnp.exp(cum)], axis=1)
    y2 = jnp.concatenate([b * e_neg, k * e_neg], axis=1)
    mask = jnp.concatenate([jnp.concatenate([strict, strict], axis=1), jnp.concatenate([incl, incl], axis=1)], axis=0)
    g2 = jnp.where(mask, gram(x2, y2, "nt"), 0.0)
    t2 = mm(x2, z0, "nn") + mm(g2[:, :, c:], v, "nn")
    u = _tri_solve(g2[:, :c, :c], t2[:, :c])
    y = t2[:, c:] + mm(g2[:, c:, :c], u, "nn")
    ki = lax.broadcasted_iota(jnp.int32, (n_k, n_k), 0)
    kj = lax.broadcasted_iota(jnp.int32, (n_k, n_k), 1)
    dmat = jnp.where(ki == kj, jnp.broadcast_to(jnp.exp(cum_end), (n_h, n_k, n_k)), 0.0)
    z_end = mm(dmat, z0, "nn") + mm(jnp.concatenate([b * e_end, k * e_end], axis=1), jnp.concatenate([u, v], axis=1), "tn")
    return y, z_end


def _heads(x):
    return jnp.stack([x[:, h * HEAD_DIM:(h + 1) * HEAD_DIM] for h in range(N_HEADS)])


def _unheads(x):
    return jnp.concatenate([x[h] for h in range(N_HEADS)], axis=-1)


def _rec_params():
    return pltpu.CompilerParams(dimension_semantics=("arbitrary",), vmem_limit_bytes=VMEM_LIMIT, has_side_effects=True)


def _rec_fwd(u, lw, k, a, b, xch):
    t_len = lw.shape[0]
    c = min(REC_CHUNK, t_len)
    nc = t_len // c
    n_x = len(xch)
    x_specs, x_shapes, x_sems = _exchange_io(xch, [False] * n_x)
    sizes = [a_.size * a_.dtype.itemsize for a_ in xch]
    pass_step = [min(nc - 1, int(0.9 * nc * sum(sizes[:j + 1]) / sum(sizes)) + 1) for j in range(n_x)]

    def body(*refs):
        r_ref, v_ref, lw_ref, k_ref, a_ref, b_ref = refs[:6]
        x_in = refs[6:6 + n_x]
        y_ref, zs_ref = refs[6 + n_x:8 + n_x]
        x_out = refs[8 + n_x:8 + 2 * n_x]
        z_scr = refs[8 + 2 * n_x]
        start, forward, wait = _gather_plan(x_in, x_out, *refs[9 + 2 * n_x:])
        i = pl.program_id(0)

        @pl.when(i == 0)
        def _():
            start()
            z_scr[...] = jnp.zeros_like(z_scr)

        z0 = z_scr[...]
        zs_ref[0] = z0
        y, z_end = _chunk_fwd(z0, _heads(r_ref[...]), _heads(lw_ref[...]), _heads(k_ref[...]), _heads(v_ref[...]),
                              _heads(a_ref[...]), _heads(b_ref[...]))
        y_ref[...] = _unheads(y)
        z_scr[...] = z_end

        for j in range(n_x):
            pl.when(i == pass_step[j])(functools.partial(forward, j))

        @pl.when(i == nc - 1)
        def _():
            wait()

    blk = lambda cb: pl.BlockSpec((c, RWKV_DIM), functools.partial(lambda i, q: (i, q), q=cb))
    res = pl.pallas_call(
        body, name="rwkv_rec_fwd", grid=(nc,),
        in_specs=[blk(0), blk(2)] + [blk(0)] * 4 + x_specs,
        out_specs=[blk(0), pl.BlockSpec((1, N_HEADS, HEAD_DIM, HEAD_DIM), lambda i: (i, 0, 0, 0))] + x_specs,
        out_shape=[jax.ShapeDtypeStruct((t_len, RWKV_DIM), F32),
                   jax.ShapeDtypeStruct((nc, N_HEADS, HEAD_DIM, HEAD_DIM), F32)] + x_shapes,
        scratch_shapes=[pltpu.VMEM((N_HEADS, HEAD_DIM, HEAD_DIM), F32)] + x_sems,
        compiler_params=_rec_params(),
    )(u, u, lw, k, a, b, *xch)
    return res[0], res[1], res[2:]


def _rec_bwd(u, lw, k, a, b, zs, dy, xch, xch_scatter):
    t_len = lw.shape[0]
    c = min(REC_CHUNK, t_len)
    nc = t_len // c
    n_x = len(xch)
    x_specs, x_shapes, x_sems = _exchange_io(xch, xch_scatter)

    def body(*refs):
        r_ref, v_ref, lw_ref, k_ref, a_ref, b_ref, zs_ref, dy_ref = refs[:8]
        x_in = refs[8:8 + n_x]
        g_refs = refs[8 + n_x:14 + n_x]
        x_out = refs[14 + n_x:14 + 2 * n_x]
        dz_scr = refs[14 + 2 * n_x]
        start, wait = _exchange_plan(x_in, x_out, xch_scatter, *refs[15 + 2 * n_x:])
        i = pl.program_id(0)

        @pl.when(i == 0)
        def _():
            start()
            dz_scr[...] = jnp.zeros_like(dz_scr)

        _, vjp = jax.vjp(_chunk_fwd, zs_ref[0], _heads(r_ref[...]), _heads(lw_ref[...]), _heads(k_ref[...]),
                         _heads(v_ref[...]), _heads(a_ref[...]), _heads(b_ref[...]))
        dz0, dr, dlw, dk, dv, da, db = vjp((_heads(dy_ref[...]), dz_scr[...]))
        for ref, val in zip(g_refs, (dr, dv, dlw, dk, da, db)):
            ref[...] = _unheads(val)
        dz_scr[...] = dz0

        @pl.when(i == nc - 1)
        def _():
            wait()

    blk = lambda cb: pl.BlockSpec((c, RWKV_DIM), functools.partial(lambda i, q: (nc - 1 - i, q), q=cb))
    res = pl.pallas_call(
        body, name="rwkv_rec_bwd", grid=(nc,),
        in_specs=[blk(0), blk(2)] + [blk(0)] * 4
                 + [pl.BlockSpec((1, N_HEADS, HEAD_DIM, HEAD_DIM), lambda i: (nc - 1 - i, 0, 0, 0)), blk(0)] + x_specs,
        out_specs=[blk(0)] * 6 + x_specs,
        out_shape=[jax.ShapeDtypeStruct((t_len, RWKV_DIM), F32)] * 6 + x_shapes,
        scratch_shapes=[pltpu.VMEM((N_HEADS, HEAD_DIM, HEAD_DIM), F32)] + x_sems,
        compiler_params=_rec_params(),
    )(u, u, lw, k, a, b, zs, dy, *xch)
    return res[:6], res[6:]


_EARLY = ["w_in", "conv_w", "w_lora_up", "a_lora_up", "g_lora_up"]
_LATE = ["w_out", "w_up", "w_down", "w_ple_gate", "w_ple_proj"]
_SHARDED = _EARLY + _LATE
_COL_SHARDED = {"w_in", "conv_w", "w_lora_up", "a_lora_up", "g_lora_up", "w_up", "w_ple_proj"}
_BF16_GATHER = {"w_in", "w_out", "w_up", "w_down", "w_ple_gate", "w_ple_proj"}
_REPLICATED = ["norm_mix_g", "shift_mu", "w0", "a0", "k_k", "k_a", "r_k", "ln_x_g", "ln_x_b", "norm_mlp_g", "norm_ple_g",
               "norm_final_g"]
_WEIGHTS = ["norm_mix_g", "w_in", "conv_w", "shift_mu", "w_lora_up", "w0", "a_lora_up", "a0", "g_lora_up", "k_k", "k_a", "r_k",
            "ln_x_g", "ln_x_b", "w_out", "norm_mlp_g", "w_up", "w_down", "norm_ple_g", "w_ple_gate", "w_ple_proj", "norm_final_g"]


def _unshard(name, g):
    if name in _COL_SHARDED:
        return jnp.moveaxis(g, 0, 1).reshape(g.shape[1], N_DEV * g.shape[2])
    return g.reshape(N_DEV * g.shape[1], g.shape[2])


def _reshard(name, full):
    if name in _COL_SHARDED:
        return jnp.moveaxis(full.reshape(full.shape[0], N_DEV, full.shape[1] // N_DEV), 1, 0)
    return full.reshape(N_DEV, full.shape[0] // N_DEV, full.shape[1])


def _pad_in_cols(a):
    z = lambda n: jnp.zeros(a.shape[:-1] + (n,), a.dtype)
    conv = [a[..., part * CONV_DIM + j * LANE:part * CONV_DIM + (j + 1) * LANE] for j in range(CONV_DIM // LANE) for part in range(3)]
    return jnp.concatenate(conv + [a[..., CONV_COLS:3136], z(64), a[..., 3136:3200], z(64), a[..., 3200:3360], z(96)], axis=-1)


def _unpad_in_cols(a):
    conv = [a[..., (3 * j + part) * LANE:(3 * j + part + 1) * LANE] for part in range(3) for j in range(CONV_DIM // LANE)]
    return jnp.concatenate(conv + [a[..., CONV_COLS:3136], a[..., 3200:3264], a[..., 3328:3488]], axis=-1)


def _assemble_w_in(g):
    n_dev, rows, cols = g.shape

    def body(g_ref, o_ref):
        o_ref[...] = _pad_in_cols(jnp.concatenate([g_ref[d] for d in range(n_dev)], axis=1))

    return pl.pallas_call(
        body, name="w_in_assemble", grid=(rows // ROW_BLOCK,),
        in_specs=[pl.BlockSpec((n_dev, ROW_BLOCK, cols), lambda i: (0, i, 0))],
        out_specs=pl.BlockSpec((ROW_BLOCK, IN_PAD), lambda i: (i, 0)),
        out_shape=jax.ShapeDtypeStruct((rows, IN_PAD), g.dtype), compiler_params=_params(("arbitrary",)),
    )(g)


def _split_w_in_grad(dw):
    rows = dw.shape[0]
    cols = IN_COLS // N_DEV

    def body(d_ref, o_ref):
        full = _unpad_in_cols(d_ref[...])
        for d in range(N_DEV):
            o_ref[d] = full[:, cols * d:cols * (d + 1)]

    return pl.pallas_call(
        body, name="w_in_grad_split", grid=(rows // ROW_BLOCK,),
        in_specs=[pl.BlockSpec((ROW_BLOCK, IN_PAD), lambda i: (i, 0))],
        out_specs=pl.BlockSpec((N_DEV, ROW_BLOCK, cols), lambda i: (0, i, 0)),
        out_shape=jax.ShapeDtypeStruct((N_DEV, rows, cols), dw.dtype), compiler_params=_params(("arbitrary",)),
    )(dw)


def _pad_rows(a, rows):
    return jnp.concatenate([a, jnp.zeros((rows - a.shape[0],) + a.shape[1:], a.dtype)], axis=0)


SEG_W = [RWKV_DIM, RWKV_DIM, RWKV_DIM, LANE, LANE, 2 * LANE]
SEG_OFF = [0, 512, 1024, XW_OFF, XA_OFF, XG_OFF]


def _rwkv_pre_bwd(proj, u, grads, mu, small, dproj):
    t_len = u.shape[0]
    tr = min(ROW_BLOCK, t_len)
    nb = t_len // tr
    sub = 8
    n_g = len(grads)
    acc_shapes = [(1, RW_PAD)] + [(1, RWKV_DIM)] * 4 + [(LANE, RWKV_DIM), (LANE, RWKV_DIM), (2 * LANE, RWKV_DIM)]

    def body(*refs):
        seg_refs, halo_refs = refs[:6], refs[6:12]
        k_ref, xw_ref, xa_ref, xg_ref = refs[12:16]
        g_refs = refs[16:16 + n_g]
        mu_ref = refs[16 + n_g]
        prm_refs = refs[17 + n_g:24 + n_g]
        out_hbm = refs[25 + n_g]
        acc_refs = refs[26 + n_g:26 + n_g + len(acc_shapes)]
        vbuf, sems, carry = refs[26 + n_g + len(acc_shapes):]
        i = pl.program_id(0)
        blk = nb - 1 - i
        dr1, dr2, dv1, dv2, dlw, dk1, dk2, da, db, dg = [g[...] for g in g_refs]
        _, vjp = jax.vjp(_rwkv_pre, k_ref[...], xw_ref[...], xa_ref[...], xg_ref[...], *[p_[...] for p_ in prm_refs])
        dk, dxw, dxa, dxg, *dprm = vjp((dlw, dk1 + dk2, da, db, dg))
        du = jnp.concatenate([dr1 + dr2, dk, dv1 + dv2, dxw, dxa, dxg], axis=1)
        mu_v = mu_ref[...]

        @pl.when(i == 0)
        def _():
            carry[...] = jnp.zeros_like(carry)

        rows = lax.broadcasted_iota(jnp.int32, du.shape, 0)
        nxt = jnp.where(rows == tr - 1, carry[...], pltpu.roll(du, tr - 1, 0))
        d_rw = du - mu_v * du + mu_v * nxt
        d_mu = []
        for s_ref, h_ref, off, wd in zip(seg_refs, halo_refs, SEG_OFF, SEG_W):
            cur = s_ref[...]
            r0 = lax.broadcasted_iota(jnp.int32, cur.shape, 0)
            prev = jnp.where(r0 == 0, jnp.where(blk == 0, 0.0, h_ref[sub - 1:sub, :]), pltpu.roll(cur, 1, 0))
            d_mu.append(jnp.sum(du[:, off:off + wd] * (prev - cur), axis=0, keepdims=True))
        sums = [jnp.concatenate(d_mu, axis=1)] + list(dprm)

        @pl.when(i == 0)
        def _():
            for a_ref, val in zip(acc_refs, sums):
                a_ref[...] = val

        @pl.when(i > 0)
        def _():
            for a_ref, val in zip(acc_refs, sums):
                a_ref[...] += val

        carry[...] = du[0:1, :]
        slot = i % 2

        def writeback(s, b):
            return pltpu.make_async_copy(vbuf.at[s], out_hbm.at[pl.ds(b * tr, tr), pl.ds(CONV_COLS, RW_PAD)], sems.at[s])

        @pl.when(i >= 2)
        def _():
            writeback(slot, blk + 2).wait()

        vbuf[slot] = d_rw.astype(vbuf.dtype)
        writeback(slot, blk).start()

        @pl.when(i == nb - 1)
        def _():
            writeback(slot, blk).wait()
            if nb > 1:
                writeback(1 - slot, blk + 1).wait()

    rev = lambda w_, cb: pl.BlockSpec((tr, w_), functools.partial(lambda i, c: (nb - 1 - i, c), c=cb))
    halo = lambda w_, cb: pl.BlockSpec((sub, w_), functools.partial(
        lambda i, c: (jnp.maximum((nb - 1 - i) * (tr // sub) - 1, 0), c), c=cb))
    whole = lambda a: pl.BlockSpec(a.shape, functools.partial(lambda i, n: (0,) * n, n=a.ndim))
    segs = [(wd, (CONV_COLS + off) // wd) for off, wd in zip(SEG_OFF, SEG_W)]
    u_cols = [(512, 1), (LANE, XW_OFF // LANE), (LANE, XA_OFF // LANE), (2 * LANE, XG_OFF // (2 * LANE))]
    any_spec = pl.BlockSpec(memory_space=pl.ANY)
    res = pl.pallas_call(
        body, name="rwkv_pre_bwd", grid=(nb,),
        in_specs=[rev(*s) for s in segs] + [halo(*s) for s in segs] + [rev(*c) for c in u_cols]
                 + [rev(RWKV_DIM, 0)] * n_g + [whole(mu)] + [whole(p_) for p_ in small] + [any_spec],
        out_specs=[any_spec] + [pl.BlockSpec(s, functools.partial(lambda i, n: (0,) * n, n=len(s))) for s in acc_shapes],
        out_shape=[jax.ShapeDtypeStruct(dproj.shape, dproj.dtype)] + [jax.ShapeDtypeStruct(s, F32) for s in acc_shapes],
        scratch_shapes=[pltpu.VMEM((2, tr, RW_PAD), dproj.dtype), pltpu.SemaphoreType.DMA((2,)), pltpu.VMEM((1, RW_PAD), F32)],
        input_output_aliases={24 + n_g: 0},
        compiler_params=_params(("arbitrary",)),
    )(*[proj] * 12, *[u] * 4, *grads, mu, *small, dproj)
    return res


def _local_step(x, p, tgt, w, early_shards, late_shards):
    row = lambda v: v.reshape(1, -1)
    w = dict(w)

    xn1, *gathered = _rowwise("rms_mix", lambda h, g: (_rms(h, g),), [x], [w["norm_mix_g"]], [(D_MODEL, BF16)],
                              gather=early_shards)
    w.update({n: _unshard(n, g_) for n, g_ in zip(_EARLY[1:], gathered[1:])})
    w["w_in"] = _assemble_w_in(gathered[0])
    w["w_lora_up"] = _pad_rows(w["w_lora_up"], LANE)
    w["a_lora_up"] = _pad_rows(w["a_lora_up"], LANE)
    w["g_lora_up"] = _pad_rows(w["g_lora_up"], 2 * LANE)
    proj = _matmul("in_proj", xn1, w["w_in"], "nn", [F32], tm=2048, tn=512, tk=D_MODEL)
    n_cb = CONV_DIM // LANE

    def conv_fwd(blk, cw):
        gb, gc, hx = blk[:, :LANE], blk[:, LANE:2 * LANE], blk[:, 2 * LANE:]
        uu = gc * hx
        return (gb * (uu * cw[2:3] + _shift_down(uu, 1) * cw[1:2] + _shift_down(uu, 2) * cw[0:1]),)

    (y_conv,) = _colwise("conv_fwd", conv_fwd, n_cb, [(proj, 3 * LANE)], [w["conv_w"]], [(CONV_DIM, BF16, LANE)])

    small = [w["w0"], w["a0"], w["k_k"], w["k_a"], w["w_lora_up"], w["a_lora_up"], w["g_lora_up"]]
    def pre_fwd(*xs):
        cur, prev_rows, mu, prm = xs[:6], xs[6:12], xs[12], xs[13:]
        segs = []
        for c_, p_, off, wd in zip(cur, prev_rows, SEG_OFF, SEG_W):
            rows = lax.broadcasted_iota(jnp.int32, c_.shape, 0)
            prev = jnp.where(rows == 0, p_, pltpu.roll(c_, 1, 0))
            segs.append(c_ + mu[:, off:off + wd] * (prev - c_))
        return (jnp.concatenate(segs, axis=1),) + tuple(_rwkv_pre(segs[1], segs[3], segs[4], segs[5], *prm))

    proj_segs = [(proj, wd, (CONV_COLS + off) // wd) for off, wd in zip(SEG_OFF, SEG_W)]
    u, lw, k_h, ra, rb, g = _rowwise("rwkv_pre", pre_fwd, proj_segs, [w["shift_mu"]] + small,
                                     [(RW_PAD, F32)] + [(RWKV_DIM, F32)] * 5, halo=True)
    u_k, u_xw, u_xa, u_xg = (u, 512, 1), (u, LANE, XW_OFF // LANE), (u, LANE, XA_OFF // LANE), (u, 2 * LANE, XG_OFF // (2 * LANE))
    y_rec, zs, late = _rec_fwd(u, lw, k_h, ra, rb, late_shards)
    for n, gathered in zip(_LATE, late):
        w[n] = _unshard(n, gathered)
    post_c = [w["ln_x_g"], w["ln_x_b"], w["r_k"]]
    u_r, u_v = (u, 512, 0), (u, 512, 2)
    (y_rwkv,) = _rowwise("rwkv_post", lambda *xs: (_rwkv_post(*xs),), [y_rec, u_r, k_h, u_v, g], post_c, [(RWKV_DIM, BF16)],
                         tr=2 * ROW_BLOCK)
    ycat = jnp.concatenate([y_conv, y_rwkv], axis=1)
    def res_norm(acc, r_, g_):
        h = acc + r_
        return h, _rms(h, g_)

    h1, xn2 = _matmul("out_proj", ycat, w["w_out"], "nn", [F32, BF16], tm=1024, tn=D_MODEL, tk=D_MODEL, extras=[x],
                      consts=[w["norm_mlp_g"]], epilogue=res_norm)

    square = lambda h: h.astype(F32) * h.astype(F32)
    hid = _matmul("mlp_up", xn2, w["w_up"], "nn", [BF16], tm=2048, tn=1024, tk=D_MODEL,
                  epilogue=lambda acc: (jnp.maximum(acc, 0.0),))
    h2, xn3 = _matmul("mlp_down", hid, w["w_down"], "nn", [F32, BF16], tm=512, tn=D_MODEL, tk=D_FF, extras=[h1],
                      consts=[w["norm_ple_g"]], epilogue=res_norm, a_map=square)
    zg = _matmul("ple_gate", xn3, w["w_ple_gate"], "nn", [F32], tm=1024, tn=1024, tk=D_MODEL)
    pp = _matmul("ple_proj", p, w["w_ple_proj"], "nn", [F32], tm=1024, tn=1024, tk=PLE_DIM)

    def head(h2_, zg_, pp_, tg, gf):
        gate = _sigmoid(zg_)
        h3 = h2_ + gate * pp_
        out = _rms(h3, gf)
        err = out - tg
        dh3, dgf = _rms_bwd(h3, gf, err * (1.0 / D_MODEL))
        loss = jnp.sum(jnp.sum(err * err, axis=1, keepdims=True), axis=0, keepdims=True) * (0.5 / D_MODEL)
        return dh3, dh3 * pp_ * gate * (1.0 - gate), dh3 * gate, dgf, loss

    dh3, dzg, dpp, d_norm_final, loss = _rowwise(
        "head", head, [h2, zg, pp, tgt], [row(w["norm_final_g"])], [(D_MODEL, F32), (D_MODEL, BF16), (D_MODEL, BF16)],
        [(1, D_MODEL), (1, 1)], tr=2 * ROW_BLOCK)

    d_w_ple_proj = _matmul("d_ple_proj", p, dpp, "tn", [BF16], tm=PLE_DIM, tn=D_MODEL // N_DEV, tk=4096, col_blocks_out=True)
    d_w_ple_gate = _matmul("d_ple_gate", xn3, dzg, "tn", [BF16], tm=512, tn=1024, tk=4096)

    def norm_bwd(dxn, h, dres, g_):
        dh, dg = _rms_bwd(h, g_, dxn)
        dh = dh + dres
        return dh, dh, dg

    nb = dict(tm=512, tn=D_MODEL, epilogue=norm_bwd, sums=[(1, D_MODEL)])
    dh2, dh2_b, d_norm_ple = _matmul("dx_ple_gate", dzg, w["w_ple_gate"], "nt", [F32, BF16], tk=D_MODEL,
                                     extras=[h2, dh3], consts=[w["norm_ple_g"]], **nb)
    d_w_down = _matmul("d_mlp_down", hid, dh2_b, "tn", [BF16], tm=512, tn=1024, tk=4096, a_map=square)
    dpre = _matmul("dx_mlp_down", dh2_b, w["w_down"], "nt", [BF16], tm=2048, tn=1024, tk=D_MODEL, extras=[hid],
                   epilogue=lambda acc, hid_: (acc * (2.0 * hid_.astype(F32)),))
    d_w_up = _matmul("d_mlp_up", xn2, dpre, "tn", [BF16], tm=1024, tn=D_FF // N_DEV, tk=4096, col_blocks_out=True)
    dh1, dh1_b, d_norm_mlp = _matmul("dx_mlp_up", dpre, w["w_up"], "nt", [F32, BF16], tk=D_FF,
                                     extras=[h1, dh2], consts=[w["norm_mlp_g"]], **nb)
    d_w_out = _matmul("d_out_proj", ycat, dh1_b, "tn", [BF16], tm=512, tn=1024, tk=4096)
    dycat = _matmul("dx_out_proj", dh1_b, w["w_out"], "nt", [F32], tm=1024, tn=1024, tk=D_MODEL)
    late_grads = dict(w_out=d_w_out, w_up=d_w_up, w_down=d_w_down, w_ple_gate=d_w_ple_gate, w_ple_proj=d_w_ple_proj)

    def conv_bwd(dy, blk, cw):
        gb, gc, hx = blk[:, :LANE], blk[:, LANE:2 * LANE], blk[:, 2 * LANE:]
        uu = gc * hx
        u1, u2 = _shift_down(uu, 1), _shift_down(uu, 2)
        dconv = dy * gb
        du = dconv * cw[2:3] + _shift_up(dconv, 1) * cw[1:2] + _shift_up(dconv, 2) * cw[0:1]
        s = lambda z: jnp.sum(z, axis=0, keepdims=True)
        d_blk = jnp.concatenate([dy * (uu * cw[2:3] + u1 * cw[1:2] + u2 * cw[0:1]), du * hx, du * gc], axis=1)
        return d_blk, s(dconv * u2), s(dconv * u1), s(dconv * uu)

    dproj, dcw0, dcw1, dcw2 = _colwise(
        "conv_bwd", conv_bwd, n_cb, [(dycat, LANE), (proj, 3 * LANE)], [w["conv_w"]],
        [(IN_PAD, BF16, 3 * LANE)], [(1, CONV_DIM)] * 3)

    def post_bwd(dy, y, r, k_h_, v, g_, ln_g, ln_b, r_k):
        _, vjp = jax.vjp(_rwkv_post, y, r, k_h_, v, g_, ln_g, ln_b, r_k)
        return vjp(dy)

    dy_rec, dr_p, dk_p, dv_p, dg, d_ln_g, d_ln_b, d_r_k = _rowwise(
        "rwkv_post_bwd", post_bwd, [(dycat, 512, 1), y_rec, u_r, k_h, u_v, g], post_c,
        [(RWKV_DIM, F32)] * 5, [(1, RWKV_DIM)] * 3)
    (dr_r, dv_r, dlw, dk_r, da, db), late_parts = _rec_bwd(
        u, lw, k_h, ra, rb, zs, dy_rec,
        [late_grads[n] if n in ("w_up", "w_ple_proj") else _reshard(n, late_grads[n]) for n in _LATE], [True] * len(_LATE))

    dproj, d_mu, d_w0, d_a0, d_k_k, d_k_a, d_wl, d_al, d_gl = _rwkv_pre_bwd(
        proj, u, [dr_p, dr_r, dv_p, dv_r, dlw, dk_p, dk_r, da, db, dg], w["shift_mu"], small, dproj)
    d_w_in = _matmul("d_in_proj", xn1, dproj, "tn", [BF16], tm=1024, tn=896, tk=4096)
    early_grads = dict(conv_w=jnp.concatenate([dcw0, dcw1, dcw2], axis=0),
                       w_lora_up=d_wl[:64], a_lora_up=d_al[:64], g_lora_up=d_gl[:160])
    early_send = [_split_w_in_grad(d_w_in)] + [_reshard(n, early_grads[n]) for n in _EARLY[1:]]
    s_sems, r_sems, sent, landing, token = _scatter_start("early_scatter_start", early_send,
                                                          [jnp.zeros(a.shape, a.dtype) for a in early_send])
    dx, d_norm_mix = _matmul(
        "dx_in_proj", dproj, w["w_in"], "nt", [F32], tk=IN_PAD, extras=[x, dh1], consts=[w["norm_mix_g"] + token[0:1, 0:1]],
        **dict(nb, epilogue=lambda *a: norm_bwd(*a)[1:]))

    grads = dict(
        norm_mix_g=d_norm_mix, shift_mu=d_mu, w0=d_w0, a0=d_a0, k_k=d_k_k, k_a=d_k_a, r_k=d_r_k,
        ln_x_g=d_ln_g, ln_x_b=d_ln_b, norm_mlp_g=d_norm_mlp, norm_ple_g=d_norm_ple, norm_final_g=d_norm_final)
    return loss, dx, grads, dict(zip(_LATE, late_parts)), (s_sems, r_sems, sent, landing)


def _adam_update(partials, w_ref, m_ref, v_ref, g_ref, d_ref, nm_ref, nv_ref):
    g = partials[0].astype(F32)
    for part in partials[1:]:
        g = g + part.astype(F32)
    nm =ADAM_B1 * m_ref[...] + (1.0 - ADAM_B1) * g
    nv = ADAM_B2 * v_ref[...] + (1.0 - ADAM_B2) * (g * g)
    m_hat = nm / (1.0 - ADAM_B1 ** ADAM_STEP)
    v_hat = nv / (1.0 - ADAM_B2 ** ADAM_STEP)
    g_ref[...] = g
    d_ref[...] = -ADAM_LR * (m_hat / (jnp.sqrt(v_hat) + ADAM_EPS) + ADAM_WD * w_ref[...])
    nm_ref[...] = nm
    nv_ref[...] = nv


SMALL_ROWS = 8


def _small_layout(widths):
    widths = list(widths) + [1]
    fill, place = [0] * SMALL_ROWS, [None] * len(widths)
    for j in sorted(range(len(widths)), key=lambda q: -widths[q]):
        row = fill.index(min(fill))
        place[j] = (row, fill[row])
        fill[row] += -(-widths[j] // LANE) * LANE
    return place, max(fill)


def _pack_small(vecs, loss):
    place, total = _small_layout([v_.shape[1] for v_ in vecs])
    n = len(vecs)

    def body(*refs):
        out = jnp.zeros((SMALL_ROWS, total), F32)
        row_id = lax.broadcasted_iota(jnp.int32, (SMALL_ROWS, total), 0)
        for row in range(SMALL_ROWS):
            mine = sorted((off, j) for j, (r_, off) in enumerate(place) if r_ == row)
            pieces, at = [], 0
            for off, j in mine:
                val = refs[j][...]
                pieces.append(val)
                at = off + val.shape[1]
                pad = -val.shape[1] % LANE
                if pad:
                    pieces.append(jnp.zeros((1, pad), F32))
                    at += pad
            if total > at:
                pieces.append(jnp.zeros((1, total - at), F32))
            out = jnp.where(row_id == row, jnp.broadcast_to(jnp.concatenate(pieces, axis=1), (SMALL_ROWS, total)), out)
        refs[n + 1][...] = out

    return pl.pallas_call(body, name="pack_small", out_shape=jax.ShapeDtypeStruct((SMALL_ROWS, total), F32))(*vecs, loss)


def _adamw_small(packed, ws, ms, vs):
    n = len(ws)
    place, _ = _small_layout([w_.shape[1] for w_ in ws])

    def body(p_ref, *refs):
        w_refs, m_refs, v_refs, outs = refs[:n], refs[n:2 * n], refs[2 * n:3 * n], refs[3 * n:]
        for j in range(n):
            row, off = place[j]
            cols = pl.ds(off, ws[j].shape[1])
            _adam_update([p_ref[s, row:row + 1, cols] for s in range(N_DEV)], w_refs[j], m_refs[j], v_refs[j],
                         *outs[4 * j:4 * j + 4])
        row, off = place[n]
        total = p_ref[0, row:row + 1, off:off + 1]
        for s in range(1, N_DEV):
            total = total + p_ref[s, row:row + 1, off:off + 1]
        outs[4 * n][...] = total

    res = pl.pallas_call(
        body, name="adamw_small",
        out_shape=[jax.ShapeDtypeStruct(w_.shape, F32) for w_ in ws for _ in range(4)] + [jax.ShapeDtypeStruct((1, 1), F32)],
    )(packed, *ws, *ms, *vs)
    return [res[4 * j:4 * j + 4] for j in range(n)], res[4 * n]


def _adamw(name, parts, w, m, v, own=None, me=None):
    rows, cols = w.shape[-2:]
    lead = w.ndim - 2
    tr = rows if rows * cols * 4 * 8 <= (4 << 20) else max(8, (4 << 20) // (cols * 4 * 8) // 8 * 8)
    while rows % tr:
        tr -= 8
    shape4 = [jax.ShapeDtypeStruct(w.shape, F32)] * 4
    if own is None:
        def body(p_ref, *refs):
            _adam_update([p_ref[s] for s in range(N_DEV)], *refs)

        blk = pl.BlockSpec((None,) * lead + (tr, cols), lambda i: (0,) * lead + (i, 0))
        return pl.pallas_call(
            body, name=name, grid=(rows // tr,),
            in_specs=[pl.BlockSpec((N_DEV, tr, cols), lambda i: (0, i, 0)), blk, blk, blk], out_specs=[blk] * 4,
            out_shape=shape4, compiler_params=_params(("arbitrary",)),
        )(parts, w, m, v)

    def body_own(me_ref, p_ref, own_ref, *refs):
        mine = own_ref[...]
        _adam_update([jnp.where(me_ref[0] == s, mine, p_ref[s]) for s in range(N_DEV)], *refs)

    blk = pl.BlockSpec((None,) * lead + (tr, cols), lambda i, me_ref: (0,) * lead + (i, 0))
    return pl.pallas_call(
        body_own, name=name, out_shape=shape4,
        grid_spec=pltpu.PrefetchScalarGridSpec(
            num_scalar_prefetch=1, grid=(rows // tr,),
            in_specs=[pl.BlockSpec((N_DEV, tr, cols), lambda i, me_ref: (0, i, 0)),
                      pl.BlockSpec((None, tr, cols), lambda i, me_ref: (me_ref[0], i, 0)), blk, blk, blk],
            out_specs=[blk] * 4),
        compiler_params=_params(("arbitrary",)),
    )(me, parts, own, w, m, v)


def kernel(x, p, norm_mix_g, w_in, conv_w, shift_mu, w_lora_up, w0, a_lora_up, a0, g_lora_up, k_k, k_a, r_k, ln_x_g, ln_x_b, w_out, norm_mlp_g, w_up, w_down, norm_ple_g, w_ple_gate, w_ple_proj, norm_final_g, loss_target, m_norm_mix_g, m_w_in, m_conv_w, m_shift_mu, m_w_lora_up, m_w0, m_a_lora_up, m_a0, m_g_lora_up, m_k_k, m_k_a, m_r_k, m_ln_x_g, m_ln_x_b, m_w_out, m_norm_mlp_g, m_w_up, m_w_down, m_norm_ple_g, m_w_ple_gate, m_w_ple_proj, m_norm_final_g, v_norm_mix_g, v_w_in, v_conv_w, v_shift_mu, v_w_lora_up, v_w0, v_a_lora_up, v_a0, v_g_lora_up, v_k_k, v_k_a, v_r_k, v_ln_x_g, v_ln_x_b, v_w_out, v_norm_mlp_g, v_w_up, v_w_down, v_norm_ple_g, v_w_ple_gate, v_w_ple_proj, v_norm_final_g):
    args = dict(locals())
    wts = {n: args[n] for n in _WEIGHTS}
    mom = {n: args["m_" + n] for n in _WEIGHTS}
    var = {n: args["v_" + n] for n in _WEIGHTS}
    shard2d = lambda a: a.reshape(a.shape[-2:])
    pad_mu = lambda a: _pad_in_cols(jnp.concatenate([jnp.zeros((1, CONV_COLS), F32), a], axis=1))[:, CONV_COLS:]
    unpad_mu = lambda a: _unpad_in_cols(jnp.concatenate([jnp.zeros((1, CONV_COLS), F32), a], axis=1))[:, CONV_COLS:]

    shards = {n: shard2d(wts[n]).astype(BF16 if n in _BF16_GATHER else F32) for n in _SHARDED}
    w = {n: wts[n].reshape(1, -1) for n in _REPLICATED}
    w["shift_mu"] = pad_mu(wts["shift_mu"])

    loss, dx, grads, parts, in_flight = _local_step(x[0], p[0, 0], loss_target[0], w, [shards[n] for n in _EARLY],
                                                    [shards[n] for n in _LATE])

    out = {n: _adamw("adamw_" + n, parts[n], wts[n], mom[n], var[n]) for n in _LATE}
    grads["shift_mu"] = unpad_mu(grads["shift_mu"])
    flat = lambda a: a.reshape(1, -1)
    (small_parts,) = _exchange("gather_small", [_pack_small([flat(grads[n]) for n in _REPLICATED], loss)], [False])
    small, loss_total = _adamw_small(small_parts, *[[flat(d[n]) for n in _REPLICATED] for d in (wts, mom, var)])
    for n, res in zip(_REPLICATED, small):
        out[n] = [r.reshape(wts[n].shape) for r in res]

    sent, early_parts = _scatter_wait("early_scatter_wait", *in_flight, after=[dx, small_parts] + [out[n][1] for n in _LATE])
    me = (4 * lax.axis_index("x") + 2 * lax.axis_index("y") + lax.axis_index("c")).astype(jnp.int32).reshape(1)
    for n, prt, own in zip(_EARLY, early_parts, sent):
        out[n] = _adamw("adamw_" + n, prt, wts[n], mom[n], var[n], own=own, me=me)
    return (loss_total[0, 0], dx[None], *[out[n][0] for n in _WEIGHTS], *[out[n][1] for n in _WEIGHTS],
            *[out[n][2] for n in _WEIGHTS], *[out[n][3] for n in _WEIGHTS])
```

```python
import functools

import jax
import jax.numpy as jnp
from jax import lax
from jax.experimental import pallas as pl
from jax.experimental.pallas import tpu as pltpu

F32 = jnp.float32
BF16 = jnp.bfloat16

N_DEV = 8
D_MODEL = 1024
CONV_DIM = 512
RWKV_DIM = 512
HEAD_DIM = 64
N_HEADS = 8
D_FF = 4096
PLE_DIM = 256
RMS_EPS = 1e-6
GN_EPS = 64e-5
L2_EPS = 1e-12
ADAM_LR, ADAM_B1, ADAM_B2, ADAM_EPS, ADAM_WD, ADAM_STEP = 0.001, 0.9, 0.999, 1e-08, 0.01, 10

CONV_COLS = 3 * CONV_DIM
RW_PAD = 2048
IN_PAD = CONV_COLS + RW_PAD
IN_COLS = 3360
XW_OFF, XA_OFF, XG_OFF = 1536, 1664, 1792
REC_CHUNK = 128
REC_PASSES = 1
ROW_BLOCK = 256
LANE = 128
VMEM_LIMIT = 56 * 1024 * 1024


def _dims(dn, ndim):
    if ndim == 3:
        return {"nn": (((2,), (1,)), ((0,), (0,))), "nt": (((2,), (2,)), ((0,), (0,))),
                "tn": (((1,), (1,)), ((0,), (0,)))}[dn]
    return {"nn": (((1,), (0,)), ((), ())), "nt": (((1,), (1,)), ((), ())), "tn": (((0,), (0,)), ((), ()))}[dn]


def _split2(x):
    hi = x.astype(BF16)
    return hi, (x - hi.astype(F32)).astype(BF16)


def _mm_raw(x, y, dn, passes):
    f = lambda p, q: lax.dot_general(p, q, _dims(dn, x.ndim), preferred_element_type=F32)
    if passes == 1:
        return f(x.astype(BF16), y.astype(BF16))
    xh, xl = _split2(x)
    yh, yl = _split2(y)
    if passes == 2:
        return f(xh, yh) + f(xh, yl)
    return f(xh, yh) + f(xh, yl) + f(xl, yh)


@functools.partial(jax.custom_vjp, nondiff_argnums=(2, 3))
def _mm(x, y, dn, passes):
    return _mm_raw(x, y, dn, passes)


def _mm_fwd(x, y, dn, passes):
    return _mm_raw(x, y, dn, passes), (x, y)


def _mm_bwd(dn, passes, res, d):
    x, y = res
    if dn == "nn":
        return _mm(d, y, "nt", passes), _mm(x, d, "tn", passes)
    if dn == "nt":
        return _mm(d, y, "nn", passes), _mm(d, x, "tn", passes)
    return _mm(y, d, "nt", passes), _mm(x, d, "nn", passes)


_mm.defvjp(_mm_fwd, _mm_bwd)


def _head_ones():
    i = lax.broadcasted_iota(jnp.int32, (RWKV_DIM, RWKV_DIM), 0) // HEAD_DIM
    j = lax.broadcasted_iota(jnp.int32, (RWKV_DIM, RWKV_DIM), 1) // HEAD_DIM
    return (i == j).astype(BF16)


def _hsum_raw(x):
    ones = _head_ones()
    f = lambda p: lax.dot_general(p, ones, _dims("nn", 2), preferred_element_type=F32)
    x1, x2 = _split2(x)
    return f(x1) + f(x2)


@jax.custom_vjp
def _hsum(x):
    return _hsum_raw(x)


_hsum.defvjp(lambda x: (_hsum_raw(x), None), lambda _, d: (_hsum(d),))


def _sigmoid(x):
    return 0.5 + 0.5 * jnp.tanh(0.5 * x)


def _softplus(x):
    return jnp.maximum(x, 0.0) + jnp.log(1.0 + jnp.exp(-jnp.abs(x)))


def _params(sem):
    return pltpu.CompilerParams(dimension_semantics=sem, vmem_limit_bytes=VMEM_LIMIT)


def _rowwise(name, fn, rows, consts, row_outs, acc_outs=(), tr=ROW_BLOCK, halo=False, gather=()):
    rows = [r if isinstance(r, tuple) else (r, r.shape[1], 0) for r in rows]
    t_len = rows[0][0].shape[0]
    tr = min(tr, t_len)
    n_r, n_c, n_o, n_a, n_x = len(rows), len(consts), len(row_outs), len(acc_outs), len(gather)
    n_h = n_r if halo else 0
    sub = 8
    x_specs, x_shapes, x_sems = _exchange_io(gather, [False] * n_x) if n_x else ([], [], [])
    nb = t_len // tr

    def body(*refs):
        if n_x:
            n_in = n_r + n_h + n_c
            start, forward, wait = _gather_plan(refs[n_in:n_in + n_x], refs[len(refs) - 3 - n_x:len(refs) - 3], *refs[len(refs) - 3:])
            pl.when(pl.program_id(0) == 0)(start)
            refs = refs[:n_in] + refs[n_in + n_x:len(refs) - 3 - n_x]
        ins = [r[...] for r in refs[:n_r]]
        ins += [jnp.where(pl.program_id(0) == 0, 0.0, r[sub - 1:sub, :]) for r in refs[n_r:n_r + n_h]]
        ins += [r[...] for r in refs[n_r + n_h:n_r + n_h + n_c]]
        refs = refs[:n_r] + refs[n_r + n_h:]
        outs = fn(*ins)
        o_refs = refs[n_r + n_c:n_r + n_c + n_o]
        a_refs = refs[n_r + n_c + n_o:]
        for o_ref, val in zip(o_refs, outs[:n_o]):
            o_ref[...] = val.astype(o_ref.dtype)
        if n_a:
            first = pl.program_id(0) == 0

            @pl.when(first)
            def _():
                for a_ref, val in zip(a_refs, outs[n_o:]):
                    a_ref[...] = val

            @pl.when(jnp.logical_not(first))
            def _():
                for a_ref, val in zip(a_refs, outs[n_o:]):
                    a_ref[...] += val

        if n_x:
            @pl.when(pl.program_id(0) == nb - 1)
            def _():
                for j in range(n_x):
                    forward(j)
                wait()

    in_specs = [pl.BlockSpec((tr, w), functools.partial(lambda i, c: (i, c), c=cb)) for _, w, cb in rows]
    if halo:
        in_specs += [pl.BlockSpec((sub, w), functools.partial(lambda i, c: (jnp.maximum(i * (tr // sub) - 1, 0), c), c=cb))
                     for _, w, cb in rows]
    in_specs += [pl.BlockSpec(c.shape, functools.partial(lambda i, n: (0,) * n, n=c.ndim)) for c in consts]
    out_specs = [pl.BlockSpec((tr, w), lambda i: (i, 0)) for w, _ in row_outs]
    out_specs += [pl.BlockSpec(s, functools.partial(lambda i, n: (0,) * n, n=len(s))) for s in acc_outs]
    out_shape = [jax.ShapeDtypeStruct((t_len, w), dt) for w, dt in row_outs]
    out_shape += [jax.ShapeDtypeStruct(s, F32) for s in acc_outs]
    return pl.pallas_call(
        body, name=name, grid=(nb,), in_specs=in_specs + x_specs, out_specs=out_specs + x_specs,
        out_shape=out_shape + x_shapes, scratch_shapes=x_sems,
        compiler_params=pltpu.CompilerParams(dimension_semantics=("arbitrary",), vmem_limit_bytes=VMEM_LIMIT,
                                             has_side_effects=bool(n_x)),
    )(*[r[0] for r in rows], *([r[0] for r in rows] if halo else []), *consts, *gather)


def _colwise(name, fn, n_blocks, cols, prms, col_outs, prm_outs=()):
    t_len = cols[0][0].shape[0]
    n_i = len(cols) + len(prms)

    def body(*refs):
        outs = fn(*[r[...] for r in refs[:n_i]])
        for o_ref, val in zip(refs[n_i:], outs):
            o_ref[...] = val.astype(o_ref.dtype)

    spec = lambda r, w: pl.BlockSpec((r, w), lambda j: (0, j))
    in_specs = [spec(t_len, w) for _, w in cols] + [spec(a.shape[0], LANE) for a in prms]
    out_specs = [spec(t_len, bw) for _, _, bw in col_outs] + [spec(r, LANE) for r, _ in prm_outs]
    out_shape = [jax.ShapeDtypeStruct((t_len, w), dt) for w, dt, _ in col_outs]
    out_shape += [jax.ShapeDtypeStruct((r, w), F32) for r, w in prm_outs]
    return pl.pallas_call(
        body, name=name, grid=(n_blocks,), in_specs=in_specs, out_specs=out_specs, out_shape=out_shape,
        compiler_params=_params(("arbitrary",)),
    )(*[c[0] for c in cols], *prms)


def _matmul(name, a, b, dn, outs, *, tm, tn, tk, extras=(), consts=(), epilogue=None, sums=(), xch=(), xch_scatter=(),
            a_map=None, col_blocks_out=False):
    if dn == "nn":
        (m, k), n = a.shape, b.shape[1]
    elif dn == "nt":
        (m, k), n = a.shape, b.shape[0]
    else:
        (k, m), n = a.shape, b.shape[1]
    tm, tn, tk = min(tm, m), min(tn, n), min(tk, k)
    nk = k // tk
    grid = (m // tm, n // tn, nk)
    assert nk == 1 and (not sums or grid[1] == 1)
    a_spec = pl.BlockSpec((tk, tm), lambda i, j, q: (q, i)) if dn == "tn" else pl.BlockSpec((tm, tk), lambda i, j, q: (i, q))
    b_spec = pl.BlockSpec((tn, tk), lambda i, j, q: (j, q)) if dn == "nt" else pl.BlockSpec((tk, tn), lambda i, j, q: (q, j))
    o_spec = pl.BlockSpec((tm, tn), lambda i, j, q: (i, j))
    c_spec = pl.BlockSpec((1, tn), lambda i, j, q: (0, j))
    n_e, n_c, n_o, n_s, n_x = len(extras), len(consts), len(outs), len(sums), len(xch)
    x_specs, x_shapes, x_sems = _exchange_io(xch, xch_scatter) if n_x else ([], [], [])

    def body(*refs):
        a_ref, b_ref = refs[:2]
        e_refs = refs[2:2 + n_e + n_c]
        x_in = refs[2 + n_e + n_c:2 + n_e + n_c + n_x]
        rest = refs[2 + n_e + n_c + n_x:]
        o_refs, s_refs, x_out, scratch = rest[:n_o], rest[n_o:n_o + n_s], rest[n_o + n_s:n_o + n_s + n_x], rest[n_o + n_s + n_x:]
        step = (pl.program_id(0) * grid[1] + pl.program_id(1)) * nk + pl.program_id(2)
        if n_x:
            start, wait = _exchange_plan(x_in, x_out, xch_scatter, *scratch[len(scratch) - 3:])
            pl.when(step == 0)(start)
        a_blk = a_ref[...] if a_map is None else a_map(a_ref[...])
        acc = lax.dot_general(a_blk.astype(BF16), b_ref[...].astype(BF16), _dims(dn, 2), preferred_element_type=F32)
        vals = (acc,) if epilogue is None else epilogue(acc, *[e[...] for e in e_refs])
        for o_ref, val in zip(o_refs, vals[:n_o]):
            o_ref[...] = val.astype(o_ref.dtype)
        if n_s:
            @pl.when(step == 0)
            def _():
                for s_ref, val in zip(s_refs, vals[n_o:]):
                    s_ref[...] = val

            @pl.when(step > 0)
            def _():
                for s_ref, val in zip(s_refs, vals[n_o:]):
                    s_ref[...] += val

        if n_x:
            pl.when(step == grid[0] * grid[1] * nk - 1)(wait)

    plain = not (n_s or n_x)
    res = pl.pallas_call(
        body, name=name, grid=grid,
        in_specs=[a_spec, b_spec] + [o_spec] * n_e + [c_spec] * n_c + x_specs,
        out_specs=[pl.BlockSpec((None, tm, tn), lambda i, j, q: (j, i, 0)) if col_blocks_out else o_spec] * n_o
                  + [c_spec] * n_s + x_specs,
        out_shape=[jax.ShapeDtypeStruct((n // tn, m, tn) if col_blocks_out else (m, n), dt) for dt in outs] + [jax.ShapeDtypeStruct(s, F32) for s in sums] + x_shapes,
        scratch_shapes=x_sems,
        compiler_params=pltpu.CompilerParams(
            dimension_semantics=("parallel", "parallel", "arbitrary") if plain else ("arbitrary",) * 3,
            vmem_limit_bytes=VMEM_LIMIT, has_side_effects=bool(n_x)),
    )(a, b, *extras, *consts, *xch)
    return res[0] if len(res) == 1 else res


def _rms(h, g):
    return h * lax.rsqrt(jnp.mean(h * h, axis=-1, keepdims=True) + RMS_EPS) * g


def _rms_bwd(h, g, dy):
    rs = lax.rsqrt(jnp.mean(h * h, axis=-1, keepdims=True) + RMS_EPS)
    n = h * rs
    dn = dy * g
    dh = rs * (dn - n * jnp.mean(dn * n, axis=-1, keepdims=True))
    return dh, jnp.sum(dy * n, axis=0, keepdims=True)


def _rwkv_pre(k, xw, xa, xg, w0, a0, k_k, k_a, wl, al, gl):
    zw = w0 + _mm(jnp.tanh(xw), wl, "nn", 1)
    lw = -jnp.exp(-_softplus(-zw) - 0.5)
    iclr = _sigmoid(a0 + _mm(xa, al, "nn", 1))
    g = _mm(_sigmoid(xg), gl, "nn", 1)
    kk0 = k * k_k
    kk = kk0 * lax.rsqrt(jnp.maximum(_hsum(kk0 * kk0), L2_EPS * L2_EPS))
    k_h = k * (1.0 + (iclr - 1.0) * k_a)
    return lw, k_h, -kk, kk * iclr, g


def _rwkv_post(y, r, k_h, v, g, ln_g, ln_b, r_k):
    mu = _hsum(y) * (1.0 / HEAD_DIM)
    yc = y - mu
    var = _hsum(yc * yc) * (1.0 / HEAD_DIM)
    yo = yc * lax.rsqrt(var + GN_EPS) * ln_g + ln_b
    bonus = _hsum(r * k_h * r_k) * v
    return (yo + bonus) * g


def _shift_down(x, n):
    rows = lax.broadcasted_iota(jnp.int32, x.shape, 0)
    return jnp.where(rows < n, 0.0, pltpu.roll(x, n, 0))


def _shift_up(x, n):
    t_len = x.shape[0]
    rows = lax.broadcasted_iota(jnp.int32, x.shape, 0)
    return jnp.where(rows >= t_len - n, 0.0, pltpu.roll(x, t_len - n, 0))


def _exchange_plan(ins, outs, scatter, send_sems, recv_sems, local_sems):
    x, y, c = lax.axis_index("x"), lax.axis_index("y"), lax.axis_index("c")
    me = 4 * x + 2 * y + c

    def local(i):
        return pltpu.make_async_copy(ins[i].at[me] if scatter[i] else ins[i], outs[i].at[me], local_sems.at[i])

    def send(i, rel):
        return pltpu.make_async_remote_copy(
            src_ref=ins[i].at[me ^ rel] if scatter[i] else ins[i], dst_ref=outs[i].at[me],
            send_sem=send_sems.at[i, rel - 1], recv_sem=recv_sems.at[i, rel - 1],
            device_id=(x ^ (rel >> 2), y ^ ((rel >> 1) & 1), c ^ (rel & 1)), device_id_type=pl.DeviceIdType.MESH)

    def landed(i, rel):
        slot = outs[i].at[me ^ rel]
        return pltpu.make_async_remote_copy(
            src_ref=slot, dst_ref=slot, send_sem=send_sems.at[i, rel - 1], recv_sem=recv_sems.at[i, rel - 1],
            device_id=(x, y, c), device_id_type=pl.DeviceIdType.MESH)

    def start():
        for i in range(len(ins)):
            local(i).start()
            for rel in range(1, N_DEV):
                send(i, rel).start()

    def wait():
        for i in range(len(ins)):
            local(i).wait()
            for rel in range(1, N_DEV):
                landed(i, rel).wait_recv()
            for rel in range(1, N_DEV):
                send(i, rel).wait_send()

    return start, wait


def _gather_plan(ins, outs, send_sems, recv_sems, local_sems):
    x, y, c = lax.axis_index("x"), lax.axis_index("y"), lax.axis_index("c")
    me = 4 * x + 2 * y + c
    direct, chips = (1, 2, 4, 6), (2, 4, 6)

    def local(i):
        return pltpu.make_async_copy(ins[i], outs[i].at[me], local_sems.at[i])

    def send(i, rel):
        return pltpu.make_async_remote_copy(
            src_ref=ins[i], dst_ref=outs[i].at[me], send_sem=send_sems.at[i, rel - 1], recv_sem=recv_sems.at[i, rel - 1],
            device_id=(x ^ (rel >> 2), y ^ ((rel >> 1) & 1), c ^ (rel & 1)), device_id_type=pl.DeviceIdType.MESH)

    def passed(i, rel):
        slot = outs[i].at[me ^ rel]
        return pltpu.make_async_remote_copy(
            src_ref=slot, dst_ref=slot, send_sem=send_sems.at[i, rel], recv_sem=recv_sems.at[i, rel],
            device_id=(x, y, 1 - c), device_id_type=pl.DeviceIdType.MESH)

    def landed(i, rel):
        slot = outs[i].at[me ^ rel]
        return pltpu.make_async_remote_copy(
            src_ref=slot, dst_ref=slot, send_sem=send_sems.at[i, rel - 1], recv_sem=recv_sems.at[i, rel - 1],
            device_id=(x, y, c), device_id_type=pl.DeviceIdType.MESH)

    def start():
        for i in range(len(ins)):
            local(i).start()
            for rel in direct:
                send(i, rel).start()

    def forward(i):
        for rel in chips:
            landed(i, rel).wait_recv()
            passed(i, rel).start()

    def wait():
        for i in range(len(ins)):
            local(i).wait()
            for rel in (1, 3, 5, 7):
                landed(i, rel).wait_recv()
            for rel in direct:
                send(i, rel).wait_send()
            for rel in chips:
                passed(i, rel).wait_send()

    return start, forward, wait


def _exchange_io(arrays, scatter):
    n = len(arrays)
    any_spec = pl.BlockSpec(memory_space=pl.ANY)
    out_shape = [jax.ShapeDtypeStruct(a.shape if sc else (N_DEV,) + a.shape, a.dtype) for a, sc in zip(arrays, scatter)]
    sems = [pltpu.SemaphoreType.DMA((n, N_DEV - 1)), pltpu.SemaphoreType.DMA((n, N_DEV - 1)), pltpu.SemaphoreType.DMA((n,))]
    return [any_spec] * n, out_shape, sems


def _exchange(name, arrays, scatter):
    n = len(arrays)
    specs, out_shape, sems = _exchange_io(arrays, scatter)

    def body(*refs):
        if any(scatter):
            start, wait = _exchange_plan(refs[:n], refs[n:2 * n], scatter, *refs[2 * n:])
            start()
        else:
            start, forward, wait = _gather_plan(refs[:n], refs[n:2 * n], *refs[2 * n:])
            start()
            for i in range(n):
                forward(i)
        wait()

    return pl.pallas_call(
        body, name=name, in_specs=specs, out_specs=specs, out_shape=out_shape, scratch_shapes=sems,
        compiler_params=pltpu.CompilerParams(has_side_effects=True),
    )(*arrays)


def _scatter_start(name, arrays, lands):
    n = len(arrays)
    hbm = pl.BlockSpec(memory_space=pltpu.HBM)

    def body(*refs):
        ins, land, send_sems, recv_sems = refs[:n], refs[n:2 * n], refs[2 * n], refs[2 * n + 1]
        token = refs[4 * n + 2]
        x, y, c = lax.axis_index("x"), lax.axis_index("y"), lax.axis_index("c")
        me = 4 * x + 2 * y + c
        for i in range(n):
            for rel in range(1, N_DEV):
                k = i * (N_DEV - 1) + rel - 1
                pltpu.make_async_remote_copy(
                    src_ref=ins[i].at[me ^ rel], dst_ref=land[i].at[me], send_sem=send_sems.at[k],
                    recv_sem=recv_sems.at[k], device_id=(x ^ (rel >> 2), y ^ ((rel >> 1) & 1), c ^ (rel & 1)),
                    device_id_type=pl.DeviceIdType.MESH).start()
        token[...] = jnp.zeros_like(token)

    sem = pltpu.SemaphoreType.DMA((n * (N_DEV - 1),))
    bufs = [pltpu.HBM(a.shape, a.dtype) for a in list(arrays) + list(lands)]
    res = pl.pallas_call(
        body, name=name, out_shape=(sem, sem, *bufs, jax.ShapeDtypeStruct((8, LANE), F32)),
        in_specs=[hbm] * (2 * n),
        out_specs=(pl.BlockSpec(memory_space=pltpu.SEMAPHORE),) * 2 + (hbm,) * (2 * n) + (pl.BlockSpec(memory_space=pltpu.VMEM),),
        input_output_aliases={i: 2 + i for i in range(2 * n)},
        compiler_params=pltpu.CompilerParams(has_side_effects=pltpu.SideEffectType.DATAFLOW_SIDE_EFFECTING),
    )(*[pltpu.with_memory_space_constraint(a, pltpu.HBM) for a in list(arrays) + list(lands)])
    return res[0], res[1], res[2:2 + n], res[2 + n:2 + 2 * n], res[2 + 2 * n]


def _scatter_wait(name, send_sems, recv_sems, arrays, lands, after):
    n, n_after = len(arrays), len(after)
    hbm = pl.BlockSpec(memory_space=pltpu.HBM)

    def body(*refs):
        ins, land, s_sems, r_sems = refs[:n], refs[n:2 * n], refs[2 * n], refs[2 * n + 1]
        x, y, c = lax.axis_index("x"), lax.axis_index("y"), lax.axis_index("c")
        me = 4 * x + 2 * y + c
        for i in range(n):
            for rel in range(1, N_DEV):
                k = i * (N_DEV - 1) + rel - 1
                cp = pltpu.make_async_remote_copy(
                    src_ref=ins[i].at[me ^ rel], dst_ref=land[i].at[me ^ rel], send_sem=s_sems.at[k],
                    recv_sem=r_sems.at[k], device_id=(x, y, c), device_id_type=pl.DeviceIdType.MESH)
                cp.wait_send()
                cp.wait_recv()

    res = pl.pallas_call(
        body, name=name, out_shape=[pltpu.HBM(a.shape, a.dtype) for a in list(arrays) + list(lands)],
        in_specs=[hbm] * (2 * n) + [pl.BlockSpec(memory_space=pltpu.SEMAPHORE)] * 2 + [pl.BlockSpec(memory_space=pl.ANY)] * n_after,
        out_specs=[hbm] * (2 * n), input_output_aliases={i: i for i in range(2 * n)},
        compiler_params=pltpu.CompilerParams(has_side_effects=pltpu.SideEffectType.DATAFLOW_SIDE_EFFECTING),
    )(*arrays, *lands, send_sems, recv_sems, *after)
    return res[:n], res[n:]


def _tri_powers(low):
    powers, n = [low], 1
    while 2 * n < low.shape[-1]:
        powers.append(_mm(powers[-1], powers[-1], "nn", REC_PASSES))
        n *= 2
    return powers


@jax.custom_vjp
def _tri_solve(low, rhs):
    for p in _tri_powers(low):
        rhs = rhs + _mm(p, rhs, "nn", REC_PASSES)
    return rhs


def _tri_solve_fwd(low, rhs):
    powers = _tri_powers(low)
    for p in powers:
        rhs = rhs + _mm(p, rhs, "nn", REC_PASSES)
    return rhs, (powers, rhs)


def _tri_solve_bwd(res, d):
    powers, u = res
    for p in powers:
        d = d + _mm(p, d, "tn", REC_PASSES)
    return _mm(d, u, "nt", REC_PASSES), d


_tri_solve.defvjp(_tri_solve_fwd, _tri_solve_bwd)


def _chunk_fwd(z0, r, lw, k, v, a, b):
    n_h, c, n_k = r.shape
    mm = functools.partial(_mm, passes=REC_PASSES)
    gram = functools.partial(_mm, passes=2)
    ti = lax.broadcasted_iota(jnp.int32, (c, c), 0)
    si = lax.broadcasted_iota(jnp.int32, (c, c), 1)
    strict, incl = si < ti, si <= ti
    cum = _mm(jnp.broadcast_to(incl.astype(F32), (n_h, c, c)), lw, "nn", 3)
    cum_end = cum[:, c - 1:c, :]
    e_neg, e_end = jnp.exp(-cum), jnp.exp(cum_end - cum)
    x2 = jnp.concatenate([a * jnp.exp(cum - lw), r * jnp.exp(cum)], axis=1)
    y2 = jnp.concatenate([b * e_neg, k * e_neg], axis=1)
    mask = jnp.concatenate([jnp.concatenate([strict, strict], axis=1), jnp.concatenate([incl, incl], axis=1)], axis=0)
    g2 = jnp.where(mask, gram(x2, y2, "nt"), 0.0)
    t2 = mm(x2, z0, "nn") + mm(g2[:, :, c:], v, "nn")
    u = _tri_solve(g2[:, :c, :c], t2[:, :c])
    y = t2[:, c:] + mm(g2[:, c:, :c], u, "nn")
    ki = lax.broadcasted_iota(jnp.int32, (n_k, n_k), 0)
    kj = lax.broadcasted_iota(jnp.int32, (n_k, n_k), 1)
    dmat = jnp.where(ki == kj, jnp.broadcast_to(jnp.exp(cum_end), (n_h, n_k, n_k)), 0.0)
    z_end = mm(dmat, z0, "nn") + mm(jnp.concatenate([b * e_end, k * e_end], axis=1), jnp.concatenate([u, v], axis=1), "tn")
    return y, z_end


def _heads(x):
    return jnp.stack([x[:, h * HEAD_DIM:(h + 1) * HEAD_DIM] for h in range(N_HEADS)])


def _unheads(x):
    return jnp.concatenate([x[h] for h in range(N_HEADS)], axis=-1)


def _rec_params():
    return pltpu.CompilerParams(dimension_semantics=("arbitrary",), vmem_limit_bytes=VMEM_LIMIT, has_side_effects=True)


def _rec_fwd(u, lw, k, a, b, xch):
    t_len = lw.shape[0]
    c = min(REC_CHUNK, t_len)
    nc = t_len // c
    n_x = len(xch)
    x_specs, x_shapes, x_sems = _exchange_io(xch, [False] * n_x)
    sizes = [a_.size * a_.dtype.itemsize for a_ in xch]
    pass_step = [min(nc - 1, int(0.9 * nc * sum(sizes[:j + 1]) / sum(sizes)) + 1) for j in range(n_x)]

    def body(*refs):
        r_ref, v_ref, lw_ref, k_ref, a_ref, b_ref = refs[:6]
        x_in = refs[6:6 + n_x]
        y_ref, zs_ref = refs[6 + n_x:8 + n_x]
        x_out = refs[8 + n_x:8 + 2 * n_x]
        z_scr = refs[8 + 2 * n_x]
        start, forward, wait = _gather_plan(x_in, x_out, *refs[9 + 2 * n_x:])
        i = pl.program_id(0)

        @pl.when(i == 0)
        def _():
            start()
            z_scr[...] = jnp.zeros_like(z_scr)

        z0 = z_scr[...]
        zs_ref[0] = z0
        y, z_end = _chunk_fwd(z0, _heads(r_ref[...]), _heads(lw_ref[...]), _heads(k_ref[...]), _heads(v_ref[...]),
                              _heads(a_ref[...]), _heads(b_ref[...]))
        y_ref[...] = _unheads(y)
        z_scr[...] = z_end

        for j in range(n_x):
            pl.when(i == pass_step[j])(functools.partial(forward, j))

        @pl.when(i == nc - 1)
        def _():
            wait()

    blk = lambda cb: pl.BlockSpec((c, RWKV_DIM), functools.partial(lambda i, q: (i, q), q=cb))
    res = pl.pallas_call(
        body, name="rwkv_rec_fwd", grid=(nc,),
        in_specs=[blk(0), blk(2)] + [blk(0)] * 4 + x_specs,
        out_specs=[blk(0), pl.BlockSpec((1, N_HEADS, HEAD_DIM, HEAD_DIM), lambda i: (i, 0, 0, 0))] + x_specs,
        out_shape=[jax.ShapeDtypeStruct((t_len, RWKV_DIM), F32),
                   jax.ShapeDtypeStruct((nc, N_HEADS, HEAD_DIM, HEAD_DIM), F32)] + x_shapes,
        scratch_shapes=[pltpu.VMEM((N_HEADS, HEAD_DIM, HEAD_DIM), F32)] + x_sems,
        compiler_params=_rec_params(),
    )(u, u, lw, k, a, b, *xch)
    return res[0], res[1], res[2:]


def _rec_bwd(u, lw, k, a, b, zs, dy, xch, xch_scatter):
    t_len = lw.shape[0]
    c = min(REC_CHUNK, t_len)
    nc = t_len // c
    n_x = len(xch)
    x_specs, x_shapes, x_sems = _exchange_io(xch, xch_scatter)

    def body(*refs):
        r_ref, v_ref, lw_ref, k_ref, a_ref, b_ref, zs_ref, dy_ref = refs[:8]
        x_in = refs[8:8 + n_x]
        g_refs = refs[8 + n_x:14 + n_x]
        x_out = refs[14 + n_x:14 + 2 * n_x]
        dz_scr = refs[14 + 2 * n_x]
        start, wait = _exchange_plan(x_in, x_out, xch_scatter, *refs[15 + 2 * n_x:])
        i = pl.program_id(0)

        @pl.when(i == 0)
        def _():
            start()
            dz_scr[...] = jnp.zeros_like(dz_scr)

        _, vjp = jax.vjp(_chunk_fwd, zs_ref[0], _heads(r_ref[...]), _heads(lw_ref[...]), _heads(k_ref[...]),
                         _heads(v_ref[...]), _heads(a_ref[...]), _heads(b_ref[...]))
        dz0, dr, dlw, dk, dv, da, db = vjp((_heads(dy_ref[...]), dz_scr[...]))
        for ref, val in zip(g_refs, (dr, dv, dlw, dk, da, db)):
            ref[...] = _unheads(val)
        dz_scr[...] = dz0

        @pl.when(i == nc - 1)
        def _():
            wait()

    blk = lambda cb: pl.BlockSpec((c, RWKV_DIM), functools.partial(lambda i, q: (nc - 1 - i, q), q=cb))
    res = pl.pallas_call(
        body, name="rwkv_rec_bwd", grid=(nc,),
        in_specs=[blk(0), blk(2)] + [blk(0)] * 4
                 + [pl.BlockSpec((1, N_HEADS, HEAD_DIM, HEAD_DIM), lambda i: (nc - 1 - i, 0, 0, 0)), blk(0)] + x_specs,
        out_specs=[blk(0)] * 6 + x_specs,
        out_shape=[jax.ShapeDtypeStruct((t_len, RWKV_DIM), F32)] * 6 + x_shapes,
        scratch_shapes=[pltpu.VMEM((N_HEADS, HEAD_DIM, HEAD_DIM), F32)] + x_sems,
        compiler_params=_rec_params(),
    )(u, u, lw, k, a, b, zs, dy, *xch)
    return res[:6], res[6:]


_EARLY = ["w_in", "conv_w", "w_lora_up", "a_lora_up", "g_lora_up"]
_LATE = ["w_out", "w_up", "w_down", "w_ple_gate", "w_ple_proj"]
_SHARDED = _EARLY + _LATE
_COL_SHARDED = {"w_in", "conv_w", "w_lora_up", "a_lora_up", "g_lora_up", "w_up", "w_ple_proj"}
_BF16_GATHER = {"w_in", "w_out", "w_up", "w_down", "w_ple_gate", "w_ple_proj"}
_REPLICATED = ["norm_mix_g", "shift_mu", "w0", "a0", "k_k", "k_a", "r_k", "ln_x_g", "ln_x_b", "norm_mlp_g", "norm_ple_g",
               "norm_final_g"]
_WEIGHTS = ["norm_mix_g", "w_in", "conv_w", "shift_mu", "w_lora_up", "w0", "a_lora_up", "a0", "g_lora_up", "k_k", "k_a", "r_k",
            "ln_x_g", "ln_x_b", "w_out", "norm_mlp_g", "w_up", "w_down", "norm_ple_g", "w_ple_gate", "w_ple_proj", "norm_final_g"]


def _unshard(name, g):
    if name in _COL_SHARDED:
        return jnp.moveaxis(g, 0, 1).reshape(g.shape[1], N_DEV * g.shape[2])
    return g.reshape(N_DEV * g.shape[1], g.shape[2])


def _reshard(name, full):
    if name in _COL_SHARDED:
        return jnp.moveaxis(full.reshape(full.shape[0], N_DEV, full.shape[1] // N_DEV), 1, 0)
    return full.reshape(N_DEV, full.shape[0] // N_DEV, full.shape[1])


def _pad_in_cols(a):
    z = lambda n: jnp.zeros(a.shape[:-1] + (n,), a.dtype)
    conv = [a[..., part * CONV_DIM + j * LANE:part * CONV_DIM + (j + 1) * LANE] for j in range(CONV_DIM // LANE) for part in range(3)]
    return jnp.concatenate(conv + [a[..., CONV_COLS:3136], z(64), a[..., 3136:3200], z(64), a[..., 3200:3360], z(96)], axis=-1)


def _unpad_in_cols(a):
    conv = [a[..., (3 * j + part) * LANE:(3 * j + part + 1) * LANE] for part in range(3) for j in range(CONV_DIM // LANE)]
    return jnp.concatenate(conv + [a[..., CONV_COLS:3136], a[..., 3200:3264], a[..., 3328:3488]], axis=-1)


def _assemble_w_in(g):
    n_dev, rows, cols = g.shape

    def body(g_ref, o_ref):
        o_ref[...] = _pad_in_cols(jnp.concatenate([g_ref[d] for d in range(n_dev)], axis=1))

    return pl.pallas_call(
        body, name="w_in_assemble", grid=(rows // ROW_BLOCK,),
        in_specs=[pl.BlockSpec((n_dev, ROW_BLOCK, cols), lambda i: (0, i, 0))],
        out_specs=pl.BlockSpec((ROW_BLOCK, IN_PAD), lambda i: (i, 0)),
        out_shape=jax.ShapeDtypeStruct((rows, IN_PAD), g.dtype), compiler_params=_params(("arbitrary",)),
    )(g)


def _split_w_in_grad(dw):
    rows = dw.shape[0]
    cols = IN_COLS // N_DEV

    def body(d_ref, o_ref):
        full = _unpad_in_cols(d_ref[...])
        for d in range(N_DEV):
            o_ref[d] = full[:, cols * d:cols * (d + 1)]

    return pl.pallas_call(
        body, name="w_in_grad_split", grid=(rows // ROW_BLOCK,),
        in_specs=[pl.BlockSpec((ROW_BLOCK, IN_PAD), lambda i: (i, 0))],
        out_specs=pl.BlockSpec((N_DEV, ROW_BLOCK, cols), lambda i: (0, i, 0)),
        out_shape=jax.ShapeDtypeStruct((N_DEV, rows, cols), dw.dtype), compiler_params=_params(("arbitrary",)),
    )(dw)


def _pad_rows(a, rows):
    return jnp.concatenate([a, jnp.zeros((rows - a.shape[0],) + a.shape[1:], a.dtype)], axis=0)


SEG_W = [RWKV_DIM, RWKV_DIM, RWKV_DIM, LANE, LANE, 2 * LANE]
SEG_OFF = [0, 512, 1024, XW_OFF, XA_OFF, XG_OFF]


def _rwkv_pre_bwd(proj, u, grads, mu, small, dproj):
    t_len = u.shape[0]
    tr = min(ROW_BLOCK, t_len)
    nb = t_len // tr
    sub = 8
    n_g = len(grads)
    acc_shapes = [(1, RW_PAD)] + [(1, RWKV_DIM)] * 4 + [(LANE, RWKV_DIM), (LANE, RWKV_DIM), (2 * LANE, RWKV_DIM)]

    def body(*refs):
        seg_refs, halo_refs = refs[:6], refs[6:12]
        k_ref, xw_ref, xa_ref, xg_ref = refs[12:16]
        g_refs = refs[16:16 + n_g]
        mu_ref = refs[16 + n_g]
        prm_refs = refs[17 + n_g:24 + n_g]
        out_hbm = refs[25 + n_g]
        acc_refs = refs[26 + n_g:26 + n_g + len(acc_shapes)]
        vbuf, sems, carry = refs[26 + n_g + len(acc_shapes):]
        i = pl.program_id(0)
        blk = nb - 1 - i
        dr1, dr2, dv1, dv2, dlw, dk1, dk2, da, db, dg = [g[...] for g in g_refs]
        _, vjp = jax.vjp(_rwkv_pre, k_ref[...], xw_ref[...], xa_ref[...], xg_ref[...], *[p_[...] for p_ in prm_refs])
        dk, dxw, dxa, dxg, *dprm = vjp((dlw, dk1 + dk2, da, db, dg))
        du = jnp.concatenate([dr1 + dr2, dk, dv1 + dv2, dxw, dxa, dxg], axis=1)
        mu_v = mu_ref[...]

        @pl.when(i == 0)
        def _():
            carry[...] = jnp.zeros_like(carry)

        rows = lax.broadcasted_iota(jnp.int32, du.shape, 0)
        nxt = jnp.where(rows == tr - 1, carry[...], pltpu.roll(du, tr - 1, 0))
        d_rw = du - mu_v * du + mu_v * nxt
        d_mu = []
        for s_ref, h_ref, off, wd in zip(seg_refs, halo_refs, SEG_OFF, SEG_W):
            cur = s_ref[...]
            r0 = lax.broadcasted_iota(jnp.int32, cur.shape, 0)
            prev = jnp.where(r0 == 0, jnp.where(blk == 0, 0.0, h_ref[sub - 1:sub, :]), pltpu.roll(cur, 1, 0))
            d_mu.append(jnp.sum(du[:, off:off + wd] * (prev - cur), axis=0, keepdims=True))
        sums = [jnp.concatenate(d_mu, axis=1)] + list(dprm)

        @pl.when(i == 0)
        def _():
            for a_ref, val in zip(acc_refs, sums):
                a_ref[...] = val

        @pl.when(i > 0)
        def _():
            for a_ref, val in zip(acc_refs, sums):
                a_ref[...] += val

        carry[...] = du[0:1, :]
        slot = i % 2

        def writeback(s, b):
            return pltpu.make_async_copy(vbuf.at[s], out_hbm.at[pl.ds(b * tr, tr), pl.ds(CONV_COLS, RW_PAD)], sems.at[s])

        @pl.when(i >= 2)
        def _():
            writeback(slot, blk + 2).wait()

        vbuf[slot] = d_rw.astype(vbuf.dtype)
        writeback(slot, blk).start()

        @pl.when(i == nb - 1)
        def _():
            writeback(slot, blk).wait()
            if nb > 1:
                writeback(1 - slot, blk + 1).wait()

    rev = lambda w_, cb: pl.BlockSpec((tr, w_), functools.partial(lambda i, c: (nb - 1 - i, c), c=cb))
    halo = lambda w_, cb: pl.BlockSpec((sub, w_), functools.partial(
        lambda i, c: (jnp.maximum((nb - 1 - i) * (tr // sub) - 1, 0), c), c=cb))
    whole = lambda a: pl.BlockSpec(a.shape, functools.partial(lambda i, n: (0,) * n, n=a.ndim))
    segs = [(wd, (CONV_COLS + off) // wd) for off, wd in zip(SEG_OFF, SEG_W)]
    u_cols = [(512, 1), (LANE, XW_OFF // LANE), (LANE, XA_OFF // LANE), (2 * LANE, XG_OFF // (2 * LANE))]
    any_spec = pl.BlockSpec(memory_space=pl.ANY)
    res = pl.pallas_call(
        body, name="rwkv_pre_bwd", grid=(nb,),
        in_specs=[rev(*s) for s in segs] + [halo(*s) for s in segs] + [rev(*c) for c in u_cols]
                 + [rev(RWKV_DIM, 0)] * n_g + [whole(mu)] + [whole(p_) for p_ in small] + [any_spec],
        out_specs=[any_spec] + [pl.BlockSpec(s, functools.partial(lambda i, n: (0,) * n, n=len(s))) for s in acc_shapes],
        out_shape=[jax.ShapeDtypeStruct(dproj.shape, dproj.dtype)] + [jax.ShapeDtypeStruct(s, F32) for s in acc_shapes],
        scratch_shapes=[pltpu.VMEM((2, tr, RW_PAD), dproj.dtype), pltpu.SemaphoreType.DMA((2,)), pltpu.VMEM((1, RW_PAD), F32)],
        input_output_aliases={24 + n_g: 0},
        compiler_params=_params(("arbitrary",)),
    )(*[proj] * 12, *[u] * 4, *grads, mu, *small, dproj)
    return res


def _local_step(x, p, tgt, w, early_shards, late_shards):
    row = lambda v: v.reshape(1, -1)
    w = dict(w)

    xn1, *gathered = _rowwise("rms_mix", lambda h, g: (_rms(h, g),), [x], [w["norm_mix_g"]], [(D_MODEL, BF16)],
                              gather=early_shards)
    w.update({n: _unshard(n, g_) for n, g_ in zip(_EARLY[1:], gathered[1:])})
    w["w_in"] = _assemble_w_in(gathered[0])
    w["w_lora_up"] = _pad_rows(w["w_lora_up"], LANE)
    w["a_lora_up"] = _pad_rows(w["a_lora_up"], LANE)
    w["g_lora_up"] = _pad_rows(w["g_lora_up"], 2 * LANE)
    proj = _matmul("in_proj", xn1, w["w_in"], "nn", [F32], tm=2048, tn=512, tk=D_MODEL)
    n_cb = CONV_DIM // LANE

    def conv_fwd(blk, cw):
        gb, gc, hx = blk[:, :LANE], blk[:, LANE:2 * LANE], blk[:, 2 * LANE:]
        uu = gc * hx
        return (gb * (uu * cw[2:3] + _shift_down(uu, 1) * cw[1:2] + _shift_down(uu, 2) * cw[0:1]),)

    (y_conv,) = _colwise("conv_fwd", conv_fwd, n_cb, [(proj, 3 * LANE)], [w["conv_w"]], [(CONV_DIM, BF16, LANE)])

    small = [w["w0"], w["a0"], w["k_k"], w["k_a"], w["w_lora_up"], w["a_lora_up"], w["g_lora_up"]]
    def pre_fwd(*xs):
        cur, prev_rows, mu, prm = xs[:6], xs[6:12], xs[12], xs[13:]
        segs = []
        for c_, p_, off, wd in zip(cur, prev_rows, SEG_OFF, SEG_W):
            rows = lax.broadcasted_iota(jnp.int32, c_.shape, 0)
            prev = jnp.where(rows == 0, p_, pltpu.roll(c_, 1, 0))
            segs.append(c_ + mu[:, off:off + wd] * (prev - c_))
        return (jnp.concatenate(segs, axis=1),) + tuple(_rwkv_pre(segs[1], segs[3], segs[4], segs[5], *prm))

    proj_segs = [(proj, wd, (CONV_COLS + off) // wd) for off, wd in zip(SEG_OFF, SEG_W)]
    u, lw, k_h, ra, rb, g = _rowwise("rwkv_pre", pre_fwd, proj_segs, [w["shift_mu"]] + small,
                                     [(RW_PAD, F32)] + [(RWKV_DIM, F32)] * 5, halo=True)
    u_k, u_xw, u_xa, u_xg = (u, 512, 1), (u, LANE, XW_OFF // LANE), (u, LANE, XA_OFF // LANE), (u, 2 * LANE, XG_OFF // (2 * LANE))
    y_rec, zs, late = _rec_fwd(u, lw, k_h, ra, rb, late_shards)
    for n, gathered in zip(_LATE, late):
        w[n] = _unshard(n, gathered)
    post_c = [w["ln_x_g"], w["ln_x_b"], w["r_k"]]
    u_r, u_v = (u, 512, 0), (u, 512, 2)
    (y_rwkv,) = _rowwise("rwkv_post", lambda *xs: (_rwkv_post(*xs),), [y_rec, u_r, k_h, u_v, g], post_c, [(RWKV_DIM, BF16)],
                         tr=2 * ROW_BLOCK)
    ycat = jnp.concatenate([y_conv, y_rwkv], axis=1)
    def res_norm(acc, r_, g_):
        h = acc + r_
        return h, _rms(h, g_)

    h1, xn2 = _matmul("out_proj", ycat, w["w_out"], "nn", [F32, BF16], tm=1024, tn=D_MODEL, tk=D_MODEL, extras=[x],
                      consts=[w["norm_mlp_g"]], epilogue=res_norm)

    square = lambda h: h.astype(F32) * h.astype(F32)
    hid = _matmul("mlp_up", xn2, w["w_up"], "nn", [BF16], tm=2048, tn=1024, tk=D_MODEL,
                  epilogue=lambda acc: (jnp.maximum(acc, 0.0),))
    h2, xn3 = _matmul("mlp_down", hid, w["w_down"], "nn", [F32, BF16], tm=512, tn=D_MODEL, tk=D_FF, extras=[h1],
                      consts=[w["norm_ple_g"]], epilogue=res_norm, a_map=square)
    zg = _matmul("ple_gate", xn3, w["w_ple_gate"], "nn", [F32], tm=1024, tn=1024, tk=D_MODEL)
    pp = _matmul("ple_proj", p, w["w_ple_proj"], "nn", [F32], tm=1024, tn=1024, tk=PLE_DIM)

    def head(h2_, zg_, pp_, tg, gf):
        gate = _sigmoid(zg_)
        h3 = h2_ + gate * pp_
        out = _rms(h3, gf)
        err = out - tg
        dh3, dgf = _rms_bwd(h3, gf, err * (1.0 / D_MODEL))
        loss = jnp.sum(jnp.sum(err * err, axis=1, keepdims=True), axis=0, keepdims=True) * (0.5 / D_MODEL)
        return dh3, dh3 * pp_ * gate * (1.0 - gate), dh3 * gate, dgf, loss

    dh3, dzg, dpp, d_norm_final, loss = _rowwise(
        "head", head, [h2, zg, pp, tgt], [row(w["norm_final_g"])], [(D_MODEL, F32), (D_MODEL, BF16), (D_MODEL, BF16)],
        [(1, D_MODEL), (1, 1)], tr=2 * ROW_BLOCK)

    d_w_ple_proj = _matmul("d_ple_proj", p, dpp, "tn", [BF16], tm=PLE_DIM, tn=D_MODEL // N_DEV, tk=4096, col_blocks_out=True)
    d_w_ple_gate = _matmul("d_ple_gate", xn3, dzg, "tn", [BF16], tm=512, tn=1024, tk=4096)

    def norm_bwd(dxn, h, dres, g_):
        dh, dg = _rms_bwd(h, g_, dxn)
        dh = dh + dres
        return dh, dh, dg

    nb = dict(tm=512, tn=D_MODEL, epilogue=norm_bwd, sums=[(1, D_MODEL)])
    dh2, dh2_b, d_norm_ple = _matmul("dx_ple_gate", dzg, w["w_ple_gate"], "nt", [F32, BF16], tk=D_MODEL,
                                     extras=[h2, dh3], consts=[w["norm_ple_g"]], **nb)
    d_w_down = _matmul("d_mlp_down", hid, dh2_b, "tn", [BF16], tm=512, tn=1024, tk=4096, a_map=square)
    dpre = _matmul("dx_mlp_down", dh2_b, w["w_down"], "nt", [BF16], tm=2048, tn=1024, tk=D_MODEL, extras=[hid],
                   epilogue=lambda acc, hid_: (acc * (2.0 * hid_.astype(F32)),))
    d_w_up = _matmul("d_mlp_up", xn2, dpre, "tn", [BF16], tm=1024, tn=D_FF // N_DEV, tk=4096, col_blocks_out=True)
    dh1, dh1_b, d_norm_mlp = _matmul("dx_mlp_up", dpre, w["w_up"], "nt", [F32, BF16], tk=D_FF,
                                     extras=[h1, dh2], consts=[w["norm_mlp_g"]], **nb)
    d_w_out = _matmul("d_out_proj", ycat, dh1_b, "tn", [BF16], tm=512, tn=1024, tk=4096)
    dycat = _matmul("dx_out_proj", dh1_b, w["w_out"], "nt", [F32], tm=1024, tn=1024, tk=D_MODEL)
    late_grads = dict(w_out=d_w_out, w_up=d_w_up, w_down=d_w_down, w_ple_gate=d_w_ple_gate, w_ple_proj=d_w_ple_proj)

    def conv_bwd(dy, blk, cw):
        gb, gc, hx = blk[:, :LANE], blk[:, LANE:2 * LANE], blk[:, 2 * LANE:]
        uu = gc * hx
        u1, u2 = _shift_down(uu, 1), _shift_down(uu, 2)
        dconv = dy * gb
        du = dconv * cw[2:3] + _shift_up(dconv, 1) * cw[1:2] + _shift_up(dconv, 2) * cw[0:1]
        s = lambda z: jnp.sum(z, axis=0, keepdims=True)
        d_blk = jnp.concatenate([dy * (uu * cw[2:3] + u1 * cw[1:2] + u2 * cw[0:1]), du * hx, du * gc], axis=1)
        return d_blk, s(dconv * u2), s(dconv * u1), s(dconv * uu)

    dproj, dcw0, dcw1, dcw2 = _colwise(
        "conv_bwd", conv_bwd, n_cb, [(dycat, LANE), (proj, 3 * LANE)], [w["conv_w"]],
        [(IN_PAD, BF16, 3 * LANE)], [(1, CONV_DIM)] * 3)

    def post_bwd(dy, y, r, k_h_, v, g_, ln_g, ln_b, r_k):
        _, vjp = jax.vjp(_rwkv_post, y, r, k_h_, v, g_, ln_g, ln_b, r_k)
        return vjp(dy)

    dy_rec, dr_p, dk_p, dv_p, dg, d_ln_g, d_ln_b, d_r_k = _rowwise(
        "rwkv_post_bwd", post_bwd, [(dycat, 512, 1), y_rec, u_r, k_h, u_v, g], post_c,
        [(RWKV_DIM, F32)] * 5, [(1, RWKV_DIM)] * 3)
    (dr_r, dv_r, dlw, dk_r, da, db), late_parts = _rec_bwd(
        u, lw, k_h, ra, rb, zs, dy_rec,
        [late_grads[n] if n in ("w_up", "w_ple_proj") else _reshard(n, late_grads[n]) for n in _LATE], [True] * len(_LATE))

    dproj, d_mu, d_w0, d_a0, d_k_k, d_k_a, d_wl, d_al, d_gl = _rwkv_pre_bwd(
        proj, u, [dr_p, dr_r, dv_p, dv_r, dlw, dk_p, dk_r, da, db, dg], w["shift_mu"], small, dproj)
    d_w_in = _matmul("d_in_proj", xn1, dproj, "tn", [BF16], tm=1024, tn=896, tk=4096)
    early_grads = dict(conv_w=jnp.concatenate([dcw0, dcw1, dcw2], axis=0),
                       w_lora_up=d_wl[:64], a_lora_up=d_al[:64], g_lora_up=d_gl[:160])
    early_send = [_split_w_in_grad(d_w_in)] + [_reshard(n, early_grads[n]) for n in _EARLY[1:]]
    s_sems, r_sems, sent, landing, token = _scatter_start("early_scatter_start", early_send,
                                                          [jnp.zeros(a.shape, a.dtype) for a in early_send])
    dx, d_norm_mix = _matmul(
        "dx_in_proj", dproj, w["w_in"], "nt", [F32], tk=IN_PAD, extras=[x, dh1], consts=[w["norm_mix_g"] + token[0:1, 0:1]],
        **dict(nb, epilogue=lambda *a: norm_bwd(*a)[1:]))

    grads = dict(
        norm_mix_g=d_norm_mix, shift_mu=d_mu, w0=d_w0, a0=d_a0, k_k=d_k_k, k_a=d_k_a, r_k=d_r_k,
        ln_x_g=d_ln_g, ln_x_b=d_ln_b, norm_mlp_g=d_norm_mlp, norm_ple_g=d_norm_ple, norm_final_g=d_norm_final)
    return loss, dx, grads, dict(zip(_LATE, late_parts)), (s_sems, r_sems, sent, landing)


def _adam_update(partials, w_ref, m_ref, v_ref, g_ref, d_ref, nm_ref, nv_ref):
    g = partials[0].astype(F32)
    for part in partials[1:]:
        g = g + part.astype(F32)
    nm =ADAM_B1 * m_ref[...] + (1.0 - ADAM_B1) * g
    nv = ADAM_B2 * v_ref[...] + (1.0 - ADAM_B2) * (g * g)
    m_hat = nm / (1.0 - ADAM_B1 ** ADAM_STEP)
    v_hat = nv / (1.0 - ADAM_B2 ** ADAM_STEP)
    g_ref[...] = g
    d_ref[...] = -ADAM_LR * (m_hat / (jnp.sqrt(v_hat) + ADAM_EPS) + ADAM_WD * w_ref[...])
    nm_ref[...] = nm
    nv_ref[...] = nv


SMALL_ROWS = 8


def _small_layout(widths):
    widths = list(widths) + [1]
    fill, place = [0] * SMALL_ROWS, [None] * len(widths)
    for j in sorted(range(len(widths)), key=lambda q: -widths[q]):
        row = fill.index(min(fill))
        place[j] = (row, fill[row])
        fill[row] += -(-widths[j] // LANE) * LANE
    return place, max(fill)


def _pack_small(vecs, loss):
    place, total = _small_layout([v_.shape[1] for v_ in vecs])
    n = len(vecs)

    def body(*refs):
        out = jnp.zeros((SMALL_ROWS, total), F32)
        row_id = lax.broadcasted_iota(jnp.int32, (SMALL_ROWS, total), 0)
        for row in range(SMALL_ROWS):
            mine = sorted((off, j) for j, (r_, off) in enumerate(place) if r_ == row)
            pieces, at = [], 0
            for off, j in mine:
                val = refs[j][...]
                pieces.append(val)
                at = off + val.shape[1]
                pad = -val.shape[1] % LANE
                if pad:
                    pieces.append(jnp.zeros((1, pad), F32))
                    at += pad
            if total > at:
                pieces.append(jnp.zeros((1, total - at), F32))
            out = jnp.where(row_id == row, jnp.broadcast_to(jnp.concatenate(pieces, axis=1), (SMALL_ROWS, total)), out)
        refs[n + 1][...] = out

    return pl.pallas_call(body, name="pack_small", out_shape=jax.ShapeDtypeStruct((SMALL_ROWS, total), F32))(*vecs, loss)


def _adamw_small(packed, ws, ms, vs):
    n = len(ws)
    place, _ = _small_layout([w_.shape[1] for w_ in ws])

    def body(p_ref, *refs):
        w_refs, m_refs, v_refs, outs = refs[:n], refs[n:2 * n], refs[2 * n:3 * n], refs[3 * n:]
        for j in range(n):
            row, off = place[j]
            cols = pl.ds(off, ws[j].shape[1])
            _adam_update([p_ref[s, row:row + 1, cols] for s in range(N_DEV)], w_refs[j], m_refs[j], v_refs[j],
                         *outs[4 * j:4 * j + 4])
        row, off = place[n]
        total = p_ref[0, row:row + 1, off:off + 1]
        for s in range(1, N_DEV):
            total = total + p_ref[s, row:row + 1, off:off + 1]
        outs[4 * n][...] = total

    res = pl.pallas_call(
        body, name="adamw_small",
        out_shape=[jax.ShapeDtypeStruct(w_.shape, F32) for w_ in ws for _ in range(4)] + [jax.ShapeDtypeStruct((1, 1), F32)],
    )(packed, *ws, *ms, *vs)
    return [res[4 * j:4 * j + 4] for j in range(n)], res[4 * n]


def _adamw(name, parts, w, m, v, own=None, me=None):
    rows, cols = w.shape[-2:]
    lead = w.ndim - 2
    tr = rows if rows * cols * 4 * 8 <= (4 << 20) else max(8, (4 << 20) // (cols * 4 * 8) // 8 * 8)
    while rows % tr:
        tr -= 8
    shape4 = [jax.ShapeDtypeStruct(w.shape, F32)] * 4
    if own is None:
        def body(p_ref, *refs):
            _adam_update([p_ref[s] for s in range(N_DEV)], *refs)

        blk = pl.BlockSpec((None,) * lead + (tr, cols), lambda i: (0,) * lead + (i, 0))
        return pl.pallas_call(
            body, name=name, grid=(rows // tr,),
            in_specs=[pl.BlockSpec((N_DEV, tr, cols), lambda i: (0, i, 0)), blk, blk, blk], out_specs=[blk] * 4,
            out_shape=shape4, compiler_params=_params(("arbitrary",)),
        )(parts, w, m, v)

    def body_own(me_ref, p_ref, own_ref, *refs):
        mine = own_ref[...]
        _adam_update([jnp.where(me_ref[0] == s, mine, p_ref[s]) for s in range(N_DEV)], *refs)

    blk = pl.BlockSpec((None,) * lead + (tr, cols), lambda i, me_ref: (0,) * lead + (i, 0))
    return pl.pallas_call(
        body_own, name=name, out_shape=shape4,
        grid_spec=pltpu.PrefetchScalarGridSpec(
            num_scalar_prefetch=1, grid=(rows // tr,),
            in_specs=[pl.BlockSpec((N_DEV, tr, cols), lambda i, me_ref: (0, i, 0)),
                      pl.BlockSpec((None, tr, cols), lambda i, me_ref: (me_ref[0], i, 0)), blk, blk, blk],
            out_specs=[blk] * 4),
        compiler_params=_params(("arbitrary",)),
    )(me, parts, own, w, m, v)


def kernel(x, p, norm_mix_g, w_in, conv_w, shift_mu, w_lora_up, w0, a_lora_up, a0, g_lora_up, k_k, k_a, r_k, ln_x_g, ln_x_b, w_out, norm_mlp_g, w_up, w_down, norm_ple_g, w_ple_gate, w_ple_proj, norm_final_g, loss_target, m_norm_mix_g, m_w_in, m_conv_w, m_shift_mu, m_w_lora_up, m_w0, m_a_lora_up, m_a0, m_g_lora_up, m_k_k, m_k_a, m_r_k, m_ln_x_g, m_ln_x_b, m_w_out, m_norm_mlp_g, m_w_up, m_w_down, m_norm_ple_g, m_w_ple_gate, m_w_ple_proj, m_norm_final_g, v_norm_mix_g, v_w_in, v_conv_w, v_shift_mu, v_w_lora_up, v_w0, v_a_lora_up, v_a0, v_g_lora_up, v_k_k, v_k_a, v_r_k, v_ln_x_g, v_ln_x_b, v_w_out, v_norm_mlp_g, v_w_up, v_w_down, v_norm_ple_g, v_w_ple_gate, v_w_ple_proj, v_norm_final_g):
    args = dict(locals())
    wts = {n: args[n] for n in _WEIGHTS}
    mom = {n: args["m_" + n] for n in _WEIGHTS}
    var = {n: args["v_" + n] for n in _WEIGHTS}
    shard2d = lambda a: a.reshape(a.shape[-2:])
    pad_mu = lambda a: _pad_in_cols(jnp.concatenate([jnp.zeros((1, CONV_COLS), F32), a], axis=1))[:, CONV_COLS:]
    unpad_mu = lambda a: _unpad_in_cols(jnp.concatenate([jnp.zeros((1, CONV_COLS), F32), a], axis=1))[:, CONV_COLS:]

    shards = {n: shard2d(wts[n]).astype(BF16 if n in _BF16_GATHER else F32) for n in _SHARDED}
    w = {n: wts[n].reshape(1, -1) for n in _REPLICATED}
    w["shift_mu"] = pad_mu(wts["shift_mu"])

    loss, dx, grads, parts, in_flight = _local_step(x[0], p[0, 0], loss_target[0], w, [shards[n] for n in _EARLY],
                                                    [shards[n] for n in _LATE])

    out = {n: _adamw("adamw_" + n, parts[n], wts[n], mom[n], var[n]) for n in _LATE}
    sent, early_parts = _scatter_wait("early_scatter_wait", *in_flight, after=[dx] + [out[n][1] for n in _LATE])
    me = (4 * lax.axis_index("x") + 2 * lax.axis_index("y") + lax.axis_index("c")).astype(jnp.int32).reshape(1)
    for n, prt, own in zip(_EARLY, early_parts, sent):
        out[n] = _adamw("adamw_" + n, prt, wts[n], mom[n], var[n], own=own, me=me)

    grads["shift_mu"] = unpad_mu(grads["shift_mu"])
    flat = lambda a: a.reshape(1, -1)
    (small_parts,) = _exchange("gather_small", [_pack_small([flat(grads[n]) for n in _REPLICATED], loss)], [False])
    small, loss_total = _adamw_small(small_parts, *[[flat(d[n]) for n in _REPLICATED] for d in (wts, mom, var)])
    for n, res in zip(_REPLICATED, small):
        out[n] = [r.reshape(wts[n].shape) for r in res]
    return (loss_total[0, 0], dx[None], *[out[n][0] for n in _WEIGHTS], *[out[n][1] for n in _WEIGHTS],
            *[out[n][2] for n in _WEIGHTS], *[out[n][3] for n in _WEIGHTS])
```

```python
import functools

import jax
import jax.numpy as jnp
from jax import lax
from jax.experimental import pallas as pl
from jax.experimental.pallas import tpu as pltpu

F32 = jnp.float32
BF16 = jnp.bfloat16

N_DEV = 8
D_MODEL = 1024
CONV_DIM = 512
RWKV_DIM = 512
HEAD_DIM = 64
N_HEADS = 8
D_FF = 4096
PLE_DIM = 256
RMS_EPS = 1e-6
GN_EPS = 64e-5
L2_EPS = 1e-12
ADAM_LR, ADAM_B1, ADAM_B2, ADAM_EPS, ADAM_WD, ADAM_STEP = 0.001, 0.9, 0.999, 1e-08, 0.01, 10

CONV_COLS = 3 * CONV_DIM
RW_PAD = 2048
IN_PAD = CONV_COLS + RW_PAD
IN_COLS = 3360
XW_OFF, XA_OFF, XG_OFF = 1536, 1664, 1792
REC_CHUNK = 128
REC_PASSES = 1
ROW_BLOCK = 256
LANE = 128
VMEM_LIMIT = 56 * 1024 * 1024


def _dims(dn, ndim):
    if ndim == 3:
        return {"nn": (((2,), (1,)), ((0,), (0,))), "nt": (((2,), (2,)), ((0,), (0,))),
                "tn": (((1,), (1,)), ((0,), (0,)))}[dn]
    return {"nn": (((1,), (0,)), ((), ())), "nt": (((1,), (1,)), ((), ())), "tn": (((0,), (0,)), ((), ()))}[dn]


def _split2(x):
    hi = x.astype(BF16)
    return hi, (x - hi.astype(F32)).astype(BF16)


def _mm_raw(x, y, dn, passes):
    f = lambda p, q: lax.dot_general(p, q, _dims(dn, x.ndim), preferred_element_type=F32)
    if passes == 1:
        return f(x.astype(BF16), y.astype(BF16))
    xh, xl = _split2(x)
    yh, yl = _split2(y)
    if passes == 2:
        return f(xh, yh) + f(xh, yl)
    return f(xh, yh) + f(xh, yl) + f(xl, yh)


@functools.partial(jax.custom_vjp, nondiff_argnums=(2, 3))
def _mm(x, y, dn, passes):
    return _mm_raw(x, y, dn, passes)


def _mm_fwd(x, y, dn, passes):
    return _mm_raw(x, y, dn, passes), (x, y)


def _mm_bwd(dn, passes, res, d):
    x, y = res
    if dn == "nn":
        return _mm(d, y, "nt", passes), _mm(x, d, "tn", passes)
    if dn == "nt":
        return _mm(d, y, "nn", passes), _mm(d, x, "tn", passes)
    return _mm(y, d, "nt", passes), _mm(x, d, "nn", passes)


_mm.defvjp(_mm_fwd, _mm_bwd)


def _head_ones():
    i = lax.broadcasted_iota(jnp.int32, (RWKV_DIM, RWKV_DIM), 0) // HEAD_DIM
    j = lax.broadcasted_iota(jnp.int32, (RWKV_DIM, RWKV_DIM), 1) // HEAD_DIM
    return (i == j).astype(BF16)


def _hsum_raw(x):
    ones = _head_ones()
    f = lambda p: lax.dot_general(p, ones, _dims("nn", 2), preferred_element_type=F32)
    x1, x2 = _split2(x)
    return f(x1) + f(x2)


@jax.custom_vjp
def _hsum(x):
    return _hsum_raw(x)


_hsum.defvjp(lambda x: (_hsum_raw(x), None), lambda _, d: (_hsum(d),))


def _sigmoid(x):
    return 0.5 + 0.5 * jnp.tanh(0.5 * x)


def _softplus(x):
    return jnp.maximum(x, 0.0) + jnp.log(1.0 + jnp.exp(-jnp.abs(x)))


def _params(sem):
    return pltpu.CompilerParams(dimension_semantics=sem, vmem_limit_bytes=VMEM_LIMIT)


def _rowwise(name, fn, rows, consts, row_outs, acc_outs=(), tr=ROW_BLOCK, halo=False, gather=()):
    rows = [r if isinstance(r, tuple) else (r, r.shape[1], 0) for r in rows]
    t_len = rows[0][0].shape[0]
    tr = min(tr, t_len)
    n_r, n_c, n_o, n_a, n_x = len(rows), len(consts), len(row_outs), len(acc_outs), len(gather)
    n_h = n_r if halo else 0
    sub = 8
    x_specs, x_shapes, x_sems = _exchange_io(gather, [False] * n_x) if n_x else ([], [], [])
    nb = t_len // tr

    def body(*refs):
        if n_x:
            n_in = n_r + n_h + n_c
            start, forward, wait = _gather_plan(refs[n_in:n_in + n_x], refs[len(refs) - 3 - n_x:len(refs) - 3], *refs[len(refs) - 3:])
            pl.when(pl.program_id(0) == 0)(start)
            refs = refs[:n_in] + refs[n_in + n_x:len(refs) - 3 - n_x]
        ins = [r[...] for r in refs[:n_r]]
        ins += [jnp.where(pl.program_id(0) == 0, 0.0, r[sub - 1:sub, :]) for r in refs[n_r:n_r + n_h]]
        ins += [r[...] for r in refs[n_r + n_h:n_r + n_h + n_c]]
        refs = refs[:n_r] + refs[n_r + n_h:]
        outs = fn(*ins)
        o_refs = refs[n_r + n_c:n_r + n_c + n_o]
        a_refs = refs[n_r + n_c + n_o:]
        for o_ref, val in zip(o_refs, outs[:n_o]):
            o_ref[...] = val.astype(o_ref.dtype)
        if n_a:
            first = pl.program_id(0) == 0

            @pl.when(first)
            def _():
                for a_ref, val in zip(a_refs, outs[n_o:]):
                    a_ref[...] = val

            @pl.when(jnp.logical_not(first))
            def _():
                for a_ref, val in zip(a_refs, outs[n_o:]):
                    a_ref[...] += val

        if n_x:
            @pl.when(pl.program_id(0) == nb - 1)
            def _():
                for j in range(n_x):
                    forward(j)
                wait()

    in_specs = [pl.BlockSpec((tr, w), functools.partial(lambda i, c: (i, c), c=cb)) for _, w, cb in rows]
    if halo:
        in_specs += [pl.BlockSpec((sub, w), functools.partial(lambda i, c: (jnp.maximum(i * (tr // sub) - 1, 0), c), c=cb))
                     for _, w, cb in rows]
    in_specs += [pl.BlockSpec(c.shape, functools.partial(lambda i, n: (0,) * n, n=c.ndim)) for c in consts]
    out_specs = [pl.BlockSpec((tr, w), lambda i: (i, 0)) for w, _ in row_outs]
    out_specs += [pl.BlockSpec(s, functools.partial(lambda i, n: (0,) * n, n=len(s))) for s in acc_outs]
    out_shape = [jax.ShapeDtypeStruct((t_len, w), dt) for w, dt in row_outs]
    out_shape += [jax.ShapeDtypeStruct(s, F32) for s in acc_outs]
    return pl.pallas_call(
        body, name=name, grid=(nb,), in_specs=in_specs + x_specs, out_specs=out_specs + x_specs,
        out_shape=out_shape + x_shapes, scratch_shapes=x_sems,
        compiler_params=pltpu.CompilerParams(dimension_semantics=("arbitrary",), vmem_limit_bytes=VMEM_LIMIT,
                                             has_side_effects=bool(n_x)),
    )(*[r[0] for r in rows], *([r[0] for r in rows] if halo else []), *consts, *gather)


def _colwise(name, fn, n_blocks, cols, prms, col_outs, prm_outs=()):
    t_len = cols[0][0].shape[0]
    n_i = len(cols) + len(prms)

    def body(*refs):
        outs = fn(*[r[...] for r in refs[:n_i]])
        for o_ref, val in zip(refs[n_i:], outs):
            o_ref[...] = val.astype(o_ref.dtype)

    spec = lambda r, w: pl.BlockSpec((r, w), lambda j: (0, j))
    in_specs = [spec(t_len, w) for _, w in cols] + [spec(a.shape[0], LANE) for a in prms]
    out_specs = [spec(t_len, bw) for _, _, bw in col_outs] + [spec(r, LANE) for r, _ in prm_outs]
    out_shape = [jax.ShapeDtypeStruct((t_len, w), dt) for w, dt, _ in col_outs]
    out_shape += [jax.ShapeDtypeStruct((r, w), F32) for r, w in prm_outs]
    return pl.pallas_call(
        body, name=name, grid=(n_blocks,), in_specs=in_specs, out_specs=out_specs, out_shape=out_shape,
        compiler_params=_params(("arbitrary",)),
    )(*[c[0] for c in cols], *prms)


def _matmul(name, a, b, dn, outs, *, tm, tn, tk, extras=(), consts=(), epilogue=None, sums=(), xch=(), xch_scatter=(),
            a_map=None, col_blocks_out=False):
    if dn == "nn":
        (m, k), n = a.shape, b.shape[1]
    elif dn == "nt":
        (m, k), n = a.shape, b.shape[0]
    else:
        (k, m), n = a.shape, b.shape[1]
    tm, tn, tk = min(tm, m), min(tn, n), min(tk, k)
    nk = k // tk
    grid = (m // tm, n // tn, nk)
    assert nk == 1 and (not sums or grid[1] == 1)
    a_spec = pl.BlockSpec((tk, tm), lambda i, j, q: (q, i)) if dn == "tn" else pl.BlockSpec((tm, tk), lambda i, j, q: (i, q))
    b_spec = pl.BlockSpec((tn, tk), lambda i, j, q: (j, q)) if dn == "nt" else pl.BlockSpec((tk, tn), lambda i, j, q: (q, j))
    o_spec = pl.BlockSpec((tm, tn), lambda i, j, q: (i, j))
    c_spec = pl.BlockSpec((1, tn), lambda i, j, q: (0, j))
    n_e, n_c, n_o, n_s, n_x = len(extras), len(consts), len(outs), len(sums), len(xch)
    x_specs, x_shapes, x_sems = _exchange_io(xch, xch_scatter) if n_x else ([], [], [])

    def body(*refs):
        a_ref, b_ref = refs[:2]
        e_refs = refs[2:2 + n_e + n_c]
        x_in = refs[2 + n_e + n_c:2 + n_e + n_c + n_x]
        rest = refs[2 + n_e + n_c + n_x:]
        o_refs, s_refs, x_out, scratch = rest[:n_o], rest[n_o:n_o + n_s], rest[n_o + n_s:n_o + n_s + n_x], rest[n_o + n_s + n_x:]
        step = (pl.program_id(0) * grid[1] + pl.program_id(1)) * nk + pl.program_id(2)
        if n_x:
            start, wait = _exchange_plan(x_in, x_out, xch_scatter, *scratch[len(scratch) - 3:])
            pl.when(step == 0)(start)
        a_blk = a_ref[...] if a_map is None else a_map(a_ref[...])
        acc = lax.dot_general(a_blk.astype(BF16), b_ref[...].astype(BF16), _dims(dn, 2), preferred_element_type=F32)
        vals = (acc,) if epilogue is None else epilogue(acc, *[e[...] for e in e_refs])
        for o_ref, val in zip(o_refs, vals[:n_o]):
            o_ref[...] = val.astype(o_ref.dtype)
        if n_s:
            @pl.when(step == 0)
            def _():
                for s_ref, val in zip(s_refs, vals[n_o:]):
                    s_ref[...] = val

            @pl.when(step > 0)
            def _():
                for s_ref, val in zip(s_refs, vals[n_o:]):
                    s_ref[...] += val

        if n_x:
            pl.when(step == grid[0] * grid[1] * nk - 1)(wait)

    plain = not (n_s or n_x)
    res = pl.pallas_call(
        body, name=name, grid=grid,
        in_specs=[a_spec, b_spec] + [o_spec] * n_e + [c_spec] * n_c + x_specs,
        out_specs=[pl.BlockSpec((None, tm, tn), lambda i, j, q: (j, i, 0)) if col_blocks_out else o_spec] * n_o
                  + [c_spec] * n_s + x_specs,
        out_shape=[jax.ShapeDtypeStruct((n // tn, m, tn) if col_blocks_out else (m, n), dt) for dt in outs] + [jax.ShapeDtypeStruct(s, F32) for s in sums] + x_shapes,
        scratch_shapes=x_sems,
        compiler_params=pltpu.CompilerParams(
            dimension_semantics=("parallel", "parallel", "arbitrary") if plain else ("arbitrary",) * 3,
            vmem_limit_bytes=VMEM_LIMIT, has_side_effects=bool(n_x)),
    )(a, b, *extras, *consts, *xch)
    return res[0] if len(res) == 1 else res


def _rms(h, g):
    return h * lax.rsqrt(jnp.mean(h * h, axis=-1, keepdims=True) + RMS_EPS) * g


def _rms_bwd(h, g, dy):
    rs = lax.rsqrt(jnp.mean(h * h, axis=-1, keepdims=True) + RMS_EPS)
    n = h * rs
    dn = dy * g
    dh = rs * (dn - n * jnp.mean(dn * n, axis=-1, keepdims=True))
    return dh, jnp.sum(dy * n, axis=0, keepdims=True)


def _rwkv_pre(k, xw, xa, xg, w0, a0, k_k, k_a, wl, al, gl):
    zw = w0 + _mm(jnp.tanh(xw), wl, "nn", 1)
    lw = -jnp.exp(-_softplus(-zw) - 0.5)
    iclr = _sigmoid(a0 + _mm(xa, al, "nn", 1))
    g = _mm(_sigmoid(xg), gl, "nn", 1)
    kk0 = k * k_k
    kk = kk0 * lax.rsqrt(jnp.maximum(_hsum(kk0 * kk0), L2_EPS * L2_EPS))
    k_h = k * (1.0 + (iclr - 1.0) * k_a)
    return lw, k_h, -kk, kk * iclr, g


def _rwkv_post(y, r, k_h, v, g, ln_g, ln_b, r_k):
    mu = _hsum(y) * (1.0 / HEAD_DIM)
    yc = y - mu
    var = _hsum(yc * yc) * (1.0 / HEAD_DIM)
    yo = yc * lax.rsqrt(var + GN_EPS) * ln_g + ln_b
    bonus = _hsum(r * k_h * r_k) * v
    return (yo + bonus) * g


def _shift_down(x, n):
    rows = lax.broadcasted_iota(jnp.int32, x.shape, 0)
    return jnp.where(rows < n, 0.0, pltpu.roll(x, n, 0))


def _shift_up(x, n):
    t_len = x.shape[0]
    rows = lax.broadcasted_iota(jnp.int32, x.shape, 0)
    return jnp.where(rows >= t_len - n, 0.0, pltpu.roll(x, t_len - n, 0))


def _exchange_plan(ins, outs, scatter, send_sems, recv_sems, local_sems):
    x, y, c = lax.axis_index("x"), lax.axis_index("y"), lax.axis_index("c")
    me = 4 * x + 2 * y + c

    def local(i):
        return pltpu.make_async_copy(ins[i].at[me] if scatter[i] else ins[i], outs[i].at[me], local_sems.at[i])

    def send(i, rel):
        return pltpu.make_async_remote_copy(
            src_ref=ins[i].at[me ^ rel] if scatter[i] else ins[i], dst_ref=outs[i].at[me],
            send_sem=send_sems.at[i, rel - 1], recv_sem=recv_sems.at[i, rel - 1],
            device_id=(x ^ (rel >> 2), y ^ ((rel >> 1) & 1), c ^ (rel & 1)), device_id_type=pl.DeviceIdType.MESH)

    def landed(i, rel):
        slot = outs[i].at[me ^ rel]
        return pltpu.make_async_remote_copy(
            src_ref=slot, dst_ref=slot, send_sem=send_sems.at[i, rel - 1], recv_sem=recv_sems.at[i, rel - 1],
            device_id=(x, y, c), device_id_type=pl.DeviceIdType.MESH)

    def start():
        for i in range(len(ins)):
            local(i).start()
            for rel in range(1, N_DEV):
                send(i, rel).start()

    def wait():
        for i in range(len(ins)):
            local(i).wait()
            for rel in range(1, N_DEV):
                landed(i, rel).wait_recv()
            for rel in range(1, N_DEV):
                send(i, rel).wait_send()

    return start, wait


def _gather_plan(ins, outs, send_sems, recv_sems, local_sems):
    x, y, c = lax.axis_index("x"), lax.axis_index("y"), lax.axis_index("c")
    me = 4 * x + 2 * y + c
    direct, chips = (1, 2, 4, 6), (2, 4, 6)

    def local(i):
        return pltpu.make_async_copy(ins[i], outs[i].at[me], local_sems.at[i])

    def send(i, rel):
        return pltpu.make_async_remote_copy(
            src_ref=ins[i], dst_ref=outs[i].at[me], send_sem=send_sems.at[i, rel - 1], recv_sem=recv_sems.at[i, rel - 1],
            device_id=(x ^ (rel >> 2), y ^ ((rel >> 1) & 1), c ^ (rel & 1)), device_id_type=pl.DeviceIdType.MESH)

    def passed(i, rel):
        slot = outs[i].at[me ^ rel]
        return pltpu.make_async_remote_copy(
            src_ref=slot, dst_ref=slot, send_sem=send_sems.at[i, rel], recv_sem=recv_sems.at[i, rel],
            device_id=(x, y, 1 - c), device_id_type=pl.DeviceIdType.MESH)

    def landed(i, rel):
        slot = outs[i].at[me ^ rel]
        return pltpu.make_async_remote_copy(
            src_ref=slot, dst_ref=slot, send_sem=send_sems.at[i, rel - 1], recv_sem=recv_sems.at[i, rel - 1],
            device_id=(x, y, c), device_id_type=pl.DeviceIdType.MESH)

    def start():
        for i in range(len(ins)):
            local(i).start()
            for rel in direct:
                send(i, rel).start()

    def forward(i):
        for rel in chips:
            landed(i, rel).wait_recv()
            passed(i, rel).start()

    def wait():
        for i in range(len(ins)):
            local(i).wait()
            for rel in (1, 3, 5, 7):
                landed(i, rel).wait_recv()
            for rel in direct:
                send(i, rel).wait_send()
            for rel in chips:
                passed(i, rel).wait_send()

    return start, forward, wait


def _exchange_io(arrays, scatter):
    n = len(arrays)
    any_spec = pl.BlockSpec(memory_space=pl.ANY)
    out_shape = [jax.ShapeDtypeStruct(a.shape if sc else (N_DEV,) + a.shape, a.dtype) for a, sc in zip(arrays, scatter)]
    sems = [pltpu.SemaphoreType.DMA((n, N_DEV - 1)), pltpu.SemaphoreType.DMA((n, N_DEV - 1)), pltpu.SemaphoreType.DMA((n,))]
    return [any_spec] * n, out_shape, sems


def _exchange(name, arrays, scatter):
    n = len(arrays)
    specs, out_shape, sems = _exchange_io(arrays, scatter)

    def body(*refs):
        if any(scatter):
            start, wait = _exchange_plan(refs[:n], refs[n:2 * n], scatter, *refs[2 * n:])
            start()
        else:
            start, forward, wait = _gather_plan(refs[:n], refs[n:2 * n], *refs[2 * n:])
            start()
            for i in range(n):
                forward(i)
        wait()

    return pl.pallas_call(
        body, name=name, in_specs=specs, out_specs=specs, out_shape=out_shape, scratch_shapes=sems,
        compiler_params=pltpu.CompilerParams(has_side_effects=True),
    )(*arrays)


def _scatter_start(name, arrays, lands):
    n = len(arrays)
    hbm = pl.BlockSpec(memory_space=pltpu.HBM)

    def body(*refs):
        ins, land, send_sems, recv_sems = refs[:n], refs[n:2 * n], refs[2 * n], refs[2 * n + 1]
        token = refs[4 * n + 2]
        x, y, c = lax.axis_index("x"), lax.axis_index("y"), lax.axis_index("c")
        me = 4 * x + 2 * y + c
        for i in range(n):
            for rel in range(1, N_DEV):
                k = i * (N_DEV - 1) + rel - 1
                pltpu.make_async_remote_copy(
                    src_ref=ins[i].at[me ^ rel], dst_ref=land[i].at[me], send_sem=send_sems.at[k],
                    recv_sem=recv_sems.at[k], device_id=(x ^ (rel >> 2), y ^ ((rel >> 1) & 1), c ^ (rel & 1)),
                    device_id_type=pl.DeviceIdType.MESH).start()
        token[...] = jnp.zeros_like(token)

    sem = pltpu.SemaphoreType.DMA((n * (N_DEV - 1),))
    bufs = [pltpu.HBM(a.shape, a.dtype) for a in list(arrays) + list(lands)]
    res = pl.pallas_call(
        body, name=name, out_shape=(sem, sem, *bufs, jax.ShapeDtypeStruct((8, LANE), F32)),
        in_specs=[hbm] * (2 * n),
        out_specs=(pl.BlockSpec(memory_space=pltpu.SEMAPHORE),) * 2 + (hbm,) * (2 * n) + (pl.BlockSpec(memory_space=pltpu.VMEM),),
        input_output_aliases={i: 2 + i for i in range(2 * n)},
        compiler_params=pltpu.CompilerParams(has_side_effects=pltpu.SideEffectType.DATAFLOW_SIDE_EFFECTING),
    )(*[pltpu.with_memory_space_constraint(a, pltpu.HBM) for a in list(arrays) + list(lands)])
    return res[0], res[1], res[2:2 + n], res[2 + n:2 + 2 * n], res[2 + 2 * n]


def _scatter_wait(name, send_sems, recv_sems, arrays, lands, after):
    n, n_after = len(arrays), len(after)
    hbm = pl.BlockSpec(memory_space=pltpu.HBM)

    def body(*refs):
        ins, land, s_sems, r_sems = refs[:n], refs[n:2 * n], refs[2 * n], refs[2 * n + 1]
        x, y, c = lax.axis_index("x"), lax.axis_index("y"), lax.axis_index("c")
        me = 4 * x + 2 * y + c
        for i in range(n):
            for rel in range(1, N_DEV):
                k = i * (N_DEV - 1) + rel - 1
                cp = pltpu.make_async_remote_copy(
                    src_ref=ins[i].at[me ^ rel], dst_ref=land[i].at[me ^ rel], send_sem=s_sems.at[k],
                    recv_sem=r_sems.at[k], device_id=(x, y, c), device_id_type=pl.DeviceIdType.MESH)
                cp.wait_send()
                cp.wait_recv()

    res = pl.pallas_call(
        body, name=name, out_shape=[pltpu.HBM(a.shape, a.dtype) for a in list(arrays) + list(lands)],
        in_specs=[hbm] * (2 * n) + [pl.BlockSpec(memory_space=pltpu.SEMAPHORE)] * 2 + [pl.BlockSpec(memory_space=pl.ANY)] * n_after,
        out_specs=[hbm] * (2 * n), input_output_aliases={i: i for i in range(2 * n)},
        compiler_params=pltpu.CompilerParams(has_side_effects=pltpu.SideEffectType.DATAFLOW_SIDE_EFFECTING),
    )(*arrays, *lands, send_sems, recv_sems, *after)
    return res[:n], res[n:]


def _tri_powers(low):
    powers, n = [low], 1
    while 2 * n < low.shape[-1]:
        powers.append(_mm(powers[-1], powers[-1], "nn", REC_PASSES))
        n *= 2
    return powers


@jax.custom_vjp
def _tri_solve(low, rhs):
    for p in _tri_powers(low):
        rhs = rhs + _mm(p, rhs, "nn", REC_PASSES)
    return rhs


def _tri_solve_fwd(low, rhs):
    powers = _tri_powers(low)
    for p in powers:
        rhs = rhs + _mm(p, rhs, "nn", REC_PASSES)
    return rhs, (powers, rhs)


def _tri_solve_bwd(res, d):
    powers, u = res
    for p in powers:
        d = d + _mm(p, d, "tn", REC_PASSES)
    return _mm(d, u, "nt", REC_PASSES), d


_tri_solve.defvjp(_tri_solve_fwd, _tri_solve_bwd)


def _heads(x):
    return jnp.stack([x[:, h * HEAD_DIM:(h + 1) * HEAD_DIM] for h in range(N_HEADS)])


def _unheads(x):
    return jnp.concatenate([x[h] for h in range(N_HEADS)], axis=-1)


def _chunk_fwd(z0, r, lw, k, v, a, b):
    c = r.shape[0]
    n_h, n_k = z0.shape[0], z0.shape[1]
    mm = functools.partial(_mm, passes=REC_PASSES)
    gram = functools.partial(_mm, passes=2)
    ti = lax.broadcasted_iota(jnp.int32, (c, c), 0)
    si = lax.broadcasted_iota(jnp.int32, (c, c), 1)
    strict, incl = si < ti, si <= ti
    cum = _mm(incl.astype(F32), lw, "nn", 3)
    cum_end = cum[c - 1:c, :]
    e_neg, e_end = jnp.exp(-cum), jnp.exp(cum_end - cum)
    x2 = jnp.concatenate([_heads(a * jnp.exp(cum - lw)), _heads(r * jnp.exp(cum))], axis=1)
    y2 = jnp.concatenate([_heads(b * e_neg), _heads(k * e_neg)], axis=1)
    vh = _heads(v)
    mask = jnp.concatenate([jnp.concatenate([strict, strict], axis=1), jnp.concatenate([incl, incl], axis=1)], axis=0)
    g2 = jnp.where(mask, gram(x2, y2, "nt"), 0.0)
    t2 = mm(x2, z0, "nn") + mm(g2[:, :, c:], vh, "nn")
    u = _tri_solve(g2[:, :c, :c], t2[:, :c])
    y = t2[:, c:] + mm(g2[:, c:, :c], u, "nn")
    ki = lax.broadcasted_iota(jnp.int32, (n_k, n_k), 0)
    kj = lax.broadcasted_iota(jnp.int32, (n_k, n_k), 1)
    dmat = jnp.where(ki == kj, jnp.broadcast_to(_heads(jnp.exp(cum_end)), (n_h, n_k, n_k)), 0.0)
    z_end = mm(dmat, z0, "nn") + mm(jnp.concatenate([_heads(b * e_end), _heads(k * e_end)], axis=1),
                                    jnp.concatenate([u, vh], axis=1), "tn")
    return _unheads(y), z_end


def _rec_params():
    return pltpu.CompilerParams(dimension_semantics=("arbitrary",), vmem_limit_bytes=VMEM_LIMIT, has_side_effects=True)


def _rec_fwd(u, lw, k, a, b, xch):
    t_len = lw.shape[0]
    c = min(REC_CHUNK, t_len)
    nc = t_len // c
    n_x = len(xch)
    x_specs, x_shapes, x_sems = _exchange_io(xch, [False] * n_x)
    sizes = [a_.size * a_.dtype.itemsize for a_ in xch]
    pass_step = [min(nc - 1, int(0.9 * nc * sum(sizes[:j + 1]) / sum(sizes)) + 1) for j in range(n_x)]

    def body(*refs):
        r_ref, v_ref, lw_ref, k_ref, a_ref, b_ref = refs[:6]
        x_in = refs[6:6 + n_x]
        y_ref, zs_ref = refs[6 + n_x:8 + n_x]
        x_out = refs[8 + n_x:8 + 2 * n_x]
        z_scr = refs[8 + 2 * n_x]
        start, forward, wait = _gather_plan(x_in, x_out, *refs[9 + 2 * n_x:])
        i = pl.program_id(0)

        @pl.when(i == 0)
        def _():
            start()
            z_scr[...] = jnp.zeros_like(z_scr)

        z0 = z_scr[...]
        zs_ref[0] = z0
        y, z_end = _chunk_fwd(z0, r_ref[...], lw_ref[...], k_ref[...], v_ref[...], a_ref[...], b_ref[...])
        y_ref[...] = y
        z_scr[...] = z_end

        for j in range(n_x):
            pl.when(i == pass_step[j])(functools.partial(forward, j))

        @pl.when(i == nc - 1)
        def _():
            wait()

    blk = lambda cb: pl.BlockSpec((c, RWKV_DIM), functools.partial(lambda i, q: (i, q), q=cb))
    res = pl.pallas_call(
        body, name="rwkv_rec_fwd", grid=(nc,),
        in_specs=[blk(0), blk(2)] + [blk(0)] * 4 + x_specs,
        out_specs=[blk(0), pl.BlockSpec((1, N_HEADS, HEAD_DIM, HEAD_DIM), lambda i: (i, 0, 0, 0))] + x_specs,
        out_shape=[jax.ShapeDtypeStruct((t_len, RWKV_DIM), F32),
                   jax.ShapeDtypeStruct((nc, N_HEADS, HEAD_DIM, HEAD_DIM), F32)] + x_shapes,
        scratch_shapes=[pltpu.VMEM((N_HEADS, HEAD_DIM, HEAD_DIM), F32)] + x_sems,
        compiler_params=_rec_params(),
    )(u, u, lw, k, a, b, *xch)
    return res[0], res[1], res[2:]


def _rec_bwd(u, lw, k, a, b, zs, dy):
    t_len = lw.shape[0]
    c = min(REC_CHUNK, t_len)
    nc = t_len // c

    def body(r_ref, v_ref, lw_ref, k_ref, a_ref, b_ref, zs_ref, dy_ref, *rest):
        g_refs, dz_scr = rest[:6], rest[6]

        @pl.when(pl.program_id(0) == 0)
        def _():
            dz_scr[...] = jnp.zeros_like(dz_scr)

        _, vjp = jax.vjp(_chunk_fwd, zs_ref[0], r_ref[...], lw_ref[...], k_ref[...], v_ref[...], a_ref[...], b_ref[...])
        dz0, dr, dlw, dk, dv, da, db = vjp((dy_ref[...], dz_scr[...]))
        for ref, val in zip(g_refs, (dr, dv, dlw, dk, da, db)):
            ref[...] = val
        dz_scr[...] = dz0

    blk = lambda cb: pl.BlockSpec((c, RWKV_DIM), functools.partial(lambda i, q: (nc - 1 - i, q), q=cb))
    return pl.pallas_call(
        body, name="rwkv_rec_bwd", grid=(nc,),
        in_specs=[blk(0), blk(2)] + [blk(0)] * 4
                 + [pl.BlockSpec((1, N_HEADS, HEAD_DIM, HEAD_DIM), lambda i: (nc - 1 - i, 0, 0, 0)), blk(0)],
        out_specs=[blk(0)] * 6, out_shape=[jax.ShapeDtypeStruct((t_len, RWKV_DIM), F32)] * 6,
        scratch_shapes=[pltpu.VMEM((N_HEADS, HEAD_DIM, HEAD_DIM), F32)], compiler_params=_params(("arbitrary",)),
    )(u, u, lw, k, a, b, zs, dy)


_EARLY = ["w_in", "conv_w", "w_lora_up", "a_lora_up", "g_lora_up"]
_LATE = ["w_out", "w_up", "w_down", "w_ple_gate", "w_ple_proj"]
_SHARDED = _EARLY + _LATE
_COL_SHARDED = {"w_in", "conv_w", "w_lora_up", "a_lora_up", "g_lora_up", "w_up", "w_ple_proj"}
_BF16_GATHER = {"w_in", "w_out", "w_up", "w_down", "w_ple_gate", "w_ple_proj"}
_REPLICATED = ["norm_mix_g", "shift_mu", "w0", "a0", "k_k", "k_a", "r_k", "ln_x_g", "ln_x_b", "norm_mlp_g", "norm_ple_g",
               "norm_final_g"]
_WEIGHTS = ["norm_mix_g", "w_in", "conv_w", "shift_mu", "w_lora_up", "w0", "a_lora_up", "a0", "g_lora_up", "k_k", "k_a", "r_k",
            "ln_x_g", "ln_x_b", "w_out", "norm_mlp_g", "w_up", "w_down", "norm_ple_g", "w_ple_gate", "w_ple_proj", "norm_final_g"]


def _unshard(name, g):
    if name in _COL_SHARDED:
        return jnp.moveaxis(g, 0, 1).reshape(g.shape[1], N_DEV * g.shape[2])
    return g.reshape(N_DEV * g.shape[1], g.shape[2])


def _reshard(name, full):
    if name in _COL_SHARDED:
        return jnp.moveaxis(full.reshape(full.shape[0], N_DEV, full.shape[1] // N_DEV), 1, 0)
    return full.reshape(N_DEV, full.shape[0] // N_DEV, full.shape[1])


def _pad_in_cols(a):
    z = lambda n: jnp.zeros(a.shape[:-1] + (n,), a.dtype)
    conv = [a[..., part * CONV_DIM + j * LANE:part * CONV_DIM + (j + 1) * LANE] for j in range(CONV_DIM // LANE) for part in range(3)]
    return jnp.concatenate(conv + [a[..., CONV_COLS:3136], z(64), a[..., 3136:3200], z(64), a[..., 3200:3360], z(96)], axis=-1)


def _unpad_in_cols(a):
    conv = [a[..., (3 * j + part) * LANE:(3 * j + part + 1) * LANE] for part in range(3) for j in range(CONV_DIM // LANE)]
    return jnp.concatenate(conv + [a[..., CONV_COLS:3136], a[..., 3200:3264], a[..., 3328:3488]], axis=-1)


def _assemble_w_in(g):
    n_dev, rows, cols = g.shape

    def body(g_ref, o_ref):
        o_ref[...] = _pad_in_cols(jnp.concatenate([g_ref[d] for d in range(n_dev)], axis=1))

    return pl.pallas_call(
        body, name="w_in_assemble", grid=(rows // ROW_BLOCK,),
        in_specs=[pl.BlockSpec((n_dev, ROW_BLOCK, cols), lambda i: (0, i, 0))],
        out_specs=pl.BlockSpec((ROW_BLOCK, IN_PAD), lambda i: (i, 0)),
        out_shape=jax.ShapeDtypeStruct((rows, IN_PAD), g.dtype), compiler_params=_params(("arbitrary",)),
    )(g)


def _split_w_in_grad(dw):
    rows = dw.shape[0]
    cols = IN_COLS // N_DEV

    def body(d_ref, o_ref):
        full = _unpad_in_cols(d_ref[...])
        for d in range(N_DEV):
            o_ref[d] = full[:, cols * d:cols * (d + 1)]

    return pl.pallas_call(
        body, name="w_in_grad_split", grid=(rows // ROW_BLOCK,),
        in_specs=[pl.BlockSpec((ROW_BLOCK, IN_PAD), lambda i: (i, 0))],
        out_specs=pl.BlockSpec((N_DEV, ROW_BLOCK, cols), lambda i: (0, i, 0)),
        out_shape=jax.ShapeDtypeStruct((N_DEV, rows, cols), dw.dtype), compiler_params=_params(("arbitrary",)),
    )(dw)


def _pad_rows(a, rows):
    return jnp.concatenate([a, jnp.zeros((rows - a.shape[0],) + a.shape[1:], a.dtype)], axis=0)


SEG_W = [RWKV_DIM, RWKV_DIM, RWKV_DIM, LANE, LANE, 2 * LANE]
SEG_OFF = [0, 512, 1024, XW_OFF, XA_OFF, XG_OFF]


def _rwkv_pre_bwd(proj, u, grads, mu, small, dproj):
    t_len = u.shape[0]
    tr = min(ROW_BLOCK, t_len)
    nb = t_len // tr
    sub = 8
    n_g = len(grads)
    acc_shapes = [(1, RW_PAD)] + [(1, RWKV_DIM)] * 4 + [(LANE, RWKV_DIM), (LANE, RWKV_DIM), (2 * LANE, RWKV_DIM)]

    def body(*refs):
        seg_refs, halo_refs = refs[:6], refs[6:12]
        k_ref, xw_ref, xa_ref, xg_ref = refs[12:16]
        g_refs = refs[16:16 + n_g]
        mu_ref = refs[16 + n_g]
        prm_refs = refs[17 + n_g:24 + n_g]
        out_hbm = refs[25 + n_g]
        acc_refs = refs[26 + n_g:26 + n_g + len(acc_shapes)]
        vbuf, sems, carry = refs[26 + n_g + len(acc_shapes):]
        i = pl.program_id(0)
        blk = nb - 1 - i
        dr1, dr2, dv1, dv2, dlw, dk1, dk2, da, db, dg = [g[...] for g in g_refs]
        _, vjp = jax.vjp(_rwkv_pre, k_ref[...], xw_ref[...], xa_ref[...], xg_ref[...], *[p_[...] for p_ in prm_refs])
        dk, dxw, dxa, dxg, *dprm = vjp((dlw, dk1 + dk2, da, db, dg))
        du = jnp.concatenate([dr1 + dr2, dk, dv1 + dv2, dxw, dxa, dxg], axis=1)
        mu_v = mu_ref[...]

        @pl.when(i == 0)
        def _():
            carry[...] = jnp.zeros_like(carry)

        rows = lax.broadcasted_iota(jnp.int32, du.shape, 0)
        nxt = jnp.where(rows == tr - 1, carry[...], pltpu.roll(du, tr - 1, 0))
        d_rw = du - mu_v * du + mu_v * nxt
        d_mu = []
        for s_ref, h_ref, off, wd in zip(seg_refs, halo_refs, SEG_OFF, SEG_W):
            cur = s_ref[...]
            r0 = lax.broadcasted_iota(jnp.int32, cur.shape, 0)
            prev = jnp.where(r0 == 0, jnp.where(blk == 0, 0.0, h_ref[sub - 1:sub, :]), pltpu.roll(cur, 1, 0))
            d_mu.append(jnp.sum(du[:, off:off + wd] * (prev - cur), axis=0, keepdims=True))
        sums = [jnp.concatenate(d_mu, axis=1)] + list(dprm)

        @pl.when(i == 0)
        def _():
            for a_ref, val in zip(acc_refs, sums):
                a_ref[...] = val

        @pl.when(i > 0)
        def _():
            for a_ref, val in zip(acc_refs, sums):
                a_ref[...] += val

        carry[...] = du[0:1, :]
        slot = i % 2

        def writeback(s, b):
            return pltpu.make_async_copy(vbuf.at[s], out_hbm.at[pl.ds(b * tr, tr), pl.ds(CONV_COLS, RW_PAD)], sems.at[s])

        @pl.when(i >= 2)
        def _():
            writeback(slot, blk + 2).wait()

        vbuf[slot] = d_rw.astype(vbuf.dtype)
        writeback(slot, blk).start()

        @pl.when(i == nb - 1)
        def _():
            writeback(slot, blk).wait()
            if nb > 1:
                writeback(1 - slot, blk + 1).wait()

    rev = lambda w_, cb: pl.BlockSpec((tr, w_), functools.partial(lambda i, c: (nb - 1 - i, c), c=cb))
    halo = lambda w_, cb: pl.BlockSpec((sub, w_), functools.partial(
        lambda i, c: (jnp.maximum((nb - 1 - i) * (tr // sub) - 1, 0), c), c=cb))
    whole = lambda a: pl.BlockSpec(a.shape, functools.partial(lambda i, n: (0,) * n, n=a.ndim))
    segs = [(wd, (CONV_COLS + off) // wd) for off, wd in zip(SEG_OFF, SEG_W)]
    u_cols = [(512, 1), (LANE, XW_OFF // LANE), (LANE, XA_OFF // LANE), (2 * LANE, XG_OFF // (2 * LANE))]
    any_spec = pl.BlockSpec(memory_space=pl.ANY)
    res = pl.pallas_call(
        body, name="rwkv_pre_bwd", grid=(nb,),
        in_specs=[rev(*s) for s in segs] + [halo(*s) for s in segs] + [rev(*c) for c in u_cols]
                 + [rev(RWKV_DIM, 0)] * n_g + [whole(mu)] + [whole(p_) for p_ in small] + [any_spec],
        out_specs=[any_spec] + [pl.BlockSpec(s, functools.partial(lambda i, n: (0,) * n, n=len(s))) for s in acc_shapes],
        out_shape=[jax.ShapeDtypeStruct(dproj.shape, dproj.dtype)] + [jax.ShapeDtypeStruct(s, F32) for s in acc_shapes],
        scratch_shapes=[pltpu.VMEM((2, tr, RW_PAD), dproj.dtype), pltpu.SemaphoreType.DMA((2,)), pltpu.VMEM((1, RW_PAD), F32)],
        input_output_aliases={24 + n_g: 0},
        compiler_params=_params(("arbitrary",)),
    )(*[proj] * 12, *[u] * 4, *grads, mu, *small, dproj)
    return res


def _local_step(x, p, tgt, w, early_shards, late_shards):
    row = lambda v: v.reshape(1, -1)
    w = dict(w)

    xn1, *gathered = _rowwise("rms_mix", lambda h, g: (_rms(h, g),), [x], [w["norm_mix_g"]], [(D_MODEL, BF16)],
                              gather=early_shards)
    w.update({n: _unshard(n, g_) for n, g_ in zip(_EARLY[1:], gathered[1:])})
    w["w_in"] = _assemble_w_in(gathered[0])
    w["w_lora_up"] = _pad_rows(w["w_lora_up"], LANE)
    w["a_lora_up"] = _pad_rows(w["a_lora_up"], LANE)
    w["g_lora_up"] = _pad_rows(w["g_lora_up"], 2 * LANE)
    proj = _matmul("in_proj", xn1, w["w_in"], "nn", [F32], tm=2048, tn=512, tk=D_MODEL)
    n_cb = CONV_DIM // LANE

    def conv_fwd(blk, cw):
        gb, gc, hx = blk[:, :LANE], blk[:, LANE:2 * LANE], blk[:, 2 * LANE:]
        uu = gc * hx
        return (gb * (uu * cw[2:3] + _shift_down(uu, 1) * cw[1:2] + _shift_down(uu, 2) * cw[0:1]),)

    (y_conv,) = _colwise("conv_fwd", conv_fwd, n_cb, [(proj, 3 * LANE)], [w["conv_w"]], [(CONV_DIM, BF16, LANE)])

    small = [w["w0"], w["a0"], w["k_k"], w["k_a"], w["w_lora_up"], w["a_lora_up"], w["g_lora_up"]]
    def pre_fwd(*xs):
        cur, prev_rows, mu, prm = xs[:6], xs[6:12], xs[12], xs[13:]
        segs = []
        for c_, p_, off, wd in zip(cur, prev_rows, SEG_OFF, SEG_W):
            rows = lax.broadcasted_iota(jnp.int32, c_.shape, 0)
            prev = jnp.where(rows == 0, p_, pltpu.roll(c_, 1, 0))
            segs.append(c_ + mu[:, off:off + wd] * (prev - c_))
        return (jnp.concatenate(segs, axis=1),) + tuple(_rwkv_pre(segs[1], segs[3], segs[4], segs[5], *prm))

    proj_segs = [(proj, wd, (CONV_COLS + off) // wd) for off, wd in zip(SEG_OFF, SEG_W)]
    u, lw, k_h, ra, rb, g = _rowwise("rwkv_pre", pre_fwd, proj_segs, [w["shift_mu"]] + small,
                                     [(RW_PAD, F32)] + [(RWKV_DIM, F32)] * 5, halo=True)
    u_k, u_xw, u_xa, u_xg = (u, 512, 1), (u, LANE, XW_OFF // LANE), (u, LANE, XA_OFF // LANE), (u, 2 * LANE, XG_OFF // (2 * LANE))
    y_rec, zs, late = _rec_fwd(u, lw, k_h, ra, rb, late_shards)
    for n, gathered in zip(_LATE, late):
        w[n] = _unshard(n, gathered)
    post_c = [w["ln_x_g"], w["ln_x_b"], w["r_k"]]
    u_r, u_v = (u, 512, 0), (u, 512, 2)
    (y_rwkv,) = _rowwise("rwkv_post", lambda *xs: (_rwkv_post(*xs),), [y_rec, u_r, k_h, u_v, g], post_c, [(RWKV_DIM, BF16)],
                         tr=2 * ROW_BLOCK)
    ycat = jnp.concatenate([y_conv, y_rwkv], axis=1)
    def res_norm(acc, r_, g_):
        h = acc + r_
        return h, _rms(h, g_)

    h1, xn2 = _matmul("out_proj", ycat, w["w_out"], "nn", [F32, BF16], tm=1024, tn=D_MODEL, tk=D_MODEL, extras=[x],
                      consts=[w["norm_mlp_g"]], epilogue=res_norm)

    square = lambda h: h.astype(F32) * h.astype(F32)
    hid = _matmul("mlp_up", xn2, w["w_up"], "nn", [BF16], tm=2048, tn=1024, tk=D_MODEL,
                  epilogue=lambda acc: (jnp.maximum(acc, 0.0),))
    h2, xn3 = _matmul("mlp_down", hid, w["w_down"], "nn", [F32, BF16], tm=512, tn=D_MODEL, tk=D_FF, extras=[h1],
                      consts=[w["norm_ple_g"]], epilogue=res_norm, a_map=square)
    zg = _matmul("ple_gate", xn3, w["w_ple_gate"], "nn", [F32], tm=1024, tn=1024, tk=D_MODEL)
    pp = _matmul("ple_proj", p, w["w_ple_proj"], "nn", [F32], tm=1024, tn=1024, tk=PLE_DIM)

    def head(h2_, zg_, pp_, tg, gf):
        gate = _sigmoid(zg_)
        h3 = h2_ + gate * pp_
        out = _rms(h3, gf)
        err = out - tg
        dh3, dgf = _rms_bwd(h3, gf, err * (1.0 / D_MODEL))
        loss = jnp.sum(jnp.sum(err * err, axis=1, keepdims=True), axis=0, keepdims=True) * (0.5 / D_MODEL)
        return dh3, dh3 * pp_ * gate * (1.0 - gate), dh3 * gate, dgf, loss

    dh3, dzg, dpp, d_norm_final, loss = _rowwise(
        "head", head, [h2, zg, pp, tgt], [row(w["norm_final_g"])], [(D_MODEL, F32), (D_MODEL, BF16), (D_MODEL, BF16)],
        [(1, D_MODEL), (1, 1)], tr=2 * ROW_BLOCK)

    d_w_ple_proj = _matmul("d_ple_proj", p, dpp, "tn", [BF16], tm=PLE_DIM, tn=D_MODEL // N_DEV, tk=4096, col_blocks_out=True)
    d_w_ple_gate = _matmul("d_ple_gate", xn3, dzg, "tn", [BF16], tm=512, tn=1024, tk=4096)

    def norm_bwd(dxn, h, dres, g_):
        dh, dg = _rms_bwd(h, g_, dxn)
        dh = dh + dres
        return dh, dh, dg

    nb = dict(tm=512, tn=D_MODEL, epilogue=norm_bwd, sums=[(1, D_MODEL)])
    dh2, dh2_b, d_norm_ple = _matmul("dx_ple_gate", dzg, w["w_ple_gate"], "nt", [F32, BF16], tk=D_MODEL,
                                     extras=[h2, dh3], consts=[w["norm_ple_g"]], **nb)
    d_w_down = _matmul("d_mlp_down", hid, dh2_b, "tn", [BF16], tm=512, tn=1024, tk=4096, a_map=square)
    dpre = _matmul("dx_mlp_down", dh2_b, w["w_down"], "nt", [BF16], tm=2048, tn=1024, tk=D_MODEL, extras=[hid],
                   epilogue=lambda acc, hid_: (acc * (2.0 * hid_.astype(F32)),))
    d_w_up = _matmul("d_mlp_up", xn2, dpre, "tn", [BF16], tm=1024, tn=D_FF // N_DEV, tk=4096, col_blocks_out=True)
    dh1, dh1_b, d_norm_mlp = _matmul("dx_mlp_up", dpre, w["w_up"], "nt", [F32, BF16], tk=D_FF,
                                     extras=[h1, dh2], consts=[w["norm_mlp_g"]], **nb)
    d_w_out = _matmul("d_out_proj", ycat, dh1_b, "tn", [BF16], tm=512, tn=1024, tk=4096)
    dycat = _matmul("dx_out_proj", dh1_b, w["w_out"], "nt", [F32], tm=1024, tn=1024, tk=D_MODEL)
    late_grads = dict(w_out=d_w_out, w_up=d_w_up, w_down=d_w_down, w_ple_gate=d_w_ple_gate, w_ple_proj=d_w_ple_proj)
    late_send = [late_grads[n] if n in ("w_up", "w_ple_proj") else _reshard(n, late_grads[n]) for n in _LATE]
    *late_flight, late_token = _scatter_start("late_scatter_start", late_send, [lax.empty(a.shape, a.dtype) for a in late_send])
    conv_w_bwd = w["conv_w"] + late_token[0:1, 0:1]

    def conv_bwd(dy, blk, cw):
        gb, gc, hx = blk[:, :LANE], blk[:, LANE:2 * LANE], blk[:, 2 * LANE:]
        uu = gc * hx
        u1, u2 = _shift_down(uu, 1), _shift_down(uu, 2)
        dconv = dy * gb
        du = dconv * cw[2:3] + _shift_up(dconv, 1) * cw[1:2] + _shift_up(dconv, 2) * cw[0:1]
        s = lambda z: jnp.sum(z, axis=0, keepdims=True)
        d_blk = jnp.concatenate([dy * (uu * cw[2:3] + u1 * cw[1:2] + u2 * cw[0:1]), du * hx, du * gc], axis=1)
        return d_blk, s(dconv * u2), s(dconv * u1), s(dconv * uu)

    dproj, dcw0, dcw1, dcw2 = _colwise(
        "conv_bwd", conv_bwd, n_cb, [(dycat, LANE), (proj, 3 * LANE)], [conv_w_bwd],
        [(IN_PAD, BF16, 3 * LANE)], [(1, CONV_DIM)] * 3)

    def post_bwd(dy, y, r, k_h_, v, g_, ln_g, ln_b, r_k):
        _, vjp = jax.vjp(_rwkv_post, y, r, k_h_, v, g_, ln_g, ln_b, r_k)
        return vjp(dy)

    dy_rec, dr_p, dk_p, dv_p, dg, d_ln_g, d_ln_b, d_r_k = _rowwise(
        "rwkv_post_bwd", post_bwd, [(dycat, 512, 1), y_rec, u_r, k_h, u_v, g], post_c,
        [(RWKV_DIM, F32)] * 5, [(1, RWKV_DIM)] * 3)
    dr_r, dv_r, dlw, dk_r, da, db = _rec_bwd(u, lw, k_h, ra, rb, zs, dy_rec)

    dproj, d_mu, d_w0, d_a0, d_k_k, d_k_a, d_wl, d_al, d_gl = _rwkv_pre_bwd(
        proj, u, [dr_p, dr_r, dv_p, dv_r, dlw, dk_p, dk_r, da, db, dg], w["shift_mu"], small, dproj)
    d_w_in = _matmul("d_in_proj", xn1, dproj, "tn", [BF16], tm=1024, tn=896, tk=4096)
    early_grads = dict(conv_w=jnp.concatenate([dcw0, dcw1, dcw2], axis=0),
                       w_lora_up=d_wl[:64], a_lora_up=d_al[:64], g_lora_up=d_gl[:160])
    early_send = [_split_w_in_grad(d_w_in)] + [_reshard(n, early_grads[n]) for n in _EARLY[1:]]
    *early_flight, token = _scatter_start("early_scatter_start", early_send, [lax.empty(a.shape, a.dtype) for a in early_send])
    dx, d_norm_mix = _matmul(
        "dx_in_proj", dproj, w["w_in"], "nt", [F32], tk=IN_PAD, extras=[x, dh1], consts=[w["norm_mix_g"] + token[0:1, 0:1]],
        **dict(nb, epilogue=lambda *a: norm_bwd(*a)[1:]))

    grads = dict(
        norm_mix_g=d_norm_mix, shift_mu=d_mu, w0=d_w0, a0=d_a0, k_k=d_k_k, k_a=d_k_a, r_k=d_r_k,
        ln_x_g=d_ln_g, ln_x_b=d_ln_b, norm_mlp_g=d_norm_mlp, norm_ple_g=d_norm_ple, norm_final_g=d_norm_final)
    return loss, dx, grads, late_flight, early_flight, d_w_in


def _adam_update(partials, w_ref, m_ref, v_ref, g_ref, d_ref, nm_ref, nv_ref):
    g = partials[0].astype(F32)
    for part in partials[1:]:
        g = g + part.astype(F32)
    nm =ADAM_B1 * m_ref[...] + (1.0 - ADAM_B1) * g
    nv = ADAM_B2 * v_ref[...] + (1.0 - ADAM_B2) * (g * g)
    m_hat = nm / (1.0 - ADAM_B1 ** ADAM_STEP)
    v_hat = nv / (1.0 - ADAM_B2 ** ADAM_STEP)
    g_ref[...] = g
    d_ref[...] = -ADAM_LR * (m_hat / (jnp.sqrt(v_hat) + ADAM_EPS) + ADAM_WD * w_ref[...])
    nm_ref[...] = nm
    nv_ref[...] = nv


SMALL_ROWS = 8


def _small_layout(widths):
    widths = list(widths) + [1]
    fill, place = [0] * SMALL_ROWS, [None] * len(widths)
    for j in sorted(range(len(widths)), key=lambda q: -widths[q]):
        row = fill.index(min(fill))
        place[j] = (row, fill[row])
        fill[row] += -(-widths[j] // LANE) * LANE
    return place, max(fill)


def _pack_small(vecs, loss):
    place, total = _small_layout([v_.shape[1] for v_ in vecs])
    n = len(vecs)

    def body(*refs):
        out = jnp.zeros((SMALL_ROWS, total), F32)
        row_id = lax.broadcasted_iota(jnp.int32, (SMALL_ROWS, total), 0)
        for row in range(SMALL_ROWS):
            mine = sorted((off, j) for j, (r_, off) in enumerate(place) if r_ == row)
            pieces, at = [], 0
            for off, j in mine:
                val = refs[j][...]
                pieces.append(val)
                at = off + val.shape[1]
                pad = -val.shape[1] % LANE
                if pad:
                    pieces.append(jnp.zeros((1, pad), F32))
                    at += pad
            if total > at:
                pieces.append(jnp.zeros((1, total - at), F32))
            out = jnp.where(row_id == row, jnp.broadcast_to(jnp.concatenate(pieces, axis=1), (SMALL_ROWS, total)), out)
        refs[n + 1][...] = out

    return pl.pallas_call(body, name="pack_small", out_shape=jax.ShapeDtypeStruct((SMALL_ROWS, total), F32))(*vecs, loss)


def _adamw_small(packed, ws, ms, vs):
    n = len(ws)
    place, _ = _small_layout([w_.shape[1] for w_ in ws])

    def body(p_ref, *refs):
        w_refs, m_refs, v_refs, outs = refs[:n], refs[n:2 * n], refs[2 * n:3 * n], refs[3 * n:]
        for j in range(n):
            row, off = place[j]
            cols = pl.ds(off, ws[j].shape[1])
            _adam_update([p_ref[s, row:row + 1, cols] for s in range(N_DEV)], w_refs[j], m_refs[j], v_refs[j],
                         *outs[4 * j:4 * j + 4])
        row, off = place[n]
        total = p_ref[0, row:row + 1, off:off + 1]
        for s in range(1, N_DEV):
            total = total + p_ref[s, row:row + 1, off:off + 1]
        outs[4 * n][...] = total

    res = pl.pallas_call(
        body, name="adamw_small",
        out_shape=[jax.ShapeDtypeStruct(w_.shape, F32) for w_ in ws for _ in range(4)] + [jax.ShapeDtypeStruct((1, 1), F32)],
    )(packed, *ws, *ms, *vs)
    return [res[4 * j:4 * j + 4] for j in range(n)], res[4 * n]


def _adamw(name, parts, w, m, v, own=None, me=None):
    rows, cols = w.shape[-2:]
    lead = w.ndim - 2
    tr = rows if rows * cols * 4 * 8 <= (4 << 20) else max(8, (4 << 20) // (cols * 4 * 8) // 8 * 8)
    while rows % tr:
        tr -= 8
    shape4 = [jax.ShapeDtypeStruct(w.shape, F32)] * 4
    if own is None:
        def body(p_ref, *refs):
            _adam_update([p_ref[s] for s in range(N_DEV)], *refs)

        blk = pl.BlockSpec((None,) * lead + (tr, cols), lambda i: (0,) * lead + (i, 0))
        return pl.pallas_call(
            body, name=name, grid=(rows // tr,),
            in_specs=[pl.BlockSpec((N_DEV, tr, cols), lambda i: (0, i, 0)), blk, blk, blk], out_specs=[blk] * 4,
            out_shape=shape4, compiler_params=_params(("arbitrary",)),
        )(parts, w, m, v)

    def body_own(me_ref, p_ref, own_ref, *refs):
        mine = own_ref[...]
        _adam_update([jnp.where(me_ref[0] == s, mine, p_ref[s]) for s in range(N_DEV)], *refs)

    blk = pl.BlockSpec((None,) * lead + (tr, cols), lambda i, me_ref: (0,) * lead + (i, 0))
    return pl.pallas_call(
        body_own, name=name, out_shape=shape4,
        grid_spec=pltpu.PrefetchScalarGridSpec(
            num_scalar_prefetch=1, grid=(rows // tr,),
            in_specs=[pl.BlockSpec((N_DEV, tr, cols), lambda i, me_ref: (0, i, 0)),
                      pl.BlockSpec((None, tr, cols), lambda i, me_ref: (me_ref[0], i, 0)), blk, blk, blk],
            out_specs=[blk] * 4),
        compiler_params=_params(("arbitrary",)),
    )(me, parts, own, w, m, v)


def kernel(x, p, norm_mix_g, w_in, conv_w, shift_mu, w_lora_up, w0, a_lora_up, a0, g_lora_up, k_k, k_a, r_k, ln_x_g, ln_x_b, w_out, norm_mlp_g, w_up, w_down, norm_ple_g, w_ple_gate, w_ple_proj, norm_final_g, loss_target, m_norm_mix_g, m_w_in, m_conv_w, m_shift_mu, m_w_lora_up, m_w0, m_a_lora_up, m_a0, m_g_lora_up, m_k_k, m_k_a, m_r_k, m_ln_x_g, m_ln_x_b, m_w_out, m_norm_mlp_g, m_w_up, m_w_down, m_norm_ple_g, m_w_ple_gate, m_w_ple_proj, m_norm_final_g, v_norm_mix_g, v_w_in, v_conv_w, v_shift_mu, v_w_lora_up, v_w0, v_a_lora_up, v_a0, v_g_lora_up, v_k_k, v_k_a, v_r_k, v_ln_x_g, v_ln_x_b, v_w_out, v_norm_mlp_g, v_w_up, v_w_down, v_norm_ple_g, v_w_ple_gate, v_w_ple_proj, v_norm_final_g):
    args = dict(locals())
    wts = {n: args[n] for n in _WEIGHTS}
    mom = {n: args["m_" + n] for n in _WEIGHTS}
    var = {n: args["v_" + n] for n in _WEIGHTS}
    shard2d = lambda a: a.reshape(a.shape[-2:])
    pad_mu = lambda a: _pad_in_cols(jnp.concatenate([jnp.zeros((1, CONV_COLS), F32), a], axis=1))[:, CONV_COLS:]
    unpad_mu = lambda a: _unpad_in_cols(jnp.concatenate([jnp.zeros((1, CONV_COLS), F32), a], axis=1))[:, CONV_COLS:]

    shards = {n: shard2d(wts[n]).astype(BF16 if n in _BF16_GATHER else F32) for n in _SHARDED}
    w = {n: wts[n].reshape(1, -1) for n in _REPLICATED}
    w["shift_mu"] = pad_mu(wts["shift_mu"])

    loss, dx, grads, late_flight, early_flight, d_w_in = _local_step(
        x[0], p[0, 0], loss_target[0], w, [shards[n] for n in _EARLY], [shards[n] for n in _LATE])

    me = (4 * lax.axis_index("x") + 2 * lax.axis_index("y") + lax.axis_index("c")).astype(jnp.int32).reshape(1)
    late_sent, late_parts = _scatter_wait("late_scatter_wait", *late_flight, after=[d_w_in])
    out = {n: _adamw("adamw_" + n, prt, wts[n], mom[n], var[n], own=own, me=me)
           for n, prt, own in zip(_LATE, late_parts, late_sent)}
    early_sent, early_parts = _scatter_wait("early_scatter_wait", *early_flight, after=[dx] + [out[n][1] for n in _LATE])
    for n, prt, own in zip(_EARLY, early_parts, early_sent):
        out[n] = _adamw("adamw_" + n, prt, wts[n], mom[n], var[n], own=own, me=me)

    grads["shift_mu"] = unpad_mu(grads["shift_mu"])
    flat = lambda a: a.reshape(1, -1)
    (small_parts,) = _exchange("gather_small", [_pack_small([flat(grads[n]) for n in _REPLICATED], loss)], [False])
    small, loss_total = _adamw_small(small_parts, *[[flat(d[n]) for n in _REPLICATED] for d in (wts, mom, var)])
    for n, res in zip(_REPLICATED, small):
        out[n] = [r.reshape(wts[n].shape) for r in res]
    return (loss_total[0, 0], dx[None], *[out[n][0] for n in _WEIGHTS], *[out[n][1] for n in _WEIGHTS],
            *[out[n][2] for n in _WEIGHTS], *[out[n][3] for n in _WEIGHTS])
```

```python
import functools

import jax
import jax.numpy as jnp
from jax import lax
from jax.experimental import pallas as pl
from jax.experimental.pallas import tpu as pltpu

F32 = jnp.float32
BF16 = jnp.bfloat16

N_DEV = 8
D_MODEL = 1024
CONV_DIM = 512
RWKV_DIM = 512
HEAD_DIM = 64
N_HEADS = 8
D_FF = 4096
PLE_DIM = 256
RMS_EPS = 1e-6
GN_EPS = 64e-5
L2_EPS = 1e-12
ADAM_LR, ADAM_B1, ADAM_B2, ADAM_EPS, ADAM_WD, ADAM_STEP = 0.001, 0.9, 0.999, 1e-08, 0.01, 10

CONV_COLS = 3 * CONV_DIM
RW_PAD = 2048
IN_PAD = CONV_COLS + RW_PAD
IN_COLS = 3360
XW_OFF, XA_OFF, XG_OFF = 1536, 1664, 1792
REC_CHUNK = 128
REC_PASSES = 1
ROW_BLOCK = 256
LANE = 128
VMEM_LIMIT = 56 * 1024 * 1024


def _dims(dn, ndim):
    if ndim == 3:
        return {"nn": (((2,), (1,)), ((0,), (0,))), "nt": (((2,), (2,)), ((0,), (0,))),
                "tn": (((1,), (1,)), ((0,), (0,)))}[dn]
    return {"nn": (((1,), (0,)), ((), ())), "nt": (((1,), (1,)), ((), ())), "tn": (((0,), (0,)), ((), ()))}[dn]


def _split2(x):
    hi = x.astype(BF16)
    return hi, (x - hi.astype(F32)).astype(BF16)


def _mm_raw(x, y, dn, passes):
    f = lambda p, q: lax.dot_general(p, q, _dims(dn, x.ndim), preferred_element_type=F32)
    if passes == 1:
        return f(x.astype(BF16), y.astype(BF16))
    xh, xl = _split2(x)
    yh, yl = _split2(y)
    if passes == 2:
        return f(xh, yh) + f(xh, yl)
    return f(xh, yh) + f(xh, yl) + f(xl, yh)


@functools.partial(jax.custom_vjp, nondiff_argnums=(2, 3))
def _mm(x, y, dn, passes):
    return _mm_raw(x, y, dn, passes)


def _mm_fwd(x, y, dn, passes):
    return _mm_raw(x, y, dn, passes), (x, y)


def _mm_bwd(dn, passes, res, d):
    x, y = res
    if dn == "nn":
        return _mm(d, y, "nt", passes), _mm(x, d, "tn", passes)
    if dn == "nt":
        return _mm(d, y, "nn", passes), _mm(d, x, "tn", passes)
    return _mm(y, d, "nt", passes), _mm(x, d, "nn", passes)


_mm.defvjp(_mm_fwd, _mm_bwd)


def _head_ones():
    i = lax.broadcasted_iota(jnp.int32, (RWKV_DIM, RWKV_DIM), 0) // HEAD_DIM
    j = lax.broadcasted_iota(jnp.int32, (RWKV_DIM, RWKV_DIM), 1) // HEAD_DIM
    return (i == j).astype(BF16)


def _hsum_raw(x):
    ones = _head_ones()
    f = lambda p: lax.dot_general(p, ones, _dims("nn", 2), preferred_element_type=F32)
    x1, x2 = _split2(x)
    return f(x1) + f(x2)


@jax.custom_vjp
def _hsum(x):
    return _hsum_raw(x)


_hsum.defvjp(lambda x: (_hsum_raw(x), None), lambda _, d: (_hsum(d),))


def _sigmoid(x):
    return 0.5 + 0.5 * jnp.tanh(0.5 * x)


def _softplus(x):
    return jnp.maximum(x, 0.0) + jnp.log(1.0 + jnp.exp(-jnp.abs(x)))


def _params(sem):
    return pltpu.CompilerParams(dimension_semantics=sem, vmem_limit_bytes=VMEM_LIMIT)


def _rowwise(name, fn, rows, consts, row_outs, acc_outs=(), tr=ROW_BLOCK, halo=False, gather=()):
    rows = [r if isinstance(r, tuple) else (r, r.shape[1], 0) for r in rows]
    t_len = rows[0][0].shape[0]
    tr = min(tr, t_len)
    n_r, n_c, n_o, n_a, n_x = len(rows), len(consts), len(row_outs), len(acc_outs), len(gather)
    n_h = n_r if halo else 0
    sub = 8
    x_specs, x_shapes, x_sems = _exchange_io(gather, [False] * n_x) if n_x else ([], [], [])
    nb = t_len // tr

    def body(*refs):
        if n_x:
            n_in = n_r + n_h + n_c
            start, forward, wait = _gather_plan(refs[n_in:n_in + n_x], refs[len(refs) - 3 - n_x:len(refs) - 3], *refs[len(refs) - 3:])
            pl.when(pl.program_id(0) == 0)(start)
            refs = refs[:n_in] + refs[n_in + n_x:len(refs) - 3 - n_x]
        ins = [r[...] for r in refs[:n_r]]
        ins += [jnp.where(pl.program_id(0) == 0, 0.0, r[sub - 1:sub, :]) for r in refs[n_r:n_r + n_h]]
        ins += [r[...] for r in refs[n_r + n_h:n_r + n_h + n_c]]
        refs = refs[:n_r] + refs[n_r + n_h:]
        outs = fn(*ins)
        o_refs = refs[n_r + n_c:n_r + n_c + n_o]
        a_refs = refs[n_r + n_c + n_o:]
        for o_ref, val in zip(o_refs, outs[:n_o]):
            o_ref[...] = val.astype(o_ref.dtype)
        if n_a:
            first = pl.program_id(0) == 0

            @pl.when(first)
            def _():
                for a_ref, val in zip(a_refs, outs[n_o:]):
                    a_ref[...] = val

            @pl.when(jnp.logical_not(first))
            def _():
                for a_ref, val in zip(a_refs, outs[n_o:]):
                    a_ref[...] += val

        if n_x:
            @pl.when(pl.program_id(0) == nb - 1)
            def _():
                for j in range(n_x):
                    forward(j)
                wait()

    in_specs = [pl.BlockSpec((tr, w), functools.partial(lambda i, c: (i, c), c=cb)) for _, w, cb in rows]
    if halo:
        in_specs += [pl.BlockSpec((sub, w), functools.partial(lambda i, c: (jnp.maximum(i * (tr // sub) - 1, 0), c), c=cb))
                     for _, w, cb in rows]
    in_specs += [pl.BlockSpec(c.shape, functools.partial(lambda i, n: (0,) * n, n=c.ndim)) for c in consts]
    out_specs = [pl.BlockSpec((tr, w), lambda i: (i, 0)) for w, _ in row_outs]
    out_specs += [pl.BlockSpec(s, functools.partial(lambda i, n: (0,) * n, n=len(s))) for s in acc_outs]
    out_shape = [jax.ShapeDtypeStruct((t_len, w), dt) for w, dt in row_outs]
    out_shape += [jax.ShapeDtypeStruct(s, F32) for s in acc_outs]
    return pl.pallas_call(
        body, name=name, grid=(nb,), in_specs=in_specs + x_specs, out_specs=out_specs + x_specs,
        out_shape=out_shape + x_shapes, scratch_shapes=x_sems,
        compiler_params=pltpu.CompilerParams(dimension_semantics=("arbitrary",), vmem_limit_bytes=VMEM_LIMIT,
                                             has_side_effects=bool(n_x)),
    )(*[r[0] for r in rows], *([r[0] for r in rows] if halo else []), *consts, *gather)


def _colwise(name, fn, n_blocks, cols, prms, col_outs, prm_outs=()):
    t_len = cols[0][0].shape[0]
    n_i = len(cols) + len(prms)

    def body(*refs):
        outs = fn(*[r[...] for r in refs[:n_i]])
        for o_ref, val in zip(refs[n_i:], outs):
            o_ref[...] = val.astype(o_ref.dtype)

    spec = lambda r, w: pl.BlockSpec((r, w), lambda j: (0, j))
    in_specs = [spec(t_len, w) for _, w in cols] + [spec(a.shape[0], LANE) for a in prms]
    out_specs = [spec(t_len, bw) for _, _, bw in col_outs] + [spec(r, LANE) for r, _ in prm_outs]
    out_shape = [jax.ShapeDtypeStruct((t_len, w), dt) for w, dt, _ in col_outs]
    out_shape += [jax.ShapeDtypeStruct((r, w), F32) for r, w in prm_outs]
    return pl.pallas_call(
        body, name=name, grid=(n_blocks,), in_specs=in_specs, out_specs=out_specs, out_shape=out_shape,
        compiler_params=_params(("arbitrary",)),
    )(*[c[0] for c in cols], *prms)


def _matmul(name, a, b, dn, outs, *, tm, tn, tk, extras=(), consts=(), epilogue=None, sums=(), xch=(), xch_scatter=(),
            a_map=None, col_blocks_out=False):
    if dn == "nn":
        (m, k), n = a.shape, b.shape[1]
    elif dn == "nt":
        (m, k), n = a.shape, b.shape[0]
    else:
        (k, m), n = a.shape, b.shape[1]
    tm, tn, tk = min(tm, m), min(tn, n), min(tk, k)
    nk = k // tk
    grid = (m // tm, n // tn, nk)
    assert nk == 1 and (not sums or grid[1] == 1)
    a_spec = pl.BlockSpec((tk, tm), lambda i, j, q: (q, i)) if dn == "tn" else pl.BlockSpec((tm, tk), lambda i, j, q: (i, q))
    b_spec = pl.BlockSpec((tn, tk), lambda i, j, q: (j, q)) if dn == "nt" else pl.BlockSpec((tk, tn), lambda i, j, q: (q, j))
    o_spec = pl.BlockSpec((tm, tn), lambda i, j, q: (i, j))
    c_spec = pl.BlockSpec((1, tn), lambda i, j, q: (0, j))
    n_e, n_c, n_o, n_s, n_x = len(extras), len(consts), len(outs), len(sums), len(xch)
    x_specs, x_shapes, x_sems = _exchange_io(xch, xch_scatter) if n_x else ([], [], [])

    def body(*refs):
        a_ref, b_ref = refs[:2]
        e_refs = refs[2:2 + n_e + n_c]
        x_in = refs[2 + n_e + n_c:2 + n_e + n_c + n_x]
        rest = refs[2 + n_e + n_c + n_x:]
        o_refs, s_refs, x_out, scratch = rest[:n_o], rest[n_o:n_o + n_s], rest[n_o + n_s:n_o + n_s + n_x], rest[n_o + n_s + n_x:]
        step = (pl.program_id(0) * grid[1] + pl.program_id(1)) * nk + pl.program_id(2)
        if n_x:
            start, wait = _exchange_plan(x_in, x_out, xch_scatter, *scratch[len(scratch) - 3:])
            pl.when(step == 0)(start)
        a_blk = a_ref[...] if a_map is None else a_map(a_ref[...])
        acc = lax.dot_general(a_blk.astype(BF16), b_ref[...].astype(BF16), _dims(dn, 2), preferred_element_type=F32)
        vals = (acc,) if epilogue is None else epilogue(acc, *[e[...] for e in e_refs])
        for o_ref, val in zip(o_refs, vals[:n_o]):
            o_ref[...] = val.astype(o_ref.dtype)
        if n_s:
            @pl.when(step == 0)
            def _():
                for s_ref, val in zip(s_refs, vals[n_o:]):
                    s_ref[...] = val

            @pl.when(step > 0)
            def _():
                for s_ref, val in zip(s_refs, vals[n_o:]):
                    s_ref[...] += val

        if n_x:
            pl.when(step == grid[0] * grid[1] * nk - 1)(wait)

    plain = not (n_s or n_x)
    res = pl.pallas_call(
        body, name=name, grid=grid,
        in_specs=[a_spec, b_spec] + [o_spec] * n_e + [c_spec] * n_c + x_specs,
        out_specs=[pl.BlockSpec((None, tm, tn), lambda i, j, q: (j, i, 0)) if col_blocks_out else o_spec] * n_o
                  + [c_spec] * n_s + x_specs,
        out_shape=[jax.ShapeDtypeStruct((n // tn, m, tn) if col_blocks_out else (m, n), dt) for dt in outs] + [jax.ShapeDtypeStruct(s, F32) for s in sums] + x_shapes,
        scratch_shapes=x_sems,
        compiler_params=pltpu.CompilerParams(
            dimension_semantics=("parallel", "parallel", "arbitrary") if plain else ("arbitrary",) * 3,
            vmem_limit_bytes=VMEM_LIMIT, has_side_effects=bool(n_x)),
    )(a, b, *extras, *consts, *xch)
    return res[0] if len(res) == 1 else res


def _rms(h, g):
    return h * lax.rsqrt(jnp.mean(h * h, axis=-1, keepdims=True) + RMS_EPS) * g


def _rms_bwd(h, g, dy):
    rs = lax.rsqrt(jnp.mean(h * h, axis=-1, keepdims=True) + RMS_EPS)
    n = h * rs
    dn = dy * g
    dh = rs * (dn - n * jnp.mean(dn * n, axis=-1, keepdims=True))
    return dh, jnp.sum(dy * n, axis=0, keepdims=True)


def _rwkv_pre(k, xw, xa, xg, w0, a0, k_k, k_a, wl, al, gl):
    zw = w0 + _mm(jnp.tanh(xw), wl, "nn", 1)
    lw = -jnp.exp(-_softplus(-zw) - 0.5)
    iclr = _sigmoid(a0 + _mm(xa, al, "nn", 1))
    g = _mm(_sigmoid(xg), gl, "nn", 1)
    kk0 = k * k_k
    kk = kk0 * lax.rsqrt(jnp.maximum(_hsum(kk0 * kk0), L2_EPS * L2_EPS))
    k_h = k * (1.0 + (iclr - 1.0) * k_a)
    return lw, k_h, -kk, kk * iclr, g


def _rwkv_post(y, r, k_h, v, g, ln_g, ln_b, r_k):
    mu = _hsum(y) * (1.0 / HEAD_DIM)
    yc = y - mu
    var = _hsum(yc * yc) * (1.0 / HEAD_DIM)
    yo = yc * lax.rsqrt(var + GN_EPS) * ln_g + ln_b
    bonus = _hsum(r * k_h * r_k) * v
    return (yo + bonus) * g


def _shift_down(x, n):
    rows = lax.broadcasted_iota(jnp.int32, x.shape, 0)
    return jnp.where(rows < n, 0.0, pltpu.roll(x, n, 0))


def _shift_up(x, n):
    t_len = x.shape[0]
    rows = lax.broadcasted_iota(jnp.int32, x.shape, 0)
    return jnp.where(rows >= t_len - n, 0.0, pltpu.roll(x, t_len - n, 0))


def _exchange_plan(ins, outs, scatter, send_sems, recv_sems, local_sems):
    x, y, c = lax.axis_index("x"), lax.axis_index("y"), lax.axis_index("c")
    me = 4 * x + 2 * y + c

    def local(i):
        return pltpu.make_async_copy(ins[i].at[me] if scatter[i] else ins[i], outs[i].at[me], local_sems.at[i])

    def send(i, rel):
        return pltpu.make_async_remote_copy(
            src_ref=ins[i].at[me ^ rel] if scatter[i] else ins[i], dst_ref=outs[i].at[me],
            send_sem=send_sems.at[i, rel - 1], recv_sem=recv_sems.at[i, rel - 1],
            device_id=(x ^ (rel >> 2), y ^ ((rel >> 1) & 1), c ^ (rel & 1)), device_id_type=pl.DeviceIdType.MESH)

    def landed(i, rel):
        slot = outs[i].at[me ^ rel]
        return pltpu.make_async_remote_copy(
            src_ref=slot, dst_ref=slot, send_sem=send_sems.at[i, rel - 1], recv_sem=recv_sems.at[i, rel - 1],
            device_id=(x, y, c), device_id_type=pl.DeviceIdType.MESH)

    def start():
        for i in range(len(ins)):
            local(i).start()
            for rel in range(1, N_DEV):
                send(i, rel).start()

    def wait():
        for i in range(len(ins)):
            local(i).wait()
            for rel in range(1, N_DEV):
                landed(i, rel).wait_recv()
            for rel in range(1, N_DEV):
                send(i, rel).wait_send()

    return start, wait


def _gather_plan(ins, outs, send_sems, recv_sems, local_sems):
    x, y, c = lax.axis_index("x"), lax.axis_index("y"), lax.axis_index("c")
    me = 4 * x + 2 * y + c
    direct, chips = (1, 2, 4, 6), (2, 4, 6)

    def local(i):
        return pltpu.make_async_copy(ins[i], outs[i].at[me], local_sems.at[i])

    def send(i, rel):
        return pltpu.make_async_remote_copy(
            src_ref=ins[i], dst_ref=outs[i].at[me], send_sem=send_sems.at[i, rel - 1], recv_sem=recv_sems.at[i, rel - 1],
            device_id=(x ^ (rel >> 2), y ^ ((rel >> 1) & 1), c ^ (rel & 1)), device_id_type=pl.DeviceIdType.MESH)

    def passed(i, rel):
        slot = outs[i].at[me ^ rel]
        return pltpu.make_async_remote_copy(
            src_ref=slot, dst_ref=slot, send_sem=send_sems.at[i, rel], recv_sem=recv_sems.at[i, rel],
            device_id=(x, y, 1 - c), device_id_type=pl.DeviceIdType.MESH)

    def landed(i, rel):
        slot = outs[i].at[me ^ rel]
        return pltpu.make_async_remote_copy(
            src_ref=slot, dst_ref=slot, send_sem=send_sems.at[i, rel - 1], recv_sem=recv_sems.at[i, rel - 1],
            device_id=(x, y, c), device_id_type=pl.DeviceIdType.MESH)

    def start():
        for i in range(len(ins)):
            local(i).start()
            for rel in direct:
                send(i, rel).start()

    def forward(i):
        for rel in chips:
            landed(i, rel).wait_recv()
            passed(i, rel).start()

    def wait():
        for i in range(len(ins)):
            local(i).wait()
            for rel in (1, 3, 5, 7):
                landed(i, rel).wait_recv()
            for rel in direct:
                send(i, rel).wait_send()
            for rel in chips:
                passed(i, rel).wait_send()

    return start, forward, wait


def _exchange_io(arrays, scatter):
    n = len(arrays)
    any_spec = pl.BlockSpec(memory_space=pl.ANY)
    out_shape = [jax.ShapeDtypeStruct(a.shape if sc else (N_DEV,) + a.shape, a.dtype) for a, sc in zip(arrays, scatter)]
    sems = [pltpu.SemaphoreType.DMA((n, N_DEV - 1)), pltpu.SemaphoreType.DMA((n, N_DEV - 1)), pltpu.SemaphoreType.DMA((n,))]
    return [any_spec] * n, out_shape, sems


def _exchange(name, arrays, scatter):
    n = len(arrays)
    specs, out_shape, sems = _exchange_io(arrays, scatter)

    def body(*refs):
        if any(scatter):
            start, wait = _exchange_plan(refs[:n], refs[n:2 * n], scatter, *refs[2 * n:])
            start()
        else:
            start, forward, wait = _gather_plan(refs[:n], refs[n:2 * n], *refs[2 * n:])
            start()
            for i in range(n):
                forward(i)
        wait()

    return pl.pallas_call(
        body, name=name, in_specs=specs, out_specs=specs, out_shape=out_shape, scratch_shapes=sems,
        compiler_params=pltpu.CompilerParams(has_side_effects=True),
    )(*arrays)


def _scatter_start(name, arrays, lands):
    n = len(arrays)
    hbm = pl.BlockSpec(memory_space=pltpu.HBM)

    def body(*refs):
        ins, land, send_sems, recv_sems = refs[:n], refs[n:2 * n], refs[2 * n], refs[2 * n + 1]
        token = refs[4 * n + 2]
        x, y, c = lax.axis_index("x"), lax.axis_index("y"), lax.axis_index("c")
        me = 4 * x + 2 * y + c
        for i in range(n):
            for rel in range(1, N_DEV):
                k = i * (N_DEV - 1) + rel - 1
                pltpu.make_async_remote_copy(
                    src_ref=ins[i].at[me ^ rel], dst_ref=land[i].at[me], send_sem=send_sems.at[k],
                    recv_sem=recv_sems.at[k], device_id=(x ^ (rel >> 2), y ^ ((rel >> 1) & 1), c ^ (rel & 1)),
                    device_id_type=pl.DeviceIdType.MESH).start()
        token[...] = jnp.zeros_like(token)

    sem = pltpu.SemaphoreType.DMA((n * (N_DEV - 1),))
    bufs = [pltpu.HBM(a.shape, a.dtype) for a in list(arrays) + list(lands)]
    res = pl.pallas_call(
        body, name=name, out_shape=(sem, sem, *bufs, jax.ShapeDtypeStruct((8, LANE), F32)),
        in_specs=[hbm] * (2 * n),
        out_specs=(pl.BlockSpec(memory_space=pltpu.SEMAPHORE),) * 2 + (hbm,) * (2 * n) + (pl.BlockSpec(memory_space=pltpu.VMEM),),
        input_output_aliases={i: 2 + i for i in range(2 * n)},
        compiler_params=pltpu.CompilerParams(has_side_effects=pltpu.SideEffectType.DATAFLOW_SIDE_EFFECTING),
    )(*[pltpu.with_memory_space_constraint(a, pltpu.HBM) for a in list(arrays) + list(lands)])
    return res[0], res[1], res[2:2 + n], res[2 + n:2 + 2 * n], res[2 + 2 * n]


def _scatter_wait(name, send_sems, recv_sems, arrays, lands, after):
    n, n_after = len(arrays), len(after)
    hbm = pl.BlockSpec(memory_space=pltpu.HBM)

    def body(*refs):
        ins, land, s_sems, r_sems = refs[:n], refs[n:2 * n], refs[2 * n], refs[2 * n + 1]
        x, y, c = lax.axis_index("x"), lax.axis_index("y"), lax.axis_index("c")
        me = 4 * x + 2 * y + c
        for i in range(n):
            for rel in range(1, N_DEV):
                k = i * (N_DEV - 1) + rel - 1
                cp = pltpu.make_async_remote_copy(
                    src_ref=ins[i].at[me ^ rel], dst_ref=land[i].at[me ^ rel], send_sem=s_sems.at[k],
                    recv_sem=r_sems.at[k], device_id=(x, y, c), device_id_type=pl.DeviceIdType.MESH)
                cp.wait_send()
                cp.wait_recv()

    res = pl.pallas_call(
        body, name=name, out_shape=[pltpu.HBM(a.shape, a.dtype) for a in list(arrays) + list(lands)],
        in_specs=[hbm] * (2 * n) + [pl.BlockSpec(memory_space=pltpu.SEMAPHORE)] * 2 + [pl.BlockSpec(memory_space=pl.ANY)] * n_after,
        out_specs=[hbm] * (2 * n), input_output_aliases={i: i for i in range(2 * n)},
        compiler_params=pltpu.CompilerParams(has_side_effects=pltpu.SideEffectType.DATAFLOW_SIDE_EFFECTING),
    )(*arrays, *lands, send_sems, recv_sems, *after)
    return res[:n], res[n:]


def _tri_powers(low):
    powers, n = [low], 1
    while 2 * n < low.shape[-1]:
        powers.append(_mm(powers[-1], powers[-1], "nn", REC_PASSES))
        n *= 2
    return powers


@jax.custom_vjp
def _tri_solve(low, rhs):
    for p in _tri_powers(low):
        rhs = rhs + _mm(p, rhs, "nn", REC_PASSES)
    return rhs


def _tri_solve_fwd(low, rhs):
    powers = _tri_powers(low)
    for p in powers:
        rhs = rhs + _mm(p, rhs, "nn", REC_PASSES)
    return rhs, (powers, rhs)


def _tri_solve_bwd(res, d):
    powers, u = res
    for p in powers:
        d = d + _mm(p, d, "tn", REC_PASSES)
    return _mm(d, u, "nt", REC_PASSES), d


_tri_solve.defvjp(_tri_solve_fwd, _tri_solve_bwd)


def _heads(x):
    return jnp.stack([x[:, h * HEAD_DIM:(h + 1) * HEAD_DIM] for h in range(N_HEADS)])


def _unheads(x):
    return jnp.concatenate([x[h] for h in range(N_HEADS)], axis=-1)


def _chunk_fwd(z0, r, lw, k, v, a, b):
    c = r.shape[0]
    n_h, n_k = z0.shape[0], z0.shape[1]
    mm = functools.partial(_mm, passes=REC_PASSES)
    gram = functools.partial(_mm, passes=2)
    ti = lax.broadcasted_iota(jnp.int32, (c, c), 0)
    si = lax.broadcasted_iota(jnp.int32, (c, c), 1)
    strict, incl = si < ti, si <= ti
    cum = _mm(incl.astype(F32), lw, "nn", 3)
    cum_end = cum[c - 1:c, :]
    e_neg, e_end = jnp.exp(-cum), jnp.exp(cum_end - cum)
    x2 = jnp.concatenate([_heads(a * jnp.exp(cum - lw)), _heads(r * jnp.exp(cum))], axis=1)
    y2 = jnp.concatenate([_heads(b * e_neg), _heads(k * e_neg)], axis=1)
    vh = _heads(v)
    mask = jnp.concatenate([jnp.concatenate([strict, strict], axis=1), jnp.concatenate([incl, incl], axis=1)], axis=0)
    g2 = jnp.where(mask, gram(x2, y2, "nt"), 0.0)
    t2 = mm(x2, z0, "nn") + mm(g2[:, :, c:], vh, "nn")
    u = _tri_solve(g2[:, :c, :c], t2[:, :c])
    y = t2[:, c:] + mm(g2[:, c:, :c], u, "nn")
    ki = lax.broadcasted_iota(jnp.int32, (n_k, n_k), 0)
    kj = lax.broadcasted_iota(jnp.int32, (n_k, n_k), 1)
    dmat = jnp.where(ki == kj, jnp.broadcast_to(_heads(jnp.exp(cum_end)), (n_h, n_k, n_k)), 0.0)
    z_end = mm(dmat, z0, "nn") + mm(jnp.concatenate([_heads(b * e_end), _heads(k * e_end)], axis=1),
                                    jnp.concatenate([u, vh], axis=1), "tn")
    return _unheads(y), z_end


def _rec_params():
    return pltpu.CompilerParams(dimension_semantics=("arbitrary",), vmem_limit_bytes=VMEM_LIMIT, has_side_effects=True)


def _rec_fwd(u, lw, k, a, b, xch):
    t_len = lw.shape[0]
    c = min(REC_CHUNK, t_len)
    nc = t_len // c
    n_x = len(xch)
    x_specs, x_shapes, x_sems = _exchange_io(xch, [False] * n_x)
    sizes = [a_.size * a_.dtype.itemsize for a_ in xch]
    pass_step = [min(nc - 1, int(0.9 * nc * sum(sizes[:j + 1]) / sum(sizes)) + 1) for j in range(n_x)]

    def body(*refs):
        r_ref, v_ref, lw_ref, k_ref, a_ref, b_ref = refs[:6]
        x_in = refs[6:6 + n_x]
        y_ref, zs_ref = refs[6 + n_x:8 + n_x]
        x_out = refs[8 + n_x:8 + 2 * n_x]
        z_scr = refs[8 + 2 * n_x]
        start, forward, wait = _gather_plan(x_in, x_out, *refs[9 + 2 * n_x:])
        i = pl.program_id(0)

        @pl.when(i == 0)
        def _():
            start()
            z_scr[...] = jnp.zeros_like(z_scr)

        z0 = z_scr[...]
        zs_ref[0] = z0
        y, z_end = _chunk_fwd(z0, r_ref[...], lw_ref[...], k_ref[...], v_ref[...], a_ref[...], b_ref[...])
        y_ref[...] = y
        z_scr[...] = z_end

        for j in range(n_x):
            pl.when(i == pass_step[j])(functools.partial(forward, j))

        @pl.when(i == nc - 1)
        def _():
            wait()

    blk = lambda cb: pl.BlockSpec((c, RWKV_DIM), functools.partial(lambda i, q: (i, q), q=cb))
    res = pl.pallas_call(
        body, name="rwkv_rec_fwd", grid=(nc,),
        in_specs=[blk(0), blk(2)] + [blk(0)] * 4 + x_specs,
        out_specs=[blk(0), pl.BlockSpec((1, N_HEADS, HEAD_DIM, HEAD_DIM), lambda i: (i, 0, 0, 0))] + x_specs,
        out_shape=[jax.ShapeDtypeStruct((t_len, RWKV_DIM), F32),
                   jax.ShapeDtypeStruct((nc, N_HEADS, HEAD_DIM, HEAD_DIM), F32)] + x_shapes,
        scratch_shapes=[pltpu.VMEM((N_HEADS, HEAD_DIM, HEAD_DIM), F32)] + x_sems,
        compiler_params=_rec_params(),
    )(u, u, lw, k, a, b, *xch)
    return res[0], res[1], res[2:]


def _rec_bwd(u, lw, k, a, b, zs, dy):
    t_len = lw.shape[0]
    c = min(REC_CHUNK, t_len)
    nc = t_len // c

    def body(r_ref, v_ref, lw_ref, k_ref, a_ref, b_ref, zs_ref, dy_ref, *rest):
        g_refs, dz_scr = rest[:6], rest[6]

        @pl.when(pl.program_id(0) == 0)
        def _():
            dz_scr[...] = jnp.zeros_like(dz_scr)

        _, vjp = jax.vjp(_chunk_fwd, zs_ref[0], r_ref[...], lw_ref[...], k_ref[...], v_ref[...], a_ref[...], b_ref[...])
        dz0, dr, dlw, dk, dv, da, db = vjp((dy_ref[...], dz_scr[...]))
        for ref, val in zip(g_refs, (dr, dv, dlw, dk, da, db)):
            ref[...] = val
        dz_scr[...] = dz0

    blk = lambda cb: pl.BlockSpec((c, RWKV_DIM), functools.partial(lambda i, q: (nc - 1 - i, q), q=cb))
    return pl.pallas_call(
        body, name="rwkv_rec_bwd", grid=(nc,),
        in_specs=[blk(0), blk(2)] + [blk(0)] * 4
                 + [pl.BlockSpec((1, N_HEADS, HEAD_DIM, HEAD_DIM), lambda i: (nc - 1 - i, 0, 0, 0)), blk(0)],
        out_specs=[blk(0)] * 6, out_shape=[jax.ShapeDtypeStruct((t_len, RWKV_DIM), F32)] * 6,
        scratch_shapes=[pltpu.VMEM((N_HEADS, HEAD_DIM, HEAD_DIM), F32)], compiler_params=_params(("arbitrary",)),
    )(u, u, lw, k, a, b, zs, dy)


_EARLY = ["w_in", "conv_w", "w_lora_up", "a_lora_up", "g_lora_up"]
_LATE = ["w_out", "w_up", "w_down", "w_ple_gate", "w_ple_proj"]
_SHARDED = _EARLY + _LATE
_COL_SHARDED = {"w_in", "conv_w", "w_lora_up", "a_lora_up", "g_lora_up", "w_up", "w_ple_proj"}
_BF16_GATHER = {"w_in", "w_out", "w_up", "w_down", "w_ple_gate", "w_ple_proj"}
_REPLICATED = ["norm_mix_g", "shift_mu", "w0", "a0", "k_k", "k_a", "r_k", "ln_x_g", "ln_x_b", "norm_mlp_g", "norm_ple_g",
               "norm_final_g"]
_WEIGHTS = ["norm_mix_g", "w_in", "conv_w", "shift_mu", "w_lora_up", "w0", "a_lora_up", "a0", "g_lora_up", "k_k", "k_a", "r_k",
            "ln_x_g", "ln_x_b", "w_out", "norm_mlp_g", "w_up", "w_down", "norm_ple_g", "w_ple_gate", "w_ple_proj", "norm_final_g"]


def _unshard(name, g):
    if name in _COL_SHARDED:
        return jnp.moveaxis(g, 0, 1).reshape(g.shape[1], N_DEV * g.shape[2])
    return g.reshape(N_DEV * g.shape[1], g.shape[2])


def _reshard(name, full):
    if name in _COL_SHARDED:
        return jnp.moveaxis(full.reshape(full.shape[0], N_DEV, full.shape[1] // N_DEV), 1, 0)
    return full.reshape(N_DEV, full.shape[0] // N_DEV, full.shape[1])


def _pad_in_cols(a):
    z = lambda n: jnp.zeros(a.shape[:-1] + (n,), a.dtype)
    conv = [a[..., part * CONV_DIM + j * LANE:part * CONV_DIM + (j + 1) * LANE] for j in range(CONV_DIM // LANE) for part in range(3)]
    return jnp.concatenate(conv + [a[..., CONV_COLS:3136], z(64), a[..., 3136:3200], z(64), a[..., 3200:3360], z(96)], axis=-1)


def _unpad_in_cols(a):
    conv = [a[..., (3 * j + part) * LANE:(3 * j + part + 1) * LANE] for part in range(3) for j in range(CONV_DIM // LANE)]
    return jnp.concatenate(conv + [a[..., CONV_COLS:3136], a[..., 3200:3264], a[..., 3328:3488]], axis=-1)


def _assemble_w_in(g):
    n_dev, rows, cols = g.shape

    def body(g_ref, o_ref):
        o_ref[...] = _pad_in_cols(jnp.concatenate([g_ref[d] for d in range(n_dev)], axis=1))

    return pl.pallas_call(
        body, name="w_in_assemble", grid=(rows // ROW_BLOCK,),
        in_specs=[pl.BlockSpec((n_dev, ROW_BLOCK, cols), lambda i: (0, i, 0))],
        out_specs=pl.BlockSpec((ROW_BLOCK, IN_PAD), lambda i: (i, 0)),
        out_shape=jax.ShapeDtypeStruct((rows, IN_PAD), g.dtype), compiler_params=_params(("arbitrary",)),
    )(g)


def _split_w_in_grad(dw):
    rows = dw.shape[0]
    cols = IN_COLS // N_DEV

    def body(d_ref, o_ref):
        full = _unpad_in_cols(d_ref[...])
        for d in range(N_DEV):
            o_ref[d] = full[:, cols * d:cols * (d + 1)]

    return pl.pallas_call(
        body, name="w_in_grad_split", grid=(rows // ROW_BLOCK,),
        in_specs=[pl.BlockSpec((ROW_BLOCK, IN_PAD), lambda i: (i, 0))],
        out_specs=pl.BlockSpec((N_DEV, ROW_BLOCK, cols), lambda i: (0, i, 0)),
        out_shape=jax.ShapeDtypeStruct((N_DEV, rows, cols), dw.dtype), compiler_params=_params(("arbitrary",)),
    )(dw)


def _pad_rows(a, rows):
    return jnp.concatenate([a, jnp.zeros((rows - a.shape[0],) + a.shape[1:], a.dtype)], axis=0)


SEG_W = [RWKV_DIM, RWKV_DIM, RWKV_DIM, LANE, LANE, 2 * LANE]
SEG_OFF = [0, 512, 1024, XW_OFF, XA_OFF, XG_OFF]


def _rwkv_pre_bwd(proj, u, grads, mu, small, dproj):
    t_len = u.shape[0]
    tr = min(ROW_BLOCK, t_len)
    nb = t_len // tr
    sub = 8
    n_g = len(grads)
    acc_shapes = [(1, RW_PAD)] + [(1, RWKV_DIM)] * 4 + [(LANE, RWKV_DIM), (LANE, RWKV_DIM), (2 * LANE, RWKV_DIM)]

    def body(*refs):
        seg_refs, halo_refs = refs[:6], refs[6:12]
        k_ref, xw_ref, xa_ref, xg_ref = refs[12:16]
        g_refs = refs[16:16 + n_g]
        mu_ref = refs[16 + n_g]
        prm_refs = refs[17 + n_g:24 + n_g]
        out_hbm = refs[25 + n_g]
        acc_refs = refs[26 + n_g:26 + n_g + len(acc_shapes)]
        vbuf, sems, carry = refs[26 + n_g + len(acc_shapes):]
        i = pl.program_id(0)
        blk = nb - 1 - i
        dr1, dr2, dv1, dv2, dlw, dk1, dk2, da, db, dg = [g[...] for g in g_refs]
        _, vjp = jax.vjp(_rwkv_pre, k_ref[...], xw_ref[...], xa_ref[...], xg_ref[...], *[p_[...] for p_ in prm_refs])
        dk, dxw, dxa, dxg, *dprm = vjp((dlw, dk1 + dk2, da, db, dg))
        du = jnp.concatenate([dr1 + dr2, dk, dv1 + dv2, dxw, dxa, dxg], axis=1)
        mu_v = mu_ref[...]

        @pl.when(i == 0)
        def _():
            carry[...] = jnp.zeros_like(carry)

        rows = lax.broadcasted_iota(jnp.int32, du.shape, 0)
        nxt = jnp.where(rows == tr - 1, carry[...], pltpu.roll(du, tr - 1, 0))
        d_rw = du - mu_v * du + mu_v * nxt
        d_mu = []
        for s_ref, h_ref, off, wd in zip(seg_refs, halo_refs, SEG_OFF, SEG_W):
            cur = s_ref[...]
            r0 = lax.broadcasted_iota(jnp.int32, cur.shape, 0)
            prev = jnp.where(r0 == 0, jnp.where(blk == 0, 0.0, h_ref[sub - 1:sub, :]), pltpu.roll(cur, 1, 0))
            d_mu.append(jnp.sum(du[:, off:off + wd] * (prev - cur), axis=0, keepdims=True))
        sums = [jnp.concatenate(d_mu, axis=1)] + list(dprm)

        @pl.when(i == 0)
        def _():
            for a_ref, val in zip(acc_refs, sums):
                a_ref[...] = val

        @pl.when(i > 0)
        def _():
            for a_ref, val in zip(acc_refs, sums):
                a_ref[...] += val

        carry[...] = du[0:1, :]
        slot = i % 2

        def writeback(s, b):
            return pltpu.make_async_copy(vbuf.at[s], out_hbm.at[pl.ds(b * tr, tr), pl.ds(CONV_COLS, RW_PAD)], sems.at[s])

        @pl.when(i >= 2)
        def _():
            writeback(slot, blk + 2).wait()

        vbuf[slot] = d_rw.astype(vbuf.dtype)
        writeback(slot, blk).start()

        @pl.when(i == nb - 1)
        def _():
            writeback(slot, blk).wait()
            if nb > 1:
                writeback(1 - slot, blk + 1).wait()

    rev = lambda w_, cb: pl.BlockSpec((tr, w_), functools.partial(lambda i, c: (nb - 1 - i, c), c=cb))
    halo = lambda w_, cb: pl.BlockSpec((sub, w_), functools.partial(
        lambda i, c: (jnp.maximum((nb - 1 - i) * (tr // sub) - 1, 0), c), c=cb))
    whole = lambda a: pl.BlockSpec(a.shape, functools.partial(lambda i, n: (0,) * n, n=a.ndim))
    segs = [(wd, (CONV_COLS + off) // wd) for off, wd in zip(SEG_OFF, SEG_W)]
    u_cols = [(512, 1), (LANE, XW_OFF // LANE), (LANE, XA_OFF // LANE), (2 * LANE, XG_OFF // (2 * LANE))]
    any_spec = pl.BlockSpec(memory_space=pl.ANY)
    res = pl.pallas_call(
        body, name="rwkv_pre_bwd", grid=(nb,),
        in_specs=[rev(*s) for s in segs] + [halo(*s) for s in segs] + [rev(*c) for c in u_cols]
                 + [rev(RWKV_DIM, 0)] * n_g + [whole(mu)] + [whole(p_) for p_ in small] + [any_spec],
        out_specs=[any_spec] + [pl.BlockSpec(s, functools.partial(lambda i, n: (0,) * n, n=len(s))) for s in acc_shapes],
        out_shape=[jax.ShapeDtypeStruct(dproj.shape, dproj.dtype)] + [jax.ShapeDtypeStruct(s, F32) for s in acc_shapes],
        scratch_shapes=[pltpu.VMEM((2, tr, RW_PAD), dproj.dtype), pltpu.SemaphoreType.DMA((2,)), pltpu.VMEM((1, RW_PAD), F32)],
        input_output_aliases={24 + n_g: 0},
        compiler_params=_params(("arbitrary",)),
    )(*[proj] * 12, *[u] * 4, *grads, mu, *small, dproj)
    return res


def _local_step(x, p, tgt, w, early_shards, late_shards):
    row = lambda v: v.reshape(1, -1)
    w = dict(w)

    xn1, *gathered = _rowwise("rms_mix", lambda h, g: (_rms(h, g),), [x], [w["norm_mix_g"]], [(D_MODEL, BF16)],
                              gather=early_shards)
    w.update({n: _unshard(n, g_) for n, g_ in zip(_EARLY[1:], gathered[1:])})
    w["w_in"] = _assemble_w_in(gathered[0])
    w["w_lora_up"] = _pad_rows(w["w_lora_up"], LANE)
    w["a_lora_up"] = _pad_rows(w["a_lora_up"], LANE)
    w["g_lora_up"] = _pad_rows(w["g_lora_up"], 2 * LANE)
    proj = _matmul("in_proj", xn1, w["w_in"], "nn", [F32], tm=2048, tn=512, tk=D_MODEL)
    n_cb = CONV_DIM // LANE

    def conv_fwd(blk, cw):
        gb, gc, hx = blk[:, :LANE], blk[:, LANE:2 * LANE], blk[:, 2 * LANE:]
        uu = gc * hx
        return (gb * (uu * cw[2:3] + _shift_down(uu, 1) * cw[1:2] + _shift_down(uu, 2) * cw[0:1]),)

    (y_conv,) = _colwise("conv_fwd", conv_fwd, n_cb, [(proj, 3 * LANE)], [w["conv_w"]], [(CONV_DIM, BF16, LANE)])

    small = [w["w0"], w["a0"], w["k_k"], w["k_a"], w["w_lora_up"], w["a_lora_up"], w["g_lora_up"]]
    def pre_fwd(*xs):
        cur, prev_rows, mu, prm = xs[:6], xs[6:12], xs[12], xs[13:]
        segs = []
        for c_, p_, off, wd in zip(cur, prev_rows, SEG_OFF, SEG_W):
            rows = lax.broadcasted_iota(jnp.int32, c_.shape, 0)
            prev = jnp.where(rows == 0, p_, pltpu.roll(c_, 1, 0))
            segs.append(c_ + mu[:, off:off + wd] * (prev - c_))
        return (jnp.concatenate(segs, axis=1),) + tuple(_rwkv_pre(segs[1], segs[3], segs[4], segs[5], *prm))

    proj_segs = [(proj, wd, (CONV_COLS + off) // wd) for off, wd in zip(SEG_OFF, SEG_W)]
    big = [n for n in _LATE if n in ("w_up", "w_down")]
    rest = [n for n in _LATE if n not in big]
    u, lw, k_h, ra, rb, g, *got_rest = _rowwise(
        "rwkv_pre", pre_fwd, proj_segs, [w["shift_mu"]] + small, [(RW_PAD, F32)] + [(RWKV_DIM, F32)] * 5, halo=True,
        gather=[late_shards[_LATE.index(n)] for n in rest])
    y_rec, zs, got_big = _rec_fwd(u, lw, k_h, ra, rb, [late_shards[_LATE.index(n)] for n in big])
    for n, gathered in list(zip(rest, got_rest)) + list(zip(big, got_big)):
        w[n] = _unshard(n, gathered)
    post_c = [w["ln_x_g"], w["ln_x_b"], w["r_k"]]
    u_r, u_v = (u, 512, 0), (u, 512, 2)
    (y_rwkv,) = _rowwise("rwkv_post", lambda *xs: (_rwkv_post(*xs),), [y_rec, u_r, k_h, u_v, g], post_c, [(RWKV_DIM, BF16)],
                         tr=2 * ROW_BLOCK)
    ycat = jnp.concatenate([y_conv, y_rwkv], axis=1)
    def res_norm(acc, r_, g_):
        h = acc + r_
        return h, _rms(h, g_)

    h1, xn2 = _matmul("out_proj", ycat, w["w_out"], "nn", [F32, BF16], tm=1024, tn=D_MODEL, tk=D_MODEL, extras=[x],
                      consts=[w["norm_mlp_g"]], epilogue=res_norm)

    square = lambda h: h.astype(F32) * h.astype(F32)
    hid = _matmul("mlp_up", xn2, w["w_up"], "nn", [BF16], tm=2048, tn=1024, tk=D_MODEL,
                  epilogue=lambda acc: (jnp.maximum(acc, 0.0),))
    h2, xn3 = _matmul("mlp_down", hid, w["w_down"], "nn", [F32, BF16], tm=512, tn=D_MODEL, tk=D_FF, extras=[h1],
                      consts=[w["norm_ple_g"]], epilogue=res_norm, a_map=square)
    zg = _matmul("ple_gate", xn3, w["w_ple_gate"], "nn", [F32], tm=1024, tn=1024, tk=D_MODEL)
    pp = _matmul("ple_proj", p, w["w_ple_proj"], "nn", [F32], tm=1024, tn=1024, tk=PLE_DIM)

    def head(h2_, zg_, pp_, tg, gf):
        gate = _sigmoid(zg_)
        h3 = h2_ + gate * pp_
        out = _rms(h3, gf)
        err = out - tg
        dh3, dgf = _rms_bwd(h3, gf, err * (1.0 / D_MODEL))
        loss = jnp.sum(jnp.sum(err * err, axis=1, keepdims=True), axis=0, keepdims=True) * (0.5 / D_MODEL)
        return dh3, dh3 * pp_ * gate * (1.0 - gate), dh3 * gate, dgf, loss

    dh3, dzg, dpp, d_norm_final, loss = _rowwise(
        "head", head, [h2, zg, pp, tgt], [row(w["norm_final_g"])], [(D_MODEL, F32), (D_MODEL, BF16), (D_MODEL, BF16)],
        [(1, D_MODEL), (1, 1)], tr=2 * ROW_BLOCK)

    d_w_ple_proj = _matmul("d_ple_proj", p, dpp, "tn", [BF16], tm=PLE_DIM, tn=D_MODEL // N_DEV, tk=4096, col_blocks_out=True)
    d_w_ple_gate = _matmul("d_ple_gate", xn3, dzg, "tn", [BF16], tm=512, tn=1024, tk=4096)

    def norm_bwd(dxn, h, dres, g_):
        dh, dg = _rms_bwd(h, g_, dxn)
        dh = dh + dres
        return dh, dh, dg

    nb = dict(tm=512, tn=D_MODEL, epilogue=norm_bwd, sums=[(1, D_MODEL)])
    dh2, dh2_b, d_norm_ple = _matmul("dx_ple_gate", dzg, w["w_ple_gate"], "nt", [F32, BF16], tk=D_MODEL,
                                     extras=[h2, dh3], consts=[w["norm_ple_g"]], **nb)
    d_w_down = _matmul("d_mlp_down", hid, dh2_b, "tn", [BF16], tm=512, tn=1024, tk=4096, a_map=square)
    dpre = _matmul("dx_mlp_down", dh2_b, w["w_down"], "nt", [BF16], tm=2048, tn=1024, tk=D_MODEL, extras=[hid],
                   epilogue=lambda acc, hid_: (acc * (2.0 * hid_.astype(F32)),))
    d_w_up = _matmul("d_mlp_up", xn2, dpre, "tn", [BF16], tm=1024, tn=D_FF // N_DEV, tk=4096, col_blocks_out=True)
    dh1, dh1_b, d_norm_mlp = _matmul("dx_mlp_up", dpre, w["w_up"], "nt", [F32, BF16], tk=D_FF,
                                     extras=[h1, dh2], consts=[w["norm_mlp_g"]], **nb)
    d_w_out = _matmul("d_out_proj", ycat, dh1_b, "tn", [BF16], tm=512, tn=1024, tk=4096)
    dycat = _matmul("dx_out_proj", dh1_b, w["w_out"], "nt", [F32], tm=1024, tn=1024, tk=D_MODEL)
    late_grads = dict(w_out=d_w_out, w_up=d_w_up, w_down=d_w_down, w_ple_gate=d_w_ple_gate, w_ple_proj=d_w_ple_proj)
    late_send = [late_grads[n] if n in ("w_up", "w_ple_proj") else _reshard(n, late_grads[n]) for n in _LATE]
    *late_flight, late_token = _scatter_start("late_scatter_start", late_send, [lax.empty(a.shape, a.dtype) for a in late_send])
    conv_w_bwd = w["conv_w"] + late_token[0:1, 0:1]

    def conv_bwd(dy, blk, cw):
        gb, gc, hx = blk[:, :LANE], blk[:, LANE:2 * LANE], blk[:, 2 * LANE:]
        uu = gc * hx
        u1, u2 = _shift_down(uu, 1), _shift_down(uu, 2)
        dconv = dy * gb
        du = dconv * cw[2:3] + _shift_up(dconv, 1) * cw[1:2] + _shift_up(dconv, 2) * cw[0:1]
        s = lambda z: jnp.sum(z, axis=0, keepdims=True)
        d_blk = jnp.concatenate([dy * (uu * cw[2:3] + u1 * cw[1:2] + u2 * cw[0:1]), du * hx, du * gc], axis=1)
        return d_blk, s(dconv * u2), s(dconv * u1), s(dconv * uu)

    dproj, dcw0, dcw1, dcw2 = _colwise(
        "conv_bwd", conv_bwd, n_cb, [(dycat, LANE), (proj, 3 * LANE)], [conv_w_bwd],
        [(IN_PAD, BF16, 3 * LANE)], [(1, CONV_DIM)] * 3)

    def post_bwd(dy, y, r, k_h_, v, g_, ln_g, ln_b, r_k):
        _, vjp = jax.vjp(_rwkv_post, y, r, k_h_, v, g_, ln_g, ln_b, r_k)
        return vjp(dy)

    dy_rec, dr_p, dk_p, dv_p, dg, d_ln_g, d_ln_b, d_r_k = _rowwise(
        "rwkv_post_bwd", post_bwd, [(dycat, 512, 1), y_rec, u_r, k_h, u_v, g], post_c,
        [(RWKV_DIM, F32)] * 5, [(1, RWKV_DIM)] * 3)
    dr_r, dv_r, dlw, dk_r, da, db = _rec_bwd(u, lw, k_h, ra, rb, zs, dy_rec)

    dproj, d_mu, d_w0, d_a0, d_k_k, d_k_a, d_wl, d_al, d_gl = _rwkv_pre_bwd(
        proj, u, [dr_p, dr_r, dv_p, dv_r, dlw, dk_p, dk_r, da, db, dg], w["shift_mu"], small, dproj)
    d_w_in = _matmul("d_in_proj", xn1, dproj, "tn", [BF16], tm=1024, tn=896, tk=4096)
    early_grads = dict(conv_w=jnp.concatenate([dcw0, dcw1, dcw2], axis=0),
                       w_lora_up=d_wl[:64], a_lora_up=d_al[:64], g_lora_up=d_gl[:160])
    early_send = [_split_w_in_grad(d_w_in)] + [_reshard(n, early_grads[n]) for n in _EARLY[1:]]
    *early_flight, token = _scatter_start("early_scatter_start", early_send, [lax.empty(a.shape, a.dtype) for a in early_send])
    dx, d_norm_mix = _matmul(
        "dx_in_proj", dproj, w["w_in"], "nt", [F32], tk=IN_PAD, extras=[x, dh1], consts=[w["norm_mix_g"] + token[0:1, 0:1]],
        **dict(nb, epilogue=lambda *a: norm_bwd(*a)[1:]))

    grads = dict(
        norm_mix_g=d_norm_mix, shift_mu=d_mu, w0=d_w0, a0=d_a0, k_k=d_k_k, k_a=d_k_a, r_k=d_r_k,
        ln_x_g=d_ln_g, ln_x_b=d_ln_b, norm_mlp_g=d_norm_mlp, norm_ple_g=d_norm_ple, norm_final_g=d_norm_final)
    return loss, dx, grads, late_flight, early_flight, d_w_in


def _adam_update(partials, w_ref, m_ref, v_ref, g_ref, d_ref, nm_ref, nv_ref):
    g = partials[0].astype(F32)
    for part in partials[1:]:
        g = g + part.astype(F32)
    nm =ADAM_B1 * m_ref[...] + (1.0 - ADAM_B1) * g
    nv = ADAM_B2 * v_ref[...] + (1.0 - ADAM_B2) * (g * g)
    m_hat = nm / (1.0 - ADAM_B1 ** ADAM_STEP)
    v_hat = nv / (1.0 - ADAM_B2 ** ADAM_STEP)
    g_ref[...] = g
    d_ref[...] = -ADAM_LR * (m_hat / (jnp.sqrt(v_hat) + ADAM_EPS) + ADAM_WD * w_ref[...])
    nm_ref[...] = nm
    nv_ref[...] = nv


SMALL_ROWS = 8


def _small_layout(widths):
    widths = list(widths) + [1]
    fill, place = [0] * SMALL_ROWS, [None] * len(widths)
    for j in sorted(range(len(widths)), key=lambda q: -widths[q]):
        row = fill.index(min(fill))
        place[j] = (row, fill[row])
        fill[row] += -(-widths[j] // LANE) * LANE
    return place, max(fill)


def _pack_small(vecs, loss):
    place, total = _small_layout([v_.shape[1] for v_ in vecs])
    n = len(vecs)

    def body(*refs):
        out = jnp.zeros((SMALL_ROWS, total), F32)
        row_id = lax.broadcasted_iota(jnp.int32, (SMALL_ROWS, total), 0)
        for row in range(SMALL_ROWS):
            mine = sorted((off, j) for j, (r_, off) in enumerate(place) if r_ == row)
            pieces, at = [], 0
            for off, j in mine:
                val = refs[j][...]
                pieces.append(val)
                at = off + val.shape[1]
                pad = -val.shape[1] % LANE
                if pad:
                    pieces.append(jnp.zeros((1, pad), F32))
                    at += pad
            if total > at:
                pieces.append(jnp.zeros((1, total - at), F32))
            out = jnp.where(row_id == row, jnp.broadcast_to(jnp.concatenate(pieces, axis=1), (SMALL_ROWS, total)), out)
        refs[n + 1][...] = out

    return pl.pallas_call(body, name="pack_small", out_shape=jax.ShapeDtypeStruct((SMALL_ROWS, total), F32))(*vecs, loss)


def _adamw_small(packed, ws, ms, vs):
    n = len(ws)
    place, _ = _small_layout([w_.shape[1] for w_ in ws])

    def body(p_ref, *refs):
        w_refs, m_refs, v_refs, outs = refs[:n], refs[n:2 * n], refs[2 * n:3 * n], refs[3 * n:]
        for j in range(n):
            row, off = place[j]
            cols = pl.ds(off, ws[j].shape[1])
            _adam_update([p_ref[s, row:row + 1, cols] for s in range(N_DEV)], w_refs[j], m_refs[j], v_refs[j],
                         *outs[4 * j:4 * j + 4])
        row, off = place[n]
        total = p_ref[0, row:row + 1, off:off + 1]
        for s in range(1, N_DEV):
            total = total + p_ref[s, row:row + 1, off:off + 1]
        outs[4 * n][...] = total

    res = pl.pallas_call(
        body, name="adamw_small",
        out_shape=[jax.ShapeDtypeStruct(w_.shape, F32) for w_ in ws for _ in range(4)] + [jax.ShapeDtypeStruct((1, 1), F32)],
    )(packed, *ws, *ms, *vs)
    return [res[4 * j:4 * j + 4] for j in range(n)], res[4 * n]


def _adamw(name, parts, w, m, v, own=None, me=None):
    rows, cols = w.shape[-2:]
    lead = w.ndim - 2
    tr = rows if rows * cols * 4 * 8 <= (4 << 20) else max(8, (4 << 20) // (cols * 4 * 8) // 8 * 8)
    while rows % tr:
        tr -= 8
    shape4 = [jax.ShapeDtypeStruct(w.shape, F32)] * 4
    if own is None:
        def body(p_ref, *refs):
            _adam_update([p_ref[s] for s in range(N_DEV)], *refs)

        blk = pl.BlockSpec((None,) * lead + (tr, cols), lambda i: (0,) * lead + (i, 0))
        return pl.pallas_call(
            body, name=name, grid=(rows // tr,),
            in_specs=[pl.BlockSpec((N_DEV, tr, cols), lambda i: (0, i, 0)), blk, blk, blk], out_specs=[blk] * 4,
            out_shape=shape4, compiler_params=_params(("arbitrary",)),
        )(parts, w, m, v)

    def body_own(me_ref, p_ref, own_ref, *refs):
        mine = own_ref[...]
        _adam_update([jnp.where(me_ref[0] == s, mine, p_ref[s]) for s in range(N_DEV)], *refs)

    blk = pl.BlockSpec((None,) * lead + (tr, cols), lambda i, me_ref: (0,) * lead + (i, 0))
    return pl.pallas_call(
        body_own, name=name, out_shape=shape4,
        grid_spec=pltpu.PrefetchScalarGridSpec(
            num_scalar_prefetch=1, grid=(rows // tr,),
            in_specs=[pl.BlockSpec((N_DEV, tr, cols), lambda i, me_ref: (0, i, 0)),
                      pl.BlockSpec((None, tr, cols), lambda i, me_ref: (me_ref[0], i, 0)), blk, blk, blk],
            out_specs=[blk] * 4),
        compiler_params=_params(("arbitrary",)),
    )(me, parts, own, w, m, v)


def kernel(x, p, norm_mix_g, w_in, conv_w, shift_mu, w_lora_up, w0, a_lora_up, a0, g_lora_up, k_k, k_a, r_k, ln_x_g, ln_x_b, w_out, norm_mlp_g, w_up, w_down, norm_ple_g, w_ple_gate, w_ple_proj, norm_final_g, loss_target, m_norm_mix_g, m_w_in, m_conv_w, m_shift_mu, m_w_lora_up, m_w0, m_a_lora_up, m_a0, m_g_lora_up, m_k_k, m_k_a, m_r_k, m_ln_x_g, m_ln_x_b, m_w_out, m_norm_mlp_g, m_w_up, m_w_down, m_norm_ple_g, m_w_ple_gate, m_w_ple_proj, m_norm_final_g, v_norm_mix_g, v_w_in, v_conv_w, v_shift_mu, v_w_lora_up, v_w0, v_a_lora_up, v_a0, v_g_lora_up, v_k_k, v_k_a, v_r_k, v_ln_x_g, v_ln_x_b, v_w_out, v_norm_mlp_g, v_w_up, v_w_down, v_norm_ple_g, v_w_ple_gate, v_w_ple_proj, v_norm_final_g):
    args = dict(locals())
    wts = {n: args[n] for n in _WEIGHTS}
    mom = {n: args["m_" + n] for n in _WEIGHTS}
    var = {n: args["v_" + n] for n in _WEIGHTS}
    shard2d = lambda a: a.reshape(a.shape[-2:])
    pad_mu = lambda a: _pad_in_cols(jnp.concatenate([jnp.zeros((1, CONV_COLS), F32), a], axis=1))[:, CONV_COLS:]
    unpad_mu = lambda a: _unpad_in_cols(jnp.concatenate([jnp.zeros((1, CONV_COLS), F32), a], axis=1))[:, CONV_COLS:]

    shards = {n: shard2d(wts[n]).astype(BF16 if n in _BF16_GATHER else F32) for n in _SHARDED}
    w = {n: wts[n].reshape(1, -1) for n in _REPLICATED}
    w["shift_mu"] = pad_mu(wts["shift_mu"])

    loss, dx, grads, late_flight, early_flight, d_w_in = _local_step(
        x[0], p[0, 0], loss_target[0], w, [shards[n] for n in _EARLY], [shards[n] for n in _LATE])

    me = (4 * lax.axis_index("x") + 2 * lax.axis_index("y") + lax.axis_index("c")).astype(jnp.int32).reshape(1)
    late_sent, late_parts = _scatter_wait("late_scatter_wait", *late_flight, after=[d_w_in])
    out = {n: _adamw("adamw_" + n, prt, wts[n], mom[n], var[n], own=own, me=me)
           for n, prt, own in zip(_LATE, late_parts, late_sent)}
    early_sent, early_parts = _scatter_wait("early_scatter_wait", *early_flight, after=[dx] + [out[n][1] for n in _LATE])
    for n, prt, own in zip(_EARLY, early_parts, early_sent):
        out[n] = _adamw("adamw_" + n, prt, wts[n], mom[n], var[n], own=own, me=me)

    grads["shift_mu"] = unpad_mu(grads["shift_mu"])
    flat = lambda a: a.reshape(1, -1)
    (small_parts,) = _exchange("gather_small", [_pack_small([flat(grads[n]) for n in _REPLICATED], loss)], [False])
    small, loss_total = _adamw_small(small_parts, *[[flat(d[n]) for n in _REPLICATED] for d in (wts, mom, var)])
    for n, res in zip(_REPLICATED, small):
        out[n] = [r.reshape(wts[n].shape) for r in res]
    return (loss_total[0, 0], dx[None], *[out[n][0] for n in _WEIGHTS], *[out[n][1] for n in _WEIGHTS],
            *[out[n][2] for n in _WEIGHTS], *[out[n][3] for n in _WEIGHTS])
```

```python
import functools

import jax
import jax.numpy as jnp
from jax import lax
from jax.experimental import pallas as pl
from jax.experimental.pallas import tpu as pltpu

F32 = jnp.float32
BF16 = jnp.bfloat16

N_DEV = 8
D_MODEL = 1024
CONV_DIM = 512
RWKV_DIM = 512
HEAD_DIM = 64
N_HEADS = 8
D_FF = 4096
PLE_DIM = 256
RMS_EPS = 1e-6
GN_EPS = 64e-5
L2_EPS = 1e-12
ADAM_LR, ADAM_B1, ADAM_B2, ADAM_EPS, ADAM_WD, ADAM_STEP = 0.001, 0.9, 0.999, 1e-08, 0.01, 10

CONV_COLS = 3 * CONV_DIM
RW_PAD = 2048
IN_PAD = CONV_COLS + RW_PAD
IN_COLS = 3360
XW_OFF, XA_OFF, XG_OFF = 1536, 1664, 1792
REC_CHUNK = 128
REC_PASSES = 1
ROW_BLOCK = 256
LANE = 128
VMEM_LIMIT = 56 * 1024 * 1024


def _dims(dn, ndim):
    if ndim == 3:
        return {"nn": (((2,), (1,)), ((0,), (0,))), "nt": (((2,), (2,)), ((0,), (0,))),
                "tn": (((1,), (1,)), ((0,), (0,)))}[dn]
    return {"nn": (((1,), (0,)), ((), ())), "nt": (((1,), (1,)), ((), ())), "tn": (((0,), (0,)), ((), ()))}[dn]


def _split2(x):
    hi = x.astype(BF16)
    return hi, (x - hi.astype(F32)).astype(BF16)


def _mm_raw(x, y, dn, passes):
    f = lambda p, q: lax.dot_general(p, q, _dims(dn, x.ndim), preferred_element_type=F32)
    if passes == 1:
        return f(x.astype(BF16), y.astype(BF16))
    xh, xl = _split2(x)
    yh, yl = _split2(y)
    if passes == 2:
        return f(xh, yh) + f(xh, yl)
    return f(xh, yh) + f(xh, yl) + f(xl, yh)


@functools.partial(jax.custom_vjp, nondiff_argnums=(2, 3))
def _mm(x, y, dn, passes):
    return _mm_raw(x, y, dn, passes)


def _mm_fwd(x, y, dn, passes):
    return _mm_raw(x, y, dn, passes), (x, y)


def _mm_bwd(dn, passes, res, d):
    x, y = res
    if dn == "nn":
        return _mm(d, y, "nt", passes), _mm(x, d, "tn", passes)
    if dn == "nt":
        return _mm(d, y, "nn", passes), _mm(d, x, "tn", passes)
    return _mm(y, d, "nt", passes), _mm(x, d, "nn", passes)


_mm.defvjp(_mm_fwd, _mm_bwd)


def _head_ones():
    i = lax.broadcasted_iota(jnp.int32, (RWKV_DIM, RWKV_DIM), 0) // HEAD_DIM
    j = lax.broadcasted_iota(jnp.int32, (RWKV_DIM, RWKV_DIM), 1) // HEAD_DIM
    return (i == j).astype(BF16)


def _hsum_raw(x):
    ones = _head_ones()
    f = lambda p: lax.dot_general(p, ones, _dims("nn", 2), preferred_element_type=F32)
    x1, x2 = _split2(x)
    return f(x1) + f(x2)


@jax.custom_vjp
def _hsum(x):
    return _hsum_raw(x)


_hsum.defvjp(lambda x: (_hsum_raw(x), None), lambda _, d: (_hsum(d),))


def _sigmoid(x):
    return 0.5 + 0.5 * jnp.tanh(0.5 * x)


def _softplus(x):
    return jnp.maximum(x, 0.0) + jnp.log(1.0 + jnp.exp(-jnp.abs(x)))


def _params(sem):
    return pltpu.CompilerParams(dimension_semantics=sem, vmem_limit_bytes=VMEM_LIMIT)


def _rowwise(name, fn, rows, consts, row_outs, acc_outs=(), tr=ROW_BLOCK, halo=False, gather=()):
    rows = [r if isinstance(r, tuple) else (r, r.shape[1], 0) for r in rows]
    t_len = rows[0][0].shape[0]
    tr = min(tr, t_len)
    n_r, n_c, n_o, n_a, n_x = len(rows), len(consts), len(row_outs), len(acc_outs), len(gather)
    n_h = n_r if halo else 0
    sub = 8
    x_specs, x_shapes, x_sems = _exchange_io(gather, [False] * n_x) if n_x else ([], [], [])
    nb = t_len // tr

    def body(*refs):
        if n_x:
            n_in = n_r + n_h + n_c
            start, forward, wait = _gather_plan(refs[n_in:n_in + n_x], refs[len(refs) - 3 - n_x:len(refs) - 3], *refs[len(refs) - 3:])
            pl.when(pl.program_id(0) == 0)(start)
            refs = refs[:n_in] + refs[n_in + n_x:len(refs) - 3 - n_x]
        ins = [r[...] for r in refs[:n_r]]
        ins += [jnp.where(pl.program_id(0) == 0, 0.0, r[sub - 1:sub, :]) for r in refs[n_r:n_r + n_h]]
        ins += [r[...] for r in refs[n_r + n_h:n_r + n_h + n_c]]
        refs = refs[:n_r] + refs[n_r + n_h:]
        outs = fn(*ins)
        o_refs = refs[n_r + n_c:n_r + n_c + n_o]
        a_refs = refs[n_r + n_c + n_o:]
        for o_ref, val in zip(o_refs, outs[:n_o]):
            o_ref[...] = val.astype(o_ref.dtype)
        if n_a:
            first = pl.program_id(0) == 0

            @pl.when(first)
            def _():
                for a_ref, val in zip(a_refs, outs[n_o:]):
                    a_ref[...] = val

            @pl.when(jnp.logical_not(first))
            def _():
                for a_ref, val in zip(a_refs, outs[n_o:]):
                    a_ref[...] += val

        if n_x:
            @pl.when(pl.program_id(0) == nb - 1)
            def _():
                for j in range(n_x):
                    forward(j)
                wait()

    in_specs = [pl.BlockSpec((tr, w), functools.partial(lambda i, c: (i, c), c=cb)) for _, w, cb in rows]
    if halo:
        in_specs += [pl.BlockSpec((sub, w), functools.partial(lambda i, c: (jnp.maximum(i * (tr // sub) - 1, 0), c), c=cb))
                     for _, w, cb in rows]
    in_specs += [pl.BlockSpec(c.shape, functools.partial(lambda i, n: (0,) * n, n=c.ndim)) for c in consts]
    out_specs = [pl.BlockSpec((tr, w), lambda i: (i, 0)) for w, _ in row_outs]
    out_specs += [pl.BlockSpec(s, functools.partial(lambda i, n: (0,) * n, n=len(s))) for s in acc_outs]
    out_shape = [jax.ShapeDtypeStruct((t_len, w), dt) for w, dt in row_outs]
    out_shape += [jax.ShapeDtypeStruct(s, F32) for s in acc_outs]
    return pl.pallas_call(
        body, name=name, grid=(nb,), in_specs=in_specs + x_specs, out_specs=out_specs + x_specs,
        out_shape=out_shape + x_shapes, scratch_shapes=x_sems,
        compiler_params=pltpu.CompilerParams(dimension_semantics=("arbitrary",), vmem_limit_bytes=VMEM_LIMIT,
                                             has_side_effects=bool(n_x)),
    )(*[r[0] for r in rows], *([r[0] for r in rows] if halo else []), *consts, *gather)


def _colwise(name, fn, n_blocks, cols, prms, col_outs, prm_outs=()):
    t_len = cols[0][0].shape[0]
    n_i = len(cols) + len(prms)

    def body(*refs):
        outs = fn(*[r[...] for r in refs[:n_i]])
        for o_ref, val in zip(refs[n_i:], outs):
            o_ref[...] = val.astype(o_ref.dtype)

    spec = lambda r, w: pl.BlockSpec((r, w), lambda j: (0, j))
    in_specs = [spec(t_len, w) for _, w in cols] + [spec(a.shape[0], LANE) for a in prms]
    out_specs = [spec(t_len, bw) for _, _, bw in col_outs] + [spec(r, LANE) for r, _ in prm_outs]
    out_shape = [jax.ShapeDtypeStruct((t_len, w), dt) for w, dt, _ in col_outs]
    out_shape += [jax.ShapeDtypeStruct((r, w), F32) for r, w in prm_outs]
    return pl.pallas_call(
        body, name=name, grid=(n_blocks,), in_specs=in_specs, out_specs=out_specs, out_shape=out_shape,
        compiler_params=_params(("arbitrary",)),
    )(*[c[0] for c in cols], *prms)


def _matmul(name, a, b, dn, outs, *, tm, tn, tk, extras=(), consts=(), epilogue=None, sums=(), xch=(), xch_scatter=(),
            a_map=None, col_blocks_out=False):
    if dn == "nn":
        (m, k), n = a.shape, b.shape[1]
    elif dn == "nt":
        (m, k), n = a.shape, b.shape[0]
    else:
        (k, m), n = a.shape, b.shape[1]
    tm, tn, tk = min(tm, m), min(tn, n), min(tk, k)
    nk = k // tk
    grid = (m // tm, n // tn, nk)
    assert nk == 1 and (not sums or grid[1] == 1)
    a_spec = pl.BlockSpec((tk, tm), lambda i, j, q: (q, i)) if dn == "tn" else pl.BlockSpec((tm, tk), lambda i, j, q: (i, q))
    b_spec = pl.BlockSpec((tn, tk), lambda i, j, q: (j, q)) if dn == "nt" else pl.BlockSpec((tk, tn), lambda i, j, q: (q, j))
    o_spec = pl.BlockSpec((tm, tn), lambda i, j, q: (i, j))
    c_spec = pl.BlockSpec((1, tn), lambda i, j, q: (0, j))
    n_e, n_c, n_o, n_s, n_x = len(extras), len(consts), len(outs), len(sums), len(xch)
    x_specs, x_shapes, x_sems = _exchange_io(xch, xch_scatter) if n_x else ([], [], [])

    def body(*refs):
        a_ref, b_ref = refs[:2]
        e_refs = refs[2:2 + n_e + n_c]
        x_in = refs[2 + n_e + n_c:2 + n_e + n_c + n_x]
        rest = refs[2 + n_e + n_c + n_x:]
        o_refs, s_refs, x_out, scratch = rest[:n_o], rest[n_o:n_o + n_s], rest[n_o + n_s:n_o + n_s + n_x], rest[n_o + n_s + n_x:]
        step = (pl.program_id(0) * grid[1] + pl.program_id(1)) * nk + pl.program_id(2)
        if n_x:
            start, wait = _exchange_plan(x_in, x_out, xch_scatter, *scratch[len(scratch) - 3:])
            pl.when(step == 0)(start)
        a_blk = a_ref[...] if a_map is None else a_map(a_ref[...])
        acc = lax.dot_general(a_blk.astype(BF16), b_ref[...].astype(BF16), _dims(dn, 2), preferred_element_type=F32)
        vals = (acc,) if epilogue is None else epilogue(acc, *[e[...] for e in e_refs])
        for o_ref, val in zip(o_refs, vals[:n_o]):
            o_ref[...] = val.astype(o_ref.dtype)
        if n_s:
            @pl.when(step == 0)
            def _():
                for s_ref, val in zip(s_refs, vals[n_o:]):
                    s_ref[...] = val

            @pl.when(step > 0)
            def _():
                for s_ref, val in zip(s_refs, vals[n_o:]):
                    s_ref[...] += val

        if n_x:
            pl.when(step == grid[0] * grid[1] * nk - 1)(wait)

    plain = not (n_s or n_x)
    res = pl.pallas_call(
        body, name=name, grid=grid,
        in_specs=[a_spec, b_spec] + [o_spec] * n_e + [c_spec] * n_c + x_specs,
        out_specs=[pl.BlockSpec((None, tm, tn), lambda i, j, q: (j, i, 0)) if col_blocks_out else o_spec] * n_o
                  + [c_spec] * n_s + x_specs,
        out_shape=[jax.ShapeDtypeStruct((n // tn, m, tn) if col_blocks_out else (m, n), dt) for dt in outs] + [jax.ShapeDtypeStruct(s, F32) for s in sums] + x_shapes,
        scratch_shapes=x_sems,
        compiler_params=pltpu.CompilerParams(
            dimension_semantics=("parallel", "parallel", "arbitrary") if plain else ("arbitrary",) * 3,
            vmem_limit_bytes=VMEM_LIMIT, has_side_effects=bool(n_x)),
    )(a, b, *extras, *consts, *xch)
    return res[0] if len(res) == 1 else res


def _rms(h, g):
    return h * lax.rsqrt(jnp.mean(h * h, axis=-1, keepdims=True) + RMS_EPS) * g


def _rms_bwd(h, g, dy):
    rs = lax.rsqrt(jnp.mean(h * h, axis=-1, keepdims=True) + RMS_EPS)
    n = h * rs
    dn = dy * g
    dh = rs * (dn - n * jnp.mean(dn * n, axis=-1, keepdims=True))
    return dh, jnp.sum(dy * n, axis=0, keepdims=True)


def _rwkv_pre(k, xw, xa, xg, w0, a0, k_k, k_a, wl, al, gl):
    zw = w0 + _mm(jnp.tanh(xw), wl, "nn", 1)
    lw = -jnp.exp(-_softplus(-zw) - 0.5)
    iclr = _sigmoid(a0 + _mm(xa, al, "nn", 1))
    g = _mm(_sigmoid(xg), gl, "nn", 1)
    kk0 = k * k_k
    kk = kk0 * lax.rsqrt(jnp.maximum(_hsum(kk0 * kk0), L2_EPS * L2_EPS))
    k_h = k * (1.0 + (iclr - 1.0) * k_a)
    return lw, k_h, -kk, kk * iclr, g


def _rwkv_post(y, r, k_h, v, g, ln_g, ln_b, r_k):
    mu = _hsum(y) * (1.0 / HEAD_DIM)
    yc = y - mu
    var = _hsum(yc * yc) * (1.0 / HEAD_DIM)
    yo = yc * lax.rsqrt(var + GN_EPS) * ln_g + ln_b
    bonus = _hsum(r * k_h * r_k) * v
    return (yo + bonus) * g


def _shift_down(x, n):
    rows = lax.broadcasted_iota(jnp.int32, x.shape, 0)
    return jnp.where(rows < n, 0.0, pltpu.roll(x, n, 0))


def _shift_up(x, n):
    t_len = x.shape[0]
    rows = lax.broadcasted_iota(jnp.int32, x.shape, 0)
    return jnp.where(rows >= t_len - n, 0.0, pltpu.roll(x, t_len - n, 0))


def _exchange_plan(ins, outs, scatter, send_sems, recv_sems, local_sems):
    x, y, c = lax.axis_index("x"), lax.axis_index("y"), lax.axis_index("c")
    me = 4 * x + 2 * y + c

    def local(i):
        return pltpu.make_async_copy(ins[i].at[me] if scatter[i] else ins[i], outs[i].at[me], local_sems.at[i])

    def send(i, rel):
        return pltpu.make_async_remote_copy(
            src_ref=ins[i].at[me ^ rel] if scatter[i] else ins[i], dst_ref=outs[i].at[me],
            send_sem=send_sems.at[i, rel - 1], recv_sem=recv_sems.at[i, rel - 1],
            device_id=(x ^ (rel >> 2), y ^ ((rel >> 1) & 1), c ^ (rel & 1)), device_id_type=pl.DeviceIdType.MESH)

    def landed(i, rel):
        slot = outs[i].at[me ^ rel]
        return pltpu.make_async_remote_copy(
            src_ref=slot, dst_ref=slot, send_sem=send_sems.at[i, rel - 1], recv_sem=recv_sems.at[i, rel - 1],
            device_id=(x, y, c), device_id_type=pl.DeviceIdType.MESH)

    def start():
        for i in range(len(ins)):
            local(i).start()
            for rel in range(1, N_DEV):
                send(i, rel).start()

    def wait():
        for i in range(len(ins)):
            local(i).wait()
            for rel in range(1, N_DEV):
                landed(i, rel).wait_recv()
            for rel in range(1, N_DEV):
                send(i, rel).wait_send()

    return start, wait


def _gather_plan(ins, outs, send_sems, recv_sems, local_sems):
    x, y, c = lax.axis_index("x"), lax.axis_index("y"), lax.axis_index("c")
    me = 4 * x + 2 * y + c
    direct, chips = (1, 2, 4, 6), (2, 4, 6)

    def local(i):
        return pltpu.make_async_copy(ins[i], outs[i].at[me], local_sems.at[i])

    def send(i, rel):
        return pltpu.make_async_remote_copy(
            src_ref=ins[i], dst_ref=outs[i].at[me], send_sem=send_sems.at[i, rel - 1], recv_sem=recv_sems.at[i, rel - 1],
            device_id=(x ^ (rel >> 2), y ^ ((rel >> 1) & 1), c ^ (rel & 1)), device_id_type=pl.DeviceIdType.MESH)

    def passed(i, rel):
        slot = outs[i].at[me ^ rel]
        return pltpu.make_async_remote_copy(
            src_ref=slot, dst_ref=slot, send_sem=send_sems.at[i, rel], recv_sem=recv_sems.at[i, rel],
            device_id=(x, y, 1 - c), device_id_type=pl.DeviceIdType.MESH)

    def landed(i, rel):
        slot = outs[i].at[me ^ rel]
        return pltpu.make_async_remote_copy(
            src_ref=slot, dst_ref=slot, send_sem=send_sems.at[i, rel - 1], recv_sem=recv_sems.at[i, rel - 1],
            device_id=(x, y, c), device_id_type=pl.DeviceIdType.MESH)

    def start():
        for i in range(len(ins)):
            local(i).start()
            for rel in direct:
                send(i, rel).start()

    def forward(i):
        for rel in chips:
            landed(i, rel).wait_recv()
            passed(i, rel).start()

    def wait():
        for i in range(len(ins)):
            local(i).wait()
            for rel in (1, 3, 5, 7):
                landed(i, rel).wait_recv()
            for rel in direct:
                send(i, rel).wait_send()
            for rel in chips:
                passed(i, rel).wait_send()

    return start, forward, wait


def _exchange_io(arrays, scatter):
    n = len(arrays)
    any_spec = pl.BlockSpec(memory_space=pl.ANY)
    out_shape = [jax.ShapeDtypeStruct(a.shape if sc else (N_DEV,) + a.shape, a.dtype) for a, sc in zip(arrays, scatter)]
    sems = [pltpu.SemaphoreType.DMA((n, N_DEV - 1)), pltpu.SemaphoreType.DMA((n, N_DEV - 1)), pltpu.SemaphoreType.DMA((n,))]
    return [any_spec] * n, out_shape, sems


def _exchange(name, arrays, scatter):
    n = len(arrays)
    specs, out_shape, sems = _exchange_io(arrays, scatter)

    def body(*refs):
        if any(scatter):
            start, wait = _exchange_plan(refs[:n], refs[n:2 * n], scatter, *refs[2 * n:])
            start()
        else:
            start, forward, wait = _gather_plan(refs[:n], refs[n:2 * n], *refs[2 * n:])
            start()
            for i in range(n):
                forward(i)
        wait()

    return pl.pallas_call(
        body, name=name, in_specs=specs, out_specs=specs, out_shape=out_shape, scratch_shapes=sems,
        compiler_params=pltpu.CompilerParams(has_side_effects=True),
    )(*arrays)


def _scatter_start(name, arrays, lands):
    n = len(arrays)
    hbm = pl.BlockSpec(memory_space=pltpu.HBM)

    def body(*refs):
        ins, land, send_sems, recv_sems = refs[:n], refs[n:2 * n], refs[2 * n], refs[2 * n + 1]
        token = refs[4 * n + 2]
        x, y, c = lax.axis_index("x"), lax.axis_index("y"), lax.axis_index("c")
        me = 4 * x + 2 * y + c
        for i in range(n):
            for rel in range(1, N_DEV):
                k = i * (N_DEV - 1) + rel - 1
                pltpu.make_async_remote_copy(
                    src_ref=ins[i].at[me ^ rel], dst_ref=land[i].at[me], send_sem=send_sems.at[k],
                    recv_sem=recv_sems.at[k], device_id=(x ^ (rel >> 2), y ^ ((rel >> 1) & 1), c ^ (rel & 1)),
                    device_id_type=pl.DeviceIdType.MESH).start()
        token[...] = jnp.zeros_like(token)

    sem = pltpu.SemaphoreType.DMA((n * (N_DEV - 1),))
    bufs = [pltpu.HBM(a.shape, a.dtype) for a in list(arrays) + list(lands)]
    res = pl.pallas_call(
        body, name=name, out_shape=(sem, sem, *bufs, jax.ShapeDtypeStruct((8, LANE), F32)),
        in_specs=[hbm] * (2 * n),
        out_specs=(pl.BlockSpec(memory_space=pltpu.SEMAPHORE),) * 2 + (hbm,) * (2 * n) + (pl.BlockSpec(memory_space=pltpu.VMEM),),
        input_output_aliases={i: 2 + i for i in range(2 * n)},
        compiler_params=pltpu.CompilerParams(has_side_effects=pltpu.SideEffectType.DATAFLOW_SIDE_EFFECTING),
    )(*[pltpu.with_memory_space_constraint(a, pltpu.HBM) for a in list(arrays) + list(lands)])
    return res[0], res[1], res[2:2 + n], res[2 + n:2 + 2 * n], res[2 + 2 * n]


def _scatter_wait(name, send_sems, recv_sems, arrays, lands, after):
    n, n_after = len(arrays), len(after)
    hbm = pl.BlockSpec(memory_space=pltpu.HBM)

    def body(*refs):
        ins, land, s_sems, r_sems = refs[:n], refs[n:2 * n], refs[2 * n], refs[2 * n + 1]
        x, y, c = lax.axis_index("x"), lax.axis_index("y"), lax.axis_index("c")
        me = 4 * x + 2 * y + c
        for i in range(n):
            for rel in range(1, N_DEV):
                k = i * (N_DEV - 1) + rel - 1
                cp = pltpu.make_async_remote_copy(
                    src_ref=ins[i].at[me ^ rel], dst_ref=land[i].at[me ^ rel], send_sem=s_sems.at[k],
                    recv_sem=r_sems.at[k], device_id=(x, y, c), device_id_type=pl.DeviceIdType.MESH)
                cp.wait_send()
                cp.wait_recv()

    res = pl.pallas_call(
        body, name=name, out_shape=[pltpu.HBM(a.shape, a.dtype) for a in list(arrays) + list(lands)],
        in_specs=[hbm] * (2 * n) + [pl.BlockSpec(memory_space=pltpu.SEMAPHORE)] * 2 + [pl.BlockSpec(memory_space=pl.ANY)] * n_after,
        out_specs=[hbm] * (2 * n), input_output_aliases={i: i for i in range(2 * n)},
        compiler_params=pltpu.CompilerParams(has_side_effects=pltpu.SideEffectType.DATAFLOW_SIDE_EFFECTING),
    )(*arrays, *lands, send_sems, recv_sems, *after)
    return res[:n], res[n:]


def _tri_powers(low):
    powers, n, p = [low.astype(BF16)], 1, low
    while 2 * n < low.shape[-1]:
        p = _mm(p, p, "nn", REC_PASSES)
        powers.append(p.astype(BF16))
        n *= 2
    return powers


@jax.custom_vjp
def _tri_solve(low, rhs, powers):
    del low
    for p in powers:
        rhs = rhs + _mm(p, rhs, "nn", REC_PASSES)
    return rhs


def _tri_solve_fwd(low, rhs, powers):
    out = _tri_solve(low, rhs, powers)
    return out, (powers, out)


def _tri_solve_bwd(res, d):
    powers, u = res
    for p in powers:
        d = d + _mm(p, d, "tn", REC_PASSES)
    return _mm(d, u, "nt", REC_PASSES), d, [jnp.zeros_like(p) for p in powers]


_tri_solve.defvjp(_tri_solve_fwd, _tri_solve_bwd)


def _heads(x):
    return jnp.stack([x[:, h * HEAD_DIM:(h + 1) * HEAD_DIM] for h in range(N_HEADS)])


def _unheads(x):
    return jnp.concatenate([x[h] for h in range(N_HEADS)], axis=-1)


def _chunk_fwd(z0, r, lw, k, v, a, b, powers=None):
    c = r.shape[0]
    n_h, n_k = z0.shape[0], z0.shape[1]
    mm = functools.partial(_mm, passes=REC_PASSES)
    gram = functools.partial(_mm, passes=2)
    ti = lax.broadcasted_iota(jnp.int32, (c, c), 0)
    si = lax.broadcasted_iota(jnp.int32, (c, c), 1)
    strict, incl = si < ti, si <= ti
    cum = _mm(incl.astype(F32), lw, "nn", 3)
    cum_end = cum[c - 1:c, :]
    e_neg, e_end = jnp.exp(-cum), jnp.exp(cum_end - cum)
    x2 = jnp.concatenate([_heads(a * jnp.exp(cum - lw)), _heads(r * jnp.exp(cum))], axis=1)
    y2 = jnp.concatenate([_heads(b * e_neg), _heads(k * e_neg)], axis=1)
    vh = _heads(v)
    mask = jnp.concatenate([jnp.concatenate([strict, strict], axis=1), jnp.concatenate([incl, incl], axis=1)], axis=0)
    g2 = jnp.where(mask, gram(x2, y2, "nt"), 0.0)
    t2 = mm(x2, z0, "nn") + mm(g2[:, :, c:], vh, "nn")
    low = g2[:, :c, :c]
    powers = _tri_powers(low) if powers is None else powers
    u = _tri_solve(low, t2[:, :c], powers)
    y = t2[:, c:] + mm(g2[:, c:, :c], u, "nn")
    ki = lax.broadcasted_iota(jnp.int32, (n_k, n_k), 0)
    kj = lax.broadcasted_iota(jnp.int32, (n_k, n_k), 1)
    dmat = jnp.where(ki == kj, jnp.broadcast_to(_heads(jnp.exp(cum_end)), (n_h, n_k, n_k)), 0.0)
    z_end = mm(dmat, z0, "nn") + mm(jnp.concatenate([_heads(b * e_end), _heads(k * e_end)], axis=1),
                                    jnp.concatenate([u, vh], axis=1), "tn")
    return _unheads(y), z_end, powers


def _rec_params():
    return pltpu.CompilerParams(dimension_semantics=("arbitrary",), vmem_limit_bytes=VMEM_LIMIT, has_side_effects=True)


def _rec_fwd(u, lw, k, a, b, xch):
    t_len = lw.shape[0]
    c = min(REC_CHUNK, t_len)
    nc = t_len // c
    n_x = len(xch)
    x_specs, x_shapes, x_sems = _exchange_io(xch, [False] * n_x)
    n_pow = max(1, (c - 1).bit_length())
    sizes = [a_.size * a_.dtype.itemsize for a_ in xch]
    pass_step = [min(nc - 1, int(0.9 * nc * sum(sizes[:j + 1]) / sum(sizes)) + 1) for j in range(n_x)]

    def body(*refs):
        r_ref, v_ref, lw_ref, k_ref, a_ref, b_ref = refs[:6]
        x_in = refs[6:6 + n_x]
        y_ref, zs_ref, pw_ref = refs[6 + n_x:9 + n_x]
        x_out = refs[9 + n_x:9 + 2 * n_x]
        z_scr = refs[9 + 2 * n_x]
        start, forward, wait = _gather_plan(x_in, x_out, *refs[10 + 2 * n_x:])
        i = pl.program_id(0)

        @pl.when(i == 0)
        def _():
            start()
            z_scr[...] = jnp.zeros_like(z_scr)

        z0 = z_scr[...]
        zs_ref[0] = z0
        y, z_end, powers = _chunk_fwd(z0, r_ref[...], lw_ref[...], k_ref[...], v_ref[...], a_ref[...], b_ref[...])
        y_ref[...] = y
        z_scr[...] = z_end
        pw_ref[0] = jnp.concatenate(powers, axis=0)

        for j in range(n_x):
            pl.when(i == pass_step[j])(functools.partial(forward, j))

        @pl.when(i == nc - 1)
        def _():
            wait()

    blk = lambda cb: pl.BlockSpec((c, RWKV_DIM), functools.partial(lambda i, q: (i, q), q=cb))
    res = pl.pallas_call(
        body, name="rwkv_rec_fwd", grid=(nc,),
        in_specs=[blk(0), blk(2)] + [blk(0)] * 4 + x_specs,
        out_specs=[blk(0), pl.BlockSpec((1, N_HEADS, HEAD_DIM, HEAD_DIM), lambda i: (i, 0, 0, 0)),
                   pl.BlockSpec((1, n_pow * N_HEADS, c, c), lambda i: (i, 0, 0, 0))] + x_specs,
        out_shape=[jax.ShapeDtypeStruct((t_len, RWKV_DIM), F32),
                   jax.ShapeDtypeStruct((nc, N_HEADS, HEAD_DIM, HEAD_DIM), F32),
                   jax.ShapeDtypeStruct((nc, n_pow * N_HEADS, c, c), BF16)] + x_shapes,
        scratch_shapes=[pltpu.VMEM((N_HEADS, HEAD_DIM, HEAD_DIM), F32)] + x_sems,
        compiler_params=_rec_params(),
    )(u, u, lw, k, a, b, *xch)
    return res[0], res[1], res[2], res[3:]


def _rec_bwd(u, lw, k, a, b, zs, pw, dy):
    t_len = lw.shape[0]
    c = min(REC_CHUNK, t_len)
    nc = t_len // c

    def body(r_ref, v_ref, lw_ref, k_ref, a_ref, b_ref, zs_ref, pw_ref, dy_ref, *rest):
        g_refs, dz_scr = rest[:6], rest[6]
        powers = [pw_ref[0, j * N_HEADS:(j + 1) * N_HEADS] for j in range(pw.shape[1] // N_HEADS)]
        chunk = lambda *xs: _chunk_fwd(*xs, powers=powers)[:2]

        @pl.when(pl.program_id(0) == 0)
        def _():
            dz_scr[...] = jnp.zeros_like(dz_scr)

        _, vjp = jax.vjp(chunk, zs_ref[0], r_ref[...], lw_ref[...], k_ref[...], v_ref[...], a_ref[...], b_ref[...])
        dz0, dr, dlw, dk, dv, da, db = vjp((dy_ref[...], dz_scr[...]))
        for ref, val in zip(g_refs, (dr, dv, dlw, dk, da, db)):
            ref[...] = val
        dz_scr[...] = dz0

    blk = lambda cb: pl.BlockSpec((c, RWKV_DIM), functools.partial(lambda i, q: (nc - 1 - i, q), q=cb))
    return pl.pallas_call(
        body, name="rwkv_rec_bwd", grid=(nc,),
        in_specs=[blk(0), blk(2)] + [blk(0)] * 4
                 + [pl.BlockSpec((1, N_HEADS, HEAD_DIM, HEAD_DIM), lambda i: (nc - 1 - i, 0, 0, 0)),
                    pl.BlockSpec((1,) + pw.shape[1:], lambda i: (nc - 1 - i, 0, 0, 0)), blk(0)],
        out_specs=[blk(0)] * 6, out_shape=[jax.ShapeDtypeStruct((t_len, RWKV_DIM), F32)] * 6,
        scratch_shapes=[pltpu.VMEM((N_HEADS, HEAD_DIM, HEAD_DIM), F32)], compiler_params=_params(("arbitrary",)),
    )(u, u, lw, k, a, b, zs, pw, dy)


_EARLY = ["w_in", "conv_w", "w_lora_up", "a_lora_up", "g_lora_up"]
_LATE = ["w_out", "w_up", "w_down", "w_ple_gate", "w_ple_proj"]
_SHARDED = _EARLY + _LATE
_COL_SHARDED = {"w_in", "conv_w", "w_lora_up", "a_lora_up", "g_lora_up", "w_up", "w_ple_proj"}
_BF16_GATHER = {"w_in", "w_out", "w_up", "w_down", "w_ple_gate", "w_ple_proj"}
_REPLICATED = ["norm_mix_g", "shift_mu", "w0", "a0", "k_k", "k_a", "r_k", "ln_x_g", "ln_x_b", "norm_mlp_g", "norm_ple_g",
               "norm_final_g"]
_WEIGHTS = ["norm_mix_g", "w_in", "conv_w", "shift_mu", "w_lora_up", "w0", "a_lora_up", "a0", "g_lora_up", "k_k", "k_a", "r_k",
            "ln_x_g", "ln_x_b", "w_out", "norm_mlp_g", "w_up", "w_down", "norm_ple_g", "w_ple_gate", "w_ple_proj", "norm_final_g"]


def _unshard(name, g):
    if name in _COL_SHARDED:
        return jnp.moveaxis(g, 0, 1).reshape(g.shape[1], N_DEV * g.shape[2])
    return g.reshape(N_DEV * g.shape[1], g.shape[2])


def _reshard(name, full):
    if name in _COL_SHARDED:
        return jnp.moveaxis(full.reshape(full.shape[0], N_DEV, full.shape[1] // N_DEV), 1, 0)
    return full.reshape(N_DEV, full.shape[0] // N_DEV, full.shape[1])


def _pad_in_cols(a):
    z = lambda n: jnp.zeros(a.shape[:-1] + (n,), a.dtype)
    conv = [a[..., part * CONV_DIM + j * LANE:part * CONV_DIM + (j + 1) * LANE] for j in range(CONV_DIM // LANE) for part in range(3)]
    return jnp.concatenate(conv + [a[..., CONV_COLS:3136], z(64), a[..., 3136:3200], z(64), a[..., 3200:3360], z(96)], axis=-1)


def _unpad_in_cols(a):
    conv = [a[..., (3 * j + part) * LANE:(3 * j + part + 1) * LANE] for part in range(3) for j in range(CONV_DIM // LANE)]
    return jnp.concatenate(conv + [a[..., CONV_COLS:3136], a[..., 3200:3264], a[..., 3328:3488]], axis=-1)


def _assemble_w_in(g):
    n_dev, rows, cols = g.shape

    def body(g_ref, o_ref):
        o_ref[...] = _pad_in_cols(jnp.concatenate([g_ref[d] for d in range(n_dev)], axis=1))

    return pl.pallas_call(
        body, name="w_in_assemble", grid=(rows // ROW_BLOCK,),
        in_specs=[pl.BlockSpec((n_dev, ROW_BLOCK, cols), lambda i: (0, i, 0))],
        out_specs=pl.BlockSpec((ROW_BLOCK, IN_PAD), lambda i: (i, 0)),
        out_shape=jax.ShapeDtypeStruct((rows, IN_PAD), g.dtype), compiler_params=_params(("arbitrary",)),
    )(g)


def _split_w_in_grad(dw):
    rows = dw.shape[0]
    cols = IN_COLS // N_DEV

    def body(d_ref, o_ref):
        full = _unpad_in_cols(d_ref[...])
        for d in range(N_DEV):
            o_ref[d] = full[:, cols * d:cols * (d + 1)]

    return pl.pallas_call(
        body, name="w_in_grad_split", grid=(rows // ROW_BLOCK,),
        in_specs=[pl.BlockSpec((ROW_BLOCK, IN_PAD), lambda i: (i, 0))],
        out_specs=pl.BlockSpec((N_DEV, ROW_BLOCK, cols), lambda i: (0, i, 0)),
        out_shape=jax.ShapeDtypeStruct((N_DEV, rows, cols), dw.dtype), compiler_params=_params(("arbitrary",)),
    )(dw)


def _pad_rows(a, rows):
    return jnp.concatenate([a, jnp.zeros((rows - a.shape[0],) + a.shape[1:], a.dtype)], axis=0)


SEG_W = [RWKV_DIM, RWKV_DIM, RWKV_DIM, LANE, LANE, 2 * LANE]
SEG_OFF = [0, 512, 1024, XW_OFF, XA_OFF, XG_OFF]


def _rwkv_pre_bwd(proj, u, grads, mu, small, dproj):
    t_len = u.shape[0]
    tr = min(ROW_BLOCK, t_len)
    nb = t_len // tr
    sub = 8
    n_g = len(grads)
    acc_shapes = [(1, RW_PAD)] + [(1, RWKV_DIM)] * 4 + [(LANE, RWKV_DIM), (LANE, RWKV_DIM), (2 * LANE, RWKV_DIM)]

    def body(*refs):
        seg_refs, halo_refs = refs[:6], refs[6:12]
        k_ref, xw_ref, xa_ref, xg_ref = refs[12:16]
        g_refs = refs[16:16 + n_g]
        mu_ref = refs[16 + n_g]
        prm_refs = refs[17 + n_g:24 + n_g]
        out_hbm = refs[25 + n_g]
        acc_refs = refs[26 + n_g:26 + n_g + len(acc_shapes)]
        vbuf, sems, carry = refs[26 + n_g + len(acc_shapes):]
        i = pl.program_id(0)
        blk = nb - 1 - i
        dr1, dr2, dv1, dv2, dlw, dk1, dk2, da, db, dg = [g[...] for g in g_refs]
        _, vjp = jax.vjp(_rwkv_pre, k_ref[...], xw_ref[...], xa_ref[...], xg_ref[...], *[p_[...] for p_ in prm_refs])
        dk, dxw, dxa, dxg, *dprm = vjp((dlw, dk1 + dk2, da, db, dg))
        du = jnp.concatenate([dr1 + dr2, dk, dv1 + dv2, dxw, dxa, dxg], axis=1)
        mu_v = mu_ref[...]

        @pl.when(i == 0)
        def _():
            carry[...] = jnp.zeros_like(carry)

        rows = lax.broadcasted_iota(jnp.int32, du.shape, 0)
        nxt = jnp.where(rows == tr - 1, carry[...], pltpu.roll(du, tr - 1, 0))
        d_rw = du - mu_v * du + mu_v * nxt
        d_mu = []
        for s_ref, h_ref, off, wd in zip(seg_refs, halo_refs, SEG_OFF, SEG_W):
            cur = s_ref[...]
            r0 = lax.broadcasted_iota(jnp.int32, cur.shape, 0)
            prev = jnp.where(r0 == 0, jnp.where(blk == 0, 0.0, h_ref[sub - 1:sub, :]), pltpu.roll(cur, 1, 0))
            d_mu.append(jnp.sum(du[:, off:off + wd] * (prev - cur), axis=0, keepdims=True))
        sums = [jnp.concatenate(d_mu, axis=1)] + list(dprm)

        @pl.when(i == 0)
        def _():
            for a_ref, val in zip(acc_refs, sums):
                a_ref[...] = val

        @pl.when(i > 0)
        def _():
            for a_ref, val in zip(acc_refs, sums):
                a_ref[...] += val

        carry[...] = du[0:1, :]
        slot = i % 2

        def writeback(s, b):
            return pltpu.make_async_copy(vbuf.at[s], out_hbm.at[pl.ds(b * tr, tr), pl.ds(CONV_COLS, RW_PAD)], sems.at[s])

        @pl.when(i >= 2)
        def _():
            writeback(slot, blk + 2).wait()

        vbuf[slot] = d_rw.astype(vbuf.dtype)
        writeback(slot, blk).start()

        @pl.when(i == nb - 1)
        def _():
            writeback(slot, blk).wait()
            if nb > 1:
                writeback(1 - slot, blk + 1).wait()

    rev = lambda w_, cb: pl.BlockSpec((tr, w_), functools.partial(lambda i, c: (nb - 1 - i, c), c=cb))
    halo = lambda w_, cb: pl.BlockSpec((sub, w_), functools.partial(
        lambda i, c: (jnp.maximum((nb - 1 - i) * (tr // sub) - 1, 0), c), c=cb))
    whole = lambda a: pl.BlockSpec(a.shape, functools.partial(lambda i, n: (0,) * n, n=a.ndim))
    segs = [(wd, (CONV_COLS + off) // wd) for off, wd in zip(SEG_OFF, SEG_W)]
    u_cols = [(512, 1), (LANE, XW_OFF // LANE), (LANE, XA_OFF // LANE), (2 * LANE, XG_OFF // (2 * LANE))]
    any_spec = pl.BlockSpec(memory_space=pl.ANY)
    res = pl.pallas_call(
        body, name="rwkv_pre_bwd", grid=(nb,),
        in_specs=[rev(*s) for s in segs] + [halo(*s) for s in segs] + [rev(*c) for c in u_cols]
                 + [rev(RWKV_DIM, 0)] * n_g + [whole(mu)] + [whole(p_) for p_ in small] + [any_spec],
        out_specs=[any_spec] + [pl.BlockSpec(s, functools.partial(lambda i, n: (0,) * n, n=len(s))) for s in acc_shapes],
        out_shape=[jax.ShapeDtypeStruct(dproj.shape, dproj.dtype)] + [jax.ShapeDtypeStruct(s, F32) for s in acc_shapes],
        scratch_shapes=[pltpu.VMEM((2, tr, RW_PAD), dproj.dtype), pltpu.SemaphoreType.DMA((2,)), pltpu.VMEM((1, RW_PAD), F32)],
        input_output_aliases={24 + n_g: 0},
        compiler_params=_params(("arbitrary",)),
    )(*[proj] * 12, *[u] * 4, *grads, mu, *small, dproj)
    return res


def _local_step(x, p, tgt, w, early_shards, late_shards):
    row = lambda v: v.reshape(1, -1)
    w = dict(w)

    xn1, *gathered = _rowwise("rms_mix", lambda h, g: (_rms(h, g),), [x], [w["norm_mix_g"]], [(D_MODEL, BF16)],
                              gather=early_shards)
    w.update({n: _unshard(n, g_) for n, g_ in zip(_EARLY[1:], gathered[1:])})
    w["w_in"] = _assemble_w_in(gathered[0])
    w["w_lora_up"] = _pad_rows(w["w_lora_up"], LANE)
    w["a_lora_up"] = _pad_rows(w["a_lora_up"], LANE)
    w["g_lora_up"] = _pad_rows(w["g_lora_up"], 2 * LANE)
    proj = _matmul("in_proj", xn1, w["w_in"], "nn", [F32], tm=2048, tn=512, tk=D_MODEL)
    n_cb = CONV_DIM // LANE

    def conv_fwd(blk, cw):
        gb, gc, hx = blk[:, :LANE], blk[:, LANE:2 * LANE], blk[:, 2 * LANE:]
        uu = gc * hx
        return (gb * (uu * cw[2:3] + _shift_down(uu, 1) * cw[1:2] + _shift_down(uu, 2) * cw[0:1]),)

    (y_conv,) = _colwise("conv_fwd", conv_fwd, n_cb, [(proj, 3 * LANE)], [w["conv_w"]], [(CONV_DIM, BF16, LANE)])

    small = [w["w0"], w["a0"], w["k_k"], w["k_a"], w["w_lora_up"], w["a_lora_up"], w["g_lora_up"]]
    def pre_fwd(*xs):
        cur, prev_rows, mu, prm = xs[:6], xs[6:12], xs[12], xs[13:]
        segs = []
        for c_, p_, off, wd in zip(cur, prev_rows, SEG_OFF, SEG_W):
            rows = lax.broadcasted_iota(jnp.int32, c_.shape, 0)
            prev = jnp.where(rows == 0, p_, pltpu.roll(c_, 1, 0))
            segs.append(c_ + mu[:, off:off + wd] * (prev - c_))
        return (jnp.concatenate(segs, axis=1),) + tuple(_rwkv_pre(segs[1], segs[3], segs[4], segs[5], *prm))

    proj_segs = [(proj, wd, (CONV_COLS + off) // wd) for off, wd in zip(SEG_OFF, SEG_W)]
    big = [n for n in _LATE if n in ("w_up", "w_down")]
    rest = [n for n in _LATE if n not in big]
    u, lw, k_h, ra, rb, g, *got_rest = _rowwise(
        "rwkv_pre", pre_fwd, proj_segs, [w["shift_mu"]] + small, [(RW_PAD, F32)] + [(RWKV_DIM, F32)] * 5, halo=True,
        gather=[late_shards[_LATE.index(n)] for n in rest])
    y_rec, zs, pw, got_big = _rec_fwd(u, lw, k_h, ra, rb, [late_shards[_LATE.index(n)] for n in big])
    for n, gathered in list(zip(rest, got_rest)) + list(zip(big, got_big)):
        w[n] = _unshard(n, gathered)
    post_c = [w["ln_x_g"], w["ln_x_b"], w["r_k"]]
    u_r, u_v = (u, 512, 0), (u, 512, 2)
    (y_rwkv,) = _rowwise("rwkv_post", lambda *xs: (_rwkv_post(*xs),), [y_rec, u_r, k_h, u_v, g], post_c, [(RWKV_DIM, BF16)],
                         tr=2 * ROW_BLOCK)
    ycat = jnp.concatenate([y_conv, y_rwkv], axis=1)
    def res_norm(acc, r_, g_):
        h = acc + r_
        return h, _rms(h, g_)

    h1, xn2 = _matmul("out_proj", ycat, w["w_out"], "nn", [F32, BF16], tm=1024, tn=D_MODEL, tk=D_MODEL, extras=[x],
                      consts=[w["norm_mlp_g"]], epilogue=res_norm)

    square = lambda h: h.astype(F32) * h.astype(F32)
    hid = _matmul("mlp_up", xn2, w["w_up"], "nn", [BF16], tm=2048, tn=1024, tk=D_MODEL,
                  epilogue=lambda acc: (jnp.maximum(acc, 0.0),))
    h2, xn3 = _matmul("mlp_down", hid, w["w_down"], "nn", [F32, BF16], tm=512, tn=D_MODEL, tk=D_FF, extras=[h1],
                      consts=[w["norm_ple_g"]], epilogue=res_norm, a_map=square)
    zg = _matmul("ple_gate", xn3, w["w_ple_gate"], "nn", [F32], tm=1024, tn=1024, tk=D_MODEL)
    pp = _matmul("ple_proj", p, w["w_ple_proj"], "nn", [F32], tm=1024, tn=1024, tk=PLE_DIM)

    def head(h2_, zg_, pp_, tg, gf):
        gate = _sigmoid(zg_)
        h3 = h2_ + gate * pp_
        out = _rms(h3, gf)
        err = out - tg
        dh3, dgf = _rms_bwd(h3, gf, err * (1.0 / D_MODEL))
        loss = jnp.sum(jnp.sum(err * err, axis=1, keepdims=True), axis=0, keepdims=True) * (0.5 / D_MODEL)
        return dh3, dh3 * pp_ * gate * (1.0 - gate), dh3 * gate, dgf, loss

    dh3, dzg, dpp, d_norm_final, loss = _rowwise(
        "head", head, [h2, zg, pp, tgt], [row(w["norm_final_g"])], [(D_MODEL, F32), (D_MODEL, BF16), (D_MODEL, BF16)],
        [(1, D_MODEL), (1, 1)], tr=2 * ROW_BLOCK)

    d_w_ple_proj = _matmul("d_ple_proj", p, dpp, "tn", [BF16], tm=PLE_DIM, tn=D_MODEL // N_DEV, tk=4096, col_blocks_out=True)
    d_w_ple_gate = _matmul("d_ple_gate", xn3, dzg, "tn", [BF16], tm=512, tn=1024, tk=4096)

    def norm_bwd(dxn, h, dres, g_):
        dh, dg = _rms_bwd(h, g_, dxn)
        dh = dh + dres
        return dh, dh, dg

    nb = dict(tm=512, tn=D_MODEL, epilogue=norm_bwd, sums=[(1, D_MODEL)])
    dh2, dh2_b, d_norm_ple = _matmul("dx_ple_gate", dzg, w["w_ple_gate"], "nt", [F32, BF16], tk=D_MODEL,
                                     extras=[h2, dh3], consts=[w["norm_ple_g"]], **nb)
    d_w_down = _matmul("d_mlp_down", hid, dh2_b, "tn", [BF16], tm=512, tn=1024, tk=4096, a_map=square)
    dpre = _matmul("dx_mlp_down", dh2_b, w["w_down"], "nt", [BF16], tm=2048, tn=1024, tk=D_MODEL, extras=[hid],
                   epilogue=lambda acc, hid_: (acc * (2.0 * hid_.astype(F32)),))
    d_w_up = _matmul("d_mlp_up", xn2, dpre, "tn", [BF16], tm=1024, tn=D_FF // N_DEV, tk=4096, col_blocks_out=True)
    dh1, dh1_b, d_norm_mlp = _matmul("dx_mlp_up", dpre, w["w_up"], "nt", [F32, BF16], tk=D_FF,
                                     extras=[h1, dh2], consts=[w["norm_mlp_g"]], **nb)
    d_w_out = _matmul("d_out_proj", ycat, dh1_b, "tn", [BF16], tm=512, tn=1024, tk=4096)
    dycat = _matmul("dx_out_proj", dh1_b, w["w_out"], "nt", [F32], tm=1024, tn=1024, tk=D_MODEL)
    late_grads = dict(w_out=d_w_out, w_up=d_w_up, w_down=d_w_down, w_ple_gate=d_w_ple_gate, w_ple_proj=d_w_ple_proj)
    late_send = [late_grads[n] if n in ("w_up", "w_ple_proj") else _reshard(n, late_grads[n]) for n in _LATE]
    *late_flight, late_token = _scatter_start("late_scatter_start", late_send, [lax.empty(a.shape, a.dtype) for a in late_send])
    conv_w_bwd = w["conv_w"] + late_token[0:1, 0:1]

    def conv_bwd(dy, blk, cw):
        gb, gc, hx = blk[:, :LANE], blk[:, LANE:2 * LANE], blk[:, 2 * LANE:]
        uu = gc * hx
        u1, u2 = _shift_down(uu, 1), _shift_down(uu, 2)
        dconv = dy * gb
        du = dconv * cw[2:3] + _shift_up(dconv, 1) * cw[1:2] + _shift_up(dconv, 2) * cw[0:1]
        s = lambda z: jnp.sum(z, axis=0, keepdims=True)
        d_blk = jnp.concatenate([dy * (uu * cw[2:3] + u1 * cw[1:2] + u2 * cw[0:1]), du * hx, du * gc], axis=1)
        return d_blk, s(dconv * u2), s(dconv * u1), s(dconv * uu)

    dproj, dcw0, dcw1, dcw2 = _colwise(
        "conv_bwd", conv_bwd, n_cb, [(dycat, LANE), (proj, 3 * LANE)], [conv_w_bwd],
        [(IN_PAD, BF16, 3 * LANE)], [(1, CONV_DIM)] * 3)

    def post_bwd(dy, y, r, k_h_, v, g_, ln_g, ln_b, r_k):
        _, vjp = jax.vjp(_rwkv_post, y, r, k_h_, v, g_, ln_g, ln_b, r_k)
        return vjp(dy)

    dy_rec, dr_p, dk_p, dv_p, dg, d_ln_g, d_ln_b, d_r_k = _rowwise(
        "rwkv_post_bwd", post_bwd, [(dycat, 512, 1), y_rec, u_r, k_h, u_v, g], post_c,
        [(RWKV_DIM, F32)] * 5, [(1, RWKV_DIM)] * 3)
    dr_r, dv_r, dlw, dk_r, da, db = _rec_bwd(u, lw, k_h, ra, rb, zs, pw, dy_rec)

    dproj, d_mu, d_w0, d_a0, d_k_k, d_k_a, d_wl, d_al, d_gl = _rwkv_pre_bwd(
        proj, u, [dr_p, dr_r, dv_p, dv_r, dlw, dk_p, dk_r, da, db, dg], w["shift_mu"], small, dproj)
    d_w_in = _matmul("d_in_proj", xn1, dproj, "tn", [BF16], tm=1024, tn=896, tk=4096)
    early_grads = dict(conv_w=jnp.concatenate([dcw0, dcw1, dcw2], axis=0),
                       w_lora_up=d_wl[:64], a_lora_up=d_al[:64], g_lora_up=d_gl[:160])
    early_send = [_split_w_in_grad(d_w_in)] + [_reshard(n, early_grads[n]) for n in _EARLY[1:]]
    *early_flight, token = _scatter_start("early_scatter_start", early_send, [lax.empty(a.shape, a.dtype) for a in early_send])
    dx, d_norm_mix = _matmul(
        "dx_in_proj", dproj, w["w_in"], "nt", [F32], tk=IN_PAD, extras=[x, dh1], consts=[w["norm_mix_g"] + token[0:1, 0:1]],
        **dict(nb, epilogue=lambda *a: norm_bwd(*a)[1:]))

    grads = dict(
        norm_mix_g=d_norm_mix, shift_mu=d_mu, w0=d_w0, a0=d_a0, k_k=d_k_k, k_a=d_k_a, r_k=d_r_k,
        ln_x_g=d_ln_g, ln_x_b=d_ln_b, norm_mlp_g=d_norm_mlp, norm_ple_g=d_norm_ple, norm_final_g=d_norm_final)
    return loss, dx, grads, late_flight, early_flight, d_w_in


def _adam_update(partials, w_ref, m_ref, v_ref, g_ref, d_ref, nm_ref, nv_ref):
    g = partials[0].astype(F32)
    for part in partials[1:]:
        g = g + part.astype(F32)
    nm =ADAM_B1 * m_ref[...] + (1.0 - ADAM_B1) * g
    nv = ADAM_B2 * v_ref[...] + (1.0 - ADAM_B2) * (g * g)
    m_hat = nm / (1.0 - ADAM_B1 ** ADAM_STEP)
    v_hat = nv / (1.0 - ADAM_B2 ** ADAM_STEP)
    g_ref[...] = g
    d_ref[...] = -ADAM_LR * (m_hat / (jnp.sqrt(v_hat) + ADAM_EPS) + ADAM_WD * w_ref[...])
    nm_ref[...] = nm
    nv_ref[...] = nv


SMALL_ROWS = 8


def _small_layout(widths):
    widths = list(widths) + [1]
    fill, place = [0] * SMALL_ROWS, [None] * len(widths)
    for j in sorted(range(len(widths)), key=lambda q: -widths[q]):
        row = fill.index(min(fill))
        place[j] = (row, fill[row])
        fill[row] += -(-widths[j] // LANE) * LANE
    return place, max(fill)


def _pack_small(vecs, loss):
    place, total = _small_layout([v_.shape[1] for v_ in vecs])
    n = len(vecs)

    def body(*refs):
        out = jnp.zeros((SMALL_ROWS, total), F32)
        row_id = lax.broadcasted_iota(jnp.int32, (SMALL_ROWS, total), 0)
        for row in range(SMALL_ROWS):
            mine = sorted((off, j) for j, (r_, off) in enumerate(place) if r_ == row)
            pieces, at = [], 0
            for off, j in mine:
                val = refs[j][...]
                pieces.append(val)
                at = off + val.shape[1]
                pad = -val.shape[1] % LANE
                if pad:
                    pieces.append(jnp.zeros((1, pad), F32))
                    at += pad
            if total > at:
                pieces.append(jnp.zeros((1, total - at), F32))
            out = jnp.where(row_id == row, jnp.broadcast_to(jnp.concatenate(pieces, axis=1), (SMALL_ROWS, total)), out)
        refs[n + 1][...] = out

    return pl.pallas_call(body, name="pack_small", out_shape=jax.ShapeDtypeStruct((SMALL_ROWS, total), F32))(*vecs, loss)


def _adamw_small(packed, ws, ms, vs):
    n = len(ws)
    place, _ = _small_layout([w_.shape[1] for w_ in ws])

    def body(p_ref, *refs):
        w_refs, m_refs, v_refs, outs = refs[:n], refs[n:2 * n], refs[2 * n:3 * n], refs[3 * n:]
        for j in range(n):
            row, off = place[j]
            cols = pl.ds(off, ws[j].shape[1])
            _adam_update([p_ref[s, row:row + 1, cols] for s in range(N_DEV)], w_refs[j], m_refs[j], v_refs[j],
                         *outs[4 * j:4 * j + 4])
        row, off = place[n]
        total = p_ref[0, row:row + 1, off:off + 1]
        for s in range(1, N_DEV):
            total = total + p_ref[s, row:row + 1, off:off + 1]
        outs[4 * n][...] = total

    res = pl.pallas_call(
        body, name="adamw_small",
        out_shape=[jax.ShapeDtypeStruct(w_.shape, F32) for w_ in ws for _ in range(4)] + [jax.ShapeDtypeStruct((1, 1), F32)],
    )(packed, *ws, *ms, *vs)
    return [res[4 * j:4 * j + 4] for j in range(n)], res[4 * n]


def _adamw(name, parts, w, m, v, own=None, me=None):
    rows, cols = w.shape[-2:]
    lead = w.ndim - 2
    tr = rows if rows * cols * 4 * 8 <= (4 << 20) else max(8, (4 << 20) // (cols * 4 * 8) // 8 * 8)
    while rows % tr:
        tr -= 8
    shape4 = [jax.ShapeDtypeStruct(w.shape, F32)] * 4
    if own is None:
        def body(p_ref, *refs):
            _adam_update([p_ref[s] for s in range(N_DEV)], *refs)

        blk = pl.BlockSpec((None,) * lead + (tr, cols), lambda i: (0,) * lead + (i, 0))
        return pl.pallas_call(
            body, name=name, grid=(rows // tr,),
            in_specs=[pl.BlockSpec((N_DEV, tr, cols), lambda i: (0, i, 0)), blk, blk, blk], out_specs=[blk] * 4,
            out_shape=shape4, compiler_params=_params(("arbitrary",)),
        )(parts, w, m, v)

    def body_own(me_ref, p_ref, own_ref, *refs):
        mine = own_ref[...]
        _adam_update([jnp.where(me_ref[0] == s, mine, p_ref[s]) for s in range(N_DEV)], *refs)

    blk = pl.BlockSpec((None,) * lead + (tr, cols), lambda i, me_ref: (0,) * lead + (i, 0))
    return pl.pallas_call(
        body_own, name=name, out_shape=shape4,
        grid_spec=pltpu.PrefetchScalarGridSpec(
            num_scalar_prefetch=1, grid=(rows // tr,),
            in_specs=[pl.BlockSpec((N_DEV, tr, cols), lambda i, me_ref: (0, i, 0)),
                      pl.BlockSpec((None, tr, cols), lambda i, me_ref: (me_ref[0], i, 0)), blk, blk, blk],
            out_specs=[blk] * 4),
        compiler_params=_params(("arbitrary",)),
    )(me, parts, own, w, m, v)


def kernel(x, p, norm_mix_g, w_in, conv_w, shift_mu, w_lora_up, w0, a_lora_up, a0, g_lora_up, k_k, k_a, r_k, ln_x_g, ln_x_b, w_out, norm_mlp_g, w_up, w_down, norm_ple_g, w_ple_gate, w_ple_proj, norm_final_g, loss_target, m_norm_mix_g, m_w_in, m_conv_w, m_shift_mu, m_w_lora_up, m_w0, m_a_lora_up, m_a0, m_g_lora_up, m_k_k, m_k_a, m_r_k, m_ln_x_g, m_ln_x_b, m_w_out, m_norm_mlp_g, m_w_up, m_w_down, m_norm_ple_g, m_w_ple_gate, m_w_ple_proj, m_norm_final_g, v_norm_mix_g, v_w_in, v_conv_w, v_shift_mu, v_w_lora_up, v_w0, v_a_lora_up, v_a0, v_g_lora_up, v_k_k, v_k_a, v_r_k, v_ln_x_g, v_ln_x_b, v_w_out, v_norm_mlp_g, v_w_up, v_w_down, v_norm_ple_g, v_w_ple_gate, v_w_ple_proj, v_norm_final_g):
    args = dict(locals())
    wts = {n: args[n] for n in _WEIGHTS}
    mom = {n: args["m_" + n] for n in _WEIGHTS}
    var = {n: args["v_" + n] for n in _WEIGHTS}
    shard2d = lambda a: a.reshape(a.shape[-2:])
    pad_mu = lambda a: _pad_in_cols(jnp.concatenate([jnp.zeros((1, CONV_COLS), F32), a], axis=1))[:, CONV_COLS:]
    unpad_mu = lambda a: _unpad_in_cols(jnp.concatenate([jnp.zeros((1, CONV_COLS), F32), a], axis=1))[:, CONV_COLS:]

    shards = {n: shard2d(wts[n]).astype(BF16 if n in _BF16_GATHER else F32) for n in _SHARDED}
    w = {n: wts[n].reshape(1, -1) for n in _REPLICATED}
    w["shift_mu"] = pad_mu(wts["shift_mu"])

    loss, dx, grads, late_flight, early_flight, d_w_in = _local_step(
        x[0], p[0, 0], loss_target[0], w, [shards[n] for n in _EARLY], [shards[n] for n in _LATE])

    me = (4 * lax.axis_index("x") + 2 * lax.axis_index("y") + lax.axis_index("c")).astype(jnp.int32).reshape(1)
    late_sent, late_parts = _scatter_wait("late_scatter_wait", *late_flight, after=[d_w_in])
    out = {n: _adamw("adamw_" + n, prt, wts[n], mom[n], var[n], own=own, me=me)
           for n, prt, own in zip(_LATE, late_parts, late_sent)}
    early_sent, early_parts = _scatter_wait("early_scatter_wait", *early_flight, after=[dx] + [out[n][1] for n in _LATE])
    for n, prt, own in zip(_EARLY, early_parts, early_sent):
        out[n] = _adamw("adamw_" + n, prt, wts[n], mom[n], var[n], own=own, me=me)

    grads["shift_mu"] = unpad_mu(grads["shift_mu"])
    flat = lambda a: a.reshape(1, -1)
    (small_parts,) = _exchange("gather_small", [_pack_small([flat(grads[n]) for n in _REPLICATED], loss)], [False])
    small, loss_total = _adamw_small(small_parts, *[[flat(d[n]) for n in _REPLICATED] for d in (wts, mom, var)])
    for n, res in zip(_REPLICATED, small):
        out[n] = [r.reshape(wts[n].shape) for r in res]
    return (loss_total[0, 0], dx[None], *[out[n][0] for n in _WEIGHTS], *[out[n][1] for n in _WEIGHTS],
            *[out[n][2] for n in _WEIGHTS], *[out[n][3] for n in _WEIGHTS])
```

```python
import functools

import jax
import jax.numpy as jnp
from jax import lax
from jax.experimental import pallas as pl
from jax.experimental.pallas import tpu as pltpu

F32 = jnp.float32
BF16 = jnp.bfloat16

N_DEV = 8
D_MODEL = 1024
CONV_DIM = 512
RWKV_DIM = 512
HEAD_DIM = 64
N_HEADS = 8
D_FF = 4096
PLE_DIM = 256
RMS_EPS = 1e-6
GN_EPS = 64e-5
L2_EPS = 1e-12
ADAM_LR, ADAM_B1, ADAM_B2, ADAM_EPS, ADAM_WD, ADAM_STEP = 0.001, 0.9, 0.999, 1e-08, 0.01, 10

CONV_COLS = 3 * CONV_DIM
RW_PAD = 2048
IN_PAD = CONV_COLS + RW_PAD
IN_COLS = 3360
XW_OFF, XA_OFF, XG_OFF = 1536, 1664, 1792
REC_CHUNK = 128
REC_PASSES = 1
ROW_BLOCK = 256
LANE = 128
VMEM_LIMIT = 56 * 1024 * 1024


def _dims(dn, ndim):
    if ndim == 3:
        return {"nn": (((2,), (1,)), ((0,), (0,))), "nt": (((2,), (2,)), ((0,), (0,))),
                "tn": (((1,), (1,)), ((0,), (0,)))}[dn]
    return {"nn": (((1,), (0,)), ((), ())), "nt": (((1,), (1,)), ((), ())), "tn": (((0,), (0,)), ((), ()))}[dn]


def _split2(x):
    hi = x.astype(BF16)
    return hi, (x - hi.astype(F32)).astype(BF16)


def _mm_raw(x, y, dn, passes):
    f = lambda p, q: lax.dot_general(p, q, _dims(dn, x.ndim), preferred_element_type=F32)
    if passes == 1:
        return f(x.astype(BF16), y.astype(BF16))
    xh, xl = _split2(x)
    yh, yl = _split2(y)
    if passes == 2:
        return f(xh, yh) + f(xh, yl)
    return f(xh, yh) + f(xh, yl) + f(xl, yh)


@functools.partial(jax.custom_vjp, nondiff_argnums=(2, 3))
def _mm(x, y, dn, passes):
    return _mm_raw(x, y, dn, passes)


def _mm_fwd(x, y, dn, passes):
    return _mm_raw(x, y, dn, passes), (x, y)


def _mm_bwd(dn, passes, res, d):
    x, y = res
    if dn == "nn":
        return _mm(d, y, "nt", passes), _mm(x, d, "tn", passes)
    if dn == "nt":
        return _mm(d, y, "nn", passes), _mm(d, x, "tn", passes)
    return _mm(y, d, "nt", passes), _mm(x, d, "nn", passes)


_mm.defvjp(_mm_fwd, _mm_bwd)


def _head_ones():
    i = lax.broadcasted_iota(jnp.int32, (RWKV_DIM, RWKV_DIM), 0) // HEAD_DIM
    j = lax.broadcasted_iota(jnp.int32, (RWKV_DIM, RWKV_DIM), 1) // HEAD_DIM
    return (i == j).astype(BF16)


def _hsum_raw(x):
    ones = _head_ones()
    f = lambda p: lax.dot_general(p, ones, _dims("nn", 2), preferred_element_type=F32)
    x1, x2 = _split2(x)
    return f(x1) + f(x2)


@jax.custom_vjp
def _hsum(x):
    return _hsum_raw(x)


_hsum.defvjp(lambda x: (_hsum_raw(x), None), lambda _, d: (_hsum(d),))


def _sigmoid(x):
    return 0.5 + 0.5 * jnp.tanh(0.5 * x)


def _softplus(x):
    return jnp.maximum(x, 0.0) + jnp.log(1.0 + jnp.exp(-jnp.abs(x)))


def _params(sem):
    return pltpu.CompilerParams(dimension_semantics=sem, vmem_limit_bytes=VMEM_LIMIT)


def _rowwise(name, fn, rows, consts, row_outs, acc_outs=(), tr=ROW_BLOCK, halo=False, gather=()):
    rows = [r if isinstance(r, tuple) else (r, r.shape[1], 0) for r in rows]
    t_len = rows[0][0].shape[0]
    tr = min(tr, t_len)
    n_r, n_c, n_o, n_a, n_x = len(rows), len(consts), len(row_outs), len(acc_outs), len(gather)
    n_h = n_r if halo else 0
    sub = 8
    x_specs, x_shapes, x_sems = _exchange_io(gather, [False] * n_x) if n_x else ([], [], [])
    nb = t_len // tr

    def body(*refs):
        if n_x:
            n_in = n_r + n_h + n_c
            start, forward, wait = _gather_plan(refs[n_in:n_in + n_x], refs[len(refs) - 3 - n_x:len(refs) - 3], *refs[len(refs) - 3:])
            pl.when(pl.program_id(0) == 0)(start)
            refs = refs[:n_in] + refs[n_in + n_x:len(refs) - 3 - n_x]
        ins = [r[...] for r in refs[:n_r]]
        ins += [jnp.where(pl.program_id(0) == 0, 0.0, r[sub - 1:sub, :]) for r in refs[n_r:n_r + n_h]]
        ins += [r[...] for r in refs[n_r + n_h:n_r + n_h + n_c]]
        refs = refs[:n_r] + refs[n_r + n_h:]
        outs = fn(*ins)
        o_refs = refs[n_r + n_c:n_r + n_c + n_o]
        a_refs = refs[n_r + n_c + n_o:]
        for o_ref, val in zip(o_refs, outs[:n_o]):
            o_ref[...] = val.astype(o_ref.dtype)
        if n_a:
            first = pl.program_id(0) == 0

            @pl.when(first)
            def _():
                for a_ref, val in zip(a_refs, outs[n_o:]):
                    a_ref[...] = val

            @pl.when(jnp.logical_not(first))
            def _():
                for a_ref, val in zip(a_refs, outs[n_o:]):
                    a_ref[...] += val

        if n_x:
            @pl.when(pl.program_id(0) == nb - 1)
            def _():
                for j in range(n_x):
                    forward(j)
                wait()

    in_specs = [pl.BlockSpec((tr, w), functools.partial(lambda i, c: (i, c), c=cb)) for _, w, cb in rows]
    if halo:
        in_specs += [pl.BlockSpec((sub, w), functools.partial(lambda i, c: (jnp.maximum(i * (tr // sub) - 1, 0), c), c=cb))
                     for _, w, cb in rows]
    in_specs += [pl.BlockSpec(c.shape, functools.partial(lambda i, n: (0,) * n, n=c.ndim)) for c in consts]
    out_specs = [pl.BlockSpec((tr, w), lambda i: (i, 0)) for w, _ in row_outs]
    out_specs += [pl.BlockSpec(s, functools.partial(lambda i, n: (0,) * n, n=len(s))) for s in acc_outs]
    out_shape = [jax.ShapeDtypeStruct((t_len, w), dt) for w, dt in row_outs]
    out_shape += [jax.ShapeDtypeStruct(s, F32) for s in acc_outs]
    return pl.pallas_call(
        body, name=name, grid=(nb,), in_specs=in_specs + x_specs, out_specs=out_specs + x_specs,
        out_shape=out_shape + x_shapes, scratch_shapes=x_sems,
        compiler_params=pltpu.CompilerParams(dimension_semantics=("arbitrary",), vmem_limit_bytes=VMEM_LIMIT,
                                             has_side_effects=bool(n_x)),
    )(*[r[0] for r in rows], *([r[0] for r in rows] if halo else []), *consts, *gather)


def _colwise(name, fn, n_blocks, cols, prms, col_outs, prm_outs=()):
    t_len = cols[0][0].shape[0]
    n_i = len(cols) + len(prms)

    def body(*refs):
        outs = fn(*[r[...] for r in refs[:n_i]])
        for o_ref, val in zip(refs[n_i:], outs):
            o_ref[...] = val.astype(o_ref.dtype)

    spec = lambda r, w: pl.BlockSpec((r, w), lambda j: (0, j))
    in_specs = [spec(t_len, w) for _, w in cols] + [spec(a.shape[0], LANE) for a in prms]
    out_specs = [spec(t_len, bw) for _, _, bw in col_outs] + [spec(r, LANE) for r, _ in prm_outs]
    out_shape = [jax.ShapeDtypeStruct((t_len, w), dt) for w, dt, _ in col_outs]
    out_shape += [jax.ShapeDtypeStruct((r, w), F32) for r, w in prm_outs]
    return pl.pallas_call(
        body, name=name, grid=(n_blocks,), in_specs=in_specs, out_specs=out_specs, out_shape=out_shape,
        compiler_params=_params(("arbitrary",)),
    )(*[c[0] for c in cols], *prms)


def _matmul(name, a, b, dn, outs, *, tm, tn, tk, extras=(), consts=(), epilogue=None, sums=(), xch=(), xch_scatter=(),
            a_map=None, col_blocks_out=False):
    if dn == "nn":
        (m, k), n = a.shape, b.shape[1]
    elif dn == "nt":
        (m, k), n = a.shape, b.shape[0]
    else:
        (k, m), n = a.shape, b.shape[1]
    tm, tn, tk = min(tm, m), min(tn, n), min(tk, k)
    nk = k // tk
    grid = (m // tm, n // tn, nk)
    assert nk == 1 and (not sums or grid[1] == 1)
    a_spec = pl.BlockSpec((tk, tm), lambda i, j, q: (q, i)) if dn == "tn" else pl.BlockSpec((tm, tk), lambda i, j, q: (i, q))
    b_spec = pl.BlockSpec((tn, tk), lambda i, j, q: (j, q)) if dn == "nt" else pl.BlockSpec((tk, tn), lambda i, j, q: (q, j))
    o_spec = pl.BlockSpec((tm, tn), lambda i, j, q: (i, j))
    c_spec = pl.BlockSpec((1, tn), lambda i, j, q: (0, j))
    n_e, n_c, n_o, n_s, n_x = len(extras), len(consts), len(outs), len(sums), len(xch)
    x_specs, x_shapes, x_sems = _exchange_io(xch, xch_scatter) if n_x else ([], [], [])

    def body(*refs):
        a_ref, b_ref = refs[:2]
        e_refs = refs[2:2 + n_e + n_c]
        x_in = refs[2 + n_e + n_c:2 + n_e + n_c + n_x]
        rest = refs[2 + n_e + n_c + n_x:]
        o_refs, s_refs, x_out, scratch = rest[:n_o], rest[n_o:n_o + n_s], rest[n_o + n_s:n_o + n_s + n_x], rest[n_o + n_s + n_x:]
        step = (pl.program_id(0) * grid[1] + pl.program_id(1)) * nk + pl.program_id(2)
        if n_x:
            start, wait = _exchange_plan(x_in, x_out, xch_scatter, *scratch[len(scratch) - 3:])
            pl.when(step == 0)(start)
        a_blk = a_ref[...] if a_map is None else a_map(a_ref[...])
        acc = lax.dot_general(a_blk.astype(BF16), b_ref[...].astype(BF16), _dims(dn, 2), preferred_element_type=F32)
        vals = (acc,) if epilogue is None else epilogue(acc, *[e[...] for e in e_refs])
        for o_ref, val in zip(o_refs, vals[:n_o]):
            o_ref[...] = val.astype(o_ref.dtype)
        if n_s:
            @pl.when(step == 0)
            def _():
                for s_ref, val in zip(s_refs, vals[n_o:]):
                    s_ref[...] = val

            @pl.when(step > 0)
            def _():
                for s_ref, val in zip(s_refs, vals[n_o:]):
                    s_ref[...] += val

        if n_x:
            pl.when(step == grid[0] * grid[1] * nk - 1)(wait)

    plain = not (n_s or n_x)
    res = pl.pallas_call(
        body, name=name, grid=grid,
        in_specs=[a_spec, b_spec] + [o_spec] * n_e + [c_spec] * n_c + x_specs,
        out_specs=[pl.BlockSpec((None, tm, tn), lambda i, j, q: (j, i, 0)) if col_blocks_out else o_spec] * n_o
                  + [c_spec] * n_s + x_specs,
        out_shape=[jax.ShapeDtypeStruct((n // tn, m, tn) if col_blocks_out else (m, n), dt) for dt in outs] + [jax.ShapeDtypeStruct(s, F32) for s in sums] + x_shapes,
        scratch_shapes=x_sems,
        compiler_params=pltpu.CompilerParams(
            dimension_semantics=("parallel", "parallel", "arbitrary") if plain else ("arbitrary",) * 3,
            vmem_limit_bytes=VMEM_LIMIT, has_side_effects=bool(n_x)),
    )(a, b, *extras, *consts, *xch)
    return res[0] if len(res) == 1 else res


def _rms(h, g):
    return h * lax.rsqrt(jnp.mean(h * h, axis=-1, keepdims=True) + RMS_EPS) * g


def _rms_bwd(h, g, dy):
    rs = lax.rsqrt(jnp.mean(h * h, axis=-1, keepdims=True) + RMS_EPS)
    n = h * rs
    dn = dy * g
    dh = rs * (dn - n * jnp.mean(dn * n, axis=-1, keepdims=True))
    return dh, jnp.sum(dy * n, axis=0, keepdims=True)


def _rwkv_pre(k, xw, xa, xg, w0, a0, k_k, k_a, wl, al, gl):
    zw = w0 + _mm(jnp.tanh(xw), wl, "nn", 1)
    lw = -jnp.exp(-_softplus(-zw) - 0.5)
    iclr = _sigmoid(a0 + _mm(xa, al, "nn", 1))
    g = _mm(_sigmoid(xg), gl, "nn", 1)
    kk0 = k * k_k
    kk = kk0 * lax.rsqrt(jnp.maximum(_hsum(kk0 * kk0), L2_EPS * L2_EPS))
    k_h = k * (1.0 + (iclr - 1.0) * k_a)
    return lw, k_h, -kk, kk * iclr, g


def _rwkv_post(y, r, k_h, v, g, ln_g, ln_b, r_k):
    mu = _hsum(y) * (1.0 / HEAD_DIM)
    yc = y - mu
    var = _hsum(yc * yc) * (1.0 / HEAD_DIM)
    yo = yc * lax.rsqrt(var + GN_EPS) * ln_g + ln_b
    bonus = _hsum(r * k_h * r_k) * v
    return (yo + bonus) * g


def _shift_down(x, n):
    rows = lax.broadcasted_iota(jnp.int32, x.shape, 0)
    return jnp.where(rows < n, 0.0, pltpu.roll(x, n, 0))


def _shift_up(x, n):
    t_len = x.shape[0]
    rows = lax.broadcasted_iota(jnp.int32, x.shape, 0)
    return jnp.where(rows >= t_len - n, 0.0, pltpu.roll(x, t_len - n, 0))


def _exchange_plan(ins, outs, scatter, send_sems, recv_sems, local_sems):
    x, y, c = lax.axis_index("x"), lax.axis_index("y"), lax.axis_index("c")
    me = 4 * x + 2 * y + c

    def local(i):
        return pltpu.make_async_copy(ins[i].at[me] if scatter[i] else ins[i], outs[i].at[me], local_sems.at[i])

    def send(i, rel):
        return pltpu.make_async_remote_copy(
            src_ref=ins[i].at[me ^ rel] if scatter[i] else ins[i], dst_ref=outs[i].at[me],
            send_sem=send_sems.at[i, rel - 1], recv_sem=recv_sems.at[i, rel - 1],
            device_id=(x ^ (rel >> 2), y ^ ((rel >> 1) & 1), c ^ (rel & 1)), device_id_type=pl.DeviceIdType.MESH)

    def landed(i, rel):
        slot = outs[i].at[me ^ rel]
        return pltpu.make_async_remote_copy(
            src_ref=slot, dst_ref=slot, send_sem=send_sems.at[i, rel - 1], recv_sem=recv_sems.at[i, rel - 1],
            device_id=(x, y, c), device_id_type=pl.DeviceIdType.MESH)

    def start():
        for i in range(len(ins)):
            local(i).start()
            for rel in range(1, N_DEV):
                send(i, rel).start()

    def wait():
        for i in range(len(ins)):
            local(i).wait()
            for rel in range(1, N_DEV):
                landed(i, rel).wait_recv()
            for rel in range(1, N_DEV):
                send(i, rel).wait_send()

    return start, wait


def _gather_plan(ins, outs, send_sems, recv_sems, local_sems):
    x, y, c = lax.axis_index("x"), lax.axis_index("y"), lax.axis_index("c")
    me = 4 * x + 2 * y + c
    direct, chips = (1, 2, 4, 6), (2, 4, 6)

    def local(i):
        return pltpu.make_async_copy(ins[i], outs[i].at[me], local_sems.at[i])

    def send(i, rel):
        return pltpu.make_async_remote_copy(
            src_ref=ins[i], dst_ref=outs[i].at[me], send_sem=send_sems.at[i, rel - 1], recv_sem=recv_sems.at[i, rel - 1],
            device_id=(x ^ (rel >> 2), y ^ ((rel >> 1) & 1), c ^ (rel & 1)), device_id_type=pl.DeviceIdType.MESH)

    def passed(i, rel):
        slot = outs[i].at[me ^ rel]
        return pltpu.make_async_remote_copy(
            src_ref=slot, dst_ref=slot, send_sem=send_sems.at[i, rel], recv_sem=recv_sems.at[i, rel],
            device_id=(x, y, 1 - c), device_id_type=pl.DeviceIdType.MESH)

    def landed(i, rel):
        slot = outs[i].at[me ^ rel]
        return pltpu.make_async_remote_copy(
            src_ref=slot, dst_ref=slot, send_sem=send_sems.at[i, rel - 1], recv_sem=recv_sems.at[i, rel - 1],
            device_id=(x, y, c), device_id_type=pl.DeviceIdType.MESH)

    def start():
        for i in range(len(ins)):
            local(i).start()
            for rel in direct:
                send(i, rel).start()

    def forward(i):
        for rel in chips:
            landed(i, rel).wait_recv()
            passed(i, rel).start()

    def wait():
        for i in range(len(ins)):
            local(i).wait()
            for rel in (1, 3, 5, 7):
                landed(i, rel).wait_recv()
            for rel in direct:
                send(i, rel).wait_send()
            for rel in chips:
                passed(i, rel).wait_send()

    return start, forward, wait


def _exchange_io(arrays, scatter):
    n = len(arrays)
    any_spec = pl.BlockSpec(memory_space=pl.ANY)
    out_shape = [jax.ShapeDtypeStruct(a.shape if sc else (N_DEV,) + a.shape, a.dtype) for a, sc in zip(arrays, scatter)]
    sems = [pltpu.SemaphoreType.DMA((n, N_DEV - 1)), pltpu.SemaphoreType.DMA((n, N_DEV - 1)), pltpu.SemaphoreType.DMA((n,))]
    return [any_spec] * n, out_shape, sems


def _exchange(name, arrays, scatter):
    n = len(arrays)
    specs, out_shape, sems = _exchange_io(arrays, scatter)

    def body(*refs):
        if any(scatter):
            start, wait = _exchange_plan(refs[:n], refs[n:2 * n], scatter, *refs[2 * n:])
            start()
        else:
            start, forward, wait = _gather_plan(refs[:n], refs[n:2 * n], *refs[2 * n:])
            start()
            for i in range(n):
                forward(i)
        wait()

    return pl.pallas_call(
        body, name=name, in_specs=specs, out_specs=specs, out_shape=out_shape, scratch_shapes=sems,
        compiler_params=pltpu.CompilerParams(has_side_effects=True),
    )(*arrays)


def _scatter_start(name, arrays, lands):
    n = len(arrays)
    hbm = pl.BlockSpec(memory_space=pltpu.HBM)

    def body(*refs):
        ins, land, send_sems, recv_sems = refs[:n], refs[n:2 * n], refs[2 * n], refs[2 * n + 1]
        token = refs[4 * n + 2]
        x, y, c = lax.axis_index("x"), lax.axis_index("y"), lax.axis_index("c")
        me = 4 * x + 2 * y + c
        for i in range(n):
            for rel in range(1, N_DEV):
                k = i * (N_DEV - 1) + rel - 1
                pltpu.make_async_remote_copy(
                    src_ref=ins[i].at[me ^ rel], dst_ref=land[i].at[me], send_sem=send_sems.at[k],
                    recv_sem=recv_sems.at[k], device_id=(x ^ (rel >> 2), y ^ ((rel >> 1) & 1), c ^ (rel & 1)),
                    device_id_type=pl.DeviceIdType.MESH).start()
        token[...] = jnp.zeros_like(token)

    sem = pltpu.SemaphoreType.DMA((n * (N_DEV - 1),))
    bufs = [pltpu.HBM(a.shape, a.dtype) for a in list(arrays) + list(lands)]
    res = pl.pallas_call(
        body, name=name, out_shape=(sem, sem, *bufs, jax.ShapeDtypeStruct((8, LANE), F32)),
        in_specs=[hbm] * (2 * n),
        out_specs=(pl.BlockSpec(memory_space=pltpu.SEMAPHORE),) * 2 + (hbm,) * (2 * n) + (pl.BlockSpec(memory_space=pltpu.VMEM),),
        input_output_aliases={i: 2 + i for i in range(2 * n)},
        compiler_params=pltpu.CompilerParams(has_side_effects=pltpu.SideEffectType.DATAFLOW_SIDE_EFFECTING),
    )(*[pltpu.with_memory_space_constraint(a, pltpu.HBM) for a in list(arrays) + list(lands)])
    return res[0], res[1], res[2:2 + n], res[2 + n:2 + 2 * n], res[2 + 2 * n]


def _scatter_wait(name, send_sems, recv_sems, arrays, lands, after):
    n, n_after = len(arrays), len(after)
    hbm = pl.BlockSpec(memory_space=pltpu.HBM)

    def body(*refs):
        ins, land, s_sems, r_sems = refs[:n], refs[n:2 * n], refs[2 * n], refs[2 * n + 1]
        x, y, c = lax.axis_index("x"), lax.axis_index("y"), lax.axis_index("c")
        me = 4 * x + 2 * y + c
        for i in range(n):
            for rel in range(1, N_DEV):
                k = i * (N_DEV - 1) + rel - 1
                cp = pltpu.make_async_remote_copy(
                    src_ref=ins[i].at[me ^ rel], dst_ref=land[i].at[me ^ rel], send_sem=s_sems.at[k],
                    recv_sem=r_sems.at[k], device_id=(x, y, c), device_id_type=pl.DeviceIdType.MESH)
                cp.wait_send()
                cp.wait_recv()

    res = pl.pallas_call(
        body, name=name, out_shape=[pltpu.HBM(a.shape, a.dtype) for a in list(arrays) + list(lands)],
        in_specs=[hbm] * (2 * n) + [pl.BlockSpec(memory_space=pltpu.SEMAPHORE)] * 2 + [pl.BlockSpec(memory_space=pl.ANY)] * n_after,
        out_specs=[hbm] * (2 * n), input_output_aliases={i: i for i in range(2 * n)},
        compiler_params=pltpu.CompilerParams(has_side_effects=pltpu.SideEffectType.DATAFLOW_SIDE_EFFECTING),
    )(*arrays, *lands, send_sems, recv_sems, *after)
    return res[:n], res[n:]


def _tri_powers(low):
    powers, n, p = [low.astype(BF16)], 1, low
    while 2 * n < low.shape[-1]:
        p = _mm(p, p, "nn", REC_PASSES)
        powers.append(p.astype(BF16))
        n *= 2
    return powers


@jax.custom_vjp
def _tri_solve(low, rhs, powers):
    del low
    for p in powers:
        rhs = rhs + _mm(p, rhs, "nn", REC_PASSES)
    return rhs


def _tri_solve_fwd(low, rhs, powers):
    out = _tri_solve(low, rhs, powers)
    return out, (powers, out)


def _tri_solve_bwd(res, d):
    powers, u = res
    for p in powers:
        d = d + _mm(p, d, "tn", REC_PASSES)
    return _mm(d, u, "nt", REC_PASSES), d, [jnp.zeros_like(p) for p in powers]


_tri_solve.defvjp(_tri_solve_fwd, _tri_solve_bwd)


def _heads(x):
    return jnp.stack([x[:, h * HEAD_DIM:(h + 1) * HEAD_DIM] for h in range(N_HEADS)])


def _unheads(x):
    return jnp.concatenate([x[h] for h in range(N_HEADS)], axis=-1)


def _causal_masks(c):
    ti = lax.broadcasted_iota(jnp.int32, (c, c), 0)
    si = lax.broadcasted_iota(jnp.int32, (c, c), 1)
    strict, incl = si < ti, si <= ti
    both = jnp.concatenate([jnp.concatenate([strict, strict], axis=1), jnp.concatenate([incl, incl], axis=1)], axis=0)
    return strict, incl, both


@jax.custom_vjp
def _gram_given(x2, y2, value):
    del x2, y2
    return value.astype(F32)


def _gram_given_fwd(x2, y2, value):
    return value.astype(F32), (x2, y2, value)


def _gram_given_bwd(res, d):
    x2, y2, value = res
    d = jnp.where(_causal_masks(d.shape[-1] // 2)[2], d, 0.0)
    return _mm(d, y2, "nn", 2), _mm(d, x2, "tn", 2), jnp.zeros_like(value)


_gram_given.defvjp(_gram_given_fwd, _gram_given_bwd)


def _chunk_fwd(z0, r, lw, k, v, a, b, powers=None, gram_value=None):
    c = r.shape[0]
    n_h, n_k = z0.shape[0], z0.shape[1]
    mm = functools.partial(_mm, passes=REC_PASSES)
    gram = functools.partial(_mm, passes=2)
    _, incl, mask = _causal_masks(c)
    cum = _mm(incl.astype(F32), lw, "nn", 3)
    cum_end = cum[c - 1:c, :]
    e_neg, e_end = jnp.exp(-cum), jnp.exp(cum_end - cum)
    x2 = jnp.concatenate([_heads(a * jnp.exp(cum - lw)), _heads(r * jnp.exp(cum))], axis=1)
    y2 = jnp.concatenate([_heads(b * e_neg), _heads(k * e_neg)], axis=1)
    vh = _heads(v)
    g2 = jnp.where(mask, gram(x2, y2, "nt"), 0.0) if gram_value is None else _gram_given(x2, y2, gram_value)
    t2 = mm(x2, z0, "nn") + mm(g2[:, :, c:], vh, "nn")
    low = g2[:, :c, :c]
    powers = _tri_powers(low) if powers is None else powers
    u = _tri_solve(low, t2[:, :c], powers)
    y = t2[:, c:] + mm(g2[:, c:, :c], u, "nn")
    ki = lax.broadcasted_iota(jnp.int32, (n_k, n_k), 0)
    kj = lax.broadcasted_iota(jnp.int32, (n_k, n_k), 1)
    dmat = jnp.where(ki == kj, jnp.broadcast_to(_heads(jnp.exp(cum_end)), (n_h, n_k, n_k)), 0.0)
    z_end = mm(dmat, z0, "nn") + mm(jnp.concatenate([_heads(b * e_end), _heads(k * e_end)], axis=1),
                                    jnp.concatenate([u, vh], axis=1), "tn")
    return _unheads(y), z_end, powers, g2


def _rec_params():
    return pltpu.CompilerParams(dimension_semantics=("arbitrary",), vmem_limit_bytes=VMEM_LIMIT, has_side_effects=True)


def _rec_fwd(u, lw, k, a, b, xch):
    t_len = lw.shape[0]
    c = min(REC_CHUNK, t_len)
    nc = t_len // c
    n_x = len(xch)
    x_specs, x_shapes, x_sems = _exchange_io(xch, [False] * n_x)
    n_pow = max(1, (c - 1).bit_length())
    sizes = [a_.size * a_.dtype.itemsize for a_ in xch]
    pass_step = [min(nc - 1, int(0.9 * nc * sum(sizes[:j + 1]) / sum(sizes)) + 1) for j in range(n_x)]

    def body(*refs):
        r_ref, v_ref, lw_ref, k_ref, a_ref, b_ref = refs[:6]
        x_in = refs[6:6 + n_x]
        y_ref, zs_ref, pw_ref, gs_ref = refs[6 + n_x:10 + n_x]
        x_out = refs[10 + n_x:10 + 2 * n_x]
        z_scr = refs[10 + 2 * n_x]
        start, forward, wait = _gather_plan(x_in, x_out, *refs[11 + 2 * n_x:])
        i = pl.program_id(0)

        @pl.when(i == 0)
        def _():
            start()
            z_scr[...] = jnp.zeros_like(z_scr)

        z0 = z_scr[...]
        zs_ref[0] = z0
        y, z_end, powers, g2 = _chunk_fwd(z0, r_ref[...], lw_ref[...], k_ref[...], v_ref[...], a_ref[...], b_ref[...])
        y_ref[...] = y
        z_scr[...] = z_end
        pw_ref[0] = jnp.concatenate(powers, axis=0)
        gs_ref[0] = g2.astype(BF16)

        for j in range(n_x):
            pl.when(i == pass_step[j])(functools.partial(forward, j))

        @pl.when(i == nc - 1)
        def _():
            wait()

    blk = lambda cb: pl.BlockSpec((c, RWKV_DIM), functools.partial(lambda i, q: (i, q), q=cb))
    res = pl.pallas_call(
        body, name="rwkv_rec_fwd", grid=(nc,),
        in_specs=[blk(0), blk(2)] + [blk(0)] * 4 + x_specs,
        out_specs=[blk(0), pl.BlockSpec((1, N_HEADS, HEAD_DIM, HEAD_DIM), lambda i: (i, 0, 0, 0)),
                   pl.BlockSpec((1, n_pow * N_HEADS, c, c), lambda i: (i, 0, 0, 0)),
                   pl.BlockSpec((1, N_HEADS, 2 * c, 2 * c), lambda i: (i, 0, 0, 0))] + x_specs,
        out_shape=[jax.ShapeDtypeStruct((t_len, RWKV_DIM), F32),
                   jax.ShapeDtypeStruct((nc, N_HEADS, HEAD_DIM, HEAD_DIM), F32),
                   jax.ShapeDtypeStruct((nc, n_pow * N_HEADS, c, c), BF16),
                   jax.ShapeDtypeStruct((nc, N_HEADS, 2 * c, 2 * c), BF16)] + x_shapes,
        scratch_shapes=[pltpu.VMEM((N_HEADS, HEAD_DIM, HEAD_DIM), F32)] + x_sems,
        compiler_params=_rec_params(),
    )(u, u, lw, k, a, b, *xch)
    return res[0], res[1:4], res[4:]


def _rec_bwd(u, lw, k, a, b, saved, dy):
    t_len = lw.shape[0]
    c = min(REC_CHUNK, t_len)
    nc = t_len // c

    zs, pw, gs = saved

    def body(r_ref, v_ref, lw_ref, k_ref, a_ref, b_ref, zs_ref, pw_ref, gs_ref, dy_ref, *rest):
        g_refs, dz_scr = rest[:6], rest[6]
        powers = [pw_ref[0, j * N_HEADS:(j + 1) * N_HEADS] for j in range(pw.shape[1] // N_HEADS)]
        chunk = lambda *xs: _chunk_fwd(*xs, powers=powers, gram_value=gs_ref[0])[:2]

        @pl.when(pl.program_id(0) == 0)
        def _():
            dz_scr[...] = jnp.zeros_like(dz_scr)

        _, vjp = jax.vjp(chunk, zs_ref[0], r_ref[...], lw_ref[...], k_ref[...], v_ref[...], a_ref[...], b_ref[...])
        dz0, dr, dlw, dk, dv, da, db = vjp((dy_ref[...], dz_scr[...]))
        for ref, val in zip(g_refs, (dr, dv, dlw, dk, da, db)):
            ref[...] = val
        dz_scr[...] = dz0

    blk = lambda cb: pl.BlockSpec((c, RWKV_DIM), functools.partial(lambda i, q: (nc - 1 - i, q), q=cb))
    return pl.pallas_call(
        body, name="rwkv_rec_bwd", grid=(nc,),
        in_specs=[blk(0), blk(2)] + [blk(0)] * 4
                 + [pl.BlockSpec((1, N_HEADS, HEAD_DIM, HEAD_DIM), lambda i: (nc - 1 - i, 0, 0, 0)),
                    pl.BlockSpec((1,) + pw.shape[1:], lambda i: (nc - 1 - i, 0, 0, 0)),
                    pl.BlockSpec((1,) + gs.shape[1:], lambda i: (nc - 1 - i, 0, 0, 0)), blk(0)],
        out_specs=[blk(0)] * 6, out_shape=[jax.ShapeDtypeStruct((t_len, RWKV_DIM), F32)] * 6,
        scratch_shapes=[pltpu.VMEM((N_HEADS, HEAD_DIM, HEAD_DIM), F32)], compiler_params=_params(("arbitrary",)),
    )(u, u, lw, k, a, b, zs, pw, gs, dy)


_EARLY = ["w_in", "conv_w", "w_lora_up", "a_lora_up", "g_lora_up"]
_LATE = ["w_out", "w_up", "w_down", "w_ple_gate", "w_ple_proj"]
_SHARDED = _EARLY + _LATE
_COL_SHARDED = {"w_in", "conv_w", "w_lora_up", "a_lora_up", "g_lora_up", "w_up", "w_ple_proj"}
_BF16_GATHER = {"w_in", "w_out", "w_up", "w_down", "w_ple_gate", "w_ple_proj"}
_REPLICATED = ["norm_mix_g", "shift_mu", "w0", "a0", "k_k", "k_a", "r_k", "ln_x_g", "ln_x_b", "norm_mlp_g", "norm_ple_g",
               "norm_final_g"]
_WEIGHTS = ["norm_mix_g", "w_in", "conv_w", "shift_mu", "w_lora_up", "w0", "a_lora_up", "a0", "g_lora_up", "k_k", "k_a", "r_k",
            "ln_x_g", "ln_x_b", "w_out", "norm_mlp_g", "w_up", "w_down", "norm_ple_g", "w_ple_gate", "w_ple_proj", "norm_final_g"]


def _unshard(name, g):
    if name in _COL_SHARDED:
        return jnp.moveaxis(g, 0, 1).reshape(g.shape[1], N_DEV * g.shape[2])
    return g.reshape(N_DEV * g.shape[1], g.shape[2])


def _reshard(name, full):
    if name in _COL_SHARDED:
        return jnp.moveaxis(full.reshape(full.shape[0], N_DEV, full.shape[1] // N_DEV), 1, 0)
    return full.reshape(N_DEV, full.shape[0] // N_DEV, full.shape[1])


def _pad_in_cols(a):
    z = lambda n: jnp.zeros(a.shape[:-1] + (n,), a.dtype)
    conv = [a[..., part * CONV_DIM + j * LANE:part * CONV_DIM + (j + 1) * LANE] for j in range(CONV_DIM // LANE) for part in range(3)]
    return jnp.concatenate(conv + [a[..., CONV_COLS:3136], z(64), a[..., 3136:3200], z(64), a[..., 3200:3360], z(96)], axis=-1)


def _unpad_in_cols(a):
    conv = [a[..., (3 * j + part) * LANE:(3 * j + part + 1) * LANE] for part in range(3) for j in range(CONV_DIM // LANE)]
    return jnp.concatenate(conv + [a[..., CONV_COLS:3136], a[..., 3200:3264], a[..., 3328:3488]], axis=-1)


def _assemble_w_in(g):
    n_dev, rows, cols = g.shape

    def body(g_ref, o_ref):
        o_ref[...] = _pad_in_cols(jnp.concatenate([g_ref[d] for d in range(n_dev)], axis=1))

    return pl.pallas_call(
        body, name="w_in_assemble", grid=(rows // ROW_BLOCK,),
        in_specs=[pl.BlockSpec((n_dev, ROW_BLOCK, cols), lambda i: (0, i, 0))],
        out_specs=pl.BlockSpec((ROW_BLOCK, IN_PAD), lambda i: (i, 0)),
        out_shape=jax.ShapeDtypeStruct((rows, IN_PAD), g.dtype), compiler_params=_params(("arbitrary",)),
    )(g)


def _split_w_in_grad(dw):
    rows = dw.shape[0]
    cols = IN_COLS // N_DEV

    def body(d_ref, o_ref):
        full = _unpad_in_cols(d_ref[...])
        for d in range(N_DEV):
            o_ref[d] = full[:, cols * d:cols * (d + 1)]

    return pl.pallas_call(
        body, name="w_in_grad_split", grid=(rows // ROW_BLOCK,),
        in_specs=[pl.BlockSpec((ROW_BLOCK, IN_PAD), lambda i: (i, 0))],
        out_specs=pl.BlockSpec((N_DEV, ROW_BLOCK, cols), lambda i: (0, i, 0)),
        out_shape=jax.ShapeDtypeStruct((N_DEV, rows, cols), dw.dtype), compiler_params=_params(("arbitrary",)),
    )(dw)


def _pad_rows(a, rows):
    return jnp.concatenate([a, jnp.zeros((rows - a.shape[0],) + a.shape[1:], a.dtype)], axis=0)


SEG_W = [RWKV_DIM, RWKV_DIM, RWKV_DIM, LANE, LANE, 2 * LANE]
SEG_OFF = [0, 512, 1024, XW_OFF, XA_OFF, XG_OFF]


def _rwkv_pre_bwd(proj, u, grads, mu, small, dproj):
    t_len = u.shape[0]
    tr = min(ROW_BLOCK, t_len)
    nb = t_len // tr
    sub = 8
    n_g = len(grads)
    acc_shapes = [(1, RW_PAD)] + [(1, RWKV_DIM)] * 4 + [(LANE, RWKV_DIM), (LANE, RWKV_DIM), (2 * LANE, RWKV_DIM)]

    def body(*refs):
        seg_refs, halo_refs = refs[:6], refs[6:12]
        k_ref, xw_ref, xa_ref, xg_ref = refs[12:16]
        g_refs = refs[16:16 + n_g]
        mu_ref = refs[16 + n_g]
        prm_refs = refs[17 + n_g:24 + n_g]
        out_hbm = refs[25 + n_g]
        acc_refs = refs[26 + n_g:26 + n_g + len(acc_shapes)]
        vbuf, sems, carry = refs[26 + n_g + len(acc_shapes):]
        i = pl.program_id(0)
        blk = nb - 1 - i
        dr1, dr2, dv1, dv2, dlw, dk1, dk2, da, db, dg = [g[...] for g in g_refs]
        _, vjp = jax.vjp(_rwkv_pre, k_ref[...], xw_ref[...], xa_ref[...], xg_ref[...], *[p_[...] for p_ in prm_refs])
        dk, dxw, dxa, dxg, *dprm = vjp((dlw, dk1 + dk2, da, db, dg))
        du = jnp.concatenate([dr1 + dr2, dk, dv1 + dv2, dxw, dxa, dxg], axis=1)
        mu_v = mu_ref[...]

        @pl.when(i == 0)
        def _():
            carry[...] = jnp.zeros_like(carry)

        rows = lax.broadcasted_iota(jnp.int32, du.shape, 0)
        nxt = jnp.where(rows == tr - 1, carry[...], pltpu.roll(du, tr - 1, 0))
        d_rw = du - mu_v * du + mu_v * nxt
        d_mu = []
        for s_ref, h_ref, off, wd in zip(seg_refs, halo_refs, SEG_OFF, SEG_W):
            cur = s_ref[...]
            r0 = lax.broadcasted_iota(jnp.int32, cur.shape, 0)
            prev = jnp.where(r0 == 0, jnp.where(blk == 0, 0.0, h_ref[sub - 1:sub, :]), pltpu.roll(cur, 1, 0))
            d_mu.append(jnp.sum(du[:, off:off + wd] * (prev - cur), axis=0, keepdims=True))
        sums = [jnp.concatenate(d_mu, axis=1)] + list(dprm)

        @pl.when(i == 0)
        def _():
            for a_ref, val in zip(acc_refs, sums):
                a_ref[...] = val

        @pl.when(i > 0)
        def _():
            for a_ref, val in zip(acc_refs, sums):
                a_ref[...] += val

        carry[...] = du[0:1, :]
        slot = i % 2

        def writeback(s, b):
            return pltpu.make_async_copy(vbuf.at[s], out_hbm.at[pl.ds(b * tr, tr), pl.ds(CONV_COLS, RW_PAD)], sems.at[s])

        @pl.when(i >= 2)
        def _():
            writeback(slot, blk + 2).wait()

        vbuf[slot] = d_rw.astype(vbuf.dtype)
        writeback(slot, blk).start()

        @pl.when(i == nb - 1)
        def _():
            writeback(slot, blk).wait()
            if nb > 1:
                writeback(1 - slot, blk + 1).wait()

    rev = lambda w_, cb: pl.BlockSpec((tr, w_), functools.partial(lambda i, c: (nb - 1 - i, c), c=cb))
    halo = lambda w_, cb: pl.BlockSpec((sub, w_), functools.partial(
        lambda i, c: (jnp.maximum((nb - 1 - i) * (tr // sub) - 1, 0), c), c=cb))
    whole = lambda a: pl.BlockSpec(a.shape, functools.partial(lambda i, n: (0,) * n, n=a.ndim))
    segs = [(wd, (CONV_COLS + off) // wd) for off, wd in zip(SEG_OFF, SEG_W)]
    u_cols = [(512, 1), (LANE, XW_OFF // LANE), (LANE, XA_OFF // LANE), (2 * LANE, XG_OFF // (2 * LANE))]
    any_spec = pl.BlockSpec(memory_space=pl.ANY)
    res = pl.pallas_call(
        body, name="rwkv_pre_bwd", grid=(nb,),
        in_specs=[rev(*s) for s in segs] + [halo(*s) for s in segs] + [rev(*c) for c in u_cols]
                 + [rev(RWKV_DIM, 0)] * n_g + [whole(mu)] + [whole(p_) for p_ in small] + [any_spec],
        out_specs=[any_spec] + [pl.BlockSpec(s, functools.partial(lambda i, n: (0,) * n, n=len(s))) for s in acc_shapes],
        out_shape=[jax.ShapeDtypeStruct(dproj.shape, dproj.dtype)] + [jax.ShapeDtypeStruct(s, F32) for s in acc_shapes],
        scratch_shapes=[pltpu.VMEM((2, tr, RW_PAD), dproj.dtype), pltpu.SemaphoreType.DMA((2,)), pltpu.VMEM((1, RW_PAD), F32)],
        input_output_aliases={24 + n_g: 0},
        compiler_params=_params(("arbitrary",)),
    )(*[proj] * 12, *[u] * 4, *grads, mu, *small, dproj)
    return res


def _local_step(x, p, tgt, w, early_shards, late_shards):
    row = lambda v: v.reshape(1, -1)
    w = dict(w)

    xn1, *gathered = _rowwise("rms_mix", lambda h, g: (_rms(h, g),), [x], [w["norm_mix_g"]], [(D_MODEL, BF16)],
                              gather=early_shards)
    w.update({n: _unshard(n, g_) for n, g_ in zip(_EARLY[1:], gathered[1:])})
    w["w_in"] = _assemble_w_in(gathered[0])
    w["w_lora_up"] = _pad_rows(w["w_lora_up"], LANE)
    w["a_lora_up"] = _pad_rows(w["a_lora_up"], LANE)
    w["g_lora_up"] = _pad_rows(w["g_lora_up"], 2 * LANE)
    proj = _matmul("in_proj", xn1, w["w_in"], "nn", [F32], tm=2048, tn=512, tk=D_MODEL)
    n_cb = CONV_DIM // LANE

    def conv_fwd(blk, cw):
        gb, gc, hx = blk[:, :LANE], blk[:, LANE:2 * LANE], blk[:, 2 * LANE:]
        uu = gc * hx
        return (gb * (uu * cw[2:3] + _shift_down(uu, 1) * cw[1:2] + _shift_down(uu, 2) * cw[0:1]),)

    (y_conv,) = _colwise("conv_fwd", conv_fwd, n_cb, [(proj, 3 * LANE)], [w["conv_w"]], [(CONV_DIM, BF16, LANE)])

    small = [w["w0"], w["a0"], w["k_k"], w["k_a"], w["w_lora_up"], w["a_lora_up"], w["g_lora_up"]]
    def pre_fwd(*xs):
        cur, prev_rows, mu, prm = xs[:6], xs[6:12], xs[12], xs[13:]
        segs = []
        for c_, p_, off, wd in zip(cur, prev_rows, SEG_OFF, SEG_W):
            rows = lax.broadcasted_iota(jnp.int32, c_.shape, 0)
            prev = jnp.where(rows == 0, p_, pltpu.roll(c_, 1, 0))
            segs.append(c_ + mu[:, off:off + wd] * (prev - c_))
        return (jnp.concatenate(segs, axis=1),) + tuple(_rwkv_pre(segs[1], segs[3], segs[4], segs[5], *prm))

    proj_segs = [(proj, wd, (CONV_COLS + off) // wd) for off, wd in zip(SEG_OFF, SEG_W)]
    big = [n for n in _LATE if n in ("w_up", "w_down")]
    rest = [n for n in _LATE if n not in big]
    u, lw, k_h, ra, rb, g, *got_rest = _rowwise(
        "rwkv_pre", pre_fwd, proj_segs, [w["shift_mu"]] + small, [(RW_PAD, F32)] + [(RWKV_DIM, F32)] * 5, halo=True,
        gather=[late_shards[_LATE.index(n)] for n in rest])
    y_rec, rec_saved, got_big = _rec_fwd(u, lw, k_h, ra, rb, [late_shards[_LATE.index(n)] for n in big])
    for n, gathered in list(zip(rest, got_rest)) + list(zip(big, got_big)):
        w[n] = _unshard(n, gathered)
    post_c = [w["ln_x_g"], w["ln_x_b"], w["r_k"]]
    u_r, u_v = (u, 512, 0), (u, 512, 2)
    (y_rwkv,) = _rowwise("rwkv_post", lambda *xs: (_rwkv_post(*xs),), [y_rec, u_r, k_h, u_v, g], post_c, [(RWKV_DIM, BF16)],
                         tr=2 * ROW_BLOCK)
    ycat = jnp.concatenate([y_conv, y_rwkv], axis=1)
    def res_norm(acc, r_, g_):
        h = acc + r_
        return h, _rms(h, g_)

    h1, xn2 = _matmul("out_proj", ycat, w["w_out"], "nn", [F32, BF16], tm=1024, tn=D_MODEL, tk=D_MODEL, extras=[x],
                      consts=[w["norm_mlp_g"]], epilogue=res_norm)

    square = lambda h: h.astype(F32) * h.astype(F32)
    hid = _matmul("mlp_up", xn2, w["w_up"], "nn", [BF16], tm=2048, tn=1024, tk=D_MODEL,
                  epilogue=lambda acc: (jnp.maximum(acc, 0.0),))
    h2, xn3 = _matmul("mlp_down", hid, w["w_down"], "nn", [F32, BF16], tm=512, tn=D_MODEL, tk=D_FF, extras=[h1],
                      consts=[w["norm_ple_g"]], epilogue=res_norm, a_map=square)
    zg = _matmul("ple_gate", xn3, w["w_ple_gate"], "nn", [F32], tm=1024, tn=1024, tk=D_MODEL)
    pp = _matmul("ple_proj", p, w["w_ple_proj"], "nn", [F32], tm=1024, tn=1024, tk=PLE_DIM)

    def head(h2_, zg_, pp_, tg, gf):
        gate = _sigmoid(zg_)
        h3 = h2_ + gate * pp_
        out = _rms(h3, gf)
        err = out - tg
        dh3, dgf = _rms_bwd(h3, gf, err * (1.0 / D_MODEL))
        loss = jnp.sum(jnp.sum(err * err, axis=1, keepdims=True), axis=0, keepdims=True) * (0.5 / D_MODEL)
        return dh3, dh3 * pp_ * gate * (1.0 - gate), dh3 * gate, dgf, loss

    dh3, dzg, dpp, d_norm_final, loss = _rowwise(
        "head", head, [h2, zg, pp, tgt], [row(w["norm_final_g"])], [(D_MODEL, F32), (D_MODEL, BF16), (D_MODEL, BF16)],
        [(1, D_MODEL), (1, 1)], tr=2 * ROW_BLOCK)

    d_w_ple_proj = _matmul("d_ple_proj", p, dpp, "tn", [BF16], tm=PLE_DIM, tn=D_MODEL // N_DEV, tk=4096, col_blocks_out=True)
    d_w_ple_gate = _matmul("d_ple_gate", xn3, dzg, "tn", [BF16], tm=512, tn=1024, tk=4096)

    def norm_bwd(dxn, h, dres, g_):
        dh, dg = _rms_bwd(h, g_, dxn)
        dh = dh + dres
        return dh, dh, dg

    nb = dict(tm=512, tn=D_MODEL, epilogue=norm_bwd, sums=[(1, D_MODEL)])
    dh2, dh2_b, d_norm_ple = _matmul("dx_ple_gate", dzg, w["w_ple_gate"], "nt", [F32, BF16], tk=D_MODEL,
                                     extras=[h2, dh3], consts=[w["norm_ple_g"]], **nb)
    d_w_down = _matmul("d_mlp_down", hid, dh2_b, "tn", [BF16], tm=512, tn=1024, tk=4096, a_map=square)
    dpre = _matmul("dx_mlp_down", dh2_b, w["w_down"], "nt", [BF16], tm=2048, tn=1024, tk=D_MODEL, extras=[hid],
                   epilogue=lambda acc, hid_: (acc * (2.0 * hid_.astype(F32)),))
    d_w_up = _matmul("d_mlp_up", xn2, dpre, "tn", [BF16], tm=1024, tn=D_FF // N_DEV, tk=4096, col_blocks_out=True)
    dh1, dh1_b, d_norm_mlp = _matmul("dx_mlp_up", dpre, w["w_up"], "nt", [F32, BF16], tk=D_FF,
                                     extras=[h1, dh2], consts=[w["norm_mlp_g"]], **nb)
    d_w_out = _matmul("d_out_proj", ycat, dh1_b, "tn", [BF16], tm=512, tn=1024, tk=4096)
    dycat = _matmul("dx_out_proj", dh1_b, w["w_out"], "nt", [F32], tm=1024, tn=1024, tk=D_MODEL)
    late_grads = dict(w_out=d_w_out, w_up=d_w_up, w_down=d_w_down, w_ple_gate=d_w_ple_gate, w_ple_proj=d_w_ple_proj)
    late_send = [late_grads[n] if n in ("w_up", "w_ple_proj") else _reshard(n, late_grads[n]) for n in _LATE]
    *late_flight, late_token = _scatter_start("late_scatter_start", late_send, [lax.empty(a.shape, a.dtype) for a in late_send])
    conv_w_bwd = w["conv_w"] + late_token[0:1, 0:1]

    def conv_bwd(dy, blk, cw):
        gb, gc, hx = blk[:, :LANE], blk[:, LANE:2 * LANE], blk[:, 2 * LANE:]
        uu = gc * hx
        u1, u2 = _shift_down(uu, 1), _shift_down(uu, 2)
        dconv = dy * gb
        du = dconv * cw[2:3] + _shift_up(dconv, 1) * cw[1:2] + _shift_up(dconv, 2) * cw[0:1]
        s = lambda z: jnp.sum(z, axis=0, keepdims=True)
        d_blk = jnp.concatenate([dy * (uu * cw[2:3] + u1 * cw[1:2] + u2 * cw[0:1]), du * hx, du * gc], axis=1)
        return d_blk, s(dconv * u2), s(dconv * u1), s(dconv * uu)

    dproj, dcw0, dcw1, dcw2 = _colwise(
        "conv_bwd", conv_bwd, n_cb, [(dycat, LANE), (proj, 3 * LANE)], [conv_w_bwd],
        [(IN_PAD, BF16, 3 * LANE)], [(1, CONV_DIM)] * 3)

    def post_bwd(dy, y, r, k_h_, v, g_, ln_g, ln_b, r_k):
        _, vjp = jax.vjp(_rwkv_post, y, r, k_h_, v, g_, ln_g, ln_b, r_k)
        return vjp(dy)

    dy_rec, dr_p, dk_p, dv_p, dg, d_ln_g, d_ln_b, d_r_k = _rowwise(
        "rwkv_post_bwd", post_bwd, [(dycat, 512, 1), y_rec, u_r, k_h, u_v, g], post_c,
        [(RWKV_DIM, F32)] * 5, [(1, RWKV_DIM)] * 3)
    dr_r, dv_r, dlw, dk_r, da, db = _rec_bwd(u, lw, k_h, ra, rb, rec_saved, dy_rec)

    dproj, d_mu, d_w0, d_a0, d_k_k, d_k_a, d_wl, d_al, d_gl = _rwkv_pre_bwd(
        proj, u, [dr_p, dr_r, dv_p, dv_r, dlw, dk_p, dk_r, da, db, dg], w["shift_mu"], small, dproj)
    d_w_in = _matmul("d_in_proj", xn1, dproj, "tn", [BF16], tm=1024, tn=896, tk=4096)
    early_grads = dict(conv_w=jnp.concatenate([dcw0, dcw1, dcw2], axis=0),
                       w_lora_up=d_wl[:64], a_lora_up=d_al[:64], g_lora_up=d_gl[:160])
    early_send = [_split_w_in_grad(d_w_in)] + [_reshard(n, early_grads[n]) for n in _EARLY[1:]]
    *early_flight, token = _scatter_start("early_scatter_start", early_send, [lax.empty(a.shape, a.dtype) for a in early_send])
    dx, d_norm_mix = _matmul(
        "dx_in_proj", dproj, w["w_in"], "nt", [F32], tk=IN_PAD, extras=[x, dh1], consts=[w["norm_mix_g"] + token[0:1, 0:1]],
        **dict(nb, epilogue=lambda *a: norm_bwd(*a)[1:]))

    grads = dict(
        norm_mix_g=d_norm_mix, shift_mu=d_mu, w0=d_w0, a0=d_a0, k_k=d_k_k, k_a=d_k_a, r_k=d_r_k,
        ln_x_g=d_ln_g, ln_x_b=d_ln_b, norm_mlp_g=d_norm_mlp, norm_ple_g=d_norm_ple, norm_final_g=d_norm_final)
    return loss, dx, grads, late_flight, early_flight, d_w_in


def _adam_update(partials, w_ref, m_ref, v_ref, g_ref, d_ref, nm_ref, nv_ref):
    g = partials[0].astype(F32)
    for part in partials[1:]:
        g = g + part.astype(F32)
    nm =ADAM_B1 * m_ref[...] + (1.0 - ADAM_B1) * g
    nv = ADAM_B2 * v_ref[...] + (1.0 - ADAM_B2) * (g * g)
    m_hat = nm / (1.0 - ADAM_B1 ** ADAM_STEP)
    v_hat = nv / (1.0 - ADAM_B2 ** ADAM_STEP)
    g_ref[...] = g
    d_ref[...] = -ADAM_LR * (m_hat / (jnp.sqrt(v_hat) + ADAM_EPS) + ADAM_WD * w_ref[...])
    nm_ref[...] = nm
    nv_ref[...] = nv


SMALL_ROWS = 8


def _small_layout(widths):
    widths = list(widths) + [1]
    fill, place = [0] * SMALL_ROWS, [None] * len(widths)
    for j in sorted(range(len(widths)), key=lambda q: -widths[q]):
        row = fill.index(min(fill))
        place[j] = (row, fill[row])
        fill[row] += -(-widths[j] // LANE) * LANE
    return place, max(fill)


def _pack_small(vecs, loss):
    place, total = _small_layout([v_.shape[1] for v_ in vecs])
    n = len(vecs)

    def body(*refs):
        out = jnp.zeros((SMALL_ROWS, total), F32)
        row_id = lax.broadcasted_iota(jnp.int32, (SMALL_ROWS, total), 0)
        for row in range(SMALL_ROWS):
            mine = sorted((off, j) for j, (r_, off) in enumerate(place) if r_ == row)
            pieces, at = [], 0
            for off, j in mine:
                val = refs[j][...]
                pieces.append(val)
                at = off + val.shape[1]
                pad = -val.shape[1] % LANE
                if pad:
                    pieces.append(jnp.zeros((1, pad), F32))
                    at += pad
            if total > at:
                pieces.append(jnp.zeros((1, total - at), F32))
            out = jnp.where(row_id == row, jnp.broadcast_to(jnp.concatenate(pieces, axis=1), (SMALL_ROWS, total)), out)
        refs[n + 1][...] = out

    return pl.pallas_call(body, name="pack_small", out_shape=jax.ShapeDtypeStruct((SMALL_ROWS, total), F32))(*vecs, loss)


def _adamw_small(packed, ws, ms, vs):
    n = len(ws)
    place, _ = _small_layout([w_.shape[1] for w_ in ws])

    def body(p_ref, *refs):
        w_refs, m_refs, v_refs, outs = refs[:n], refs[n:2 * n], refs[2 * n:3 * n], refs[3 * n:]
        for j in range(n):
            row, off = place[j]
            cols = pl.ds(off, ws[j].shape[1])
            _adam_update([p_ref[s, row:row + 1, cols] for s in range(N_DEV)], w_refs[j], m_refs[j], v_refs[j],
                         *outs[4 * j:4 * j + 4])
        row, off = place[n]
        total = p_ref[0, row:row + 1, off:off + 1]
        for s in range(1, N_DEV):
            total = total + p_ref[s, row:row + 1, off:off + 1]
        outs[4 * n][...] = total

    res = pl.pallas_call(
        body, name="adamw_small",
        out_shape=[jax.ShapeDtypeStruct(w_.shape, F32) for w_ in ws for _ in range(4)] + [jax.ShapeDtypeStruct((1, 1), F32)],
    )(packed, *ws, *ms, *vs)
    return [res[4 * j:4 * j + 4] for j in range(n)], res[4 * n]


def _adamw(name, parts, w, m, v, own=None, me=None):
    rows, cols = w.shape[-2:]
    lead = w.ndim - 2
    tr = rows if rows * cols * 4 * 8 <= (4 << 20) else max(8, (4 << 20) // (cols * 4 * 8) // 8 * 8)
    while rows % tr:
        tr -= 8
    shape4 = [jax.ShapeDtypeStruct(w.shape, F32)] * 4
    if own is None:
        def body(p_ref, *refs):
            _adam_update([p_ref[s] for s in range(N_DEV)], *refs)

        blk = pl.BlockSpec((None,) * lead + (tr, cols), lambda i: (0,) * lead + (i, 0))
        return pl.pallas_call(
            body, name=name, grid=(rows // tr,),
            in_specs=[pl.BlockSpec((N_DEV, tr, cols), lambda i: (0, i, 0)), blk, blk, blk], out_specs=[blk] * 4,
            out_shape=shape4, compiler_params=_params(("arbitrary",)),
        )(parts, w, m, v)

    def body_own(me_ref, p_ref, own_ref, *refs):
        mine = own_ref[...]
        _adam_update([jnp.where(me_ref[0] == s, mine, p_ref[s]) for s in range(N_DEV)], *refs)

    blk = pl.BlockSpec((None,) * lead + (tr, cols), lambda i, me_ref: (0,) * lead + (i, 0))
    return pl.pallas_call(
        body_own, name=name, out_shape=shape4,
        grid_spec=pltpu.PrefetchScalarGridSpec(
            num_scalar_prefetch=1, grid=(rows // tr,),
            in_specs=[pl.BlockSpec((N_DEV, tr, cols), lambda i, me_ref: (0, i, 0)),
                      pl.BlockSpec((None, tr, cols), lambda i, me_ref: (me_ref[0], i, 0)), blk, blk, blk],
            out_specs=[blk] * 4),
        compiler_params=_params(("arbitrary",)),
    )(me, parts, own, w, m, v)


def kernel(x, p, norm_mix_g, w_in, conv_w, shift_mu, w_lora_up, w0, a_lora_up, a0, g_lora_up, k_k, k_a, r_k, ln_x_g, ln_x_b, w_out, norm_mlp_g, w_up, w_down, norm_ple_g, w_ple_gate, w_ple_proj, norm_final_g, loss_target, m_norm_mix_g, m_w_in, m_conv_w, m_shift_mu, m_w_lora_up, m_w0, m_a_lora_up, m_a0, m_g_lora_up, m_k_k, m_k_a, m_r_k, m_ln_x_g, m_ln_x_b, m_w_out, m_norm_mlp_g, m_w_up, m_w_down, m_norm_ple_g, m_w_ple_gate, m_w_ple_proj, m_norm_final_g, v_norm_mix_g, v_w_in, v_conv_w, v_shift_mu, v_w_lora_up, v_w0, v_a_lora_up, v_a0, v_g_lora_up, v_k_k, v_k_a, v_r_k, v_ln_x_g, v_ln_x_b, v_w_out, v_norm_mlp_g, v_w_up, v_w_down, v_norm_ple_g, v_w_ple_gate, v_w_ple_proj, v_norm_final_g):
    args = dict(locals())
    wts = {n: args[n] for n in _WEIGHTS}
    mom = {n: args["m_" + n] for n in _WEIGHTS}
    var = {n: args["v_" + n] for n in _WEIGHTS}
    shard2d = lambda a: a.reshape(a.shape[-2:])
    pad_mu = lambda a: _pad_in_cols(jnp.concatenate([jnp.zeros((1, CONV_COLS), F32), a], axis=1))[:, CONV_COLS:]
    unpad_mu = lambda a: _unpad_in_cols(jnp.concatenate([jnp.zeros((1, CONV_COLS), F32), a], axis=1))[:, CONV_COLS:]

    shards = {n: shard2d(wts[n]).astype(BF16 if n in _BF16_GATHER else F32) for n in _SHARDED}
    w = {n: wts[n].reshape(1, -1) for n in _REPLICATED}
    w["shift_mu"] = pad_mu(wts["shift_mu"])

    loss, dx, grads, late_flight, early_flight, d_w_in = _local_step(
        x[0], p[0, 0], loss_target[0], w, [shards[n] for n in _EARLY], [shards[n] for n in _LATE])

    me = (4 * lax.axis_index("x") + 2 * lax.axis_index("y") + lax.axis_index("c")).astype(jnp.int32).reshape(1)
    late_sent, late_parts = _scatter_wait("late_scatter_wait", *late_flight, after=[d_w_in])
    out = {n: _adamw("adamw_" + n, prt, wts[n], mom[n], var[n], own=own, me=me)
           for n, prt, own in zip(_LATE, late_parts, late_sent)}
    early_sent, early_parts = _scatter_wait("early_scatter_wait", *early_flight, after=[dx] + [out[n][1] for n in _LATE])
    for n, prt, own in zip(_EARLY, early_parts, early_sent):
        out[n] = _adamw("adamw_" + n, prt, wts[n], mom[n], var[n], own=own, me=me)

    grads["shift_mu"] = unpad_mu(grads["shift_mu"])
    flat = lambda a: a.reshape(1, -1)
    (small_parts,) = _exchange("gather_small", [_pack_small([flat(grads[n]) for n in _REPLICATED], loss)], [False])
    small, loss_total = _adamw_small(small_parts, *[[flat(d[n]) for n in _REPLICATED] for d in (wts, mom, var)])
    for n, res in zip(_REPLICATED, small):
        out[n] = [r.reshape(wts[n].shape) for r in res]
    return (loss_total[0, 0], dx[None], *[out[n][0] for n in _WEIGHTS], *[out[n][1] for n in _WEIGHTS],
            *[out[n][2] for n in _WEIGHTS], *[out[n][3] for n in _WEIGHTS])
```

```python
import functools

import jax
import jax.numpy as jnp
from jax import lax
from jax.experimental import pallas as pl
from jax.experimental.pallas import tpu as pltpu

F32 = jnp.float32
BF16 = jnp.bfloat16

N_DEV = 8
D_MODEL = 1024
CONV_DIM = 512
RWKV_DIM = 512
HEAD_DIM = 64
N_HEADS = 8
D_FF = 4096
PLE_DIM = 256
RMS_EPS = 1e-6
GN_EPS = 64e-5
L2_EPS = 1e-12
ADAM_LR, ADAM_B1, ADAM_B2, ADAM_EPS, ADAM_WD, ADAM_STEP = 0.001, 0.9, 0.999, 1e-08, 0.01, 10

CONV_COLS = 3 * CONV_DIM
RW_PAD = 2048
IN_PAD = CONV_COLS + RW_PAD
IN_COLS = 3360
XW_OFF, XA_OFF, XG_OFF = 1536, 1664, 1792
REC_CHUNK = 128
REC_PASSES = 1
ROW_BLOCK = 256
LANE = 128
VMEM_LIMIT = 56 * 1024 * 1024


def _dims(dn, ndim):
    if ndim == 3:
        return {"nn": (((2,), (1,)), ((0,), (0,))), "nt": (((2,), (2,)), ((0,), (0,))),
                "tn": (((1,), (1,)), ((0,), (0,)))}[dn]
    return {"nn": (((1,), (0,)), ((), ())), "nt": (((1,), (1,)), ((), ())), "tn": (((0,), (0,)), ((), ()))}[dn]


def _split2(x):
    hi = x.astype(BF16)
    return hi, (x - hi.astype(F32)).astype(BF16)


def _mm_raw(x, y, dn, passes):
    f = lambda p, q: lax.dot_general(p, q, _dims(dn, x.ndim), preferred_element_type=F32)
    if passes == 1:
        return f(x.astype(BF16), y.astype(BF16))
    xh, xl = _split2(x)
    yh, yl = _split2(y)
    if passes == 2:
        return f(xh, yh) + f(xh, yl)
    return f(xh, yh) + f(xh, yl) + f(xl, yh)


@functools.partial(jax.custom_vjp, nondiff_argnums=(2, 3))
def _mm(x, y, dn, passes):
    return _mm_raw(x, y, dn, passes)


def _mm_fwd(x, y, dn, passes):
    return _mm_raw(x, y, dn, passes), (x, y)


def _mm_bwd(dn, passes, res, d):
    x, y = res
    if dn == "nn":
        return _mm(d, y, "nt", passes), _mm(x, d, "tn", passes)
    if dn == "nt":
        return _mm(d, y, "nn", passes), _mm(d, x, "tn", passes)
    return _mm(y, d, "nt", passes), _mm(x, d, "nn", passes)


_mm.defvjp(_mm_fwd, _mm_bwd)


def _head_ones():
    i = lax.broadcasted_iota(jnp.int32, (RWKV_DIM, RWKV_DIM), 0) // HEAD_DIM
    j = lax.broadcasted_iota(jnp.int32, (RWKV_DIM, RWKV_DIM), 1) // HEAD_DIM
    return (i == j).astype(BF16)


def _hsum_raw(x):
    ones = _head_ones()
    f = lambda p: lax.dot_general(p, ones, _dims("nn", 2), preferred_element_type=F32)
    x1, x2 = _split2(x)
    return f(x1) + f(x2)


@jax.custom_vjp
def _hsum(x):
    return _hsum_raw(x)


_hsum.defvjp(lambda x: (_hsum_raw(x), None), lambda _, d: (_hsum(d),))


def _sigmoid(x):
    return 0.5 + 0.5 * jnp.tanh(0.5 * x)


def _softplus(x):
    return jnp.maximum(x, 0.0) + jnp.log(1.0 + jnp.exp(-jnp.abs(x)))


def _params(sem):
    return pltpu.CompilerParams(dimension_semantics=sem, vmem_limit_bytes=VMEM_LIMIT)


def _rowwise(name, fn, rows, consts, row_outs, acc_outs=(), tr=ROW_BLOCK, halo=False, gather=()):
    rows = [r if isinstance(r, tuple) else (r, r.shape[1], 0) for r in rows]
    t_len = rows[0][0].shape[0]
    tr = min(tr, t_len)
    n_r, n_c, n_o, n_a, n_x = len(rows), len(consts), len(row_outs), len(acc_outs), len(gather)
    n_h = n_r if halo else 0
    sub = 8
    x_specs, x_shapes, x_sems = _exchange_io(gather, [False] * n_x) if n_x else ([], [], [])
    nb = t_len // tr

    def body(*refs):
        if n_x:
            n_in = n_r + n_h + n_c
            start, forward, wait = _gather_plan(refs[n_in:n_in + n_x], refs[len(refs) - 3 - n_x:len(refs) - 3], *refs[len(refs) - 3:])
            pl.when(pl.program_id(0) == 0)(start)
            refs = refs[:n_in] + refs[n_in + n_x:len(refs) - 3 - n_x]
        ins = [r[...] for r in refs[:n_r]]
        ins += [jnp.where(pl.program_id(0) == 0, 0.0, r[sub - 1:sub, :]) for r in refs[n_r:n_r + n_h]]
        ins += [r[...] for r in refs[n_r + n_h:n_r + n_h + n_c]]
        refs = refs[:n_r] + refs[n_r + n_h:]
        outs = fn(*ins)
        o_refs = refs[n_r + n_c:n_r + n_c + n_o]
        a_refs = refs[n_r + n_c + n_o:]
        for o_ref, val in zip(o_refs, outs[:n_o]):
            o_ref[...] = val.astype(o_ref.dtype)
        if n_a:
            first = pl.program_id(0) == 0

            @pl.when(first)
            def _():
                for a_ref, val in zip(a_refs, outs[n_o:]):
                    a_ref[...] = val

            @pl.when(jnp.logical_not(first))
            def _():
                for a_ref, val in zip(a_refs, outs[n_o:]):
                    a_ref[...] += val

        if n_x:
            @pl.when(pl.program_id(0) == nb - 1)
            def _():
                for j in range(n_x):
                    forward(j)
                wait()

    in_specs = [pl.BlockSpec((tr, w), functools.partial(lambda i, c: (i, c), c=cb)) for _, w, cb in rows]
    if halo:
        in_specs += [pl.BlockSpec((sub, w), functools.partial(lambda i, c: (jnp.maximum(i * (tr // sub) - 1, 0), c), c=cb))
                     for _, w, cb in rows]
    in_specs += [pl.BlockSpec(c.shape, functools.partial(lambda i, n: (0,) * n, n=c.ndim)) for c in consts]
    out_specs = [pl.BlockSpec((tr, w), lambda i: (i, 0)) for w, _ in row_outs]
    out_specs += [pl.BlockSpec(s, functools.partial(lambda i, n: (0,) * n, n=len(s))) for s in acc_outs]
    out_shape = [jax.ShapeDtypeStruct((t_len, w), dt) for w, dt in row_outs]
    out_shape += [jax.ShapeDtypeStruct(s, F32) for s in acc_outs]
    return pl.pallas_call(
        body, name=name, grid=(nb,), in_specs=in_specs + x_specs, out_specs=out_specs + x_specs,
        out_shape=out_shape + x_shapes, scratch_shapes=x_sems,
        compiler_params=pltpu.CompilerParams(dimension_semantics=("arbitrary",), vmem_limit_bytes=VMEM_LIMIT,
                                             has_side_effects=bool(n_x)),
    )(*[r[0] for r in rows], *([r[0] for r in rows] if halo else []), *consts, *gather)


def _colwise(name, fn, n_blocks, cols, prms, col_outs, prm_outs=()):
    t_len = cols[0][0].shape[0]
    n_i = len(cols) + len(prms)

    def body(*refs):
        outs = fn(*[r[...] for r in refs[:n_i]])
        for o_ref, val in zip(refs[n_i:], outs):
            o_ref[...] = val.astype(o_ref.dtype)

    spec = lambda r, w: pl.BlockSpec((r, w), lambda j: (0, j))
    in_specs = [spec(t_len, w) for _, w in cols] + [spec(a.shape[0], LANE) for a in prms]
    out_specs = [spec(t_len, bw) for _, _, bw in col_outs] + [spec(r, LANE) for r, _ in prm_outs]
    out_shape = [jax.ShapeDtypeStruct((t_len, w), dt) for w, dt, _ in col_outs]
    out_shape += [jax.ShapeDtypeStruct((r, w), F32) for r, w in prm_outs]
    return pl.pallas_call(
        body, name=name, grid=(n_blocks,), in_specs=in_specs, out_specs=out_specs, out_shape=out_shape,
        compiler_params=_params(("arbitrary",)),
    )(*[c[0] for c in cols], *prms)


def _matmul(name, a, b, dn, outs, *, tm, tn, tk, extras=(), consts=(), epilogue=None, sums=(), xch=(), xch_scatter=(),
            a_map=None, col_blocks_out=False):
    if dn == "nn":
        (m, k), n = a.shape, b.shape[1]
    elif dn == "nt":
        (m, k), n = a.shape, b.shape[0]
    else:
        (k, m), n = a.shape, b.shape[1]
    tm, tn, tk = min(tm, m), min(tn, n), min(tk, k)
    nk = k // tk
    grid = (m // tm, n // tn, nk)
    assert nk == 1 and (not sums or grid[1] == 1)
    a_spec = pl.BlockSpec((tk, tm), lambda i, j, q: (q, i)) if dn == "tn" else pl.BlockSpec((tm, tk), lambda i, j, q: (i, q))
    b_spec = pl.BlockSpec((tn, tk), lambda i, j, q: (j, q)) if dn == "nt" else pl.BlockSpec((tk, tn), lambda i, j, q: (q, j))
    o_spec = pl.BlockSpec((tm, tn), lambda i, j, q: (i, j))
    c_spec = pl.BlockSpec((1, tn), lambda i, j, q: (0, j))
    n_e, n_c, n_o, n_s, n_x = len(extras), len(consts), len(outs), len(sums), len(xch)
    x_specs, x_shapes, x_sems = _exchange_io(xch, xch_scatter) if n_x else ([], [], [])

    def body(*refs):
        a_ref, b_ref = refs[:2]
        e_refs = refs[2:2 + n_e + n_c]
        x_in = refs[2 + n_e + n_c:2 + n_e + n_c + n_x]
        rest = refs[2 + n_e + n_c + n_x:]
        o_refs, s_refs, x_out, scratch = rest[:n_o], rest[n_o:n_o + n_s], rest[n_o + n_s:n_o + n_s + n_x], rest[n_o + n_s + n_x:]
        step = (pl.program_id(0) * grid[1] + pl.program_id(1)) * nk + pl.program_id(2)
        if n_x:
            start, wait = _exchange_plan(x_in, x_out, xch_scatter, *scratch[len(scratch) - 3:])
            pl.when(step == 0)(start)
        a_blk = a_ref[...] if a_map is None else a_map(a_ref[...])
        acc = lax.dot_general(a_blk.astype(BF16), b_ref[...].astype(BF16), _dims(dn, 2), preferred_element_type=F32)
        vals = (acc,) if epilogue is None else epilogue(acc, *[e[...] for e in e_refs])
        for o_ref, val in zip(o_refs, vals[:n_o]):
            o_ref[...] = val.astype(o_ref.dtype)
        if n_s:
            @pl.when(step == 0)
            def _():
                for s_ref, val in zip(s_refs, vals[n_o:]):
                    s_ref[...] = val

            @pl.when(step > 0)
            def _():
                for s_ref, val in zip(s_refs, vals[n_o:]):
                    s_ref[...] += val

        if n_x:
            pl.when(step == grid[0] * grid[1] * nk - 1)(wait)

    plain = not (n_s or n_x)
    res = pl.pallas_call(
        body, name=name, grid=grid,
        in_specs=[a_spec, b_spec] + [o_spec] * n_e + [c_spec] * n_c + x_specs,
        out_specs=[pl.BlockSpec((None, tm, tn), lambda i, j, q: (j, i, 0)) if col_blocks_out else o_spec] * n_o
                  + [c_spec] * n_s + x_specs,
        out_shape=[jax.ShapeDtypeStruct((n // tn, m, tn) if col_blocks_out else (m, n), dt) for dt in outs] + [jax.ShapeDtypeStruct(s, F32) for s in sums] + x_shapes,
        scratch_shapes=x_sems,
        compiler_params=pltpu.CompilerParams(
            dimension_semantics=("parallel", "parallel", "arbitrary") if plain else ("arbitrary",) * 3,
            vmem_limit_bytes=VMEM_LIMIT, has_side_effects=bool(n_x)),
    )(a, b, *extras, *consts, *xch)
    return res[0] if len(res) == 1 else res


def _rms(h, g):
    return h * lax.rsqrt(jnp.mean(h * h, axis=-1, keepdims=True) + RMS_EPS) * g


def _rms_bwd(h, g, dy):
    rs = lax.rsqrt(jnp.mean(h * h, axis=-1, keepdims=True) + RMS_EPS)
    n = h * rs
    dn = dy * g
    dh = rs * (dn - n * jnp.mean(dn * n, axis=-1, keepdims=True))
    return dh, jnp.sum(dy * n, axis=0, keepdims=True)


def _rwkv_pre(k, xw, xa, xg, w0, a0, k_k, k_a, wl, al, gl):
    zw = w0 + _mm(jnp.tanh(xw), wl, "nn", 1)
    lw = -jnp.exp(-_softplus(-zw) - 0.5)
    iclr = _sigmoid(a0 + _mm(xa, al, "nn", 1))
    g = _mm(_sigmoid(xg), gl, "nn", 1)
    kk0 = k * k_k
    kk = kk0 * lax.rsqrt(jnp.maximum(_hsum(kk0 * kk0), L2_EPS * L2_EPS))
    k_h = k * (1.0 + (iclr - 1.0) * k_a)
    return lw, k_h, -kk, kk * iclr, g


def _rwkv_post(y, r, k_h, v, g, ln_g, ln_b, r_k):
    mu = _hsum(y) * (1.0 / HEAD_DIM)
    yc = y - mu
    var = _hsum(yc * yc) * (1.0 / HEAD_DIM)
    yo = yc * lax.rsqrt(var + GN_EPS) * ln_g + ln_b
    bonus = _hsum(r * k_h * r_k) * v
    return (yo + bonus) * g


def _shift_down(x, n):
    rows = lax.broadcasted_iota(jnp.int32, x.shape, 0)
    return jnp.where(rows < n, 0.0, pltpu.roll(x, n, 0))


def _shift_up(x, n):
    t_len = x.shape[0]
    rows = lax.broadcasted_iota(jnp.int32, x.shape, 0)
    return jnp.where(rows >= t_len - n, 0.0, pltpu.roll(x, t_len - n, 0))


def _exchange_plan(ins, outs, scatter, send_sems, recv_sems, local_sems):
    x, y, c = lax.axis_index("x"), lax.axis_index("y"), lax.axis_index("c")
    me = 4 * x + 2 * y + c

    def local(i):
        return pltpu.make_async_copy(ins[i].at[me] if scatter[i] else ins[i], outs[i].at[me], local_sems.at[i])

    def send(i, rel):
        return pltpu.make_async_remote_copy(
            src_ref=ins[i].at[me ^ rel] if scatter[i] else ins[i], dst_ref=outs[i].at[me],
            send_sem=send_sems.at[i, rel - 1], recv_sem=recv_sems.at[i, rel - 1],
            device_id=(x ^ (rel >> 2), y ^ ((rel >> 1) & 1), c ^ (rel & 1)), device_id_type=pl.DeviceIdType.MESH)

    def landed(i, rel):
        slot = outs[i].at[me ^ rel]
        return pltpu.make_async_remote_copy(
            src_ref=slot, dst_ref=slot, send_sem=send_sems.at[i, rel - 1], recv_sem=recv_sems.at[i, rel - 1],
            device_id=(x, y, c), device_id_type=pl.DeviceIdType.MESH)

    def start():
        for i in range(len(ins)):
            local(i).start()
            for rel in range(1, N_DEV):
                send(i, rel).start()

    def wait():
        for i in range(len(ins)):
            local(i).wait()
            for rel in range(1, N_DEV):
                landed(i, rel).wait_recv()
            for rel in range(1, N_DEV):
                send(i, rel).wait_send()

    return start, wait


def _gather_plan(ins, outs, send_sems, recv_sems, local_sems):
    x, y, c = lax.axis_index("x"), lax.axis_index("y"), lax.axis_index("c")
    me = 4 * x + 2 * y + c
    direct, chips = (1, 2, 4, 6), (2, 4, 6)

    def local(i):
        return pltpu.make_async_copy(ins[i], outs[i].at[me], local_sems.at[i])

    def send(i, rel):
        return pltpu.make_async_remote_copy(
            src_ref=ins[i], dst_ref=outs[i].at[me], send_sem=send_sems.at[i, rel - 1], recv_sem=recv_sems.at[i, rel - 1],
            device_id=(x ^ (rel >> 2), y ^ ((rel >> 1) & 1), c ^ (rel & 1)), device_id_type=pl.DeviceIdType.MESH)

    def passed(i, rel):
        slot = outs[i].at[me ^ rel]
        return pltpu.make_async_remote_copy(
            src_ref=slot, dst_ref=slot, send_sem=send_sems.at[i, rel], recv_sem=recv_sems.at[i, rel],
            device_id=(x, y, 1 - c), device_id_type=pl.DeviceIdType.MESH)

    def landed(i, rel):
        slot = outs[i].at[me ^ rel]
        return pltpu.make_async_remote_copy(
            src_ref=slot, dst_ref=slot, send_sem=send_sems.at[i, rel - 1], recv_sem=recv_sems.at[i, rel - 1],
            device_id=(x, y, c), device_id_type=pl.DeviceIdType.MESH)

    def start():
        for i in range(len(ins)):
            local(i).start()
            for rel in direct:
                send(i, rel).start()

    def forward(i):
        for rel in chips:
            landed(i, rel).wait_recv()
            passed(i, rel).start()

    def wait():
        for i in range(len(ins)):
            local(i).wait()
            for rel in (1, 3, 5, 7):
                landed(i, rel).wait_recv()
            for rel in direct:
                send(i, rel).wait_send()
            for rel in chips:
                passed(i, rel).wait_send()

    return start, forward, wait


def _exchange_io(arrays, scatter):
    n = len(arrays)
    any_spec = pl.BlockSpec(memory_space=pl.ANY)
    out_shape = [jax.ShapeDtypeStruct(a.shape if sc else (N_DEV,) + a.shape, a.dtype) for a, sc in zip(arrays, scatter)]
    sems = [pltpu.SemaphoreType.DMA((n, N_DEV - 1)), pltpu.SemaphoreType.DMA((n, N_DEV - 1)), pltpu.SemaphoreType.DMA((n,))]
    return [any_spec] * n, out_shape, sems


def _exchange(name, arrays, scatter):
    n = len(arrays)
    specs, out_shape, sems = _exchange_io(arrays, scatter)

    def body(*refs):
        if any(scatter):
            start, wait = _exchange_plan(refs[:n], refs[n:2 * n], scatter, *refs[2 * n:])
            start()
        else:
            start, forward, wait = _gather_plan(refs[:n], refs[n:2 * n], *refs[2 * n:])
            start()
            for i in range(n):
                forward(i)
        wait()

    return pl.pallas_call(
        body, name=name, in_specs=specs, out_specs=specs, out_shape=out_shape, scratch_shapes=sems,
        compiler_params=pltpu.CompilerParams(has_side_effects=True),
    )(*arrays)


def _scatter_start(name, arrays, lands):
    n = len(arrays)
    hbm = pl.BlockSpec(memory_space=pltpu.HBM)

    def body(*refs):
        ins, land, send_sems, recv_sems = refs[:n], refs[n:2 * n], refs[2 * n], refs[2 * n + 1]
        token = refs[4 * n + 2]
        x, y, c = lax.axis_index("x"), lax.axis_index("y"), lax.axis_index("c")
        me = 4 * x + 2 * y + c
        for i in range(n):
            for rel in range(1, N_DEV):
                k = i * (N_DEV - 1) + rel - 1
                pltpu.make_async_remote_copy(
                    src_ref=ins[i].at[me ^ rel], dst_ref=land[i].at[me], send_sem=send_sems.at[k],
                    recv_sem=recv_sems.at[k], device_id=(x ^ (rel >> 2), y ^ ((rel >> 1) & 1), c ^ (rel & 1)),
                    device_id_type=pl.DeviceIdType.MESH).start()
        token[...] = jnp.zeros_like(token)

    sem = pltpu.SemaphoreType.DMA((n * (N_DEV - 1),))
    bufs = [pltpu.HBM(a.shape, a.dtype) for a in list(arrays) + list(lands)]
    res = pl.pallas_call(
        body, name=name, out_shape=(sem, sem, *bufs, jax.ShapeDtypeStruct((8, LANE), F32)),
        in_specs=[hbm] * (2 * n),
        out_specs=(pl.BlockSpec(memory_space=pltpu.SEMAPHORE),) * 2 + (hbm,) * (2 * n) + (pl.BlockSpec(memory_space=pltpu.VMEM),),
        input_output_aliases={i: 2 + i for i in range(2 * n)},
        compiler_params=pltpu.CompilerParams(has_side_effects=pltpu.SideEffectType.DATAFLOW_SIDE_EFFECTING),
    )(*[pltpu.with_memory_space_constraint(a, pltpu.HBM) for a in list(arrays) + list(lands)])
    return res[0], res[1], res[2:2 + n], res[2 + n:2 + 2 * n], res[2 + 2 * n]


def _scatter_wait(name, send_sems, recv_sems, arrays, lands, after):
    n, n_after = len(arrays), len(after)
    hbm = pl.BlockSpec(memory_space=pltpu.HBM)

    def body(*refs):
        ins, land, s_sems, r_sems = refs[:n], refs[n:2 * n], refs[2 * n], refs[2 * n + 1]
        x, y, c = lax.axis_index("x"), lax.axis_index("y"), lax.axis_index("c")
        me = 4 * x + 2 * y + c
        for i in range(n):
            for rel in range(1, N_DEV):
                k = i * (N_DEV - 1) + rel - 1
                cp = pltpu.make_async_remote_copy(
                    src_ref=ins[i].at[me ^ rel], dst_ref=land[i].at[me ^ rel], send_sem=s_sems.at[k],
                    recv_sem=r_sems.at[k], device_id=(x, y, c), device_id_type=pl.DeviceIdType.MESH)
                cp.wait_send()
                cp.wait_recv()

    res = pl.pallas_call(
        body, name=name, out_shape=[pltpu.HBM(a.shape, a.dtype) for a in list(arrays) + list(lands)],
        in_specs=[hbm] * (2 * n) + [pl.BlockSpec(memory_space=pltpu.SEMAPHORE)] * 2 + [pl.BlockSpec(memory_space=pl.ANY)] * n_after,
        out_specs=[hbm] * (2 * n), input_output_aliases={i: i for i in range(2 * n)},
        compiler_params=pltpu.CompilerParams(has_side_effects=pltpu.SideEffectType.DATAFLOW_SIDE_EFFECTING),
    )(*arrays, *lands, send_sems, recv_sems, *after)
    return res[:n], res[n:]


def _tri_powers(low):
    powers, n, p = [low.astype(BF16)], 1, low
    while 2 * n < low.shape[-1]:
        p = _mm(p, p, "nn", REC_PASSES)
        powers.append(p.astype(BF16))
        n *= 2
    return powers


@jax.custom_vjp
def _tri_solve(low, rhs, powers):
    del low
    for p in powers:
        rhs = rhs + _mm(p, rhs, "nn", REC_PASSES)
    return rhs


def _tri_solve_fwd(low, rhs, powers):
    out = _tri_solve(low, rhs, powers)
    return out, (powers, out)


def _tri_solve_bwd(res, d):
    powers, u = res
    for p in powers:
        d = d + _mm(p, d, "tn", REC_PASSES)
    return _mm(d, u, "nt", REC_PASSES), d, [jnp.zeros_like(p) for p in powers]


_tri_solve.defvjp(_tri_solve_fwd, _tri_solve_bwd)


@jax.custom_vjp
def _tri_solve_given(low, rhs, powers, value):
    del low, rhs, powers
    return value


def _tri_solve_given_fwd(low, rhs, powers, value):
    return value, (powers, value)


def _tri_solve_given_bwd(res, d):
    return _tri_solve_bwd(res, d) + (jnp.zeros_like(res[1]),)


_tri_solve_given.defvjp(_tri_solve_given_fwd, _tri_solve_given_bwd)


def _heads(x):
    return jnp.stack([x[:, h * HEAD_DIM:(h + 1) * HEAD_DIM] for h in range(N_HEADS)])


def _unheads(x):
    return jnp.concatenate([x[h] for h in range(N_HEADS)], axis=-1)


def _causal_masks(c):
    ti = lax.broadcasted_iota(jnp.int32, (c, c), 0)
    si = lax.broadcasted_iota(jnp.int32, (c, c), 1)
    strict, incl = si < ti, si <= ti
    both = jnp.concatenate([jnp.concatenate([strict, strict], axis=1), jnp.concatenate([incl, incl], axis=1)], axis=0)
    return strict, incl, both


@jax.custom_vjp
def _gram_given(x2, y2, value):
    del x2, y2
    return value.astype(F32)


def _gram_given_fwd(x2, y2, value):
    return value.astype(F32), (x2, y2, value)


def _gram_given_bwd(res, d):
    x2, y2, value = res
    d = jnp.where(_causal_masks(d.shape[-1] // 2)[2], d, 0.0)
    return _mm(d, y2, "nn", 2), _mm(d, x2, "tn", 2), jnp.zeros_like(value)


_gram_given.defvjp(_gram_given_fwd, _gram_given_bwd)


def _chunk_fwd(z0, r, lw, k, v, a, b, powers=None, gram_value=None, u_value=None):
    c = r.shape[0]
    n_h, n_k = z0.shape[0], z0.shape[1]
    mm = functools.partial(_mm, passes=REC_PASSES)
    gram = functools.partial(_mm, passes=2)
    _, incl, mask = _causal_masks(c)
    cum = _mm(incl.astype(F32), lw, "nn", 3)
    cum_end = cum[c - 1:c, :]
    e_neg, e_end = jnp.exp(-cum), jnp.exp(cum_end - cum)
    x2 = jnp.concatenate([_heads(a * jnp.exp(cum - lw)), _heads(r * jnp.exp(cum))], axis=1)
    y2 = jnp.concatenate([_heads(b * e_neg), _heads(k * e_neg)], axis=1)
    vh = _heads(v)
    g2 = jnp.where(mask, gram(x2, y2, "nt"), 0.0) if gram_value is None else _gram_given(x2, y2, gram_value)
    t2 = mm(x2, z0, "nn") + mm(g2[:, :, c:], vh, "nn")
    low = g2[:, :c, :c]
    powers = _tri_powers(low) if powers is None else powers
    u = _tri_solve(low, t2[:, :c], powers) if u_value is None else _tri_solve_given(low, t2[:, :c], powers, u_value)
    y = t2[:, c:] + mm(g2[:, c:, :c], u, "nn")
    ki = lax.broadcasted_iota(jnp.int32, (n_k, n_k), 0)
    kj = lax.broadcasted_iota(jnp.int32, (n_k, n_k), 1)
    dmat = jnp.where(ki == kj, jnp.broadcast_to(_heads(jnp.exp(cum_end)), (n_h, n_k, n_k)), 0.0)
    z_end = mm(dmat, z0, "nn") + mm(jnp.concatenate([_heads(b * e_end), _heads(k * e_end)], axis=1),
                                    jnp.concatenate([u, vh], axis=1), "tn")
    return _unheads(y), z_end, powers, g2, u


def _rec_params():
    return pltpu.CompilerParams(dimension_semantics=("arbitrary",), vmem_limit_bytes=VMEM_LIMIT, has_side_effects=True)


def _rec_fwd(u, lw, k, a, b, xch):
    t_len = lw.shape[0]
    c = min(REC_CHUNK, t_len)
    nc = t_len // c
    n_x = len(xch)
    x_specs, x_shapes, x_sems = _exchange_io(xch, [False] * n_x)
    n_pow = max(1, (c - 1).bit_length())
    sizes = [a_.size * a_.dtype.itemsize for a_ in xch]
    pass_step = [min(nc - 1, int(0.9 * nc * sum(sizes[:j + 1]) / sum(sizes)) + 1) for j in range(n_x)]

    def body(*refs):
        r_ref, v_ref, lw_ref, k_ref, a_ref, b_ref = refs[:6]
        x_in = refs[6:6 + n_x]
        y_ref, zs_ref, pw_ref, gs_ref, us_ref = refs[6 + n_x:11 + n_x]
        x_out = refs[11 + n_x:11 + 2 * n_x]
        z_scr = refs[11 + 2 * n_x]
        start, forward, wait = _gather_plan(x_in, x_out, *refs[12 + 2 * n_x:])
        i = pl.program_id(0)

        @pl.when(i == 0)
        def _():
            start()
            z_scr[...] = jnp.zeros_like(z_scr)

        z0 = z_scr[...]
        zs_ref[0] = z0
        y, z_end, powers, g2, u_rows = _chunk_fwd(z0, r_ref[...], lw_ref[...], k_ref[...], v_ref[...], a_ref[...], b_ref[...])
        y_ref[...] = y
        z_scr[...] = z_end
        pw_ref[0] = jnp.concatenate(powers, axis=0)
        gs_ref[0] = g2.astype(BF16)
        us_ref[0] = u_rows

        for j in range(n_x):
            pl.when(i == pass_step[j])(functools.partial(forward, j))

        @pl.when(i == nc - 1)
        def _():
            wait()

    blk = lambda cb: pl.BlockSpec((c, RWKV_DIM), functools.partial(lambda i, q: (i, q), q=cb))
    res = pl.pallas_call(
        body, name="rwkv_rec_fwd", grid=(nc,),
        in_specs=[blk(0), blk(2)] + [blk(0)] * 4 + x_specs,
        out_specs=[blk(0), pl.BlockSpec((1, N_HEADS, HEAD_DIM, HEAD_DIM), lambda i: (i, 0, 0, 0)),
                   pl.BlockSpec((1, n_pow * N_HEADS, c, c), lambda i: (i, 0, 0, 0)),
                   pl.BlockSpec((1, N_HEADS, 2 * c, 2 * c), lambda i: (i, 0, 0, 0)),
                   pl.BlockSpec((1, N_HEADS, c, HEAD_DIM), lambda i: (i, 0, 0, 0))] + x_specs,
        out_shape=[jax.ShapeDtypeStruct((t_len, RWKV_DIM), F32),
                   jax.ShapeDtypeStruct((nc, N_HEADS, HEAD_DIM, HEAD_DIM), F32),
                   jax.ShapeDtypeStruct((nc, n_pow * N_HEADS, c, c), BF16),
                   jax.ShapeDtypeStruct((nc, N_HEADS, 2 * c, 2 * c), BF16),
                   jax.ShapeDtypeStruct((nc, N_HEADS, c, HEAD_DIM), F32)] + x_shapes,
        scratch_shapes=[pltpu.VMEM((N_HEADS, HEAD_DIM, HEAD_DIM), F32)] + x_sems,
        compiler_params=_rec_params(),
    )(u, u, lw, k, a, b, *xch)
    return res[0], res[1:5], res[5:]


def _rec_bwd(u, lw, k, a, b, saved, dy):
    t_len = lw.shape[0]
    c = min(REC_CHUNK, t_len)
    nc = t_len // c

    zs, pw, gs, us = saved

    def body(r_ref, v_ref, lw_ref, k_ref, a_ref, b_ref, zs_ref, pw_ref, gs_ref, us_ref, dy_ref, *rest):
        g_refs, dz_scr = rest[:6], rest[6]
        powers = [pw_ref[0, j * N_HEADS:(j + 1) * N_HEADS] for j in range(pw.shape[1] // N_HEADS)]
        chunk = lambda *xs: _chunk_fwd(*xs, powers=powers, gram_value=gs_ref[0], u_value=us_ref[0])[:2]

        @pl.when(pl.program_id(0) == 0)
        def _():
            dz_scr[...] = jnp.zeros_like(dz_scr)

        _, vjp = jax.vjp(chunk, zs_ref[0], r_ref[...], lw_ref[...], k_ref[...], v_ref[...], a_ref[...], b_ref[...])
        dz0, dr, dlw, dk, dv, da, db = vjp((dy_ref[...], dz_scr[...]))
        for ref, val in zip(g_refs, (dr, dv, dlw, dk, da, db)):
            ref[...] = val
        dz_scr[...] = dz0

    blk = lambda cb: pl.BlockSpec((c, RWKV_DIM), functools.partial(lambda i, q: (nc - 1 - i, q), q=cb))
    return pl.pallas_call(
        body, name="rwkv_rec_bwd", grid=(nc,),
        in_specs=[blk(0), blk(2)] + [blk(0)] * 4
                 + [pl.BlockSpec((1, N_HEADS, HEAD_DIM, HEAD_DIM), lambda i: (nc - 1 - i, 0, 0, 0)),
                    pl.BlockSpec((1,) + pw.shape[1:], lambda i: (nc - 1 - i, 0, 0, 0)),
                    pl.BlockSpec((1,) + gs.shape[1:], lambda i: (nc - 1 - i, 0, 0, 0)),
                    pl.BlockSpec((1,) + us.shape[1:], lambda i: (nc - 1 - i, 0, 0, 0)), blk(0)],
        out_specs=[blk(0)] * 6, out_shape=[jax.ShapeDtypeStruct((t_len, RWKV_DIM), F32)] * 6,
        scratch_shapes=[pltpu.VMEM((N_HEADS, HEAD_DIM, HEAD_DIM), F32)], compiler_params=_params(("arbitrary",)),
    )(u, u, lw, k, a, b, zs, pw, gs, us, dy)


_EARLY = ["w_in", "conv_w", "w_lora_up", "a_lora_up", "g_lora_up"]
_LATE = ["w_out", "w_up", "w_down", "w_ple_gate", "w_ple_proj"]
_SHARDED = _EARLY + _LATE
_COL_SHARDED = {"w_in", "conv_w", "w_lora_up", "a_lora_up", "g_lora_up", "w_up", "w_ple_proj"}
_BF16_GATHER = {"w_in", "w_out", "w_up", "w_down", "w_ple_gate", "w_ple_proj"}
_REPLICATED = ["norm_mix_g", "shift_mu", "w0", "a0", "k_k", "k_a", "r_k", "ln_x_g", "ln_x_b", "norm_mlp_g", "norm_ple_g",
               "norm_final_g"]
_WEIGHTS = ["norm_mix_g", "w_in", "conv_w", "shift_mu", "w_lora_up", "w0", "a_lora_up", "a0", "g_lora_up", "k_k", "k_a", "r_k",
            "ln_x_g", "ln_x_b", "w_out", "norm_mlp_g", "w_up", "w_down", "norm_ple_g", "w_ple_gate", "w_ple_proj", "norm_final_g"]


def _unshard(name, g):
    if name in _COL_SHARDED:
        return jnp.moveaxis(g, 0, 1).reshape(g.shape[1], N_DEV * g.shape[2])
    return g.reshape(N_DEV * g.shape[1], g.shape[2])


def _reshard(name, full):
    if name in _COL_SHARDED:
        return jnp.moveaxis(full.reshape(full.shape[0], N_DEV, full.shape[1] // N_DEV), 1, 0)
    return full.reshape(N_DEV, full.shape[0] // N_DEV, full.shape[1])


def _pad_in_cols(a):
    z = lambda n: jnp.zeros(a.shape[:-1] + (n,), a.dtype)
    conv = [a[..., part * CONV_DIM + j * LANE:part * CONV_DIM + (j + 1) * LANE] for j in range(CONV_DIM // LANE) for part in range(3)]
    return jnp.concatenate(conv + [a[..., CONV_COLS:3136], z(64), a[..., 3136:3200], z(64), a[..., 3200:3360], z(96)], axis=-1)


def _unpad_in_cols(a):
    conv = [a[..., (3 * j + part) * LANE:(3 * j + part + 1) * LANE] for part in range(3) for j in range(CONV_DIM // LANE)]
    return jnp.concatenate(conv + [a[..., CONV_COLS:3136], a[..., 3200:3264], a[..., 3328:3488]], axis=-1)


def _assemble_w_in(g):
    n_dev, rows, cols = g.shape

    def body(g_ref, o_ref):
        o_ref[...] = _pad_in_cols(jnp.concatenate([g_ref[d] for d in range(n_dev)], axis=1))

    return pl.pallas_call(
        body, name="w_in_assemble", grid=(rows // ROW_BLOCK,),
        in_specs=[pl.BlockSpec((n_dev, ROW_BLOCK, cols), lambda i: (0, i, 0))],
        out_specs=pl.BlockSpec((ROW_BLOCK, IN_PAD), lambda i: (i, 0)),
        out_shape=jax.ShapeDtypeStruct((rows, IN_PAD), g.dtype), compiler_params=_params(("arbitrary",)),
    )(g)


def _split_w_in_grad(dw):
    rows = dw.shape[0]
    cols = IN_COLS // N_DEV

    def body(d_ref, o_ref):
        full = _unpad_in_cols(d_ref[...])
        for d in range(N_DEV):
            o_ref[d] = full[:, cols * d:cols * (d + 1)]

    return pl.pallas_call(
        body, name="w_in_grad_split", grid=(rows // ROW_BLOCK,),
        in_specs=[pl.BlockSpec((ROW_BLOCK, IN_PAD), lambda i: (i, 0))],
        out_specs=pl.BlockSpec((N_DEV, ROW_BLOCK, cols), lambda i: (0, i, 0)),
        out_shape=jax.ShapeDtypeStruct((N_DEV, rows, cols), dw.dtype), compiler_params=_params(("arbitrary",)),
    )(dw)


def _pad_rows(a, rows):
    return jnp.concatenate([a, jnp.zeros((rows - a.shape[0],) + a.shape[1:], a.dtype)], axis=0)


SEG_W = [RWKV_DIM, RWKV_DIM, RWKV_DIM, LANE, LANE, 2 * LANE]
SEG_OFF = [0, 512, 1024, XW_OFF, XA_OFF, XG_OFF]


def _rwkv_pre_bwd(proj, u, grads, mu, small, dproj):
    t_len = u.shape[0]
    tr = min(ROW_BLOCK, t_len)
    nb = t_len // tr
    sub = 8
    n_g = len(grads)
    acc_shapes = [(1, RW_PAD)] + [(1, RWKV_DIM)] * 4 + [(LANE, RWKV_DIM), (LANE, RWKV_DIM), (2 * LANE, RWKV_DIM)]

    def body(*refs):
        seg_refs, halo_refs = refs[:6], refs[6:12]
        k_ref, xw_ref, xa_ref, xg_ref = refs[12:16]
        g_refs = refs[16:16 + n_g]
        mu_ref = refs[16 + n_g]
        prm_refs = refs[17 + n_g:24 + n_g]
        out_hbm = refs[25 + n_g]
        acc_refs = refs[26 + n_g:26 + n_g + len(acc_shapes)]
        vbuf, sems, carry = refs[26 + n_g + len(acc_shapes):]
        i = pl.program_id(0)
        blk = nb - 1 - i
        dr1, dr2, dv1, dv2, dlw, dk1, dk2, da, db, dg = [g[...] for g in g_refs]
        _, vjp = jax.vjp(_rwkv_pre, k_ref[...], xw_ref[...], xa_ref[...], xg_ref[...], *[p_[...] for p_ in prm_refs])
        dk, dxw, dxa, dxg, *dprm = vjp((dlw, dk1 + dk2, da, db, dg))
        du = jnp.concatenate([dr1 + dr2, dk, dv1 + dv2, dxw, dxa, dxg], axis=1)
        mu_v = mu_ref[...]

        @pl.when(i == 0)
        def _():
            carry[...] = jnp.zeros_like(carry)

        rows = lax.broadcasted_iota(jnp.int32, du.shape, 0)
        nxt = jnp.where(rows == tr - 1, carry[...], pltpu.roll(du, tr - 1, 0))
        d_rw = du - mu_v * du + mu_v * nxt
        d_mu = []
        for s_ref, h_ref, off, wd in zip(seg_refs, halo_refs, SEG_OFF, SEG_W):
            cur = s_ref[...]
            r0 = lax.broadcasted_iota(jnp.int32, cur.shape, 0)
            prev = jnp.where(r0 == 0, jnp.where(blk == 0, 0.0, h_ref[sub - 1:sub, :]), pltpu.roll(cur, 1, 0))
            d_mu.append(jnp.sum(du[:, off:off + wd] * (prev - cur), axis=0, keepdims=True))
        sums = [jnp.concatenate(d_mu, axis=1)] + list(dprm)

        @pl.when(i == 0)
        def _():
            for a_ref, val in zip(acc_refs, sums):
                a_ref[...] = val

        @pl.when(i > 0)
        def _():
            for a_ref, val in zip(acc_refs, sums):
                a_ref[...] += val

        carry[...] = du[0:1, :]
        slot = i % 2

        def writeback(s, b):
            return pltpu.make_async_copy(vbuf.at[s], out_hbm.at[pl.ds(b * tr, tr), pl.ds(CONV_COLS, RW_PAD)], sems.at[s])

        @pl.when(i >= 2)
        def _():
            writeback(slot, blk + 2).wait()

        vbuf[slot] = d_rw.astype(vbuf.dtype)
        writeback(slot, blk).start()

        @pl.when(i == nb - 1)
        def _():
            writeback(slot, blk).wait()
            if nb > 1:
                writeback(1 - slot, blk + 1).wait()

    rev = lambda w_, cb: pl.BlockSpec((tr, w_), functools.partial(lambda i, c: (nb - 1 - i, c), c=cb))
    halo = lambda w_, cb: pl.BlockSpec((sub, w_), functools.partial(
        lambda i, c: (jnp.maximum((nb - 1 - i) * (tr // sub) - 1, 0), c), c=cb))
    whole = lambda a: pl.BlockSpec(a.shape, functools.partial(lambda i, n: (0,) * n, n=a.ndim))
    segs = [(wd, (CONV_COLS + off) // wd) for off, wd in zip(SEG_OFF, SEG_W)]
    u_cols = [(512, 1), (LANE, XW_OFF // LANE), (LANE, XA_OFF // LANE), (2 * LANE, XG_OFF // (2 * LANE))]
    any_spec = pl.BlockSpec(memory_space=pl.ANY)
    res = pl.pallas_call(
        body, name="rwkv_pre_bwd", grid=(nb,),
        in_specs=[rev(*s) for s in segs] + [halo(*s) for s in segs] + [rev(*c) for c in u_cols]
                 + [rev(RWKV_DIM, 0)] * n_g + [whole(mu)] + [whole(p_) for p_ in small] + [any_spec],
        out_specs=[any_spec] + [pl.BlockSpec(s, functools.partial(lambda i, n: (0,) * n, n=len(s))) for s in acc_shapes],
        out_shape=[jax.ShapeDtypeStruct(dproj.shape, dproj.dtype)] + [jax.ShapeDtypeStruct(s, F32) for s in acc_shapes],
        scratch_shapes=[pltpu.VMEM((2, tr, RW_PAD), dproj.dtype), pltpu.SemaphoreType.DMA((2,)), pltpu.VMEM((1, RW_PAD), F32)],
        input_output_aliases={24 + n_g: 0},
        compiler_params=_params(("arbitrary",)),
    )(*[proj] * 12, *[u] * 4, *grads, mu, *small, dproj)
    return res


def _local_step(x, p, tgt, w, early_shards, late_shards):
    row = lambda v: v.reshape(1, -1)
    w = dict(w)

    xn1, *gathered = _rowwise("rms_mix", lambda h, g: (_rms(h, g),), [x], [w["norm_mix_g"]], [(D_MODEL, BF16)],
                              gather=early_shards)
    w.update({n: _unshard(n, g_) for n, g_ in zip(_EARLY[1:], gathered[1:])})
    w["w_in"] = _assemble_w_in(gathered[0])
    w["w_lora_up"] = _pad_rows(w["w_lora_up"], LANE)
    w["a_lora_up"] = _pad_rows(w["a_lora_up"], LANE)
    w["g_lora_up"] = _pad_rows(w["g_lora_up"], 2 * LANE)
    proj = _matmul("in_proj", xn1, w["w_in"], "nn", [F32], tm=2048, tn=512, tk=D_MODEL)
    n_cb = CONV_DIM // LANE

    def conv_fwd(blk, cw):
        gb, gc, hx = blk[:, :LANE], blk[:, LANE:2 * LANE], blk[:, 2 * LANE:]
        uu = gc * hx
        return (gb * (uu * cw[2:3] + _shift_down(uu, 1) * cw[1:2] + _shift_down(uu, 2) * cw[0:1]),)

    (y_conv,) = _colwise("conv_fwd", conv_fwd, n_cb, [(proj, 3 * LANE)], [w["conv_w"]], [(CONV_DIM, BF16, LANE)])

    small = [w["w0"], w["a0"], w["k_k"], w["k_a"], w["w_lora_up"], w["a_lora_up"], w["g_lora_up"]]
    def pre_fwd(*xs):
        cur, prev_rows, mu, prm = xs[:6], xs[6:12], xs[12], xs[13:]
        segs = []
        for c_, p_, off, wd in zip(cur, prev_rows, SEG_OFF, SEG_W):
            rows = lax.broadcasted_iota(jnp.int32, c_.shape, 0)
            prev = jnp.where(rows == 0, p_, pltpu.roll(c_, 1, 0))
            segs.append(c_ + mu[:, off:off + wd] * (prev - c_))
        return (jnp.concatenate(segs, axis=1),) + tuple(_rwkv_pre(segs[1], segs[3], segs[4], segs[5], *prm))

    proj_segs = [(proj, wd, (CONV_COLS + off) // wd) for off, wd in zip(SEG_OFF, SEG_W)]
    big = [n for n in _LATE if n in ("w_up", "w_down")]
    rest = [n for n in _LATE if n not in big]
    u, lw, k_h, ra, rb, g, *got_rest = _rowwise(
        "rwkv_pre", pre_fwd, proj_segs, [w["shift_mu"]] + small, [(RW_PAD, F32)] + [(RWKV_DIM, F32)] * 5, halo=True,
        gather=[late_shards[_LATE.index(n)] for n in rest])
    y_rec, rec_saved, got_big = _rec_fwd(u, lw, k_h, ra, rb, [late_shards[_LATE.index(n)] for n in big])
    for n, gathered in list(zip(rest, got_rest)) + list(zip(big, got_big)):
        w[n] = _unshard(n, gathered)
    post_c = [w["ln_x_g"], w["ln_x_b"], w["r_k"]]
    u_r, u_v = (u, 512, 0), (u, 512, 2)
    (y_rwkv,) = _rowwise("rwkv_post", lambda *xs: (_rwkv_post(*xs),), [y_rec, u_r, k_h, u_v, g], post_c, [(RWKV_DIM, BF16)],
                         tr=2 * ROW_BLOCK)
    ycat = jnp.concatenate([y_conv, y_rwkv], axis=1)
    def res_norm(acc, r_, g_):
        h = acc + r_
        return h, _rms(h, g_)

    h1, xn2 = _matmul("out_proj", ycat, w["w_out"], "nn", [F32, BF16], tm=1024, tn=D_MODEL, tk=D_MODEL, extras=[x],
                      consts=[w["norm_mlp_g"]], epilogue=res_norm)

    square = lambda h: h.astype(F32) * h.astype(F32)
    hid = _matmul("mlp_up", xn2, w["w_up"], "nn", [BF16], tm=2048, tn=1024, tk=D_MODEL,
                  epilogue=lambda acc: (jnp.maximum(acc, 0.0),))
    h2, xn3 = _matmul("mlp_down", hid, w["w_down"], "nn", [F32, BF16], tm=512, tn=D_MODEL, tk=D_FF, extras=[h1],
                      consts=[w["norm_ple_g"]], epilogue=res_norm, a_map=square)
    zg = _matmul("ple_gate", xn3, w["w_ple_gate"], "nn", [F32], tm=1024, tn=1024, tk=D_MODEL)
    pp = _matmul("ple_proj", p, w["w_ple_proj"], "nn", [F32], tm=1024, tn=1024, tk=PLE_DIM)

    def head(h2_, zg_, pp_, tg, gf):
        gate = _sigmoid(zg_)
        h3 = h2_ + gate * pp_
        out = _rms(h3, gf)
        err = out - tg
        dh3, dgf = _rms_bwd(h3, gf, err * (1.0 / D_MODEL))
        loss = jnp.sum(jnp.sum(err * err, axis=1, keepdims=True), axis=0, keepdims=True) * (0.5 / D_MODEL)
        return dh3, dh3 * pp_ * gate * (1.0 - gate), dh3 * gate, dgf, loss

    dh3, dzg, dpp, d_norm_final, loss = _rowwise(
        "head", head, [h2, zg, pp, tgt], [row(w["norm_final_g"])], [(D_MODEL, F32), (D_MODEL, BF16), (D_MODEL, BF16)],
        [(1, D_MODEL), (1, 1)], tr=2 * ROW_BLOCK)

    d_w_ple_proj = _matmul("d_ple_proj", p, dpp, "tn", [BF16], tm=PLE_DIM, tn=D_MODEL // N_DEV, tk=4096, col_blocks_out=True)
    d_w_ple_gate = _matmul("d_ple_gate", xn3, dzg, "tn", [BF16], tm=512, tn=1024, tk=4096)

    def norm_bwd(dxn, h, dres, g_):
        dh, dg = _rms_bwd(h, g_, dxn)
        dh = dh + dres
        return dh, dh, dg

    nb = dict(tm=512, tn=D_MODEL, epilogue=norm_bwd, sums=[(1, D_MODEL)])
    dh2, dh2_b, d_norm_ple = _matmul("dx_ple_gate", dzg, w["w_ple_gate"], "nt", [F32, BF16], tk=D_MODEL,
                                     extras=[h2, dh3], consts=[w["norm_ple_g"]], **nb)
    d_w_down = _matmul("d_mlp_down", hid, dh2_b, "tn", [BF16], tm=512, tn=1024, tk=4096, a_map=square)
    dpre = _matmul("dx_mlp_down", dh2_b, w["w_down"], "nt", [BF16], tm=2048, tn=1024, tk=D_MODEL, extras=[hid],
                   epilogue=lambda acc, hid_: (acc * (2.0 * hid_.astype(F32)),))
    d_w_up = _matmul("d_mlp_up", xn2, dpre, "tn", [BF16], tm=1024, tn=D_FF // N_DEV, tk=4096, col_blocks_out=True)
    dh1, dh1_b, d_norm_mlp = _matmul("dx_mlp_up", dpre, w["w_up"], "nt", [F32, BF16], tk=D_FF,
                                     extras=[h1, dh2], consts=[w["norm_mlp_g"]], **nb)
    d_w_out = _matmul("d_out_proj", ycat, dh1_b, "tn", [BF16], tm=512, tn=1024, tk=4096)
    dycat = _matmul("dx_out_proj", dh1_b, w["w_out"], "nt", [F32], tm=1024, tn=1024, tk=D_MODEL)
    late_grads = dict(w_out=d_w_out, w_up=d_w_up, w_down=d_w_down, w_ple_gate=d_w_ple_gate, w_ple_proj=d_w_ple_proj)
    late_send = [late_grads[n] if n in ("w_up", "w_ple_proj") else _reshard(n, late_grads[n]) for n in _LATE]
    *late_flight, late_token = _scatter_start("late_scatter_start", late_send, [lax.empty(a.shape, a.dtype) for a in late_send])
    conv_w_bwd = w["conv_w"] + late_token[0:1, 0:1]

    def conv_bwd(dy, blk, cw):
        gb, gc, hx = blk[:, :LANE], blk[:, LANE:2 * LANE], blk[:, 2 * LANE:]
        uu = gc * hx
        u1, u2 = _shift_down(uu, 1), _shift_down(uu, 2)
        dconv = dy * gb
        du = dconv * cw[2:3] + _shift_up(dconv, 1) * cw[1:2] + _shift_up(dconv, 2) * cw[0:1]
        s = lambda z: jnp.sum(z, axis=0, keepdims=True)
        d_blk = jnp.concatenate([dy * (uu * cw[2:3] + u1 * cw[1:2] + u2 * cw[0:1]), du * hx, du * gc], axis=1)
        return d_blk, s(dconv * u2), s(dconv * u1), s(dconv * uu)

    dproj, dcw0, dcw1, dcw2 = _colwise(
        "conv_bwd", conv_bwd, n_cb, [(dycat, LANE), (proj, 3 * LANE)], [conv_w_bwd],
        [(IN_PAD, BF16, 3 * LANE)], [(1, CONV_DIM)] * 3)

    def post_bwd(dy, y, r, k_h_, v, g_, ln_g, ln_b, r_k):
        _, vjp = jax.vjp(_rwkv_post, y, r, k_h_, v, g_, ln_g, ln_b, r_k)
        return vjp(dy)

    dy_rec, dr_p, dk_p, dv_p, dg, d_ln_g, d_ln_b, d_r_k = _rowwise(
        "rwkv_post_bwd", post_bwd, [(dycat, 512, 1), y_rec, u_r, k_h, u_v, g], post_c,
        [(RWKV_DIM, F32)] * 5, [(1, RWKV_DIM)] * 3)
    dr_r, dv_r, dlw, dk_r, da, db = _rec_bwd(u, lw, k_h, ra, rb, rec_saved, dy_rec)

    dproj, d_mu, d_w0, d_a0, d_k_k, d_k_a, d_wl, d_al, d_gl = _rwkv_pre_bwd(
        proj, u, [dr_p, dr_r, dv_p, dv_r, dlw, dk_p, dk_r, da, db, dg], w["shift_mu"], small, dproj)
    d_w_in = _matmul("d_in_proj", xn1, dproj, "tn", [BF16], tm=1024, tn=896, tk=4096)
    early_grads = dict(conv_w=jnp.concatenate([dcw0, dcw1, dcw2], axis=0),
                       w_lora_up=d_wl[:64], a_lora_up=d_al[:64], g_lora_up=d_gl[:160])
    early_send = [_split_w_in_grad(d_w_in)] + [_reshard(n, early_grads[n]) for n in _EARLY[1:]]
    *early_flight, token = _scatter_start("early_scatter_start", early_send, [lax.empty(a.shape, a.dtype) for a in early_send])
    dx, d_norm_mix = _matmul(
        "dx_in_proj", dproj, w["w_in"], "nt", [F32], tk=IN_PAD, extras=[x, dh1], consts=[w["norm_mix_g"] + token[0:1, 0:1]],
        **dict(nb, epilogue=lambda *a: norm_bwd(*a)[1:]))

    grads = dict(
        norm_mix_g=d_norm_mix, shift_mu=d_mu, w0=d_w0, a0=d_a0, k_k=d_k_k, k_a=d_k_a, r_k=d_r_k,
        ln_x_g=d_ln_g, ln_x_b=d_ln_b, norm_mlp_g=d_norm_mlp, norm_ple_g=d_norm_ple, norm_final_g=d_norm_final)
    return loss, dx, grads, late_flight, early_flight, d_w_in


def _adam_update(partials, w_ref, m_ref, v_ref, g_ref, d_ref, nm_ref, nv_ref):
    g = partials[0].astype(F32)
    for part in partials[1:]:
        g = g + part.astype(F32)
    nm =ADAM_B1 * m_ref[...] + (1.0 - ADAM_B1) * g
    nv = ADAM_B2 * v_ref[...] + (1.0 - ADAM_B2) * (g * g)
    m_hat = nm / (1.0 - ADAM_B1 ** ADAM_STEP)
    v_hat = nv / (1.0 - ADAM_B2 ** ADAM_STEP)
    g_ref[...] = g
    d_ref[...] = -ADAM_LR * (m_hat / (jnp.sqrt(v_hat) + ADAM_EPS) + ADAM_WD * w_ref[...])
    nm_ref[...] = nm
    nv_ref[...] = nv


SMALL_ROWS = 8


def _small_layout(widths):
    widths = list(widths) + [1]
    fill, place = [0] * SMALL_ROWS, [None] * len(widths)
    for j in sorted(range(len(widths)), key=lambda q: -widths[q]):
        row = fill.index(min(fill))
        place[j] = (row, fill[row])
        fill[row] += -(-widths[j] // LANE) * LANE
    return place, max(fill)


def _pack_small(vecs, loss):
    place, total = _small_layout([v_.shape[1] for v_ in vecs])
    n = len(vecs)

    def body(*refs):
        out = jnp.zeros((SMALL_ROWS, total), F32)
        row_id = lax.broadcasted_iota(jnp.int32, (SMALL_ROWS, total), 0)
        for row in range(SMALL_ROWS):
            mine = sorted((off, j) for j, (r_, off) in enumerate(place) if r_ == row)
            pieces, at = [], 0
            for off, j in mine:
                val = refs[j][...]
                pieces.append(val)
                at = off + val.shape[1]
                pad = -val.shape[1] % LANE
                if pad:
                    pieces.append(jnp.zeros((1, pad), F32))
                    at += pad
            if total > at:
                pieces.append(jnp.zeros((1, total - at), F32))
            out = jnp.where(row_id == row, jnp.broadcast_to(jnp.concatenate(pieces, axis=1), (SMALL_ROWS, total)), out)
        refs[n + 1][...] = out

    return pl.pallas_call(body, name="pack_small", out_shape=jax.ShapeDtypeStruct((SMALL_ROWS, total), F32))(*vecs, loss)


def _adamw_small(packed, ws, ms, vs):
    n = len(ws)
    place, _ = _small_layout([w_.shape[1] for w_ in ws])

    def body(p_ref, *refs):
        w_refs, m_refs, v_refs, outs = refs[:n], refs[n:2 * n], refs[2 * n:3 * n], refs[3 * n:]
        for j in range(n):
            row, off = place[j]
            cols = pl.ds(off, ws[j].shape[1])
            _adam_update([p_ref[s, row:row + 1, cols] for s in range(N_DEV)], w_refs[j], m_refs[j], v_refs[j],
                         *outs[4 * j:4 * j + 4])
        row, off = place[n]
        total = p_ref[0, row:row + 1, off:off + 1]
        for s in range(1, N_DEV):
            total = total + p_ref[s, row:row + 1, off:off + 1]
        outs[4 * n][...] = total

    res = pl.pallas_call(
        body, name="adamw_small",
        out_shape=[jax.ShapeDtypeStruct(w_.shape, F32) for w_ in ws for _ in range(4)] + [jax.ShapeDtypeStruct((1, 1), F32)],
    )(packed, *ws, *ms, *vs)
    return [res[4 * j:4 * j + 4] for j in range(n)], res[4 * n]


def _adamw(name, parts, w, m, v, own=None, me=None):
    rows, cols = w.shape[-2:]
    lead = w.ndim - 2
    tr = rows if rows * cols * 4 * 8 <= (4 << 20) else max(8, (4 << 20) // (cols * 4 * 8) // 8 * 8)
    while rows % tr:
        tr -= 8
    shape4 = [jax.ShapeDtypeStruct(w.shape, F32)] * 4
    if own is None:
        def body(p_ref, *refs):
            _adam_update([p_ref[s] for s in range(N_DEV)], *refs)

        blk = pl.BlockSpec((None,) * lead + (tr, cols), lambda i: (0,) * lead + (i, 0))
        return pl.pallas_call(
            body, name=name, grid=(rows // tr,),
            in_specs=[pl.BlockSpec((N_DEV, tr, cols), lambda i: (0, i, 0)), blk, blk, blk], out_specs=[blk] * 4,
            out_shape=shape4, compiler_params=_params(("arbitrary",)),
        )(parts, w, m, v)

    def body_own(me_ref, p_ref, own_ref, *refs):
        mine = own_ref[...]
        _adam_update([jnp.where(me_ref[0] == s, mine, p_ref[s]) for s in range(N_DEV)], *refs)

    blk = pl.BlockSpec((None,) * lead + (tr, cols), lambda i, me_ref: (0,) * lead + (i, 0))
    return pl.pallas_call(
        body_own, name=name, out_shape=shape4,
        grid_spec=pltpu.PrefetchScalarGridSpec(
            num_scalar_prefetch=1, grid=(rows // tr,),
            in_specs=[pl.BlockSpec((N_DEV, tr, cols), lambda i, me_ref: (0, i, 0)),
                      pl.BlockSpec((None, tr, cols), lambda i, me_ref: (me_ref[0], i, 0)), blk, blk, blk],
            out_specs=[blk] * 4),
        compiler_params=_params(("arbitrary",)),
    )(me, parts, own, w, m, v)


def kernel(x, p, norm_mix_g, w_in, conv_w, shift_mu, w_lora_up, w0, a_lora_up, a0, g_lora_up, k_k, k_a, r_k, ln_x_g, ln_x_b, w_out, norm_mlp_g, w_up, w_down, norm_ple_g, w_ple_gate, w_ple_proj, norm_final_g, loss_target, m_norm_mix_g, m_w_in, m_conv_w, m_shift_mu, m_w_lora_up, m_w0, m_a_lora_up, m_a0, m_g_lora_up, m_k_k, m_k_a, m_r_k, m_ln_x_g, m_ln_x_b, m_w_out, m_norm_mlp_g, m_w_up, m_w_down, m_norm_ple_g, m_w_ple_gate, m_w_ple_proj, m_norm_final_g, v_norm_mix_g, v_w_in, v_conv_w, v_shift_mu, v_w_lora_up, v_w0, v_a_lora_up, v_a0, v_g_lora_up, v_k_k, v_k_a, v_r_k, v_ln_x_g, v_ln_x_b, v_w_out, v_norm_mlp_g, v_w_up, v_w_down, v_norm_ple_g, v_w_ple_gate, v_w_ple_proj, v_norm_final_g):
    args = dict(locals())
    wts = {n: args[n] for n in _WEIGHTS}
    mom = {n: args["m_" + n] for n in _WEIGHTS}
    var = {n: args["v_" + n] for n in _WEIGHTS}
    shard2d = lambda a: a.reshape(a.shape[-2:])
    pad_mu = lambda a: _pad_in_cols(jnp.concatenate([jnp.zeros((1, CONV_COLS), F32), a], axis=1))[:, CONV_COLS:]
    unpad_mu = lambda a: _unpad_in_cols(jnp.concatenate([jnp.zeros((1, CONV_COLS), F32), a], axis=1))[:, CONV_COLS:]

    shards = {n: shard2d(wts[n]).astype(BF16 if n in _BF16_GATHER else F32) for n in _SHARDED}
    w = {n: wts[n].reshape(1, -1) for n in _REPLICATED}
    w["shift_mu"] = pad_mu(wts["shift_mu"])

    loss, dx, grads, late_flight, early_flight, d_w_in = _local_step(
        x[0], p[0, 0], loss_target[0], w, [shards[n] for n in _EARLY], [shards[n] for n in _LATE])

    me = (4 * lax.axis_index("x") + 2 * lax.axis_index("y") + lax.axis_index("c")).astype(jnp.int32).reshape(1)
    late_sent, late_parts = _scatter_wait("late_scatter_wait", *late_flight, after=[d_w_in])
    out = {n: _adamw("adamw_" + n, prt, wts[n], mom[n], var[n], own=own, me=me)
           for n, prt, own in zip(_LATE, late_parts, late_sent)}
    early_sent, early_parts = _scatter_wait("early_scatter_wait", *early_flight, after=[dx] + [out[n][1] for n in _LATE])
    for n, prt, own in zip(_EARLY, early_parts, early_sent):
        out[n] = _adamw("adamw_" + n, prt, wts[n], mom[n], var[n], own=own, me=me)

    grads["shift_mu"] = unpad_mu(grads["shift_mu"])
    flat = lambda a: a.reshape(1, -1)
    (small_parts,) = _exchange("gather_small", [_pack_small([flat(grads[n]) for n in _REPLICATED], loss)], [False])
    small, loss_total = _adamw_small(small_parts, *[[flat(d[n]) for n in _REPLICATED] for d in (wts, mom, var)])
    for n, res in zip(_REPLICATED, small):
        out[n] = [r.reshape(wts[n].shape) for r in res]
    return (loss_total[0, 0], dx[None], *[out[n][0] for n in _WEIGHTS], *[out[n][1] for n in _WEIGHTS],
            *[out[n][2] for n in _WEIGHTS], *[out[n][3] for n in _WEIGHTS])
```

```python
import functools

import jax
import jax.numpy as jnp
from jax import lax
from jax.experimental import pallas as pl
from jax.experimental.pallas import tpu as pltpu

F32 = jnp.float32
BF16 = jnp.bfloat16

N_DEV = 8
D_MODEL = 1024
CONV_DIM = 512
RWKV_DIM = 512
HEAD_DIM = 64
N_HEADS = 8
D_FF = 4096
PLE_DIM = 256
RMS_EPS = 1e-6
GN_EPS = 64e-5
L2_EPS = 1e-12
ADAM_LR, ADAM_B1, ADAM_B2, ADAM_EPS, ADAM_WD, ADAM_STEP = 0.001, 0.9, 0.999, 1e-08, 0.01, 10

CONV_COLS = 3 * CONV_DIM
RW_PAD = 2048
IN_PAD = CONV_COLS + RW_PAD
IN_COLS = 3360
XW_OFF, XA_OFF, XG_OFF = 1536, 1664, 1792
REC_CHUNK = 128
REC_CHUNKS_PER_STEP = 2
REC_PASSES = 1
ROW_BLOCK = 256
LANE = 128
VMEM_LIMIT = 56 * 1024 * 1024


def _dims(dn, ndim):
    if ndim == 3:
        return {"nn": (((2,), (1,)), ((0,), (0,))), "nt": (((2,), (2,)), ((0,), (0,))),
                "tn": (((1,), (1,)), ((0,), (0,)))}[dn]
    return {"nn": (((1,), (0,)), ((), ())), "nt": (((1,), (1,)), ((), ())), "tn": (((0,), (0,)), ((), ()))}[dn]


def _split2(x):
    hi = x.astype(BF16)
    return hi, (x - hi.astype(F32)).astype(BF16)


def _mm_raw(x, y, dn, passes):
    f = lambda p, q: lax.dot_general(p, q, _dims(dn, x.ndim), preferred_element_type=F32)
    if passes == 1:
        return f(x.astype(BF16), y.astype(BF16))
    xh, xl = _split2(x)
    yh, yl = _split2(y)
    if passes == 2:
        return f(xh, yh) + f(xh, yl)
    return f(xh, yh) + f(xh, yl) + f(xl, yh)


@functools.partial(jax.custom_vjp, nondiff_argnums=(2, 3))
def _mm(x, y, dn, passes):
    return _mm_raw(x, y, dn, passes)


def _mm_fwd(x, y, dn, passes):
    return _mm_raw(x, y, dn, passes), (x, y)


def _mm_bwd(dn, passes, res, d):
    x, y = res
    if dn == "nn":
        return _mm(d, y, "nt", passes), _mm(x, d, "tn", passes)
    if dn == "nt":
        return _mm(d, y, "nn", passes), _mm(d, x, "tn", passes)
    return _mm(y, d, "nt", passes), _mm(x, d, "nn", passes)


_mm.defvjp(_mm_fwd, _mm_bwd)


def _head_ones():
    i = lax.broadcasted_iota(jnp.int32, (RWKV_DIM, RWKV_DIM), 0) // HEAD_DIM
    j = lax.broadcasted_iota(jnp.int32, (RWKV_DIM, RWKV_DIM), 1) // HEAD_DIM
    return (i == j).astype(BF16)


def _hsum_raw(x):
    ones = _head_ones()
    f = lambda p: lax.dot_general(p, ones, _dims("nn", 2), preferred_element_type=F32)
    x1, x2 = _split2(x)
    return f(x1) + f(x2)


@jax.custom_vjp
def _hsum(x):
    return _hsum_raw(x)


_hsum.defvjp(lambda x: (_hsum_raw(x), None), lambda _, d: (_hsum(d),))


def _sigmoid(x):
    return 0.5 + 0.5 * jnp.tanh(0.5 * x)


def _softplus(x):
    return jnp.maximum(x, 0.0) + jnp.log(1.0 + jnp.exp(-jnp.abs(x)))


def _params(sem):
    return pltpu.CompilerParams(dimension_semantics=sem, vmem_limit_bytes=VMEM_LIMIT)


def _rowwise(name, fn, rows, consts, row_outs, acc_outs=(), tr=ROW_BLOCK, halo=False, gather=()):
    rows = [r if isinstance(r, tuple) else (r, r.shape[1], 0) for r in rows]
    t_len = rows[0][0].shape[0]
    tr = min(tr, t_len)
    n_r, n_c, n_o, n_a, n_x = len(rows), len(consts), len(row_outs), len(acc_outs), len(gather)
    n_h = n_r if halo else 0
    sub = 8
    x_specs, x_shapes, x_sems = _exchange_io(gather, [False] * n_x) if n_x else ([], [], [])
    nb = t_len // tr

    def body(*refs):
        if n_x:
            n_in = n_r + n_h + n_c
            start, forward, wait = _gather_plan(refs[n_in:n_in + n_x], refs[len(refs) - 3 - n_x:len(refs) - 3], *refs[len(refs) - 3:])
            pl.when(pl.program_id(0) == 0)(start)
            refs = refs[:n_in] + refs[n_in + n_x:len(refs) - 3 - n_x]
        ins = [r[...] for r in refs[:n_r]]
        ins += [jnp.where(pl.program_id(0) == 0, 0.0, r[sub - 1:sub, :]) for r in refs[n_r:n_r + n_h]]
        ins += [r[...] for r in refs[n_r + n_h:n_r + n_h + n_c]]
        refs = refs[:n_r] + refs[n_r + n_h:]
        outs = fn(*ins)
        o_refs = refs[n_r + n_c:n_r + n_c + n_o]
        a_refs = refs[n_r + n_c + n_o:]
        for o_ref, val in zip(o_refs, outs[:n_o]):
            o_ref[...] = val.astype(o_ref.dtype)
        if n_a:
            first = pl.program_id(0) == 0

            @pl.when(first)
            def _():
                for a_ref, val in zip(a_refs, outs[n_o:]):
                    a_ref[...] = val

            @pl.when(jnp.logical_not(first))
            def _():
                for a_ref, val in zip(a_refs, outs[n_o:]):
                    a_ref[...] += val

        if n_x:
            @pl.when(pl.program_id(0) == nb - 1)
            def _():
                for j in range(n_x):
                    forward(j)
                wait()

    in_specs = [pl.BlockSpec((tr, w), functools.partial(lambda i, c: (i, c), c=cb)) for _, w, cb in rows]
    if halo:
        in_specs += [pl.BlockSpec((sub, w), functools.partial(lambda i, c: (jnp.maximum(i * (tr // sub) - 1, 0), c), c=cb))
                     for _, w, cb in rows]
    in_specs += [pl.BlockSpec(c.shape, functools.partial(lambda i, n: (0,) * n, n=c.ndim)) for c in consts]
    out_specs = [pl.BlockSpec((tr, w), lambda i: (i, 0)) for w, _ in row_outs]
    out_specs += [pl.BlockSpec(s, functools.partial(lambda i, n: (0,) * n, n=len(s))) for s in acc_outs]
    out_shape = [jax.ShapeDtypeStruct((t_len, w), dt) for w, dt in row_outs]
    out_shape += [jax.ShapeDtypeStruct(s, F32) for s in acc_outs]
    return pl.pallas_call(
        body, name=name, grid=(nb,), in_specs=in_specs + x_specs, out_specs=out_specs + x_specs,
        out_shape=out_shape + x_shapes, scratch_shapes=x_sems,
        compiler_params=pltpu.CompilerParams(dimension_semantics=("arbitrary",), vmem_limit_bytes=VMEM_LIMIT,
                                             has_side_effects=bool(n_x)),
    )(*[r[0] for r in rows], *([r[0] for r in rows] if halo else []), *consts, *gather)


def _colwise(name, fn, n_blocks, cols, prms, col_outs, prm_outs=()):
    t_len = cols[0][0].shape[0]
    n_i = len(cols) + len(prms)

    def body(*refs):
        outs = fn(*[r[...] for r in refs[:n_i]])
        for o_ref, val in zip(refs[n_i:], outs):
            o_ref[...] = val.astype(o_ref.dtype)

    spec = lambda r, w: pl.BlockSpec((r, w), lambda j: (0, j))
    in_specs = [spec(t_len, w) for _, w in cols] + [spec(a.shape[0], LANE) for a in prms]
    out_specs = [spec(t_len, bw) for _, _, bw in col_outs] + [spec(r, LANE) for r, _ in prm_outs]
    out_shape = [jax.ShapeDtypeStruct((t_len, w), dt) for w, dt, _ in col_outs]
    out_shape += [jax.ShapeDtypeStruct((r, w), F32) for r, w in prm_outs]
    return pl.pallas_call(
        body, name=name, grid=(n_blocks,), in_specs=in_specs, out_specs=out_specs, out_shape=out_shape,
        compiler_params=_params(("arbitrary",)),
    )(*[c[0] for c in cols], *prms)


def _matmul(name, a, b, dn, outs, *, tm, tn, tk, extras=(), consts=(), epilogue=None, sums=(), xch=(), xch_scatter=(),
            a_map=None, col_blocks_out=False):
    if dn == "nn":
        (m, k), n = a.shape, b.shape[1]
    elif dn == "nt":
        (m, k), n = a.shape, b.shape[0]
    else:
        (k, m), n = a.shape, b.shape[1]
    tm, tn, tk = min(tm, m), min(tn, n), min(tk, k)
    nk = k // tk
    grid = (m // tm, n // tn, nk)
    assert nk == 1 and (not sums or grid[1] == 1)
    a_spec = pl.BlockSpec((tk, tm), lambda i, j, q: (q, i)) if dn == "tn" else pl.BlockSpec((tm, tk), lambda i, j, q: (i, q))
    b_spec = pl.BlockSpec((tn, tk), lambda i, j, q: (j, q)) if dn == "nt" else pl.BlockSpec((tk, tn), lambda i, j, q: (q, j))
    o_spec = pl.BlockSpec((tm, tn), lambda i, j, q: (i, j))
    c_spec = pl.BlockSpec((1, tn), lambda i, j, q: (0, j))
    n_e, n_c, n_o, n_s, n_x = len(extras), len(consts), len(outs), len(sums), len(xch)
    x_specs, x_shapes, x_sems = _exchange_io(xch, xch_scatter) if n_x else ([], [], [])

    def body(*refs):
        a_ref, b_ref = refs[:2]
        e_refs = refs[2:2 + n_e + n_c]
        x_in = refs[2 + n_e + n_c:2 + n_e + n_c + n_x]
        rest = refs[2 + n_e + n_c + n_x:]
        o_refs, s_refs, x_out, scratch = rest[:n_o], rest[n_o:n_o + n_s], rest[n_o + n_s:n_o + n_s + n_x], rest[n_o + n_s + n_x:]
        step = (pl.program_id(0) * grid[1] + pl.program_id(1)) * nk + pl.program_id(2)
        if n_x:
            start, wait = _exchange_plan(x_in, x_out, xch_scatter, *scratch[len(scratch) - 3:])
            pl.when(step == 0)(start)
        a_blk = a_ref[...] if a_map is None else a_map(a_ref[...])
        acc = lax.dot_general(a_blk.astype(BF16), b_ref[...].astype(BF16), _dims(dn, 2), preferred_element_type=F32)
        vals = (acc,) if epilogue is None else epilogue(acc, *[e[...] for e in e_refs])
        for o_ref, val in zip(o_refs, vals[:n_o]):
            o_ref[...] = val.astype(o_ref.dtype)
        if n_s:
            @pl.when(step == 0)
            def _():
                for s_ref, val in zip(s_refs, vals[n_o:]):
                    s_ref[...] = val

            @pl.when(step > 0)
            def _():
                for s_ref, val in zip(s_refs, vals[n_o:]):
                    s_ref[...] += val

        if n_x:
            pl.when(step == grid[0] * grid[1] * nk - 1)(wait)

    plain = not (n_s or n_x)
    res = pl.pallas_call(
        body, name=name, grid=grid,
        in_specs=[a_spec, b_spec] + [o_spec] * n_e + [c_spec] * n_c + x_specs,
        out_specs=[pl.BlockSpec((None, tm, tn), lambda i, j, q: (j, i, 0)) if col_blocks_out else o_spec] * n_o
                  + [c_spec] * n_s + x_specs,
        out_shape=[jax.ShapeDtypeStruct((n // tn, m, tn) if col_blocks_out else (m, n), dt) for dt in outs] + [jax.ShapeDtypeStruct(s, F32) for s in sums] + x_shapes,
        scratch_shapes=x_sems,
        compiler_params=pltpu.CompilerParams(
            dimension_semantics=("parallel", "parallel", "arbitrary") if plain else ("arbitrary",) * 3,
            vmem_limit_bytes=VMEM_LIMIT, has_side_effects=bool(n_x)),
    )(a, b, *extras, *consts, *xch)
    return res[0] if len(res) == 1 else res


def _rms(h, g):
    return h * lax.rsqrt(jnp.mean(h * h, axis=-1, keepdims=True) + RMS_EPS) * g


def _rms_bwd(h, g, dy):
    rs = lax.rsqrt(jnp.mean(h * h, axis=-1, keepdims=True) + RMS_EPS)
    n = h * rs
    dn = dy * g
    dh = rs * (dn - n * jnp.mean(dn * n, axis=-1, keepdims=True))
    return dh, jnp.sum(dy * n, axis=0, keepdims=True)


def _rwkv_pre(k, xw, xa, xg, w0, a0, k_k, k_a, wl, al, gl):
    zw = w0 + _mm(jnp.tanh(xw), wl, "nn", 1)
    lw = -jnp.exp(-_softplus(-zw) - 0.5)
    iclr = _sigmoid(a0 + _mm(xa, al, "nn", 1))
    g = _mm(_sigmoid(xg), gl, "nn", 1)
    kk0 = k * k_k
    kk = kk0 * lax.rsqrt(jnp.maximum(_hsum(kk0 * kk0), L2_EPS * L2_EPS))
    k_h = k * (1.0 + (iclr - 1.0) * k_a)
    return lw, k_h, -kk, kk * iclr, g


def _rwkv_post(y, r, k_h, v, g, ln_g, ln_b, r_k):
    mu = _hsum(y) * (1.0 / HEAD_DIM)
    yc = y - mu
    var = _hsum(yc * yc) * (1.0 / HEAD_DIM)
    yo = yc * lax.rsqrt(var + GN_EPS) * ln_g + ln_b
    bonus = _hsum(r * k_h * r_k) * v
    return (yo + bonus) * g


def _shift_down(x, n):
    rows = lax.broadcasted_iota(jnp.int32, x.shape, 0)
    return jnp.where(rows < n, 0.0, pltpu.roll(x, n, 0))


def _shift_up(x, n):
    t_len = x.shape[0]
    rows = lax.broadcasted_iota(jnp.int32, x.shape, 0)
    return jnp.where(rows >= t_len - n, 0.0, pltpu.roll(x, t_len - n, 0))


def _exchange_plan(ins, outs, scatter, send_sems, recv_sems, local_sems):
    x, y, c = lax.axis_index("x"), lax.axis_index("y"), lax.axis_index("c")
    me = 4 * x + 2 * y + c

    def local(i):
        return pltpu.make_async_copy(ins[i].at[me] if scatter[i] else ins[i], outs[i].at[me], local_sems.at[i])

    def send(i, rel):
        return pltpu.make_async_remote_copy(
            src_ref=ins[i].at[me ^ rel] if scatter[i] else ins[i], dst_ref=outs[i].at[me],
            send_sem=send_sems.at[i, rel - 1], recv_sem=recv_sems.at[i, rel - 1],
            device_id=(x ^ (rel >> 2), y ^ ((rel >> 1) & 1), c ^ (rel & 1)), device_id_type=pl.DeviceIdType.MESH)

    def landed(i, rel):
        slot = outs[i].at[me ^ rel]
        return pltpu.make_async_remote_copy(
            src_ref=slot, dst_ref=slot, send_sem=send_sems.at[i, rel - 1], recv_sem=recv_sems.at[i, rel - 1],
            device_id=(x, y, c), device_id_type=pl.DeviceIdType.MESH)

    def start():
        for i in range(len(ins)):
            local(i).start()
            for rel in range(1, N_DEV):
                send(i, rel).start()

    def wait():
        for i in range(len(ins)):
            local(i).wait()
            for rel in range(1, N_DEV):
                landed(i, rel).wait_recv()
            for rel in range(1, N_DEV):
                send(i, rel).wait_send()

    return start, wait


def _gather_plan(ins, outs, send_sems, recv_sems, local_sems):
    x, y, c = lax.axis_index("x"), lax.axis_index("y"), lax.axis_index("c")
    me = 4 * x + 2 * y + c
    direct, chips = (1, 2, 4, 6), (2, 4, 6)

    def local(i):
        return pltpu.make_async_copy(ins[i], outs[i].at[me], local_sems.at[i])

    def send(i, rel):
        return pltpu.make_async_remote_copy(
            src_ref=ins[i], dst_ref=outs[i].at[me], send_sem=send_sems.at[i, rel - 1], recv_sem=recv_sems.at[i, rel - 1],
            device_id=(x ^ (rel >> 2), y ^ ((rel >> 1) & 1), c ^ (rel & 1)), device_id_type=pl.DeviceIdType.MESH)

    def passed(i, rel):
        slot = outs[i].at[me ^ rel]
        return pltpu.make_async_remote_copy(
            src_ref=slot, dst_ref=slot, send_sem=send_sems.at[i, rel], recv_sem=recv_sems.at[i, rel],
            device_id=(x, y, 1 - c), device_id_type=pl.DeviceIdType.MESH)

    def landed(i, rel):
        slot = outs[i].at[me ^ rel]
        return pltpu.make_async_remote_copy(
            src_ref=slot, dst_ref=slot, send_sem=send_sems.at[i, rel - 1], recv_sem=recv_sems.at[i, rel - 1],
            device_id=(x, y, c), device_id_type=pl.DeviceIdType.MESH)

    def start():
        for i in range(len(ins)):
            local(i).start()
            for rel in direct:
                send(i, rel).start()

    def forward(i):
        for rel in chips:
            landed(i, rel).wait_recv()
            passed(i, rel).start()

    def wait():
        for i in range(len(ins)):
            local(i).wait()
            for rel in (1, 3, 5, 7):
                landed(i, rel).wait_recv()
            for rel in direct:
                send(i, rel).wait_send()
            for rel in chips:
                passed(i, rel).wait_send()

    return start, forward, wait


def _exchange_io(arrays, scatter):
    n = len(arrays)
    any_spec = pl.BlockSpec(memory_space=pl.ANY)
    out_shape = [jax.ShapeDtypeStruct(a.shape if sc else (N_DEV,) + a.shape, a.dtype) for a, sc in zip(arrays, scatter)]
    sems = [pltpu.SemaphoreType.DMA((n, N_DEV - 1)), pltpu.SemaphoreType.DMA((n, N_DEV - 1)), pltpu.SemaphoreType.DMA((n,))]
    return [any_spec] * n, out_shape, sems


def _exchange(name, arrays, scatter):
    n = len(arrays)
    specs, out_shape, sems = _exchange_io(arrays, scatter)

    def body(*refs):
        if any(scatter):
            start, wait = _exchange_plan(refs[:n], refs[n:2 * n], scatter, *refs[2 * n:])
            start()
        else:
            start, forward, wait = _gather_plan(refs[:n], refs[n:2 * n], *refs[2 * n:])
            start()
            for i in range(n):
                forward(i)
        wait()

    return pl.pallas_call(
        body, name=name, in_specs=specs, out_specs=specs, out_shape=out_shape, scratch_shapes=sems,
        compiler_params=pltpu.CompilerParams(has_side_effects=True),
    )(*arrays)


def _scatter_start(name, arrays, lands):
    n = len(arrays)
    hbm = pl.BlockSpec(memory_space=pltpu.HBM)

    def body(*refs):
        ins, land, send_sems, recv_sems = refs[:n], refs[n:2 * n], refs[2 * n], refs[2 * n + 1]
        token = refs[4 * n + 2]
        x, y, c = lax.axis_index("x"), lax.axis_index("y"), lax.axis_index("c")
        me = 4 * x + 2 * y + c
        for i in range(n):
            for rel in range(1, N_DEV):
                k = i * (N_DEV - 1) + rel - 1
                pltpu.make_async_remote_copy(
                    src_ref=ins[i].at[me ^ rel], dst_ref=land[i].at[me], send_sem=send_sems.at[k],
                    recv_sem=recv_sems.at[k], device_id=(x ^ (rel >> 2), y ^ ((rel >> 1) & 1), c ^ (rel & 1)),
                    device_id_type=pl.DeviceIdType.MESH).start()
        token[...] = jnp.zeros_like(token)

    sem = pltpu.SemaphoreType.DMA((n * (N_DEV - 1),))
    bufs = [pltpu.HBM(a.shape, a.dtype) for a in list(arrays) + list(lands)]
    res = pl.pallas_call(
        body, name=name, out_shape=(sem, sem, *bufs, jax.ShapeDtypeStruct((8, LANE), F32)),
        in_specs=[hbm] * (2 * n),
        out_specs=(pl.BlockSpec(memory_space=pltpu.SEMAPHORE),) * 2 + (hbm,) * (2 * n) + (pl.BlockSpec(memory_space=pltpu.VMEM),),
        input_output_aliases={i: 2 + i for i in range(2 * n)},
        compiler_params=pltpu.CompilerParams(has_side_effects=pltpu.SideEffectType.DATAFLOW_SIDE_EFFECTING),
    )(*[pltpu.with_memory_space_constraint(a, pltpu.HBM) for a in list(arrays) + list(lands)])
    return res[0], res[1], res[2:2 + n], res[2 + n:2 + 2 * n], res[2 + 2 * n]


def _scatter_wait(name, send_sems, recv_sems, arrays, lands, after):
    n, n_after = len(arrays), len(after)
    hbm = pl.BlockSpec(memory_space=pltpu.HBM)

    def body(*refs):
        ins, land, s_sems, r_sems = refs[:n], refs[n:2 * n], refs[2 * n], refs[2 * n + 1]
        x, y, c = lax.axis_index("x"), lax.axis_index("y"), lax.axis_index("c")
        me = 4 * x + 2 * y + c
        for i in range(n):
            for rel in range(1, N_DEV):
                k = i * (N_DEV - 1) + rel - 1
                cp = pltpu.make_async_remote_copy(
                    src_ref=ins[i].at[me ^ rel], dst_ref=land[i].at[me ^ rel], send_sem=s_sems.at[k],
                    recv_sem=r_sems.at[k], device_id=(x, y, c), device_id_type=pl.DeviceIdType.MESH)
                cp.wait_send()
                cp.wait_recv()

    res = pl.pallas_call(
        body, name=name, out_shape=[pltpu.HBM(a.shape, a.dtype) for a in list(arrays) + list(lands)],
        in_specs=[hbm] * (2 * n) + [pl.BlockSpec(memory_space=pltpu.SEMAPHORE)] * 2 + [pl.BlockSpec(memory_space=pl.ANY)] * n_after,
        out_specs=[hbm] * (2 * n), input_output_aliases={i: i for i in range(2 * n)},
        compiler_params=pltpu.CompilerParams(has_side_effects=pltpu.SideEffectType.DATAFLOW_SIDE_EFFECTING),
    )(*arrays, *lands, send_sems, recv_sems, *after)
    return res[:n], res[n:]


def _tri_powers(low):
    powers, n, p = [low.astype(BF16)], 1, low
    while 2 * n < low.shape[-1]:
        p = _mm(p, p, "nn", REC_PASSES)
        powers.append(p.astype(BF16))
        n *= 2
    return powers


@jax.custom_vjp
def _tri_solve(low, rhs, powers):
    del low
    for p in powers:
        rhs = rhs + _mm(p, rhs, "nn", REC_PASSES)
    return rhs


def _tri_solve_fwd(low, rhs, powers):
    out = _tri_solve(low, rhs, powers)
    return out, (powers, out)


def _tri_solve_bwd(res, d):
    powers, u = res
    for p in powers:
        d = d + _mm(p, d, "tn", REC_PASSES)
    return _mm(d, u, "nt", REC_PASSES), d, [jnp.zeros_like(p) for p in powers]


_tri_solve.defvjp(_tri_solve_fwd, _tri_solve_bwd)


@jax.custom_vjp
def _tri_solve_given(low, rhs, powers, value):
    del low, rhs, powers
    return value


def _tri_solve_given_fwd(low, rhs, powers, value):
    return value, (powers, value)


def _tri_solve_given_bwd(res, d):
    return _tri_solve_bwd(res, d) + (jnp.zeros_like(res[1]),)


_tri_solve_given.defvjp(_tri_solve_given_fwd, _tri_solve_given_bwd)


def _heads(x):
    return jnp.stack([x[:, h * HEAD_DIM:(h + 1) * HEAD_DIM] for h in range(N_HEADS)])


def _unheads(x):
    return jnp.concatenate([x[h] for h in range(N_HEADS)], axis=-1)


def _causal_masks(c):
    ti = lax.broadcasted_iota(jnp.int32, (c, c), 0)
    si = lax.broadcasted_iota(jnp.int32, (c, c), 1)
    strict, incl = si < ti, si <= ti
    both = jnp.concatenate([jnp.concatenate([strict, strict], axis=1), jnp.concatenate([incl, incl], axis=1)], axis=0)
    return strict, incl, both


@jax.custom_vjp
def _gram_given(x2, y2, value):
    del x2, y2
    return value.astype(F32)


def _gram_given_fwd(x2, y2, value):
    return value.astype(F32), (x2, y2, value)


def _gram_given_bwd(res, d):
    x2, y2, value = res
    d = jnp.where(_causal_masks(d.shape[-1] // 2)[2], d, 0.0)
    return _mm(d, y2, "nn", 2), _mm(d, x2, "tn", 2), jnp.zeros_like(value)


_gram_given.defvjp(_gram_given_fwd, _gram_given_bwd)


def _chunk_fwd(z0, r, lw, k, v, a, b, powers=None, gram_value=None, u_value=None):
    c = r.shape[0]
    n_h, n_k = z0.shape[0], z0.shape[1]
    mm = functools.partial(_mm, passes=REC_PASSES)
    gram = functools.partial(_mm, passes=2)
    _, incl, mask = _causal_masks(c)
    cum = _mm(incl.astype(F32), lw, "nn", 3)
    cum_end = cum[c - 1:c, :]
    e_neg, e_end = jnp.exp(-cum), jnp.exp(cum_end - cum)
    x2 = jnp.concatenate([_heads(a * jnp.exp(cum - lw)), _heads(r * jnp.exp(cum))], axis=1)
    y2 = jnp.concatenate([_heads(b * e_neg), _heads(k * e_neg)], axis=1)
    vh = _heads(v)
    g2 = jnp.where(mask, gram(x2, y2, "nt"), 0.0) if gram_value is None else _gram_given(x2, y2, gram_value)
    t2 = mm(x2, z0, "nn") + mm(g2[:, :, c:], vh, "nn")
    low = g2[:, :c, :c]
    powers = _tri_powers(low) if powers is None else powers
    u = _tri_solve(low, t2[:, :c], powers) if u_value is None else _tri_solve_given(low, t2[:, :c], powers, u_value)
    y = t2[:, c:] + mm(g2[:, c:, :c], u, "nn")
    ki = lax.broadcasted_iota(jnp.int32, (n_k, n_k), 0)
    kj = lax.broadcasted_iota(jnp.int32, (n_k, n_k), 1)
    dmat = jnp.where(ki == kj, jnp.broadcast_to(_heads(jnp.exp(cum_end)), (n_h, n_k, n_k)), 0.0)
    z_end = mm(dmat, z0, "nn") + mm(jnp.concatenate([_heads(b * e_end), _heads(k * e_end)], axis=1),
                                    jnp.concatenate([u, vh], axis=1), "tn")
    return _unheads(y), z_end, powers, g2, u


def _rec_params():
    return pltpu.CompilerParams(dimension_semantics=("arbitrary",), vmem_limit_bytes=VMEM_LIMIT, has_side_effects=True)


def _rec_fwd(u, lw, k, a, b, xch):
    t_len = lw.shape[0]
    c = min(REC_CHUNK, t_len)
    nc = t_len // c
    per = REC_CHUNKS_PER_STEP if nc % REC_CHUNKS_PER_STEP == 0 else 1
    steps = nc // per
    n_x = len(xch)
    x_specs, x_shapes, x_sems = _exchange_io(xch, [False] * n_x)
    n_pow = max(1, (c - 1).bit_length())
    sizes = [a_.size * a_.dtype.itemsize for a_ in xch]
    pass_step = [min(steps - 1, int(0.9 * steps * sum(sizes[:j + 1]) / sum(sizes)) + 1) for j in range(n_x)]

    def body(*refs):
        r_ref, v_ref, lw_ref, k_ref, a_ref, b_ref = refs[:6]
        x_in = refs[6:6 + n_x]
        y_ref, zs_ref, pw_ref, gs_ref, us_ref = refs[6 + n_x:11 + n_x]
        x_out = refs[11 + n_x:11 + 2 * n_x]
        z_scr = refs[11 + 2 * n_x]
        start, forward, wait = _gather_plan(x_in, x_out, *refs[12 + 2 * n_x:])
        i = pl.program_id(0)

        @pl.when(i == 0)
        def _():
            start()
            z_scr[...] = jnp.zeros_like(z_scr)

        for s in range(per):
            rows = pl.ds(s * c, c)
            z0 = z_scr[...]
            zs_ref[s] = z0
            y, z_end, powers, g2, u_rows = _chunk_fwd(z0, r_ref[rows, :], lw_ref[rows, :], k_ref[rows, :], v_ref[rows, :],
                                                      a_ref[rows, :], b_ref[rows, :])
            y_ref[rows, :] = y
            z_scr[...] = z_end
            pw_ref[s] = jnp.concatenate(powers, axis=0)
            gs_ref[s] = g2.astype(BF16)
            us_ref[s] = u_rows

        for j in range(n_x):
            pl.when(i == pass_step[j])(functools.partial(forward, j))

        @pl.when(i == steps - 1)
        def _():
            wait()

    blk = lambda cb: pl.BlockSpec((per * c, RWKV_DIM), functools.partial(lambda i, q: (i, q), q=cb))
    res = pl.pallas_call(
        body, name="rwkv_rec_fwd", grid=(steps,),
        in_specs=[blk(0), blk(2)] + [blk(0)] * 4 + x_specs,
        out_specs=[blk(0), pl.BlockSpec((per, N_HEADS, HEAD_DIM, HEAD_DIM), lambda i: (i, 0, 0, 0)),
                   pl.BlockSpec((per, n_pow * N_HEADS, c, c), lambda i: (i, 0, 0, 0)),
                   pl.BlockSpec((per, N_HEADS, 2 * c, 2 * c), lambda i: (i, 0, 0, 0)),
                   pl.BlockSpec((per, N_HEADS, c, HEAD_DIM), lambda i: (i, 0, 0, 0))] + x_specs,
        out_shape=[jax.ShapeDtypeStruct((t_len, RWKV_DIM), F32),
                   jax.ShapeDtypeStruct((nc, N_HEADS, HEAD_DIM, HEAD_DIM), F32),
                   jax.ShapeDtypeStruct((nc, n_pow * N_HEADS, c, c), BF16),
                   jax.ShapeDtypeStruct((nc, N_HEADS, 2 * c, 2 * c), BF16),
                   jax.ShapeDtypeStruct((nc, N_HEADS, c, HEAD_DIM), F32)] + x_shapes,
        scratch_shapes=[pltpu.VMEM((N_HEADS, HEAD_DIM, HEAD_DIM), F32)] + x_sems,
        compiler_params=_rec_params(),
    )(u, u, lw, k, a, b, *xch)
    return res[0], res[1:5], res[5:]


def _rec_bwd(u, lw, k, a, b, saved, dy):
    t_len = lw.shape[0]
    c = min(REC_CHUNK, t_len)
    nc = t_len // c
    per = REC_CHUNKS_PER_STEP if nc % REC_CHUNKS_PER_STEP == 0 else 1
    steps = nc // per

    zs, pw, gs, us = saved

    def body(r_ref, v_ref, lw_ref, k_ref, a_ref, b_ref, zs_ref, pw_ref, gs_ref, us_ref, dy_ref, *rest):
        g_refs, dz_scr = rest[:6], rest[6]

        @pl.when(pl.program_id(0) == 0)
        def _():
            dz_scr[...] = jnp.zeros_like(dz_scr)

        for s in reversed(range(per)):
            rows = pl.ds(s * c, c)
            powers = [pw_ref[s, j * N_HEADS:(j + 1) * N_HEADS] for j in range(pw.shape[1] // N_HEADS)]
            chunk = functools.partial(lambda gram, u_val, pws, *xs: _chunk_fwd(*xs, powers=pws, gram_value=gram, u_value=u_val)[:2],
                                      gs_ref[s], us_ref[s], powers)
            _, vjp = jax.vjp(chunk, zs_ref[s], r_ref[rows, :], lw_ref[rows, :], k_ref[rows, :], v_ref[rows, :],
                             a_ref[rows, :], b_ref[rows, :])
            dz0, dr, dlw, dk, dv, da, db = vjp((dy_ref[rows, :], dz_scr[...]))
            for ref, val in zip(g_refs, (dr, dv, dlw, dk, da, db)):
                ref[rows, :] = val
            dz_scr[...] = dz0

    blk = lambda cb: pl.BlockSpec((per * c, RWKV_DIM), functools.partial(lambda i, q: (steps - 1 - i, q), q=cb))
    saved_blk = lambda arr: pl.BlockSpec((per,) + arr.shape[1:], lambda i: (steps - 1 - i, 0, 0, 0))
    return pl.pallas_call(
        body, name="rwkv_rec_bwd", grid=(steps,),
        in_specs=[blk(0), blk(2)] + [blk(0)] * 4 + [saved_blk(zs), saved_blk(pw), saved_blk(gs), saved_blk(us), blk(0)],
        out_specs=[blk(0)] * 6, out_shape=[jax.ShapeDtypeStruct((t_len, RWKV_DIM), F32)] * 6,
        scratch_shapes=[pltpu.VMEM((N_HEADS, HEAD_DIM, HEAD_DIM), F32)], compiler_params=_params(("arbitrary",)),
    )(u, u, lw, k, a, b, zs, pw, gs, us, dy)


_EARLY = ["w_in", "conv_w", "w_lora_up", "a_lora_up", "g_lora_up"]
_LATE = ["w_out", "w_up", "w_down", "w_ple_gate", "w_ple_proj"]
_SHARDED = _EARLY + _LATE
_COL_SHARDED = {"w_in", "conv_w", "w_lora_up", "a_lora_up", "g_lora_up", "w_up", "w_ple_proj"}
_BF16_GATHER = {"w_in", "w_out", "w_up", "w_down", "w_ple_gate", "w_ple_proj"}
_REPLICATED = ["norm_mix_g", "shift_mu", "w0", "a0", "k_k", "k_a", "r_k", "ln_x_g", "ln_x_b", "norm_mlp_g", "norm_ple_g",
               "norm_final_g"]
_WEIGHTS = ["norm_mix_g", "w_in", "conv_w", "shift_mu", "w_lora_up", "w0", "a_lora_up", "a0", "g_lora_up", "k_k", "k_a", "r_k",
            "ln_x_g", "ln_x_b", "w_out", "norm_mlp_g", "w_up", "w_down", "norm_ple_g", "w_ple_gate", "w_ple_proj", "norm_final_g"]


def _unshard(name, g):
    if name in _COL_SHARDED:
        return jnp.moveaxis(g, 0, 1).reshape(g.shape[1], N_DEV * g.shape[2])
    return g.reshape(N_DEV * g.shape[1], g.shape[2])


def _reshard(name, full):
    if name in _COL_SHARDED:
        return jnp.moveaxis(full.reshape(full.shape[0], N_DEV, full.shape[1] // N_DEV), 1, 0)
    return full.reshape(N_DEV, full.shape[0] // N_DEV, full.shape[1])


def _pad_in_cols(a):
    z = lambda n: jnp.zeros(a.shape[:-1] + (n,), a.dtype)
    conv = [a[..., part * CONV_DIM + j * LANE:part * CONV_DIM + (j + 1) * LANE] for j in range(CONV_DIM // LANE) for part in range(3)]
    return jnp.concatenate(conv + [a[..., CONV_COLS:3136], z(64), a[..., 3136:3200], z(64), a[..., 3200:3360], z(96)], axis=-1)


def _unpad_in_cols(a):
    conv = [a[..., (3 * j + part) * LANE:(3 * j + part + 1) * LANE] for part in range(3) for j in range(CONV_DIM // LANE)]
    return jnp.concatenate(conv + [a[..., CONV_COLS:3136], a[..., 3200:3264], a[..., 3328:3488]], axis=-1)


def _assemble_w_in(g):
    n_dev, rows, cols = g.shape

    def body(g_ref, o_ref):
        o_ref[...] = _pad_in_cols(jnp.concatenate([g_ref[d] for d in range(n_dev)], axis=1))

    return pl.pallas_call(
        body, name="w_in_assemble", grid=(rows // ROW_BLOCK,),
        in_specs=[pl.BlockSpec((n_dev, ROW_BLOCK, cols), lambda i: (0, i, 0))],
        out_specs=pl.BlockSpec((ROW_BLOCK, IN_PAD), lambda i: (i, 0)),
        out_shape=jax.ShapeDtypeStruct((rows, IN_PAD), g.dtype), compiler_params=_params(("arbitrary",)),
    )(g)


def _split_w_in_grad(dw):
    rows = dw.shape[0]
    cols = IN_COLS // N_DEV

    def body(d_ref, o_ref):
        full = _unpad_in_cols(d_ref[...])
        for d in range(N_DEV):
            o_ref[d] = full[:, cols * d:cols * (d + 1)]

    return pl.pallas_call(
        body, name="w_in_grad_split", grid=(rows // ROW_BLOCK,),
        in_specs=[pl.BlockSpec((ROW_BLOCK, IN_PAD), lambda i: (i, 0))],
        out_specs=pl.BlockSpec((N_DEV, ROW_BLOCK, cols), lambda i: (0, i, 0)),
        out_shape=jax.ShapeDtypeStruct((N_DEV, rows, cols), dw.dtype), compiler_params=_params(("arbitrary",)),
    )(dw)


def _pad_rows(a, rows):
    return jnp.concatenate([a, jnp.zeros((rows - a.shape[0],) + a.shape[1:], a.dtype)], axis=0)


SEG_W = [RWKV_DIM, RWKV_DIM, RWKV_DIM, LANE, LANE, 2 * LANE]
SEG_OFF = [0, 512, 1024, XW_OFF, XA_OFF, XG_OFF]


def _rwkv_pre_bwd(proj, u, grads, mu, small, dproj):
    t_len = u.shape[0]
    tr = min(ROW_BLOCK, t_len)
    nb = t_len // tr
    sub = 8
    n_g = len(grads)
    acc_shapes = [(1, RW_PAD)] + [(1, RWKV_DIM)] * 4 + [(LANE, RWKV_DIM), (LANE, RWKV_DIM), (2 * LANE, RWKV_DIM)]

    def body(*refs):
        seg_refs, halo_refs = refs[:6], refs[6:12]
        k_ref, xw_ref, xa_ref, xg_ref = refs[12:16]
        g_refs = refs[16:16 + n_g]
        mu_ref = refs[16 + n_g]
        prm_refs = refs[17 + n_g:24 + n_g]
        out_hbm = refs[25 + n_g]
        acc_refs = refs[26 + n_g:26 + n_g + len(acc_shapes)]
        vbuf, sems, carry = refs[26 + n_g + len(acc_shapes):]
        i = pl.program_id(0)
        blk = nb - 1 - i
        dr1, dr2, dv1, dv2, dlw, dk1, dk2, da, db, dg = [g[...] for g in g_refs]
        _, vjp = jax.vjp(_rwkv_pre, k_ref[...], xw_ref[...], xa_ref[...], xg_ref[...], *[p_[...] for p_ in prm_refs])
        dk, dxw, dxa, dxg, *dprm = vjp((dlw, dk1 + dk2, da, db, dg))
        du = jnp.concatenate([dr1 + dr2, dk, dv1 + dv2, dxw, dxa, dxg], axis=1)
        mu_v = mu_ref[...]

        @pl.when(i == 0)
        def _():
            carry[...] = jnp.zeros_like(carry)

        rows = lax.broadcasted_iota(jnp.int32, du.shape, 0)
        nxt = jnp.where(rows == tr - 1, carry[...], pltpu.roll(du, tr - 1, 0))
        d_rw = du - mu_v * du + mu_v * nxt
        d_mu = []
        for s_ref, h_ref, off, wd in zip(seg_refs, halo_refs, SEG_OFF, SEG_W):
            cur = s_ref[...]
            r0 = lax.broadcasted_iota(jnp.int32, cur.shape, 0)
            prev = jnp.where(r0 == 0, jnp.where(blk == 0, 0.0, h_ref[sub - 1:sub, :]), pltpu.roll(cur, 1, 0))
            d_mu.append(jnp.sum(du[:, off:off + wd] * (prev - cur), axis=0, keepdims=True))
        sums = [jnp.concatenate(d_mu, axis=1)] + list(dprm)

        @pl.when(i == 0)
        def _():
            for a_ref, val in zip(acc_refs, sums):
                a_ref[...] = val

        @pl.when(i > 0)
        def _():
            for a_ref, val in zip(acc_refs, sums):
                a_ref[...] += val

        carry[...] = du[0:1, :]
        slot = i % 2

        def writeback(s, b):
            return pltpu.make_async_copy(vbuf.at[s], out_hbm.at[pl.ds(b * tr, tr), pl.ds(CONV_COLS, RW_PAD)], sems.at[s])

        @pl.when(i >= 2)
        def _():
            writeback(slot, blk + 2).wait()

        vbuf[slot] = d_rw.astype(vbuf.dtype)
        writeback(slot, blk).start()

        @pl.when(i == nb - 1)
        def _():
            writeback(slot, blk).wait()
            if nb > 1:
                writeback(1 - slot, blk + 1).wait()

    rev = lambda w_, cb: pl.BlockSpec((tr, w_), functools.partial(lambda i, c: (nb - 1 - i, c), c=cb))
    halo = lambda w_, cb: pl.BlockSpec((sub, w_), functools.partial(
        lambda i, c: (jnp.maximum((nb - 1 - i) * (tr // sub) - 1, 0), c), c=cb))
    whole = lambda a: pl.BlockSpec(a.shape, functools.partial(lambda i, n: (0,) * n, n=a.ndim))
    segs = [(wd, (CONV_COLS + off) // wd) for off, wd in zip(SEG_OFF, SEG_W)]
    u_cols = [(512, 1), (LANE, XW_OFF // LANE), (LANE, XA_OFF // LANE), (2 * LANE, XG_OFF // (2 * LANE))]
    any_spec = pl.BlockSpec(memory_space=pl.ANY)
    res = pl.pallas_call(
        body, name="rwkv_pre_bwd", grid=(nb,),
        in_specs=[rev(*s) for s in segs] + [halo(*s) for s in segs] + [rev(*c) for c in u_cols]
                 + [rev(RWKV_DIM, 0)] * n_g + [whole(mu)] + [whole(p_) for p_ in small] + [any_spec],
        out_specs=[any_spec] + [pl.BlockSpec(s, functools.partial(lambda i, n: (0,) * n, n=len(s))) for s in acc_shapes],
        out_shape=[jax.ShapeDtypeStruct(dproj.shape, dproj.dtype)] + [jax.ShapeDtypeStruct(s, F32) for s in acc_shapes],
        scratch_shapes=[pltpu.VMEM((2, tr, RW_PAD), dproj.dtype), pltpu.SemaphoreType.DMA((2,)), pltpu.VMEM((1, RW_PAD), F32)],
        input_output_aliases={24 + n_g: 0},
        compiler_params=_params(("arbitrary",)),
    )(*[proj] * 12, *[u] * 4, *grads, mu, *small, dproj)
    return res


def _local_step(x, p, tgt, w, early_shards, late_shards):
    row = lambda v: v.reshape(1, -1)
    w = dict(w)

    xn1, *gathered = _rowwise("rms_mix", lambda h, g: (_rms(h, g),), [x], [w["norm_mix_g"]], [(D_MODEL, BF16)],
                              gather=early_shards)
    w.update({n: _unshard(n, g_) for n, g_ in zip(_EARLY[1:], gathered[1:])})
    w["w_in"] = _assemble_w_in(gathered[0])
    w["w_lora_up"] = _pad_rows(w["w_lora_up"], LANE)
    w["a_lora_up"] = _pad_rows(w["a_lora_up"], LANE)
    w["g_lora_up"] = _pad_rows(w["g_lora_up"], 2 * LANE)
    proj = _matmul("in_proj", xn1, w["w_in"], "nn", [F32], tm=2048, tn=512, tk=D_MODEL)
    n_cb = CONV_DIM // LANE

    def conv_fwd(blk, cw):
        gb, gc, hx = blk[:, :LANE], blk[:, LANE:2 * LANE], blk[:, 2 * LANE:]
        uu = gc * hx
        return (gb * (uu * cw[2:3] + _shift_down(uu, 1) * cw[1:2] + _shift_down(uu, 2) * cw[0:1]),)

    (y_conv,) = _colwise("conv_fwd", conv_fwd, n_cb, [(proj, 3 * LANE)], [w["conv_w"]], [(CONV_DIM, BF16, LANE)])

    small = [w["w0"], w["a0"], w["k_k"], w["k_a"], w["w_lora_up"], w["a_lora_up"], w["g_lora_up"]]
    def pre_fwd(*xs):
        cur, prev_rows, mu, prm = xs[:6], xs[6:12], xs[12], xs[13:]
        segs = []
        for c_, p_, off, wd in zip(cur, prev_rows, SEG_OFF, SEG_W):
            rows = lax.broadcasted_iota(jnp.int32, c_.shape, 0)
            prev = jnp.where(rows == 0, p_, pltpu.roll(c_, 1, 0))
            segs.append(c_ + mu[:, off:off + wd] * (prev - c_))
        return (jnp.concatenate(segs, axis=1),) + tuple(_rwkv_pre(segs[1], segs[3], segs[4], segs[5], *prm))

    proj_segs = [(proj, wd, (CONV_COLS + off) // wd) for off, wd in zip(SEG_OFF, SEG_W)]
    big = [n for n in _LATE if n in ("w_up", "w_down")]
    rest = [n for n in _LATE if n not in big]
    u, lw, k_h, ra, rb, g, *got_rest = _rowwise(
        "rwkv_pre", pre_fwd, proj_segs, [w["shift_mu"]] + small, [(RW_PAD, F32)] + [(RWKV_DIM, F32)] * 5, halo=True,
        gather=[late_shards[_LATE.index(n)] for n in rest])
    y_rec, rec_saved, got_big = _rec_fwd(u, lw, k_h, ra, rb, [late_shards[_LATE.index(n)] for n in big])
    for n, gathered in list(zip(rest, got_rest)) + list(zip(big, got_big)):
        w[n] = _unshard(n, gathered)
    post_c = [w["ln_x_g"], w["ln_x_b"], w["r_k"]]
    u_r, u_v = (u, 512, 0), (u, 512, 2)
    (y_rwkv,) = _rowwise("rwkv_post", lambda *xs: (_rwkv_post(*xs),), [y_rec, u_r, k_h, u_v, g], post_c, [(RWKV_DIM, BF16)],
                         tr=2 * ROW_BLOCK)
    ycat = jnp.concatenate([y_conv, y_rwkv], axis=1)
    def res_norm(acc, r_, g_):
        h = acc + r_
        return h, _rms(h, g_)

    h1, xn2 = _matmul("out_proj", ycat, w["w_out"], "nn", [F32, BF16], tm=1024, tn=D_MODEL, tk=D_MODEL, extras=[x],
                      consts=[w["norm_mlp_g"]], epilogue=res_norm)

    square = lambda h: h.astype(F32) * h.astype(F32)
    hid = _matmul("mlp_up", xn2, w["w_up"], "nn", [BF16], tm=2048, tn=1024, tk=D_MODEL,
                  epilogue=lambda acc: (jnp.maximum(acc, 0.0),))
    h2, xn3 = _matmul("mlp_down", hid, w["w_down"], "nn", [F32, BF16], tm=512, tn=D_MODEL, tk=D_FF, extras=[h1],
                      consts=[w["norm_ple_g"]], epilogue=res_norm, a_map=square)
    zg = _matmul("ple_gate", xn3, w["w_ple_gate"], "nn", [F32], tm=1024, tn=1024, tk=D_MODEL)
    pp = _matmul("ple_proj", p, w["w_ple_proj"], "nn", [F32], tm=1024, tn=1024, tk=PLE_DIM)

    def head(h2_, zg_, pp_, tg, gf):
        gate = _sigmoid(zg_)
        h3 = h2_ + gate * pp_
        out = _rms(h3, gf)
        err = out - tg
        dh3, dgf = _rms_bwd(h3, gf, err * (1.0 / D_MODEL))
        loss = jnp.sum(jnp.sum(err * err, axis=1, keepdims=True), axis=0, keepdims=True) * (0.5 / D_MODEL)
        return dh3, dh3 * pp_ * gate * (1.0 - gate), dh3 * gate, dgf, loss

    dh3, dzg, dpp, d_norm_final, loss = _rowwise(
        "head", head, [h2, zg, pp, tgt], [row(w["norm_final_g"])], [(D_MODEL, F32), (D_MODEL, BF16), (D_MODEL, BF16)],
        [(1, D_MODEL), (1, 1)], tr=2 * ROW_BLOCK)

    d_w_ple_proj = _matmul("d_ple_proj", p, dpp, "tn", [BF16], tm=PLE_DIM, tn=D_MODEL // N_DEV, tk=4096, col_blocks_out=True)
    d_w_ple_gate = _matmul("d_ple_gate", xn3, dzg, "tn", [BF16], tm=512, tn=1024, tk=4096)

    def norm_bwd(dxn, h, dres, g_):
        dh, dg = _rms_bwd(h, g_, dxn)
        dh = dh + dres
        return dh, dh, dg

    nb = dict(tm=512, tn=D_MODEL, epilogue=norm_bwd, sums=[(1, D_MODEL)])
    dh2, dh2_b, d_norm_ple = _matmul("dx_ple_gate", dzg, w["w_ple_gate"], "nt", [F32, BF16], tk=D_MODEL,
                                     extras=[h2, dh3], consts=[w["norm_ple_g"]], **nb)
    d_w_down = _matmul("d_mlp_down", hid, dh2_b, "tn", [BF16], tm=512, tn=1024, tk=4096, a_map=square)
    dpre = _matmul("dx_mlp_down", dh2_b, w["w_down"], "nt", [BF16], tm=2048, tn=1024, tk=D_MODEL, extras=[hid],
                   epilogue=lambda acc, hid_: (acc * (2.0 * hid_.astype(F32)),))
    d_w_up = _matmul("d_mlp_up", xn2, dpre, "tn", [BF16], tm=1024, tn=D_FF // N_DEV, tk=4096, col_blocks_out=True)
    dh1, dh1_b, d_norm_mlp = _matmul("dx_mlp_up", dpre, w["w_up"], "nt", [F32, BF16], tk=D_FF,
                                     extras=[h1, dh2], consts=[w["norm_mlp_g"]], **nb)
    d_w_out = _matmul("d_out_proj", ycat, dh1_b, "tn", [BF16], tm=512, tn=1024, tk=4096)
    dycat = _matmul("dx_out_proj", dh1_b, w["w_out"], "nt", [F32], tm=1024, tn=1024, tk=D_MODEL)
    late_grads = dict(w_out=d_w_out, w_up=d_w_up, w_down=d_w_down, w_ple_gate=d_w_ple_gate, w_ple_proj=d_w_ple_proj)
    late_send = [late_grads[n] if n in ("w_up", "w_ple_proj") else _reshard(n, late_grads[n]) for n in _LATE]
    *late_flight, late_token = _scatter_start("late_scatter_start", late_send, [lax.empty(a.shape, a.dtype) for a in late_send])
    conv_w_bwd = w["conv_w"] + late_token[0:1, 0:1]

    def conv_bwd(dy, blk, cw):
        gb, gc, hx = blk[:, :LANE], blk[:, LANE:2 * LANE], blk[:, 2 * LANE:]
        uu = gc * hx
        u1, u2 = _shift_down(uu, 1), _shift_down(uu, 2)
        dconv = dy * gb
        du = dconv * cw[2:3] + _shift_up(dconv, 1) * cw[1:2] + _shift_up(dconv, 2) * cw[0:1]
        s = lambda z: jnp.sum(z, axis=0, keepdims=True)
        d_blk = jnp.concatenate([dy * (uu * cw[2:3] + u1 * cw[1:2] + u2 * cw[0:1]), du * hx, du * gc], axis=1)
        return d_blk, s(dconv * u2), s(dconv * u1), s(dconv * uu)

    dproj, dcw0, dcw1, dcw2 = _colwise(
        "conv_bwd", conv_bwd, n_cb, [(dycat, LANE), (proj, 3 * LANE)], [conv_w_bwd],
        [(IN_PAD, BF16, 3 * LANE)], [(1, CONV_DIM)] * 3)

    def post_bwd(dy, y, r, k_h_, v, g_, ln_g, ln_b, r_k):
        _, vjp = jax.vjp(_rwkv_post, y, r, k_h_, v, g_, ln_g, ln_b, r_k)
        return vjp(dy)

    dy_rec, dr_p, dk_p, dv_p, dg, d_ln_g, d_ln_b, d_r_k = _rowwise(
        "rwkv_post_bwd", post_bwd, [(dycat, 512, 1), y_rec, u_r, k_h, u_v, g], post_c,
        [(RWKV_DIM, F32)] * 5, [(1, RWKV_DIM)] * 3)
    dr_r, dv_r, dlw, dk_r, da, db = _rec_bwd(u, lw, k_h, ra, rb, rec_saved, dy_rec)

    dproj, d_mu, d_w0, d_a0, d_k_k, d_k_a, d_wl, d_al, d_gl = _rwkv_pre_bwd(
        proj, u, [dr_p, dr_r, dv_p, dv_r, dlw, dk_p, dk_r, da, db, dg], w["shift_mu"], small, dproj)
    d_w_in = _matmul("d_in_proj", xn1, dproj, "tn", [BF16], tm=1024, tn=896, tk=4096)
    early_grads = dict(conv_w=jnp.concatenate([dcw0, dcw1, dcw2], axis=0),
                       w_lora_up=d_wl[:64], a_lora_up=d_al[:64], g_lora_up=d_gl[:160])
    early_send = [_split_w_in_grad(d_w_in)] + [_reshard(n, early_grads[n]) for n in _EARLY[1:]]
    *early_flight, token = _scatter_start("early_scatter_start", early_send, [lax.empty(a.shape, a.dtype) for a in early_send])
    dx, d_norm_mix = _matmul(
        "dx_in_proj", dproj, w["w_in"], "nt", [F32], tk=IN_PAD, extras=[x, dh1], consts=[w["norm_mix_g"] + token[0:1, 0:1]],
        **dict(nb, epilogue=lambda *a: norm_bwd(*a)[1:]))

    grads = dict(
        norm_mix_g=d_norm_mix, shift_mu=d_mu, w0=d_w0, a0=d_a0, k_k=d_k_k, k_a=d_k_a, r_k=d_r_k,
        ln_x_g=d_ln_g, ln_x_b=d_ln_b, norm_mlp_g=d_norm_mlp, norm_ple_g=d_norm_ple, norm_final_g=d_norm_final)
    return loss, dx, grads, late_flight, early_flight, d_w_in


def _adam_update(partials, w_ref, m_ref, v_ref, g_ref, d_ref, nm_ref, nv_ref):
    g = partials[0].astype(F32)
    for part in partials[1:]:
        g = g + part.astype(F32)
    nm =ADAM_B1 * m_ref[...] + (1.0 - ADAM_B1) * g
    nv = ADAM_B2 * v_ref[...] + (1.0 - ADAM_B2) * (g * g)
    m_hat = nm / (1.0 - ADAM_B1 ** ADAM_STEP)
    v_hat = nv / (1.0 - ADAM_B2 ** ADAM_STEP)
    g_ref[...] = g
    d_ref[...] = -ADAM_LR * (m_hat / (jnp.sqrt(v_hat) + ADAM_EPS) + ADAM_WD * w_ref[...])
    nm_ref[...] = nm
    nv_ref[...] = nv


SMALL_ROWS = 8


def _small_layout(widths):
    widths = list(widths) + [1]
    fill, place = [0] * SMALL_ROWS, [None] * len(widths)
    for j in sorted(range(len(widths)), key=lambda q: -widths[q]):
        row = fill.index(min(fill))
        place[j] = (row, fill[row])
        fill[row] += -(-widths[j] // LANE) * LANE
    return place, max(fill)


def _pack_small(vecs, loss):
    place, total = _small_layout([v_.shape[1] for v_ in vecs])
    n = len(vecs)

    def body(*refs):
        out = jnp.zeros((SMALL_ROWS, total), F32)
        row_id = lax.broadcasted_iota(jnp.int32, (SMALL_ROWS, total), 0)
        for row in range(SMALL_ROWS):
            mine = sorted((off, j) for j, (r_, off) in enumerate(place) if r_ == row)
            pieces, at = [], 0
            for off, j in mine:
                val = refs[j][...]
                pieces.append(val)
                at = off + val.shape[1]
                pad = -val.shape[1] % LANE
                if pad:
                    pieces.append(jnp.zeros((1, pad), F32))
                    at += pad
            if total > at:
                pieces.append(jnp.zeros((1, total - at), F32))
            out = jnp.where(row_id == row, jnp.broadcast_to(jnp.concatenate(pieces, axis=1), (SMALL_ROWS, total)), out)
        refs[n + 1][...] = out

    return pl.pallas_call(body, name="pack_small", out_shape=jax.ShapeDtypeStruct((SMALL_ROWS, total), F32))(*vecs, loss)


def _adamw_small(packed, ws, ms, vs):
    n = len(ws)
    place, _ = _small_layout([w_.shape[1] for w_ in ws])

    def body(p_ref, *refs):
        w_refs, m_refs, v_refs, outs = refs[:n], refs[n:2 * n], refs[2 * n:3 * n], refs[3 * n:]
        for j in range(n):
            row, off = place[j]
            cols = pl.ds(off, ws[j].shape[1])
            _adam_update([p_ref[s, row:row + 1, cols] for s in range(N_DEV)], w_refs[j], m_refs[j], v_refs[j],
                         *outs[4 * j:4 * j + 4])
        row, off = place[n]
        total = p_ref[0, row:row + 1, off:off + 1]
        for s in range(1, N_DEV):
            total = total + p_ref[s, row:row + 1, off:off + 1]
        outs[4 * n][...] = total

    res = pl.pallas_call(
        body, name="adamw_small",
        out_shape=[jax.ShapeDtypeStruct(w_.shape, F32) for w_ in ws for _ in range(4)] + [jax.ShapeDtypeStruct((1, 1), F32)],
    )(packed, *ws, *ms, *vs)
    return [res[4 * j:4 * j + 4] for j in range(n)], res[4 * n]


def _adamw(name, parts, w, m, v, own=None, me=None):
    rows, cols = w.shape[-2:]
    lead = w.ndim - 2
    tr = rows if rows * cols * 4 * 8 <= (4 << 20) else max(8, (4 << 20) // (cols * 4 * 8) // 8 * 8)
    while rows % tr:
        tr -= 8
    shape4 = [jax.ShapeDtypeStruct(w.shape, F32)] * 4
    if own is None:
        def body(p_ref, *refs):
            _adam_update([p_ref[s] for s in range(N_DEV)], *refs)

        blk = pl.BlockSpec((None,) * lead + (tr, cols), lambda i: (0,) * lead + (i, 0))
        return pl.pallas_call(
            body, name=name, grid=(rows // tr,),
            in_specs=[pl.BlockSpec((N_DEV, tr, cols), lambda i: (0, i, 0)), blk, blk, blk], out_specs=[blk] * 4,
            out_shape=shape4, compiler_params=_params(("arbitrary",)),
        )(parts, w, m, v)

    def body_own(me_ref, p_ref, own_ref, *refs):
        mine = own_ref[...]
        _adam_update([jnp.where(me_ref[0] == s, mine, p_ref[s]) for s in range(N_DEV)], *refs)

    blk = pl.BlockSpec((None,) * lead + (tr, cols), lambda i, me_ref: (0,) * lead + (i, 0))
    return pl.pallas_call(
        body_own, name=name, out_shape=shape4,
        grid_spec=pltpu.PrefetchScalarGridSpec(
            num_scalar_prefetch=1, grid=(rows // tr,),
            in_specs=[pl.BlockSpec((N_DEV, tr, cols), lambda i, me_ref: (0, i, 0)),
                      pl.BlockSpec((None, tr, cols), lambda i, me_ref: (me_ref[0], i, 0)), blk, blk, blk],
            out_specs=[blk] * 4),
        compiler_params=_params(("arbitrary",)),
    )(me, parts, own, w, m, v)


def kernel(x, p, norm_mix_g, w_in, conv_w, shift_mu, w_lora_up, w0, a_lora_up, a0, g_lora_up, k_k, k_a, r_k, ln_x_g, ln_x_b, w_out, norm_mlp_g, w_up, w_down, norm_ple_g, w_ple_gate, w_ple_proj, norm_final_g, loss_target, m_norm_mix_g, m_w_in, m_conv_w, m_shift_mu, m_w_lora_up, m_w0, m_a_lora_up, m_a0, m_g_lora_up, m_k_k, m_k_a, m_r_k, m_ln_x_g, m_ln_x_b, m_w_out, m_norm_mlp_g, m_w_up, m_w_down, m_norm_ple_g, m_w_ple_gate, m_w_ple_proj, m_norm_final_g, v_norm_mix_g, v_w_in, v_conv_w, v_shift_mu, v_w_lora_up, v_w0, v_a_lora_up, v_a0, v_g_lora_up, v_k_k, v_k_a, v_r_k, v_ln_x_g, v_ln_x_b, v_w_out, v_norm_mlp_g, v_w_up, v_w_down, v_norm_ple_g, v_w_ple_gate, v_w_ple_proj, v_norm_final_g):
    args = dict(locals())
    wts = {n: args[n] for n in _WEIGHTS}
    mom = {n: args["m_" + n] for n in _WEIGHTS}
    var = {n: args["v_" + n] for n in _WEIGHTS}
    shard2d = lambda a: a.reshape(a.shape[-2:])
    pad_mu = lambda a: _pad_in_cols(jnp.concatenate([jnp.zeros((1, CONV_COLS), F32), a], axis=1))[:, CONV_COLS:]
    unpad_mu = lambda a: _unpad_in_cols(jnp.concatenate([jnp.zeros((1, CONV_COLS), F32), a], axis=1))[:, CONV_COLS:]

    shards = {n: shard2d(wts[n]).astype(BF16 if n in _BF16_GATHER else F32) for n in _SHARDED}
    w = {n: wts[n].reshape(1, -1) for n in _REPLICATED}
    w["shift_mu"] = pad_mu(wts["shift_mu"])

    loss, dx, grads, late_flight, early_flight, d_w_in = _local_step(
        x[0], p[0, 0], loss_target[0], w, [shards[n] for n in _EARLY], [shards[n] for n in _LATE])

    me = (4 * lax.axis_index("x") + 2 * lax.axis_index("y") + lax.axis_index("c")).astype(jnp.int32).reshape(1)
    late_sent, late_parts = _scatter_wait("late_scatter_wait", *late_flight, after=[d_w_in])
    out = {n: _adamw("adamw_" + n, prt, wts[n], mom[n], var[n], own=own, me=me)
           for n, prt, own in zip(_LATE, late_parts, late_sent)}
    early_sent, early_parts = _scatter_wait("early_scatter_wait", *early_flight, after=[dx] + [out[n][1] for n in _LATE])
    for n, prt, own in zip(_EARLY, early_parts, early_sent):
        out[n] = _adamw("adamw_" + n, prt, wts[n], mom[n], var[n], own=own, me=me)

    grads["shift_mu"] = unpad_mu(grads["shift_mu"])
    flat = lambda a: a.reshape(1, -1)
    (small_parts,) = _exchange("gather_small", [_pack_small([flat(grads[n]) for n in _REPLICATED], loss)], [False])
    small, loss_total = _adamw_small(small_parts, *[[flat(d[n]) for n in _REPLICATED] for d in (wts, mom, var)])
    for n, res in zip(_REPLICATED, small):
        out[n] = [r.reshape(wts[n].shape) for r in res]
    return (loss_total[0, 0], dx[None], *[out[n][0] for n in _WEIGHTS], *[out[n][1] for n in _WEIGHTS],
            *[out[n][2] for n in _WEIGHTS], *[out[n][3] for n in _WEIGHTS])
```

```python
import functools

import jax
import jax.numpy as jnp
from jax import lax
from jax.experimental import pallas as pl
from jax.experimental.pallas import tpu as pltpu

F32 = jnp.float32
BF16 = jnp.bfloat16

N_DEV = 8
D_MODEL = 1024
CONV_DIM = 512
RWKV_DIM = 512
HEAD_DIM = 64
N_HEADS = 8
D_FF = 4096
PLE_DIM = 256
RMS_EPS = 1e-6
GN_EPS = 64e-5
L2_EPS = 1e-12
ADAM_LR, ADAM_B1, ADAM_B2, ADAM_EPS, ADAM_WD, ADAM_STEP = 0.001, 0.9, 0.999, 1e-08, 0.01, 10

CONV_COLS = 3 * CONV_DIM
RW_PAD = 2048
IN_PAD = CONV_COLS + RW_PAD
IN_COLS = 3360
XW_OFF, XA_OFF, XG_OFF = 1536, 1664, 1792
REC_CHUNK = 128
REC_CHUNKS_PER_STEP = 2
REC_PASSES = 1
ROW_BLOCK = 256
LANE = 128
VMEM_LIMIT = 56 * 1024 * 1024


def _dims(dn, ndim):
    if ndim == 3:
        return {"nn": (((2,), (1,)), ((0,), (0,))), "nt": (((2,), (2,)), ((0,), (0,))),
                "tn": (((1,), (1,)), ((0,), (0,)))}[dn]
    return {"nn": (((1,), (0,)), ((), ())), "nt": (((1,), (1,)), ((), ())), "tn": (((0,), (0,)), ((), ()))}[dn]


def _split2(x):
    hi = x.astype(BF16)
    return hi, (x - hi.astype(F32)).astype(BF16)


def _mm_raw(x, y, dn, passes):
    f = lambda p, q: lax.dot_general(p, q, _dims(dn, x.ndim), preferred_element_type=F32)
    if passes == 1:
        return f(x.astype(BF16), y.astype(BF16))
    xh, xl = _split2(x)
    yh, yl = _split2(y)
    if passes == 2:
        return f(xh, yh) + f(xh, yl)
    return f(xh, yh) + f(xh, yl) + f(xl, yh)


@functools.partial(jax.custom_vjp, nondiff_argnums=(2, 3))
def _mm(x, y, dn, passes):
    return _mm_raw(x, y, dn, passes)


def _mm_fwd(x, y, dn, passes):
    return _mm_raw(x, y, dn, passes), (x, y)


def _mm_bwd(dn, passes, res, d):
    x, y = res
    if dn == "nn":
        return _mm(d, y, "nt", passes), _mm(x, d, "tn", passes)
    if dn == "nt":
        return _mm(d, y, "nn", passes), _mm(d, x, "tn", passes)
    return _mm(y, d, "nt", passes), _mm(x, d, "nn", passes)


_mm.defvjp(_mm_fwd, _mm_bwd)


def _head_ones():
    i = lax.broadcasted_iota(jnp.int32, (RWKV_DIM, RWKV_DIM), 0) // HEAD_DIM
    j = lax.broadcasted_iota(jnp.int32, (RWKV_DIM, RWKV_DIM), 1) // HEAD_DIM
    return (i == j).astype(BF16)


def _hsum_raw(x):
    ones = _head_ones()
    f = lambda p: lax.dot_general(p, ones, _dims("nn", 2), preferred_element_type=F32)
    x1, x2 = _split2(x)
    return f(x1) + f(x2)


@jax.custom_vjp
def _hsum(x):
    return _hsum_raw(x)


_hsum.defvjp(lambda x: (_hsum_raw(x), None), lambda _, d: (_hsum(d),))


def _sigmoid(x):
    return 0.5 + 0.5 * jnp.tanh(0.5 * x)


def _softplus(x):
    return jnp.maximum(x, 0.0) + jnp.log(1.0 + jnp.exp(-jnp.abs(x)))


def _params(sem):
    return pltpu.CompilerParams(dimension_semantics=sem, vmem_limit_bytes=VMEM_LIMIT)


def _rowwise(name, fn, rows, consts, row_outs, acc_outs=(), tr=ROW_BLOCK, halo=False, gather=()):
    rows = [r if isinstance(r, tuple) else (r, r.shape[1], 0) for r in rows]
    t_len = rows[0][0].shape[0]
    tr = min(tr, t_len)
    n_r, n_c, n_o, n_a, n_x = len(rows), len(consts), len(row_outs), len(acc_outs), len(gather)
    n_h = n_r if halo else 0
    sub = 8
    x_specs, x_shapes, x_sems = _exchange_io(gather, [False] * n_x) if n_x else ([], [], [])
    nb = t_len // tr

    def body(*refs):
        if n_x:
            n_in = n_r + n_h + n_c
            start, forward, wait = _gather_plan(refs[n_in:n_in + n_x], refs[len(refs) - 3 - n_x:len(refs) - 3], *refs[len(refs) - 3:])
            pl.when(pl.program_id(0) == 0)(start)
            refs = refs[:n_in] + refs[n_in + n_x:len(refs) - 3 - n_x]
        ins = [r[...] for r in refs[:n_r]]
        ins += [jnp.where(pl.program_id(0) == 0, 0.0, r[sub - 1:sub, :]) for r in refs[n_r:n_r + n_h]]
        ins += [r[...] for r in refs[n_r + n_h:n_r + n_h + n_c]]
        refs = refs[:n_r] + refs[n_r + n_h:]
        outs = fn(*ins)
        o_refs = refs[n_r + n_c:n_r + n_c + n_o]
        a_refs = refs[n_r + n_c + n_o:]
        for o_ref, val in zip(o_refs, outs[:n_o]):
            o_ref[...] = val.astype(o_ref.dtype)
        if n_a:
            first = pl.program_id(0) == 0

            @pl.when(first)
            def _():
                for a_ref, val in zip(a_refs, outs[n_o:]):
                    a_ref[...] = val

            @pl.when(jnp.logical_not(first))
            def _():
                for a_ref, val in zip(a_refs, outs[n_o:]):
                    a_ref[...] += val

        if n_x:
            @pl.when(pl.program_id(0) == nb - 1)
            def _():
                for j in range(n_x):
                    forward(j)
                wait()

    in_specs = [pl.BlockSpec((tr, w), functools.partial(lambda i, c: (i, c), c=cb)) for _, w, cb in rows]
    if halo:
        in_specs += [pl.BlockSpec((sub, w), functools.partial(lambda i, c: (jnp.maximum(i * (tr // sub) - 1, 0), c), c=cb))
                     for _, w, cb in rows]
    in_specs += [pl.BlockSpec(c.shape, functools.partial(lambda i, n: (0,) * n, n=c.ndim)) for c in consts]
    out_specs = [pl.BlockSpec((tr, w), lambda i: (i, 0)) for w, _ in row_outs]
    out_specs += [pl.BlockSpec(s, functools.partial(lambda i, n: (0,) * n, n=len(s))) for s in acc_outs]
    out_shape = [jax.ShapeDtypeStruct((t_len, w), dt) for w, dt in row_outs]
    out_shape += [jax.ShapeDtypeStruct(s, F32) for s in acc_outs]
    return pl.pallas_call(
        body, name=name, grid=(nb,), in_specs=in_specs + x_specs, out_specs=out_specs + x_specs,
        out_shape=out_shape + x_shapes, scratch_shapes=x_sems,
        compiler_params=pltpu.CompilerParams(dimension_semantics=("arbitrary",), vmem_limit_bytes=VMEM_LIMIT,
                                             has_side_effects=bool(n_x)),
    )(*[r[0] for r in rows], *([r[0] for r in rows] if halo else []), *consts, *gather)


def _colwise(name, fn, n_blocks, cols, prms, col_outs, prm_outs=()):
    t_len = cols[0][0].shape[0]
    n_i = len(cols) + len(prms)

    def body(*refs):
        outs = fn(*[r[...] for r in refs[:n_i]])
        for o_ref, val in zip(refs[n_i:], outs):
            o_ref[...] = val.astype(o_ref.dtype)

    spec = lambda r, w: pl.BlockSpec((r, w), lambda j: (0, j))
    in_specs = [spec(t_len, w) for _, w in cols] + [spec(a.shape[0], LANE) for a in prms]
    out_specs = [spec(t_len, bw) for _, _, bw in col_outs] + [spec(r, LANE) for r, _ in prm_outs]
    out_shape = [jax.ShapeDtypeStruct((t_len, w), dt) for w, dt, _ in col_outs]
    out_shape += [jax.ShapeDtypeStruct((r, w), F32) for r, w in prm_outs]
    return pl.pallas_call(
        body, name=name, grid=(n_blocks,), in_specs=in_specs, out_specs=out_specs, out_shape=out_shape,
        compiler_params=_params(("arbitrary",)),
    )(*[c[0] for c in cols], *prms)


def _matmul(name, a, b, dn, outs, *, tm, tn, tk, extras=(), consts=(), epilogue=None, sums=(), xch=(), xch_scatter=(),
            a_map=None, col_blocks_out=False):
    if dn == "nn":
        (m, k), n = a.shape, b.shape[1]
    elif dn == "nt":
        (m, k), n = a.shape, b.shape[0]
    else:
        (k, m), n = a.shape, b.shape[1]
    tm, tn, tk = min(tm, m), min(tn, n), min(tk, k)
    nk = k // tk
    grid = (m // tm, n // tn, nk)
    assert nk == 1 and (not sums or grid[1] == 1)
    a_spec = pl.BlockSpec((tk, tm), lambda i, j, q: (q, i)) if dn == "tn" else pl.BlockSpec((tm, tk), lambda i, j, q: (i, q))
    b_spec = pl.BlockSpec((tn, tk), lambda i, j, q: (j, q)) if dn == "nt" else pl.BlockSpec((tk, tn), lambda i, j, q: (q, j))
    o_spec = pl.BlockSpec((tm, tn), lambda i, j, q: (i, j))
    c_spec = pl.BlockSpec((1, tn), lambda i, j, q: (0, j))
    n_e, n_c, n_o, n_s, n_x = len(extras), len(consts), len(outs), len(sums), len(xch)
    x_specs, x_shapes, x_sems = _exchange_io(xch, xch_scatter) if n_x else ([], [], [])

    def body(*refs):
        a_ref, b_ref = refs[:2]
        e_refs = refs[2:2 + n_e + n_c]
        x_in = refs[2 + n_e + n_c:2 + n_e + n_c + n_x]
        rest = refs[2 + n_e + n_c + n_x:]
        o_refs, s_refs, x_out, scratch = rest[:n_o], rest[n_o:n_o + n_s], rest[n_o + n_s:n_o + n_s + n_x], rest[n_o + n_s + n_x:]
        step = (pl.program_id(0) * grid[1] + pl.program_id(1)) * nk + pl.program_id(2)
        if n_x:
            start, wait = _exchange_plan(x_in, x_out, xch_scatter, *scratch[len(scratch) - 3:])
            pl.when(step == 0)(start)
        a_blk = a_ref[...] if a_map is None else a_map(a_ref[...])
        acc = lax.dot_general(a_blk.astype(BF16), b_ref[...].astype(BF16), _dims(dn, 2), preferred_element_type=F32)
        vals = (acc,) if epilogue is None else epilogue(acc, *[e[...] for e in e_refs])
        for o_ref, val in zip(o_refs, vals[:n_o]):
            o_ref[...] = val.astype(o_ref.dtype)
        if n_s:
            @pl.when(step == 0)
            def _():
                for s_ref, val in zip(s_refs, vals[n_o:]):
                    s_ref[...] = val

            @pl.when(step > 0)
            def _():
                for s_ref, val in zip(s_refs, vals[n_o:]):
                    s_ref[...] += val

        if n_x:
            pl.when(step == grid[0] * grid[1] * nk - 1)(wait)

    plain = not (n_s or n_x)
    res = pl.pallas_call(
        body, name=name, grid=grid,
        in_specs=[a_spec, b_spec] + [o_spec] * n_e + [c_spec] * n_c + x_specs,
        out_specs=[pl.BlockSpec((None, tm, tn), lambda i, j, q: (j, i, 0)) if col_blocks_out else o_spec] * n_o
                  + [c_spec] * n_s + x_specs,
        out_shape=[jax.ShapeDtypeStruct((n // tn, m, tn) if col_blocks_out else (m, n), dt) for dt in outs] + [jax.ShapeDtypeStruct(s, F32) for s in sums] + x_shapes,
        scratch_shapes=x_sems,
        compiler_params=pltpu.CompilerParams(
            dimension_semantics=("parallel", "parallel", "arbitrary") if plain else ("arbitrary",) * 3,
            vmem_limit_bytes=VMEM_LIMIT, has_side_effects=bool(n_x)),
    )(a, b, *extras, *consts, *xch)
    return res[0] if len(res) == 1 else res


def _rms(h, g):
    return h * lax.rsqrt(jnp.mean(h * h, axis=-1, keepdims=True) + RMS_EPS) * g


def _rms_bwd(h, g, dy):
    rs = lax.rsqrt(jnp.mean(h * h, axis=-1, keepdims=True) + RMS_EPS)
    n = h * rs
    dn = dy * g
    dh = rs * (dn - n * jnp.mean(dn * n, axis=-1, keepdims=True))
    return dh, jnp.sum(dy * n, axis=0, keepdims=True)


def _rwkv_pre(k, xw, xa, xg, w0, a0, k_k, k_a, wl, al, gl):
    zw = w0 + _mm(jnp.tanh(xw), wl, "nn", 1)
    lw = -jnp.exp(-_softplus(-zw) - 0.5)
    iclr = _sigmoid(a0 + _mm(xa, al, "nn", 1))
    g = _mm(_sigmoid(xg), gl, "nn", 1)
    kk0 = k * k_k
    kk = kk0 * lax.rsqrt(jnp.maximum(_hsum(kk0 * kk0), L2_EPS * L2_EPS))
    k_h = k * (1.0 + (iclr - 1.0) * k_a)
    return lw, k_h, -kk, kk * iclr, g


def _rwkv_post(y, r, k_h, v, g, ln_g, ln_b, r_k):
    mu = _hsum(y) * (1.0 / HEAD_DIM)
    yc = y - mu
    var = _hsum(yc * yc) * (1.0 / HEAD_DIM)
    yo = yc * lax.rsqrt(var + GN_EPS) * ln_g + ln_b
    bonus = _hsum(r * k_h * r_k) * v
    return (yo + bonus) * g


def _shift_down(x, n):
    rows = lax.broadcasted_iota(jnp.int32, x.shape, 0)
    return jnp.where(rows < n, 0.0, pltpu.roll(x, n, 0))


def _shift_up(x, n):
    t_len = x.shape[0]
    rows = lax.broadcasted_iota(jnp.int32, x.shape, 0)
    return jnp.where(rows >= t_len - n, 0.0, pltpu.roll(x, t_len - n, 0))


def _exchange_plan(ins, outs, scatter, send_sems, recv_sems, local_sems):
    x, y, c = lax.axis_index("x"), lax.axis_index("y"), lax.axis_index("c")
    me = 4 * x + 2 * y + c

    def local(i):
        return pltpu.make_async_copy(ins[i].at[me] if scatter[i] else ins[i], outs[i].at[me], local_sems.at[i])

    def send(i, rel):
        return pltpu.make_async_remote_copy(
            src_ref=ins[i].at[me ^ rel] if scatter[i] else ins[i], dst_ref=outs[i].at[me],
            send_sem=send_sems.at[i, rel - 1], recv_sem=recv_sems.at[i, rel - 1],
            device_id=(x ^ (rel >> 2), y ^ ((rel >> 1) & 1), c ^ (rel & 1)), device_id_type=pl.DeviceIdType.MESH)

    def landed(i, rel):
        slot = outs[i].at[me ^ rel]
        return pltpu.make_async_remote_copy(
            src_ref=slot, dst_ref=slot, send_sem=send_sems.at[i, rel - 1], recv_sem=recv_sems.at[i, rel - 1],
            device_id=(x, y, c), device_id_type=pl.DeviceIdType.MESH)

    def start():
        for i in range(len(ins)):
            local(i).start()
            for rel in range(1, N_DEV):
                send(i, rel).start()

    def wait():
        for i in range(len(ins)):
            local(i).wait()
            for rel in range(1, N_DEV):
                landed(i, rel).wait_recv()
            for rel in range(1, N_DEV):
                send(i, rel).wait_send()

    return start, wait


def _gather_plan(ins, outs, send_sems, recv_sems, local_sems):
    x, y, c = lax.axis_index("x"), lax.axis_index("y"), lax.axis_index("c")
    me = 4 * x + 2 * y + c
    direct, chips = (1, 2, 4, 6), (2, 4, 6)

    def local(i):
        return pltpu.make_async_copy(ins[i], outs[i].at[me], local_sems.at[i])

    def send(i, rel):
        return pltpu.make_async_remote_copy(
            src_ref=ins[i], dst_ref=outs[i].at[me], send_sem=send_sems.at[i, rel - 1], recv_sem=recv_sems.at[i, rel - 1],
            device_id=(x ^ (rel >> 2), y ^ ((rel >> 1) & 1), c ^ (rel & 1)), device_id_type=pl.DeviceIdType.MESH)

    def passed(i, rel):
        slot = outs[i].at[me ^ rel]
        return pltpu.make_async_remote_copy(
            src_ref=slot, dst_ref=slot, send_sem=send_sems.at[i, rel], recv_sem=recv_sems.at[i, rel],
            device_id=(x, y, 1 - c), device_id_type=pl.DeviceIdType.MESH)

    def landed(i, rel):
        slot = outs[i].at[me ^ rel]
        return pltpu.make_async_remote_copy(
            src_ref=slot, dst_ref=slot, send_sem=send_sems.at[i, rel - 1], recv_sem=recv_sems.at[i, rel - 1],
            device_id=(x, y, c), device_id_type=pl.DeviceIdType.MESH)

    def start():
        for i in range(len(ins)):
            local(i).start()
            for rel in direct:
                send(i, rel).start()

    def forward(i):
        for rel in chips:
            landed(i, rel).wait_recv()
            passed(i, rel).start()

    def wait():
        for i in range(len(ins)):
            local(i).wait()
            for rel in (1, 3, 5, 7):
                landed(i, rel).wait_recv()
            for rel in direct:
                send(i, rel).wait_send()
            for rel in chips:
                passed(i, rel).wait_send()

    return start, forward, wait


def _exchange_io(arrays, scatter):
    n = len(arrays)
    any_spec = pl.BlockSpec(memory_space=pl.ANY)
    out_shape = [jax.ShapeDtypeStruct(a.shape if sc else (N_DEV,) + a.shape, a.dtype) for a, sc in zip(arrays, scatter)]
    sems = [pltpu.SemaphoreType.DMA((n, N_DEV - 1)), pltpu.SemaphoreType.DMA((n, N_DEV - 1)), pltpu.SemaphoreType.DMA((n,))]
    return [any_spec] * n, out_shape, sems


def _exchange(name, arrays, scatter):
    n = len(arrays)
    specs, out_shape, sems = _exchange_io(arrays, scatter)

    def body(*refs):
        if any(scatter):
            start, wait = _exchange_plan(refs[:n], refs[n:2 * n], scatter, *refs[2 * n:])
            start()
        else:
            start, forward, wait = _gather_plan(refs[:n], refs[n:2 * n], *refs[2 * n:])
            start()
            for i in range(n):
                forward(i)
        wait()

    return pl.pallas_call(
        body, name=name, in_specs=specs, out_specs=specs, out_shape=out_shape, scratch_shapes=sems,
        compiler_params=pltpu.CompilerParams(has_side_effects=True),
    )(*arrays)


def _scatter_start(name, arrays, lands):
    n = len(arrays)
    hbm = pl.BlockSpec(memory_space=pltpu.HBM)

    def body(*refs):
        ins, land, send_sems, recv_sems = refs[:n], refs[n:2 * n], refs[2 * n], refs[2 * n + 1]
        token = refs[4 * n + 2]
        x, y, c = lax.axis_index("x"), lax.axis_index("y"), lax.axis_index("c")
        me = 4 * x + 2 * y + c
        for i in range(n):
            for rel in range(1, N_DEV):
                k = i * (N_DEV - 1) + rel - 1
                pltpu.make_async_remote_copy(
                    src_ref=ins[i].at[me ^ rel], dst_ref=land[i].at[me], send_sem=send_sems.at[k],
                    recv_sem=recv_sems.at[k], device_id=(x ^ (rel >> 2), y ^ ((rel >> 1) & 1), c ^ (rel & 1)),
                    device_id_type=pl.DeviceIdType.MESH).start()
        token[...] = jnp.zeros_like(token)

    sem = pltpu.SemaphoreType.DMA((n * (N_DEV - 1),))
    bufs = [pltpu.HBM(a.shape, a.dtype) for a in list(arrays) + list(lands)]
    res = pl.pallas_call(
        body, name=name, out_shape=(sem, sem, *bufs, jax.ShapeDtypeStruct((8, LANE), F32)),
        in_specs=[hbm] * (2 * n),
        out_specs=(pl.BlockSpec(memory_space=pltpu.SEMAPHORE),) * 2 + (hbm,) * (2 * n) + (pl.BlockSpec(memory_space=pltpu.VMEM),),
        input_output_aliases={i: 2 + i for i in range(2 * n)},
        compiler_params=pltpu.CompilerParams(has_side_effects=pltpu.SideEffectType.DATAFLOW_SIDE_EFFECTING),
    )(*[pltpu.with_memory_space_constraint(a, pltpu.HBM) for a in list(arrays) + list(lands)])
    return res[0], res[1], res[2:2 + n], res[2 + n:2 + 2 * n], res[2 + 2 * n]


def _scatter_wait(name, send_sems, recv_sems, arrays, lands, after):
    n, n_after = len(arrays), len(after)
    hbm = pl.BlockSpec(memory_space=pltpu.HBM)

    def body(*refs):
        ins, land, s_sems, r_sems = refs[:n], refs[n:2 * n], refs[2 * n], refs[2 * n + 1]
        x, y, c = lax.axis_index("x"), lax.axis_index("y"), lax.axis_index("c")
        me = 4 * x + 2 * y + c
        for i in range(n):
            for rel in range(1, N_DEV):
                k = i * (N_DEV - 1) + rel - 1
                cp = pltpu.make_async_remote_copy(
                    src_ref=ins[i].at[me ^ rel], dst_ref=land[i].at[me ^ rel], send_sem=s_sems.at[k],
                    recv_sem=r_sems.at[k], device_id=(x, y, c), device_id_type=pl.DeviceIdType.MESH)
                cp.wait_send()
                cp.wait_recv()

    res = pl.pallas_call(
        body, name=name, out_shape=[pltpu.HBM(a.shape, a.dtype) for a in list(arrays) + list(lands)],
        in_specs=[hbm] * (2 * n) + [pl.BlockSpec(memory_space=pltpu.SEMAPHORE)] * 2 + [pl.BlockSpec(memory_space=pl.ANY)] * n_after,
        out_specs=[hbm] * (2 * n), input_output_aliases={i: i for i in range(2 * n)},
        compiler_params=pltpu.CompilerParams(has_side_effects=pltpu.SideEffectType.DATAFLOW_SIDE_EFFECTING),
    )(*arrays, *lands, send_sems, recv_sems, *after)
    return res[:n], res[n:]


def _gather_start(name, shards, after):
    n, n_after = len(shards), len(after)
    hbm = pl.BlockSpec(memory_space=pltpu.HBM)
    lands = [lax.empty((N_DEV,) + a.shape, a.dtype) for a in shards]

    def body(*refs):
        ins, land = refs[:n], refs[n:2 * n]
        outs = refs[2 * n + n_after:]
        send_sems, recv_sems, token = outs[:n], outs[n:2 * n], outs[4 * n]
        x, y, c = lax.axis_index("x"), lax.axis_index("y"), lax.axis_index("c")
        me = 4 * x + 2 * y + c
        for i in range(n):
            for rel in range(1, N_DEV):
                pltpu.make_async_remote_copy(
                    src_ref=ins[i], dst_ref=land[i].at[me], send_sem=send_sems[i].at[rel - 1],
                    recv_sem=recv_sems[i].at[rel - 1], device_id=(x ^ (rel >> 2), y ^ ((rel >> 1) & 1), c ^ (rel & 1)),
                    device_id_type=pl.DeviceIdType.MESH).start()
        token[...] = jnp.zeros_like(token)

    sem = pltpu.SemaphoreType.DMA((N_DEV - 1,))
    bufs = [pltpu.HBM(a.shape, a.dtype) for a in list(shards) + lands]
    res = pl.pallas_call(
        body, name=name, out_shape=(*[sem] * (2 * n), *bufs, jax.ShapeDtypeStruct((8, LANE), F32)),
        in_specs=[hbm] * (2 * n) + [pl.BlockSpec(memory_space=pl.ANY)] * n_after,
        out_specs=(pl.BlockSpec(memory_space=pltpu.SEMAPHORE),) * (2 * n) + (hbm,) * (2 * n)
                  + (pl.BlockSpec(memory_space=pltpu.VMEM),),
        input_output_aliases={i: 2 * n + i for i in range(2 * n)},
        compiler_params=pltpu.CompilerParams(has_side_effects=pltpu.SideEffectType.DATAFLOW_SIDE_EFFECTING),
    )(*[pltpu.with_memory_space_constraint(a, pltpu.HBM) for a in list(shards) + lands], *after)
    return res[:n], res[n:2 * n], res[2 * n:3 * n], res[3 * n:4 * n], res[4 * n]


def _gather_wait(name, send_sems, recv_sems, shards, lands, after):
    n, n_after = len(shards), len(after)
    hbm = pl.BlockSpec(memory_space=pltpu.HBM)

    def body(*refs):
        ins, land = refs[:n], refs[n:2 * n]
        s_sems, r_sems = refs[2 * n:3 * n], refs[3 * n:4 * n]
        x, y, c = lax.axis_index("x"), lax.axis_index("y"), lax.axis_index("c")
        me = 4 * x + 2 * y + c
        for i in range(n):
            for rel in range(1, N_DEV):
                cp = pltpu.make_async_remote_copy(
                    src_ref=ins[i], dst_ref=land[i].at[me ^ rel], send_sem=s_sems[i].at[rel - 1],
                    recv_sem=r_sems[i].at[rel - 1], device_id=(x, y, c), device_id_type=pl.DeviceIdType.MESH)
                cp.wait_send()
                cp.wait_recv()

    res = pl.pallas_call(
        body, name=name, out_shape=[pltpu.HBM(a.shape, a.dtype) for a in list(shards) + list(lands)],
        in_specs=[hbm] * (2 * n) + [pl.BlockSpec(memory_space=pltpu.SEMAPHORE)] * (2 * n)
                 + [pl.BlockSpec(memory_space=pl.ANY)] * n_after,
        out_specs=[hbm] * (2 * n), input_output_aliases={i: i for i in range(2 * n)},
        compiler_params=pltpu.CompilerParams(has_side_effects=pltpu.SideEffectType.DATAFLOW_SIDE_EFFECTING),
    )(*shards, *lands, *send_sems, *recv_sems, *after)
    me = 4 * lax.axis_index("x") + 2 * lax.axis_index("y") + lax.axis_index("c")
    return [lax.dynamic_update_slice_in_dim(land, shard[None], me, axis=0) for shard, land in zip(res[:n], res[n:])]


def _tri_powers(low):
    powers, n, p = [low.astype(BF16)], 1, low
    while 2 * n < low.shape[-1]:
        p = _mm(p, p, "nn", REC_PASSES)
        powers.append(p.astype(BF16))
        n *= 2
    return powers


@jax.custom_vjp
def _tri_solve(low, rhs, powers):
    del low
    for p in powers:
        rhs = rhs + _mm(p, rhs, "nn", REC_PASSES)
    return rhs


def _tri_solve_fwd(low, rhs, powers):
    out = _tri_solve(low, rhs, powers)
    return out, (powers, out)


def _tri_solve_bwd(res, d):
    powers, u = res
    for p in powers:
        d = d + _mm(p, d, "tn", REC_PASSES)
    return _mm(d, u, "nt", REC_PASSES), d, [jnp.zeros_like(p) for p in powers]


_tri_solve.defvjp(_tri_solve_fwd, _tri_solve_bwd)


@jax.custom_vjp
def _tri_solve_given(low, rhs, powers, value):
    del low, rhs, powers
    return value


def _tri_solve_given_fwd(low, rhs, powers, value):
    return value, (powers, value)


def _tri_solve_given_bwd(res, d):
    return _tri_solve_bwd(res, d) + (jnp.zeros_like(res[1]),)


_tri_solve_given.defvjp(_tri_solve_given_fwd, _tri_solve_given_bwd)


def _heads(x):
    return jnp.stack([x[:, h * HEAD_DIM:(h + 1) * HEAD_DIM] for h in range(N_HEADS)])


def _unheads(x):
    return jnp.concatenate([x[h] for h in range(N_HEADS)], axis=-1)


def _causal_masks(c):
    ti = lax.broadcasted_iota(jnp.int32, (c, c), 0)
    si = lax.broadcasted_iota(jnp.int32, (c, c), 1)
    strict, incl = si < ti, si <= ti
    both = jnp.concatenate([jnp.concatenate([strict, strict], axis=1), jnp.concatenate([incl, incl], axis=1)], axis=0)
    return strict, incl, both


@jax.custom_vjp
def _gram_given(x2, y2, value):
    del x2, y2
    return value.astype(F32)


def _gram_given_fwd(x2, y2, value):
    return value.astype(F32), (x2, y2, value)


def _gram_given_bwd(res, d):
    x2, y2, value = res
    d = jnp.where(_causal_masks(d.shape[-1] // 2)[2], d, 0.0)
    return _mm(d, y2, "nn", 2), _mm(d, x2, "tn", 2), jnp.zeros_like(value)


_gram_given.defvjp(_gram_given_fwd, _gram_given_bwd)


def _chunk_fwd(z0, r, lw, k, v, a, b, powers=None, gram_value=None, u_value=None):
    c = r.shape[0]
    n_h, n_k = z0.shape[0], z0.shape[1]
    mm = functools.partial(_mm, passes=REC_PASSES)
    gram = functools.partial(_mm, passes=2)
    _, incl, mask = _causal_masks(c)
    cum = _mm(incl.astype(F32), lw, "nn", 3)
    cum_end = cum[c - 1:c, :]
    e_neg, e_end = jnp.exp(-cum), jnp.exp(cum_end - cum)
    x2 = jnp.concatenate([_heads(a * jnp.exp(cum - lw)), _heads(r * jnp.exp(cum))], axis=1)
    y2 = jnp.concatenate([_heads(b * e_neg), _heads(k * e_neg)], axis=1)
    vh = _heads(v)
    g2 = jnp.where(mask, gram(x2, y2, "nt"), 0.0) if gram_value is None else _gram_given(x2, y2, gram_value)
    t2 = mm(x2, z0, "nn") + mm(g2[:, :, c:], vh, "nn")
    low = g2[:, :c, :c]
    powers = _tri_powers(low) if powers is None else powers
    u = _tri_solve(low, t2[:, :c], powers) if u_value is None else _tri_solve_given(low, t2[:, :c], powers, u_value)
    y = t2[:, c:] + mm(g2[:, c:, :c], u, "nn")
    ki = lax.broadcasted_iota(jnp.int32, (n_k, n_k), 0)
    kj = lax.broadcasted_iota(jnp.int32, (n_k, n_k), 1)
    dmat = jnp.where(ki == kj, jnp.broadcast_to(_heads(jnp.exp(cum_end)), (n_h, n_k, n_k)), 0.0)
    z_end = mm(dmat, z0, "nn") + mm(jnp.concatenate([_heads(b * e_end), _heads(k * e_end)], axis=1),
                                    jnp.concatenate([u, vh], axis=1), "tn")
    return _unheads(y), z_end, powers, g2, u


def _rec_params():
    return pltpu.CompilerParams(dimension_semantics=("arbitrary",), vmem_limit_bytes=VMEM_LIMIT, has_side_effects=True)


def _rec_fwd(u, lw, k, a, b, xch):
    t_len = lw.shape[0]
    c = min(REC_CHUNK, t_len)
    nc = t_len // c
    per = REC_CHUNKS_PER_STEP if nc % REC_CHUNKS_PER_STEP == 0 else 1
    steps = nc // per
    n_x = len(xch)
    x_specs, x_shapes, x_sems = _exchange_io(xch, [False] * n_x) if n_x else ([], [], [])
    n_pow = max(1, (c - 1).bit_length())
    sizes = [a_.size * a_.dtype.itemsize for a_ in xch]
    pass_step = [min(steps - 1, int(0.9 * steps * sum(sizes[:j + 1]) / sum(sizes)) + 1) for j in range(n_x)]

    def body(*refs):
        r_ref, v_ref, lw_ref, k_ref, a_ref, b_ref = refs[:6]
        x_in = refs[6:6 + n_x]
        y_ref, zs_ref, pw_ref, gs_ref, us_ref = refs[6 + n_x:11 + n_x]
        x_out = refs[11 + n_x:11 + 2 * n_x]
        z_scr = refs[11 + 2 * n_x]
        i = pl.program_id(0)
        if n_x:
            start, forward, wait = _gather_plan(x_in, x_out, *refs[12 + 2 * n_x:])
            pl.when(i == 0)(start)

        @pl.when(i == 0)
        def _():
            z_scr[...] = jnp.zeros_like(z_scr)

        for s in range(per):
            rows = pl.ds(s * c, c)
            z0 = z_scr[...]
            zs_ref[s] = z0
            y, z_end, powers, g2, u_rows = _chunk_fwd(z0, r_ref[rows, :], lw_ref[rows, :], k_ref[rows, :], v_ref[rows, :],
                                                      a_ref[rows, :], b_ref[rows, :])
            y_ref[rows, :] = y
            z_scr[...] = z_end
            pw_ref[s] = jnp.concatenate(powers, axis=0)
            gs_ref[s] = g2.astype(BF16)
            us_ref[s] = u_rows

        for j in range(n_x):
            pl.when(i == pass_step[j])(functools.partial(forward, j))
        if n_x:
            pl.when(i == steps - 1)(wait)

    blk = lambda cb: pl.BlockSpec((per * c, RWKV_DIM), functools.partial(lambda i, q: (i, q), q=cb))
    res = pl.pallas_call(
        body, name="rwkv_rec_fwd", grid=(steps,),
        in_specs=[blk(0), blk(2)] + [blk(0)] * 4 + x_specs,
        out_specs=[blk(0), pl.BlockSpec((per, N_HEADS, HEAD_DIM, HEAD_DIM), lambda i: (i, 0, 0, 0)),
                   pl.BlockSpec((per, n_pow * N_HEADS, c, c), lambda i: (i, 0, 0, 0)),
                   pl.BlockSpec((per, N_HEADS, 2 * c, 2 * c), lambda i: (i, 0, 0, 0)),
                   pl.BlockSpec((per, N_HEADS, c, HEAD_DIM), lambda i: (i, 0, 0, 0))] + x_specs,
        out_shape=[jax.ShapeDtypeStruct((t_len, RWKV_DIM), F32),
                   jax.ShapeDtypeStruct((nc, N_HEADS, HEAD_DIM, HEAD_DIM), F32),
                   jax.ShapeDtypeStruct((nc, n_pow * N_HEADS, c, c), BF16),
                   jax.ShapeDtypeStruct((nc, N_HEADS, 2 * c, 2 * c), BF16),
                   jax.ShapeDtypeStruct((nc, N_HEADS, c, HEAD_DIM), F32)] + x_shapes,
        scratch_shapes=[pltpu.VMEM((N_HEADS, HEAD_DIM, HEAD_DIM), F32)] + x_sems,
        compiler_params=_rec_params() if n_x else _params(("arbitrary",)),
    )(u, u, lw, k, a, b, *xch)
    return res[0], res[1:5], res[5:]


def _rec_bwd(u, lw, k, a, b, saved, dy):
    t_len = lw.shape[0]
    c = min(REC_CHUNK, t_len)
    nc = t_len // c
    per = REC_CHUNKS_PER_STEP if nc % REC_CHUNKS_PER_STEP == 0 else 1
    steps = nc // per

    zs, pw, gs, us = saved

    def body(r_ref, v_ref, lw_ref, k_ref, a_ref, b_ref, zs_ref, pw_ref, gs_ref, us_ref, dy_ref, *rest):
        g_refs, dz_scr = rest[:6], rest[6]

        @pl.when(pl.program_id(0) == 0)
        def _():
            dz_scr[...] = jnp.zeros_like(dz_scr)

        for s in reversed(range(per)):
            rows = pl.ds(s * c, c)
            powers = [pw_ref[s, j * N_HEADS:(j + 1) * N_HEADS] for j in range(pw.shape[1] // N_HEADS)]
            chunk = functools.partial(lambda gram, u_val, pws, *xs: _chunk_fwd(*xs, powers=pws, gram_value=gram, u_value=u_val)[:2],
                                      gs_ref[s], us_ref[s], powers)
            _, vjp = jax.vjp(chunk, zs_ref[s], r_ref[rows, :], lw_ref[rows, :], k_ref[rows, :], v_ref[rows, :],
                             a_ref[rows, :], b_ref[rows, :])
            dz0, dr, dlw, dk, dv, da, db = vjp((dy_ref[rows, :], dz_scr[...]))
            for ref, val in zip(g_refs, (dr, dv, dlw, dk, da, db)):
                ref[rows, :] = val
            dz_scr[...] = dz0

    blk = lambda cb: pl.BlockSpec((per * c, RWKV_DIM), functools.partial(lambda i, q: (steps - 1 - i, q), q=cb))
    saved_blk = lambda arr: pl.BlockSpec((per,) + arr.shape[1:], lambda i: (steps - 1 - i, 0, 0, 0))
    return pl.pallas_call(
        body, name="rwkv_rec_bwd", grid=(steps,),
        in_specs=[blk(0), blk(2)] + [blk(0)] * 4 + [saved_blk(zs), saved_blk(pw), saved_blk(gs), saved_blk(us), blk(0)],
        out_specs=[blk(0)] * 6, out_shape=[jax.ShapeDtypeStruct((t_len, RWKV_DIM), F32)] * 6,
        scratch_shapes=[pltpu.VMEM((N_HEADS, HEAD_DIM, HEAD_DIM), F32)], compiler_params=_params(("arbitrary",)),
    )(u, u, lw, k, a, b, zs, pw, gs, us, dy)


_EARLY = ["w_in", "conv_w", "w_lora_up", "a_lora_up", "g_lora_up"]
_LATE = ["w_out", "w_up", "w_down", "w_ple_gate", "w_ple_proj"]
_SHARDED = _EARLY + _LATE
_COL_SHARDED = {"w_in", "conv_w", "w_lora_up", "a_lora_up", "g_lora_up", "w_up", "w_ple_proj"}
_BF16_GATHER = {"w_in", "w_out", "w_up", "w_down", "w_ple_gate", "w_ple_proj"}
_REPLICATED = ["norm_mix_g", "shift_mu", "w0", "a0", "k_k", "k_a", "r_k", "ln_x_g", "ln_x_b", "norm_mlp_g", "norm_ple_g",
               "norm_final_g"]
_WEIGHTS = ["norm_mix_g", "w_in", "conv_w", "shift_mu", "w_lora_up", "w0", "a_lora_up", "a0", "g_lora_up", "k_k", "k_a", "r_k",
            "ln_x_g", "ln_x_b", "w_out", "norm_mlp_g", "w_up", "w_down", "norm_ple_g", "w_ple_gate", "w_ple_proj", "norm_final_g"]


def _unshard(name, g):
    if name in _COL_SHARDED:
        return jnp.moveaxis(g, 0, 1).reshape(g.shape[1], N_DEV * g.shape[2])
    return g.reshape(N_DEV * g.shape[1], g.shape[2])


def _reshard(name, full):
    if name in _COL_SHARDED:
        return jnp.moveaxis(full.reshape(full.shape[0], N_DEV, full.shape[1] // N_DEV), 1, 0)
    return full.reshape(N_DEV, full.shape[0] // N_DEV, full.shape[1])


def _pad_in_cols(a):
    z = lambda n: jnp.zeros(a.shape[:-1] + (n,), a.dtype)
    conv = [a[..., part * CONV_DIM + j * LANE:part * CONV_DIM + (j + 1) * LANE] for j in range(CONV_DIM // LANE) for part in range(3)]
    return jnp.concatenate(conv + [a[..., CONV_COLS:3136], z(64), a[..., 3136:3200], z(64), a[..., 3200:3360], z(96)], axis=-1)


def _unpad_in_cols(a):
    conv = [a[..., (3 * j + part) * LANE:(3 * j + part + 1) * LANE] for part in range(3) for j in range(CONV_DIM // LANE)]
    return jnp.concatenate(conv + [a[..., CONV_COLS:3136], a[..., 3200:3264], a[..., 3328:3488]], axis=-1)


def _assemble_w_in(g):
    n_dev, rows, cols = g.shape

    def body(g_ref, o_ref):
        o_ref[...] = _pad_in_cols(jnp.concatenate([g_ref[d] for d in range(n_dev)], axis=1))

    return pl.pallas_call(
        body, name="w_in_assemble", grid=(rows // ROW_BLOCK,),
        in_specs=[pl.BlockSpec((n_dev, ROW_BLOCK, cols), lambda i: (0, i, 0))],
        out_specs=pl.BlockSpec((ROW_BLOCK, IN_PAD), lambda i: (i, 0)),
        out_shape=jax.ShapeDtypeStruct((rows, IN_PAD), g.dtype), compiler_params=_params(("arbitrary",)),
    )(g)


def _split_w_in_grad(dw):
    rows = dw.shape[0]
    cols = IN_COLS // N_DEV

    def body(d_ref, o_ref):
        full = _unpad_in_cols(d_ref[...])
        for d in range(N_DEV):
            o_ref[d] = full[:, cols * d:cols * (d + 1)]

    return pl.pallas_call(
        body, name="w_in_grad_split", grid=(rows // ROW_BLOCK,),
        in_specs=[pl.BlockSpec((ROW_BLOCK, IN_PAD), lambda i: (i, 0))],
        out_specs=pl.BlockSpec((N_DEV, ROW_BLOCK, cols), lambda i: (0, i, 0)),
        out_shape=jax.ShapeDtypeStruct((N_DEV, rows, cols), dw.dtype), compiler_params=_params(("arbitrary",)),
    )(dw)


def _pad_rows(a, rows):
    return jnp.concatenate([a, jnp.zeros((rows - a.shape[0],) + a.shape[1:], a.dtype)], axis=0)


SEG_W = [RWKV_DIM, RWKV_DIM, RWKV_DIM, LANE, LANE, 2 * LANE]
SEG_OFF = [0, 512, 1024, XW_OFF, XA_OFF, XG_OFF]


def _rwkv_pre_bwd(proj, u, grads, mu, small, dproj):
    t_len = u.shape[0]
    tr = min(ROW_BLOCK, t_len)
    nb = t_len // tr
    sub = 8
    n_g = len(grads)
    acc_shapes = [(1, RW_PAD)] + [(1, RWKV_DIM)] * 4 + [(LANE, RWKV_DIM), (LANE, RWKV_DIM), (2 * LANE, RWKV_DIM)]

    def body(*refs):
        seg_refs, halo_refs = refs[:6], refs[6:12]
        k_ref, xw_ref, xa_ref, xg_ref = refs[12:16]
        g_refs = refs[16:16 + n_g]
        mu_ref = refs[16 + n_g]
        prm_refs = refs[17 + n_g:24 + n_g]
        out_hbm = refs[25 + n_g]
        acc_refs = refs[26 + n_g:26 + n_g + len(acc_shapes)]
        vbuf, sems, carry = refs[26 + n_g + len(acc_shapes):]
        i = pl.program_id(0)
        blk = nb - 1 - i
        dr1, dr2, dv1, dv2, dlw, dk1, dk2, da, db, dg = [g[...] for g in g_refs]
        _, vjp = jax.vjp(_rwkv_pre, k_ref[...], xw_ref[...], xa_ref[...], xg_ref[...], *[p_[...] for p_ in prm_refs])
        dk, dxw, dxa, dxg, *dprm = vjp((dlw, dk1 + dk2, da, db, dg))
        du = jnp.concatenate([dr1 + dr2, dk, dv1 + dv2, dxw, dxa, dxg], axis=1)
        mu_v = mu_ref[...]

        @pl.when(i == 0)
        def _():
            carry[...] = jnp.zeros_like(carry)

        rows = lax.broadcasted_iota(jnp.int32, du.shape, 0)
        nxt = jnp.where(rows == tr - 1, carry[...], pltpu.roll(du, tr - 1, 0))
        d_rw = du - mu_v * du + mu_v * nxt
        d_mu = []
        for s_ref, h_ref, off, wd in zip(seg_refs, halo_refs, SEG_OFF, SEG_W):
            cur = s_ref[...]
            r0 = lax.broadcasted_iota(jnp.int32, cur.shape, 0)
            prev = jnp.where(r0 == 0, jnp.where(blk == 0, 0.0, h_ref[sub - 1:sub, :]), pltpu.roll(cur, 1, 0))
            d_mu.append(jnp.sum(du[:, off:off + wd] * (prev - cur), axis=0, keepdims=True))
        sums = [jnp.concatenate(d_mu, axis=1)] + list(dprm)

        @pl.when(i == 0)
        def _():
            for a_ref, val in zip(acc_refs, sums):
                a_ref[...] = val

        @pl.when(i > 0)
        def _():
            for a_ref, val in zip(acc_refs, sums):
                a_ref[...] += val

        carry[...] = du[0:1, :]
        slot = i % 2

        def writeback(s, b):
            return pltpu.make_async_copy(vbuf.at[s], out_hbm.at[pl.ds(b * tr, tr), pl.ds(CONV_COLS, RW_PAD)], sems.at[s])

        @pl.when(i >= 2)
        def _():
            writeback(slot, blk + 2).wait()

        vbuf[slot] = d_rw.astype(vbuf.dtype)
        writeback(slot, blk).start()

        @pl.when(i == nb - 1)
        def _():
            writeback(slot, blk).wait()
            if nb > 1:
                writeback(1 - slot, blk + 1).wait()

    rev = lambda w_, cb: pl.BlockSpec((tr, w_), functools.partial(lambda i, c: (nb - 1 - i, c), c=cb))
    halo = lambda w_, cb: pl.BlockSpec((sub, w_), functools.partial(
        lambda i, c: (jnp.maximum((nb - 1 - i) * (tr // sub) - 1, 0), c), c=cb))
    whole = lambda a: pl.BlockSpec(a.shape, functools.partial(lambda i, n: (0,) * n, n=a.ndim))
    segs = [(wd, (CONV_COLS + off) // wd) for off, wd in zip(SEG_OFF, SEG_W)]
    u_cols = [(512, 1), (LANE, XW_OFF // LANE), (LANE, XA_OFF // LANE), (2 * LANE, XG_OFF // (2 * LANE))]
    any_spec = pl.BlockSpec(memory_space=pl.ANY)
    res = pl.pallas_call(
        body, name="rwkv_pre_bwd", grid=(nb,),
        in_specs=[rev(*s) for s in segs] + [halo(*s) for s in segs] + [rev(*c) for c in u_cols]
                 + [rev(RWKV_DIM, 0)] * n_g + [whole(mu)] + [whole(p_) for p_ in small] + [any_spec],
        out_specs=[any_spec] + [pl.BlockSpec(s, functools.partial(lambda i, n: (0,) * n, n=len(s))) for s in acc_shapes],
        out_shape=[jax.ShapeDtypeStruct(dproj.shape, dproj.dtype)] + [jax.ShapeDtypeStruct(s, F32) for s in acc_shapes],
        scratch_shapes=[pltpu.VMEM((2, tr, RW_PAD), dproj.dtype), pltpu.SemaphoreType.DMA((2,)), pltpu.VMEM((1, RW_PAD), F32)],
        input_output_aliases={24 + n_g: 0},
        compiler_params=_params(("arbitrary",)),
    )(*[proj] * 12, *[u] * 4, *grads, mu, *small, dproj)
    return res


def _local_step(x, p, tgt, w, early_shards, late_shards):
    row = lambda v: v.reshape(1, -1)
    w = dict(w)

    xn1, *gathered = _rowwise("rms_mix", lambda h, g: (_rms(h, g),), [x], [w["norm_mix_g"]], [(D_MODEL, BF16)],
                              gather=early_shards)
    w.update({n: _unshard(n, g_) for n, g_ in zip(_EARLY[1:], gathered[1:])})
    w["w_in"] = _assemble_w_in(gathered[0])
    w["w_lora_up"] = _pad_rows(w["w_lora_up"], LANE)
    w["a_lora_up"] = _pad_rows(w["a_lora_up"], LANE)
    w["g_lora_up"] = _pad_rows(w["g_lora_up"], 2 * LANE)
    lg_send, lg_recv, lg_shards, lg_lands, lg_token = _gather_start("late_gather_start", late_shards, after=[xn1])
    w["shift_mu"] = w["shift_mu"] + lg_token[0:1, 0:1]
    proj = _matmul("in_proj", xn1, w["w_in"], "nn", [F32], tm=2048, tn=512, tk=D_MODEL)
    n_cb = CONV_DIM // LANE

    def conv_fwd(blk, cw):
        gb, gc, hx = blk[:, :LANE], blk[:, LANE:2 * LANE], blk[:, 2 * LANE:]
        uu = gc * hx
        return (gb * (uu * cw[2:3] + _shift_down(uu, 1) * cw[1:2] + _shift_down(uu, 2) * cw[0:1]),)

    (y_conv,) = _colwise("conv_fwd", conv_fwd, n_cb, [(proj, 3 * LANE)], [w["conv_w"]], [(CONV_DIM, BF16, LANE)])

    small = [w["w0"], w["a0"], w["k_k"], w["k_a"], w["w_lora_up"], w["a_lora_up"], w["g_lora_up"]]
    def pre_fwd(*xs):
        cur, prev_rows, mu, prm = xs[:6], xs[6:12], xs[12], xs[13:]
        segs = []
        for c_, p_, off, wd in zip(cur, prev_rows, SEG_OFF, SEG_W):
            rows = lax.broadcasted_iota(jnp.int32, c_.shape, 0)
            prev = jnp.where(rows == 0, p_, pltpu.roll(c_, 1, 0))
            segs.append(c_ + mu[:, off:off + wd] * (prev - c_))
        return (jnp.concatenate(segs, axis=1),) + tuple(_rwkv_pre(segs[1], segs[3], segs[4], segs[5], *prm))

    proj_segs = [(proj, wd, (CONV_COLS + off) // wd) for off, wd in zip(SEG_OFF, SEG_W)]
    u, lw, k_h, ra, rb, g = _rowwise(
        "rwkv_pre", pre_fwd, proj_segs, [w["shift_mu"]] + small, [(RW_PAD, F32)] + [(RWKV_DIM, F32)] * 5, halo=True)
    y_rec, rec_saved, _ = _rec_fwd(u, lw, k_h, ra, rb, [])

    def late_weight(names, after):
        idx = [_LATE.index(n) for n in names]
        got = _gather_wait("late_gather_wait_" + names[0], [lg_send[i] for i in idx], [lg_recv[i] for i in idx],
                           [lg_shards[i] for i in idx], [lg_lands[i] for i in idx], after)
        return [_unshard(n, g_) for n, g_ in zip(names, got)]
    post_c = [w["ln_x_g"], w["ln_x_b"], w["r_k"]]
    u_r, u_v = (u, 512, 0), (u, 512, 2)
    (y_rwkv,) = _rowwise("rwkv_post", lambda *xs: (_rwkv_post(*xs),), [y_rec, u_r, k_h, u_v, g], post_c, [(RWKV_DIM, BF16)],
                         tr=2 * ROW_BLOCK)
    ycat = jnp.concatenate([y_conv, y_rwkv], axis=1)
    (w["w_out"],) = late_weight(["w_out"], [ycat])
    def res_norm(acc, r_, g_):
        h = acc + r_
        return h, _rms(h, g_)

    h1, xn2 = _matmul("out_proj", ycat, w["w_out"], "nn", [F32, BF16], tm=1024, tn=D_MODEL, tk=D_MODEL, extras=[x],
                      consts=[w["norm_mlp_g"]], epilogue=res_norm)

    square = lambda h: h.astype(F32) * h.astype(F32)
    (w["w_up"],) = late_weight(["w_up"], [xn2])
    hid = _matmul("mlp_up", xn2, w["w_up"], "nn", [BF16], tm=2048, tn=1024, tk=D_MODEL,
                  epilogue=lambda acc: (jnp.maximum(acc, 0.0),))
    (w["w_down"],) = late_weight(["w_down"], [hid])
    h2, xn3 = _matmul("mlp_down", hid, w["w_down"], "nn", [F32, BF16], tm=512, tn=D_MODEL, tk=D_FF, extras=[h1],
                      consts=[w["norm_ple_g"]], epilogue=res_norm, a_map=square)
    w["w_ple_gate"], w["w_ple_proj"] = late_weight(["w_ple_gate", "w_ple_proj"], [xn3])
    zg =_matmul("ple_gate", xn3, w["w_ple_gate"], "nn", [F32], tm=1024, tn=1024, tk=D_MODEL)
    pp = _matmul("ple_proj", p, w["w_ple_proj"], "nn", [F32], tm=1024, tn=1024, tk=PLE_DIM)

    def head(h2_, zg_, pp_, tg, gf):
        gate = _sigmoid(zg_)
        h3 = h2_ + gate * pp_
        out = _rms(h3, gf)
        err = out - tg
        dh3, dgf = _rms_bwd(h3, gf, err * (1.0 / D_MODEL))
        loss = jnp.sum(jnp.sum(err * err, axis=1, keepdims=True), axis=0, keepdims=True) * (0.5 / D_MODEL)
        return dh3, dh3 * pp_ * gate * (1.0 - gate), dh3 * gate, dgf, loss

    dh3, dzg, dpp, d_norm_final, loss = _rowwise(
        "head", head, [h2, zg, pp, tgt], [row(w["norm_final_g"])], [(D_MODEL, F32), (D_MODEL, BF16), (D_MODEL, BF16)],
        [(1, D_MODEL), (1, 1)], tr=2 * ROW_BLOCK)

    d_w_ple_proj = _matmul("d_ple_proj", p, dpp, "tn", [BF16], tm=PLE_DIM, tn=D_MODEL // N_DEV, tk=4096, col_blocks_out=True)
    d_w_ple_gate = _matmul("d_ple_gate", xn3, dzg, "tn", [BF16], tm=512, tn=1024, tk=4096)

    def norm_bwd(dxn, h, dres, g_):
        dh, dg = _rms_bwd(h, g_, dxn)
        dh = dh + dres
        return dh, dh, dg

    nb = dict(tm=512, tn=D_MODEL, epilogue=norm_bwd, sums=[(1, D_MODEL)])
    dh2, dh2_b, d_norm_ple = _matmul("dx_ple_gate", dzg, w["w_ple_gate"], "nt", [F32, BF16], tk=D_MODEL,
                                     extras=[h2, dh3], consts=[w["norm_ple_g"]], **nb)
    d_w_down = _matmul("d_mlp_down", hid, dh2_b, "tn", [BF16], tm=512, tn=1024, tk=4096, a_map=square)
    dpre = _matmul("dx_mlp_down", dh2_b, w["w_down"], "nt", [BF16], tm=2048, tn=1024, tk=D_MODEL, extras=[hid],
                   epilogue=lambda acc, hid_: (acc * (2.0 * hid_.astype(F32)),))
    d_w_up = _matmul("d_mlp_up", xn2, dpre, "tn", [BF16], tm=1024, tn=D_FF // N_DEV, tk=4096, col_blocks_out=True)
    dh1, dh1_b, d_norm_mlp = _matmul("dx_mlp_up", dpre, w["w_up"], "nt", [F32, BF16], tk=D_FF,
                                     extras=[h1, dh2], consts=[w["norm_mlp_g"]], **nb)
    d_w_out = _matmul("d_out_proj", ycat, dh1_b, "tn", [BF16], tm=512, tn=1024, tk=4096)
    dycat = _matmul("dx_out_proj", dh1_b, w["w_out"], "nt", [F32], tm=1024, tn=1024, tk=D_MODEL)
    late_grads = dict(w_out=d_w_out, w_up=d_w_up, w_down=d_w_down, w_ple_gate=d_w_ple_gate, w_ple_proj=d_w_ple_proj)
    late_send = [late_grads[n] if n in ("w_up", "w_ple_proj") else _reshard(n, late_grads[n]) for n in _LATE]
    *late_flight, late_token = _scatter_start("late_scatter_start", late_send, [lax.empty(a.shape, a.dtype) for a in late_send])
    conv_w_bwd = w["conv_w"] + late_token[0:1, 0:1]

    def conv_bwd(dy, blk, cw):
        gb, gc, hx = blk[:, :LANE], blk[:, LANE:2 * LANE], blk[:, 2 * LANE:]
        uu = gc * hx
        u1, u2 = _shift_down(uu, 1), _shift_down(uu, 2)
        dconv = dy * gb
        du = dconv * cw[2:3] + _shift_up(dconv, 1) * cw[1:2] + _shift_up(dconv, 2) * cw[0:1]
        s = lambda z: jnp.sum(z, axis=0, keepdims=True)
        d_blk = jnp.concatenate([dy * (uu * cw[2:3] + u1 * cw[1:2] + u2 * cw[0:1]), du * hx, du * gc], axis=1)
        return d_blk, s(dconv * u2), s(dconv * u1), s(dconv * uu)

    dproj, dcw0, dcw1, dcw2 = _colwise(
        "conv_bwd", conv_bwd, n_cb, [(dycat, LANE), (proj, 3 * LANE)], [conv_w_bwd],
        [(IN_PAD, BF16, 3 * LANE)], [(1, CONV_DIM)] * 3)

    def post_bwd(dy, y, r, k_h_, v, g_, ln_g, ln_b, r_k):
        _, vjp = jax.vjp(_rwkv_post, y, r, k_h_, v, g_, ln_g, ln_b, r_k)
        return vjp(dy)

    dy_rec, dr_p, dk_p, dv_p, dg, d_ln_g, d_ln_b, d_r_k = _rowwise(
        "rwkv_post_bwd", post_bwd, [(dycat, 512, 1), y_rec, u_r, k_h, u_v, g], post_c,
        [(RWKV_DIM, F32)] * 5, [(1, RWKV_DIM)] * 3)
    dr_r, dv_r, dlw, dk_r, da, db = _rec_bwd(u, lw, k_h, ra, rb, rec_saved, dy_rec)

    dproj, d_mu, d_w0, d_a0, d_k_k, d_k_a, d_wl, d_al, d_gl = _rwkv_pre_bwd(
        proj, u, [dr_p, dr_r, dv_p, dv_r, dlw, dk_p, dk_r, da, db, dg], w["shift_mu"], small, dproj)
    d_w_in = _matmul("d_in_proj", xn1, dproj, "tn", [BF16], tm=1024, tn=896, tk=4096)
    early_grads = dict(conv_w=jnp.concatenate([dcw0, dcw1, dcw2], axis=0),
                       w_lora_up=d_wl[:64], a_lora_up=d_al[:64], g_lora_up=d_gl[:160])
    early_send = [_split_w_in_grad(d_w_in)] + [_reshard(n, early_grads[n]) for n in _EARLY[1:]]
    *early_flight, token = _scatter_start("early_scatter_start", early_send, [lax.empty(a.shape, a.dtype) for a in early_send])
    dx, d_norm_mix = _matmul(
        "dx_in_proj", dproj, w["w_in"], "nt", [F32], tk=IN_PAD, extras=[x, dh1], consts=[w["norm_mix_g"] + token[0:1, 0:1]],
        **dict(nb, epilogue=lambda *a: norm_bwd(*a)[1:]))

    grads = dict(
        norm_mix_g=d_norm_mix, shift_mu=d_mu, w0=d_w0, a0=d_a0, k_k=d_k_k, k_a=d_k_a, r_k=d_r_k,
        ln_x_g=d_ln_g, ln_x_b=d_ln_b, norm_mlp_g=d_norm_mlp, norm_ple_g=d_norm_ple, norm_final_g=d_norm_final)
    return loss, dx, grads, late_flight, early_flight, d_w_in


def _adam_update(partials, w_ref, m_ref, v_ref, g_ref, d_ref, nm_ref, nv_ref):
    g = partials[0].astype(F32)
    for part in partials[1:]:
        g = g + part.astype(F32)
    nm =ADAM_B1 * m_ref[...] + (1.0 - ADAM_B1) * g
    nv = ADAM_B2 * v_ref[...] + (1.0 - ADAM_B2) * (g * g)
    m_hat = nm / (1.0 - ADAM_B1 ** ADAM_STEP)
    v_hat = nv / (1.0 - ADAM_B2 ** ADAM_STEP)
    g_ref[...] = g
    d_ref[...] = -ADAM_LR * (m_hat / (jnp.sqrt(v_hat) + ADAM_EPS) + ADAM_WD * w_ref[...])
    nm_ref[...] = nm
    nv_ref[...] = nv


SMALL_ROWS = 8


def _small_layout(widths):
    widths = list(widths) + [1]
    fill, place = [0] * SMALL_ROWS, [None] * len(widths)
    for j in sorted(range(len(widths)), key=lambda q: -widths[q]):
        row = fill.index(min(fill))
        place[j] = (row, fill[row])
        fill[row] += -(-widths[j] // LANE) * LANE
    return place, max(fill)


def _pack_small(vecs, loss):
    place, total = _small_layout([v_.shape[1] for v_ in vecs])
    n = len(vecs)

    def body(*refs):
        out = jnp.zeros((SMALL_ROWS, total), F32)
        row_id = lax.broadcasted_iota(jnp.int32, (SMALL_ROWS, total), 0)
        for row in range(SMALL_ROWS):
            mine = sorted((off, j) for j, (r_, off) in enumerate(place) if r_ == row)
            pieces, at = [], 0
            for off, j in mine:
                val = refs[j][...]
                pieces.append(val)
                at = off + val.shape[1]
                pad = -val.shape[1] % LANE
                if pad:
                    pieces.append(jnp.zeros((1, pad), F32))
                    at += pad
            if total > at:
                pieces.append(jnp.zeros((1, total - at), F32))
            out = jnp.where(row_id == row, jnp.broadcast_to(jnp.concatenate(pieces, axis=1), (SMALL_ROWS, total)), out)
        refs[n + 1][...] = out

    return pl.pallas_call(body, name="pack_small", out_shape=jax.ShapeDtypeStruct((SMALL_ROWS, total), F32))(*vecs, loss)


def _adamw_small(packed, ws, ms, vs):
    n = len(ws)
    place, _ = _small_layout([w_.shape[1] for w_ in ws])

    def body(p_ref, *refs):
        w_refs, m_refs, v_refs, outs = refs[:n], refs[n:2 * n], refs[2 * n:3 * n], refs[3 * n:]
        for j in range(n):
            row, off = place[j]
            cols = pl.ds(off, ws[j].shape[1])
            _adam_update([p_ref[s, row:row + 1, cols] for s in range(N_DEV)], w_refs[j], m_refs[j], v_refs[j],
                         *outs[4 * j:4 * j + 4])
        row, off = place[n]
        total = p_ref[0, row:row + 1, off:off + 1]
        for s in range(1, N_DEV):
            total = total + p_ref[s, row:row + 1, off:off + 1]
        outs[4 * n][...] = total

    res = pl.pallas_call(
        body, name="adamw_small",
        out_shape=[jax.ShapeDtypeStruct(w_.shape, F32) for w_ in ws for _ in range(4)] + [jax.ShapeDtypeStruct((1, 1), F32)],
    )(packed, *ws, *ms, *vs)
    return [res[4 * j:4 * j + 4] for j in range(n)], res[4 * n]


def _adamw(name, parts, w, m, v, own=None, me=None):
    rows, cols = w.shape[-2:]
    lead = w.ndim - 2
    tr = rows if rows * cols * 4 * 8 <= (4 << 20) else max(8, (4 << 20) // (cols * 4 * 8) // 8 * 8)
    while rows % tr:
        tr -= 8
    shape4 = [jax.ShapeDtypeStruct(w.shape, F32)] * 4
    if own is None:
        def body(p_ref, *refs):
            _adam_update([p_ref[s] for s in range(N_DEV)], *refs)

        blk = pl.BlockSpec((None,) * lead + (tr, cols), lambda i: (0,) * lead + (i, 0))
        return pl.pallas_call(
            body, name=name, grid=(rows // tr,),
            in_specs=[pl.BlockSpec((N_DEV, tr, cols), lambda i: (0, i, 0)), blk, blk, blk], out_specs=[blk] * 4,
            out_shape=shape4, compiler_params=_params(("arbitrary",)),
        )(parts, w, m, v)

    def body_own(me_ref, p_ref, own_ref, *refs):
        mine = own_ref[...]
        _adam_update([jnp.where(me_ref[0] == s, mine, p_ref[s]) for s in range(N_DEV)], *refs)

    blk = pl.BlockSpec((None,) * lead + (tr, cols), lambda i, me_ref: (0,) * lead + (i, 0))
    return pl.pallas_call(
        body_own, name=name, out_shape=shape4,
        grid_spec=pltpu.PrefetchScalarGridSpec(
            num_scalar_prefetch=1, grid=(rows // tr,),
            in_specs=[pl.BlockSpec((N_DEV, tr, cols), lambda i, me_ref: (0, i, 0)),
                      pl.BlockSpec((None, tr, cols), lambda i, me_ref: (me_ref[0], i, 0)), blk, blk, blk],
            out_specs=[blk] * 4),
        compiler_params=_params(("arbitrary",)),
    )(me, parts, own, w, m, v)


def kernel(x, p, norm_mix_g, w_in, conv_w, shift_mu, w_lora_up, w0, a_lora_up, a0, g_lora_up, k_k, k_a, r_k, ln_x_g, ln_x_b, w_out, norm_mlp_g, w_up, w_down, norm_ple_g, w_ple_gate, w_ple_proj, norm_final_g, loss_target, m_norm_mix_g, m_w_in, m_conv_w, m_shift_mu, m_w_lora_up, m_w0, m_a_lora_up, m_a0, m_g_lora_up, m_k_k, m_k_a, m_r_k, m_ln_x_g, m_ln_x_b, m_w_out, m_norm_mlp_g, m_w_up, m_w_down, m_norm_ple_g, m_w_ple_gate, m_w_ple_proj, m_norm_final_g, v_norm_mix_g, v_w_in, v_conv_w, v_shift_mu, v_w_lora_up, v_w0, v_a_lora_up, v_a0, v_g_lora_up, v_k_k, v_k_a, v_r_k, v_ln_x_g, v_ln_x_b, v_w_out, v_norm_mlp_g, v_w_up, v_w_down, v_norm_ple_g, v_w_ple_gate, v_w_ple_proj, v_norm_final_g):
    args = dict(locals())
    wts = {n: args[n] for n in _WEIGHTS}
    mom = {n: args["m_" + n] for n in _WEIGHTS}
    var = {n: args["v_" + n] for n in _WEIGHTS}
    shard2d = lambda a: a.reshape(a.shape[-2:])
    pad_mu = lambda a: _pad_in_cols(jnp.concatenate([jnp.zeros((1, CONV_COLS), F32), a], axis=1))[:, CONV_COLS:]
    unpad_mu = lambda a: _unpad_in_cols(jnp.concatenate([jnp.zeros((1, CONV_COLS), F32), a], axis=1))[:, CONV_COLS:]

    shards = {n: shard2d(wts[n]).astype(BF16 if n in _BF16_GATHER else F32) for n in _SHARDED}
    w = {n: wts[n].reshape(1, -1) for n in _REPLICATED}
    w["shift_mu"] = pad_mu(wts["shift_mu"])

    loss, dx, grads, late_flight, early_flight, d_w_in = _local_step(
        x[0], p[0, 0], loss_target[0], w, [shards[n] for n in _EARLY], [shards[n] for n in _LATE])

    me = (4 * lax.axis_index("x") + 2 * lax.axis_index("y") + lax.axis_index("c")).astype(jnp.int32).reshape(1)
    late_sent, late_parts = _scatter_wait("late_scatter_wait", *late_flight, after=[d_w_in])
    out = {n: _adamw("adamw_" + n, prt, wts[n], mom[n], var[n], own=own, me=me)
           for n, prt, own in zip(_LATE, late_parts, late_sent)}
    early_sent, early_parts = _scatter_wait("early_scatter_wait", *early_flight, after=[dx] + [out[n][1] for n in _LATE])
    for n, prt, own in zip(_EARLY, early_parts, early_sent):
        out[n] = _adamw("adamw_" + n, prt, wts[n], mom[n], var[n], own=own, me=me)

    grads["shift_mu"] = unpad_mu(grads["shift_mu"])
    flat = lambda a: a.reshape(1, -1)
    (small_parts,) = _exchange("gather_small", [_pack_small([flat(grads[n]) for n in _REPLICATED], loss)], [False])
    small, loss_total = _adamw_small(small_parts, *[[flat(d[n]) for n in _REPLICATED] for d in (wts, mom, var)])
    for n, res in zip(_REPLICATED, small):
        out[n] = [r.reshape(wts[n].shape) for r in res]
    return (loss_total[0, 0], dx[None], *[out[n][0] for n in _WEIGHTS], *[out[n][1] for n in _WEIGHTS],
            *[out[n][2] for n in _WEIGHTS], *[out[n][3] for n in _WEIGHTS])
```

```python
import functools

import jax
import jax.numpy as jnp
from jax import lax
from jax.experimental import pallas as pl
from jax.experimental.pallas import tpu as pltpu

F32 = jnp.float32
BF16 = jnp.bfloat16

N_DEV = 8
D_MODEL = 1024
CONV_DIM = 512
RWKV_DIM = 512
HEAD_DIM = 64
N_HEADS = 8
D_FF = 4096
PLE_DIM = 256
RMS_EPS = 1e-6
GN_EPS = 64e-5
L2_EPS = 1e-12
ADAM_LR, ADAM_B1, ADAM_B2, ADAM_EPS, ADAM_WD, ADAM_STEP = 0.001, 0.9, 0.999, 1e-08, 0.01, 10

CONV_COLS = 3 * CONV_DIM
RW_PAD = 2048
IN_PAD = CONV_COLS + RW_PAD
IN_COLS = 3360
XW_OFF, XA_OFF, XG_OFF = 1536, 1664, 1792
REC_CHUNK = 128
REC_CHUNKS_PER_STEP = 2
REC_PASSES = 1
ROW_BLOCK = 256
LANE = 128
VMEM_LIMIT = 56 * 1024 * 1024


def _dims(dn, ndim):
    if ndim == 3:
        return {"nn": (((2,), (1,)), ((0,), (0,))), "nt": (((2,), (2,)), ((0,), (0,))),
                "tn": (((1,), (1,)), ((0,), (0,)))}[dn]
    return {"nn": (((1,), (0,)), ((), ())), "nt": (((1,), (1,)), ((), ())), "tn": (((0,), (0,)), ((), ()))}[dn]


def _split2(x):
    hi = x.astype(BF16)
    return hi, (x - hi.astype(F32)).astype(BF16)


def _mm_raw(x, y, dn, passes):
    f = lambda p, q: lax.dot_general(p, q, _dims(dn, x.ndim), preferred_element_type=F32)
    if passes == 1:
        return f(x.astype(BF16), y.astype(BF16))
    xh, xl = _split2(x)
    yh, yl = _split2(y)
    if passes == 2:
        return f(xh, yh) + f(xh, yl)
    return f(xh, yh) + f(xh, yl) + f(xl, yh)


@functools.partial(jax.custom_vjp, nondiff_argnums=(2, 3))
def _mm(x, y, dn, passes):
    return _mm_raw(x, y, dn, passes)


def _mm_fwd(x, y, dn, passes):
    return _mm_raw(x, y, dn, passes), (x, y)


def _mm_bwd(dn, passes, res, d):
    x, y = res
    if dn == "nn":
        return _mm(d, y, "nt", passes), _mm(x, d, "tn", passes)
    if dn == "nt":
        return _mm(d, y, "nn", passes), _mm(d, x, "tn", passes)
    return _mm(y, d, "nt", passes), _mm(x, d, "nn", passes)


_mm.defvjp(_mm_fwd, _mm_bwd)


def _head_ones():
    i = lax.broadcasted_iota(jnp.int32, (RWKV_DIM, RWKV_DIM), 0) // HEAD_DIM
    j = lax.broadcasted_iota(jnp.int32, (RWKV_DIM, RWKV_DIM), 1) // HEAD_DIM
    return (i == j).astype(BF16)


def _hsum_raw(x):
    ones = _head_ones()
    f = lambda p: lax.dot_general(p, ones, _dims("nn", 2), preferred_element_type=F32)
    x1, x2 = _split2(x)
    return f(x1) + f(x2)


@jax.custom_vjp
def _hsum(x):
    return _hsum_raw(x)


_hsum.defvjp(lambda x: (_hsum_raw(x), None), lambda _, d: (_hsum(d),))


def _sigmoid(x):
    return 0.5 + 0.5 * jnp.tanh(0.5 * x)


def _softplus(x):
    return jnp.maximum(x, 0.0) + jnp.log(1.0 + jnp.exp(-jnp.abs(x)))


def _params(sem):
    return pltpu.CompilerParams(dimension_semantics=sem, vmem_limit_bytes=VMEM_LIMIT)


def _rowwise(name, fn, rows, consts, row_outs, acc_outs=(), tr=ROW_BLOCK, halo=False, gather=()):
    rows = [r if isinstance(r, tuple) else (r, r.shape[1], 0) for r in rows]
    t_len = rows[0][0].shape[0]
    tr = min(tr, t_len)
    n_r, n_c, n_o, n_a, n_x = len(rows), len(consts), len(row_outs), len(acc_outs), len(gather)
    n_h = n_r if halo else 0
    sub = 8
    x_specs, x_shapes, x_sems = _exchange_io(gather, [False] * n_x) if n_x else ([], [], [])
    nb = t_len // tr

    def body(*refs):
        if n_x:
            n_in = n_r + n_h + n_c
            start, forward, wait = _gather_plan(refs[n_in:n_in + n_x], refs[len(refs) - 3 - n_x:len(refs) - 3], *refs[len(refs) - 3:])
            pl.when(pl.program_id(0) == 0)(start)
            refs = refs[:n_in] + refs[n_in + n_x:len(refs) - 3 - n_x]
        ins = [r[...] for r in refs[:n_r]]
        ins += [jnp.where(pl.program_id(0) == 0, 0.0, r[sub - 1:sub, :]) for r in refs[n_r:n_r + n_h]]
        ins += [r[...] for r in refs[n_r + n_h:n_r + n_h + n_c]]
        refs = refs[:n_r] + refs[n_r + n_h:]
        outs = fn(*ins)
        o_refs = refs[n_r + n_c:n_r + n_c + n_o]
        a_refs = refs[n_r + n_c + n_o:]
        for o_ref, val in zip(o_refs, outs[:n_o]):
            o_ref[...] = val.astype(o_ref.dtype)
        if n_a:
            first = pl.program_id(0) == 0

            @pl.when(first)
            def _():
                for a_ref, val in zip(a_refs, outs[n_o:]):
                    a_ref[...] = val

            @pl.when(jnp.logical_not(first))
            def _():
                for a_ref, val in zip(a_refs, outs[n_o:]):
                    a_ref[...] += val

        if n_x:
            @pl.when(pl.program_id(0) == nb - 1)
            def _():
                for j in range(n_x):
                    forward(j)
                wait()

    in_specs = [pl.BlockSpec((tr, w), functools.partial(lambda i, c: (i, c), c=cb)) for _, w, cb in rows]
    if halo:
        in_specs += [pl.BlockSpec((sub, w), functools.partial(lambda i, c: (jnp.maximum(i * (tr // sub) - 1, 0), c), c=cb))
                     for _, w, cb in rows]
    in_specs += [pl.BlockSpec(c.shape, functools.partial(lambda i, n: (0,) * n, n=c.ndim)) for c in consts]
    out_specs = [pl.BlockSpec((tr, w), lambda i: (i, 0)) for w, _ in row_outs]
    out_specs += [pl.BlockSpec(s, functools.partial(lambda i, n: (0,) * n, n=len(s))) for s in acc_outs]
    out_shape = [jax.ShapeDtypeStruct((t_len, w), dt) for w, dt in row_outs]
    out_shape += [jax.ShapeDtypeStruct(s, F32) for s in acc_outs]
    return pl.pallas_call(
        body, name=name, grid=(nb,), in_specs=in_specs + x_specs, out_specs=out_specs + x_specs,
        out_shape=out_shape + x_shapes, scratch_shapes=x_sems,
        compiler_params=pltpu.CompilerParams(dimension_semantics=("arbitrary",), vmem_limit_bytes=VMEM_LIMIT,
                                             has_side_effects=bool(n_x)),
    )(*[r[0] for r in rows], *([r[0] for r in rows] if halo else []), *consts, *gather)


def _colwise(name, fn, n_blocks, cols, prms, col_outs, prm_outs=()):
    t_len = cols[0][0].shape[0]
    n_i = len(cols) + len(prms)

    def body(*refs):
        outs = fn(*[r[...] for r in refs[:n_i]])
        for o_ref, val in zip(refs[n_i:], outs):
            o_ref[...] = val.astype(o_ref.dtype)

    spec = lambda r, w: pl.BlockSpec((r, w), lambda j: (0, j))
    in_specs = [spec(t_len, w) for _, w in cols] + [spec(a.shape[0], LANE) for a in prms]
    out_specs = [spec(t_len, bw) for _, _, bw in col_outs] + [spec(r, LANE) for r, _ in prm_outs]
    out_shape = [jax.ShapeDtypeStruct((t_len, w), dt) for w, dt, _ in col_outs]
    out_shape += [jax.ShapeDtypeStruct((r, w), F32) for r, w in prm_outs]
    return pl.pallas_call(
        body, name=name, grid=(n_blocks,), in_specs=in_specs, out_specs=out_specs, out_shape=out_shape,
        compiler_params=_params(("arbitrary",)),
    )(*[c[0] for c in cols], *prms)


def _matmul(name, a, b, dn, outs, *, tm, tn, tk, extras=(), consts=(), epilogue=None, sums=(), xch=(), xch_scatter=(),
            a_map=None, col_blocks_out=False):
    if dn == "nn":
        (m, k), n = a.shape, b.shape[1]
    elif dn == "nt":
        (m, k), n = a.shape, b.shape[0]
    else:
        (k, m), n = a.shape, b.shape[1]
    tm, tn, tk = min(tm, m), min(tn, n), min(tk, k)
    nk = k // tk
    grid = (m // tm, n // tn, nk)
    assert nk == 1 and (not sums or grid[1] == 1)
    a_spec = pl.BlockSpec((tk, tm), lambda i, j, q: (q, i)) if dn == "tn" else pl.BlockSpec((tm, tk), lambda i, j, q: (i, q))
    b_spec = pl.BlockSpec((tn, tk), lambda i, j, q: (j, q)) if dn == "nt" else pl.BlockSpec((tk, tn), lambda i, j, q: (q, j))
    o_spec = pl.BlockSpec((tm, tn), lambda i, j, q: (i, j))
    c_spec = pl.BlockSpec((1, tn), lambda i, j, q: (0, j))
    n_e, n_c, n_o, n_s, n_x = len(extras), len(consts), len(outs), len(sums), len(xch)
    x_specs, x_shapes, x_sems = _exchange_io(xch, xch_scatter) if n_x else ([], [], [])

    def body(*refs):
        a_ref, b_ref = refs[:2]
        e_refs = refs[2:2 + n_e + n_c]
        x_in = refs[2 + n_e + n_c:2 + n_e + n_c + n_x]
        rest = refs[2 + n_e + n_c + n_x:]
        o_refs, s_refs, x_out, scratch = rest[:n_o], rest[n_o:n_o + n_s], rest[n_o + n_s:n_o + n_s + n_x], rest[n_o + n_s + n_x:]
        step = (pl.program_id(0) * grid[1] + pl.program_id(1)) * nk + pl.program_id(2)
        if n_x:
            start, wait = _exchange_plan(x_in, x_out, xch_scatter, *scratch[len(scratch) - 3:])
            pl.when(step == 0)(start)
        a_blk = a_ref[...] if a_map is None else a_map(a_ref[...])
        acc = lax.dot_general(a_blk.astype(BF16), b_ref[...].astype(BF16), _dims(dn, 2), preferred_element_type=F32)
        vals = (acc,) if epilogue is None else epilogue(acc, *[e[...] for e in e_refs])
        for o_ref, val in zip(o_refs, vals[:n_o]):
            o_ref[...] = val.astype(o_ref.dtype)
        if n_s:
            @pl.when(step == 0)
            def _():
                for s_ref, val in zip(s_refs, vals[n_o:]):
                    s_ref[...] = val

            @pl.when(step > 0)
            def _():
                for s_ref, val in zip(s_refs, vals[n_o:]):
                    s_ref[...] += val

        if n_x:
            pl.when(step == grid[0] * grid[1] * nk - 1)(wait)

    plain = not (n_s or n_x)
    res = pl.pallas_call(
        body, name=name, grid=grid,
        in_specs=[a_spec, b_spec] + [o_spec] * n_e + [c_spec] * n_c + x_specs,
        out_specs=[pl.BlockSpec((None, tm, tn), lambda i, j, q: (j, i, 0)) if col_blocks_out else o_spec] * n_o
                  + [c_spec] * n_s + x_specs,
        out_shape=[jax.ShapeDtypeStruct((n // tn, m, tn) if col_blocks_out else (m, n), dt) for dt in outs] + [jax.ShapeDtypeStruct(s, F32) for s in sums] + x_shapes,
        scratch_shapes=x_sems,
        compiler_params=pltpu.CompilerParams(
            dimension_semantics=("parallel", "parallel", "arbitrary") if plain else ("arbitrary",) * 3,
            vmem_limit_bytes=VMEM_LIMIT, has_side_effects=bool(n_x)),
    )(a, b, *extras, *consts, *xch)
    return res[0] if len(res) == 1 else res


def _rms(h, g):
    return h * lax.rsqrt(jnp.mean(h * h, axis=-1, keepdims=True) + RMS_EPS) * g


def _rms_bwd(h, g, dy):
    rs = lax.rsqrt(jnp.mean(h * h, axis=-1, keepdims=True) + RMS_EPS)
    n = h * rs
    dn = dy * g
    dh = rs * (dn - n * jnp.mean(dn * n, axis=-1, keepdims=True))
    return dh, jnp.sum(dy * n, axis=0, keepdims=True)


def _rwkv_pre(k, xw, xa, xg, w0, a0, k_k, k_a, wl, al, gl):
    zw = w0 + _mm(jnp.tanh(xw), wl, "nn", 1)
    lw = -jnp.exp(-_softplus(-zw) - 0.5)
    iclr = _sigmoid(a0 + _mm(xa, al, "nn", 1))
    g = _mm(_sigmoid(xg), gl, "nn", 1)
    kk0 = k * k_k
    kk = kk0 * lax.rsqrt(jnp.maximum(_hsum(kk0 * kk0), L2_EPS * L2_EPS))
    k_h = k * (1.0 + (iclr - 1.0) * k_a)
    return lw, k_h, -kk, kk * iclr, g


def _rwkv_post(y, r, k_h, v, g, ln_g, ln_b, r_k):
    mu = _hsum(y) * (1.0 / HEAD_DIM)
    yc = y - mu
    var = _hsum(yc * yc) * (1.0 / HEAD_DIM)
    yo = yc * lax.rsqrt(var + GN_EPS) * ln_g + ln_b
    bonus = _hsum(r * k_h * r_k) * v
    return (yo + bonus) * g


def _shift_down(x, n):
    rows = lax.broadcasted_iota(jnp.int32, x.shape, 0)
    return jnp.where(rows < n, 0.0, pltpu.roll(x, n, 0))


def _shift_up(x, n):
    t_len = x.shape[0]
    rows = lax.broadcasted_iota(jnp.int32, x.shape, 0)
    return jnp.where(rows >= t_len - n, 0.0, pltpu.roll(x, t_len - n, 0))


def _exchange_plan(ins, outs, scatter, send_sems, recv_sems, local_sems):
    x, y, c = lax.axis_index("x"), lax.axis_index("y"), lax.axis_index("c")
    me = 4 * x + 2 * y + c

    def local(i):
        return pltpu.make_async_copy(ins[i].at[me] if scatter[i] else ins[i], outs[i].at[me], local_sems.at[i])

    def send(i, rel):
        return pltpu.make_async_remote_copy(
            src_ref=ins[i].at[me ^ rel] if scatter[i] else ins[i], dst_ref=outs[i].at[me],
            send_sem=send_sems.at[i, rel - 1], recv_sem=recv_sems.at[i, rel - 1],
            device_id=(x ^ (rel >> 2), y ^ ((rel >> 1) & 1), c ^ (rel & 1)), device_id_type=pl.DeviceIdType.MESH)

    def landed(i, rel):
        slot = outs[i].at[me ^ rel]
        return pltpu.make_async_remote_copy(
            src_ref=slot, dst_ref=slot, send_sem=send_sems.at[i, rel - 1], recv_sem=recv_sems.at[i, rel - 1],
            device_id=(x, y, c), device_id_type=pl.DeviceIdType.MESH)

    def start():
        for i in range(len(ins)):
            local(i).start()
            for rel in range(1, N_DEV):
                send(i, rel).start()

    def wait():
        for i in range(len(ins)):
            local(i).wait()
            for rel in range(1, N_DEV):
                landed(i, rel).wait_recv()
            for rel in range(1, N_DEV):
                send(i, rel).wait_send()

    return start, wait


def _gather_plan(ins, outs, send_sems, recv_sems, local_sems):
    x, y, c = lax.axis_index("x"), lax.axis_index("y"), lax.axis_index("c")
    me = 4 * x + 2 * y + c
    direct, chips = (1, 2, 4, 6), (2, 4, 6)

    def local(i):
        return pltpu.make_async_copy(ins[i], outs[i].at[me], local_sems.at[i])

    def send(i, rel):
        return pltpu.make_async_remote_copy(
            src_ref=ins[i], dst_ref=outs[i].at[me], send_sem=send_sems.at[i, rel - 1], recv_sem=recv_sems.at[i, rel - 1],
            device_id=(x ^ (rel >> 2), y ^ ((rel >> 1) & 1), c ^ (rel & 1)), device_id_type=pl.DeviceIdType.MESH)

    def passed(i, rel):
        slot = outs[i].at[me ^ rel]
        return pltpu.make_async_remote_copy(
            src_ref=slot, dst_ref=slot, send_sem=send_sems.at[i, rel], recv_sem=recv_sems.at[i, rel],
            device_id=(x, y, 1 - c), device_id_type=pl.DeviceIdType.MESH)

    def landed(i, rel):
        slot = outs[i].at[me ^ rel]
        return pltpu.make_async_remote_copy(
            src_ref=slot, dst_ref=slot, send_sem=send_sems.at[i, rel - 1], recv_sem=recv_sems.at[i, rel - 1],
            device_id=(x, y, c), device_id_type=pl.DeviceIdType.MESH)

    def start():
        for i in range(len(ins)):
            local(i).start()
            for rel in direct:
                send(i, rel).start()

    def forward(i):
        for rel in chips:
            landed(i, rel).wait_recv()
            passed(i, rel).start()

    def wait():
        for i in range(len(ins)):
            local(i).wait()
            for rel in (1, 3, 5, 7):
                landed(i, rel).wait_recv()
            for rel in direct:
                send(i, rel).wait_send()
            for rel in chips:
                passed(i, rel).wait_send()

    return start, forward, wait


def _exchange_io(arrays, scatter):
    n = len(arrays)
    any_spec = pl.BlockSpec(memory_space=pl.ANY)
    out_shape = [jax.ShapeDtypeStruct(a.shape if sc else (N_DEV,) + a.shape, a.dtype) for a, sc in zip(arrays, scatter)]
    sems = [pltpu.SemaphoreType.DMA((n, N_DEV - 1)), pltpu.SemaphoreType.DMA((n, N_DEV - 1)), pltpu.SemaphoreType.DMA((n,))]
    return [any_spec] * n, out_shape, sems


def _exchange(name, arrays, scatter):
    n = len(arrays)
    specs, out_shape, sems = _exchange_io(arrays, scatter)

    def body(*refs):
        if any(scatter):
            start, wait = _exchange_plan(refs[:n], refs[n:2 * n], scatter, *refs[2 * n:])
            start()
        else:
            start, forward, wait = _gather_plan(refs[:n], refs[n:2 * n], *refs[2 * n:])
            start()
            for i in range(n):
                forward(i)
        wait()

    return pl.pallas_call(
        body, name=name, in_specs=specs, out_specs=specs, out_shape=out_shape, scratch_shapes=sems,
        compiler_params=pltpu.CompilerParams(has_side_effects=True),
    )(*arrays)


def _scatter_start(name, arrays, lands):
    n = len(arrays)
    hbm = pl.BlockSpec(memory_space=pltpu.HBM)

    def body(*refs):
        ins, land, send_sems, recv_sems = refs[:n], refs[n:2 * n], refs[2 * n], refs[2 * n + 1]
        token = refs[4 * n + 2]
        x, y, c = lax.axis_index("x"), lax.axis_index("y"), lax.axis_index("c")
        me = 4 * x + 2 * y + c
        for i in range(n):
            for rel in range(1, N_DEV):
                k = i * (N_DEV - 1) + rel - 1
                pltpu.make_async_remote_copy(
                    src_ref=ins[i].at[me ^ rel], dst_ref=land[i].at[me], send_sem=send_sems.at[k],
                    recv_sem=recv_sems.at[k], device_id=(x ^ (rel >> 2), y ^ ((rel >> 1) & 1), c ^ (rel & 1)),
                    device_id_type=pl.DeviceIdType.MESH).start()
        token[...] = jnp.zeros_like(token)

    sem = pltpu.SemaphoreType.DMA((n * (N_DEV - 1),))
    bufs = [pltpu.HBM(a.shape, a.dtype) for a in list(arrays) + list(lands)]
    res = pl.pallas_call(
        body, name=name, out_shape=(sem, sem, *bufs, jax.ShapeDtypeStruct((8, LANE), F32)),
        in_specs=[hbm] * (2 * n),
        out_specs=(pl.BlockSpec(memory_space=pltpu.SEMAPHORE),) * 2 + (hbm,) * (2 * n) + (pl.BlockSpec(memory_space=pltpu.VMEM),),
        input_output_aliases={i: 2 + i for i in range(2 * n)},
        compiler_params=pltpu.CompilerParams(has_side_effects=pltpu.SideEffectType.DATAFLOW_SIDE_EFFECTING),
    )(*[pltpu.with_memory_space_constraint(a, pltpu.HBM) for a in list(arrays) + list(lands)])
    return res[0], res[1], res[2:2 + n], res[2 + n:2 + 2 * n], res[2 + 2 * n]


def _scatter_wait(name, send_sems, recv_sems, arrays, lands, after):
    n, n_after = len(arrays), len(after)
    hbm = pl.BlockSpec(memory_space=pltpu.HBM)

    def body(*refs):
        ins, land, s_sems, r_sems = refs[:n], refs[n:2 * n], refs[2 * n], refs[2 * n + 1]
        x, y, c = lax.axis_index("x"), lax.axis_index("y"), lax.axis_index("c")
        me = 4 * x + 2 * y + c
        for i in range(n):
            for rel in range(1, N_DEV):
                k = i * (N_DEV - 1) + rel - 1
                cp = pltpu.make_async_remote_copy(
                    src_ref=ins[i].at[me ^ rel], dst_ref=land[i].at[me ^ rel], send_sem=s_sems.at[k],
                    recv_sem=r_sems.at[k], device_id=(x, y, c), device_id_type=pl.DeviceIdType.MESH)
                cp.wait_send()
                cp.wait_recv()

    res = pl.pallas_call(
        body, name=name, out_shape=[pltpu.HBM(a.shape, a.dtype) for a in list(arrays) + list(lands)],
        in_specs=[hbm] * (2 * n) + [pl.BlockSpec(memory_space=pltpu.SEMAPHORE)] * 2 + [pl.BlockSpec(memory_space=pl.ANY)] * n_after,
        out_specs=[hbm] * (2 * n), input_output_aliases={i: i for i in range(2 * n)},
        compiler_params=pltpu.CompilerParams(has_side_effects=pltpu.SideEffectType.DATAFLOW_SIDE_EFFECTING),
    )(*arrays, *lands, send_sems, recv_sems, *after)
    return res[:n], res[n:]


def _gather_start(name, shards, after):
    n, n_after = len(shards), len(after)
    hbm = pl.BlockSpec(memory_space=pltpu.HBM)
    lands = [lax.empty((N_DEV,) + a.shape, a.dtype) for a in shards]

    def body(*refs):
        ins, land = refs[:n], refs[n:2 * n]
        outs = refs[2 * n + n_after:]
        send_sems, recv_sems, token = outs[:n], outs[n:2 * n], outs[4 * n]
        x, y, c = lax.axis_index("x"), lax.axis_index("y"), lax.axis_index("c")
        me = 4 * x + 2 * y + c
        for i in range(n):
            for rel in range(1, N_DEV):
                pltpu.make_async_remote_copy(
                    src_ref=ins[i], dst_ref=land[i].at[me], send_sem=send_sems[i].at[rel - 1],
                    recv_sem=recv_sems[i].at[rel - 1], device_id=(x ^ (rel >> 2), y ^ ((rel >> 1) & 1), c ^ (rel & 1)),
                    device_id_type=pl.DeviceIdType.MESH).start()
        token[...] = jnp.zeros_like(token)

    sem = pltpu.SemaphoreType.DMA((N_DEV - 1,))
    bufs = [pltpu.HBM(a.shape, a.dtype) for a in list(shards) + lands]
    res = pl.pallas_call(
        body, name=name, out_shape=(*[sem] * (2 * n), *bufs, jax.ShapeDtypeStruct((8, LANE), F32)),
        in_specs=[hbm] * (2 * n) + [pl.BlockSpec(memory_space=pl.ANY)] * n_after,
        out_specs=(pl.BlockSpec(memory_space=pltpu.SEMAPHORE),) * (2 * n) + (hbm,) * (2 * n)
                  + (pl.BlockSpec(memory_space=pltpu.VMEM),),
        input_output_aliases={i: 2 * n + i for i in range(2 * n)},
        compiler_params=pltpu.CompilerParams(has_side_effects=pltpu.SideEffectType.DATAFLOW_SIDE_EFFECTING),
    )(*[pltpu.with_memory_space_constraint(a, pltpu.HBM) for a in list(shards) + lands], *after)
    return res[:n], res[n:2 * n], res[2 * n:3 * n], res[3 * n:4 * n], res[4 * n]


def _gather_wait(name, send_sems, recv_sems, shards, lands, after):
    n, n_after = len(shards), len(after)
    hbm = pl.BlockSpec(memory_space=pltpu.HBM)

    def body(*refs):
        ins, land = refs[:n], refs[n:2 * n]
        s_sems, r_sems = refs[2 * n:3 * n], refs[3 * n:4 * n]
        x, y, c = lax.axis_index("x"), lax.axis_index("y"), lax.axis_index("c")
        me = 4 * x + 2 * y + c
        for i in range(n):
            for rel in range(1, N_DEV):
                cp = pltpu.make_async_remote_copy(
                    src_ref=ins[i], dst_ref=land[i].at[me ^ rel], send_sem=s_sems[i].at[rel - 1],
                    recv_sem=r_sems[i].at[rel - 1], device_id=(x, y, c), device_id_type=pl.DeviceIdType.MESH)
                cp.wait_send()
                cp.wait_recv()

    res = pl.pallas_call(
        body, name=name, out_shape=[pltpu.HBM(a.shape, a.dtype) for a in list(shards) + list(lands)],
        in_specs=[hbm] * (2 * n) + [pl.BlockSpec(memory_space=pltpu.SEMAPHORE)] * (2 * n)
                 + [pl.BlockSpec(memory_space=pl.ANY)] * n_after,
        out_specs=[hbm] * (2 * n), input_output_aliases={i: i for i in range(2 * n)},
        compiler_params=pltpu.CompilerParams(has_side_effects=pltpu.SideEffectType.DATAFLOW_SIDE_EFFECTING),
    )(*shards, *lands, *send_sems, *recv_sems, *after)
    me = 4 * lax.axis_index("x") + 2 * lax.axis_index("y") + lax.axis_index("c")
    return [lax.dynamic_update_slice_in_dim(land, shard[None], me, axis=0) for shard, land in zip(res[:n], res[n:])]


def _tri_powers(low):
    powers, n, p = [low.astype(BF16)], 1, low
    while 2 * n < low.shape[-1]:
        p = _mm(p, p, "nn", REC_PASSES)
        powers.append(p.astype(BF16))
        n *= 2
    return powers


@jax.custom_vjp
def _tri_solve(low, rhs, powers):
    del low
    for p in powers:
        rhs = rhs + _mm(p, rhs, "nn", REC_PASSES)
    return rhs


def _tri_solve_fwd(low, rhs, powers):
    out = _tri_solve(low, rhs, powers)
    return out, (powers, out)


def _tri_solve_bwd(res, d):
    powers, u = res
    for p in powers:
        d = d + _mm(p, d, "tn", REC_PASSES)
    return _mm(d, u, "nt", REC_PASSES), d, [jnp.zeros_like(p) for p in powers]


_tri_solve.defvjp(_tri_solve_fwd, _tri_solve_bwd)


@jax.custom_vjp
def _tri_solve_given(low, rhs, powers, value):
    del low, rhs, powers
    return value


def _tri_solve_given_fwd(low, rhs, powers, value):
    return value, (powers, value)


def _tri_solve_given_bwd(res, d):
    return _tri_solve_bwd(res, d) + (jnp.zeros_like(res[1]),)


_tri_solve_given.defvjp(_tri_solve_given_fwd, _tri_solve_given_bwd)


def _heads(x):
    return jnp.stack([x[:, h * HEAD_DIM:(h + 1) * HEAD_DIM] for h in range(N_HEADS)])


def _unheads(x):
    return jnp.concatenate([x[h] for h in range(N_HEADS)], axis=-1)


def _causal_masks(c):
    ti = lax.broadcasted_iota(jnp.int32, (c, c), 0)
    si = lax.broadcasted_iota(jnp.int32, (c, c), 1)
    strict, incl = si < ti, si <= ti
    both = jnp.concatenate([jnp.concatenate([strict, strict], axis=1), jnp.concatenate([incl, incl], axis=1)], axis=0)
    return strict, incl, both


@jax.custom_vjp
def _gram_given(x2, y2, value):
    del x2, y2
    return value.astype(F32)


def _gram_given_fwd(x2, y2, value):
    return value.astype(F32), (x2, y2, value)


def _gram_given_bwd(res, d):
    x2, y2, value = res
    d = jnp.where(_causal_masks(d.shape[-1] // 2)[2], d, 0.0)
    return _mm(d, y2, "nn", 2), _mm(d, x2, "tn", 2), jnp.zeros_like(value)


_gram_given.defvjp(_gram_given_fwd, _gram_given_bwd)


def _chunk_fwd(z0, r, lw, k, v, a, b, powers=None, gram_value=None, u_value=None):
    c = r.shape[0]
    n_h, n_k = z0.shape[0], z0.shape[1]
    mm = functools.partial(_mm, passes=REC_PASSES)
    gram = functools.partial(_mm, passes=2)
    _, incl, mask = _causal_masks(c)
    cum = _mm(incl.astype(F32), lw, "nn", 3)
    cum_end = cum[c - 1:c, :]
    e_neg, e_end = jnp.exp(-cum), jnp.exp(cum_end - cum)
    x2 = jnp.concatenate([_heads(a * jnp.exp(cum - lw)), _heads(r * jnp.exp(cum))], axis=1)
    y2 = jnp.concatenate([_heads(b * e_neg), _heads(k * e_neg)], axis=1)
    vh = _heads(v)
    g2 = jnp.where(mask, gram(x2, y2, "nt"), 0.0) if gram_value is None else _gram_given(x2, y2, gram_value)
    t2 = mm(x2, z0, "nn") + mm(g2[:, :, c:], vh, "nn")
    low = g2[:, :c, :c]
    powers = _tri_powers(low) if powers is None else powers
    u = _tri_solve(low, t2[:, :c], powers) if u_value is None else _tri_solve_given(low, t2[:, :c], powers, u_value)
    y = t2[:, c:] + mm(g2[:, c:, :c], u, "nn")
    ki = lax.broadcasted_iota(jnp.int32, (n_k, n_k), 0)
    kj = lax.broadcasted_iota(jnp.int32, (n_k, n_k), 1)
    dmat = jnp.where(ki == kj, jnp.broadcast_to(_heads(jnp.exp(cum_end)), (n_h, n_k, n_k)), 0.0)
    z_end = mm(dmat, z0, "nn") + mm(jnp.concatenate([_heads(b * e_end), _heads(k * e_end)], axis=1),
                                    jnp.concatenate([u, vh], axis=1), "tn")
    return _unheads(y), z_end, powers, g2, u


def _rec_params():
    return pltpu.CompilerParams(dimension_semantics=("arbitrary",), vmem_limit_bytes=VMEM_LIMIT, has_side_effects=True)


def _rec_fwd(u, lw, k, a, b, xch):
    t_len = lw.shape[0]
    c = min(REC_CHUNK, t_len)
    nc = t_len // c
    per = REC_CHUNKS_PER_STEP if nc % REC_CHUNKS_PER_STEP == 0 else 1
    steps = nc // per
    n_x = len(xch)
    x_specs, x_shapes, x_sems = _exchange_io(xch, [False] * n_x) if n_x else ([], [], [])
    n_pow = max(1, (c - 1).bit_length())
    sizes = [a_.size * a_.dtype.itemsize for a_ in xch]
    pass_step = [min(steps - 1, int(0.9 * steps * sum(sizes[:j + 1]) / sum(sizes)) + 1) for j in range(n_x)]

    def body(*refs):
        r_ref, v_ref, lw_ref, k_ref, a_ref, b_ref = refs[:6]
        x_in = refs[6:6 + n_x]
        y_ref, zs_ref, pw_ref, gs_ref, us_ref = refs[6 + n_x:11 + n_x]
        x_out = refs[11 + n_x:11 + 2 * n_x]
        z_scr = refs[11 + 2 * n_x]
        i = pl.program_id(0)
        if n_x:
            start, forward, wait = _gather_plan(x_in, x_out, *refs[12 + 2 * n_x:])
            pl.when(i == 0)(start)

        @pl.when(i == 0)
        def _():
            z_scr[...] = jnp.zeros_like(z_scr)

        for s in range(per):
            rows = pl.ds(s * c, c)
            z0 = z_scr[...]
            zs_ref[s] = z0
            y, z_end, powers, g2, u_rows = _chunk_fwd(z0, r_ref[rows, :], lw_ref[rows, :], k_ref[rows, :], v_ref[rows, :],
                                                      a_ref[rows, :], b_ref[rows, :])
            y_ref[rows, :] = y
            z_scr[...] = z_end
            pw_ref[s] = jnp.concatenate(powers, axis=0)
            gs_ref[s] = g2.astype(BF16)
            us_ref[s] = u_rows

        for j in range(n_x):
            pl.when(i == pass_step[j])(functools.partial(forward, j))
        if n_x:
            pl.when(i == steps - 1)(wait)

    blk = lambda cb: pl.BlockSpec((per * c, RWKV_DIM), functools.partial(lambda i, q: (i, q), q=cb))
    res = pl.pallas_call(
        body, name="rwkv_rec_fwd", grid=(steps,),
        in_specs=[blk(0), blk(2)] + [blk(0)] * 4 + x_specs,
        out_specs=[blk(0), pl.BlockSpec((per, N_HEADS, HEAD_DIM, HEAD_DIM), lambda i: (i, 0, 0, 0)),
                   pl.BlockSpec((per, n_pow * N_HEADS, c, c), lambda i: (i, 0, 0, 0)),
                   pl.BlockSpec((per, N_HEADS, 2 * c, 2 * c), lambda i: (i, 0, 0, 0)),
                   pl.BlockSpec((per, N_HEADS, c, HEAD_DIM), lambda i: (i, 0, 0, 0))] + x_specs,
        out_shape=[jax.ShapeDtypeStruct((t_len, RWKV_DIM), F32),
                   jax.ShapeDtypeStruct((nc, N_HEADS, HEAD_DIM, HEAD_DIM), F32),
                   jax.ShapeDtypeStruct((nc, n_pow * N_HEADS, c, c), BF16),
                   jax.ShapeDtypeStruct((nc, N_HEADS, 2 * c, 2 * c), BF16),
                   jax.ShapeDtypeStruct((nc, N_HEADS, c, HEAD_DIM), F32)] + x_shapes,
        scratch_shapes=[pltpu.VMEM((N_HEADS, HEAD_DIM, HEAD_DIM), F32)] + x_sems,
        compiler_params=_rec_params() if n_x else _params(("arbitrary",)),
    )(u, u, lw, k, a, b, *xch)
    return res[0], res[1:5], res[5:]


def _rec_bwd(u, lw, k, a, b, saved, dy):
    t_len = lw.shape[0]
    c = min(REC_CHUNK, t_len)
    nc = t_len // c
    per = REC_CHUNKS_PER_STEP if nc % REC_CHUNKS_PER_STEP == 0 else 1
    steps = nc // per

    zs, pw, gs, us = saved

    def body(r_ref, v_ref, lw_ref, k_ref, a_ref, b_ref, zs_ref, pw_ref, gs_ref, us_ref, dy_ref, *rest):
        g_refs, dz_scr = rest[:6], rest[6]

        @pl.when(pl.program_id(0) == 0)
        def _():
            dz_scr[...] = jnp.zeros_like(dz_scr)

        for s in reversed(range(per)):
            rows = pl.ds(s * c, c)
            powers = [pw_ref[s, j * N_HEADS:(j + 1) * N_HEADS] for j in range(pw.shape[1] // N_HEADS)]
            chunk = functools.partial(lambda gram, u_val, pws, *xs: _chunk_fwd(*xs, powers=pws, gram_value=gram, u_value=u_val)[:2],
                                      gs_ref[s], us_ref[s], powers)
            _, vjp = jax.vjp(chunk, zs_ref[s], r_ref[rows, :], lw_ref[rows, :], k_ref[rows, :], v_ref[rows, :],
                             a_ref[rows, :], b_ref[rows, :])
            dz0, dr, dlw, dk, dv, da, db = vjp((dy_ref[rows, :], dz_scr[...]))
            for ref, val in zip(g_refs, (dr, dv, dlw, dk, da, db)):
                ref[rows, :] = val
            dz_scr[...] = dz0

    blk = lambda cb: pl.BlockSpec((per * c, RWKV_DIM), functools.partial(lambda i, q: (steps - 1 - i, q), q=cb))
    saved_blk = lambda arr: pl.BlockSpec((per,) + arr.shape[1:], lambda i: (steps - 1 - i, 0, 0, 0))
    return pl.pallas_call(
        body, name="rwkv_rec_bwd", grid=(steps,),
        in_specs=[blk(0), blk(2)] + [blk(0)] * 4 + [saved_blk(zs), saved_blk(pw), saved_blk(gs), saved_blk(us), blk(0)],
        out_specs=[blk(0)] * 6, out_shape=[jax.ShapeDtypeStruct((t_len, RWKV_DIM), F32)] * 6,
        scratch_shapes=[pltpu.VMEM((N_HEADS, HEAD_DIM, HEAD_DIM), F32)], compiler_params=_params(("arbitrary",)),
    )(u, u, lw, k, a, b, zs, pw, gs, us, dy)


_EARLY = ["w_in", "conv_w", "w_lora_up", "a_lora_up", "g_lora_up"]
_LATE = ["w_out", "w_up", "w_down", "w_ple_gate", "w_ple_proj"]
_SHARDED = _EARLY + _LATE
_COL_SHARDED = {"w_in", "conv_w", "w_lora_up", "a_lora_up", "g_lora_up", "w_up", "w_ple_proj"}
_BF16_GATHER = {"w_in", "w_out", "w_up", "w_down", "w_ple_gate", "w_ple_proj"}
_REPLICATED = ["norm_mix_g", "shift_mu", "w0", "a0", "k_k", "k_a", "r_k", "ln_x_g", "ln_x_b", "norm_mlp_g", "norm_ple_g",
               "norm_final_g"]
_WEIGHTS = ["norm_mix_g", "w_in", "conv_w", "shift_mu", "w_lora_up", "w0", "a_lora_up", "a0", "g_lora_up", "k_k", "k_a", "r_k",
            "ln_x_g", "ln_x_b", "w_out", "norm_mlp_g", "w_up", "w_down", "norm_ple_g", "w_ple_gate", "w_ple_proj", "norm_final_g"]


def _unshard(name, g):
    if name in _COL_SHARDED:
        return jnp.moveaxis(g, 0, 1).reshape(g.shape[1], N_DEV * g.shape[2])
    return g.reshape(N_DEV * g.shape[1], g.shape[2])


def _reshard(name, full):
    if name in _COL_SHARDED:
        return jnp.moveaxis(full.reshape(full.shape[0], N_DEV, full.shape[1] // N_DEV), 1, 0)
    return full.reshape(N_DEV, full.shape[0] // N_DEV, full.shape[1])


def _pad_in_cols(a):
    z = lambda n: jnp.zeros(a.shape[:-1] + (n,), a.dtype)
    conv = [a[..., part * CONV_DIM + j * LANE:part * CONV_DIM + (j + 1) * LANE] for j in range(CONV_DIM // LANE) for part in range(3)]
    return jnp.concatenate(conv + [a[..., CONV_COLS:3136], z(64), a[..., 3136:3200], z(64), a[..., 3200:3360], z(96)], axis=-1)


def _unpad_in_cols(a):
    conv = [a[..., (3 * j + part) * LANE:(3 * j + part + 1) * LANE] for part in range(3) for j in range(CONV_DIM // LANE)]
    return jnp.concatenate(conv + [a[..., CONV_COLS:3136], a[..., 3200:3264], a[..., 3328:3488]], axis=-1)


def _assemble_w_in(g):
    n_dev, rows, cols = g.shape

    def body(g_ref, o_ref):
        o_ref[...] = _pad_in_cols(jnp.concatenate([g_ref[d] for d in range(n_dev)], axis=1))

    return pl.pallas_call(
        body, name="w_in_assemble", grid=(rows // ROW_BLOCK,),
        in_specs=[pl.BlockSpec((n_dev, ROW_BLOCK, cols), lambda i: (0, i, 0))],
        out_specs=pl.BlockSpec((ROW_BLOCK, IN_PAD), lambda i: (i, 0)),
        out_shape=jax.ShapeDtypeStruct((rows, IN_PAD), g.dtype), compiler_params=_params(("arbitrary",)),
    )(g)


def _split_w_in_grad(dw):
    rows = dw.shape[0]
    cols = IN_COLS // N_DEV

    def body(d_ref, o_ref):
        full = _unpad_in_cols(d_ref[...])
        for d in range(N_DEV):
            o_ref[d] = full[:, cols * d:cols * (d + 1)]

    return pl.pallas_call(
        body, name="w_in_grad_split", grid=(rows // ROW_BLOCK,),
        in_specs=[pl.BlockSpec((ROW_BLOCK, IN_PAD), lambda i: (i, 0))],
        out_specs=pl.BlockSpec((N_DEV, ROW_BLOCK, cols), lambda i: (0, i, 0)),
        out_shape=jax.ShapeDtypeStruct((N_DEV, rows, cols), dw.dtype), compiler_params=_params(("arbitrary",)),
    )(dw)


def _pad_rows(a, rows):
    return jnp.concatenate([a, jnp.zeros((rows - a.shape[0],) + a.shape[1:], a.dtype)], axis=0)


SEG_W = [RWKV_DIM, RWKV_DIM, RWKV_DIM, LANE, LANE, 2 * LANE]
SEG_OFF = [0, 512, 1024, XW_OFF, XA_OFF, XG_OFF]


def _rwkv_pre_bwd(proj, u, grads, mu, small, dproj):
    t_len = u.shape[0]
    tr = min(ROW_BLOCK, t_len)
    nb = t_len // tr
    sub = 8
    n_g = len(grads)
    acc_shapes = [(1, RW_PAD)] + [(1, RWKV_DIM)] * 4 + [(LANE, RWKV_DIM), (LANE, RWKV_DIM), (2 * LANE, RWKV_DIM)]

    def body(*refs):
        seg_refs, halo_refs = refs[:6], refs[6:12]
        k_ref, xw_ref, xa_ref, xg_ref = refs[12:16]
        g_refs = refs[16:16 + n_g]
        mu_ref = refs[16 + n_g]
        prm_refs = refs[17 + n_g:24 + n_g]
        out_hbm = refs[25 + n_g]
        acc_refs = refs[26 + n_g:26 + n_g + len(acc_shapes)]
        vbuf, sems, carry = refs[26 + n_g + len(acc_shapes):]
        i = pl.program_id(0)
        blk = nb - 1 - i
        dr1, dr2, dv1, dv2, dlw, dk1, dk2, da, db, dg = [g[...] for g in g_refs]
        _, vjp = jax.vjp(_rwkv_pre, k_ref[...], xw_ref[...], xa_ref[...], xg_ref[...], *[p_[...] for p_ in prm_refs])
        dk, dxw, dxa, dxg, *dprm = vjp((dlw, dk1 + dk2, da, db, dg))
        du = jnp.concatenate([dr1 + dr2, dk, dv1 + dv2, dxw, dxa, dxg], axis=1)
        mu_v = mu_ref[...]

        @pl.when(i == 0)
        def _():
            carry[...] = jnp.zeros_like(carry)

        rows = lax.broadcasted_iota(jnp.int32, du.shape, 0)
        nxt = jnp.where(rows == tr - 1, carry[...], pltpu.roll(du, tr - 1, 0))
        d_rw = du - mu_v * du + mu_v * nxt
        d_mu = []
        for s_ref, h_ref, off, wd in zip(seg_refs, halo_refs, SEG_OFF, SEG_W):
            cur = s_ref[...]
            r0 = lax.broadcasted_iota(jnp.int32, cur.shape, 0)
            prev = jnp.where(r0 == 0, jnp.where(blk == 0, 0.0, h_ref[sub - 1:sub, :]), pltpu.roll(cur, 1, 0))
            d_mu.append(jnp.sum(du[:, off:off + wd] * (prev - cur), axis=0, keepdims=True))
        sums = [jnp.concatenate(d_mu, axis=1)] + list(dprm)

        @pl.when(i == 0)
        def _():
            for a_ref, val in zip(acc_refs, sums):
                a_ref[...] = val

        @pl.when(i > 0)
        def _():
            for a_ref, val in zip(acc_refs, sums):
                a_ref[...] += val

        carry[...] = du[0:1, :]
        slot = i % 2

        def writeback(s, b):
            return pltpu.make_async_copy(vbuf.at[s], out_hbm.at[pl.ds(b * tr, tr), pl.ds(CONV_COLS, RW_PAD)], sems.at[s])

        @pl.when(i >= 2)
        def _():
            writeback(slot, blk + 2).wait()

        vbuf[slot] = d_rw.astype(vbuf.dtype)
        writeback(slot, blk).start()

        @pl.when(i == nb - 1)
        def _():
            writeback(slot, blk).wait()
            if nb > 1:
                writeback(1 - slot, blk + 1).wait()

    rev = lambda w_, cb: pl.BlockSpec((tr, w_), functools.partial(lambda i, c: (nb - 1 - i, c), c=cb))
    halo = lambda w_, cb: pl.BlockSpec((sub, w_), functools.partial(
        lambda i, c: (jnp.maximum((nb - 1 - i) * (tr // sub) - 1, 0), c), c=cb))
    whole = lambda a: pl.BlockSpec(a.shape, functools.partial(lambda i, n: (0,) * n, n=a.ndim))
    segs = [(wd, (CONV_COLS + off) // wd) for off, wd in zip(SEG_OFF, SEG_W)]
    u_cols = [(512, 1), (LANE, XW_OFF // LANE), (LANE, XA_OFF // LANE), (2 * LANE, XG_OFF // (2 * LANE))]
    any_spec = pl.BlockSpec(memory_space=pl.ANY)
    res = pl.pallas_call(
        body, name="rwkv_pre_bwd", grid=(nb,),
        in_specs=[rev(*s) for s in segs] + [halo(*s) for s in segs] + [rev(*c) for c in u_cols]
                 + [rev(RWKV_DIM, 0)] * n_g + [whole(mu)] + [whole(p_) for p_ in small] + [any_spec],
        out_specs=[any_spec] + [pl.BlockSpec(s, functools.partial(lambda i, n: (0,) * n, n=len(s))) for s in acc_shapes],
        out_shape=[jax.ShapeDtypeStruct(dproj.shape, dproj.dtype)] + [jax.ShapeDtypeStruct(s, F32) for s in acc_shapes],
        scratch_shapes=[pltpu.VMEM((2, tr, RW_PAD), dproj.dtype), pltpu.SemaphoreType.DMA((2,)), pltpu.VMEM((1, RW_PAD), F32)],
        input_output_aliases={24 + n_g: 0},
        compiler_params=_params(("arbitrary",)),
    )(*[proj] * 12, *[u] * 4, *grads, mu, *small, dproj)
    return res


def _local_step(x, p, tgt, w, early_shards, late_shards):
    row = lambda v: v.reshape(1, -1)
    w = dict(w)

    xn1, *gathered = _rowwise("rms_mix", lambda h, g: (_rms(h, g),), [x], [w["norm_mix_g"]], [(D_MODEL, BF16)],
                              gather=early_shards)
    w.update({n: _unshard(n, g_) for n, g_ in zip(_EARLY[1:], gathered[1:])})
    w["w_in"] = _assemble_w_in(gathered[0])
    w["w_lora_up"] = _pad_rows(w["w_lora_up"], LANE)
    w["a_lora_up"] = _pad_rows(w["a_lora_up"], LANE)
    w["g_lora_up"] = _pad_rows(w["g_lora_up"], 2 * LANE)
    lg_send, lg_recv, lg_shards, lg_lands, lg_token = _gather_start("late_gather_start", late_shards, after=[xn1])
    w["shift_mu"] = w["shift_mu"] + lg_token[0:1, 0:1]
    proj = _matmul("in_proj", xn1, w["w_in"], "nn", [F32], tm=2048, tn=512, tk=D_MODEL)
    n_cb = CONV_DIM // LANE

    def conv_fwd(blk, cw):
        gb, gc, hx = blk[:, :LANE], blk[:, LANE:2 * LANE], blk[:, 2 * LANE:]
        uu = gc * hx
        return (gb * (uu * cw[2:3] + _shift_down(uu, 1) * cw[1:2] + _shift_down(uu, 2) * cw[0:1]),)

    (y_conv,) = _colwise("conv_fwd", conv_fwd, n_cb, [(proj, 3 * LANE)], [w["conv_w"]], [(CONV_DIM, BF16, LANE)])

    small = [w["w0"], w["a0"], w["k_k"], w["k_a"], w["w_lora_up"], w["a_lora_up"], w["g_lora_up"]]
    def pre_fwd(*xs):
        cur, prev_rows, mu, prm = xs[:6], xs[6:12], xs[12], xs[13:]
        segs = []
        for c_, p_, off, wd in zip(cur, prev_rows, SEG_OFF, SEG_W):
            rows = lax.broadcasted_iota(jnp.int32, c_.shape, 0)
            prev = jnp.where(rows == 0, p_, pltpu.roll(c_, 1, 0))
            segs.append(c_ + mu[:, off:off + wd] * (prev - c_))
        return (jnp.concatenate(segs, axis=1),) + tuple(_rwkv_pre(segs[1], segs[3], segs[4], segs[5], *prm))

    proj_segs = [(proj, wd, (CONV_COLS + off) // wd) for off, wd in zip(SEG_OFF, SEG_W)]
    u, lw, k_h, ra, rb, g = _rowwise(
        "rwkv_pre", pre_fwd, proj_segs, [w["shift_mu"]] + small, [(RW_PAD, F32)] + [(RWKV_DIM, F32)] * 5, halo=True)
    y_rec, rec_saved, _ = _rec_fwd(u, lw, k_h, ra, rb, [])

    def late_weight(names, after):
        idx = [_LATE.index(n) for n in names]
        got = _gather_wait("late_gather_wait_" + names[0], [lg_send[i] for i in idx], [lg_recv[i] for i in idx],
                           [lg_shards[i] for i in idx], [lg_lands[i] for i in idx], after)
        return [_unshard(n, g_) for n, g_ in zip(names, got)]

    w["w_out"], w["w_up"], w["w_down"] = late_weight(["w_out", "w_up", "w_down"], [y_rec])
    post_c = [w["ln_x_g"], w["ln_x_b"], w["r_k"]]
    u_r, u_v = (u, 512, 0), (u, 512, 2)
    (y_rwkv,) = _rowwise("rwkv_post", lambda *xs: (_rwkv_post(*xs),), [y_rec, u_r, k_h, u_v, g], post_c, [(RWKV_DIM, BF16)],
                         tr=2 * ROW_BLOCK)
    ycat = jnp.concatenate([y_conv, y_rwkv], axis=1)
    def res_norm(acc, r_, g_):
        h = acc + r_
        return h, _rms(h, g_)

    h1, xn2 = _matmul("out_proj", ycat, w["w_out"], "nn", [F32, BF16], tm=1024, tn=D_MODEL, tk=D_MODEL, extras=[x],
                      consts=[w["norm_mlp_g"]], epilogue=res_norm)

    square = lambda h: h.astype(F32) * h.astype(F32)
    hid = _matmul("mlp_up", xn2, w["w_up"], "nn", [BF16], tm=2048, tn=1024, tk=D_MODEL,
                  epilogue=lambda acc: (jnp.maximum(acc, 0.0),))
    h2, xn3 = _matmul("mlp_down", hid, w["w_down"], "nn", [F32, BF16], tm=512, tn=D_MODEL, tk=D_FF, extras=[h1],
                      consts=[w["norm_ple_g"]], epilogue=res_norm, a_map=square)
    w["w_ple_gate"], w["w_ple_proj"] = late_weight(["w_ple_gate", "w_ple_proj"], [xn3])
    zg =_matmul("ple_gate", xn3, w["w_ple_gate"], "nn", [F32], tm=1024, tn=1024, tk=D_MODEL)
    pp = _matmul("ple_proj", p, w["w_ple_proj"], "nn", [F32], tm=1024, tn=1024, tk=PLE_DIM)

    def head(h2_, zg_, pp_, tg, gf):
        gate = _sigmoid(zg_)
        h3 = h2_ + gate * pp_
        out = _rms(h3, gf)
        err = out - tg
        dh3, dgf = _rms_bwd(h3, gf, err * (1.0 / D_MODEL))
        loss = jnp.sum(jnp.sum(err * err, axis=1, keepdims=True), axis=0, keepdims=True) * (0.5 / D_MODEL)
        return dh3, dh3 * pp_ * gate * (1.0 - gate), dh3 * gate, dgf, loss

    dh3, dzg, dpp, d_norm_final, loss = _rowwise(
        "head", head, [h2, zg, pp, tgt], [row(w["norm_final_g"])], [(D_MODEL, F32), (D_MODEL, BF16), (D_MODEL, BF16)],
        [(1, D_MODEL), (1, 1)], tr=2 * ROW_BLOCK)

    d_w_ple_proj = _matmul("d_ple_proj", p, dpp, "tn", [BF16], tm=PLE_DIM, tn=D_MODEL // N_DEV, tk=4096, col_blocks_out=True)
    d_w_ple_gate = _matmul("d_ple_gate", xn3, dzg, "tn", [BF16], tm=512, tn=1024, tk=4096)

    def norm_bwd(dxn, h, dres, g_):
        dh, dg = _rms_bwd(h, g_, dxn)
        dh = dh + dres
        return dh, dh, dg

    nb = dict(tm=512, tn=D_MODEL, epilogue=norm_bwd, sums=[(1, D_MODEL)])
    dh2, dh2_b, d_norm_ple = _matmul("dx_ple_gate", dzg, w["w_ple_gate"], "nt", [F32, BF16], tk=D_MODEL,
                                     extras=[h2, dh3], consts=[w["norm_ple_g"]], **nb)
    d_w_down = _matmul("d_mlp_down", hid, dh2_b, "tn", [BF16], tm=512, tn=1024, tk=4096, a_map=square)
    dpre = _matmul("dx_mlp_down", dh2_b, w["w_down"], "nt", [BF16], tm=2048, tn=1024, tk=D_MODEL, extras=[hid],
                   epilogue=lambda acc, hid_: (acc * (2.0 * hid_.astype(F32)),))
    d_w_up = _matmul("d_mlp_up", xn2, dpre, "tn", [BF16], tm=1024, tn=D_FF // N_DEV, tk=4096, col_blocks_out=True)
    dh1, dh1_b, d_norm_mlp = _matmul("dx_mlp_up", dpre, w["w_up"], "nt", [F32, BF16], tk=D_FF,
                                     extras=[h1, dh2], consts=[w["norm_mlp_g"]], **nb)
    d_w_out = _matmul("d_out_proj", ycat, dh1_b, "tn", [BF16], tm=512, tn=1024, tk=4096)
    dycat = _matmul("dx_out_proj", dh1_b, w["w_out"], "nt", [F32], tm=1024, tn=1024, tk=D_MODEL)
    late_grads = dict(w_out=d_w_out, w_up=d_w_up, w_down=d_w_down, w_ple_gate=d_w_ple_gate, w_ple_proj=d_w_ple_proj)
    late_send = [late_grads[n] if n in ("w_up", "w_ple_proj") else _reshard(n, late_grads[n]) for n in _LATE]
    *late_flight, late_token = _scatter_start("late_scatter_start", late_send, [lax.empty(a.shape, a.dtype) for a in late_send])
    conv_w_bwd = w["conv_w"] + late_token[0:1, 0:1]

    def conv_bwd(dy, blk, cw):
        gb, gc, hx = blk[:, :LANE], blk[:, LANE:2 * LANE], blk[:, 2 * LANE:]
        uu = gc * hx
        u1, u2 = _shift_down(uu, 1), _shift_down(uu, 2)
        dconv = dy * gb
        du = dconv * cw[2:3] + _shift_up(dconv, 1) * cw[1:2] + _shift_up(dconv, 2) * cw[0:1]
        s = lambda z: jnp.sum(z, axis=0, keepdims=True)
        d_blk = jnp.concatenate([dy * (uu * cw[2:3] + u1 * cw[1:2] + u2 * cw[0:1]), du * hx, du * gc], axis=1)
        return d_blk, s(dconv * u2), s(dconv * u1), s(dconv * uu)

    dproj, dcw0, dcw1, dcw2 = _colwise(
        "conv_bwd", conv_bwd, n_cb, [(dycat, LANE), (proj, 3 * LANE)], [conv_w_bwd],
        [(IN_PAD, BF16, 3 * LANE)], [(1, CONV_DIM)] * 3)

    def post_bwd(dy, y, r, k_h_, v, g_, ln_g, ln_b, r_k):
        _, vjp = jax.vjp(_rwkv_post, y, r, k_h_, v, g_, ln_g, ln_b, r_k)
        return vjp(dy)

    dy_rec, dr_p, dk_p, dv_p, dg, d_ln_g, d_ln_b, d_r_k = _rowwise(
        "rwkv_post_bwd", post_bwd, [(dycat, 512, 1), y_rec, u_r, k_h, u_v, g], post_c,
        [(RWKV_DIM, F32)] * 5, [(1, RWKV_DIM)] * 3)
    dr_r, dv_r, dlw, dk_r, da, db = _rec_bwd(u, lw, k_h, ra, rb, rec_saved, dy_rec)

    dproj, d_mu, d_w0, d_a0, d_k_k, d_k_a, d_wl, d_al, d_gl = _rwkv_pre_bwd(
        proj, u, [dr_p, dr_r, dv_p, dv_r, dlw, dk_p, dk_r, da, db, dg], w["shift_mu"], small, dproj)
    d_w_in = _matmul("d_in_proj", xn1, dproj, "tn", [BF16], tm=1024, tn=896, tk=4096)
    early_grads = dict(conv_w=jnp.concatenate([dcw0, dcw1, dcw2], axis=0),
                       w_lora_up=d_wl[:64], a_lora_up=d_al[:64], g_lora_up=d_gl[:160])
    early_send = [_split_w_in_grad(d_w_in)] + [_reshard(n, early_grads[n]) for n in _EARLY[1:]]
    *early_flight, token = _scatter_start("early_scatter_start", early_send, [lax.empty(a.shape, a.dtype) for a in early_send])
    dx, d_norm_mix = _matmul(
        "dx_in_proj", dproj, w["w_in"], "nt", [F32], tk=IN_PAD, extras=[x, dh1], consts=[w["norm_mix_g"] + token[0:1, 0:1]],
        **dict(nb, epilogue=lambda *a: norm_bwd(*a)[1:]))

    grads = dict(
        norm_mix_g=d_norm_mix, shift_mu=d_mu, w0=d_w0, a0=d_a0, k_k=d_k_k, k_a=d_k_a, r_k=d_r_k,
        ln_x_g=d_ln_g, ln_x_b=d_ln_b, norm_mlp_g=d_norm_mlp, norm_ple_g=d_norm_ple, norm_final_g=d_norm_final)
    return loss, dx, grads, late_flight, early_flight, d_w_in


def _adam_update(partials, w_ref, m_ref, v_ref, g_ref, d_ref, nm_ref, nv_ref):
    g = partials[0].astype(F32)
    for part in partials[1:]:
        g = g + part.astype(F32)
    nm =ADAM_B1 * m_ref[...] + (1.0 - ADAM_B1) * g
    nv = ADAM_B2 * v_ref[...] + (1.0 - ADAM_B2) * (g * g)
    m_hat = nm / (1.0 - ADAM_B1 ** ADAM_STEP)
    v_hat = nv / (1.0 - ADAM_B2 ** ADAM_STEP)
    g_ref[...] = g
    d_ref[...] = -ADAM_LR * (m_hat / (jnp.sqrt(v_hat) + ADAM_EPS) + ADAM_WD * w_ref[...])
    nm_ref[...] = nm
    nv_ref[...] = nv


SMALL_ROWS = 8


def _small_layout(widths):
    widths = list(widths) + [1]
    fill, place = [0] * SMALL_ROWS, [None] * len(widths)
    for j in sorted(range(len(widths)), key=lambda q: -widths[q]):
        row = fill.index(min(fill))
        place[j] = (row, fill[row])
        fill[row] += -(-widths[j] // LANE) * LANE
    return place, max(fill)


def _pack_small(vecs, loss):
    place, total = _small_layout([v_.shape[1] for v_ in vecs])
    n = len(vecs)

    def body(*refs):
        out = jnp.zeros((SMALL_ROWS, total), F32)
        row_id = lax.broadcasted_iota(jnp.int32, (SMALL_ROWS, total), 0)
        for row in range(SMALL_ROWS):
            mine = sorted((off, j) for j, (r_, off) in enumerate(place) if r_ == row)
            pieces, at = [], 0
            for off, j in mine:
                val = refs[j][...]
                pieces.append(val)
                at = off + val.shape[1]
                pad = -val.shape[1] % LANE
                if pad:
                    pieces.append(jnp.zeros((1, pad), F32))
                    at += pad
            if total > at:
                pieces.append(jnp.zeros((1, total - at), F32))
            out = jnp.where(row_id == row, jnp.broadcast_to(jnp.concatenate(pieces, axis=1), (SMALL_ROWS, total)), out)
        refs[n + 1][...] = out

    return pl.pallas_call(body, name="pack_small", out_shape=jax.ShapeDtypeStruct((SMALL_ROWS, total), F32))(*vecs, loss)


def _adamw_small(packed, ws, ms, vs):
    n = len(ws)
    place, _ = _small_layout([w_.shape[1] for w_ in ws])

    def body(p_ref, *refs):
        w_refs, m_refs, v_refs, outs = refs[:n], refs[n:2 * n], refs[2 * n:3 * n], refs[3 * n:]
        for j in range(n):
            row, off = place[j]
            cols = pl.ds(off, ws[j].shape[1])
            _adam_update([p_ref[s, row:row + 1, cols] for s in range(N_DEV)], w_refs[j], m_refs[j], v_refs[j],
                         *outs[4 * j:4 * j + 4])
        row, off = place[n]
        total = p_ref[0, row:row + 1, off:off + 1]
        for s in range(1, N_DEV):
            total = total + p_ref[s, row:row + 1, off:off + 1]
        outs[4 * n][...] = total

    res = pl.pallas_call(
        body, name="adamw_small",
        out_shape=[jax.ShapeDtypeStruct(w_.shape, F32) for w_ in ws for _ in range(4)] + [jax.ShapeDtypeStruct((1, 1), F32)],
    )(packed, *ws, *ms, *vs)
    return [res[4 * j:4 * j + 4] for j in range(n)], res[4 * n]


def _adamw(name, parts, w, m, v, own=None, me=None):
    rows, cols = w.shape[-2:]
    lead = w.ndim - 2
    tr = rows if rows * cols * 4 * 8 <= (4 << 20) else max(8, (4 << 20) // (cols * 4 * 8) // 8 * 8)
    while rows % tr:
        tr -= 8
    shape4 = [jax.ShapeDtypeStruct(w.shape, F32)] * 4
    if own is None:
        def body(p_ref, *refs):
            _adam_update([p_ref[s] for s in range(N_DEV)], *refs)

        blk = pl.BlockSpec((None,) * lead + (tr, cols), lambda i: (0,) * lead + (i, 0))
        return pl.pallas_call(
            body, name=name, grid=(rows // tr,),
            in_specs=[pl.BlockSpec((N_DEV, tr, cols), lambda i: (0, i, 0)), blk, blk, blk], out_specs=[blk] * 4,
            out_shape=shape4, compiler_params=_params(("arbitrary",)),
        )(parts, w, m, v)

    def body_own(me_ref, p_ref, own_ref, *refs):
        mine = own_ref[...]
        _adam_update([jnp.where(me_ref[0] == s, mine, p_ref[s]) for s in range(N_DEV)], *refs)

    blk = pl.BlockSpec((None,) * lead + (tr, cols), lambda i, me_ref: (0,) * lead + (i, 0))
    return pl.pallas_call(
        body_own, name=name, out_shape=shape4,
        grid_spec=pltpu.PrefetchScalarGridSpec(
            num_scalar_prefetch=1, grid=(rows // tr,),
            in_specs=[pl.BlockSpec((N_DEV, tr, cols), lambda i, me_ref: (0, i, 0)),
                      pl.BlockSpec((None, tr, cols), lambda i, me_ref: (me_ref[0], i, 0)), blk, blk, blk],
            out_specs=[blk] * 4),
        compiler_params=_params(("arbitrary",)),
    )(me, parts, own, w, m, v)


def kernel(x, p, norm_mix_g, w_in, conv_w, shift_mu, w_lora_up, w0, a_lora_up, a0, g_lora_up, k_k, k_a, r_k, ln_x_g, ln_x_b, w_out, norm_mlp_g, w_up, w_down, norm_ple_g, w_ple_gate, w_ple_proj, norm_final_g, loss_target, m_norm_mix_g, m_w_in, m_conv_w, m_shift_mu, m_w_lora_up, m_w0, m_a_lora_up, m_a0, m_g_lora_up, m_k_k, m_k_a, m_r_k, m_ln_x_g, m_ln_x_b, m_w_out, m_norm_mlp_g, m_w_up, m_w_down, m_norm_ple_g, m_w_ple_gate, m_w_ple_proj, m_norm_final_g, v_norm_mix_g, v_w_in, v_conv_w, v_shift_mu, v_w_lora_up, v_w0, v_a_lora_up, v_a0, v_g_lora_up, v_k_k, v_k_a, v_r_k, v_ln_x_g, v_ln_x_b, v_w_out, v_norm_mlp_g, v_w_up, v_w_down, v_norm_ple_g, v_w_ple_gate, v_w_ple_proj, v_norm_final_g):
    args = dict(locals())
    wts = {n: args[n] for n in _WEIGHTS}
    mom = {n: args["m_" + n] for n in _WEIGHTS}
    var = {n: args["v_" + n] for n in _WEIGHTS}
    shard2d = lambda a: a.reshape(a.shape[-2:])
    pad_mu = lambda a: _pad_in_cols(jnp.concatenate([jnp.zeros((1, CONV_COLS), F32), a], axis=1))[:, CONV_COLS:]
    unpad_mu = lambda a: _unpad_in_cols(jnp.concatenate([jnp.zeros((1, CONV_COLS), F32), a], axis=1))[:, CONV_COLS:]

    shards = {n: shard2d(wts[n]).astype(BF16 if n in _BF16_GATHER else F32) for n in _SHARDED}
    w = {n: wts[n].reshape(1, -1) for n in _REPLICATED}
    w["shift_mu"] = pad_mu(wts["shift_mu"])

    loss, dx, grads, late_flight, early_flight, d_w_in = _local_step(
        x[0], p[0, 0], loss_target[0], w, [shards[n] for n in _EARLY], [shards[n] for n in _LATE])

    me = (4 * lax.axis_index("x") + 2 * lax.axis_index("y") + lax.axis_index("c")).astype(jnp.int32).reshape(1)
    late_sent, late_parts = _scatter_wait("late_scatter_wait", *late_flight, after=[d_w_in])
    out = {n: _adamw("adamw_" + n, prt, wts[n], mom[n], var[n], own=own, me=me)
           for n, prt, own in zip(_LATE, late_parts, late_sent)}
    early_sent, early_parts = _scatter_wait("early_scatter_wait", *early_flight, after=[dx] + [out[n][1] for n in _LATE])
    for n, prt, own in zip(_EARLY, early_parts, early_sent):
        out[n] = _adamw("adamw_" + n, prt, wts[n], mom[n], var[n], own=own, me=me)

    grads["shift_mu"] = unpad_mu(grads["shift_mu"])
    flat = lambda a: a.reshape(1, -1)
    (small_parts,) = _exchange("gather_small", [_pack_small([flat(grads[n]) for n in _REPLICATED], loss)], [False])
    small, loss_total = _adamw_small(small_parts, *[[flat(d[n]) for n in _REPLICATED] for d in (wts, mom, var)])
    for n, res in zip(_REPLICATED, small):
        out[n] = [r.reshape(wts[n].shape) for r in res]
    return (loss_total[0, 0], dx[None], *[out[n][0] for n in _WEIGHTS], *[out[n][1] for n in _WEIGHTS],
            *[out[n][2] for n in _WEIGHTS], *[out[n][3] for n in _WEIGHTS])
```

```python
import functools

import jax
import jax.numpy as jnp
from jax import lax
from jax.experimental import pallas as pl
from jax.experimental.pallas import tpu as pltpu

F32 = jnp.float32
BF16 = jnp.bfloat16

N_DEV = 8
D_MODEL = 1024
CONV_DIM = 512
RWKV_DIM = 512
HEAD_DIM = 64
N_HEADS = 8
D_FF = 4096
PLE_DIM = 256
RMS_EPS = 1e-6
GN_EPS = 64e-5
L2_EPS = 1e-12
ADAM_LR, ADAM_B1, ADAM_B2, ADAM_EPS, ADAM_WD, ADAM_STEP = 0.001, 0.9, 0.999, 1e-08, 0.01, 10

CONV_COLS = 3 * CONV_DIM
RW_PAD = 2048
IN_PAD = CONV_COLS + RW_PAD
IN_COLS = 3360
XW_OFF, XA_OFF, XG_OFF = 1536, 1664, 1792
REC_CHUNK = 128
REC_CHUNKS_PER_STEP = 2
REC_PASSES = 1
ROW_BLOCK = 256
LANE = 128
VMEM_LIMIT = 56 * 1024 * 1024


def _dims(dn, ndim):
    if ndim == 3:
        return {"nn": (((2,), (1,)), ((0,), (0,))), "nt": (((2,), (2,)), ((0,), (0,))),
                "tn": (((1,), (1,)), ((0,), (0,)))}[dn]
    return {"nn": (((1,), (0,)), ((), ())), "nt": (((1,), (1,)), ((), ())), "tn": (((0,), (0,)), ((), ()))}[dn]


def _split2(x):
    hi = x.astype(BF16)
    return hi, (x - hi.astype(F32)).astype(BF16)


def _mm_raw(x, y, dn, passes):
    f = lambda p, q: lax.dot_general(p, q, _dims(dn, x.ndim), preferred_element_type=F32)
    if passes == 1:
        return f(x.astype(BF16), y.astype(BF16))
    xh, xl = _split2(x)
    yh, yl = _split2(y)
    if passes == 2:
        return f(xh, yh) + f(xh, yl)
    return f(xh, yh) + f(xh, yl) + f(xl, yh)


@functools.partial(jax.custom_vjp, nondiff_argnums=(2, 3))
def _mm(x, y, dn, passes):
    return _mm_raw(x, y, dn, passes)


def _mm_fwd(x, y, dn, passes):
    return _mm_raw(x, y, dn, passes), (x, y)


def _mm_bwd(dn, passes, res, d):
    x, y = res
    if dn == "nn":
        return _mm(d, y, "nt", passes), _mm(x, d, "tn", passes)
    if dn == "nt":
        return _mm(d, y, "nn", passes), _mm(d, x, "tn", passes)
    return _mm(y, d, "nt", passes), _mm(x, d, "nn", passes)


_mm.defvjp(_mm_fwd, _mm_bwd)


def _head_ones():
    i = lax.broadcasted_iota(jnp.int32, (RWKV_DIM, RWKV_DIM), 0) // HEAD_DIM
    j = lax.broadcasted_iota(jnp.int32, (RWKV_DIM, RWKV_DIM), 1) // HEAD_DIM
    return (i == j).astype(BF16)


def _hsum_raw(x):
    ones = _head_ones()
    f = lambda p: lax.dot_general(p, ones, _dims("nn", 2), preferred_element_type=F32)
    x1, x2 = _split2(x)
    return f(x1) + f(x2)


@jax.custom_vjp
def _hsum(x):
    return _hsum_raw(x)


_hsum.defvjp(lambda x: (_hsum_raw(x), None), lambda _, d: (_hsum(d),))


def _sigmoid(x):
    return 0.5 + 0.5 * jnp.tanh(0.5 * x)


def _softplus(x):
    return jnp.maximum(x, 0.0) + jnp.log(1.0 + jnp.exp(-jnp.abs(x)))


def _params(sem):
    return pltpu.CompilerParams(dimension_semantics=sem, vmem_limit_bytes=VMEM_LIMIT)


def _rowwise(name, fn, rows, consts, row_outs, acc_outs=(), tr=ROW_BLOCK, halo=False, gather=()):
    rows = [r if isinstance(r, tuple) else (r, r.shape[1], 0) for r in rows]
    t_len = rows[0][0].shape[0]
    tr = min(tr, t_len)
    n_r, n_c, n_o, n_a, n_x = len(rows), len(consts), len(row_outs), len(acc_outs), len(gather)
    n_h = n_r if halo else 0
    sub = 8
    x_specs, x_shapes, x_sems = _exchange_io(gather, [False] * n_x) if n_x else ([], [], [])
    nb = t_len // tr

    def body(*refs):
        if n_x:
            n_in = n_r + n_h + n_c
            start, forward, wait = _gather_plan(refs[n_in:n_in + n_x], refs[len(refs) - 3 - n_x:len(refs) - 3], *refs[len(refs) - 3:])
            pl.when(pl.program_id(0) == 0)(start)
            refs = refs[:n_in] + refs[n_in + n_x:len(refs) - 3 - n_x]
        ins = [r[...] for r in refs[:n_r]]
        ins += [jnp.where(pl.program_id(0) == 0, 0.0, r[sub - 1:sub, :]) for r in refs[n_r:n_r + n_h]]
        ins += [r[...] for r in refs[n_r + n_h:n_r + n_h + n_c]]
        refs = refs[:n_r] + refs[n_r + n_h:]
        outs = fn(*ins)
        o_refs = refs[n_r + n_c:n_r + n_c + n_o]
        a_refs = refs[n_r + n_c + n_o:]
        for o_ref, val in zip(o_refs, outs[:n_o]):
            o_ref[...] = val.astype(o_ref.dtype)
        if n_a:
            first = pl.program_id(0) == 0

            @pl.when(first)
            def _():
                for a_ref, val in zip(a_refs, outs[n_o:]):
                    a_ref[...] = val

            @pl.when(jnp.logical_not(first))
            def _():
                for a_ref, val in zip(a_refs, outs[n_o:]):
                    a_ref[...] += val

        if n_x:
            @pl.when(pl.program_id(0) == nb - 1)
            def _():
                for j in range(n_x):
                    forward(j)
                wait()

    in_specs = [pl.BlockSpec((tr, w), functools.partial(lambda i, c: (i, c), c=cb)) for _, w, cb in rows]
    if halo:
        in_specs += [pl.BlockSpec((sub, w), functools.partial(lambda i, c: (jnp.maximum(i * (tr // sub) - 1, 0), c), c=cb))
                     for _, w, cb in rows]
    in_specs += [pl.BlockSpec(c.shape, functools.partial(lambda i, n: (0,) * n, n=c.ndim)) for c in consts]
    out_specs = [pl.BlockSpec((tr, w), lambda i: (i, 0)) for w, _ in row_outs]
    out_specs += [pl.BlockSpec(s, functools.partial(lambda i, n: (0,) * n, n=len(s))) for s in acc_outs]
    out_shape = [jax.ShapeDtypeStruct((t_len, w), dt) for w, dt in row_outs]
    out_shape += [jax.ShapeDtypeStruct(s, F32) for s in acc_outs]
    return pl.pallas_call(
        body, name=name, grid=(nb,), in_specs=in_specs + x_specs, out_specs=out_specs + x_specs,
        out_shape=out_shape + x_shapes, scratch_shapes=x_sems,
        compiler_params=pltpu.CompilerParams(dimension_semantics=("arbitrary",), vmem_limit_bytes=VMEM_LIMIT,
                                             has_side_effects=bool(n_x)),
    )(*[r[0] for r in rows], *([r[0] for r in rows] if halo else []), *consts, *gather)


def _colwise(name, fn, n_blocks, cols, prms, col_outs, prm_outs=()):
    t_len = cols[0][0].shape[0]
    n_i = len(cols) + len(prms)

    def body(*refs):
        outs = fn(*[r[...] for r in refs[:n_i]])
        for o_ref, val in zip(refs[n_i:], outs):
            o_ref[...] = val.astype(o_ref.dtype)

    spec = lambda r, w: pl.BlockSpec((r, w), lambda j: (0, j))
    in_specs = [spec(t_len, w) for _, w in cols] + [spec(a.shape[0], LANE) for a in prms]
    out_specs = [spec(t_len, bw) for _, _, bw in col_outs] + [spec(r, LANE) for r, _ in prm_outs]
    out_shape = [jax.ShapeDtypeStruct((t_len, w), dt) for w, dt, _ in col_outs]
    out_shape += [jax.ShapeDtypeStruct((r, w), F32) for r, w in prm_outs]
    return pl.pallas_call(
        body, name=name, grid=(n_blocks,), in_specs=in_specs, out_specs=out_specs, out_shape=out_shape,
        compiler_params=_params(("arbitrary",)),
    )(*[c[0] for c in cols], *prms)


def _matmul(name, a, b, dn, outs, *, tm, tn, tk, extras=(), consts=(), epilogue=None, sums=(), xch=(), xch_scatter=(),
            a_map=None, col_blocks_out=False):
    if dn == "nn":
        (m, k), n = a.shape, b.shape[1]
    elif dn == "nt":
        (m, k), n = a.shape, b.shape[0]
    else:
        (k, m), n = a.shape, b.shape[1]
    tm, tn, tk = min(tm, m), min(tn, n), min(tk, k)
    nk = k // tk
    grid = (m // tm, n // tn, nk)
    assert nk == 1 and (not sums or grid[1] == 1)
    a_spec = pl.BlockSpec((tk, tm), lambda i, j, q: (q, i)) if dn == "tn" else pl.BlockSpec((tm, tk), lambda i, j, q: (i, q))
    b_spec = pl.BlockSpec((tn, tk), lambda i, j, q: (j, q)) if dn == "nt" else pl.BlockSpec((tk, tn), lambda i, j, q: (q, j))
    o_spec = pl.BlockSpec((tm, tn), lambda i, j, q: (i, j))
    c_spec = pl.BlockSpec((1, tn), lambda i, j, q: (0, j))
    n_e, n_c, n_o, n_s, n_x = len(extras), len(consts), len(outs), len(sums), len(xch)
    x_specs, x_shapes, x_sems = _exchange_io(xch, xch_scatter) if n_x else ([], [], [])

    def body(*refs):
        a_ref, b_ref = refs[:2]
        e_refs = refs[2:2 + n_e + n_c]
        x_in = refs[2 + n_e + n_c:2 + n_e + n_c + n_x]
        rest = refs[2 + n_e + n_c + n_x:]
        o_refs, s_refs, x_out, scratch = rest[:n_o], rest[n_o:n_o + n_s], rest[n_o + n_s:n_o + n_s + n_x], rest[n_o + n_s + n_x:]
        step = (pl.program_id(0) * grid[1] + pl.program_id(1)) * nk + pl.program_id(2)
        if n_x:
            start, wait = _exchange_plan(x_in, x_out, xch_scatter, *scratch[len(scratch) - 3:])
            pl.when(step == 0)(start)
        a_blk = a_ref[...] if a_map is None else a_map(a_ref[...])
        acc = lax.dot_general(a_blk.astype(BF16), b_ref[...].astype(BF16), _dims(dn, 2), preferred_element_type=F32)
        vals = (acc,) if epilogue is None else epilogue(acc, *[e[...] for e in e_refs])
        for o_ref, val in zip(o_refs, vals[:n_o]):
            o_ref[...] = val.astype(o_ref.dtype)
        if n_s:
            @pl.when(step == 0)
            def _():
                for s_ref, val in zip(s_refs, vals[n_o:]):
                    s_ref[...] = val

            @pl.when(step > 0)
            def _():
                for s_ref, val in zip(s_refs, vals[n_o:]):
                    s_ref[...] += val

        if n_x:
            pl.when(step == grid[0] * grid[1] * nk - 1)(wait)

    plain = not (n_s or n_x)
    res = pl.pallas_call(
        body, name=name, grid=grid,
        in_specs=[a_spec, b_spec] + [o_spec] * n_e + [c_spec] * n_c + x_specs,
        out_specs=[pl.BlockSpec((None, tm, tn), lambda i, j, q: (j, i, 0)) if col_blocks_out else o_spec] * n_o
                  + [c_spec] * n_s + x_specs,
        out_shape=[jax.ShapeDtypeStruct((n // tn, m, tn) if col_blocks_out else (m, n), dt) for dt in outs] + [jax.ShapeDtypeStruct(s, F32) for s in sums] + x_shapes,
        scratch_shapes=x_sems,
        compiler_params=pltpu.CompilerParams(
            dimension_semantics=("parallel", "parallel", "arbitrary") if plain else ("arbitrary",) * 3,
            vmem_limit_bytes=VMEM_LIMIT, has_side_effects=bool(n_x)),
    )(a, b, *extras, *consts, *xch)
    return res[0] if len(res) == 1 else res


def _rms(h, g):
    return h * lax.rsqrt(jnp.mean(h * h, axis=-1, keepdims=True) + RMS_EPS) * g


def _rms_bwd(h, g, dy):
    rs = lax.rsqrt(jnp.mean(h * h, axis=-1, keepdims=True) + RMS_EPS)
    n = h * rs
    dn = dy * g
    dh = rs * (dn - n * jnp.mean(dn * n, axis=-1, keepdims=True))
    return dh, jnp.sum(dy * n, axis=0, keepdims=True)


def _rwkv_pre(k, xw, xa, xg, w0, a0, k_k, k_a, wl, al, gl):
    zw = w0 + _mm(jnp.tanh(xw), wl, "nn", 1)
    lw = -jnp.exp(-_softplus(-zw) - 0.5)
    iclr = _sigmoid(a0 + _mm(xa, al, "nn", 1))
    g = _mm(_sigmoid(xg), gl, "nn", 1)
    kk0 = k * k_k
    kk = kk0 * lax.rsqrt(jnp.maximum(_hsum(kk0 * kk0), L2_EPS * L2_EPS))
    k_h = k * (1.0 + (iclr - 1.0) * k_a)
    return lw, k_h, -kk, kk * iclr, g


def _rwkv_post(y, r, k_h, v, g, ln_g, ln_b, r_k):
    mu = _hsum(y) * (1.0 / HEAD_DIM)
    yc = y - mu
    var = _hsum(yc * yc) * (1.0 / HEAD_DIM)
    yo = yc * lax.rsqrt(var + GN_EPS) * ln_g + ln_b
    bonus = _hsum(r * k_h * r_k) * v
    return (yo + bonus) * g


def _shift_down(x, n):
    rows = lax.broadcasted_iota(jnp.int32, x.shape, 0)
    return jnp.where(rows < n, 0.0, pltpu.roll(x, n, 0))


def _shift_up(x, n):
    t_len = x.shape[0]
    rows = lax.broadcasted_iota(jnp.int32, x.shape, 0)
    return jnp.where(rows >= t_len - n, 0.0, pltpu.roll(x, t_len - n, 0))


def _exchange_plan(ins, outs, scatter, send_sems, recv_sems, local_sems):
    x, y, c = lax.axis_index("x"), lax.axis_index("y"), lax.axis_index("c")
    me = 4 * x + 2 * y + c

    def local(i):
        return pltpu.make_async_copy(ins[i].at[me] if scatter[i] else ins[i], outs[i].at[me], local_sems.at[i])

    def send(i, rel):
        return pltpu.make_async_remote_copy(
            src_ref=ins[i].at[me ^ rel] if scatter[i] else ins[i], dst_ref=outs[i].at[me],
            send_sem=send_sems.at[i, rel - 1], recv_sem=recv_sems.at[i, rel - 1],
            device_id=(x ^ (rel >> 2), y ^ ((rel >> 1) & 1), c ^ (rel & 1)), device_id_type=pl.DeviceIdType.MESH)

    def landed(i, rel):
        slot = outs[i].at[me ^ rel]
        return pltpu.make_async_remote_copy(
            src_ref=slot, dst_ref=slot, send_sem=send_sems.at[i, rel - 1], recv_sem=recv_sems.at[i, rel - 1],
            device_id=(x, y, c), device_id_type=pl.DeviceIdType.MESH)

    def start():
        for i in range(len(ins)):
            local(i).start()
            for rel in range(1, N_DEV):
                send(i, rel).start()

    def wait():
        for i in range(len(ins)):
            local(i).wait()
            for rel in range(1, N_DEV):
                landed(i, rel).wait_recv()
            for rel in range(1, N_DEV):
                send(i, rel).wait_send()

    return start, wait


def _gather_plan(ins, outs, send_sems, recv_sems, local_sems):
    x, y, c = lax.axis_index("x"), lax.axis_index("y"), lax.axis_index("c")
    me = 4 * x + 2 * y + c
    direct, chips = (1, 2, 4, 6), (2, 4, 6)

    def local(i):
        return pltpu.make_async_copy(ins[i], outs[i].at[me], local_sems.at[i])

    def send(i, rel):
        return pltpu.make_async_remote_copy(
            src_ref=ins[i], dst_ref=outs[i].at[me], send_sem=send_sems.at[i, rel - 1], recv_sem=recv_sems.at[i, rel - 1],
            device_id=(x ^ (rel >> 2), y ^ ((rel >> 1) & 1), c ^ (rel & 1)), device_id_type=pl.DeviceIdType.MESH)

    def passed(i, rel):
        slot = outs[i].at[me ^ rel]
        return pltpu.make_async_remote_copy(
            src_ref=slot, dst_ref=slot, send_sem=send_sems.at[i, rel], recv_sem=recv_sems.at[i, rel],
            device_id=(x, y, 1 - c), device_id_type=pl.DeviceIdType.MESH)

    def landed(i, rel):
        slot = outs[i].at[me ^ rel]
        return pltpu.make_async_remote_copy(
            src_ref=slot, dst_ref=slot, send_sem=send_sems.at[i, rel - 1], recv_sem=recv_sems.at[i, rel - 1],
            device_id=(x, y, c), device_id_type=pl.DeviceIdType.MESH)

    def start():
        for i in range(len(ins)):
            local(i).start()
            for rel in direct:
                send(i, rel).start()

    def forward(i):
        for rel in chips:
            landed(i, rel).wait_recv()
            passed(i, rel).start()

    def wait():
        for i in range(len(ins)):
            local(i).wait()
            for rel in (1, 3, 5, 7):
                landed(i, rel).wait_recv()
            for rel in direct:
                send(i, rel).wait_send()
            for rel in chips:
                passed(i, rel).wait_send()

    return start, forward, wait


def _exchange_io(arrays, scatter):
    n = len(arrays)
    any_spec = pl.BlockSpec(memory_space=pl.ANY)
    out_shape = [jax.ShapeDtypeStruct(a.shape if sc else (N_DEV,) + a.shape, a.dtype) for a, sc in zip(arrays, scatter)]
    sems = [pltpu.SemaphoreType.DMA((n, N_DEV - 1)), pltpu.SemaphoreType.DMA((n, N_DEV - 1)), pltpu.SemaphoreType.DMA((n,))]
    return [any_spec] * n, out_shape, sems


def _exchange(name, arrays, scatter):
    n = len(arrays)
    specs, out_shape, sems = _exchange_io(arrays, scatter)

    def body(*refs):
        if any(scatter):
            start, wait = _exchange_plan(refs[:n], refs[n:2 * n], scatter, *refs[2 * n:])
            start()
        else:
            start, forward, wait = _gather_plan(refs[:n], refs[n:2 * n], *refs[2 * n:])
            start()
            for i in range(n):
                forward(i)
        wait()

    return pl.pallas_call(
        body, name=name, in_specs=specs, out_specs=specs, out_shape=out_shape, scratch_shapes=sems,
        compiler_params=pltpu.CompilerParams(has_side_effects=True),
    )(*arrays)


def _scatter_start(name, arrays, lands):
    n = len(arrays)
    hbm = pl.BlockSpec(memory_space=pltpu.HBM)

    def body(*refs):
        ins, land, send_sems, recv_sems = refs[:n], refs[n:2 * n], refs[2 * n], refs[2 * n + 1]
        token = refs[4 * n + 2]
        x, y, c = lax.axis_index("x"), lax.axis_index("y"), lax.axis_index("c")
        me = 4 * x + 2 * y + c
        for i in range(n):
            for rel in range(1, N_DEV):
                k = i * (N_DEV - 1) + rel - 1
                pltpu.make_async_remote_copy(
                    src_ref=ins[i].at[me ^ rel], dst_ref=land[i].at[me], send_sem=send_sems.at[k],
                    recv_sem=recv_sems.at[k], device_id=(x ^ (rel >> 2), y ^ ((rel >> 1) & 1), c ^ (rel & 1)),
                    device_id_type=pl.DeviceIdType.MESH).start()
        token[...] = jnp.zeros_like(token)

    sem = pltpu.SemaphoreType.DMA((n * (N_DEV - 1),))
    bufs = [pltpu.HBM(a.shape, a.dtype) for a in list(arrays) + list(lands)]
    res = pl.pallas_call(
        body, name=name, out_shape=(sem, sem, *bufs, jax.ShapeDtypeStruct((8, LANE), F32)),
        in_specs=[hbm] * (2 * n),
        out_specs=(pl.BlockSpec(memory_space=pltpu.SEMAPHORE),) * 2 + (hbm,) * (2 * n) + (pl.BlockSpec(memory_space=pltpu.VMEM),),
        input_output_aliases={i: 2 + i for i in range(2 * n)},
        compiler_params=pltpu.CompilerParams(has_side_effects=pltpu.SideEffectType.DATAFLOW_SIDE_EFFECTING),
    )(*[pltpu.with_memory_space_constraint(a, pltpu.HBM) for a in list(arrays) + list(lands)])
    return res[0], res[1], res[2:2 + n], res[2 + n:2 + 2 * n], res[2 + 2 * n]


def _scatter_wait(name, send_sems, recv_sems, arrays, lands, after):
    n, n_after = len(arrays), len(after)
    hbm = pl.BlockSpec(memory_space=pltpu.HBM)

    def body(*refs):
        ins, land, s_sems, r_sems = refs[:n], refs[n:2 * n], refs[2 * n], refs[2 * n + 1]
        x, y, c = lax.axis_index("x"), lax.axis_index("y"), lax.axis_index("c")
        me = 4 * x + 2 * y + c
        for i in range(n):
            for rel in range(1, N_DEV):
                k = i * (N_DEV - 1) + rel - 1
                cp = pltpu.make_async_remote_copy(
                    src_ref=ins[i].at[me ^ rel], dst_ref=land[i].at[me ^ rel], send_sem=s_sems.at[k],
                    recv_sem=r_sems.at[k], device_id=(x, y, c), device_id_type=pl.DeviceIdType.MESH)
                cp.wait_send()
                cp.wait_recv()

    res = pl.pallas_call(
        body, name=name, out_shape=[pltpu.HBM(a.shape, a.dtype) for a in list(arrays) + list(lands)],
        in_specs=[hbm] * (2 * n) + [pl.BlockSpec(memory_space=pltpu.SEMAPHORE)] * 2 + [pl.BlockSpec(memory_space=pl.ANY)] * n_after,
        out_specs=[hbm] * (2 * n), input_output_aliases={i: i for i in range(2 * n)},
        compiler_params=pltpu.CompilerParams(has_side_effects=pltpu.SideEffectType.DATAFLOW_SIDE_EFFECTING),
    )(*arrays, *lands, send_sems, recv_sems, *after)
    return res[:n], res[n:]


def _gather_start(name, shards, after):
    n, n_after = len(shards), len(after)
    hbm = pl.BlockSpec(memory_space=pltpu.HBM)
    lands = [lax.empty((N_DEV,) + a.shape, a.dtype) for a in shards]

    def body(*refs):
        ins, land = refs[:n], refs[n:2 * n]
        outs = refs[2 * n + n_after:]
        send_sems, recv_sems, token = outs[:n], outs[n:2 * n], outs[4 * n]
        x, y, c = lax.axis_index("x"), lax.axis_index("y"), lax.axis_index("c")
        me = 4 * x + 2 * y + c
        for i in range(n):
            for rel in range(N_DEV):
                pltpu.make_async_remote_copy(
                    src_ref=ins[i], dst_ref=land[i].at[me], send_sem=send_sems[i].at[rel],
                    recv_sem=recv_sems[i].at[rel], device_id=(x ^ (rel >> 2), y ^ ((rel >> 1) & 1), c ^ (rel & 1)),
                    device_id_type=pl.DeviceIdType.MESH).start()
        token[...] = jnp.zeros_like(token)

    sem = pltpu.SemaphoreType.DMA((N_DEV,))
    bufs = [pltpu.HBM(a.shape, a.dtype) for a in list(shards) + lands]
    res = pl.pallas_call(
        body, name=name, out_shape=(*[sem] * (2 * n), *bufs, jax.ShapeDtypeStruct((8, LANE), F32)),
        in_specs=[hbm] * (2 * n) + [pl.BlockSpec(memory_space=pl.ANY)] * n_after,
        out_specs=(pl.BlockSpec(memory_space=pltpu.SEMAPHORE),) * (2 * n) + (hbm,) * (2 * n)
                  + (pl.BlockSpec(memory_space=pltpu.VMEM),),
        input_output_aliases={i: 2 * n + i for i in range(2 * n)},
        compiler_params=pltpu.CompilerParams(has_side_effects=pltpu.SideEffectType.DATAFLOW_SIDE_EFFECTING),
    )(*[pltpu.with_memory_space_constraint(a, pltpu.HBM) for a in list(shards) + lands], *after)
    return res[:n], res[n:2 * n], res[2 * n:3 * n], res[3 * n:4 * n], res[4 * n]


def _gather_wait(name, send_sems, recv_sems, shards, lands, after):
    n, n_after = len(shards), len(after)
    hbm = pl.BlockSpec(memory_space=pltpu.HBM)

    def body(*refs):
        ins, land = refs[:n], refs[n:2 * n]
        s_sems, r_sems = refs[2 * n:3 * n], refs[3 * n:4 * n]
        x, y, c = lax.axis_index("x"), lax.axis_index("y"), lax.axis_index("c")
        me = 4 * x + 2 * y + c
        for i in range(n):
            for rel in range(N_DEV):
                cp = pltpu.make_async_remote_copy(
                    src_ref=ins[i], dst_ref=land[i].at[me ^ rel], send_sem=s_sems[i].at[rel],
                    recv_sem=r_sems[i].at[rel], device_id=(x, y, c), device_id_type=pl.DeviceIdType.MESH)
                cp.wait_send()
                cp.wait_recv()

    res = pl.pallas_call(
        body, name=name, out_shape=[pltpu.HBM(a.shape, a.dtype) for a in list(shards) + list(lands)],
        in_specs=[hbm] * (2 * n) + [pl.BlockSpec(memory_space=pltpu.SEMAPHORE)] * (2 * n)
                 + [pl.BlockSpec(memory_space=pl.ANY)] * n_after,
        out_specs=[hbm] * (2 * n), input_output_aliases={i: i for i in range(2 * n)},
        compiler_params=pltpu.CompilerParams(has_side_effects=pltpu.SideEffectType.DATAFLOW_SIDE_EFFECTING),
    )(*shards, *lands, *send_sems, *recv_sems, *after)
    return res[n:]


def _tri_powers(low):
    powers, n, p = [low.astype(BF16)], 1, low
    while 2 * n < low.shape[-1]:
        p = _mm(p, p, "nn", REC_PASSES)
        powers.append(p.astype(BF16))
        n *= 2
    return powers


@jax.custom_vjp
def _tri_solve(low, rhs, powers):
    del low
    for p in powers:
        rhs = rhs + _mm(p, rhs, "nn", REC_PASSES)
    return rhs


def _tri_solve_fwd(low, rhs, powers):
    out = _tri_solve(low, rhs, powers)
    return out, (powers, out)


def _tri_solve_bwd(res, d):
    powers, u = res
    for p in powers:
        d = d + _mm(p, d, "tn", REC_PASSES)
    return _mm(d, u, "nt", REC_PASSES), d, [jnp.zeros_like(p) for p in powers]


_tri_solve.defvjp(_tri_solve_fwd, _tri_solve_bwd)


@jax.custom_vjp
def _tri_solve_given(low, rhs, powers, value):
    del low, rhs, powers
    return value


def _tri_solve_given_fwd(low, rhs, powers, value):
    return value, (powers, value)


def _tri_solve_given_bwd(res, d):
    return _tri_solve_bwd(res, d) + (jnp.zeros_like(res[1]),)


_tri_solve_given.defvjp(_tri_solve_given_fwd, _tri_solve_given_bwd)


def _heads(x):
    return jnp.stack([x[:, h * HEAD_DIM:(h + 1) * HEAD_DIM] for h in range(N_HEADS)])


def _unheads(x):
    return jnp.concatenate([x[h] for h in range(N_HEADS)], axis=-1)


def _causal_masks(c):
    ti = lax.broadcasted_iota(jnp.int32, (c, c), 0)
    si = lax.broadcasted_iota(jnp.int32, (c, c), 1)
    strict, incl = si < ti, si <= ti
    both = jnp.concatenate([jnp.concatenate([strict, strict], axis=1), jnp.concatenate([incl, incl], axis=1)], axis=0)
    return strict, incl, both


@jax.custom_vjp
def _gram_given(x2, y2, value):
    del x2, y2
    return value.astype(F32)


def _gram_given_fwd(x2, y2, value):
    return value.astype(F32), (x2, y2, value)


def _gram_given_bwd(res, d):
    x2, y2, value = res
    d = jnp.where(_causal_masks(d.shape[-1] // 2)[2], d, 0.0)
    return _mm(d, y2, "nn", 2), _mm(d, x2, "tn", 2), jnp.zeros_like(value)


_gram_given.defvjp(_gram_given_fwd, _gram_given_bwd)


def _chunk_fwd(z0, r, lw, k, v, a, b, powers=None, gram_value=None, u_value=None):
    c = r.shape[0]
    n_h, n_k = z0.shape[0], z0.shape[1]
    mm = functools.partial(_mm, passes=REC_PASSES)
    gram = functools.partial(_mm, passes=2)
    _, incl, mask = _causal_masks(c)
    cum = _mm(incl.astype(F32), lw, "nn", 3)
    cum_end = cum[c - 1:c, :]
    e_neg, e_end = jnp.exp(-cum), jnp.exp(cum_end - cum)
    x2 = jnp.concatenate([_heads(a * jnp.exp(cum - lw)), _heads(r * jnp.exp(cum))], axis=1)
    y2 = jnp.concatenate([_heads(b * e_neg), _heads(k * e_neg)], axis=1)
    vh = _heads(v)
    g2 = jnp.where(mask, gram(x2, y2, "nt"), 0.0) if gram_value is None else _gram_given(x2, y2, gram_value)
    t2 = mm(x2, z0, "nn") + mm(g2[:, :, c:], vh, "nn")
    low = g2[:, :c, :c]
    powers = _tri_powers(low) if powers is None else powers
    u = _tri_solve(low, t2[:, :c], powers) if u_value is None else _tri_solve_given(low, t2[:, :c], powers, u_value)
    y = t2[:, c:] + mm(g2[:, c:, :c], u, "nn")
    ki = lax.broadcasted_iota(jnp.int32, (n_k, n_k), 0)
    kj = lax.broadcasted_iota(jnp.int32, (n_k, n_k), 1)
    dmat = jnp.where(ki == kj, jnp.broadcast_to(_heads(jnp.exp(cum_end)), (n_h, n_k, n_k)), 0.0)
    z_end = mm(dmat, z0, "nn") + mm(jnp.concatenate([_heads(b * e_end), _heads(k * e_end)], axis=1),
                                    jnp.concatenate([u, vh], axis=1), "tn")
    return _unheads(y), z_end, powers, g2, u


def _rec_params():
    return pltpu.CompilerParams(dimension_semantics=("arbitrary",), vmem_limit_bytes=VMEM_LIMIT, has_side_effects=True)


def _rec_fwd(u, lw, k, a, b, xch):
    t_len = lw.shape[0]
    c = min(REC_CHUNK, t_len)
    nc = t_len // c
    per = REC_CHUNKS_PER_STEP if nc % REC_CHUNKS_PER_STEP == 0 else 1
    steps = nc // per
    n_x = len(xch)
    x_specs, x_shapes, x_sems = _exchange_io(xch, [False] * n_x) if n_x else ([], [], [])
    n_pow = max(1, (c - 1).bit_length())
    sizes = [a_.size * a_.dtype.itemsize for a_ in xch]
    pass_step = [min(steps - 1, int(0.9 * steps * sum(sizes[:j + 1]) / sum(sizes)) + 1) for j in range(n_x)]

    def body(*refs):
        r_ref, v_ref, lw_ref, k_ref, a_ref, b_ref = refs[:6]
        x_in = refs[6:6 + n_x]
        y_ref, zs_ref, pw_ref, gs_ref, us_ref = refs[6 + n_x:11 + n_x]
        x_out = refs[11 + n_x:11 + 2 * n_x]
        z_scr = refs[11 + 2 * n_x]
        i = pl.program_id(0)
        if n_x:
            start, forward, wait = _gather_plan(x_in, x_out, *refs[12 + 2 * n_x:])
            pl.when(i == 0)(start)

        @pl.when(i == 0)
        def _():
            z_scr[...] = jnp.zeros_like(z_scr)

        for s in range(per):
            rows = pl.ds(s * c, c)
            z0 = z_scr[...]
            zs_ref[s] = z0
            y, z_end, powers, g2, u_rows = _chunk_fwd(z0, r_ref[rows, :], lw_ref[rows, :], k_ref[rows, :], v_ref[rows, :],
                                                      a_ref[rows, :], b_ref[rows, :])
            y_ref[rows, :] = y
            z_scr[...] = z_end
            pw_ref[s] = jnp.concatenate(powers, axis=0)
            gs_ref[s] = g2.astype(BF16)
            us_ref[s] = u_rows

        for j in range(n_x):
            pl.when(i == pass_step[j])(functools.partial(forward, j))
        if n_x:
            pl.when(i == steps - 1)(wait)

    blk = lambda cb: pl.BlockSpec((per * c, RWKV_DIM), functools.partial(lambda i, q: (i, q), q=cb))
    res = pl.pallas_call(
        body, name="rwkv_rec_fwd", grid=(steps,),
        in_specs=[blk(0), blk(2)] + [blk(0)] * 4 + x_specs,
        out_specs=[blk(0), pl.BlockSpec((per, N_HEADS, HEAD_DIM, HEAD_DIM), lambda i: (i, 0, 0, 0)),
                   pl.BlockSpec((per, n_pow * N_HEADS, c, c), lambda i: (i, 0, 0, 0)),
                   pl.BlockSpec((per, N_HEADS, 2 * c, 2 * c), lambda i: (i, 0, 0, 0)),
                   pl.BlockSpec((per, N_HEADS, c, HEAD_DIM), lambda i: (i, 0, 0, 0))] + x_specs,
        out_shape=[jax.ShapeDtypeStruct((t_len, RWKV_DIM), F32),
                   jax.ShapeDtypeStruct((nc, N_HEADS, HEAD_DIM, HEAD_DIM), F32),
                   jax.ShapeDtypeStruct((nc, n_pow * N_HEADS, c, c), BF16),
                   jax.ShapeDtypeStruct((nc, N_HEADS, 2 * c, 2 * c), BF16),
                   jax.ShapeDtypeStruct((nc, N_HEADS, c, HEAD_DIM), F32)] + x_shapes,
        scratch_shapes=[pltpu.VMEM((N_HEADS, HEAD_DIM, HEAD_DIM), F32)] + x_sems,
        compiler_params=_rec_params() if n_x else _params(("arbitrary",)),
    )(u, u, lw, k, a, b, *xch)
    return res[0], res[1:5], res[5:]


def _rec_bwd(u, lw, k, a, b, saved, dy):
    t_len = lw.shape[0]
    c = min(REC_CHUNK, t_len)
    nc = t_len // c
    per = REC_CHUNKS_PER_STEP if nc % REC_CHUNKS_PER_STEP == 0 else 1
    steps = nc // per

    zs, pw, gs, us = saved

    def body(r_ref, v_ref, lw_ref, k_ref, a_ref, b_ref, zs_ref, pw_ref, gs_ref, us_ref, dy_ref, *rest):
        g_refs, dz_scr = rest[:6], rest[6]

        @pl.when(pl.program_id(0) == 0)
        def _():
            dz_scr[...] = jnp.zeros_like(dz_scr)

        for s in reversed(range(per)):
            rows = pl.ds(s * c, c)
            powers = [pw_ref[s, j * N_HEADS:(j + 1) * N_HEADS] for j in range(pw.shape[1] // N_HEADS)]
            chunk = functools.partial(lambda gram, u_val, pws, *xs: _chunk_fwd(*xs, powers=pws, gram_value=gram, u_value=u_val)[:2],
                                      gs_ref[s], us_ref[s], powers)
            _, vjp = jax.vjp(chunk, zs_ref[s], r_ref[rows, :], lw_ref[rows, :], k_ref[rows, :], v_ref[rows, :],
                             a_ref[rows, :], b_ref[rows, :])
            dz0, dr, dlw, dk, dv, da, db = vjp((dy_ref[rows, :], dz_scr[...]))
            for ref, val in zip(g_refs, (dr, dv, dlw, dk, da, db)):
                ref[rows, :] = val
            dz_scr[...] = dz0

    blk = lambda cb: pl.BlockSpec((per * c, RWKV_DIM), functools.partial(lambda i, q: (steps - 1 - i, q), q=cb))
    saved_blk = lambda arr: pl.BlockSpec((per,) + arr.shape[1:], lambda i: (steps - 1 - i, 0, 0, 0))
    return pl.pallas_call(
        body, name="rwkv_rec_bwd", grid=(steps,),
        in_specs=[blk(0), blk(2)] + [blk(0)] * 4 + [saved_blk(zs), saved_blk(pw), saved_blk(gs), saved_blk(us), blk(0)],
        out_specs=[blk(0)] * 6, out_shape=[jax.ShapeDtypeStruct((t_len, RWKV_DIM), F32)] * 6,
        scratch_shapes=[pltpu.VMEM((N_HEADS, HEAD_DIM, HEAD_DIM), F32)], compiler_params=_params(("arbitrary",)),
    )(u, u, lw, k, a, b, zs, pw, gs, us, dy)


_EARLY = ["w_in", "conv_w", "w_lora_up", "a_lora_up", "g_lora_up"]
_LATE = ["w_out", "w_up", "w_down", "w_ple_gate", "w_ple_proj"]
_SHARDED = _EARLY + _LATE
_COL_SHARDED = {"w_in", "conv_w", "w_lora_up", "a_lora_up", "g_lora_up", "w_up", "w_ple_proj"}
_BF16_GATHER = {"w_in", "w_out", "w_up", "w_down", "w_ple_gate", "w_ple_proj"}
_REPLICATED = ["norm_mix_g", "shift_mu", "w0", "a0", "k_k", "k_a", "r_k", "ln_x_g", "ln_x_b", "norm_mlp_g", "norm_ple_g",
               "norm_final_g"]
_WEIGHTS = ["norm_mix_g", "w_in", "conv_w", "shift_mu", "w_lora_up", "w0", "a_lora_up", "a0", "g_lora_up", "k_k", "k_a", "r_k",
            "ln_x_g", "ln_x_b", "w_out", "norm_mlp_g", "w_up", "w_down", "norm_ple_g", "w_ple_gate", "w_ple_proj", "norm_final_g"]


def _unshard(name, g):
    if name in _COL_SHARDED:
        return jnp.moveaxis(g, 0, 1).reshape(g.shape[1], N_DEV * g.shape[2])
    return g.reshape(N_DEV * g.shape[1], g.shape[2])


def _reshard(name, full):
    if name in _COL_SHARDED:
        return jnp.moveaxis(full.reshape(full.shape[0], N_DEV, full.shape[1] // N_DEV), 1, 0)
    return full.reshape(N_DEV, full.shape[0] // N_DEV, full.shape[1])


def _pad_in_cols(a):
    z = lambda n: jnp.zeros(a.shape[:-1] + (n,), a.dtype)
    conv = [a[..., part * CONV_DIM + j * LANE:part * CONV_DIM + (j + 1) * LANE] for j in range(CONV_DIM // LANE) for part in range(3)]
    return jnp.concatenate(conv + [a[..., CONV_COLS:3136], z(64), a[..., 3136:3200], z(64), a[..., 3200:3360], z(96)], axis=-1)


def _unpad_in_cols(a):
    conv = [a[..., (3 * j + part) * LANE:(3 * j + part + 1) * LANE] for part in range(3) for j in range(CONV_DIM // LANE)]
    return jnp.concatenate(conv + [a[..., CONV_COLS:3136], a[..., 3200:3264], a[..., 3328:3488]], axis=-1)


def _assemble_w_in(g):
    n_dev, rows, cols = g.shape

    def body(g_ref, o_ref):
        o_ref[...] = _pad_in_cols(jnp.concatenate([g_ref[d] for d in range(n_dev)], axis=1))

    return pl.pallas_call(
        body, name="w_in_assemble", grid=(rows // ROW_BLOCK,),
        in_specs=[pl.BlockSpec((n_dev, ROW_BLOCK, cols), lambda i: (0, i, 0))],
        out_specs=pl.BlockSpec((ROW_BLOCK, IN_PAD), lambda i: (i, 0)),
        out_shape=jax.ShapeDtypeStruct((rows, IN_PAD), g.dtype), compiler_params=_params(("arbitrary",)),
    )(g)


def _split_w_in_grad(dw):
    rows = dw.shape[0]
    cols = IN_COLS // N_DEV

    def body(d_ref, o_ref):
        full = _unpad_in_cols(d_ref[...])
        for d in range(N_DEV):
            o_ref[d] = full[:, cols * d:cols * (d + 1)]

    return pl.pallas_call(
        body, name="w_in_grad_split", grid=(rows // ROW_BLOCK,),
        in_specs=[pl.BlockSpec((ROW_BLOCK, IN_PAD), lambda i: (i, 0))],
        out_specs=pl.BlockSpec((N_DEV, ROW_BLOCK, cols), lambda i: (0, i, 0)),
        out_shape=jax.ShapeDtypeStruct((N_DEV, rows, cols), dw.dtype), compiler_params=_params(("arbitrary",)),
    )(dw)


def _pad_rows(a, rows):
    return jnp.concatenate([a, jnp.zeros((rows - a.shape[0],) + a.shape[1:], a.dtype)], axis=0)


SEG_W = [RWKV_DIM, RWKV_DIM, RWKV_DIM, LANE, LANE, 2 * LANE]
SEG_OFF = [0, 512, 1024, XW_OFF, XA_OFF, XG_OFF]


def _rwkv_pre_bwd(proj, u, grads, mu, small, dproj):
    t_len = u.shape[0]
    tr = min(ROW_BLOCK, t_len)
    nb = t_len // tr
    sub = 8
    n_g = len(grads)
    acc_shapes = [(1, RW_PAD)] + [(1, RWKV_DIM)] * 4 + [(LANE, RWKV_DIM), (LANE, RWKV_DIM), (2 * LANE, RWKV_DIM)]

    def body(*refs):
        seg_refs, halo_refs = refs[:6], refs[6:12]
        k_ref, xw_ref, xa_ref, xg_ref = refs[12:16]
        g_refs = refs[16:16 + n_g]
        mu_ref = refs[16 + n_g]
        prm_refs = refs[17 + n_g:24 + n_g]
        out_hbm = refs[25 + n_g]
        acc_refs = refs[26 + n_g:26 + n_g + len(acc_shapes)]
        vbuf, sems, carry = refs[26 + n_g + len(acc_shapes):]
        i = pl.program_id(0)
        blk = nb - 1 - i
        dr1, dr2, dv1, dv2, dlw, dk1, dk2, da, db, dg = [g[...] for g in g_refs]
        _, vjp = jax.vjp(_rwkv_pre, k_ref[...], xw_ref[...], xa_ref[...], xg_ref[...], *[p_[...] for p_ in prm_refs])
        dk, dxw, dxa, dxg, *dprm = vjp((dlw, dk1 + dk2, da, db, dg))
        du = jnp.concatenate([dr1 + dr2, dk, dv1 + dv2, dxw, dxa, dxg], axis=1)
        mu_v = mu_ref[...]

        @pl.when(i == 0)
        def _():
            carry[...] = jnp.zeros_like(carry)

        rows = lax.broadcasted_iota(jnp.int32, du.shape, 0)
        nxt = jnp.where(rows == tr - 1, carry[...], pltpu.roll(du, tr - 1, 0))
        d_rw = du - mu_v * du + mu_v * nxt
        d_mu = []
        for s_ref, h_ref, off, wd in zip(seg_refs, halo_refs, SEG_OFF, SEG_W):
            cur = s_ref[...]
            r0 = lax.broadcasted_iota(jnp.int32, cur.shape, 0)
            prev = jnp.where(r0 == 0, jnp.where(blk == 0, 0.0, h_ref[sub - 1:sub, :]), pltpu.roll(cur, 1, 0))
            d_mu.append(jnp.sum(du[:, off:off + wd] * (prev - cur), axis=0, keepdims=True))
        sums = [jnp.concatenate(d_mu, axis=1)] + list(dprm)

        @pl.when(i == 0)
        def _():
            for a_ref, val in zip(acc_refs, sums):
                a_ref[...] = val

        @pl.when(i > 0)
        def _():
            for a_ref, val in zip(acc_refs, sums):
                a_ref[...] += val

        carry[...] = du[0:1, :]
        slot = i % 2

        def writeback(s, b):
            return pltpu.make_async_copy(vbuf.at[s], out_hbm.at[pl.ds(b * tr, tr), pl.ds(CONV_COLS, RW_PAD)], sems.at[s])

        @pl.when(i >= 2)
        def _():
            writeback(slot, blk + 2).wait()

        vbuf[slot] = d_rw.astype(vbuf.dtype)
        writeback(slot, blk).start()

        @pl.when(i == nb - 1)
        def _():
            writeback(slot, blk).wait()
            if nb > 1:
                writeback(1 - slot, blk + 1).wait()

    rev = lambda w_, cb: pl.BlockSpec((tr, w_), functools.partial(lambda i, c: (nb - 1 - i, c), c=cb))
    halo = lambda w_, cb: pl.BlockSpec((sub, w_), functools.partial(
        lambda i, c: (jnp.maximum((nb - 1 - i) * (tr // sub) - 1, 0), c), c=cb))
    whole = lambda a: pl.BlockSpec(a.shape, functools.partial(lambda i, n: (0,) * n, n=a.ndim))
    segs = [(wd, (CONV_COLS + off) // wd) for off, wd in zip(SEG_OFF, SEG_W)]
    u_cols = [(512, 1), (LANE, XW_OFF // LANE), (LANE, XA_OFF // LANE), (2 * LANE, XG_OFF // (2 * LANE))]
    any_spec = pl.BlockSpec(memory_space=pl.ANY)
    res = pl.pallas_call(
        body, name="rwkv_pre_bwd", grid=(nb,),
        in_specs=[rev(*s) for s in segs] + [halo(*s) for s in segs] + [rev(*c) for c in u_cols]
                 + [rev(RWKV_DIM, 0)] * n_g + [whole(mu)] + [whole(p_) for p_ in small] + [any_spec],
        out_specs=[any_spec] + [pl.BlockSpec(s, functools.partial(lambda i, n: (0,) * n, n=len(s))) for s in acc_shapes],
        out_shape=[jax.ShapeDtypeStruct(dproj.shape, dproj.dtype)] + [jax.ShapeDtypeStruct(s, F32) for s in acc_shapes],
        scratch_shapes=[pltpu.VMEM((2, tr, RW_PAD), dproj.dtype), pltpu.SemaphoreType.DMA((2,)), pltpu.VMEM((1, RW_PAD), F32)],
        input_output_aliases={24 + n_g: 0},
        compiler_params=_params(("arbitrary",)),
    )(*[proj] * 12, *[u] * 4, *grads, mu, *small, dproj)
    return res


def _local_step(x, p, tgt, w, early_shards, late_shards):
    row = lambda v: v.reshape(1, -1)
    w = dict(w)

    xn1, *gathered = _rowwise("rms_mix", lambda h, g: (_rms(h, g),), [x], [w["norm_mix_g"]], [(D_MODEL, BF16)],
                              gather=early_shards)
    w.update({n: _unshard(n, g_) for n, g_ in zip(_EARLY[1:], gathered[1:])})
    w["w_in"] = _assemble_w_in(gathered[0])
    w["w_lora_up"] = _pad_rows(w["w_lora_up"], LANE)
    w["a_lora_up"] = _pad_rows(w["a_lora_up"], LANE)
    w["g_lora_up"] = _pad_rows(w["g_lora_up"], 2 * LANE)
    lg_send, lg_recv, lg_shards, lg_lands, lg_token = _gather_start("late_gather_start", late_shards, after=[xn1])
    w["shift_mu"] = w["shift_mu"] + lg_token[0:1, 0:1]
    proj = _matmul("in_proj", xn1, w["w_in"], "nn", [F32], tm=2048, tn=512, tk=D_MODEL)
    n_cb = CONV_DIM // LANE

    def conv_fwd(blk, cw):
        gb, gc, hx = blk[:, :LANE], blk[:, LANE:2 * LANE], blk[:, 2 * LANE:]
        uu = gc * hx
        return (gb * (uu * cw[2:3] + _shift_down(uu, 1) * cw[1:2] + _shift_down(uu, 2) * cw[0:1]),)

    (y_conv,) = _colwise("conv_fwd", conv_fwd, n_cb, [(proj, 3 * LANE)], [w["conv_w"]], [(CONV_DIM, BF16, LANE)])

    small = [w["w0"], w["a0"], w["k_k"], w["k_a"], w["w_lora_up"], w["a_lora_up"], w["g_lora_up"]]
    def pre_fwd(*xs):
        cur, prev_rows, mu, prm = xs[:6], xs[6:12], xs[12], xs[13:]
        segs = []
        for c_, p_, off, wd in zip(cur, prev_rows, SEG_OFF, SEG_W):
            rows = lax.broadcasted_iota(jnp.int32, c_.shape, 0)
            prev = jnp.where(rows == 0, p_, pltpu.roll(c_, 1, 0))
            segs.append(c_ + mu[:, off:off + wd] * (prev - c_))
        return (jnp.concatenate(segs, axis=1),) + tuple(_rwkv_pre(segs[1], segs[3], segs[4], segs[5], *prm))

    proj_segs = [(proj, wd, (CONV_COLS + off) // wd) for off, wd in zip(SEG_OFF, SEG_W)]
    u, lw, k_h, ra, rb, g = _rowwise(
        "rwkv_pre", pre_fwd, proj_segs, [w["shift_mu"]] + small, [(RW_PAD, F32)] + [(RWKV_DIM, F32)] * 5, halo=True)
    y_rec, rec_saved, _ = _rec_fwd(u, lw, k_h, ra, rb, [])

    def late_weight(names, after):
        idx = [_LATE.index(n) for n in names]
        got = _gather_wait("late_gather_wait_" + names[0], [lg_send[i] for i in idx], [lg_recv[i] for i in idx],
                           [lg_shards[i] for i in idx], [lg_lands[i] for i in idx], after)
        return [_unshard(n, g_) for n, g_ in zip(names, got)]

    w["w_out"], w["w_up"], w["w_down"] = late_weight(["w_out", "w_up", "w_down"], [y_rec])
    post_c = [w["ln_x_g"], w["ln_x_b"], w["r_k"]]
    u_r, u_v = (u, 512, 0), (u, 512, 2)
    (y_rwkv,) = _rowwise("rwkv_post", lambda *xs: (_rwkv_post(*xs),), [y_rec, u_r, k_h, u_v, g], post_c, [(RWKV_DIM, BF16)],
                         tr=2 * ROW_BLOCK)
    ycat = jnp.concatenate([y_conv, y_rwkv], axis=1)
    def res_norm(acc, r_, g_):
        h = acc + r_
        return h, _rms(h, g_)

    h1, xn2 = _matmul("out_proj", ycat, w["w_out"], "nn", [F32, BF16], tm=1024, tn=D_MODEL, tk=D_MODEL, extras=[x],
                      consts=[w["norm_mlp_g"]], epilogue=res_norm)

    square = lambda h: h.astype(F32) * h.astype(F32)
    hid = _matmul("mlp_up", xn2, w["w_up"], "nn", [BF16], tm=2048, tn=1024, tk=D_MODEL,
                  epilogue=lambda acc: (jnp.maximum(acc, 0.0),))
    h2, xn3 = _matmul("mlp_down", hid, w["w_down"], "nn", [F32, BF16], tm=512, tn=D_MODEL, tk=D_FF, extras=[h1],
                      consts=[w["norm_ple_g"]], epilogue=res_norm, a_map=square)
    w["w_ple_gate"], w["w_ple_proj"] = late_weight(["w_ple_gate", "w_ple_proj"], [xn3])
    zg =_matmul("ple_gate", xn3, w["w_ple_gate"], "nn", [F32], tm=1024, tn=1024, tk=D_MODEL)
    pp = _matmul("ple_proj", p, w["w_ple_proj"], "nn", [F32], tm=1024, tn=1024, tk=PLE_DIM)

    def head(h2_, zg_, pp_, tg, gf):
        gate = _sigmoid(zg_)
        h3 = h2_ + gate * pp_
        out = _rms(h3, gf)
        err = out - tg
        dh3, dgf = _rms_bwd(h3, gf, err * (1.0 / D_MODEL))
        loss = jnp.sum(jnp.sum(err * err, axis=1, keepdims=True), axis=0, keepdims=True) * (0.5 / D_MODEL)
        return dh3, dh3 * pp_ * gate * (1.0 - gate), dh3 * gate, dgf, loss

    dh3, dzg, dpp, d_norm_final, loss = _rowwise(
        "head", head, [h2, zg, pp, tgt], [row(w["norm_final_g"])], [(D_MODEL, F32), (D_MODEL, BF16), (D_MODEL, BF16)],
        [(1, D_MODEL), (1, 1)], tr=2 * ROW_BLOCK)

    d_w_ple_proj = _matmul("d_ple_proj", p, dpp, "tn", [BF16], tm=PLE_DIM, tn=D_MODEL // N_DEV, tk=4096, col_blocks_out=True)
    d_w_ple_gate = _matmul("d_ple_gate", xn3, dzg, "tn", [BF16], tm=512, tn=1024, tk=4096)

    def norm_bwd(dxn, h, dres, g_):
        dh, dg = _rms_bwd(h, g_, dxn)
        dh = dh + dres
        return dh, dh, dg

    nb = dict(tm=512, tn=D_MODEL, epilogue=norm_bwd, sums=[(1, D_MODEL)])
    dh2, dh2_b, d_norm_ple = _matmul("dx_ple_gate", dzg, w["w_ple_gate"], "nt", [F32, BF16], tk=D_MODEL,
                                     extras=[h2, dh3], consts=[w["norm_ple_g"]], **nb)
    d_w_down = _matmul("d_mlp_down", hid, dh2_b, "tn", [BF16], tm=512, tn=1024, tk=4096, a_map=square)
    dpre = _matmul("dx_mlp_down", dh2_b, w["w_down"], "nt", [BF16], tm=2048, tn=1024, tk=D_MODEL, extras=[hid],
                   epilogue=lambda acc, hid_: (acc * (2.0 * hid_.astype(F32)),))
    d_w_up = _matmul("d_mlp_up", xn2, dpre, "tn", [BF16], tm=1024, tn=D_FF // N_DEV, tk=4096, col_blocks_out=True)
    dh1, dh1_b, d_norm_mlp = _matmul("dx_mlp_up", dpre, w["w_up"], "nt", [F32, BF16], tk=D_FF,
                                     extras=[h1, dh2], consts=[w["norm_mlp_g"]], **nb)
    d_w_out = _matmul("d_out_proj", ycat, dh1_b, "tn", [BF16], tm=512, tn=1024, tk=4096)
    dycat = _matmul("dx_out_proj", dh1_b, w["w_out"], "nt", [F32], tm=1024, tn=1024, tk=D_MODEL)
    late_grads = dict(w_out=d_w_out, w_up=d_w_up, w_down=d_w_down, w_ple_gate=d_w_ple_gate, w_ple_proj=d_w_ple_proj)
    late_send = [late_grads[n] if n in ("w_up", "w_ple_proj") else _reshard(n, late_grads[n]) for n in _LATE]
    *late_flight, late_token = _scatter_start("late_scatter_start", late_send, [lax.empty(a.shape, a.dtype) for a in late_send])
    conv_w_bwd = w["conv_w"] + late_token[0:1, 0:1]

    def conv_bwd(dy, blk, cw):
        gb, gc, hx = blk[:, :LANE], blk[:, LANE:2 * LANE], blk[:, 2 * LANE:]
        uu = gc * hx
        u1, u2 = _shift_down(uu, 1), _shift_down(uu, 2)
        dconv = dy * gb
        du = dconv * cw[2:3] + _shift_up(dconv, 1) * cw[1:2] + _shift_up(dconv, 2) * cw[0:1]
        s = lambda z: jnp.sum(z, axis=0, keepdims=True)
        d_blk = jnp.concatenate([dy * (uu * cw[2:3] + u1 * cw[1:2] + u2 * cw[0:1]), du * hx, du * gc], axis=1)
        return d_blk, s(dconv * u2), s(dconv * u1), s(dconv * uu)

    dproj, dcw0, dcw1, dcw2 = _colwise(
        "conv_bwd", conv_bwd, n_cb, [(dycat, LANE), (proj, 3 * LANE)], [conv_w_bwd],
        [(IN_PAD, BF16, 3 * LANE)], [(1, CONV_DIM)] * 3)

    def post_bwd(dy, y, r, k_h_, v, g_, ln_g, ln_b, r_k):
        _, vjp = jax.vjp(_rwkv_post, y, r, k_h_, v, g_, ln_g, ln_b, r_k)
        return vjp(dy)

    dy_rec, dr_p, dk_p, dv_p, dg, d_ln_g, d_ln_b, d_r_k = _rowwise(
        "rwkv_post_bwd", post_bwd, [(dycat, 512, 1), y_rec, u_r, k_h, u_v, g], post_c,
        [(RWKV_DIM, F32)] * 5, [(1, RWKV_DIM)] * 3)
    dr_r, dv_r, dlw, dk_r, da, db = _rec_bwd(u, lw, k_h, ra, rb, rec_saved, dy_rec)

    dproj, d_mu, d_w0, d_a0, d_k_k, d_k_a, d_wl, d_al, d_gl = _rwkv_pre_bwd(
        proj, u, [dr_p, dr_r, dv_p, dv_r, dlw, dk_p, dk_r, da, db, dg], w["shift_mu"], small, dproj)
    d_w_in = _matmul("d_in_proj", xn1, dproj, "tn", [BF16], tm=1024, tn=896, tk=4096)
    early_grads = dict(conv_w=jnp.concatenate([dcw0, dcw1, dcw2], axis=0),
                       w_lora_up=d_wl[:64], a_lora_up=d_al[:64], g_lora_up=d_gl[:160])
    early_send = [_split_w_in_grad(d_w_in)] + [_reshard(n, early_grads[n]) for n in _EARLY[1:]]
    *early_flight, token = _scatter_start("early_scatter_start", early_send, [lax.empty(a.shape, a.dtype) for a in early_send])
    dx, d_norm_mix = _matmul(
        "dx_in_proj", dproj, w["w_in"], "nt", [F32], tk=IN_PAD, extras=[x, dh1], consts=[w["norm_mix_g"] + token[0:1, 0:1]],
        **dict(nb, epilogue=lambda *a: norm_bwd(*a)[1:]))

    grads = dict(
        norm_mix_g=d_norm_mix, shift_mu=d_mu, w0=d_w0, a0=d_a0, k_k=d_k_k, k_a=d_k_a, r_k=d_r_k,
        ln_x_g=d_ln_g, ln_x_b=d_ln_b, norm_mlp_g=d_norm_mlp, norm_ple_g=d_norm_ple, norm_final_g=d_norm_final)
    return loss, dx, grads, late_flight, early_flight, d_w_in


def _adam_update(partials, w_ref, m_ref, v_ref, g_ref, d_ref, nm_ref, nv_ref):
    g = partials[0].astype(F32)
    for part in partials[1:]:
        g = g + part.astype(F32)
    nm =ADAM_B1 * m_ref[...] + (1.0 - ADAM_B1) * g
    nv = ADAM_B2 * v_ref[...] + (1.0 - ADAM_B2) * (g * g)
    m_hat = nm / (1.0 - ADAM_B1 ** ADAM_STEP)
    v_hat = nv / (1.0 - ADAM_B2 ** ADAM_STEP)
    g_ref[...] = g
    d_ref[...] = -ADAM_LR * (m_hat / (jnp.sqrt(v_hat) + ADAM_EPS) + ADAM_WD * w_ref[...])
    nm_ref[...] = nm
    nv_ref[...] = nv


SMALL_ROWS = 8


def _small_layout(widths):
    widths = list(widths) + [1]
    fill, place = [0] * SMALL_ROWS, [None] * len(widths)
    for j in sorted(range(len(widths)), key=lambda q: -widths[q]):
        row = fill.index(min(fill))
        place[j] = (row, fill[row])
        fill[row] += -(-widths[j] // LANE) * LANE
    return place, max(fill)


def _pack_small(vecs, loss):
    place, total = _small_layout([v_.shape[1] for v_ in vecs])
    n = len(vecs)

    def body(*refs):
        out = jnp.zeros((SMALL_ROWS, total), F32)
        row_id = lax.broadcasted_iota(jnp.int32, (SMALL_ROWS, total), 0)
        for row in range(SMALL_ROWS):
            mine = sorted((off, j) for j, (r_, off) in enumerate(place) if r_ == row)
            pieces, at = [], 0
            for off, j in mine:
                val = refs[j][...]
                pieces.append(val)
                at = off + val.shape[1]
                pad = -val.shape[1] % LANE
                if pad:
                    pieces.append(jnp.zeros((1, pad), F32))
                    at += pad
            if total > at:
                pieces.append(jnp.zeros((1, total - at), F32))
            out = jnp.where(row_id == row, jnp.broadcast_to(jnp.concatenate(pieces, axis=1), (SMALL_ROWS, total)), out)
        refs[n + 1][...] = out

    return pl.pallas_call(body, name="pack_small", out_shape=jax.ShapeDtypeStruct((SMALL_ROWS, total), F32))(*vecs, loss)


def _adamw_small(packed, ws, ms, vs):
    n = len(ws)
    place, _ = _small_layout([w_.shape[1] for w_ in ws])

    def body(p_ref, *refs):
        w_refs, m_refs, v_refs, outs = refs[:n], refs[n:2 * n], refs[2 * n:3 * n], refs[3 * n:]
        for j in range(n):
            row, off = place[j]
            cols = pl.ds(off, ws[j].shape[1])
            _adam_update([p_ref[s, row:row + 1, cols] for s in range(N_DEV)], w_refs[j], m_refs[j], v_refs[j],
                         *outs[4 * j:4 * j + 4])
        row, off = place[n]
        total = p_ref[0, row:row + 1, off:off + 1]
        for s in range(1, N_DEV):
            total = total + p_ref[s, row:row + 1, off:off + 1]
        outs[4 * n][...] = total

    res = pl.pallas_call(
        body, name="adamw_small",
        out_shape=[jax.ShapeDtypeStruct(w_.shape, F32) for w_ in ws for _ in range(4)] + [jax.ShapeDtypeStruct((1, 1), F32)],
    )(packed, *ws, *ms, *vs)
    return [res[4 * j:4 * j + 4] for j in range(n)], res[4 * n]


def _adamw(name, parts, w, m, v, own=None, me=None):
    rows, cols = w.shape[-2:]
    lead = w.ndim - 2
    tr = rows if rows * cols * 4 * 8 <= (4 << 20) else max(8, (4 << 20) // (cols * 4 * 8) // 8 * 8)
    while rows % tr:
        tr -= 8
    shape4 = [jax.ShapeDtypeStruct(w.shape, F32)] * 4
    if own is None:
        def body(p_ref, *refs):
            _adam_update([p_ref[s] for s in range(N_DEV)], *refs)

        blk = pl.BlockSpec((None,) * lead + (tr, cols), lambda i: (0,) * lead + (i, 0))
        return pl.pallas_call(
            body, name=name, grid=(rows // tr,),
            in_specs=[pl.BlockSpec((N_DEV, tr, cols), lambda i: (0, i, 0)), blk, blk, blk], out_specs=[blk] * 4,
            out_shape=shape4, compiler_params=_params(("arbitrary",)),
        )(parts, w, m, v)

    def body_own(me_ref, p_ref, own_ref, *refs):
        mine = own_ref[...]
        _adam_update([jnp.where(me_ref[0] == s, mine, p_ref[s]) for s in range(N_DEV)], *refs)

    blk = pl.BlockSpec((None,) * lead + (tr, cols), lambda i, me_ref: (0,) * lead + (i, 0))
    return pl.pallas_call(
        body_own, name=name, out_shape=shape4,
        grid_spec=pltpu.PrefetchScalarGridSpec(
            num_scalar_prefetch=1, grid=(rows // tr,),
            in_specs=[pl.BlockSpec((N_DEV, tr, cols), lambda i, me_ref: (0, i, 0)),
                      pl.BlockSpec((None, tr, cols), lambda i, me_ref: (me_ref[0], i, 0)), blk, blk, blk],
            out_specs=[blk] * 4),
        compiler_params=_params(("arbitrary",)),
    )(me, parts, own, w, m, v)


def kernel(x, p, norm_mix_g, w_in, conv_w, shift_mu, w_lora_up, w0, a_lora_up, a0, g_lora_up, k_k, k_a, r_k, ln_x_g, ln_x_b, w_out, norm_mlp_g, w_up, w_down, norm_ple_g, w_ple_gate, w_ple_proj, norm_final_g, loss_target, m_norm_mix_g, m_w_in, m_conv_w, m_shift_mu, m_w_lora_up, m_w0, m_a_lora_up, m_a0, m_g_lora_up, m_k_k, m_k_a, m_r_k, m_ln_x_g, m_ln_x_b, m_w_out, m_norm_mlp_g, m_w_up, m_w_down, m_norm_ple_g, m_w_ple_gate, m_w_ple_proj, m_norm_final_g, v_norm_mix_g, v_w_in, v_conv_w, v_shift_mu, v_w_lora_up, v_w0, v_a_lora_up, v_a0, v_g_lora_up, v_k_k, v_k_a, v_r_k, v_ln_x_g, v_ln_x_b, v_w_out, v_norm_mlp_g, v_w_up, v_w_down, v_norm_ple_g, v_w_ple_gate, v_w_ple_proj, v_norm_final_g):
    args = dict(locals())
    wts = {n: args[n] for n in _WEIGHTS}
    mom = {n: args["m_" + n] for n in _WEIGHTS}
    var = {n: args["v_" + n] for n in _WEIGHTS}
    shard2d = lambda a: a.reshape(a.shape[-2:])
    pad_mu = lambda a: _pad_in_cols(jnp.concatenate([jnp.zeros((1, CONV_COLS), F32), a], axis=1))[:, CONV_COLS:]
    unpad_mu = lambda a: _unpad_in_cols(jnp.concatenate([jnp.zeros((1, CONV_COLS), F32), a], axis=1))[:, CONV_COLS:]

    shards = {n: shard2d(wts[n]).astype(BF16 if n in _BF16_GATHER else F32) for n in _SHARDED}
    w = {n: wts[n].reshape(1, -1) for n in _REPLICATED}
    w["shift_mu"] = pad_mu(wts["shift_mu"])

    loss, dx, grads, late_flight, early_flight, d_w_in = _local_step(
        x[0], p[0, 0], loss_target[0], w, [shards[n] for n in _EARLY], [shards[n] for n in _LATE])

    me = (4 * lax.axis_index("x") + 2 * lax.axis_index("y") + lax.axis_index("c")).astype(jnp.int32).reshape(1)
    late_sent, late_parts = _scatter_wait("late_scatter_wait", *late_flight, after=[d_w_in])
    out = {n: _adamw("adamw_" + n, prt, wts[n], mom[n], var[n], own=own, me=me)
           for n, prt, own in zip(_LATE, late_parts, late_sent)}
    early_sent, early_parts = _scatter_wait("early_scatter_wait", *early_flight, after=[dx] + [out[n][1] for n in _LATE])
    for n, prt, own in zip(_EARLY, early_parts, early_sent):
        out[n] = _adamw("adamw_" + n, prt, wts[n], mom[n], var[n], own=own, me=me)

    grads["shift_mu"] = unpad_mu(grads["shift_mu"])
    flat = lambda a: a.reshape(1, -1)
    (small_parts,) = _exchange("gather_small", [_pack_small([flat(grads[n]) for n in _REPLICATED], loss)], [False])
    small, loss_total = _adamw_small(small_parts, *[[flat(d[n]) for n in _REPLICATED] for d in (wts, mom, var)])
    for n, res in zip(_REPLICATED, small):
        out[n] = [r.reshape(wts[n].shape) for r in res]
    return (loss_total[0, 0], dx[None], *[out[n][0] for n in _WEIGHTS], *[out[n][1] for n in _WEIGHTS],
            *[out[n][2] for n in _WEIGHTS], *[out[n][3] for n in _WEIGHTS])
```

```python
import functools

import jax
import jax.numpy as jnp
from jax import lax
from jax.experimental import pallas as pl
from jax.experimental.pallas import tpu as pltpu

F32 = jnp.float32
BF16 = jnp.bfloat16

N_DEV = 8
D_MODEL = 1024
CONV_DIM = 512
RWKV_DIM = 512
HEAD_DIM = 64
N_HEADS = 8
D_FF = 4096
PLE_DIM = 256
RMS_EPS = 1e-6
GN_EPS = 64e-5
L2_EPS = 1e-12
ADAM_LR, ADAM_B1, ADAM_B2, ADAM_EPS, ADAM_WD, ADAM_STEP = 0.001, 0.9, 0.999, 1e-08, 0.01, 10

CONV_COLS = 3 * CONV_DIM
RW_PAD = 2048
IN_PAD = CONV_COLS + RW_PAD
IN_COLS = 3360
XW_OFF, XA_OFF, XG_OFF = 1536, 1664, 1792
REC_CHUNK = 128
REC_CHUNKS_PER_STEP = 2
REC_PASSES = 1
ROW_BLOCK = 256
LANE = 128
VMEM_LIMIT = 56 * 1024 * 1024


def _dims(dn, ndim):
    if ndim == 3:
        return {"nn": (((2,), (1,)), ((0,), (0,))), "nt": (((2,), (2,)), ((0,), (0,))),
                "tn": (((1,), (1,)), ((0,), (0,)))}[dn]
    return {"nn": (((1,), (0,)), ((), ())), "nt": (((1,), (1,)), ((), ())), "tn": (((0,), (0,)), ((), ()))}[dn]


def _split2(x):
    hi = x.astype(BF16)
    return hi, (x - hi.astype(F32)).astype(BF16)


def _mm_raw(x, y, dn, passes):
    f = lambda p, q: lax.dot_general(p, q, _dims(dn, x.ndim), preferred_element_type=F32)
    if passes == 1:
        return f(x.astype(BF16), y.astype(BF16))
    xh, xl = _split2(x)
    yh, yl = _split2(y)
    if passes == 2:
        return f(xh, yh) + f(xh, yl)
    return f(xh, yh) + f(xh, yl) + f(xl, yh)


@functools.partial(jax.custom_vjp, nondiff_argnums=(2, 3))
def _mm(x, y, dn, passes):
    return _mm_raw(x, y, dn, passes)


def _mm_fwd(x, y, dn, passes):
    return _mm_raw(x, y, dn, passes), (x, y)


def _mm_bwd(dn, passes, res, d):
    x, y = res
    if dn == "nn":
        return _mm(d, y, "nt", passes), _mm(x, d, "tn", passes)
    if dn == "nt":
        return _mm(d, y, "nn", passes), _mm(d, x, "tn", passes)
    return _mm(y, d, "nt", passes), _mm(x, d, "nn", passes)


_mm.defvjp(_mm_fwd, _mm_bwd)


def _head_ones():
    i = lax.broadcasted_iota(jnp.int32, (RWKV_DIM, RWKV_DIM), 0) // HEAD_DIM
    j = lax.broadcasted_iota(jnp.int32, (RWKV_DIM, RWKV_DIM), 1) // HEAD_DIM
    return (i == j).astype(BF16)


def _hsum_raw(x):
    ones = _head_ones()
    f = lambda p: lax.dot_general(p, ones, _dims("nn", 2), preferred_element_type=F32)
    x1, x2 = _split2(x)
    return f(x1) + f(x2)


@jax.custom_vjp
def _hsum(x):
    return _hsum_raw(x)


_hsum.defvjp(lambda x: (_hsum_raw(x), None), lambda _, d: (_hsum(d),))


def _sigmoid(x):
    return 0.5 + 0.5 * jnp.tanh(0.5 * x)


def _softplus(x):
    return jnp.maximum(x, 0.0) + jnp.log(1.0 + jnp.exp(-jnp.abs(x)))


def _params(sem):
    return pltpu.CompilerParams(dimension_semantics=sem, vmem_limit_bytes=VMEM_LIMIT)


def _rowwise(name, fn, rows, consts, row_outs, acc_outs=(), tr=ROW_BLOCK, halo=False, gather=()):
    rows = [r if isinstance(r, tuple) else (r, r.shape[1], 0) for r in rows]
    t_len = rows[0][0].shape[0]
    tr = min(tr, t_len)
    n_r, n_c, n_o, n_a, n_x = len(rows), len(consts), len(row_outs), len(acc_outs), len(gather)
    n_h = n_r if halo else 0
    sub = 8
    x_specs, x_shapes, x_sems = _gather_io(gather) if n_x else ([], [], [])
    nb = t_len // tr

    def body(*refs):
        if n_x:
            n_in = n_r + n_h + n_c
            start, forward, wait = _gather_plan(refs[n_in:n_in + n_x], refs[len(refs) - 3 - n_x:len(refs) - 3], *refs[len(refs) - 3:])
            pl.when(pl.program_id(0) == 0)(start)
            refs = refs[:n_in] + refs[n_in + n_x:len(refs) - 3 - n_x]
        ins = [r[...] for r in refs[:n_r]]
        ins += [jnp.where(pl.program_id(0) == 0, 0.0, r[sub - 1:sub, :]) for r in refs[n_r:n_r + n_h]]
        ins += [r[...] for r in refs[n_r + n_h:n_r + n_h + n_c]]
        refs = refs[:n_r] + refs[n_r + n_h:]
        outs = fn(*ins)
        o_refs = refs[n_r + n_c:n_r + n_c + n_o]
        a_refs = refs[n_r + n_c + n_o:]
        for o_ref, val in zip(o_refs, outs[:n_o]):
            o_ref[...] = val.astype(o_ref.dtype)
        if n_a:
            first = pl.program_id(0) == 0

            @pl.when(first)
            def _():
                for a_ref, val in zip(a_refs, outs[n_o:]):
                    a_ref[...] = val

            @pl.when(jnp.logical_not(first))
            def _():
                for a_ref, val in zip(a_refs, outs[n_o:]):
                    a_ref[...] += val

        if n_x:
            @pl.when(pl.program_id(0) == nb - 1)
            def _():
                for j in range(n_x):
                    forward(j)
                wait()

    in_specs = [pl.BlockSpec((tr, w), functools.partial(lambda i, c: (i, c), c=cb)) for _, w, cb in rows]
    if halo:
        in_specs += [pl.BlockSpec((sub, w), functools.partial(lambda i, c: (jnp.maximum(i * (tr // sub) - 1, 0), c), c=cb))
                     for _, w, cb in rows]
    in_specs += [pl.BlockSpec(c.shape, functools.partial(lambda i, n: (0,) * n, n=c.ndim)) for c in consts]
    out_specs = [pl.BlockSpec((tr, w), lambda i: (i, 0)) for w, _ in row_outs]
    out_specs += [pl.BlockSpec(s, functools.partial(lambda i, n: (0,) * n, n=len(s))) for s in acc_outs]
    out_shape = [jax.ShapeDtypeStruct((t_len, w), dt) for w, dt in row_outs]
    out_shape += [jax.ShapeDtypeStruct(s, F32) for s in acc_outs]
    return pl.pallas_call(
        body, name=name, grid=(nb,), in_specs=in_specs + x_specs, out_specs=out_specs + x_specs,
        out_shape=out_shape + x_shapes, scratch_shapes=x_sems,
        compiler_params=pltpu.CompilerParams(dimension_semantics=("arbitrary",), vmem_limit_bytes=VMEM_LIMIT,
                                             has_side_effects=bool(n_x)),
    )(*[r[0] for r in rows], *([r[0] for r in rows] if halo else []), *consts, *gather)


def _colwise(name, fn, n_blocks, cols, prms, col_outs, prm_outs=()):
    t_len = cols[0][0].shape[0]
    n_i = len(cols) + len(prms)

    def body(*refs):
        outs = fn(*[r[...] for r in refs[:n_i]])
        for o_ref, val in zip(refs[n_i:], outs):
            o_ref[...] = val.astype(o_ref.dtype)

    spec = lambda r, w: pl.BlockSpec((r, w), lambda j: (0, j))
    in_specs = [spec(t_len, w) for _, w in cols] + [spec(a.shape[0], LANE) for a in prms]
    out_specs = [spec(t_len, bw) for _, _, bw in col_outs] + [spec(r, LANE) for r, _ in prm_outs]
    out_shape = [jax.ShapeDtypeStruct((t_len, w), dt) for w, dt, _ in col_outs]
    out_shape += [jax.ShapeDtypeStruct((r, w), F32) for r, w in prm_outs]
    return pl.pallas_call(
        body, name=name, grid=(n_blocks,), in_specs=in_specs, out_specs=out_specs, out_shape=out_shape,
        compiler_params=_params(("arbitrary",)),
    )(*[c[0] for c in cols], *prms)


def _matmul(name, a, b, dn, outs, *, tm, tn, tk, extras=(), consts=(), epilogue=None, sums=(), a_map=None,
            col_blocks_out=False):
    if dn == "nn":
        (m, k), n = a.shape, b.shape[1]
    elif dn == "nt":
        (m, k), n = a.shape, b.shape[0]
    else:
        (k, m), n = a.shape, b.shape[1]
    tm, tn, tk = min(tm, m), min(tn, n), min(tk, k)
    nk = k // tk
    grid = (m // tm, n // tn, nk)
    assert nk == 1 and (not sums or grid[1] == 1)
    a_spec = pl.BlockSpec((tk, tm), lambda i, j, q: (q, i)) if dn == "tn" else pl.BlockSpec((tm, tk), lambda i, j, q: (i, q))
    b_spec = pl.BlockSpec((tn, tk), lambda i, j, q: (j, q)) if dn == "nt" else pl.BlockSpec((tk, tn), lambda i, j, q: (q, j))
    o_spec = pl.BlockSpec((tm, tn), lambda i, j, q: (i, j))
    c_spec = pl.BlockSpec((1, tn), lambda i, j, q: (0, j))
    n_e, n_c, n_o, n_s = len(extras), len(consts), len(outs), len(sums)

    def body(*refs):
        a_ref, b_ref = refs[:2]
        e_refs = refs[2:2 + n_e + n_c]
        o_refs, s_refs = refs[2 + n_e + n_c:2 + n_e + n_c + n_o], refs[2 + n_e + n_c + n_o:]
        step = pl.program_id(0) * grid[1] + pl.program_id(1)
        a_blk = a_ref[...] if a_map is None else a_map(a_ref[...])
        acc = lax.dot_general(a_blk.astype(BF16), b_ref[...].astype(BF16), _dims(dn, 2), preferred_element_type=F32)
        vals = (acc,) if epilogue is None else epilogue(acc, *[e[...] for e in e_refs])
        for o_ref, val in zip(o_refs, vals[:n_o]):
            o_ref[...] = val.astype(o_ref.dtype)
        if n_s:
            @pl.when(step == 0)
            def _():
                for s_ref, val in zip(s_refs, vals[n_o:]):
                    s_ref[...] = val

            @pl.when(step > 0)
            def _():
                for s_ref, val in zip(s_refs, vals[n_o:]):
                    s_ref[...] += val

    res = pl.pallas_call(
        body, name=name, grid=grid,
        in_specs=[a_spec, b_spec] + [o_spec] * n_e + [c_spec] * n_c,
        out_specs=[pl.BlockSpec((None, tm, tn), lambda i, j, q: (j, i, 0)) if col_blocks_out else o_spec] * n_o
                  + [c_spec] * n_s,
        out_shape=[jax.ShapeDtypeStruct((n // tn, m, tn) if col_blocks_out else (m, n), dt) for dt in outs]
                  + [jax.ShapeDtypeStruct(s, F32) for s in sums],
        compiler_params=_params(("arbitrary",) * 3 if n_s else ("parallel", "parallel", "arbitrary")),
    )(a, b, *extras, *consts)
    return res[0] if len(res) == 1 else res


def _rms(h, g):
    return h * lax.rsqrt(jnp.mean(h * h, axis=-1, keepdims=True) + RMS_EPS) * g


def _rms_bwd(h, g, dy):
    rs = lax.rsqrt(jnp.mean(h * h, axis=-1, keepdims=True) + RMS_EPS)
    n = h * rs
    dn = dy * g
    dh = rs * (dn - n * jnp.mean(dn * n, axis=-1, keepdims=True))
    return dh, jnp.sum(dy * n, axis=0, keepdims=True)


def _rwkv_pre(k, xw, xa, xg, w0, a0, k_k, k_a, wl, al, gl):
    zw = w0 + _mm(jnp.tanh(xw), wl, "nn", 1)
    lw = -jnp.exp(-_softplus(-zw) - 0.5)
    iclr = _sigmoid(a0 + _mm(xa, al, "nn", 1))
    g = _mm(_sigmoid(xg), gl, "nn", 1)
    kk0 = k * k_k
    kk = kk0 * lax.rsqrt(jnp.maximum(_hsum(kk0 * kk0), L2_EPS * L2_EPS))
    k_h = k * (1.0 + (iclr - 1.0) * k_a)
    return lw, k_h, -kk, kk * iclr, g


def _rwkv_post(y, r, k_h, v, g, ln_g, ln_b, r_k):
    mu = _hsum(y) * (1.0 / HEAD_DIM)
    yc = y - mu
    var = _hsum(yc * yc) * (1.0 / HEAD_DIM)
    yo = yc * lax.rsqrt(var + GN_EPS) * ln_g + ln_b
    bonus = _hsum(r * k_h * r_k) * v
    return (yo + bonus) * g


def _shift_down(x, n):
    rows = lax.broadcasted_iota(jnp.int32, x.shape, 0)
    return jnp.where(rows < n, 0.0, pltpu.roll(x, n, 0))


def _shift_up(x, n):
    t_len = x.shape[0]
    rows = lax.broadcasted_iota(jnp.int32, x.shape, 0)
    return jnp.where(rows >= t_len - n, 0.0, pltpu.roll(x, t_len - n, 0))


def _gather_plan(ins, outs, send_sems, recv_sems, local_sems):
    x, y, c = lax.axis_index("x"), lax.axis_index("y"), lax.axis_index("c")
    me = 4 * x + 2 * y + c
    direct, chips = (1, 2, 4, 6), (2, 4, 6)

    def local(i):
        return pltpu.make_async_copy(ins[i], outs[i].at[me], local_sems.at[i])

    def send(i, rel):
        return pltpu.make_async_remote_copy(
            src_ref=ins[i], dst_ref=outs[i].at[me], send_sem=send_sems.at[i, rel - 1], recv_sem=recv_sems.at[i, rel - 1],
            device_id=(x ^ (rel >> 2), y ^ ((rel >> 1) & 1), c ^ (rel & 1)), device_id_type=pl.DeviceIdType.MESH)

    def passed(i, rel):
        slot = outs[i].at[me ^ rel]
        return pltpu.make_async_remote_copy(
            src_ref=slot, dst_ref=slot, send_sem=send_sems.at[i, rel], recv_sem=recv_sems.at[i, rel],
            device_id=(x, y, 1 - c), device_id_type=pl.DeviceIdType.MESH)

    def landed(i, rel):
        slot = outs[i].at[me ^ rel]
        return pltpu.make_async_remote_copy(
            src_ref=slot, dst_ref=slot, send_sem=send_sems.at[i, rel - 1], recv_sem=recv_sems.at[i, rel - 1],
            device_id=(x, y, c), device_id_type=pl.DeviceIdType.MESH)

    def start():
        for i in range(len(ins)):
            local(i).start()
            for rel in direct:
                send(i, rel).start()

    def forward(i):
        for rel in chips:
            landed(i, rel).wait_recv()
            passed(i, rel).start()

    def wait():
        for i in range(len(ins)):
            local(i).wait()
            for rel in (1, 3, 5, 7):
                landed(i, rel).wait_recv()
            for rel in direct:
                send(i, rel).wait_send()
            for rel in chips:
                passed(i, rel).wait_send()

    return start, forward, wait


def _gather_io(arrays):
    n = len(arrays)
    any_spec = pl.BlockSpec(memory_space=pl.ANY)
    out_shape = [jax.ShapeDtypeStruct((N_DEV,) + a.shape, a.dtype) for a in arrays]
    sems = [pltpu.SemaphoreType.DMA((n, N_DEV - 1)), pltpu.SemaphoreType.DMA((n, N_DEV - 1)), pltpu.SemaphoreType.DMA((n,))]
    return [any_spec] * n, out_shape, sems


def _all_gather(name, arrays):
    n = len(arrays)
    specs, out_shape, sems = _gather_io(arrays)

    def body(*refs):
        start, forward, wait = _gather_plan(refs[:n], refs[n:2 * n], *refs[2 * n:])
        start()
        for i in range(n):
            forward(i)
        wait()

    return pl.pallas_call(
        body, name=name, in_specs=specs, out_specs=specs, out_shape=out_shape, scratch_shapes=sems,
        compiler_params=pltpu.CompilerParams(has_side_effects=True),
    )(*arrays)


def _scatter_start(name, arrays, lands):
    n = len(arrays)
    hbm = pl.BlockSpec(memory_space=pltpu.HBM)

    def body(*refs):
        ins, land, send_sems, recv_sems = refs[:n], refs[n:2 * n], refs[2 * n], refs[2 * n + 1]
        token = refs[4 * n + 2]
        x, y, c = lax.axis_index("x"), lax.axis_index("y"), lax.axis_index("c")
        me = 4 * x + 2 * y + c
        for i in range(n):
            for rel in range(1, N_DEV):
                k = i * (N_DEV - 1) + rel - 1
                pltpu.make_async_remote_copy(
                    src_ref=ins[i].at[me ^ rel], dst_ref=land[i].at[me], send_sem=send_sems.at[k],
                    recv_sem=recv_sems.at[k], device_id=(x ^ (rel >> 2), y ^ ((rel >> 1) & 1), c ^ (rel & 1)),
                    device_id_type=pl.DeviceIdType.MESH).start()
        token[...] = jnp.zeros_like(token)

    sem = pltpu.SemaphoreType.DMA((n * (N_DEV - 1),))
    bufs = [pltpu.HBM(a.shape, a.dtype) for a in list(arrays) + list(lands)]
    res = pl.pallas_call(
        body, name=name, out_shape=(sem, sem, *bufs, jax.ShapeDtypeStruct((8, LANE), F32)),
        in_specs=[hbm] * (2 * n),
        out_specs=(pl.BlockSpec(memory_space=pltpu.SEMAPHORE),) * 2 + (hbm,) * (2 * n) + (pl.BlockSpec(memory_space=pltpu.VMEM),),
        input_output_aliases={i: 2 + i for i in range(2 * n)},
        compiler_params=pltpu.CompilerParams(has_side_effects=pltpu.SideEffectType.DATAFLOW_SIDE_EFFECTING),
    )(*[pltpu.with_memory_space_constraint(a, pltpu.HBM) for a in list(arrays) + list(lands)])
    return res[0], res[1], res[2:2 + n], res[2 + n:2 + 2 * n], res[2 + 2 * n]


def _scatter_wait(name, send_sems, recv_sems, arrays, lands, after):
    n, n_after = len(arrays), len(after)
    hbm = pl.BlockSpec(memory_space=pltpu.HBM)

    def body(*refs):
        ins, land, s_sems, r_sems = refs[:n], refs[n:2 * n], refs[2 * n], refs[2 * n + 1]
        x, y, c = lax.axis_index("x"), lax.axis_index("y"), lax.axis_index("c")
        me = 4 * x + 2 * y + c
        for i in range(n):
            for rel in range(1, N_DEV):
                k = i * (N_DEV - 1) + rel - 1
                cp = pltpu.make_async_remote_copy(
                    src_ref=ins[i].at[me ^ rel], dst_ref=land[i].at[me ^ rel], send_sem=s_sems.at[k],
                    recv_sem=r_sems.at[k], device_id=(x, y, c), device_id_type=pl.DeviceIdType.MESH)
                cp.wait_send()
                cp.wait_recv()

    res = pl.pallas_call(
        body, name=name, out_shape=[pltpu.HBM(a.shape, a.dtype) for a in list(arrays) + list(lands)],
        in_specs=[hbm] * (2 * n) + [pl.BlockSpec(memory_space=pltpu.SEMAPHORE)] * 2 + [pl.BlockSpec(memory_space=pl.ANY)] * n_after,
        out_specs=[hbm] * (2 * n), input_output_aliases={i: i for i in range(2 * n)},
        compiler_params=pltpu.CompilerParams(has_side_effects=pltpu.SideEffectType.DATAFLOW_SIDE_EFFECTING),
    )(*arrays, *lands, send_sems, recv_sems, *after)
    return res[:n], res[n:]


def _gather_start(name, shards, after):
    n, n_after = len(shards), len(after)
    hbm = pl.BlockSpec(memory_space=pltpu.HBM)
    lands = [lax.empty((N_DEV,) + a.shape, a.dtype) for a in shards]

    def body(*refs):
        ins, land = refs[:n], refs[n:2 * n]
        outs = refs[2 * n + n_after:]
        send_sems, recv_sems, token = outs[:n], outs[n:2 * n], outs[4 * n]
        x, y, c = lax.axis_index("x"), lax.axis_index("y"), lax.axis_index("c")
        me = 4 * x + 2 * y + c
        for i in range(n):
            for rel in range(N_DEV):
                pltpu.make_async_remote_copy(
                    src_ref=ins[i], dst_ref=land[i].at[me], send_sem=send_sems[i].at[rel],
                    recv_sem=recv_sems[i].at[rel], device_id=(x ^ (rel >> 2), y ^ ((rel >> 1) & 1), c ^ (rel & 1)),
                    device_id_type=pl.DeviceIdType.MESH).start()
        token[...] = jnp.zeros_like(token)

    sem = pltpu.SemaphoreType.DMA((N_DEV,))
    bufs = [pltpu.HBM(a.shape, a.dtype) for a in list(shards) + lands]
    res = pl.pallas_call(
        body, name=name, out_shape=(*[sem] * (2 * n), *bufs, jax.ShapeDtypeStruct((8, LANE), F32)),
        in_specs=[hbm] * (2 * n) + [pl.BlockSpec(memory_space=pl.ANY)] * n_after,
        out_specs=(pl.BlockSpec(memory_space=pltpu.SEMAPHORE),) * (2 * n) + (hbm,) * (2 * n)
                  + (pl.BlockSpec(memory_space=pltpu.VMEM),),
        input_output_aliases={i: 2 * n + i for i in range(2 * n)},
        compiler_params=pltpu.CompilerParams(has_side_effects=pltpu.SideEffectType.DATAFLOW_SIDE_EFFECTING),
    )(*[pltpu.with_memory_space_constraint(a, pltpu.HBM) for a in list(shards) + lands], *after)
    return res[:n], res[n:2 * n], res[2 * n:3 * n], res[3 * n:4 * n], res[4 * n]


def _gather_wait(name, send_sems, recv_sems, shards, lands, after):
    n, n_after = len(shards), len(after)
    hbm = pl.BlockSpec(memory_space=pltpu.HBM)

    def body(*refs):
        ins, land = refs[:n], refs[n:2 * n]
        s_sems, r_sems = refs[2 * n:3 * n], refs[3 * n:4 * n]
        x, y, c = lax.axis_index("x"), lax.axis_index("y"), lax.axis_index("c")
        me = 4 * x + 2 * y + c
        for i in range(n):
            for rel in range(N_DEV):
                cp = pltpu.make_async_remote_copy(
                    src_ref=ins[i], dst_ref=land[i].at[me ^ rel], send_sem=s_sems[i].at[rel],
                    recv_sem=r_sems[i].at[rel], device_id=(x, y, c), device_id_type=pl.DeviceIdType.MESH)
                cp.wait_send()
                cp.wait_recv()

    res = pl.pallas_call(
        body, name=name, out_shape=[pltpu.HBM(a.shape, a.dtype) for a in list(shards) + list(lands)],
        in_specs=[hbm] * (2 * n) + [pl.BlockSpec(memory_space=pltpu.SEMAPHORE)] * (2 * n)
                 + [pl.BlockSpec(memory_space=pl.ANY)] * n_after,
        out_specs=[hbm] * (2 * n), input_output_aliases={i: i for i in range(2 * n)},
        compiler_params=pltpu.CompilerParams(has_side_effects=pltpu.SideEffectType.DATAFLOW_SIDE_EFFECTING),
    )(*shards, *lands, *send_sems, *recv_sems, *after)
    return res[n:]


def _tri_powers(low):
    powers, n, p = [low.astype(BF16)], 1, low
    while 2 * n < low.shape[-1]:
        p = _mm(p, p, "nn", REC_PASSES)
        powers.append(p.astype(BF16))
        n *= 2
    return powers


@jax.custom_vjp
def _tri_solve(low, rhs, powers):
    del low
    for p in powers:
        rhs = rhs + _mm(p, rhs, "nn", REC_PASSES)
    return rhs


def _tri_solve_fwd(low, rhs, powers):
    out = _tri_solve(low, rhs, powers)
    return out, (powers, out)


def _tri_solve_bwd(res, d):
    powers, u = res
    for p in powers:
        d = d + _mm(p, d, "tn", REC_PASSES)
    return _mm(d, u, "nt", REC_PASSES), d, [jnp.zeros_like(p) for p in powers]


_tri_solve.defvjp(_tri_solve_fwd, _tri_solve_bwd)


@jax.custom_vjp
def _tri_solve_given(low, rhs, powers, value):
    del low, rhs, powers
    return value


def _tri_solve_given_fwd(low, rhs, powers, value):
    return value, (powers, value)


def _tri_solve_given_bwd(res, d):
    return _tri_solve_bwd(res, d) + (jnp.zeros_like(res[1]),)


_tri_solve_given.defvjp(_tri_solve_given_fwd, _tri_solve_given_bwd)


def _heads(x):
    return jnp.stack([x[:, h * HEAD_DIM:(h + 1) * HEAD_DIM] for h in range(N_HEADS)])


def _unheads(x):
    return jnp.concatenate([x[h] for h in range(N_HEADS)], axis=-1)


def _causal_masks(c):
    ti = lax.broadcasted_iota(jnp.int32, (c, c), 0)
    si = lax.broadcasted_iota(jnp.int32, (c, c), 1)
    strict, incl = si < ti, si <= ti
    both = jnp.concatenate([jnp.concatenate([strict, strict], axis=1), jnp.concatenate([incl, incl], axis=1)], axis=0)
    return strict, incl, both


@jax.custom_vjp
def _gram_given(x2, y2, value):
    del x2, y2
    return value.astype(F32)


def _gram_given_fwd(x2, y2, value):
    return value.astype(F32), (x2, y2, value)


def _gram_given_bwd(res, d):
    x2, y2, value = res
    d = jnp.where(_causal_masks(d.shape[-1] // 2)[2], d, 0.0)
    return _mm(d, y2, "nn", 2), _mm(d, x2, "tn", 2), jnp.zeros_like(value)


_gram_given.defvjp(_gram_given_fwd, _gram_given_bwd)


def _chunk_fwd(z0, r, lw, k, v, a, b, powers=None, gram_value=None, u_value=None):
    c = r.shape[0]
    n_h, n_k = z0.shape[0], z0.shape[1]
    mm = functools.partial(_mm, passes=REC_PASSES)
    gram = functools.partial(_mm, passes=2)
    _, incl, mask = _causal_masks(c)
    cum = _mm(incl.astype(F32), lw, "nn", 3)
    cum_end = cum[c - 1:c, :]
    e_neg, e_end = jnp.exp(-cum), jnp.exp(cum_end - cum)
    x2 = jnp.concatenate([_heads(a * jnp.exp(cum - lw)), _heads(r * jnp.exp(cum))], axis=1)
    y2 = jnp.concatenate([_heads(b * e_neg), _heads(k * e_neg)], axis=1)
    vh = _heads(v)
    g2 = jnp.where(mask, gram(x2, y2, "nt"), 0.0) if gram_value is None else _gram_given(x2, y2, gram_value)
    t2 = mm(x2, z0, "nn") + mm(g2[:, :, c:], vh, "nn")
    low = g2[:, :c, :c]
    powers = _tri_powers(low) if powers is None else powers
    u = _tri_solve(low, t2[:, :c], powers) if u_value is None else _tri_solve_given(low, t2[:, :c], powers, u_value)
    y = t2[:, c:] + mm(g2[:, c:, :c], u, "nn")
    ki = lax.broadcasted_iota(jnp.int32, (n_k, n_k), 0)
    kj = lax.broadcasted_iota(jnp.int32, (n_k, n_k), 1)
    dmat = jnp.where(ki == kj, jnp.broadcast_to(_heads(jnp.exp(cum_end)), (n_h, n_k, n_k)), 0.0)
    z_end = mm(dmat, z0, "nn") + mm(jnp.concatenate([_heads(b * e_end), _heads(k * e_end)], axis=1),
                                    jnp.concatenate([u, vh], axis=1), "tn")
    return _unheads(y), z_end, powers, g2, u


def _rec_fwd(u, lw, k, a, b):
    t_len = lw.shape[0]
    c = min(REC_CHUNK, t_len)
    nc = t_len // c
    per = REC_CHUNKS_PER_STEP if nc % REC_CHUNKS_PER_STEP == 0 else 1
    steps = nc // per
    n_pow = max(1, (c - 1).bit_length())

    def body(r_ref, v_ref, lw_ref, k_ref, a_ref, b_ref, y_ref, zs_ref, pw_ref, gs_ref, us_ref, z_scr):
        @pl.when(pl.program_id(0) == 0)
        def _():
            z_scr[...] = jnp.zeros_like(z_scr)

        for s in range(per):
            rows = pl.ds(s * c, c)
            z0 = z_scr[...]
            zs_ref[s] = z0
            y, z_end, powers, g2, u_rows = _chunk_fwd(z0, r_ref[rows, :], lw_ref[rows, :], k_ref[rows, :], v_ref[rows, :],
                                                      a_ref[rows, :], b_ref[rows, :])
            y_ref[rows, :] = y
            z_scr[...] = z_end
            pw_ref[s] = jnp.concatenate(powers, axis=0)
            gs_ref[s] = g2.astype(BF16)
            us_ref[s] = u_rows

    blk = lambda cb: pl.BlockSpec((per * c, RWKV_DIM), functools.partial(lambda i, q: (i, q), q=cb))
    res = pl.pallas_call(
        body, name="rwkv_rec_fwd", grid=(steps,),
        in_specs=[blk(0), blk(2)] + [blk(0)] * 4,
        out_specs=[blk(0), pl.BlockSpec((per, N_HEADS, HEAD_DIM, HEAD_DIM), lambda i: (i, 0, 0, 0)),
                   pl.BlockSpec((per, n_pow * N_HEADS, c, c), lambda i: (i, 0, 0, 0)),
                   pl.BlockSpec((per, N_HEADS, 2 * c, 2 * c), lambda i: (i, 0, 0, 0)),
                   pl.BlockSpec((per, N_HEADS, c, HEAD_DIM), lambda i: (i, 0, 0, 0))],
        out_shape=[jax.ShapeDtypeStruct((t_len, RWKV_DIM), F32),
                   jax.ShapeDtypeStruct((nc, N_HEADS, HEAD_DIM, HEAD_DIM), F32),
                   jax.ShapeDtypeStruct((nc, n_pow * N_HEADS, c, c), BF16),
                   jax.ShapeDtypeStruct((nc, N_HEADS, 2 * c, 2 * c), BF16),
                   jax.ShapeDtypeStruct((nc, N_HEADS, c, HEAD_DIM), F32)],
        scratch_shapes=[pltpu.VMEM((N_HEADS, HEAD_DIM, HEAD_DIM), F32)], compiler_params=_params(("arbitrary",)),
    )(u, u, lw, k, a, b)
    return res[0], res[1:]


def _rec_bwd(u, lw, k, a, b, saved, dy):
    t_len = lw.shape[0]
    c = min(REC_CHUNK, t_len)
    nc = t_len // c
    per = REC_CHUNKS_PER_STEP if nc % REC_CHUNKS_PER_STEP == 0 else 1
    steps = nc // per

    zs, pw, gs, us = saved

    def body(r_ref, v_ref, lw_ref, k_ref, a_ref, b_ref, zs_ref, pw_ref, gs_ref, us_ref, dy_ref, *rest):
        g_refs, dz_scr = rest[:6], rest[6]

        @pl.when(pl.program_id(0) == 0)
        def _():
            dz_scr[...] = jnp.zeros_like(dz_scr)

        for s in reversed(range(per)):
            rows = pl.ds(s * c, c)
            powers = [pw_ref[s, j * N_HEADS:(j + 1) * N_HEADS] for j in range(pw.shape[1] // N_HEADS)]
            chunk = functools.partial(lambda gram, u_val, pws, *xs: _chunk_fwd(*xs, powers=pws, gram_value=gram, u_value=u_val)[:2],
                                      gs_ref[s], us_ref[s], powers)
            _, vjp = jax.vjp(chunk, zs_ref[s], r_ref[rows, :], lw_ref[rows, :], k_ref[rows, :], v_ref[rows, :],
                             a_ref[rows, :], b_ref[rows, :])
            dz0, dr, dlw, dk, dv, da, db = vjp((dy_ref[rows, :], dz_scr[...]))
            for ref, val in zip(g_refs, (dr, dv, dlw, dk, da, db)):
                ref[rows, :] = val
            dz_scr[...] = dz0

    blk = lambda cb: pl.BlockSpec((per * c, RWKV_DIM), functools.partial(lambda i, q: (steps - 1 - i, q), q=cb))
    saved_blk = lambda arr: pl.BlockSpec((per,) + arr.shape[1:], lambda i: (steps - 1 - i, 0, 0, 0))
    return pl.pallas_call(
        body, name="rwkv_rec_bwd", grid=(steps,),
        in_specs=[blk(0), blk(2)] + [blk(0)] * 4 + [saved_blk(zs), saved_blk(pw), saved_blk(gs), saved_blk(us), blk(0)],
        out_specs=[blk(0)] * 6, out_shape=[jax.ShapeDtypeStruct((t_len, RWKV_DIM), F32)] * 6,
        scratch_shapes=[pltpu.VMEM((N_HEADS, HEAD_DIM, HEAD_DIM), F32)], compiler_params=_params(("arbitrary",)),
    )(u, u, lw, k, a, b, zs, pw, gs, us, dy)


_EARLY = ["w_in", "conv_w", "w_lora_up", "a_lora_up", "g_lora_up"]
_LATE = ["w_out", "w_up", "w_down", "w_ple_gate", "w_ple_proj"]
_SHARDED = _EARLY + _LATE
_COL_SHARDED = {"w_in", "conv_w", "w_lora_up", "a_lora_up", "g_lora_up", "w_up", "w_ple_proj"}
_BF16_GATHER = {"w_in", "w_out", "w_up", "w_down", "w_ple_gate", "w_ple_proj"}
_REPLICATED = ["norm_mix_g", "shift_mu", "w0", "a0", "k_k", "k_a", "r_k", "ln_x_g", "ln_x_b", "norm_mlp_g", "norm_ple_g",
               "norm_final_g"]
_WEIGHTS = ["norm_mix_g", "w_in", "conv_w", "shift_mu", "w_lora_up", "w0", "a_lora_up", "a0", "g_lora_up", "k_k", "k_a", "r_k",
            "ln_x_g", "ln_x_b", "w_out", "norm_mlp_g", "w_up", "w_down", "norm_ple_g", "w_ple_gate", "w_ple_proj", "norm_final_g"]


def _unshard(name, g):
    if name in _COL_SHARDED:
        return jnp.moveaxis(g, 0, 1).reshape(g.shape[1], N_DEV * g.shape[2])
    return g.reshape(N_DEV * g.shape[1], g.shape[2])


def _reshard(name, full):
    if name in _COL_SHARDED:
        return jnp.moveaxis(full.reshape(full.shape[0], N_DEV, full.shape[1] // N_DEV), 1, 0)
    return full.reshape(N_DEV, full.shape[0] // N_DEV, full.shape[1])


def _pad_in_cols(a):
    z = lambda n: jnp.zeros(a.shape[:-1] + (n,), a.dtype)
    conv = [a[..., part * CONV_DIM + j * LANE:part * CONV_DIM + (j + 1) * LANE] for j in range(CONV_DIM // LANE) for part in range(3)]
    return jnp.concatenate(conv + [a[..., CONV_COLS:3136], z(64), a[..., 3136:3200], z(64), a[..., 3200:3360], z(96)], axis=-1)


def _unpad_in_cols(a):
    conv = [a[..., (3 * j + part) * LANE:(3 * j + part + 1) * LANE] for part in range(3) for j in range(CONV_DIM // LANE)]
    return jnp.concatenate(conv + [a[..., CONV_COLS:3136], a[..., 3200:3264], a[..., 3328:3488]], axis=-1)


def _assemble_w_in(g):
    n_dev, rows, cols = g.shape

    def body(g_ref, o_ref):
        o_ref[...] = _pad_in_cols(jnp.concatenate([g_ref[d] for d in range(n_dev)], axis=1))

    return pl.pallas_call(
        body, name="w_in_assemble", grid=(rows // ROW_BLOCK,),
        in_specs=[pl.BlockSpec((n_dev, ROW_BLOCK, cols), lambda i: (0, i, 0))],
        out_specs=pl.BlockSpec((ROW_BLOCK, IN_PAD), lambda i: (i, 0)),
        out_shape=jax.ShapeDtypeStruct((rows, IN_PAD), g.dtype), compiler_params=_params(("arbitrary",)),
    )(g)


def _split_w_in_grad(dw):
    rows = dw.shape[0]
    cols = IN_COLS // N_DEV

    def body(d_ref, o_ref):
        full = _unpad_in_cols(d_ref[...])
        for d in range(N_DEV):
            o_ref[d] = full[:, cols * d:cols * (d + 1)]

    return pl.pallas_call(
        body, name="w_in_grad_split", grid=(rows // ROW_BLOCK,),
        in_specs=[pl.BlockSpec((ROW_BLOCK, IN_PAD), lambda i: (i, 0))],
        out_specs=pl.BlockSpec((N_DEV, ROW_BLOCK, cols), lambda i: (0, i, 0)),
        out_shape=jax.ShapeDtypeStruct((N_DEV, rows, cols), dw.dtype), compiler_params=_params(("arbitrary",)),
    )(dw)


def _pad_rows(a, rows):
    return jnp.concatenate([a, jnp.zeros((rows - a.shape[0],) + a.shape[1:], a.dtype)], axis=0)


SEG_W = [RWKV_DIM, RWKV_DIM, RWKV_DIM, LANE, LANE, 2 * LANE]
SEG_OFF = [0, 512, 1024, XW_OFF, XA_OFF, XG_OFF]


def _rwkv_pre_bwd(proj, u, grads, mu, small, dproj):
    t_len = u.shape[0]
    tr = min(ROW_BLOCK, t_len)
    nb = t_len // tr
    sub = 8
    n_g = len(grads)
    acc_shapes = [(1, RW_PAD)] + [(1, RWKV_DIM)] * 4 + [(LANE, RWKV_DIM), (LANE, RWKV_DIM), (2 * LANE, RWKV_DIM)]

    def body(*refs):
        seg_refs, halo_refs = refs[:6], refs[6:12]
        k_ref, xw_ref, xa_ref, xg_ref = refs[12:16]
        g_refs = refs[16:16 + n_g]
        mu_ref = refs[16 + n_g]
        prm_refs = refs[17 + n_g:24 + n_g]
        out_hbm = refs[25 + n_g]
        acc_refs = refs[26 + n_g:26 + n_g + len(acc_shapes)]
        vbuf, sems, carry = refs[26 + n_g + len(acc_shapes):]
        i = pl.program_id(0)
        blk = nb - 1 - i
        dr1, dr2, dv1, dv2, dlw, dk1, dk2, da, db, dg = [g[...] for g in g_refs]
        _, vjp = jax.vjp(_rwkv_pre, k_ref[...], xw_ref[...], xa_ref[...], xg_ref[...], *[p_[...] for p_ in prm_refs])
        dk, dxw, dxa, dxg, *dprm = vjp((dlw, dk1 + dk2, da, db, dg))
        du = jnp.concatenate([dr1 + dr2, dk, dv1 + dv2, dxw, dxa, dxg], axis=1)
        mu_v = mu_ref[...]

        @pl.when(i == 0)
        def _():
            carry[...] = jnp.zeros_like(carry)

        rows = lax.broadcasted_iota(jnp.int32, du.shape, 0)
        nxt = jnp.where(rows == tr - 1, carry[...], pltpu.roll(du, tr - 1, 0))
        d_rw = du - mu_v * du + mu_v * nxt
        d_mu = []
        for s_ref, h_ref, off, wd in zip(seg_refs, halo_refs, SEG_OFF, SEG_W):
            cur = s_ref[...]
            r0 = lax.broadcasted_iota(jnp.int32, cur.shape, 0)
            prev = jnp.where(r0 == 0, jnp.where(blk == 0, 0.0, h_ref[sub - 1:sub, :]), pltpu.roll(cur, 1, 0))
            d_mu.append(jnp.sum(du[:, off:off + wd] * (prev - cur), axis=0, keepdims=True))
        sums = [jnp.concatenate(d_mu, axis=1)] + list(dprm)

        @pl.when(i == 0)
        def _():
            for a_ref, val in zip(acc_refs, sums):
                a_ref[...] = val

        @pl.when(i > 0)
        def _():
            for a_ref, val in zip(acc_refs, sums):
                a_ref[...] += val

        carry[...] = du[0:1, :]
        slot = i % 2

        def writeback(s, b):
            return pltpu.make_async_copy(vbuf.at[s], out_hbm.at[pl.ds(b * tr, tr), pl.ds(CONV_COLS, RW_PAD)], sems.at[s])

        @pl.when(i >= 2)
        def _():
            writeback(slot, blk + 2).wait()

        vbuf[slot] = d_rw.astype(vbuf.dtype)
        writeback(slot, blk).start()

        @pl.when(i == nb - 1)
        def _():
            writeback(slot, blk).wait()
            if nb > 1:
                writeback(1 - slot, blk + 1).wait()

    rev = lambda w_, cb: pl.BlockSpec((tr, w_), functools.partial(lambda i, c: (nb - 1 - i, c), c=cb))
    halo = lambda w_, cb: pl.BlockSpec((sub, w_), functools.partial(
        lambda i, c: (jnp.maximum((nb - 1 - i) * (tr // sub) - 1, 0), c), c=cb))
    whole = lambda a: pl.BlockSpec(a.shape, functools.partial(lambda i, n: (0,) * n, n=a.ndim))
    segs = [(wd, (CONV_COLS + off) // wd) for off, wd in zip(SEG_OFF, SEG_W)]
    u_cols = [(512, 1), (LANE, XW_OFF // LANE), (LANE, XA_OFF // LANE), (2 * LANE, XG_OFF // (2 * LANE))]
    any_spec = pl.BlockSpec(memory_space=pl.ANY)
    res = pl.pallas_call(
        body, name="rwkv_pre_bwd", grid=(nb,),
        in_specs=[rev(*s) for s in segs] + [halo(*s) for s in segs] + [rev(*c) for c in u_cols]
                 + [rev(RWKV_DIM, 0)] * n_g + [whole(mu)] + [whole(p_) for p_ in small] + [any_spec],
        out_specs=[any_spec] + [pl.BlockSpec(s, functools.partial(lambda i, n: (0,) * n, n=len(s))) for s in acc_shapes],
        out_shape=[jax.ShapeDtypeStruct(dproj.shape, dproj.dtype)] + [jax.ShapeDtypeStruct(s, F32) for s in acc_shapes],
        scratch_shapes=[pltpu.VMEM((2, tr, RW_PAD), dproj.dtype), pltpu.SemaphoreType.DMA((2,)), pltpu.VMEM((1, RW_PAD), F32)],
        input_output_aliases={24 + n_g: 0},
        compiler_params=_params(("arbitrary",)),
    )(*[proj] * 12, *[u] * 4, *grads, mu, *small, dproj)
    return res


def _local_step(x, p, tgt, w, early_shards, late_shards):
    row = lambda v: v.reshape(1, -1)
    w = dict(w)

    xn1, *gathered = _rowwise("rms_mix", lambda h, g: (_rms(h, g),), [x], [w["norm_mix_g"]], [(D_MODEL, BF16)],
                              gather=early_shards)
    w.update({n: _unshard(n, g_) for n, g_ in zip(_EARLY[1:], gathered[1:])})
    w["w_in"] = _assemble_w_in(gathered[0])
    w["w_lora_up"] = _pad_rows(w["w_lora_up"], LANE)
    w["a_lora_up"] = _pad_rows(w["a_lora_up"], LANE)
    w["g_lora_up"] = _pad_rows(w["g_lora_up"], 2 * LANE)
    lg_send, lg_recv, lg_shards, lg_lands, lg_token = _gather_start("late_gather_start", late_shards, after=[xn1])
    w["shift_mu"] = w["shift_mu"] + lg_token[0:1, 0:1]
    proj = _matmul("in_proj", xn1, w["w_in"], "nn", [F32], tm=2048, tn=512, tk=D_MODEL)
    n_cb = CONV_DIM // LANE

    def conv_fwd(blk, cw):
        gb, gc, hx = blk[:, :LANE], blk[:, LANE:2 * LANE], blk[:, 2 * LANE:]
        uu = gc * hx
        return (gb * (uu * cw[2:3] + _shift_down(uu, 1) * cw[1:2] + _shift_down(uu, 2) * cw[0:1]),)

    (y_conv,) = _colwise("conv_fwd", conv_fwd, n_cb, [(proj, 3 * LANE)], [w["conv_w"]], [(CONV_DIM, BF16, LANE)])

    small = [w["w0"], w["a0"], w["k_k"], w["k_a"], w["w_lora_up"], w["a_lora_up"], w["g_lora_up"]]
    def pre_fwd(*xs):
        cur, prev_rows, mu, prm = xs[:6], xs[6:12], xs[12], xs[13:]
        segs = []
        for c_, p_, off, wd in zip(cur, prev_rows, SEG_OFF, SEG_W):
            rows = lax.broadcasted_iota(jnp.int32, c_.shape, 0)
            prev = jnp.where(rows == 0, p_, pltpu.roll(c_, 1, 0))
            segs.append(c_ + mu[:, off:off + wd] * (prev - c_))
        return (jnp.concatenate(segs, axis=1),) + tuple(_rwkv_pre(segs[1], segs[3], segs[4], segs[5], *prm))

    proj_segs = [(proj, wd, (CONV_COLS + off) // wd) for off, wd in zip(SEG_OFF, SEG_W)]
    u, lw, k_h, ra, rb, g = _rowwise(
        "rwkv_pre", pre_fwd, proj_segs, [w["shift_mu"]] + small, [(RW_PAD, F32)] + [(RWKV_DIM, F32)] * 5, halo=True)
    y_rec, rec_saved = _rec_fwd(u, lw, k_h, ra, rb)

    def late_weight(names, after):
        idx = [_LATE.index(n) for n in names]
        got = _gather_wait("late_gather_wait_" + names[0], [lg_send[i] for i in idx], [lg_recv[i] for i in idx],
                           [lg_shards[i] for i in idx], [lg_lands[i] for i in idx], after)
        return [_unshard(n, g_) for n, g_ in zip(names, got)]

    w["w_out"], w["w_up"], w["w_down"] = late_weight(["w_out", "w_up", "w_down"], [y_rec])
    post_c = [w["ln_x_g"], w["ln_x_b"], w["r_k"]]
    u_r, u_v = (u, 512, 0), (u, 512, 2)
    (y_rwkv,) = _rowwise("rwkv_post", lambda *xs: (_rwkv_post(*xs),), [y_rec, u_r, k_h, u_v, g], post_c, [(RWKV_DIM, BF16)],
                         tr=2 * ROW_BLOCK)
    ycat = jnp.concatenate([y_conv, y_rwkv], axis=1)
    def res_norm(acc, r_, g_):
        h = acc + r_
        return h, _rms(h, g_)

    h1, xn2 = _matmul("out_proj", ycat, w["w_out"], "nn", [F32, BF16], tm=1024, tn=D_MODEL, tk=D_MODEL, extras=[x],
                      consts=[w["norm_mlp_g"]], epilogue=res_norm)

    square = lambda h: h.astype(F32) * h.astype(F32)
    hid = _matmul("mlp_up", xn2, w["w_up"], "nn", [BF16], tm=2048, tn=1024, tk=D_MODEL,
                  epilogue=lambda acc: (jnp.maximum(acc, 0.0),))
    h2, xn3 = _matmul("mlp_down", hid, w["w_down"], "nn", [F32, BF16], tm=512, tn=D_MODEL, tk=D_FF, extras=[h1],
                      consts=[w["norm_ple_g"]], epilogue=res_norm, a_map=square)
    w["w_ple_gate"], w["w_ple_proj"] = late_weight(["w_ple_gate", "w_ple_proj"], [xn3])
    zg =_matmul("ple_gate", xn3, w["w_ple_gate"], "nn", [F32], tm=1024, tn=1024, tk=D_MODEL)
    pp = _matmul("ple_proj", p, w["w_ple_proj"], "nn", [F32], tm=1024, tn=1024, tk=PLE_DIM)

    def head(h2_, zg_, pp_, tg, gf):
        gate = _sigmoid(zg_)
        h3 = h2_ + gate * pp_
        out = _rms(h3, gf)
        err = out - tg
        dh3, dgf = _rms_bwd(h3, gf, err * (1.0 / D_MODEL))
        loss = jnp.sum(jnp.sum(err * err, axis=1, keepdims=True), axis=0, keepdims=True) * (0.5 / D_MODEL)
        return dh3, dh3 * pp_ * gate * (1.0 - gate), dh3 * gate, dgf, loss

    dh3, dzg, dpp, d_norm_final, loss = _rowwise(
        "head", head, [h2, zg, pp, tgt], [row(w["norm_final_g"])], [(D_MODEL, F32), (D_MODEL, BF16), (D_MODEL, BF16)],
        [(1, D_MODEL), (1, 1)], tr=2 * ROW_BLOCK)

    d_w_ple_proj = _matmul("d_ple_proj", p, dpp, "tn", [BF16], tm=PLE_DIM, tn=D_MODEL // N_DEV, tk=4096, col_blocks_out=True)
    d_w_ple_gate = _matmul("d_ple_gate", xn3, dzg, "tn", [BF16], tm=512, tn=1024, tk=4096)

    def norm_bwd(dxn, h, dres, g_):
        dh, dg = _rms_bwd(h, g_, dxn)
        dh = dh + dres
        return dh, dh, dg

    nb = dict(tm=512, tn=D_MODEL, epilogue=norm_bwd, sums=[(1, D_MODEL)])
    dh2, dh2_b, d_norm_ple = _matmul("dx_ple_gate", dzg, w["w_ple_gate"], "nt", [F32, BF16], tk=D_MODEL,
                                     extras=[h2, dh3], consts=[w["norm_ple_g"]], **nb)
    d_w_down = _matmul("d_mlp_down", hid, dh2_b, "tn", [BF16], tm=512, tn=1024, tk=4096, a_map=square)
    dpre = _matmul("dx_mlp_down", dh2_b, w["w_down"], "nt", [BF16], tm=2048, tn=1024, tk=D_MODEL, extras=[hid],
                   epilogue=lambda acc, hid_: (acc * (2.0 * hid_.astype(F32)),))
    d_w_up = _matmul("d_mlp_up", xn2, dpre, "tn", [BF16], tm=1024, tn=D_FF // N_DEV, tk=4096, col_blocks_out=True)
    dh1, dh1_b, d_norm_mlp = _matmul("dx_mlp_up", dpre, w["w_up"], "nt", [F32, BF16], tk=D_FF,
                                     extras=[h1, dh2], consts=[w["norm_mlp_g"]], **nb)
    d_w_out = _matmul("d_out_proj", ycat, dh1_b, "tn", [BF16], tm=512, tn=1024, tk=4096)
    dycat = _matmul("dx_out_proj", dh1_b, w["w_out"], "nt", [F32], tm=1024, tn=1024, tk=D_MODEL)
    late_grads = dict(w_out=d_w_out, w_up=d_w_up, w_down=d_w_down, w_ple_gate=d_w_ple_gate, w_ple_proj=d_w_ple_proj)
    late_send = [late_grads[n] if n in ("w_up", "w_ple_proj") else _reshard(n, late_grads[n]) for n in _LATE]
    *late_flight, late_token = _scatter_start("late_scatter_start", late_send, [lax.empty(a.shape, a.dtype) for a in late_send])
    conv_w_bwd = w["conv_w"] + late_token[0:1, 0:1]

    def conv_bwd(dy, blk, cw):
        gb, gc, hx = blk[:, :LANE], blk[:, LANE:2 * LANE], blk[:, 2 * LANE:]
        uu = gc * hx
        u1, u2 = _shift_down(uu, 1), _shift_down(uu, 2)
        dconv = dy * gb
        du = dconv * cw[2:3] + _shift_up(dconv, 1) * cw[1:2] + _shift_up(dconv, 2) * cw[0:1]
        s = lambda z: jnp.sum(z, axis=0, keepdims=True)
        d_blk = jnp.concatenate([dy * (uu * cw[2:3] + u1 * cw[1:2] + u2 * cw[0:1]), du * hx, du * gc], axis=1)
        return d_blk, s(dconv * u2), s(dconv * u1), s(dconv * uu)

    dproj, dcw0, dcw1, dcw2 = _colwise(
        "conv_bwd", conv_bwd, n_cb, [(dycat, LANE), (proj, 3 * LANE)], [conv_w_bwd],
        [(IN_PAD, BF16, 3 * LANE)], [(1, CONV_DIM)] * 3)

    def post_bwd(dy, y, r, k_h_, v, g_, ln_g, ln_b, r_k):
        _, vjp = jax.vjp(_rwkv_post, y, r, k_h_, v, g_, ln_g, ln_b, r_k)
        return vjp(dy)

    dy_rec, dr_p, dk_p, dv_p, dg, d_ln_g, d_ln_b, d_r_k = _rowwise(
        "rwkv_post_bwd", post_bwd, [(dycat, 512, 1), y_rec, u_r, k_h, u_v, g], post_c,
        [(RWKV_DIM, F32)] * 5, [(1, RWKV_DIM)] * 3)
    dr_r, dv_r, dlw, dk_r, da, db = _rec_bwd(u, lw, k_h, ra, rb, rec_saved, dy_rec)

    dproj, d_mu, d_w0, d_a0, d_k_k, d_k_a, d_wl, d_al, d_gl = _rwkv_pre_bwd(
        proj, u, [dr_p, dr_r, dv_p, dv_r, dlw, dk_p, dk_r, da, db, dg], w["shift_mu"], small, dproj)
    d_w_in = _matmul("d_in_proj", xn1, dproj, "tn", [BF16], tm=1024, tn=896, tk=4096)
    early_grads = dict(conv_w=jnp.concatenate([dcw0, dcw1, dcw2], axis=0),
                       w_lora_up=d_wl[:64], a_lora_up=d_al[:64], g_lora_up=d_gl[:160])
    early_send = [_split_w_in_grad(d_w_in)] + [_reshard(n, early_grads[n]) for n in _EARLY[1:]]
    *early_flight, token = _scatter_start("early_scatter_start", early_send, [lax.empty(a.shape, a.dtype) for a in early_send])
    dx, d_norm_mix = _matmul(
        "dx_in_proj", dproj, w["w_in"], "nt", [F32], tk=IN_PAD, extras=[x, dh1], consts=[w["norm_mix_g"] + token[0:1, 0:1]],
        **dict(nb, epilogue=lambda *a: norm_bwd(*a)[1:]))

    grads = dict(
        norm_mix_g=d_norm_mix, shift_mu=d_mu, w0=d_w0, a0=d_a0, k_k=d_k_k, k_a=d_k_a, r_k=d_r_k,
        ln_x_g=d_ln_g, ln_x_b=d_ln_b, norm_mlp_g=d_norm_mlp, norm_ple_g=d_norm_ple, norm_final_g=d_norm_final)
    return loss, dx, grads, late_flight, early_flight, d_w_in


def _adam_update(partials, w_ref, m_ref, v_ref, g_ref, d_ref, nm_ref, nv_ref):
    g = partials[0].astype(F32)
    for part in partials[1:]:
        g = g + part.astype(F32)
    nm =ADAM_B1 * m_ref[...] + (1.0 - ADAM_B1) * g
    nv = ADAM_B2 * v_ref[...] + (1.0 - ADAM_B2) * (g * g)
    m_hat = nm / (1.0 - ADAM_B1 ** ADAM_STEP)
    v_hat = nv / (1.0 - ADAM_B2 ** ADAM_STEP)
    g_ref[...] = g
    d_ref[...] = -ADAM_LR * (m_hat / (jnp.sqrt(v_hat) + ADAM_EPS) + ADAM_WD * w_ref[...])
    nm_ref[...] = nm
    nv_ref[...] = nv


SMALL_ROWS = 8


def _small_layout(widths):
    widths = list(widths) + [1]
    fill, place = [0] * SMALL_ROWS, [None] * len(widths)
    for j in sorted(range(len(widths)), key=lambda q: -widths[q]):
        row = fill.index(min(fill))
        place[j] = (row, fill[row])
        fill[row] += -(-widths[j] // LANE) * LANE
    return place, max(fill)


def _pack_small(vecs, loss):
    place, total = _small_layout([v_.shape[1] for v_ in vecs])
    n = len(vecs)

    def body(*refs):
        out = jnp.zeros((SMALL_ROWS, total), F32)
        row_id = lax.broadcasted_iota(jnp.int32, (SMALL_ROWS, total), 0)
        for row in range(SMALL_ROWS):
            mine = sorted((off, j) for j, (r_, off) in enumerate(place) if r_ == row)
            pieces, at = [], 0
            for off, j in mine:
                val = refs[j][...]
                pieces.append(val)
                at = off + val.shape[1]
                pad = -val.shape[1] % LANE
                if pad:
                    pieces.append(jnp.zeros((1, pad), F32))
                    at += pad
            if total > at:
                pieces.append(jnp.zeros((1, total - at), F32))
            out = jnp.where(row_id == row, jnp.broadcast_to(jnp.concatenate(pieces, axis=1), (SMALL_ROWS, total)), out)
        refs[n + 1][...] = out

    return pl.pallas_call(body, name="pack_small", out_shape=jax.ShapeDtypeStruct((SMALL_ROWS, total), F32))(*vecs, loss)


def _adamw_small(packed, ws, ms, vs):
    n = len(ws)
    place, _ = _small_layout([w_.shape[1] for w_ in ws])

    def body(p_ref, *refs):
        w_refs, m_refs, v_refs, outs = refs[:n], refs[n:2 * n], refs[2 * n:3 * n], refs[3 * n:]
        for j in range(n):
            row, off = place[j]
            cols = pl.ds(off, ws[j].shape[1])
            _adam_update([p_ref[s, row:row + 1, cols] for s in range(N_DEV)], w_refs[j], m_refs[j], v_refs[j],
                         *outs[4 * j:4 * j + 4])
        row, off = place[n]
        total = p_ref[0, row:row + 1, off:off + 1]
        for s in range(1, N_DEV):
            total = total + p_ref[s, row:row + 1, off:off + 1]
        outs[4 * n][...] = total

    res = pl.pallas_call(
        body, name="adamw_small",
        out_shape=[jax.ShapeDtypeStruct(w_.shape, F32) for w_ in ws for _ in range(4)] + [jax.ShapeDtypeStruct((1, 1), F32)],
    )(packed, *ws, *ms, *vs)
    return [res[4 * j:4 * j + 4] for j in range(n)], res[4 * n]


def _adamw(name, parts, w, m, v, own=None, me=None):
    rows, cols = w.shape[-2:]
    lead = w.ndim - 2
    tr = rows if rows * cols * 4 * 8 <= (4 << 20) else max(8, (4 << 20) // (cols * 4 * 8) // 8 * 8)
    while rows % tr:
        tr -= 8
    shape4 = [jax.ShapeDtypeStruct(w.shape, F32)] * 4
    if own is None:
        def body(p_ref, *refs):
            _adam_update([p_ref[s] for s in range(N_DEV)], *refs)

        blk = pl.BlockSpec((None,) * lead + (tr, cols), lambda i: (0,) * lead + (i, 0))
        return pl.pallas_call(
            body, name=name, grid=(rows // tr,),
            in_specs=[pl.BlockSpec((N_DEV, tr, cols), lambda i: (0, i, 0)), blk, blk, blk], out_specs=[blk] * 4,
            out_shape=shape4, compiler_params=_params(("arbitrary",)),
        )(parts, w, m, v)

    def body_own(me_ref, p_ref, own_ref, *refs):
        mine = own_ref[...]
        _adam_update([jnp.where(me_ref[0] == s, mine, p_ref[s]) for s in range(N_DEV)], *refs)

    blk = pl.BlockSpec((None,) * lead + (tr, cols), lambda i, me_ref: (0,) * lead + (i, 0))
    return pl.pallas_call(
        body_own, name=name, out_shape=shape4,
        grid_spec=pltpu.PrefetchScalarGridSpec(
            num_scalar_prefetch=1, grid=(rows // tr,),
            in_specs=[pl.BlockSpec((N_DEV, tr, cols), lambda i, me_ref: (0, i, 0)),
                      pl.BlockSpec((None, tr, cols), lambda i, me_ref: (me_ref[0], i, 0)), blk, blk, blk],
            out_specs=[blk] * 4),
        compiler_params=_params(("arbitrary",)),
    )(me, parts, own, w, m, v)


def kernel(x, p, norm_mix_g, w_in, conv_w, shift_mu, w_lora_up, w0, a_lora_up, a0, g_lora_up, k_k, k_a, r_k, ln_x_g, ln_x_b, w_out, norm_mlp_g, w_up, w_down, norm_ple_g, w_ple_gate, w_ple_proj, norm_final_g, loss_target, m_norm_mix_g, m_w_in, m_conv_w, m_shift_mu, m_w_lora_up, m_w0, m_a_lora_up, m_a0, m_g_lora_up, m_k_k, m_k_a, m_r_k, m_ln_x_g, m_ln_x_b, m_w_out, m_norm_mlp_g, m_w_up, m_w_down, m_norm_ple_g, m_w_ple_gate, m_w_ple_proj, m_norm_final_g, v_norm_mix_g, v_w_in, v_conv_w, v_shift_mu, v_w_lora_up, v_w0, v_a_lora_up, v_a0, v_g_lora_up, v_k_k, v_k_a, v_r_k, v_ln_x_g, v_ln_x_b, v_w_out, v_norm_mlp_g, v_w_up, v_w_down, v_norm_ple_g, v_w_ple_gate, v_w_ple_proj, v_norm_final_g):
    args = dict(locals())
    wts = {n: args[n] for n in _WEIGHTS}
    mom = {n: args["m_" + n] for n in _WEIGHTS}
    var = {n: args["v_" + n] for n in _WEIGHTS}
    shard2d = lambda a: a.reshape(a.shape[-2:])
    pad_mu = lambda a: _pad_in_cols(jnp.concatenate([jnp.zeros((1, CONV_COLS), F32), a], axis=1))[:, CONV_COLS:]
    unpad_mu = lambda a: _unpad_in_cols(jnp.concatenate([jnp.zeros((1, CONV_COLS), F32), a], axis=1))[:, CONV_COLS:]

    shards = {n: shard2d(wts[n]).astype(BF16 if n in _BF16_GATHER else F32) for n in _SHARDED}
    w = {n: wts[n].reshape(1, -1) for n in _REPLICATED}
    w["shift_mu"] = pad_mu(wts["shift_mu"])

    loss, dx, grads, late_flight, early_flight, d_w_in = _local_step(
        x[0], p[0, 0], loss_target[0], w, [shards[n] for n in _EARLY], [shards[n] for n in _LATE])

    me = (4 * lax.axis_index("x") + 2 * lax.axis_index("y") + lax.axis_index("c")).astype(jnp.int32).reshape(1)
    late_sent, late_parts = _scatter_wait("late_scatter_wait", *late_flight, after=[d_w_in])
    out = {n: _adamw("adamw_" + n, prt, wts[n], mom[n], var[n], own=own, me=me)
           for n, prt, own in zip(_LATE, late_parts, late_sent)}
    early_sent, early_parts = _scatter_wait("early_scatter_wait", *early_flight, after=[dx] + [out[n][1] for n in _LATE])
    for n, prt, own in zip(_EARLY, early_parts, early_sent):
        out[n] = _adamw("adamw_" + n, prt, wts[n], mom[n], var[n], own=own, me=me)

    grads["shift_mu"] = unpad_mu(grads["shift_mu"])
    flat = lambda a: a.reshape(1, -1)
    (small_parts,) = _all_gather("gather_small", [_pack_small([flat(grads[n]) for n in _REPLICATED], loss)])
    small, loss_total = _adamw_small(small_parts, *[[flat(d[n]) for n in _REPLICATED] for d in (wts, mom, var)])
    for n, res in zip(_REPLICATED, small):
        out[n] = [r.reshape(wts[n].shape) for r in res]
    return (loss_total[0, 0], dx[None], *[out[n][0] for n in _WEIGHTS], *[out[n][1] for n in _WEIGHTS],
            *[out[n][2] for n in _WEIGHTS], *[out[n][3] for n in _WEIGHTS])
```

```python
import functools

import jax
import jax.numpy as jnp
from jax import lax
from jax.experimental import pallas as pl
from jax.experimental.pallas import tpu as pltpu

F32 = jnp.float32
BF16 = jnp.bfloat16

N_DEV = 8
D_MODEL = 1024
CONV_DIM = 512
RWKV_DIM = 512
HEAD_DIM = 64
N_HEADS = 8
D_FF = 4096
PLE_DIM = 256
RMS_EPS = 1e-6
GN_EPS = 64e-5
L2_EPS = 1e-12
ADAM_LR, ADAM_B1, ADAM_B2, ADAM_EPS, ADAM_WD, ADAM_STEP = 0.001, 0.9, 0.999, 1e-08, 0.01, 10

CONV_COLS = 3 * CONV_DIM
RW_PAD = 2048
IN_PAD = CONV_COLS + RW_PAD
IN_COLS = 3360
XW_OFF, XA_OFF, XG_OFF = 1536, 1664, 1792
REC_CHUNK = 128
REC_CHUNKS_PER_STEP = 2
REC_CHUNKS_PER_STEP_FWD = 4
REC_PASSES = 1
ROW_BLOCK = 256
LANE = 128
VMEM_LIMIT = 56 * 1024 * 1024


def _dims(dn, ndim):
    if ndim == 3:
        return {"nn": (((2,), (1,)), ((0,), (0,))), "nt": (((2,), (2,)), ((0,), (0,))),
                "tn": (((1,), (1,)), ((0,), (0,)))}[dn]
    return {"nn": (((1,), (0,)), ((), ())), "nt": (((1,), (1,)), ((), ())), "tn": (((0,), (0,)), ((), ()))}[dn]


def _split2(x):
    hi = x.astype(BF16)
    return hi, (x - hi.astype(F32)).astype(BF16)


def _mm_raw(x, y, dn, passes):
    f = lambda p, q: lax.dot_general(p, q, _dims(dn, x.ndim), preferred_element_type=F32)
    if passes == 1:
        return f(x.astype(BF16), y.astype(BF16))
    xh, xl = _split2(x)
    yh, yl = _split2(y)
    if passes == 2:
        return f(xh, yh) + f(xh, yl)
    return f(xh, yh) + f(xh, yl) + f(xl, yh)


@functools.partial(jax.custom_vjp, nondiff_argnums=(2, 3))
def _mm(x, y, dn, passes):
    return _mm_raw(x, y, dn, passes)


def _mm_fwd(x, y, dn, passes):
    return _mm_raw(x, y, dn, passes), (x, y)


def _mm_bwd(dn, passes, res, d):
    x, y = res
    if dn == "nn":
        return _mm(d, y, "nt", passes), _mm(x, d, "tn", passes)
    if dn == "nt":
        return _mm(d, y, "nn", passes), _mm(d, x, "tn", passes)
    return _mm(y, d, "nt", passes), _mm(x, d, "nn", passes)


_mm.defvjp(_mm_fwd, _mm_bwd)


def _head_ones():
    i = lax.broadcasted_iota(jnp.int32, (RWKV_DIM, RWKV_DIM), 0) // HEAD_DIM
    j = lax.broadcasted_iota(jnp.int32, (RWKV_DIM, RWKV_DIM), 1) // HEAD_DIM
    return (i == j).astype(BF16)


def _hsum_raw(x):
    ones = _head_ones()
    f = lambda p: lax.dot_general(p, ones, _dims("nn", 2), preferred_element_type=F32)
    x1, x2 = _split2(x)
    return f(x1) + f(x2)


@jax.custom_vjp
def _hsum(x):
    return _hsum_raw(x)


_hsum.defvjp(lambda x: (_hsum_raw(x), None), lambda _, d: (_hsum(d),))


def _sigmoid(x):
    return 0.5 + 0.5 * jnp.tanh(0.5 * x)


def _softplus(x):
    return jnp.maximum(x, 0.0) + jnp.log(1.0 + jnp.exp(-jnp.abs(x)))


def _params(sem):
    return pltpu.CompilerParams(dimension_semantics=sem, vmem_limit_bytes=VMEM_LIMIT)


def _rowwise(name, fn, rows, consts, row_outs, acc_outs=(), tr=ROW_BLOCK, halo=False, gather=()):
    rows = [r if isinstance(r, tuple) else (r, r.shape[1], 0) for r in rows]
    t_len = rows[0][0].shape[0]
    tr = min(tr, t_len)
    n_r, n_c, n_o, n_a, n_x = len(rows), len(consts), len(row_outs), len(acc_outs), len(gather)
    n_h = n_r if halo else 0
    sub = 8
    x_specs, x_shapes, x_sems = _gather_io(gather) if n_x else ([], [], [])
    nb = t_len // tr

    def body(*refs):
        if n_x:
            n_in = n_r + n_h + n_c
            start, forward, wait = _gather_plan(refs[n_in:n_in + n_x], refs[len(refs) - 3 - n_x:len(refs) - 3], *refs[len(refs) - 3:])
            pl.when(pl.program_id(0) == 0)(start)
            refs = refs[:n_in] + refs[n_in + n_x:len(refs) - 3 - n_x]
        ins = [r[...] for r in refs[:n_r]]
        ins += [jnp.where(pl.program_id(0) == 0, 0.0, r[sub - 1:sub, :]) for r in refs[n_r:n_r + n_h]]
        ins += [r[...] for r in refs[n_r + n_h:n_r + n_h + n_c]]
        refs = refs[:n_r] + refs[n_r + n_h:]
        outs = fn(*ins)
        o_refs = refs[n_r + n_c:n_r + n_c + n_o]
        a_refs = refs[n_r + n_c + n_o:]
        for o_ref, val in zip(o_refs, outs[:n_o]):
            o_ref[...] = val.astype(o_ref.dtype)
        if n_a:
            first = pl.program_id(0) == 0

            @pl.when(first)
            def _():
                for a_ref, val in zip(a_refs, outs[n_o:]):
                    a_ref[...] = val

            @pl.when(jnp.logical_not(first))
            def _():
                for a_ref, val in zip(a_refs, outs[n_o:]):
                    a_ref[...] += val

        if n_x:
            @pl.when(pl.program_id(0) == nb - 1)
            def _():
                for j in range(n_x):
                    forward(j)
                wait()

    in_specs = [pl.BlockSpec((tr, w), functools.partial(lambda i, c: (i, c), c=cb)) for _, w, cb in rows]
    if halo:
        in_specs += [pl.BlockSpec((sub, w), functools.partial(lambda i, c: (jnp.maximum(i * (tr // sub) - 1, 0), c), c=cb))
                     for _, w, cb in rows]
    in_specs += [pl.BlockSpec(c.shape, functools.partial(lambda i, n: (0,) * n, n=c.ndim)) for c in consts]
    out_specs = [pl.BlockSpec((tr, w), lambda i: (i, 0)) for w, _ in row_outs]
    out_specs += [pl.BlockSpec(s, functools.partial(lambda i, n: (0,) * n, n=len(s))) for s in acc_outs]
    out_shape = [jax.ShapeDtypeStruct((t_len, w), dt) for w, dt in row_outs]
    out_shape += [jax.ShapeDtypeStruct(s, F32) for s in acc_outs]
    return pl.pallas_call(
        body, name=name, grid=(nb,), in_specs=in_specs + x_specs, out_specs=out_specs + x_specs,
        out_shape=out_shape + x_shapes, scratch_shapes=x_sems,
        compiler_params=pltpu.CompilerParams(dimension_semantics=("arbitrary",), vmem_limit_bytes=VMEM_LIMIT,
                                             has_side_effects=bool(n_x)),
    )(*[r[0] for r in rows], *([r[0] for r in rows] if halo else []), *consts, *gather)


def _colwise(name, fn, n_blocks, cols, prms, col_outs, prm_outs=()):
    t_len = cols[0][0].shape[0]
    n_i = len(cols) + len(prms)

    def body(*refs):
        outs = fn(*[r[...] for r in refs[:n_i]])
        for o_ref, val in zip(refs[n_i:], outs):
            o_ref[...] = val.astype(o_ref.dtype)

    spec = lambda r, w: pl.BlockSpec((r, w), lambda j: (0, j))
    in_specs = [spec(t_len, w) for _, w in cols] + [spec(a.shape[0], LANE) for a in prms]
    out_specs = [spec(t_len, bw) for _, _, bw in col_outs] + [spec(r, LANE) for r, _ in prm_outs]
    out_shape = [jax.ShapeDtypeStruct((t_len, w), dt) for w, dt, _ in col_outs]
    out_shape += [jax.ShapeDtypeStruct((r, w), F32) for r, w in prm_outs]
    return pl.pallas_call(
        body, name=name, grid=(n_blocks,), in_specs=in_specs, out_specs=out_specs, out_shape=out_shape,
        compiler_params=_params(("arbitrary",)),
    )(*[c[0] for c in cols], *prms)


def _matmul(name, a, b, dn, outs, *, tm, tn, tk, extras=(), consts=(), epilogue=None, sums=(), a_map=None,
            col_blocks_out=False):
    if dn == "nn":
        (m, k), n = a.shape, b.shape[1]
    elif dn == "nt":
        (m, k), n = a.shape, b.shape[0]
    else:
        (k, m), n = a.shape, b.shape[1]
    tm, tn, tk = min(tm, m), min(tn, n), min(tk, k)
    nk = k // tk
    grid = (m // tm, n // tn, nk)
    assert nk == 1 and (not sums or grid[1] == 1)
    a_spec = pl.BlockSpec((tk, tm), lambda i, j, q: (q, i)) if dn == "tn" else pl.BlockSpec((tm, tk), lambda i, j, q: (i, q))
    b_spec = pl.BlockSpec((tn, tk), lambda i, j, q: (j, q)) if dn == "nt" else pl.BlockSpec((tk, tn), lambda i, j, q: (q, j))
    o_spec = pl.BlockSpec((tm, tn), lambda i, j, q: (i, j))
    c_spec = pl.BlockSpec((1, tn), lambda i, j, q: (0, j))
    n_e, n_c, n_o, n_s = len(extras), len(consts), len(outs), len(sums)

    def body(*refs):
        a_ref, b_ref = refs[:2]
        e_refs = refs[2:2 + n_e + n_c]
        o_refs, s_refs = refs[2 + n_e + n_c:2 + n_e + n_c + n_o], refs[2 + n_e + n_c + n_o:]
        step = pl.program_id(0) * grid[1] + pl.program_id(1)
        a_blk = a_ref[...] if a_map is None else a_map(a_ref[...])
        acc = lax.dot_general(a_blk.astype(BF16), b_ref[...].astype(BF16), _dims(dn, 2), preferred_element_type=F32)
        vals = (acc,) if epilogue is None else epilogue(acc, *[e[...] for e in e_refs])
        for o_ref, val in zip(o_refs, vals[:n_o]):
            o_ref[...] = val.astype(o_ref.dtype)
        if n_s:
            @pl.when(step == 0)
            def _():
                for s_ref, val in zip(s_refs, vals[n_o:]):
                    s_ref[...] = val

            @pl.when(step > 0)
            def _():
                for s_ref, val in zip(s_refs, vals[n_o:]):
                    s_ref[...] += val

    res = pl.pallas_call(
        body, name=name, grid=grid,
        in_specs=[a_spec, b_spec] + [o_spec] * n_e + [c_spec] * n_c,
        out_specs=[pl.BlockSpec((None, tm, tn), lambda i, j, q: (j, i, 0)) if col_blocks_out else o_spec] * n_o
                  + [c_spec] * n_s,
        out_shape=[jax.ShapeDtypeStruct((n // tn, m, tn) if col_blocks_out else (m, n), dt) for dt in outs]
                  + [jax.ShapeDtypeStruct(s, F32) for s in sums],
        compiler_params=_params(("arbitrary",) * 3 if n_s else ("parallel", "parallel", "arbitrary")),
    )(a, b, *extras, *consts)
    return res[0] if len(res) == 1 else res


def _rms(h, g):
    return h * lax.rsqrt(jnp.mean(h * h, axis=-1, keepdims=True) + RMS_EPS) * g


def _rms_bwd(h, g, dy):
    rs = lax.rsqrt(jnp.mean(h * h, axis=-1, keepdims=True) + RMS_EPS)
    n = h * rs
    dn = dy * g
    dh = rs * (dn - n * jnp.mean(dn * n, axis=-1, keepdims=True))
    return dh, jnp.sum(dy * n, axis=0, keepdims=True)


def _rwkv_pre(k, xw, xa, xg, w0, a0, k_k, k_a, wl, al, gl):
    zw = w0 + _mm(jnp.tanh(xw), wl, "nn", 1)
    lw = -jnp.exp(-_softplus(-zw) - 0.5)
    iclr = _sigmoid(a0 + _mm(xa, al, "nn", 1))
    g = _mm(_sigmoid(xg), gl, "nn", 1)
    kk0 = k * k_k
    kk = kk0 * lax.rsqrt(jnp.maximum(_hsum(kk0 * kk0), L2_EPS * L2_EPS))
    k_h = k * (1.0 + (iclr - 1.0) * k_a)
    return lw, k_h, -kk, kk * iclr, g


def _rwkv_post(y, r, k_h, v, g, ln_g, ln_b, r_k):
    mu = _hsum(y) * (1.0 / HEAD_DIM)
    yc = y - mu
    var = _hsum(yc * yc) * (1.0 / HEAD_DIM)
    yo = yc * lax.rsqrt(var + GN_EPS) * ln_g + ln_b
    bonus = _hsum(r * k_h * r_k) * v
    return (yo + bonus) * g


def _shift_down(x, n):
    rows = lax.broadcasted_iota(jnp.int32, x.shape, 0)
    return jnp.where(rows < n, 0.0, pltpu.roll(x, n, 0))


def _shift_up(x, n):
    t_len = x.shape[0]
    rows = lax.broadcasted_iota(jnp.int32, x.shape, 0)
    return jnp.where(rows >= t_len - n, 0.0, pltpu.roll(x, t_len - n, 0))


def _gather_plan(ins, outs, send_sems, recv_sems, local_sems):
    x, y, c = lax.axis_index("x"), lax.axis_index("y"), lax.axis_index("c")
    me = 4 * x + 2 * y + c
    direct, chips = (1, 2, 4, 6), (2, 4, 6)

    def local(i):
        return pltpu.make_async_copy(ins[i], outs[i].at[me], local_sems.at[i])

    def send(i, rel):
        return pltpu.make_async_remote_copy(
            src_ref=ins[i], dst_ref=outs[i].at[me], send_sem=send_sems.at[i, rel - 1], recv_sem=recv_sems.at[i, rel - 1],
            device_id=(x ^ (rel >> 2), y ^ ((rel >> 1) & 1), c ^ (rel & 1)), device_id_type=pl.DeviceIdType.MESH)

    def passed(i, rel):
        slot = outs[i].at[me ^ rel]
        return pltpu.make_async_remote_copy(
            src_ref=slot, dst_ref=slot, send_sem=send_sems.at[i, rel], recv_sem=recv_sems.at[i, rel],
            device_id=(x, y, 1 - c), device_id_type=pl.DeviceIdType.MESH)

    def landed(i, rel):
        slot = outs[i].at[me ^ rel]
        return pltpu.make_async_remote_copy(
            src_ref=slot, dst_ref=slot, send_sem=send_sems.at[i, rel - 1], recv_sem=recv_sems.at[i, rel - 1],
            device_id=(x, y, c), device_id_type=pl.DeviceIdType.MESH)

    def start():
        for i in range(len(ins)):
            local(i).start()
            for rel in direct:
                send(i, rel).start()

    def forward(i):
        for rel in chips:
            landed(i, rel).wait_recv()
            passed(i, rel).start()

    def wait():
        for i in range(len(ins)):
            local(i).wait()
            for rel in (1, 3, 5, 7):
                landed(i, rel).wait_recv()
            for rel in direct:
                send(i, rel).wait_send()
            for rel in chips:
                passed(i, rel).wait_send()

    return start, forward, wait


def _gather_io(arrays):
    n = len(arrays)
    any_spec = pl.BlockSpec(memory_space=pl.ANY)
    out_shape = [jax.ShapeDtypeStruct((N_DEV,) + a.shape, a.dtype) for a in arrays]
    sems = [pltpu.SemaphoreType.DMA((n, N_DEV - 1)), pltpu.SemaphoreType.DMA((n, N_DEV - 1)), pltpu.SemaphoreType.DMA((n,))]
    return [any_spec] * n, out_shape, sems


def _all_gather(name, arrays):
    n = len(arrays)
    specs, out_shape, sems = _gather_io(arrays)

    def body(*refs):
        start, forward, wait = _gather_plan(refs[:n], refs[n:2 * n], *refs[2 * n:])
        start()
        for i in range(n):
            forward(i)
        wait()

    return pl.pallas_call(
        body, name=name, in_specs=specs, out_specs=specs, out_shape=out_shape, scratch_shapes=sems,
        compiler_params=pltpu.CompilerParams(has_side_effects=True),
    )(*arrays)


def _scatter_start(name, arrays, lands):
    n = len(arrays)
    hbm = pl.BlockSpec(memory_space=pltpu.HBM)

    def body(*refs):
        ins, land, send_sems, recv_sems = refs[:n], refs[n:2 * n], refs[2 * n], refs[2 * n + 1]
        token = refs[4 * n + 2]
        x, y, c = lax.axis_index("x"), lax.axis_index("y"), lax.axis_index("c")
        me = 4 * x + 2 * y + c
        for i in range(n):
            for rel in range(1, N_DEV):
                k = i * (N_DEV - 1) + rel - 1
                pltpu.make_async_remote_copy(
                    src_ref=ins[i].at[me ^ rel], dst_ref=land[i].at[me], send_sem=send_sems.at[k],
                    recv_sem=recv_sems.at[k], device_id=(x ^ (rel >> 2), y ^ ((rel >> 1) & 1), c ^ (rel & 1)),
                    device_id_type=pl.DeviceIdType.MESH).start()
        token[...] = jnp.zeros_like(token)

    sem = pltpu.SemaphoreType.DMA((n * (N_DEV - 1),))
    bufs = [pltpu.HBM(a.shape, a.dtype) for a in list(arrays) + list(lands)]
    res = pl.pallas_call(
        body, name=name, out_shape=(sem, sem, *bufs, jax.ShapeDtypeStruct((8, LANE), F32)),
        in_specs=[hbm] * (2 * n),
        out_specs=(pl.BlockSpec(memory_space=pltpu.SEMAPHORE),) * 2 + (hbm,) * (2 * n) + (pl.BlockSpec(memory_space=pltpu.VMEM),),
        input_output_aliases={i: 2 + i for i in range(2 * n)},
        compiler_params=pltpu.CompilerParams(has_side_effects=pltpu.SideEffectType.DATAFLOW_SIDE_EFFECTING),
    )(*[pltpu.with_memory_space_constraint(a, pltpu.HBM) for a in list(arrays) + list(lands)])
    return res[0], res[1], res[2:2 + n], res[2 + n:2 + 2 * n], res[2 + 2 * n]


def _scatter_wait(name, send_sems, recv_sems, arrays, lands, after):
    n, n_after = len(arrays), len(after)
    hbm = pl.BlockSpec(memory_space=pltpu.HBM)

    def body(*refs):
        ins, land, s_sems, r_sems = refs[:n], refs[n:2 * n], refs[2 * n], refs[2 * n + 1]
        x, y, c = lax.axis_index("x"), lax.axis_index("y"), lax.axis_index("c")
        me = 4 * x + 2 * y + c
        for i in range(n):
            for rel in range(1, N_DEV):
                k = i * (N_DEV - 1) + rel - 1
                cp = pltpu.make_async_remote_copy(
                    src_ref=ins[i].at[me ^ rel], dst_ref=land[i].at[me ^ rel], send_sem=s_sems.at[k],
                    recv_sem=r_sems.at[k], device_id=(x, y, c), device_id_type=pl.DeviceIdType.MESH)
                cp.wait_send()
                cp.wait_recv()

    res = pl.pallas_call(
        body, name=name, out_shape=[pltpu.HBM(a.shape, a.dtype) for a in list(arrays) + list(lands)],
        in_specs=[hbm] * (2 * n) + [pl.BlockSpec(memory_space=pltpu.SEMAPHORE)] * 2 + [pl.BlockSpec(memory_space=pl.ANY)] * n_after,
        out_specs=[hbm] * (2 * n), input_output_aliases={i: i for i in range(2 * n)},
        compiler_params=pltpu.CompilerParams(has_side_effects=pltpu.SideEffectType.DATAFLOW_SIDE_EFFECTING),
    )(*arrays, *lands, send_sems, recv_sems, *after)
    return res[:n], res[n:]


def _gather_start(name, shards, after):
    n, n_after = len(shards), len(after)
    hbm = pl.BlockSpec(memory_space=pltpu.HBM)
    lands = [lax.empty((N_DEV,) + a.shape, a.dtype) for a in shards]

    def body(*refs):
        ins, land = refs[:n], refs[n:2 * n]
        outs = refs[2 * n + n_after:]
        send_sems, recv_sems, token = outs[:n], outs[n:2 * n], outs[4 * n]
        x, y, c = lax.axis_index("x"), lax.axis_index("y"), lax.axis_index("c")
        me = 4 * x + 2 * y + c
        for i in range(n):
            for rel in range(N_DEV):
                pltpu.make_async_remote_copy(
                    src_ref=ins[i], dst_ref=land[i].at[me], send_sem=send_sems[i].at[rel],
                    recv_sem=recv_sems[i].at[rel], device_id=(x ^ (rel >> 2), y ^ ((rel >> 1) & 1), c ^ (rel & 1)),
                    device_id_type=pl.DeviceIdType.MESH).start()
        token[...] = jnp.zeros_like(token)

    sem = pltpu.SemaphoreType.DMA((N_DEV,))
    bufs = [pltpu.HBM(a.shape, a.dtype) for a in list(shards) + lands]
    res = pl.pallas_call(
        body, name=name, out_shape=(*[sem] * (2 * n), *bufs, jax.ShapeDtypeStruct((8, LANE), F32)),
        in_specs=[hbm] * (2 * n) + [pl.BlockSpec(memory_space=pl.ANY)] * n_after,
        out_specs=(pl.BlockSpec(memory_space=pltpu.SEMAPHORE),) * (2 * n) + (hbm,) * (2 * n)
                  + (pl.BlockSpec(memory_space=pltpu.VMEM),),
        input_output_aliases={i: 2 * n + i for i in range(2 * n)},
        compiler_params=pltpu.CompilerParams(has_side_effects=pltpu.SideEffectType.DATAFLOW_SIDE_EFFECTING),
    )(*[pltpu.with_memory_space_constraint(a, pltpu.HBM) for a in list(shards) + lands], *after)
    return res[:n], res[n:2 * n], res[2 * n:3 * n], res[3 * n:4 * n], res[4 * n]


def _gather_wait(name, send_sems, recv_sems, shards, lands, after):
    n, n_after = len(shards), len(after)
    hbm = pl.BlockSpec(memory_space=pltpu.HBM)

    def body(*refs):
        ins, land = refs[:n], refs[n:2 * n]
        s_sems, r_sems = refs[2 * n:3 * n], refs[3 * n:4 * n]
        x, y, c = lax.axis_index("x"), lax.axis_index("y"), lax.axis_index("c")
        me = 4 * x + 2 * y + c
        for i in range(n):
            for rel in range(N_DEV):
                cp = pltpu.make_async_remote_copy(
                    src_ref=ins[i], dst_ref=land[i].at[me ^ rel], send_sem=s_sems[i].at[rel],
                    recv_sem=r_sems[i].at[rel], device_id=(x, y, c), device_id_type=pl.DeviceIdType.MESH)
                cp.wait_send()
                cp.wait_recv()

    res = pl.pallas_call(
        body, name=name, out_shape=[pltpu.HBM(a.shape, a.dtype) for a in list(shards) + list(lands)],
        in_specs=[hbm] * (2 * n) + [pl.BlockSpec(memory_space=pltpu.SEMAPHORE)] * (2 * n)
                 + [pl.BlockSpec(memory_space=pl.ANY)] * n_after,
        out_specs=[hbm] * (2 * n), input_output_aliases={i: i for i in range(2 * n)},
        compiler_params=pltpu.CompilerParams(has_side_effects=pltpu.SideEffectType.DATAFLOW_SIDE_EFFECTING),
    )(*shards, *lands, *send_sems, *recv_sems, *after)
    return res[n:]


def _tri_powers(low):
    powers, n, p = [low.astype(BF16)], 1, low
    while 2 * n < low.shape[-1]:
        p = _mm(p, p, "nn", REC_PASSES)
        powers.append(p.astype(BF16))
        n *= 2
    return powers


@jax.custom_vjp
def _tri_solve(low, rhs, powers):
    del low
    for p in powers:
        rhs = rhs + _mm(p, rhs, "nn", REC_PASSES)
    return rhs


def _tri_solve_fwd(low, rhs, powers):
    out = _tri_solve(low, rhs, powers)
    return out, (powers, out)


def _tri_solve_bwd(res, d):
    powers, u = res
    for p in powers:
        d = d + _mm(p, d, "tn", REC_PASSES)
    return _mm(d, u, "nt", REC_PASSES), d, [jnp.zeros_like(p) for p in powers]


_tri_solve.defvjp(_tri_solve_fwd, _tri_solve_bwd)


@jax.custom_vjp
def _tri_solve_given(low, rhs, powers, value):
    del low, rhs, powers
    return value


def _tri_solve_given_fwd(low, rhs, powers, value):
    return value, (powers, value)


def _tri_solve_given_bwd(res, d):
    return _tri_solve_bwd(res, d) + (jnp.zeros_like(res[1]),)


_tri_solve_given.defvjp(_tri_solve_given_fwd, _tri_solve_given_bwd)


def _heads(x):
    return jnp.stack([x[:, h * HEAD_DIM:(h + 1) * HEAD_DIM] for h in range(N_HEADS)])


def _unheads(x):
    return jnp.concatenate([x[h] for h in range(N_HEADS)], axis=-1)


def _causal_masks(c):
    ti = lax.broadcasted_iota(jnp.int32, (c, c), 0)
    si = lax.broadcasted_iota(jnp.int32, (c, c), 1)
    strict, incl = si < ti, si <= ti
    both = jnp.concatenate([jnp.concatenate([strict, strict], axis=1), jnp.concatenate([incl, incl], axis=1)], axis=0)
    return strict, incl, both


@jax.custom_vjp
def _gram_given(x2, y2, value):
    del x2, y2
    return value.astype(F32)


def _gram_given_fwd(x2, y2, value):
    return value.astype(F32), (x2, y2, value)


def _gram_given_bwd(res, d):
    x2, y2, value = res
    d = jnp.where(_causal_masks(d.shape[-1] // 2)[2], d, 0.0)
    return _mm(d, y2, "nn", 2), _mm(d, x2, "tn", 2), jnp.zeros_like(value)


_gram_given.defvjp(_gram_given_fwd, _gram_given_bwd)


def _chunk_fwd(z0, r, lw, k, v, a, b, powers=None, gram_value=None, u_value=None):
    c = r.shape[0]
    n_h, n_k = z0.shape[0], z0.shape[1]
    mm = functools.partial(_mm, passes=REC_PASSES)
    gram = functools.partial(_mm, passes=2)
    _, incl, mask = _causal_masks(c)
    cum = _mm(incl.astype(F32), lw, "nn", 3)
    cum_end = cum[c - 1:c, :]
    e_neg, e_end = jnp.exp(-cum), jnp.exp(cum_end - cum)
    x2 = jnp.concatenate([_heads(a * jnp.exp(cum - lw)), _heads(r * jnp.exp(cum))], axis=1)
    y2 = jnp.concatenate([_heads(b * e_neg), _heads(k * e_neg)], axis=1)
    vh = _heads(v)
    g2 = jnp.where(mask, gram(x2, y2, "nt"), 0.0) if gram_value is None else _gram_given(x2, y2, gram_value)
    t2 = mm(x2, z0, "nn") + mm(g2[:, :, c:], vh, "nn")
    low = g2[:, :c, :c]
    powers = _tri_powers(low) if powers is None else powers
    u = _tri_solve(low, t2[:, :c], powers) if u_value is None else _tri_solve_given(low, t2[:, :c], powers, u_value)
    y = t2[:, c:] + mm(g2[:, c:, :c], u, "nn")
    ki = lax.broadcasted_iota(jnp.int32, (n_k, n_k), 0)
    kj = lax.broadcasted_iota(jnp.int32, (n_k, n_k), 1)
    dmat = jnp.where(ki == kj, jnp.broadcast_to(_heads(jnp.exp(cum_end)), (n_h, n_k, n_k)), 0.0)
    z_end = mm(dmat, z0, "nn") + mm(jnp.concatenate([_heads(b * e_end), _heads(k * e_end)], axis=1),
                                    jnp.concatenate([u, vh], axis=1), "tn")
    return _unheads(y), z_end, powers, g2, u


def _rec_fwd(u, lw, k, a, b):
    t_len = lw.shape[0]
    c = min(REC_CHUNK, t_len)
    nc = t_len // c
    per = REC_CHUNKS_PER_STEP_FWD if nc % REC_CHUNKS_PER_STEP_FWD == 0 else 1
    steps = nc // per
    n_pow = max(1, (c - 1).bit_length())

    def body(r_ref, v_ref, lw_ref, k_ref, a_ref, b_ref, y_ref, zs_ref, pw_ref, gs_ref, us_ref, z_scr):
        @pl.when(pl.program_id(0) == 0)
        def _():
            z_scr[...] = jnp.zeros_like(z_scr)

        for s in range(per):
            rows = pl.ds(s * c, c)
            z0 = z_scr[...]
            zs_ref[s] = z0
            y, z_end, powers, g2, u_rows = _chunk_fwd(z0, r_ref[rows, :], lw_ref[rows, :], k_ref[rows, :], v_ref[rows, :],
                                                      a_ref[rows, :], b_ref[rows, :])
            y_ref[rows, :] = y
            z_scr[...] = z_end
            pw_ref[s] = jnp.concatenate(powers, axis=0)
            gs_ref[s] = g2.astype(BF16)
            us_ref[s] = u_rows

    blk = lambda cb: pl.BlockSpec((per * c, RWKV_DIM), functools.partial(lambda i, q: (i, q), q=cb))
    res = pl.pallas_call(
        body, name="rwkv_rec_fwd", grid=(steps,),
        in_specs=[blk(0), blk(2)] + [blk(0)] * 4,
        out_specs=[blk(0), pl.BlockSpec((per, N_HEADS, HEAD_DIM, HEAD_DIM), lambda i: (i, 0, 0, 0)),
                   pl.BlockSpec((per, n_pow * N_HEADS, c, c), lambda i: (i, 0, 0, 0)),
                   pl.BlockSpec((per, N_HEADS, 2 * c, 2 * c), lambda i: (i, 0, 0, 0)),
                   pl.BlockSpec((per, N_HEADS, c, HEAD_DIM), lambda i: (i, 0, 0, 0))],
        out_shape=[jax.ShapeDtypeStruct((t_len, RWKV_DIM), F32),
                   jax.ShapeDtypeStruct((nc, N_HEADS, HEAD_DIM, HEAD_DIM), F32),
                   jax.ShapeDtypeStruct((nc, n_pow * N_HEADS, c, c), BF16),
                   jax.ShapeDtypeStruct((nc, N_HEADS, 2 * c, 2 * c), BF16),
                   jax.ShapeDtypeStruct((nc, N_HEADS, c, HEAD_DIM), F32)],
        scratch_shapes=[pltpu.VMEM((N_HEADS, HEAD_DIM, HEAD_DIM), F32)], compiler_params=_params(("arbitrary",)),
    )(u, u, lw, k, a, b)
    return res[0], res[1:]


def _rec_bwd(u, lw, k, a, b, saved, dy):
    t_len = lw.shape[0]
    c = min(REC_CHUNK, t_len)
    nc = t_len // c
    per = REC_CHUNKS_PER_STEP if nc % REC_CHUNKS_PER_STEP == 0 else 1
    steps = nc // per

    zs, pw, gs, us = saved

    def body(r_ref, v_ref, lw_ref, k_ref, a_ref, b_ref, zs_ref, pw_ref, gs_ref, us_ref, dy_ref, *rest):
        g_refs, dz_scr = rest[:6], rest[6]

        @pl.when(pl.program_id(0) == 0)
        def _():
            dz_scr[...] = jnp.zeros_like(dz_scr)

        for s in reversed(range(per)):
            rows = pl.ds(s * c, c)
            powers = [pw_ref[s, j * N_HEADS:(j + 1) * N_HEADS] for j in range(pw.shape[1] // N_HEADS)]
            chunk = functools.partial(lambda gram, u_val, pws, *xs: _chunk_fwd(*xs, powers=pws, gram_value=gram, u_value=u_val)[:2],
                                      gs_ref[s], us_ref[s], powers)
            _, vjp = jax.vjp(chunk, zs_ref[s], r_ref[rows, :], lw_ref[rows, :], k_ref[rows, :], v_ref[rows, :],
                             a_ref[rows, :], b_ref[rows, :])
            dz0, dr, dlw, dk, dv, da, db = vjp((dy_ref[rows, :], dz_scr[...]))
            for ref, val in zip(g_refs, (dr, dv, dlw, dk, da, db)):
                ref[rows, :] = val
            dz_scr[...] = dz0

    blk = lambda cb: pl.BlockSpec((per * c, RWKV_DIM), functools.partial(lambda i, q: (steps - 1 - i, q), q=cb))
    saved_blk = lambda arr: pl.BlockSpec((per,) + arr.shape[1:], lambda i: (steps - 1 - i, 0, 0, 0))
    return pl.pallas_call(
        body, name="rwkv_rec_bwd", grid=(steps,),
        in_specs=[blk(0), blk(2)] + [blk(0)] * 4 + [saved_blk(zs), saved_blk(pw), saved_blk(gs), saved_blk(us), blk(0)],
        out_specs=[blk(0)] * 6, out_shape=[jax.ShapeDtypeStruct((t_len, RWKV_DIM), F32)] * 6,
        scratch_shapes=[pltpu.VMEM((N_HEADS, HEAD_DIM, HEAD_DIM), F32)], compiler_params=_params(("arbitrary",)),
    )(u, u, lw, k, a, b, zs, pw, gs, us, dy)


_EARLY = ["w_in", "conv_w", "w_lora_up", "a_lora_up", "g_lora_up"]
_LATE = ["w_out", "w_up", "w_down", "w_ple_gate", "w_ple_proj"]
_SHARDED = _EARLY + _LATE
_COL_SHARDED = {"w_in", "conv_w", "w_lora_up", "a_lora_up", "g_lora_up", "w_up", "w_ple_proj"}
_BF16_GATHER = {"w_in", "w_out", "w_up", "w_down", "w_ple_gate", "w_ple_proj"}
_REPLICATED = ["norm_mix_g", "shift_mu", "w0", "a0", "k_k", "k_a", "r_k", "ln_x_g", "ln_x_b", "norm_mlp_g", "norm_ple_g",
               "norm_final_g"]
_WEIGHTS = ["norm_mix_g", "w_in", "conv_w", "shift_mu", "w_lora_up", "w0", "a_lora_up", "a0", "g_lora_up", "k_k", "k_a", "r_k",
            "ln_x_g", "ln_x_b", "w_out", "norm_mlp_g", "w_up", "w_down", "norm_ple_g", "w_ple_gate", "w_ple_proj", "norm_final_g"]


def _unshard(name, g):
    if name in _COL_SHARDED:
        return jnp.moveaxis(g, 0, 1).reshape(g.shape[1], N_DEV * g.shape[2])
    return g.reshape(N_DEV * g.shape[1], g.shape[2])


def _reshard(name, full):
    if name in _COL_SHARDED:
        return jnp.moveaxis(full.reshape(full.shape[0], N_DEV, full.shape[1] // N_DEV), 1, 0)
    return full.reshape(N_DEV, full.shape[0] // N_DEV, full.shape[1])


def _pad_in_cols(a):
    z = lambda n: jnp.zeros(a.shape[:-1] + (n,), a.dtype)
    conv = [a[..., part * CONV_DIM + j * LANE:part * CONV_DIM + (j + 1) * LANE] for j in range(CONV_DIM // LANE) for part in range(3)]
    return jnp.concatenate(conv + [a[..., CONV_COLS:3136], z(64), a[..., 3136:3200], z(64), a[..., 3200:3360], z(96)], axis=-1)


def _unpad_in_cols(a):
    conv = [a[..., (3 * j + part) * LANE:(3 * j + part + 1) * LANE] for part in range(3) for j in range(CONV_DIM // LANE)]
    return jnp.concatenate(conv + [a[..., CONV_COLS:3136], a[..., 3200:3264], a[..., 3328:3488]], axis=-1)


def _assemble_w_in(g):
    n_dev, rows, cols = g.shape

    def body(g_ref, o_ref):
        o_ref[...] = _pad_in_cols(jnp.concatenate([g_ref[d] for d in range(n_dev)], axis=1))

    return pl.pallas_call(
        body, name="w_in_assemble", grid=(rows // ROW_BLOCK,),
        in_specs=[pl.BlockSpec((n_dev, ROW_BLOCK, cols), lambda i: (0, i, 0))],
        out_specs=pl.BlockSpec((ROW_BLOCK, IN_PAD), lambda i: (i, 0)),
        out_shape=jax.ShapeDtypeStruct((rows, IN_PAD), g.dtype), compiler_params=_params(("arbitrary",)),
    )(g)


def _split_w_in_grad(dw):
    rows = dw.shape[0]
    cols = IN_COLS // N_DEV

    def body(d_ref, o_ref):
        full = _unpad_in_cols(d_ref[...])
        for d in range(N_DEV):
            o_ref[d] = full[:, cols * d:cols * (d + 1)]

    return pl.pallas_call(
        body, name="w_in_grad_split", grid=(rows // ROW_BLOCK,),
        in_specs=[pl.BlockSpec((ROW_BLOCK, IN_PAD), lambda i: (i, 0))],
        out_specs=pl.BlockSpec((N_DEV, ROW_BLOCK, cols), lambda i: (0, i, 0)),
        out_shape=jax.ShapeDtypeStruct((N_DEV, rows, cols), dw.dtype), compiler_params=_params(("arbitrary",)),
    )(dw)


def _pad_rows(a, rows):
    return jnp.concatenate([a, jnp.zeros((rows - a.shape[0],) + a.shape[1:], a.dtype)], axis=0)


SEG_W = [RWKV_DIM, RWKV_DIM, RWKV_DIM, LANE, LANE, 2 * LANE]
SEG_OFF = [0, 512, 1024, XW_OFF, XA_OFF, XG_OFF]


def _rwkv_pre_bwd(proj, u, grads, mu, small, dproj):
    t_len = u.shape[0]
    tr = min(ROW_BLOCK, t_len)
    nb = t_len // tr
    sub = 8
    n_g = len(grads)
    acc_shapes = [(1, RW_PAD)] + [(1, RWKV_DIM)] * 4 + [(LANE, RWKV_DIM), (LANE, RWKV_DIM), (2 * LANE, RWKV_DIM)]

    def body(*refs):
        seg_refs, halo_refs = refs[:6], refs[6:12]
        k_ref, xw_ref, xa_ref, xg_ref = refs[12:16]
        g_refs = refs[16:16 + n_g]
        mu_ref = refs[16 + n_g]
        prm_refs = refs[17 + n_g:24 + n_g]
        out_hbm = refs[25 + n_g]
        acc_refs = refs[26 + n_g:26 + n_g + len(acc_shapes)]
        vbuf, sems, carry = refs[26 + n_g + len(acc_shapes):]
        i = pl.program_id(0)
        blk = nb - 1 - i
        dr1, dr2, dv1, dv2, dlw, dk1, dk2, da, db, dg = [g[...] for g in g_refs]
        _, vjp = jax.vjp(_rwkv_pre, k_ref[...], xw_ref[...], xa_ref[...], xg_ref[...], *[p_[...] for p_ in prm_refs])
        dk, dxw, dxa, dxg, *dprm = vjp((dlw, dk1 + dk2, da, db, dg))
        du = jnp.concatenate([dr1 + dr2, dk, dv1 + dv2, dxw, dxa, dxg], axis=1)
        mu_v = mu_ref[...]

        @pl.when(i == 0)
        def _():
            carry[...] = jnp.zeros_like(carry)

        rows = lax.broadcasted_iota(jnp.int32, du.shape, 0)
        nxt = jnp.where(rows == tr - 1, carry[...], pltpu.roll(du, tr - 1, 0))
        d_rw = du - mu_v * du + mu_v * nxt
        d_mu = []
        for s_ref, h_ref, off, wd in zip(seg_refs, halo_refs, SEG_OFF, SEG_W):
            cur = s_ref[...]
            r0 = lax.broadcasted_iota(jnp.int32, cur.shape, 0)
            prev = jnp.where(r0 == 0, jnp.where(blk == 0, 0.0, h_ref[sub - 1:sub, :]), pltpu.roll(cur, 1, 0))
            d_mu.append(jnp.sum(du[:, off:off + wd] * (prev - cur), axis=0, keepdims=True))
        sums = [jnp.concatenate(d_mu, axis=1)] + list(dprm)

        @pl.when(i == 0)
        def _():
            for a_ref, val in zip(acc_refs, sums):
                a_ref[...] = val

        @pl.when(i > 0)
        def _():
            for a_ref, val in zip(acc_refs, sums):
                a_ref[...] += val

        carry[...] = du[0:1, :]
        slot = i % 2

        def writeback(s, b):
            return pltpu.make_async_copy(vbuf.at[s], out_hbm.at[pl.ds(b * tr, tr), pl.ds(CONV_COLS, RW_PAD)], sems.at[s])

        @pl.when(i >= 2)
        def _():
            writeback(slot, blk + 2).wait()

        vbuf[slot] = d_rw.astype(vbuf.dtype)
        writeback(slot, blk).start()

        @pl.when(i == nb - 1)
        def _():
            writeback(slot, blk).wait()
            if nb > 1:
                writeback(1 - slot, blk + 1).wait()

    rev = lambda w_, cb: pl.BlockSpec((tr, w_), functools.partial(lambda i, c: (nb - 1 - i, c), c=cb))
    halo = lambda w_, cb: pl.BlockSpec((sub, w_), functools.partial(
        lambda i, c: (jnp.maximum((nb - 1 - i) * (tr // sub) - 1, 0), c), c=cb))
    whole = lambda a: pl.BlockSpec(a.shape, functools.partial(lambda i, n: (0,) * n, n=a.ndim))
    segs = [(wd, (CONV_COLS + off) // wd) for off, wd in zip(SEG_OFF, SEG_W)]
    u_cols = [(512, 1), (LANE, XW_OFF // LANE), (LANE, XA_OFF // LANE), (2 * LANE, XG_OFF // (2 * LANE))]
    any_spec = pl.BlockSpec(memory_space=pl.ANY)
    res = pl.pallas_call(
        body, name="rwkv_pre_bwd", grid=(nb,),
        in_specs=[rev(*s) for s in segs] + [halo(*s) for s in segs] + [rev(*c) for c in u_cols]
                 + [rev(RWKV_DIM, 0)] * n_g + [whole(mu)] + [whole(p_) for p_ in small] + [any_spec],
        out_specs=[any_spec] + [pl.BlockSpec(s, functools.partial(lambda i, n: (0,) * n, n=len(s))) for s in acc_shapes],
        out_shape=[jax.ShapeDtypeStruct(dproj.shape, dproj.dtype)] + [jax.ShapeDtypeStruct(s, F32) for s in acc_shapes],
        scratch_shapes=[pltpu.VMEM((2, tr, RW_PAD), dproj.dtype), pltpu.SemaphoreType.DMA((2,)), pltpu.VMEM((1, RW_PAD), F32)],
        input_output_aliases={24 + n_g: 0},
        compiler_params=_params(("arbitrary",)),
    )(*[proj] * 12, *[u] * 4, *grads, mu, *small, dproj)
    return res


def _local_step(x, p, tgt, w, early_shards, late_shards):
    row = lambda v: v.reshape(1, -1)
    w = dict(w)

    xn1, *gathered = _rowwise("rms_mix", lambda h, g: (_rms(h, g),), [x], [w["norm_mix_g"]], [(D_MODEL, BF16)],
                              gather=early_shards)
    w.update({n: _unshard(n, g_) for n, g_ in zip(_EARLY[1:], gathered[1:])})
    w["w_in"] = _assemble_w_in(gathered[0])
    w["w_lora_up"] = _pad_rows(w["w_lora_up"], LANE)
    w["a_lora_up"] = _pad_rows(w["a_lora_up"], LANE)
    w["g_lora_up"] = _pad_rows(w["g_lora_up"], 2 * LANE)
    lg_send, lg_recv, lg_shards, lg_lands, lg_token = _gather_start("late_gather_start", late_shards, after=[xn1])
    w["shift_mu"] = w["shift_mu"] + lg_token[0:1, 0:1]
    proj = _matmul("in_proj", xn1, w["w_in"], "nn", [F32], tm=2048, tn=512, tk=D_MODEL)
    n_cb = CONV_DIM // LANE

    def conv_fwd(blk, cw):
        gb, gc, hx = blk[:, :LANE], blk[:, LANE:2 * LANE], blk[:, 2 * LANE:]
        uu = gc * hx
        return (gb * (uu * cw[2:3] + _shift_down(uu, 1) * cw[1:2] + _shift_down(uu, 2) * cw[0:1]),)

    (y_conv,) = _colwise("conv_fwd", conv_fwd, n_cb, [(proj, 3 * LANE)], [w["conv_w"]], [(CONV_DIM, BF16, LANE)])

    small = [w["w0"], w["a0"], w["k_k"], w["k_a"], w["w_lora_up"], w["a_lora_up"], w["g_lora_up"]]
    def pre_fwd(*xs):
        cur, prev_rows, mu, prm = xs[:6], xs[6:12], xs[12], xs[13:]
        segs = []
        for c_, p_, off, wd in zip(cur, prev_rows, SEG_OFF, SEG_W):
            rows = lax.broadcasted_iota(jnp.int32, c_.shape, 0)
            prev = jnp.where(rows == 0, p_, pltpu.roll(c_, 1, 0))
            segs.append(c_ + mu[:, off:off + wd] * (prev - c_))
        return (jnp.concatenate(segs, axis=1),) + tuple(_rwkv_pre(segs[1], segs[3], segs[4], segs[5], *prm))

    proj_segs = [(proj, wd, (CONV_COLS + off) // wd) for off, wd in zip(SEG_OFF, SEG_W)]
    u, lw, k_h, ra, rb, g = _rowwise(
        "rwkv_pre", pre_fwd, proj_segs, [w["shift_mu"]] + small, [(RW_PAD, F32)] + [(RWKV_DIM, F32)] * 5, halo=True)
    y_rec, rec_saved = _rec_fwd(u, lw, k_h, ra, rb)

    def late_weight(names, after):
        idx = [_LATE.index(n) for n in names]
        got = _gather_wait("late_gather_wait_" + names[0], [lg_send[i] for i in idx], [lg_recv[i] for i in idx],
                           [lg_shards[i] for i in idx], [lg_lands[i] for i in idx], after)
        return [_unshard(n, g_) for n, g_ in zip(names, got)]

    w["w_out"], w["w_up"], w["w_down"] = late_weight(["w_out", "w_up", "w_down"], [y_rec])
    post_c = [w["ln_x_g"], w["ln_x_b"], w["r_k"]]
    u_r, u_v = (u, 512, 0), (u, 512, 2)
    (y_rwkv,) = _rowwise("rwkv_post", lambda *xs: (_rwkv_post(*xs),), [y_rec, u_r, k_h, u_v, g], post_c, [(RWKV_DIM, BF16)],
                         tr=2 * ROW_BLOCK)
    ycat = jnp.concatenate([y_conv, y_rwkv], axis=1)
    def res_norm(acc, r_, g_):
        h = acc + r_
        return h, _rms(h, g_)

    h1, xn2 = _matmul("out_proj", ycat, w["w_out"], "nn", [F32, BF16], tm=1024, tn=D_MODEL, tk=D_MODEL, extras=[x],
                      consts=[w["norm_mlp_g"]], epilogue=res_norm)

    square = lambda h: h.astype(F32) * h.astype(F32)
    hid = _matmul("mlp_up", xn2, w["w_up"], "nn", [BF16], tm=2048, tn=1024, tk=D_MODEL,
                  epilogue=lambda acc: (jnp.maximum(acc, 0.0),))
    h2, xn3 = _matmul("mlp_down", hid, w["w_down"], "nn", [F32, BF16], tm=512, tn=D_MODEL, tk=D_FF, extras=[h1],
                      consts=[w["norm_ple_g"]], epilogue=res_norm, a_map=square)
    w["w_ple_gate"], w["w_ple_proj"] = late_weight(["w_ple_gate", "w_ple_proj"], [xn3])
    zg =_matmul("ple_gate", xn3, w["w_ple_gate"], "nn", [F32], tm=1024, tn=1024, tk=D_MODEL)
    pp = _matmul("ple_proj", p, w["w_ple_proj"], "nn", [F32], tm=1024, tn=1024, tk=PLE_DIM)

    def head(h2_, zg_, pp_, tg, gf):
        gate = _sigmoid(zg_)
        h3 = h2_ + gate * pp_
        out = _rms(h3, gf)
        err = out - tg
        dh3, dgf = _rms_bwd(h3, gf, err * (1.0 / D_MODEL))
        loss = jnp.sum(jnp.sum(err * err, axis=1, keepdims=True), axis=0, keepdims=True) * (0.5 / D_MODEL)
        return dh3, dh3 * pp_ * gate * (1.0 - gate), dh3 * gate, dgf, loss

    dh3, dzg, dpp, d_norm_final, loss = _rowwise(
        "head", head, [h2, zg, pp, tgt], [row(w["norm_final_g"])], [(D_MODEL, F32), (D_MODEL, BF16), (D_MODEL, BF16)],
        [(1, D_MODEL), (1, 1)], tr=2 * ROW_BLOCK)

    d_w_ple_proj = _matmul("d_ple_proj", p, dpp, "tn", [BF16], tm=PLE_DIM, tn=D_MODEL // N_DEV, tk=4096, col_blocks_out=True)
    d_w_ple_gate = _matmul("d_ple_gate", xn3, dzg, "tn", [BF16], tm=512, tn=1024, tk=4096)

    def norm_bwd(dxn, h, dres, g_):
        dh, dg = _rms_bwd(h, g_, dxn)
        dh = dh + dres
        return dh, dh, dg

    nb = dict(tm=512, tn=D_MODEL, epilogue=norm_bwd, sums=[(1, D_MODEL)])
    dh2, dh2_b, d_norm_ple = _matmul("dx_ple_gate", dzg, w["w_ple_gate"], "nt", [F32, BF16], tk=D_MODEL,
                                     extras=[h2, dh3], consts=[w["norm_ple_g"]], **nb)
    d_w_down = _matmul("d_mlp_down", hid, dh2_b, "tn", [BF16], tm=512, tn=1024, tk=4096, a_map=square)
    dpre = _matmul("dx_mlp_down", dh2_b, w["w_down"], "nt", [BF16], tm=2048, tn=1024, tk=D_MODEL, extras=[hid],
                   epilogue=lambda acc, hid_: (acc * (2.0 * hid_.astype(F32)),))
    d_w_up = _matmul("d_mlp_up", xn2, dpre, "tn", [BF16], tm=1024, tn=D_FF // N_DEV, tk=4096, col_blocks_out=True)
    dh1, dh1_b, d_norm_mlp = _matmul("dx_mlp_up", dpre, w["w_up"], "nt", [F32, BF16], tk=D_FF,
                                     extras=[h1, dh2], consts=[w["norm_mlp_g"]], **nb)
    d_w_out = _matmul("d_out_proj", ycat, dh1_b, "tn", [BF16], tm=512, tn=1024, tk=4096)
    dycat = _matmul("dx_out_proj", dh1_b, w["w_out"], "nt", [F32], tm=1024, tn=1024, tk=D_MODEL)
    late_grads = dict(w_out=d_w_out, w_up=d_w_up, w_down=d_w_down, w_ple_gate=d_w_ple_gate, w_ple_proj=d_w_ple_proj)
    late_send = [late_grads[n] if n in ("w_up", "w_ple_proj") else _reshard(n, late_grads[n]) for n in _LATE]
    *late_flight, late_token = _scatter_start("late_scatter_start", late_send, [lax.empty(a.shape, a.dtype) for a in late_send])
    conv_w_bwd = w["conv_w"] + late_token[0:1, 0:1]

    def conv_bwd(dy, blk, cw):
        gb, gc, hx = blk[:, :LANE], blk[:, LANE:2 * LANE], blk[:, 2 * LANE:]
        uu = gc * hx
        u1, u2 = _shift_down(uu, 1), _shift_down(uu, 2)
        dconv = dy * gb
        du = dconv * cw[2:3] + _shift_up(dconv, 1) * cw[1:2] + _shift_up(dconv, 2) * cw[0:1]
        s = lambda z: jnp.sum(z, axis=0, keepdims=True)
        d_blk = jnp.concatenate([dy * (uu * cw[2:3] + u1 * cw[1:2] + u2 * cw[0:1]), du * hx, du * gc], axis=1)
        return d_blk, s(dconv * u2), s(dconv * u1), s(dconv * uu)

    dproj, dcw0, dcw1, dcw2 = _colwise(
        "conv_bwd", conv_bwd, n_cb, [(dycat, LANE), (proj, 3 * LANE)], [conv_w_bwd],
        [(IN_PAD, BF16, 3 * LANE)], [(1, CONV_DIM)] * 3)

    def post_bwd(dy, y, r, k_h_, v, g_, ln_g, ln_b, r_k):
        _, vjp = jax.vjp(_rwkv_post, y, r, k_h_, v, g_, ln_g, ln_b, r_k)
        return vjp(dy)

    dy_rec, dr_p, dk_p, dv_p, dg, d_ln_g, d_ln_b, d_r_k = _rowwise(
        "rwkv_post_bwd", post_bwd, [(dycat, 512, 1), y_rec, u_r, k_h, u_v, g], post_c,
        [(RWKV_DIM, F32)] * 5, [(1, RWKV_DIM)] * 3)
    dr_r, dv_r, dlw, dk_r, da, db = _rec_bwd(u, lw, k_h, ra, rb, rec_saved, dy_rec)

    dproj, d_mu, d_w0, d_a0, d_k_k, d_k_a, d_wl, d_al, d_gl = _rwkv_pre_bwd(
        proj, u, [dr_p, dr_r, dv_p, dv_r, dlw, dk_p, dk_r, da, db, dg], w["shift_mu"], small, dproj)
    d_w_in = _matmul("d_in_proj", xn1, dproj, "tn", [BF16], tm=1024, tn=896, tk=4096)
    early_grads = dict(conv_w=jnp.concatenate([dcw0, dcw1, dcw2], axis=0),
                       w_lora_up=d_wl[:64], a_lora_up=d_al[:64], g_lora_up=d_gl[:160])
    early_send = [_split_w_in_grad(d_w_in)] + [_reshard(n, early_grads[n]) for n in _EARLY[1:]]
    *early_flight, token = _scatter_start("early_scatter_start", early_send, [lax.empty(a.shape, a.dtype) for a in early_send])
    dx, d_norm_mix = _matmul(
        "dx_in_proj", dproj, w["w_in"], "nt", [F32], tk=IN_PAD, extras=[x, dh1], consts=[w["norm_mix_g"] + token[0:1, 0:1]],
        **dict(nb, epilogue=lambda *a: norm_bwd(*a)[1:]))

    grads = dict(
        norm_mix_g=d_norm_mix, shift_mu=d_mu, w0=d_w0, a0=d_a0, k_k=d_k_k, k_a=d_k_a, r_k=d_r_k,
        ln_x_g=d_ln_g, ln_x_b=d_ln_b, norm_mlp_g=d_norm_mlp, norm_ple_g=d_norm_ple, norm_final_g=d_norm_final)
    return loss, dx, grads, late_flight, early_flight, d_w_in


def _adam_update(partials, w_ref, m_ref, v_ref, g_ref, d_ref, nm_ref, nv_ref):
    g = partials[0].astype(F32)
    for part in partials[1:]:
        g = g + part.astype(F32)
    nm =ADAM_B1 * m_ref[...] + (1.0 - ADAM_B1) * g
    nv = ADAM_B2 * v_ref[...] + (1.0 - ADAM_B2) * (g * g)
    m_hat = nm / (1.0 - ADAM_B1 ** ADAM_STEP)
    v_hat = nv / (1.0 - ADAM_B2 ** ADAM_STEP)
    g_ref[...] = g
    d_ref[...] = -ADAM_LR * (m_hat / (jnp.sqrt(v_hat) + ADAM_EPS) + ADAM_WD * w_ref[...])
    nm_ref[...] = nm
    nv_ref[...] = nv


SMALL_ROWS = 8


def _small_layout(widths):
    widths = list(widths) + [1]
    fill, place = [0] * SMALL_ROWS, [None] * len(widths)
    for j in sorted(range(len(widths)), key=lambda q: -widths[q]):
        row = fill.index(min(fill))
        place[j] = (row, fill[row])
        fill[row] += -(-widths[j] // LANE) * LANE
    return place, max(fill)


def _pack_small(vecs, loss):
    place, total = _small_layout([v_.shape[1] for v_ in vecs])
    n = len(vecs)

    def body(*refs):
        out = jnp.zeros((SMALL_ROWS, total), F32)
        row_id = lax.broadcasted_iota(jnp.int32, (SMALL_ROWS, total), 0)
        for row in range(SMALL_ROWS):
            mine = sorted((off, j) for j, (r_, off) in enumerate(place) if r_ == row)
            pieces, at = [], 0
            for off, j in mine:
                val = refs[j][...]
                pieces.append(val)
                at = off + val.shape[1]
                pad = -val.shape[1] % LANE
                if pad:
                    pieces.append(jnp.zeros((1, pad), F32))
                    at += pad
            if total > at:
                pieces.append(jnp.zeros((1, total - at), F32))
            out = jnp.where(row_id == row, jnp.broadcast_to(jnp.concatenate(pieces, axis=1), (SMALL_ROWS, total)), out)
        refs[n + 1][...] = out

    return pl.pallas_call(body, name="pack_small", out_shape=jax.ShapeDtypeStruct((SMALL_ROWS, total), F32))(*vecs, loss)


def _adamw_small(packed, ws, ms, vs):
    n = len(ws)
    place, _ = _small_layout([w_.shape[1] for w_ in ws])

    def body(p_ref, *refs):
        w_refs, m_refs, v_refs, outs = refs[:n], refs[n:2 * n], refs[2 * n:3 * n], refs[3 * n:]
        for j in range(n):
            row, off = place[j]
            cols = pl.ds(off, ws[j].shape[1])
            _adam_update([p_ref[s, row:row + 1, cols] for s in range(N_DEV)], w_refs[j], m_refs[j], v_refs[j],
                         *outs[4 * j:4 * j + 4])
        row, off = place[n]
        total = p_ref[0, row:row + 1, off:off + 1]
        for s in range(1, N_DEV):
            total = total + p_ref[s, row:row + 1, off:off + 1]
        outs[4 * n][...] = total

    res = pl.pallas_call(
        body, name="adamw_small",
        out_shape=[jax.ShapeDtypeStruct(w_.shape, F32) for w_ in ws for _ in range(4)] + [jax.ShapeDtypeStruct((1, 1), F32)],
    )(packed, *ws, *ms, *vs)
    return [res[4 * j:4 * j + 4] for j in range(n)], res[4 * n]


def _adamw(name, parts, w, m, v, own=None, me=None):
    rows, cols = w.shape[-2:]
    lead = w.ndim - 2
    tr = rows if rows * cols * 4 * 8 <= (4 << 20) else max(8, (4 << 20) // (cols * 4 * 8) // 8 * 8)
    while rows % tr:
        tr -= 8
    shape4 = [jax.ShapeDtypeStruct(w.shape, F32)] * 4
    if own is None:
        def body(p_ref, *refs):
            _adam_update([p_ref[s] for s in range(N_DEV)], *refs)

        blk = pl.BlockSpec((None,) * lead + (tr, cols), lambda i: (0,) * lead + (i, 0))
        return pl.pallas_call(
            body, name=name, grid=(rows // tr,),
            in_specs=[pl.BlockSpec((N_DEV, tr, cols), lambda i: (0, i, 0)), blk, blk, blk], out_specs=[blk] * 4,
            out_shape=shape4, compiler_params=_params(("arbitrary",)),
        )(parts, w, m, v)

    def body_own(me_ref, p_ref, own_ref, *refs):
        mine = own_ref[...]
        _adam_update([jnp.where(me_ref[0] == s, mine, p_ref[s]) for s in range(N_DEV)], *refs)

    blk = pl.BlockSpec((None,) * lead + (tr, cols), lambda i, me_ref: (0,) * lead + (i, 0))
    return pl.pallas_call(
        body_own, name=name, out_shape=shape4,
        grid_spec=pltpu.PrefetchScalarGridSpec(
            num_scalar_prefetch=1, grid=(rows // tr,),
            in_specs=[pl.BlockSpec((N_DEV, tr, cols), lambda i, me_ref: (0, i, 0)),
                      pl.BlockSpec((None, tr, cols), lambda i, me_ref: (me_ref[0], i, 0)), blk, blk, blk],
            out_specs=[blk] * 4),
        compiler_params=_params(("arbitrary",)),
    )(me, parts, own, w, m, v)


def kernel(x, p, norm_mix_g, w_in, conv_w, shift_mu, w_lora_up, w0, a_lora_up, a0, g_lora_up, k_k, k_a, r_k, ln_x_g, ln_x_b, w_out, norm_mlp_g, w_up, w_down, norm_ple_g, w_ple_gate, w_ple_proj, norm_final_g, loss_target, m_norm_mix_g, m_w_in, m_conv_w, m_shift_mu, m_w_lora_up, m_w0, m_a_lora_up, m_a0, m_g_lora_up, m_k_k, m_k_a, m_r_k, m_ln_x_g, m_ln_x_b, m_w_out, m_norm_mlp_g, m_w_up, m_w_down, m_norm_ple_g, m_w_ple_gate, m_w_ple_proj, m_norm_final_g, v_norm_mix_g, v_w_in, v_conv_w, v_shift_mu, v_w_lora_up, v_w0, v_a_lora_up, v_a0, v_g_lora_up, v_k_k, v_k_a, v_r_k, v_ln_x_g, v_ln_x_b, v_w_out, v_norm_mlp_g, v_w_up, v_w_down, v_norm_ple_g, v_w_ple_gate, v_w_ple_proj, v_norm_final_g):
    args = dict(locals())
    wts = {n: args[n] for n in _WEIGHTS}
    mom = {n: args["m_" + n] for n in _WEIGHTS}
    var = {n: args["v_" + n] for n in _WEIGHTS}
    shard2d = lambda a: a.reshape(a.shape[-2:])
    pad_mu = lambda a: _pad_in_cols(jnp.concatenate([jnp.zeros((1, CONV_COLS), F32), a], axis=1))[:, CONV_COLS:]
    unpad_mu = lambda a: _unpad_in_cols(jnp.concatenate([jnp.zeros((1, CONV_COLS), F32), a], axis=1))[:, CONV_COLS:]

    shards = {n: shard2d(wts[n]).astype(BF16 if n in _BF16_GATHER else F32) for n in _SHARDED}
    w = {n: wts[n].reshape(1, -1) for n in _REPLICATED}
    w["shift_mu"] = pad_mu(wts["shift_mu"])

    loss, dx, grads, late_flight, early_flight, d_w_in = _local_step(
        x[0], p[0, 0], loss_target[0], w, [shards[n] for n in _EARLY], [shards[n] for n in _LATE])

    me = (4 * lax.axis_index("x") + 2 * lax.axis_index("y") + lax.axis_index("c")).astype(jnp.int32).reshape(1)
    late_sent, late_parts = _scatter_wait("late_scatter_wait", *late_flight, after=[d_w_in])
    out = {n: _adamw("adamw_" + n, prt, wts[n], mom[n], var[n], own=own, me=me)
           for n, prt, own in zip(_LATE, late_parts, late_sent)}
    early_sent, early_parts = _scatter_wait("early_scatter_wait", *early_flight, after=[dx] + [out[n][1] for n in _LATE])
    for n, prt, own in zip(_EARLY, early_parts, early_sent):
        out[n] = _adamw("adamw_" + n, prt, wts[n], mom[n], var[n], own=own, me=me)

    grads["shift_mu"] = unpad_mu(grads["shift_mu"])
    flat = lambda a: a.reshape(1, -1)
    (small_parts,) = _all_gather("gather_small", [_pack_small([flat(grads[n]) for n in _REPLICATED], loss)])
    small, loss_total = _adamw_small(small_parts, *[[flat(d[n]) for n in _REPLICATED] for d in (wts, mom, var)])
    for n, res in zip(_REPLICATED, small):
        out[n] = [r.reshape(wts[n].shape) for r in res]
    return (loss_total[0, 0], dx[None], *[out[n][0] for n in _WEIGHTS], *[out[n][1] for n in _WEIGHTS],
            *[out[n][2] for n in _WEIGHTS], *[out[n][3] for n in _WEIGHTS])
```

```python
import functools

import jax
import jax.numpy as jnp
from jax import lax
from jax.experimental import pallas as pl
from jax.experimental.pallas import tpu as pltpu

F32 = jnp.float32
BF16 = jnp.bfloat16

N_DEV = 8
D_MODEL = 1024
CONV_DIM = 512
RWKV_DIM = 512
HEAD_DIM = 64
N_HEADS = 8
D_FF = 4096
PLE_DIM = 256
RMS_EPS = 1e-6
GN_EPS = 64e-5
L2_EPS = 1e-12
ADAM_LR, ADAM_B1, ADAM_B2, ADAM_EPS, ADAM_WD, ADAM_STEP = 0.001, 0.9, 0.999, 1e-08, 0.01, 10

CONV_COLS = 3 * CONV_DIM
RW_PAD = 2048
IN_PAD = CONV_COLS + RW_PAD
IN_COLS = 3360
XW_OFF, XA_OFF, XG_OFF = 1536, 1664, 1792
REC_CHUNK = 128
REC_CHUNKS_PER_STEP = 2
REC_PASSES = 1
ROW_BLOCK = 256
LANE = 128
VMEM_LIMIT = 56 * 1024 * 1024


def _dims(dn, ndim):
    if ndim == 3:
        return {"nn": (((2,), (1,)), ((0,), (0,))), "nt": (((2,), (2,)), ((0,), (0,))),
                "tn": (((1,), (1,)), ((0,), (0,)))}[dn]
    return {"nn": (((1,), (0,)), ((), ())), "nt": (((1,), (1,)), ((), ())), "tn": (((0,), (0,)), ((), ()))}[dn]


def _split2(x):
    hi = x.astype(BF16)
    return hi, (x - hi.astype(F32)).astype(BF16)


def _mm_raw(x, y, dn, passes):
    f = lambda p, q: lax.dot_general(p, q, _dims(dn, x.ndim), preferred_element_type=F32)
    if passes == 1:
        return f(x.astype(BF16), y.astype(BF16))
    xh, xl = _split2(x)
    yh, yl = _split2(y)
    if passes == 2:
        return f(xh, yh) + f(xh, yl)
    return f(xh, yh) + f(xh, yl) + f(xl, yh)


@functools.partial(jax.custom_vjp, nondiff_argnums=(2, 3))
def _mm(x, y, dn, passes):
    return _mm_raw(x, y, dn, passes)


def _mm_fwd(x, y, dn, passes):
    return _mm_raw(x, y, dn, passes), (x, y)


def _mm_bwd(dn, passes, res, d):
    x, y = res
    if dn == "nn":
        return _mm(d, y, "nt", passes), _mm(x, d, "tn", passes)
    if dn == "nt":
        return _mm(d, y, "nn", passes), _mm(d, x, "tn", passes)
    return _mm(y, d, "nt", passes), _mm(x, d, "nn", passes)


_mm.defvjp(_mm_fwd, _mm_bwd)


def _head_ones():
    i = lax.broadcasted_iota(jnp.int32, (RWKV_DIM, RWKV_DIM), 0) // HEAD_DIM
    j = lax.broadcasted_iota(jnp.int32, (RWKV_DIM, RWKV_DIM), 1) // HEAD_DIM
    return (i == j).astype(BF16)


def _hsum_raw(x):
    ones = _head_ones()
    f = lambda p: lax.dot_general(p, ones, _dims("nn", 2), preferred_element_type=F32)
    x1, x2 = _split2(x)
    return f(x1) + f(x2)


@jax.custom_vjp
def _hsum(x):
    return _hsum_raw(x)


_hsum.defvjp(lambda x: (_hsum_raw(x), None), lambda _, d: (_hsum(d),))


def _sigmoid(x):
    return 0.5 + 0.5 * jnp.tanh(0.5 * x)


def _softplus(x):
    return jnp.maximum(x, 0.0) + jnp.log(1.0 + jnp.exp(-jnp.abs(x)))


def _params(sem):
    return pltpu.CompilerParams(dimension_semantics=sem, vmem_limit_bytes=VMEM_LIMIT)


def _rowwise(name, fn, rows, consts, row_outs, acc_outs=(), tr=ROW_BLOCK, halo=False, gather=()):
    rows = [r if isinstance(r, tuple) else (r, r.shape[1], 0) for r in rows]
    t_len = rows[0][0].shape[0]
    tr = min(tr, t_len)
    n_r, n_c, n_o, n_a, n_x = len(rows), len(consts), len(row_outs), len(acc_outs), len(gather)
    n_h = n_r if halo else 0
    sub = 8
    x_specs, x_shapes, x_sems = _gather_io(gather) if n_x else ([], [], [])
    nb = t_len // tr

    def body(*refs):
        if n_x:
            n_in = n_r + n_h + n_c
            start, forward, wait = _gather_plan(refs[n_in:n_in + n_x], refs[len(refs) - 3 - n_x:len(refs) - 3], *refs[len(refs) - 3:])
            pl.when(pl.program_id(0) == 0)(start)
            refs = refs[:n_in] + refs[n_in + n_x:len(refs) - 3 - n_x]
        ins = [r[...] for r in refs[:n_r]]
        ins += [jnp.where(pl.program_id(0) == 0, 0.0, r[sub - 1:sub, :]) for r in refs[n_r:n_r + n_h]]
        ins += [r[...] for r in refs[n_r + n_h:n_r + n_h + n_c]]
        refs = refs[:n_r] + refs[n_r + n_h:]
        outs = fn(*ins)
        o_refs = refs[n_r + n_c:n_r + n_c + n_o]
        a_refs = refs[n_r + n_c + n_o:]
        for o_ref, val in zip(o_refs, outs[:n_o]):
            o_ref[...] = val.astype(o_ref.dtype)
        if n_a:
            first = pl.program_id(0) == 0

            @pl.when(first)
            def _():
                for a_ref, val in zip(a_refs, outs[n_o:]):
                    a_ref[...] = val

            @pl.when(jnp.logical_not(first))
            def _():
                for a_ref, val in zip(a_refs, outs[n_o:]):
                    a_ref[...] += val

        if n_x:
            @pl.when(pl.program_id(0) == nb - 1)
            def _():
                for j in range(n_x):
                    forward(j)
                wait()

    in_specs = [pl.BlockSpec((tr, w), functools.partial(lambda i, c: (i, c), c=cb)) for _, w, cb in rows]
    if halo:
        in_specs += [pl.BlockSpec((sub, w), functools.partial(lambda i, c: (jnp.maximum(i * (tr // sub) - 1, 0), c), c=cb))
                     for _, w, cb in rows]
    in_specs += [pl.BlockSpec(c.shape, functools.partial(lambda i, n: (0,) * n, n=c.ndim)) for c in consts]
    out_specs = [pl.BlockSpec((tr, w), lambda i: (i, 0)) for w, _ in row_outs]
    out_specs += [pl.BlockSpec(s, functools.partial(lambda i, n: (0,) * n, n=len(s))) for s in acc_outs]
    out_shape = [jax.ShapeDtypeStruct((t_len, w), dt) for w, dt in row_outs]
    out_shape += [jax.ShapeDtypeStruct(s, F32) for s in acc_outs]
    return pl.pallas_call(
        body, name=name, grid=(nb,), in_specs=in_specs + x_specs, out_specs=out_specs + x_specs,
        out_shape=out_shape + x_shapes, scratch_shapes=x_sems,
        compiler_params=pltpu.CompilerParams(dimension_semantics=("arbitrary",), vmem_limit_bytes=VMEM_LIMIT,
                                             has_side_effects=bool(n_x)),
    )(*[r[0] for r in rows], *([r[0] for r in rows] if halo else []), *consts, *gather)


def _colwise(name, fn, n_blocks, cols, prms, col_outs, prm_outs=()):
    t_len = cols[0][0].shape[0]
    n_i = len(cols) + len(prms)

    def body(*refs):
        outs = fn(*[r[...] for r in refs[:n_i]])
        for o_ref, val in zip(refs[n_i:], outs):
            o_ref[...] = val.astype(o_ref.dtype)

    spec = lambda r, w: pl.BlockSpec((r, w), lambda j: (0, j))
    in_specs = [spec(t_len, w) for _, w in cols] + [spec(a.shape[0], LANE) for a in prms]
    out_specs = [spec(t_len, bw) for _, _, bw in col_outs] + [spec(r, LANE) for r, _ in prm_outs]
    out_shape = [jax.ShapeDtypeStruct((t_len, w), dt) for w, dt, _ in col_outs]
    out_shape += [jax.ShapeDtypeStruct((r, w), F32) for r, w in prm_outs]
    return pl.pallas_call(
        body, name=name, grid=(n_blocks,), in_specs=in_specs, out_specs=out_specs, out_shape=out_shape,
        compiler_params=_params(("arbitrary",)),
    )(*[c[0] for c in cols], *prms)


def _matmul(name, a, b, dn, outs, *, tm, tn, tk, extras=(), consts=(), epilogue=None, sums=(), a_map=None,
            col_blocks_out=False):
    if dn == "nn":
        (m, k), n = a.shape, b.shape[1]
    elif dn == "nt":
        (m, k), n = a.shape, b.shape[0]
    else:
        (k, m), n = a.shape, b.shape[1]
    tm, tn, tk = min(tm, m), min(tn, n), min(tk, k)
    nk = k // tk
    grid = (m // tm, n // tn, nk)
    assert nk == 1 and (not sums or grid[1] == 1)
    a_spec = pl.BlockSpec((tk, tm), lambda i, j, q: (q, i)) if dn == "tn" else pl.BlockSpec((tm, tk), lambda i, j, q: (i, q))
    b_spec = pl.BlockSpec((tn, tk), lambda i, j, q: (j, q)) if dn == "nt" else pl.BlockSpec((tk, tn), lambda i, j, q: (q, j))
    o_spec = pl.BlockSpec((tm, tn), lambda i, j, q: (i, j))
    c_spec = pl.BlockSpec((1, tn), lambda i, j, q: (0, j))
    n_e, n_c, n_o, n_s = len(extras), len(consts), len(outs), len(sums)

    def body(*refs):
        a_ref, b_ref = refs[:2]
        e_refs = refs[2:2 + n_e + n_c]
        o_refs, s_refs = refs[2 + n_e + n_c:2 + n_e + n_c + n_o], refs[2 + n_e + n_c + n_o:]
        step = pl.program_id(0) * grid[1] + pl.program_id(1)
        a_blk = a_ref[...] if a_map is None else a_map(a_ref[...])
        acc = lax.dot_general(a_blk.astype(BF16), b_ref[...].astype(BF16), _dims(dn, 2), preferred_element_type=F32)
        vals = (acc,) if epilogue is None else epilogue(acc, *[e[...] for e in e_refs])
        for o_ref, val in zip(o_refs, vals[:n_o]):
            o_ref[...] = val.astype(o_ref.dtype)
        if n_s:
            @pl.when(step == 0)
            def _():
                for s_ref, val in zip(s_refs, vals[n_o:]):
                    s_ref[...] = val

            @pl.when(step > 0)
            def _():
                for s_ref, val in zip(s_refs, vals[n_o:]):
                    s_ref[...] += val

    res = pl.pallas_call(
        body, name=name, grid=grid,
        in_specs=[a_spec, b_spec] + [o_spec] * n_e + [c_spec] * n_c,
        out_specs=[pl.BlockSpec((None, tm, tn), lambda i, j, q: (j, i, 0)) if col_blocks_out else o_spec] * n_o
                  + [c_spec] * n_s,
        out_shape=[jax.ShapeDtypeStruct((n // tn, m, tn) if col_blocks_out else (m, n), dt) for dt in outs]
                  + [jax.ShapeDtypeStruct(s, F32) for s in sums],
        compiler_params=_params(("arbitrary",) * 3 if n_s else ("parallel", "parallel", "arbitrary")),
    )(a, b, *extras, *consts)
    return res[0] if len(res) == 1 else res


def _rms(h, g):
    return h * lax.rsqrt(jnp.mean(h * h, axis=-1, keepdims=True) + RMS_EPS) * g


def _rms_bwd(h, g, dy):
    rs = lax.rsqrt(jnp.mean(h * h, axis=-1, keepdims=True) + RMS_EPS)
    n = h * rs
    dn = dy * g
    dh = rs * (dn - n * jnp.mean(dn * n, axis=-1, keepdims=True))
    return dh, jnp.sum(dy * n, axis=0, keepdims=True)


def _rwkv_pre(k, xw, xa, xg, w0, a0, k_k, k_a, wl, al, gl):
    zw = w0 + _mm(jnp.tanh(xw), wl, "nn", 1)
    lw = -jnp.exp(-_softplus(-zw) - 0.5)
    iclr = _sigmoid(a0 + _mm(xa, al, "nn", 1))
    g = _mm(_sigmoid(xg), gl, "nn", 1)
    kk0 = k * k_k
    kk = kk0 * lax.rsqrt(jnp.maximum(_hsum(kk0 * kk0), L2_EPS * L2_EPS))
    k_h = k * (1.0 + (iclr - 1.0) * k_a)
    return lw, k_h, -kk, kk * iclr, g


def _rwkv_pre_vjp(k, xw, xa, xg, lw, w0, a0, k_k, k_a, wl, al, gl, dlw, dk_h, da, db, dg):
    del w0
    s0 = lambda z: jnp.sum(z, axis=0, keepdims=True)
    iclr = _sigmoid(a0 + _mm(xa, al, "nn", 1))
    kk0 = k * k_k
    s = _hsum(kk0 * kk0)
    inv = lax.rsqrt(jnp.maximum(s, L2_EPS * L2_EPS))
    kk = kk0 * inv
    dkk = db * iclr - da
    diclr = db * kk + dk_h * (k * k_a)
    dkk0 = dkk * inv - jnp.where(s > L2_EPS * L2_EPS, inv * inv * inv * _hsum(dkk * kk0), 0.0) * kk0
    dk = dk_h * (1.0 + (iclr - 1.0) * k_a) + dkk0 * k_k
    dza = diclr * iclr * (1.0 - iclr)
    dzw = dlw * lw * (1.0 + lw * 1.6487212707001282)
    th = jnp.tanh(xw)
    sg = _sigmoid(xg)
    dxw = _mm(dzw, wl, "nt", 1) * (1.0 - th * th)
    dxa = _mm(dza, al, "nt", 1)
    dxg = _mm(dg, gl, "nt", 1) * sg * (1.0 - sg)
    return (dk, dxw, dxa, dxg, s0(dzw), s0(dza), s0(dkk0 * k), s0(dk_h * k * (iclr - 1.0)),
            _mm(th, dzw, "tn", 1), _mm(xa, dza, "tn", 1), _mm(sg, dg, "tn", 1))


def _rwkv_post(y, r, k_h, v, g, ln_g, ln_b, r_k):
    mu = _hsum(y) * (1.0 / HEAD_DIM)
    yc = y - mu
    var = _hsum(yc * yc) * (1.0 / HEAD_DIM)
    yo = yc * lax.rsqrt(var + GN_EPS) * ln_g + ln_b
    bonus = _hsum(r * k_h * r_k) * v
    return (yo + bonus) * g


def _shift_down(x, n):
    rows = lax.broadcasted_iota(jnp.int32, x.shape, 0)
    return jnp.where(rows < n, 0.0, pltpu.roll(x, n, 0))


def _shift_up(x, n):
    t_len = x.shape[0]
    rows = lax.broadcasted_iota(jnp.int32, x.shape, 0)
    return jnp.where(rows >= t_len - n, 0.0, pltpu.roll(x, t_len - n, 0))


def _gather_plan(ins, outs, send_sems, recv_sems, local_sems):
    x, y, c = lax.axis_index("x"), lax.axis_index("y"), lax.axis_index("c")
    me = 4 * x + 2 * y + c
    direct, chips = (1, 2, 4, 6), (2, 4, 6)

    def local(i):
        return pltpu.make_async_copy(ins[i], outs[i].at[me], local_sems.at[i])

    def send(i, rel):
        return pltpu.make_async_remote_copy(
            src_ref=ins[i], dst_ref=outs[i].at[me], send_sem=send_sems.at[i, rel - 1], recv_sem=recv_sems.at[i, rel - 1],
            device_id=(x ^ (rel >> 2), y ^ ((rel >> 1) & 1), c ^ (rel & 1)), device_id_type=pl.DeviceIdType.MESH)

    def passed(i, rel):
        slot = outs[i].at[me ^ rel]
        return pltpu.make_async_remote_copy(
            src_ref=slot, dst_ref=slot, send_sem=send_sems.at[i, rel], recv_sem=recv_sems.at[i, rel],
            device_id=(x, y, 1 - c), device_id_type=pl.DeviceIdType.MESH)

    def landed(i, rel):
        slot = outs[i].at[me ^ rel]
        return pltpu.make_async_remote_copy(
            src_ref=slot, dst_ref=slot, send_sem=send_sems.at[i, rel - 1], recv_sem=recv_sems.at[i, rel - 1],
            device_id=(x, y, c), device_id_type=pl.DeviceIdType.MESH)

    def start():
        for i in range(len(ins)):
            local(i).start()
            for rel in direct:
                send(i, rel).start()

    def forward(i):
        for rel in chips:
            landed(i, rel).wait_recv()
            passed(i, rel).start()

    def wait():
        for i in range(len(ins)):
            local(i).wait()
            for rel in (1, 3, 5, 7):
                landed(i, rel).wait_recv()
            for rel in direct:
                send(i, rel).wait_send()
            for rel in chips:
                passed(i, rel).wait_send()

    return start, forward, wait


def _gather_io(arrays):
    n = len(arrays)
    any_spec = pl.BlockSpec(memory_space=pl.ANY)
    out_shape = [jax.ShapeDtypeStruct((N_DEV,) + a.shape, a.dtype) for a in arrays]
    sems = [pltpu.SemaphoreType.DMA((n, N_DEV - 1)), pltpu.SemaphoreType.DMA((n, N_DEV - 1)), pltpu.SemaphoreType.DMA((n,))]
    return [any_spec] * n, out_shape, sems


def _all_gather(name, arrays):
    n = len(arrays)
    specs, out_shape, sems = _gather_io(arrays)

    def body(*refs):
        start, forward, wait = _gather_plan(refs[:n], refs[n:2 * n], *refs[2 * n:])
        start()
        for i in range(n):
            forward(i)
        wait()

    return pl.pallas_call(
        body, name=name, in_specs=specs, out_specs=specs, out_shape=out_shape, scratch_shapes=sems,
        compiler_params=pltpu.CompilerParams(has_side_effects=True),
    )(*arrays)


def _scatter_start(name, arrays, lands):
    n = len(arrays)
    hbm = pl.BlockSpec(memory_space=pltpu.HBM)

    def body(*refs):
        ins, land, send_sems, recv_sems = refs[:n], refs[n:2 * n], refs[2 * n], refs[2 * n + 1]
        token = refs[4 * n + 2]
        x, y, c = lax.axis_index("x"), lax.axis_index("y"), lax.axis_index("c")
        me = 4 * x + 2 * y + c
        for i in range(n):
            for rel in range(1, N_DEV):
                k = i * (N_DEV - 1) + rel - 1
                pltpu.make_async_remote_copy(
                    src_ref=ins[i].at[me ^ rel], dst_ref=land[i].at[me], send_sem=send_sems.at[k],
                    recv_sem=recv_sems.at[k], device_id=(x ^ (rel >> 2), y ^ ((rel >> 1) & 1), c ^ (rel & 1)),
                    device_id_type=pl.DeviceIdType.MESH).start()
        token[...] = jnp.zeros_like(token)

    sem = pltpu.SemaphoreType.DMA((n * (N_DEV - 1),))
    bufs = [pltpu.HBM(a.shape, a.dtype) for a in list(arrays) + list(lands)]
    res = pl.pallas_call(
        body, name=name, out_shape=(sem, sem, *bufs, jax.ShapeDtypeStruct((8, LANE), F32)),
        in_specs=[hbm] * (2 * n),
        out_specs=(pl.BlockSpec(memory_space=pltpu.SEMAPHORE),) * 2 + (hbm,) * (2 * n) + (pl.BlockSpec(memory_space=pltpu.VMEM),),
        input_output_aliases={i: 2 + i for i in range(2 * n)},
        compiler_params=pltpu.CompilerParams(has_side_effects=pltpu.SideEffectType.DATAFLOW_SIDE_EFFECTING),
    )(*[pltpu.with_memory_space_constraint(a, pltpu.HBM) for a in list(arrays) + list(lands)])
    return res[0], res[1], res[2:2 + n], res[2 + n:2 + 2 * n], res[2 + 2 * n]


def _scatter_wait(name, send_sems, recv_sems, arrays, lands, after):
    n, n_after = len(arrays), len(after)
    hbm = pl.BlockSpec(memory_space=pltpu.HBM)

    def body(*refs):
        ins, land, s_sems, r_sems = refs[:n], refs[n:2 * n], refs[2 * n], refs[2 * n + 1]
        x, y, c = lax.axis_index("x"), lax.axis_index("y"), lax.axis_index("c")
        me = 4 * x + 2 * y + c
        for i in range(n):
            for rel in range(1, N_DEV):
                k = i * (N_DEV - 1) + rel - 1
                cp = pltpu.make_async_remote_copy(
                    src_ref=ins[i].at[me ^ rel], dst_ref=land[i].at[me ^ rel], send_sem=s_sems.at[k],
                    recv_sem=r_sems.at[k], device_id=(x, y, c), device_id_type=pl.DeviceIdType.MESH)
                cp.wait_send()
                cp.wait_recv()

    res = pl.pallas_call(
        body, name=name, out_shape=[pltpu.HBM(a.shape, a.dtype) for a in list(arrays) + list(lands)],
        in_specs=[hbm] * (2 * n) + [pl.BlockSpec(memory_space=pltpu.SEMAPHORE)] * 2 + [pl.BlockSpec(memory_space=pl.ANY)] * n_after,
        out_specs=[hbm] * (2 * n), input_output_aliases={i: i for i in range(2 * n)},
        compiler_params=pltpu.CompilerParams(has_side_effects=pltpu.SideEffectType.DATAFLOW_SIDE_EFFECTING),
    )(*arrays, *lands, send_sems, recv_sems, *after)
    return res[:n], res[n:]


def _gather_start(name, shards, after):
    n, n_after = len(shards), len(after)
    hbm = pl.BlockSpec(memory_space=pltpu.HBM)
    lands = [lax.empty((N_DEV,) + a.shape, a.dtype) for a in shards]

    def body(*refs):
        ins, land = refs[:n], refs[n:2 * n]
        outs = refs[2 * n + n_after:]
        send_sems, recv_sems, token = outs[:n], outs[n:2 * n], outs[4 * n]
        x, y, c = lax.axis_index("x"), lax.axis_index("y"), lax.axis_index("c")
        me = 4 * x + 2 * y + c
        for i in range(n):
            for rel in range(N_DEV):
                pltpu.make_async_remote_copy(
                    src_ref=ins[i], dst_ref=land[i].at[me], send_sem=send_sems[i].at[rel],
                    recv_sem=recv_sems[i].at[rel], device_id=(x ^ (rel >> 2), y ^ ((rel >> 1) & 1), c ^ (rel & 1)),
                    device_id_type=pl.DeviceIdType.MESH).start()
        token[...] = jnp.zeros_like(token)

    sem = pltpu.SemaphoreType.DMA((N_DEV,))
    bufs = [pltpu.HBM(a.shape, a.dtype) for a in list(shards) + lands]
    res = pl.pallas_call(
        body, name=name, out_shape=(*[sem] * (2 * n), *bufs, jax.ShapeDtypeStruct((8, LANE), F32)),
        in_specs=[hbm] * (2 * n) + [pl.BlockSpec(memory_space=pl.ANY)] * n_after,
        out_specs=(pl.BlockSpec(memory_space=pltpu.SEMAPHORE),) * (2 * n) + (hbm,) * (2 * n)
                  + (pl.BlockSpec(memory_space=pltpu.VMEM),),
        input_output_aliases={i: 2 * n + i for i in range(2 * n)},
        compiler_params=pltpu.CompilerParams(has_side_effects=pltpu.SideEffectType.DATAFLOW_SIDE_EFFECTING),
    )(*[pltpu.with_memory_space_constraint(a, pltpu.HBM) for a in list(shards) + lands], *after)
    return res[:n], res[n:2 * n], res[2 * n:3 * n], res[3 * n:4 * n], res[4 * n]


def _gather_wait(name, send_sems, recv_sems, shards, lands, after):
    n, n_after = len(shards), len(after)
    hbm = pl.BlockSpec(memory_space=pltpu.HBM)

    def body(*refs):
        ins, land = refs[:n], refs[n:2 * n]
        s_sems, r_sems = refs[2 * n:3 * n], refs[3 * n:4 * n]
        x, y, c = lax.axis_index("x"), lax.axis_index("y"), lax.axis_index("c")
        me = 4 * x + 2 * y + c
        for i in range(n):
            for rel in range(N_DEV):
                cp = pltpu.make_async_remote_copy(
                    src_ref=ins[i], dst_ref=land[i].at[me ^ rel], send_sem=s_sems[i].at[rel],
                    recv_sem=r_sems[i].at[rel], device_id=(x, y, c), device_id_type=pl.DeviceIdType.MESH)
                cp.wait_send()
                cp.wait_recv()

    res = pl.pallas_call(
        body, name=name, out_shape=[pltpu.HBM(a.shape, a.dtype) for a in list(shards) + list(lands)],
        in_specs=[hbm] * (2 * n) + [pl.BlockSpec(memory_space=pltpu.SEMAPHORE)] * (2 * n)
                 + [pl.BlockSpec(memory_space=pl.ANY)] * n_after,
        out_specs=[hbm] * (2 * n), input_output_aliases={i: i for i in range(2 * n)},
        compiler_params=pltpu.CompilerParams(has_side_effects=pltpu.SideEffectType.DATAFLOW_SIDE_EFFECTING),
    )(*shards, *lands, *send_sems, *recv_sems, *after)
    return res[n:]


def _tri_powers(low):
    powers, n, p = [low.astype(BF16)], 1, low
    while 2 * n < low.shape[-1]:
        p = _mm(p, p, "nn", REC_PASSES)
        powers.append(p.astype(BF16))
        n *= 2
    return powers


@jax.custom_vjp
def _tri_solve(low, rhs, powers):
    del low
    for p in powers:
        rhs = rhs + _mm(p, rhs, "nn", REC_PASSES)
    return rhs


def _tri_solve_fwd(low, rhs, powers):
    out = _tri_solve(low, rhs, powers)
    return out, (powers, out)


def _tri_solve_bwd(res, d):
    powers, u = res
    for p in powers:
        d = d + _mm(p, d, "tn", REC_PASSES)
    return _mm(d, u, "nt", REC_PASSES), d, [jnp.zeros_like(p) for p in powers]


_tri_solve.defvjp(_tri_solve_fwd, _tri_solve_bwd)


@jax.custom_vjp
def _tri_solve_given(low, rhs, powers, value):
    del low, rhs, powers
    return value


def _tri_solve_given_fwd(low, rhs, powers, value):
    return value, (powers, value)


def _tri_solve_given_bwd(res, d):
    return _tri_solve_bwd(res, d) + (jnp.zeros_like(res[1]),)


_tri_solve_given.defvjp(_tri_solve_given_fwd, _tri_solve_given_bwd)


def _heads(x):
    return jnp.stack([x[:, h * HEAD_DIM:(h + 1) * HEAD_DIM] for h in range(N_HEADS)])


def _unheads(x):
    return jnp.concatenate([x[h] for h in range(N_HEADS)], axis=-1)


def _causal_masks(c):
    ti = lax.broadcasted_iota(jnp.int32, (c, c), 0)
    si = lax.broadcasted_iota(jnp.int32, (c, c), 1)
    strict, incl = si < ti, si <= ti
    both = jnp.concatenate([jnp.concatenate([strict, strict], axis=1), jnp.concatenate([incl, incl], axis=1)], axis=0)
    return strict, incl, both


@jax.custom_vjp
def _gram_given(x2, y2, value):
    del x2, y2
    return value.astype(F32)


def _gram_given_fwd(x2, y2, value):
    return value.astype(F32), (x2, y2, value)


def _gram_given_bwd(res, d):
    x2, y2, value = res
    d = jnp.where(_causal_masks(d.shape[-1] // 2)[2], d, 0.0)
    return _mm(d, y2, "nn", 2), _mm(d, x2, "tn", 2), jnp.zeros_like(value)


_gram_given.defvjp(_gram_given_fwd, _gram_given_bwd)


def _chunk_fwd(z0, r, lw, k, v, a, b, powers=None, gram_value=None, u_value=None):
    c = r.shape[0]
    n_h, n_k = z0.shape[0], z0.shape[1]
    mm = functools.partial(_mm, passes=REC_PASSES)
    gram = functools.partial(_mm, passes=2)
    _, incl, mask = _causal_masks(c)
    cum = _mm(incl.astype(F32), lw, "nn", 3)
    cum_end = cum[c - 1:c, :]
    e_neg, e_end = jnp.exp(-cum), jnp.exp(cum_end - cum)
    x2 = jnp.concatenate([_heads(a * jnp.exp(cum - lw)), _heads(r * jnp.exp(cum))], axis=1)
    y2 = jnp.concatenate([_heads(b * e_neg), _heads(k * e_neg)], axis=1)
    vh = _heads(v)
    g2 = jnp.where(mask, gram(x2, y2, "nt"), 0.0) if gram_value is None else _gram_given(x2, y2, gram_value)
    t2 = mm(x2, z0, "nn") + mm(g2[:, :, c:], vh, "nn")
    low = g2[:, :c, :c]
    powers = _tri_powers(low) if powers is None else powers
    u = _tri_solve(low, t2[:, :c], powers) if u_value is None else _tri_solve_given(low, t2[:, :c], powers, u_value)
    y = t2[:, c:] + mm(g2[:, c:, :c], u, "nn")
    ki = lax.broadcasted_iota(jnp.int32, (n_k, n_k), 0)
    kj = lax.broadcasted_iota(jnp.int32, (n_k, n_k), 1)
    dmat = jnp.where(ki == kj, jnp.broadcast_to(_heads(jnp.exp(cum_end)), (n_h, n_k, n_k)), 0.0)
    z_end = mm(dmat, z0, "nn") + mm(jnp.concatenate([_heads(b * e_end), _heads(k * e_end)], axis=1),
                                    jnp.concatenate([u, vh], axis=1), "tn")
    return _unheads(y), z_end, powers, g2, u


def _rec_fwd(u, lw, k, a, b):
    t_len = lw.shape[0]
    c = min(REC_CHUNK, t_len)
    nc = t_len // c
    per = REC_CHUNKS_PER_STEP if nc % REC_CHUNKS_PER_STEP == 0 else 1
    steps = nc // per
    n_pow = max(1, (c - 1).bit_length())

    def body(r_ref, v_ref, lw_ref, k_ref, a_ref, b_ref, y_ref, zs_ref, pw_ref, gs_ref, us_ref, z_scr):
        @pl.when(pl.program_id(0) == 0)
        def _():
            z_scr[...] = jnp.zeros_like(z_scr)

        for s in range(per):
            rows = pl.ds(s * c, c)
            z0 = z_scr[...]
            zs_ref[s] = z0
            y, z_end, powers, g2, u_rows = _chunk_fwd(z0, r_ref[rows, :], lw_ref[rows, :], k_ref[rows, :], v_ref[rows, :],
                                                      a_ref[rows, :], b_ref[rows, :])
            y_ref[rows, :] = y
            z_scr[...] = z_end
            pw_ref[s] = jnp.concatenate(powers, axis=0)
            gs_ref[s] = g2.astype(BF16)
            us_ref[s] = u_rows

    blk = lambda cb: pl.BlockSpec((per * c, RWKV_DIM), functools.partial(lambda i, q: (i, q), q=cb))
    res = pl.pallas_call(
        body, name="rwkv_rec_fwd", grid=(steps,),
        in_specs=[blk(0), blk(2)] + [blk(0)] * 4,
        out_specs=[blk(0), pl.BlockSpec((per, N_HEADS, HEAD_DIM, HEAD_DIM), lambda i: (i, 0, 0, 0)),
                   pl.BlockSpec((per, n_pow * N_HEADS, c, c), lambda i: (i, 0, 0, 0)),
                   pl.BlockSpec((per, N_HEADS, 2 * c, 2 * c), lambda i: (i, 0, 0, 0)),
                   pl.BlockSpec((per, N_HEADS, c, HEAD_DIM), lambda i: (i, 0, 0, 0))],
        out_shape=[jax.ShapeDtypeStruct((t_len, RWKV_DIM), F32),
                   jax.ShapeDtypeStruct((nc, N_HEADS, HEAD_DIM, HEAD_DIM), F32),
                   jax.ShapeDtypeStruct((nc, n_pow * N_HEADS, c, c), BF16),
                   jax.ShapeDtypeStruct((nc, N_HEADS, 2 * c, 2 * c), BF16),
                   jax.ShapeDtypeStruct((nc, N_HEADS, c, HEAD_DIM), F32)],
        scratch_shapes=[pltpu.VMEM((N_HEADS, HEAD_DIM, HEAD_DIM), F32)], compiler_params=_params(("arbitrary",)),
    )(u, u, lw, k, a, b)
    return res[0], res[1:]


def _rec_bwd(u, lw, k, a, b, saved, dy):
    t_len = lw.shape[0]
    c = min(REC_CHUNK, t_len)
    nc = t_len // c
    per = REC_CHUNKS_PER_STEP if nc % REC_CHUNKS_PER_STEP == 0 else 1
    steps = nc // per

    zs, pw, gs, us = saved

    def body(r_ref, v_ref, lw_ref, k_ref, a_ref, b_ref, zs_ref, pw_ref, gs_ref, us_ref, dy_ref, *rest):
        g_refs, dz_scr = rest[:6], rest[6]

        @pl.when(pl.program_id(0) == 0)
        def _():
            dz_scr[...] = jnp.zeros_like(dz_scr)

        for s in reversed(range(per)):
            rows = pl.ds(s * c, c)
            powers = [pw_ref[s, j * N_HEADS:(j + 1) * N_HEADS] for j in range(pw.shape[1] // N_HEADS)]
            chunk = functools.partial(lambda gram, u_val, pws, *xs: _chunk_fwd(*xs, powers=pws, gram_value=gram, u_value=u_val)[:2],
                                      gs_ref[s], us_ref[s], powers)
            _, vjp = jax.vjp(chunk, zs_ref[s], r_ref[rows, :], lw_ref[rows, :], k_ref[rows, :], v_ref[rows, :],
                             a_ref[rows, :], b_ref[rows, :])
            dz0, dr, dlw, dk, dv, da, db = vjp((dy_ref[rows, :], dz_scr[...]))
            for ref, val in zip(g_refs, (dr, dv, dlw, dk, da, db)):
                ref[rows, :] = val
            dz_scr[...] = dz0

    blk = lambda cb: pl.BlockSpec((per * c, RWKV_DIM), functools.partial(lambda i, q: (steps - 1 - i, q), q=cb))
    saved_blk = lambda arr: pl.BlockSpec((per,) + arr.shape[1:], lambda i: (steps - 1 - i, 0, 0, 0))
    return pl.pallas_call(
        body, name="rwkv_rec_bwd", grid=(steps,),
        in_specs=[blk(0), blk(2)] + [blk(0)] * 4 + [saved_blk(zs), saved_blk(pw), saved_blk(gs), saved_blk(us), blk(0)],
        out_specs=[blk(0)] * 6, out_shape=[jax.ShapeDtypeStruct((t_len, RWKV_DIM), F32)] * 6,
        scratch_shapes=[pltpu.VMEM((N_HEADS, HEAD_DIM, HEAD_DIM), F32)], compiler_params=_params(("arbitrary",)),
    )(u, u, lw, k, a, b, zs, pw, gs, us, dy)


_EARLY = ["w_in", "conv_w", "w_lora_up", "a_lora_up", "g_lora_up"]
_LATE = ["w_out", "w_up", "w_down", "w_ple_gate", "w_ple_proj"]
_SHARDED = _EARLY + _LATE
_COL_SHARDED = {"w_in", "conv_w", "w_lora_up", "a_lora_up", "g_lora_up", "w_up", "w_ple_proj"}
_BF16_GATHER = {"w_in", "w_out", "w_up", "w_down", "w_ple_gate", "w_ple_proj"}
_REPLICATED = ["norm_mix_g", "shift_mu", "w0", "a0", "k_k", "k_a", "r_k", "ln_x_g", "ln_x_b", "norm_mlp_g", "norm_ple_g",
               "norm_final_g"]
_WEIGHTS = ["norm_mix_g", "w_in", "conv_w", "shift_mu", "w_lora_up", "w0", "a_lora_up", "a0", "g_lora_up", "k_k", "k_a", "r_k",
            "ln_x_g", "ln_x_b", "w_out", "norm_mlp_g", "w_up", "w_down", "norm_ple_g", "w_ple_gate", "w_ple_proj", "norm_final_g"]


def _unshard(name, g):
    if name in _COL_SHARDED:
        return jnp.moveaxis(g, 0, 1).reshape(g.shape[1], N_DEV * g.shape[2])
    return g.reshape(N_DEV * g.shape[1], g.shape[2])


def _reshard(name, full):
    if name in _COL_SHARDED:
        return jnp.moveaxis(full.reshape(full.shape[0], N_DEV, full.shape[1] // N_DEV), 1, 0)
    return full.reshape(N_DEV, full.shape[0] // N_DEV, full.shape[1])


def _pad_in_cols(a):
    z = lambda n: jnp.zeros(a.shape[:-1] + (n,), a.dtype)
    conv = [a[..., part * CONV_DIM + j * LANE:part * CONV_DIM + (j + 1) * LANE] for j in range(CONV_DIM // LANE) for part in range(3)]
    return jnp.concatenate(conv + [a[..., CONV_COLS:3136], z(64), a[..., 3136:3200], z(64), a[..., 3200:3360], z(96)], axis=-1)


def _unpad_in_cols(a):
    conv = [a[..., (3 * j + part) * LANE:(3 * j + part + 1) * LANE] for part in range(3) for j in range(CONV_DIM // LANE)]
    return jnp.concatenate(conv + [a[..., CONV_COLS:3136], a[..., 3200:3264], a[..., 3328:3488]], axis=-1)


def _assemble_w_in(g):
    n_dev, rows, cols = g.shape

    def body(g_ref, o_ref):
        o_ref[...] = _pad_in_cols(jnp.concatenate([g_ref[d] for d in range(n_dev)], axis=1))

    return pl.pallas_call(
        body, name="w_in_assemble", grid=(rows // ROW_BLOCK,),
        in_specs=[pl.BlockSpec((n_dev, ROW_BLOCK, cols), lambda i: (0, i, 0))],
        out_specs=pl.BlockSpec((ROW_BLOCK, IN_PAD), lambda i: (i, 0)),
        out_shape=jax.ShapeDtypeStruct((rows, IN_PAD), g.dtype), compiler_params=_params(("arbitrary",)),
    )(g)


def _split_w_in_grad(dw):
    rows = dw.shape[0]
    cols = IN_COLS // N_DEV

    def body(d_ref, o_ref):
        full = _unpad_in_cols(d_ref[...])
        for d in range(N_DEV):
            o_ref[d] = full[:, cols * d:cols * (d + 1)]

    return pl.pallas_call(
        body, name="w_in_grad_split", grid=(rows // ROW_BLOCK,),
        in_specs=[pl.BlockSpec((ROW_BLOCK, IN_PAD), lambda i: (i, 0))],
        out_specs=pl.BlockSpec((N_DEV, ROW_BLOCK, cols), lambda i: (0, i, 0)),
        out_shape=jax.ShapeDtypeStruct((N_DEV, rows, cols), dw.dtype), compiler_params=_params(("arbitrary",)),
    )(dw)


def _pad_rows(a, rows):
    return jnp.concatenate([a, jnp.zeros((rows - a.shape[0],) + a.shape[1:], a.dtype)], axis=0)


SEG_W = [RWKV_DIM, RWKV_DIM, RWKV_DIM, LANE, LANE, 2 * LANE]
SEG_OFF = [0, 512, 1024, XW_OFF, XA_OFF, XG_OFF]


def _rwkv_pre_bwd(proj, u, grads, mu, small, dproj):
    t_len = u.shape[0]
    tr = min(ROW_BLOCK, t_len)
    nb = t_len // tr
    sub = 8
    n_g = len(grads)
    acc_shapes = [(1, RW_PAD)] + [(1, RWKV_DIM)] * 4 + [(LANE, RWKV_DIM), (LANE, RWKV_DIM), (2 * LANE, RWKV_DIM)]

    def body(*refs):
        seg_refs, halo_refs = refs[:6], refs[6:12]
        k_ref, xw_ref, xa_ref, xg_ref = refs[12:16]
        g_refs = refs[16:16 + n_g]
        mu_ref = refs[16 + n_g]
        prm_refs = refs[17 + n_g:24 + n_g]
        out_hbm = refs[25 + n_g]
        acc_refs = refs[26 + n_g:26 + n_g + len(acc_shapes)]
        vbuf, sems, carry = refs[26 + n_g + len(acc_shapes):]
        i = pl.program_id(0)
        blk = nb - 1 - i
        dr1, dr2, dv1, dv2, dlw, dk1, dk2, da, db, dg, lw_rows = [g[...] for g in g_refs]
        dk, dxw, dxa, dxg, *dprm = _rwkv_pre_vjp(k_ref[...], xw_ref[...], xa_ref[...], xg_ref[...], lw_rows,
                                                  *[p_[...] for p_ in prm_refs], dlw, dk1 + dk2, da, db, dg)
        du = jnp.concatenate([dr1 + dr2, dk, dv1 + dv2, dxw, dxa, dxg], axis=1)
        mu_v = mu_ref[...]

        @pl.when(i == 0)
        def _():
            carry[...] = jnp.zeros_like(carry)

        rows = lax.broadcasted_iota(jnp.int32, du.shape, 0)
        nxt = jnp.where(rows == tr - 1, carry[...], pltpu.roll(du, tr - 1, 0))
        d_rw = du - mu_v * du + mu_v * nxt
        d_mu = []
        for s_ref, h_ref, off, wd in zip(seg_refs, halo_refs, SEG_OFF, SEG_W):
            cur = s_ref[...]
            r0 = lax.broadcasted_iota(jnp.int32, cur.shape, 0)
            prev = jnp.where(r0 == 0, jnp.where(blk == 0, 0.0, h_ref[sub - 1:sub, :]), pltpu.roll(cur, 1, 0))
            d_mu.append(jnp.sum(du[:, off:off + wd] * (prev - cur), axis=0, keepdims=True))
        sums = [jnp.concatenate(d_mu, axis=1)] + list(dprm)

        @pl.when(i == 0)
        def _():
            for a_ref, val in zip(acc_refs, sums):
                a_ref[...] = val

        @pl.when(i > 0)
        def _():
            for a_ref, val in zip(acc_refs, sums):
                a_ref[...] += val

        carry[...] = du[0:1, :]
        slot = i % 2

        def writeback(s, b):
            return pltpu.make_async_copy(vbuf.at[s], out_hbm.at[pl.ds(b * tr, tr), pl.ds(CONV_COLS, RW_PAD)], sems.at[s])

        @pl.when(i >= 2)
        def _():
            writeback(slot, blk + 2).wait()

        vbuf[slot] = d_rw.astype(vbuf.dtype)
        writeback(slot, blk).start()

        @pl.when(i == nb - 1)
        def _():
            writeback(slot, blk).wait()
            if nb > 1:
                writeback(1 - slot, blk + 1).wait()

    rev = lambda w_, cb: pl.BlockSpec((tr, w_), functools.partial(lambda i, c: (nb - 1 - i, c), c=cb))
    halo = lambda w_, cb: pl.BlockSpec((sub, w_), functools.partial(
        lambda i, c: (jnp.maximum((nb - 1 - i) * (tr // sub) - 1, 0), c), c=cb))
    whole = lambda a: pl.BlockSpec(a.shape, functools.partial(lambda i, n: (0,) * n, n=a.ndim))
    segs = [(wd, (CONV_COLS + off) // wd) for off, wd in zip(SEG_OFF, SEG_W)]
    u_cols = [(512, 1), (LANE, XW_OFF // LANE), (LANE, XA_OFF // LANE), (2 * LANE, XG_OFF // (2 * LANE))]
    any_spec = pl.BlockSpec(memory_space=pl.ANY)
    res = pl.pallas_call(
        body, name="rwkv_pre_bwd", grid=(nb,),
        in_specs=[rev(*s) for s in segs] + [halo(*s) for s in segs] + [rev(*c) for c in u_cols]
                 + [rev(RWKV_DIM, 0)] * n_g + [whole(mu)] + [whole(p_) for p_ in small] + [any_spec],
        out_specs=[any_spec] + [pl.BlockSpec(s, functools.partial(lambda i, n: (0,) * n, n=len(s))) for s in acc_shapes],
        out_shape=[jax.ShapeDtypeStruct(dproj.shape, dproj.dtype)] + [jax.ShapeDtypeStruct(s, F32) for s in acc_shapes],
        scratch_shapes=[pltpu.VMEM((2, tr, RW_PAD), dproj.dtype), pltpu.SemaphoreType.DMA((2,)), pltpu.VMEM((1, RW_PAD), F32)],
        input_output_aliases={24 + n_g: 0},
        compiler_params=_params(("arbitrary",)),
    )(*[proj] * 12, *[u] * 4, *grads, mu, *small, dproj)
    return res


def _local_step(x, p, tgt, w, early_shards, late_shards):
    row = lambda v: v.reshape(1, -1)
    w = dict(w)

    xn1, *gathered = _rowwise("rms_mix", lambda h, g: (_rms(h, g),), [x], [w["norm_mix_g"]], [(D_MODEL, BF16)],
                              gather=early_shards)
    w.update({n: _unshard(n, g_) for n, g_ in zip(_EARLY[1:], gathered[1:])})
    w["w_in"] = _assemble_w_in(gathered[0])
    w["w_lora_up"] = _pad_rows(w["w_lora_up"], LANE)
    w["a_lora_up"] = _pad_rows(w["a_lora_up"], LANE)
    w["g_lora_up"] = _pad_rows(w["g_lora_up"], 2 * LANE)
    lg_send, lg_recv, lg_shards, lg_lands, lg_token = _gather_start("late_gather_start", late_shards, after=[xn1])
    w["shift_mu"] = w["shift_mu"] + lg_token[0:1, 0:1]
    proj = _matmul("in_proj", xn1, w["w_in"], "nn", [F32], tm=2048, tn=512, tk=D_MODEL)
    n_cb = CONV_DIM // LANE

    def conv_fwd(blk, cw):
        gb, gc, hx = blk[:, :LANE], blk[:, LANE:2 * LANE], blk[:, 2 * LANE:]
        uu = gc * hx
        return (gb * (uu * cw[2:3] + _shift_down(uu, 1) * cw[1:2] + _shift_down(uu, 2) * cw[0:1]),)

    (y_conv,) = _colwise("conv_fwd", conv_fwd, n_cb, [(proj, 3 * LANE)], [w["conv_w"]], [(CONV_DIM, BF16, LANE)])

    small = [w["w0"], w["a0"], w["k_k"], w["k_a"], w["w_lora_up"], w["a_lora_up"], w["g_lora_up"]]
    def pre_fwd(*xs):
        cur, prev_rows, mu, prm = xs[:6], xs[6:12], xs[12], xs[13:]
        segs = []
        for c_, p_, off, wd in zip(cur, prev_rows, SEG_OFF, SEG_W):
            rows = lax.broadcasted_iota(jnp.int32, c_.shape, 0)
            prev = jnp.where(rows == 0, p_, pltpu.roll(c_, 1, 0))
            segs.append(c_ + mu[:, off:off + wd] * (prev - c_))
        return (jnp.concatenate(segs, axis=1),) + tuple(_rwkv_pre(segs[1], segs[3], segs[4], segs[5], *prm))

    proj_segs = [(proj, wd, (CONV_COLS + off) // wd) for off, wd in zip(SEG_OFF, SEG_W)]
    u, lw, k_h, ra, rb, g = _rowwise(
        "rwkv_pre", pre_fwd, proj_segs, [w["shift_mu"]] + small, [(RW_PAD, F32)] + [(RWKV_DIM, F32)] * 5, halo=True)
    y_rec, rec_saved = _rec_fwd(u, lw, k_h, ra, rb)

    def late_weight(names, after):
        idx = [_LATE.index(n) for n in names]
        got = _gather_wait("late_gather_wait_" + names[0], [lg_send[i] for i in idx], [lg_recv[i] for i in idx],
                           [lg_shards[i] for i in idx], [lg_lands[i] for i in idx], after)
        return [_unshard(n, g_) for n, g_ in zip(names, got)]

    w["w_out"], w["w_up"], w["w_down"] = late_weight(["w_out", "w_up", "w_down"], [y_rec])
    post_c = [w["ln_x_g"], w["ln_x_b"], w["r_k"]]
    u_r, u_v = (u, 512, 0), (u, 512, 2)
    (y_rwkv,) = _rowwise("rwkv_post", lambda *xs: (_rwkv_post(*xs),), [y_rec, u_r, k_h, u_v, g], post_c, [(RWKV_DIM, BF16)],
                         tr=2 * ROW_BLOCK)
    ycat = jnp.concatenate([y_conv, y_rwkv], axis=1)
    def res_norm(acc, r_, g_):
        h = acc + r_
        return h, _rms(h, g_)

    h1, xn2 = _matmul("out_proj", ycat, w["w_out"], "nn", [F32, BF16], tm=1024, tn=D_MODEL, tk=D_MODEL, extras=[x],
                      consts=[w["norm_mlp_g"]], epilogue=res_norm)

    square = lambda h: h.astype(F32) * h.astype(F32)
    hid = _matmul("mlp_up", xn2, w["w_up"], "nn", [BF16], tm=2048, tn=1024, tk=D_MODEL,
                  epilogue=lambda acc: (jnp.maximum(acc, 0.0),))
    h2, xn3 = _matmul("mlp_down", hid, w["w_down"], "nn", [F32, BF16], tm=512, tn=D_MODEL, tk=D_FF, extras=[h1],
                      consts=[w["norm_ple_g"]], epilogue=res_norm, a_map=square)
    w["w_ple_gate"], w["w_ple_proj"] = late_weight(["w_ple_gate", "w_ple_proj"], [xn3])
    zg =_matmul("ple_gate", xn3, w["w_ple_gate"], "nn", [F32], tm=1024, tn=1024, tk=D_MODEL)
    pp = _matmul("ple_proj", p, w["w_ple_proj"], "nn", [F32], tm=1024, tn=1024, tk=PLE_DIM)

    def head(h2_, zg_, pp_, tg, gf):
        gate = _sigmoid(zg_)
        h3 = h2_ + gate * pp_
        out = _rms(h3, gf)
        err = out - tg
        dh3, dgf = _rms_bwd(h3, gf, err * (1.0 / D_MODEL))
        loss = jnp.sum(jnp.sum(err * err, axis=1, keepdims=True), axis=0, keepdims=True) * (0.5 / D_MODEL)
        return dh3, dh3 * pp_ * gate * (1.0 - gate), dh3 * gate, dgf, loss

    dh3, dzg, dpp, d_norm_final, loss = _rowwise(
        "head", head, [h2, zg, pp, tgt], [row(w["norm_final_g"])], [(D_MODEL, F32), (D_MODEL, BF16), (D_MODEL, BF16)],
        [(1, D_MODEL), (1, 1)], tr=2 * ROW_BLOCK)

    d_w_ple_proj = _matmul("d_ple_proj", p, dpp, "tn", [BF16], tm=PLE_DIM, tn=D_MODEL // N_DEV, tk=4096, col_blocks_out=True)
    d_w_ple_gate = _matmul("d_ple_gate", xn3, dzg, "tn", [BF16], tm=512, tn=1024, tk=4096)

    def norm_bwd(dxn, h, dres, g_):
        dh, dg = _rms_bwd(h, g_, dxn)
        dh = dh + dres
        return dh, dh, dg

    nb = dict(tm=512, tn=D_MODEL, epilogue=norm_bwd, sums=[(1, D_MODEL)])
    dh2, dh2_b, d_norm_ple = _matmul("dx_ple_gate", dzg, w["w_ple_gate"], "nt", [F32, BF16], tk=D_MODEL,
                                     extras=[h2, dh3], consts=[w["norm_ple_g"]], **nb)
    d_w_down = _matmul("d_mlp_down", hid, dh2_b, "tn", [BF16], tm=512, tn=1024, tk=4096, a_map=square)
    dpre = _matmul("dx_mlp_down", dh2_b, w["w_down"], "nt", [BF16], tm=2048, tn=1024, tk=D_MODEL, extras=[hid],
                   epilogue=lambda acc, hid_: (acc * (2.0 * hid_.astype(F32)),))
    d_w_up = _matmul("d_mlp_up", xn2, dpre, "tn", [BF16], tm=1024, tn=D_FF // N_DEV, tk=4096, col_blocks_out=True)
    dh1, dh1_b, d_norm_mlp = _matmul("dx_mlp_up", dpre, w["w_up"], "nt", [F32, BF16], tk=D_FF,
                                     extras=[h1, dh2], consts=[w["norm_mlp_g"]], **nb)
    d_w_out = _matmul("d_out_proj", ycat, dh1_b, "tn", [BF16], tm=512, tn=1024, tk=4096)
    dycat = _matmul("dx_out_proj", dh1_b, w["w_out"], "nt", [F32], tm=1024, tn=1024, tk=D_MODEL)
    late_grads = dict(w_out=d_w_out, w_up=d_w_up, w_down=d_w_down, w_ple_gate=d_w_ple_gate, w_ple_proj=d_w_ple_proj)
    late_send = [late_grads[n] if n in ("w_up", "w_ple_proj") else _reshard(n, late_grads[n]) for n in _LATE]
    *late_flight, late_token = _scatter_start("late_scatter_start", late_send, [lax.empty(a.shape, a.dtype) for a in late_send])
    conv_w_bwd = w["conv_w"] + late_token[0:1, 0:1]

    def conv_bwd(dy, blk, cw):
        gb, gc, hx = blk[:, :LANE], blk[:, LANE:2 * LANE], blk[:, 2 * LANE:]
        uu = gc * hx
        u1, u2 = _shift_down(uu, 1), _shift_down(uu, 2)
        dconv = dy * gb
        du = dconv * cw[2:3] + _shift_up(dconv, 1) * cw[1:2] + _shift_up(dconv, 2) * cw[0:1]
        s = lambda z: jnp.sum(z, axis=0, keepdims=True)
        d_blk = jnp.concatenate([dy * (uu * cw[2:3] + u1 * cw[1:2] + u2 * cw[0:1]), du * hx, du * gc], axis=1)
        return d_blk, s(dconv * u2), s(dconv * u1), s(dconv * uu)

    dproj, dcw0, dcw1, dcw2 = _colwise(
        "conv_bwd", conv_bwd, n_cb, [(dycat, LANE), (proj, 3 * LANE)], [conv_w_bwd],
        [(IN_PAD, BF16, 3 * LANE)], [(1, CONV_DIM)] * 3)

    def post_bwd(dy, y, r, k_h_, v, g_, ln_g, ln_b, r_k):
        _, vjp = jax.vjp(_rwkv_post, y, r, k_h_, v, g_, ln_g, ln_b, r_k)
        return vjp(dy)

    dy_rec, dr_p, dk_p, dv_p, dg, d_ln_g, d_ln_b, d_r_k = _rowwise(
        "rwkv_post_bwd", post_bwd, [(dycat, 512, 1), y_rec, u_r, k_h, u_v, g], post_c,
        [(RWKV_DIM, F32)] * 5, [(1, RWKV_DIM)] * 3)
    dr_r, dv_r, dlw, dk_r, da, db = _rec_bwd(u, lw, k_h, ra, rb, rec_saved, dy_rec)

    dproj, d_mu, d_w0, d_a0, d_k_k, d_k_a, d_wl, d_al, d_gl = _rwkv_pre_bwd(
        proj, u, [dr_p, dr_r, dv_p, dv_r, dlw, dk_p, dk_r, da, db, dg, lw], w["shift_mu"], small, dproj)
    d_w_in = _matmul("d_in_proj", xn1, dproj, "tn", [BF16], tm=1024, tn=896, tk=4096)
    early_grads = dict(conv_w=jnp.concatenate([dcw0, dcw1, dcw2], axis=0),
                       w_lora_up=d_wl[:64], a_lora_up=d_al[:64], g_lora_up=d_gl[:160])
    early_send = [_split_w_in_grad(d_w_in)] + [_reshard(n, early_grads[n]) for n in _EARLY[1:]]
    *early_flight, token = _scatter_start("early_scatter_start", early_send, [lax.empty(a.shape, a.dtype) for a in early_send])
    dx, d_norm_mix = _matmul(
        "dx_in_proj", dproj, w["w_in"], "nt", [F32], tk=IN_PAD, extras=[x, dh1], consts=[w["norm_mix_g"] + token[0:1, 0:1]],
        **dict(nb, epilogue=lambda *a: norm_bwd(*a)[1:]))

    grads = dict(
        norm_mix_g=d_norm_mix, shift_mu=d_mu, w0=d_w0, a0=d_a0, k_k=d_k_k, k_a=d_k_a, r_k=d_r_k,
        ln_x_g=d_ln_g, ln_x_b=d_ln_b, norm_mlp_g=d_norm_mlp, norm_ple_g=d_norm_ple, norm_final_g=d_norm_final)
    return loss, dx, grads, late_flight, early_flight, d_w_in


def _adam_update(partials, w_ref, m_ref, v_ref, g_ref, d_ref, nm_ref, nv_ref):
    g = partials[0].astype(F32)
    for part in partials[1:]:
        g = g + part.astype(F32)
    nm =ADAM_B1 * m_ref[...] + (1.0 - ADAM_B1) * g
    nv = ADAM_B2 * v_ref[...] + (1.0 - ADAM_B2) * (g * g)
    m_hat = nm / (1.0 - ADAM_B1 ** ADAM_STEP)
    v_hat = nv / (1.0 - ADAM_B2 ** ADAM_STEP)
    g_ref[...] = g
    d_ref[...] = -ADAM_LR * (m_hat / (jnp.sqrt(v_hat) + ADAM_EPS) + ADAM_WD * w_ref[...])
    nm_ref[...] = nm
    nv_ref[...] = nv


SMALL_ROWS = 8


def _small_layout(widths):
    widths = list(widths) + [1]
    fill, place = [0] * SMALL_ROWS, [None] * len(widths)
    for j in sorted(range(len(widths)), key=lambda q: -widths[q]):
        row = fill.index(min(fill))
        place[j] = (row, fill[row])
        fill[row] += -(-widths[j] // LANE) * LANE
    return place, max(fill)


def _pack_small(vecs, loss):
    place, total = _small_layout([v_.shape[1] for v_ in vecs])
    n = len(vecs)

    def body(*refs):
        out = jnp.zeros((SMALL_ROWS, total), F32)
        row_id = lax.broadcasted_iota(jnp.int32, (SMALL_ROWS, total), 0)
        for row in range(SMALL_ROWS):
            mine = sorted((off, j) for j, (r_, off) in enumerate(place) if r_ == row)
            pieces, at = [], 0
            for off, j in mine:
                val = refs[j][...]
                pieces.append(val)
                at = off + val.shape[1]
                pad = -val.shape[1] % LANE
                if pad:
                    pieces.append(jnp.zeros((1, pad), F32))
                    at += pad
            if total > at:
                pieces.append(jnp.zeros((1, total - at), F32))
            out = jnp.where(row_id == row, jnp.broadcast_to(jnp.concatenate(pieces, axis=1), (SMALL_ROWS, total)), out)
        refs[n + 1][...] = out

    return pl.pallas_call(body, name="pack_small", out_shape=jax.ShapeDtypeStruct((SMALL_ROWS, total), F32))(*vecs, loss)


def _adamw_small(packed, ws, ms, vs):
    n = len(ws)
    place, _ = _small_layout([w_.shape[1] for w_ in ws])

    def body(p_ref, *refs):
        w_refs, m_refs, v_refs, outs = refs[:n], refs[n:2 * n], refs[2 * n:3 * n], refs[3 * n:]
        for j in range(n):
            row, off = place[j]
            cols = pl.ds(off, ws[j].shape[1])
            _adam_update([p_ref[s, row:row + 1, cols] for s in range(N_DEV)], w_refs[j], m_refs[j], v_refs[j],
                         *outs[4 * j:4 * j + 4])
        row, off = place[n]
        total = p_ref[0, row:row + 1, off:off + 1]
        for s in range(1, N_DEV):
            total = total + p_ref[s, row:row + 1, off:off + 1]
        outs[4 * n][...] = total

    res = pl.pallas_call(
        body, name="adamw_small",
        out_shape=[jax.ShapeDtypeStruct(w_.shape, F32) for w_ in ws for _ in range(4)] + [jax.ShapeDtypeStruct((1, 1), F32)],
    )(packed, *ws, *ms, *vs)
    return [res[4 * j:4 * j + 4] for j in range(n)], res[4 * n]


def _adamw(name, parts, w, m, v, own=None, me=None):
    rows, cols = w.shape[-2:]
    lead = w.ndim - 2
    tr = rows if rows * cols * 4 * 8 <= (4 << 20) else max(8, (4 << 20) // (cols * 4 * 8) // 8 * 8)
    while rows % tr:
        tr -= 8
    shape4 = [jax.ShapeDtypeStruct(w.shape, F32)] * 4
    if own is None:
        def body(p_ref, *refs):
            _adam_update([p_ref[s] for s in range(N_DEV)], *refs)

        blk = pl.BlockSpec((None,) * lead + (tr, cols), lambda i: (0,) * lead + (i, 0))
        return pl.pallas_call(
            body, name=name, grid=(rows // tr,),
            in_specs=[pl.BlockSpec((N_DEV, tr, cols), lambda i: (0, i, 0)), blk, blk, blk], out_specs=[blk] * 4,
            out_shape=shape4, compiler_params=_params(("arbitrary",)),
        )(parts, w, m, v)

    def body_own(me_ref, p_ref, own_ref, *refs):
        mine = own_ref[...]
        _adam_update([jnp.where(me_ref[0] == s, mine, p_ref[s]) for s in range(N_DEV)], *refs)

    blk = pl.BlockSpec((None,) * lead + (tr, cols), lambda i, me_ref: (0,) * lead + (i, 0))
    return pl.pallas_call(
        body_own, name=name, out_shape=shape4,
        grid_spec=pltpu.PrefetchScalarGridSpec(
            num_scalar_prefetch=1, grid=(rows // tr,),
            in_specs=[pl.BlockSpec((N_DEV, tr, cols), lambda i, me_ref: (0, i, 0)),
                      pl.BlockSpec((None, tr, cols), lambda i, me_ref: (me_ref[0], i, 0)), blk, blk, blk],
            out_specs=[blk] * 4),
        compiler_params=_params(("arbitrary",)),
    )(me, parts, own, w, m, v)


def kernel(x, p, norm_mix_g, w_in, conv_w, shift_mu, w_lora_up, w0, a_lora_up, a0, g_lora_up, k_k, k_a, r_k, ln_x_g, ln_x_b, w_out, norm_mlp_g, w_up, w_down, norm_ple_g, w_ple_gate, w_ple_proj, norm_final_g, loss_target, m_norm_mix_g, m_w_in, m_conv_w, m_shift_mu, m_w_lora_up, m_w0, m_a_lora_up, m_a0, m_g_lora_up, m_k_k, m_k_a, m_r_k, m_ln_x_g, m_ln_x_b, m_w_out, m_norm_mlp_g, m_w_up, m_w_down, m_norm_ple_g, m_w_ple_gate, m_w_ple_proj, m_norm_final_g, v_norm_mix_g, v_w_in, v_conv_w, v_shift_mu, v_w_lora_up, v_w0, v_a_lora_up, v_a0, v_g_lora_up, v_k_k, v_k_a, v_r_k, v_ln_x_g, v_ln_x_b, v_w_out, v_norm_mlp_g, v_w_up, v_w_down, v_norm_ple_g, v_w_ple_gate, v_w_ple_proj, v_norm_final_g):
    args = dict(locals())
    wts = {n: args[n] for n in _WEIGHTS}
    mom = {n: args["m_" + n] for n in _WEIGHTS}
    var = {n: args["v_" + n] for n in _WEIGHTS}
    shard2d = lambda a: a.reshape(a.shape[-2:])
    pad_mu = lambda a: _pad_in_cols(jnp.concatenate([jnp.zeros((1, CONV_COLS), F32), a], axis=1))[:, CONV_COLS:]
    unpad_mu = lambda a: _unpad_in_cols(jnp.concatenate([jnp.zeros((1, CONV_COLS), F32), a], axis=1))[:, CONV_COLS:]

    shards = {n: shard2d(wts[n]).astype(BF16 if n in _BF16_GATHER else F32) for n in _SHARDED}
    w = {n: wts[n].reshape(1, -1) for n in _REPLICATED}
    w["shift_mu"] = pad_mu(wts["shift_mu"])

    loss, dx, grads, late_flight, early_flight, d_w_in = _local_step(
        x[0], p[0, 0], loss_target[0], w, [shards[n] for n in _EARLY], [shards[n] for n in _LATE])

    me = (4 * lax.axis_index("x") + 2 * lax.axis_index("y") + lax.axis_index("c")).astype(jnp.int32).reshape(1)
    late_sent, late_parts = _scatter_wait("late_scatter_wait", *late_flight, after=[d_w_in])
    out = {n: _adamw("adamw_" + n, prt, wts[n], mom[n], var[n], own=own, me=me)
           for n, prt, own in zip(_LATE, late_parts, late_sent)}
    early_sent, early_parts = _scatter_wait("early_scatter_wait", *early_flight, after=[dx] + [out[n][1] for n in _LATE])
    for n, prt, own in zip(_EARLY, early_parts, early_sent):
        out[n] = _adamw("adamw_" + n, prt, wts[n], mom[n], var[n], own=own, me=me)

    grads["shift_mu"] = unpad_mu(grads["shift_mu"])
    flat = lambda a: a.reshape(1, -1)
    (small_parts,) = _all_gather("gather_small", [_pack_small([flat(grads[n]) for n in _REPLICATED], loss)])
    small, loss_total = _adamw_small(small_parts, *[[flat(d[n]) for n in _REPLICATED] for d in (wts, mom, var)])
    for n, res in zip(_REPLICATED, small):
        out[n] = [r.reshape(wts[n].shape) for r in res]
    return (loss_total[0, 0], dx[None], *[out[n][0] for n in _WEIGHTS], *[out[n][1] for n in _WEIGHTS],
            *[out[n][2] for n in _WEIGHTS], *[out[n][3] for n in _WEIGHTS])
```

```python
import functools

import jax
import jax.numpy as jnp
from jax import lax
from jax.experimental import pallas as pl
from jax.experimental.pallas import tpu as pltpu

F32 = jnp.float32
BF16 = jnp.bfloat16

N_DEV = 8
D_MODEL = 1024
CONV_DIM = 512
RWKV_DIM = 512
HEAD_DIM = 64
N_HEADS = 8
D_FF = 4096
PLE_DIM = 256
RMS_EPS = 1e-6
GN_EPS = 64e-5
L2_EPS = 1e-12
ADAM_LR, ADAM_B1, ADAM_B2, ADAM_EPS, ADAM_WD, ADAM_STEP = 0.001, 0.9, 0.999, 1e-08, 0.01, 10

CONV_COLS = 3 * CONV_DIM
RW_PAD = 2048
IN_PAD = CONV_COLS + RW_PAD
IN_COLS = 3360
XW_OFF, XA_OFF, XG_OFF = 1536, 1664, 1792
REC_CHUNK = 128
REC_CHUNKS_PER_STEP = 2
REC_PASSES = 1
ROW_BLOCK = 256
LANE = 128
VMEM_LIMIT = 56 * 1024 * 1024


def _dims(dn, ndim):
    if ndim == 3:
        return {"nn": (((2,), (1,)), ((0,), (0,))), "nt": (((2,), (2,)), ((0,), (0,))),
                "tn": (((1,), (1,)), ((0,), (0,)))}[dn]
    return {"nn": (((1,), (0,)), ((), ())), "nt": (((1,), (1,)), ((), ())), "tn": (((0,), (0,)), ((), ()))}[dn]


def _split2(x):
    hi = x.astype(BF16)
    return hi, (x - hi.astype(F32)).astype(BF16)


def _mm_raw(x, y, dn, passes):
    f = lambda p, q: lax.dot_general(p, q, _dims(dn, x.ndim), preferred_element_type=F32)
    if passes == 1:
        return f(x.astype(BF16), y.astype(BF16))
    xh, xl = _split2(x)
    yh, yl = _split2(y)
    if passes == 2:
        return f(xh, yh) + f(xh, yl)
    return f(xh, yh) + f(xh, yl) + f(xl, yh)


@functools.partial(jax.custom_vjp, nondiff_argnums=(2, 3))
def _mm(x, y, dn, passes):
    return _mm_raw(x, y, dn, passes)


def _mm_fwd(x, y, dn, passes):
    return _mm_raw(x, y, dn, passes), (x, y)


def _mm_bwd(dn, passes, res, d):
    x, y = res
    if dn == "nn":
        return _mm(d, y, "nt", passes), _mm(x, d, "tn", passes)
    if dn == "nt":
        return _mm(d, y, "nn", passes), _mm(d, x, "tn", passes)
    return _mm(y, d, "nt", passes), _mm(x, d, "nn", passes)


_mm.defvjp(_mm_fwd, _mm_bwd)


HSUM_COLS = 256


def _head_ones():
    i = lax.broadcasted_iota(jnp.int32, (HSUM_COLS, HSUM_COLS), 0) // HEAD_DIM
    j = lax.broadcasted_iota(jnp.int32, (HSUM_COLS, HSUM_COLS), 1) // HEAD_DIM
    return (i == j).astype(BF16)


def _hsum_raw(x):
    ones = _head_ones()
    f = lambda p: jnp.concatenate(
        [lax.dot_general(p[:, c:c + HSUM_COLS], ones, _dims("nn", 2), preferred_element_type=F32)
         for c in range(0, p.shape[1], HSUM_COLS)], axis=1)
    x1, x2 = _split2(x)
    return f(x1) + f(x2)


@jax.custom_vjp
def _hsum(x):
    return _hsum_raw(x)


_hsum.defvjp(lambda x: (_hsum_raw(x), None), lambda _, d: (_hsum(d),))


def _sigmoid(x):
    return 0.5 + 0.5 * jnp.tanh(0.5 * x)


def _softplus(x):
    return jnp.maximum(x, 0.0) + jnp.log(1.0 + jnp.exp(-jnp.abs(x)))


def _params(sem):
    return pltpu.CompilerParams(dimension_semantics=sem, vmem_limit_bytes=VMEM_LIMIT)


def _rowwise(name, fn, rows, consts, row_outs, acc_outs=(), tr=ROW_BLOCK, halo=False, gather=()):
    rows = [r if isinstance(r, tuple) else (r, r.shape[1], 0) for r in rows]
    t_len = rows[0][0].shape[0]
    tr = min(tr, t_len)
    n_r, n_c, n_o, n_a, n_x = len(rows), len(consts), len(row_outs), len(acc_outs), len(gather)
    n_h = n_r if halo else 0
    sub = 8
    x_specs, x_shapes, x_sems = _gather_io(gather) if n_x else ([], [], [])
    nb = t_len // tr

    def body(*refs):
        if n_x:
            n_in = n_r + n_h + n_c
            start, forward, wait = _gather_plan(refs[n_in:n_in + n_x], refs[len(refs) - 3 - n_x:len(refs) - 3], *refs[len(refs) - 3:])
            pl.when(pl.program_id(0) == 0)(start)
            refs = refs[:n_in] + refs[n_in + n_x:len(refs) - 3 - n_x]
        ins = [r[...] for r in refs[:n_r]]
        ins += [jnp.where(pl.program_id(0) == 0, 0.0, r[sub - 1:sub, :]) for r in refs[n_r:n_r + n_h]]
        ins += [r[...] for r in refs[n_r + n_h:n_r + n_h + n_c]]
        refs = refs[:n_r] + refs[n_r + n_h:]
        outs = fn(*ins)
        o_refs = refs[n_r + n_c:n_r + n_c + n_o]
        a_refs = refs[n_r + n_c + n_o:]
        for o_ref, val in zip(o_refs, outs[:n_o]):
            o_ref[...] = val.astype(o_ref.dtype)
        if n_a:
            first = pl.program_id(0) == 0

            @pl.when(first)
            def _():
                for a_ref, val in zip(a_refs, outs[n_o:]):
                    a_ref[...] = val

            @pl.when(jnp.logical_not(first))
            def _():
                for a_ref, val in zip(a_refs, outs[n_o:]):
                    a_ref[...] += val

        if n_x:
            @pl.when(pl.program_id(0) == nb - 1)
            def _():
                for j in range(n_x):
                    forward(j)
                wait()

    in_specs = [pl.BlockSpec((tr, w), functools.partial(lambda i, c: (i, c), c=cb)) for _, w, cb in rows]
    if halo:
        in_specs += [pl.BlockSpec((sub, w), functools.partial(lambda i, c: (jnp.maximum(i * (tr // sub) - 1, 0), c), c=cb))
                     for _, w, cb in rows]
    in_specs += [pl.BlockSpec(c.shape, functools.partial(lambda i, n: (0,) * n, n=c.ndim)) for c in consts]
    out_specs = [pl.BlockSpec((tr, w), lambda i: (i, 0)) for w, _ in row_outs]
    out_specs += [pl.BlockSpec(s, functools.partial(lambda i, n: (0,) * n, n=len(s))) for s in acc_outs]
    out_shape = [jax.ShapeDtypeStruct((t_len, w), dt) for w, dt in row_outs]
    out_shape += [jax.ShapeDtypeStruct(s, F32) for s in acc_outs]
    return pl.pallas_call(
        body, name=name, grid=(nb,), in_specs=in_specs + x_specs, out_specs=out_specs + x_specs,
        out_shape=out_shape + x_shapes, scratch_shapes=x_sems,
        compiler_params=pltpu.CompilerParams(dimension_semantics=("arbitrary",), vmem_limit_bytes=VMEM_LIMIT,
                                             has_side_effects=bool(n_x)),
    )(*[r[0] for r in rows], *([r[0] for r in rows] if halo else []), *consts, *gather)


def _colwise(name, fn, n_blocks, cols, prms, col_outs, prm_outs=()):
    t_len = cols[0][0].shape[0]
    n_i = len(cols) + len(prms)

    def body(*refs):
        outs = fn(*[r[...] for r in refs[:n_i]])
        for o_ref, val in zip(refs[n_i:], outs):
            o_ref[...] = val.astype(o_ref.dtype)

    spec = lambda r, w: pl.BlockSpec((r, w), lambda j: (0, j))
    in_specs = [spec(t_len, w) for _, w in cols] + [spec(a.shape[0], LANE) for a in prms]
    out_specs = [spec(t_len, bw) for _, _, bw in col_outs] + [spec(r, LANE) for r, _ in prm_outs]
    out_shape = [jax.ShapeDtypeStruct((t_len, w), dt) for w, dt, _ in col_outs]
    out_shape += [jax.ShapeDtypeStruct((r, w), F32) for r, w in prm_outs]
    return pl.pallas_call(
        body, name=name, grid=(n_blocks,), in_specs=in_specs, out_specs=out_specs, out_shape=out_shape,
        compiler_params=_params(("arbitrary",)),
    )(*[c[0] for c in cols], *prms)


def _matmul(name, a, b, dn, outs, *, tm, tn, tk, extras=(), consts=(), epilogue=None, sums=(), a_map=None,
            col_blocks_out=False):
    if dn == "nn":
        (m, k), n = a.shape, b.shape[1]
    elif dn == "nt":
        (m, k), n = a.shape, b.shape[0]
    else:
        (k, m), n = a.shape, b.shape[1]
    tm, tn, tk = min(tm, m), min(tn, n), min(tk, k)
    nk = k // tk
    grid = (m // tm, n // tn, nk)
    assert nk == 1 and (not sums or grid[1] == 1)
    a_spec = pl.BlockSpec((tk, tm), lambda i, j, q: (q, i)) if dn == "tn" else pl.BlockSpec((tm, tk), lambda i, j, q: (i, q))
    b_spec = pl.BlockSpec((tn, tk), lambda i, j, q: (j, q)) if dn == "nt" else pl.BlockSpec((tk, tn), lambda i, j, q: (q, j))
    o_spec = pl.BlockSpec((tm, tn), lambda i, j, q: (i, j))
    c_spec = pl.BlockSpec((1, tn), lambda i, j, q: (0, j))
    n_e, n_c, n_o, n_s = len(extras), len(consts), len(outs), len(sums)

    def body(*refs):
        a_ref, b_ref = refs[:2]
        e_refs = refs[2:2 + n_e + n_c]
        o_refs, s_refs = refs[2 + n_e + n_c:2 + n_e + n_c + n_o], refs[2 + n_e + n_c + n_o:]
        step = pl.program_id(0) * grid[1] + pl.program_id(1)
        a_blk = a_ref[...] if a_map is None else a_map(a_ref[...])
        acc = lax.dot_general(a_blk.astype(BF16), b_ref[...].astype(BF16), _dims(dn, 2), preferred_element_type=F32)
        vals = (acc,) if epilogue is None else epilogue(acc, *[e[...] for e in e_refs])
        for o_ref, val in zip(o_refs, vals[:n_o]):
            o_ref[...] = val.astype(o_ref.dtype)
        if n_s:
            @pl.when(step == 0)
            def _():
                for s_ref, val in zip(s_refs, vals[n_o:]):
                    s_ref[...] = val

            @pl.when(step > 0)
            def _():
                for s_ref, val in zip(s_refs, vals[n_o:]):
                    s_ref[...] += val

    res = pl.pallas_call(
        body, name=name, grid=grid,
        in_specs=[a_spec, b_spec] + [o_spec] * n_e + [c_spec] * n_c,
        out_specs=[pl.BlockSpec((None, tm, tn), lambda i, j, q: (j, i, 0)) if col_blocks_out else o_spec] * n_o
                  + [c_spec] * n_s,
        out_shape=[jax.ShapeDtypeStruct((n // tn, m, tn) if col_blocks_out else (m, n), dt) for dt in outs]
                  + [jax.ShapeDtypeStruct(s, F32) for s in sums],
        compiler_params=_params(("arbitrary",) * 3 if n_s else ("parallel", "parallel", "arbitrary")),
    )(a, b, *extras, *consts)
    return res[0] if len(res) == 1 else res


def _rms(h, g):
    return h * lax.rsqrt(jnp.mean(h * h, axis=-1, keepdims=True) + RMS_EPS) * g


def _rms_bwd(h, g, dy):
    rs = lax.rsqrt(jnp.mean(h * h, axis=-1, keepdims=True) + RMS_EPS)
    n = h * rs
    dn = dy * g
    dh = rs * (dn - n * jnp.mean(dn * n, axis=-1, keepdims=True))
    return dh, jnp.sum(dy * n, axis=0, keepdims=True)


def _rwkv_pre(k, xw, xa, xg, w0, a0, k_k, k_a, wl, al, gl):
    zw = w0 + _mm(jnp.tanh(xw), wl, "nn", 1)
    lw = -jnp.exp(-_softplus(-zw) - 0.5)
    iclr = _sigmoid(a0 + _mm(xa, al, "nn", 1))
    g = _mm(_sigmoid(xg), gl, "nn", 1)
    kk0 = k * k_k
    kk = kk0 * lax.rsqrt(jnp.maximum(_hsum(kk0 * kk0), L2_EPS * L2_EPS))
    k_h = k * (1.0 + (iclr - 1.0) * k_a)
    return lw, k_h, -kk, kk * iclr, g


def _rwkv_pre_vjp(k, xw, xa, xg, lw, w0, a0, k_k, k_a, wl, al, gl, dlw, dk_h, da, db, dg):
    del w0
    s0 = lambda z: jnp.sum(z, axis=0, keepdims=True)
    iclr = _sigmoid(a0 + _mm(xa, al, "nn", 1))
    kk0 = k * k_k
    s = _hsum(kk0 * kk0)
    inv = lax.rsqrt(jnp.maximum(s, L2_EPS * L2_EPS))
    kk = kk0 * inv
    dkk = db * iclr - da
    diclr = db * kk + dk_h * (k * k_a)
    dkk0 = dkk * inv - jnp.where(s > L2_EPS * L2_EPS, inv * inv * inv * _hsum(dkk * kk0), 0.0) * kk0
    dk = dk_h * (1.0 + (iclr - 1.0) * k_a) + dkk0 * k_k
    dza = diclr * iclr * (1.0 - iclr)
    dzw = dlw * lw * (1.0 + lw * 1.6487212707001282)
    th = jnp.tanh(xw)
    sg = _sigmoid(xg)
    dxw = _mm(dzw, wl, "nt", 1) * (1.0 - th * th)
    dxa = _mm(dza, al, "nt", 1)
    dxg = _mm(dg, gl, "nt", 1) * sg * (1.0 - sg)
    return (dk, dxw, dxa, dxg, s0(dzw), s0(dza), s0(dkk0 * k), s0(dk_h * k * (iclr - 1.0)),
            _mm(th, dzw, "tn", 1), _mm(xa, dza, "tn", 1), _mm(sg, dg, "tn", 1))


def _rwkv_post(y, r, k_h, v, g, ln_g, ln_b, r_k):
    mu = _hsum(y) * (1.0 / HEAD_DIM)
    yc = y - mu
    var = _hsum(yc * yc) * (1.0 / HEAD_DIM)
    yo = yc * lax.rsqrt(var + GN_EPS) * ln_g + ln_b
    bonus = _hsum(r * k_h * r_k) * v
    return (yo + bonus) * g


def _shift_down(x, n):
    rows = lax.broadcasted_iota(jnp.int32, x.shape, 0)
    return jnp.where(rows < n, 0.0, pltpu.roll(x, n, 0))


def _shift_up(x, n):
    t_len = x.shape[0]
    rows = lax.broadcasted_iota(jnp.int32, x.shape, 0)
    return jnp.where(rows >= t_len - n, 0.0, pltpu.roll(x, t_len - n, 0))


def _gather_plan(ins, outs, send_sems, recv_sems, local_sems):
    x, y, c = lax.axis_index("x"), lax.axis_index("y"), lax.axis_index("c")
    me = 4 * x + 2 * y + c
    direct, chips = (1, 2, 4, 6), (2, 4, 6)

    def local(i):
        return pltpu.make_async_copy(ins[i], outs[i].at[me], local_sems.at[i])

    def send(i, rel):
        return pltpu.make_async_remote_copy(
            src_ref=ins[i], dst_ref=outs[i].at[me], send_sem=send_sems.at[i, rel - 1], recv_sem=recv_sems.at[i, rel - 1],
            device_id=(x ^ (rel >> 2), y ^ ((rel >> 1) & 1), c ^ (rel & 1)), device_id_type=pl.DeviceIdType.MESH)

    def passed(i, rel):
        slot = outs[i].at[me ^ rel]
        return pltpu.make_async_remote_copy(
            src_ref=slot, dst_ref=slot, send_sem=send_sems.at[i, rel], recv_sem=recv_sems.at[i, rel],
            device_id=(x, y, 1 - c), device_id_type=pl.DeviceIdType.MESH)

    def landed(i, rel):
        slot = outs[i].at[me ^ rel]
        return pltpu.make_async_remote_copy(
            src_ref=slot, dst_ref=slot, send_sem=send_sems.at[i, rel - 1], recv_sem=recv_sems.at[i, rel - 1],
            device_id=(x, y, c), device_id_type=pl.DeviceIdType.MESH)

    def start():
        for i in range(len(ins)):
            local(i).start()
            for rel in direct:
                send(i, rel).start()

    def forward(i):
        for rel in chips:
            landed(i, rel).wait_recv()
            passed(i, rel).start()

    def wait():
        for i in range(len(ins)):
            local(i).wait()
            for rel in (1, 3, 5, 7):
                landed(i, rel).wait_recv()
            for rel in direct:
                send(i, rel).wait_send()
            for rel in chips:
                passed(i, rel).wait_send()

    return start, forward, wait


def _gather_io(arrays):
    n = len(arrays)
    any_spec = pl.BlockSpec(memory_space=pl.ANY)
    out_shape = [jax.ShapeDtypeStruct((N_DEV,) + a.shape, a.dtype) for a in arrays]
    sems = [pltpu.SemaphoreType.DMA((n, N_DEV - 1)), pltpu.SemaphoreType.DMA((n, N_DEV - 1)), pltpu.SemaphoreType.DMA((n,))]
    return [any_spec] * n, out_shape, sems


def _all_gather(name, arrays):
    n = len(arrays)
    specs, out_shape, sems = _gather_io(arrays)

    def body(*refs):
        start, forward, wait = _gather_plan(refs[:n], refs[n:2 * n], *refs[2 * n:])
        start()
        for i in range(n):
            forward(i)
        wait()

    return pl.pallas_call(
        body, name=name, in_specs=specs, out_specs=specs, out_shape=out_shape, scratch_shapes=sems,
        compiler_params=pltpu.CompilerParams(has_side_effects=True),
    )(*arrays)


def _scatter_start(name, arrays, lands):
    n = len(arrays)
    hbm = pl.BlockSpec(memory_space=pltpu.HBM)

    def body(*refs):
        ins, land, send_sems, recv_sems = refs[:n], refs[n:2 * n], refs[2 * n], refs[2 * n + 1]
        token = refs[4 * n + 2]
        x, y, c = lax.axis_index("x"), lax.axis_index("y"), lax.axis_index("c")
        me = 4 * x + 2 * y + c
        for i in range(n):
            for rel in range(1, N_DEV):
                k = i * (N_DEV - 1) + rel - 1
                pltpu.make_async_remote_copy(
                    src_ref=ins[i].at[me ^ rel], dst_ref=land[i].at[me], send_sem=send_sems.at[k],
                    recv_sem=recv_sems.at[k], device_id=(x ^ (rel >> 2), y ^ ((rel >> 1) & 1), c ^ (rel & 1)),
                    device_id_type=pl.DeviceIdType.MESH).start()
        token[...] = jnp.zeros_like(token)

    sem = pltpu.SemaphoreType.DMA((n * (N_DEV - 1),))
    bufs = [pltpu.HBM(a.shape, a.dtype) for a in list(arrays) + list(lands)]
    res = pl.pallas_call(
        body, name=name, out_shape=(sem, sem, *bufs, jax.ShapeDtypeStruct((8, LANE), F32)),
        in_specs=[hbm] * (2 * n),
        out_specs=(pl.BlockSpec(memory_space=pltpu.SEMAPHORE),) * 2 + (hbm,) * (2 * n) + (pl.BlockSpec(memory_space=pltpu.VMEM),),
        input_output_aliases={i: 2 + i for i in range(2 * n)},
        compiler_params=pltpu.CompilerParams(has_side_effects=pltpu.SideEffectType.DATAFLOW_SIDE_EFFECTING),
    )(*[pltpu.with_memory_space_constraint(a, pltpu.HBM) for a in list(arrays) + list(lands)])
    return res[0], res[1], res[2:2 + n], res[2 + n:2 + 2 * n], res[2 + 2 * n]


def _scatter_wait(name, send_sems, recv_sems, arrays, lands, after):
    n, n_after = len(arrays), len(after)
    hbm = pl.BlockSpec(memory_space=pltpu.HBM)

    def body(*refs):
        ins, land, s_sems, r_sems = refs[:n], refs[n:2 * n], refs[2 * n], refs[2 * n + 1]
        x, y, c = lax.axis_index("x"), lax.axis_index("y"), lax.axis_index("c")
        me = 4 * x + 2 * y + c
        for i in range(n):
            for rel in range(1, N_DEV):
                k = i * (N_DEV - 1) + rel - 1
                cp = pltpu.make_async_remote_copy(
                    src_ref=ins[i].at[me ^ rel], dst_ref=land[i].at[me ^ rel], send_sem=s_sems.at[k],
                    recv_sem=r_sems.at[k], device_id=(x, y, c), device_id_type=pl.DeviceIdType.MESH)
                cp.wait_send()
                cp.wait_recv()

    res = pl.pallas_call(
        body, name=name, out_shape=[pltpu.HBM(a.shape, a.dtype) for a in list(arrays) + list(lands)],
        in_specs=[hbm] * (2 * n) + [pl.BlockSpec(memory_space=pltpu.SEMAPHORE)] * 2 + [pl.BlockSpec(memory_space=pl.ANY)] * n_after,
        out_specs=[hbm] * (2 * n), input_output_aliases={i: i for i in range(2 * n)},
        compiler_params=pltpu.CompilerParams(has_side_effects=pltpu.SideEffectType.DATAFLOW_SIDE_EFFECTING),
    )(*arrays, *lands, send_sems, recv_sems, *after)
    return res[:n], res[n:]


def _gather_start(name, shards, after):
    n, n_after = len(shards), len(after)
    hbm = pl.BlockSpec(memory_space=pltpu.HBM)
    lands = [lax.empty((N_DEV,) + a.shape, a.dtype) for a in shards]

    def body(*refs):
        ins, land = refs[:n], refs[n:2 * n]
        outs = refs[2 * n + n_after:]
        send_sems, recv_sems, token = outs[:n], outs[n:2 * n], outs[4 * n]
        x, y, c = lax.axis_index("x"), lax.axis_index("y"), lax.axis_index("c")
        me = 4 * x + 2 * y + c
        for i in range(n):
            for rel in range(N_DEV):
                pltpu.make_async_remote_copy(
                    src_ref=ins[i], dst_ref=land[i].at[me], send_sem=send_sems[i].at[rel],
                    recv_sem=recv_sems[i].at[rel], device_id=(x ^ (rel >> 2), y ^ ((rel >> 1) & 1), c ^ (rel & 1)),
                    device_id_type=pl.DeviceIdType.MESH).start()
        token[...] = jnp.zeros_like(token)

    sem = pltpu.SemaphoreType.DMA((N_DEV,))
    bufs = [pltpu.HBM(a.shape, a.dtype) for a in list(shards) + lands]
    res = pl.pallas_call(
        body, name=name, out_shape=(*[sem] * (2 * n), *bufs, jax.ShapeDtypeStruct((8, LANE), F32)),
        in_specs=[hbm] * (2 * n) + [pl.BlockSpec(memory_space=pl.ANY)] * n_after,
        out_specs=(pl.BlockSpec(memory_space=pltpu.SEMAPHORE),) * (2 * n) + (hbm,) * (2 * n)
                  + (pl.BlockSpec(memory_space=pltpu.VMEM),),
        input_output_aliases={i: 2 * n + i for i in range(2 * n)},
        compiler_params=pltpu.CompilerParams(has_side_effects=pltpu.SideEffectType.DATAFLOW_SIDE_EFFECTING),
    )(*[pltpu.with_memory_space_constraint(a, pltpu.HBM) for a in list(shards) + lands], *after)
    return res[:n], res[n:2 * n], res[2 * n:3 * n], res[3 * n:4 * n], res[4 * n]


def _gather_wait(name, send_sems, recv_sems, shards, lands, after):
    n, n_after = len(shards), len(after)
    hbm = pl.BlockSpec(memory_space=pltpu.HBM)

    def body(*refs):
        ins, land = refs[:n], refs[n:2 * n]
        s_sems, r_sems = refs[2 * n:3 * n], refs[3 * n:4 * n]
        x, y, c = lax.axis_index("x"), lax.axis_index("y"), lax.axis_index("c")
        me = 4 * x + 2 * y + c
        for i in range(n):
            for rel in range(N_DEV):
                cp = pltpu.make_async_remote_copy(
                    src_ref=ins[i], dst_ref=land[i].at[me ^ rel], send_sem=s_sems[i].at[rel],
                    recv_sem=r_sems[i].at[rel], device_id=(x, y, c), device_id_type=pl.DeviceIdType.MESH)
                cp.wait_send()
                cp.wait_recv()

    res = pl.pallas_call(
        body, name=name, out_shape=[pltpu.HBM(a.shape, a.dtype) for a in list(shards) + list(lands)],
        in_specs=[hbm] * (2 * n) + [pl.BlockSpec(memory_space=pltpu.SEMAPHORE)] * (2 * n)
                 + [pl.BlockSpec(memory_space=pl.ANY)] * n_after,
        out_specs=[hbm] * (2 * n), input_output_aliases={i: i for i in range(2 * n)},
        compiler_params=pltpu.CompilerParams(has_side_effects=pltpu.SideEffectType.DATAFLOW_SIDE_EFFECTING),
    )(*shards, *lands, *send_sems, *recv_sems, *after)
    return res[n:]


def _tri_powers(low):
    powers, n, p = [low.astype(BF16)], 1, low
    while 2 * n < low.shape[-1]:
        p = _mm(p, p, "nn", REC_PASSES)
        powers.append(p.astype(BF16))
        n *= 2
    return powers


@jax.custom_vjp
def _tri_solve(low, rhs, powers):
    del low
    for p in powers:
        rhs = rhs + _mm(p, rhs, "nn", REC_PASSES)
    return rhs


def _tri_solve_fwd(low, rhs, powers):
    out = _tri_solve(low, rhs, powers)
    return out, (powers, out)


def _tri_solve_bwd(res, d):
    powers, u = res
    for p in powers:
        d = d + _mm(p, d, "tn", REC_PASSES)
    return _mm(d, u, "nt", REC_PASSES), d, [jnp.zeros_like(p) for p in powers]


_tri_solve.defvjp(_tri_solve_fwd, _tri_solve_bwd)


@jax.custom_vjp
def _tri_solve_given(low, rhs, powers, value):
    del low, rhs, powers
    return value


def _tri_solve_given_fwd(low, rhs, powers, value):
    return value, (powers, value)


def _tri_solve_given_bwd(res, d):
    return _tri_solve_bwd(res, d) + (jnp.zeros_like(res[1]),)


_tri_solve_given.defvjp(_tri_solve_given_fwd, _tri_solve_given_bwd)


def _heads(x):
    return jnp.stack([x[:, h * HEAD_DIM:(h + 1) * HEAD_DIM] for h in range(N_HEADS)])


def _unheads(x):
    return jnp.concatenate([x[h] for h in range(N_HEADS)], axis=-1)


def _causal_masks(c):
    ti = lax.broadcasted_iota(jnp.int32, (c, c), 0)
    si = lax.broadcasted_iota(jnp.int32, (c, c), 1)
    strict, incl = si < ti, si <= ti
    both = jnp.concatenate([jnp.concatenate([strict, strict], axis=1), jnp.concatenate([incl, incl], axis=1)], axis=0)
    return strict, incl, both


@jax.custom_vjp
def _gram_given(x2, y2, value):
    del x2, y2
    return value.astype(F32)


def _gram_given_fwd(x2, y2, value):
    return value.astype(F32), (x2, y2, value)


def _gram_given_bwd(res, d):
    x2, y2, value = res
    d = jnp.where(_causal_masks(d.shape[-1] // 2)[2], d, 0.0)
    return _mm(d, y2, "nn", 2), _mm(d, x2, "tn", 2), jnp.zeros_like(value)


_gram_given.defvjp(_gram_given_fwd, _gram_given_bwd)


def _chunk_fwd(z0, r, lw, k, v, a, b, powers=None, gram_value=None, u_value=None):
    c = r.shape[0]
    n_h, n_k = z0.shape[0], z0.shape[1]
    mm = functools.partial(_mm, passes=REC_PASSES)
    gram = functools.partial(_mm, passes=2)
    _, incl, mask = _causal_masks(c)
    cum = _mm(incl.astype(F32), lw, "nn", 3)
    cum_end = cum[c - 1:c, :]
    e_neg, e_end = jnp.exp(-cum), jnp.exp(cum_end - cum)
    x2 = jnp.concatenate([_heads(a * jnp.exp(cum - lw)), _heads(r * jnp.exp(cum))], axis=1)
    y2 = jnp.concatenate([_heads(b * e_neg), _heads(k * e_neg)], axis=1)
    vh = _heads(v)
    g2 = jnp.where(mask, gram(x2, y2, "nt"), 0.0) if gram_value is None else _gram_given(x2, y2, gram_value)
    t2 = mm(x2, z0, "nn") + mm(g2[:, :, c:], vh, "nn")
    low = g2[:, :c, :c]
    powers = _tri_powers(low) if powers is None else powers
    u = _tri_solve(low, t2[:, :c], powers) if u_value is None else _tri_solve_given(low, t2[:, :c], powers, u_value)
    y = t2[:, c:] + mm(g2[:, c:, :c], u, "nn")
    ki = lax.broadcasted_iota(jnp.int32, (n_k, n_k), 0)
    kj = lax.broadcasted_iota(jnp.int32, (n_k, n_k), 1)
    dmat = jnp.where(ki == kj, jnp.broadcast_to(_heads(jnp.exp(cum_end)), (n_h, n_k, n_k)), 0.0)
    z_end = mm(dmat, z0, "nn") + mm(jnp.concatenate([_heads(b * e_end), _heads(k * e_end)], axis=1),
                                    jnp.concatenate([u, vh], axis=1), "tn")
    return _unheads(y), z_end, powers, g2, u


def _rec_fwd(u, lw, k, a, b):
    t_len = lw.shape[0]
    c = min(REC_CHUNK, t_len)
    nc = t_len // c
    per = REC_CHUNKS_PER_STEP if nc % REC_CHUNKS_PER_STEP == 0 else 1
    steps = nc // per
    n_pow = max(1, (c - 1).bit_length())

    def body(r_ref, v_ref, lw_ref, k_ref, a_ref, b_ref, y_ref, zs_ref, pw_ref, gs_ref, us_ref, z_scr):
        @pl.when(pl.program_id(0) == 0)
        def _():
            z_scr[...] = jnp.zeros_like(z_scr)

        for s in range(per):
            rows = pl.ds(s * c, c)
            z0 = z_scr[...]
            zs_ref[s] = z0
            y, z_end, powers, g2, u_rows = _chunk_fwd(z0, r_ref[rows, :], lw_ref[rows, :], k_ref[rows, :], v_ref[rows, :],
                                                      a_ref[rows, :], b_ref[rows, :])
            y_ref[rows, :] = y
            z_scr[...] = z_end
            pw_ref[s] = jnp.concatenate(powers, axis=0)
            gs_ref[s] = g2.astype(BF16)
            us_ref[s] = u_rows

    blk = lambda cb: pl.BlockSpec((per * c, RWKV_DIM), functools.partial(lambda i, q: (i, q), q=cb))
    res = pl.pallas_call(
        body, name="rwkv_rec_fwd", grid=(steps,),
        in_specs=[blk(0), blk(2)] + [blk(0)] * 4,
        out_specs=[blk(0), pl.BlockSpec((per, N_HEADS, HEAD_DIM, HEAD_DIM), lambda i: (i, 0, 0, 0)),
                   pl.BlockSpec((per, n_pow * N_HEADS, c, c), lambda i: (i, 0, 0, 0)),
                   pl.BlockSpec((per, N_HEADS, 2 * c, 2 * c), lambda i: (i, 0, 0, 0)),
                   pl.BlockSpec((per, N_HEADS, c, HEAD_DIM), lambda i: (i, 0, 0, 0))],
        out_shape=[jax.ShapeDtypeStruct((t_len, RWKV_DIM), F32),
                   jax.ShapeDtypeStruct((nc, N_HEADS, HEAD_DIM, HEAD_DIM), F32),
                   jax.ShapeDtypeStruct((nc, n_pow * N_HEADS, c, c), BF16),
                   jax.ShapeDtypeStruct((nc, N_HEADS, 2 * c, 2 * c), BF16),
                   jax.ShapeDtypeStruct((nc, N_HEADS, c, HEAD_DIM), F32)],
        scratch_shapes=[pltpu.VMEM((N_HEADS, HEAD_DIM, HEAD_DIM), F32)], compiler_params=_params(("arbitrary",)),
    )(u, u, lw, k, a, b)
    return res[0], res[1:]


def _rec_bwd(u, lw, k, a, b, saved, dy):
    t_len = lw.shape[0]
    c = min(REC_CHUNK, t_len)
    nc = t_len // c
    per = REC_CHUNKS_PER_STEP if nc % REC_CHUNKS_PER_STEP == 0 else 1
    steps = nc // per

    zs, pw, gs, us = saved

    def body(r_ref, v_ref, lw_ref, k_ref, a_ref, b_ref, zs_ref, pw_ref, gs_ref, us_ref, dy_ref, *rest):
        g_refs, dz_scr = rest[:6], rest[6]

        @pl.when(pl.program_id(0) == 0)
        def _():
            dz_scr[...] = jnp.zeros_like(dz_scr)

        for s in reversed(range(per)):
            rows = pl.ds(s * c, c)
            powers = [pw_ref[s, j * N_HEADS:(j + 1) * N_HEADS] for j in range(pw.shape[1] // N_HEADS)]
            chunk = functools.partial(lambda gram, u_val, pws, *xs: _chunk_fwd(*xs, powers=pws, gram_value=gram, u_value=u_val)[:2],
                                      gs_ref[s], us_ref[s], powers)
            _, vjp = jax.vjp(chunk, zs_ref[s], r_ref[rows, :], lw_ref[rows, :], k_ref[rows, :], v_ref[rows, :],
                             a_ref[rows, :], b_ref[rows, :])
            dz0, dr, dlw, dk, dv, da, db = vjp((dy_ref[rows, :], dz_scr[...]))
            for ref, val in zip(g_refs, (dr, dv, dlw, dk, da, db)):
                ref[rows, :] = val
            dz_scr[...] = dz0

    blk = lambda cb: pl.BlockSpec((per * c, RWKV_DIM), functools.partial(lambda i, q: (steps - 1 - i, q), q=cb))
    saved_blk = lambda arr: pl.BlockSpec((per,) + arr.shape[1:], lambda i: (steps - 1 - i, 0, 0, 0))
    return pl.pallas_call(
        body, name="rwkv_rec_bwd", grid=(steps,),
        in_specs=[blk(0), blk(2)] + [blk(0)] * 4 + [saved_blk(zs), saved_blk(pw), saved_blk(gs), saved_blk(us), blk(0)],
        out_specs=[blk(0)] * 6, out_shape=[jax.ShapeDtypeStruct((t_len, RWKV_DIM), F32)] * 6,
        scratch_shapes=[pltpu.VMEM((N_HEADS, HEAD_DIM, HEAD_DIM), F32)], compiler_params=_params(("arbitrary",)),
    )(u, u, lw, k, a, b, zs, pw, gs, us, dy)


_EARLY = ["w_in", "conv_w", "w_lora_up", "a_lora_up", "g_lora_up"]
_LATE = ["w_out", "w_up", "w_down", "w_ple_gate", "w_ple_proj"]
_SHARDED = _EARLY + _LATE
_COL_SHARDED = {"w_in", "conv_w", "w_lora_up", "a_lora_up", "g_lora_up", "w_up", "w_ple_proj"}
_BF16_GATHER = {"w_in", "w_out", "w_up", "w_down", "w_ple_gate", "w_ple_proj"}
_REPLICATED = ["norm_mix_g", "shift_mu", "w0", "a0", "k_k", "k_a", "r_k", "ln_x_g", "ln_x_b", "norm_mlp_g", "norm_ple_g",
               "norm_final_g"]
_WEIGHTS = ["norm_mix_g", "w_in", "conv_w", "shift_mu", "w_lora_up", "w0", "a_lora_up", "a0", "g_lora_up", "k_k", "k_a", "r_k",
            "ln_x_g", "ln_x_b", "w_out", "norm_mlp_g", "w_up", "w_down", "norm_ple_g", "w_ple_gate", "w_ple_proj", "norm_final_g"]


def _unshard(name, g):
    if name in _COL_SHARDED:
        return jnp.moveaxis(g, 0, 1).reshape(g.shape[1], N_DEV * g.shape[2])
    return g.reshape(N_DEV * g.shape[1], g.shape[2])


def _reshard(name, full):
    if name in _COL_SHARDED:
        return jnp.moveaxis(full.reshape(full.shape[0], N_DEV, full.shape[1] // N_DEV), 1, 0)
    return full.reshape(N_DEV, full.shape[0] // N_DEV, full.shape[1])


def _pad_in_cols(a):
    z = lambda n: jnp.zeros(a.shape[:-1] + (n,), a.dtype)
    conv = [a[..., part * CONV_DIM + j * LANE:part * CONV_DIM + (j + 1) * LANE] for j in range(CONV_DIM // LANE) for part in range(3)]
    return jnp.concatenate(conv + [a[..., CONV_COLS:3136], z(64), a[..., 3136:3200], z(64), a[..., 3200:3360], z(96)], axis=-1)


def _unpad_in_cols(a):
    conv = [a[..., (3 * j + part) * LANE:(3 * j + part + 1) * LANE] for part in range(3) for j in range(CONV_DIM // LANE)]
    return jnp.concatenate(conv + [a[..., CONV_COLS:3136], a[..., 3200:3264], a[..., 3328:3488]], axis=-1)


def _assemble_w_in(g):
    n_dev, rows, cols = g.shape

    def body(g_ref, o_ref):
        o_ref[...] = _pad_in_cols(jnp.concatenate([g_ref[d] for d in range(n_dev)], axis=1))

    return pl.pallas_call(
        body, name="w_in_assemble", grid=(rows // ROW_BLOCK,),
        in_specs=[pl.BlockSpec((n_dev, ROW_BLOCK, cols), lambda i: (0, i, 0))],
        out_specs=pl.BlockSpec((ROW_BLOCK, IN_PAD), lambda i: (i, 0)),
        out_shape=jax.ShapeDtypeStruct((rows, IN_PAD), g.dtype), compiler_params=_params(("arbitrary",)),
    )(g)


def _split_w_in_grad(dw):
    rows = dw.shape[0]
    cols = IN_COLS // N_DEV

    def body(d_ref, o_ref):
        full = _unpad_in_cols(d_ref[...])
        for d in range(N_DEV):
            o_ref[d] = full[:, cols * d:cols * (d + 1)]

    return pl.pallas_call(
        body, name="w_in_grad_split", grid=(rows // ROW_BLOCK,),
        in_specs=[pl.BlockSpec((ROW_BLOCK, IN_PAD), lambda i: (i, 0))],
        out_specs=pl.BlockSpec((N_DEV, ROW_BLOCK, cols), lambda i: (0, i, 0)),
        out_shape=jax.ShapeDtypeStruct((N_DEV, rows, cols), dw.dtype), compiler_params=_params(("arbitrary",)),
    )(dw)


def _pad_rows(a, rows):
    return jnp.concatenate([a, jnp.zeros((rows - a.shape[0],) + a.shape[1:], a.dtype)], axis=0)


SEG_W = [RWKV_DIM, RWKV_DIM, RWKV_DIM, LANE, LANE, 2 * LANE]
SEG_OFF = [0, 512, 1024, XW_OFF, XA_OFF, XG_OFF]


def _rwkv_pre_bwd(proj, u, grads, mu, small, dproj):
    t_len = u.shape[0]
    tr = min(ROW_BLOCK, t_len)
    nb = t_len // tr
    sub = 8
    n_g = len(grads)
    acc_shapes = [(1, RW_PAD)] + [(1, RWKV_DIM)] * 4 + [(LANE, RWKV_DIM), (LANE, RWKV_DIM), (2 * LANE, RWKV_DIM)]

    def body(*refs):
        seg_refs, halo_refs = refs[:6], refs[6:12]
        k_ref, xw_ref, xa_ref, xg_ref = refs[12:16]
        g_refs = refs[16:16 + n_g]
        mu_ref = refs[16 + n_g]
        prm_refs = refs[17 + n_g:24 + n_g]
        out_hbm = refs[25 + n_g]
        acc_refs = refs[26 + n_g:26 + n_g + len(acc_shapes)]
        vbuf, sems, carry = refs[26 + n_g + len(acc_shapes):]
        i = pl.program_id(0)
        blk = nb - 1 - i
        dr1, dr2, dv1, dv2, dlw, dk1, dk2, da, db, dg, lw_rows = [g[...] for g in g_refs]
        dk, dxw, dxa, dxg, *dprm = _rwkv_pre_vjp(k_ref[...], xw_ref[...], xa_ref[...], xg_ref[...], lw_rows,
                                                  *[p_[...] for p_ in prm_refs], dlw, dk1 + dk2, da, db, dg)
        du = jnp.concatenate([dr1 + dr2, dk, dv1 + dv2, dxw, dxa, dxg], axis=1)
        mu_v = mu_ref[...]

        @pl.when(i == 0)
        def _():
            carry[...] = jnp.zeros_like(carry)

        rows = lax.broadcasted_iota(jnp.int32, du.shape, 0)
        nxt = jnp.where(rows == tr - 1, carry[...], pltpu.roll(du, tr - 1, 0))
        d_rw = du - mu_v * du + mu_v * nxt
        d_mu = []
        for s_ref, h_ref, off, wd in zip(seg_refs, halo_refs, SEG_OFF, SEG_W):
            cur = s_ref[...]
            r0 = lax.broadcasted_iota(jnp.int32, cur.shape, 0)
            prev = jnp.where(r0 == 0, jnp.where(blk == 0, 0.0, h_ref[sub - 1:sub, :]), pltpu.roll(cur, 1, 0))
            d_mu.append(jnp.sum(du[:, off:off + wd] * (prev - cur), axis=0, keepdims=True))
        sums = [jnp.concatenate(d_mu, axis=1)] + list(dprm)

        @pl.when(i == 0)
        def _():
            for a_ref, val in zip(acc_refs, sums):
                a_ref[...] = val

        @pl.when(i > 0)
        def _():
            for a_ref, val in zip(acc_refs, sums):
                a_ref[...] += val

        carry[...] = du[0:1, :]
        slot = i % 2

        def writeback(s, b):
            return pltpu.make_async_copy(vbuf.at[s], out_hbm.at[pl.ds(b * tr, tr), pl.ds(CONV_COLS, RW_PAD)], sems.at[s])

        @pl.when(i >= 2)
        def _():
            writeback(slot, blk + 2).wait()

        vbuf[slot] = d_rw.astype(vbuf.dtype)
        writeback(slot, blk).start()

        @pl.when(i == nb - 1)
        def _():
            writeback(slot, blk).wait()
            if nb > 1:
                writeback(1 - slot, blk + 1).wait()

    rev = lambda w_, cb: pl.BlockSpec((tr, w_), functools.partial(lambda i, c: (nb - 1 - i, c), c=cb))
    halo = lambda w_, cb: pl.BlockSpec((sub, w_), functools.partial(
        lambda i, c: (jnp.maximum((nb - 1 - i) * (tr // sub) - 1, 0), c), c=cb))
    whole = lambda a: pl.BlockSpec(a.shape, functools.partial(lambda i, n: (0,) * n, n=a.ndim))
    segs = [(wd, (CONV_COLS + off) // wd) for off, wd in zip(SEG_OFF, SEG_W)]
    u_cols = [(512, 1), (LANE, XW_OFF // LANE), (LANE, XA_OFF // LANE), (2 * LANE, XG_OFF // (2 * LANE))]
    any_spec = pl.BlockSpec(memory_space=pl.ANY)
    res = pl.pallas_call(
        body, name="rwkv_pre_bwd", grid=(nb,),
        in_specs=[rev(*s) for s in segs] + [halo(*s) for s in segs] + [rev(*c) for c in u_cols]
                 + [rev(RWKV_DIM, 0)] * n_g + [whole(mu)] + [whole(p_) for p_ in small] + [any_spec],
        out_specs=[any_spec] + [pl.BlockSpec(s, functools.partial(lambda i, n: (0,) * n, n=len(s))) for s in acc_shapes],
        out_shape=[jax.ShapeDtypeStruct(dproj.shape, dproj.dtype)] + [jax.ShapeDtypeStruct(s, F32) for s in acc_shapes],
        scratch_shapes=[pltpu.VMEM((2, tr, RW_PAD), dproj.dtype), pltpu.SemaphoreType.DMA((2,)), pltpu.VMEM((1, RW_PAD), F32)],
        input_output_aliases={24 + n_g: 0},
        compiler_params=_params(("arbitrary",)),
    )(*[proj] * 12, *[u] * 4, *grads, mu, *small, dproj)
    return res


def _local_step(x, p, tgt, w, early_shards, late_shards):
    row = lambda v: v.reshape(1, -1)
    w = dict(w)

    xn1, *gathered = _rowwise("rms_mix", lambda h, g: (_rms(h, g),), [x], [w["norm_mix_g"]], [(D_MODEL, BF16)],
                              gather=early_shards)
    w.update({n: _unshard(n, g_) for n, g_ in zip(_EARLY[1:], gathered[1:])})
    w["w_in"] = _assemble_w_in(gathered[0])
    w["w_lora_up"] = _pad_rows(w["w_lora_up"], LANE)
    w["a_lora_up"] = _pad_rows(w["a_lora_up"], LANE)
    w["g_lora_up"] = _pad_rows(w["g_lora_up"], 2 * LANE)
    lg_send, lg_recv, lg_shards, lg_lands, lg_token = _gather_start("late_gather_start", late_shards, after=[xn1])
    w["shift_mu"] = w["shift_mu"] + lg_token[0:1, 0:1]
    proj = _matmul("in_proj", xn1, w["w_in"], "nn", [F32], tm=2048, tn=512, tk=D_MODEL)
    n_cb = CONV_DIM // LANE

    def conv_fwd(blk, cw):
        gb, gc, hx = blk[:, :LANE], blk[:, LANE:2 * LANE], blk[:, 2 * LANE:]
        uu = gc * hx
        return (gb * (uu * cw[2:3] + _shift_down(uu, 1) * cw[1:2] + _shift_down(uu, 2) * cw[0:1]),)

    (y_conv,) = _colwise("conv_fwd", conv_fwd, n_cb, [(proj, 3 * LANE)], [w["conv_w"]], [(CONV_DIM, BF16, LANE)])

    small = [w["w0"], w["a0"], w["k_k"], w["k_a"], w["w_lora_up"], w["a_lora_up"], w["g_lora_up"]]
    def pre_fwd(*xs):
        cur, prev_rows, mu, prm = xs[:6], xs[6:12], xs[12], xs[13:]
        segs = []
        for c_, p_, off, wd in zip(cur, prev_rows, SEG_OFF, SEG_W):
            rows = lax.broadcasted_iota(jnp.int32, c_.shape, 0)
            prev = jnp.where(rows == 0, p_, pltpu.roll(c_, 1, 0))
            segs.append(c_ + mu[:, off:off + wd] * (prev - c_))
        return (jnp.concatenate(segs, axis=1),) + tuple(_rwkv_pre(segs[1], segs[3], segs[4], segs[5], *prm))

    proj_segs = [(proj, wd, (CONV_COLS + off) // wd) for off, wd in zip(SEG_OFF, SEG_W)]
    u, lw, k_h, ra, rb, g = _rowwise(
        "rwkv_pre", pre_fwd, proj_segs, [w["shift_mu"]] + small, [(RW_PAD, F32)] + [(RWKV_DIM, F32)] * 5, halo=True)
    y_rec, rec_saved = _rec_fwd(u, lw, k_h, ra, rb)

    def late_weight(names, after):
        idx = [_LATE.index(n) for n in names]
        got = _gather_wait("late_gather_wait_" + names[0], [lg_send[i] for i in idx], [lg_recv[i] for i in idx],
                           [lg_shards[i] for i in idx], [lg_lands[i] for i in idx], after)
        return [_unshard(n, g_) for n, g_ in zip(names, got)]

    w["w_out"], w["w_up"], w["w_down"] = late_weight(["w_out", "w_up", "w_down"], [y_rec])
    post_c = [w["ln_x_g"], w["ln_x_b"], w["r_k"]]
    u_r, u_v = (u, 512, 0), (u, 512, 2)
    (y_rwkv,) = _rowwise("rwkv_post", lambda *xs: (_rwkv_post(*xs),), [y_rec, u_r, k_h, u_v, g], post_c, [(RWKV_DIM, BF16)],
                         tr=2 * ROW_BLOCK)
    ycat = jnp.concatenate([y_conv, y_rwkv], axis=1)
    def res_norm(acc, r_, g_):
        h = acc + r_
        return h, _rms(h, g_)

    h1, xn2 = _matmul("out_proj", ycat, w["w_out"], "nn", [F32, BF16], tm=1024, tn=D_MODEL, tk=D_MODEL, extras=[x],
                      consts=[w["norm_mlp_g"]], epilogue=res_norm)

    square = lambda h: h.astype(F32) * h.astype(F32)
    hid = _matmul("mlp_up", xn2, w["w_up"], "nn", [BF16], tm=2048, tn=1024, tk=D_MODEL,
                  epilogue=lambda acc: (jnp.maximum(acc, 0.0),))
    h2, xn3 = _matmul("mlp_down", hid, w["w_down"], "nn", [F32, BF16], tm=512, tn=D_MODEL, tk=D_FF, extras=[h1],
                      consts=[w["norm_ple_g"]], epilogue=res_norm, a_map=square)
    w["w_ple_gate"], w["w_ple_proj"] = late_weight(["w_ple_gate", "w_ple_proj"], [xn3])
    zg =_matmul("ple_gate", xn3, w["w_ple_gate"], "nn", [F32], tm=1024, tn=1024, tk=D_MODEL)
    pp = _matmul("ple_proj", p, w["w_ple_proj"], "nn", [F32], tm=1024, tn=1024, tk=PLE_DIM)

    def head(h2_, zg_, pp_, tg, gf):
        gate = _sigmoid(zg_)
        h3 = h2_ + gate * pp_
        out = _rms(h3, gf)
        err = out - tg
        dh3, dgf = _rms_bwd(h3, gf, err * (1.0 / D_MODEL))
        loss = jnp.sum(jnp.sum(err * err, axis=1, keepdims=True), axis=0, keepdims=True) * (0.5 / D_MODEL)
        return dh3, dh3 * pp_ * gate * (1.0 - gate), dh3 * gate, dgf, loss

    dh3, dzg, dpp, d_norm_final, loss = _rowwise(
        "head", head, [h2, zg, pp, tgt], [row(w["norm_final_g"])], [(D_MODEL, F32), (D_MODEL, BF16), (D_MODEL, BF16)],
        [(1, D_MODEL), (1, 1)], tr=2 * ROW_BLOCK)

    d_w_ple_proj = _matmul("d_ple_proj", p, dpp, "tn", [BF16], tm=PLE_DIM, tn=D_MODEL // N_DEV, tk=4096, col_blocks_out=True)
    d_w_ple_gate = _matmul("d_ple_gate", xn3, dzg, "tn", [BF16], tm=512, tn=1024, tk=4096)

    def norm_bwd(dxn, h, dres, g_):
        dh, dg = _rms_bwd(h, g_, dxn)
        dh = dh + dres
        return dh, dh, dg

    nb = dict(tm=512, tn=D_MODEL, epilogue=norm_bwd, sums=[(1, D_MODEL)])
    dh2, dh2_b, d_norm_ple = _matmul("dx_ple_gate", dzg, w["w_ple_gate"], "nt", [F32, BF16], tk=D_MODEL,
                                     extras=[h2, dh3], consts=[w["norm_ple_g"]], **nb)
    d_w_down = _matmul("d_mlp_down", hid, dh2_b, "tn", [BF16], tm=512, tn=1024, tk=4096, a_map=square)
    dpre = _matmul("dx_mlp_down", dh2_b, w["w_down"], "nt", [BF16], tm=2048, tn=1024, tk=D_MODEL, extras=[hid],
                   epilogue=lambda acc, hid_: (acc * (2.0 * hid_.astype(F32)),))
    d_w_up = _matmul("d_mlp_up", xn2, dpre, "tn", [BF16], tm=1024, tn=D_FF // N_DEV, tk=4096, col_blocks_out=True)
    dh1, dh1_b, d_norm_mlp = _matmul("dx_mlp_up", dpre, w["w_up"], "nt", [F32, BF16], tk=D_FF,
                                     extras=[h1, dh2], consts=[w["norm_mlp_g"]], **nb)
    d_w_out = _matmul("d_out_proj", ycat, dh1_b, "tn", [BF16], tm=512, tn=1024, tk=4096)
    dycat = _matmul("dx_out_proj", dh1_b, w["w_out"], "nt", [F32], tm=1024, tn=1024, tk=D_MODEL)
    late_grads = dict(w_out=d_w_out, w_up=d_w_up, w_down=d_w_down, w_ple_gate=d_w_ple_gate, w_ple_proj=d_w_ple_proj)
    late_send = [late_grads[n] if n in ("w_up", "w_ple_proj") else _reshard(n, late_grads[n]) for n in _LATE]
    *late_flight, late_token = _scatter_start("late_scatter_start", late_send, [lax.empty(a.shape, a.dtype) for a in late_send])
    conv_w_bwd = w["conv_w"] + late_token[0:1, 0:1]

    def conv_bwd(dy, blk, cw):
        gb, gc, hx = blk[:, :LANE], blk[:, LANE:2 * LANE], blk[:, 2 * LANE:]
        uu = gc * hx
        u1, u2 = _shift_down(uu, 1), _shift_down(uu, 2)
        dconv = dy * gb
        du = dconv * cw[2:3] + _shift_up(dconv, 1) * cw[1:2] + _shift_up(dconv, 2) * cw[0:1]
        s = lambda z: jnp.sum(z, axis=0, keepdims=True)
        d_blk = jnp.concatenate([dy * (uu * cw[2:3] + u1 * cw[1:2] + u2 * cw[0:1]), du * hx, du * gc], axis=1)
        return d_blk, s(dconv * u2), s(dconv * u1), s(dconv * uu)

    dproj, dcw0, dcw1, dcw2 = _colwise(
        "conv_bwd", conv_bwd, n_cb, [(dycat, LANE), (proj, 3 * LANE)], [conv_w_bwd],
        [(IN_PAD, BF16, 3 * LANE)], [(1, CONV_DIM)] * 3)

    def post_bwd(dy, y, r, k_h_, v, g_, ln_g, ln_b, r_k):
        _, vjp = jax.vjp(_rwkv_post, y, r, k_h_, v, g_, ln_g, ln_b, r_k)
        return vjp(dy)

    dy_rec, dr_p, dk_p, dv_p, dg, d_ln_g, d_ln_b, d_r_k = _rowwise(
        "rwkv_post_bwd", post_bwd, [(dycat, 512, 1), y_rec, u_r, k_h, u_v, g], post_c,
        [(RWKV_DIM, F32)] * 5, [(1, RWKV_DIM)] * 3)
    dr_r, dv_r, dlw, dk_r, da, db = _rec_bwd(u, lw, k_h, ra, rb, rec_saved, dy_rec)

    dproj, d_mu, d_w0, d_a0, d_k_k, d_k_a, d_wl, d_al, d_gl = _rwkv_pre_bwd(
        proj, u, [dr_p, dr_r, dv_p, dv_r, dlw, dk_p, dk_r, da, db, dg, lw], w["shift_mu"], small, dproj)
    d_w_in = _matmul("d_in_proj", xn1, dproj, "tn", [BF16], tm=1024, tn=896, tk=4096)
    early_grads = dict(conv_w=jnp.concatenate([dcw0, dcw1, dcw2], axis=0),
                       w_lora_up=d_wl[:64], a_lora_up=d_al[:64], g_lora_up=d_gl[:160])
    early_send = [_split_w_in_grad(d_w_in)] + [_reshard(n, early_grads[n]) for n in _EARLY[1:]]
    *early_flight, token = _scatter_start("early_scatter_start", early_send, [lax.empty(a.shape, a.dtype) for a in early_send])
    dx, d_norm_mix = _matmul(
        "dx_in_proj", dproj, w["w_in"], "nt", [F32], tk=IN_PAD, extras=[x, dh1], consts=[w["norm_mix_g"] + token[0:1, 0:1]],
        **dict(nb, epilogue=lambda *a: norm_bwd(*a)[1:]))

    grads = dict(
        norm_mix_g=d_norm_mix, shift_mu=d_mu, w0=d_w0, a0=d_a0, k_k=d_k_k, k_a=d_k_a, r_k=d_r_k,
        ln_x_g=d_ln_g, ln_x_b=d_ln_b, norm_mlp_g=d_norm_mlp, norm_ple_g=d_norm_ple, norm_final_g=d_norm_final)
    return loss, dx, grads, late_flight, early_flight, d_w_in


def _adam_update(partials, w_ref, m_ref, v_ref, g_ref, d_ref, nm_ref, nv_ref):
    g = partials[0].astype(F32)
    for part in partials[1:]:
        g = g + part.astype(F32)
    nm =ADAM_B1 * m_ref[...] + (1.0 - ADAM_B1) * g
    nv = ADAM_B2 * v_ref[...] + (1.0 - ADAM_B2) * (g * g)
    m_hat = nm / (1.0 - ADAM_B1 ** ADAM_STEP)
    v_hat = nv / (1.0 - ADAM_B2 ** ADAM_STEP)
    g_ref[...] = g
    d_ref[...] = -ADAM_LR * (m_hat / (jnp.sqrt(v_hat) + ADAM_EPS) + ADAM_WD * w_ref[...])
    nm_ref[...] = nm
    nv_ref[...] = nv


SMALL_ROWS = 8


def _small_layout(widths):
    widths = list(widths) + [1]
    fill, place = [0] * SMALL_ROWS, [None] * len(widths)
    for j in sorted(range(len(widths)), key=lambda q: -widths[q]):
        row = fill.index(min(fill))
        place[j] = (row, fill[row])
        fill[row] += -(-widths[j] // LANE) * LANE
    return place, max(fill)


def _pack_small(vecs, loss):
    place, total = _small_layout([v_.shape[1] for v_ in vecs])
    n = len(vecs)

    def body(*refs):
        out = jnp.zeros((SMALL_ROWS, total), F32)
        row_id = lax.broadcasted_iota(jnp.int32, (SMALL_ROWS, total), 0)
        for row in range(SMALL_ROWS):
            mine = sorted((off, j) for j, (r_, off) in enumerate(place) if r_ == row)
            pieces, at = [], 0
            for off, j in mine:
                val = refs[j][...]
                pieces.append(val)
                at = off + val.shape[1]
                pad = -val.shape[1] % LANE
                if pad:
                    pieces.append(jnp.zeros((1, pad), F32))
                    at += pad
            if total > at:
                pieces.append(jnp.zeros((1, total - at), F32))
            out = jnp.where(row_id == row, jnp.broadcast_to(jnp.concatenate(pieces, axis=1), (SMALL_ROWS, total)), out)
        refs[n + 1][...] = out

    return pl.pallas_call(body, name="pack_small", out_shape=jax.ShapeDtypeStruct((SMALL_ROWS, total), F32))(*vecs, loss)


def _adamw_small(packed, ws, ms, vs):
    n = len(ws)
    place, _ = _small_layout([w_.shape[1] for w_ in ws])

    def body(p_ref, *refs):
        w_refs, m_refs, v_refs, outs = refs[:n], refs[n:2 * n], refs[2 * n:3 * n], refs[3 * n:]
        for j in range(n):
            row, off = place[j]
            cols = pl.ds(off, ws[j].shape[1])
            _adam_update([p_ref[s, row:row + 1, cols] for s in range(N_DEV)], w_refs[j], m_refs[j], v_refs[j],
                         *outs[4 * j:4 * j + 4])
        row, off = place[n]
        total = p_ref[0, row:row + 1, off:off + 1]
        for s in range(1, N_DEV):
            total = total + p_ref[s, row:row + 1, off:off + 1]
        outs[4 * n][...] = total

    res = pl.pallas_call(
        body, name="adamw_small",
        out_shape=[jax.ShapeDtypeStruct(w_.shape, F32) for w_ in ws for _ in range(4)] + [jax.ShapeDtypeStruct((1, 1), F32)],
    )(packed, *ws, *ms, *vs)
    return [res[4 * j:4 * j + 4] for j in range(n)], res[4 * n]


def _adamw(name, parts, w, m, v, own=None, me=None):
    rows, cols = w.shape[-2:]
    lead = w.ndim - 2
    tr = rows if rows * cols * 4 * 8 <= (4 << 20) else max(8, (4 << 20) // (cols * 4 * 8) // 8 * 8)
    while rows % tr:
        tr -= 8
    shape4 = [jax.ShapeDtypeStruct(w.shape, F32)] * 4
    if own is None:
        def body(p_ref, *refs):
            _adam_update([p_ref[s] for s in range(N_DEV)], *refs)

        blk = pl.BlockSpec((None,) * lead + (tr, cols), lambda i: (0,) * lead + (i, 0))
        return pl.pallas_call(
            body, name=name, grid=(rows // tr,),
            in_specs=[pl.BlockSpec((N_DEV, tr, cols), lambda i: (0, i, 0)), blk, blk, blk], out_specs=[blk] * 4,
            out_shape=shape4, compiler_params=_params(("arbitrary",)),
        )(parts, w, m, v)

    def body_own(me_ref, p_ref, own_ref, *refs):
        mine = own_ref[...]
        _adam_update([jnp.where(me_ref[0] == s, mine, p_ref[s]) for s in range(N_DEV)], *refs)

    blk = pl.BlockSpec((None,) * lead + (tr, cols), lambda i, me_ref: (0,) * lead + (i, 0))
    return pl.pallas_call(
        body_own, name=name, out_shape=shape4,
        grid_spec=pltpu.PrefetchScalarGridSpec(
            num_scalar_prefetch=1, grid=(rows // tr,),
            in_specs=[pl.BlockSpec((N_DEV, tr, cols), lambda i, me_ref: (0, i, 0)),
                      pl.BlockSpec((None, tr, cols), lambda i, me_ref: (me_ref[0], i, 0)), blk, blk, blk],
            out_specs=[blk] * 4),
        compiler_params=_params(("arbitrary",)),
    )(me, parts, own, w, m, v)


def kernel(x, p, norm_mix_g, w_in, conv_w, shift_mu, w_lora_up, w0, a_lora_up, a0, g_lora_up, k_k, k_a, r_k, ln_x_g, ln_x_b, w_out, norm_mlp_g, w_up, w_down, norm_ple_g, w_ple_gate, w_ple_proj, norm_final_g, loss_target, m_norm_mix_g, m_w_in, m_conv_w, m_shift_mu, m_w_lora_up, m_w0, m_a_lora_up, m_a0, m_g_lora_up, m_k_k, m_k_a, m_r_k, m_ln_x_g, m_ln_x_b, m_w_out, m_norm_mlp_g, m_w_up, m_w_down, m_norm_ple_g, m_w_ple_gate, m_w_ple_proj, m_norm_final_g, v_norm_mix_g, v_w_in, v_conv_w, v_shift_mu, v_w_lora_up, v_w0, v_a_lora_up, v_a0, v_g_lora_up, v_k_k, v_k_a, v_r_k, v_ln_x_g, v_ln_x_b, v_w_out, v_norm_mlp_g, v_w_up, v_w_down, v_norm_ple_g, v_w_ple_gate, v_w_ple_proj, v_norm_final_g):
    args = dict(locals())
    wts = {n: args[n] for n in _WEIGHTS}
    mom = {n: args["m_" + n] for n in _WEIGHTS}
    var = {n: args["v_" + n] for n in _WEIGHTS}
    shard2d = lambda a: a.reshape(a.shape[-2:])
    pad_mu = lambda a: _pad_in_cols(jnp.concatenate([jnp.zeros((1, CONV_COLS), F32), a], axis=1))[:, CONV_COLS:]
    unpad_mu = lambda a: _unpad_in_cols(jnp.concatenate([jnp.zeros((1, CONV_COLS), F32), a], axis=1))[:, CONV_COLS:]

    shards = {n: shard2d(wts[n]).astype(BF16 if n in _BF16_GATHER else F32) for n in _SHARDED}
    w = {n: wts[n].reshape(1, -1) for n in _REPLICATED}
    w["shift_mu"] = pad_mu(wts["shift_mu"])

    loss, dx, grads, late_flight, early_flight, d_w_in = _local_step(
        x[0], p[0, 0], loss_target[0], w, [shards[n] for n in _EARLY], [shards[n] for n in _LATE])

    me = (4 * lax.axis_index("x") + 2 * lax.axis_index("y") + lax.axis_index("c")).astype(jnp.int32).reshape(1)
    late_sent, late_parts = _scatter_wait("late_scatter_wait", *late_flight, after=[d_w_in])
    out = {n: _adamw("adamw_" + n, prt, wts[n], mom[n], var[n], own=own, me=me)
           for n, prt, own in zip(_LATE, late_parts, late_sent)}
    early_sent, early_parts = _scatter_wait("early_scatter_wait", *early_flight, after=[dx] + [out[n][1] for n in _LATE])
    for n, prt, own in zip(_EARLY, early_parts, early_sent):
        out[n] = _adamw("adamw_" + n, prt, wts[n], mom[n], var[n], own=own, me=me)

    grads["shift_mu"] = unpad_mu(grads["shift_mu"])
    flat = lambda a: a.reshape(1, -1)
    (small_parts,) = _all_gather("gather_small", [_pack_small([flat(grads[n]) for n in _REPLICATED], loss)])
    small, loss_total = _adamw_small(small_parts, *[[flat(d[n]) for n in _REPLICATED] for d in (wts, mom, var)])
    for n, res in zip(_REPLICATED, small):
        out[n] = [r.reshape(wts[n].shape) for r in res]
    return (loss_total[0, 0], dx[None], *[out[n][0] for n in _WEIGHTS], *[out[n][1] for n in _WEIGHTS],
            *[out[n][2] for n in _WEIGHTS], *[out[n][3] for n in _WEIGHTS])
```

```python
import functools

import jax
import jax.numpy as jnp
from jax import lax
from jax.experimental import pallas as pl
from jax.experimental.pallas import tpu as pltpu

F32 = jnp.float32
BF16 = jnp.bfloat16

N_DEV = 8
D_MODEL = 1024
CONV_DIM = 512
RWKV_DIM = 512
HEAD_DIM = 64
N_HEADS = 8
D_FF = 4096
PLE_DIM = 256
RMS_EPS = 1e-6
GN_EPS = 64e-5
L2_EPS = 1e-12
ADAM_LR, ADAM_B1, ADAM_B2, ADAM_EPS, ADAM_WD, ADAM_STEP = 0.001, 0.9, 0.999, 1e-08, 0.01, 10

CONV_COLS = 3 * CONV_DIM
RW_PAD = 2048
IN_PAD = CONV_COLS + RW_PAD
IN_COLS = 3360
XW_OFF, XA_OFF, XG_OFF = 1536, 1664, 1792
REC_CHUNK = 128
REC_CHUNKS_PER_STEP = 2
REC_PASSES = 1
ROW_BLOCK = 256
LANE = 128
VMEM_LIMIT = 56 * 1024 * 1024


def _dims(dn, ndim):
    if ndim == 3:
        return {"nn": (((2,), (1,)), ((0,), (0,))), "nt": (((2,), (2,)), ((0,), (0,))),
                "tn": (((1,), (1,)), ((0,), (0,)))}[dn]
    return {"nn": (((1,), (0,)), ((), ())), "nt": (((1,), (1,)), ((), ())), "tn": (((0,), (0,)), ((), ()))}[dn]


def _split2(x):
    hi = x.astype(BF16)
    return hi, (x - hi.astype(F32)).astype(BF16)


def _mm_raw(x, y, dn, passes):
    f = lambda p, q: lax.dot_general(p, q, _dims(dn, x.ndim), preferred_element_type=F32)
    if passes == 1:
        return f(x.astype(BF16), y.astype(BF16))
    xh, xl = _split2(x)
    yh, yl = _split2(y)
    if passes == 2:
        return f(xh, yh) + f(xh, yl)
    return f(xh, yh) + f(xh, yl) + f(xl, yh)


@functools.partial(jax.custom_vjp, nondiff_argnums=(2, 3))
def _mm(x, y, dn, passes):
    return _mm_raw(x, y, dn, passes)


def _mm_fwd(x, y, dn, passes):
    return _mm_raw(x, y, dn, passes), (x, y)


def _mm_bwd(dn, passes, res, d):
    x, y = res
    if dn == "nn":
        return _mm(d, y, "nt", passes), _mm(x, d, "tn", passes)
    if dn == "nt":
        return _mm(d, y, "nn", passes), _mm(d, x, "tn", passes)
    return _mm(y, d, "nt", passes), _mm(x, d, "nn", passes)


_mm.defvjp(_mm_fwd, _mm_bwd)


HSUM_COLS = 256


def _head_ones():
    i = lax.broadcasted_iota(jnp.int32, (HSUM_COLS, HSUM_COLS), 0) // HEAD_DIM
    j = lax.broadcasted_iota(jnp.int32, (HSUM_COLS, HSUM_COLS), 1) // HEAD_DIM
    return (i == j).astype(BF16)


def _hsum_raw(x):
    ones = _head_ones()
    f = lambda p: jnp.concatenate(
        [lax.dot_general(p[:, c:c + HSUM_COLS], ones, _dims("nn", 2), preferred_element_type=F32)
         for c in range(0, p.shape[1], HSUM_COLS)], axis=1)
    x1, x2 = _split2(x)
    return f(x1) + f(x2)


@jax.custom_vjp
def _hsum(x):
    return _hsum_raw(x)


_hsum.defvjp(lambda x: (_hsum_raw(x), None), lambda _, d: (_hsum(d),))


def _sigmoid(x):
    return 0.5 + 0.5 * jnp.tanh(0.5 * x)


def _softplus(x):
    return jnp.maximum(x, 0.0) + jnp.log(1.0 + jnp.exp(-jnp.abs(x)))


def _params(sem):
    return pltpu.CompilerParams(dimension_semantics=sem, vmem_limit_bytes=VMEM_LIMIT)


def _rowwise(name, fn, rows, consts, row_outs, acc_outs=(), tr=ROW_BLOCK, halo=False, gather=()):
    rows = [r if isinstance(r, tuple) else (r, r.shape[1], 0) for r in rows]
    t_len = rows[0][0].shape[0]
    tr = min(tr, t_len)
    n_r, n_c, n_o, n_a, n_x = len(rows), len(consts), len(row_outs), len(acc_outs), len(gather)
    n_h = n_r if halo else 0
    sub = 8
    x_specs, x_shapes, x_sems = _gather_io(gather) if n_x else ([], [], [])
    nb = t_len // tr

    def body(*refs):
        if n_x:
            n_in = n_r + n_h + n_c
            start, forward, wait = _gather_plan(refs[n_in:n_in + n_x], refs[len(refs) - 3 - n_x:len(refs) - 3], *refs[len(refs) - 3:])
            pl.when(pl.program_id(0) == 0)(start)
            refs = refs[:n_in] + refs[n_in + n_x:len(refs) - 3 - n_x]
        ins = [r[...] for r in refs[:n_r]]
        ins += [jnp.where(pl.program_id(0) == 0, 0.0, r[sub - 1:sub, :]) for r in refs[n_r:n_r + n_h]]
        ins += [r[...] for r in refs[n_r + n_h:n_r + n_h + n_c]]
        refs = refs[:n_r] + refs[n_r + n_h:]
        outs = fn(*ins)
        o_refs = refs[n_r + n_c:n_r + n_c + n_o]
        a_refs = refs[n_r + n_c + n_o:]
        for o_ref, val in zip(o_refs, outs[:n_o]):
            o_ref[...] = val.astype(o_ref.dtype)
        if n_a:
            first = pl.program_id(0) == 0

            @pl.when(first)
            def _():
                for a_ref, val in zip(a_refs, outs[n_o:]):
                    a_ref[...] = val

            @pl.when(jnp.logical_not(first))
            def _():
                for a_ref, val in zip(a_refs, outs[n_o:]):
                    a_ref[...] += val

        if n_x:
            @pl.when(pl.program_id(0) == nb - 1)
            def _():
                for j in range(n_x):
                    forward(j)
                wait()

    in_specs = [pl.BlockSpec((tr, w), functools.partial(lambda i, c: (i, c), c=cb)) for _, w, cb in rows]
    if halo:
        in_specs += [pl.BlockSpec((sub, w), functools.partial(lambda i, c: (jnp.maximum(i * (tr // sub) - 1, 0), c), c=cb))
                     for _, w, cb in rows]
    in_specs += [pl.BlockSpec(c.shape, functools.partial(lambda i, n: (0,) * n, n=c.ndim)) for c in consts]
    out_specs = [pl.BlockSpec((tr, w), lambda i: (i, 0)) for w, _ in row_outs]
    out_specs += [pl.BlockSpec(s, functools.partial(lambda i, n: (0,) * n, n=len(s))) for s in acc_outs]
    out_shape = [jax.ShapeDtypeStruct((t_len, w), dt) for w, dt in row_outs]
    out_shape += [jax.ShapeDtypeStruct(s, F32) for s in acc_outs]
    return pl.pallas_call(
        body, name=name, grid=(nb,), in_specs=in_specs + x_specs, out_specs=out_specs + x_specs,
        out_shape=out_shape + x_shapes, scratch_shapes=x_sems,
        compiler_params=pltpu.CompilerParams(dimension_semantics=("arbitrary",), vmem_limit_bytes=VMEM_LIMIT,
                                             has_side_effects=bool(n_x)),
    )(*[r[0] for r in rows], *([r[0] for r in rows] if halo else []), *consts, *gather)


def _colwise(name, fn, n_blocks, cols, prms, col_outs, prm_outs=()):
    t_len = cols[0][0].shape[0]
    n_i = len(cols) + len(prms)

    def body(*refs):
        outs = fn(*[r[...] for r in refs[:n_i]])
        for o_ref, val in zip(refs[n_i:], outs):
            o_ref[...] = val.astype(o_ref.dtype)

    spec = lambda r, w: pl.BlockSpec((r, w), lambda j: (0, j))
    in_specs = [spec(t_len, w) for _, w in cols] + [spec(a.shape[0], LANE) for a in prms]
    out_specs = [spec(t_len, bw) for _, _, bw in col_outs] + [spec(r, LANE) for r, _ in prm_outs]
    out_shape = [jax.ShapeDtypeStruct((t_len, w), dt) for w, dt, _ in col_outs]
    out_shape += [jax.ShapeDtypeStruct((r, w), F32) for r, w in prm_outs]
    return pl.pallas_call(
        body, name=name, grid=(n_blocks,), in_specs=in_specs, out_specs=out_specs, out_shape=out_shape,
        compiler_params=_params(("arbitrary",)),
    )(*[c[0] for c in cols], *prms)


def _matmul(name, a, b, dn, outs, *, tm, tn, tk, extras=(), consts=(), epilogue=None, sums=(), a_map=None,
            col_blocks_out=False):
    if dn == "nn":
        (m, k), n = a.shape, b.shape[1]
    elif dn == "nt":
        (m, k), n = a.shape, b.shape[0]
    else:
        (k, m), n = a.shape, b.shape[1]
    tm, tn, tk = min(tm, m), min(tn, n), min(tk, k)
    nk = k // tk
    grid = (m // tm, n // tn, nk)
    assert nk == 1 and (not sums or grid[1] == 1)
    a_spec = pl.BlockSpec((tk, tm), lambda i, j, q: (q, i)) if dn == "tn" else pl.BlockSpec((tm, tk), lambda i, j, q: (i, q))
    b_spec = pl.BlockSpec((tn, tk), lambda i, j, q: (j, q)) if dn == "nt" else pl.BlockSpec((tk, tn), lambda i, j, q: (q, j))
    o_spec = pl.BlockSpec((tm, tn), lambda i, j, q: (i, j))
    c_spec = pl.BlockSpec((1, tn), lambda i, j, q: (0, j))
    n_e, n_c, n_o, n_s = len(extras), len(consts), len(outs), len(sums)

    def body(*refs):
        a_ref, b_ref = refs[:2]
        e_refs = refs[2:2 + n_e + n_c]
        o_refs, s_refs = refs[2 + n_e + n_c:2 + n_e + n_c + n_o], refs[2 + n_e + n_c + n_o:]
        step = pl.program_id(0) * grid[1] + pl.program_id(1)
        a_blk = a_ref[...] if a_map is None else a_map(a_ref[...])
        acc = lax.dot_general(a_blk.astype(BF16), b_ref[...].astype(BF16), _dims(dn, 2), preferred_element_type=F32)
        vals = (acc,) if epilogue is None else epilogue(acc, *[e[...] for e in e_refs])
        for o_ref, val in zip(o_refs, vals[:n_o]):
            o_ref[...] = val.astype(o_ref.dtype)
        if n_s:
            @pl.when(step == 0)
            def _():
                for s_ref, val in zip(s_refs, vals[n_o:]):
                    s_ref[...] = val

            @pl.when(step > 0)
            def _():
                for s_ref, val in zip(s_refs, vals[n_o:]):
                    s_ref[...] += val

    res = pl.pallas_call(
        body, name=name, grid=grid,
        in_specs=[a_spec, b_spec] + [o_spec] * n_e + [c_spec] * n_c,
        out_specs=[pl.BlockSpec((None, tm, tn), lambda i, j, q: (j, i, 0)) if col_blocks_out else o_spec] * n_o
                  + [c_spec] * n_s,
        out_shape=[jax.ShapeDtypeStruct((n // tn, m, tn) if col_blocks_out else (m, n), dt) for dt in outs]
                  + [jax.ShapeDtypeStruct(s, F32) for s in sums],
        compiler_params=_params(("arbitrary",) * 3 if n_s else ("parallel", "parallel", "arbitrary")),
    )(a, b, *extras, *consts)
    return res[0] if len(res) == 1 else res


def _rms(h, g):
    return h * lax.rsqrt(jnp.mean(h * h, axis=-1, keepdims=True) + RMS_EPS) * g


def _rms_bwd(h, g, dy):
    rs = lax.rsqrt(jnp.mean(h * h, axis=-1, keepdims=True) + RMS_EPS)
    n = h * rs
    dn = dy * g
    dh = rs * (dn - n * jnp.mean(dn * n, axis=-1, keepdims=True))
    return dh, jnp.sum(dy * n, axis=0, keepdims=True)


def _rwkv_pre(k, xw, xa, xg, w0, a0, k_k, k_a, wl, al, gl):
    zw = w0 + _mm(jnp.tanh(xw), wl, "nn", 1)
    lw = -jnp.exp(-_softplus(-zw) - 0.5)
    iclr = _sigmoid(a0 + _mm(xa, al, "nn", 1))
    g = _mm(_sigmoid(xg), gl, "nn", 1)
    kk0 = k * k_k
    kk = kk0 * lax.rsqrt(jnp.maximum(_hsum(kk0 * kk0), L2_EPS * L2_EPS))
    k_h = k * (1.0 + (iclr - 1.0) * k_a)
    return lw, k_h, -kk, kk * iclr, g


def _rwkv_pre_vjp(k, xw, xa, xg, lw, w0, a0, k_k, k_a, wl, al, gl, dlw, dk_h, da, db, dg):
    del w0
    s0 = lambda z: jnp.sum(z, axis=0, keepdims=True)
    iclr = _sigmoid(a0 + _mm(xa, al, "nn", 1))
    kk0 = k * k_k
    s = _hsum(kk0 * kk0)
    inv = lax.rsqrt(jnp.maximum(s, L2_EPS * L2_EPS))
    kk = kk0 * inv
    dkk = db * iclr - da
    diclr = db * kk + dk_h * (k * k_a)
    dkk0 = dkk * inv - jnp.where(s > L2_EPS * L2_EPS, inv * inv * inv * _hsum(dkk * kk0), 0.0) * kk0
    dk = dk_h * (1.0 + (iclr - 1.0) * k_a) + dkk0 * k_k
    dza = diclr * iclr * (1.0 - iclr)
    dzw = dlw * lw * (1.0 + lw * 1.6487212707001282)
    th = jnp.tanh(xw)
    sg = _sigmoid(xg)
    dxw = _mm(dzw, wl, "nt", 1) * (1.0 - th * th)
    dxa = _mm(dza, al, "nt", 1)
    dxg = _mm(dg, gl, "nt", 1) * sg * (1.0 - sg)
    return (dk, dxw, dxa, dxg, s0(dzw), s0(dza), s0(dkk0 * k), s0(dk_h * k * (iclr - 1.0)),
            _mm(th, dzw, "tn", 1), _mm(xa, dza, "tn", 1), _mm(sg, dg, "tn", 1))


def _rwkv_post(y, r, k_h, v, g, ln_g, ln_b, r_k):
    mu = _hsum(y) * (1.0 / HEAD_DIM)
    yc = y - mu
    var = _hsum(yc * yc) * (1.0 / HEAD_DIM)
    yo = yc * lax.rsqrt(var + GN_EPS) * ln_g + ln_b
    bonus = _hsum(r * k_h * r_k) * v
    return (yo + bonus) * g


def _shift_down(x, n):
    rows = lax.broadcasted_iota(jnp.int32, x.shape, 0)
    return jnp.where(rows < n, 0.0, pltpu.roll(x, n, 0))


def _shift_up(x, n):
    t_len = x.shape[0]
    rows = lax.broadcasted_iota(jnp.int32, x.shape, 0)
    return jnp.where(rows >= t_len - n, 0.0, pltpu.roll(x, t_len - n, 0))


def _gather_plan(ins, outs, send_sems, recv_sems, local_sems):
    x, y, c = lax.axis_index("x"), lax.axis_index("y"), lax.axis_index("c")
    me = 4 * x + 2 * y + c
    direct, chips = (1, 2, 4, 6), (2, 4, 6)

    def local(i):
        return pltpu.make_async_copy(ins[i], outs[i].at[me], local_sems.at[i])

    def send(i, rel):
        return pltpu.make_async_remote_copy(
            src_ref=ins[i], dst_ref=outs[i].at[me], send_sem=send_sems.at[i, rel - 1], recv_sem=recv_sems.at[i, rel - 1],
            device_id=(x ^ (rel >> 2), y ^ ((rel >> 1) & 1), c ^ (rel & 1)), device_id_type=pl.DeviceIdType.MESH)

    def passed(i, rel):
        slot = outs[i].at[me ^ rel]
        return pltpu.make_async_remote_copy(
            src_ref=slot, dst_ref=slot, send_sem=send_sems.at[i, rel], recv_sem=recv_sems.at[i, rel],
            device_id=(x, y, 1 - c), device_id_type=pl.DeviceIdType.MESH)

    def landed(i, rel):
        slot = outs[i].at[me ^ rel]
        return pltpu.make_async_remote_copy(
            src_ref=slot, dst_ref=slot, send_sem=send_sems.at[i, rel - 1], recv_sem=recv_sems.at[i, rel - 1],
            device_id=(x, y, c), device_id_type=pl.DeviceIdType.MESH)

    def start():
        for i in range(len(ins)):
            local(i).start()
            for rel in direct:
                send(i, rel).start()

    def forward(i):
        for rel in chips:
            landed(i, rel).wait_recv()
            passed(i, rel).start()

    def wait():
        for i in range(len(ins)):
            local(i).wait()
            for rel in (1, 3, 5, 7):
                landed(i, rel).wait_recv()
            for rel in direct:
                send(i, rel).wait_send()
            for rel in chips:
                passed(i, rel).wait_send()

    return start, forward, wait


def _gather_io(arrays):
    n = len(arrays)
    any_spec = pl.BlockSpec(memory_space=pl.ANY)
    out_shape = [jax.ShapeDtypeStruct((N_DEV,) + a.shape, a.dtype) for a in arrays]
    sems = [pltpu.SemaphoreType.DMA((n, N_DEV - 1)), pltpu.SemaphoreType.DMA((n, N_DEV - 1)), pltpu.SemaphoreType.DMA((n,))]
    return [any_spec] * n, out_shape, sems


def _all_gather(name, arrays):
    n = len(arrays)
    specs, out_shape, sems = _gather_io(arrays)

    def body(*refs):
        start, forward, wait = _gather_plan(refs[:n], refs[n:2 * n], *refs[2 * n:])
        start()
        for i in range(n):
            forward(i)
        wait()

    return pl.pallas_call(
        body, name=name, in_specs=specs, out_specs=specs, out_shape=out_shape, scratch_shapes=sems,
        compiler_params=pltpu.CompilerParams(has_side_effects=True),
    )(*arrays)


def _scatter_start(name, arrays, lands):
    n = len(arrays)
    hbm = pl.BlockSpec(memory_space=pltpu.HBM)

    def body(*refs):
        ins, land, send_sems, recv_sems = refs[:n], refs[n:2 * n], refs[2 * n], refs[2 * n + 1]
        token = refs[4 * n + 2]
        x, y, c = lax.axis_index("x"), lax.axis_index("y"), lax.axis_index("c")
        me = 4 * x + 2 * y + c
        for i in range(n):
            for rel in range(1, N_DEV):
                k = i * (N_DEV - 1) + rel - 1
                pltpu.make_async_remote_copy(
                    src_ref=ins[i].at[me ^ rel], dst_ref=land[i].at[me], send_sem=send_sems.at[k],
                    recv_sem=recv_sems.at[k], device_id=(x ^ (rel >> 2), y ^ ((rel >> 1) & 1), c ^ (rel & 1)),
                    device_id_type=pl.DeviceIdType.MESH).start()
        token[...] = jnp.zeros_like(token)

    sem = pltpu.SemaphoreType.DMA((n * (N_DEV - 1),))
    bufs = [pltpu.HBM(a.shape, a.dtype) for a in list(arrays) + list(lands)]
    res = pl.pallas_call(
        body, name=name, out_shape=(sem, sem, *bufs, jax.ShapeDtypeStruct((8, LANE), F32)),
        in_specs=[hbm] * (2 * n),
        out_specs=(pl.BlockSpec(memory_space=pltpu.SEMAPHORE),) * 2 + (hbm,) * (2 * n) + (pl.BlockSpec(memory_space=pltpu.VMEM),),
        input_output_aliases={i: 2 + i for i in range(2 * n)},
        compiler_params=pltpu.CompilerParams(has_side_effects=pltpu.SideEffectType.DATAFLOW_SIDE_EFFECTING),
    )(*[pltpu.with_memory_space_constraint(a, pltpu.HBM) for a in list(arrays) + list(lands)])
    return res[0], res[1], res[2:2 + n], res[2 + n:2 + 2 * n], res[2 + 2 * n]


def _scatter_wait(name, send_sems, recv_sems, arrays, lands, after):
    n, n_after = len(arrays), len(after)
    hbm = pl.BlockSpec(memory_space=pltpu.HBM)

    def body(*refs):
        ins, land, s_sems, r_sems = refs[:n], refs[n:2 * n], refs[2 * n], refs[2 * n + 1]
        x, y, c = lax.axis_index("x"), lax.axis_index("y"), lax.axis_index("c")
        me = 4 * x + 2 * y + c
        for i in range(n):
            for rel in range(1, N_DEV):
                k = i * (N_DEV - 1) + rel - 1
                cp = pltpu.make_async_remote_copy(
                    src_ref=ins[i].at[me ^ rel], dst_ref=land[i].at[me ^ rel], send_sem=s_sems.at[k],
                    recv_sem=r_sems.at[k], device_id=(x, y, c), device_id_type=pl.DeviceIdType.MESH)
                cp.wait_send()
                cp.wait_recv()

    res = pl.pallas_call(
        body, name=name, out_shape=[pltpu.HBM(a.shape, a.dtype) for a in list(arrays) + list(lands)],
        in_specs=[hbm] * (2 * n) + [pl.BlockSpec(memory_space=pltpu.SEMAPHORE)] * 2 + [pl.BlockSpec(memory_space=pl.ANY)] * n_after,
        out_specs=[hbm] * (2 * n), input_output_aliases={i: i for i in range(2 * n)},
        compiler_params=pltpu.CompilerParams(has_side_effects=pltpu.SideEffectType.DATAFLOW_SIDE_EFFECTING),
    )(*arrays, *lands, send_sems, recv_sems, *after)
    return res[:n], res[n:]


def _gather_start(name, shards, after):
    n, n_after = len(shards), len(after)
    hbm = pl.BlockSpec(memory_space=pltpu.HBM)
    lands = [lax.empty((N_DEV,) + a.shape, a.dtype) for a in shards]

    def body(*refs):
        ins, land = refs[:n], refs[n:2 * n]
        outs = refs[2 * n + n_after:]
        send_sems, recv_sems, token = outs[:n], outs[n:2 * n], outs[4 * n]
        x, y, c = lax.axis_index("x"), lax.axis_index("y"), lax.axis_index("c")
        me = 4 * x + 2 * y + c
        for i in range(n):
            for rel in range(N_DEV):
                pltpu.make_async_remote_copy(
                    src_ref=ins[i], dst_ref=land[i].at[me], send_sem=send_sems[i].at[rel],
                    recv_sem=recv_sems[i].at[rel], device_id=(x ^ (rel >> 2), y ^ ((rel >> 1) & 1), c ^ (rel & 1)),
                    device_id_type=pl.DeviceIdType.MESH).start()
        token[...] = jnp.zeros_like(token)

    sem = pltpu.SemaphoreType.DMA((N_DEV,))
    bufs = [pltpu.HBM(a.shape, a.dtype) for a in list(shards) + lands]
    res = pl.pallas_call(
        body, name=name, out_shape=(*[sem] * (2 * n), *bufs, jax.ShapeDtypeStruct((8, LANE), F32)),
        in_specs=[hbm] * (2 * n) + [pl.BlockSpec(memory_space=pl.ANY)] * n_after,
        out_specs=(pl.BlockSpec(memory_space=pltpu.SEMAPHORE),) * (2 * n) + (hbm,) * (2 * n)
                  + (pl.BlockSpec(memory_space=pltpu.VMEM),),
        input_output_aliases={i: 2 * n + i for i in range(2 * n)},
        compiler_params=pltpu.CompilerParams(has_side_effects=pltpu.SideEffectType.DATAFLOW_SIDE_EFFECTING),
    )(*[pltpu.with_memory_space_constraint(a, pltpu.HBM) for a in list(shards) + lands], *after)
    return res[:n], res[n:2 * n], res[2 * n:3 * n], res[3 * n:4 * n], res[4 * n]


def _gather_wait(name, send_sems, recv_sems, shards, lands, after):
    n, n_after = len(shards), len(after)
    hbm = pl.BlockSpec(memory_space=pltpu.HBM)

    def body(*refs):
        ins, land = refs[:n], refs[n:2 * n]
        s_sems, r_sems = refs[2 * n:3 * n], refs[3 * n:4 * n]
        x, y, c = lax.axis_index("x"), lax.axis_index("y"), lax.axis_index("c")
        me = 4 * x + 2 * y + c
        for i in range(n):
            for rel in range(N_DEV):
                cp = pltpu.make_async_remote_copy(
                    src_ref=ins[i], dst_ref=land[i].at[me ^ rel], send_sem=s_sems[i].at[rel],
                    recv_sem=r_sems[i].at[rel], device_id=(x, y, c), device_id_type=pl.DeviceIdType.MESH)
                cp.wait_send()
                cp.wait_recv()

    res = pl.pallas_call(
        body, name=name, out_shape=[pltpu.HBM(a.shape, a.dtype) for a in list(shards) + list(lands)],
        in_specs=[hbm] * (2 * n) + [pl.BlockSpec(memory_space=pltpu.SEMAPHORE)] * (2 * n)
                 + [pl.BlockSpec(memory_space=pl.ANY)] * n_after,
        out_specs=[hbm] * (2 * n), input_output_aliases={i: i for i in range(2 * n)},
        compiler_params=pltpu.CompilerParams(has_side_effects=pltpu.SideEffectType.DATAFLOW_SIDE_EFFECTING),
    )(*shards, *lands, *send_sems, *recv_sems, *after)
    return res[n:]


def _tri_powers(low):
    powers, n, p = [low.astype(BF16)], 1, low
    while 2 * n < low.shape[-1]:
        p = _mm(p, p, "nn", REC_PASSES)
        powers.append(p.astype(BF16))
        n *= 2
    return powers


@jax.custom_vjp
def _tri_solve(low, rhs, powers):
    del low
    for p in powers:
        rhs = rhs + _mm(p, rhs, "nn", REC_PASSES)
    return rhs


def _tri_solve_fwd(low, rhs, powers):
    out = _tri_solve(low, rhs, powers)
    return out, (powers, out)


def _tri_solve_bwd(res, d):
    powers, u = res
    for p in powers:
        d = d + _mm(p, d, "tn", REC_PASSES)
    return _mm(d, u, "nt", REC_PASSES), d, [jnp.zeros_like(p) for p in powers]


_tri_solve.defvjp(_tri_solve_fwd, _tri_solve_bwd)


@jax.custom_vjp
def _tri_solve_given(low, rhs, powers, value):
    del low, rhs, powers
    return value


def _tri_solve_given_fwd(low, rhs, powers, value):
    return value, (powers, value)


def _tri_solve_given_bwd(res, d):
    return _tri_solve_bwd(res, d) + (jnp.zeros_like(res[1]),)


_tri_solve_given.defvjp(_tri_solve_given_fwd, _tri_solve_given_bwd)


def _heads(x):
    return jnp.stack([x[:, h * HEAD_DIM:(h + 1) * HEAD_DIM] for h in range(N_HEADS)])


def _unheads(x):
    return jnp.concatenate([x[h] for h in range(N_HEADS)], axis=-1)


def _causal_masks(c):
    ti = lax.broadcasted_iota(jnp.int32, (c, c), 0)
    si = lax.broadcasted_iota(jnp.int32, (c, c), 1)
    strict, incl = si < ti, si <= ti
    both = jnp.concatenate([jnp.concatenate([strict, strict], axis=1), jnp.concatenate([incl, incl], axis=1)], axis=0)
    return strict, incl, both


@jax.custom_vjp
def _gram_given(x2, y2, value):
    del x2, y2
    return value.astype(F32)


def _gram_given_fwd(x2, y2, value):
    return value.astype(F32), (x2, y2, value)


def _gram_given_bwd(res, d):
    x2, y2, value = res
    d = jnp.where(_causal_masks(d.shape[-1] // 2)[2], d, 0.0)
    return _mm(d, y2, "nn", 2), _mm(d, x2, "tn", 2), jnp.zeros_like(value)


_gram_given.defvjp(_gram_given_fwd, _gram_given_bwd)


def _chunk_fwd(z0, r, lw, k, v, a, b, powers=None, gram_value=None, u_value=None):
    c = r.shape[0]
    n_h, n_k = z0.shape[0], z0.shape[1]
    mm = functools.partial(_mm, passes=REC_PASSES)
    gram = functools.partial(_mm, passes=2)
    _, incl, mask = _causal_masks(c)
    cum = _mm(incl.astype(F32), lw, "nn", 3)
    cum_end = cum[c - 1:c, :]
    e_neg, e_end = jnp.exp(-cum), jnp.exp(cum_end - cum)
    x2 = jnp.concatenate([_heads(a * jnp.exp(cum - lw)), _heads(r * jnp.exp(cum))], axis=1)
    y2 = jnp.concatenate([_heads(b * e_neg), _heads(k * e_neg)], axis=1)
    vh = _heads(v)
    g2 = jnp.where(mask, gram(x2, y2, "nt"), 0.0) if gram_value is None else _gram_given(x2, y2, gram_value)
    t2 = mm(x2, z0, "nn") + mm(g2[:, :, c:], vh, "nn")
    low = g2[:, :c, :c]
    powers = _tri_powers(low) if powers is None else powers
    u = _tri_solve(low, t2[:, :c], powers) if u_value is None else _tri_solve_given(low, t2[:, :c], powers, u_value)
    y = t2[:, c:] + mm(g2[:, c:, :c], u, "nn")
    ki = lax.broadcasted_iota(jnp.int32, (n_k, n_k), 0)
    kj = lax.broadcasted_iota(jnp.int32, (n_k, n_k), 1)
    dmat = jnp.where(ki == kj, jnp.broadcast_to(_heads(jnp.exp(cum_end)), (n_h, n_k, n_k)), 0.0)
    z_end = mm(dmat, z0, "nn") + mm(jnp.concatenate([_heads(b * e_end), _heads(k * e_end)], axis=1),
                                    jnp.concatenate([u, vh], axis=1), "tn")
    return _unheads(y), z_end, powers, g2, u


def _rec_fwd(u, lw, k, a, b):
    t_len = lw.shape[0]
    c = min(REC_CHUNK, t_len)
    nc = t_len // c
    per = REC_CHUNKS_PER_STEP if nc % REC_CHUNKS_PER_STEP == 0 else 1
    steps = nc // per
    n_pow = max(1, (c - 1).bit_length())

    def body(r_ref, v_ref, lw_ref, k_ref, a_ref, b_ref, y_ref, zs_ref, pw_ref, gs_ref, us_ref, z_scr):
        @pl.when(pl.program_id(0) == 0)
        def _():
            z_scr[...] = jnp.zeros_like(z_scr)

        for s in range(per):
            rows = pl.ds(s * c, c)
            z0 = z_scr[...]
            zs_ref[s] = z0
            y, z_end, powers, g2, u_rows = _chunk_fwd(z0, r_ref[rows, :], lw_ref[rows, :], k_ref[rows, :], v_ref[rows, :],
                                                      a_ref[rows, :], b_ref[rows, :])
            y_ref[rows, :] = y
            z_scr[...] = z_end
            pw_ref[s] = jnp.concatenate(powers, axis=0)
            gs_ref[s] = g2.astype(BF16)
            us_ref[s] = u_rows

    blk = lambda cb: pl.BlockSpec((per * c, RWKV_DIM), functools.partial(lambda i, q: (i, q), q=cb))
    res = pl.pallas_call(
        body, name="rwkv_rec_fwd", grid=(steps,),
        in_specs=[blk(0), blk(2)] + [blk(0)] * 4,
        out_specs=[blk(0), pl.BlockSpec((per, N_HEADS, HEAD_DIM, HEAD_DIM), lambda i: (i, 0, 0, 0)),
                   pl.BlockSpec((per, n_pow * N_HEADS, c, c), lambda i: (i, 0, 0, 0)),
                   pl.BlockSpec((per, N_HEADS, 2 * c, 2 * c), lambda i: (i, 0, 0, 0)),
                   pl.BlockSpec((per, N_HEADS, c, HEAD_DIM), lambda i: (i, 0, 0, 0))],
        out_shape=[jax.ShapeDtypeStruct((t_len, RWKV_DIM), F32),
                   jax.ShapeDtypeStruct((nc, N_HEADS, HEAD_DIM, HEAD_DIM), F32),
                   jax.ShapeDtypeStruct((nc, n_pow * N_HEADS, c, c), BF16),
                   jax.ShapeDtypeStruct((nc, N_HEADS, 2 * c, 2 * c), BF16),
                   jax.ShapeDtypeStruct((nc, N_HEADS, c, HEAD_DIM), F32)],
        scratch_shapes=[pltpu.VMEM((N_HEADS, HEAD_DIM, HEAD_DIM), F32)], compiler_params=_params(("arbitrary",)),
    )(u, u, lw, k, a, b)
    return res[0], res[1:]


def _rec_bwd(u, lw, k, a, b, saved, dy):
    t_len = lw.shape[0]
    c = min(REC_CHUNK, t_len)
    nc = t_len // c
    per = REC_CHUNKS_PER_STEP if nc % REC_CHUNKS_PER_STEP == 0 else 1
    steps = nc // per

    zs, pw, gs, us = saved

    def body(r_ref, v_ref, lw_ref, k_ref, a_ref, b_ref, zs_ref, pw_ref, gs_ref, us_ref, dy_ref, *rest):
        g_refs, dz_scr = rest[:6], rest[6]

        @pl.when(pl.program_id(0) == 0)
        def _():
            dz_scr[...] = jnp.zeros_like(dz_scr)

        for s in reversed(range(per)):
            rows = pl.ds(s * c, c)
            powers = [pw_ref[s, j * N_HEADS:(j + 1) * N_HEADS] for j in range(pw.shape[1] // N_HEADS)]
            chunk = functools.partial(lambda gram, u_val, pws, *xs: _chunk_fwd(*xs, powers=pws, gram_value=gram, u_value=u_val)[:2],
                                      gs_ref[s], us_ref[s], powers)
            _, vjp = jax.vjp(chunk, zs_ref[s], r_ref[rows, :], lw_ref[rows, :], k_ref[rows, :], v_ref[rows, :],
                             a_ref[rows, :], b_ref[rows, :])
            dz0, dr, dlw, dk, dv, da, db = vjp((dy_ref[rows, :], dz_scr[...]))
            for ref, val in zip(g_refs, (dr, dv, dlw, dk, da, db)):
                ref[rows, :] = val
            dz_scr[...] = dz0

    blk = lambda cb: pl.BlockSpec((per * c, RWKV_DIM), functools.partial(lambda i, q: (steps - 1 - i, q), q=cb))
    saved_blk = lambda arr: pl.BlockSpec((per,) + arr.shape[1:], lambda i: (steps - 1 - i, 0, 0, 0))
    return pl.pallas_call(
        body, name="rwkv_rec_bwd", grid=(steps,),
        in_specs=[blk(0), blk(2)] + [blk(0)] * 4 + [saved_blk(zs), saved_blk(pw), saved_blk(gs), saved_blk(us), blk(0)],
        out_specs=[blk(0)] * 6, out_shape=[jax.ShapeDtypeStruct((t_len, RWKV_DIM), F32)] * 6,
        scratch_shapes=[pltpu.VMEM((N_HEADS, HEAD_DIM, HEAD_DIM), F32)], compiler_params=_params(("arbitrary",)),
    )(u, u, lw, k, a, b, zs, pw, gs, us, dy)


_EARLY = ["w_in", "conv_w", "w_lora_up", "a_lora_up", "g_lora_up"]
_LATE = ["w_out", "w_up", "w_down", "w_ple_gate", "w_ple_proj"]
_SHARDED = _EARLY + _LATE
_COL_SHARDED = {"w_in", "conv_w", "w_lora_up", "a_lora_up", "g_lora_up", "w_up", "w_ple_proj"}
_BF16_GATHER = {"w_in", "w_out", "w_up", "w_down", "w_ple_gate", "w_ple_proj"}
_REPLICATED = ["norm_mix_g", "shift_mu", "w0", "a0", "k_k", "k_a", "r_k", "ln_x_g", "ln_x_b", "norm_mlp_g", "norm_ple_g",
               "norm_final_g"]
_WEIGHTS = ["norm_mix_g", "w_in", "conv_w", "shift_mu", "w_lora_up", "w0", "a_lora_up", "a0", "g_lora_up", "k_k", "k_a", "r_k",
            "ln_x_g", "ln_x_b", "w_out", "norm_mlp_g", "w_up", "w_down", "norm_ple_g", "w_ple_gate", "w_ple_proj", "norm_final_g"]


def _unshard(name, g):
    if name in _COL_SHARDED:
        return jnp.moveaxis(g, 0, 1).reshape(g.shape[1], N_DEV * g.shape[2])
    return g.reshape(N_DEV * g.shape[1], g.shape[2])


def _reshard(name, full):
    if name in _COL_SHARDED:
        return jnp.moveaxis(full.reshape(full.shape[0], N_DEV, full.shape[1] // N_DEV), 1, 0)
    return full.reshape(N_DEV, full.shape[0] // N_DEV, full.shape[1])


def _pad_in_cols(a):
    z = lambda n: jnp.zeros(a.shape[:-1] + (n,), a.dtype)
    conv = [a[..., part * CONV_DIM + j * LANE:part * CONV_DIM + (j + 1) * LANE] for j in range(CONV_DIM // LANE) for part in range(3)]
    return jnp.concatenate(conv + [a[..., CONV_COLS:3136], z(64), a[..., 3136:3200], z(64), a[..., 3200:3360], z(96)], axis=-1)


def _unpad_in_cols(a):
    conv = [a[..., (3 * j + part) * LANE:(3 * j + part + 1) * LANE] for part in range(3) for j in range(CONV_DIM // LANE)]
    return jnp.concatenate(conv + [a[..., CONV_COLS:3136], a[..., 3200:3264], a[..., 3328:3488]], axis=-1)


def _assemble_w_in(g):
    n_dev, rows, cols = g.shape

    def body(g_ref, o_ref):
        o_ref[...] = _pad_in_cols(jnp.concatenate([g_ref[d] for d in range(n_dev)], axis=1))

    return pl.pallas_call(
        body, name="w_in_assemble", grid=(rows // ROW_BLOCK,),
        in_specs=[pl.BlockSpec((n_dev, ROW_BLOCK, cols), lambda i: (0, i, 0))],
        out_specs=pl.BlockSpec((ROW_BLOCK, IN_PAD), lambda i: (i, 0)),
        out_shape=jax.ShapeDtypeStruct((rows, IN_PAD), g.dtype), compiler_params=_params(("arbitrary",)),
    )(g)


def _split_w_in_grad(dw):
    rows = dw.shape[0]
    cols = IN_COLS // N_DEV

    def body(d_ref, o_ref):
        full = _unpad_in_cols(d_ref[...])
        for d in range(N_DEV):
            o_ref[d] = full[:, cols * d:cols * (d + 1)]

    return pl.pallas_call(
        body, name="w_in_grad_split", grid=(rows // ROW_BLOCK,),
        in_specs=[pl.BlockSpec((ROW_BLOCK, IN_PAD), lambda i: (i, 0))],
        out_specs=pl.BlockSpec((N_DEV, ROW_BLOCK, cols), lambda i: (0, i, 0)),
        out_shape=jax.ShapeDtypeStruct((N_DEV, rows, cols), dw.dtype), compiler_params=_params(("arbitrary",)),
    )(dw)


def _pad_rows(a, rows):
    return jnp.concatenate([a, jnp.zeros((rows - a.shape[0],) + a.shape[1:], a.dtype)], axis=0)


SEG_W = [RWKV_DIM, RWKV_DIM, RWKV_DIM, LANE, LANE, 2 * LANE]
SEG_OFF = [0, 512, 1024, XW_OFF, XA_OFF, XG_OFF]


def _rwkv_pre_bwd(proj, u, grads, mu, small, dproj):
    t_len = u.shape[0]
    tr = min(ROW_BLOCK, t_len)
    nb = t_len // tr
    sub = 8
    n_g = len(grads)
    acc_shapes = [(1, RW_PAD)] + [(1, RWKV_DIM)] * 4 + [(LANE, RWKV_DIM), (LANE, RWKV_DIM), (2 * LANE, RWKV_DIM)]

    def body(*refs):
        seg_refs, halo_refs = refs[:6], refs[6:12]
        k_ref, xw_ref, xa_ref, xg_ref = refs[12:16]
        g_refs = refs[16:16 + n_g]
        mu_ref = refs[16 + n_g]
        prm_refs = refs[17 + n_g:24 + n_g]
        out_hbm = refs[25 + n_g]
        acc_refs = refs[26 + n_g:26 + n_g + len(acc_shapes)]
        vbuf, sems, carry = refs[26 + n_g + len(acc_shapes):]
        i = pl.program_id(0)
        blk = nb - 1 - i
        dr1, dr2, dv1, dv2, dlw, dk1, dk2, da, db, dg, lw_rows = [g[...] for g in g_refs]
        dk, dxw, dxa, dxg, *dprm = _rwkv_pre_vjp(k_ref[...], xw_ref[...], xa_ref[...], xg_ref[...], lw_rows,
                                                  *[p_[...] for p_ in prm_refs], dlw, dk1 + dk2, da, db, dg)
        du = jnp.concatenate([dr1 + dr2, dk, dv1 + dv2, dxw, dxa, dxg], axis=1)
        mu_v = mu_ref[...]

        @pl.when(i == 0)
        def _():
            carry[...] = jnp.zeros_like(carry)

        rows = lax.broadcasted_iota(jnp.int32, du.shape, 0)
        nxt = jnp.where(rows == tr - 1, carry[...], pltpu.roll(du, tr - 1, 0))
        d_rw = du - mu_v * du + mu_v * nxt
        d_mu = []
        for s_ref, h_ref, off, wd in zip(seg_refs, halo_refs, SEG_OFF, SEG_W):
            cur = s_ref[...]
            r0 = lax.broadcasted_iota(jnp.int32, cur.shape, 0)
            prev = jnp.where(r0 == 0, jnp.where(blk == 0, 0.0, h_ref[sub - 1:sub, :]), pltpu.roll(cur, 1, 0))
            d_mu.append(jnp.sum(du[:, off:off + wd] * (prev - cur), axis=0, keepdims=True))
        sums = [jnp.concatenate(d_mu, axis=1)] + list(dprm)

        @pl.when(i == 0)
        def _():
            for a_ref, val in zip(acc_refs, sums):
                a_ref[...] = val

        @pl.when(i > 0)
        def _():
            for a_ref, val in zip(acc_refs, sums):
                a_ref[...] += val

        carry[...] = du[0:1, :]
        slot = i % 2

        def writeback(s, b):
            return pltpu.make_async_copy(vbuf.at[s], out_hbm.at[pl.ds(b * tr, tr), pl.ds(CONV_COLS, RW_PAD)], sems.at[s])

        @pl.when(i >= 2)
        def _():
            writeback(slot, blk + 2).wait()

        vbuf[slot] = d_rw.astype(vbuf.dtype)
        writeback(slot, blk).start()

        @pl.when(i == nb - 1)
        def _():
            writeback(slot, blk).wait()
            if nb > 1:
                writeback(1 - slot, blk + 1).wait()

    rev = lambda w_, cb: pl.BlockSpec((tr, w_), functools.partial(lambda i, c: (nb - 1 - i, c), c=cb))
    halo = lambda w_, cb: pl.BlockSpec((sub, w_), functools.partial(
        lambda i, c: (jnp.maximum((nb - 1 - i) * (tr // sub) - 1, 0), c), c=cb))
    whole = lambda a: pl.BlockSpec(a.shape, functools.partial(lambda i, n: (0,) * n, n=a.ndim))
    segs = [(wd, (CONV_COLS + off) // wd) for off, wd in zip(SEG_OFF, SEG_W)]
    u_cols = [(512, 1), (LANE, XW_OFF // LANE), (LANE, XA_OFF // LANE), (2 * LANE, XG_OFF // (2 * LANE))]
    any_spec = pl.BlockSpec(memory_space=pl.ANY)
    res = pl.pallas_call(
        body, name="rwkv_pre_bwd", grid=(nb,),
        in_specs=[rev(*s) for s in segs] + [halo(*s) for s in segs] + [rev(*c) for c in u_cols]
                 + [rev(RWKV_DIM, 0)] * n_g + [whole(mu)] + [whole(p_) for p_ in small] + [any_spec],
        out_specs=[any_spec] + [pl.BlockSpec(s, functools.partial(lambda i, n: (0,) * n, n=len(s))) for s in acc_shapes],
        out_shape=[jax.ShapeDtypeStruct(dproj.shape, dproj.dtype)] + [jax.ShapeDtypeStruct(s, F32) for s in acc_shapes],
        scratch_shapes=[pltpu.VMEM((2, tr, RW_PAD), dproj.dtype), pltpu.SemaphoreType.DMA((2,)), pltpu.VMEM((1, RW_PAD), F32)],
        input_output_aliases={24 + n_g: 0},
        compiler_params=_params(("arbitrary",)),
    )(*[proj] * 12, *[u] * 4, *grads, mu, *small, dproj)
    return res


def _local_step(x, p, tgt, w, early_shards, late_shards):
    row = lambda v: v.reshape(1, -1)
    w = dict(w)

    xn1, *gathered = _rowwise("rms_mix", lambda h, g: (_rms(h, g),), [x], [w["norm_mix_g"]], [(D_MODEL, BF16)],
                              gather=early_shards)
    w.update({n: _unshard(n, g_) for n, g_ in zip(_EARLY[1:], gathered[1:])})
    w["w_in"] = _assemble_w_in(gathered[0])
    w["w_lora_up"] = _pad_rows(w["w_lora_up"], LANE)
    w["a_lora_up"] = _pad_rows(w["a_lora_up"], LANE)
    w["g_lora_up"] = _pad_rows(w["g_lora_up"], 2 * LANE)
    lg_send, lg_recv, lg_shards, lg_lands, lg_token = _gather_start("late_gather_start", late_shards, after=[xn1])
    w["shift_mu"] = w["shift_mu"] + lg_token[0:1, 0:1]
    proj = _matmul("in_proj", xn1, w["w_in"], "nn", [F32], tm=2048, tn=512, tk=D_MODEL)
    n_cb = CONV_DIM // LANE

    def conv_fwd(blk, cw):
        gb, gc, hx = blk[:, :LANE], blk[:, LANE:2 * LANE], blk[:, 2 * LANE:]
        uu = gc * hx
        return (gb * (uu * cw[2:3] + _shift_down(uu, 1) * cw[1:2] + _shift_down(uu, 2) * cw[0:1]),)

    (y_conv,) = _colwise("conv_fwd", conv_fwd, n_cb, [(proj, 3 * LANE)], [w["conv_w"]], [(CONV_DIM, BF16, LANE)])

    small = [w["w0"], w["a0"], w["k_k"], w["k_a"], w["w_lora_up"], w["a_lora_up"], w["g_lora_up"]]
    def pre_fwd(*xs):
        cur, prev_rows, mu, prm = xs[:6], xs[6:12], xs[12], xs[13:]
        segs = []
        for c_, p_, off, wd in zip(cur, prev_rows, SEG_OFF, SEG_W):
            rows = lax.broadcasted_iota(jnp.int32, c_.shape, 0)
            prev = jnp.where(rows == 0, p_, pltpu.roll(c_, 1, 0))
            segs.append(c_ + mu[:, off:off + wd] * (prev - c_))
        return (jnp.concatenate(segs, axis=1),) + tuple(_rwkv_pre(segs[1], segs[3], segs[4], segs[5], *prm))

    proj_segs = [(proj, wd, (CONV_COLS + off) // wd) for off, wd in zip(SEG_OFF, SEG_W)]
    u, lw, k_h, ra, rb, g = _rowwise(
        "rwkv_pre", pre_fwd, proj_segs, [w["shift_mu"]] + small, [(RW_PAD, F32)] + [(RWKV_DIM, F32)] * 5, halo=True)
    y_rec, rec_saved = _rec_fwd(u, lw, k_h, ra, rb)

    def late_weight(names, after):
        idx = [_LATE.index(n) for n in names]
        got = _gather_wait("late_gather_wait_" + names[0], [lg_send[i] for i in idx], [lg_recv[i] for i in idx],
                           [lg_shards[i] for i in idx], [lg_lands[i] for i in idx], after)
        return [_unshard(n, g_) for n, g_ in zip(names, got)]

    w["w_out"], w["w_up"], w["w_down"] = late_weight(["w_out", "w_up", "w_down"], [y_rec])
    post_c = [w["ln_x_g"], w["ln_x_b"], w["r_k"]]
    u_r, u_v = (u, 512, 0), (u, 512, 2)
    (y_rwkv,) = _rowwise("rwkv_post", lambda *xs: (_rwkv_post(*xs),), [y_rec, u_r, k_h, u_v, g], post_c, [(RWKV_DIM, BF16)],
                         tr=2 * ROW_BLOCK)
    ycat = jnp.concatenate([y_conv, y_rwkv], axis=1)
    def res_norm(acc, r_, g_):
        h = acc + r_
        return h, _rms(h, g_)

    h1, xn2 = _matmul("out_proj", ycat, w["w_out"], "nn", [F32, BF16], tm=1024, tn=D_MODEL, tk=D_MODEL, extras=[x],
                      consts=[w["norm_mlp_g"]], epilogue=res_norm)

    square = lambda h: h.astype(F32) * h.astype(F32)
    hid = _matmul("mlp_up", xn2, w["w_up"], "nn", [BF16], tm=2048, tn=1024, tk=D_MODEL,
                  epilogue=lambda acc: (jnp.maximum(acc, 0.0),))
    h2, xn3 = _matmul("mlp_down", hid, w["w_down"], "nn", [F32, BF16], tm=512, tn=D_MODEL, tk=D_FF, extras=[h1],
                      consts=[w["norm_ple_g"]], epilogue=res_norm, a_map=square)
    w["w_ple_gate"], w["w_ple_proj"] = late_weight(["w_ple_gate", "w_ple_proj"], [xn3])
    zg =_matmul("ple_gate", xn3, w["w_ple_gate"], "nn", [F32], tm=1024, tn=1024, tk=D_MODEL)
    pp = _matmul("ple_proj", p, w["w_ple_proj"], "nn", [F32], tm=1024, tn=1024, tk=PLE_DIM)

    def head(h2_, zg_, pp_, tg, gf):
        gate = _sigmoid(zg_)
        h3 = h2_ + gate * pp_
        out = _rms(h3, gf)
        err = out - tg
        dh3, dgf = _rms_bwd(h3, gf, err * (1.0 / D_MODEL))
        loss = jnp.sum(jnp.sum(err * err, axis=1, keepdims=True), axis=0, keepdims=True) * (0.5 / D_MODEL)
        return dh3, dh3 * pp_ * gate * (1.0 - gate), dh3 * gate, dgf, loss

    dh3, dzg, dpp, d_norm_final, loss = _rowwise(
        "head", head, [h2, zg, pp, tgt], [row(w["norm_final_g"])], [(D_MODEL, F32), (D_MODEL, BF16), (D_MODEL, BF16)],
        [(1, D_MODEL), (1, 1)], tr=2 * ROW_BLOCK)

    d_w_ple_proj = _matmul("d_ple_proj", p, dpp, "tn", [BF16], tm=PLE_DIM, tn=D_MODEL // N_DEV, tk=4096, col_blocks_out=True)
    d_w_ple_gate = _matmul("d_ple_gate", xn3, dzg, "tn", [BF16], tm=512, tn=1024, tk=4096)

    def norm_bwd(dxn, h, dres, g_):
        dh, dg = _rms_bwd(h, g_, dxn)
        dh = dh + dres
        return dh, dh, dg

    nb = dict(tm=512, tn=D_MODEL, epilogue=norm_bwd, sums=[(1, D_MODEL)])
    dh2, dh2_b, d_norm_ple = _matmul("dx_ple_gate", dzg, w["w_ple_gate"], "nt", [F32, BF16], tk=D_MODEL,
                                     extras=[h2, dh3], consts=[w["norm_ple_g"]], **nb)
    d_w_down = _matmul("d_mlp_down", hid, dh2_b, "tn", [BF16], tm=512, tn=1024, tk=4096, a_map=square)
    dpre = _matmul("dx_mlp_down", dh2_b, w["w_down"], "nt", [BF16], tm=2048, tn=1024, tk=D_MODEL, extras=[hid],
                   epilogue=lambda acc, hid_: (acc * (2.0 * hid_.astype(F32)),))
    d_w_up = _matmul("d_mlp_up", xn2, dpre, "tn", [BF16], tm=1024, tn=D_FF // N_DEV, tk=4096, col_blocks_out=True)
    dh1, dh1_b, d_norm_mlp = _matmul("dx_mlp_up", dpre, w["w_up"], "nt", [F32, BF16], tk=D_FF,
                                     extras=[h1, dh2], consts=[w["norm_mlp_g"]], **nb)
    d_w_out = _matmul("d_out_proj", ycat, dh1_b, "tn", [BF16], tm=512, tn=1024, tk=4096)
    dycat = _matmul("dx_out_proj", dh1_b, w["w_out"], "nt", [F32], tm=1024, tn=1024, tk=D_MODEL)
    late_grads = dict(w_out=d_w_out, w_up=d_w_up, w_down=d_w_down, w_ple_gate=d_w_ple_gate, w_ple_proj=d_w_ple_proj)
    late_send = [late_grads[n] if n in ("w_up", "w_ple_proj") else _reshard(n, late_grads[n]) for n in _LATE]
    *late_flight, late_token = _scatter_start("late_scatter_start", late_send, [lax.empty(a.shape, a.dtype) for a in late_send])
    conv_w_bwd = w["conv_w"] + late_token[0:1, 0:1]

    def conv_bwd(dy, blk, cw):
        gb, gc, hx = blk[:, :LANE], blk[:, LANE:2 * LANE], blk[:, 2 * LANE:]
        uu = gc * hx
        u1, u2 = _shift_down(uu, 1), _shift_down(uu, 2)
        dconv = dy * gb
        du = dconv * cw[2:3] + _shift_up(dconv, 1) * cw[1:2] + _shift_up(dconv, 2) * cw[0:1]
        s = lambda z: jnp.sum(z, axis=0, keepdims=True)
        d_blk = jnp.concatenate([dy * (uu * cw[2:3] + u1 * cw[1:2] + u2 * cw[0:1]), du * hx, du * gc], axis=1)
        return d_blk, s(dconv * u2), s(dconv * u1), s(dconv * uu)

    dproj, dcw0, dcw1, dcw2 = _colwise(
        "conv_bwd", conv_bwd, n_cb, [(dycat, LANE), (proj, 3 * LANE)], [conv_w_bwd],
        [(IN_PAD, BF16, 3 * LANE)], [(1, CONV_DIM)] * 3)

    def post_bwd(dy, y, r, k_h_, v, g_, ln_g, ln_b, r_k):
        _, vjp = jax.vjp(_rwkv_post, y, r, k_h_, v, g_, ln_g, ln_b, r_k)
        return vjp(dy)

    dy_rec, dr_p, dk_p, dv_p, dg, d_ln_g, d_ln_b, d_r_k = _rowwise(
        "rwkv_post_bwd", post_bwd, [(dycat, 512, 1), y_rec, u_r, k_h, u_v, g], post_c,
        [(RWKV_DIM, F32)] * 5, [(1, RWKV_DIM)] * 3, tr=2 * ROW_BLOCK)
    dr_r, dv_r, dlw, dk_r, da, db = _rec_bwd(u, lw, k_h, ra, rb, rec_saved, dy_rec)

    dproj, d_mu, d_w0, d_a0, d_k_k, d_k_a, d_wl, d_al, d_gl = _rwkv_pre_bwd(
        proj, u, [dr_p, dr_r, dv_p, dv_r, dlw, dk_p, dk_r, da, db, dg, lw], w["shift_mu"], small, dproj)
    d_w_in = _matmul("d_in_proj", xn1, dproj, "tn", [BF16], tm=1024, tn=896, tk=4096)
    early_grads = dict(conv_w=jnp.concatenate([dcw0, dcw1, dcw2], axis=0),
                       w_lora_up=d_wl[:64], a_lora_up=d_al[:64], g_lora_up=d_gl[:160])
    early_send = [_split_w_in_grad(d_w_in)] + [_reshard(n, early_grads[n]) for n in _EARLY[1:]]
    *early_flight, token = _scatter_start("early_scatter_start", early_send, [lax.empty(a.shape, a.dtype) for a in early_send])
    dx, d_norm_mix = _matmul(
        "dx_in_proj", dproj, w["w_in"], "nt", [F32], tk=IN_PAD, extras=[x, dh1], consts=[w["norm_mix_g"] + token[0:1, 0:1]],
        **dict(nb, epilogue=lambda *a: norm_bwd(*a)[1:]))

    grads = dict(
        norm_mix_g=d_norm_mix, shift_mu=d_mu, w0=d_w0, a0=d_a0, k_k=d_k_k, k_a=d_k_a, r_k=d_r_k,
        ln_x_g=d_ln_g, ln_x_b=d_ln_b, norm_mlp_g=d_norm_mlp, norm_ple_g=d_norm_ple, norm_final_g=d_norm_final)
    return loss, dx, grads, late_flight, early_flight, d_w_in


def _adam_update(partials, w_ref, m_ref, v_ref, g_ref, d_ref, nm_ref, nv_ref):
    g = partials[0].astype(F32)
    for part in partials[1:]:
        g = g + part.astype(F32)
    nm =ADAM_B1 * m_ref[...] + (1.0 - ADAM_B1) * g
    nv = ADAM_B2 * v_ref[...] + (1.0 - ADAM_B2) * (g * g)
    m_hat = nm / (1.0 - ADAM_B1 ** ADAM_STEP)
    v_hat = nv / (1.0 - ADAM_B2 ** ADAM_STEP)
    g_ref[...] = g
    d_ref[...] = -ADAM_LR * (m_hat / (jnp.sqrt(v_hat) + ADAM_EPS) + ADAM_WD * w_ref[...])
    nm_ref[...] = nm
    nv_ref[...] = nv


SMALL_ROWS = 8


def _small_layout(widths):
    widths = list(widths) + [1]
    fill, place = [0] * SMALL_ROWS, [None] * len(widths)
    for j in sorted(range(len(widths)), key=lambda q: -widths[q]):
        row = fill.index(min(fill))
        place[j] = (row, fill[row])
        fill[row] += -(-widths[j] // LANE) * LANE
    return place, max(fill)


def _pack_small(vecs, loss):
    place, total = _small_layout([v_.shape[1] for v_ in vecs])
    n = len(vecs)

    def body(*refs):
        out = jnp.zeros((SMALL_ROWS, total), F32)
        row_id = lax.broadcasted_iota(jnp.int32, (SMALL_ROWS, total), 0)
        for row in range(SMALL_ROWS):
            mine = sorted((off, j) for j, (r_, off) in enumerate(place) if r_ == row)
            pieces, at = [], 0
            for off, j in mine:
                val = refs[j][...]
                pieces.append(val)
                at = off + val.shape[1]
                pad = -val.shape[1] % LANE
                if pad:
                    pieces.append(jnp.zeros((1, pad), F32))
                    at += pad
            if total > at:
                pieces.append(jnp.zeros((1, total - at), F32))
            out = jnp.where(row_id == row, jnp.broadcast_to(jnp.concatenate(pieces, axis=1), (SMALL_ROWS, total)), out)
        refs[n + 1][...] = out

    return pl.pallas_call(body, name="pack_small", out_shape=jax.ShapeDtypeStruct((SMALL_ROWS, total), F32))(*vecs, loss)


def _adamw_small(packed, ws, ms, vs):
    n = len(ws)
    place, _ = _small_layout([w_.shape[1] for w_ in ws])

    def body(p_ref, *refs):
        w_refs, m_refs, v_refs, outs = refs[:n], refs[n:2 * n], refs[2 * n:3 * n], refs[3 * n:]
        for j in range(n):
            row, off = place[j]
            cols = pl.ds(off, ws[j].shape[1])
            _adam_update([p_ref[s, row:row + 1, cols] for s in range(N_DEV)], w_refs[j], m_refs[j], v_refs[j],
                         *outs[4 * j:4 * j + 4])
        row, off = place[n]
        total = p_ref[0, row:row + 1, off:off + 1]
        for s in range(1, N_DEV):
            total = total + p_ref[s, row:row + 1, off:off + 1]
        outs[4 * n][...] = total

    res = pl.pallas_call(
        body, name="adamw_small",
        out_shape=[jax.ShapeDtypeStruct(w_.shape, F32) for w_ in ws for _ in range(4)] + [jax.ShapeDtypeStruct((1, 1), F32)],
    )(packed, *ws, *ms, *vs)
    return [res[4 * j:4 * j + 4] for j in range(n)], res[4 * n]


def _adamw(name, parts, w, m, v, own=None, me=None):
    rows, cols = w.shape[-2:]
    lead = w.ndim - 2
    tr = rows if rows * cols * 4 * 8 <= (4 << 20) else max(8, (4 << 20) // (cols * 4 * 8) // 8 * 8)
    while rows % tr:
        tr -= 8
    shape4 = [jax.ShapeDtypeStruct(w.shape, F32)] * 4
    if own is None:
        def body(p_ref, *refs):
            _adam_update([p_ref[s] for s in range(N_DEV)], *refs)

        blk = pl.BlockSpec((None,) * lead + (tr, cols), lambda i: (0,) * lead + (i, 0))
        return pl.pallas_call(
            body, name=name, grid=(rows // tr,),
            in_specs=[pl.BlockSpec((N_DEV, tr, cols), lambda i: (0, i, 0)), blk, blk, blk], out_specs=[blk] * 4,
            out_shape=shape4, compiler_params=_params(("arbitrary",)),
        )(parts, w, m, v)

    def body_own(me_ref, p_ref, own_ref, *refs):
        mine = own_ref[...]
        _adam_update([jnp.where(me_ref[0] == s, mine, p_ref[s]) for s in range(N_DEV)], *refs)

    blk = pl.BlockSpec((None,) * lead + (tr, cols), lambda i, me_ref: (0,) * lead + (i, 0))
    return pl.pallas_call(
        body_own, name=name, out_shape=shape4,
        grid_spec=pltpu.PrefetchScalarGridSpec(
            num_scalar_prefetch=1, grid=(rows // tr,),
            in_specs=[pl.BlockSpec((N_DEV, tr, cols), lambda i, me_ref: (0, i, 0)),
                      pl.BlockSpec((None, tr, cols), lambda i, me_ref: (me_ref[0], i, 0)), blk, blk, blk],
            out_specs=[blk] * 4),
        compiler_params=_params(("arbitrary",)),
    )(me, parts, own, w, m, v)


def kernel(x, p, norm_mix_g, w_in, conv_w, shift_mu, w_lora_up, w0, a_lora_up, a0, g_lora_up, k_k, k_a, r_k, ln_x_g, ln_x_b, w_out, norm_mlp_g, w_up, w_down, norm_ple_g, w_ple_gate, w_ple_proj, norm_final_g, loss_target, m_norm_mix_g, m_w_in, m_conv_w, m_shift_mu, m_w_lora_up, m_w0, m_a_lora_up, m_a0, m_g_lora_up, m_k_k, m_k_a, m_r_k, m_ln_x_g, m_ln_x_b, m_w_out, m_norm_mlp_g, m_w_up, m_w_down, m_norm_ple_g, m_w_ple_gate, m_w_ple_proj, m_norm_final_g, v_norm_mix_g, v_w_in, v_conv_w, v_shift_mu, v_w_lora_up, v_w0, v_a_lora_up, v_a0, v_g_lora_up, v_k_k, v_k_a, v_r_k, v_ln_x_g, v_ln_x_b, v_w_out, v_norm_mlp_g, v_w_up, v_w_down, v_norm_ple_g, v_w_ple_gate, v_w_ple_proj, v_norm_final_g):
    args = dict(locals())
    wts = {n: args[n] for n in _WEIGHTS}
    mom = {n: args["m_" + n] for n in _WEIGHTS}
    var = {n: args["v_" + n] for n in _WEIGHTS}
    shard2d = lambda a: a.reshape(a.shape[-2:])
    pad_mu = lambda a: _pad_in_cols(jnp.concatenate([jnp.zeros((1, CONV_COLS), F32), a], axis=1))[:, CONV_COLS:]
    unpad_mu = lambda a: _unpad_in_cols(jnp.concatenate([jnp.zeros((1, CONV_COLS), F32), a], axis=1))[:, CONV_COLS:]

    shards = {n: shard2d(wts[n]).astype(BF16 if n in _BF16_GATHER else F32) for n in _SHARDED}
    w = {n: wts[n].reshape(1, -1) for n in _REPLICATED}
    w["shift_mu"] = pad_mu(wts["shift_mu"])

    loss, dx, grads, late_flight, early_flight, d_w_in = _local_step(
        x[0], p[0, 0], loss_target[0], w, [shards[n] for n in _EARLY], [shards[n] for n in _LATE])

    me = (4 * lax.axis_index("x") + 2 * lax.axis_index("y") + lax.axis_index("c")).astype(jnp.int32).reshape(1)
    late_sent, late_parts = _scatter_wait("late_scatter_wait", *late_flight, after=[d_w_in])
    out = {n: _adamw("adamw_" + n, prt, wts[n], mom[n], var[n], own=own, me=me)
           for n, prt, own in zip(_LATE, late_parts, late_sent)}
    early_sent, early_parts = _scatter_wait("early_scatter_wait", *early_flight, after=[dx] + [out[n][1] for n in _LATE])
    for n, prt, own in zip(_EARLY, early_parts, early_sent):
        out[n] = _adamw("adamw_" + n, prt, wts[n], mom[n], var[n], own=own, me=me)

    grads["shift_mu"] = unpad_mu(grads["shift_mu"])
    flat = lambda a: a.reshape(1, -1)
    (small_parts,) = _all_gather("gather_small", [_pack_small([flat(grads[n]) for n in _REPLICATED], loss)])
    small, loss_total = _adamw_small(small_parts, *[[flat(d[n]) for n in _REPLICATED] for d in (wts, mom, var)])
    for n, res in zip(_REPLICATED, small):
        out[n] = [r.reshape(wts[n].shape) for r in res]
    return (loss_total[0, 0], dx[None], *[out[n][0] for n in _WEIGHTS], *[out[n][1] for n in _WEIGHTS],
            *[out[n][2] for n in _WEIGHTS], *[out[n][3] for n in _WEIGHTS])
```

```python
import functools

import jax
import jax.numpy as jnp
from jax import lax
from jax.experimental import pallas as pl
from jax.experimental.pallas import tpu as pltpu

F32 = jnp.float32
BF16 = jnp.bfloat16

N_DEV = 8
D_MODEL = 1024
CONV_DIM = 512
RWKV_DIM = 512
HEAD_DIM = 64
N_HEADS = 8
D_FF = 4096
PLE_DIM = 256
RMS_EPS = 1e-6
GN_EPS = 64e-5
L2_EPS = 1e-12
ADAM_LR, ADAM_B1, ADAM_B2, ADAM_EPS, ADAM_WD, ADAM_STEP = 0.001, 0.9, 0.999, 1e-08, 0.01, 10

CONV_COLS = 3 * CONV_DIM
RW_PAD = 2048
IN_PAD = CONV_COLS + RW_PAD
IN_COLS = 3360
XW_OFF, XA_OFF, XG_OFF = 1536, 1664, 1792
REC_CHUNK = 128
REC_CHUNKS_PER_STEP = 2
REC_PASSES = 1
ROW_BLOCK = 256
LANE = 128
VMEM_LIMIT = 56 * 1024 * 1024


def _dims(dn, ndim):
    if ndim == 3:
        return {"nn": (((2,), (1,)), ((0,), (0,))), "nt": (((2,), (2,)), ((0,), (0,))),
                "tn": (((1,), (1,)), ((0,), (0,)))}[dn]
    return {"nn": (((1,), (0,)), ((), ())), "nt": (((1,), (1,)), ((), ())), "tn": (((0,), (0,)), ((), ()))}[dn]


def _split2(x):
    hi = x.astype(BF16)
    return hi, (x - hi.astype(F32)).astype(BF16)


def _mm_raw(x, y, dn, passes):
    f = lambda p, q: lax.dot_general(p, q, _dims(dn, x.ndim), preferred_element_type=F32)
    if passes == 1:
        return f(x.astype(BF16), y.astype(BF16))
    xh, xl = _split2(x)
    yh, yl = _split2(y)
    if passes == 2:
        return f(xh, yh) + f(xh, yl)
    return f(xh, yh) + f(xh, yl) + f(xl, yh)


@functools.partial(jax.custom_vjp, nondiff_argnums=(2, 3))
def _mm(x, y, dn, passes):
    return _mm_raw(x, y, dn, passes)


def _mm_fwd(x, y, dn, passes):
    return _mm_raw(x, y, dn, passes), (x, y)


def _mm_bwd(dn, passes, res, d):
    x, y = res
    if dn == "nn":
        return _mm(d, y, "nt", passes), _mm(x, d, "tn", passes)
    if dn == "nt":
        return _mm(d, y, "nn", passes), _mm(d, x, "tn", passes)
    return _mm(y, d, "nt", passes), _mm(x, d, "nn", passes)


_mm.defvjp(_mm_fwd, _mm_bwd)


HSUM_COLS = 256


def _head_ones():
    i = lax.broadcasted_iota(jnp.int32, (HSUM_COLS, HSUM_COLS), 0) // HEAD_DIM
    j = lax.broadcasted_iota(jnp.int32, (HSUM_COLS, HSUM_COLS), 1) // HEAD_DIM
    return (i == j).astype(BF16)


def _hsum_raw(x):
    ones = _head_ones()
    f = lambda p: jnp.concatenate(
        [lax.dot_general(p[:, c:c + HSUM_COLS], ones, _dims("nn", 2), preferred_element_type=F32)
         for c in range(0, p.shape[1], HSUM_COLS)], axis=1)
    x1, x2 = _split2(x)
    return f(x1) + f(x2)


@jax.custom_vjp
def _hsum(x):
    return _hsum_raw(x)


_hsum.defvjp(lambda x: (_hsum_raw(x), None), lambda _, d: (_hsum(d),))


def _sigmoid(x):
    return 0.5 + 0.5 * jnp.tanh(0.5 * x)


def _softplus(x):
    return jnp.maximum(x, 0.0) + jnp.log(1.0 + jnp.exp(-jnp.abs(x)))


def _params(sem):
    return pltpu.CompilerParams(dimension_semantics=sem, vmem_limit_bytes=VMEM_LIMIT)


def _rowwise(name, fn, rows, consts, row_outs, acc_outs=(), tr=ROW_BLOCK, halo=False, gather=()):
    rows = [r if isinstance(r, tuple) else (r, r.shape[1], 0) for r in rows]
    t_len = rows[0][0].shape[0]
    tr = min(tr, t_len)
    n_r, n_c, n_o, n_a, n_x = len(rows), len(consts), len(row_outs), len(acc_outs), len(gather)
    n_h = n_r if halo else 0
    sub = 8
    x_specs, x_shapes, x_sems = _gather_io(gather) if n_x else ([], [], [])
    nb = t_len // tr

    def body(*refs):
        if n_x:
            n_in = n_r + n_h + n_c
            start, forward, wait = _gather_plan(refs[n_in:n_in + n_x], refs[len(refs) - 3 - n_x:len(refs) - 3], *refs[len(refs) - 3:])
            pl.when(pl.program_id(0) == 0)(start)
            refs = refs[:n_in] + refs[n_in + n_x:len(refs) - 3 - n_x]
        ins = [r[...] for r in refs[:n_r]]
        ins += [jnp.where(pl.program_id(0) == 0, 0.0, r[sub - 1:sub, :]) for r in refs[n_r:n_r + n_h]]
        ins += [r[...] for r in refs[n_r + n_h:n_r + n_h + n_c]]
        refs = refs[:n_r] + refs[n_r + n_h:]
        outs = fn(*ins)
        o_refs = refs[n_r + n_c:n_r + n_c + n_o]
        a_refs = refs[n_r + n_c + n_o:]
        for o_ref, val in zip(o_refs, outs[:n_o]):
            o_ref[...] = val.astype(o_ref.dtype)
        if n_a:
            first = pl.program_id(0) == 0

            @pl.when(first)
            def _():
                for a_ref, val in zip(a_refs, outs[n_o:]):
                    a_ref[...] = val

            @pl.when(jnp.logical_not(first))
            def _():
                for a_ref, val in zip(a_refs, outs[n_o:]):
                    a_ref[...] += val

        if n_x:
            @pl.when(pl.program_id(0) == nb - 1)
            def _():
                for j in range(n_x):
                    forward(j)
                wait()

    in_specs = [pl.BlockSpec((tr, w), functools.partial(lambda i, c: (i, c), c=cb)) for _, w, cb in rows]
    if halo:
        in_specs += [pl.BlockSpec((sub, w), functools.partial(lambda i, c: (jnp.maximum(i * (tr // sub) - 1, 0), c), c=cb))
                     for _, w, cb in rows]
    in_specs += [pl.BlockSpec(c.shape, functools.partial(lambda i, n: (0,) * n, n=c.ndim)) for c in consts]
    out_specs = [pl.BlockSpec((tr, w), lambda i: (i, 0)) for w, _ in row_outs]
    out_specs += [pl.BlockSpec(s, functools.partial(lambda i, n: (0,) * n, n=len(s))) for s in acc_outs]
    out_shape = [jax.ShapeDtypeStruct((t_len, w), dt) for w, dt in row_outs]
    out_shape += [jax.ShapeDtypeStruct(s, F32) for s in acc_outs]
    return pl.pallas_call(
        body, name=name, grid=(nb,), in_specs=in_specs + x_specs, out_specs=out_specs + x_specs,
        out_shape=out_shape + x_shapes, scratch_shapes=x_sems,
        compiler_params=pltpu.CompilerParams(dimension_semantics=("arbitrary",), vmem_limit_bytes=VMEM_LIMIT,
                                             has_side_effects=bool(n_x)),
    )(*[r[0] for r in rows], *([r[0] for r in rows] if halo else []), *consts, *gather)


def _colwise(name, fn, n_blocks, cols, prms, col_outs, prm_outs=()):
    t_len = cols[0][0].shape[0]
    n_i = len(cols) + len(prms)

    def body(*refs):
        outs = fn(*[r[...] for r in refs[:n_i]])
        for o_ref, val in zip(refs[n_i:], outs):
            o_ref[...] = val.astype(o_ref.dtype)

    spec = lambda r, w: pl.BlockSpec((r, w), lambda j: (0, j))
    in_specs = [spec(t_len, w) for _, w in cols] + [spec(a.shape[0], LANE) for a in prms]
    out_specs = [spec(t_len, bw) for _, _, bw in col_outs] + [spec(r, LANE) for r, _ in prm_outs]
    out_shape = [jax.ShapeDtypeStruct((t_len, w), dt) for w, dt, _ in col_outs]
    out_shape += [jax.ShapeDtypeStruct((r, w), F32) for r, w in prm_outs]
    return pl.pallas_call(
        body, name=name, grid=(n_blocks,), in_specs=in_specs, out_specs=out_specs, out_shape=out_shape,
        compiler_params=_params(("arbitrary",)),
    )(*[c[0] for c in cols], *prms)


def _matmul(name, a, b, dn, outs, *, tm, tn, tk, extras=(), consts=(), epilogue=None, sums=(), a_map=None,
            col_blocks_out=False):
    if dn == "nn":
        (m, k), n = a.shape, b.shape[1]
    elif dn == "nt":
        (m, k), n = a.shape, b.shape[0]
    else:
        (k, m), n = a.shape, b.shape[1]
    tm, tn, tk = min(tm, m), min(tn, n), min(tk, k)
    nk = k // tk
    grid = (m // tm, n // tn, nk)
    assert nk == 1 and (not sums or grid[1] == 1)
    a_spec = pl.BlockSpec((tk, tm), lambda i, j, q: (q, i)) if dn == "tn" else pl.BlockSpec((tm, tk), lambda i, j, q: (i, q))
    b_spec = pl.BlockSpec((tn, tk), lambda i, j, q: (j, q)) if dn == "nt" else pl.BlockSpec((tk, tn), lambda i, j, q: (q, j))
    o_spec = pl.BlockSpec((tm, tn), lambda i, j, q: (i, j))
    c_spec = pl.BlockSpec((1, tn), lambda i, j, q: (0, j))
    n_e, n_c, n_o, n_s = len(extras), len(consts), len(outs), len(sums)

    def body(*refs):
        a_ref, b_ref = refs[:2]
        e_refs = refs[2:2 + n_e + n_c]
        o_refs, s_refs = refs[2 + n_e + n_c:2 + n_e + n_c + n_o], refs[2 + n_e + n_c + n_o:]
        step = pl.program_id(0) * grid[1] + pl.program_id(1)
        a_blk = a_ref[...] if a_map is None else a_map(a_ref[...])
        acc = lax.dot_general(a_blk.astype(BF16), b_ref[...].astype(BF16), _dims(dn, 2), preferred_element_type=F32)
        vals = (acc,) if epilogue is None else epilogue(acc, *[e[...] for e in e_refs])
        for o_ref, val in zip(o_refs, vals[:n_o]):
            o_ref[...] = val.astype(o_ref.dtype)
        if n_s:
            @pl.when(step == 0)
            def _():
                for s_ref, val in zip(s_refs, vals[n_o:]):
                    s_ref[...] = val

            @pl.when(step > 0)
            def _():
                for s_ref, val in zip(s_refs, vals[n_o:]):
                    s_ref[...] += val

    res = pl.pallas_call(
        body, name=name, grid=grid,
        in_specs=[a_spec, b_spec] + [o_spec] * n_e + [c_spec] * n_c,
        out_specs=[pl.BlockSpec((None, tm, tn), lambda i, j, q: (j, i, 0)) if col_blocks_out else o_spec] * n_o
                  + [c_spec] * n_s,
        out_shape=[jax.ShapeDtypeStruct((n // tn, m, tn) if col_blocks_out else (m, n), dt) for dt in outs]
                  + [jax.ShapeDtypeStruct(s, F32) for s in sums],
        compiler_params=_params(("arbitrary",) * 3 if n_s else ("parallel", "parallel", "arbitrary")),
    )(a, b, *extras, *consts)
    return res[0] if len(res) == 1 else res


def _rms(h, g):
    return h * lax.rsqrt(jnp.mean(h * h, axis=-1, keepdims=True) + RMS_EPS) * g


def _rms_bwd(h, g, dy):
    rs = lax.rsqrt(jnp.mean(h * h, axis=-1, keepdims=True) + RMS_EPS)
    n = h * rs
    dn = dy * g
    dh = rs * (dn - n * jnp.mean(dn * n, axis=-1, keepdims=True))
    return dh, jnp.sum(dy * n, axis=0, keepdims=True)


def _rwkv_pre(k, xw, xa, xg, w0, a0, k_k, k_a, wl, al, gl):
    zw = w0 + _mm(jnp.tanh(xw), wl, "nn", 1)
    lw = -jnp.exp(-_softplus(-zw) - 0.5)
    iclr = _sigmoid(a0 + _mm(xa, al, "nn", 1))
    g = _mm(_sigmoid(xg), gl, "nn", 1)
    kk0 = k * k_k
    kk = kk0 * lax.rsqrt(jnp.maximum(_hsum(kk0 * kk0), L2_EPS * L2_EPS))
    k_h = k * (1.0 + (iclr - 1.0) * k_a)
    return lw, k_h, -kk, kk * iclr, g


def _rwkv_pre_vjp(k, xw, xa, xg, lw, w0, a0, k_k, k_a, wl, al, gl, dlw, dk_h, da, db, dg):
    del w0
    s0 = lambda z: jnp.sum(z, axis=0, keepdims=True)
    iclr = _sigmoid(a0 + _mm(xa, al, "nn", 1))
    kk0 = k * k_k
    s = _hsum(kk0 * kk0)
    inv = lax.rsqrt(jnp.maximum(s, L2_EPS * L2_EPS))
    kk = kk0 * inv
    dkk = db * iclr - da
    diclr = db * kk + dk_h * (k * k_a)
    dkk0 = dkk * inv - jnp.where(s > L2_EPS * L2_EPS, inv * inv * inv * _hsum(dkk * kk0), 0.0) * kk0
    dk = dk_h * (1.0 + (iclr - 1.0) * k_a) + dkk0 * k_k
    dza = diclr * iclr * (1.0 - iclr)
    dzw = dlw * lw * (1.0 + lw * 1.6487212707001282)
    th = jnp.tanh(xw)
    sg = _sigmoid(xg)
    dxw = _mm(dzw, wl, "nt", 1) * (1.0 - th * th)
    dxa = _mm(dza, al, "nt", 1)
    dxg = _mm(dg, gl, "nt", 1) * sg * (1.0 - sg)
    return (dk, dxw, dxa, dxg, s0(dzw), s0(dza), s0(dkk0 * k), s0(dk_h * k * (iclr - 1.0)),
            _mm(th, dzw, "tn", 1), _mm(xa, dza, "tn", 1), _mm(sg, dg, "tn", 1))


def _rwkv_post(y, r, k_h, v, g, ln_g, ln_b, r_k):
    mu = _hsum(y) * (1.0 / HEAD_DIM)
    yc = y - mu
    var = _hsum(yc * yc) * (1.0 / HEAD_DIM)
    yo = yc * lax.rsqrt(var + GN_EPS) * ln_g + ln_b
    bonus = _hsum(r * k_h * r_k) * v
    return (yo + bonus) * g


def _shift_down(x, n):
    rows = lax.broadcasted_iota(jnp.int32, x.shape, 0)
    return jnp.where(rows < n, 0.0, pltpu.roll(x, n, 0))


def _shift_up(x, n):
    t_len = x.shape[0]
    rows = lax.broadcasted_iota(jnp.int32, x.shape, 0)
    return jnp.where(rows >= t_len - n, 0.0, pltpu.roll(x, t_len - n, 0))


def _gather_plan(ins, outs, send_sems, recv_sems, local_sems):
    x, y, c = lax.axis_index("x"), lax.axis_index("y"), lax.axis_index("c")
    me = 4 * x + 2 * y + c
    direct, chips = (1, 2, 4, 6), (2, 4, 6)

    def local(i):
        return pltpu.make_async_copy(ins[i], outs[i].at[me], local_sems.at[i])

    def send(i, rel):
        return pltpu.make_async_remote_copy(
            src_ref=ins[i], dst_ref=outs[i].at[me], send_sem=send_sems.at[i, rel - 1], recv_sem=recv_sems.at[i, rel - 1],
            device_id=(x ^ (rel >> 2), y ^ ((rel >> 1) & 1), c ^ (rel & 1)), device_id_type=pl.DeviceIdType.MESH)

    def passed(i, rel):
        slot = outs[i].at[me ^ rel]
        return pltpu.make_async_remote_copy(
            src_ref=slot, dst_ref=slot, send_sem=send_sems.at[i, rel], recv_sem=recv_sems.at[i, rel],
            device_id=(x, y, 1 - c), device_id_type=pl.DeviceIdType.MESH)

    def landed(i, rel):
        slot = outs[i].at[me ^ rel]
        return pltpu.make_async_remote_copy(
            src_ref=slot, dst_ref=slot, send_sem=send_sems.at[i, rel - 1], recv_sem=recv_sems.at[i, rel - 1],
            device_id=(x, y, c), device_id_type=pl.DeviceIdType.MESH)

    def start():
        for i in range(len(ins)):
            local(i).start()
            for rel in direct:
                send(i, rel).start()

    def forward(i):
        for rel in chips:
            landed(i, rel).wait_recv()
            passed(i, rel).start()

    def wait():
        for i in range(len(ins)):
            local(i).wait()
            for rel in (1, 3, 5, 7):
                landed(i, rel).wait_recv()
            for rel in direct:
                send(i, rel).wait_send()
            for rel in chips:
                passed(i, rel).wait_send()

    return start, forward, wait


def _gather_io(arrays):
    n = len(arrays)
    any_spec = pl.BlockSpec(memory_space=pl.ANY)
    out_shape = [jax.ShapeDtypeStruct((N_DEV,) + a.shape, a.dtype) for a in arrays]
    sems = [pltpu.SemaphoreType.DMA((n, N_DEV - 1)), pltpu.SemaphoreType.DMA((n, N_DEV - 1)), pltpu.SemaphoreType.DMA((n,))]
    return [any_spec] * n, out_shape, sems


def _all_gather(name, arrays):
    n = len(arrays)
    specs, out_shape, sems = _gather_io(arrays)

    def body(*refs):
        start, forward, wait = _gather_plan(refs[:n], refs[n:2 * n], *refs[2 * n:])
        start()
        for i in range(n):
            forward(i)
        wait()

    return pl.pallas_call(
        body, name=name, in_specs=specs, out_specs=specs, out_shape=out_shape, scratch_shapes=sems,
        compiler_params=pltpu.CompilerParams(has_side_effects=True),
    )(*arrays)


def _scatter_start(name, arrays, lands):
    n = len(arrays)
    hbm = pl.BlockSpec(memory_space=pltpu.HBM)

    def body(*refs):
        ins, land, send_sems, recv_sems = refs[:n], refs[n:2 * n], refs[2 * n], refs[2 * n + 1]
        token = refs[4 * n + 2]
        x, y, c = lax.axis_index("x"), lax.axis_index("y"), lax.axis_index("c")
        me = 4 * x + 2 * y + c
        for i in range(n):
            for rel in range(1, N_DEV):
                k = i * (N_DEV - 1) + rel - 1
                pltpu.make_async_remote_copy(
                    src_ref=ins[i].at[me ^ rel], dst_ref=land[i].at[me], send_sem=send_sems.at[k],
                    recv_sem=recv_sems.at[k], device_id=(x ^ (rel >> 2), y ^ ((rel >> 1) & 1), c ^ (rel & 1)),
                    device_id_type=pl.DeviceIdType.MESH).start()
        token[...] = jnp.zeros_like(token)

    sem = pltpu.SemaphoreType.DMA((n * (N_DEV - 1),))
    bufs = [pltpu.HBM(a.shape, a.dtype) for a in list(arrays) + list(lands)]
    res = pl.pallas_call(
        body, name=name, out_shape=(sem, sem, *bufs, jax.ShapeDtypeStruct((8, LANE), F32)),
        in_specs=[hbm] * (2 * n),
        out_specs=(pl.BlockSpec(memory_space=pltpu.SEMAPHORE),) * 2 + (hbm,) * (2 * n) + (pl.BlockSpec(memory_space=pltpu.VMEM),),
        input_output_aliases={i: 2 + i for i in range(2 * n)},
        compiler_params=pltpu.CompilerParams(has_side_effects=pltpu.SideEffectType.DATAFLOW_SIDE_EFFECTING),
    )(*[pltpu.with_memory_space_constraint(a, pltpu.HBM) for a in list(arrays) + list(lands)])
    return res[0], res[1], res[2:2 + n], res[2 + n:2 + 2 * n], res[2 + 2 * n]


def _scatter_wait(name, send_sems, recv_sems, arrays, lands, after):
    n, n_after = len(arrays), len(after)
    hbm = pl.BlockSpec(memory_space=pltpu.HBM)

    def body(*refs):
        ins, land, s_sems, r_sems = refs[:n], refs[n:2 * n], refs[2 * n], refs[2 * n + 1]
        x, y, c = lax.axis_index("x"), lax.axis_index("y"), lax.axis_index("c")
        me = 4 * x + 2 * y + c
        for i in range(n):
            for rel in range(1, N_DEV):
                k = i * (N_DEV - 1) + rel - 1
                cp = pltpu.make_async_remote_copy(
                    src_ref=ins[i].at[me ^ rel], dst_ref=land[i].at[me ^ rel], send_sem=s_sems.at[k],
                    recv_sem=r_sems.at[k], device_id=(x, y, c), device_id_type=pl.DeviceIdType.MESH)
                cp.wait_send()
                cp.wait_recv()

    res = pl.pallas_call(
        body, name=name, out_shape=[pltpu.HBM(a.shape, a.dtype) for a in list(arrays) + list(lands)],
        in_specs=[hbm] * (2 * n) + [pl.BlockSpec(memory_space=pltpu.SEMAPHORE)] * 2 + [pl.BlockSpec(memory_space=pl.ANY)] * n_after,
        out_specs=[hbm] * (2 * n), input_output_aliases={i: i for i in range(2 * n)},
        compiler_params=pltpu.CompilerParams(has_side_effects=pltpu.SideEffectType.DATAFLOW_SIDE_EFFECTING),
    )(*arrays, *lands, send_sems, recv_sems, *after)
    return res[:n], res[n:]


def _gather_start(name, shards, after):
    n, n_after = len(shards), len(after)
    hbm = pl.BlockSpec(memory_space=pltpu.HBM)
    lands = [lax.empty((N_DEV,) + a.shape, a.dtype) for a in shards]

    def body(*refs):
        ins, land = refs[:n], refs[n:2 * n]
        outs = refs[2 * n + n_after:]
        send_sems, recv_sems, token = outs[:n], outs[n:2 * n], outs[4 * n]
        x, y, c = lax.axis_index("x"), lax.axis_index("y"), lax.axis_index("c")
        me = 4 * x + 2 * y + c
        for i in range(n):
            for rel in range(N_DEV):
                pltpu.make_async_remote_copy(
                    src_ref=ins[i], dst_ref=land[i].at[me], send_sem=send_sems[i].at[rel],
                    recv_sem=recv_sems[i].at[rel], device_id=(x ^ (rel >> 2), y ^ ((rel >> 1) & 1), c ^ (rel & 1)),
                    device_id_type=pl.DeviceIdType.MESH).start()
        token[...] = jnp.zeros_like(token)

    sem = pltpu.SemaphoreType.DMA((N_DEV,))
    bufs = [pltpu.HBM(a.shape, a.dtype) for a in list(shards) + lands]
    res = pl.pallas_call(
        body, name=name, out_shape=(*[sem] * (2 * n), *bufs, jax.ShapeDtypeStruct((8, LANE), F32)),
        in_specs=[hbm] * (2 * n) + [pl.BlockSpec(memory_space=pl.ANY)] * n_after,
        out_specs=(pl.BlockSpec(memory_space=pltpu.SEMAPHORE),) * (2 * n) + (hbm,) * (2 * n)
                  + (pl.BlockSpec(memory_space=pltpu.VMEM),),
        input_output_aliases={i: 2 * n + i for i in range(2 * n)},
        compiler_params=pltpu.CompilerParams(has_side_effects=pltpu.SideEffectType.DATAFLOW_SIDE_EFFECTING),
    )(*[pltpu.with_memory_space_constraint(a, pltpu.HBM) for a in list(shards) + lands], *after)
    return res[:n], res[n:2 * n], res[2 * n:3 * n], res[3 * n:4 * n], res[4 * n]


def _gather_wait(name, send_sems, recv_sems, shards, lands, after):
    n, n_after = len(shards), len(after)
    hbm = pl.BlockSpec(memory_space=pltpu.HBM)

    def body(*refs):
        ins, land = refs[:n], refs[n:2 * n]
        s_sems, r_sems = refs[2 * n:3 * n], refs[3 * n:4 * n]
        x, y, c = lax.axis_index("x"), lax.axis_index("y"), lax.axis_index("c")
        me = 4 * x + 2 * y + c
        for i in range(n):
            for rel in range(N_DEV):
                cp = pltpu.make_async_remote_copy(
                    src_ref=ins[i], dst_ref=land[i].at[me ^ rel], send_sem=s_sems[i].at[rel],
                    recv_sem=r_sems[i].at[rel], device_id=(x, y, c), device_id_type=pl.DeviceIdType.MESH)
                cp.wait_send()
                cp.wait_recv()

    res = pl.pallas_call(
        body, name=name, out_shape=[pltpu.HBM(a.shape, a.dtype) for a in list(shards) + list(lands)],
        in_specs=[hbm] * (2 * n) + [pl.BlockSpec(memory_space=pltpu.SEMAPHORE)] * (2 * n)
                 + [pl.BlockSpec(memory_space=pl.ANY)] * n_after,
        out_specs=[hbm] * (2 * n), input_output_aliases={i: i for i in range(2 * n)},
        compiler_params=pltpu.CompilerParams(has_side_effects=pltpu.SideEffectType.DATAFLOW_SIDE_EFFECTING),
    )(*shards, *lands, *send_sems, *recv_sems, *after)
    return res[n:]


def _tri_powers(low):
    powers, n, p = [low.astype(BF16)], 1, low
    while 2 * n < low.shape[-1]:
        p = _mm(p, p, "nn", REC_PASSES)
        powers.append(p.astype(BF16))
        n *= 2
    return powers


@jax.custom_vjp
def _tri_solve(low, rhs, powers):
    del low
    for p in powers:
        rhs = rhs + _mm(p, rhs, "nn", REC_PASSES)
    return rhs


def _tri_solve_fwd(low, rhs, powers):
    out = _tri_solve(low, rhs, powers)
    return out, (powers, out)


def _tri_solve_bwd(res, d):
    powers, u = res
    for p in powers:
        d = d + _mm(p, d, "tn", REC_PASSES)
    return _mm(d, u, "nt", REC_PASSES), d, [jnp.zeros_like(p) for p in powers]


_tri_solve.defvjp(_tri_solve_fwd, _tri_solve_bwd)


@jax.custom_vjp
def _tri_solve_given(low, rhs, powers, value):
    del low, rhs, powers
    return value


def _tri_solve_given_fwd(low, rhs, powers, value):
    return value, (powers, value)


def _tri_solve_given_bwd(res, d):
    return _tri_solve_bwd(res, d) + (jnp.zeros_like(res[1]),)


_tri_solve_given.defvjp(_tri_solve_given_fwd, _tri_solve_given_bwd)


def _heads(x):
    return jnp.stack([x[:, h * HEAD_DIM:(h + 1) * HEAD_DIM] for h in range(N_HEADS)])


def _unheads(x):
    return jnp.concatenate([x[h] for h in range(N_HEADS)], axis=-1)


def _causal_masks(c):
    ti = lax.broadcasted_iota(jnp.int32, (c, c), 0)
    si = lax.broadcasted_iota(jnp.int32, (c, c), 1)
    strict, incl = si < ti, si <= ti
    both = jnp.concatenate([jnp.concatenate([strict, strict], axis=1), jnp.concatenate([incl, incl], axis=1)], axis=0)
    return strict, incl, both


@jax.custom_vjp
def _gram_given(x2, y2, value):
    del x2, y2
    return value.astype(F32)


def _gram_given_fwd(x2, y2, value):
    return value.astype(F32), (x2, y2, value)


def _gram_given_bwd(res, d):
    x2, y2, value = res
    d = jnp.where(_causal_masks(d.shape[-1] // 2)[2], d, 0.0)
    return _mm(d, y2, "nn", 2), _mm(d, x2, "tn", 2), jnp.zeros_like(value)


_gram_given.defvjp(_gram_given_fwd, _gram_given_bwd)


def _chunk_fwd(z0, r, lw, k, v, a, b, powers=None, gram_value=None, u_value=None):
    c = r.shape[0]
    n_h, n_k = z0.shape[0], z0.shape[1]
    mm = functools.partial(_mm, passes=REC_PASSES)
    gram = functools.partial(_mm, passes=2)
    _, incl, mask = _causal_masks(c)
    cum = _mm(incl.astype(F32), lw, "nn", 3)
    cum_end = cum[c - 1:c, :]
    e_neg, e_end = jnp.exp(-cum), jnp.exp(cum_end - cum)
    x2 = jnp.concatenate([_heads(a * jnp.exp(cum - lw)), _heads(r * jnp.exp(cum))], axis=1)
    y2 = jnp.concatenate([_heads(b * e_neg), _heads(k * e_neg)], axis=1)
    vh = _heads(v)
    g2 = jnp.where(mask, gram(x2, y2, "nt"), 0.0) if gram_value is None else _gram_given(x2, y2, gram_value)
    t2 = mm(x2, z0, "nn") + mm(g2[:, :, c:], vh, "nn")
    low = g2[:, :c, :c]
    powers = _tri_powers(low) if powers is None else powers
    u = _tri_solve(low, t2[:, :c], powers) if u_value is None else _tri_solve_given(low, t2[:, :c], powers, u_value)
    y = t2[:, c:] + mm(g2[:, c:, :c], u, "nn")
    ki = lax.broadcasted_iota(jnp.int32, (n_k, n_k), 0)
    kj = lax.broadcasted_iota(jnp.int32, (n_k, n_k), 1)
    dmat = jnp.where(ki == kj, jnp.broadcast_to(_heads(jnp.exp(cum_end)), (n_h, n_k, n_k)), 0.0)
    z_end = mm(dmat, z0, "nn") + mm(jnp.concatenate([_heads(b * e_end), _heads(k * e_end)], axis=1),
                                    jnp.concatenate([u, vh], axis=1), "tn")
    return _unheads(y), z_end, powers, g2, u


def _rec_fwd(u, lw, k, a, b):
    t_len = lw.shape[0]
    c = min(REC_CHUNK, t_len)
    nc = t_len // c
    per = REC_CHUNKS_PER_STEP if nc % REC_CHUNKS_PER_STEP == 0 else 1
    steps = nc // per
    n_pow = max(1, (c - 1).bit_length())

    def body(r_ref, v_ref, lw_ref, k_ref, a_ref, b_ref, y_ref, zs_ref, pw_ref, gs_ref, us_ref, z_scr):
        @pl.when(pl.program_id(0) == 0)
        def _():
            z_scr[...] = jnp.zeros_like(z_scr)

        for s in range(per):
            rows = pl.ds(s * c, c)
            z0 = z_scr[...]
            zs_ref[s] = z0
            y, z_end, powers, g2, u_rows = _chunk_fwd(z0, r_ref[rows, :], lw_ref[rows, :], k_ref[rows, :], v_ref[rows, :],
                                                      a_ref[rows, :], b_ref[rows, :])
            y_ref[rows, :] = y
            z_scr[...] = z_end
            pw_ref[s] = jnp.concatenate(powers, axis=0)
            gs_ref[s] = g2.astype(BF16)
            us_ref[s] = u_rows

    blk = lambda cb: pl.BlockSpec((per * c, RWKV_DIM), functools.partial(lambda i, q: (i, q), q=cb))
    res = pl.pallas_call(
        body, name="rwkv_rec_fwd", grid=(steps,),
        in_specs=[blk(0), blk(2)] + [blk(0)] * 4,
        out_specs=[blk(0), pl.BlockSpec((per, N_HEADS, HEAD_DIM, HEAD_DIM), lambda i: (i, 0, 0, 0)),
                   pl.BlockSpec((per, n_pow * N_HEADS, c, c), lambda i: (i, 0, 0, 0)),
                   pl.BlockSpec((per, N_HEADS, 2 * c, 2 * c), lambda i: (i, 0, 0, 0)),
                   pl.BlockSpec((per, N_HEADS, c, HEAD_DIM), lambda i: (i, 0, 0, 0))],
        out_shape=[jax.ShapeDtypeStruct((t_len, RWKV_DIM), F32),
                   jax.ShapeDtypeStruct((nc, N_HEADS, HEAD_DIM, HEAD_DIM), F32),
                   jax.ShapeDtypeStruct((nc, n_pow * N_HEADS, c, c), BF16),
                   jax.ShapeDtypeStruct((nc, N_HEADS, 2 * c, 2 * c), BF16),
                   jax.ShapeDtypeStruct((nc, N_HEADS, c, HEAD_DIM), F32)],
        scratch_shapes=[pltpu.VMEM((N_HEADS, HEAD_DIM, HEAD_DIM), F32)], compiler_params=_params(("arbitrary",)),
    )(u, u, lw, k, a, b)
    return res[0], res[1:]


def _rec_bwd(u, lw, k, a, b, saved, dy):
    t_len = lw.shape[0]
    c = min(REC_CHUNK, t_len)
    nc = t_len // c
    per = REC_CHUNKS_PER_STEP if nc % REC_CHUNKS_PER_STEP == 0 else 1
    steps = nc // per

    zs, pw, gs, us = saved

    def body(r_ref, v_ref, lw_ref, k_ref, a_ref, b_ref, zs_ref, pw_ref, gs_ref, us_ref, dy_ref, *rest):
        g_refs, dz_scr = rest[:6], rest[6]

        @pl.when(pl.program_id(0) == 0)
        def _():
            dz_scr[...] = jnp.zeros_like(dz_scr)

        for s in reversed(range(per)):
            rows = pl.ds(s * c, c)
            powers = [pw_ref[s, j * N_HEADS:(j + 1) * N_HEADS] for j in range(pw.shape[1] // N_HEADS)]
            chunk = functools.partial(lambda gram, u_val, pws, *xs: _chunk_fwd(*xs, powers=pws, gram_value=gram, u_value=u_val)[:2],
                                      gs_ref[s], us_ref[s], powers)
            _, vjp = jax.vjp(chunk, zs_ref[s], r_ref[rows, :], lw_ref[rows, :], k_ref[rows, :], v_ref[rows, :],
                             a_ref[rows, :], b_ref[rows, :])
            dz0, dr, dlw, dk, dv, da, db = vjp((dy_ref[rows, :], dz_scr[...]))
            for ref, val in zip(g_refs, (dr, dv, dlw, dk, da, db)):
                ref[rows, :] = val
            dz_scr[...] = dz0

    blk = lambda cb: pl.BlockSpec((per * c, RWKV_DIM), functools.partial(lambda i, q: (steps - 1 - i, q), q=cb))
    saved_blk = lambda arr: pl.BlockSpec((per,) + arr.shape[1:], lambda i: (steps - 1 - i, 0, 0, 0))
    return pl.pallas_call(
        body, name="rwkv_rec_bwd", grid=(steps,),
        in_specs=[blk(0), blk(2)] + [blk(0)] * 4 + [saved_blk(zs), saved_blk(pw), saved_blk(gs), saved_blk(us), blk(0)],
        out_specs=[blk(0)] * 6, out_shape=[jax.ShapeDtypeStruct((t_len, RWKV_DIM), F32)] * 6,
        scratch_shapes=[pltpu.VMEM((N_HEADS, HEAD_DIM, HEAD_DIM), F32)], compiler_params=_params(("arbitrary",)),
    )(u, u, lw, k, a, b, zs, pw, gs, us, dy)


_EARLY = ["w_in", "conv_w", "w_lora_up", "a_lora_up", "g_lora_up"]
_LATE = ["w_out", "w_up", "w_down", "w_ple_gate", "w_ple_proj"]
_SHARDED = _EARLY + _LATE
_COL_SHARDED = {"w_in", "conv_w", "w_lora_up", "a_lora_up", "g_lora_up", "w_up", "w_ple_proj"}
_BF16_GATHER = {"w_in", "w_out", "w_up", "w_down", "w_ple_gate", "w_ple_proj"}
_REPLICATED = ["norm_mix_g", "shift_mu", "w0", "a0", "k_k", "k_a", "r_k", "ln_x_g", "ln_x_b", "norm_mlp_g", "norm_ple_g",
               "norm_final_g"]
_WEIGHTS = ["norm_mix_g", "w_in", "conv_w", "shift_mu", "w_lora_up", "w0", "a_lora_up", "a0", "g_lora_up", "k_k", "k_a", "r_k",
            "ln_x_g", "ln_x_b", "w_out", "norm_mlp_g", "w_up", "w_down", "norm_ple_g", "w_ple_gate", "w_ple_proj", "norm_final_g"]


def _unshard(name, g):
    if name in _COL_SHARDED:
        return jnp.moveaxis(g, 0, 1).reshape(g.shape[1], N_DEV * g.shape[2])
    return g.reshape(N_DEV * g.shape[1], g.shape[2])


def _reshard(name, full):
    if name in _COL_SHARDED:
        return jnp.moveaxis(full.reshape(full.shape[0], N_DEV, full.shape[1] // N_DEV), 1, 0)
    return full.reshape(N_DEV, full.shape[0] // N_DEV, full.shape[1])


def _pad_in_cols(a):
    z = lambda n: jnp.zeros(a.shape[:-1] + (n,), a.dtype)
    conv = [a[..., part * CONV_DIM + j * LANE:part * CONV_DIM + (j + 1) * LANE] for j in range(CONV_DIM // LANE) for part in range(3)]
    return jnp.concatenate(conv + [a[..., CONV_COLS:3136], z(64), a[..., 3136:3200], z(64), a[..., 3200:3360], z(96)], axis=-1)


def _unpad_in_cols(a):
    conv = [a[..., (3 * j + part) * LANE:(3 * j + part + 1) * LANE] for part in range(3) for j in range(CONV_DIM // LANE)]
    return jnp.concatenate(conv + [a[..., CONV_COLS:3136], a[..., 3200:3264], a[..., 3328:3488]], axis=-1)


def _assemble_w_in(g):
    n_dev, rows, cols = g.shape

    def body(g_ref, o_ref):
        o_ref[...] = _pad_in_cols(jnp.concatenate([g_ref[d] for d in range(n_dev)], axis=1))

    return pl.pallas_call(
        body, name="w_in_assemble", grid=(rows // ROW_BLOCK,),
        in_specs=[pl.BlockSpec((n_dev, ROW_BLOCK, cols), lambda i: (0, i, 0))],
        out_specs=pl.BlockSpec((ROW_BLOCK, IN_PAD), lambda i: (i, 0)),
        out_shape=jax.ShapeDtypeStruct((rows, IN_PAD), g.dtype), compiler_params=_params(("arbitrary",)),
    )(g)


def _split_w_in_grad(dw):
    rows = dw.shape[0]
    cols = IN_COLS // N_DEV

    def body(d_ref, o_ref):
        full = _unpad_in_cols(d_ref[...])
        for d in range(N_DEV):
            o_ref[d] = full[:, cols * d:cols * (d + 1)]

    return pl.pallas_call(
        body, name="w_in_grad_split", grid=(rows // ROW_BLOCK,),
        in_specs=[pl.BlockSpec((ROW_BLOCK, IN_PAD), lambda i: (i, 0))],
        out_specs=pl.BlockSpec((N_DEV, ROW_BLOCK, cols), lambda i: (0, i, 0)),
        out_shape=jax.ShapeDtypeStruct((N_DEV, rows, cols), dw.dtype), compiler_params=_params(("arbitrary",)),
    )(dw)


def _pad_rows(a, rows):
    return jnp.concatenate([a, jnp.zeros((rows - a.shape[0],) + a.shape[1:], a.dtype)], axis=0)


SEG_W = [RWKV_DIM, RWKV_DIM, RWKV_DIM, LANE, LANE, 2 * LANE]
SEG_OFF = [0, 512, 1024, XW_OFF, XA_OFF, XG_OFF]


def _rwkv_pre_bwd(proj, u, grads, mu, small, dproj):
    t_len = u.shape[0]
    tr = min(ROW_BLOCK, t_len)
    nb = t_len // tr
    sub = 8
    n_g = len(grads)
    acc_shapes = [(1, RW_PAD)] + [(1, RWKV_DIM)] * 4 + [(LANE, RWKV_DIM), (LANE, RWKV_DIM), (2 * LANE, RWKV_DIM)]

    def body(*refs):
        seg_refs, halo_refs = refs[:6], refs[6:12]
        k_ref, xw_ref, xa_ref, xg_ref = refs[12:16]
        g_refs = refs[16:16 + n_g]
        mu_ref = refs[16 + n_g]
        prm_refs = refs[17 + n_g:24 + n_g]
        out_hbm = refs[25 + n_g]
        acc_refs = refs[26 + n_g:26 + n_g + len(acc_shapes)]
        vbuf, sems, carry = refs[26 + n_g + len(acc_shapes):]
        i = pl.program_id(0)
        blk = nb - 1 - i
        dr1, dr2, dv1, dv2, dlw, dk1, dk2, da, db, dg, lw_rows = [g[...] for g in g_refs]
        dk, dxw, dxa, dxg, *dprm = _rwkv_pre_vjp(k_ref[...], xw_ref[...], xa_ref[...], xg_ref[...], lw_rows,
                                                  *[p_[...] for p_ in prm_refs], dlw, dk1 + dk2, da, db, dg)
        du = jnp.concatenate([dr1 + dr2, dk, dv1 + dv2, dxw, dxa, dxg], axis=1)
        mu_v = mu_ref[...]

        @pl.when(i == 0)
        def _():
            carry[...] = jnp.zeros_like(carry)

        rows = lax.broadcasted_iota(jnp.int32, du.shape, 0)
        nxt = jnp.where(rows == tr - 1, carry[...], pltpu.roll(du, tr - 1, 0))
        d_rw = du - mu_v * du + mu_v * nxt
        d_mu = []
        for s_ref, h_ref, off, wd in zip(seg_refs, halo_refs, SEG_OFF, SEG_W):
            cur = s_ref[...]
            r0 = lax.broadcasted_iota(jnp.int32, cur.shape, 0)
            prev = jnp.where(r0 == 0, jnp.where(blk == 0, 0.0, h_ref[sub - 1:sub, :]), pltpu.roll(cur, 1, 0))
            d_mu.append(jnp.sum(du[:, off:off + wd] * (prev - cur), axis=0, keepdims=True))
        sums = [jnp.concatenate(d_mu, axis=1)] + list(dprm)

        @pl.when(i == 0)
        def _():
            for a_ref, val in zip(acc_refs, sums):
                a_ref[...] = val

        @pl.when(i > 0)
        def _():
            for a_ref, val in zip(acc_refs, sums):
                a_ref[...] += val

        carry[...] = du[0:1, :]
        slot = i % 2

        def writeback(s, b):
            return pltpu.make_async_copy(vbuf.at[s], out_hbm.at[pl.ds(b * tr, tr), pl.ds(CONV_COLS, RW_PAD)], sems.at[s])

        @pl.when(i >= 2)
        def _():
            writeback(slot, blk + 2).wait()

        vbuf[slot] = d_rw.astype(vbuf.dtype)
        writeback(slot, blk).start()

        @pl.when(i == nb - 1)
        def _():
            writeback(slot, blk).wait()
            if nb > 1:
                writeback(1 - slot, blk + 1).wait()

    rev = lambda w_, cb: pl.BlockSpec((tr, w_), functools.partial(lambda i, c: (nb - 1 - i, c), c=cb))
    halo = lambda w_, cb: pl.BlockSpec((sub, w_), functools.partial(
        lambda i, c: (jnp.maximum((nb - 1 - i) * (tr // sub) - 1, 0), c), c=cb))
    whole = lambda a: pl.BlockSpec(a.shape, functools.partial(lambda i, n: (0,) * n, n=a.ndim))
    segs = [(wd, (CONV_COLS + off) // wd) for off, wd in zip(SEG_OFF, SEG_W)]
    u_cols = [(512, 1), (LANE, XW_OFF // LANE), (LANE, XA_OFF // LANE), (2 * LANE, XG_OFF // (2 * LANE))]
    any_spec = pl.BlockSpec(memory_space=pl.ANY)
    res = pl.pallas_call(
        body, name="rwkv_pre_bwd", grid=(nb,),
        in_specs=[rev(*s) for s in segs] + [halo(*s) for s in segs] + [rev(*c) for c in u_cols]
                 + [rev(RWKV_DIM, 0)] * n_g + [whole(mu)] + [whole(p_) for p_ in small] + [any_spec],
        out_specs=[any_spec] + [pl.BlockSpec(s, functools.partial(lambda i, n: (0,) * n, n=len(s))) for s in acc_shapes],
        out_shape=[jax.ShapeDtypeStruct(dproj.shape, dproj.dtype)] + [jax.ShapeDtypeStruct(s, F32) for s in acc_shapes],
        scratch_shapes=[pltpu.VMEM((2, tr, RW_PAD), dproj.dtype), pltpu.SemaphoreType.DMA((2,)), pltpu.VMEM((1, RW_PAD), F32)],
        input_output_aliases={24 + n_g: 0},
        compiler_params=_params(("arbitrary",)),
    )(*[proj] * 12, *[u] * 4, *grads, mu, *small, dproj)
    return res


def _local_step(x, p, tgt, w, early_shards, late_shards):
    row = lambda v: v.reshape(1, -1)
    w = dict(w)

    xn1, *gathered = _rowwise("rms_mix", lambda h, g: (_rms(h, g),), [x], [w["norm_mix_g"]], [(D_MODEL, BF16)],
                              gather=early_shards)
    w.update({n: _unshard(n, g_) for n, g_ in zip(_EARLY[1:], gathered[1:])})
    w["w_in"] = _assemble_w_in(gathered[0])
    w["w_lora_up"] = _pad_rows(w["w_lora_up"], LANE)
    w["a_lora_up"] = _pad_rows(w["a_lora_up"], LANE)
    w["g_lora_up"] = _pad_rows(w["g_lora_up"], 2 * LANE)
    lg_send, lg_recv, lg_shards, lg_lands, lg_token = _gather_start("late_gather_start", late_shards, after=[xn1])
    w["shift_mu"] = w["shift_mu"] + lg_token[0:1, 0:1]
    proj = _matmul("in_proj", xn1, w["w_in"], "nn", [F32], tm=2048, tn=512, tk=D_MODEL)
    n_cb = CONV_DIM // LANE

    def conv_fwd(blk, cw):
        gb, gc, hx = blk[:, :LANE], blk[:, LANE:2 * LANE], blk[:, 2 * LANE:]
        uu = gc * hx
        return (gb * (uu * cw[2:3] + _shift_down(uu, 1) * cw[1:2] + _shift_down(uu, 2) * cw[0:1]),)

    (y_conv,) = _colwise("conv_fwd", conv_fwd, n_cb, [(proj, 3 * LANE)], [w["conv_w"]], [(CONV_DIM, BF16, LANE)])

    small = [w["w0"], w["a0"], w["k_k"], w["k_a"], w["w_lora_up"], w["a_lora_up"], w["g_lora_up"]]
    def pre_fwd(*xs):
        cur, prev_rows, mu, prm = xs[:6], xs[6:12], xs[12], xs[13:]
        segs = []
        for c_, p_, off, wd in zip(cur, prev_rows, SEG_OFF, SEG_W):
            rows = lax.broadcasted_iota(jnp.int32, c_.shape, 0)
            prev = jnp.where(rows == 0, p_, pltpu.roll(c_, 1, 0))
            segs.append(c_ + mu[:, off:off + wd] * (prev - c_))
        return (jnp.concatenate(segs, axis=1),) + tuple(_rwkv_pre(segs[1], segs[3], segs[4], segs[5], *prm))

    proj_segs = [(proj, wd, (CONV_COLS + off) // wd) for off, wd in zip(SEG_OFF, SEG_W)]
    u, lw, k_h, ra, rb, g = _rowwise(
        "rwkv_pre", pre_fwd, proj_segs, [w["shift_mu"]] + small, [(RW_PAD, F32)] + [(RWKV_DIM, F32)] * 5, tr=2 * ROW_BLOCK, halo=True)
    y_rec, rec_saved = _rec_fwd(u, lw, k_h, ra, rb)

    def late_weight(names, after):
        idx = [_LATE.index(n) for n in names]
        got = _gather_wait("late_gather_wait_" + names[0], [lg_send[i] for i in idx], [lg_recv[i] for i in idx],
                           [lg_shards[i] for i in idx], [lg_lands[i] for i in idx], after)
        return [_unshard(n, g_) for n, g_ in zip(names, got)]

    w["w_out"], w["w_up"], w["w_down"] = late_weight(["w_out", "w_up", "w_down"], [y_rec])
    post_c = [w["ln_x_g"], w["ln_x_b"], w["r_k"]]
    u_r, u_v = (u, 512, 0), (u, 512, 2)
    (y_rwkv,) = _rowwise("rwkv_post", lambda *xs: (_rwkv_post(*xs),), [y_rec, u_r, k_h, u_v, g], post_c, [(RWKV_DIM, BF16)],
                         tr=2 * ROW_BLOCK)
    ycat = jnp.concatenate([y_conv, y_rwkv], axis=1)
    def res_norm(acc, r_, g_):
        h = acc + r_
        return h, _rms(h, g_)

    h1, xn2 = _matmul("out_proj", ycat, w["w_out"], "nn", [F32, BF16], tm=1024, tn=D_MODEL, tk=D_MODEL, extras=[x],
                      consts=[w["norm_mlp_g"]], epilogue=res_norm)

    square = lambda h: h.astype(F32) * h.astype(F32)
    hid = _matmul("mlp_up", xn2, w["w_up"], "nn", [BF16], tm=2048, tn=1024, tk=D_MODEL,
                  epilogue=lambda acc: (jnp.maximum(acc, 0.0),))
    h2, xn3 = _matmul("mlp_down", hid, w["w_down"], "nn", [F32, BF16], tm=512, tn=D_MODEL, tk=D_FF, extras=[h1],
                      consts=[w["norm_ple_g"]], epilogue=res_norm, a_map=square)
    w["w_ple_gate"], w["w_ple_proj"] = late_weight(["w_ple_gate", "w_ple_proj"], [xn3])
    zg =_matmul("ple_gate", xn3, w["w_ple_gate"], "nn", [F32], tm=1024, tn=1024, tk=D_MODEL)
    pp = _matmul("ple_proj", p, w["w_ple_proj"], "nn", [F32], tm=1024, tn=1024, tk=PLE_DIM)

    def head(h2_, zg_, pp_, tg, gf):
        gate = _sigmoid(zg_)
        h3 = h2_ + gate * pp_
        out = _rms(h3, gf)
        err = out - tg
        dh3, dgf = _rms_bwd(h3, gf, err * (1.0 / D_MODEL))
        loss = jnp.sum(jnp.sum(err * err, axis=1, keepdims=True), axis=0, keepdims=True) * (0.5 / D_MODEL)
        return dh3, dh3 * pp_ * gate * (1.0 - gate), dh3 * gate, dgf, loss

    dh3, dzg, dpp, d_norm_final, loss = _rowwise(
        "head", head, [h2, zg, pp, tgt], [row(w["norm_final_g"])], [(D_MODEL, F32), (D_MODEL, BF16), (D_MODEL, BF16)],
        [(1, D_MODEL), (1, 1)], tr=2 * ROW_BLOCK)

    d_w_ple_proj = _matmul("d_ple_proj", p, dpp, "tn", [BF16], tm=PLE_DIM, tn=D_MODEL // N_DEV, tk=4096, col_blocks_out=True)
    d_w_ple_gate = _matmul("d_ple_gate", xn3, dzg, "tn", [BF16], tm=512, tn=1024, tk=4096)

    def norm_bwd(dxn, h, dres, g_):
        dh, dg = _rms_bwd(h, g_, dxn)
        dh = dh + dres
        return dh, dh, dg

    nb = dict(tm=512, tn=D_MODEL, epilogue=norm_bwd, sums=[(1, D_MODEL)])
    dh2, dh2_b, d_norm_ple = _matmul("dx_ple_gate", dzg, w["w_ple_gate"], "nt", [F32, BF16], tk=D_MODEL,
                                     extras=[h2, dh3], consts=[w["norm_ple_g"]], **nb)
    d_w_down = _matmul("d_mlp_down", hid, dh2_b, "tn", [BF16], tm=512, tn=1024, tk=4096, a_map=square)
    dpre = _matmul("dx_mlp_down", dh2_b, w["w_down"], "nt", [BF16], tm=2048, tn=1024, tk=D_MODEL, extras=[hid],
                   epilogue=lambda acc, hid_: (acc * (2.0 * hid_.astype(F32)),))
    d_w_up = _matmul("d_mlp_up", xn2, dpre, "tn", [BF16], tm=1024, tn=D_FF // N_DEV, tk=4096, col_blocks_out=True)
    dh1, dh1_b, d_norm_mlp = _matmul("dx_mlp_up", dpre, w["w_up"], "nt", [F32, BF16], tk=D_FF,
                                     extras=[h1, dh2], consts=[w["norm_mlp_g"]], **nb)
    d_w_out = _matmul("d_out_proj", ycat, dh1_b, "tn", [BF16], tm=512, tn=1024, tk=4096)
    dycat = _matmul("dx_out_proj", dh1_b, w["w_out"], "nt", [F32], tm=1024, tn=1024, tk=D_MODEL)
    late_grads = dict(w_out=d_w_out, w_up=d_w_up, w_down=d_w_down, w_ple_gate=d_w_ple_gate, w_ple_proj=d_w_ple_proj)
    late_send = [late_grads[n] if n in ("w_up", "w_ple_proj") else _reshard(n, late_grads[n]) for n in _LATE]
    *late_flight, late_token = _scatter_start("late_scatter_start", late_send, [lax.empty(a.shape, a.dtype) for a in late_send])
    conv_w_bwd = w["conv_w"] + late_token[0:1, 0:1]

    def conv_bwd(dy, blk, cw):
        gb, gc, hx = blk[:, :LANE], blk[:, LANE:2 * LANE], blk[:, 2 * LANE:]
        uu = gc * hx
        u1, u2 = _shift_down(uu, 1), _shift_down(uu, 2)
        dconv = dy * gb
        du = dconv * cw[2:3] + _shift_up(dconv, 1) * cw[1:2] + _shift_up(dconv, 2) * cw[0:1]
        s = lambda z: jnp.sum(z, axis=0, keepdims=True)
        d_blk = jnp.concatenate([dy * (uu * cw[2:3] + u1 * cw[1:2] + u2 * cw[0:1]), du * hx, du * gc], axis=1)
        return d_blk, s(dconv * u2), s(dconv * u1), s(dconv * uu)

    dproj, dcw0, dcw1, dcw2 = _colwise(
        "conv_bwd", conv_bwd, n_cb, [(dycat, LANE), (proj, 3 * LANE)], [conv_w_bwd],
        [(IN_PAD, BF16, 3 * LANE)], [(1, CONV_DIM)] * 3)

    def post_bwd(dy, y, r, k_h_, v, g_, ln_g, ln_b, r_k):
        _, vjp = jax.vjp(_rwkv_post, y, r, k_h_, v, g_, ln_g, ln_b, r_k)
        return vjp(dy)

    dy_rec, dr_p, dk_p, dv_p, dg, d_ln_g, d_ln_b, d_r_k = _rowwise(
        "rwkv_post_bwd", post_bwd, [(dycat, 512, 1), y_rec, u_r, k_h, u_v, g], post_c,
        [(RWKV_DIM, F32)] * 5, [(1, RWKV_DIM)] * 3, tr=2 * ROW_BLOCK)
    dr_r, dv_r, dlw, dk_r, da, db = _rec_bwd(u, lw, k_h, ra, rb, rec_saved, dy_rec)

    dproj, d_mu, d_w0, d_a0, d_k_k, d_k_a, d_wl, d_al, d_gl = _rwkv_pre_bwd(
        proj, u, [dr_p, dr_r, dv_p, dv_r, dlw, dk_p, dk_r, da, db, dg, lw], w["shift_mu"], small, dproj)
    d_w_in = _matmul("d_in_proj", xn1, dproj, "tn", [BF16], tm=1024, tn=896, tk=4096)
    early_grads = dict(conv_w=jnp.concatenate([dcw0, dcw1, dcw2], axis=0),
                       w_lora_up=d_wl[:64], a_lora_up=d_al[:64], g_lora_up=d_gl[:160])
    early_send = [_split_w_in_grad(d_w_in)] + [_reshard(n, early_grads[n]) for n in _EARLY[1:]]
    *early_flight, token = _scatter_start("early_scatter_start", early_send, [lax.empty(a.shape, a.dtype) for a in early_send])
    dx, d_norm_mix = _matmul(
        "dx_in_proj", dproj, w["w_in"], "nt", [F32], tk=IN_PAD, extras=[x, dh1], consts=[w["norm_mix_g"] + token[0:1, 0:1]],
        **dict(nb, epilogue=lambda *a: norm_bwd(*a)[1:]))

    grads = dict(
        norm_mix_g=d_norm_mix, shift_mu=d_mu, w0=d_w0, a0=d_a0, k_k=d_k_k, k_a=d_k_a, r_k=d_r_k,
        ln_x_g=d_ln_g, ln_x_b=d_ln_b, norm_mlp_g=d_norm_mlp, norm_ple_g=d_norm_ple, norm_final_g=d_norm_final)
    return loss, dx, grads, late_flight, early_flight, d_w_in


def _adam_update(partials, w_ref, m_ref, v_ref, g_ref, d_ref, nm_ref, nv_ref):
    g = partials[0].astype(F32)
    for part in partials[1:]:
        g = g + part.astype(F32)
    nm =ADAM_B1 * m_ref[...] + (1.0 - ADAM_B1) * g
    nv = ADAM_B2 * v_ref[...] + (1.0 - ADAM_B2) * (g * g)
    m_hat = nm / (1.0 - ADAM_B1 ** ADAM_STEP)
    v_hat = nv / (1.0 - ADAM_B2 ** ADAM_STEP)
    g_ref[...] = g
    d_ref[...] = -ADAM_LR * (m_hat / (jnp.sqrt(v_hat) + ADAM_EPS) + ADAM_WD * w_ref[...])
    nm_ref[...] = nm
    nv_ref[...] = nv


SMALL_ROWS = 8


def _small_layout(widths):
    widths = list(widths) + [1]
    fill, place = [0] * SMALL_ROWS, [None] * len(widths)
    for j in sorted(range(len(widths)), key=lambda q: -widths[q]):
        row = fill.index(min(fill))
        place[j] = (row, fill[row])
        fill[row] += -(-widths[j] // LANE) * LANE
    return place, max(fill)


def _pack_small(vecs, loss):
    place, total = _small_layout([v_.shape[1] for v_ in vecs])
    n = len(vecs)

    def body(*refs):
        out = jnp.zeros((SMALL_ROWS, total), F32)
        row_id = lax.broadcasted_iota(jnp.int32, (SMALL_ROWS, total), 0)
        for row in range(SMALL_ROWS):
            mine = sorted((off, j) for j, (r_, off) in enumerate(place) if r_ == row)
            pieces, at = [], 0
            for off, j in mine:
                val = refs[j][...]
                pieces.append(val)
                at = off + val.shape[1]
                pad = -val.shape[1] % LANE
                if pad:
                    pieces.append(jnp.zeros((1, pad), F32))
                    at += pad
            if total > at:
                pieces.append(jnp.zeros((1, total - at), F32))
            out = jnp.where(row_id == row, jnp.broadcast_to(jnp.concatenate(pieces, axis=1), (SMALL_ROWS, total)), out)
        refs[n + 1][...] = out

    return pl.pallas_call(body, name="pack_small", out_shape=jax.ShapeDtypeStruct((SMALL_ROWS, total), F32))(*vecs, loss)


def _adamw_small(packed, ws, ms, vs):
    n = len(ws)
    place, _ = _small_layout([w_.shape[1] for w_ in ws])

    def body(p_ref, *refs):
        w_refs, m_refs, v_refs, outs = refs[:n], refs[n:2 * n], refs[2 * n:3 * n], refs[3 * n:]
        for j in range(n):
            row, off = place[j]
            cols = pl.ds(off, ws[j].shape[1])
            _adam_update([p_ref[s, row:row + 1, cols] for s in range(N_DEV)], w_refs[j], m_refs[j], v_refs[j],
                         *outs[4 * j:4 * j + 4])
        row, off = place[n]
        total = p_ref[0, row:row + 1, off:off + 1]
        for s in range(1, N_DEV):
            total = total + p_ref[s, row:row + 1, off:off + 1]
        outs[4 * n][...] = total

    res = pl.pallas_call(
        body, name="adamw_small",
        out_shape=[jax.ShapeDtypeStruct(w_.shape, F32) for w_ in ws for _ in range(4)] + [jax.ShapeDtypeStruct((1, 1), F32)],
    )(packed, *ws, *ms, *vs)
    return [res[4 * j:4 * j + 4] for j in range(n)], res[4 * n]


def _adamw(name, parts, w, m, v, own=None, me=None):
    rows, cols = w.shape[-2:]
    lead = w.ndim - 2
    tr = rows if rows * cols * 4 * 8 <= (4 << 20) else max(8, (4 << 20) // (cols * 4 * 8) // 8 * 8)
    while rows % tr:
        tr -= 8
    shape4 = [jax.ShapeDtypeStruct(w.shape, F32)] * 4
    if own is None:
        def body(p_ref, *refs):
            _adam_update([p_ref[s] for s in range(N_DEV)], *refs)

        blk = pl.BlockSpec((None,) * lead + (tr, cols), lambda i: (0,) * lead + (i, 0))
        return pl.pallas_call(
            body, name=name, grid=(rows // tr,),
            in_specs=[pl.BlockSpec((N_DEV, tr, cols), lambda i: (0, i, 0)), blk, blk, blk], out_specs=[blk] * 4,
            out_shape=shape4, compiler_params=_params(("arbitrary",)),
        )(parts, w, m, v)

    def body_own(me_ref, p_ref, own_ref, *refs):
        mine = own_ref[...]
        _adam_update([jnp.where(me_ref[0] == s, mine, p_ref[s]) for s in range(N_DEV)], *refs)

    blk = pl.BlockSpec((None,) * lead + (tr, cols), lambda i, me_ref: (0,) * lead + (i, 0))
    return pl.pallas_call(
        body_own, name=name, out_shape=shape4,
        grid_spec=pltpu.PrefetchScalarGridSpec(
            num_scalar_prefetch=1, grid=(rows // tr,),
            in_specs=[pl.BlockSpec((N_DEV, tr, cols), lambda i, me_ref: (0, i, 0)),
                      pl.BlockSpec((None, tr, cols), lambda i, me_ref: (me_ref[0], i, 0)), blk, blk, blk],
            out_specs=[blk] * 4),
        compiler_params=_params(("arbitrary",)),
    )(me, parts, own, w, m, v)


def kernel(x, p, norm_mix_g, w_in, conv_w, shift_mu, w_lora_up, w0, a_lora_up, a0, g_lora_up, k_k, k_a, r_k, ln_x_g, ln_x_b, w_out, norm_mlp_g, w_up, w_down, norm_ple_g, w_ple_gate, w_ple_proj, norm_final_g, loss_target, m_norm_mix_g, m_w_in, m_conv_w, m_shift_mu, m_w_lora_up, m_w0, m_a_lora_up, m_a0, m_g_lora_up, m_k_k, m_k_a, m_r_k, m_ln_x_g, m_ln_x_b, m_w_out, m_norm_mlp_g, m_w_up, m_w_down, m_norm_ple_g, m_w_ple_gate, m_w_ple_proj, m_norm_final_g, v_norm_mix_g, v_w_in, v_conv_w, v_shift_mu, v_w_lora_up, v_w0, v_a_lora_up, v_a0, v_g_lora_up, v_k_k, v_k_a, v_r_k, v_ln_x_g, v_ln_x_b, v_w_out, v_norm_mlp_g, v_w_up, v_w_down, v_norm_ple_g, v_w_ple_gate, v_w_ple_proj, v_norm_final_g):
    args = dict(locals())
    wts = {n: args[n] for n in _WEIGHTS}
    mom = {n: args["m_" + n] for n in _WEIGHTS}
    var = {n: args["v_" + n] for n in _WEIGHTS}
    shard2d = lambda a: a.reshape(a.shape[-2:])
    pad_mu = lambda a: _pad_in_cols(jnp.concatenate([jnp.zeros((1, CONV_COLS), F32), a], axis=1))[:, CONV_COLS:]
    unpad_mu = lambda a: _unpad_in_cols(jnp.concatenate([jnp.zeros((1, CONV_COLS), F32), a], axis=1))[:, CONV_COLS:]

    shards = {n: shard2d(wts[n]).astype(BF16 if n in _BF16_GATHER else F32) for n in _SHARDED}
    w = {n: wts[n].reshape(1, -1) for n in _REPLICATED}
    w["shift_mu"] = pad_mu(wts["shift_mu"])

    loss, dx, grads, late_flight, early_flight, d_w_in = _local_step(
        x[0], p[0, 0], loss_target[0], w, [shards[n] for n in _EARLY], [shards[n] for n in _LATE])

    me = (4 * lax.axis_index("x") + 2 * lax.axis_index("y") + lax.axis_index("c")).astype(jnp.int32).reshape(1)
    late_sent, late_parts = _scatter_wait("late_scatter_wait", *late_flight, after=[d_w_in])
    out = {n: _adamw("adamw_" + n, prt, wts[n], mom[n], var[n], own=own, me=me)
           for n, prt, own in zip(_LATE, late_parts, late_sent)}
    early_sent, early_parts = _scatter_wait("early_scatter_wait", *early_flight, after=[dx] + [out[n][1] for n in _LATE])
    for n, prt, own in zip(_EARLY, early_parts, early_sent):
        out[n] = _adamw("adamw_" + n, prt, wts[n], mom[n], var[n], own=own, me=me)

    grads["shift_mu"] = unpad_mu(grads["shift_mu"])
    flat = lambda a: a.reshape(1, -1)
    (small_parts,) = _all_gather("gather_small", [_pack_small([flat(grads[n]) for n in _REPLICATED], loss)])
    small, loss_total = _adamw_small(small_parts, *[[flat(d[n]) for n in _REPLICATED] for d in (wts, mom, var)])
    for n, res in zip(_REPLICATED, small):
        out[n] = [r.reshape(wts[n].shape) for r in res]
    return (loss_total[0, 0], dx[None], *[out[n][0] for n in _WEIGHTS], *[out[n][1] for n in _WEIGHTS],
            *[out[n][2] for n in _WEIGHTS], *[out[n][3] for n in _WEIGHTS])
```

```python
import functools

import jax
import jax.numpy as jnp
from jax import lax
from jax.experimental import pallas as pl
from jax.experimental.pallas import tpu as pltpu

F32 = jnp.float32
BF16 = jnp.bfloat16

N_DEV = 8
D_MODEL = 1024
CONV_DIM = 512
RWKV_DIM = 512
HEAD_DIM = 64
N_HEADS = 8
D_FF = 4096
PLE_DIM = 256
RMS_EPS = 1e-6
GN_EPS = 64e-5
L2_EPS = 1e-12
ADAM_LR, ADAM_B1, ADAM_B2, ADAM_EPS, ADAM_WD, ADAM_STEP = 0.001, 0.9, 0.999, 1e-08, 0.01, 10

CONV_COLS = 3 * CONV_DIM
RW_PAD = 2048
IN_PAD = CONV_COLS + RW_PAD
IN_COLS = 3360
XW_OFF, XA_OFF, XG_OFF = 1536, 1664, 1792
REC_CHUNK = 128
REC_CHUNKS_PER_STEP = 2
REC_PASSES = 1
ROW_BLOCK = 256
LANE = 128
VMEM_LIMIT = 56 * 1024 * 1024


def _dims(dn, ndim):
    if ndim == 3:
        return {"nn": (((2,), (1,)), ((0,), (0,))), "nt": (((2,), (2,)), ((0,), (0,))),
                "tn": (((1,), (1,)), ((0,), (0,)))}[dn]
    return {"nn": (((1,), (0,)), ((), ())), "nt": (((1,), (1,)), ((), ())), "tn": (((0,), (0,)), ((), ()))}[dn]


def _split2(x):
    hi = x.astype(BF16)
    return hi, (x - hi.astype(F32)).astype(BF16)


def _mm_raw(x, y, dn, passes):
    f = lambda p, q: lax.dot_general(p, q, _dims(dn, x.ndim), preferred_element_type=F32)
    if passes == 1:
        return f(x.astype(BF16), y.astype(BF16))
    xh, xl = _split2(x)
    yh, yl = _split2(y)
    if passes == 2:
        return f(xh, yh) + f(xh, yl)
    return f(xh, yh) + f(xh, yl) + f(xl, yh)


@functools.partial(jax.custom_vjp, nondiff_argnums=(2, 3))
def _mm(x, y, dn, passes):
    return _mm_raw(x, y, dn, passes)


def _mm_fwd(x, y, dn, passes):
    return _mm_raw(x, y, dn, passes), (x, y)


def _mm_bwd(dn, passes, res, d):
    x, y = res
    if dn == "nn":
        return _mm(d, y, "nt", passes), _mm(x, d, "tn", passes)
    if dn == "nt":
        return _mm(d, y, "nn", passes), _mm(d, x, "tn", passes)
    return _mm(y, d, "nt", passes), _mm(x, d, "nn", passes)


_mm.defvjp(_mm_fwd, _mm_bwd)


HSUM_COLS = 256


def _head_ones():
    i = lax.broadcasted_iota(jnp.int32, (HSUM_COLS, HSUM_COLS), 0) // HEAD_DIM
    j = lax.broadcasted_iota(jnp.int32, (HSUM_COLS, HSUM_COLS), 1) // HEAD_DIM
    return (i == j).astype(BF16)


def _hsum_raw(x):
    ones = _head_ones()
    n = x.shape[0]
    p = jnp.concatenate(_split2(x), axis=0)
    s = jnp.concatenate(
        [lax.dot_general(p[:, c:c + HSUM_COLS], ones, _dims("nn", 2), preferred_element_type=F32)
         for c in range(0, p.shape[1], HSUM_COLS)], axis=1)
    return s[:n] + s[n:]


@jax.custom_vjp
def _hsum(x):
    return _hsum_raw(x)


_hsum.defvjp(lambda x: (_hsum_raw(x), None), lambda _, d: (_hsum(d),))


def _sigmoid(x):
    return 0.5 + 0.5 * jnp.tanh(0.5 * x)


def _softplus(x):
    return jnp.maximum(x, 0.0) + jnp.log(1.0 + jnp.exp(-jnp.abs(x)))


def _params(sem):
    return pltpu.CompilerParams(dimension_semantics=sem, vmem_limit_bytes=VMEM_LIMIT)


def _rowwise(name, fn, rows, consts, row_outs, acc_outs=(), tr=ROW_BLOCK, halo=False, gather=()):
    rows = [r if isinstance(r, tuple) else (r, r.shape[1], 0) for r in rows]
    t_len = rows[0][0].shape[0]
    tr = min(tr, t_len)
    n_r, n_c, n_o, n_a, n_x = len(rows), len(consts), len(row_outs), len(acc_outs), len(gather)
    n_h = n_r if halo else 0
    sub = 8
    x_specs, x_shapes, x_sems = _gather_io(gather) if n_x else ([], [], [])
    nb = t_len // tr

    def body(*refs):
        if n_x:
            n_in = n_r + n_h + n_c
            start, forward, wait = _gather_plan(refs[n_in:n_in + n_x], refs[len(refs) - 3 - n_x:len(refs) - 3], *refs[len(refs) - 3:])
            pl.when(pl.program_id(0) == 0)(start)
            refs = refs[:n_in] + refs[n_in + n_x:len(refs) - 3 - n_x]
        ins = [r[...] for r in refs[:n_r]]
        ins += [jnp.where(pl.program_id(0) == 0, 0.0, r[sub - 1:sub, :]) for r in refs[n_r:n_r + n_h]]
        ins += [r[...] for r in refs[n_r + n_h:n_r + n_h + n_c]]
        refs = refs[:n_r] + refs[n_r + n_h:]
        outs = fn(*ins)
        o_refs = refs[n_r + n_c:n_r + n_c + n_o]
        a_refs = refs[n_r + n_c + n_o:]
        for o_ref, val in zip(o_refs, outs[:n_o]):
            o_ref[...] = val.astype(o_ref.dtype)
        if n_a:
            first = pl.program_id(0) == 0

            @pl.when(first)
            def _():
                for a_ref, val in zip(a_refs, outs[n_o:]):
                    a_ref[...] = val

            @pl.when(jnp.logical_not(first))
            def _():
                for a_ref, val in zip(a_refs, outs[n_o:]):
                    a_ref[...] += val

        if n_x:
            @pl.when(pl.program_id(0) == nb - 1)
            def _():
                for j in range(n_x):
                    forward(j)
                wait()

    in_specs = [pl.BlockSpec((tr, w), functools.partial(lambda i, c: (i, c), c=cb)) for _, w, cb in rows]
    if halo:
        in_specs += [pl.BlockSpec((sub, w), functools.partial(lambda i, c: (jnp.maximum(i * (tr // sub) - 1, 0), c), c=cb))
                     for _, w, cb in rows]
    in_specs += [pl.BlockSpec(c.shape, functools.partial(lambda i, n: (0,) * n, n=c.ndim)) for c in consts]
    out_specs = [pl.BlockSpec((tr, w), lambda i: (i, 0)) for w, _ in row_outs]
    out_specs += [pl.BlockSpec(s, functools.partial(lambda i, n: (0,) * n, n=len(s))) for s in acc_outs]
    out_shape = [jax.ShapeDtypeStruct((t_len, w), dt) for w, dt in row_outs]
    out_shape += [jax.ShapeDtypeStruct(s, F32) for s in acc_outs]
    return pl.pallas_call(
        body, name=name, grid=(nb,), in_specs=in_specs + x_specs, out_specs=out_specs + x_specs,
        out_shape=out_shape + x_shapes, scratch_shapes=x_sems,
        compiler_params=pltpu.CompilerParams(dimension_semantics=("arbitrary",), vmem_limit_bytes=VMEM_LIMIT,
                                             has_side_effects=bool(n_x)),
    )(*[r[0] for r in rows], *([r[0] for r in rows] if halo else []), *consts, *gather)


def _colwise(name, fn, n_blocks, cols, prms, col_outs, prm_outs=()):
    t_len = cols[0][0].shape[0]
    n_i = len(cols) + len(prms)

    def body(*refs):
        outs = fn(*[r[...] for r in refs[:n_i]])
        for o_ref, val in zip(refs[n_i:], outs):
            o_ref[...] = val.astype(o_ref.dtype)

    spec = lambda r, w: pl.BlockSpec((r, w), lambda j: (0, j))
    in_specs = [spec(t_len, w) for _, w in cols] + [spec(a.shape[0], LANE) for a in prms]
    out_specs = [spec(t_len, bw) for _, _, bw in col_outs] + [spec(r, LANE) for r, _ in prm_outs]
    out_shape = [jax.ShapeDtypeStruct((t_len, w), dt) for w, dt, _ in col_outs]
    out_shape += [jax.ShapeDtypeStruct((r, w), F32) for r, w in prm_outs]
    return pl.pallas_call(
        body, name=name, grid=(n_blocks,), in_specs=in_specs, out_specs=out_specs, out_shape=out_shape,
        compiler_params=_params(("arbitrary",)),
    )(*[c[0] for c in cols], *prms)


def _matmul(name, a, b, dn, outs, *, tm, tn, tk, extras=(), consts=(), epilogue=None, sums=(), a_map=None,
            col_blocks_out=False):
    if dn == "nn":
        (m, k), n = a.shape, b.shape[1]
    elif dn == "nt":
        (m, k), n = a.shape, b.shape[0]
    else:
        (k, m), n = a.shape, b.shape[1]
    tm, tn, tk = min(tm, m), min(tn, n), min(tk, k)
    nk = k // tk
    grid = (m // tm, n // tn, nk)
    assert nk == 1 and (not sums or grid[1] == 1)
    a_spec = pl.BlockSpec((tk, tm), lambda i, j, q: (q, i)) if dn == "tn" else pl.BlockSpec((tm, tk), lambda i, j, q: (i, q))
    b_spec = pl.BlockSpec((tn, tk), lambda i, j, q: (j, q)) if dn == "nt" else pl.BlockSpec((tk, tn), lambda i, j, q: (q, j))
    o_spec = pl.BlockSpec((tm, tn), lambda i, j, q: (i, j))
    c_spec = pl.BlockSpec((1, tn), lambda i, j, q: (0, j))
    n_e, n_c, n_o, n_s = len(extras), len(consts), len(outs), len(sums)

    def body(*refs):
        a_ref, b_ref = refs[:2]
        e_refs = refs[2:2 + n_e + n_c]
        o_refs, s_refs = refs[2 + n_e + n_c:2 + n_e + n_c + n_o], refs[2 + n_e + n_c + n_o:]
        step = pl.program_id(0) * grid[1] + pl.program_id(1)
        a_blk = a_ref[...] if a_map is None else a_map(a_ref[...])
        acc = lax.dot_general(a_blk.astype(BF16), b_ref[...].astype(BF16), _dims(dn, 2), preferred_element_type=F32)
        vals = (acc,) if epilogue is None else epilogue(acc, *[e[...] for e in e_refs])
        for o_ref, val in zip(o_refs, vals[:n_o]):
            o_ref[...] = val.astype(o_ref.dtype)
        if n_s:
            @pl.when(step == 0)
            def _():
                for s_ref, val in zip(s_refs, vals[n_o:]):
                    s_ref[...] = val

            @pl.when(step > 0)
            def _():
                for s_ref, val in zip(s_refs, vals[n_o:]):
                    s_ref[...] += val

    res = pl.pallas_call(
        body, name=name, grid=grid,
        in_specs=[a_spec, b_spec] + [o_spec] * n_e + [c_spec] * n_c,
        out_specs=[pl.BlockSpec((None, tm, tn), lambda i, j, q: (j, i, 0)) if col_blocks_out else o_spec] * n_o
                  + [c_spec] * n_s,
        out_shape=[jax.ShapeDtypeStruct((n // tn, m, tn) if col_blocks_out else (m, n), dt) for dt in outs]
                  + [jax.ShapeDtypeStruct(s, F32) for s in sums],
        compiler_params=_params(("arbitrary",) * 3 if n_s else ("parallel", "parallel", "arbitrary")),
    )(a, b, *extras, *consts)
    return res[0] if len(res) == 1 else res


def _rms(h, g):
    return h * lax.rsqrt(jnp.mean(h * h, axis=-1, keepdims=True) + RMS_EPS) * g


def _rms_bwd(h, g, dy):
    rs = lax.rsqrt(jnp.mean(h * h, axis=-1, keepdims=True) + RMS_EPS)
    n = h * rs
    dn = dy * g
    dh = rs * (dn - n * jnp.mean(dn * n, axis=-1, keepdims=True))
    return dh, jnp.sum(dy * n, axis=0, keepdims=True)


def _rwkv_pre(k, xw, xa, xg, w0, a0, k_k, k_a, wl, al, gl):
    zw = w0 + _mm(jnp.tanh(xw), wl, "nn", 1)
    lw = -jnp.exp(-_softplus(-zw) - 0.5)
    iclr = _sigmoid(a0 + _mm(xa, al, "nn", 1))
    g = _mm(_sigmoid(xg), gl, "nn", 1)
    kk0 = k * k_k
    kk = kk0 * lax.rsqrt(jnp.maximum(_hsum(kk0 * kk0), L2_EPS * L2_EPS))
    k_h = k * (1.0 + (iclr - 1.0) * k_a)
    return lw, k_h, -kk, kk * iclr, g


def _rwkv_pre_vjp(k, xw, xa, xg, lw, w0, a0, k_k, k_a, wl, al, gl, dlw, dk_h, da, db, dg):
    del w0
    s0 = lambda z: jnp.sum(z, axis=0, keepdims=True)
    iclr = _sigmoid(a0 + _mm(xa, al, "nn", 1))
    kk0 = k * k_k
    s = _hsum(kk0 * kk0)
    inv = lax.rsqrt(jnp.maximum(s, L2_EPS * L2_EPS))
    kk = kk0 * inv
    dkk = db * iclr - da
    diclr = db * kk + dk_h * (k * k_a)
    dkk0 = dkk * inv - jnp.where(s > L2_EPS * L2_EPS, inv * inv * inv * _hsum(dkk * kk0), 0.0) * kk0
    dk = dk_h * (1.0 + (iclr - 1.0) * k_a) + dkk0 * k_k
    dza = diclr * iclr * (1.0 - iclr)
    dzw = dlw * lw * (1.0 + lw * 1.6487212707001282)
    th = jnp.tanh(xw)
    sg = _sigmoid(xg)
    dxw = _mm(dzw, wl, "nt", 1) * (1.0 - th * th)
    dxa = _mm(dza, al, "nt", 1)
    dxg = _mm(dg, gl, "nt", 1) * sg * (1.0 - sg)
    return (dk, dxw, dxa, dxg, s0(dzw), s0(dza), s0(dkk0 * k), s0(dk_h * k * (iclr - 1.0)),
            _mm(th, dzw, "tn", 1), _mm(xa, dza, "tn", 1), _mm(sg, dg, "tn", 1))


def _rwkv_post(y, r, k_h, v, g, ln_g, ln_b, r_k):
    mu = _hsum(y) * (1.0 / HEAD_DIM)
    yc = y - mu
    var = _hsum(yc * yc) * (1.0 / HEAD_DIM)
    yo = yc * lax.rsqrt(var + GN_EPS) * ln_g + ln_b
    bonus = _hsum(r * k_h * r_k) * v
    return (yo + bonus) * g


def _shift_down(x, n):
    rows = lax.broadcasted_iota(jnp.int32, x.shape, 0)
    return jnp.where(rows < n, 0.0, pltpu.roll(x, n, 0))


def _shift_up(x, n):
    t_len = x.shape[0]
    rows = lax.broadcasted_iota(jnp.int32, x.shape, 0)
    return jnp.where(rows >= t_len - n, 0.0, pltpu.roll(x, t_len - n, 0))


def _gather_plan(ins, outs, send_sems, recv_sems, local_sems):
    x, y, c = lax.axis_index("x"), lax.axis_index("y"), lax.axis_index("c")
    me = 4 * x + 2 * y + c
    direct, chips = (1, 2, 4, 6), (2, 4, 6)

    def local(i):
        return pltpu.make_async_copy(ins[i], outs[i].at[me], local_sems.at[i])

    def send(i, rel):
        return pltpu.make_async_remote_copy(
            src_ref=ins[i], dst_ref=outs[i].at[me], send_sem=send_sems.at[i, rel - 1], recv_sem=recv_sems.at[i, rel - 1],
            device_id=(x ^ (rel >> 2), y ^ ((rel >> 1) & 1), c ^ (rel & 1)), device_id_type=pl.DeviceIdType.MESH)

    def passed(i, rel):
        slot = outs[i].at[me ^ rel]
        return pltpu.make_async_remote_copy(
            src_ref=slot, dst_ref=slot, send_sem=send_sems.at[i, rel], recv_sem=recv_sems.at[i, rel],
            device_id=(x, y, 1 - c), device_id_type=pl.DeviceIdType.MESH)

    def landed(i, rel):
        slot = outs[i].at[me ^ rel]
        return pltpu.make_async_remote_copy(
            src_ref=slot, dst_ref=slot, send_sem=send_sems.at[i, rel - 1], recv_sem=recv_sems.at[i, rel - 1],
            device_id=(x, y, c), device_id_type=pl.DeviceIdType.MESH)

    def start():
        for i in range(len(ins)):
            local(i).start()
            for rel in direct:
                send(i, rel).start()

    def forward(i):
        for rel in chips:
            landed(i, rel).wait_recv()
            passed(i, rel).start()

    def wait():
        for i in range(len(ins)):
            local(i).wait()
            for rel in (1, 3, 5, 7):
                landed(i, rel).wait_recv()
            for rel in direct:
                send(i, rel).wait_send()
            for rel in chips:
                passed(i, rel).wait_send()

    return start, forward, wait


def _gather_io(arrays):
    n = len(arrays)
    any_spec = pl.BlockSpec(memory_space=pl.ANY)
    out_shape = [jax.ShapeDtypeStruct((N_DEV,) + a.shape, a.dtype) for a in arrays]
    sems = [pltpu.SemaphoreType.DMA((n, N_DEV - 1)), pltpu.SemaphoreType.DMA((n, N_DEV - 1)), pltpu.SemaphoreType.DMA((n,))]
    return [any_spec] * n, out_shape, sems


def _all_gather(name, arrays):
    n = len(arrays)
    specs, out_shape, sems = _gather_io(arrays)

    def body(*refs):
        start, forward, wait = _gather_plan(refs[:n], refs[n:2 * n], *refs[2 * n:])
        start()
        for i in range(n):
            forward(i)
        wait()

    return pl.pallas_call(
        body, name=name, in_specs=specs, out_specs=specs, out_shape=out_shape, scratch_shapes=sems,
        compiler_params=pltpu.CompilerParams(has_side_effects=True),
    )(*arrays)


def _scatter_start(name, arrays, lands):
    n = len(arrays)
    hbm = pl.BlockSpec(memory_space=pltpu.HBM)

    def body(*refs):
        ins, land, send_sems, recv_sems = refs[:n], refs[n:2 * n], refs[2 * n], refs[2 * n + 1]
        token = refs[4 * n + 2]
        x, y, c = lax.axis_index("x"), lax.axis_index("y"), lax.axis_index("c")
        me = 4 * x + 2 * y + c
        for i in range(n):
            for rel in range(1, N_DEV):
                k = i * (N_DEV - 1) + rel - 1
                pltpu.make_async_remote_copy(
                    src_ref=ins[i].at[me ^ rel], dst_ref=land[i].at[me], send_sem=send_sems.at[k],
                    recv_sem=recv_sems.at[k], device_id=(x ^ (rel >> 2), y ^ ((rel >> 1) & 1), c ^ (rel & 1)),
                    device_id_type=pl.DeviceIdType.MESH).start()
        token[...] = jnp.zeros_like(token)

    sem = pltpu.SemaphoreType.DMA((n * (N_DEV - 1),))
    bufs = [pltpu.HBM(a.shape, a.dtype) for a in list(arrays) + list(lands)]
    res = pl.pallas_call(
        body, name=name, out_shape=(sem, sem, *bufs, jax.ShapeDtypeStruct((8, LANE), F32)),
        in_specs=[hbm] * (2 * n),
        out_specs=(pl.BlockSpec(memory_space=pltpu.SEMAPHORE),) * 2 + (hbm,) * (2 * n) + (pl.BlockSpec(memory_space=pltpu.VMEM),),
        input_output_aliases={i: 2 + i for i in range(2 * n)},
        compiler_params=pltpu.CompilerParams(has_side_effects=pltpu.SideEffectType.DATAFLOW_SIDE_EFFECTING),
    )(*[pltpu.with_memory_space_constraint(a, pltpu.HBM) for a in list(arrays) + list(lands)])
    return res[0], res[1], res[2:2 + n], res[2 + n:2 + 2 * n], res[2 + 2 * n]


def _scatter_wait(name, send_sems, recv_sems, arrays, lands, after):
    n, n_after = len(arrays), len(after)
    hbm = pl.BlockSpec(memory_space=pltpu.HBM)

    def body(*refs):
        ins, land, s_sems, r_sems = refs[:n], refs[n:2 * n], refs[2 * n], refs[2 * n + 1]
        x, y, c = lax.axis_index("x"), lax.axis_index("y"), lax.axis_index("c")
        me = 4 * x + 2 * y + c
        for i in range(n):
            for rel in range(1, N_DEV):
                k = i * (N_DEV - 1) + rel - 1
                cp = pltpu.make_async_remote_copy(
                    src_ref=ins[i].at[me ^ rel], dst_ref=land[i].at[me ^ rel], send_sem=s_sems.at[k],
                    recv_sem=r_sems.at[k], device_id=(x, y, c), device_id_type=pl.DeviceIdType.MESH)
                cp.wait_send()
                cp.wait_recv()

    res = pl.pallas_call(
        body, name=name, out_shape=[pltpu.HBM(a.shape, a.dtype) for a in list(arrays) + list(lands)],
        in_specs=[hbm] * (2 * n) + [pl.BlockSpec(memory_space=pltpu.SEMAPHORE)] * 2 + [pl.BlockSpec(memory_space=pl.ANY)] * n_after,
        out_specs=[hbm] * (2 * n), input_output_aliases={i: i for i in range(2 * n)},
        compiler_params=pltpu.CompilerParams(has_side_effects=pltpu.SideEffectType.DATAFLOW_SIDE_EFFECTING),
    )(*arrays, *lands, send_sems, recv_sems, *after)
    return res[:n], res[n:]


def _gather_start(name, shards, after):
    n, n_after = len(shards), len(after)
    hbm = pl.BlockSpec(memory_space=pltpu.HBM)
    lands = [lax.empty((N_DEV,) + a.shape, a.dtype) for a in shards]

    def body(*refs):
        ins, land = refs[:n], refs[n:2 * n]
        outs = refs[2 * n + n_after:]
        send_sems, recv_sems, token = outs[:n], outs[n:2 * n], outs[4 * n]
        x, y, c = lax.axis_index("x"), lax.axis_index("y"), lax.axis_index("c")
        me = 4 * x + 2 * y + c
        for i in range(n):
            for rel in range(N_DEV):
                pltpu.make_async_remote_copy(
                    src_ref=ins[i], dst_ref=land[i].at[me], send_sem=send_sems[i].at[rel],
                    recv_sem=recv_sems[i].at[rel], device_id=(x ^ (rel >> 2), y ^ ((rel >> 1) & 1), c ^ (rel & 1)),
                    device_id_type=pl.DeviceIdType.MESH).start()
        token[...] = jnp.zeros_like(token)

    sem = pltpu.SemaphoreType.DMA((N_DEV,))
    bufs = [pltpu.HBM(a.shape, a.dtype) for a in list(shards) + lands]
    res = pl.pallas_call(
        body, name=name, out_shape=(*[sem] * (2 * n), *bufs, jax.ShapeDtypeStruct((8, LANE), F32)),
        in_specs=[hbm] * (2 * n) + [pl.BlockSpec(memory_space=pl.ANY)] * n_after,
        out_specs=(pl.BlockSpec(memory_space=pltpu.SEMAPHORE),) * (2 * n) + (hbm,) * (2 * n)
                  + (pl.BlockSpec(memory_space=pltpu.VMEM),),
        input_output_aliases={i: 2 * n + i for i in range(2 * n)},
        compiler_params=pltpu.CompilerParams(has_side_effects=pltpu.SideEffectType.DATAFLOW_SIDE_EFFECTING),
    )(*[pltpu.with_memory_space_constraint(a, pltpu.HBM) for a in list(shards) + lands], *after)
    return res[:n], res[n:2 * n], res[2 * n:3 * n], res[3 * n:4 * n], res[4 * n]


def _gather_wait(name, send_sems, recv_sems, shards, lands, after):
    n, n_after = len(shards), len(after)
    hbm = pl.BlockSpec(memory_space=pltpu.HBM)

    def body(*refs):
        ins, land = refs[:n], refs[n:2 * n]
        s_sems, r_sems = refs[2 * n:3 * n], refs[3 * n:4 * n]
        x, y, c = lax.axis_index("x"), lax.axis_index("y"), lax.axis_index("c")
        me = 4 * x + 2 * y + c
        for i in range(n):
            for rel in range(N_DEV):
                cp = pltpu.make_async_remote_copy(
                    src_ref=ins[i], dst_ref=land[i].at[me ^ rel], send_sem=s_sems[i].at[rel],
                    recv_sem=r_sems[i].at[rel], device_id=(x, y, c), device_id_type=pl.DeviceIdType.MESH)
                cp.wait_send()
                cp.wait_recv()

    res = pl.pallas_call(
        body, name=name, out_shape=[pltpu.HBM(a.shape, a.dtype) for a in list(shards) + list(lands)],
        in_specs=[hbm] * (2 * n) + [pl.BlockSpec(memory_space=pltpu.SEMAPHORE)] * (2 * n)
                 + [pl.BlockSpec(memory_space=pl.ANY)] * n_after,
        out_specs=[hbm] * (2 * n), input_output_aliases={i: i for i in range(2 * n)},
        compiler_params=pltpu.CompilerParams(has_side_effects=pltpu.SideEffectType.DATAFLOW_SIDE_EFFECTING),
    )(*shards, *lands, *send_sems, *recv_sems, *after)
    return res[n:]


def _tri_powers(low):
    powers, n, p = [low.astype(BF16)], 1, low
    while 2 * n < low.shape[-1]:
        p = _mm(p, p, "nn", REC_PASSES)
        powers.append(p.astype(BF16))
        n *= 2
    return powers


@jax.custom_vjp
def _tri_solve(low, rhs, powers):
    del low
    for p in powers:
        rhs = rhs + _mm(p, rhs, "nn", REC_PASSES)
    return rhs


def _tri_solve_fwd(low, rhs, powers):
    out = _tri_solve(low, rhs, powers)
    return out, (powers, out)


def _tri_solve_bwd(res, d):
    powers, u = res
    for p in powers:
        d = d + _mm(p, d, "tn", REC_PASSES)
    return _mm(d, u, "nt", REC_PASSES), d, [jnp.zeros_like(p) for p in powers]


_tri_solve.defvjp(_tri_solve_fwd, _tri_solve_bwd)


@jax.custom_vjp
def _tri_solve_given(low, rhs, powers, value):
    del low, rhs, powers
    return value


def _tri_solve_given_fwd(low, rhs, powers, value):
    return value, (powers, value)


def _tri_solve_given_bwd(res, d):
    return _tri_solve_bwd(res, d) + (jnp.zeros_like(res[1]),)


_tri_solve_given.defvjp(_tri_solve_given_fwd, _tri_solve_given_bwd)


def _heads(x):
    return jnp.stack([x[:, h * HEAD_DIM:(h + 1) * HEAD_DIM] for h in range(N_HEADS)])


def _unheads(x):
    return jnp.concatenate([x[h] for h in range(N_HEADS)], axis=-1)


def _causal_masks(c):
    ti = lax.broadcasted_iota(jnp.int32, (c, c), 0)
    si = lax.broadcasted_iota(jnp.int32, (c, c), 1)
    strict, incl = si < ti, si <= ti
    both = jnp.concatenate([jnp.concatenate([strict, strict], axis=1), jnp.concatenate([incl, incl], axis=1)], axis=0)
    return strict, incl, both


@jax.custom_vjp
def _gram_given(x2, y2, value):
    del x2, y2
    return value.astype(F32)


def _gram_given_fwd(x2, y2, value):
    return value.astype(F32), (x2, y2, value)


def _gram_given_bwd(res, d):
    x2, y2, value = res
    d = jnp.where(_causal_masks(d.shape[-1] // 2)[2], d, 0.0)
    return _mm(d, y2, "nn", 2), _mm(d, x2, "tn", 2), jnp.zeros_like(value)


_gram_given.defvjp(_gram_given_fwd, _gram_given_bwd)


def _chunk_fwd(z0, r, lw, k, v, a, b, powers=None, gram_value=None, u_value=None):
    c = r.shape[0]
    n_h, n_k = z0.shape[0], z0.shape[1]
    mm = functools.partial(_mm, passes=REC_PASSES)
    gram = functools.partial(_mm, passes=2)
    _, incl, mask = _causal_masks(c)
    cum = _mm(incl.astype(F32), lw, "nn", 3)
    cum_end = cum[c - 1:c, :]
    e_neg, e_end = jnp.exp(-cum), jnp.exp(cum_end - cum)
    x2 = jnp.concatenate([_heads(a * jnp.exp(cum - lw)), _heads(r * jnp.exp(cum))], axis=1)
    y2 = jnp.concatenate([_heads(b * e_neg), _heads(k * e_neg)], axis=1)
    vh = _heads(v)
    g2 = jnp.where(mask, gram(x2, y2, "nt"), 0.0) if gram_value is None else _gram_given(x2, y2, gram_value)
    t2 = mm(x2, z0, "nn") + mm(g2[:, :, c:], vh, "nn")
    low = g2[:, :c, :c]
    powers = _tri_powers(low) if powers is None else powers
    u = _tri_solve(low, t2[:, :c], powers) if u_value is None else _tri_solve_given(low, t2[:, :c], powers, u_value)
    y = t2[:, c:] + mm(g2[:, c:, :c], u, "nn")
    ki = lax.broadcasted_iota(jnp.int32, (n_k, n_k), 0)
    kj = lax.broadcasted_iota(jnp.int32, (n_k, n_k), 1)
    dmat = jnp.where(ki == kj, jnp.broadcast_to(_heads(jnp.exp(cum_end)), (n_h, n_k, n_k)), 0.0)
    z_end = mm(dmat, z0, "nn") + mm(jnp.concatenate([_heads(b * e_end), _heads(k * e_end)], axis=1),
                                    jnp.concatenate([u, vh], axis=1), "tn")
    return _unheads(y), z_end, powers, g2, u


def _rec_fwd(u, lw, k, a, b):
    t_len = lw.shape[0]
    c = min(REC_CHUNK, t_len)
    nc = t_len // c
    per = REC_CHUNKS_PER_STEP if nc % REC_CHUNKS_PER_STEP == 0 else 1
    steps = nc // per
    n_pow = max(1, (c - 1).bit_length())

    def body(r_ref, v_ref, lw_ref, k_ref, a_ref, b_ref, y_ref, zs_ref, pw_ref, gs_ref, us_ref, z_scr):
        @pl.when(pl.program_id(0) == 0)
        def _():
            z_scr[...] = jnp.zeros_like(z_scr)

        for s in range(per):
            rows = pl.ds(s * c, c)
            z0 = z_scr[...]
            zs_ref[s] = z0
            y, z_end, powers, g2, u_rows = _chunk_fwd(z0, r_ref[rows, :], lw_ref[rows, :], k_ref[rows, :], v_ref[rows, :],
                                                      a_ref[rows, :], b_ref[rows, :])
            y_ref[rows, :] = y
            z_scr[...] = z_end
            pw_ref[s] = jnp.concatenate(powers, axis=0)
            gs_ref[s] = g2.astype(BF16)
            us_ref[s] = u_rows

    blk = lambda cb: pl.BlockSpec((per * c, RWKV_DIM), functools.partial(lambda i, q: (i, q), q=cb))
    res = pl.pallas_call(
        body, name="rwkv_rec_fwd", grid=(steps,),
        in_specs=[blk(0), blk(2)] + [blk(0)] * 4,
        out_specs=[blk(0), pl.BlockSpec((per, N_HEADS, HEAD_DIM, HEAD_DIM), lambda i: (i, 0, 0, 0)),
                   pl.BlockSpec((per, n_pow * N_HEADS, c, c), lambda i: (i, 0, 0, 0)),
                   pl.BlockSpec((per, N_HEADS, 2 * c, 2 * c), lambda i: (i, 0, 0, 0)),
                   pl.BlockSpec((per, N_HEADS, c, HEAD_DIM), lambda i: (i, 0, 0, 0))],
        out_shape=[jax.ShapeDtypeStruct((t_len, RWKV_DIM), F32),
                   jax.ShapeDtypeStruct((nc, N_HEADS, HEAD_DIM, HEAD_DIM), F32),
                   jax.ShapeDtypeStruct((nc, n_pow * N_HEADS, c, c), BF16),
                   jax.ShapeDtypeStruct((nc, N_HEADS, 2 * c, 2 * c), BF16),
                   jax.ShapeDtypeStruct((nc, N_HEADS, c, HEAD_DIM), F32)],
        scratch_shapes=[pltpu.VMEM((N_HEADS, HEAD_DIM, HEAD_DIM), F32)], compiler_params=_params(("arbitrary",)),
    )(u, u, lw, k, a, b)
    return res[0], res[1:]


def _rec_bwd(u, lw, k, a, b, saved, dy):
    t_len = lw.shape[0]
    c = min(REC_CHUNK, t_len)
    nc = t_len // c
    per = REC_CHUNKS_PER_STEP if nc % REC_CHUNKS_PER_STEP == 0 else 1
    steps = nc // per

    zs, pw, gs, us = saved

    def body(r_ref, v_ref, lw_ref, k_ref, a_ref, b_ref, zs_ref, pw_ref, gs_ref, us_ref, dy_ref, *rest):
        g_refs, dz_scr = rest[:6], rest[6]

        @pl.when(pl.program_id(0) == 0)
        def _():
            dz_scr[...] = jnp.zeros_like(dz_scr)

        for s in reversed(range(per)):
            rows = pl.ds(s * c, c)
            powers = [pw_ref[s, j * N_HEADS:(j + 1) * N_HEADS] for j in range(pw.shape[1] // N_HEADS)]
            chunk = functools.partial(lambda gram, u_val, pws, *xs: _chunk_fwd(*xs, powers=pws, gram_value=gram, u_value=u_val)[:2],
                                      gs_ref[s], us_ref[s], powers)
            _, vjp = jax.vjp(chunk, zs_ref[s], r_ref[rows, :], lw_ref[rows, :], k_ref[rows, :], v_ref[rows, :],
                             a_ref[rows, :], b_ref[rows, :])
            dz0, dr, dlw, dk, dv, da, db = vjp((dy_ref[rows, :], dz_scr[...]))
            for ref, val in zip(g_refs, (dr, dv, dlw, dk, da, db)):
                ref[rows, :] = val
            dz_scr[...] = dz0

    blk = lambda cb: pl.BlockSpec((per * c, RWKV_DIM), functools.partial(lambda i, q: (steps - 1 - i, q), q=cb))
    saved_blk = lambda arr: pl.BlockSpec((per,) + arr.shape[1:], lambda i: (steps - 1 - i, 0, 0, 0))
    return pl.pallas_call(
        body, name="rwkv_rec_bwd", grid=(steps,),
        in_specs=[blk(0), blk(2)] + [blk(0)] * 4 + [saved_blk(zs), saved_blk(pw), saved_blk(gs), saved_blk(us), blk(0)],
        out_specs=[blk(0)] * 6, out_shape=[jax.ShapeDtypeStruct((t_len, RWKV_DIM), F32)] * 6,
        scratch_shapes=[pltpu.VMEM((N_HEADS, HEAD_DIM, HEAD_DIM), F32)], compiler_params=_params(("arbitrary",)),
    )(u, u, lw, k, a, b, zs, pw, gs, us, dy)


_EARLY = ["w_in", "conv_w", "w_lora_up", "a_lora_up", "g_lora_up"]
_LATE = ["w_out", "w_up", "w_down", "w_ple_gate", "w_ple_proj"]
_SHARDED = _EARLY + _LATE
_COL_SHARDED = {"w_in", "conv_w", "w_lora_up", "a_lora_up", "g_lora_up", "w_up", "w_ple_proj"}
_BF16_GATHER = {"w_in", "w_out", "w_up", "w_down", "w_ple_gate", "w_ple_proj"}
_REPLICATED = ["norm_mix_g", "shift_mu", "w0", "a0", "k_k", "k_a", "r_k", "ln_x_g", "ln_x_b", "norm_mlp_g", "norm_ple_g",
               "norm_final_g"]
_WEIGHTS = ["norm_mix_g", "w_in", "conv_w", "shift_mu", "w_lora_up", "w0", "a_lora_up", "a0", "g_lora_up", "k_k", "k_a", "r_k",
            "ln_x_g", "ln_x_b", "w_out", "norm_mlp_g", "w_up", "w_down", "norm_ple_g", "w_ple_gate", "w_ple_proj", "norm_final_g"]


def _unshard(name, g):
    if name in _COL_SHARDED:
        return jnp.moveaxis(g, 0, 1).reshape(g.shape[1], N_DEV * g.shape[2])
    return g.reshape(N_DEV * g.shape[1], g.shape[2])


def _reshard(name, full):
    if name in _COL_SHARDED:
        return jnp.moveaxis(full.reshape(full.shape[0], N_DEV, full.shape[1] // N_DEV), 1, 0)
    return full.reshape(N_DEV, full.shape[0] // N_DEV, full.shape[1])


def _pad_in_cols(a):
    z = lambda n: jnp.zeros(a.shape[:-1] + (n,), a.dtype)
    conv = [a[..., part * CONV_DIM + j * LANE:part * CONV_DIM + (j + 1) * LANE] for j in range(CONV_DIM // LANE) for part in range(3)]
    return jnp.concatenate(conv + [a[..., CONV_COLS:3136], z(64), a[..., 3136:3200], z(64), a[..., 3200:3360], z(96)], axis=-1)


def _unpad_in_cols(a):
    conv = [a[..., (3 * j + part) * LANE:(3 * j + part + 1) * LANE] for part in range(3) for j in range(CONV_DIM // LANE)]
    return jnp.concatenate(conv + [a[..., CONV_COLS:3136], a[..., 3200:3264], a[..., 3328:3488]], axis=-1)


def _assemble_w_in(g):
    n_dev, rows, cols = g.shape

    def body(g_ref, o_ref):
        o_ref[...] = _pad_in_cols(jnp.concatenate([g_ref[d] for d in range(n_dev)], axis=1))

    return pl.pallas_call(
        body, name="w_in_assemble", grid=(rows // ROW_BLOCK,),
        in_specs=[pl.BlockSpec((n_dev, ROW_BLOCK, cols), lambda i: (0, i, 0))],
        out_specs=pl.BlockSpec((ROW_BLOCK, IN_PAD), lambda i: (i, 0)),
        out_shape=jax.ShapeDtypeStruct((rows, IN_PAD), g.dtype), compiler_params=_params(("arbitrary",)),
    )(g)


def _split_w_in_grad(dw):
    rows = dw.shape[0]
    cols = IN_COLS // N_DEV

    def body(d_ref, o_ref):
        full = _unpad_in_cols(d_ref[...])
        for d in range(N_DEV):
            o_ref[d] = full[:, cols * d:cols * (d + 1)]

    return pl.pallas_call(
        body, name="w_in_grad_split", grid=(rows // ROW_BLOCK,),
        in_specs=[pl.BlockSpec((ROW_BLOCK, IN_PAD), lambda i: (i, 0))],
        out_specs=pl.BlockSpec((N_DEV, ROW_BLOCK, cols), lambda i: (0, i, 0)),
        out_shape=jax.ShapeDtypeStruct((N_DEV, rows, cols), dw.dtype), compiler_params=_params(("arbitrary",)),
    )(dw)


def _pad_rows(a, rows):
    return jnp.concatenate([a, jnp.zeros((rows - a.shape[0],) + a.shape[1:], a.dtype)], axis=0)


SEG_W = [RWKV_DIM, RWKV_DIM, RWKV_DIM, LANE, LANE, 2 * LANE]
SEG_OFF = [0, 512, 1024, XW_OFF, XA_OFF, XG_OFF]


def _rwkv_pre_bwd(proj, u, grads, mu, small, dproj):
    t_len = u.shape[0]
    tr = min(ROW_BLOCK, t_len)
    nb = t_len // tr
    sub = 8
    n_g = len(grads)
    acc_shapes = [(1, RW_PAD)] + [(1, RWKV_DIM)] * 4 + [(LANE, RWKV_DIM), (LANE, RWKV_DIM), (2 * LANE, RWKV_DIM)]

    def body(*refs):
        seg_refs, halo_refs = refs[:6], refs[6:12]
        k_ref, xw_ref, xa_ref, xg_ref = refs[12:16]
        g_refs = refs[16:16 + n_g]
        mu_ref = refs[16 + n_g]
        prm_refs = refs[17 + n_g:24 + n_g]
        out_hbm = refs[25 + n_g]
        acc_refs = refs[26 + n_g:26 + n_g + len(acc_shapes)]
        vbuf, sems, carry = refs[26 + n_g + len(acc_shapes):]
        i = pl.program_id(0)
        blk = nb - 1 - i
        dr1, dr2, dv1, dv2, dlw, dk1, dk2, da, db, dg, lw_rows = [g[...] for g in g_refs]
        dk, dxw, dxa, dxg, *dprm = _rwkv_pre_vjp(k_ref[...], xw_ref[...], xa_ref[...], xg_ref[...], lw_rows,
                                                  *[p_[...] for p_ in prm_refs], dlw, dk1 + dk2, da, db, dg)
        du = jnp.concatenate([dr1 + dr2, dk, dv1 + dv2, dxw, dxa, dxg], axis=1)
        mu_v = mu_ref[...]

        @pl.when(i == 0)
        def _():
            carry[...] = jnp.zeros_like(carry)

        rows = lax.broadcasted_iota(jnp.int32, du.shape, 0)
        nxt = jnp.where(rows == tr - 1, carry[...], pltpu.roll(du, tr - 1, 0))
        d_rw = du - mu_v * du + mu_v * nxt
        d_mu = []
        for s_ref, h_ref, off, wd in zip(seg_refs, halo_refs, SEG_OFF, SEG_W):
            cur = s_ref[...]
            r0 = lax.broadcasted_iota(jnp.int32, cur.shape, 0)
            prev = jnp.where(r0 == 0, jnp.where(blk == 0, 0.0, h_ref[sub - 1:sub, :]), pltpu.roll(cur, 1, 0))
            d_mu.append(jnp.sum(du[:, off:off + wd] * (prev - cur), axis=0, keepdims=True))
        sums = [jnp.concatenate(d_mu, axis=1)] + list(dprm)

        @pl.when(i == 0)
        def _():
            for a_ref, val in zip(acc_refs, sums):
                a_ref[...] = val

        @pl.when(i > 0)
        def _():
            for a_ref, val in zip(acc_refs, sums):
                a_ref[...] += val

        carry[...] = du[0:1, :]
        slot = i % 2

        def writeback(s, b):
            return pltpu.make_async_copy(vbuf.at[s], out_hbm.at[pl.ds(b * tr, tr), pl.ds(CONV_COLS, RW_PAD)], sems.at[s])

        @pl.when(i >= 2)
        def _():
            writeback(slot, blk + 2).wait()

        vbuf[slot] = d_rw.astype(vbuf.dtype)
        writeback(slot, blk).start()

        @pl.when(i == nb - 1)
        def _():
            writeback(slot, blk).wait()
            if nb > 1:
                writeback(1 - slot, blk + 1).wait()

    rev = lambda w_, cb: pl.BlockSpec((tr, w_), functools.partial(lambda i, c: (nb - 1 - i, c), c=cb))
    halo = lambda w_, cb: pl.BlockSpec((sub, w_), functools.partial(
        lambda i, c: (jnp.maximum((nb - 1 - i) * (tr // sub) - 1, 0), c), c=cb))
    whole = lambda a: pl.BlockSpec(a.shape, functools.partial(lambda i, n: (0,) * n, n=a.ndim))
    segs = [(wd, (CONV_COLS + off) // wd) for off, wd in zip(SEG_OFF, SEG_W)]
    u_cols = [(512, 1), (LANE, XW_OFF // LANE), (LANE, XA_OFF // LANE), (2 * LANE, XG_OFF // (2 * LANE))]
    any_spec = pl.BlockSpec(memory_space=pl.ANY)
    res = pl.pallas_call(
        body, name="rwkv_pre_bwd", grid=(nb,),
        in_specs=[rev(*s) for s in segs] + [halo(*s) for s in segs] + [rev(*c) for c in u_cols]
                 + [rev(RWKV_DIM, 0)] * n_g + [whole(mu)] + [whole(p_) for p_ in small] + [any_spec],
        out_specs=[any_spec] + [pl.BlockSpec(s, functools.partial(lambda i, n: (0,) * n, n=len(s))) for s in acc_shapes],
        out_shape=[jax.ShapeDtypeStruct(dproj.shape, dproj.dtype)] + [jax.ShapeDtypeStruct(s, F32) for s in acc_shapes],
        scratch_shapes=[pltpu.VMEM((2, tr, RW_PAD), dproj.dtype), pltpu.SemaphoreType.DMA((2,)), pltpu.VMEM((1, RW_PAD), F32)],
        input_output_aliases={24 + n_g: 0},
        compiler_params=_params(("arbitrary",)),
    )(*[proj] * 12, *[u] * 4, *grads, mu, *small, dproj)
    return res


def _local_step(x, p, tgt, w, early_shards, late_shards):
    row = lambda v: v.reshape(1, -1)
    w = dict(w)

    xn1, *gathered = _rowwise("rms_mix", lambda h, g: (_rms(h, g),), [x], [w["norm_mix_g"]], [(D_MODEL, BF16)],
                              gather=early_shards)
    w.update({n: _unshard(n, g_) for n, g_ in zip(_EARLY[1:], gathered[1:])})
    w["w_in"] = _assemble_w_in(gathered[0])
    w["w_lora_up"] = _pad_rows(w["w_lora_up"], LANE)
    w["a_lora_up"] = _pad_rows(w["a_lora_up"], LANE)
    w["g_lora_up"] = _pad_rows(w["g_lora_up"], 2 * LANE)
    lg_send, lg_recv, lg_shards, lg_lands, lg_token = _gather_start("late_gather_start", late_shards, after=[xn1])
    w["shift_mu"] = w["shift_mu"] + lg_token[0:1, 0:1]
    proj = _matmul("in_proj", xn1, w["w_in"], "nn", [F32], tm=2048, tn=512, tk=D_MODEL)
    n_cb = CONV_DIM // LANE

    def conv_fwd(blk, cw):
        gb, gc, hx = blk[:, :LANE], blk[:, LANE:2 * LANE], blk[:, 2 * LANE:]
        uu = gc * hx
        return (gb * (uu * cw[2:3] + _shift_down(uu, 1) * cw[1:2] + _shift_down(uu, 2) * cw[0:1]),)

    (y_conv,) = _colwise("conv_fwd", conv_fwd, n_cb, [(proj, 3 * LANE)], [w["conv_w"]], [(CONV_DIM, BF16, LANE)])

    small = [w["w0"], w["a0"], w["k_k"], w["k_a"], w["w_lora_up"], w["a_lora_up"], w["g_lora_up"]]
    def pre_fwd(*xs):
        cur, prev_rows, mu, prm = xs[:6], xs[6:12], xs[12], xs[13:]
        segs = []
        for c_, p_, off, wd in zip(cur, prev_rows, SEG_OFF, SEG_W):
            rows = lax.broadcasted_iota(jnp.int32, c_.shape, 0)
            prev = jnp.where(rows == 0, p_, pltpu.roll(c_, 1, 0))
            segs.append(c_ + mu[:, off:off + wd] * (prev - c_))
        return (jnp.concatenate(segs, axis=1),) + tuple(_rwkv_pre(segs[1], segs[3], segs[4], segs[5], *prm))

    proj_segs = [(proj, wd, (CONV_COLS + off) // wd) for off, wd in zip(SEG_OFF, SEG_W)]
    u, lw, k_h, ra, rb, g = _rowwise(
        "rwkv_pre", pre_fwd, proj_segs, [w["shift_mu"]] + small, [(RW_PAD, F32)] + [(RWKV_DIM, F32)] * 5, tr=2 * ROW_BLOCK, halo=True)
    y_rec, rec_saved = _rec_fwd(u, lw, k_h, ra, rb)

    def late_weight(names, after):
        idx = [_LATE.index(n) for n in names]
        got = _gather_wait("late_gather_wait_" + names[0], [lg_send[i] for i in idx], [lg_recv[i] for i in idx],
                           [lg_shards[i] for i in idx], [lg_lands[i] for i in idx], after)
        return [_unshard(n, g_) for n, g_ in zip(names, got)]

    w["w_out"], w["w_up"], w["w_down"] = late_weight(["w_out", "w_up", "w_down"], [y_rec])
    post_c = [w["ln_x_g"], w["ln_x_b"], w["r_k"]]
    u_r, u_v = (u, 512, 0), (u, 512, 2)
    (y_rwkv,) = _rowwise("rwkv_post", lambda *xs: (_rwkv_post(*xs),), [y_rec, u_r, k_h, u_v, g], post_c, [(RWKV_DIM, BF16)],
                         tr=2 * ROW_BLOCK)
    ycat = jnp.concatenate([y_conv, y_rwkv], axis=1)
    def res_norm(acc, r_, g_):
        h = acc + r_
        return h, _rms(h, g_)

    h1, xn2 = _matmul("out_proj", ycat, w["w_out"], "nn", [F32, BF16], tm=1024, tn=D_MODEL, tk=D_MODEL, extras=[x],
                      consts=[w["norm_mlp_g"]], epilogue=res_norm)

    square = lambda h: h.astype(F32) * h.astype(F32)
    hid = _matmul("mlp_up", xn2, w["w_up"], "nn", [BF16], tm=2048, tn=1024, tk=D_MODEL,
                  epilogue=lambda acc: (jnp.maximum(acc, 0.0),))
    h2, xn3 = _matmul("mlp_down", hid, w["w_down"], "nn", [F32, BF16], tm=512, tn=D_MODEL, tk=D_FF, extras=[h1],
                      consts=[w["norm_ple_g"]], epilogue=res_norm, a_map=square)
    w["w_ple_gate"], w["w_ple_proj"] = late_weight(["w_ple_gate", "w_ple_proj"], [xn3])
    zg =_matmul("ple_gate", xn3, w["w_ple_gate"], "nn", [F32], tm=1024, tn=1024, tk=D_MODEL)
    pp = _matmul("ple_proj", p, w["w_ple_proj"], "nn", [F32], tm=1024, tn=1024, tk=PLE_DIM)

    def head(h2_, zg_, pp_, tg, gf):
        gate = _sigmoid(zg_)
        h3 = h2_ + gate * pp_
        out = _rms(h3, gf)
        err = out - tg
        dh3, dgf = _rms_bwd(h3, gf, err * (1.0 / D_MODEL))
        loss = jnp.sum(jnp.sum(err * err, axis=1, keepdims=True), axis=0, keepdims=True) * (0.5 / D_MODEL)
        return dh3, dh3 * pp_ * gate * (1.0 - gate), dh3 * gate, dgf, loss

    dh3, dzg, dpp, d_norm_final, loss = _rowwise(
        "head", head, [h2, zg, pp, tgt], [row(w["norm_final_g"])], [(D_MODEL, F32), (D_MODEL, BF16), (D_MODEL, BF16)],
        [(1, D_MODEL), (1, 1)], tr=2 * ROW_BLOCK)

    d_w_ple_proj = _matmul("d_ple_proj", p, dpp, "tn", [BF16], tm=PLE_DIM, tn=D_MODEL // N_DEV, tk=4096, col_blocks_out=True)
    d_w_ple_gate = _matmul("d_ple_gate", xn3, dzg, "tn", [BF16], tm=512, tn=1024, tk=4096)

    def norm_bwd(dxn, h, dres, g_):
        dh, dg = _rms_bwd(h, g_, dxn)
        dh = dh + dres
        return dh, dh, dg

    nb = dict(tm=512, tn=D_MODEL, epilogue=norm_bwd, sums=[(1, D_MODEL)])
    dh2, dh2_b, d_norm_ple = _matmul("dx_ple_gate", dzg, w["w_ple_gate"], "nt", [F32, BF16], tk=D_MODEL,
                                     extras=[h2, dh3], consts=[w["norm_ple_g"]], **nb)
    d_w_down = _matmul("d_mlp_down", hid, dh2_b, "tn", [BF16], tm=512, tn=1024, tk=4096, a_map=square)
    dpre = _matmul("dx_mlp_down", dh2_b, w["w_down"], "nt", [BF16], tm=2048, tn=1024, tk=D_MODEL, extras=[hid],
                   epilogue=lambda acc, hid_: (acc * (2.0 * hid_.astype(F32)),))
    d_w_up = _matmul("d_mlp_up", xn2, dpre, "tn", [BF16], tm=1024, tn=D_FF // N_DEV, tk=4096, col_blocks_out=True)
    dh1, dh1_b, d_norm_mlp = _matmul("dx_mlp_up", dpre, w["w_up"], "nt", [F32, BF16], tk=D_FF,
                                     extras=[h1, dh2], consts=[w["norm_mlp_g"]], **nb)
    d_w_out = _matmul("d_out_proj", ycat, dh1_b, "tn", [BF16], tm=512, tn=1024, tk=4096)
    dycat = _matmul("dx_out_proj", dh1_b, w["w_out"], "nt", [F32], tm=1024, tn=1024, tk=D_MODEL)
    late_grads = dict(w_out=d_w_out, w_up=d_w_up, w_down=d_w_down, w_ple_gate=d_w_ple_gate, w_ple_proj=d_w_ple_proj)
    late_send = [late_grads[n] if n in ("w_up", "w_ple_proj") else _reshard(n, late_grads[n]) for n in _LATE]
    *late_flight, late_token = _scatter_start("late_scatter_start", late_send, [lax.empty(a.shape, a.dtype) for a in late_send])
    conv_w_bwd = w["conv_w"] + late_token[0:1, 0:1]

    def conv_bwd(dy, blk, cw):
        gb, gc, hx = blk[:, :LANE], blk[:, LANE:2 * LANE], blk[:, 2 * LANE:]
        uu = gc * hx
        u1, u2 = _shift_down(uu, 1), _shift_down(uu, 2)
        dconv = dy * gb
        du = dconv * cw[2:3] + _shift_up(dconv, 1) * cw[1:2] + _shift_up(dconv, 2) * cw[0:1]
        s = lambda z: jnp.sum(z, axis=0, keepdims=True)
        d_blk = jnp.concatenate([dy * (uu * cw[2:3] + u1 * cw[1:2] + u2 * cw[0:1]), du * hx, du * gc], axis=1)
        return d_blk, s(dconv * u2), s(dconv * u1), s(dconv * uu)

    dproj, dcw0, dcw1, dcw2 = _colwise(
        "conv_bwd", conv_bwd, n_cb, [(dycat, LANE), (proj, 3 * LANE)], [conv_w_bwd],
        [(IN_PAD, BF16, 3 * LANE)], [(1, CONV_DIM)] * 3)

    def post_bwd(dy, y, r, k_h_, v, g_, ln_g, ln_b, r_k):
        _, vjp = jax.vjp(_rwkv_post, y, r, k_h_, v, g_, ln_g, ln_b, r_k)
        return vjp(dy)

    dy_rec, dr_p, dk_p, dv_p, dg, d_ln_g, d_ln_b, d_r_k = _rowwise(
        "rwkv_post_bwd", post_bwd, [(dycat, 512, 1), y_rec, u_r, k_h, u_v, g], post_c,
        [(RWKV_DIM, F32)] * 5, [(1, RWKV_DIM)] * 3, tr=2 * ROW_BLOCK)
    dr_r, dv_r, dlw, dk_r, da, db = _rec_bwd(u, lw, k_h, ra, rb, rec_saved, dy_rec)

    dproj, d_mu, d_w0, d_a0, d_k_k, d_k_a, d_wl, d_al, d_gl = _rwkv_pre_bwd(
        proj, u, [dr_p, dr_r, dv_p, dv_r, dlw, dk_p, dk_r, da, db, dg, lw], w["shift_mu"], small, dproj)
    d_w_in = _matmul("d_in_proj", xn1, dproj, "tn", [BF16], tm=1024, tn=896, tk=4096)
    early_grads = dict(conv_w=jnp.concatenate([dcw0, dcw1, dcw2], axis=0),
                       w_lora_up=d_wl[:64], a_lora_up=d_al[:64], g_lora_up=d_gl[:160])
    early_send = [_split_w_in_grad(d_w_in)] + [_reshard(n, early_grads[n]) for n in _EARLY[1:]]
    *early_flight, token = _scatter_start("early_scatter_start", early_send, [lax.empty(a.shape, a.dtype) for a in early_send])
    dx, d_norm_mix = _matmul(
        "dx_in_proj", dproj, w["w_in"], "nt", [F32], tk=IN_PAD, extras=[x, dh1], consts=[w["norm_mix_g"] + token[0:1, 0:1]],
        **dict(nb, epilogue=lambda *a: norm_bwd(*a)[1:]))

    grads = dict(
        norm_mix_g=d_norm_mix, shift_mu=d_mu, w0=d_w0, a0=d_a0, k_k=d_k_k, k_a=d_k_a, r_k=d_r_k,
        ln_x_g=d_ln_g, ln_x_b=d_ln_b, norm_mlp_g=d_norm_mlp, norm_ple_g=d_norm_ple, norm_final_g=d_norm_final)
    return loss, dx, grads, late_flight, early_flight, d_w_in


def _adam_update(partials, w_ref, m_ref, v_ref, g_ref, d_ref, nm_ref, nv_ref):
    g = partials[0].astype(F32)
    for part in partials[1:]:
        g = g + part.astype(F32)
    nm =ADAM_B1 * m_ref[...] + (1.0 - ADAM_B1) * g
    nv = ADAM_B2 * v_ref[...] + (1.0 - ADAM_B2) * (g * g)
    m_hat = nm / (1.0 - ADAM_B1 ** ADAM_STEP)
    v_hat = nv / (1.0 - ADAM_B2 ** ADAM_STEP)
    g_ref[...] = g
    d_ref[...] = -ADAM_LR * (m_hat / (jnp.sqrt(v_hat) + ADAM_EPS) + ADAM_WD * w_ref[...])
    nm_ref[...] = nm
    nv_ref[...] = nv


SMALL_ROWS = 8


def _small_layout(widths):
    widths = list(widths) + [1]
    fill, place = [0] * SMALL_ROWS, [None] * len(widths)
    for j in sorted(range(len(widths)), key=lambda q: -widths[q]):
        row = fill.index(min(fill))
        place[j] = (row, fill[row])
        fill[row] += -(-widths[j] // LANE) * LANE
    return place, max(fill)


def _pack_small(vecs, loss):
    place, total = _small_layout([v_.shape[1] for v_ in vecs])
    n = len(vecs)

    def body(*refs):
        out = jnp.zeros((SMALL_ROWS, total), F32)
        row_id = lax.broadcasted_iota(jnp.int32, (SMALL_ROWS, total), 0)
        for row in range(SMALL_ROWS):
            mine = sorted((off, j) for j, (r_, off) in enumerate(place) if r_ == row)
            pieces, at = [], 0
            for off, j in mine:
                val = refs[j][...]
                pieces.append(val)
                at = off + val.shape[1]
                pad = -val.shape[1] % LANE
                if pad:
                    pieces.append(jnp.zeros((1, pad), F32))
                    at += pad
            if total > at:
                pieces.append(jnp.zeros((1, total - at), F32))
            out = jnp.where(row_id == row, jnp.broadcast_to(jnp.concatenate(pieces, axis=1), (SMALL_ROWS, total)), out)
        refs[n + 1][...] = out

    return pl.pallas_call(body, name="pack_small", out_shape=jax.ShapeDtypeStruct((SMALL_ROWS, total), F32))(*vecs, loss)


def _adamw_small(packed, ws, ms, vs):
    n = len(ws)
    place, _ = _small_layout([w_.shape[1] for w_ in ws])

    def body(p_ref, *refs):
        w_refs, m_refs, v_refs, outs = refs[:n], refs[n:2 * n], refs[2 * n:3 * n], refs[3 * n:]
        for j in range(n):
            row, off = place[j]
            cols = pl.ds(off, ws[j].shape[1])
            _adam_update([p_ref[s, row:row + 1, cols] for s in range(N_DEV)], w_refs[j], m_refs[j], v_refs[j],
                         *outs[4 * j:4 * j + 4])
        row, off = place[n]
        total = p_ref[0, row:row + 1, off:off + 1]
        for s in range(1, N_DEV):
            total = total + p_ref[s, row:row + 1, off:off + 1]
        outs[4 * n][...] = total

    res = pl.pallas_call(
        body, name="adamw_small",
        out_shape=[jax.ShapeDtypeStruct(w_.shape, F32) for w_ in ws for _ in range(4)] + [jax.ShapeDtypeStruct((1, 1), F32)],
    )(packed, *ws, *ms, *vs)
    return [res[4 * j:4 * j + 4] for j in range(n)], res[4 * n]


def _adamw(name, parts, w, m, v, own=None, me=None):
    rows, cols = w.shape[-2:]
    lead = w.ndim - 2
    tr = rows if rows * cols * 4 * 8 <= (4 << 20) else max(8, (4 << 20) // (cols * 4 * 8) // 8 * 8)
    while rows % tr:
        tr -= 8
    shape4 = [jax.ShapeDtypeStruct(w.shape, F32)] * 4
    if own is None:
        def body(p_ref, *refs):
            _adam_update([p_ref[s] for s in range(N_DEV)], *refs)

        blk = pl.BlockSpec((None,) * lead + (tr, cols), lambda i: (0,) * lead + (i, 0))
        return pl.pallas_call(
            body, name=name, grid=(rows // tr,),
            in_specs=[pl.BlockSpec((N_DEV, tr, cols), lambda i: (0, i, 0)), blk, blk, blk], out_specs=[blk] * 4,
            out_shape=shape4, compiler_params=_params(("arbitrary",)),
        )(parts, w, m, v)

    def body_own(me_ref, p_ref, own_ref, *refs):
        mine = own_ref[...]
        _adam_update([jnp.where(me_ref[0] == s, mine, p_ref[s]) for s in range(N_DEV)], *refs)

    blk = pl.BlockSpec((None,) * lead + (tr, cols), lambda i, me_ref: (0,) * lead + (i, 0))
    return pl.pallas_call(
        body_own, name=name, out_shape=shape4,
        grid_spec=pltpu.PrefetchScalarGridSpec(
            num_scalar_prefetch=1, grid=(rows // tr,),
            in_specs=[pl.BlockSpec((N_DEV, tr, cols), lambda i, me_ref: (0, i, 0)),
                      pl.BlockSpec((None, tr, cols), lambda i, me_ref: (me_ref[0], i, 0)), blk, blk, blk],
            out_specs=[blk] * 4),
        compiler_params=_params(("arbitrary",)),
    )(me, parts, own, w, m, v)


def kernel(x, p, norm_mix_g, w_in, conv_w, shift_mu, w_lora_up, w0, a_lora_up, a0, g_lora_up, k_k, k_a, r_k, ln_x_g, ln_x_b, w_out, norm_mlp_g, w_up, w_down, norm_ple_g, w_ple_gate, w_ple_proj, norm_final_g, loss_target, m_norm_mix_g, m_w_in, m_conv_w, m_shift_mu, m_w_lora_up, m_w0, m_a_lora_up, m_a0, m_g_lora_up, m_k_k, m_k_a, m_r_k, m_ln_x_g, m_ln_x_b, m_w_out, m_norm_mlp_g, m_w_up, m_w_down, m_norm_ple_g, m_w_ple_gate, m_w_ple_proj, m_norm_final_g, v_norm_mix_g, v_w_in, v_conv_w, v_shift_mu, v_w_lora_up, v_w0, v_a_lora_up, v_a0, v_g_lora_up, v_k_k, v_k_a, v_r_k, v_ln_x_g, v_ln_x_b, v_w_out, v_norm_mlp_g, v_w_up, v_w_down, v_norm_ple_g, v_w_ple_gate, v_w_ple_proj, v_norm_final_g):
    args = dict(locals())
    wts = {n: args[n] for n in _WEIGHTS}
    mom = {n: args["m_" + n] for n in _WEIGHTS}
    var = {n: args["v_" + n] for n in _WEIGHTS}
    shard2d = lambda a: a.reshape(a.shape[-2:])
    pad_mu = lambda a: _pad_in_cols(jnp.concatenate([jnp.zeros((1, CONV_COLS), F32), a], axis=1))[:, CONV_COLS:]
    unpad_mu = lambda a: _unpad_in_cols(jnp.concatenate([jnp.zeros((1, CONV_COLS), F32), a], axis=1))[:, CONV_COLS:]

    shards = {n: shard2d(wts[n]).astype(BF16 if n in _BF16_GATHER else F32) for n in _SHARDED}
    w = {n: wts[n].reshape(1, -1) for n in _REPLICATED}
    w["shift_mu"] = pad_mu(wts["shift_mu"])

    loss, dx, grads, late_flight, early_flight, d_w_in = _local_step(
        x[0], p[0, 0], loss_target[0], w, [shards[n] for n in _EARLY], [shards[n] for n in _LATE])

    me = (4 * lax.axis_index("x") + 2 * lax.axis_index("y") + lax.axis_index("c")).astype(jnp.int32).reshape(1)
    late_sent, late_parts = _scatter_wait("late_scatter_wait", *late_flight, after=[d_w_in])
    out = {n: _adamw("adamw_" + n, prt, wts[n], mom[n], var[n], own=own, me=me)
           for n, prt, own in zip(_LATE, late_parts, late_sent)}
    early_sent, early_parts = _scatter_wait("early_scatter_wait", *early_flight, after=[dx] + [out[n][1] for n in _LATE])
    for n, prt, own in zip(_EARLY, early_parts, early_sent):
        out[n] = _adamw("adamw_" + n, prt, wts[n], mom[n], var[n], own=own, me=me)

    grads["shift_mu"] = unpad_mu(grads["shift_mu"])
    flat = lambda a: a.reshape(1, -1)
    (small_parts,) = _all_gather("gather_small", [_pack_small([flat(grads[n]) for n in _REPLICATED], loss)])
    small, loss_total = _adamw_small(small_parts, *[[flat(d[n]) for n in _REPLICATED] for d in (wts, mom, var)])
    for n, res in zip(_REPLICATED, small):
        out[n] = [r.reshape(wts[n].shape) for r in res]
    return (loss_total[0, 0], dx[None], *[out[n][0] for n in _WEIGHTS], *[out[n][1] for n in _WEIGHTS],
            *[out[n][2] for n in _WEIGHTS], *[out[n][3] for n in _WEIGHTS])
```
